```python
import jax, jax.numpy as jnp
from jax import lax
import numpy as np

D_MODEL = 1024
BATCH = 8
SEQ = 8192
DEPTH = 1

GRID_W = 64
CTX_LEN = 256
HEAD_DIM = 64
N_Q_HEADS = 8
N_KV_HEADS = 2
GQA_GROUP = N_Q_HEADS // N_KV_HEADS
WINDOW = 128
BLOCK = 128
ROPE_BASE = 10000.0
ROPE_PAIRS = HEAD_DIM // 4
N_GMLP_GROUPS = 8
GMLP_GROUP_DIM = 64
GMLP_WIDTH = N_GMLP_GROUPS * GMLP_GROUP_DIM
CHUNK = 128
FFN_HIDDEN = ((8 * D_MODEL // 3 + 255) // 256) * 256
Q_W = N_Q_HEADS * HEAD_DIM
KV_W = N_KV_HEADS * HEAD_DIM
IN_SPLITS = (Q_W, Q_W + KV_W, Q_W + 2 * KV_W, Q_W + 2 * KV_W + GMLP_WIDTH,
             Q_W + 2 * KV_W + 2 * GMLP_WIDTH, Q_W + 2 * KV_W + 2 * GMLP_WIDTH + D_MODEL)
IN_W = Q_W + 2 * KV_W + 2 * GMLP_WIDTH + 2 * D_MODEL
LN_EPS = 1e-5
NEG_INF = -1e30
DEEPNORM_ALPHA = (2 * DEPTH) ** 0.25
DEEPNORM_BETA = (8 * DEPTH) ** -0.25

kernel_name = 'hybrid_window_gqa_gmlp_dit_block'


def layer_norm(x, g=None, b=None):
    xf = x.astype(jnp.float32)
    mu = jnp.mean(xf, axis=-1, keepdims=True)
    var = jnp.mean(jnp.square(xf - mu), axis=-1, keepdims=True)
    y = (xf - mu) * lax.rsqrt(var + LN_EPS)
    if g is not None:
        y = y * g.astype(jnp.float32) + b.astype(jnp.float32)
    return y.astype(x.dtype)


def modulate(y, shift, scale):
    return y * (1 + scale[..., None, :]) + shift[..., None, :]


def axial_rope(t, rows, cols):
    inv = ROPE_BASE ** (-jnp.arange(ROPE_PAIRS, dtype=jnp.float32) / ROPE_PAIRS)

    def rot(xa, pos):
        ang = pos.astype(jnp.float32)[:, None] * inv
        cos = jnp.cos(ang)[:, None, :].astype(t.dtype)
        sin = jnp.sin(ang)[:, None, :].astype(t.dtype)
        x1, x2 = xa[..., :ROPE_PAIRS], xa[..., ROPE_PAIRS:]
        return jnp.concatenate([x1 * cos - x2 * sin, x1 * sin + x2 * cos], axis=-1)

    half = HEAD_DIM // 2
    return jnp.concatenate([rot(t[..., :half], rows), rot(t[..., half:], cols)], axis=-1)


def window_attention(q, k, v, kc, vc, sink):
    B, L = q.shape[:2]
    nb = L // BLOCK
    C = kc.shape[1]
    qb = q.reshape(B, nb, BLOCK, N_KV_HEADS, GQA_GROUP, HEAD_DIM)

    def band(t):
        tp = jnp.pad(t, ((0, 0), (BLOCK, BLOCK), (0, 0), (0, 0)))
        tp = tp.reshape(B, nb + 2, BLOCK, N_KV_HEADS, HEAD_DIM)
        return jnp.concatenate([tp[:, :-2], tp[:, 1:-1], tp[:, 2:]], axis=2)

    kw, vw = band(k), band(v)
    scale = HEAD_DIM ** -0.5
    s_loc = jnp.einsum('bnqhgd,bnjhd->bnhgqj', qb, kw).astype(jnp.float32) * scale
    s_ctx = jnp.einsum('bnqhgd,bchd->bnhgqc', qb, kc).astype(jnp.float32) * scale
    qi = jnp.arange(BLOCK)[:, None]
    kj = jnp.arange(3 * BLOCK)[None, :]
    rel = kj - BLOCK - qi
    kpos = jnp.arange(nb)[:, None, None] * BLOCK - BLOCK + kj[None]
    valid = (jnp.abs(rel) <= WINDOW)[None] & (kpos >= 0) & (kpos < L)
    s_loc = jnp.where(valid[None, :, None, None], s_loc, NEG_INF)
    sink_l = jnp.broadcast_to(
        sink.astype(jnp.float32).reshape(N_KV_HEADS, GQA_GROUP)[None, None, :, :, None, None],
        s_loc.shape[:-1] + (1,))
    logits = jnp.concatenate([sink_l, s_ctx, s_loc], axis=-1)
    probs = jax.nn.softmax(logits, axis=-1).astype(v.dtype)
    p_ctx, p_loc = probs[..., 1:1 + C], probs[..., 1 + C:]
    out = (jnp.einsum('bnhgqc,bchd->bnqhgd', p_ctx, vc)
           + jnp.einsum('bnhgqj,bnjhd->bnqhgd', p_loc, vw))
    return out.reshape(B, L, Q_W)


def context_attention(q, k, v, sink):
    B, C = q.shape[:2]
    qg = q.reshape(B, C, N_KV_HEADS, GQA_GROUP, HEAD_DIM)
    s = jnp.einsum('bqhgd,bkhd->bhgqk', qg, k).astype(jnp.float32) * HEAD_DIM ** -0.5
    sink_l = jnp.broadcast_to(
        sink.astype(jnp.float32).reshape(N_KV_HEADS, GQA_GROUP)[None, :, :, None, None],
        s.shape[:-1] + (1,))
    probs = jax.nn.softmax(jnp.concatenate([sink_l, s], axis=-1), axis=-1).astype(v.dtype)
    out = jnp.einsum('bhgqk,bkhd->bqhgd', probs[..., 1:], v)
    return out.reshape(B, C, Q_W)


def chunk_gmlp(u, vb, ln_g, ln_b, w_s, b_s):
    B, L = u.shape[:2]
    nc = L // CHUNK
    vn = layer_norm(vb, ln_g, ln_b).reshape(B, nc, CHUNK, N_GMLP_GROUPS, GMLP_GROUP_DIM)
    s = jnp.einsum('gij,bnjgd->bnigd', w_s, vn) + b_s.T[None, None, :, :, None]
    return u * s.reshape(B, L, GMLP_WIDTH)


def token_mixer(h, w_in, sink, gmlp_g, gmlp_b, w_s, b_s, w_a, w_b, w_o, kc, vc, pos):
    B, L = h.shape[:2]
    q, k, v, u, vb, ga, gb = jnp.split(h @ w_in, IN_SPLITS, axis=-1)
    q = q.reshape(B, L, N_Q_HEADS, HEAD_DIM)
    k = k.reshape(B, L, N_KV_HEADS, HEAD_DIM)
    v = v.reshape(B, L, N_KV_HEADS, HEAD_DIM)
    if pos is None:
        ya = context_attention(q, k, v, sink)
    else:
        rows, cols = pos
        ya = window_attention(axial_rope(q, rows, cols), axial_rope(k, rows, cols), v, kc, vc, sink)
    yb = chunk_gmlp(jax.nn.gelu(u), jax.nn.gelu(vb), gmlp_g, gmlp_b, w_s, b_s)
    merged = jax.nn.sigmoid(ga) * (ya @ w_a) + jax.nn.sigmoid(gb) * (yb @ w_b)
    return merged @ w_o


def context_kv(hc, w_in):
    B, C = hc.shape[:2]
    k, v = jnp.split(hc @ w_in[:, Q_W:Q_W + 2 * KV_W], 2, axis=-1)
    return (k.reshape(B, C, N_KV_HEADS, HEAD_DIM), v.reshape(B, C, N_KV_HEADS, HEAD_DIM))


def swiglu(h, w_ffn_in, w_ffn_out):
    gate, up = jnp.split(h @ w_ffn_in, 2, axis=-1)
    return (jax.nn.silu(gate) * up) @ w_ffn_out


def post_norm(res, out, g, b):
    return layer_norm(DEEPNORM_ALPHA * res + out, g, b)


def _fwd_setup_inputs(seed: int = 0) -> dict:
    key = jax.random.key(seed)
    ks = jax.random.split(key, 21)
    f32 = jnp.float32

    def nrm(k, shape, s):
        return jax.random.normal(k, shape, f32) * s

    return {
        'x': nrm(ks[0], (BATCH, SEQ, D_MODEL), 1.0),
        'c': nrm(ks[1], (BATCH, D_MODEL), 1.0),
        'ctx': nrm(ks[2], (BATCH, CTX_LEN, D_MODEL), 1.0),
        'c_ctx': nrm(ks[3], (D_MODEL,), 1.0),
        'w_ada': nrm(ks[4], (DEPTH, D_MODEL, 6 * D_MODEL), 0.5 * D_MODEL ** -0.5),
        'b_ada': nrm(ks[5], (DEPTH, 6 * D_MODEL), 0.02),
        'w_in': nrm(ks[6], (DEPTH, D_MODEL, IN_W), D_MODEL ** -0.5),
        'attn_sink': nrm(ks[7], (DEPTH, N_Q_HEADS), 0.5),
        'gmlp_ln_g': 1.0 + nrm(ks[8], (DEPTH, GMLP_WIDTH), 0.02),
        'gmlp_ln_b': nrm(ks[9], (DEPTH, GMLP_WIDTH), 0.02),
        'w_spatial': nrm(ks[10], (DEPTH, N_GMLP_GROUPS, CHUNK, CHUNK), CHUNK ** -0.5),
        'b_spatial': 1.0 + nrm(ks[11], (DEPTH, N_GMLP_GROUPS, CHUNK), 0.02),
        'w_branch_a': nrm(ks[12], (DEPTH, Q_W, D_MODEL), Q_W ** -0.5),
        'w_branch_b': nrm(ks[13], (DEPTH, GMLP_WIDTH, D_MODEL), GMLP_WIDTH ** -0.5),
        'w_out': nrm(ks[14], (DEPTH, D_MODEL, D_MODEL), DEEPNORM_BETA * D_MODEL ** -0.5),
        'ln1_g': 1.0 + nrm(ks[15], (DEPTH, D_MODEL), 0.02),
        'ln1_b': nrm(ks[16], (DEPTH, D_MODEL), 0.02),
        'w_ffn_in': nrm(ks[17], (DEPTH, D_MODEL, 2 * FFN_HIDDEN), D_MODEL ** -0.5),
        'w_ffn_out': nrm(ks[18], (DEPTH, FFN_HIDDEN, D_MODEL), DEEPNORM_BETA * FFN_HIDDEN ** -0.5),
        'ln2_g': 1.0 + nrm(ks[19], (DEPTH, D_MODEL), 0.02),
        'ln2_b': nrm(ks[20], (DEPTH, D_MODEL), 0.02),
    }


def _fwd_reference(x, c, ctx, c_ctx, w_ada, b_ada, w_in, attn_sink, gmlp_ln_g, gmlp_ln_b,
              w_spatial, b_spatial, w_branch_a, w_branch_b, w_out, ln1_g, ln1_b,
              w_ffn_in, w_ffn_out, ln2_g, ln2_b):
    L = x.shape[1]
    n_rows = L // GRID_W
    rows = jnp.repeat(jnp.arange(n_rows, dtype=jnp.int32), GRID_W)
    cols = jnp.tile(jnp.arange(GRID_W, dtype=jnp.int32), n_rows)

    for layer in range(DEPTH):
        mod_x = jnp.split(jax.nn.silu(c) @ w_ada[layer] + b_ada[layer], 6, axis=-1)
        mod_c = jnp.split(jax.nn.silu(c_ctx) @ w_ada[layer] + b_ada[layer], 6, axis=-1)
        mix_params = (w_in[layer], attn_sink[layer], gmlp_ln_g[layer], gmlp_ln_b[layer],
                      w_spatial[layer], b_spatial[layer], w_branch_a[layer], w_branch_b[layer],
                      w_out[layer])

        hc = modulate(layer_norm(ctx), mod_c[0], mod_c[1])
        kc, vc = context_kv(hc, w_in[layer])

        h = modulate(layer_norm(x), mod_x[0], mod_x[1])
        mix = token_mixer(h, *mix_params, kc, vc, (rows, cols))
        x_mid = post_norm(x, mod_x[2][:, None, :] * mix, ln1_g[layer], ln1_b[layer])
        h2 = modulate(layer_norm(x_mid), mod_x[3], mod_x[4])
        x_new = post_norm(x_mid, mod_x[5][:, None, :] * swiglu(h2, w_ffn_in[layer], w_ffn_out[layer]),
                          ln2_g[layer], ln2_b[layer])

        if layer < DEPTH - 1:
            mix_c = token_mixer(hc, *mix_params, None, None, None)
            ctx_mid = post_norm(ctx, mod_c[2] * mix_c, ln1_g[layer], ln1_b[layer])
            h2c = modulate(layer_norm(ctx_mid), mod_c[3], mod_c[4])
            ctx = post_norm(ctx_mid, mod_c[5] * swiglu(h2c, w_ffn_in[layer], w_ffn_out[layer]),
                            ln2_g[layer], ln2_b[layer])
        x = x_new
    return x


import jax as _jax
import jax.numpy as _jnp

TWIN_FORMAT = 'train_step'
FWD_PARAMS = ['x', 'c', 'ctx', 'c_ctx', 'w_ada', 'b_ada', 'w_in', 'attn_sink', 'gmlp_ln_g', 'gmlp_ln_b', 'w_spatial', 'b_spatial', 'w_branch_a', 'w_branch_b', 'w_out', 'ln1_g', 'ln1_b', 'w_ffn_in', 'w_ffn_out', 'ln2_g', 'ln2_b']
TWIN_WEIGHTS = ['c_ctx', 'w_ada', 'b_ada', 'w_in', 'attn_sink', 'gmlp_ln_g', 'gmlp_ln_b', 'w_spatial', 'b_spatial', 'w_branch_a', 'w_branch_b', 'w_out', 'ln1_g', 'ln1_b', 'w_ffn_in', 'w_ffn_out', 'ln2_g', 'ln2_b']
TWIN_DIFF_INPUT = 'x'
TWIN_INPUTS = ['x', 'c', 'ctx', 'c_ctx', 'w_ada', 'b_ada', 'w_in', 'attn_sink', 'gmlp_ln_g', 'gmlp_ln_b', 'w_spatial', 'b_spatial', 'w_branch_a', 'w_branch_b', 'w_out', 'ln1_g', 'ln1_b', 'w_ffn_in', 'w_ffn_out', 'ln2_g', 'ln2_b', 'loss_target', 'm_c_ctx', 'm_w_ada', 'm_b_ada', 'm_w_in', 'm_attn_sink', 'm_gmlp_ln_g', 'm_gmlp_ln_b', 'm_w_spatial', 'm_b_spatial', 'm_w_branch_a', 'm_w_branch_b', 'm_w_out', 'm_ln1_g', 'm_ln1_b', 'm_w_ffn_in', 'm_w_ffn_out', 'm_ln2_g', 'm_ln2_b', 'v_c_ctx', 'v_w_ada', 'v_b_ada', 'v_w_in', 'v_attn_sink', 'v_gmlp_ln_g', 'v_gmlp_ln_b', 'v_w_spatial', 'v_b_spatial', 'v_w_branch_a', 'v_w_branch_b', 'v_w_out', 'v_ln1_g', 'v_ln1_b', 'v_w_ffn_in', 'v_w_ffn_out', 'v_ln2_g', 'v_ln2_b']
TWIN_OUTPUTS = ['loss', 'grad_x', 'grad_c_ctx', 'grad_w_ada', 'grad_b_ada', 'grad_w_in', 'grad_attn_sink', 'grad_gmlp_ln_g', 'grad_gmlp_ln_b', 'grad_w_spatial', 'grad_b_spatial', 'grad_w_branch_a', 'grad_w_branch_b', 'grad_w_out', 'grad_ln1_g', 'grad_ln1_b', 'grad_w_ffn_in', 'grad_w_ffn_out', 'grad_ln2_g', 'grad_ln2_b', 'delta_c_ctx', 'delta_w_ada', 'delta_b_ada', 'delta_w_in', 'delta_attn_sink', 'delta_gmlp_ln_g', 'delta_gmlp_ln_b', 'delta_w_spatial', 'delta_b_spatial', 'delta_w_branch_a', 'delta_w_branch_b', 'delta_w_out', 'delta_ln1_g', 'delta_ln1_b', 'delta_w_ffn_in', 'delta_w_ffn_out', 'delta_ln2_g', 'delta_ln2_b', 'new_m_c_ctx', 'new_m_w_ada', 'new_m_b_ada', 'new_m_w_in', 'new_m_attn_sink', 'new_m_gmlp_ln_g', 'new_m_gmlp_ln_b', 'new_m_w_spatial', 'new_m_b_spatial', 'new_m_w_branch_a', 'new_m_w_branch_b', 'new_m_w_out', 'new_m_ln1_g', 'new_m_ln1_b', 'new_m_w_ffn_in', 'new_m_w_ffn_out', 'new_m_ln2_g', 'new_m_ln2_b', 'new_v_c_ctx', 'new_v_w_ada', 'new_v_b_ada', 'new_v_w_in', 'new_v_attn_sink', 'new_v_gmlp_ln_g', 'new_v_gmlp_ln_b', 'new_v_w_spatial', 'new_v_b_spatial', 'new_v_w_branch_a', 'new_v_w_branch_b', 'new_v_w_out', 'new_v_ln1_g', 'new_v_ln1_b', 'new_v_w_ffn_in', 'new_v_w_ffn_out', 'new_v_ln2_g', 'new_v_ln2_b']
TWIN_LEAF_KINDS = {'loss': 'loss', 'grad_x': 'grad_x', 'grad_c_ctx': 'grad_w', 'grad_w_ada': 'grad_w', 'grad_b_ada': 'grad_w', 'grad_w_in': 'grad_w', 'grad_attn_sink': 'grad_w', 'grad_gmlp_ln_g': 'grad_w', 'grad_gmlp_ln_b': 'grad_w', 'grad_w_spatial': 'grad_w', 'grad_b_spatial': 'grad_w', 'grad_w_branch_a': 'grad_w', 'grad_w_branch_b': 'grad_w', 'grad_w_out': 'grad_w', 'grad_ln1_g': 'grad_w', 'grad_ln1_b': 'grad_w', 'grad_w_ffn_in': 'grad_w', 'grad_w_ffn_out': 'grad_w', 'grad_ln2_g': 'grad_w', 'grad_ln2_b': 'grad_w', 'delta_c_ctx': 'delta_w', 'delta_w_ada': 'delta_w', 'delta_b_ada': 'delta_w', 'delta_w_in': 'delta_w', 'delta_attn_sink': 'delta_w', 'delta_gmlp_ln_g': 'delta_w', 'delta_gmlp_ln_b': 'delta_w', 'delta_w_spatial': 'delta_w', 'delta_b_spatial': 'delta_w', 'delta_w_branch_a': 'delta_w', 'delta_w_branch_b': 'delta_w', 'delta_w_out': 'delta_w', 'delta_ln1_g': 'delta_w', 'delta_ln1_b': 'delta_w', 'delta_w_ffn_in': 'delta_w', 'delta_w_ffn_out': 'delta_w', 'delta_ln2_g': 'delta_w', 'delta_ln2_b': 'delta_w', 'new_m_c_ctx': 'new_m', 'new_m_w_ada': 'new_m', 'new_m_b_ada': 'new_m', 'new_m_w_in': 'new_m', 'new_m_attn_sink': 'new_m', 'new_m_gmlp_ln_g': 'new_m', 'new_m_gmlp_ln_b': 'new_m', 'new_m_w_spatial': 'new_m', 'new_m_b_spatial': 'new_m', 'new_m_w_branch_a': 'new_m', 'new_m_w_branch_b': 'new_m', 'new_m_w_out': 'new_m', 'new_m_ln1_g': 'new_m', 'new_m_ln1_b': 'new_m', 'new_m_w_ffn_in': 'new_m', 'new_m_w_ffn_out': 'new_m', 'new_m_ln2_g': 'new_m', 'new_m_ln2_b': 'new_m', 'new_v_c_ctx': 'new_v', 'new_v_w_ada': 'new_v', 'new_v_b_ada': 'new_v', 'new_v_w_in': 'new_v', 'new_v_attn_sink': 'new_v', 'new_v_gmlp_ln_g': 'new_v', 'new_v_gmlp_ln_b': 'new_v', 'new_v_w_spatial': 'new_v', 'new_v_b_spatial': 'new_v', 'new_v_w_branch_a': 'new_v', 'new_v_w_branch_b': 'new_v', 'new_v_w_out': 'new_v', 'new_v_ln1_g': 'new_v', 'new_v_ln1_b': 'new_v', 'new_v_w_ffn_in': 'new_v', 'new_v_w_ffn_out': 'new_v', 'new_v_ln2_g': 'new_v', 'new_v_ln2_b': 'new_v'}


def _forward(args):
    return _fwd_reference(*[args[k] for k in FWD_PARAMS])


def _output_shape():
    def fwd():
        inp = _fwd_setup_inputs(0)
        return _fwd_reference(*[inp[k] for k in FWD_PARAMS])
    out = _jax.eval_shape(fwd)
    return out.shape, out.dtype

N_MICROBATCH = 1
ADAM_LR = 0.001
ADAM_B1 = 0.9
ADAM_B2 = 0.999
ADAM_EPS = 1e-08
ADAM_WD = 0.01
ADAM_STEP = 10
PER_EXAMPLE_BATCH_AXIS = {'x': 0, 'c': 0, 'ctx': 0, 'loss_target': 0}
SHARED_INPUTS = []
_WEIGHT_DTYPES = {'c_ctx': _jnp.float32, 'w_ada': _jnp.float32, 'b_ada': _jnp.float32, 'w_in': _jnp.float32, 'attn_sink': _jnp.float32, 'gmlp_ln_g': _jnp.float32, 'gmlp_ln_b': _jnp.float32, 'w_spatial': _jnp.float32, 'b_spatial': _jnp.float32, 'w_branch_a': _jnp.float32, 'w_branch_b': _jnp.float32, 'w_out': _jnp.float32, 'ln1_g': _jnp.float32, 'ln1_b': _jnp.float32, 'w_ffn_in': _jnp.float32, 'w_ffn_out': _jnp.float32, 'ln2_g': _jnp.float32, 'ln2_b': _jnp.float32}
MOMENT_SCALE = {'c_ctx': 4.687359e-03, 'w_ada': 3.427619e-02, 'b_ada': 6.290545e-02, 'w_in': 1.571715e-02, 'attn_sink': 2.019512e-04, 'gmlp_ln_g': 2.308218e-02, 'gmlp_ln_b': 2.191468e-02, 'w_spatial': 1.483250e-02, 'b_spatial': 1.508949e-02, 'w_branch_a': 6.716754e-03, 'w_branch_b': 2.211657e-02, 'w_out': 3.873289e-02, 'ln1_g': 2.207508e+00, 'ln1_b': 9.791911e-01, 'w_ffn_in': 1.654848e-02, 'w_ffn_out': 4.569711e-02, 'ln2_g': 6.398484e+01, 'ln2_b': 1.663535e+00}


def _to_microbatches(a, axis):
    t = _jnp.moveaxis(a, axis, 0)
    t = t.reshape((N_MICROBATCH, t.shape[0] // N_MICROBATCH) + t.shape[1:])
    return _jnp.moveaxis(t, 1, axis + 1)


def setup_inputs(seed: int = 0) -> dict:
    inp = _fwd_setup_inputs(seed)
    key = _jax.random.fold_in(_jax.random.key(seed), 7919)
    shape, _ = _output_shape()
    out = dict(inp)
    out["loss_target"] = _jax.random.normal(_jax.random.fold_in(key, 0), shape, _jnp.float32)
    for i, name in enumerate(TWIN_WEIGHTS):
        w = inp[name].astype(_jnp.float32)
        if MOMENT_SCALE is None:
            s = _jnp.sqrt(_jnp.mean(_jnp.square(w)) + 1e-30)
        else:
            s = MOMENT_SCALE[name]
        km, kv = _jax.random.split(_jax.random.fold_in(key, i + 1))
        out[name] = w
        out["m_" + name] = s * _jax.random.normal(km, w.shape, _jnp.float32)
        out["v_" + name] = (s * s) * _jax.random.uniform(kv, w.shape, _jnp.float32, 0.5, 1.5)
    if N_MICROBATCH > 1:
        for name, axis in PER_EXAMPLE_BATCH_AXIS.items():
            out[name] = _to_microbatches(out[name], axis)
    return {'x': out['x'], 'c': out['c'], 'ctx': out['ctx'], 'c_ctx': out['c_ctx'], 'w_ada': out['w_ada'], 'b_ada': out['b_ada'], 'w_in': out['w_in'], 'attn_sink': out['attn_sink'], 'gmlp_ln_g': out['gmlp_ln_g'], 'gmlp_ln_b': out['gmlp_ln_b'], 'w_spatial': out['w_spatial'], 'b_spatial': out['b_spatial'], 'w_branch_a': out['w_branch_a'], 'w_branch_b': out['w_branch_b'], 'w_out': out['w_out'], 'ln1_g': out['ln1_g'], 'ln1_b': out['ln1_b'], 'w_ffn_in': out['w_ffn_in'], 'w_ffn_out': out['w_ffn_out'], 'ln2_g': out['ln2_g'], 'ln2_b': out['ln2_b'], 'loss_target': out['loss_target'], 'm_c_ctx': out['m_c_ctx'], 'm_w_ada': out['m_w_ada'], 'm_b_ada': out['m_b_ada'], 'm_w_in': out['m_w_in'], 'm_attn_sink': out['m_attn_sink'], 'm_gmlp_ln_g': out['m_gmlp_ln_g'], 'm_gmlp_ln_b': out['m_gmlp_ln_b'], 'm_w_spatial': out['m_w_spatial'], 'm_b_spatial': out['m_b_spatial'], 'm_w_branch_a': out['m_w_branch_a'], 'm_w_branch_b': out['m_w_branch_b'], 'm_w_out': out['m_w_out'], 'm_ln1_g': out['m_ln1_g'], 'm_ln1_b': out['m_ln1_b'], 'm_w_ffn_in': out['m_w_ffn_in'], 'm_w_ffn_out': out['m_w_ffn_out'], 'm_ln2_g': out['m_ln2_g'], 'm_ln2_b': out['m_ln2_b'], 'v_c_ctx': out['v_c_ctx'], 'v_w_ada': out['v_w_ada'], 'v_b_ada': out['v_b_ada'], 'v_w_in': out['v_w_in'], 'v_attn_sink': out['v_attn_sink'], 'v_gmlp_ln_g': out['v_gmlp_ln_g'], 'v_gmlp_ln_b': out['v_gmlp_ln_b'], 'v_w_spatial': out['v_w_spatial'], 'v_b_spatial': out['v_b_spatial'], 'v_w_branch_a': out['v_w_branch_a'], 'v_w_branch_b': out['v_w_branch_b'], 'v_w_out': out['v_w_out'], 'v_ln1_g': out['v_ln1_g'], 'v_ln1_b': out['v_ln1_b'], 'v_w_ffn_in': out['v_w_ffn_in'], 'v_w_ffn_out': out['v_w_ffn_out'], 'v_ln2_g': out['v_ln2_g'], 'v_ln2_b': out['v_ln2_b']}


def _loss(weights, diff, rest, loss_target):
    with _jax.named_scope("forward"):
        args = {**rest, TWIN_DIFF_INPUT: diff, **{k: w.astype(_WEIGHT_DTYPES[k]) for k, w in weights.items()}}
        y = _forward(args)
    with _jax.named_scope("loss_head"):
        err = _jnp.square(y.astype(_jnp.float32) - loss_target)
        return 0.5 * _jnp.sum(_jnp.mean(err, axis=-1)) if err.ndim else 0.5 * err


def _adamw(w, g, m, v):
    m = ADAM_B1 * m + (1.0 - ADAM_B1) * g
    v = ADAM_B2 * v + (1.0 - ADAM_B2) * _jnp.square(g)
    m_hat = m / (1.0 - ADAM_B1 ** ADAM_STEP)
    v_hat = v / (1.0 - ADAM_B2 ** ADAM_STEP)
    delta = -ADAM_LR * (m_hat / (_jnp.sqrt(v_hat) + ADAM_EPS) + ADAM_WD * w)
    return delta, m, v


def reference(x, c, ctx, c_ctx, w_ada, b_ada, w_in, attn_sink, gmlp_ln_g, gmlp_ln_b, w_spatial, b_spatial, w_branch_a, w_branch_b, w_out, ln1_g, ln1_b, w_ffn_in, w_ffn_out, ln2_g, ln2_b, loss_target, m_c_ctx, m_w_ada, m_b_ada, m_w_in, m_attn_sink, m_gmlp_ln_g, m_gmlp_ln_b, m_w_spatial, m_b_spatial, m_w_branch_a, m_w_branch_b, m_w_out, m_ln1_g, m_ln1_b, m_w_ffn_in, m_w_ffn_out, m_ln2_g, m_ln2_b, v_c_ctx, v_w_ada, v_b_ada, v_w_in, v_attn_sink, v_gmlp_ln_g, v_gmlp_ln_b, v_w_spatial, v_b_spatial, v_w_branch_a, v_w_branch_b, v_w_out, v_ln1_g, v_ln1_b, v_w_ffn_in, v_w_ffn_out, v_ln2_g, v_ln2_b):
    given = dict(x=x, c=c, ctx=ctx, c_ctx=c_ctx, w_ada=w_ada, b_ada=b_ada, w_in=w_in, attn_sink=attn_sink, gmlp_ln_g=gmlp_ln_g, gmlp_ln_b=gmlp_ln_b, w_spatial=w_spatial, b_spatial=b_spatial, w_branch_a=w_branch_a, w_branch_b=w_branch_b, w_out=w_out, ln1_g=ln1_g, ln1_b=ln1_b, w_ffn_in=w_ffn_in, w_ffn_out=w_ffn_out, ln2_g=ln2_g, ln2_b=ln2_b, loss_target=loss_target, m_c_ctx=m_c_ctx, m_w_ada=m_w_ada, m_b_ada=m_b_ada, m_w_in=m_w_in, m_attn_sink=m_attn_sink, m_gmlp_ln_g=m_gmlp_ln_g, m_gmlp_ln_b=m_gmlp_ln_b, m_w_spatial=m_w_spatial, m_b_spatial=m_b_spatial, m_w_branch_a=m_w_branch_a, m_w_branch_b=m_w_branch_b, m_w_out=m_w_out, m_ln1_g=m_ln1_g, m_ln1_b=m_ln1_b, m_w_ffn_in=m_w_ffn_in, m_w_ffn_out=m_w_ffn_out, m_ln2_g=m_ln2_g, m_ln2_b=m_ln2_b, v_c_ctx=v_c_ctx, v_w_ada=v_w_ada, v_b_ada=v_b_ada, v_w_in=v_w_in, v_attn_sink=v_attn_sink, v_gmlp_ln_g=v_gmlp_ln_g, v_gmlp_ln_b=v_gmlp_ln_b, v_w_spatial=v_w_spatial, v_b_spatial=v_b_spatial, v_w_branch_a=v_w_branch_a, v_w_branch_b=v_w_branch_b, v_w_out=v_w_out, v_ln1_g=v_ln1_g, v_ln1_b=v_ln1_b, v_w_ffn_in=v_w_ffn_in, v_w_ffn_out=v_w_ffn_out, v_ln2_g=v_ln2_g, v_ln2_b=v_ln2_b)
    weights = {n: given[n] for n in TWIN_WEIGHTS}
    shared = {n: given[n] for n in SHARED_INPUTS}
    per_example = {n: given[n] for n in ['x', 'c', 'ctx']}
    grad_fn = _jax.value_and_grad(_loss, argnums=(0, 1))

    def one_microbatch(ex, loss_target):
        ex = dict(ex)
        diff = ex.pop(TWIN_DIFF_INPUT)
        return grad_fn(weights, diff, {**shared, **ex}, loss_target)

    if N_MICROBATCH == 1:
        loss, (grad_w, grad_x) = one_microbatch(per_example, given["loss_target"])
    else:
        def body(carry, xs):
            loss_sum, grad_sum = carry
            l_k, (gw_k, gx_k) = one_microbatch(xs[0], xs[1])
            with _jax.named_scope("update"):
                return (loss_sum + l_k, _jax.tree.map(_jnp.add, grad_sum, gw_k)), gx_k

        init = (_jnp.zeros((), _jnp.float32), _jax.tree.map(_jnp.zeros_like, weights))
        (loss, grad_w), grad_x = _jax.lax.scan(body, init, (per_example, given["loss_target"]))
    with _jax.named_scope("update"):
        delta_w, new_m, new_v = {}, {}, {}
        for n in TWIN_WEIGHTS:
            delta_w[n], new_m[n], new_v[n] = _adamw(weights[n], grad_w[n], given["m_" + n], given["v_" + n])
    return (loss, grad_x, *[grad_w[n] for n in TWIN_WEIGHTS], *[delta_w[n] for n in TWIN_WEIGHTS],
            *[new_m[n] for n in TWIN_WEIGHTS], *[new_v[n] for n in TWIN_WEIGHTS])
```

```python
import functools
import math

import jax
import jax.numpy as jnp
from jax import lax
from jax.experimental import pallas as pl
from jax.experimental.pallas import tpu as pltpu

F32 = jnp.float32
BF16 = jnp.bfloat16
MESH = pl.DeviceIdType.MESH

N_DEV = 8
D = 1024
HEAD_DIM = 64
N_Q_HEADS = 8
N_KV_HEADS = 2
GQA_GROUP = 4
BLK = 128
Q_W = 512
KV_W = 128
GM_W = 512
N_GROUPS = 8
GROUP_DIM = 64
FFN_H = 2816
IN_W = 3840
O_Q, O_K, O_V, O_U, O_VB, O_GA, O_GB = 0, 512, 640, 768, 1280, 1792, 2816
LN_EPS = 1e-5
NEG_INF = -1e30
ALPHA = 2.0 ** 0.25
ROPE_BASE = 10000.0
ROPE_PAIRS = 16
Q_SCALE = HEAD_DIM ** -0.5
GELU_K0 = math.sqrt(2.0 / math.pi)
GELU_K1 = 0.044715

ADAM_LR = 0.001
ADAM_B1 = 0.9
ADAM_B2 = 0.999
ADAM_EPS = 1e-08
ADAM_WD = 0.01
ADAM_STEP = 10

VMEM_LIMIT = 56 * 1024 * 1024
SMALL_ROWS = 144
NT = (((1,), (1,)), ((), ()))
TN = (((0,), (0,)), ((), ()))


def _params(sem=None):
    return pltpu.CompilerParams(dimension_semantics=sem, vmem_limit_bytes=VMEM_LIMIT)


def _row(tm, w):
    return pl.BlockSpec((tm, w), lambda i: (i, 0))


def _full(shape):
    nd = len(shape)
    return pl.BlockSpec(shape, lambda i: (0,) * nd)


def _resident(shape):
    nd = len(shape)
    return pl.BlockSpec(shape, lambda i: (0,) * nd, pipeline_mode=pl.Buffered(1))


def _sds(shape, dt):
    return jax.ShapeDtypeStruct(shape, dt)


def _ln(xf):
    mu = jnp.mean(xf, axis=-1, keepdims=True)
    xc = xf - mu
    var = jnp.mean(xc * xc, axis=-1, keepdims=True)
    rstd = lax.rsqrt(var + LN_EPS)
    return xc * rstd, rstd


def _ln_bwd(dn, n, rstd):
    m1 = jnp.mean(dn, axis=-1, keepdims=True)
    m2 = jnp.mean(dn * n, axis=-1, keepdims=True)
    return rstd * (dn - m1 - n * m2)


def _colsum(t):
    return jnp.sum(t, axis=0, keepdims=True)


def _sigmoid(x):
    return 1.0 / (1.0 + jnp.exp(-x))


def _gelu(x):
    t = jnp.tanh(GELU_K0 * (x + GELU_K1 * (x * x * x)))
    return x * (0.5 * (1.0 + t)), t


def _gelu_grad(x, t):
    return 0.5 * (1.0 + t) + 0.5 * x * (1.0 - t * t) * (GELU_K0 * (1.0 + 3.0 * GELU_K1 * x * x))


def _swap16(t):
    lane = lax.broadcasted_iota(jnp.int32, t.shape, 1)
    return jnp.where((lane & 16) == 0, pltpu.roll(t, 112, 1), pltpu.roll(t, 16, 1))


def _rope(t, cos, sin):
    return t * cos + _swap16(t) * sin


def _unrope(t, cos, sin):
    return t * cos - _swap16(t) * sin


def _adamw(w, g, m, v):
    m2 = ADAM_B1 * m + (1.0 - ADAM_B1) * g
    v2 = ADAM_B2 * v + (1.0 - ADAM_B2) * (g * g)
    m_hat = m2 / (1.0 - ADAM_B1 ** ADAM_STEP)
    v_hat = v2 / (1.0 - ADAM_B2 ** ADAM_STEP)
    delta = -ADAM_LR * (m_hat / (jnp.sqrt(v_hat) + ADAM_EPS) + ADAM_WD * w)
    return delta, m2, v2


def _rope_tables(L):
    inv = ROPE_BASE ** (-jnp.arange(ROPE_PAIRS, dtype=F32) / ROPE_PAIRS)
    t = jnp.arange(L, dtype=jnp.int32)
    rows = (t // 64).astype(F32)[:, None] * inv
    cols = (t % 64).astype(F32)[:, None] * inv
    cr, sr, cc, sc = jnp.cos(rows), jnp.sin(rows), jnp.cos(cols), jnp.sin(cols)
    cos = jnp.concatenate([cr, cr, cc, cc], axis=1)
    sin = jnp.concatenate([-sr, sr, -sc, sc], axis=1)
    return jnp.tile(cos, (1, 2)), jnp.tile(sin, (1, 2))


def _me():
    return lax.axis_index("x"), lax.axis_index("y"), lax.axis_index("c")


def _peer(mx, my, mc, k):
    return (mx ^ ((k >> 2) & 1), my ^ ((k >> 1) & 1), mc ^ (k & 1))


def _ag_small(x, name):
    R, C = x.shape

    def body(x_ref, out_ref, send_sems, recv_sems):
        mx, my, mc = _me()
        me = 4 * mx + 2 * my + mc
        out_ref[pl.ds(me, 1)] = x_ref[...][None]
        sends = []
        for k in range(1, N_DEV):
            cp = pltpu.make_async_remote_copy(
                src_ref=x_ref, dst_ref=out_ref.at[me], send_sem=send_sems.at[k - 1], recv_sem=recv_sems.at[k - 1],
                device_id=_peer(mx, my, mc, k), device_id_type=MESH)
            cp.start()
            sends.append(cp)
        for k in range(1, N_DEV):
            pltpu.make_async_remote_copy(
                src_ref=x_ref, dst_ref=out_ref.at[me ^ k], send_sem=send_sems.at[k - 1], recv_sem=recv_sems.at[k - 1],
                device_id=(mx, my, mc), device_id_type=MESH).wait_recv()
        for cp in sends:
            cp.wait_send()

    return pl.pallas_call(
        body, name=name,
        out_shape=_sds((N_DEV, R, C), x.dtype),
        in_specs=[pl.BlockSpec(memory_space=pltpu.VMEM)],
        out_specs=pl.BlockSpec(memory_space=pltpu.VMEM),
        scratch_shapes=[pltpu.SemaphoreType.DMA((N_DEV - 1,)), pltpu.SemaphoreType.DMA((N_DEV - 1,))],
        compiler_params=pltpu.CompilerParams(vmem_limit_bytes=VMEM_LIMIT),
    )(x)


def _ag_big(shards, name):
    n = len(shards)

    def body(*refs):
        x_refs, out_refs = refs[:n], refs[n:2 * n]
        send_sems, recv_sems, local_sems = refs[2 * n:]
        mx, my, mc = _me()
        me = 4 * mx + 2 * my + mc
        sib = me ^ 1
        sibling = (mx, my, 1 - mc)
        chips = [(1 - mx, my), (mx, 1 - my), (1 - mx, 1 - my)]

        def idx(px, py, pc):
            return 4 * px + 2 * py + pc

        def copy(a, k, block, to, src=None):
            dst = out_refs[a].at[block]
            return pltpu.make_async_remote_copy(
                src_ref=dst if src is None else src, dst_ref=dst,
                send_sem=send_sems.at[a * 7 + k], recv_sem=recv_sems.at[a * 7 + k],
                device_id=to, device_id_type=MESH)

        mine = [pltpu.make_async_copy(x_refs[a], out_refs[a].at[me], local_sems.at[a]) for a in range(n)]
        for cp in mine:
            cp.start()
        first = []
        for a in range(n):
            first.append(copy(a, 0, me, sibling, src=x_refs[a]))
            for j, chip in enumerate(chips):
                first.append(copy(a, 1 + j, me, (*chip, mc), src=x_refs[a]))
        for cp in first:
            cp.start()
        passed = []
        for j, chip in enumerate(chips):
            for a in range(n):
                copy(a, 1 + j, idx(*chip, mc), (mx, my, mc)).wait_recv()
                cp = copy(a, 4 + j, idx(*chip, mc), sibling)
                cp.start()
                passed.append(cp)
        for a in range(n):
            copy(a, 0, sib, (mx, my, mc)).wait_recv()
            for j, chip in enumerate(chips):
                copy(a, 4 + j, idx(*chip, 1 - mc), (mx, my, mc)).wait_recv()
        for cp in first + passed:
            cp.wait_send()
        for cp in mine:
            cp.wait()

    any_spec = pl.BlockSpec(memory_space=pl.ANY)
    return pl.pallas_call(
        body, name=name,
        out_shape=[_sds((N_DEV,) + s.shape, s.dtype) for s in shards],
        in_specs=[any_spec] * n, out_specs=[any_spec] * n,
        scratch_shapes=[pltpu.SemaphoreType.DMA((7 * n,)), pltpu.SemaphoreType.DMA((7 * n,)),
                        pltpu.SemaphoreType.DMA((n,))],
    )(*shards)


def _rs_big(blocks, name):
    n = len(blocks)

    def body(*refs):
        x_refs, out_refs = refs[:n], refs[n:2 * n]
        send_sems, recv_sems, local_sems = refs[2 * n:]
        mx, my, mc = _me()
        me = 4 * mx + 2 * my + mc
        mine = [pltpu.make_async_copy(x_refs[a].at[me], out_refs[a].at[me], local_sems.at[a]) for a in range(n)]
        for cp in mine:
            cp.start()
        sends = []
        for k in range(1, N_DEV):
            for a in range(n):
                cp = pltpu.make_async_remote_copy(
                    src_ref=x_refs[a].at[me ^ k], dst_ref=out_refs[a].at[me],
                    send_sem=send_sems.at[a * 7 + k - 1], recv_sem=recv_sems.at[a * 7 + k - 1],
                    device_id=_peer(mx, my, mc, k), device_id_type=MESH)
                cp.start()
                sends.append(cp)
        for k in range(1, N_DEV):
            for a in range(n):
                pltpu.make_async_remote_copy(
                    src_ref=x_refs[a].at[me], dst_ref=out_refs[a].at[me ^ k],
                    send_sem=send_sems.at[a * 7 + k - 1], recv_sem=recv_sems.at[a * 7 + k - 1],
                    device_id=(mx, my, mc), device_id_type=MESH).wait_recv()
        for cp in sends:
            cp.wait_send()
        for cp in mine:
            cp.wait()

    any_spec = pl.BlockSpec(memory_space=pl.ANY)
    return pl.pallas_call(
        body, name=name,
        out_shape=[_sds(b.shape, b.dtype) for b in blocks],
        in_specs=[any_spec] * n, out_specs=[any_spec] * n,
        scratch_shapes=[pltpu.SemaphoreType.DMA((7 * n,)), pltpu.SemaphoreType.DMA((7 * n,)),
                        pltpu.SemaphoreType.DMA((n,))],
    )(*blocks)


def _ada_fwd(s_in, w_ada, b_my):
    nw = w_ada.shape[1]

    def body(s_ref, w_ref, b_ref, act_ref, out_ref):
        s = s_ref[...]
        act = s * _sigmoid(s)
        act_ref[...] = act
        out_ref[...] = jnp.dot(act.astype(BF16), w_ref[...].astype(BF16), preferred_element_type=F32) + b_ref[...]

    return pl.pallas_call(
        body, name="ada_fwd", grid=(1,),
        in_specs=[_full((16, D)), _full((D, nw)), _full((1, nw))],
        out_specs=[_full((16, D)), _full((16, nw))],
        out_shape=[_sds((16, D), F32), _sds((16, nw), F32)],
        compiler_params=_params(("arbitrary",)),
    )(s_in, w_ada, b_my)


def _ada_bwd(act, dmod_my, w_ada, m, v, tr=256):
    nw = w_ada.shape[1]

    def body(act_ref, dm_ref, w_ref, m_ref, v_ref, g_ref, d_ref, m2_ref, v2_ref, pc_ref):
        dm = dm_ref[...].astype(BF16)
        g = lax.dot_general(act_ref[...].astype(BF16), dm, TN, preferred_element_type=F32)
        w = w_ref[...]
        delta, m2, v2 = _adamw(w, g, m_ref[...], v_ref[...])
        g_ref[...] = g
        d_ref[...] = delta
        m2_ref[...] = m2
        v2_ref[...] = v2
        pc_ref[...] = lax.dot_general(dm[8:16, :], w.astype(BF16), NT, preferred_element_type=F32)

    wspec = _row(tr, nw)
    return pl.pallas_call(
        body, name="ada_bwd", grid=(D // tr,),
        in_specs=[pl.BlockSpec((16, tr), lambda i: (0, i)), _full((16, nw)), wspec, wspec, wspec],
        out_specs=[wspec, wspec, wspec, wspec, pl.BlockSpec((8, tr), lambda i: (0, i))],
        out_shape=[_sds((D, nw), F32)] * 4 + [_sds((8, D), F32)],
        compiler_params=_params(("arbitrary",)),
    )(act, dmod_my, w_ada, m, v)


def _k_in(x, modv, w_in, cos, sin, tm):
    L = x.shape[0]

    def body(x_ref, mod_ref, w_ref, cos_ref, sin_ref, h_ref, q_ref, k_ref, v_ref, u_ref, vb_ref, ga_ref, gb_ref):
        n, _ = _ln(x_ref[...])
        h = (n * (1.0 + mod_ref[1:2, :]) + mod_ref[0:1, :]).astype(BF16)
        h_ref[...] = h
        c, s = cos_ref[...], sin_ref[...]

        def proj(lo, width):
            return jnp.dot(h, w_ref[:, lo:lo + width], preferred_element_type=F32)

        for i in range(4):
            q_ref[:, i * 128:(i + 1) * 128] = (_rope(proj(O_Q + i * 128, 128), c, s) * Q_SCALE).astype(BF16)
        k_ref[...] = _rope(proj(O_K, KV_W), c, s).astype(BF16)
        v_ref[...] = proj(O_V, KV_W).astype(BF16)
        u_ref[...] = proj(O_U, GM_W).astype(BF16)
        vb_ref[...] = proj(O_VB, GM_W).astype(BF16)
        ga_ref[...] = proj(O_GA, D).astype(BF16)
        gb_ref[...] = proj(O_GB, D).astype(BF16)

    widths = [D, Q_W, KV_W, KV_W, GM_W, GM_W, D, D]
    return pl.pallas_call(
        body, name="fwd_in", grid=(L // tm,),
        in_specs=[_row(tm, D), _full((8, D)), _resident((D, IN_W)), _row(tm, 128), _row(tm, 128)],
        out_specs=[_row(tm, w) for w in widths],
        out_shape=[_sds((L, w), BF16) for w in widths],
        compiler_params=_params(("arbitrary",)),
    )(x, modv, w_in, cos, sin)


def _k_ctx(ctx, modc, w_kv):
    C = ctx.shape[0]

    def body(c_ref, mod_ref, w_ref, hc_ref, kc_ref, vc_ref):
        n, _ = _ln(c_ref[...])
        hc = (n * (1.0 + mod_ref[1:2, :]) + mod_ref[0:1, :]).astype(BF16)
        hc_ref[...] = hc
        kv = jnp.dot(hc, w_ref[...], preferred_element_type=F32)
        kc_ref[...] = kv[:, :KV_W].astype(BF16)
        vc_ref[...] = kv[:, KV_W:].astype(BF16)

    return pl.pallas_call(
        body, name="fwd_ctx", grid=(1,),
        in_specs=[_full((C, D)), _full((8, D)), _full((D, 2 * KV_W))],
        out_specs=[_full((C, D)), _full((C, KV_W)), _full((C, KV_W))],
        out_shape=[_sds((C, D), BF16), _sds((C, KV_W), BF16), _sds((C, KV_W), BF16)],
        compiler_params=_params(("arbitrary",)),
    )(ctx, modc, w_kv)


def _attn_mask(n, L, C):
    shape = (GQA_GROUP * BLK, C + 3 * BLK)
    r = lax.broadcasted_iota(jnp.int32, shape, 0) & (BLK - 1)
    j = lax.broadcasted_iota(jnp.int32, shape, 1) - C
    rel = j - BLK - r
    kpos = n * BLK - BLK + j
    return (j < 0) | ((jnp.abs(rel) <= BLK) & (kpos >= 0) & (kpos < L))


def _sink_col(sink_ref, hk):
    grp = lax.broadcasted_iota(jnp.int32, (GQA_GROUP * BLK, 1), 0) >> 7
    col = jnp.full((GQA_GROUP * BLK, 1), sink_ref[hk * GQA_GROUP], F32)
    for g in range(1, GQA_GROUP):
        col = jnp.where(grp == g, sink_ref[hk * GQA_GROUP + g], col)
    return col


def _kv_specs(nb):
    prev = pl.BlockSpec((BLK, KV_W), lambda n: (jnp.clip(n - 1, 0, nb - 1), 0))
    cur = pl.BlockSpec((BLK, KV_W), lambda n: (jnp.minimum(n, nb - 1), 0))
    nxt = pl.BlockSpec((BLK, KV_W), lambda n: (jnp.minimum(n + 1, nb - 1), 0))
    return [prev, cur, nxt]


def _k_attn(sink, q, k, v, kc, vc):
    L = q.shape[0]
    C = kc.shape[0]
    nb = L // BLK

    def body(sink_ref, q_ref, kp_ref, kn_ref, kx_ref, vp_ref, vn_ref, vx_ref, kc_ref, vc_ref, ya_ref, lse_ref):
        n = pl.program_id(0)
        valid = _attn_mask(n, L, C)
        for hk in range(N_KV_HEADS):
            sl = slice(hk * HEAD_DIM, (hk + 1) * HEAD_DIM)
            kcat = jnp.concatenate([kc_ref[:, sl], kp_ref[:, sl], kn_ref[:, sl], kx_ref[:, sl]], axis=0)
            vcat = jnp.concatenate([vc_ref[:, sl], vp_ref[:, sl], vn_ref[:, sl], vx_ref[:, sl]], axis=0)
            qg = jnp.concatenate(
                [q_ref[:, (hk * GQA_GROUP + g) * HEAD_DIM:(hk * GQA_GROUP + g + 1) * HEAD_DIM] for g in range(GQA_GROUP)],
                axis=0)
            s = lax.dot_general(qg, kcat, NT, preferred_element_type=F32)
            s = jnp.where(valid, s, NEG_INF)
            sink_c = _sink_col(sink_ref, hk)
            m = jnp.maximum(jnp.max(s, axis=1, keepdims=True), sink_c)
            p = jnp.exp(s - m)
            den = jnp.sum(p, axis=1, keepdims=True) + jnp.exp(sink_c - m)
            o = jnp.dot((p / den).astype(BF16), vcat, preferred_element_type=F32)
            lse = m + jnp.log(den)
            for g in range(GQA_GROUP):
                h = hk * GQA_GROUP + g
                ya_ref[:, h * HEAD_DIM:(h + 1) * HEAD_DIM] = o[g * BLK:(g + 1) * BLK, :].astype(BF16)
                lse_ref[:, h:h + 1] = lse[g * BLK:(g + 1) * BLK, :]

    kv3 = _kv_specs(nb)
    return pl.pallas_call(
        body, name="fwd_attn", grid=(nb,),
        in_specs=[pl.BlockSpec(memory_space=pltpu.SMEM), _row(BLK, Q_W)] + kv3 + kv3
                 + [_full((C, KV_W)), _full((C, KV_W))],
        out_specs=[_row(BLK, Q_W), _row(BLK, N_Q_HEADS)],
        out_shape=[_sds((L, Q_W), BF16), _sds((L, N_Q_HEADS), F32)],
        compiler_params=_params(("arbitrary",)),
    )(sink, q, k, k, k, v, v, v, kc, vc)


def _gmlp_fwd_vals(u_ref, vb_ref, lnv_ref, ws_ref, bst_ref):
    uf = u_ref[...].astype(F32)
    vf = vb_ref[...].astype(F32)
    gu, tu = _gelu(uf)
    gv, tv = _gelu(vf)
    vhat, rstd = _ln(gv)
    vn = (vhat * lnv_ref[0:1, :] + lnv_ref[1:2, :]).astype(BF16)
    s_parts = []
    for g in range(N_GROUPS):
        sg = jnp.dot(ws_ref[g], vn[:, g * GROUP_DIM:(g + 1) * GROUP_DIM], preferred_element_type=F32)
        s_parts.append(sg + bst_ref[:, g:g + 1])
    s = jnp.concatenate(s_parts, axis=1)
    return uf, vf, gu, tu, tv, vhat, rstd, vn, s


def _k_gmlp(u, vb, lnv, ws, bst):
    L = u.shape[0]

    def body(u_ref, vb_ref, lnv_ref, ws_ref, bst_ref, yb_ref):
        _, _, gu, _, _, _, _, _, s = _gmlp_fwd_vals(u_ref, vb_ref, lnv_ref, ws_ref, bst_ref)
        yb_ref[...] = (gu * s).astype(BF16)

    return pl.pallas_call(
        body, name="fwd_gmlp", grid=(L // BLK,),
        in_specs=[_row(BLK, GM_W), _row(BLK, GM_W), _full((8, GM_W)), _full((N_GROUPS, BLK, BLK)), _full((BLK, N_GROUPS))],
        out_specs=_row(BLK, GM_W),
        out_shape=_sds((L, GM_W), BF16),
        compiler_params=_params(("arbitrary",)),
    )(u, vb, lnv, ws, bst)


def _k_merge(x, ya, yb, ga, gb, w_a, w_b, w_o, modv, lnv, tm):
    L = x.shape[0]

    def body(x_ref, ya_ref, yb_ref, ga_ref, gb_ref, wa_ref, wb_ref, wo_ref, mod_ref, ln_ref,
             mg_ref, mix_ref, xm_ref, h2_ref):
        a = jnp.dot(ya_ref[...], wa_ref[...], preferred_element_type=F32)
        b = jnp.dot(yb_ref[...], wb_ref[...], preferred_element_type=F32)
        merged = (_sigmoid(ga_ref[...].astype(F32)) * a + _sigmoid(gb_ref[...].astype(F32)) * b).astype(BF16)
        mg_ref[...] = merged
        mix = jnp.dot(merged, wo_ref[...], preferred_element_type=F32)
        mix_ref[...] = mix.astype(BF16)
        r1 = ALPHA * x_ref[...] + mod_ref[2:3, :] * mix
        r1hat, _ = _ln(r1)
        xm = r1hat * ln_ref[0:1, :] + ln_ref[1:2, :]
        xm_ref[...] = xm
        n2, _ = _ln(xm)
        h2_ref[...] = (n2 * (1.0 + mod_ref[4:5, :]) + mod_ref[3:4, :]).astype(BF16)

    return pl.pallas_call(
        body, name="fwd_merge", grid=(L // tm,),
        in_specs=[_row(tm, D), _row(tm, Q_W), _row(tm, GM_W), _row(tm, D), _row(tm, D),
                  _resident((Q_W, D)), _resident((GM_W, D)), _resident((D, D)), _full((8, D)), _full((8, D))],
        out_specs=[_row(tm, D)] * 4,
        out_shape=[_sds((L, D), BF16), _sds((L, D), BF16), _sds((L, D), F32), _sds((L, D), BF16)],
        compiler_params=_params(("arbitrary",)),
    )(x, ya, yb, ga, gb, w_a, w_b, w_o, modv, lnv)


FFN_CH = 1408


def _k_ffn(h2, xm, tgt, w_fi, w_fo, modv, lnv, tm):
    L = h2.shape[0]

    def body(h2_ref, xm_ref, t_ref, wi_ref, wo_ref, mod_ref, ln_ref, gate_ref, up_ref, a_ref, dr2_ref, df_ref, acc_ref):
        @pl.when(pl.program_id(0) == 0)
        def _():
            acc_ref[...] = jnp.zeros_like(acc_ref)

        h2v = h2_ref[...]
        f = jnp.zeros((tm, D), F32)
        for j in range(FFN_H // FFN_CH):
            lo = j * FFN_CH
            gate = jnp.dot(h2v, wi_ref[:, lo:lo + FFN_CH], preferred_element_type=F32)
            up = jnp.dot(h2v, wi_ref[:, FFN_H + lo:FFN_H + lo + FFN_CH], preferred_element_type=F32)
            act = (gate * _sigmoid(gate) * up).astype(BF16)
            gate_ref[:, lo:lo + FFN_CH] = gate.astype(BF16)
            up_ref[:, lo:lo + FFN_CH] = up.astype(BF16)
            a_ref[:, lo:lo + FFN_CH] = act
            f = f + jnp.dot(act, wo_ref[lo:lo + FFN_CH, :], preferred_element_type=F32)
        gate2 = mod_ref[5:6, :]
        r2 = ALPHA * xm_ref[...] + gate2 * f
        r2hat, rstd = _ln(r2)
        y = r2hat * ln_ref[2:3, :] + ln_ref[3:4, :]
        err = y - t_ref[...]
        dy = err * (1.0 / D)
        dr2 = _ln_bwd(dy * ln_ref[2:3, :], r2hat, rstd)
        dr2_ref[...] = dr2
        df_ref[...] = (gate2 * dr2).astype(BF16)
        acc_ref[0:1, :] += _colsum(dy * r2hat)
        acc_ref[1:2, :] += _colsum(dy)
        acc_ref[2:3, :] += _colsum(dr2 * f)
        acc_ref[3:4, :] += _colsum(err * err) * (0.5 / D)

    return pl.pallas_call(
        body, name="fwd_ffn", grid=(L // tm,),
        in_specs=[_row(tm, D), _row(tm, D), _row(tm, D), _resident((D, 2 * FFN_H)), _resident((FFN_H, D)),
                  _full((8, D)), _full((8, D))],
        out_specs=[_row(tm, FFN_H)] * 3 + [_row(tm, D), _row(tm, D), _full((8, D))],
        out_shape=[_sds((L, FFN_H), BF16)] * 3 + [_sds((L, D), F32), _sds((L, D), BF16), _sds((8, D), F32)],
        compiler_params=_params(("arbitrary",)),
    )(h2, xm, tgt, w_fi, w_fo, modv, lnv)


def _k_ffn_bwd(df, gate, up, xm, dr2, x, mix, w_fi, w_fo, modv, lnv, tm):
    L = df.shape[0]

    def body(df_ref, gate_ref, up_ref, xm_ref, dr2_ref, x_ref, mix_ref, wi_ref, wo_ref, mod_ref, ln_ref,
             dF_ref, dmix_ref, dxp_ref, acc_ref):
        @pl.when(pl.program_id(0) == 0)
        def _():
            acc_ref[...] = jnp.zeros_like(acc_ref)

        dfv = df_ref[...]
        dh2 = jnp.zeros((tm, D), F32)
        for j in range(FFN_H // FFN_CH):
            lo = j * FFN_CH
            da = lax.dot_general(dfv, wo_ref[lo:lo + FFN_CH, :], NT, preferred_element_type=F32)
            gate = gate_ref[:, lo:lo + FFN_CH].astype(F32)
            upv = up_ref[:, lo:lo + FFN_CH].astype(F32)
            sg = _sigmoid(gate)
            d_gate = (da * upv * (sg * (1.0 + gate * (1.0 - sg)))).astype(BF16)
            d_up = (da * (gate * sg)).astype(BF16)
            dF_ref[:, lo:lo + FFN_CH] = d_gate
            dF_ref[:, FFN_H + lo:FFN_H + lo + FFN_CH] = d_up
            dh2 = dh2 + lax.dot_general(d_gate, wi_ref[:, lo:lo + FFN_CH], NT, preferred_element_type=F32)
            dh2 = dh2 + lax.dot_general(d_up, wi_ref[:, FFN_H + lo:FFN_H + lo + FFN_CH], NT, preferred_element_type=F32)
        n2, rstd2 = _ln(xm_ref[...])
        acc_ref[0:1, :] += _colsum(dh2)
        acc_ref[1:2, :] += _colsum(dh2 * n2)
        dxm = ALPHA * dr2_ref[...] + _ln_bwd(dh2 * (1.0 + mod_ref[4:5, :]), n2, rstd2)
        mixf = mix_ref[...].astype(F32)
        gate1 = mod_ref[2:3, :]
        r1hat, rstd1 = _ln(ALPHA * x_ref[...] + gate1 * mixf)
        acc_ref[2:3, :] += _colsum(dxm * r1hat)
        acc_ref[3:4, :] += _colsum(dxm)
        dr1 = _ln_bwd(dxm * ln_ref[0:1, :], r1hat, rstd1)
        dmix_ref[...] = (gate1 * dr1).astype(BF16)
        dxp_ref[...] = ALPHA * dr1
        acc_ref[4:5, :] += _colsum(dr1 * mixf)

    return pl.pallas_call(
        body, name="bwd_ffn", grid=(L // tm,),
        in_specs=[_row(tm, D), _row(tm, FFN_H), _row(tm, FFN_H), _row(tm, D), _row(tm, D), _row(tm, D), _row(tm, D),
                  _resident((D, 2 * FFN_H)), _resident((FFN_H, D)), _full((8, D)), _full((8, D))],
        out_specs=[_row(tm, 2 * FFN_H), _row(tm, D), _row(tm, D), _full((8, D))],
        out_shape=[_sds((L, 2 * FFN_H), BF16), _sds((L, D), BF16), _sds((L, D), F32), _sds((8, D), F32)],
        compiler_params=_params(("arbitrary",)),
    )(df, gate, up, xm, dr2, x, mix, w_fi, w_fo, modv, lnv)


def _k_merge_bwd(dmix, ya, yb, ga, gb, w_a, w_b, w_o, tm):
    L = dmix.shape[0]

    def body(dmix_ref, ya_ref, yb_ref, ga_ref, gb_ref, wa_ref, wb_ref, wo_ref,
             dA_ref, dB_ref, dga_ref, dgb_ref, dya_ref, dyb_ref):
        dmg = lax.dot_general(dmix_ref[...], wo_ref[...], NT, preferred_element_type=F32)
        a = jnp.dot(ya_ref[...], wa_ref[...], preferred_element_type=F32)
        sa = _sigmoid(ga_ref[...].astype(F32))
        dA = (dmg * sa).astype(BF16)
        dA_ref[...] = dA
        dga_ref[...] = (dmg * a * (sa * (1.0 - sa))).astype(BF16)
        dya_ref[...] = lax.dot_general(dA, wa_ref[...], NT, preferred_element_type=F32).astype(BF16)
        b = jnp.dot(yb_ref[...], wb_ref[...], preferred_element_type=F32)
        sb = _sigmoid(gb_ref[...].astype(F32))
        dB = (dmg * sb).astype(BF16)
        dB_ref[...] = dB
        dgb_ref[...] = (dmg * b * (sb * (1.0 - sb))).astype(BF16)
        dyb_ref[...] = lax.dot_general(dB, wb_ref[...], NT, preferred_element_type=F32).astype(BF16)

    return pl.pallas_call(
        body, name="bwd_merge", grid=(L // tm,),
        in_specs=[_row(tm, D), _row(tm, Q_W), _row(tm, GM_W), _row(tm, D), _row(tm, D),
                  _resident((Q_W, D)), _resident((GM_W, D)), _resident((D, D))],
        out_specs=[_row(tm, D)] * 4 + [_row(tm, Q_W), _row(tm, GM_W)],
        out_shape=[_sds((L, D), BF16)] * 4 + [_sds((L, Q_W), BF16), _sds((L, GM_W), BF16)],
        compiler_params=_params(("arbitrary",)),
    )(dmix, ya, yb, ga, gb, w_a, w_b, w_o)


def _k_gmlp_bwd(u, vb, dyb, lnv, ws, wst, bst):
    L = u.shape[0]

    def body(u_ref, vb_ref, dyb_ref, lnv_ref, ws_ref, wst_ref, bst_ref, du_ref, dvb_ref, gws_ref, gbst_ref, gln_ref):
        @pl.when(pl.program_id(0) == 0)
        def _():
            gws_ref[...] = jnp.zeros_like(gws_ref)
            gbst_ref[...] = jnp.zeros_like(gbst_ref)
            gln_ref[...] = jnp.zeros_like(gln_ref)

        uf, vf, gu, tu, tv, vhat, rstd, vn, s = _gmlp_fwd_vals(u_ref, vb_ref, lnv_ref, ws_ref, bst_ref)
        dyb_f = dyb_ref[...].astype(F32)
        du_ref[...] = (dyb_f * s * _gelu_grad(uf, tu)).astype(BF16)
        ds = dyb_f * gu
        ds_b = ds.astype(BF16)
        dvn_parts = []
        for g in range(N_GROUPS):
            sl = slice(g * GROUP_DIM, (g + 1) * GROUP_DIM)
            gws_ref[g] += lax.dot_general(ds_b[:, sl], vn[:, sl], NT, preferred_element_type=F32)
            gbst_ref[:, g:g + 1] += jnp.sum(ds[:, sl], axis=1, keepdims=True)
            dvn_parts.append(jnp.dot(wst_ref[g], ds_b[:, sl], preferred_element_type=F32))
        dvn = jnp.concatenate(dvn_parts, axis=1)
        gln_ref[0:1, :] += _colsum(dvn * vhat)
        gln_ref[1:2, :] += _colsum(dvn)
        dgv = _ln_bwd(dvn * lnv_ref[0:1, :], vhat, rstd)
        dvb_ref[...] = (dgv * _gelu_grad(vf, tv)).astype(BF16)

    return pl.pallas_call(
        body, name="bwd_gmlp", grid=(L // BLK,),
        in_specs=[_row(BLK, GM_W)] * 3 + [_full((8, GM_W)), _full((N_GROUPS, BLK, BLK)), _full((N_GROUPS, BLK, BLK)),
                                          _full((BLK, N_GROUPS))],
        out_specs=[_row(BLK, GM_W), _row(BLK, GM_W), _full((N_GROUPS, BLK, BLK)), _full((BLK, N_GROUPS)), _full((8, GM_W))],
        out_shape=[_sds((L, GM_W), BF16), _sds((L, GM_W), BF16), _sds((N_GROUPS, BLK, BLK), F32),
                   _sds((BLK, N_GROUPS), F32), _sds((8, GM_W), F32)],
        compiler_params=_params(("arbitrary",)),
    )(u, vb, dyb, lnv, ws, wst, bst)


def _k_attn_bwd(sink, q, k, v, kc, vc, dya, lse, cos, sin):
    L = q.shape[0]
    C = kc.shape[0]
    nb = L // BLK
    NK = C + 3 * BLK

    def body(sink_ref, q_ref, kp_ref, kn_ref, kx_ref, vp_ref, vn_ref, vx_ref, kc_ref, vc_ref, do_ref, lse_ref,
             cq_ref, sq_ref, ck_ref, sk_ref,
             dq_ref, dk_ref, dv_ref, dkc_ref, dvc_ref, dsink_ref,
             dq_scr, ck_scr, cv_scr, kp_acc, kc_acc, vp_acc, vc_acc):
        n = pl.program_id(0)

        @pl.when(n == 0)
        def _():
            for r in (kp_acc, kc_acc, vp_acc, vc_acc, dkc_ref, dvc_ref, dsink_ref):
                r[...] = jnp.zeros_like(r)

        @pl.when(n < nb)
        def _():
            valid = _attn_mask(n, L, C)
            for hk in range(N_KV_HEADS):
                sl = slice(hk * HEAD_DIM, (hk + 1) * HEAD_DIM)
                kcat = jnp.concatenate([kc_ref[:, sl], kp_ref[:, sl], kn_ref[:, sl], kx_ref[:, sl]], axis=0)
                vcat = jnp.concatenate([vc_ref[:, sl], vp_ref[:, sl], vn_ref[:, sl], vx_ref[:, sl]], axis=0)
                heads = [hk * GQA_GROUP + g for g in range(GQA_GROUP)]
                qg = jnp.concatenate([q_ref[:, h * HEAD_DIM:(h + 1) * HEAD_DIM] for h in heads], axis=0)
                dog = jnp.concatenate([do_ref[:, h * HEAD_DIM:(h + 1) * HEAD_DIM] for h in heads], axis=0)
                lse_c = jnp.concatenate([lse_ref[:, h:h + 1] for h in heads], axis=0)
                s = lax.dot_general(qg, kcat, NT, preferred_element_type=F32)
                p = jnp.exp(jnp.where(valid, s, NEG_INF) - lse_c)
                dp = lax.dot_general(dog, vcat, NT, preferred_element_type=F32)
                delta = jnp.sum(p * dp, axis=1, keepdims=True)
                ds = (p * (dp - delta)).astype(BF16)
                dqs = jnp.dot(ds, kcat, preferred_element_type=F32)
                ck_scr[:, sl] = lax.dot_general(ds, qg, TN, preferred_element_type=F32)
                cv_scr[:, sl] = lax.dot_general(p.astype(BF16), dog, TN, preferred_element_type=F32)
                p_sink = jnp.exp(_sink_col(sink_ref, hk) - lse_c) * delta
                for g, h in enumerate(heads):
                    dq_scr[:, h * HEAD_DIM:(h + 1) * HEAD_DIM] = dqs[g * BLK:(g + 1) * BLK, :]
                    tot = jnp.sum(p_sink[g * BLK:(g + 1) * BLK, :], axis=0, keepdims=True)
                    dsink_ref[h:h + 1, :] -= jnp.broadcast_to(tot, (1, 128))
            cq, sq = cq_ref[...], sq_ref[...]
            for i in range(4):
                dq_ref[:, i * 128:(i + 1) * 128] = _unrope(dq_scr[:, i * 128:(i + 1) * 128] * Q_SCALE, cq, sq).astype(BF16)
            dkc_ref[...] += ck_scr[0:C, :]
            dvc_ref[...] += cv_scr[0:C, :]

        @pl.when(n >= nb)
        def _():
            ck_scr[...] = jnp.zeros_like(ck_scr)
            cv_scr[...] = jnp.zeros_like(cv_scr)

        dk_ref[...] = _unrope(kp_acc[...] + ck_scr[C:C + BLK, :], ck_ref[...], sk_ref[...]).astype(BF16)
        dv_ref[...] = (vp_acc[...] + cv_scr[C:C + BLK, :]).astype(BF16)
        kp_acc[...] = kc_acc[...] + ck_scr[C + BLK:C + 2 * BLK, :]
        vp_acc[...] = vc_acc[...] + cv_scr[C + BLK:C + 2 * BLK, :]
        kc_acc[...] = ck_scr[C + 2 * BLK:C + 3 * BLK, :]
        vc_acc[...] = cv_scr[C + 2 * BLK:C + 3 * BLK, :]

    kv3 = _kv_specs(nb)
    cur = lambda w: pl.BlockSpec((BLK, w), lambda n: (jnp.minimum(n, nb - 1), 0))
    late = lambda w: pl.BlockSpec((BLK, w), lambda n: (jnp.maximum(n - 1, 0), 0))
    return pl.pallas_call(
        body, name="bwd_attn", grid=(nb + 1,),
        in_specs=[pl.BlockSpec(memory_space=pltpu.SMEM), cur(Q_W)] + kv3 + kv3
                 + [_full((C, KV_W)), _full((C, KV_W)), cur(Q_W), cur(N_Q_HEADS), cur(128), cur(128), late(128), late(128)],
        out_specs=[cur(Q_W), late(KV_W), late(KV_W), _full((C, KV_W)), _full((C, KV_W)), _full((8, 128))],
        out_shape=[_sds((L, Q_W), BF16), _sds((L, KV_W), BF16), _sds((L, KV_W), BF16),
                   _sds((C, KV_W), F32), _sds((C, KV_W), F32), _sds((8, 128), F32)],
        scratch_shapes=[pltpu.VMEM((BLK, Q_W), F32), pltpu.VMEM((NK, KV_W), F32), pltpu.VMEM((NK, KV_W), F32)]
                       + [pltpu.VMEM((BLK, KV_W), F32)] * 4,
        compiler_params=_params(("arbitrary",)),
    )(sink, q, k, k, k, v, v, v, kc, vc, dya, lse, cos, sin, cos, sin)


def _k_ctx_bwd(ctx, modc, hc, dkc, dvc, w_kv, gw_in):
    C = ctx.shape[0]
    kv_block = pl.BlockSpec((D, 2 * KV_W), lambda i: (0, O_K // (2 * KV_W)))

    def body(c_ref, mod_ref, hc_ref, dkc_ref, dvc_ref, w_ref, gin_ref, gw_ref, dmod_ref):
        dkv = jnp.concatenate([dkc_ref[...], dvc_ref[...]], axis=1).astype(BF16)
        gw_ref[...] = gin_ref[...] + lax.dot_general(hc_ref[...], dkv, TN, preferred_element_type=F32)
        dhc = lax.dot_general(dkv, w_ref[...], NT, preferred_element_type=F32)
        n, _ = _ln(c_ref[...])
        dmod_ref[...] = jnp.zeros_like(dmod_ref)
        dmod_ref[0:1, :] = _colsum(dhc)
        dmod_ref[1:2, :] = _colsum(dhc * n)

    return pl.pallas_call(
        body, name="bwd_ctx", grid=(1,),
        in_specs=[_full((C, D)), _full((8, D)), _full((C, D)), _full((C, KV_W)), _full((C, KV_W)), _full((D, 2 * KV_W)),
                  kv_block],
        out_specs=[kv_block, _full((8, D))],
        out_shape=[_sds((D, IN_W), F32), _sds((8, D), F32)],
        input_output_aliases={6: 0},
        compiler_params=_params(("arbitrary",)),
    )(ctx, modc, hc, dkc, dvc, w_kv, gw_in)


def _k_in_bwd(dq, dk, dv, du, dvb, dga, dgb, x, dxp, w_in, modv, tm):
    L = x.shape[0]
    parts = [(O_Q, Q_W), (O_K, KV_W), (O_V, KV_W), (O_U, GM_W), (O_VB, GM_W), (O_GA, D), (O_GB, D)]

    def body(dq_ref, dk_ref, dv_ref, du_ref, dvb_ref, dga_ref, dgb_ref, x_ref, dxp_ref, w_ref, mod_ref,
             dP_ref, gx_ref, acc_ref):
        @pl.when(pl.program_id(0) == 0)
        def _():
            acc_ref[...] = jnp.zeros_like(acc_ref)

        for (lo, width), r in zip(parts, (dq_ref, dk_ref, dv_ref, du_ref, dvb_ref, dga_ref, dgb_ref)):
            dP_ref[:, lo:lo + width] = r[...]
        dh = lax.dot_general(dP_ref[...], w_ref[...], NT, preferred_element_type=F32)
        n1, rstd1 = _ln(x_ref[...])
        acc_ref[0:1, :] += _colsum(dh)
        acc_ref[1:2, :] += _colsum(dh * n1)
        gx_ref[...] = dxp_ref[...] + _ln_bwd(dh * (1.0 + mod_ref[1:2, :]), n1, rstd1)

    return pl.pallas_call(
        body, name="bwd_in", grid=(L // tm,),
        in_specs=[_row(tm, w) for _, w in parts] + [_row(tm, D), _row(tm, D), _resident((D, IN_W)), _full((8, D))],
        out_specs=[_row(tm, IN_W), _row(tm, D), _full((8, D))],
        out_shape=[_sds((L, IN_W), BF16), _sds((L, D), F32), _sds((8, D), F32)],
        compiler_params=_params(("arbitrary",)),
    )(dq, dk, dv, du, dvb, dga, dgb, x, dxp, w_in, modv)


def _wgrad(a, b, name, tn, tt):
    T, K = a.shape
    N = b.shape[1]
    nt = T // tt

    def body(a_ref, b_ref, o_ref):
        @pl.when(pl.program_id(1) == 0)
        def _():
            o_ref[...] = jnp.zeros_like(o_ref)

        o_ref[...] += lax.dot_general(a_ref[...], b_ref[...], TN, preferred_element_type=F32)

    return pl.pallas_call(
        body, name=name, grid=(N // tn, nt),
        in_specs=[pl.BlockSpec((tt, K), lambda j, t: (t, 0)), pl.BlockSpec((tt, tn), lambda j, t: (t, j))],
        out_specs=pl.BlockSpec((K, tn), lambda j, t: (0, j)),
        out_shape=_sds((K, N), F32),
        compiler_params=_params(("arbitrary", "arbitrary")),
    )(a, b)


def _adamw_reduce(parts, w, m, v, name, tr):
    R, C = w.shape

    def body(p_ref, w_ref, m_ref, v_ref, g_ref, d_ref, m2_ref, v2_ref):
        g = p_ref[0].astype(F32)
        for i in range(1, N_DEV):
            g = g + p_ref[i].astype(F32)
        delta, m2, v2 = _adamw(w_ref[...], g, m_ref[...], v_ref[...])
        g_ref[...] = g
        d_ref[...] = delta
        m2_ref[...] = m2
        v2_ref[...] = v2

    spec = _row(tr, C)
    return pl.pallas_call(
        body, name=name, grid=(R // tr,),
        in_specs=[pl.BlockSpec((N_DEV, tr, C), lambda i: (0, i, 0)), spec, spec, spec],
        out_specs=[spec] * 4,
        out_shape=[_sds((R, C), F32)] * 4,
        compiler_params=_params(("arbitrary",)),
    )(parts, w, m, v)


def _small_reduce(gath):
    def body(g_ref, out_ref):
        tot = g_ref[0]
        for i in range(1, N_DEV):
            tot = tot + g_ref[i]
        out_ref[...] = tot
        out_ref[0:2, :] = tot[0:2, :] + tot[6:8, :]
        out_ref[15:16, :] = jnp.broadcast_to(jnp.sum(tot[15:16, :], axis=1, keepdims=True), (1, D))

    return pl.pallas_call(
        body, name="small_reduce", grid=(1,),
        in_specs=[_full((N_DEV, SMALL_ROWS, D))],
        out_specs=_full((SMALL_ROWS, D)),
        out_shape=_sds((SMALL_ROWS, D), F32),
        compiler_params=_params(("arbitrary",)),
    )(gath)


def _small_adamw(w, g, m, v, name):
    shape = w.shape

    def body(w_ref, g_ref, m_ref, v_ref, d_ref, m2_ref, v2_ref):
        delta, m2, v2 = _adamw(w_ref[...], g_ref[...], m_ref[...], v_ref[...])
        d_ref[...] = delta
        m2_ref[...] = m2
        v2_ref[...] = v2

    return pl.pallas_call(
        body, name=name, grid=(1,),
        in_specs=[_full(shape)] * 4, out_specs=[_full(shape)] * 3,
        out_shape=[_sds(shape, F32)] * 3,
        compiler_params=_params(("arbitrary",)),
    )(w, g, m, v)


def _cctx_finish(gath, c_ctx, m, v):
    def body(g_ref, c_ref, m_ref, v_ref, gr_ref, d_ref, m2_ref, v2_ref):
        ds = g_ref[0]
        for i in range(1, N_DEV):
            ds = ds + g_ref[i]
        c = c_ref[...]
        sg = _sigmoid(c)
        g = ds * (sg * (1.0 + c * (1.0 - sg)))
        delta, m2, v2 = _adamw(c, g, m_ref[...], v_ref[...])
        gr_ref[...] = g
        d_ref[...] = delta
        m2_ref[...] = m2
        v2_ref[...] = v2

    return pl.pallas_call(
        body, name="cctx_finish", grid=(1,),
        in_specs=[_full((N_DEV, 8, D))] + [_full((8, D))] * 3, out_specs=[_full((8, D))] * 4,
        out_shape=[_sds((8, D), F32)] * 4,
        compiler_params=_params(("arbitrary",)),
    )(gath, c_ctx, m, v)


def _pad_rows(a, rows):
    return jnp.concatenate([a, jnp.zeros((rows - a.shape[0], a.shape[1]), a.dtype)], axis=0)


def _pack_small(b_ada, ln1_g, ln1_b, ln2_g, ln2_b, gm_g, gm_b, b_sp, sink, w_sp):
    rows = [b_ada.reshape(6, D), jnp.zeros((2, D), F32), ln1_g.reshape(1, D), ln1_b.reshape(1, D), ln2_g.reshape(1, D),
            ln2_b.reshape(1, D), jnp.concatenate([gm_g.reshape(1, GM_W), gm_b.reshape(1, GM_W)], axis=1),
            b_sp.reshape(1, D), _pad_rows(sink.reshape(1, N_Q_HEADS).T, D).T.reshape(1, D), jnp.zeros((1, D), F32),
            w_sp.reshape(N_GROUPS * BLK * BLK // D, D)]
    return jnp.concatenate(rows, axis=0)


def _unpack_small(p):
    return dict(b_ada=p[0:6].reshape(1, 6 * D), ln1_g=p[8:9], ln1_b=p[9:10], ln2_g=p[10:11], ln2_b=p[11:12],
                gmlp_ln_g=p[12:13, :GM_W], gmlp_ln_b=p[12:13, GM_W:], b_spatial=p[13:14].reshape(1, N_GROUPS, BLK),
                attn_sink=p[14:15, :N_Q_HEADS], w_spatial=p[16:].reshape(1, N_GROUPS, BLK, BLK))


def kernel(x, c, ctx, c_ctx, w_ada, b_ada, w_in, attn_sink, gmlp_ln_g, gmlp_ln_b, w_spatial, b_spatial, w_branch_a, w_branch_b, w_out, ln1_g, ln1_b, w_ffn_in, w_ffn_out, ln2_g, ln2_b, loss_target, m_c_ctx, m_w_ada, m_b_ada, m_w_in, m_attn_sink, m_gmlp_ln_g, m_gmlp_ln_b, m_w_spatial, m_b_spatial, m_w_branch_a, m_w_branch_b, m_w_out, m_ln1_g, m_ln1_b, m_w_ffn_in, m_w_ffn_out, m_ln2_g, m_ln2_b, v_c_ctx, v_w_ada, v_b_ada, v_w_in, v_attn_sink, v_gmlp_ln_g, v_gmlp_ln_b, v_w_spatial, v_b_spatial, v_w_branch_a, v_w_branch_b, v_w_out, v_ln1_g, v_ln1_b, v_w_ffn_in, v_w_ffn_out, v_ln2_g, v_ln2_b):
    L = x.shape[1]
    me = 4 * lax.axis_index("x") + 2 * lax.axis_index("y") + lax.axis_index("c")
    x2, tgt, ctx2 = x[0], loss_target[0], ctx[0]
    tm_in = min(512, L)
    tm = min(256, L)

    big = dict(w_in=w_in[0], w_branch_a=w_branch_a[0], w_branch_b=w_branch_b[0], w_out=w_out[0],
               w_ffn_in=w_ffn_in[0], w_ffn_out=w_ffn_out[0])
    col_sharded = ("w_in", "w_branch_a", "w_branch_b", "w_ffn_in")
    names = list(big)
    gathered = _ag_big([big[k].astype(BF16) for k in names], "gather_weights")
    full = {}
    for k, g in zip(names, gathered):
        if k in col_sharded:
            full[k] = g.transpose(1, 0, 2).reshape(g.shape[1], N_DEV * g.shape[2])
        else:
            full[k] = g.reshape(N_DEV * g.shape[1], g.shape[2])

    c_all = _ag_small(_pad_rows(c, 8), "gather_c")[:, 0, :]
    s_in = jnp.concatenate([c_all, c_ctx[None, :], jnp.zeros((7, D), F32)], axis=0)
    n_ada = w_ada.shape[2]
    b_my = lax.dynamic_slice(b_ada, (0, me * n_ada), (1, n_ada))
    act, mod_my = _ada_fwd(s_in, w_ada[0], b_my)
    mod_all = _ag_small(mod_my, "gather_mod").transpose(1, 0, 2).reshape(16, 6 * D)
    modv = _pad_rows(lax.dynamic_slice(mod_all, (me, 0), (1, 6 * D)).reshape(6, D), 8)
    modc = _pad_rows(mod_all[8].reshape(6, D), 8)

    lnv = _pad_rows(jnp.concatenate([ln1_g, ln1_b, ln2_g, ln2_b], axis=0), 8)
    gm_lnv = _pad_rows(jnp.concatenate([gmlp_ln_g, gmlp_ln_b], axis=0), 8)
    ws_b = w_spatial[0].astype(BF16)
    wst_b = ws_b.transpose(0, 2, 1)
    bst = b_spatial[0].T
    sink = attn_sink[0]
    cos, sin = _rope_tables(L)
    w_kv = full["w_in"][:, O_K:O_K + 2 * KV_W]

    h, q, k, v, u, vb, ga, gb = _k_in(x2, modv, full["w_in"], cos, sin, tm_in)
    hc, kc, vc = _k_ctx(ctx2, modc, w_kv)
    ya, lse = _k_attn(sink, q, k, v, kc, vc)
    yb = _k_gmlp(u, vb, gm_lnv, ws_b, bst)
    merged, mix, xm, h2 = _k_merge(x2, ya, yb, ga, gb, full["w_branch_a"], full["w_branch_b"], full["w_out"], modv, lnv, tm)
    gate, up, act_f, dr2, df, acc_f = _k_ffn(h2, xm, tgt, full["w_ffn_in"], full["w_ffn_out"], modv, lnv, tm)

    dF, dmix, dxp, acc_b = _k_ffn_bwd(df, gate, up, xm, dr2, x2, mix, full["w_ffn_in"], full["w_ffn_out"], modv, lnv, tm)
    dA, dB, dga, dgb, dya, dyb = _k_merge_bwd(dmix, ya, yb, ga, gb, full["w_branch_a"], full["w_branch_b"], full["w_out"], tm)
    du, dvb, g_ws, g_bst, g_gln = _k_gmlp_bwd(u, vb, dyb, gm_lnv, ws_b, wst_b, bst)
    dq, dk, dv, dkc, dvc, g_sink = _k_attn_bwd(sink, q, k, v, kc, vc, dya, lse, cos, sin)
    dP, grad_x, acc_i = _k_in_bwd(dq, dk, dv, du, dvb, dga, dgb, x2, dxp, full["w_in"], modv, tm)

    tt = min(512, L)
    gw_in, dmodc = _k_ctx_bwd(ctx2, modc, hc, dkc, dvc, w_kv, _wgrad(h, dP, "wgrad_in", 1280, tt))
    gw = dict(
        w_in=gw_in,
        w_branch_a=_wgrad(ya, dA, "wgrad_a", D, tt),
        w_branch_b=_wgrad(yb, dB, "wgrad_b", D, tt),
        w_out=_wgrad(merged, dmix, "wgrad_out", D, tt),
        w_ffn_in=_wgrad(h2, dF, "wgrad_ffn_in", 1408, tt),
        w_ffn_out=_wgrad(act_f, df, "wgrad_ffn_out", 512, tt),
    )

    blocks = []
    for kname in names:
        g = gw[kname].astype(BF16)
        if kname in col_sharded:
            blocks.append(g.reshape(g.shape[0], N_DEV, g.shape[1] // N_DEV).transpose(1, 0, 2))
        else:
            blocks.append(g.reshape(N_DEV, g.shape[0] // N_DEV, g.shape[1]))
    received = _rs_big(blocks, "exchange_grads")
    moments = dict(w_in=(m_w_in, v_w_in), w_branch_a=(m_w_branch_a, v_w_branch_a), w_branch_b=(m_w_branch_b, v_w_branch_b),
                   w_out=(m_w_out, v_w_out), w_ffn_in=(m_w_ffn_in, v_w_ffn_in), w_ffn_out=(m_w_ffn_out, v_w_ffn_out))
    res = {}
    for kname, parts in zip(names, received):
        mm, vv = moments[kname]
        R = big[kname].shape[0]
        res[kname] = _adamw_reduce(parts, big[kname], mm[0], vv[0], "adamw_" + kname, 256 if R % 256 == 0 else R // 2)

    dmod_x = jnp.concatenate([acc_i[0:2], acc_b[4:5], acc_b[0:2], acc_f[2:3]], axis=0)
    small = jnp.concatenate([
        dmod_x, dmodc[0:2], acc_b[2:4], acc_f[0:2],
        jnp.concatenate([g_gln[0:1], g_gln[1:2]], axis=1), g_bst.T.reshape(1, D),
        _pad_rows(g_sink[:, 0:1], D).T, acc_f[3:4], g_ws.reshape(N_GROUPS * BLK * BLK // D, D)], axis=0)
    gath = _ag_small(small, "gather_small")
    tot = _small_reduce(gath)
    g_small = _unpack_small(tot)
    loss = tot[15, 0]

    p_w = _pack_small(b_ada, ln1_g, ln1_b, ln2_g, ln2_b, gmlp_ln_g, gmlp_ln_b, b_spatial, attn_sink, w_spatial)
    p_m = _pack_small(m_b_ada, m_ln1_g, m_ln1_b, m_ln2_g, m_ln2_b, m_gmlp_ln_g, m_gmlp_ln_b, m_b_spatial, m_attn_sink, m_w_spatial)
    p_v = _pack_small(v_b_ada, v_ln1_g, v_ln1_b, v_ln2_g, v_ln2_b, v_gmlp_ln_g, v_gmlp_ln_b, v_b_spatial, v_attn_sink, v_w_spatial)
    s_d, s_m, s_v = [_unpack_small(t) for t in _small_adamw(p_w, tot, p_m, p_v, "adamw_small")]

    dmod_rows = jnp.concatenate([gath[:, 0:6, :].reshape(N_DEV, 6 * D),
                                 jnp.concatenate([tot[6:8].reshape(1, 2 * D), jnp.zeros((1, 4 * D), F32)], axis=1),
                                 jnp.zeros((7, 6 * D), F32)], axis=0)
    dmod_my = lax.dynamic_slice(dmod_rows, (0, me * n_ada), (16, n_ada))
    g_wada, d_wada, m2_wada, v2_wada, pc = _ada_bwd(act, dmod_my, w_ada[0], m_w_ada[0], v_w_ada[0])
    pc_all = _ag_small(pc, "gather_cctx")
    cc8 = lambda a: _pad_rows(a.reshape(1, D), 8)
    g_cc, d_cc, m2_cc, v2_cc = _cctx_finish(pc_all, cc8(c_ctx), cc8(m_c_ctx), cc8(v_c_ctx))

    order = ["c_ctx", "w_ada", "b_ada", "w_in", "attn_sink", "gmlp_ln_g", "gmlp_ln_b", "w_spatial", "b_spatial",
             "w_branch_a", "w_branch_b", "w_out", "ln1_g", "ln1_b", "w_ffn_in", "w_ffn_out", "ln2_g", "ln2_b"]
    grads, deltas, new_m, new_v = {}, {}, {}, {}
    grads["c_ctx"], deltas["c_ctx"], new_m["c_ctx"], new_v["c_ctx"] = g_cc[0], d_cc[0], m2_cc[0], v2_cc[0]
    grads["w_ada"], deltas["w_ada"], new_m["w_ada"], new_v["w_ada"] = g_wada[None], d_wada[None], m2_wada[None], v2_wada[None]
    for kname in names:
        g, d, m2, v2 = res[kname]
        grads[kname], deltas[kname], new_m[kname], new_v[kname] = g[None], d[None], m2[None], v2[None]
    for kname in ("b_ada", "attn_sink", "gmlp_ln_g", "gmlp_ln_b", "w_spatial", "b_spatial", "ln1_g", "ln1_b", "ln2_g", "ln2_b"):
        grads[kname], deltas[kname], new_m[kname], new_v[kname] = g_small[kname], s_d[kname], s_m[kname], s_v[kname]
    return (loss, grad_x[None], *[grads[n] for n in order], *[deltas[n] for n in order],
            *[new_m[n] for n in order], *[new_v[n] for n in order])
```

```python
import functools
import math

import jax
import jax.numpy as jnp
from jax import lax
from jax.experimental import pallas as pl
from jax.experimental.pallas import tpu as pltpu

F32 = jnp.float32
BF16 = jnp.bfloat16
MESH = pl.DeviceIdType.MESH

N_DEV = 8
D = 1024
HEAD_DIM = 64
N_Q_HEADS = 8
N_KV_HEADS = 2
GQA_GROUP = 4
BLK = 128
Q_W = 512
KV_W = 128
GM_W = 512
N_GROUPS = 8
GROUP_DIM = 64
FFN_H = 2816
IN_W = 3840
O_Q, O_K, O_V, O_U, O_VB, O_GA, O_GB = 0, 512, 640, 768, 1280, 1792, 2816
LN_EPS = 1e-5
NEG_INF = -1e30
ALPHA = 2.0 ** 0.25
ROPE_BASE = 10000.0
ROPE_PAIRS = 16
Q_SCALE = HEAD_DIM ** -0.5
GELU_K0 = math.sqrt(2.0 / math.pi)
GELU_K1 = 0.044715

ADAM_LR = 0.001
ADAM_B1 = 0.9
ADAM_B2 = 0.999
ADAM_EPS = 1e-08
ADAM_WD = 0.01
ADAM_STEP = 10

VMEM_LIMIT = 56 * 1024 * 1024
SMALL_ROWS = 144
NT = (((1,), (1,)), ((), ()))
TN = (((0,), (0,)), ((), ()))


def _params(sem=None):
    return pltpu.CompilerParams(dimension_semantics=sem, vmem_limit_bytes=VMEM_LIMIT)


def _row(tm, w):
    return pl.BlockSpec((tm, w), lambda i: (i, 0))


def _full(shape):
    nd = len(shape)
    return pl.BlockSpec(shape, lambda i: (0,) * nd)


def _resident(shape):
    nd = len(shape)
    return pl.BlockSpec(shape, lambda i: (0,) * nd, pipeline_mode=pl.Buffered(1))


def _sds(shape, dt):
    return jax.ShapeDtypeStruct(shape, dt)


def _ln(xf):
    mu = jnp.mean(xf, axis=-1, keepdims=True)
    xc = xf - mu
    var = jnp.mean(xc * xc, axis=-1, keepdims=True)
    rstd = lax.rsqrt(var + LN_EPS)
    return xc * rstd, rstd


def _ln_bwd(dn, n, rstd):
    m1 = jnp.mean(dn, axis=-1, keepdims=True)
    m2 = jnp.mean(dn * n, axis=-1, keepdims=True)
    return rstd * (dn - m1 - n * m2)


def _colsum(t):
    return jnp.sum(t, axis=0, keepdims=True)


def _sigmoid(x):
    return 1.0 / (1.0 + jnp.exp(-x))


def _gelu(x):
    t = jnp.tanh(GELU_K0 * (x + GELU_K1 * (x * x * x)))
    return x * (0.5 * (1.0 + t)), t


def _gelu_grad(x, t):
    return 0.5 * (1.0 + t) + 0.5 * x * (1.0 - t * t) * (GELU_K0 * (1.0 + 3.0 * GELU_K1 * x * x))


def _swap16(t):
    lane = lax.broadcasted_iota(jnp.int32, t.shape, 1)
    return jnp.where((lane & 16) == 0, pltpu.roll(t, 112, 1), pltpu.roll(t, 16, 1))


def _rope(t, cos, sin):
    return t * cos + _swap16(t) * sin


def _unrope(t, cos, sin):
    return t * cos - _swap16(t) * sin


def _adamw(w, g, m, v):
    m2 = ADAM_B1 * m + (1.0 - ADAM_B1) * g
    v2 = ADAM_B2 * v + (1.0 - ADAM_B2) * (g * g)
    m_hat = m2 / (1.0 - ADAM_B1 ** ADAM_STEP)
    v_hat = v2 / (1.0 - ADAM_B2 ** ADAM_STEP)
    delta = -ADAM_LR * (m_hat / (jnp.sqrt(v_hat) + ADAM_EPS) + ADAM_WD * w)
    return delta, m2, v2


def _rope_tables(L):
    inv = ROPE_BASE ** (-jnp.arange(ROPE_PAIRS, dtype=F32) / ROPE_PAIRS)
    t = jnp.arange(L, dtype=jnp.int32)
    rows = (t // 64).astype(F32)[:, None] * inv
    cols = (t % 64).astype(F32)[:, None] * inv
    cr, sr, cc, sc = jnp.cos(rows), jnp.sin(rows), jnp.cos(cols), jnp.sin(cols)
    cos = jnp.concatenate([cr, cr, cc, cc], axis=1)
    sin = jnp.concatenate([-sr, sr, -sc, sc], axis=1)
    return jnp.tile(cos, (1, 2)), jnp.tile(sin, (1, 2))


def _me():
    return lax.axis_index("x"), lax.axis_index("y"), lax.axis_index("c")


def _peer(mx, my, mc, k):
    return (mx ^ ((k >> 2) & 1), my ^ ((k >> 1) & 1), mc ^ (k & 1))


def _ag_small(x, name):
    R, C = x.shape

    def body(x_ref, out_ref, send_sems, recv_sems):
        mx, my, mc = _me()
        me = 4 * mx + 2 * my + mc
        out_ref[pl.ds(me, 1)] = x_ref[...][None]
        sends = []
        for k in range(1, N_DEV):
            cp = pltpu.make_async_remote_copy(
                src_ref=x_ref, dst_ref=out_ref.at[me], send_sem=send_sems.at[k - 1], recv_sem=recv_sems.at[k - 1],
                device_id=_peer(mx, my, mc, k), device_id_type=MESH)
            cp.start()
            sends.append(cp)
        for k in range(1, N_DEV):
            pltpu.make_async_remote_copy(
                src_ref=x_ref, dst_ref=out_ref.at[me ^ k], send_sem=send_sems.at[k - 1], recv_sem=recv_sems.at[k - 1],
                device_id=(mx, my, mc), device_id_type=MESH).wait_recv()
        for cp in sends:
            cp.wait_send()

    return pl.pallas_call(
        body, name=name,
        out_shape=_sds((N_DEV, R, C), x.dtype),
        in_specs=[pl.BlockSpec(memory_space=pltpu.VMEM)],
        out_specs=pl.BlockSpec(memory_space=pltpu.VMEM),
        scratch_shapes=[pltpu.SemaphoreType.DMA((N_DEV - 1,)), pltpu.SemaphoreType.DMA((N_DEV - 1,))],
        compiler_params=pltpu.CompilerParams(vmem_limit_bytes=VMEM_LIMIT),
    )(x)


class _Comm:
    def __init__(self, gather=(), scatter=(), spread=()):
        self.kinds = ["gather"] * len(gather) + ["scatter"] * len(scatter) + ["spread"] * len(spread)
        self.args = list(gather) + list(scatter) + list(spread)
        self.n = len(self.args)

    def out_shape(self):
        return [_sds(a.shape if k == "scatter" else (N_DEV,) + a.shape, a.dtype) for k, a in zip(self.kinds, self.args)]

    def specs(self):
        return [pl.BlockSpec(memory_space=pl.ANY)] * self.n

    def scratch(self):
        return [pltpu.SemaphoreType.DMA((7 * self.n,)), pltpu.SemaphoreType.DMA((7 * self.n,)),
                pltpu.SemaphoreType.DMA((self.n,))]

    def _plan(self, x_refs, out_refs, send_sems, recv_sems, local_sems):
        mx, my, mc = _me()
        me = 4 * mx + 2 * my + mc
        here, sibling = (mx, my, mc), (mx, my, 1 - mc)
        chips = [(1 - mx, my), (mx, 1 - my), (1 - mx, 1 - my)]
        local, first, last = [], [], []
        relay = [[], [], []]
        for a, kind in enumerate(self.kinds):
            x, out = x_refs[a], out_refs[a]

            def rc(k, src, dst, to):
                return pltpu.make_async_remote_copy(
                    src_ref=src, dst_ref=dst, send_sem=send_sems.at[7 * a + k], recv_sem=recv_sems.at[7 * a + k],
                    device_id=to, device_id_type=MESH)

            if kind == "gather":
                local.append(pltpu.make_async_copy(x, out.at[me], local_sems.at[a]))
                first.append(rc(0, x, out.at[me], sibling))
                last.append(rc(0, x, out.at[me ^ 1], here))
                for j, (cx, cy) in enumerate(chips):
                    first.append(rc(1 + j, x, out.at[me], (cx, cy, mc)))
                    landed = out.at[4 * cx + 2 * cy + mc]
                    relay[j].append((rc(1 + j, x, landed, here), rc(4 + j, landed, landed, sibling)))
                    last.append(rc(4 + j, x, out.at[4 * cx + 2 * cy + 1 - mc], here))
            else:
                own = x.at[me] if kind == "scatter" else x
                local.append(pltpu.make_async_copy(own, out.at[me], local_sems.at[a]))
                for k in range(1, N_DEV):
                    src = x.at[me ^ k] if kind == "scatter" else x
                    first.append(rc(k - 1, src, out.at[me], _peer(mx, my, mc, k)))
                    last.append(rc(k - 1, own, out.at[me ^ k], here))
        return local, first, relay[0] + relay[1] + relay[2], last

    def start(self, *refs):
        local, first, _, _ = self._plan(*refs)
        for cp in local + first:
            cp.start()

    def finish(self, *refs):
        local, first, relay, last = self._plan(*refs)
        for arrival, onward in relay:
            arrival.wait_recv()
            onward.start()
        for cp in last:
            cp.wait_recv()
        for cp in first:
            cp.wait_send()
        for _, onward in relay:
            onward.wait_send()
        for cp in local:
            cp.wait()


def _call(body, *, name, grid, in_specs, out_specs, out_shape, args, scratch=(), comm=None, aliases=None):
    params = _params(("arbitrary",))
    if comm is None:
        res = pl.pallas_call(
            body, name=name, grid=grid, in_specs=list(in_specs), out_specs=list(out_specs), out_shape=list(out_shape),
            scratch_shapes=list(scratch), input_output_aliases=aliases or {}, compiler_params=params)(*args)
        return list(res), []
    n_in, n_out, n_scr, cn = len(in_specs), len(out_specs), len(scratch), comm.n
    steps = grid[0]

    def hosted(*refs):
        ins, refs = refs[:n_in], refs[n_in:]
        cins, refs = refs[:cn], refs[cn:]
        outs, refs = refs[:n_out], refs[n_out:]
        couts, refs = refs[:cn], refs[cn:]
        scr, sems = refs[:n_scr], refs[n_scr:]

        @pl.when(pl.program_id(0) == 0)
        def _():
            comm.start(cins, couts, *sems)

        body(*ins, *outs, *scr)

        @pl.when(pl.program_id(0) == steps - 1)
        def _():
            comm.finish(cins, couts, *sems)

    res = pl.pallas_call(
        hosted, name=name, grid=grid, in_specs=list(in_specs) + comm.specs(), out_specs=list(out_specs) + comm.specs(),
        out_shape=list(out_shape) + comm.out_shape(), scratch_shapes=list(scratch) + comm.scratch(),
        input_output_aliases=aliases or {}, compiler_params=params)(*args, *comm.args)
    return list(res[:n_out]), list(res[n_out:])


def _comm_only(comm, name):
    return _call(lambda: None, name=name, grid=(1,), in_specs=[], out_specs=[], out_shape=[], args=[], comm=comm)[1]


def _ada_fwd(s_in, w_ada, b_my):
    nw = w_ada.shape[1]

    def body(s_ref, w_ref, b_ref, act_ref, out_ref):
        s = s_ref[...]
        act = s * _sigmoid(s)
        act_ref[...] = act
        out_ref[...] = jnp.dot(act.astype(BF16), w_ref[...].astype(BF16), preferred_element_type=F32) + b_ref[...]

    return pl.pallas_call(
        body, name="ada_fwd", grid=(1,),
        in_specs=[_full((16, D)), _full((D, nw)), _full((1, nw))],
        out_specs=[_full((16, D)), _full((16, nw))],
        out_shape=[_sds((16, D), F32), _sds((16, nw), F32)],
        compiler_params=_params(("arbitrary",)),
    )(s_in, w_ada, b_my)


def _ada_bwd(act, dmod_my, w_ada, m, v, tr=256):
    nw = w_ada.shape[1]

    def body(act_ref, dm_ref, w_ref, m_ref, v_ref, g_ref, d_ref, m2_ref, v2_ref, pc_ref):
        dm = dm_ref[...].astype(BF16)
        g = lax.dot_general(act_ref[...].astype(BF16), dm, TN, preferred_element_type=F32)
        w = w_ref[...]
        delta, m2, v2 = _adamw(w, g, m_ref[...], v_ref[...])
        g_ref[...] = g
        d_ref[...] = delta
        m2_ref[...] = m2
        v2_ref[...] = v2
        pc_ref[...] = lax.dot_general(dm[8:16, :], w.astype(BF16), NT, preferred_element_type=F32)

    wspec = _row(tr, nw)
    return pl.pallas_call(
        body, name="ada_bwd", grid=(D // tr,),
        in_specs=[pl.BlockSpec((16, tr), lambda i: (0, i)), _full((16, nw)), wspec, wspec, wspec],
        out_specs=[wspec, wspec, wspec, wspec, pl.BlockSpec((8, tr), lambda i: (0, i))],
        out_shape=[_sds((D, nw), F32)] * 4 + [_sds((8, D), F32)],
        compiler_params=_params(("arbitrary",)),
    )(act, dmod_my, w_ada, m, v)


def _k_in(x, modv, w_in, cos, sin, tm, comm=None):
    L = x.shape[0]

    def body(x_ref, mod_ref, w_ref, cos_ref, sin_ref, h_ref, q_ref, k_ref, v_ref, u_ref, vb_ref, ga_ref, gb_ref):
        n, _ = _ln(x_ref[...])
        h = (n * (1.0 + mod_ref[1:2, :]) + mod_ref[0:1, :]).astype(BF16)
        h_ref[...] = h
        c, s = cos_ref[...], sin_ref[...]

        def proj(lo, width):
            return jnp.dot(h, w_ref[:, lo:lo + width], preferred_element_type=F32)

        for i in range(4):
            q_ref[:, i * 128:(i + 1) * 128] = (_rope(proj(O_Q + i * 128, 128), c, s) * Q_SCALE).astype(BF16)
        k_ref[...] = _rope(proj(O_K, KV_W), c, s).astype(BF16)
        v_ref[...] = proj(O_V, KV_W).astype(BF16)
        u_ref[...] = proj(O_U, GM_W).astype(BF16)
        vb_ref[...] = proj(O_VB, GM_W).astype(BF16)
        ga_ref[...] = proj(O_GA, D).astype(BF16)
        gb_ref[...] = proj(O_GB, D).astype(BF16)

    widths = [D, Q_W, KV_W, KV_W, GM_W, GM_W, D, D]
    return _call(
        body, name="fwd_in", grid=(L // tm,),
        in_specs=[_row(tm, D), _full((8, D)), _resident((D, IN_W)), _row(tm, 128), _row(tm, 128)],
        out_specs=[_row(tm, w) for w in widths],
        out_shape=[_sds((L, w), BF16) for w in widths],
        args=(x, modv, w_in, cos, sin), comm=comm)


def _k_ctx(ctx, modc, w_kv):
    C = ctx.shape[0]

    def body(c_ref, mod_ref, w_ref, hc_ref, kc_ref, vc_ref):
        n, _ = _ln(c_ref[...])
        hc = (n * (1.0 + mod_ref[1:2, :]) + mod_ref[0:1, :]).astype(BF16)
        hc_ref[...] = hc
        kv = jnp.dot(hc, w_ref[...], preferred_element_type=F32)
        kc_ref[...] = kv[:, :KV_W].astype(BF16)
        vc_ref[...] = kv[:, KV_W:].astype(BF16)

    return pl.pallas_call(
        body, name="fwd_ctx", grid=(1,),
        in_specs=[_full((C, D)), _full((8, D)), _full((D, 2 * KV_W))],
        out_specs=[_full((C, D)), _full((C, KV_W)), _full((C, KV_W))],
        out_shape=[_sds((C, D), BF16), _sds((C, KV_W), BF16), _sds((C, KV_W), BF16)],
        compiler_params=_params(("arbitrary",)),
    )(ctx, modc, w_kv)


def _attn_mask(n, L, C):
    shape = (GQA_GROUP * BLK, C + 3 * BLK)
    r = lax.broadcasted_iota(jnp.int32, shape, 0) & (BLK - 1)
    j = lax.broadcasted_iota(jnp.int32, shape, 1) - C
    rel = j - BLK - r
    kpos = n * BLK - BLK + j
    return (j < 0) | ((jnp.abs(rel) <= BLK) & (kpos >= 0) & (kpos < L))


def _sink_col(sink_ref, hk):
    grp = lax.broadcasted_iota(jnp.int32, (GQA_GROUP * BLK, 1), 0) >> 7
    col = jnp.full((GQA_GROUP * BLK, 1), sink_ref[hk * GQA_GROUP], F32)
    for g in range(1, GQA_GROUP):
        col = jnp.where(grp == g, sink_ref[hk * GQA_GROUP + g], col)
    return col


def _kv_specs(nb):
    prev = pl.BlockSpec((BLK, KV_W), lambda n: (jnp.clip(n - 1, 0, nb - 1), 0))
    cur = pl.BlockSpec((BLK, KV_W), lambda n: (jnp.minimum(n, nb - 1), 0))
    nxt = pl.BlockSpec((BLK, KV_W), lambda n: (jnp.minimum(n + 1, nb - 1), 0))
    return [prev, cur, nxt]


def _k_attn(sink, q, k, v, kc, vc, comm=None):
    L = q.shape[0]
    C = kc.shape[0]
    nb = L // BLK

    def body(sink_ref, q_ref, kp_ref, kn_ref, kx_ref, vp_ref, vn_ref, vx_ref, kc_ref, vc_ref, ya_ref, lse_ref):
        n = pl.program_id(0)
        valid = _attn_mask(n, L, C)
        for hk in range(N_KV_HEADS):
            sl = slice(hk * HEAD_DIM, (hk + 1) * HEAD_DIM)
            kcat = jnp.concatenate([kc_ref[:, sl], kp_ref[:, sl], kn_ref[:, sl], kx_ref[:, sl]], axis=0)
            vcat = jnp.concatenate([vc_ref[:, sl], vp_ref[:, sl], vn_ref[:, sl], vx_ref[:, sl]], axis=0)
            qg = jnp.concatenate(
                [q_ref[:, (hk * GQA_GROUP + g) * HEAD_DIM:(hk * GQA_GROUP + g + 1) * HEAD_DIM] for g in range(GQA_GROUP)],
                axis=0)
            s = lax.dot_general(qg, kcat, NT, preferred_element_type=F32)
            s = jnp.where(valid, s, NEG_INF)
            sink_c = _sink_col(sink_ref, hk)
            m = jnp.maximum(jnp.max(s, axis=1, keepdims=True), sink_c)
            p = jnp.exp(s - m)
            den = jnp.sum(p, axis=1, keepdims=True) + jnp.exp(sink_c - m)
            o = jnp.dot((p / den).astype(BF16), vcat, preferred_element_type=F32)
            lse = m + jnp.log(den)
            for g in range(GQA_GROUP):
                h = hk * GQA_GROUP + g
                ya_ref[:, h * HEAD_DIM:(h + 1) * HEAD_DIM] = o[g * BLK:(g + 1) * BLK, :].astype(BF16)
                lse_ref[:, h:h + 1] = lse[g * BLK:(g + 1) * BLK, :]

    kv3 = _kv_specs(nb)
    return _call(
        body, name="fwd_attn", grid=(nb,),
        in_specs=[pl.BlockSpec(memory_space=pltpu.SMEM), _row(BLK, Q_W)] + kv3 + kv3
                 + [_full((C, KV_W)), _full((C, KV_W))],
        out_specs=[_row(BLK, Q_W), _row(BLK, N_Q_HEADS)],
        out_shape=[_sds((L, Q_W), BF16), _sds((L, N_Q_HEADS), F32)],
        args=(sink, q, k, k, k, v, v, v, kc, vc), comm=comm)


def _gmlp_fwd_vals(u_ref, vb_ref, lnv_ref, ws_ref, bst_ref):
    uf = u_ref[...].astype(F32)
    vf = vb_ref[...].astype(F32)
    gu, tu = _gelu(uf)
    gv, tv = _gelu(vf)
    vhat, rstd = _ln(gv)
    vn = (vhat * lnv_ref[0:1, :] + lnv_ref[1:2, :]).astype(BF16)
    s_parts = []
    for g in range(N_GROUPS):
        sg = jnp.dot(ws_ref[g], vn[:, g * GROUP_DIM:(g + 1) * GROUP_DIM], preferred_element_type=F32)
        s_parts.append(sg + bst_ref[:, g:g + 1])
    s = jnp.concatenate(s_parts, axis=1)
    return uf, vf, gu, tu, tv, vhat, rstd, vn, s


def _k_gmlp(u, vb, lnv, ws, bst):
    L = u.shape[0]

    def body(u_ref, vb_ref, lnv_ref, ws_ref, bst_ref, yb_ref):
        _, _, gu, _, _, _, _, _, s = _gmlp_fwd_vals(u_ref, vb_ref, lnv_ref, ws_ref, bst_ref)
        yb_ref[...] = (gu * s).astype(BF16)

    return pl.pallas_call(
        body, name="fwd_gmlp", grid=(L // BLK,),
        in_specs=[_row(BLK, GM_W), _row(BLK, GM_W), _full((8, GM_W)), _full((N_GROUPS, BLK, BLK)), _full((BLK, N_GROUPS))],
        out_specs=_row(BLK, GM_W),
        out_shape=_sds((L, GM_W), BF16),
        compiler_params=_params(("arbitrary",)),
    )(u, vb, lnv, ws, bst)


def _k_merge(x, ya, yb, ga, gb, w_a, w_b, w_o, modv, lnv, tm):
    L = x.shape[0]

    def body(x_ref, ya_ref, yb_ref, ga_ref, gb_ref, wa_ref, wb_ref, wo_ref, mod_ref, ln_ref,
             mg_ref, mix_ref, xm_ref, h2_ref):
        a = jnp.dot(ya_ref[...], wa_ref[...], preferred_element_type=F32)
        b = jnp.dot(yb_ref[...], wb_ref[...], preferred_element_type=F32)
        merged = (_sigmoid(ga_ref[...].astype(F32)) * a + _sigmoid(gb_ref[...].astype(F32)) * b).astype(BF16)
        mg_ref[...] = merged
        mix = jnp.dot(merged, wo_ref[...], preferred_element_type=F32)
        mix_ref[...] = mix.astype(BF16)
        r1 = ALPHA * x_ref[...] + mod_ref[2:3, :] * mix
        r1hat, _ = _ln(r1)
        xm = r1hat * ln_ref[0:1, :] + ln_ref[1:2, :]
        xm_ref[...] = xm
        n2, _ = _ln(xm)
        h2_ref[...] = (n2 * (1.0 + mod_ref[4:5, :]) + mod_ref[3:4, :]).astype(BF16)

    return pl.pallas_call(
        body, name="fwd_merge", grid=(L // tm,),
        in_specs=[_row(tm, D), _row(tm, Q_W), _row(tm, GM_W), _row(tm, D), _row(tm, D),
                  _resident((Q_W, D)), _resident((GM_W, D)), _resident((D, D)), _full((8, D)), _full((8, D))],
        out_specs=[_row(tm, D)] * 4,
        out_shape=[_sds((L, D), BF16), _sds((L, D), BF16), _sds((L, D), F32), _sds((L, D), BF16)],
        compiler_params=_params(("arbitrary",)),
    )(x, ya, yb, ga, gb, w_a, w_b, w_o, modv, lnv)


FFN_CH = 1408


def _k_ffn(h2, xm, tgt, w_fi, w_fo, modv, lnv, tm):
    L = h2.shape[0]

    def body(h2_ref, xm_ref, t_ref, wi_ref, wo_ref, mod_ref, ln_ref, gate_ref, up_ref, a_ref, dr2_ref, df_ref, acc_ref):
        @pl.when(pl.program_id(0) == 0)
        def _():
            acc_ref[...] = jnp.zeros_like(acc_ref)

        h2v = h2_ref[...]
        f = jnp.zeros((tm, D), F32)
        for j in range(FFN_H // FFN_CH):
            lo = j * FFN_CH
            gate = jnp.dot(h2v, wi_ref[:, lo:lo + FFN_CH], preferred_element_type=F32)
            up = jnp.dot(h2v, wi_ref[:, FFN_H + lo:FFN_H + lo + FFN_CH], preferred_element_type=F32)
            act = (gate * _sigmoid(gate) * up).astype(BF16)
            gate_ref[:, lo:lo + FFN_CH] = gate.astype(BF16)
            up_ref[:, lo:lo + FFN_CH] = up.astype(BF16)
            a_ref[:, lo:lo + FFN_CH] = act
            f = f + jnp.dot(act, wo_ref[lo:lo + FFN_CH, :], preferred_element_type=F32)
        gate2 = mod_ref[5:6, :]
        r2 = ALPHA * xm_ref[...] + gate2 * f
        r2hat, rstd = _ln(r2)
        y = r2hat * ln_ref[2:3, :] + ln_ref[3:4, :]
        err = y - t_ref[...]
        dy = err * (1.0 / D)
        dr2 = _ln_bwd(dy * ln_ref[2:3, :], r2hat, rstd)
        dr2_ref[...] = dr2
        df_ref[...] = (gate2 * dr2).astype(BF16)
        acc_ref[0:1, :] += _colsum(dy * r2hat)
        acc_ref[1:2, :] += _colsum(dy)
        acc_ref[2:3, :] += _colsum(dr2 * f)
        acc_ref[3:4, :] += _colsum(err * err) * (0.5 / D)

    return pl.pallas_call(
        body, name="fwd_ffn", grid=(L // tm,),
        in_specs=[_row(tm, D), _row(tm, D), _row(tm, D), _resident((D, 2 * FFN_H)), _resident((FFN_H, D)),
                  _full((8, D)), _full((8, D))],
        out_specs=[_row(tm, FFN_H)] * 3 + [_row(tm, D), _row(tm, D), _full((8, D))],
        out_shape=[_sds((L, FFN_H), BF16)] * 3 + [_sds((L, D), F32), _sds((L, D), BF16), _sds((8, D), F32)],
        compiler_params=_params(("arbitrary",)),
    )(h2, xm, tgt, w_fi, w_fo, modv, lnv)


def _k_ffn_bwd(df, gate, up, xm, dr2, x, mix, w_fi, w_fo, modv, lnv, tm):
    L = df.shape[0]

    def body(df_ref, gate_ref, up_ref, xm_ref, dr2_ref, x_ref, mix_ref, wi_ref, wo_ref, mod_ref, ln_ref,
             dF_ref, dmix_ref, dxp_ref, acc_ref):
        @pl.when(pl.program_id(0) == 0)
        def _():
            acc_ref[...] = jnp.zeros_like(acc_ref)

        dfv = df_ref[...]
        dh2 = jnp.zeros((tm, D), F32)
        for j in range(FFN_H // FFN_CH):
            lo = j * FFN_CH
            da = lax.dot_general(dfv, wo_ref[lo:lo + FFN_CH, :], NT, preferred_element_type=F32)
            gate = gate_ref[:, lo:lo + FFN_CH].astype(F32)
            upv = up_ref[:, lo:lo + FFN_CH].astype(F32)
            sg = _sigmoid(gate)
            d_gate = (da * upv * (sg * (1.0 + gate * (1.0 - sg)))).astype(BF16)
            d_up = (da * (gate * sg)).astype(BF16)
            dF_ref[:, lo:lo + FFN_CH] = d_gate
            dF_ref[:, FFN_H + lo:FFN_H + lo + FFN_CH] = d_up
            dh2 = dh2 + lax.dot_general(d_gate, wi_ref[:, lo:lo + FFN_CH], NT, preferred_element_type=F32)
            dh2 = dh2 + lax.dot_general(d_up, wi_ref[:, FFN_H + lo:FFN_H + lo + FFN_CH], NT, preferred_element_type=F32)
        n2, rstd2 = _ln(xm_ref[...])
        acc_ref[0:1, :] += _colsum(dh2)
        acc_ref[1:2, :] += _colsum(dh2 * n2)
        dxm = ALPHA * dr2_ref[...] + _ln_bwd(dh2 * (1.0 + mod_ref[4:5, :]), n2, rstd2)
        mixf = mix_ref[...].astype(F32)
        gate1 = mod_ref[2:3, :]
        r1hat, rstd1 = _ln(ALPHA * x_ref[...] + gate1 * mixf)
        acc_ref[2:3, :] += _colsum(dxm * r1hat)
        acc_ref[3:4, :] += _colsum(dxm)
        dr1 = _ln_bwd(dxm * ln_ref[0:1, :], r1hat, rstd1)
        dmix_ref[...] = (gate1 * dr1).astype(BF16)
        dxp_ref[...] = ALPHA * dr1
        acc_ref[4:5, :] += _colsum(dr1 * mixf)

    return pl.pallas_call(
        body, name="bwd_ffn", grid=(L // tm,),
        in_specs=[_row(tm, D), _row(tm, FFN_H), _row(tm, FFN_H), _row(tm, D), _row(tm, D), _row(tm, D), _row(tm, D),
                  _resident((D, 2 * FFN_H)), _resident((FFN_H, D)), _full((8, D)), _full((8, D))],
        out_specs=[_row(tm, 2 * FFN_H), _row(tm, D), _row(tm, D), _full((8, D))],
        out_shape=[_sds((L, 2 * FFN_H), BF16), _sds((L, D), BF16), _sds((L, D), F32), _sds((8, D), F32)],
        compiler_params=_params(("arbitrary",)),
    )(df, gate, up, xm, dr2, x, mix, w_fi, w_fo, modv, lnv)


def _k_merge_bwd(dmix, ya, yb, ga, gb, w_a, w_b, w_o, tm):
    L = dmix.shape[0]

    def body(dmix_ref, ya_ref, yb_ref, ga_ref, gb_ref, wa_ref, wb_ref, wo_ref,
             dA_ref, dB_ref, dga_ref, dgb_ref, dya_ref, dyb_ref):
        dmg = lax.dot_general(dmix_ref[...], wo_ref[...], NT, preferred_element_type=F32)
        a = jnp.dot(ya_ref[...], wa_ref[...], preferred_element_type=F32)
        sa = _sigmoid(ga_ref[...].astype(F32))
        dA = (dmg * sa).astype(BF16)
        dA_ref[...] = dA
        dga_ref[...] = (dmg * a * (sa * (1.0 - sa))).astype(BF16)
        dya_ref[...] = lax.dot_general(dA, wa_ref[...], NT, preferred_element_type=F32).astype(BF16)
        b = jnp.dot(yb_ref[...], wb_ref[...], preferred_element_type=F32)
        sb = _sigmoid(gb_ref[...].astype(F32))
        dB = (dmg * sb).astype(BF16)
        dB_ref[...] = dB
        dgb_ref[...] = (dmg * b * (sb * (1.0 - sb))).astype(BF16)
        dyb_ref[...] = lax.dot_general(dB, wb_ref[...], NT, preferred_element_type=F32).astype(BF16)

    return pl.pallas_call(
        body, name="bwd_merge", grid=(L // tm,),
        in_specs=[_row(tm, D), _row(tm, Q_W), _row(tm, GM_W), _row(tm, D), _row(tm, D),
                  _resident((Q_W, D)), _resident((GM_W, D)), _resident((D, D))],
        out_specs=[_row(tm, D)] * 4 + [_row(tm, Q_W), _row(tm, GM_W)],
        out_shape=[_sds((L, D), BF16)] * 4 + [_sds((L, Q_W), BF16), _sds((L, GM_W), BF16)],
        compiler_params=_params(("arbitrary",)),
    )(dmix, ya, yb, ga, gb, w_a, w_b, w_o)


def _k_gmlp_bwd(u, vb, dyb, lnv, ws, wst, bst):
    L = u.shape[0]

    def body(u_ref, vb_ref, dyb_ref, lnv_ref, ws_ref, wst_ref, bst_ref, du_ref, dvb_ref, gws_ref, gbst_ref, gln_ref):
        @pl.when(pl.program_id(0) == 0)
        def _():
            gws_ref[...] = jnp.zeros_like(gws_ref)
            gbst_ref[...] = jnp.zeros_like(gbst_ref)
            gln_ref[...] = jnp.zeros_like(gln_ref)

        uf, vf, gu, tu, tv, vhat, rstd, vn, s = _gmlp_fwd_vals(u_ref, vb_ref, lnv_ref, ws_ref, bst_ref)
        dyb_f = dyb_ref[...].astype(F32)
        du_ref[...] = (dyb_f * s * _gelu_grad(uf, tu)).astype(BF16)
        ds = dyb_f * gu
        ds_b = ds.astype(BF16)
        dvn_parts = []
        for g in range(N_GROUPS):
            sl = slice(g * GROUP_DIM, (g + 1) * GROUP_DIM)
            gws_ref[g] += lax.dot_general(ds_b[:, sl], vn[:, sl], NT, preferred_element_type=F32)
            gbst_ref[:, g:g + 1] += jnp.sum(ds[:, sl], axis=1, keepdims=True)
            dvn_parts.append(jnp.dot(wst_ref[g], ds_b[:, sl], preferred_element_type=F32))
        dvn = jnp.concatenate(dvn_parts, axis=1)
        gln_ref[0:1, :] += _colsum(dvn * vhat)
        gln_ref[1:2, :] += _colsum(dvn)
        dgv = _ln_bwd(dvn * lnv_ref[0:1, :], vhat, rstd)
        dvb_ref[...] = (dgv * _gelu_grad(vf, tv)).astype(BF16)

    return pl.pallas_call(
        body, name="bwd_gmlp", grid=(L // BLK,),
        in_specs=[_row(BLK, GM_W)] * 3 + [_full((8, GM_W)), _full((N_GROUPS, BLK, BLK)), _full((N_GROUPS, BLK, BLK)),
                                          _full((BLK, N_GROUPS))],
        out_specs=[_row(BLK, GM_W), _row(BLK, GM_W), _full((N_GROUPS, BLK, BLK)), _full((BLK, N_GROUPS)), _full((8, GM_W))],
        out_shape=[_sds((L, GM_W), BF16), _sds((L, GM_W), BF16), _sds((N_GROUPS, BLK, BLK), F32),
                   _sds((BLK, N_GROUPS), F32), _sds((8, GM_W), F32)],
        compiler_params=_params(("arbitrary",)),
    )(u, vb, dyb, lnv, ws, wst, bst)


def _k_attn_bwd(sink, q, k, v, kc, vc, dya, lse, cos, sin, comm=None):
    L = q.shape[0]
    C = kc.shape[0]
    nb = L // BLK
    NK = C + 3 * BLK

    def body(sink_ref, q_ref, kp_ref, kn_ref, kx_ref, vp_ref, vn_ref, vx_ref, kc_ref, vc_ref, do_ref, lse_ref,
             cq_ref, sq_ref, ck_ref, sk_ref,
             dq_ref, dk_ref, dv_ref, dkc_ref, dvc_ref, dsink_ref,
             dq_scr, ck_scr, cv_scr, kp_acc, kc_acc, vp_acc, vc_acc):
        n = pl.program_id(0)

        @pl.when(n == 0)
        def _():
            for r in (kp_acc, kc_acc, vp_acc, vc_acc, dkc_ref, dvc_ref, dsink_ref):
                r[...] = jnp.zeros_like(r)

        @pl.when(n < nb)
        def _():
            valid = _attn_mask(n, L, C)
            for hk in range(N_KV_HEADS):
                sl = slice(hk * HEAD_DIM, (hk + 1) * HEAD_DIM)
                kcat = jnp.concatenate([kc_ref[:, sl], kp_ref[:, sl], kn_ref[:, sl], kx_ref[:, sl]], axis=0)
                vcat = jnp.concatenate([vc_ref[:, sl], vp_ref[:, sl], vn_ref[:, sl], vx_ref[:, sl]], axis=0)
                heads = [hk * GQA_GROUP + g for g in range(GQA_GROUP)]
                qg = jnp.concatenate([q_ref[:, h * HEAD_DIM:(h + 1) * HEAD_DIM] for h in heads], axis=0)
                dog = jnp.concatenate([do_ref[:, h * HEAD_DIM:(h + 1) * HEAD_DIM] for h in heads], axis=0)
                lse_c = jnp.concatenate([lse_ref[:, h:h + 1] for h in heads], axis=0)
                s = lax.dot_general(qg, kcat, NT, preferred_element_type=F32)
                p = jnp.exp(jnp.where(valid, s, NEG_INF) - lse_c)
                dp = lax.dot_general(dog, vcat, NT, preferred_element_type=F32)
                delta = jnp.sum(p * dp, axis=1, keepdims=True)
                ds = (p * (dp - delta)).astype(BF16)
                dqs = jnp.dot(ds, kcat, preferred_element_type=F32)
                ck_scr[:, sl] = lax.dot_general(ds, qg, TN, preferred_element_type=F32)
                cv_scr[:, sl] = lax.dot_general(p.astype(BF16), dog, TN, preferred_element_type=F32)
                p_sink = jnp.exp(_sink_col(sink_ref, hk) - lse_c) * delta
                for g, h in enumerate(heads):
                    dq_scr[:, h * HEAD_DIM:(h + 1) * HEAD_DIM] = dqs[g * BLK:(g + 1) * BLK, :]
                    tot = jnp.sum(p_sink[g * BLK:(g + 1) * BLK, :], axis=0, keepdims=True)
                    dsink_ref[h:h + 1, :] -= jnp.broadcast_to(tot, (1, 128))
            cq, sq = cq_ref[...], sq_ref[...]
            for i in range(4):
                dq_ref[:, i * 128:(i + 1) * 128] = _unrope(dq_scr[:, i * 128:(i + 1) * 128] * Q_SCALE, cq, sq).astype(BF16)
            dkc_ref[...] += ck_scr[0:C, :]
            dvc_ref[...] += cv_scr[0:C, :]

        @pl.when(n >= nb)
        def _():
            ck_scr[...] = jnp.zeros_like(ck_scr)
            cv_scr[...] = jnp.zeros_like(cv_scr)

        dk_ref[...] = _unrope(kp_acc[...] + ck_scr[C:C + BLK, :], ck_ref[...], sk_ref[...]).astype(BF16)
        dv_ref[...] = (vp_acc[...] + cv_scr[C:C + BLK, :]).astype(BF16)
        kp_acc[...] = kc_acc[...] + ck_scr[C + BLK:C + 2 * BLK, :]
        vp_acc[...] = vc_acc[...] + cv_scr[C + BLK:C + 2 * BLK, :]
        kc_acc[...] = ck_scr[C + 2 * BLK:C + 3 * BLK, :]
        vc_acc[...] = cv_scr[C + 2 * BLK:C + 3 * BLK, :]

    kv3 = _kv_specs(nb)
    cur = lambda w: pl.BlockSpec((BLK, w), lambda n: (jnp.minimum(n, nb - 1), 0))
    late = lambda w: pl.BlockSpec((BLK, w), lambda n: (jnp.maximum(n - 1, 0), 0))
    return _call(
        body, name="bwd_attn", grid=(nb + 1,),
        in_specs=[pl.BlockSpec(memory_space=pltpu.SMEM), cur(Q_W)] + kv3 + kv3
                 + [_full((C, KV_W)), _full((C, KV_W)), cur(Q_W), cur(N_Q_HEADS), cur(128), cur(128), late(128), late(128)],
        out_specs=[cur(Q_W), late(KV_W), late(KV_W), _full((C, KV_W)), _full((C, KV_W)), _full((8, 128))],
        out_shape=[_sds((L, Q_W), BF16), _sds((L, KV_W), BF16), _sds((L, KV_W), BF16),
                   _sds((C, KV_W), F32), _sds((C, KV_W), F32), _sds((8, 128), F32)],
        scratch=[pltpu.VMEM((BLK, Q_W), F32), pltpu.VMEM((NK, KV_W), F32), pltpu.VMEM((NK, KV_W), F32)]
                + [pltpu.VMEM((BLK, KV_W), F32)] * 4,
        args=(sink, q, k, k, k, v, v, v, kc, vc, dya, lse, cos, sin, cos, sin), comm=comm)


def _k_ctx_bwd(ctx, modc, hc, dkc, dvc, w_kv, gw_in):
    C = ctx.shape[0]
    kv_block = pl.BlockSpec((D, 2 * KV_W), lambda i: (0, O_K // (2 * KV_W)))

    def body(c_ref, mod_ref, hc_ref, dkc_ref, dvc_ref, w_ref, gin_ref, gw_ref, dmod_ref):
        dkv = jnp.concatenate([dkc_ref[...], dvc_ref[...]], axis=1).astype(BF16)
        gw_ref[...] = gin_ref[...] + lax.dot_general(hc_ref[...], dkv, TN, preferred_element_type=F32)
        dhc = lax.dot_general(dkv, w_ref[...], NT, preferred_element_type=F32)
        n, _ = _ln(c_ref[...])
        dmod_ref[...] = jnp.zeros_like(dmod_ref)
        dmod_ref[0:1, :] = _colsum(dhc)
        dmod_ref[1:2, :] = _colsum(dhc * n)

    return pl.pallas_call(
        body, name="bwd_ctx", grid=(1,),
        in_specs=[_full((C, D)), _full((8, D)), _full((C, D)), _full((C, KV_W)), _full((C, KV_W)), _full((D, 2 * KV_W)),
                  kv_block],
        out_specs=[kv_block, _full((8, D))],
        out_shape=[_sds((D, IN_W), F32), _sds((8, D), F32)],
        input_output_aliases={6: 0},
        compiler_params=_params(("arbitrary",)),
    )(ctx, modc, hc, dkc, dvc, w_kv, gw_in)


def _k_in_bwd(dq, dk, dv, du, dvb, dga, dgb, x, dxp, w_in, modv, tm, comm=None):
    L = x.shape[0]
    parts = [(O_Q, Q_W), (O_K, KV_W), (O_V, KV_W), (O_U, GM_W), (O_VB, GM_W), (O_GA, D), (O_GB, D)]

    def body(dq_ref, dk_ref, dv_ref, du_ref, dvb_ref, dga_ref, dgb_ref, x_ref, dxp_ref, w_ref, mod_ref,
             dP_ref, gx_ref, acc_ref):
        @pl.when(pl.program_id(0) == 0)
        def _():
            acc_ref[...] = jnp.zeros_like(acc_ref)

        for (lo, width), r in zip(parts, (dq_ref, dk_ref, dv_ref, du_ref, dvb_ref, dga_ref, dgb_ref)):
            dP_ref[:, lo:lo + width] = r[...]
        dh = lax.dot_general(dP_ref[...], w_ref[...], NT, preferred_element_type=F32)
        n1, rstd1 = _ln(x_ref[...])
        acc_ref[0:1, :] += _colsum(dh)
        acc_ref[1:2, :] += _colsum(dh * n1)
        gx_ref[...] = dxp_ref[...] + _ln_bwd(dh * (1.0 + mod_ref[1:2, :]), n1, rstd1)

    return _call(
        body, name="bwd_in", grid=(L // tm,),
        in_specs=[_row(tm, w) for _, w in parts] + [_row(tm, D), _row(tm, D), _resident((D, IN_W)), _full((8, D))],
        out_specs=[_row(tm, IN_W), _row(tm, D), _full((8, D))],
        out_shape=[_sds((L, IN_W), BF16), _sds((L, D), F32), _sds((8, D), F32)],
        args=(dq, dk, dv, du, dvb, dga, dgb, x, dxp, w_in, modv), comm=comm)


def _wgrad(a, b, name, tn, tt):
    T, K = a.shape
    N = b.shape[1]
    nt = T // tt

    def body(a_ref, b_ref, o_ref):
        @pl.when(pl.program_id(1) == 0)
        def _():
            o_ref[...] = jnp.zeros_like(o_ref)

        o_ref[...] += lax.dot_general(a_ref[...], b_ref[...], TN, preferred_element_type=F32)

    return pl.pallas_call(
        body, name=name, grid=(N // tn, nt),
        in_specs=[pl.BlockSpec((tt, K), lambda j, t: (t, 0)), pl.BlockSpec((tt, tn), lambda j, t: (t, j))],
        out_specs=pl.BlockSpec((K, tn), lambda j, t: (0, j)),
        out_shape=_sds((K, N), F32),
        compiler_params=_params(("arbitrary", "arbitrary")),
    )(a, b)


def _adamw_reduce(parts, w, m, v, name, tr):
    R, C = w.shape

    def body(p_ref, w_ref, m_ref, v_ref, g_ref, d_ref, m2_ref, v2_ref):
        g = p_ref[0].astype(F32)
        for i in range(1, N_DEV):
            g = g + p_ref[i].astype(F32)
        delta, m2, v2 = _adamw(w_ref[...], g, m_ref[...], v_ref[...])
        g_ref[...] = g
        d_ref[...] = delta
        m2_ref[...] = m2
        v2_ref[...] = v2

    spec = _row(tr, C)
    return pl.pallas_call(
        body, name=name, grid=(R // tr,),
        in_specs=[pl.BlockSpec((N_DEV, tr, C), lambda i: (0, i, 0)), spec, spec, spec],
        out_specs=[spec] * 4,
        out_shape=[_sds((R, C), F32)] * 4,
        compiler_params=_params(("arbitrary",)),
    )(parts, w, m, v)


def _small_reduce(gath, gath_ws):
    def body(g_ref, w_ref, out_ref):
        tot = g_ref[0]
        wsum = w_ref[0]
        for i in range(1, N_DEV):
            tot = tot + g_ref[i]
            wsum = wsum + w_ref[i]
        out_ref[0:16, :] = tot
        out_ref[0:2, :] = tot[0:2, :] + tot[6:8, :]
        out_ref[15:16, :] = jnp.broadcast_to(jnp.sum(tot[15:16, :], axis=1, keepdims=True), (1, D))
        out_ref[16:SMALL_ROWS, :] = wsum

    return pl.pallas_call(
        body, name="small_reduce", grid=(1,),
        in_specs=[_full((N_DEV, 16, D)), _full((N_DEV, SMALL_ROWS - 16, D))],
        out_specs=_full((SMALL_ROWS, D)),
        out_shape=_sds((SMALL_ROWS, D), F32),
        compiler_params=_params(("arbitrary",)),
    )(gath, gath_ws)


def _small_adamw(w, g, m, v, name):
    shape = w.shape

    def body(w_ref, g_ref, m_ref, v_ref, d_ref, m2_ref, v2_ref):
        delta, m2, v2 = _adamw(w_ref[...], g_ref[...], m_ref[...], v_ref[...])
        d_ref[...] = delta
        m2_ref[...] = m2
        v2_ref[...] = v2

    return pl.pallas_call(
        body, name=name, grid=(1,),
        in_specs=[_full(shape)] * 4, out_specs=[_full(shape)] * 3,
        out_shape=[_sds(shape, F32)] * 3,
        compiler_params=_params(("arbitrary",)),
    )(w, g, m, v)


def _cctx_finish(gath, c_ctx, m, v):
    def body(g_ref, c_ref, m_ref, v_ref, gr_ref, d_ref, m2_ref, v2_ref):
        ds = g_ref[0]
        for i in range(1, N_DEV):
            ds = ds + g_ref[i]
        c = c_ref[...]
        sg = _sigmoid(c)
        g = ds * (sg * (1.0 + c * (1.0 - sg)))
        delta, m2, v2 = _adamw(c, g, m_ref[...], v_ref[...])
        gr_ref[...] = g
        d_ref[...] = delta
        m2_ref[...] = m2
        v2_ref[...] = v2

    return pl.pallas_call(
        body, name="cctx_finish", grid=(1,),
        in_specs=[_full((N_DEV, 8, D))] + [_full((8, D))] * 3, out_specs=[_full((8, D))] * 4,
        out_shape=[_sds((8, D), F32)] * 4,
        compiler_params=_params(("arbitrary",)),
    )(gath, c_ctx, m, v)


def _pad_rows(a, rows):
    return jnp.concatenate([a, jnp.zeros((rows - a.shape[0], a.shape[1]), a.dtype)], axis=0)


def _pack_small(b_ada, ln1_g, ln1_b, ln2_g, ln2_b, gm_g, gm_b, b_sp, sink, w_sp):
    rows = [b_ada.reshape(6, D), jnp.zeros((2, D), F32), ln1_g.reshape(1, D), ln1_b.reshape(1, D), ln2_g.reshape(1, D),
            ln2_b.reshape(1, D), jnp.concatenate([gm_g.reshape(1, GM_W), gm_b.reshape(1, GM_W)], axis=1),
            b_sp.reshape(1, D), _pad_rows(sink.reshape(1, N_Q_HEADS).T, D).T.reshape(1, D), jnp.zeros((1, D), F32),
            w_sp.reshape(N_GROUPS * BLK * BLK // D, D)]
    return jnp.concatenate(rows, axis=0)


def _unpack_small(p):
    return dict(b_ada=p[0:6].reshape(1, 6 * D), ln1_g=p[8:9], ln1_b=p[9:10], ln2_g=p[10:11], ln2_b=p[11:12],
                gmlp_ln_g=p[12:13, :GM_W], gmlp_ln_b=p[12:13, GM_W:], b_spatial=p[13:14].reshape(1, N_GROUPS, BLK),
                attn_sink=p[14:15, :N_Q_HEADS], w_spatial=p[16:].reshape(1, N_GROUPS, BLK, BLK))


def kernel(x, c, ctx, c_ctx, w_ada, b_ada, w_in, attn_sink, gmlp_ln_g, gmlp_ln_b, w_spatial, b_spatial, w_branch_a, w_branch_b, w_out, ln1_g, ln1_b, w_ffn_in, w_ffn_out, ln2_g, ln2_b, loss_target, m_c_ctx, m_w_ada, m_b_ada, m_w_in, m_attn_sink, m_gmlp_ln_g, m_gmlp_ln_b, m_w_spatial, m_b_spatial, m_w_branch_a, m_w_branch_b, m_w_out, m_ln1_g, m_ln1_b, m_w_ffn_in, m_w_ffn_out, m_ln2_g, m_ln2_b, v_c_ctx, v_w_ada, v_b_ada, v_w_in, v_attn_sink, v_gmlp_ln_g, v_gmlp_ln_b, v_w_spatial, v_b_spatial, v_w_branch_a, v_w_branch_b, v_w_out, v_ln1_g, v_ln1_b, v_w_ffn_in, v_w_ffn_out, v_ln2_g, v_ln2_b):
    L = x.shape[1]
    me = 4 * lax.axis_index("x") + 2 * lax.axis_index("y") + lax.axis_index("c")
    x2, tgt, ctx2 = x[0], loss_target[0], ctx[0]
    tm_in = min(512, L)
    tm = min(256, L)
    tt = min(512, L)

    big = dict(w_in=w_in[0], w_branch_a=w_branch_a[0], w_branch_b=w_branch_b[0], w_out=w_out[0],
               w_ffn_in=w_ffn_in[0], w_ffn_out=w_ffn_out[0])
    col_sharded = ("w_in", "w_branch_a", "w_branch_b", "w_ffn_in")
    shard_bf = {k: a.astype(BF16) for k, a in big.items()}

    def assemble(kname, g):
        if kname in col_sharded:
            return g.transpose(1, 0, 2).reshape(g.shape[1], N_DEV * g.shape[2])
        return g.reshape(N_DEV * g.shape[1], g.shape[2])

    def to_blocks(kname, g):
        g = g.astype(BF16)
        if kname in col_sharded:
            return g.reshape(g.shape[0], N_DEV, g.shape[1] // N_DEV).transpose(1, 0, 2)
        return g.reshape(N_DEV, g.shape[0] // N_DEV, g.shape[1])

    full = {}
    full["w_in"] = assemble("w_in", _comm_only(_Comm(gather=[shard_bf["w_in"]]), "gather_w_in")[0])
    c_all = _ag_small(_pad_rows(c, 8), "gather_c")[:, 0, :]
    s_in = jnp.concatenate([c_all, c_ctx[None, :], jnp.zeros((7, D), F32)], axis=0)
    n_ada = w_ada.shape[2]
    b_my = lax.dynamic_slice(b_ada, (0, me * n_ada), (1, n_ada))
    act, mod_my = _ada_fwd(s_in, w_ada[0], b_my)
    mod_all = _ag_small(mod_my, "gather_mod").transpose(1, 0, 2).reshape(16, 6 * D)
    modv = _pad_rows(lax.dynamic_slice(mod_all, (me, 0), (1, 6 * D)).reshape(6, D), 8)
    modc = _pad_rows(mod_all[8].reshape(6, D), 8)

    lnv = _pad_rows(jnp.concatenate([ln1_g, ln1_b, ln2_g, ln2_b], axis=0), 8)
    gm_lnv = _pad_rows(jnp.concatenate([gmlp_ln_g, gmlp_ln_b], axis=0), 8)
    ws_b = w_spatial[0].astype(BF16)
    wst_b = ws_b.transpose(0, 2, 1)
    bst = b_spatial[0].T
    sink = attn_sink[0]
    cos, sin = _rope_tables(L)
    w_kv = full["w_in"][:, O_K:O_K + 2 * KV_W]

    (h, q, k, v, u, vb, ga, gb), got = _k_in(
        x2, modv, full["w_in"], cos, sin, tm_in,
        comm=_Comm(gather=[shard_bf["w_branch_a"], shard_bf["w_branch_b"], shard_bf["w_out"]]))
    for kname, g in zip(("w_branch_a", "w_branch_b", "w_out"), got):
        full[kname] = assemble(kname, g)
    hc, kc, vc = _k_ctx(ctx2, modc, w_kv)
    (ya, lse), got = _k_attn(sink, q, k, v, kc, vc, comm=_Comm(gather=[shard_bf["w_ffn_in"], shard_bf["w_ffn_out"]]))
    for kname, g in zip(("w_ffn_in", "w_ffn_out"), got):
        full[kname] = assemble(kname, g)
    yb = _k_gmlp(u, vb, gm_lnv, ws_b, bst)
    merged, mix, xm, h2 = _k_merge(x2, ya, yb, ga, gb, full["w_branch_a"], full["w_branch_b"], full["w_out"], modv, lnv, tm)
    gate, up, act_f, dr2, df, acc_f = _k_ffn(h2, xm, tgt, full["w_ffn_in"], full["w_ffn_out"], modv, lnv, tm)

    dF, dmix, dxp, acc_b = _k_ffn_bwd(df, gate, up, xm, dr2, x2, mix, full["w_ffn_in"], full["w_ffn_out"], modv, lnv, tm)
    blk_fi = to_blocks("w_ffn_in", _wgrad(h2, dF, "wgrad_ffn_in", 1408, tt))
    blk_fo = to_blocks("w_ffn_out", _wgrad(act_f, df, "wgrad_ffn_out", 512, tt))
    dA, dB, dga, dgb, dya, dyb = _k_merge_bwd(dmix, ya, yb, ga, gb, full["w_branch_a"], full["w_branch_b"], full["w_out"], tm)
    du, dvb, g_ws, g_bst, g_gln = _k_gmlp_bwd(u, vb, dyb, gm_lnv, ws_b, wst_b, bst)
    (dq, dk, dv, dkc, dvc, g_sink), (rcv_fi, rcv_fo, gath_ws) = _k_attn_bwd(
        sink, q, k, v, kc, vc, dya, lse, cos, sin,
        comm=_Comm(scatter=[blk_fi, blk_fo], spread=[g_ws.reshape(SMALL_ROWS - 16, D)]))
    blk_a = to_blocks("w_branch_a", _wgrad(ya, dA, "wgrad_a", D, tt))
    blk_b = to_blocks("w_branch_b", _wgrad(yb, dB, "wgrad_b", D, tt))
    blk_o = to_blocks("w_out", _wgrad(merged, dmix, "wgrad_out", D, tt))
    (dP, grad_x, acc_i), (rcv_a, rcv_b, rcv_o) = _k_in_bwd(
        dq, dk, dv, du, dvb, dga, dgb, x2, dxp, full["w_in"], modv, tm, comm=_Comm(scatter=[blk_a, blk_b, blk_o]))
    gw_in, dmodc = _k_ctx_bwd(ctx2, modc, hc, dkc, dvc, w_kv, _wgrad(h, dP, "wgrad_in", 1280, tt))

    dmod_x = jnp.concatenate([acc_i[0:2], acc_b[4:5], acc_b[0:2], acc_f[2:3]], axis=0)
    small = jnp.concatenate([
        dmod_x, dmodc[0:2], acc_b[2:4], acc_f[0:2],
        jnp.concatenate([g_gln[0:1], g_gln[1:2]], axis=1), g_bst.T.reshape(1, D),
        _pad_rows(g_sink[:, 0:1], D).T, acc_f[3:4]], axis=0)
    rcv_in, gath = _comm_only(_Comm(scatter=[to_blocks("w_in", gw_in)], spread=[small]), "exchange_last")
    received = dict(w_in=rcv_in, w_branch_a=rcv_a, w_branch_b=rcv_b, w_out=rcv_o, w_ffn_in=rcv_fi, w_ffn_out=rcv_fo)
    moments = dict(w_in=(m_w_in, v_w_in), w_branch_a=(m_w_branch_a, v_w_branch_a), w_branch_b=(m_w_branch_b, v_w_branch_b),
                   w_out=(m_w_out, v_w_out), w_ffn_in=(m_w_ffn_in, v_w_ffn_in), w_ffn_out=(m_w_ffn_out, v_w_ffn_out))
    names = list(big)
    res = {}
    for kname in names:
        mm, vv = moments[kname]
        R = big[kname].shape[0]
        res[kname] = _adamw_reduce(received[kname], big[kname], mm[0], vv[0], "adamw_" + kname, 256 if R % 256 == 0 else R // 2)

    tot = _small_reduce(gath, gath_ws)
    g_small = _unpack_small(tot)
    loss = tot[15, 0]

    p_w = _pack_small(b_ada, ln1_g, ln1_b, ln2_g, ln2_b, gmlp_ln_g, gmlp_ln_b, b_spatial, attn_sink, w_spatial)
    p_m = _pack_small(m_b_ada, m_ln1_g, m_ln1_b, m_ln2_g, m_ln2_b, m_gmlp_ln_g, m_gmlp_ln_b, m_b_spatial, m_attn_sink, m_w_spatial)
    p_v = _pack_small(v_b_ada, v_ln1_g, v_ln1_b, v_ln2_g, v_ln2_b, v_gmlp_ln_g, v_gmlp_ln_b, v_b_spatial, v_attn_sink, v_w_spatial)
    s_d, s_m, s_v = [_unpack_small(t) for t in _small_adamw(p_w, tot, p_m, p_v, "adamw_small")]

    dmod_rows = jnp.concatenate([gath[:, 0:6, :].reshape(N_DEV, 6 * D),
                                 jnp.concatenate([tot[6:8].reshape(1, 2 * D), jnp.zeros((1, 4 * D), F32)], axis=1),
                                 jnp.zeros((7, 6 * D), F32)], axis=0)
    dmod_my = lax.dynamic_slice(dmod_rows, (0, me * n_ada), (16, n_ada))
    g_wada, d_wada, m2_wada, v2_wada, pc = _ada_bwd(act, dmod_my, w_ada[0], m_w_ada[0], v_w_ada[0])
    pc_all = _ag_small(pc, "gather_cctx")
    cc8 = lambda a: _pad_rows(a.reshape(1, D), 8)
    g_cc, d_cc, m2_cc, v2_cc = _cctx_finish(pc_all, cc8(c_ctx), cc8(m_c_ctx), cc8(v_c_ctx))

    order = ["c_ctx", "w_ada", "b_ada", "w_in", "attn_sink", "gmlp_ln_g", "gmlp_ln_b", "w_spatial", "b_spatial",
             "w_branch_a", "w_branch_b", "w_out", "ln1_g", "ln1_b", "w_ffn_in", "w_ffn_out", "ln2_g", "ln2_b"]
    grads, deltas, new_m, new_v = {}, {}, {}, {}
    grads["c_ctx"], deltas["c_ctx"], new_m["c_ctx"], new_v["c_ctx"] = g_cc[0], d_cc[0], m2_cc[0], v2_cc[0]
    grads["w_ada"], deltas["w_ada"], new_m["w_ada"], new_v["w_ada"] = g_wada[None], d_wada[None], m2_wada[None], v2_wada[None]
    for kname in names:
        g, d, m2, v2 = res[kname]
        grads[kname], deltas[kname], new_m[kname], new_v[kname] = g[None], d[None], m2[None], v2[None]
    for kname in ("b_ada", "attn_sink", "gmlp_ln_g", "gmlp_ln_b", "w_spatial", "b_spatial", "ln1_g", "ln1_b", "ln2_g", "ln2_b"):
        grads[kname], deltas[kname], new_m[kname], new_v[kname] = g_small[kname], s_d[kname], s_m[kname], s_v[kname]
    return (loss, grad_x[None], *[grads[n] for n in order], *[deltas[n] for n in order],
            *[new_m[n] for n in order], *[new_v[n] for n in order])
```

```python
import functools
import math

import jax
import jax.numpy as jnp
from jax import lax
from jax.experimental import pallas as pl
from jax.experimental.pallas import tpu as pltpu

F32 = jnp.float32
BF16 = jnp.bfloat16
MESH = pl.DeviceIdType.MESH

N_DEV = 8
D = 1024
HEAD_DIM = 64
N_Q_HEADS = 8
N_KV_HEADS = 2
GQA_GROUP = 4
BLK = 128
Q_W = 512
KV_W = 128
GM_W = 512
N_GROUPS = 8
GROUP_DIM = 64
FFN_H = 2816
IN_W = 3840
O_Q, O_K, O_V, O_U, O_VB, O_GA, O_GB = 0, 512, 640, 768, 1280, 1792, 2816
LN_EPS = 1e-5
NEG_INF = -1e30
ALPHA = 2.0 ** 0.25
ROPE_BASE = 10000.0
ROPE_PAIRS = 16
Q_SCALE = HEAD_DIM ** -0.5
GELU_K0 = math.sqrt(2.0 / math.pi)
GELU_K1 = 0.044715

ADAM_LR = 0.001
ADAM_B1 = 0.9
ADAM_B2 = 0.999
ADAM_EPS = 1e-08
ADAM_WD = 0.01
ADAM_STEP = 10

VMEM_LIMIT = 56 * 1024 * 1024
SMALL_ROWS = 144
NT = (((1,), (1,)), ((), ()))
TN = (((0,), (0,)), ((), ()))


def _params(sem=None):
    return pltpu.CompilerParams(dimension_semantics=sem, vmem_limit_bytes=VMEM_LIMIT)


def _row(tm, w):
    return pl.BlockSpec((tm, w), lambda i: (i, 0))


def _full(shape):
    nd = len(shape)
    return pl.BlockSpec(shape, lambda i: (0,) * nd)


def _resident(shape):
    nd = len(shape)
    return pl.BlockSpec(shape, lambda i: (0,) * nd, pipeline_mode=pl.Buffered(1))


def _sds(shape, dt):
    return jax.ShapeDtypeStruct(shape, dt)


def _ln(xf):
    mu = jnp.mean(xf, axis=-1, keepdims=True)
    xc = xf - mu
    var = jnp.mean(xc * xc, axis=-1, keepdims=True)
    rstd = lax.rsqrt(var + LN_EPS)
    return xc * rstd, rstd


def _ln_bwd(dn, n, rstd):
    m1 = jnp.mean(dn, axis=-1, keepdims=True)
    m2 = jnp.mean(dn * n, axis=-1, keepdims=True)
    return rstd * (dn - m1 - n * m2)


def _colsum(t):
    return jnp.sum(t, axis=0, keepdims=True)


def _sigmoid(x):
    return 0.5 * jnp.tanh(0.5 * x) + 0.5


def _gelu(x):
    t = jnp.tanh(GELU_K0 * (x + GELU_K1 * (x * x * x)))
    return x * (0.5 * (1.0 + t)), t


def _gelu_grad(x, t):
    return 0.5 * (1.0 + t) + 0.5 * x * (1.0 - t * t) * (GELU_K0 * (1.0 + 3.0 * GELU_K1 * x * x))


def _swap16(t):
    lane = lax.broadcasted_iota(jnp.int32, t.shape, 1)
    return jnp.where((lane & 16) == 0, pltpu.roll(t, 112, 1), pltpu.roll(t, 16, 1))


def _rope(t, cos, sin):
    return t * cos + _swap16(t) * sin


def _unrope(t, cos, sin):
    return t * cos - _swap16(t) * sin


def _adamw(w, g, m, v):
    m2 = ADAM_B1 * m + (1.0 - ADAM_B1) * g
    v2 = ADAM_B2 * v + (1.0 - ADAM_B2) * (g * g)
    m_hat = m2 / (1.0 - ADAM_B1 ** ADAM_STEP)
    v_hat = v2 / (1.0 - ADAM_B2 ** ADAM_STEP)
    delta = -ADAM_LR * (m_hat / (jnp.sqrt(v_hat) + ADAM_EPS) + ADAM_WD * w)
    return delta, m2, v2


def _rope_tables(L):
    inv = ROPE_BASE ** (-jnp.arange(ROPE_PAIRS, dtype=F32) / ROPE_PAIRS)
    t = jnp.arange(L, dtype=jnp.int32)
    rows = (t // 64).astype(F32)[:, None] * inv
    cols = (t % 64).astype(F32)[:, None] * inv
    cr, sr, cc, sc = jnp.cos(rows), jnp.sin(rows), jnp.cos(cols), jnp.sin(cols)
    cos = jnp.concatenate([cr, cr, cc, cc], axis=1)
    sin = jnp.concatenate([-sr, sr, -sc, sc], axis=1)
    return jnp.tile(cos, (1, 2)), jnp.tile(sin, (1, 2))


def _me():
    return lax.axis_index("x"), lax.axis_index("y"), lax.axis_index("c")


def _peer(mx, my, mc, k):
    return (mx ^ ((k >> 2) & 1), my ^ ((k >> 1) & 1), mc ^ (k & 1))


def _ag_small(x, name):
    R, C = x.shape

    def body(x_ref, out_ref, send_sems, recv_sems):
        mx, my, mc = _me()
        me = 4 * mx + 2 * my + mc
        out_ref[pl.ds(me, 1)] = x_ref[...][None]
        sends = []
        for k in range(1, N_DEV):
            cp = pltpu.make_async_remote_copy(
                src_ref=x_ref, dst_ref=out_ref.at[me], send_sem=send_sems.at[k - 1], recv_sem=recv_sems.at[k - 1],
                device_id=_peer(mx, my, mc, k), device_id_type=MESH)
            cp.start()
            sends.append(cp)
        for k in range(1, N_DEV):
            pltpu.make_async_remote_copy(
                src_ref=x_ref, dst_ref=out_ref.at[me ^ k], send_sem=send_sems.at[k - 1], recv_sem=recv_sems.at[k - 1],
                device_id=(mx, my, mc), device_id_type=MESH).wait_recv()
        for cp in sends:
            cp.wait_send()

    return pl.pallas_call(
        body, name=name,
        out_shape=_sds((N_DEV, R, C), x.dtype),
        in_specs=[pl.BlockSpec(memory_space=pltpu.VMEM)],
        out_specs=pl.BlockSpec(memory_space=pltpu.VMEM),
        scratch_shapes=[pltpu.SemaphoreType.DMA((N_DEV - 1,)), pltpu.SemaphoreType.DMA((N_DEV - 1,))],
        compiler_params=pltpu.CompilerParams(vmem_limit_bytes=VMEM_LIMIT),
    )(x)


class _Comm:
    def __init__(self, gather=(), scatter=(), spread=()):
        self.kinds = ["gather"] * len(gather) + ["scatter"] * len(scatter) + ["spread"] * len(spread)
        self.args = list(gather) + list(scatter) + list(spread)
        self.n = len(self.args)

    def out_shape(self):
        return [_sds(a.shape if k == "scatter" else (N_DEV,) + a.shape, a.dtype) for k, a in zip(self.kinds, self.args)]

    def specs(self):
        return [pl.BlockSpec(memory_space=pl.ANY)] * self.n

    def scratch(self):
        return [pltpu.SemaphoreType.DMA((7 * self.n,)), pltpu.SemaphoreType.DMA((7 * self.n,)),
                pltpu.SemaphoreType.DMA((self.n,))]

    def _plan(self, x_refs, out_refs, send_sems, recv_sems, local_sems):
        mx, my, mc = _me()
        me = 4 * mx + 2 * my + mc
        here, sibling = (mx, my, mc), (mx, my, 1 - mc)
        chips = [(1 - mx, my), (mx, 1 - my), (1 - mx, 1 - my)]
        local, first, last = [], [], []
        relay = [[], [], []]
        for a, kind in enumerate(self.kinds):
            x, out = x_refs[a], out_refs[a]

            def rc(k, src, dst, to):
                return pltpu.make_async_remote_copy(
                    src_ref=src, dst_ref=dst, send_sem=send_sems.at[7 * a + k], recv_sem=recv_sems.at[7 * a + k],
                    device_id=to, device_id_type=MESH)

            if kind == "gather":
                local.append(pltpu.make_async_copy(x, out.at[me], local_sems.at[a]))
                first.append(rc(0, x, out.at[me], sibling))
                last.append(rc(0, x, out.at[me ^ 1], here))
                for j, (cx, cy) in enumerate(chips):
                    first.append(rc(1 + j, x, out.at[me], (cx, cy, mc)))
                    landed = out.at[4 * cx + 2 * cy + mc]
                    relay[j].append((rc(1 + j, x, landed, here), rc(4 + j, landed, landed, sibling)))
                    last.append(rc(4 + j, x, out.at[4 * cx + 2 * cy + 1 - mc], here))
            else:
                own = x.at[me] if kind == "scatter" else x
                local.append(pltpu.make_async_copy(own, out.at[me], local_sems.at[a]))
                for k in range(1, N_DEV):
                    src = x.at[me ^ k] if kind == "scatter" else x
                    first.append(rc(k - 1, src, out.at[me], _peer(mx, my, mc, k)))
                    last.append(rc(k - 1, own, out.at[me ^ k], here))
        return local, first, relay[0] + relay[1] + relay[2], last

    def start(self, *refs):
        local, first, _, _ = self._plan(*refs)
        for cp in local + first:
            cp.start()

    def finish(self, *refs):
        local, first, relay, last = self._plan(*refs)
        for arrival, onward in relay:
            arrival.wait_recv()
            onward.start()
        for cp in last:
            cp.wait_recv()
        for cp in first:
            cp.wait_send()
        for _, onward in relay:
            onward.wait_send()
        for cp in local:
            cp.wait()


def _call(body, *, name, grid, in_specs, out_specs, out_shape, args, scratch=(), comm=None, aliases=None):
    params = _params(("arbitrary",) * len(grid))

    def at(end):
        conds = [pl.program_id(d) == (n - 1 if end else 0) for d, n in enumerate(grid)]
        return functools.reduce(lambda p, q: p & q, conds)

    if comm is None:
        res = pl.pallas_call(
            body, name=name, grid=grid, in_specs=list(in_specs), out_specs=list(out_specs), out_shape=list(out_shape),
            scratch_shapes=list(scratch), input_output_aliases=aliases or {}, compiler_params=params)(*args)
        return list(res), []
    n_in, n_out, n_scr, cn = len(in_specs), len(out_specs), len(scratch), comm.n

    def hosted(*refs):
        ins, refs = refs[:n_in], refs[n_in:]
        cins, refs = refs[:cn], refs[cn:]
        outs, refs = refs[:n_out], refs[n_out:]
        couts, refs = refs[:cn], refs[cn:]
        scr, sems = refs[:n_scr], refs[n_scr:]

        @pl.when(at(False))
        def _():
            comm.start(cins, couts, *sems)

        body(*ins, *outs, *scr)

        @pl.when(at(True))
        def _():
            comm.finish(cins, couts, *sems)

    res = pl.pallas_call(
        hosted, name=name, grid=grid, in_specs=list(in_specs) + comm.specs(), out_specs=list(out_specs) + comm.specs(),
        out_shape=list(out_shape) + comm.out_shape(), scratch_shapes=list(scratch) + comm.scratch(),
        input_output_aliases=aliases or {}, compiler_params=params)(*args, *comm.args)
    return list(res[:n_out]), list(res[n_out:])


def _comm_only(comm, name):
    return _call(lambda: None, name=name, grid=(1,), in_specs=[], out_specs=[], out_shape=[], args=[], comm=comm)[1]


def _ada_fwd(s_in, w_ada, b_my):
    nw = w_ada.shape[1]

    def body(s_ref, w_ref, b_ref, act_ref, out_ref):
        s = s_ref[...]
        act = s * _sigmoid(s)
        act_ref[...] = act
        out_ref[...] = jnp.dot(act.astype(BF16), w_ref[...].astype(BF16), preferred_element_type=F32) + b_ref[...]

    return pl.pallas_call(
        body, name="ada_fwd", grid=(1,),
        in_specs=[_full((16, D)), _full((D, nw)), _full((1, nw))],
        out_specs=[_full((16, D)), _full((16, nw))],
        out_shape=[_sds((16, D), F32), _sds((16, nw), F32)],
        compiler_params=_params(("arbitrary",)),
    )(s_in, w_ada, b_my)


def _ada_bwd(act, dmod_my, w_ada, m, v, tr=256):
    nw = w_ada.shape[1]

    def body(act_ref, dm_ref, w_ref, m_ref, v_ref, g_ref, d_ref, m2_ref, v2_ref, pc_ref):
        dm = dm_ref[...].astype(BF16)
        g = lax.dot_general(act_ref[...].astype(BF16), dm, TN, preferred_element_type=F32)
        w = w_ref[...]
        delta, m2, v2 = _adamw(w, g, m_ref[...], v_ref[...])
        g_ref[...] = g
        d_ref[...] = delta
        m2_ref[...] = m2
        v2_ref[...] = v2
        pc_ref[...] = lax.dot_general(dm[8:16, :], w.astype(BF16), NT, preferred_element_type=F32)

    wspec = _row(tr, nw)
    return pl.pallas_call(
        body, name="ada_bwd", grid=(D // tr,),
        in_specs=[pl.BlockSpec((16, tr), lambda i: (0, i)), _full((16, nw)), wspec, wspec, wspec],
        out_specs=[wspec, wspec, wspec, wspec, pl.BlockSpec((8, tr), lambda i: (0, i))],
        out_shape=[_sds((D, nw), F32)] * 4 + [_sds((8, D), F32)],
        compiler_params=_params(("arbitrary",)),
    )(act, dmod_my, w_ada, m, v)


def _k_in(x, modv, w_in, cos, sin, tm, comm=None):
    L = x.shape[0]

    def body(x_ref, mod_ref, w_ref, cos_ref, sin_ref, h_ref, q_ref, k_ref, v_ref, u_ref, vb_ref, ga_ref, gb_ref):
        n, _ = _ln(x_ref[...])
        h = (n * (1.0 + mod_ref[1:2, :]) + mod_ref[0:1, :]).astype(BF16)
        h_ref[...] = h
        c, s = cos_ref[...], sin_ref[...]

        def proj(lo, width):
            return jnp.dot(h, w_ref[:, lo:lo + width], preferred_element_type=F32)

        for i in range(4):
            q_ref[:, i * 128:(i + 1) * 128] = (_rope(proj(O_Q + i * 128, 128), c, s) * Q_SCALE).astype(BF16)
        k_ref[...] = _rope(proj(O_K, KV_W), c, s).astype(BF16)
        v_ref[...] = proj(O_V, KV_W).astype(BF16)
        u_ref[...] = proj(O_U, GM_W).astype(BF16)
        vb_ref[...] = proj(O_VB, GM_W).astype(BF16)
        ga_ref[...] = proj(O_GA, D).astype(BF16)
        gb_ref[...] = proj(O_GB, D).astype(BF16)

    widths = [D, Q_W, KV_W, KV_W, GM_W, GM_W, D, D]
    return _call(
        body, name="fwd_in", grid=(L // tm,),
        in_specs=[_row(tm, D), _full((8, D)), _resident((D, IN_W)), _row(tm, 128), _row(tm, 128)],
        out_specs=[_row(tm, w) for w in widths],
        out_shape=[_sds((L, w), BF16) for w in widths],
        args=(x, modv, w_in, cos, sin), comm=comm)


def _k_ctx(ctx, modc, w_kv):
    C = ctx.shape[0]

    def body(c_ref, mod_ref, w_ref, hc_ref, kc_ref, vc_ref):
        n, _ = _ln(c_ref[...])
        hc = (n * (1.0 + mod_ref[1:2, :]) + mod_ref[0:1, :]).astype(BF16)
        hc_ref[...] = hc
        kv = jnp.dot(hc, w_ref[...], preferred_element_type=F32)
        kc_ref[...] = kv[:, :KV_W].astype(BF16)
        vc_ref[...] = kv[:, KV_W:].astype(BF16)

    return pl.pallas_call(
        body, name="fwd_ctx", grid=(1,),
        in_specs=[_full((C, D)), _full((8, D)), _full((D, 2 * KV_W))],
        out_specs=[_full((C, D)), _full((C, KV_W)), _full((C, KV_W))],
        out_shape=[_sds((C, D), BF16), _sds((C, KV_W), BF16), _sds((C, KV_W), BF16)],
        compiler_params=_params(("arbitrary",)),
    )(ctx, modc, w_kv)


def _attn_bias():
    shape = (GQA_GROUP * BLK, 3 * BLK)
    r = lax.broadcasted_iota(jnp.int32, shape, 0) & (BLK - 1)
    j = lax.broadcasted_iota(jnp.int32, shape, 1)
    band = jnp.abs(j - BLK - r) <= BLK
    variants = [band & (j >= BLK), band, band & (j < 2 * BLK)]
    return jnp.stack([jnp.where(v, 0.0, NEG_INF).astype(F32) for v in variants])


def _bias_spec(nb):
    return pl.BlockSpec((1, GQA_GROUP * BLK, 3 * BLK),
                        lambda n: (jnp.where(n == 0, 0, jnp.where(n >= nb - 1, 2, 1)), 0, 0))


def _masked(s, bias, C):
    return jnp.concatenate([s[:, :C], s[:, C:] + bias], axis=1)


def _sink_col(sink_ref, hk):
    grp = lax.broadcasted_iota(jnp.int32, (GQA_GROUP * BLK, 1), 0) >> 7
    col = jnp.full((GQA_GROUP * BLK, 1), sink_ref[hk * GQA_GROUP], F32)
    for g in range(1, GQA_GROUP):
        col = jnp.where(grp == g, sink_ref[hk * GQA_GROUP + g], col)
    return col


def _kv_specs(nb):
    prev = pl.BlockSpec((BLK, KV_W), lambda n: (jnp.clip(n - 1, 0, nb - 1), 0))
    cur = pl.BlockSpec((BLK, KV_W), lambda n: (jnp.minimum(n, nb - 1), 0))
    nxt = pl.BlockSpec((BLK, KV_W), lambda n: (jnp.minimum(n + 1, nb - 1), 0))
    return [prev, cur, nxt]


def _k_attn(sink, q, k, v, kc, vc, bias, comm=None):
    L = q.shape[0]
    C = kc.shape[0]
    nb = L // BLK

    def body(sink_ref, q_ref, kp_ref, kn_ref, kx_ref, vp_ref, vn_ref, vx_ref, kc_ref, vc_ref, bias_ref, ya_ref, lse_ref):
        band = bias_ref[0]
        for hk in range(N_KV_HEADS):
            sl = slice(hk * HEAD_DIM, (hk + 1) * HEAD_DIM)
            kcat = jnp.concatenate([kc_ref[:, sl], kp_ref[:, sl], kn_ref[:, sl], kx_ref[:, sl]], axis=0)
            vcat = jnp.concatenate([vc_ref[:, sl], vp_ref[:, sl], vn_ref[:, sl], vx_ref[:, sl]], axis=0)
            qg = jnp.concatenate(
                [q_ref[:, (hk * GQA_GROUP + g) * HEAD_DIM:(hk * GQA_GROUP + g + 1) * HEAD_DIM] for g in range(GQA_GROUP)],
                axis=0)
            s = _masked(lax.dot_general(qg, kcat, NT, preferred_element_type=F32), band, C)
            sink_c = _sink_col(sink_ref, hk)
            m = jnp.maximum(jnp.max(s, axis=1, keepdims=True), sink_c)
            p = jnp.exp(s - m)
            den = jnp.sum(p, axis=1, keepdims=True) + jnp.exp(sink_c - m)
            o = jnp.dot(p.astype(BF16), vcat, preferred_element_type=F32) * (1.0 / den)
            lse = m + jnp.log(den)
            for g in range(GQA_GROUP):
                h = hk * GQA_GROUP + g
                ya_ref[:, h * HEAD_DIM:(h + 1) * HEAD_DIM] = o[g * BLK:(g + 1) * BLK, :].astype(BF16)
                lse_ref[:, h:h + 1] = lse[g * BLK:(g + 1) * BLK, :]

    kv3 = _kv_specs(nb)
    return _call(
        body, name="fwd_attn", grid=(nb,),
        in_specs=[pl.BlockSpec(memory_space=pltpu.SMEM), _row(BLK, Q_W)] + kv3 + kv3
                 + [_full((C, KV_W)), _full((C, KV_W)), _bias_spec(nb)],
        out_specs=[_row(BLK, Q_W), _row(BLK, N_Q_HEADS)],
        out_shape=[_sds((L, Q_W), BF16), _sds((L, N_Q_HEADS), F32)],
        args=(sink, q, k, k, k, v, v, v, kc, vc, bias), comm=comm)


GMLP_CHUNKS = 4


def _gmlp_fwd_vals(u, vb, lnv_ref, ws_ref, bst_ref):
    uf = u.astype(F32)
    vf = vb.astype(F32)
    gu, tu = _gelu(uf)
    gv, tv = _gelu(vf)
    vhat, rstd = _ln(gv)
    vn = (vhat * lnv_ref[0:1, :] + lnv_ref[1:2, :]).astype(BF16)
    s_parts = []
    for g in range(N_GROUPS):
        sg = jnp.dot(ws_ref[g], vn[:, g * GROUP_DIM:(g + 1) * GROUP_DIM], preferred_element_type=F32)
        s_parts.append(sg + bst_ref[:, g:g + 1])
    s = jnp.concatenate(s_parts, axis=1)
    return uf, vf, gu, tu, tv, vhat, rstd, vn, s


def _k_gmlp(u, vb, lnv, ws, bst):
    L = u.shape[0]
    nch = min(GMLP_CHUNKS, L // BLK)
    tm = nch * BLK

    def body(u_ref, vb_ref, lnv_ref, ws_ref, bst_ref, yb_ref):
        for c in range(nch):
            rows = slice(c * BLK, (c + 1) * BLK)
            _, _, gu, _, _, _, _, _, s = _gmlp_fwd_vals(u_ref[rows, :], vb_ref[rows, :], lnv_ref, ws_ref, bst_ref)
            yb_ref[rows, :] = (gu * s).astype(BF16)

    return pl.pallas_call(
        body, name="fwd_gmlp", grid=(L // tm,),
        in_specs=[_row(tm, GM_W), _row(tm, GM_W), _full((8, GM_W)), _full((N_GROUPS, BLK, BLK)), _full((BLK, N_GROUPS))],
        out_specs=_row(tm, GM_W),
        out_shape=_sds((L, GM_W), BF16),
        compiler_params=_params(("arbitrary",)),
    )(u, vb, lnv, ws, bst)


def _k_merge(x, ya, yb, ga, gb, w_a, w_b, w_o, modv, lnv, tm):
    L = x.shape[0]

    def body(x_ref, ya_ref, yb_ref, ga_ref, gb_ref, wa_ref, wb_ref, wo_ref, mod_ref, ln_ref,
             mg_ref, mix_ref, xm_ref, h2_ref):
        a = jnp.dot(ya_ref[...], wa_ref[...], preferred_element_type=F32)
        b = jnp.dot(yb_ref[...], wb_ref[...], preferred_element_type=F32)
        merged = (_sigmoid(ga_ref[...].astype(F32)) * a + _sigmoid(gb_ref[...].astype(F32)) * b).astype(BF16)
        mg_ref[...] = merged
        mix = jnp.dot(merged, wo_ref[...], preferred_element_type=F32)
        mix_ref[...] = mix.astype(BF16)
        r1 = ALPHA * x_ref[...] + mod_ref[2:3, :] * mix
        r1hat, _ = _ln(r1)
        xm = r1hat * ln_ref[0:1, :] + ln_ref[1:2, :]
        xm_ref[...] = xm
        n2, _ = _ln(xm)
        h2_ref[...] = (n2 * (1.0 + mod_ref[4:5, :]) + mod_ref[3:4, :]).astype(BF16)

    return pl.pallas_call(
        body, name="fwd_merge", grid=(L // tm,),
        in_specs=[_row(tm, D), _row(tm, Q_W), _row(tm, GM_W), _row(tm, D), _row(tm, D),
                  _resident((Q_W, D)), _resident((GM_W, D)), _resident((D, D)), _full((8, D)), _full((8, D))],
        out_specs=[_row(tm, D)] * 4,
        out_shape=[_sds((L, D), BF16), _sds((L, D), BF16), _sds((L, D), F32), _sds((L, D), BF16)],
        compiler_params=_params(("arbitrary",)),
    )(x, ya, yb, ga, gb, w_a, w_b, w_o, modv, lnv)


FFN_CH = 1408


def _k_ffn(h2, xm, tgt, w_fi, w_fo, modv, lnv, tm):
    L = h2.shape[0]

    def body(h2_ref, xm_ref, t_ref, wi_ref, wo_ref, mod_ref, ln_ref, gate_ref, up_ref, a_ref, dr2_ref, df_ref, acc_ref):
        @pl.when(pl.program_id(0) == 0)
        def _():
            acc_ref[...] = jnp.zeros_like(acc_ref)

        h2v = h2_ref[...]
        f = jnp.zeros((tm, D), F32)
        for j in range(FFN_H // FFN_CH):
            lo = j * FFN_CH
            gate = jnp.dot(h2v, wi_ref[:, lo:lo + FFN_CH], preferred_element_type=F32)
            up = jnp.dot(h2v, wi_ref[:, FFN_H + lo:FFN_H + lo + FFN_CH], preferred_element_type=F32)
            act = (gate * _sigmoid(gate) * up).astype(BF16)
            gate_ref[:, lo:lo + FFN_CH] = gate.astype(BF16)
            up_ref[:, lo:lo + FFN_CH] = up.astype(BF16)
            a_ref[:, lo:lo + FFN_CH] = act
            f = f + jnp.dot(act, wo_ref[lo:lo + FFN_CH, :], preferred_element_type=F32)
        gate2 = mod_ref[5:6, :]
        r2 = ALPHA * xm_ref[...] + gate2 * f
        r2hat, rstd = _ln(r2)
        y = r2hat * ln_ref[2:3, :] + ln_ref[3:4, :]
        err = y - t_ref[...]
        dy = err * (1.0 / D)
        dr2 = _ln_bwd(dy * ln_ref[2:3, :], r2hat, rstd)
        dr2_ref[...] = dr2
        df_ref[...] = (gate2 * dr2).astype(BF16)
        acc_ref[0:1, :] += _colsum(dy * r2hat)
        acc_ref[1:2, :] += _colsum(dy)
        acc_ref[2:3, :] += _colsum(dr2 * f)
        acc_ref[3:4, :] += _colsum(err * err) * (0.5 / D)

    return pl.pallas_call(
        body, name="fwd_ffn", grid=(L // tm,),
        in_specs=[_row(tm, D), _row(tm, D), _row(tm, D), _resident((D, 2 * FFN_H)), _resident((FFN_H, D)),
                  _full((8, D)), _full((8, D))],
        out_specs=[_row(tm, FFN_H)] * 3 + [_row(tm, D), _row(tm, D), _full((8, D))],
        out_shape=[_sds((L, FFN_H), BF16)] * 3 + [_sds((L, D), F32), _sds((L, D), BF16), _sds((8, D), F32)],
        compiler_params=_params(("arbitrary",)),
    )(h2, xm, tgt, w_fi, w_fo, modv, lnv)


def _k_ffn_bwd(df, gate, up, xm, dr2, x, mix, w_fi, w_fo, modv, lnv, tm):
    L = df.shape[0]

    def body(df_ref, gate_ref, up_ref, xm_ref, dr2_ref, x_ref, mix_ref, wi_ref, wo_ref, mod_ref, ln_ref,
             dF_ref, dmix_ref, dxp_ref, acc_ref):
        @pl.when(pl.program_id(0) == 0)
        def _():
            acc_ref[...] = jnp.zeros_like(acc_ref)

        dfv = df_ref[...]
        dh2 = jnp.zeros((tm, D), F32)
        for j in range(FFN_H // FFN_CH):
            lo = j * FFN_CH
            da = lax.dot_general(dfv, wo_ref[lo:lo + FFN_CH, :], NT, preferred_element_type=F32)
            gate = gate_ref[:, lo:lo + FFN_CH].astype(F32)
            upv = up_ref[:, lo:lo + FFN_CH].astype(F32)
            sg = _sigmoid(gate)
            d_gate = (da * upv * (sg * (1.0 + gate * (1.0 - sg)))).astype(BF16)
            d_up = (da * (gate * sg)).astype(BF16)
            dF_ref[:, lo:lo + FFN_CH] = d_gate
            dF_ref[:, FFN_H + lo:FFN_H + lo + FFN_CH] = d_up
            dh2 = dh2 + lax.dot_general(d_gate, wi_ref[:, lo:lo + FFN_CH], NT, preferred_element_type=F32)
            dh2 = dh2 + lax.dot_general(d_up, wi_ref[:, FFN_H + lo:FFN_H + lo + FFN_CH], NT, preferred_element_type=F32)
        n2, rstd2 = _ln(xm_ref[...])
        acc_ref[0:1, :] += _colsum(dh2)
        acc_ref[1:2, :] += _colsum(dh2 * n2)
        dxm = ALPHA * dr2_ref[...] + _ln_bwd(dh2 * (1.0 + mod_ref[4:5, :]), n2, rstd2)
        mixf = mix_ref[...].astype(F32)
        gate1 = mod_ref[2:3, :]
        r1hat, rstd1 = _ln(ALPHA * x_ref[...] + gate1 * mixf)
        acc_ref[2:3, :] += _colsum(dxm * r1hat)
        acc_ref[3:4, :] += _colsum(dxm)
        dr1 = _ln_bwd(dxm * ln_ref[0:1, :], r1hat, rstd1)
        dmix_ref[...] = (gate1 * dr1).astype(BF16)
        dxp_ref[...] = ALPHA * dr1
        acc_ref[4:5, :] += _colsum(dr1 * mixf)

    return pl.pallas_call(
        body, name="bwd_ffn", grid=(L // tm,),
        in_specs=[_row(tm, D), _row(tm, FFN_H), _row(tm, FFN_H), _row(tm, D), _row(tm, D), _row(tm, D), _row(tm, D),
                  _resident((D, 2 * FFN_H)), _resident((FFN_H, D)), _full((8, D)), _full((8, D))],
        out_specs=[_row(tm, 2 * FFN_H), _row(tm, D), _row(tm, D), _full((8, D))],
        out_shape=[_sds((L, 2 * FFN_H), BF16), _sds((L, D), BF16), _sds((L, D), F32), _sds((8, D), F32)],
        compiler_params=_params(("arbitrary",)),
    )(df, gate, up, xm, dr2, x, mix, w_fi, w_fo, modv, lnv)


def _k_merge_bwd(dmix, ya, yb, ga, gb, w_a, w_b, w_o, tm):
    L = dmix.shape[0]

    def body(dmix_ref, ya_ref, yb_ref, ga_ref, gb_ref, wa_ref, wb_ref, wo_ref,
             dA_ref, dB_ref, dga_ref, dgb_ref, dya_ref, dyb_ref):
        dmg = lax.dot_general(dmix_ref[...], wo_ref[...], NT, preferred_element_type=F32)
        a = jnp.dot(ya_ref[...], wa_ref[...], preferred_element_type=F32)
        sa = _sigmoid(ga_ref[...].astype(F32))
        dA = (dmg * sa).astype(BF16)
        dA_ref[...] = dA
        dga_ref[...] = (dmg * a * (sa * (1.0 - sa))).astype(BF16)
        dya_ref[...] = lax.dot_general(dA, wa_ref[...], NT, preferred_element_type=F32).astype(BF16)
        b = jnp.dot(yb_ref[...], wb_ref[...], preferred_element_type=F32)
        sb = _sigmoid(gb_ref[...].astype(F32))
        dB = (dmg * sb).astype(BF16)
        dB_ref[...] = dB
        dgb_ref[...] = (dmg * b * (sb * (1.0 - sb))).astype(BF16)
        dyb_ref[...] = lax.dot_general(dB, wb_ref[...], NT, preferred_element_type=F32).astype(BF16)

    return pl.pallas_call(
        body, name="bwd_merge", grid=(L // tm,),
        in_specs=[_row(tm, D), _row(tm, Q_W), _row(tm, GM_W), _row(tm, D), _row(tm, D),
                  _resident((Q_W, D)), _resident((GM_W, D)), _resident((D, D))],
        out_specs=[_row(tm, D)] * 4 + [_row(tm, Q_W), _row(tm, GM_W)],
        out_shape=[_sds((L, D), BF16)] * 4 + [_sds((L, Q_W), BF16), _sds((L, GM_W), BF16)],
        compiler_params=_params(("arbitrary",)),
    )(dmix, ya, yb, ga, gb, w_a, w_b, w_o)


def _k_gmlp_bwd(u, vb, dyb, lnv, ws, wst, bst):
    L = u.shape[0]
    nch = min(GMLP_CHUNKS, L // BLK)
    tm = nch * BLK

    def body(u_ref, vb_ref, dyb_ref, lnv_ref, ws_ref, wst_ref, bst_ref, du_ref, dvb_ref, gws_ref, gbst_ref, gln_ref):
        @pl.when(pl.program_id(0) == 0)
        def _():
            gws_ref[...] = jnp.zeros_like(gws_ref)
            gbst_ref[...] = jnp.zeros_like(gbst_ref)
            gln_ref[...] = jnp.zeros_like(gln_ref)

        gws = [None] * N_GROUPS
        gbs = [None] * N_GROUPS
        gln_g = gln_b = None
        for c in range(nch):
            rows = slice(c * BLK, (c + 1) * BLK)
            uf, vf, gu, tu, tv, vhat, rstd, vn, s = _gmlp_fwd_vals(u_ref[rows, :], vb_ref[rows, :], lnv_ref, ws_ref, bst_ref)
            dyb_f = dyb_ref[rows, :].astype(F32)
            du_ref[rows, :] = (dyb_f * s * _gelu_grad(uf, tu)).astype(BF16)
            ds = dyb_f * gu
            ds_b = ds.astype(BF16)
            dvn_parts = []
            for g in range(N_GROUPS):
                sl = slice(g * GROUP_DIM, (g + 1) * GROUP_DIM)
                gw = lax.dot_general(ds_b[:, sl], vn[:, sl], NT, preferred_element_type=F32)
                gb = jnp.sum(ds[:, sl], axis=1, keepdims=True)
                gws[g] = gw if c == 0 else gws[g] + gw
                gbs[g] = gb if c == 0 else gbs[g] + gb
                dvn_parts.append(jnp.dot(wst_ref[g], ds_b[:, sl], preferred_element_type=F32))
            dvn = jnp.concatenate(dvn_parts, axis=1)
            gg, gb_ = _colsum(dvn * vhat), _colsum(dvn)
            gln_g = gg if c == 0 else gln_g + gg
            gln_b = gb_ if c == 0 else gln_b + gb_
            dgv = _ln_bwd(dvn * lnv_ref[0:1, :], vhat, rstd)
            dvb_ref[rows, :] = (dgv * _gelu_grad(vf, tv)).astype(BF16)
        for g in range(N_GROUPS):
            gws_ref[g] += gws[g]
            gbst_ref[:, g:g + 1] += gbs[g]
        gln_ref[0:1, :] += gln_g
        gln_ref[1:2, :] += gln_b

    return pl.pallas_call(
        body, name="bwd_gmlp", grid=(L // tm,),
        in_specs=[_row(tm, GM_W)] * 3 + [_full((8, GM_W)), _full((N_GROUPS, BLK, BLK)), _full((N_GROUPS, BLK, BLK)),
                                         _full((BLK, N_GROUPS))],
        out_specs=[_row(tm, GM_W), _row(tm, GM_W), _full((N_GROUPS, BLK, BLK)), _full((BLK, N_GROUPS)), _full((8, GM_W))],
        out_shape=[_sds((L, GM_W), BF16), _sds((L, GM_W), BF16), _sds((N_GROUPS, BLK, BLK), F32),
                   _sds((BLK, N_GROUPS), F32), _sds((8, GM_W), F32)],
        compiler_params=_params(("arbitrary",)),
    )(u, vb, dyb, lnv, ws, wst, bst)


def _k_attn_bwd(sink, q, k, v, kc, vc, dya, lse, cos, sin, bias, comm=None):
    L = q.shape[0]
    C = kc.shape[0]
    nb = L // BLK
    NK = C + 3 * BLK

    def body(sink_ref, q_ref, kp_ref, kn_ref, kx_ref, vp_ref, vn_ref, vx_ref, kc_ref, vc_ref, do_ref, lse_ref,
             cq_ref, sq_ref, ck_ref, sk_ref, bias_ref,
             dq_ref, dk_ref, dv_ref, dkc_ref, dvc_ref, dsink_ref,
             dq_scr, ck_scr, cv_scr, kp_acc, kc_acc, vp_acc, vc_acc):
        n = pl.program_id(0)

        @pl.when(n == 0)
        def _():
            for r in (kp_acc, kc_acc, vp_acc, vc_acc, dkc_ref, dvc_ref, dsink_ref):
                r[...] = jnp.zeros_like(r)

        @pl.when(n < nb)
        def _():
            band = bias_ref[0]
            for hk in range(N_KV_HEADS):
                sl = slice(hk * HEAD_DIM, (hk + 1) * HEAD_DIM)
                kcat = jnp.concatenate([kc_ref[:, sl], kp_ref[:, sl], kn_ref[:, sl], kx_ref[:, sl]], axis=0)
                vcat = jnp.concatenate([vc_ref[:, sl], vp_ref[:, sl], vn_ref[:, sl], vx_ref[:, sl]], axis=0)
                heads = [hk * GQA_GROUP + g for g in range(GQA_GROUP)]
                qg = jnp.concatenate([q_ref[:, h * HEAD_DIM:(h + 1) * HEAD_DIM] for h in heads], axis=0)
                dog = jnp.concatenate([do_ref[:, h * HEAD_DIM:(h + 1) * HEAD_DIM] for h in heads], axis=0)
                lse_c = jnp.concatenate([lse_ref[:, h:h + 1] for h in heads], axis=0)
                s = _masked(lax.dot_general(qg, kcat, NT, preferred_element_type=F32), band, C)
                p = jnp.exp(s - lse_c)
                dp = lax.dot_general(dog, vcat, NT, preferred_element_type=F32)
                delta = jnp.sum(p * dp, axis=1, keepdims=True)
                ds = (p * (dp - delta)).astype(BF16)
                dqs = jnp.dot(ds, kcat, preferred_element_type=F32)
                ck_scr[:, sl] = lax.dot_general(ds, qg, TN, preferred_element_type=F32)
                cv_scr[:, sl] = lax.dot_general(p.astype(BF16), dog, TN, preferred_element_type=F32)
                p_sink = jnp.exp(_sink_col(sink_ref, hk) - lse_c) * delta
                for g, h in enumerate(heads):
                    dq_scr[:, h * HEAD_DIM:(h + 1) * HEAD_DIM] = dqs[g * BLK:(g + 1) * BLK, :]
                    tot = jnp.sum(p_sink[g * BLK:(g + 1) * BLK, :], axis=0, keepdims=True)
                    dsink_ref[h:h + 1, :] -= jnp.broadcast_to(tot, (1, 128))
            cq, sq = cq_ref[...], sq_ref[...]
            for i in range(4):
                dq_ref[:, i * 128:(i + 1) * 128] = _unrope(dq_scr[:, i * 128:(i + 1) * 128] * Q_SCALE, cq, sq).astype(BF16)
            dkc_ref[...] += ck_scr[0:C, :]
            dvc_ref[...] += cv_scr[0:C, :]

        @pl.when(n >= nb)
        def _():
            ck_scr[...] = jnp.zeros_like(ck_scr)
            cv_scr[...] = jnp.zeros_like(cv_scr)

        dk_ref[...] = _unrope(kp_acc[...] + ck_scr[C:C + BLK, :], ck_ref[...], sk_ref[...]).astype(BF16)
        dv_ref[...] = (vp_acc[...] + cv_scr[C:C + BLK, :]).astype(BF16)
        kp_acc[...] = kc_acc[...] + ck_scr[C + BLK:C + 2 * BLK, :]
        vp_acc[...] = vc_acc[...] + cv_scr[C + BLK:C + 2 * BLK, :]
        kc_acc[...] = ck_scr[C + 2 * BLK:C + 3 * BLK, :]
        vc_acc[...] = cv_scr[C + 2 * BLK:C + 3 * BLK, :]

    kv3 = _kv_specs(nb)
    cur = lambda w: pl.BlockSpec((BLK, w), lambda n: (jnp.minimum(n, nb - 1), 0))
    late = lambda w: pl.BlockSpec((BLK, w), lambda n: (jnp.maximum(n - 1, 0), 0))
    return _call(
        body, name="bwd_attn", grid=(nb + 1,),
        in_specs=[pl.BlockSpec(memory_space=pltpu.SMEM), cur(Q_W)] + kv3 + kv3
                 + [_full((C, KV_W)), _full((C, KV_W)), cur(Q_W), cur(N_Q_HEADS), cur(128), cur(128), late(128), late(128),
                    _bias_spec(nb)],
        out_specs=[cur(Q_W), late(KV_W), late(KV_W), _full((C, KV_W)), _full((C, KV_W)), _full((8, 128))],
        out_shape=[_sds((L, Q_W), BF16), _sds((L, KV_W), BF16), _sds((L, KV_W), BF16),
                   _sds((C, KV_W), F32), _sds((C, KV_W), F32), _sds((8, 128), F32)],
        scratch=[pltpu.VMEM((BLK, Q_W), F32), pltpu.VMEM((NK, KV_W), F32), pltpu.VMEM((NK, KV_W), F32)]
                + [pltpu.VMEM((BLK, KV_W), F32)] * 4,
        args=(sink, q, k, k, k, v, v, v, kc, vc, dya, lse, cos, sin, cos, sin, bias), comm=comm)


def _k_ctx_bwd(ctx, modc, hc, dkc, dvc, w_kv, gw_in):
    C = ctx.shape[0]
    kv_block = pl.BlockSpec((D, 2 * KV_W), lambda i: (0, O_K // (2 * KV_W)))

    def body(c_ref, mod_ref, hc_ref, dkc_ref, dvc_ref, w_ref, gin_ref, gw_ref, dmod_ref):
        dkv = jnp.concatenate([dkc_ref[...], dvc_ref[...]], axis=1).astype(BF16)
        gw_ref[...] = gin_ref[...] + lax.dot_general(hc_ref[...], dkv, TN, preferred_element_type=F32)
        dhc = lax.dot_general(dkv, w_ref[...], NT, preferred_element_type=F32)
        n, _ = _ln(c_ref[...])
        dmod_ref[...] = jnp.zeros_like(dmod_ref)
        dmod_ref[0:1, :] = _colsum(dhc)
        dmod_ref[1:2, :] = _colsum(dhc * n)

    return pl.pallas_call(
        body, name="bwd_ctx", grid=(1,),
        in_specs=[_full((C, D)), _full((8, D)), _full((C, D)), _full((C, KV_W)), _full((C, KV_W)), _full((D, 2 * KV_W)),
                  kv_block],
        out_specs=[kv_block, _full((8, D))],
        out_shape=[_sds((D, IN_W), F32), _sds((8, D), F32)],
        input_output_aliases={6: 0},
        compiler_params=_params(("arbitrary",)),
    )(ctx, modc, hc, dkc, dvc, w_kv, gw_in)


def _k_in_bwd(dq, dk, dv, du, dvb, dga, dgb, x, dxp, w_in, modv, tm, comm=None):
    L = x.shape[0]
    parts = [(O_Q, Q_W), (O_K, KV_W), (O_V, KV_W), (O_U, GM_W), (O_VB, GM_W), (O_GA, D), (O_GB, D)]

    def body(dq_ref, dk_ref, dv_ref, du_ref, dvb_ref, dga_ref, dgb_ref, x_ref, dxp_ref, w_ref, mod_ref,
             dP_ref, gx_ref, acc_ref):
        @pl.when(pl.program_id(0) == 0)
        def _():
            acc_ref[...] = jnp.zeros_like(acc_ref)

        for (lo, width), r in zip(parts, (dq_ref, dk_ref, dv_ref, du_ref, dvb_ref, dga_ref, dgb_ref)):
            dP_ref[:, lo:lo + width] = r[...]
        dh = lax.dot_general(dP_ref[...], w_ref[...], NT, preferred_element_type=F32)
        n1, rstd1 = _ln(x_ref[...])
        acc_ref[0:1, :] += _colsum(dh)
        acc_ref[1:2, :] += _colsum(dh * n1)
        gx_ref[...] = dxp_ref[...] + _ln_bwd(dh * (1.0 + mod_ref[1:2, :]), n1, rstd1)

    return _call(
        body, name="bwd_in", grid=(L // tm,),
        in_specs=[_row(tm, w) for _, w in parts] + [_row(tm, D), _row(tm, D), _resident((D, IN_W)), _full((8, D))],
        out_specs=[_row(tm, IN_W), _row(tm, D), _full((8, D))],
        out_shape=[_sds((L, IN_W), BF16), _sds((L, D), F32), _sds((8, D), F32)],
        args=(dq, dk, dv, du, dvb, dga, dgb, x, dxp, w_in, modv), comm=comm)


def _wgrad(a, b, name, tn, tt, comm=None):
    T, K = a.shape
    N = b.shape[1]

    def body(a_ref, b_ref, o_ref):
        @pl.when(pl.program_id(1) == 0)
        def _():
            o_ref[...] = jnp.zeros_like(o_ref)

        o_ref[...] += lax.dot_general(a_ref[...], b_ref[...], TN, preferred_element_type=F32)

    (out,), got = _call(
        body, name=name, grid=(N // tn, T // tt),
        in_specs=[pl.BlockSpec((tt, K), lambda j, t: (t, 0)), pl.BlockSpec((tt, tn), lambda j, t: (t, j))],
        out_specs=[pl.BlockSpec((K, tn), lambda j, t: (0, j))],
        out_shape=[_sds((K, N), F32)],
        args=(a, b), comm=comm)
    return (out, got) if comm is not None else out


def _adamw_reduce(parts, w, m, v, name, tr):
    R, C = w.shape

    def body(p_ref, w_ref, m_ref, v_ref, g_ref, d_ref, m2_ref, v2_ref):
        g = p_ref[0].astype(F32)
        for i in range(1, N_DEV):
            g = g + p_ref[i].astype(F32)
        delta, m2, v2 = _adamw(w_ref[...], g, m_ref[...], v_ref[...])
        g_ref[...] = g
        d_ref[...] = delta
        m2_ref[...] = m2
        v2_ref[...] = v2

    spec = _row(tr, C)
    return pl.pallas_call(
        body, name=name, grid=(R // tr,),
        in_specs=[pl.BlockSpec((N_DEV, tr, C), lambda i: (0, i, 0)), spec, spec, spec],
        out_specs=[spec] * 4,
        out_shape=[_sds((R, C), F32)] * 4,
        compiler_params=_params(("arbitrary",)),
    )(parts, w, m, v)


def _small_reduce(gath, gath_ws):
    def body(g_ref, w_ref, out_ref):
        tot = g_ref[0]
        wsum = w_ref[0]
        for i in range(1, N_DEV):
            tot = tot + g_ref[i]
            wsum = wsum + w_ref[i]
        out_ref[0:16, :] = tot
        out_ref[0:2, :] = tot[0:2, :] + tot[6:8, :]
        out_ref[15:16, :] = jnp.broadcast_to(jnp.sum(tot[15:16, :], axis=1, keepdims=True), (1, D))
        out_ref[16:SMALL_ROWS, :] = wsum

    return pl.pallas_call(
        body, name="small_reduce", grid=(1,),
        in_specs=[_full((N_DEV, 16, D)), _full((N_DEV, SMALL_ROWS - 16, D))],
        out_specs=_full((SMALL_ROWS, D)),
        out_shape=_sds((SMALL_ROWS, D), F32),
        compiler_params=_params(("arbitrary",)),
    )(gath, gath_ws)


def _small_adamw(w, g, m, v, name):
    shape = w.shape

    def body(w_ref, g_ref, m_ref, v_ref, d_ref, m2_ref, v2_ref):
        delta, m2, v2 = _adamw(w_ref[...], g_ref[...], m_ref[...], v_ref[...])
        d_ref[...] = delta
        m2_ref[...] = m2
        v2_ref[...] = v2

    return pl.pallas_call(
        body, name=name, grid=(1,),
        in_specs=[_full(shape)] * 4, out_specs=[_full(shape)] * 3,
        out_shape=[_sds(shape, F32)] * 3,
        compiler_params=_params(("arbitrary",)),
    )(w, g, m, v)


def _cctx_finish(gath, c_ctx, m, v):
    def body(g_ref, c_ref, m_ref, v_ref, gr_ref, d_ref, m2_ref, v2_ref):
        ds = g_ref[0]
        for i in range(1, N_DEV):
            ds = ds + g_ref[i]
        c = c_ref[...]
        sg = _sigmoid(c)
        g = ds * (sg * (1.0 + c * (1.0 - sg)))
        delta, m2, v2 = _adamw(c, g, m_ref[...], v_ref[...])
        gr_ref[...] = g
        d_ref[...] = delta
        m2_ref[...] = m2
        v2_ref[...] = v2

    return pl.pallas_call(
        body, name="cctx_finish", grid=(1,),
        in_specs=[_full((N_DEV, 8, D))] + [_full((8, D))] * 3, out_specs=[_full((8, D))] * 4,
        out_shape=[_sds((8, D), F32)] * 4,
        compiler_params=_params(("arbitrary",)),
    )(gath, c_ctx, m, v)


def _pad_rows(a, rows):
    return jnp.concatenate([a, jnp.zeros((rows - a.shape[0], a.shape[1]), a.dtype)], axis=0)


def _pack_small(b_ada, ln1_g, ln1_b, ln2_g, ln2_b, gm_g, gm_b, b_sp, sink, w_sp):
    rows = [b_ada.reshape(6, D), jnp.zeros((2, D), F32), ln1_g.reshape(1, D), ln1_b.reshape(1, D), ln2_g.reshape(1, D),
            ln2_b.reshape(1, D), jnp.concatenate([gm_g.reshape(1, GM_W), gm_b.reshape(1, GM_W)], axis=1),
            b_sp.reshape(1, D), _pad_rows(sink.reshape(1, N_Q_HEADS).T, D).T.reshape(1, D), jnp.zeros((1, D), F32),
            w_sp.reshape(N_GROUPS * BLK * BLK // D, D)]
    return jnp.concatenate(rows, axis=0)


def _unpack_small(p):
    return dict(b_ada=p[0:6].reshape(1, 6 * D), ln1_g=p[8:9], ln1_b=p[9:10], ln2_g=p[10:11], ln2_b=p[11:12],
                gmlp_ln_g=p[12:13, :GM_W], gmlp_ln_b=p[12:13, GM_W:], b_spatial=p[13:14].reshape(1, N_GROUPS, BLK),
                attn_sink=p[14:15, :N_Q_HEADS], w_spatial=p[16:].reshape(1, N_GROUPS, BLK, BLK))


def kernel(x, c, ctx, c_ctx, w_ada, b_ada, w_in, attn_sink, gmlp_ln_g, gmlp_ln_b, w_spatial, b_spatial, w_branch_a, w_branch_b, w_out, ln1_g, ln1_b, w_ffn_in, w_ffn_out, ln2_g, ln2_b, loss_target, m_c_ctx, m_w_ada, m_b_ada, m_w_in, m_attn_sink, m_gmlp_ln_g, m_gmlp_ln_b, m_w_spatial, m_b_spatial, m_w_branch_a, m_w_branch_b, m_w_out, m_ln1_g, m_ln1_b, m_w_ffn_in, m_w_ffn_out, m_ln2_g, m_ln2_b, v_c_ctx, v_w_ada, v_b_ada, v_w_in, v_attn_sink, v_gmlp_ln_g, v_gmlp_ln_b, v_w_spatial, v_b_spatial, v_w_branch_a, v_w_branch_b, v_w_out, v_ln1_g, v_ln1_b, v_w_ffn_in, v_w_ffn_out, v_ln2_g, v_ln2_b):
    L = x.shape[1]
    me = 4 * lax.axis_index("x") + 2 * lax.axis_index("y") + lax.axis_index("c")
    x2, tgt, ctx2 = x[0], loss_target[0], ctx[0]
    tm_in = min(512, L)
    tm = min(256, L)
    tt = min(1024, L)

    big = dict(w_in=w_in[0], w_branch_a=w_branch_a[0], w_branch_b=w_branch_b[0], w_out=w_out[0],
               w_ffn_in=w_ffn_in[0], w_ffn_out=w_ffn_out[0])
    col_sharded = ("w_in", "w_branch_a", "w_branch_b", "w_ffn_in")
    shard_bf = {k: a.astype(BF16) for k, a in big.items()}

    def assemble(kname, g):
        if kname in col_sharded:
            return g.transpose(1, 0, 2).reshape(g.shape[1], N_DEV * g.shape[2])
        return g.reshape(N_DEV * g.shape[1], g.shape[2])

    def to_blocks(kname, g):
        g = g.astype(BF16)
        if kname in col_sharded:
            return g.reshape(g.shape[0], N_DEV, g.shape[1] // N_DEV).transpose(1, 0, 2)
        return g.reshape(N_DEV, g.shape[0] // N_DEV, g.shape[1])

    full = {}
    full["w_in"] = assemble("w_in", _comm_only(_Comm(gather=[shard_bf["w_in"]]), "gather_w_in")[0])
    c_all = _ag_small(_pad_rows(c, 8), "gather_c")[:, 0, :]
    s_in = jnp.concatenate([c_all, c_ctx[None, :], jnp.zeros((7, D), F32)], axis=0)
    n_ada = w_ada.shape[2]
    b_my = lax.dynamic_slice(b_ada, (0, me * n_ada), (1, n_ada))
    act, mod_my = _ada_fwd(s_in, w_ada[0], b_my)
    mod_all = _ag_small(mod_my, "gather_mod").transpose(1, 0, 2).reshape(16, 6 * D)
    modv = _pad_rows(lax.dynamic_slice(mod_all, (me, 0), (1, 6 * D)).reshape(6, D), 8)
    modc = _pad_rows(mod_all[8].reshape(6, D), 8)

    lnv = _pad_rows(jnp.concatenate([ln1_g, ln1_b, ln2_g, ln2_b], axis=0), 8)
    gm_lnv = _pad_rows(jnp.concatenate([gmlp_ln_g, gmlp_ln_b], axis=0), 8)
    ws_b = w_spatial[0].astype(BF16)
    wst_b = ws_b.transpose(0, 2, 1)
    bst = b_spatial[0].T
    sink = attn_sink[0]
    cos, sin = _rope_tables(L)
    bias = _attn_bias()
    w_kv = full["w_in"][:, O_K:O_K + 2 * KV_W]

    (h, q, k, v, u, vb, ga, gb), got = _k_in(
        x2, modv, full["w_in"], cos, sin, tm_in,
        comm=_Comm(gather=[shard_bf["w_branch_a"], shard_bf["w_branch_b"], shard_bf["w_out"]]))
    for kname, g in zip(("w_branch_a", "w_branch_b", "w_out"), got):
        full[kname] = assemble(kname, g)
    hc, kc, vc = _k_ctx(ctx2, modc, w_kv)
    (ya, lse), got = _k_attn(sink, q, k, v, kc, vc, bias, comm=_Comm(gather=[shard_bf["w_ffn_in"], shard_bf["w_ffn_out"]]))
    for kname, g in zip(("w_ffn_in", "w_ffn_out"), got):
        full[kname] = assemble(kname, g)
    yb = _k_gmlp(u, vb, gm_lnv, ws_b, bst)
    merged, mix, xm, h2 = _k_merge(x2, ya, yb, ga, gb, full["w_branch_a"], full["w_branch_b"], full["w_out"], modv, lnv, tm_in)
    gate, up, act_f, dr2, df, acc_f = _k_ffn(h2, xm, tgt, full["w_ffn_in"], full["w_ffn_out"], modv, lnv, tm_in)

    dF, dmix, dxp, acc_b = _k_ffn_bwd(df, gate, up, xm, dr2, x2, mix, full["w_ffn_in"], full["w_ffn_out"], modv, lnv, tm)
    blk_fi = to_blocks("w_ffn_in", _wgrad(h2, dF, "wgrad_ffn_in", 1408, tt))
    blk_fo = to_blocks("w_ffn_out", _wgrad(act_f, df, "wgrad_ffn_out", 512, tt))
    dA, dB, dga, dgb, dya, dyb = _k_merge_bwd(dmix, ya, yb, ga, gb, full["w_branch_a"], full["w_branch_b"], full["w_out"], tm_in)
    du, dvb, g_ws, g_bst, g_gln = _k_gmlp_bwd(u, vb, dyb, gm_lnv, ws_b, wst_b, bst)
    (dq, dk, dv, dkc, dvc, g_sink), (rcv_fi, rcv_fo, gath_ws) = _k_attn_bwd(
        sink, q, k, v, kc, vc, dya, lse, cos, sin, bias,
        comm=_Comm(scatter=[blk_fi, blk_fo], spread=[g_ws.reshape(SMALL_ROWS - 16, D)]))
    blk_a = to_blocks("w_branch_a", _wgrad(ya, dA, "wgrad_a", D, tt))
    blk_b = to_blocks("w_branch_b", _wgrad(yb, dB, "wgrad_b", D, tt))
    blk_o = to_blocks("w_out", _wgrad(merged, dmix, "wgrad_out", D, tt))
    (dP, grad_x, acc_i), _ = _k_in_bwd(dq, dk, dv, du, dvb, dga, dgb, x2, dxp, full["w_in"], modv, tm_in)
    gw_in, (rcv_a, rcv_b, rcv_o) = _wgrad(h, dP, "wgrad_in", 1280, tt, comm=_Comm(scatter=[blk_a, blk_b, blk_o]))
    gw_in, dmodc = _k_ctx_bwd(ctx2, modc, hc, dkc, dvc, w_kv, gw_in)

    dmod_x = jnp.concatenate([acc_i[0:2], acc_b[4:5], acc_b[0:2], acc_f[2:3]], axis=0)
    small = jnp.concatenate([
        dmod_x, dmodc[0:2], acc_b[2:4], acc_f[0:2],
        jnp.concatenate([g_gln[0:1], g_gln[1:2]], axis=1), g_bst.T.reshape(1, D),
        _pad_rows(g_sink[:, 0:1], D).T, acc_f[3:4]], axis=0)
    rcv_in, gath = _comm_only(_Comm(scatter=[to_blocks("w_in", gw_in)], spread=[small]), "exchange_last")
    received = dict(w_in=rcv_in, w_branch_a=rcv_a, w_branch_b=rcv_b, w_out=rcv_o, w_ffn_in=rcv_fi, w_ffn_out=rcv_fo)
    moments = dict(w_in=(m_w_in, v_w_in), w_branch_a=(m_w_branch_a, v_w_branch_a), w_branch_b=(m_w_branch_b, v_w_branch_b),
                   w_out=(m_w_out, v_w_out), w_ffn_in=(m_w_ffn_in, v_w_ffn_in), w_ffn_out=(m_w_ffn_out, v_w_ffn_out))
    names = list(big)
    res = {}
    for kname in names:
        mm, vv = moments[kname]
        R = big[kname].shape[0]
        res[kname] = _adamw_reduce(received[kname], big[kname], mm[0], vv[0], "adamw_" + kname, 256 if R % 256 == 0 else R // 2)

    tot = _small_reduce(gath, gath_ws)
    g_small = _unpack_small(tot)
    loss = tot[15, 0]

    p_w = _pack_small(b_ada, ln1_g, ln1_b, ln2_g, ln2_b, gmlp_ln_g, gmlp_ln_b, b_spatial, attn_sink, w_spatial)
    p_m = _pack_small(m_b_ada, m_ln1_g, m_ln1_b, m_ln2_g, m_ln2_b, m_gmlp_ln_g, m_gmlp_ln_b, m_b_spatial, m_attn_sink, m_w_spatial)
    p_v = _pack_small(v_b_ada, v_ln1_g, v_ln1_b, v_ln2_g, v_ln2_b, v_gmlp_ln_g, v_gmlp_ln_b, v_b_spatial, v_attn_sink, v_w_spatial)
    s_d, s_m, s_v = [_unpack_small(t) for t in _small_adamw(p_w, tot, p_m, p_v, "adamw_small")]

    dmod_rows = jnp.concatenate([gath[:, 0:6, :].reshape(N_DEV, 6 * D),
                                 jnp.concatenate([tot[6:8].reshape(1, 2 * D), jnp.zeros((1, 4 * D), F32)], axis=1),
                                 jnp.zeros((7, 6 * D), F32)], axis=0)
    dmod_my = lax.dynamic_slice(dmod_rows, (0, me * n_ada), (16, n_ada))
    g_wada, d_wada, m2_wada, v2_wada, pc = _ada_bwd(act, dmod_my, w_ada[0], m_w_ada[0], v_w_ada[0])
    pc_all = _ag_small(pc, "gather_cctx")
    cc8 = lambda a: _pad_rows(a.reshape(1, D), 8)
    g_cc, d_cc, m2_cc, v2_cc = _cctx_finish(pc_all, cc8(c_ctx), cc8(m_c_ctx), cc8(v_c_ctx))

    order = ["c_ctx", "w_ada", "b_ada", "w_in", "attn_sink", "gmlp_ln_g", "gmlp_ln_b", "w_spatial", "b_spatial",
             "w_branch_a", "w_branch_b", "w_out", "ln1_g", "ln1_b", "w_ffn_in", "w_ffn_out", "ln2_g", "ln2_b"]
    grads, deltas, new_m, new_v = {}, {}, {}, {}
    grads["c_ctx"], deltas["c_ctx"], new_m["c_ctx"], new_v["c_ctx"] = g_cc[0], d_cc[0], m2_cc[0], v2_cc[0]
    grads["w_ada"], deltas["w_ada"], new_m["w_ada"], new_v["w_ada"] = g_wada[None], d_wada[None], m2_wada[None], v2_wada[None]
    for kname in names:
        g, d, m2, v2 = res[kname]
        grads[kname], deltas[kname], new_m[kname], new_v[kname] = g[None], d[None], m2[None], v2[None]
    for kname in ("b_ada", "attn_sink", "gmlp_ln_g", "gmlp_ln_b", "w_spatial", "b_spatial", "ln1_g", "ln1_b", "ln2_g", "ln2_b"):
        grads[kname], deltas[kname], new_m[kname], new_v[kname] = g_small[kname], s_d[kname], s_m[kname], s_v[kname]
    return (loss, grad_x[None], *[grads[n] for n in order], *[deltas[n] for n in order],
            *[new_m[n] for n in order], *[new_v[n] for n in order])
```

```python
import functools
import math

import jax
import jax.numpy as jnp
import numpy as np
from jax import lax
from jax.experimental import pallas as pl
from jax.experimental.pallas import tpu as pltpu

F32 = jnp.float32
BF16 = jnp.bfloat16
MESH = pl.DeviceIdType.MESH

N_DEV = 8
D = 1024
HEAD_DIM = 64
N_Q_HEADS = 8
N_KV_HEADS = 2
GQA_GROUP = 4
BLK = 128
Q_W = 512
KV_W = 128
GM_W = 512
N_GROUPS = 8
GROUP_DIM = 64
FFN_H = 2816
IN_W = 3840
O_Q, O_K, O_V, O_U, O_VB, O_GA, O_GB = 0, 512, 640, 768, 1280, 1792, 2816
LN_EPS = 1e-5
NEG_INF = -1e30
ALPHA = 2.0 ** 0.25
ROPE_BASE = 10000.0
ROPE_PAIRS = 16
Q_SCALE = HEAD_DIM ** -0.5
GELU_K0 = math.sqrt(2.0 / math.pi)
GELU_K1 = 0.044715

ADAM_LR = 0.001
ADAM_B1 = 0.9
ADAM_B2 = 0.999
ADAM_EPS = 1e-08
ADAM_WD = 0.01
ADAM_STEP = 10

VMEM_LIMIT = 56 * 1024 * 1024
SMALL_ROWS = 144
NT = (((1,), (1,)), ((), ()))
TN = (((0,), (0,)), ((), ()))


def _params(sem=None):
    return pltpu.CompilerParams(dimension_semantics=sem, vmem_limit_bytes=VMEM_LIMIT)


def _row(tm, w):
    return pl.BlockSpec((tm, w), lambda i: (i, 0))


def _full(shape):
    nd = len(shape)
    return pl.BlockSpec(shape, lambda i: (0,) * nd)


def _resident(shape):
    nd = len(shape)
    return pl.BlockSpec(shape, lambda i: (0,) * nd, pipeline_mode=pl.Buffered(1))


def _sds(shape, dt):
    return jax.ShapeDtypeStruct(shape, dt)


def _ln(xf):
    mu = jnp.mean(xf, axis=-1, keepdims=True)
    xc = xf - mu
    var = jnp.mean(xc * xc, axis=-1, keepdims=True)
    rstd = lax.rsqrt(var + LN_EPS)
    return xc * rstd, rstd


def _ln_bwd(dn, n, rstd):
    m1 = jnp.mean(dn, axis=-1, keepdims=True)
    m2 = jnp.mean(dn * n, axis=-1, keepdims=True)
    return rstd * (dn - m1 - n * m2)


def _colsum(t):
    return jnp.sum(t, axis=0, keepdims=True)


def _sigmoid(x):
    return 0.5 * jnp.tanh(0.5 * x) + 0.5


def _gelu(x):
    t = jnp.tanh(GELU_K0 * (x + GELU_K1 * (x * x * x)))
    return x * (0.5 * (1.0 + t)), t


def _gelu_grad(x, t):
    return 0.5 * (1.0 + t) + 0.5 * x * (1.0 - t * t) * (GELU_K0 * (1.0 + 3.0 * GELU_K1 * x * x))


def _swap16(t):
    lane = lax.broadcasted_iota(jnp.int32, t.shape, 1)
    return jnp.where((lane & 16) == 0, pltpu.roll(t, 112, 1), pltpu.roll(t, 16, 1))


def _rope(t, cos, sin):
    return t * cos + _swap16(t) * sin


def _unrope(t, cos, sin):
    return t * cos - _swap16(t) * sin


def _adamw(w, g, m, v):
    m2 = ADAM_B1 * m + (1.0 - ADAM_B1) * g
    v2 = ADAM_B2 * v + (1.0 - ADAM_B2) * (g * g)
    m_hat = m2 / (1.0 - ADAM_B1 ** ADAM_STEP)
    v_hat = v2 / (1.0 - ADAM_B2 ** ADAM_STEP)
    delta = -ADAM_LR * (m_hat / (jnp.sqrt(v_hat) + ADAM_EPS) + ADAM_WD * w)
    return delta, m2, v2


def _rope_tables(L):
    inv = (np.float32(ROPE_BASE) ** (-np.arange(ROPE_PAIRS, dtype=np.float32) / np.float32(ROPE_PAIRS))).astype(np.float32)
    t = np.arange(L, dtype=np.int32)
    rows = (t // 64).astype(np.float32)[:, None] * inv
    cols = (t % 64).astype(np.float32)[:, None] * inv
    cr, sr, cc, sc = np.cos(rows), np.sin(rows), np.cos(cols), np.sin(cols)
    cos = np.concatenate([cr, cr, cc, cc], axis=1)
    sin = np.concatenate([-sr, sr, -sc, sc], axis=1)
    return jnp.asarray(np.tile(cos, (1, 2)), F32), jnp.asarray(np.tile(sin, (1, 2)), F32)


def _me():
    return lax.axis_index("x"), lax.axis_index("y"), lax.axis_index("c")


def _peer(mx, my, mc, k):
    return (mx ^ ((k >> 2) & 1), my ^ ((k >> 1) & 1), mc ^ (k & 1))


def _ag_small(x, name):
    R, C = x.shape

    def body(x_ref, out_ref, send_sems, recv_sems):
        mx, my, mc = _me()
        me = 4 * mx + 2 * my + mc
        out_ref[pl.ds(me, 1)] = x_ref[...][None]
        sends = []
        for k in range(1, N_DEV):
            cp = pltpu.make_async_remote_copy(
                src_ref=x_ref, dst_ref=out_ref.at[me], send_sem=send_sems.at[k - 1], recv_sem=recv_sems.at[k - 1],
                device_id=_peer(mx, my, mc, k), device_id_type=MESH)
            cp.start()
            sends.append(cp)
        for k in range(1, N_DEV):
            pltpu.make_async_remote_copy(
                src_ref=x_ref, dst_ref=out_ref.at[me ^ k], send_sem=send_sems.at[k - 1], recv_sem=recv_sems.at[k - 1],
                device_id=(mx, my, mc), device_id_type=MESH).wait_recv()
        for cp in sends:
            cp.wait_send()

    return pl.pallas_call(
        body, name=name,
        out_shape=_sds((N_DEV, R, C), x.dtype),
        in_specs=[pl.BlockSpec(memory_space=pltpu.VMEM)],
        out_specs=pl.BlockSpec(memory_space=pltpu.VMEM),
        scratch_shapes=[pltpu.SemaphoreType.DMA((N_DEV - 1,)), pltpu.SemaphoreType.DMA((N_DEV - 1,))],
        compiler_params=pltpu.CompilerParams(vmem_limit_bytes=VMEM_LIMIT),
    )(x)


class _Comm:
    def __init__(self, gather=(), scatter=(), spread=()):
        self.kinds = ["gather"] * len(gather) + ["scatter"] * len(scatter) + ["spread"] * len(spread)
        self.args = list(gather) + list(scatter) + list(spread)
        self.n = len(self.args)

    def out_shape(self):
        return [_sds(a.shape if k == "scatter" else (N_DEV,) + a.shape, a.dtype) for k, a in zip(self.kinds, self.args)]

    def specs(self):
        return [pl.BlockSpec(memory_space=pl.ANY)] * self.n

    def scratch(self):
        return [pltpu.SemaphoreType.DMA((7 * self.n,)), pltpu.SemaphoreType.DMA((7 * self.n,)),
                pltpu.SemaphoreType.DMA((self.n,))]

    def _plan(self, x_refs, out_refs, send_sems, recv_sems, local_sems):
        mx, my, mc = _me()
        me = 4 * mx + 2 * my + mc
        here, sibling = (mx, my, mc), (mx, my, 1 - mc)
        chips = [(1 - mx, my), (mx, 1 - my), (1 - mx, 1 - my)]
        local, first, last = [], [], []
        relay = [[], [], []]
        for a, kind in enumerate(self.kinds):
            x, out = x_refs[a], out_refs[a]

            def rc(k, src, dst, to):
                return pltpu.make_async_remote_copy(
                    src_ref=src, dst_ref=dst, send_sem=send_sems.at[7 * a + k], recv_sem=recv_sems.at[7 * a + k],
                    device_id=to, device_id_type=MESH)

            if kind == "gather":
                local.append(pltpu.make_async_copy(x, out.at[me], local_sems.at[a]))
                first.append(rc(0, x, out.at[me], sibling))
                last.append(rc(0, x, out.at[me ^ 1], here))
                for j, (cx, cy) in enumerate(chips):
                    first.append(rc(1 + j, x, out.at[me], (cx, cy, mc)))
                    landed = out.at[4 * cx + 2 * cy + mc]
                    relay[j].append((rc(1 + j, x, landed, here), rc(4 + j, landed, landed, sibling)))
                    last.append(rc(4 + j, x, out.at[4 * cx + 2 * cy + 1 - mc], here))
            else:
                own = x.at[me] if kind == "scatter" else x
                local.append(pltpu.make_async_copy(own, out.at[me], local_sems.at[a]))
                for k in range(1, N_DEV):
                    src = x.at[me ^ k] if kind == "scatter" else x
                    first.append(rc(k - 1, src, out.at[me], _peer(mx, my, mc, k)))
                    last.append(rc(k - 1, own, out.at[me ^ k], here))
        return local, first, relay[0] + relay[1] + relay[2], last

    def start(self, *refs):
        local, first, _, _ = self._plan(*refs)
        for cp in local + first:
            cp.start()

    def finish(self, *refs):
        local, first, relay, last = self._plan(*refs)
        for arrival, onward in relay:
            arrival.wait_recv()
            onward.start()
        for cp in last:
            cp.wait_recv()
        for cp in first:
            cp.wait_send()
        for _, onward in relay:
            onward.wait_send()
        for cp in local:
            cp.wait()


def _call(body, *, name, grid, in_specs, out_specs, out_shape, args, scratch=(), comm=None, aliases=None):
    params = _params(("arbitrary",) * len(grid))

    def at(end):
        conds = [pl.program_id(d) == (n - 1 if end else 0) for d, n in enumerate(grid)]
        return functools.reduce(lambda p, q: p & q, conds)

    if comm is None:
        res = pl.pallas_call(
            body, name=name, grid=grid, in_specs=list(in_specs), out_specs=list(out_specs), out_shape=list(out_shape),
            scratch_shapes=list(scratch), input_output_aliases=aliases or {}, compiler_params=params)(*args)
        return list(res), []
    n_in, n_out, n_scr, cn = len(in_specs), len(out_specs), len(scratch), comm.n

    def hosted(*refs):
        ins, refs = refs[:n_in], refs[n_in:]
        cins, refs = refs[:cn], refs[cn:]
        outs, refs = refs[:n_out], refs[n_out:]
        couts, refs = refs[:cn], refs[cn:]
        scr, sems = refs[:n_scr], refs[n_scr:]

        @pl.when(at(False))
        def _():
            comm.start(cins, couts, *sems)

        body(*ins, *outs, *scr)

        @pl.when(at(True))
        def _():
            comm.finish(cins, couts, *sems)

    res = pl.pallas_call(
        hosted, name=name, grid=grid, in_specs=list(in_specs) + comm.specs(), out_specs=list(out_specs) + comm.specs(),
        out_shape=list(out_shape) + comm.out_shape(), scratch_shapes=list(scratch) + comm.scratch(),
        input_output_aliases=aliases or {}, compiler_params=params)(*args, *comm.args)
    return list(res[:n_out]), list(res[n_out:])


def _comm_only(comm, name):
    return _call(lambda: None, name=name, grid=(1,), in_specs=[], out_specs=[], out_shape=[], args=[], comm=comm)[1]


def _ada_fwd(s_in, w_ada, b_my):
    nw = w_ada.shape[1]

    def body(s_ref, w_ref, b_ref, act_ref, out_ref):
        s = s_ref[...]
        act = s * _sigmoid(s)
        act_ref[...] = act
        out_ref[...] = jnp.dot(act.astype(BF16), w_ref[...].astype(BF16), preferred_element_type=F32) + b_ref[...]

    return pl.pallas_call(
        body, name="ada_fwd", grid=(1,),
        in_specs=[_full((16, D)), _full((D, nw)), _full((1, nw))],
        out_specs=[_full((16, D)), _full((16, nw))],
        out_shape=[_sds((16, D), F32), _sds((16, nw), F32)],
        compiler_params=_params(("arbitrary",)),
    )(s_in, w_ada, b_my)


def _ada_bwd(act, dmod_my, w_ada, m, v, tr=256):
    nw = w_ada.shape[1]

    def body(act_ref, dm_ref, w_ref, m_ref, v_ref, g_ref, d_ref, m2_ref, v2_ref, pc_ref):
        dm = dm_ref[...].astype(BF16)
        g = lax.dot_general(act_ref[...].astype(BF16), dm, TN, preferred_element_type=F32)
        w = w_ref[...]
        delta, m2, v2 = _adamw(w, g, m_ref[...], v_ref[...])
        g_ref[...] = g
        d_ref[...] = delta
        m2_ref[...] = m2
        v2_ref[...] = v2
        pc_ref[...] = lax.dot_general(dm[8:16, :], w.astype(BF16), NT, preferred_element_type=F32)

    wspec = _row(tr, nw)
    return pl.pallas_call(
        body, name="ada_bwd", grid=(D // tr,),
        in_specs=[pl.BlockSpec((16, tr), lambda i: (0, i)), _full((16, nw)), wspec, wspec, wspec],
        out_specs=[wspec, wspec, wspec, wspec, pl.BlockSpec((8, tr), lambda i: (0, i))],
        out_shape=[_sds((D, nw), F32)] * 4 + [_sds((8, D), F32)],
        compiler_params=_params(("arbitrary",)),
    )(act, dmod_my, w_ada, m, v)


def _k_in(x, modv, w_in, cos, sin, tm, comm=None):
    L = x.shape[0]

    def body(x_ref, mod_ref, w_ref, cos_ref, sin_ref, h_ref, q_ref, k_ref, v_ref, u_ref, vb_ref, ga_ref, gb_ref):
        n, _ = _ln(x_ref[...])
        h = (n * (1.0 + mod_ref[1:2, :]) + mod_ref[0:1, :]).astype(BF16)
        h_ref[...] = h
        c, s = cos_ref[...], sin_ref[...]

        def proj(lo, width):
            return lax.dot_general(h, w_ref[lo:lo + width, :], NT, preferred_element_type=F32)

        for i in range(4):
            q_ref[:, i * 128:(i + 1) * 128] = (_rope(proj(O_Q + i * 128, 128), c, s) * Q_SCALE).astype(BF16)
        k_ref[...] = _rope(proj(O_K, KV_W), c, s).astype(BF16)
        v_ref[...] = proj(O_V, KV_W).astype(BF16)
        u_ref[...] = proj(O_U, GM_W).astype(BF16)
        vb_ref[...] = proj(O_VB, GM_W).astype(BF16)
        ga_ref[...] = proj(O_GA, D).astype(BF16)
        gb_ref[...] = proj(O_GB, D).astype(BF16)

    widths = [D, Q_W, KV_W, KV_W, GM_W, GM_W, D, D]
    return _call(
        body, name="fwd_in", grid=(L // tm,),
        in_specs=[_row(tm, D), _full((8, D)), _resident((IN_W, D)), _row(tm, 128), _row(tm, 128)],
        out_specs=[_row(tm, w) for w in widths],
        out_shape=[_sds((L, w), BF16) for w in widths],
        args=(x, modv, w_in, cos, sin), comm=comm)


def _k_ctx(ctx, modc, w_kv):
    C = ctx.shape[0]

    def body(c_ref, mod_ref, w_ref, hc_ref, kc_ref, vc_ref):
        n, _ = _ln(c_ref[...])
        hc = (n * (1.0 + mod_ref[1:2, :]) + mod_ref[0:1, :]).astype(BF16)
        hc_ref[...] = hc
        kv = lax.dot_general(hc, w_ref[...], NT, preferred_element_type=F32)
        kc_ref[...] = kv[:, :KV_W].astype(BF16)
        vc_ref[...] = kv[:, KV_W:].astype(BF16)

    return pl.pallas_call(
        body, name="fwd_ctx", grid=(1,),
        in_specs=[_full((C, D)), _full((8, D)), _full((2 * KV_W, D))],
        out_specs=[_full((C, D)), _full((C, KV_W)), _full((C, KV_W))],
        out_shape=[_sds((C, D), BF16), _sds((C, KV_W), BF16), _sds((C, KV_W), BF16)],
        compiler_params=_params(("arbitrary",)),
    )(ctx, modc, w_kv)


def _attn_bias():
    r = (np.arange(GQA_GROUP * BLK) & (BLK - 1))[:, None]
    j = np.arange(3 * BLK)[None, :]
    band = np.abs(j - BLK - r) <= BLK
    variants = [band & (j >= BLK), band, band & (j < 2 * BLK)]
    return jnp.asarray(np.stack([np.where(v, 0.0, NEG_INF) for v in variants]), F32)


def _bias_spec(nb):
    return pl.BlockSpec((1, GQA_GROUP * BLK, 3 * BLK),
                        lambda n: (jnp.where(n == 0, 0, jnp.where(n >= nb - 1, 2, 1)), 0, 0))


def _masked(s, bias, C):
    return jnp.concatenate([s[:, :C], s[:, C:] + bias], axis=1)


def _sink_col(sink_ref, hk):
    grp = lax.broadcasted_iota(jnp.int32, (GQA_GROUP * BLK, 1), 0) >> 7
    col = jnp.full((GQA_GROUP * BLK, 1), sink_ref[hk * GQA_GROUP], F32)
    for g in range(1, GQA_GROUP):
        col = jnp.where(grp == g, sink_ref[hk * GQA_GROUP + g], col)
    return col


def _kv_specs(nb):
    prev = pl.BlockSpec((BLK, KV_W), lambda n: (jnp.clip(n - 1, 0, nb - 1), 0))
    cur = pl.BlockSpec((BLK, KV_W), lambda n: (jnp.minimum(n, nb - 1), 0))
    nxt = pl.BlockSpec((BLK, KV_W), lambda n: (jnp.minimum(n + 1, nb - 1), 0))
    return [prev, cur, nxt]


def _k_attn(sink, q, k, v, kc, vc, bias, comm=None):
    L = q.shape[0]
    C = kc.shape[0]
    nb = L // BLK

    def body(sink_ref, q_ref, kp_ref, kn_ref, kx_ref, vp_ref, vn_ref, vx_ref, kc_ref, vc_ref, bias_ref, ya_ref, lse_ref):
        band = bias_ref[0]
        for hk in range(N_KV_HEADS):
            sl = slice(hk * HEAD_DIM, (hk + 1) * HEAD_DIM)
            kcat = jnp.concatenate([kc_ref[:, sl], kp_ref[:, sl], kn_ref[:, sl], kx_ref[:, sl]], axis=0)
            vcat = jnp.concatenate([vc_ref[:, sl], vp_ref[:, sl], vn_ref[:, sl], vx_ref[:, sl]], axis=0)
            qg = jnp.concatenate(
                [q_ref[:, (hk * GQA_GROUP + g) * HEAD_DIM:(hk * GQA_GROUP + g + 1) * HEAD_DIM] for g in range(GQA_GROUP)],
                axis=0)
            s = _masked(lax.dot_general(qg, kcat, NT, preferred_element_type=F32), band, C)
            sink_c = _sink_col(sink_ref, hk)
            m = jnp.maximum(jnp.max(s, axis=1, keepdims=True), sink_c)
            p = jnp.exp(s - m)
            den = jnp.sum(p, axis=1, keepdims=True) + jnp.exp(sink_c - m)
            o = jnp.dot(p.astype(BF16), vcat, preferred_element_type=F32) * (1.0 / den)
            lse = m + jnp.log(den)
            for g in range(GQA_GROUP):
                h = hk * GQA_GROUP + g
                ya_ref[:, h * HEAD_DIM:(h + 1) * HEAD_DIM] = o[g * BLK:(g + 1) * BLK, :].astype(BF16)
                lse_ref[:, h:h + 1] = lse[g * BLK:(g + 1) * BLK, :]

    kv3 = _kv_specs(nb)
    return _call(
        body, name="fwd_attn", grid=(nb,),
        in_specs=[pl.BlockSpec(memory_space=pltpu.SMEM), _row(BLK, Q_W)] + kv3 + kv3
                 + [_full((C, KV_W)), _full((C, KV_W)), _bias_spec(nb)],
        out_specs=[_row(BLK, Q_W), _row(BLK, N_Q_HEADS)],
        out_shape=[_sds((L, Q_W), BF16), _sds((L, N_Q_HEADS), F32)],
        args=(sink, q, k, k, k, v, v, v, kc, vc, bias), comm=comm)


GMLP_CHUNKS = 4


def _gmlp_fwd_vals(u, vb, lnv_ref, ws_ref, bst_ref):
    uf = u.astype(F32)
    vf = vb.astype(F32)
    gu, tu = _gelu(uf)
    gv, tv = _gelu(vf)
    vhat, rstd = _ln(gv)
    vn = (vhat * lnv_ref[0:1, :] + lnv_ref[1:2, :]).astype(BF16)
    s_parts = []
    for g in range(N_GROUPS):
        sg = jnp.dot(ws_ref[g], vn[:, g * GROUP_DIM:(g + 1) * GROUP_DIM], preferred_element_type=F32)
        s_parts.append(sg + bst_ref[:, g:g + 1])
    s = jnp.concatenate(s_parts, axis=1)
    return uf, vf, gu, tu, tv, vhat, rstd, vn, s


def _k_gmlp(u, vb, lnv, ws, bst):
    L = u.shape[0]
    nch = min(GMLP_CHUNKS, L // BLK)
    tm = nch * BLK

    def body(u_ref, vb_ref, lnv_ref, ws_ref, bst_ref, yb_ref):
        for c in range(nch):
            rows = slice(c * BLK, (c + 1) * BLK)
            _, _, gu, _, _, _, _, _, s = _gmlp_fwd_vals(u_ref[rows, :], vb_ref[rows, :], lnv_ref, ws_ref, bst_ref)
            yb_ref[rows, :] = (gu * s).astype(BF16)

    return pl.pallas_call(
        body, name="fwd_gmlp", grid=(L // tm,),
        in_specs=[_row(tm, GM_W), _row(tm, GM_W), _full((8, GM_W)), _full((N_GROUPS, BLK, BLK)), _full((BLK, N_GROUPS))],
        out_specs=_row(tm, GM_W),
        out_shape=_sds((L, GM_W), BF16),
        compiler_params=_params(("arbitrary",)),
    )(u, vb, lnv, ws, bst)


def _k_merge(x, ya, yb, ga, gb, w_a, w_b, w_o, modv, lnv, tm):
    L = x.shape[0]

    def body(x_ref, ya_ref, yb_ref, ga_ref, gb_ref, wa_ref, wb_ref, wo_ref, mod_ref, ln_ref,
             mg_ref, mix_ref, xm_ref, h2_ref):
        a = jnp.dot(ya_ref[...], wa_ref[...], preferred_element_type=F32)
        b = jnp.dot(yb_ref[...], wb_ref[...], preferred_element_type=F32)
        merged = (_sigmoid(ga_ref[...].astype(F32)) * a + _sigmoid(gb_ref[...].astype(F32)) * b).astype(BF16)
        mg_ref[...] = merged
        mix = jnp.dot(merged, wo_ref[...], preferred_element_type=F32)
        mix_ref[...] = mix.astype(BF16)
        r1 = ALPHA * x_ref[...] + mod_ref[2:3, :] * mix
        r1hat, _ = _ln(r1)
        xm = r1hat * ln_ref[0:1, :] + ln_ref[1:2, :]
        xm_ref[...] = xm
        n2, _ = _ln(xm)
        h2_ref[...] = (n2 * (1.0 + mod_ref[4:5, :]) + mod_ref[3:4, :]).astype(BF16)

    return pl.pallas_call(
        body, name="fwd_merge", grid=(L // tm,),
        in_specs=[_row(tm, D), _row(tm, Q_W), _row(tm, GM_W), _row(tm, D), _row(tm, D),
                  _resident((Q_W, D)), _resident((GM_W, D)), _resident((D, D)), _full((8, D)), _full((8, D))],
        out_specs=[_row(tm, D)] * 4,
        out_shape=[_sds((L, D), BF16), _sds((L, D), BF16), _sds((L, D), F32), _sds((L, D), BF16)],
        compiler_params=_params(("arbitrary",)),
    )(x, ya, yb, ga, gb, w_a, w_b, w_o, modv, lnv)


FFN_CH = 1408


def _k_ffn(h2, xm, tgt, w_fi, w_fo, modv, lnv, tm):
    L = h2.shape[0]

    def body(h2_ref, xm_ref, t_ref, wi_ref, wo_ref, mod_ref, ln_ref, gate_ref, up_ref, a_ref, dr2_ref, df_ref, acc_ref):
        @pl.when(pl.program_id(0) == 0)
        def _():
            acc_ref[...] = jnp.zeros_like(acc_ref)

        h2v = h2_ref[...]
        f = jnp.zeros((tm, D), F32)
        for j in range(FFN_H // FFN_CH):
            lo = j * FFN_CH
            gate = lax.dot_general(h2v, wi_ref[lo:lo + FFN_CH, :], NT, preferred_element_type=F32)
            up = lax.dot_general(h2v, wi_ref[FFN_H + lo:FFN_H + lo + FFN_CH, :], NT, preferred_element_type=F32)
            act = (gate * _sigmoid(gate) * up).astype(BF16)
            gate_ref[:, lo:lo + FFN_CH] = gate.astype(BF16)
            up_ref[:, lo:lo + FFN_CH] = up.astype(BF16)
            a_ref[:, lo:lo + FFN_CH] = act
            f = f + jnp.dot(act, wo_ref[lo:lo + FFN_CH, :], preferred_element_type=F32)
        gate2 = mod_ref[5:6, :]
        r2 = ALPHA * xm_ref[...] + gate2 * f
        r2hat, rstd = _ln(r2)
        y = r2hat * ln_ref[2:3, :] + ln_ref[3:4, :]
        err = y - t_ref[...]
        dy = err * (1.0 / D)
        dr2 = _ln_bwd(dy * ln_ref[2:3, :], r2hat, rstd)
        dr2_ref[...] = dr2
        df_ref[...] = (gate2 * dr2).astype(BF16)
        acc_ref[0:1, :] += _colsum(dy * r2hat)
        acc_ref[1:2, :] += _colsum(dy)
        acc_ref[2:3, :] += _colsum(dr2 * f)
        acc_ref[3:4, :] += _colsum(err * err) * (0.5 / D)

    return pl.pallas_call(
        body, name="fwd_ffn", grid=(L // tm,),
        in_specs=[_row(tm, D), _row(tm, D), _row(tm, D), _resident((2 * FFN_H, D)), _resident((FFN_H, D)),
                  _full((8, D)), _full((8, D))],
        out_specs=[_row(tm, FFN_H)] * 3 + [_row(tm, D), _row(tm, D), _full((8, D))],
        out_shape=[_sds((L, FFN_H), BF16)] * 3 + [_sds((L, D), F32), _sds((L, D), BF16), _sds((8, D), F32)],
        compiler_params=_params(("arbitrary",)),
    )(h2, xm, tgt, w_fi, w_fo, modv, lnv)


def _k_ffn_bwd(df, gate, up, xm, dr2, x, mix, w_fi, w_fo, modv, lnv, tm):
    L = df.shape[0]

    def body(df_ref, gate_ref, up_ref, xm_ref, dr2_ref, x_ref, mix_ref, wi_ref, wo_ref, mod_ref, ln_ref,
             dF_ref, dmix_ref, dxp_ref, acc_ref):
        @pl.when(pl.program_id(0) == 0)
        def _():
            acc_ref[...] = jnp.zeros_like(acc_ref)

        dfv = df_ref[...]
        dh2 = jnp.zeros((tm, D), F32)
        for j in range(FFN_H // FFN_CH):
            lo = j * FFN_CH
            da = lax.dot_general(dfv, wo_ref[lo:lo + FFN_CH, :], NT, preferred_element_type=F32)
            gate = gate_ref[:, lo:lo + FFN_CH].astype(F32)
            upv = up_ref[:, lo:lo + FFN_CH].astype(F32)
            sg = _sigmoid(gate)
            d_gate = (da * upv * (sg * (1.0 + gate * (1.0 - sg)))).astype(BF16)
            d_up = (da * (gate * sg)).astype(BF16)
            dF_ref[:, lo:lo + FFN_CH] = d_gate
            dF_ref[:, FFN_H + lo:FFN_H + lo + FFN_CH] = d_up
            dh2 = dh2 + jnp.dot(d_gate, wi_ref[lo:lo + FFN_CH, :], preferred_element_type=F32)
            dh2 = dh2 + jnp.dot(d_up, wi_ref[FFN_H + lo:FFN_H + lo + FFN_CH, :], preferred_element_type=F32)
        n2, rstd2 = _ln(xm_ref[...])
        acc_ref[0:1, :] += _colsum(dh2)
        acc_ref[1:2, :] += _colsum(dh2 * n2)
        dxm = ALPHA * dr2_ref[...] + _ln_bwd(dh2 * (1.0 + mod_ref[4:5, :]), n2, rstd2)
        mixf = mix_ref[...].astype(F32)
        gate1 = mod_ref[2:3, :]
        r1hat, rstd1 = _ln(ALPHA * x_ref[...] + gate1 * mixf)
        acc_ref[2:3, :] += _colsum(dxm * r1hat)
        acc_ref[3:4, :] += _colsum(dxm)
        dr1 = _ln_bwd(dxm * ln_ref[0:1, :], r1hat, rstd1)
        dmix_ref[...] = (gate1 * dr1).astype(BF16)
        dxp_ref[...] = ALPHA * dr1
        acc_ref[4:5, :] += _colsum(dr1 * mixf)

    return pl.pallas_call(
        body, name="bwd_ffn", grid=(L // tm,),
        in_specs=[_row(tm, D), _row(tm, FFN_H), _row(tm, FFN_H), _row(tm, D), _row(tm, D), _row(tm, D), _row(tm, D),
                  _resident((2 * FFN_H, D)), _resident((FFN_H, D)), _full((8, D)), _full((8, D))],
        out_specs=[_row(tm, 2 * FFN_H), _row(tm, D), _row(tm, D), _full((8, D))],
        out_shape=[_sds((L, 2 * FFN_H), BF16), _sds((L, D), BF16), _sds((L, D), F32), _sds((8, D), F32)],
        compiler_params=_params(("arbitrary",)),
    )(df, gate, up, xm, dr2, x, mix, w_fi, w_fo, modv, lnv)


def _k_merge_bwd(dmix, ya, yb, ga, gb, w_a, w_b, w_o, tm):
    L = dmix.shape[0]

    def body(dmix_ref, ya_ref, yb_ref, ga_ref, gb_ref, wa_ref, wb_ref, wo_ref,
             dA_ref, dB_ref, dga_ref, dgb_ref, dya_ref, dyb_ref):
        dmg = lax.dot_general(dmix_ref[...], wo_ref[...], NT, preferred_element_type=F32)
        a = jnp.dot(ya_ref[...], wa_ref[...], preferred_element_type=F32)
        sa = _sigmoid(ga_ref[...].astype(F32))
        dA = (dmg * sa).astype(BF16)
        dA_ref[...] = dA
        dga_ref[...] = (dmg * a * (sa * (1.0 - sa))).astype(BF16)
        dya_ref[...] = lax.dot_general(dA, wa_ref[...], NT, preferred_element_type=F32).astype(BF16)
        b = jnp.dot(yb_ref[...], wb_ref[...], preferred_element_type=F32)
        sb = _sigmoid(gb_ref[...].astype(F32))
        dB = (dmg * sb).astype(BF16)
        dB_ref[...] = dB
        dgb_ref[...] = (dmg * b * (sb * (1.0 - sb))).astype(BF16)
        dyb_ref[...] = lax.dot_general(dB, wb_ref[...], NT, preferred_element_type=F32).astype(BF16)

    return pl.pallas_call(
        body, name="bwd_merge", grid=(L // tm,),
        in_specs=[_row(tm, D), _row(tm, Q_W), _row(tm, GM_W), _row(tm, D), _row(tm, D),
                  _resident((Q_W, D)), _resident((GM_W, D)), _resident((D, D))],
        out_specs=[_row(tm, D)] * 4 + [_row(tm, Q_W), _row(tm, GM_W)],
        out_shape=[_sds((L, D), BF16)] * 4 + [_sds((L, Q_W), BF16), _sds((L, GM_W), BF16)],
        compiler_params=_params(("arbitrary",)),
    )(dmix, ya, yb, ga, gb, w_a, w_b, w_o)


def _k_gmlp_bwd(u, vb, dyb, lnv, ws, wst, bst):
    L = u.shape[0]
    nch = min(GMLP_CHUNKS, L // BLK)
    tm = nch * BLK

    def body(u_ref, vb_ref, dyb_ref, lnv_ref, ws_ref, wst_ref, bst_ref, du_ref, dvb_ref, gws_ref, gbst_ref, gln_ref):
        @pl.when(pl.program_id(0) == 0)
        def _():
            gws_ref[...] = jnp.zeros_like(gws_ref)
            gbst_ref[...] = jnp.zeros_like(gbst_ref)
            gln_ref[...] = jnp.zeros_like(gln_ref)

        gws = [None] * N_GROUPS
        gbs = [None] * N_GROUPS
        gln_g = gln_b = None
        for c in range(nch):
            rows = slice(c * BLK, (c + 1) * BLK)
            uf, vf, gu, tu, tv, vhat, rstd, vn, s = _gmlp_fwd_vals(u_ref[rows, :], vb_ref[rows, :], lnv_ref, ws_ref, bst_ref)
            dyb_f = dyb_ref[rows, :].astype(F32)
            du_ref[rows, :] = (dyb_f * s * _gelu_grad(uf, tu)).astype(BF16)
            ds = dyb_f * gu
            ds_b = ds.astype(BF16)
            dvn_parts = []
            for g in range(N_GROUPS):
                sl = slice(g * GROUP_DIM, (g + 1) * GROUP_DIM)
                gw = lax.dot_general(ds_b[:, sl], vn[:, sl], NT, preferred_element_type=F32)
                gb = jnp.sum(ds[:, sl], axis=1, keepdims=True)
                gws[g] = gw if c == 0 else gws[g] + gw
                gbs[g] = gb if c == 0 else gbs[g] + gb
                dvn_parts.append(jnp.dot(wst_ref[g], ds_b[:, sl], preferred_element_type=F32))
            dvn = jnp.concatenate(dvn_parts, axis=1)
            gg, gb_ = _colsum(dvn * vhat), _colsum(dvn)
            gln_g = gg if c == 0 else gln_g + gg
            gln_b = gb_ if c == 0 else gln_b + gb_
            dgv = _ln_bwd(dvn * lnv_ref[0:1, :], vhat, rstd)
            dvb_ref[rows, :] = (dgv * _gelu_grad(vf, tv)).astype(BF16)
        for g in range(N_GROUPS):
            gws_ref[g] += gws[g]
            gbst_ref[:, g:g + 1] += gbs[g]
        gln_ref[0:1, :] += gln_g
        gln_ref[1:2, :] += gln_b

    return pl.pallas_call(
        body, name="bwd_gmlp", grid=(L // tm,),
        in_specs=[_row(tm, GM_W)] * 3 + [_full((8, GM_W)), _full((N_GROUPS, BLK, BLK)), _full((N_GROUPS, BLK, BLK)),
                                         _full((BLK, N_GROUPS))],
        out_specs=[_row(tm, GM_W), _row(tm, GM_W), _full((N_GROUPS, BLK, BLK)), _full((BLK, N_GROUPS)), _full((8, GM_W))],
        out_shape=[_sds((L, GM_W), BF16), _sds((L, GM_W), BF16), _sds((N_GROUPS, BLK, BLK), F32),
                   _sds((BLK, N_GROUPS), F32), _sds((8, GM_W), F32)],
        compiler_params=_params(("arbitrary",)),
    )(u, vb, dyb, lnv, ws, wst, bst)


def _k_attn_bwd(sink, q, k, v, kc, vc, dya, lse, cos, sin, bias, comm=None):
    L = q.shape[0]
    C = kc.shape[0]
    nb = L // BLK
    NK = C + 3 * BLK

    def body(sink_ref, q_ref, kp_ref, kn_ref, kx_ref, vp_ref, vn_ref, vx_ref, kc_ref, vc_ref, do_ref, lse_ref,
             cq_ref, sq_ref, ck_ref, sk_ref, bias_ref,
             dq_ref, dk_ref, dv_ref, dkc_ref, dvc_ref, dsink_ref,
             dq_scr, ck_scr, cv_scr, kp_acc, kc_acc, vp_acc, vc_acc):
        n = pl.program_id(0)

        @pl.when(n == 0)
        def _():
            for r in (kp_acc, kc_acc, vp_acc, vc_acc, dkc_ref, dvc_ref, dsink_ref):
                r[...] = jnp.zeros_like(r)

        @pl.when(n < nb)
        def _():
            band = bias_ref[0]
            for hk in range(N_KV_HEADS):
                sl = slice(hk * HEAD_DIM, (hk + 1) * HEAD_DIM)
                kcat = jnp.concatenate([kc_ref[:, sl], kp_ref[:, sl], kn_ref[:, sl], kx_ref[:, sl]], axis=0)
                vcat = jnp.concatenate([vc_ref[:, sl], vp_ref[:, sl], vn_ref[:, sl], vx_ref[:, sl]], axis=0)
                heads = [hk * GQA_GROUP + g for g in range(GQA_GROUP)]
                qg = jnp.concatenate([q_ref[:, h * HEAD_DIM:(h + 1) * HEAD_DIM] for h in heads], axis=0)
                dog = jnp.concatenate([do_ref[:, h * HEAD_DIM:(h + 1) * HEAD_DIM] for h in heads], axis=0)
                lse_c = jnp.concatenate([lse_ref[:, h:h + 1] for h in heads], axis=0)
                s = _masked(lax.dot_general(qg, kcat, NT, preferred_element_type=F32), band, C)
                p = jnp.exp(s - lse_c)
                dp = lax.dot_general(dog, vcat, NT, preferred_element_type=F32)
                delta = jnp.sum(p * dp, axis=1, keepdims=True)
                ds = (p * (dp - delta)).astype(BF16)
                dqs = jnp.dot(ds, kcat, preferred_element_type=F32)
                ck_scr[:, sl] = lax.dot_general(ds, qg, TN, preferred_element_type=F32)
                cv_scr[:, sl] = lax.dot_general(p.astype(BF16), dog, TN, preferred_element_type=F32)
                p_sink = jnp.exp(_sink_col(sink_ref, hk) - lse_c) * delta
                for g, h in enumerate(heads):
                    dq_scr[:, h * HEAD_DIM:(h + 1) * HEAD_DIM] = dqs[g * BLK:(g + 1) * BLK, :]
                    tot = jnp.sum(p_sink[g * BLK:(g + 1) * BLK, :], axis=0, keepdims=True)
                    dsink_ref[h:h + 1, :] -= jnp.broadcast_to(tot, (1, 128))
            cq, sq = cq_ref[...], sq_ref[...]
            for i in range(4):
                dq_ref[:, i * 128:(i + 1) * 128] = _unrope(dq_scr[:, i * 128:(i + 1) * 128] * Q_SCALE, cq, sq).astype(BF16)
            dkc_ref[...] += ck_scr[0:C, :]
            dvc_ref[...] += cv_scr[0:C, :]

        @pl.when(n >= nb)
        def _():
            ck_scr[...] = jnp.zeros_like(ck_scr)
            cv_scr[...] = jnp.zeros_like(cv_scr)

        dk_ref[...] = _unrope(kp_acc[...] + ck_scr[C:C + BLK, :], ck_ref[...], sk_ref[...]).astype(BF16)
        dv_ref[...] = (vp_acc[...] + cv_scr[C:C + BLK, :]).astype(BF16)
        kp_acc[...] = kc_acc[...] + ck_scr[C + BLK:C + 2 * BLK, :]
        vp_acc[...] = vc_acc[...] + cv_scr[C + BLK:C + 2 * BLK, :]
        kc_acc[...] = ck_scr[C + 2 * BLK:C + 3 * BLK, :]
        vc_acc[...] = cv_scr[C + 2 * BLK:C + 3 * BLK, :]

    kv3 = _kv_specs(nb)
    cur = lambda w: pl.BlockSpec((BLK, w), lambda n: (jnp.minimum(n, nb - 1), 0))
    late = lambda w: pl.BlockSpec((BLK, w), lambda n: (jnp.maximum(n - 1, 0), 0))
    return _call(
        body, name="bwd_attn", grid=(nb + 1,),
        in_specs=[pl.BlockSpec(memory_space=pltpu.SMEM), cur(Q_W)] + kv3 + kv3
                 + [_full((C, KV_W)), _full((C, KV_W)), cur(Q_W), cur(N_Q_HEADS), cur(128), cur(128), late(128), late(128),
                    _bias_spec(nb)],
        out_specs=[cur(Q_W), late(KV_W), late(KV_W), _full((C, KV_W)), _full((C, KV_W)), _full((8, 128))],
        out_shape=[_sds((L, Q_W), BF16), _sds((L, KV_W), BF16), _sds((L, KV_W), BF16),
                   _sds((C, KV_W), F32), _sds((C, KV_W), F32), _sds((8, 128), F32)],
        scratch=[pltpu.VMEM((BLK, Q_W), F32), pltpu.VMEM((NK, KV_W), F32), pltpu.VMEM((NK, KV_W), F32)]
                + [pltpu.VMEM((BLK, KV_W), F32)] * 4,
        args=(sink, q, k, k, k, v, v, v, kc, vc, dya, lse, cos, sin, cos, sin, bias), comm=comm)


def _k_ctx_bwd(ctx, modc, hc, dkc, dvc, w_kv, gw_in):
    C = ctx.shape[0]
    kv_block = pl.BlockSpec((2 * KV_W, D), lambda i: (O_K // (2 * KV_W), 0))

    def body(c_ref, mod_ref, hc_ref, dkc_ref, dvc_ref, w_ref, gin_ref, gw_ref, dmod_ref):
        dkv = jnp.concatenate([dkc_ref[...], dvc_ref[...]], axis=1).astype(BF16)
        gw_ref[...] = gin_ref[...] + lax.dot_general(dkv, hc_ref[...], TN, preferred_element_type=F32)
        dhc = jnp.dot(dkv, w_ref[...], preferred_element_type=F32)
        n, _ = _ln(c_ref[...])
        dmod_ref[...] = jnp.zeros_like(dmod_ref)
        dmod_ref[0:1, :] = _colsum(dhc)
        dmod_ref[1:2, :] = _colsum(dhc * n)

    return pl.pallas_call(
        body, name="bwd_ctx", grid=(1,),
        in_specs=[_full((C, D)), _full((8, D)), _full((C, D)), _full((C, KV_W)), _full((C, KV_W)), _full((2 * KV_W, D)),
                  kv_block],
        out_specs=[kv_block, _full((8, D))],
        out_shape=[_sds((IN_W, D), F32), _sds((8, D), F32)],
        input_output_aliases={6: 0},
        compiler_params=_params(("arbitrary",)),
    )(ctx, modc, hc, dkc, dvc, w_kv, gw_in)


def _k_in_bwd(dq, dk, dv, du, dvb, dga, dgb, x, dxp, w_in, modv, tm, comm=None):
    L = x.shape[0]
    parts = [(O_Q, Q_W), (O_K, KV_W), (O_V, KV_W), (O_U, GM_W), (O_VB, GM_W), (O_GA, D), (O_GB, D)]

    def body(dq_ref, dk_ref, dv_ref, du_ref, dvb_ref, dga_ref, dgb_ref, x_ref, dxp_ref, w_ref, mod_ref,
             dP_ref, gx_ref, acc_ref):
        @pl.when(pl.program_id(0) == 0)
        def _():
            acc_ref[...] = jnp.zeros_like(acc_ref)

        for (lo, width), r in zip(parts, (dq_ref, dk_ref, dv_ref, du_ref, dvb_ref, dga_ref, dgb_ref)):
            dP_ref[:, lo:lo + width] = r[...]
        dh = jnp.dot(dP_ref[...], w_ref[...], preferred_element_type=F32)
        n1, rstd1 = _ln(x_ref[...])
        acc_ref[0:1, :] += _colsum(dh)
        acc_ref[1:2, :] += _colsum(dh * n1)
        gx_ref[...] = dxp_ref[...] + _ln_bwd(dh * (1.0 + mod_ref[1:2, :]), n1, rstd1)

    return _call(
        body, name="bwd_in", grid=(L // tm,),
        in_specs=[_row(tm, w) for _, w in parts] + [_row(tm, D), _row(tm, D), _resident((IN_W, D)), _full((8, D))],
        out_specs=[_row(tm, IN_W), _row(tm, D), _full((8, D))],
        out_shape=[_sds((L, IN_W), BF16), _sds((L, D), F32), _sds((8, D), F32)],
        args=(dq, dk, dv, du, dvb, dga, dgb, x, dxp, w_in, modv), comm=comm)


def _wgrad(a, b, name, tk, tt, comm=None):
    T, K = a.shape
    N = b.shape[1]

    def body(a_ref, b_ref, o_ref):
        @pl.when(pl.program_id(1) == 0)
        def _():
            o_ref[...] = jnp.zeros_like(o_ref)

        o_ref[...] += lax.dot_general(a_ref[...], b_ref[...], TN, preferred_element_type=F32)

    (out,), got = _call(
        body, name=name, grid=(K // tk, T // tt),
        in_specs=[pl.BlockSpec((tt, tk), lambda j, t: (t, j)), pl.BlockSpec((tt, N), lambda j, t: (t, 0))],
        out_specs=[pl.BlockSpec((tk, N), lambda j, t: (j, 0))],
        out_shape=[_sds((K, N), F32)],
        args=(a, b), comm=comm)
    return (out, got) if comm is not None else out


def _adamw_reduce(parts, w, m, v, name, tr):
    R, C = w.shape

    def body(p_ref, w_ref, m_ref, v_ref, g_ref, d_ref, m2_ref, v2_ref):
        g = p_ref[0].astype(F32)
        for i in range(1, N_DEV):
            g = g + p_ref[i].astype(F32)
        delta, m2, v2 = _adamw(w_ref[...], g, m_ref[...], v_ref[...])
        g_ref[...] = g
        d_ref[...] = delta
        m2_ref[...] = m2
        v2_ref[...] = v2

    spec = _row(tr, C)
    return pl.pallas_call(
        body, name=name, grid=(R // tr,),
        in_specs=[pl.BlockSpec((N_DEV, tr, C), lambda i: (0, i, 0)), spec, spec, spec],
        out_specs=[spec] * 4,
        out_shape=[_sds((R, C), F32)] * 4,
        compiler_params=_params(("arbitrary",)),
    )(parts, w, m, v)


def _small_reduce(gath, gath_ws):
    def body(g_ref, w_ref, out_ref):
        tot = g_ref[0]
        wsum = w_ref[0]
        for i in range(1, N_DEV):
            tot = tot + g_ref[i]
            wsum = wsum + w_ref[i]
        out_ref[0:16, :] = tot
        out_ref[0:2, :] = tot[0:2, :] + tot[6:8, :]
        out_ref[15:16, :] = jnp.broadcast_to(jnp.sum(tot[15:16, :], axis=1, keepdims=True), (1, D))
        out_ref[16:SMALL_ROWS, :] = wsum

    return pl.pallas_call(
        body, name="small_reduce", grid=(1,),
        in_specs=[_full((N_DEV, 16, D)), _full((N_DEV, SMALL_ROWS - 16, D))],
        out_specs=_full((SMALL_ROWS, D)),
        out_shape=_sds((SMALL_ROWS, D), F32),
        compiler_params=_params(("arbitrary",)),
    )(gath, gath_ws)


def _small_adamw(w, g, m, v, name):
    shape = w.shape

    def body(w_ref, g_ref, m_ref, v_ref, d_ref, m2_ref, v2_ref):
        delta, m2, v2 = _adamw(w_ref[...], g_ref[...], m_ref[...], v_ref[...])
        d_ref[...] = delta
        m2_ref[...] = m2
        v2_ref[...] = v2

    return pl.pallas_call(
        body, name=name, grid=(1,),
        in_specs=[_full(shape)] * 4, out_specs=[_full(shape)] * 3,
        out_shape=[_sds(shape, F32)] * 3,
        compiler_params=_params(("arbitrary",)),
    )(w, g, m, v)


def _cctx_finish(gath, c_ctx, m, v):
    def body(g_ref, c_ref, m_ref, v_ref, gr_ref, d_ref, m2_ref, v2_ref):
        ds = g_ref[0]
        for i in range(1, N_DEV):
            ds = ds + g_ref[i]
        c = c_ref[...]
        sg = _sigmoid(c)
        g = ds * (sg * (1.0 + c * (1.0 - sg)))
        delta, m2, v2 = _adamw(c, g, m_ref[...], v_ref[...])
        gr_ref[...] = g
        d_ref[...] = delta
        m2_ref[...] = m2
        v2_ref[...] = v2

    return pl.pallas_call(
        body, name="cctx_finish", grid=(1,),
        in_specs=[_full((N_DEV, 8, D))] + [_full((8, D))] * 3, out_specs=[_full((8, D))] * 4,
        out_shape=[_sds((8, D), F32)] * 4,
        compiler_params=_params(("arbitrary",)),
    )(gath, c_ctx, m, v)


def _pad_rows(a, rows):
    return jnp.concatenate([a, jnp.zeros((rows - a.shape[0], a.shape[1]), a.dtype)], axis=0)


def _pack_small(b_ada, ln1_g, ln1_b, ln2_g, ln2_b, gm_g, gm_b, b_sp, sink, w_sp):
    rows = [b_ada.reshape(6, D), jnp.zeros((2, D), F32), ln1_g.reshape(1, D), ln1_b.reshape(1, D), ln2_g.reshape(1, D),
            ln2_b.reshape(1, D), jnp.concatenate([gm_g.reshape(1, GM_W), gm_b.reshape(1, GM_W)], axis=1),
            b_sp.reshape(1, D), _pad_rows(sink.reshape(1, N_Q_HEADS).T, D).T.reshape(1, D), jnp.zeros((1, D), F32),
            w_sp.reshape(N_GROUPS * BLK * BLK // D, D)]
    return jnp.concatenate(rows, axis=0)


def _unpack_small(p):
    return dict(b_ada=p[0:6].reshape(1, 6 * D), ln1_g=p[8:9], ln1_b=p[9:10], ln2_g=p[10:11], ln2_b=p[11:12],
                gmlp_ln_g=p[12:13, :GM_W], gmlp_ln_b=p[12:13, GM_W:], b_spatial=p[13:14].reshape(1, N_GROUPS, BLK),
                attn_sink=p[14:15, :N_Q_HEADS], w_spatial=p[16:].reshape(1, N_GROUPS, BLK, BLK))


def kernel(x, c, ctx, c_ctx, w_ada, b_ada, w_in, attn_sink, gmlp_ln_g, gmlp_ln_b, w_spatial, b_spatial, w_branch_a, w_branch_b, w_out, ln1_g, ln1_b, w_ffn_in, w_ffn_out, ln2_g, ln2_b, loss_target, m_c_ctx, m_w_ada, m_b_ada, m_w_in, m_attn_sink, m_gmlp_ln_g, m_gmlp_ln_b, m_w_spatial, m_b_spatial, m_w_branch_a, m_w_branch_b, m_w_out, m_ln1_g, m_ln1_b, m_w_ffn_in, m_w_ffn_out, m_ln2_g, m_ln2_b, v_c_ctx, v_w_ada, v_b_ada, v_w_in, v_attn_sink, v_gmlp_ln_g, v_gmlp_ln_b, v_w_spatial, v_b_spatial, v_w_branch_a, v_w_branch_b, v_w_out, v_ln1_g, v_ln1_b, v_w_ffn_in, v_w_ffn_out, v_ln2_g, v_ln2_b):
    L = x.shape[1]
    me = 4 * lax.axis_index("x") + 2 * lax.axis_index("y") + lax.axis_index("c")
    x2, tgt, ctx2 = x[0], loss_target[0], ctx[0]
    tm_in = min(512, L)
    tm = min(256, L)
    tt = min(1024, L)

    transposed = ("w_in", "w_ffn_in")
    tr = lambda kname, a: a.T if kname in transposed else a
    big = dict(w_in=w_in[0].T, w_branch_a=w_branch_a[0], w_branch_b=w_branch_b[0], w_out=w_out[0],
               w_ffn_in=w_ffn_in[0].T, w_ffn_out=w_ffn_out[0])
    col_sharded = ("w_branch_a", "w_branch_b")
    shard_bf = {k: a.astype(BF16) for k, a in big.items()}

    def assemble(kname, g):
        if kname in col_sharded:
            return g.transpose(1, 0, 2).reshape(g.shape[1], N_DEV * g.shape[2])
        return g.reshape(N_DEV * g.shape[1], g.shape[2])

    def to_blocks(kname, g):
        g = g.astype(BF16)
        if kname in col_sharded:
            return g.reshape(g.shape[0], N_DEV, g.shape[1] // N_DEV).transpose(1, 0, 2)
        return g.reshape(N_DEV, g.shape[0] // N_DEV, g.shape[1])

    full = {}
    full["w_in"] = assemble("w_in", _comm_only(_Comm(gather=[shard_bf["w_in"]]), "gather_w_in")[0])
    c_all = _ag_small(_pad_rows(c, 8), "gather_c")[:, 0, :]
    s_in = jnp.concatenate([c_all, c_ctx[None, :], jnp.zeros((7, D), F32)], axis=0)
    n_ada = w_ada.shape[2]
    b_my = lax.dynamic_slice(b_ada, (0, me * n_ada), (1, n_ada))
    act, mod_my = _ada_fwd(s_in, w_ada[0], b_my)
    mod_all = _ag_small(mod_my, "gather_mod").transpose(1, 0, 2).reshape(16, 6 * D)
    modv = _pad_rows(lax.dynamic_slice(mod_all, (me, 0), (1, 6 * D)).reshape(6, D), 8)
    modc = _pad_rows(mod_all[8].reshape(6, D), 8)

    lnv = _pad_rows(jnp.concatenate([ln1_g, ln1_b, ln2_g, ln2_b], axis=0), 8)
    gm_lnv = _pad_rows(jnp.concatenate([gmlp_ln_g, gmlp_ln_b], axis=0), 8)
    ws_b = w_spatial[0].astype(BF16)
    wst_b = ws_b.transpose(0, 2, 1)
    bst = b_spatial[0].T
    sink = attn_sink[0]
    cos, sin = _rope_tables(L)
    bias = _attn_bias()
    w_kv = full["w_in"][O_K:O_K + 2 * KV_W, :]

    (h, q, k, v, u, vb, ga, gb), got = _k_in(
        x2, modv, full["w_in"], cos, sin, tm_in,
        comm=_Comm(gather=[shard_bf["w_branch_a"], shard_bf["w_branch_b"], shard_bf["w_out"]]))
    for kname, g in zip(("w_branch_a", "w_branch_b", "w_out"), got):
        full[kname] = assemble(kname, g)
    hc, kc, vc = _k_ctx(ctx2, modc, w_kv)
    (ya, lse), got = _k_attn(sink, q, k, v, kc, vc, bias, comm=_Comm(gather=[shard_bf["w_ffn_in"], shard_bf["w_ffn_out"]]))
    for kname, g in zip(("w_ffn_in", "w_ffn_out"), got):
        full[kname] = assemble(kname, g)
    yb = _k_gmlp(u, vb, gm_lnv, ws_b, bst)
    merged, mix, xm, h2 = _k_merge(x2, ya, yb, ga, gb, full["w_branch_a"], full["w_branch_b"], full["w_out"], modv, lnv, tm_in)
    gate, up, act_f, dr2, df, acc_f = _k_ffn(h2, xm, tgt, full["w_ffn_in"], full["w_ffn_out"], modv, lnv, tm_in)

    dF, dmix, dxp, acc_b = _k_ffn_bwd(df, gate, up, xm, dr2, x2, mix, full["w_ffn_in"], full["w_ffn_out"], modv, lnv, tm)
    blk_fi = to_blocks("w_ffn_in", _wgrad(dF, h2, "wgrad_ffn_in", 1408, tt))
    blk_fo = to_blocks("w_ffn_out", _wgrad(act_f, df, "wgrad_ffn_out", 1408, tt))
    dA, dB, dga, dgb, dya, dyb = _k_merge_bwd(dmix, ya, yb, ga, gb, full["w_branch_a"], full["w_branch_b"], full["w_out"], tm_in)
    du, dvb, g_ws, g_bst, g_gln = _k_gmlp_bwd(u, vb, dyb, gm_lnv, ws_b, wst_b, bst)
    (dq, dk, dv, dkc, dvc, g_sink), (rcv_fi, rcv_fo, gath_ws) = _k_attn_bwd(
        sink, q, k, v, kc, vc, dya, lse, cos, sin, bias,
        comm=_Comm(scatter=[blk_fi, blk_fo], spread=[g_ws.reshape(SMALL_ROWS - 16, D)]))
    blk_a = to_blocks("w_branch_a", _wgrad(ya, dA, "wgrad_a", Q_W, tt))
    blk_b = to_blocks("w_branch_b", _wgrad(yb, dB, "wgrad_b", GM_W, tt))
    blk_o = to_blocks("w_out", _wgrad(merged, dmix, "wgrad_out", D, tt))
    (dP, grad_x, acc_i), _ = _k_in_bwd(dq, dk, dv, du, dvb, dga, dgb, x2, dxp, full["w_in"], modv, tm_in)
    gw_in, (rcv_a, rcv_b, rcv_o) = _wgrad(dP, h, "wgrad_in", 1280, tt, comm=_Comm(scatter=[blk_a, blk_b, blk_o]))
    gw_in, dmodc = _k_ctx_bwd(ctx2, modc, hc, dkc, dvc, w_kv, gw_in)

    dmod_x = jnp.concatenate([acc_i[0:2], acc_b[4:5], acc_b[0:2], acc_f[2:3]], axis=0)
    small = jnp.concatenate([
        dmod_x, dmodc[0:2], acc_b[2:4], acc_f[0:2],
        jnp.concatenate([g_gln[0:1], g_gln[1:2]], axis=1), g_bst.T.reshape(1, D),
        _pad_rows(g_sink[:, 0:1], D).T, acc_f[3:4]], axis=0)
    rcv_in, gath = _comm_only(_Comm(scatter=[to_blocks("w_in", gw_in)], spread=[small]), "exchange_last")
    received = dict(w_in=rcv_in, w_branch_a=rcv_a, w_branch_b=rcv_b, w_out=rcv_o, w_ffn_in=rcv_fi, w_ffn_out=rcv_fo)
    moments = dict(w_in=(m_w_in, v_w_in), w_branch_a=(m_w_branch_a, v_w_branch_a), w_branch_b=(m_w_branch_b, v_w_branch_b),
                   w_out=(m_w_out, v_w_out), w_ffn_in=(m_w_ffn_in, v_w_ffn_in), w_ffn_out=(m_w_ffn_out, v_w_ffn_out))
    names = list(big)
    res = {}
    for kname in names:
        mm, vv = moments[kname]
        R = big[kname].shape[0]
        res[kname] = [tr(kname, r) for r in _adamw_reduce(
            received[kname], big[kname], tr(kname, mm[0]), tr(kname, vv[0]), "adamw_" + kname, 256 if R % 256 == 0 else R // 2)]

    tot = _small_reduce(gath, gath_ws)
    g_small = _unpack_small(tot)
    loss = tot[15, 0]

    p_w = _pack_small(b_ada, ln1_g, ln1_b, ln2_g, ln2_b, gmlp_ln_g, gmlp_ln_b, b_spatial, attn_sink, w_spatial)
    p_m = _pack_small(m_b_ada, m_ln1_g, m_ln1_b, m_ln2_g, m_ln2_b, m_gmlp_ln_g, m_gmlp_ln_b, m_b_spatial, m_attn_sink, m_w_spatial)
    p_v = _pack_small(v_b_ada, v_ln1_g, v_ln1_b, v_ln2_g, v_ln2_b, v_gmlp_ln_g, v_gmlp_ln_b, v_b_spatial, v_attn_sink, v_w_spatial)
    s_d, s_m, s_v = [_unpack_small(t) for t in _small_adamw(p_w, tot, p_m, p_v, "adamw_small")]

    dmod_rows = jnp.concatenate([gath[:, 0:6, :].reshape(N_DEV, 6 * D),
                                 jnp.concatenate([tot[6:8].reshape(1, 2 * D), jnp.zeros((1, 4 * D), F32)], axis=1),
                                 jnp.zeros((7, 6 * D), F32)], axis=0)
    dmod_my = lax.dynamic_slice(dmod_rows, (0, me * n_ada), (16, n_ada))
    g_wada, d_wada, m2_wada, v2_wada, pc = _ada_bwd(act, dmod_my, w_ada[0], m_w_ada[0], v_w_ada[0])
    pc_all = _ag_small(pc, "gather_cctx")
    cc8 = lambda a: _pad_rows(a.reshape(1, D), 8)
    g_cc, d_cc, m2_cc, v2_cc = _cctx_finish(pc_all, cc8(c_ctx), cc8(m_c_ctx), cc8(v_c_ctx))

    order = ["c_ctx", "w_ada", "b_ada", "w_in", "attn_sink", "gmlp_ln_g", "gmlp_ln_b", "w_spatial", "b_spatial",
             "w_branch_a", "w_branch_b", "w_out", "ln1_g", "ln1_b", "w_ffn_in", "w_ffn_out", "ln2_g", "ln2_b"]
    grads, deltas, new_m, new_v = {}, {}, {}, {}
    grads["c_ctx"], deltas["c_ctx"], new_m["c_ctx"], new_v["c_ctx"] = g_cc[0], d_cc[0], m2_cc[0], v2_cc[0]
    grads["w_ada"], deltas["w_ada"], new_m["w_ada"], new_v["w_ada"] = g_wada[None], d_wada[None], m2_wada[None], v2_wada[None]
    for kname in names:
        g, d, m2, v2 = res[kname]
        grads[kname], deltas[kname], new_m[kname], new_v[kname] = g[None], d[None], m2[None], v2[None]
    for kname in ("b_ada", "attn_sink", "gmlp_ln_g", "gmlp_ln_b", "w_spatial", "b_spatial", "ln1_g", "ln1_b", "ln2_g", "ln2_b"):
        grads[kname], deltas[kname], new_m[kname], new_v[kname] = g_small[kname], s_d[kname], s_m[kname], s_v[kname]
    return (loss, grad_x[None], *[grads[n] for n in order], *[deltas[n] for n in order],
            *[new_m[n] for n in order], *[new_v[n] for n in order])
```

```python
import functools
import math

import jax
import jax.numpy as jnp
import numpy as np
from jax import lax
from jax.experimental import pallas as pl
from jax.experimental.pallas import tpu as pltpu

F32 = jnp.float32
BF16 = jnp.bfloat16
MESH = pl.DeviceIdType.MESH

N_DEV = 8
D = 1024
HEAD_DIM = 64
N_Q_HEADS = 8
N_KV_HEADS = 2
GQA_GROUP = 4
BLK = 128
Q_W = 512
KV_W = 128
GM_W = 512
N_GROUPS = 8
GROUP_DIM = 64
FFN_H = 2816
IN_W = 3840
O_Q, O_K, O_V, O_U, O_VB, O_GA, O_GB = 0, 512, 640, 768, 1280, 1792, 2816
LN_EPS = 1e-5
NEG_INF = -1e30
ALPHA = 2.0 ** 0.25
ROPE_BASE = 10000.0
ROPE_PAIRS = 16
Q_SCALE = HEAD_DIM ** -0.5
GELU_K0 = math.sqrt(2.0 / math.pi)
GELU_K1 = 0.044715

ADAM_LR = 0.001
ADAM_B1 = 0.9
ADAM_B2 = 0.999
ADAM_EPS = 1e-08
ADAM_WD = 0.01
ADAM_STEP = 10

VMEM_LIMIT = 56 * 1024 * 1024
SMALL_ROWS = 144
NT = (((1,), (1,)), ((), ()))
TN = (((0,), (0,)), ((), ()))


def _params(sem=None):
    return pltpu.CompilerParams(dimension_semantics=sem, vmem_limit_bytes=VMEM_LIMIT)


def _row(tm, w):
    return pl.BlockSpec((tm, w), lambda i: (i, 0))


def _full(shape):
    nd = len(shape)
    return pl.BlockSpec(shape, lambda i: (0,) * nd)


def _resident(shape):
    nd = len(shape)
    return pl.BlockSpec(shape, lambda i: (0,) * nd, pipeline_mode=pl.Buffered(1))


def _sds(shape, dt):
    return jax.ShapeDtypeStruct(shape, dt)


def _ln(xf):
    mu = jnp.mean(xf, axis=-1, keepdims=True)
    xc = xf - mu
    var = jnp.mean(xc * xc, axis=-1, keepdims=True)
    rstd = lax.rsqrt(var + LN_EPS)
    return xc * rstd, rstd


def _ln_bwd(dn, n, rstd):
    m1 = jnp.mean(dn, axis=-1, keepdims=True)
    m2 = jnp.mean(dn * n, axis=-1, keepdims=True)
    return rstd * (dn - m1 - n * m2)


def _colsum(t):
    return jnp.sum(t, axis=0, keepdims=True)


def _sigmoid(x):
    return 0.5 * jnp.tanh(0.5 * x) + 0.5


def _gelu(x):
    t = jnp.tanh(GELU_K0 * (x + GELU_K1 * (x * x * x)))
    return x * (0.5 * (1.0 + t)), t


def _gelu_grad(x, t):
    return 0.5 * (1.0 + t) + 0.5 * x * (1.0 - t * t) * (GELU_K0 * (1.0 + 3.0 * GELU_K1 * x * x))


def _swap16(t):
    lane = lax.broadcasted_iota(jnp.int32, t.shape, 1)
    return jnp.where((lane & 16) == 0, pltpu.roll(t, 112, 1), pltpu.roll(t, 16, 1))


def _rope(t, cos, sin):
    return t * cos + _swap16(t) * sin


def _unrope(t, cos, sin):
    return t * cos - _swap16(t) * sin


def _adamw(w, g, m, v):
    m2 = ADAM_B1 * m + (1.0 - ADAM_B1) * g
    v2 = ADAM_B2 * v + (1.0 - ADAM_B2) * (g * g)
    m_hat = m2 / (1.0 - ADAM_B1 ** ADAM_STEP)
    v_hat = v2 / (1.0 - ADAM_B2 ** ADAM_STEP)
    delta = -ADAM_LR * (m_hat / (jnp.sqrt(v_hat) + ADAM_EPS) + ADAM_WD * w)
    return delta, m2, v2


def _rope_tables(L):
    inv = (np.float32(ROPE_BASE) ** (-np.arange(ROPE_PAIRS, dtype=np.float32) / np.float32(ROPE_PAIRS))).astype(np.float32)
    t = np.arange(L, dtype=np.int32)
    rows = (t // 64).astype(np.float32)[:, None] * inv
    cols = (t % 64).astype(np.float32)[:, None] * inv
    cr, sr, cc, sc = np.cos(rows), np.sin(rows), np.cos(cols), np.sin(cols)
    cos = np.concatenate([cr, cr, cc, cc], axis=1)
    sin = np.concatenate([-sr, sr, -sc, sc], axis=1)
    return jnp.asarray(np.tile(cos, (1, 2)), F32), jnp.asarray(np.tile(sin, (1, 2)), F32)


def _me():
    return lax.axis_index("x"), lax.axis_index("y"), lax.axis_index("c")


def _peer(mx, my, mc, k):
    return (mx ^ ((k >> 2) & 1), my ^ ((k >> 1) & 1), mc ^ (k & 1))


def _ag_small(x, name):
    R, C = x.shape

    def body(x_ref, out_ref, send_sems, recv_sems):
        mx, my, mc = _me()
        me = 4 * mx + 2 * my + mc
        out_ref[pl.ds(me, 1)] = x_ref[...][None]
        sends = []
        for k in range(1, N_DEV):
            cp = pltpu.make_async_remote_copy(
                src_ref=x_ref, dst_ref=out_ref.at[me], send_sem=send_sems.at[k - 1], recv_sem=recv_sems.at[k - 1],
                device_id=_peer(mx, my, mc, k), device_id_type=MESH)
            cp.start()
            sends.append(cp)
        for k in range(1, N_DEV):
            pltpu.make_async_remote_copy(
                src_ref=x_ref, dst_ref=out_ref.at[me ^ k], send_sem=send_sems.at[k - 1], recv_sem=recv_sems.at[k - 1],
                device_id=(mx, my, mc), device_id_type=MESH).wait_recv()
        for cp in sends:
            cp.wait_send()

    return pl.pallas_call(
        body, name=name,
        out_shape=_sds((N_DEV, R, C), x.dtype),
        in_specs=[pl.BlockSpec(memory_space=pltpu.VMEM)],
        out_specs=pl.BlockSpec(memory_space=pltpu.VMEM),
        scratch_shapes=[pltpu.SemaphoreType.DMA((N_DEV - 1,)), pltpu.SemaphoreType.DMA((N_DEV - 1,))],
        compiler_params=pltpu.CompilerParams(vmem_limit_bytes=VMEM_LIMIT),
    )(x)


class _Comm:
    def __init__(self, gather=(), scatter=(), spread=()):
        self.kinds = ["gather"] * len(gather) + ["scatter"] * len(scatter) + ["spread"] * len(spread)
        self.args = list(gather) + list(scatter) + list(spread)
        self.n = len(self.args)

    def out_shape(self):
        return [_sds(a.shape if k == "scatter" else (N_DEV,) + a.shape, a.dtype) for k, a in zip(self.kinds, self.args)]

    def specs(self):
        return [pl.BlockSpec(memory_space=pl.ANY)] * self.n

    def scratch(self):
        return [pltpu.SemaphoreType.DMA((7 * self.n,)), pltpu.SemaphoreType.DMA((7 * self.n,)),
                pltpu.SemaphoreType.DMA((self.n,))]

    def _plan(self, x_refs, out_refs, send_sems, recv_sems, local_sems):
        mx, my, mc = _me()
        me = 4 * mx + 2 * my + mc
        here, sibling = (mx, my, mc), (mx, my, 1 - mc)
        chips = [(1 - mx, my), (mx, 1 - my), (1 - mx, 1 - my)]
        local, first, last = [], [], []
        relay = [[], [], []]
        for a, kind in enumerate(self.kinds):
            x, out = x_refs[a], out_refs[a]

            def rc(k, src, dst, to):
                return pltpu.make_async_remote_copy(
                    src_ref=src, dst_ref=dst, send_sem=send_sems.at[7 * a + k], recv_sem=recv_sems.at[7 * a + k],
                    device_id=to, device_id_type=MESH)

            if kind == "gather":
                local.append(pltpu.make_async_copy(x, out.at[me], local_sems.at[a]))
                first.append(rc(0, x, out.at[me], sibling))
                last.append(rc(0, x, out.at[me ^ 1], here))
                for j, (cx, cy) in enumerate(chips):
                    first.append(rc(1 + j, x, out.at[me], (cx, cy, mc)))
                    landed = out.at[4 * cx + 2 * cy + mc]
                    relay[j].append((rc(1 + j, x, landed, here), rc(4 + j, landed, landed, sibling)))
                    last.append(rc(4 + j, x, out.at[4 * cx + 2 * cy + 1 - mc], here))
            else:
                own = x.at[me] if kind == "scatter" else x
                local.append(pltpu.make_async_copy(own, out.at[me], local_sems.at[a]))
                for k in range(1, N_DEV):
                    src = x.at[me ^ k] if kind == "scatter" else x
                    first.append(rc(k - 1, src, out.at[me], _peer(mx, my, mc, k)))
                    last.append(rc(k - 1, own, out.at[me ^ k], here))
        return local, first, relay[0] + relay[1] + relay[2], last

    def start(self, *refs):
        local, first, _, _ = self._plan(*refs)
        for cp in local + first:
            cp.start()

    def finish(self, *refs):
        local, first, relay, last = self._plan(*refs)
        for arrival, onward in relay:
            arrival.wait_recv()
            onward.start()
        for cp in last:
            cp.wait_recv()
        for cp in first:
            cp.wait_send()
        for _, onward in relay:
            onward.wait_send()
        for cp in local:
            cp.wait()


def _call(body, *, name, grid, in_specs, out_specs, out_shape, args, scratch=(), comm=None, aliases=None):
    params = _params(("arbitrary",) * len(grid))

    def at(end):
        conds = [pl.program_id(d) == (n - 1 if end else 0) for d, n in enumerate(grid)]
        return functools.reduce(lambda p, q: p & q, conds)

    if comm is None:
        res = pl.pallas_call(
            body, name=name, grid=grid, in_specs=list(in_specs), out_specs=list(out_specs), out_shape=list(out_shape),
            scratch_shapes=list(scratch), input_output_aliases=aliases or {}, compiler_params=params)(*args)
        return list(res), []
    n_in, n_out, n_scr, cn = len(in_specs), len(out_specs), len(scratch), comm.n

    def hosted(*refs):
        ins, refs = refs[:n_in], refs[n_in:]
        cins, refs = refs[:cn], refs[cn:]
        outs, refs = refs[:n_out], refs[n_out:]
        couts, refs = refs[:cn], refs[cn:]
        scr, sems = refs[:n_scr], refs[n_scr:]

        @pl.when(at(False))
        def _():
            comm.start(cins, couts, *sems)

        body(*ins, *outs, *scr)

        @pl.when(at(True))
        def _():
            comm.finish(cins, couts, *sems)

    res = pl.pallas_call(
        hosted, name=name, grid=grid, in_specs=list(in_specs) + comm.specs(), out_specs=list(out_specs) + comm.specs(),
        out_shape=list(out_shape) + comm.out_shape(), scratch_shapes=list(scratch) + comm.scratch(),
        input_output_aliases=aliases or {}, compiler_params=params)(*args, *comm.args)
    return list(res[:n_out]), list(res[n_out:])


def _comm_only(comm, name):
    return _call(lambda: None, name=name, grid=(1,), in_specs=[], out_specs=[], out_shape=[], args=[], comm=comm)[1]


def _exchange_rows(x_ref, out_ref, send_sems, recv_sems):
    mx, my, mc = _me()
    me = 4 * mx + 2 * my + mc
    out_ref[pl.ds(me, 1)] = x_ref[...][None]
    sends = []
    for k in range(1, N_DEV):
        cp = pltpu.make_async_remote_copy(
            src_ref=x_ref, dst_ref=out_ref.at[me], send_sem=send_sems.at[k - 1], recv_sem=recv_sems.at[k - 1],
            device_id=_peer(mx, my, mc, k), device_id_type=MESH)
        cp.start()
        sends.append(cp)
    for k in range(1, N_DEV):
        pltpu.make_async_remote_copy(
            src_ref=x_ref, dst_ref=out_ref.at[me ^ k], send_sem=send_sems.at[k - 1], recv_sem=recv_sems.at[k - 1],
            device_id=(mx, my, mc), device_id_type=MESH).wait_recv()
    for cp in sends:
        cp.wait_send()


def _prologue(c8, cctx8, w_ada, b_my, comm):
    nw = w_ada.shape[1]

    def body(c_ref, cctx_ref, w_ref, b_ref, act_ref, mod_ref, cmine_scr, call_scr, mine_scr, mall_scr, s1, r1, s2, r2):
        cmine_scr[...] = c_ref[...]
        _exchange_rows(cmine_scr, call_scr, s1, r1)
        rows = [call_scr[d][0:1, :] for d in range(N_DEV)] + [cctx_ref[0:1, :], jnp.zeros((7, D), F32)]
        s = jnp.concatenate(rows, axis=0)
        act = s * _sigmoid(s)
        act_ref[...] = act
        mine_scr[...] = jnp.dot(act.astype(BF16), w_ref[...].astype(BF16), preferred_element_type=F32) + b_ref[...]
        _exchange_rows(mine_scr, mall_scr, s2, r2)
        mod_ref[...] = mall_scr[...]

    sems = [pltpu.SemaphoreType.DMA((N_DEV - 1,))] * 4
    (act, mod), got = _call(
        body, name="prologue", grid=(1,),
        in_specs=[_full((8, D)), _full((8, D)), _full((D, nw)), _full((1, nw))],
        out_specs=[_full((16, D)), _full((N_DEV, 16, nw))],
        out_shape=[_sds((16, D), F32), _sds((N_DEV, 16, nw), F32)],
        scratch=[pltpu.VMEM((8, D), F32), pltpu.VMEM((N_DEV, 8, D), F32), pltpu.VMEM((16, nw), F32),
                 pltpu.VMEM((N_DEV, 16, nw), F32)] + sems,
        args=(c8, cctx8, w_ada, b_my), comm=comm)
    return act, mod, got


def _ada_bwd(act, dmod_my, w_ada, m, v, tr=256):
    nw = w_ada.shape[1]

    def body(act_ref, dm_ref, w_ref, m_ref, v_ref, g_ref, d_ref, m2_ref, v2_ref, pc_ref):
        dm = dm_ref[...].astype(BF16)
        g = lax.dot_general(act_ref[...].astype(BF16), dm, TN, preferred_element_type=F32)
        w = w_ref[...]
        delta, m2, v2 = _adamw(w, g, m_ref[...], v_ref[...])
        g_ref[...] = g
        d_ref[...] = delta
        m2_ref[...] = m2
        v2_ref[...] = v2
        pc_ref[...] = lax.dot_general(dm[8:16, :], w.astype(BF16), NT, preferred_element_type=F32)

    wspec = _row(tr, nw)
    return pl.pallas_call(
        body, name="ada_bwd", grid=(D // tr,),
        in_specs=[pl.BlockSpec((16, tr), lambda i: (0, i)), _full((16, nw)), wspec, wspec, wspec],
        out_specs=[wspec, wspec, wspec, wspec, pl.BlockSpec((8, tr), lambda i: (0, i))],
        out_shape=[_sds((D, nw), F32)] * 4 + [_sds((8, D), F32)],
        compiler_params=_params(("arbitrary",)),
    )(act, dmod_my, w_ada, m, v)


def _k_in(x, modv, w_in, cos, sin, tm, comm=None):
    L = x.shape[0]

    def body(x_ref, mod_ref, w_ref, cos_ref, sin_ref, h_ref, q_ref, k_ref, v_ref, u_ref, vb_ref, ga_ref, gb_ref):
        n, _ = _ln(x_ref[...])
        h = (n * (1.0 + mod_ref[1:2, :]) + mod_ref[0:1, :]).astype(BF16)
        h_ref[...] = h
        c, s = cos_ref[...], sin_ref[...]

        def proj(lo, width):
            return lax.dot_general(h, w_ref[lo:lo + width, :], NT, preferred_element_type=F32)

        for i in range(4):
            q_ref[:, i * 128:(i + 1) * 128] = (_rope(proj(O_Q + i * 128, 128), c, s) * Q_SCALE).astype(BF16)
        k_ref[...] = _rope(proj(O_K, KV_W), c, s).astype(BF16)
        v_ref[...] = proj(O_V, KV_W).astype(BF16)
        u_ref[...] = proj(O_U, GM_W).astype(BF16)
        vb_ref[...] = proj(O_VB, GM_W).astype(BF16)
        ga_ref[...] = proj(O_GA, D).astype(BF16)
        gb_ref[...] = proj(O_GB, D).astype(BF16)

    widths = [D, Q_W, KV_W, KV_W, GM_W, GM_W, D, D]
    return _call(
        body, name="fwd_in", grid=(L // tm,),
        in_specs=[_row(tm, D), _full((8, D)), _resident((IN_W, D)), _row(tm, 128), _row(tm, 128)],
        out_specs=[_row(tm, w) for w in widths],
        out_shape=[_sds((L, w), BF16) for w in widths],
        args=(x, modv, w_in, cos, sin), comm=comm)


def _k_ctx(ctx, modc, w_kv):
    C = ctx.shape[0]

    def body(c_ref, mod_ref, w_ref, hc_ref, kc_ref, vc_ref):
        n, _ = _ln(c_ref[...])
        hc = (n * (1.0 + mod_ref[1:2, :]) + mod_ref[0:1, :]).astype(BF16)
        hc_ref[...] = hc
        kv = lax.dot_general(hc, w_ref[...], NT, preferred_element_type=F32)
        kc_ref[...] = kv[:, :KV_W].astype(BF16)
        vc_ref[...] = kv[:, KV_W:].astype(BF16)

    return pl.pallas_call(
        body, name="fwd_ctx", grid=(1,),
        in_specs=[_full((C, D)), _full((8, D)), _full((2 * KV_W, D))],
        out_specs=[_full((C, D)), _full((C, KV_W)), _full((C, KV_W))],
        out_shape=[_sds((C, D), BF16), _sds((C, KV_W), BF16), _sds((C, KV_W), BF16)],
        compiler_params=_params(("arbitrary",)),
    )(ctx, modc, w_kv)


def _attn_bias():
    r = (np.arange(GQA_GROUP * BLK) & (BLK - 1))[:, None]
    j = np.arange(3 * BLK)[None, :]
    band = np.abs(j - BLK - r) <= BLK
    variants = [band & (j >= BLK), band, band & (j < 2 * BLK)]
    return jnp.asarray(np.stack([np.where(v, 0.0, NEG_INF) for v in variants]), F32)


def _bias_spec(nb):
    return pl.BlockSpec((1, GQA_GROUP * BLK, 3 * BLK),
                        lambda n: (jnp.where(n == 0, 0, jnp.where(n >= nb - 1, 2, 1)), 0, 0))


def _masked(s, bias, C):
    return jnp.concatenate([s[:, :C], s[:, C:] + bias], axis=1)


def _sink_col(sink_ref, hk):
    grp = lax.broadcasted_iota(jnp.int32, (GQA_GROUP * BLK, 1), 0) >> 7
    col = jnp.full((GQA_GROUP * BLK, 1), sink_ref[hk * GQA_GROUP], F32)
    for g in range(1, GQA_GROUP):
        col = jnp.where(grp == g, sink_ref[hk * GQA_GROUP + g], col)
    return col


def _kv_specs(nb):
    prev = pl.BlockSpec((BLK, KV_W), lambda n: (jnp.clip(n - 1, 0, nb - 1), 0))
    cur = pl.BlockSpec((BLK, KV_W), lambda n: (jnp.minimum(n, nb - 1), 0))
    nxt = pl.BlockSpec((BLK, KV_W), lambda n: (jnp.minimum(n + 1, nb - 1), 0))
    return [prev, cur, nxt]


def _k_attn(sink, q, k, v, kc, vc, bias, comm=None):
    L = q.shape[0]
    C = kc.shape[0]
    nb = L // BLK

    def body(sink_ref, q_ref, kp_ref, kn_ref, kx_ref, vp_ref, vn_ref, vx_ref, kc_ref, vc_ref, bias_ref, ya_ref, lse_ref):
        band = bias_ref[0]
        kvh = range(N_KV_HEADS)
        sl = [slice(hk * HEAD_DIM, (hk + 1) * HEAD_DIM) for hk in kvh]
        kcat = [jnp.concatenate([kc_ref[:, s_], kp_ref[:, s_], kn_ref[:, s_], kx_ref[:, s_]], axis=0) for s_ in sl]
        vcat = [jnp.concatenate([vc_ref[:, s_], vp_ref[:, s_], vn_ref[:, s_], vx_ref[:, s_]], axis=0) for s_ in sl]
        qg = [jnp.concatenate(
            [q_ref[:, (hk * GQA_GROUP + g) * HEAD_DIM:(hk * GQA_GROUP + g + 1) * HEAD_DIM] for g in range(GQA_GROUP)],
            axis=0) for hk in kvh]
        s = [_masked(lax.dot_general(qg[hk], kcat[hk], NT, preferred_element_type=F32), band, C) for hk in kvh]
        sink_c = [_sink_col(sink_ref, hk) for hk in kvh]
        m = [jnp.maximum(jnp.max(s[hk], axis=1, keepdims=True), sink_c[hk]) for hk in kvh]
        p = [jnp.exp(s[hk] - m[hk]) for hk in kvh]
        den = [jnp.sum(p[hk], axis=1, keepdims=True) + jnp.exp(sink_c[hk] - m[hk]) for hk in kvh]
        o = [jnp.dot(p[hk].astype(BF16), vcat[hk], preferred_element_type=F32) * (1.0 / den[hk]) for hk in kvh]
        for hk in kvh:
            lse = m[hk] + jnp.log(den[hk])
            for g in range(GQA_GROUP):
                h = hk * GQA_GROUP + g
                ya_ref[:, h * HEAD_DIM:(h + 1) * HEAD_DIM] = o[hk][g * BLK:(g + 1) * BLK, :].astype(BF16)
                lse_ref[:, h:h + 1] = lse[g * BLK:(g + 1) * BLK, :]

    kv3 = _kv_specs(nb)
    return _call(
        body, name="fwd_attn", grid=(nb,),
        in_specs=[pl.BlockSpec(memory_space=pltpu.SMEM), _row(BLK, Q_W)] + kv3 + kv3
                 + [_full((C, KV_W)), _full((C, KV_W)), _bias_spec(nb)],
        out_specs=[_row(BLK, Q_W), _row(BLK, N_Q_HEADS)],
        out_shape=[_sds((L, Q_W), BF16), _sds((L, N_Q_HEADS), F32)],
        args=(sink, q, k, k, k, v, v, v, kc, vc, bias), comm=comm)


GMLP_CHUNKS = 4


def _split_pair(t):
    low = lax.broadcasted_iota(jnp.int32, t.shape, 1) < GROUP_DIM
    zero = jnp.zeros_like(t)
    return jnp.where(low, t, zero), jnp.where(low, zero, t)


def _gmlp_spatial(w_ref, t_b, nch):
    rows = []
    for c in range(nch):
        tiles = []
        for pr in range(N_GROUPS // 2):
            lo, hi = _split_pair(t_b[c * BLK:(c + 1) * BLK, pr * 128:(pr + 1) * 128])
            tiles.append(jnp.dot(w_ref[2 * pr], lo, preferred_element_type=F32)
                         + jnp.dot(w_ref[2 * pr + 1], hi, preferred_element_type=F32))
        rows.append(jnp.concatenate(tiles, axis=1))
    return jnp.concatenate(rows, axis=0)


def _gmlp_fwd_vals(u, vb, lnv_ref, ws_ref, bsp_ref, nch):
    uf = u.astype(F32)
    vf = vb.astype(F32)
    gu, tu = _gelu(uf)
    gv, tv = _gelu(vf)
    vhat, rstd = _ln(gv)
    vn = (vhat * lnv_ref[0:1, :] + lnv_ref[1:2, :]).astype(BF16)
    s = _gmlp_spatial(ws_ref, vn, nch) + jnp.concatenate([bsp_ref[...]] * nch, axis=0)
    return uf, vf, gu, tu, tv, vhat, rstd, vn, s


def _k_gmlp(u, vb, lnv, ws, bsp):
    L = u.shape[0]
    nch = min(GMLP_CHUNKS, L // BLK)
    tm = nch * BLK

    def body(u_ref, vb_ref, lnv_ref, ws_ref, bsp_ref, yb_ref):
        _, _, gu, _, _, _, _, _, s = _gmlp_fwd_vals(u_ref[...], vb_ref[...], lnv_ref, ws_ref, bsp_ref, nch)
        yb_ref[...] = (gu * s).astype(BF16)

    return pl.pallas_call(
        body, name="fwd_gmlp", grid=(L // tm,),
        in_specs=[_row(tm, GM_W), _row(tm, GM_W), _full((8, GM_W)), _full((N_GROUPS, BLK, BLK)), _full((BLK, GM_W))],
        out_specs=_row(tm, GM_W),
        out_shape=_sds((L, GM_W), BF16),
        compiler_params=_params(("arbitrary",)),
    )(u, vb, lnv, ws, bsp)


def _k_merge(x, ya, yb, ga, gb, w_a, w_b, w_o, modv, lnv, tm):
    L = x.shape[0]

    def body(x_ref, ya_ref, yb_ref, ga_ref, gb_ref, wa_ref, wb_ref, wo_ref, mod_ref, ln_ref,
             mg_ref, mix_ref, xm_ref, h2_ref):
        a = jnp.dot(ya_ref[...], wa_ref[...], preferred_element_type=F32)
        b = jnp.dot(yb_ref[...], wb_ref[...], preferred_element_type=F32)
        merged = (_sigmoid(ga_ref[...].astype(F32)) * a + _sigmoid(gb_ref[...].astype(F32)) * b).astype(BF16)
        mg_ref[...] = merged
        mix = jnp.dot(merged, wo_ref[...], preferred_element_type=F32)
        mix_ref[...] = mix.astype(BF16)
        r1 = ALPHA * x_ref[...] + mod_ref[2:3, :] * mix
        r1hat, _ = _ln(r1)
        xm = r1hat * ln_ref[0:1, :] + ln_ref[1:2, :]
        xm_ref[...] = xm
        n2, _ = _ln(xm)
        h2_ref[...] = (n2 * (1.0 + mod_ref[4:5, :]) + mod_ref[3:4, :]).astype(BF16)

    return pl.pallas_call(
        body, name="fwd_merge", grid=(L // tm,),
        in_specs=[_row(tm, D), _row(tm, Q_W), _row(tm, GM_W), _row(tm, D), _row(tm, D),
                  _resident((Q_W, D)), _resident((GM_W, D)), _resident((D, D)), _full((8, D)), _full((8, D))],
        out_specs=[_row(tm, D)] * 4,
        out_shape=[_sds((L, D), BF16), _sds((L, D), BF16), _sds((L, D), F32), _sds((L, D), BF16)],
        compiler_params=_params(("arbitrary",)),
    )(x, ya, yb, ga, gb, w_a, w_b, w_o, modv, lnv)


FFN_CH = 1408


def _k_ffn(h2, xm, tgt, w_fi, w_fo, modv, lnv, tm):
    L = h2.shape[0]

    def body(h2_ref, xm_ref, t_ref, wi_ref, wo_ref, mod_ref, ln_ref, gate_ref, up_ref, a_ref, dr2_ref, df_ref, acc_ref):
        @pl.when(pl.program_id(0) == 0)
        def _():
            acc_ref[...] = jnp.zeros_like(acc_ref)

        h2v = h2_ref[...]
        f = jnp.zeros((tm, D), F32)
        for j in range(FFN_H // FFN_CH):
            lo = j * FFN_CH
            gate = lax.dot_general(h2v, wi_ref[lo:lo + FFN_CH, :], NT, preferred_element_type=F32)
            up = lax.dot_general(h2v, wi_ref[FFN_H + lo:FFN_H + lo + FFN_CH, :], NT, preferred_element_type=F32)
            act = (gate * _sigmoid(gate) * up).astype(BF16)
            gate_ref[:, lo:lo + FFN_CH] = gate.astype(BF16)
            up_ref[:, lo:lo + FFN_CH] = up.astype(BF16)
            a_ref[:, lo:lo + FFN_CH] = act
            f = f + jnp.dot(act, wo_ref[lo:lo + FFN_CH, :], preferred_element_type=F32)
        gate2 = mod_ref[5:6, :]
        r2 = ALPHA * xm_ref[...] + gate2 * f
        r2hat, rstd = _ln(r2)
        y = r2hat * ln_ref[2:3, :] + ln_ref[3:4, :]
        err = y - t_ref[...]
        dy = err * (1.0 / D)
        dr2 = _ln_bwd(dy * ln_ref[2:3, :], r2hat, rstd)
        dr2_ref[...] = dr2
        df_ref[...] = (gate2 * dr2).astype(BF16)
        acc_ref[0:1, :] += _colsum(dy * r2hat)
        acc_ref[1:2, :] += _colsum(dy)
        acc_ref[2:3, :] += _colsum(dr2 * f)
        acc_ref[3:4, :] += _colsum(err * err) * (0.5 / D)

    return pl.pallas_call(
        body, name="fwd_ffn", grid=(L // tm,),
        in_specs=[_row(tm, D), _row(tm, D), _row(tm, D), _resident((2 * FFN_H, D)), _resident((FFN_H, D)),
                  _full((8, D)), _full((8, D))],
        out_specs=[_row(tm, FFN_H)] * 3 + [_row(tm, D), _row(tm, D), _full((8, D))],
        out_shape=[_sds((L, FFN_H), BF16)] * 3 + [_sds((L, D), F32), _sds((L, D), BF16), _sds((8, D), F32)],
        compiler_params=_params(("arbitrary",)),
    )(h2, xm, tgt, w_fi, w_fo, modv, lnv)


def _k_ffn_bwd(df, gate, up, xm, dr2, x, mix, w_fi, w_fo, modv, lnv, tm):
    L = df.shape[0]

    def body(df_ref, gate_ref, up_ref, xm_ref, dr2_ref, x_ref, mix_ref, wi_ref, wo_ref, mod_ref, ln_ref,
             dF_ref, dmix_ref, dxp_ref, acc_ref):
        @pl.when(pl.program_id(0) == 0)
        def _():
            acc_ref[...] = jnp.zeros_like(acc_ref)

        dfv = df_ref[...]
        dh2 = jnp.zeros((tm, D), F32)
        for j in range(FFN_H // FFN_CH):
            lo = j * FFN_CH
            da = lax.dot_general(dfv, wo_ref[lo:lo + FFN_CH, :], NT, preferred_element_type=F32)
            gate = gate_ref[:, lo:lo + FFN_CH].astype(F32)
            upv = up_ref[:, lo:lo + FFN_CH].astype(F32)
            sg = _sigmoid(gate)
            d_gate = (da * upv * (sg * (1.0 + gate * (1.0 - sg)))).astype(BF16)
            d_up = (da * (gate * sg)).astype(BF16)
            dF_ref[:, lo:lo + FFN_CH] = d_gate
            dF_ref[:, FFN_H + lo:FFN_H + lo + FFN_CH] = d_up
            dh2 = dh2 + jnp.dot(d_gate, wi_ref[lo:lo + FFN_CH, :], preferred_element_type=F32)
            dh2 = dh2 + jnp.dot(d_up, wi_ref[FFN_H + lo:FFN_H + lo + FFN_CH, :], preferred_element_type=F32)
        n2, rstd2 = _ln(xm_ref[...])
        acc_ref[0:1, :] += _colsum(dh2)
        acc_ref[1:2, :] += _colsum(dh2 * n2)
        dxm = ALPHA * dr2_ref[...] + _ln_bwd(dh2 * (1.0 + mod_ref[4:5, :]), n2, rstd2)
        mixf = mix_ref[...].astype(F32)
        gate1 = mod_ref[2:3, :]
        r1hat, rstd1 = _ln(ALPHA * x_ref[...] + gate1 * mixf)
        acc_ref[2:3, :] += _colsum(dxm * r1hat)
        acc_ref[3:4, :] += _colsum(dxm)
        dr1 = _ln_bwd(dxm * ln_ref[0:1, :], r1hat, rstd1)
        dmix_ref[...] = (gate1 * dr1).astype(BF16)
        dxp_ref[...] = ALPHA * dr1
        acc_ref[4:5, :] += _colsum(dr1 * mixf)

    return pl.pallas_call(
        body, name="bwd_ffn", grid=(L // tm,),
        in_specs=[_row(tm, D), _row(tm, FFN_H), _row(tm, FFN_H), _row(tm, D), _row(tm, D), _row(tm, D), _row(tm, D),
                  _resident((2 * FFN_H, D)), _resident((FFN_H, D)), _full((8, D)), _full((8, D))],
        out_specs=[_row(tm, 2 * FFN_H), _row(tm, D), _row(tm, D), _full((8, D))],
        out_shape=[_sds((L, 2 * FFN_H), BF16), _sds((L, D), BF16), _sds((L, D), F32), _sds((8, D), F32)],
        compiler_params=_params(("arbitrary",)),
    )(df, gate, up, xm, dr2, x, mix, w_fi, w_fo, modv, lnv)


def _k_merge_bwd(dmix, ya, yb, ga, gb, w_a, w_b, w_o, tm):
    L = dmix.shape[0]

    def body(dmix_ref, ya_ref, yb_ref, ga_ref, gb_ref, wa_ref, wb_ref, wo_ref,
             dA_ref, dB_ref, dga_ref, dgb_ref, dya_ref, dyb_ref):
        dmg = lax.dot_general(dmix_ref[...], wo_ref[...], NT, preferred_element_type=F32)
        a = jnp.dot(ya_ref[...], wa_ref[...], preferred_element_type=F32)
        sa = _sigmoid(ga_ref[...].astype(F32))
        dA = (dmg * sa).astype(BF16)
        dA_ref[...] = dA
        dga_ref[...] = (dmg * a * (sa * (1.0 - sa))).astype(BF16)
        dya_ref[...] = lax.dot_general(dA, wa_ref[...], NT, preferred_element_type=F32).astype(BF16)
        b = jnp.dot(yb_ref[...], wb_ref[...], preferred_element_type=F32)
        sb = _sigmoid(gb_ref[...].astype(F32))
        dB = (dmg * sb).astype(BF16)
        dB_ref[...] = dB
        dgb_ref[...] = (dmg * b * (sb * (1.0 - sb))).astype(BF16)
        dyb_ref[...] = lax.dot_general(dB, wb_ref[...], NT, preferred_element_type=F32).astype(BF16)

    return pl.pallas_call(
        body, name="bwd_merge", grid=(L // tm,),
        in_specs=[_row(tm, D), _row(tm, Q_W), _row(tm, GM_W), _row(tm, D), _row(tm, D),
                  _resident((Q_W, D)), _resident((GM_W, D)), _resident((D, D))],
        out_specs=[_row(tm, D)] * 4 + [_row(tm, Q_W), _row(tm, GM_W)],
        out_shape=[_sds((L, D), BF16)] * 4 + [_sds((L, Q_W), BF16), _sds((L, GM_W), BF16)],
        compiler_params=_params(("arbitrary",)),
    )(dmix, ya, yb, ga, gb, w_a, w_b, w_o)


def _k_gmlp_bwd(u, vb, dyb, lnv, ws, wst, bsp):
    L = u.shape[0]
    nch = min(GMLP_CHUNKS, L // BLK)
    tm = nch * BLK

    def body(u_ref, vb_ref, dyb_ref, lnv_ref, ws_ref, wst_ref, bsp_ref, du_ref, dvb_ref, gws_ref, gbst_ref, gln_ref):
        @pl.when(pl.program_id(0) == 0)
        def _():
            gws_ref[...] = jnp.zeros_like(gws_ref)
            gbst_ref[...] = jnp.zeros_like(gbst_ref)
            gln_ref[...] = jnp.zeros_like(gln_ref)

        uf, vf, gu, tu, tv, vhat, rstd, vn, s = _gmlp_fwd_vals(u_ref[...], vb_ref[...], lnv_ref, ws_ref, bsp_ref, nch)
        dyb_f = dyb_ref[...].astype(F32)
        du_ref[...] = (dyb_f * s * _gelu_grad(uf, tu)).astype(BF16)
        ds = dyb_f * gu
        ds_b = ds.astype(BF16)
        for pr in range(N_GROUPS // 2):
            lanes = slice(pr * 128, (pr + 1) * 128)
            gw_lo = gw_hi = ds_sum = None
            for c in range(nch):
                rows = slice(c * BLK, (c + 1) * BLK)
                lo, hi = _split_pair(ds_b[rows, lanes])
                t_lo = lax.dot_general(lo, vn[rows, lanes], NT, preferred_element_type=F32)
                t_hi = lax.dot_general(hi, vn[rows, lanes], NT, preferred_element_type=F32)
                gw_lo = t_lo if c == 0 else gw_lo + t_lo
                gw_hi = t_hi if c == 0 else gw_hi + t_hi
                ds_sum = ds[rows, lanes] if c == 0 else ds_sum + ds[rows, lanes]
            gws_ref[2 * pr] += gw_lo
            gws_ref[2 * pr + 1] += gw_hi
            b_lo, b_hi = _split_pair(ds_sum)
            gbst_ref[:, 2 * pr:2 * pr + 1] += jnp.sum(b_lo, axis=1, keepdims=True)
            gbst_ref[:, 2 * pr + 1:2 * pr + 2] += jnp.sum(b_hi, axis=1, keepdims=True)
        dvn = _gmlp_spatial(wst_ref, ds_b, nch)
        gln_ref[0:1, :] += _colsum(dvn * vhat)
        gln_ref[1:2, :] += _colsum(dvn)
        dgv = _ln_bwd(dvn * lnv_ref[0:1, :], vhat, rstd)
        dvb_ref[...] = (dgv * _gelu_grad(vf, tv)).astype(BF16)

    return pl.pallas_call(
        body, name="bwd_gmlp", grid=(L // tm,),
        in_specs=[_row(tm, GM_W)] * 3 + [_full((8, GM_W)), _full((N_GROUPS, BLK, BLK)), _full((N_GROUPS, BLK, BLK)),
                                         _full((BLK, GM_W))],
        out_specs=[_row(tm, GM_W), _row(tm, GM_W), _full((N_GROUPS, BLK, BLK)), _full((BLK, N_GROUPS)), _full((8, GM_W))],
        out_shape=[_sds((L, GM_W), BF16), _sds((L, GM_W), BF16), _sds((N_GROUPS, BLK, BLK), F32),
                   _sds((BLK, N_GROUPS), F32), _sds((8, GM_W), F32)],
        compiler_params=_params(("arbitrary",)),
    )(u, vb, dyb, lnv, ws, wst, bsp)


def _k_attn_bwd(sink, q, k, v, kc, vc, dya, lse, cos, sin, bias, comm=None):
    L = q.shape[0]
    C = kc.shape[0]
    nb = L // BLK
    NK = C + 3 * BLK

    def body(sink_ref, q_ref, kp_ref, kn_ref, kx_ref, vp_ref, vn_ref, vx_ref, kc_ref, vc_ref, do_ref, lse_ref,
             cq_ref, sq_ref, ck_ref, sk_ref, bias_ref,
             dq_ref, dk_ref, dv_ref, dkc_ref, dvc_ref, dsink_ref,
             dq_scr, ck_scr, cv_scr, kp_acc, kc_acc, vp_acc, vc_acc):
        n = pl.program_id(0)

        @pl.when(n == 0)
        def _():
            for r in (kp_acc, kc_acc, vp_acc, vc_acc, dkc_ref, dvc_ref, dsink_ref):
                r[...] = jnp.zeros_like(r)

        @pl.when(n < nb)
        def _():
            band = bias_ref[0]
            kvh = range(N_KV_HEADS)
            sl = [slice(hk * HEAD_DIM, (hk + 1) * HEAD_DIM) for hk in kvh]
            heads = [[hk * GQA_GROUP + g for g in range(GQA_GROUP)] for hk in kvh]
            kcat = [jnp.concatenate([kc_ref[:, s_], kp_ref[:, s_], kn_ref[:, s_], kx_ref[:, s_]], axis=0) for s_ in sl]
            vcat = [jnp.concatenate([vc_ref[:, s_], vp_ref[:, s_], vn_ref[:, s_], vx_ref[:, s_]], axis=0) for s_ in sl]
            qg = [jnp.concatenate([q_ref[:, h * HEAD_DIM:(h + 1) * HEAD_DIM] for h in heads[hk]], axis=0) for hk in kvh]
            dog = [jnp.concatenate([do_ref[:, h * HEAD_DIM:(h + 1) * HEAD_DIM] for h in heads[hk]], axis=0) for hk in kvh]
            lse_c = [jnp.concatenate([lse_ref[:, h:h + 1] for h in heads[hk]], axis=0) for hk in kvh]
            s = [_masked(lax.dot_general(qg[hk], kcat[hk], NT, preferred_element_type=F32), band, C) for hk in kvh]
            dp = [lax.dot_general(dog[hk], vcat[hk], NT, preferred_element_type=F32) for hk in kvh]
            p = [jnp.exp(s[hk] - lse_c[hk]) for hk in kvh]
            delta = [jnp.sum(p[hk] * dp[hk], axis=1, keepdims=True) for hk in kvh]
            ds = [(p[hk] * (dp[hk] - delta[hk])).astype(BF16) for hk in kvh]
            dqs = [jnp.dot(ds[hk], kcat[hk], preferred_element_type=F32) for hk in kvh]
            for hk in kvh:
                p_sink = jnp.exp(_sink_col(sink_ref, hk) - lse_c[hk]) * delta[hk]
                for g, h in enumerate(heads[hk]):
                    dq_scr[:, h * HEAD_DIM:(h + 1) * HEAD_DIM] = dqs[hk][g * BLK:(g + 1) * BLK, :]
                    tot = jnp.sum(p_sink[g * BLK:(g + 1) * BLK, :], axis=0, keepdims=True)
                    dsink_ref[h:h + 1, :] -= jnp.broadcast_to(tot, (1, 128))
            for hk in kvh:
                ck_scr[:, sl[hk]] = lax.dot_general(ds[hk], qg[hk], TN, preferred_element_type=F32)
            cq, sq = cq_ref[...], sq_ref[...]
            for i in range(4):
                dq_ref[:, i * 128:(i + 1) * 128] = _unrope(dq_scr[:, i * 128:(i + 1) * 128] * Q_SCALE, cq, sq).astype(BF16)
            for hk in kvh:
                cv_scr[:, sl[hk]] = lax.dot_general(p[hk].astype(BF16), dog[hk], TN, preferred_element_type=F32)
            dkc_ref[...] += ck_scr[0:C, :]
            dvc_ref[...] += cv_scr[0:C, :]

        @pl.when(n >= nb)
        def _():
            ck_scr[...] = jnp.zeros_like(ck_scr)
            cv_scr[...] = jnp.zeros_like(cv_scr)

        dk_ref[...] = _unrope(kp_acc[...] + ck_scr[C:C + BLK, :], ck_ref[...], sk_ref[...]).astype(BF16)
        dv_ref[...] = (vp_acc[...] + cv_scr[C:C + BLK, :]).astype(BF16)
        kp_acc[...] = kc_acc[...] + ck_scr[C + BLK:C + 2 * BLK, :]
        vp_acc[...] = vc_acc[...] + cv_scr[C + BLK:C + 2 * BLK, :]
        kc_acc[...] = ck_scr[C + 2 * BLK:C + 3 * BLK, :]
        vc_acc[...] = cv_scr[C + 2 * BLK:C + 3 * BLK, :]

    kv3 = _kv_specs(nb)
    cur = lambda w: pl.BlockSpec((BLK, w), lambda n: (jnp.minimum(n, nb - 1), 0))
    late = lambda w: pl.BlockSpec((BLK, w), lambda n: (jnp.maximum(n - 1, 0), 0))
    return _call(
        body, name="bwd_attn", grid=(nb + 1,),
        in_specs=[pl.BlockSpec(memory_space=pltpu.SMEM), cur(Q_W)] + kv3 + kv3
                 + [_full((C, KV_W)), _full((C, KV_W)), cur(Q_W), cur(N_Q_HEADS), cur(128), cur(128), late(128), late(128),
                    _bias_spec(nb)],
        out_specs=[cur(Q_W), late(KV_W), late(KV_W), _full((C, KV_W)), _full((C, KV_W)), _full((8, 128))],
        out_shape=[_sds((L, Q_W), BF16), _sds((L, KV_W), BF16), _sds((L, KV_W), BF16),
                   _sds((C, KV_W), F32), _sds((C, KV_W), F32), _sds((8, 128), F32)],
        scratch=[pltpu.VMEM((BLK, Q_W), F32), pltpu.VMEM((NK, KV_W), F32), pltpu.VMEM((NK, KV_W), F32)]
                + [pltpu.VMEM((BLK, KV_W), F32)] * 4,
        args=(sink, q, k, k, k, v, v, v, kc, vc, dya, lse, cos, sin, cos, sin, bias), comm=comm)


def _k_ctx_bwd(ctx, modc, hc, dkc, dvc, w_kv):
    C = ctx.shape[0]

    def body(c_ref, mod_ref, hc_ref, dkc_ref, dvc_ref, w_ref, gw_ref, dmod_ref):
        dkv = jnp.concatenate([dkc_ref[...], dvc_ref[...]], axis=1).astype(BF16)
        gw_ref[...] = lax.dot_general(dkv, hc_ref[...], TN, preferred_element_type=F32)
        dhc = jnp.dot(dkv, w_ref[...], preferred_element_type=F32)
        n, _ = _ln(c_ref[...])
        dmod_ref[...] = jnp.zeros_like(dmod_ref)
        dmod_ref[0:1, :] = _colsum(dhc)
        dmod_ref[1:2, :] = _colsum(dhc * n)

    return pl.pallas_call(
        body, name="bwd_ctx", grid=(1,),
        in_specs=[_full((C, D)), _full((8, D)), _full((C, D)), _full((C, KV_W)), _full((C, KV_W)), _full((2 * KV_W, D))],
        out_specs=[_full((2 * KV_W, D)), _full((8, D))],
        out_shape=[_sds((2 * KV_W, D), F32), _sds((8, D), F32)],
        compiler_params=_params(("arbitrary",)),
    )(ctx, modc, hc, dkc, dvc, w_kv)


def _k_in_bwd(dq, dk, dv, du, dvb, dga, dgb, x, dxp, w_in, modv, tm, comm=None):
    L = x.shape[0]
    parts = [(O_Q, Q_W), (O_K, KV_W), (O_V, KV_W), (O_U, GM_W), (O_VB, GM_W), (O_GA, D), (O_GB, D)]

    def body(dq_ref, dk_ref, dv_ref, du_ref, dvb_ref, dga_ref, dgb_ref, x_ref, dxp_ref, w_ref, mod_ref,
             dP_ref, gx_ref, acc_ref):
        @pl.when(pl.program_id(0) == 0)
        def _():
            acc_ref[...] = jnp.zeros_like(acc_ref)

        for (lo, width), r in zip(parts, (dq_ref, dk_ref, dv_ref, du_ref, dvb_ref, dga_ref, dgb_ref)):
            dP_ref[:, lo:lo + width] = r[...]
        dh = jnp.dot(dP_ref[...], w_ref[...], preferred_element_type=F32)
        n1, rstd1 = _ln(x_ref[...])
        acc_ref[0:1, :] += _colsum(dh)
        acc_ref[1:2, :] += _colsum(dh * n1)
        gx_ref[...] = dxp_ref[...] + _ln_bwd(dh * (1.0 + mod_ref[1:2, :]), n1, rstd1)

    return _call(
        body, name="bwd_in", grid=(L // tm,),
        in_specs=[_row(tm, w) for _, w in parts] + [_row(tm, D), _row(tm, D), _resident((IN_W, D)), _full((8, D))],
        out_specs=[_row(tm, IN_W), _row(tm, D), _full((8, D))],
        out_shape=[_sds((L, IN_W), BF16), _sds((L, D), F32), _sds((8, D), F32)],
        args=(dq, dk, dv, du, dvb, dga, dgb, x, dxp, w_in, modv), comm=comm)


def _wgrad(a, b, name, tk, tt, comm=None, extra=None):
    T, K = a.shape
    N = b.shape[1]
    nt = T // tt

    def body(*refs):
        a_ref, b_ref = refs[:2]
        o_ref, acc_ref = refs[-2:]
        j, t = pl.program_id(0), pl.program_id(1)

        @pl.when(t == 0)
        def _():
            acc_ref[...] = jnp.zeros_like(acc_ref)

        acc_ref[...] += lax.dot_general(a_ref[...], b_ref[...], TN, preferred_element_type=F32)

        if extra is not None:
            lo, rows = extra[0] % tk, extra[1].shape[0]

            @pl.when((t == nt - 1) & (j == extra[0] // tk))
            def _():
                acc_ref[lo:lo + rows, :] += refs[2][...]

        @pl.when(t == nt - 1)
        def _():
            o_ref[...] = acc_ref[...].astype(BF16)

    extra_specs = [] if extra is None else [pl.BlockSpec(extra[1].shape, lambda j, t: (0, 0))]
    (out,), got = _call(
        body, name=name, grid=(K // tk, nt),
        in_specs=[pl.BlockSpec((tt, tk), lambda j, t: (t, j)), pl.BlockSpec((tt, N), lambda j, t: (t, 0))] + extra_specs,
        out_specs=[pl.BlockSpec((tk, N), lambda j, t: (j, 0))],
        out_shape=[_sds((K, N), BF16)],
        scratch=[pltpu.VMEM((tk, N), F32)],
        args=(a, b) + (() if extra is None else (extra[1],)), comm=comm)
    return (out, got) if comm is not None else out


def _adamw_reduce(parts, w, m, v, name, tr):
    R, C = w.shape

    def body(p_ref, w_ref, m_ref, v_ref, g_ref, d_ref, m2_ref, v2_ref):
        g = p_ref[0].astype(F32)
        for i in range(1, N_DEV):
            g = g + p_ref[i].astype(F32)
        delta, m2, v2 = _adamw(w_ref[...], g, m_ref[...], v_ref[...])
        g_ref[...] = g
        d_ref[...] = delta
        m2_ref[...] = m2
        v2_ref[...] = v2

    spec = _row(tr, C)
    return pl.pallas_call(
        body, name=name, grid=(R // tr,),
        in_specs=[pl.BlockSpec((N_DEV, tr, C), lambda i: (0, i, 0)), spec, spec, spec],
        out_specs=[spec] * 4,
        out_shape=[_sds((R, C), F32)] * 4,
        compiler_params=_params(("arbitrary",)),
    )(parts, w, m, v)


def _small_reduce(gath, gath_ws):
    def body(g_ref, w_ref, out_ref):
        tot = g_ref[0]
        wsum = w_ref[0]
        for i in range(1, N_DEV):
            tot = tot + g_ref[i]
            wsum = wsum + w_ref[i]
        out_ref[0:16, :] = tot
        out_ref[0:2, :] = tot[0:2, :] + tot[6:8, :]
        out_ref[15:16, :] = jnp.broadcast_to(jnp.sum(tot[15:16, :], axis=1, keepdims=True), (1, D))
        out_ref[16:SMALL_ROWS, :] = wsum

    return pl.pallas_call(
        body, name="small_reduce", grid=(1,),
        in_specs=[_full((N_DEV, 16, D)), _full((N_DEV, SMALL_ROWS - 16, D))],
        out_specs=_full((SMALL_ROWS, D)),
        out_shape=_sds((SMALL_ROWS, D), F32),
        compiler_params=_params(("arbitrary",)),
    )(gath, gath_ws)


def _small_adamw(w, g, m, v, name):
    shape = w.shape

    def body(w_ref, g_ref, m_ref, v_ref, d_ref, m2_ref, v2_ref):
        delta, m2, v2 = _adamw(w_ref[...], g_ref[...], m_ref[...], v_ref[...])
        d_ref[...] = delta
        m2_ref[...] = m2
        v2_ref[...] = v2

    return pl.pallas_call(
        body, name=name, grid=(1,),
        in_specs=[_full(shape)] * 4, out_specs=[_full(shape)] * 3,
        out_shape=[_sds(shape, F32)] * 3,
        compiler_params=_params(("arbitrary",)),
    )(w, g, m, v)


def _cctx_finish(gath, c_ctx, m, v):
    def body(g_ref, c_ref, m_ref, v_ref, gr_ref, d_ref, m2_ref, v2_ref):
        ds = g_ref[0]
        for i in range(1, N_DEV):
            ds = ds + g_ref[i]
        c = c_ref[...]
        sg = _sigmoid(c)
        g = ds * (sg * (1.0 + c * (1.0 - sg)))
        delta, m2, v2 = _adamw(c, g, m_ref[...], v_ref[...])
        gr_ref[...] = g
        d_ref[...] = delta
        m2_ref[...] = m2
        v2_ref[...] = v2

    return pl.pallas_call(
        body, name="cctx_finish", grid=(1,),
        in_specs=[_full((N_DEV, 8, D))] + [_full((8, D))] * 3, out_specs=[_full((8, D))] * 4,
        out_shape=[_sds((8, D), F32)] * 4,
        compiler_params=_params(("arbitrary",)),
    )(gath, c_ctx, m, v)


def _pad_rows(a, rows):
    return jnp.concatenate([a, jnp.zeros((rows - a.shape[0], a.shape[1]), a.dtype)], axis=0)


def _pack_small(b_ada, ln1_g, ln1_b, ln2_g, ln2_b, gm_g, gm_b, b_sp, sink, w_sp):
    rows = [b_ada.reshape(6, D), jnp.zeros((2, D), F32), ln1_g.reshape(1, D), ln1_b.reshape(1, D), ln2_g.reshape(1, D),
            ln2_b.reshape(1, D), jnp.concatenate([gm_g.reshape(1, GM_W), gm_b.reshape(1, GM_W)], axis=1),
            b_sp.reshape(1, D), _pad_rows(sink.reshape(1, N_Q_HEADS).T, D).T.reshape(1, D), jnp.zeros((1, D), F32),
            w_sp.reshape(N_GROUPS * BLK * BLK // D, D)]
    return jnp.concatenate(rows, axis=0)


def _unpack_small(p):
    return dict(b_ada=p[0:6].reshape(1, 6 * D), ln1_g=p[8:9], ln1_b=p[9:10], ln2_g=p[10:11], ln2_b=p[11:12],
                gmlp_ln_g=p[12:13, :GM_W], gmlp_ln_b=p[12:13, GM_W:], b_spatial=p[13:14].reshape(1, N_GROUPS, BLK),
                attn_sink=p[14:15, :N_Q_HEADS], w_spatial=p[16:].reshape(1, N_GROUPS, BLK, BLK))


def kernel(x, c, ctx, c_ctx, w_ada, b_ada, w_in, attn_sink, gmlp_ln_g, gmlp_ln_b, w_spatial, b_spatial, w_branch_a, w_branch_b, w_out, ln1_g, ln1_b, w_ffn_in, w_ffn_out, ln2_g, ln2_b, loss_target, m_c_ctx, m_w_ada, m_b_ada, m_w_in, m_attn_sink, m_gmlp_ln_g, m_gmlp_ln_b, m_w_spatial, m_b_spatial, m_w_branch_a, m_w_branch_b, m_w_out, m_ln1_g, m_ln1_b, m_w_ffn_in, m_w_ffn_out, m_ln2_g, m_ln2_b, v_c_ctx, v_w_ada, v_b_ada, v_w_in, v_attn_sink, v_gmlp_ln_g, v_gmlp_ln_b, v_w_spatial, v_b_spatial, v_w_branch_a, v_w_branch_b, v_w_out, v_ln1_g, v_ln1_b, v_w_ffn_in, v_w_ffn_out, v_ln2_g, v_ln2_b):
    L = x.shape[1]
    me = 4 * lax.axis_index("x") + 2 * lax.axis_index("y") + lax.axis_index("c")
    x2, tgt, ctx2 = x[0], loss_target[0], ctx[0]
    tm_in = min(512, L)
    tm = min(256, L)
    tt = min(1024, L)

    transposed = ("w_in", "w_ffn_in")
    tr = lambda kname, a: a.T if kname in transposed else a
    big = dict(w_in=w_in[0].T, w_branch_a=w_branch_a[0], w_branch_b=w_branch_b[0], w_out=w_out[0],
               w_ffn_in=w_ffn_in[0].T, w_ffn_out=w_ffn_out[0])
    col_sharded = ("w_branch_a", "w_branch_b")
    shard_bf = {k: a.astype(BF16) for k, a in big.items()}

    def assemble(kname, g):
        if kname in col_sharded:
            return g.transpose(1, 0, 2).reshape(g.shape[1], N_DEV * g.shape[2])
        return g.reshape(N_DEV * g.shape[1], g.shape[2])

    def to_blocks(kname, g):
        if kname in col_sharded:
            return g.reshape(g.shape[0], N_DEV, g.shape[1] // N_DEV).transpose(1, 0, 2)
        return g.reshape(N_DEV, g.shape[0] // N_DEV, g.shape[1])

    full = {}
    n_ada = w_ada.shape[2]
    b_my = lax.dynamic_slice(b_ada, (0, me * n_ada), (1, n_ada))
    act, mod_all, got = _prologue(_pad_rows(c, 8), _pad_rows(c_ctx[None, :], 8), w_ada[0], b_my,
                                  _Comm(gather=[shard_bf["w_in"]]))
    full["w_in"] = assemble("w_in", got[0])
    mod_all = mod_all.transpose(1, 0, 2).reshape(16, 6 * D)
    modv = _pad_rows(lax.dynamic_slice(mod_all, (me, 0), (1, 6 * D)).reshape(6, D), 8)
    modc = _pad_rows(mod_all[8].reshape(6, D), 8)

    lnv = _pad_rows(jnp.concatenate([ln1_g, ln1_b, ln2_g, ln2_b], axis=0), 8)
    gm_lnv = _pad_rows(jnp.concatenate([gmlp_ln_g, gmlp_ln_b], axis=0), 8)
    ws_b = w_spatial[0].astype(BF16)
    wst_b = ws_b.transpose(0, 2, 1)
    bsp = jnp.repeat(b_spatial[0].T, GROUP_DIM, axis=1)
    sink = attn_sink[0]
    cos, sin = _rope_tables(L)
    bias = _attn_bias()
    w_kv = full["w_in"][O_K:O_K + 2 * KV_W, :]

    (h, q, k, v, u, vb, ga, gb), got = _k_in(
        x2, modv, full["w_in"], cos, sin, tm_in,
        comm=_Comm(gather=[shard_bf["w_branch_a"], shard_bf["w_branch_b"], shard_bf["w_out"]]))
    for kname, g in zip(("w_branch_a", "w_branch_b", "w_out"), got):
        full[kname] = assemble(kname, g)
    hc, kc, vc = _k_ctx(ctx2, modc, w_kv)
    (ya, lse), got = _k_attn(sink, q, k, v, kc, vc, bias, comm=_Comm(gather=[shard_bf["w_ffn_in"], shard_bf["w_ffn_out"]]))
    for kname, g in zip(("w_ffn_in", "w_ffn_out"), got):
        full[kname] = assemble(kname, g)
    yb = _k_gmlp(u, vb, gm_lnv, ws_b, bsp)
    merged, mix, xm, h2 = _k_merge(x2, ya, yb, ga, gb, full["w_branch_a"], full["w_branch_b"], full["w_out"], modv, lnv, tm_in)
    gate, up, act_f, dr2, df, acc_f = _k_ffn(h2, xm, tgt, full["w_ffn_in"], full["w_ffn_out"], modv, lnv, tm_in)

    dF, dmix, dxp, acc_b = _k_ffn_bwd(df, gate, up, xm, dr2, x2, mix, full["w_ffn_in"], full["w_ffn_out"], modv, lnv, tm)
    blk_fi = to_blocks("w_ffn_in", _wgrad(dF, h2, "wgrad_ffn_in", 1408, tt))
    blk_fo = to_blocks("w_ffn_out", _wgrad(act_f, df, "wgrad_ffn_out", 1408, tt))
    dA, dB, dga, dgb, dya, dyb = _k_merge_bwd(dmix, ya, yb, ga, gb, full["w_branch_a"], full["w_branch_b"], full["w_out"], tm_in)
    du, dvb, g_ws, g_bst, g_gln = _k_gmlp_bwd(u, vb, dyb, gm_lnv, ws_b, wst_b, bsp)
    (dq, dk, dv, dkc, dvc, g_sink), (rcv_fi, rcv_fo, gath_ws) = _k_attn_bwd(
        sink, q, k, v, kc, vc, dya, lse, cos, sin, bias,
        comm=_Comm(scatter=[blk_fi, blk_fo], spread=[g_ws.reshape(SMALL_ROWS - 16, D)]))
    blk_a = to_blocks("w_branch_a", _wgrad(ya, dA, "wgrad_a", Q_W, tt))
    blk_b = to_blocks("w_branch_b", _wgrad(yb, dB, "wgrad_b", GM_W, tt))
    blk_o = to_blocks("w_out", _wgrad(merged, dmix, "wgrad_out", D, tt))
    (dP, grad_x, acc_i), _ = _k_in_bwd(dq, dk, dv, du, dvb, dga, dgb, x2, dxp, full["w_in"], modv, tm_in)
    g_ctx, dmodc = _k_ctx_bwd(ctx2, modc, hc, dkc, dvc, w_kv)
    gw_in, (rcv_a, rcv_b, rcv_o) = _wgrad(dP, h, "wgrad_in", 1280, tt, comm=_Comm(scatter=[blk_a, blk_b, blk_o]),
                                          extra=(O_K, g_ctx))

    dmod_x = jnp.concatenate([acc_i[0:2], acc_b[4:5], acc_b[0:2], acc_f[2:3]], axis=0)
    small = jnp.concatenate([
        dmod_x, dmodc[0:2], acc_b[2:4], acc_f[0:2],
        jnp.concatenate([g_gln[0:1], g_gln[1:2]], axis=1), g_bst.T.reshape(1, D),
        _pad_rows(g_sink[:, 0:1], D).T, acc_f[3:4]], axis=0)
    rcv_in, gath = _comm_only(_Comm(scatter=[to_blocks("w_in", gw_in)], spread=[small]), "exchange_last")
    received = dict(w_in=rcv_in, w_branch_a=rcv_a, w_branch_b=rcv_b, w_out=rcv_o, w_ffn_in=rcv_fi, w_ffn_out=rcv_fo)
    moments = dict(w_in=(m_w_in, v_w_in), w_branch_a=(m_w_branch_a, v_w_branch_a), w_branch_b=(m_w_branch_b, v_w_branch_b),
                   w_out=(m_w_out, v_w_out), w_ffn_in=(m_w_ffn_in, v_w_ffn_in), w_ffn_out=(m_w_ffn_out, v_w_ffn_out))
    names = list(big)
    res = {}
    for kname in names:
        mm, vv = moments[kname]
        R = big[kname].shape[0]
        res[kname] = [tr(kname, r) for r in _adamw_reduce(
            received[kname], big[kname], tr(kname, mm[0]), tr(kname, vv[0]), "adamw_" + kname, 256 if R % 256 == 0 else R // 2)]

    tot = _small_reduce(gath, gath_ws)
    g_small = _unpack_small(tot)
    loss = tot[15, 0]

    p_w = _pack_small(b_ada, ln1_g, ln1_b, ln2_g, ln2_b, gmlp_ln_g, gmlp_ln_b, b_spatial, attn_sink, w_spatial)
    p_m = _pack_small(m_b_ada, m_ln1_g, m_ln1_b, m_ln2_g, m_ln2_b, m_gmlp_ln_g, m_gmlp_ln_b, m_b_spatial, m_attn_sink, m_w_spatial)
    p_v = _pack_small(v_b_ada, v_ln1_g, v_ln1_b, v_ln2_g, v_ln2_b, v_gmlp_ln_g, v_gmlp_ln_b, v_b_spatial, v_attn_sink, v_w_spatial)
    s_d, s_m, s_v = [_unpack_small(t) for t in _small_adamw(p_w, tot, p_m, p_v, "adamw_small")]

    dmod_rows = jnp.concatenate([gath[:, 0:6, :].reshape(N_DEV, 6 * D),
                                 jnp.concatenate([tot[6:8].reshape(1, 2 * D), jnp.zeros((1, 4 * D), F32)], axis=1),
                                 jnp.zeros((7, 6 * D), F32)], axis=0)
    dmod_my = lax.dynamic_slice(dmod_rows, (0, me * n_ada), (16, n_ada))
    g_wada, d_wada, m2_wada, v2_wada, pc = _ada_bwd(act, dmod_my, w_ada[0], m_w_ada[0], v_w_ada[0])
    pc_all = _ag_small(pc, "gather_cctx")
    cc8 = lambda a: _pad_rows(a.reshape(1, D), 8)
    g_cc, d_cc, m2_cc, v2_cc = _cctx_finish(pc_all, cc8(c_ctx), cc8(m_c_ctx), cc8(v_c_ctx))

    order = ["c_ctx", "w_ada", "b_ada", "w_in", "attn_sink", "gmlp_ln_g", "gmlp_ln_b", "w_spatial", "b_spatial",
             "w_branch_a", "w_branch_b", "w_out", "ln1_g", "ln1_b", "w_ffn_in", "w_ffn_out", "ln2_g", "ln2_b"]
    grads, deltas, new_m, new_v = {}, {}, {}, {}
    grads["c_ctx"], deltas["c_ctx"], new_m["c_ctx"], new_v["c_ctx"] = g_cc[0], d_cc[0], m2_cc[0], v2_cc[0]
    grads["w_ada"], deltas["w_ada"], new_m["w_ada"], new_v["w_ada"] = g_wada[None], d_wada[None], m2_wada[None], v2_wada[None]
    for kname in names:
        g, d, m2, v2 = res[kname]
        grads[kname], deltas[kname], new_m[kname], new_v[kname] = g[None], d[None], m2[None], v2[None]
    for kname in ("b_ada", "attn_sink", "gmlp_ln_g", "gmlp_ln_b", "w_spatial", "b_spatial", "ln1_g", "ln1_b", "ln2_g", "ln2_b"):
        grads[kname], deltas[kname], new_m[kname], new_v[kname] = g_small[kname], s_d[kname], s_m[kname], s_v[kname]
    return (loss, grad_x[None], *[grads[n] for n in order], *[deltas[n] for n in order],
            *[new_m[n] for n in order], *[new_v[n] for n in order])
```

```python
import functools
import math

import jax
import jax.numpy as jnp
import numpy as np
from jax import lax
from jax.experimental import pallas as pl
from jax.experimental.pallas import tpu as pltpu

F32 = jnp.float32
BF16 = jnp.bfloat16
MESH = pl.DeviceIdType.MESH

N_DEV = 8
D = 1024
HEAD_DIM = 64
N_Q_HEADS = 8
N_KV_HEADS = 2
GQA_GROUP = 4
BLK = 128
Q_W = 512
KV_W = 128
GM_W = 512
N_GROUPS = 8
GROUP_DIM = 64
FFN_H = 2816
IN_W = 3840
O_Q, O_K, O_V, O_U, O_VB, O_GA, O_GB = 0, 512, 640, 768, 1280, 1792, 2816
LN_EPS = 1e-5
NEG_INF = -1e30
ALPHA = 2.0 ** 0.25
ROPE_BASE = 10000.0
ROPE_PAIRS = 16
Q_SCALE = HEAD_DIM ** -0.5
GELU_K0 = math.sqrt(2.0 / math.pi)
GELU_K1 = 0.044715

ADAM_LR = 0.001
ADAM_B1 = 0.9
ADAM_B2 = 0.999
ADAM_EPS = 1e-08
ADAM_WD = 0.01
ADAM_STEP = 10

VMEM_LIMIT = 56 * 1024 * 1024
SMALL_ROWS = 144
NT = (((1,), (1,)), ((), ()))
TN = (((0,), (0,)), ((), ()))


def _params(sem=None):
    return pltpu.CompilerParams(dimension_semantics=sem, vmem_limit_bytes=VMEM_LIMIT)


def _row(tm, w):
    return pl.BlockSpec((tm, w), lambda i: (i, 0))


def _full(shape):
    nd = len(shape)
    return pl.BlockSpec(shape, lambda i: (0,) * nd)


def _resident(shape):
    nd = len(shape)
    return pl.BlockSpec(shape, lambda i: (0,) * nd, pipeline_mode=pl.Buffered(1))


def _sds(shape, dt):
    return jax.ShapeDtypeStruct(shape, dt)


def _ln(xf):
    mu = jnp.mean(xf, axis=-1, keepdims=True)
    xc = xf - mu
    var = jnp.mean(xc * xc, axis=-1, keepdims=True)
    rstd = lax.rsqrt(var + LN_EPS)
    return xc * rstd, rstd


def _ln_bwd(dn, n, rstd):
    m1 = jnp.mean(dn, axis=-1, keepdims=True)
    m2 = jnp.mean(dn * n, axis=-1, keepdims=True)
    return rstd * (dn - m1 - n * m2)


def _colsum(t):
    return jnp.sum(t, axis=0, keepdims=True)


def _sigmoid(x):
    return 0.5 * jnp.tanh(0.5 * x) + 0.5


def _gelu(x):
    t = jnp.tanh(GELU_K0 * (x + GELU_K1 * (x * x * x)))
    return x * (0.5 * (1.0 + t)), t


def _gelu_grad(x, t):
    return 0.5 * (1.0 + t) + 0.5 * x * (1.0 - t * t) * (GELU_K0 * (1.0 + 3.0 * GELU_K1 * x * x))


def _swap16(t):
    lane = lax.broadcasted_iota(jnp.int32, t.shape, 1)
    return jnp.where((lane & 16) == 0, pltpu.roll(t, 112, 1), pltpu.roll(t, 16, 1))


def _rope(t, cos, sin):
    return t * cos + _swap16(t) * sin


def _unrope(t, cos, sin):
    return t * cos - _swap16(t) * sin


def _adamw(w, g, m, v):
    m2 = ADAM_B1 * m + (1.0 - ADAM_B1) * g
    v2 = ADAM_B2 * v + (1.0 - ADAM_B2) * (g * g)
    m_hat = m2 / (1.0 - ADAM_B1 ** ADAM_STEP)
    v_hat = v2 / (1.0 - ADAM_B2 ** ADAM_STEP)
    delta = -ADAM_LR * (m_hat / (jnp.sqrt(v_hat) + ADAM_EPS) + ADAM_WD * w)
    return delta, m2, v2


def _rope_tables(L):
    inv = (np.float32(ROPE_BASE) ** (-np.arange(ROPE_PAIRS, dtype=np.float32) / np.float32(ROPE_PAIRS))).astype(np.float32)
    t = np.arange(L, dtype=np.int32)
    rows = (t // 64).astype(np.float32)[:, None] * inv
    cols = (t % 64).astype(np.float32)[:, None] * inv
    cr, sr, cc, sc = np.cos(rows), np.sin(rows), np.cos(cols), np.sin(cols)
    cos = np.concatenate([cr, cr, cc, cc], axis=1)
    sin = np.concatenate([-sr, sr, -sc, sc], axis=1)
    return jnp.asarray(np.tile(cos, (1, 2)), F32), jnp.asarray(np.tile(sin, (1, 2)), F32)


def _me():
    return lax.axis_index("x"), lax.axis_index("y"), lax.axis_index("c")


def _peer(mx, my, mc, k):
    return (mx ^ ((k >> 2) & 1), my ^ ((k >> 1) & 1), mc ^ (k & 1))


def _ag_small(x, name):
    R, C = x.shape

    def body(x_ref, out_ref, send_sems, recv_sems):
        mx, my, mc = _me()
        me = 4 * mx + 2 * my + mc
        out_ref[pl.ds(me, 1)] = x_ref[...][None]
        sends = []
        for k in range(1, N_DEV):
            cp = pltpu.make_async_remote_copy(
                src_ref=x_ref, dst_ref=out_ref.at[me], send_sem=send_sems.at[k - 1], recv_sem=recv_sems.at[k - 1],
                device_id=_peer(mx, my, mc, k), device_id_type=MESH)
            cp.start()
            sends.append(cp)
        for k in range(1, N_DEV):
            pltpu.make_async_remote_copy(
                src_ref=x_ref, dst_ref=out_ref.at[me ^ k], send_sem=send_sems.at[k - 1], recv_sem=recv_sems.at[k - 1],
                device_id=(mx, my, mc), device_id_type=MESH).wait_recv()
        for cp in sends:
            cp.wait_send()

    return pl.pallas_call(
        body, name=name,
        out_shape=_sds((N_DEV, R, C), x.dtype),
        in_specs=[pl.BlockSpec(memory_space=pltpu.VMEM)],
        out_specs=pl.BlockSpec(memory_space=pltpu.VMEM),
        scratch_shapes=[pltpu.SemaphoreType.DMA((N_DEV - 1,)), pltpu.SemaphoreType.DMA((N_DEV - 1,))],
        compiler_params=pltpu.CompilerParams(vmem_limit_bytes=VMEM_LIMIT),
    )(x)


class _Comm:
    def __init__(self, gather=(), scatter=(), spread=()):
        self.kinds = ["gather"] * len(gather) + ["scatter"] * len(scatter) + ["spread"] * len(spread)
        self.args = list(gather) + list(scatter) + list(spread)
        self.n = len(self.args)

    def out_shape(self):
        return [_sds(a.shape if k == "scatter" else (N_DEV,) + a.shape, a.dtype) for k, a in zip(self.kinds, self.args)]

    def specs(self):
        return [pl.BlockSpec(memory_space=pl.ANY)] * self.n

    def scratch(self):
        return [pltpu.SemaphoreType.DMA((7 * self.n,)), pltpu.SemaphoreType.DMA((7 * self.n,)),
                pltpu.SemaphoreType.DMA((self.n,))]

    def _plan(self, x_refs, out_refs, send_sems, recv_sems, local_sems):
        mx, my, mc = _me()
        me = 4 * mx + 2 * my + mc
        here, sibling = (mx, my, mc), (mx, my, 1 - mc)
        chips = [(1 - mx, my), (mx, 1 - my), (1 - mx, 1 - my)]
        local, first, last = [], [], []
        relay = [[], [], []]
        for a, kind in enumerate(self.kinds):
            x, out = x_refs[a], out_refs[a]

            def rc(k, src, dst, to):
                return pltpu.make_async_remote_copy(
                    src_ref=src, dst_ref=dst, send_sem=send_sems.at[7 * a + k], recv_sem=recv_sems.at[7 * a + k],
                    device_id=to, device_id_type=MESH)

            if kind == "gather":
                local.append(pltpu.make_async_copy(x, out.at[me], local_sems.at[a]))
                first.append(rc(0, x, out.at[me], sibling))
                last.append(rc(0, x, out.at[me ^ 1], here))
                for j, (cx, cy) in enumerate(chips):
                    first.append(rc(1 + j, x, out.at[me], (cx, cy, mc)))
                    landed = out.at[4 * cx + 2 * cy + mc]
                    relay[j].append((rc(1 + j, x, landed, here), rc(4 + j, landed, landed, sibling)))
                    last.append(rc(4 + j, x, out.at[4 * cx + 2 * cy + 1 - mc], here))
            else:
                own = x.at[me] if kind == "scatter" else x
                local.append(pltpu.make_async_copy(own, out.at[me], local_sems.at[a]))
                for k in range(1, N_DEV):
                    src = x.at[me ^ k] if kind == "scatter" else x
                    first.append(rc(k - 1, src, out.at[me], _peer(mx, my, mc, k)))
                    last.append(rc(k - 1, own, out.at[me ^ k], here))
        return local, first, relay[0] + relay[1] + relay[2], last

    def start(self, *refs):
        local, first, _, _ = self._plan(*refs)
        for cp in local + first:
            cp.start()

    def finish(self, *refs):
        local, first, relay, last = self._plan(*refs)
        for arrival, onward in relay:
            arrival.wait_recv()
            onward.start()
        for cp in last:
            cp.wait_recv()
        for cp in first:
            cp.wait_send()
        for _, onward in relay:
            onward.wait_send()
        for cp in local:
            cp.wait()


def _call(body, *, name, grid, in_specs, out_specs, out_shape, args, scratch=(), comm=None, aliases=None):
    params = _params(("arbitrary",) * len(grid))

    def at(end):
        conds = [pl.program_id(d) == (n - 1 if end else 0) for d, n in enumerate(grid)]
        return functools.reduce(lambda p, q: p & q, conds)

    if comm is None:
        res = pl.pallas_call(
            body, name=name, grid=grid, in_specs=list(in_specs), out_specs=list(out_specs), out_shape=list(out_shape),
            scratch_shapes=list(scratch), input_output_aliases=aliases or {}, compiler_params=params)(*args)
        return list(res), []
    n_in, n_out, n_scr, cn = len(in_specs), len(out_specs), len(scratch), comm.n

    def hosted(*refs):
        ins, refs = refs[:n_in], refs[n_in:]
        cins, refs = refs[:cn], refs[cn:]
        outs, refs = refs[:n_out], refs[n_out:]
        couts, refs = refs[:cn], refs[cn:]
        scr, sems = refs[:n_scr], refs[n_scr:]

        @pl.when(at(False))
        def _():
            comm.start(cins, couts, *sems)

        body(*ins, *outs, *scr)

        @pl.when(at(True))
        def _():
            comm.finish(cins, couts, *sems)

    res = pl.pallas_call(
        hosted, name=name, grid=grid, in_specs=list(in_specs) + comm.specs(), out_specs=list(out_specs) + comm.specs(),
        out_shape=list(out_shape) + comm.out_shape(), scratch_shapes=list(scratch) + comm.scratch(),
        input_output_aliases=aliases or {}, compiler_params=params)(*args, *comm.args)
    return list(res[:n_out]), list(res[n_out:])


def _exchange_two_level(blk, small, name):
    _, R, C = blk.shape
    rows = small.shape[0]

    def body(blk_ref, small_ref, stage_ref, out_ref, gath_ref, a_scr, b_scr, t_scr, s1, r1, s3, r3, ss, rs, lsem):
        mx, my, mc = _me()
        me = 4 * mx + 2 * my + mc
        mine = 2 * mx + my
        here, sibling = (mx, my, mc), (mx, my, 1 - mc)

        def rc(src, dst, send, recv, to):
            return pltpu.make_async_remote_copy(src_ref=src, dst_ref=dst, send_sem=send, recv_sem=recv,
                                                device_id=to, device_id_type=MESH)

        own_small = pltpu.make_async_copy(small_ref, gath_ref.at[me], lsem.at[0])
        own_small.start()
        spread = [rc(small_ref, gath_ref.at[me], ss.at[k - 1], rs.at[k - 1], _peer(mx, my, mc, k)) for k in range(1, N_DEV)]
        to_sib = [rc(blk_ref.at[2 * p + 1 - mc], stage_ref.at[p], s1.at[p], r1.at[p], sibling) for p in range(4)]
        for cp in spread + to_sib:
            cp.start()
        own = [pltpu.make_async_copy(blk_ref.at[2 * p + mc], a_scr.at[p], lsem.at[1 + p]) for p in range(4)]
        for cp in own:
            cp.start()
        from_sib = []
        for p in range(4):
            rc(blk_ref.at[2 * p + 1 - mc], stage_ref.at[p], s1.at[p], r1.at[p], here).wait_recv()
            cp = pltpu.make_async_copy(stage_ref.at[p], b_scr.at[p], lsem.at[5 + p])
            cp.start()
            from_sib.append(cp)
        for cp in own + from_sib:
            cp.wait()
        t_scr[...] = (a_scr[...].astype(F32) + b_scr[...].astype(F32)).astype(BF16)
        keep = pltpu.make_async_copy(t_scr.at[mine], out_ref.at[mine], lsem.at[9])
        keep.start()
        onward = [rc(t_scr.at[mine ^ k], out_ref.at[mine], s3.at[k - 1], r3.at[k - 1], (mx ^ (k >> 1), my ^ (k & 1), mc))
                  for k in range(1, 4)]
        for cp in onward:
            cp.start()
        for k in range(1, 4):
            rc(t_scr.at[mine], out_ref.at[mine ^ k], s3.at[k - 1], r3.at[k - 1], here).wait_recv()
        for k in range(1, N_DEV):
            rc(small_ref, gath_ref.at[me ^ k], ss.at[k - 1], rs.at[k - 1], here).wait_recv()
        for cp in spread + to_sib + onward:
            cp.wait_send()
        keep.wait()
        own_small.wait()

    any_spec = pl.BlockSpec(memory_space=pl.ANY)
    dma = pltpu.SemaphoreType.DMA
    _, out, gath = pl.pallas_call(
        body, name=name,
        in_specs=[any_spec, any_spec], out_specs=[any_spec] * 3,
        out_shape=[_sds((4, R, C), BF16), _sds((4, R, C), BF16), _sds((N_DEV, rows, D), F32)],
        scratch_shapes=[pltpu.VMEM((4, R, C), BF16)] * 3
                       + [dma((4,)), dma((4,)), dma((3,)), dma((3,)), dma((N_DEV - 1,)), dma((N_DEV - 1,)), dma((10,))],
        compiler_params=pltpu.CompilerParams(vmem_limit_bytes=VMEM_LIMIT),
    )(blk, small)
    return out, gath


def _comm_only(comm, name):
    return _call(lambda: None, name=name, grid=(1,), in_specs=[], out_specs=[], out_shape=[], args=[], comm=comm)[1]


def _exchange_rows(x_ref, out_ref, send_sems, recv_sems):
    mx, my, mc = _me()
    me = 4 * mx + 2 * my + mc
    out_ref[pl.ds(me, 1)] = x_ref[...][None]
    sends = []
    for k in range(1, N_DEV):
        cp = pltpu.make_async_remote_copy(
            src_ref=x_ref, dst_ref=out_ref.at[me], send_sem=send_sems.at[k - 1], recv_sem=recv_sems.at[k - 1],
            device_id=_peer(mx, my, mc, k), device_id_type=MESH)
        cp.start()
        sends.append(cp)
    for k in range(1, N_DEV):
        pltpu.make_async_remote_copy(
            src_ref=x_ref, dst_ref=out_ref.at[me ^ k], send_sem=send_sems.at[k - 1], recv_sem=recv_sems.at[k - 1],
            device_id=(mx, my, mc), device_id_type=MESH).wait_recv()
    for cp in sends:
        cp.wait_send()


def _prologue(c8, cctx8, w_ada, b_my, comm):
    nw = w_ada.shape[1]

    def body(c_ref, cctx_ref, w_ref, b_ref, act_ref, mod_ref, cmine_scr, call_scr, mine_scr, mall_scr, s1, r1, s2, r2):
        cmine_scr[...] = c_ref[...]
        _exchange_rows(cmine_scr, call_scr, s1, r1)
        rows = [call_scr[d][0:1, :] for d in range(N_DEV)] + [cctx_ref[0:1, :], jnp.zeros((7, D), F32)]
        s = jnp.concatenate(rows, axis=0)
        act = s * _sigmoid(s)
        act_ref[...] = act
        mine_scr[...] = jnp.dot(act.astype(BF16), w_ref[...].astype(BF16), preferred_element_type=F32) + b_ref[...]
        _exchange_rows(mine_scr, mall_scr, s2, r2)
        mod_ref[...] = mall_scr[...]

    sems = [pltpu.SemaphoreType.DMA((N_DEV - 1,))] * 4
    (act, mod), got = _call(
        body, name="prologue", grid=(1,),
        in_specs=[_full((8, D)), _full((8, D)), _full((D, nw)), _full((1, nw))],
        out_specs=[_full((16, D)), _full((N_DEV, 16, nw))],
        out_shape=[_sds((16, D), F32), _sds((N_DEV, 16, nw), F32)],
        scratch=[pltpu.VMEM((8, D), F32), pltpu.VMEM((N_DEV, 8, D), F32), pltpu.VMEM((16, nw), F32),
                 pltpu.VMEM((N_DEV, 16, nw), F32)] + sems,
        args=(c8, cctx8, w_ada, b_my), comm=comm)
    return act, mod, got


def _ada_bwd(act, dmod_my, w_ada, m, v, tr=256):
    nw = w_ada.shape[1]

    def body(act_ref, dm_ref, w_ref, m_ref, v_ref, g_ref, d_ref, m2_ref, v2_ref, pc_ref):
        dm = dm_ref[...].astype(BF16)
        g = lax.dot_general(act_ref[...].astype(BF16), dm, TN, preferred_element_type=F32)
        w = w_ref[...]
        delta, m2, v2 = _adamw(w, g, m_ref[...], v_ref[...])
        g_ref[...] = g
        d_ref[...] = delta
        m2_ref[...] = m2
        v2_ref[...] = v2
        pc_ref[...] = lax.dot_general(dm[8:16, :], w.astype(BF16), NT, preferred_element_type=F32)

    wspec = _row(tr, nw)
    return pl.pallas_call(
        body, name="ada_bwd", grid=(D // tr,),
        in_specs=[pl.BlockSpec((16, tr), lambda i: (0, i)), _full((16, nw)), wspec, wspec, wspec],
        out_specs=[wspec, wspec, wspec, wspec, pl.BlockSpec((8, tr), lambda i: (0, i))],
        out_shape=[_sds((D, nw), F32)] * 4 + [_sds((8, D), F32)],
        compiler_params=_params(("arbitrary",)),
    )(act, dmod_my, w_ada, m, v)


def _k_in(x, modv, w_in, cos, sin, tm, comm=None):
    L = x.shape[0]

    def body(x_ref, mod_ref, w_ref, cos_ref, sin_ref, h_ref, q_ref, k_ref, v_ref, u_ref, vb_ref, ga_ref, gb_ref):
        n, _ = _ln(x_ref[...])
        h = (n * (1.0 + mod_ref[1:2, :]) + mod_ref[0:1, :]).astype(BF16)
        h_ref[...] = h
        c, s = cos_ref[...], sin_ref[...]

        def proj(lo, width):
            return lax.dot_general(h, w_ref[lo:lo + width, :], NT, preferred_element_type=F32)

        for i in range(4):
            q_ref[:, i * 128:(i + 1) * 128] = (_rope(proj(O_Q + i * 128, 128), c, s) * Q_SCALE).astype(BF16)
        k_ref[...] = _rope(proj(O_K, KV_W), c, s).astype(BF16)
        v_ref[...] = proj(O_V, KV_W).astype(BF16)
        u_ref[...] = proj(O_U, GM_W).astype(BF16)
        vb_ref[...] = proj(O_VB, GM_W).astype(BF16)
        ga_ref[...] = proj(O_GA, D).astype(BF16)
        gb_ref[...] = proj(O_GB, D).astype(BF16)

    widths = [D, Q_W, KV_W, KV_W, GM_W, GM_W, D, D]
    return _call(
        body, name="fwd_in", grid=(L // tm,),
        in_specs=[_row(tm, D), _full((8, D)), _resident((IN_W, D)), _row(tm, 128), _row(tm, 128)],
        out_specs=[_row(tm, w) for w in widths],
        out_shape=[_sds((L, w), BF16) for w in widths],
        args=(x, modv, w_in, cos, sin), comm=comm)


def _k_ctx(ctx, modc, w_kv):
    C = ctx.shape[0]

    def body(c_ref, mod_ref, w_ref, hc_ref, kc_ref, vc_ref):
        n, _ = _ln(c_ref[...])
        hc = (n * (1.0 + mod_ref[1:2, :]) + mod_ref[0:1, :]).astype(BF16)
        hc_ref[...] = hc
        kv = lax.dot_general(hc, w_ref[...], NT, preferred_element_type=F32)
        kc_ref[...] = kv[:, :KV_W].astype(BF16)
        vc_ref[...] = kv[:, KV_W:].astype(BF16)

    return pl.pallas_call(
        body, name="fwd_ctx", grid=(1,),
        in_specs=[_full((C, D)), _full((8, D)), _full((2 * KV_W, D))],
        out_specs=[_full((C, D)), _full((C, KV_W)), _full((C, KV_W))],
        out_shape=[_sds((C, D), BF16), _sds((C, KV_W), BF16), _sds((C, KV_W), BF16)],
        compiler_params=_params(("arbitrary",)),
    )(ctx, modc, w_kv)


def _attn_bias():
    r = (np.arange(GQA_GROUP * BLK) & (BLK - 1))[:, None]
    j = np.arange(3 * BLK)[None, :]
    band = np.abs(j - BLK - r) <= BLK
    variants = [band & (j >= BLK), band, band & (j < 2 * BLK)]
    return jnp.asarray(np.stack([np.where(v, 0.0, NEG_INF) for v in variants]), F32)


def _bias_spec(nb):
    return pl.BlockSpec((1, GQA_GROUP * BLK, 3 * BLK),
                        lambda n: (jnp.where(n == 0, 0, jnp.where(n >= nb - 1, 2, 1)), 0, 0))


def _masked(s, bias, C):
    return jnp.concatenate([s[:, :C], s[:, C:] + bias], axis=1)


def _sink_col(sink_ref, hk):
    grp = lax.broadcasted_iota(jnp.int32, (GQA_GROUP * BLK, 1), 0) >> 7
    col = jnp.full((GQA_GROUP * BLK, 1), sink_ref[hk * GQA_GROUP], F32)
    for g in range(1, GQA_GROUP):
        col = jnp.where(grp == g, sink_ref[hk * GQA_GROUP + g], col)
    return col


def _kv_specs(nb):
    prev = pl.BlockSpec((BLK, KV_W), lambda n: (jnp.clip(n - 1, 0, nb - 1), 0))
    cur = pl.BlockSpec((BLK, KV_W), lambda n: (jnp.minimum(n, nb - 1), 0))
    nxt = pl.BlockSpec((BLK, KV_W), lambda n: (jnp.minimum(n + 1, nb - 1), 0))
    return [prev, cur, nxt]


def _k_attn(sink, q, k, v, kc, vc, bias, comm=None):
    L = q.shape[0]
    C = kc.shape[0]
    nb = L // BLK

    def body(sink_ref, q_ref, kp_ref, kn_ref, kx_ref, vp_ref, vn_ref, vx_ref, kc_ref, vc_ref, bias_ref, ya_ref, lse_ref):
        band = bias_ref[0]
        kvh = range(N_KV_HEADS)
        sl = [slice(hk * HEAD_DIM, (hk + 1) * HEAD_DIM) for hk in kvh]
        kcat = [jnp.concatenate([kc_ref[:, s_], kp_ref[:, s_], kn_ref[:, s_], kx_ref[:, s_]], axis=0) for s_ in sl]
        vcat = [jnp.concatenate([vc_ref[:, s_], vp_ref[:, s_], vn_ref[:, s_], vx_ref[:, s_]], axis=0) for s_ in sl]
        qg = [jnp.concatenate(
            [q_ref[:, (hk * GQA_GROUP + g) * HEAD_DIM:(hk * GQA_GROUP + g + 1) * HEAD_DIM] for g in range(GQA_GROUP)],
            axis=0) for hk in kvh]
        s = [_masked(lax.dot_general(qg[hk], kcat[hk], NT, preferred_element_type=F32), band, C) for hk in kvh]
        for hk in kvh:
            sink_c = _sink_col(sink_ref, hk)
            m = jnp.maximum(jnp.max(s[hk], axis=1, keepdims=True), sink_c)
            p = jnp.exp(s[hk] - m)
            den = jnp.sum(p, axis=1, keepdims=True) + jnp.exp(sink_c - m)
            o = jnp.dot(p.astype(BF16), vcat[hk], preferred_element_type=F32) * (1.0 / den)
            lse = m + jnp.log(den)
            for g in range(GQA_GROUP):
                h = hk * GQA_GROUP + g
                ya_ref[:, h * HEAD_DIM:(h + 1) * HEAD_DIM] = o[g * BLK:(g + 1) * BLK, :].astype(BF16)
                lse_ref[:, h:h + 1] = lse[g * BLK:(g + 1) * BLK, :]

    kv3 = _kv_specs(nb)
    return _call(
        body, name="fwd_attn", grid=(nb,),
        in_specs=[pl.BlockSpec(memory_space=pltpu.SMEM), _row(BLK, Q_W)] + kv3 + kv3
                 + [_full((C, KV_W)), _full((C, KV_W)), _bias_spec(nb)],
        out_specs=[_row(BLK, Q_W), _row(BLK, N_Q_HEADS)],
        out_shape=[_sds((L, Q_W), BF16), _sds((L, N_Q_HEADS), F32)],
        args=(sink, q, k, k, k, v, v, v, kc, vc, bias), comm=comm)


GMLP_CHUNKS = 4


def _split_pair(t):
    low = lax.broadcasted_iota(jnp.int32, t.shape, 1) < GROUP_DIM
    zero = jnp.zeros_like(t)
    return jnp.where(low, t, zero), jnp.where(low, zero, t)


def _gmlp_spatial(w_ref, t_b, nch):
    rows = []
    for c in range(nch):
        tiles = []
        for pr in range(N_GROUPS // 2):
            lo, hi = _split_pair(t_b[c * BLK:(c + 1) * BLK, pr * 128:(pr + 1) * 128])
            tiles.append(jnp.dot(w_ref[2 * pr], lo, preferred_element_type=F32)
                         + jnp.dot(w_ref[2 * pr + 1], hi, preferred_element_type=F32))
        rows.append(jnp.concatenate(tiles, axis=1))
    return jnp.concatenate(rows, axis=0)


def _gmlp_fwd_vals(u, vb, lnv_ref, ws_ref, bsp_ref, nch):
    uf = u.astype(F32)
    vf = vb.astype(F32)
    gu, tu = _gelu(uf)
    gv, tv = _gelu(vf)
    vhat, rstd = _ln(gv)
    vn = (vhat * lnv_ref[0:1, :] + lnv_ref[1:2, :]).astype(BF16)
    s = _gmlp_spatial(ws_ref, vn, nch) + jnp.concatenate([bsp_ref[...]] * nch, axis=0)
    return uf, vf, gu, tu, tv, vhat, rstd, vn, s


def _k_gmlp(u, vb, lnv, ws, bsp):
    L = u.shape[0]
    nch = min(GMLP_CHUNKS, L // BLK)
    tm = nch * BLK

    def body(u_ref, vb_ref, lnv_ref, ws_ref, bsp_ref, yb_ref):
        _, _, gu, _, _, _, _, _, s = _gmlp_fwd_vals(u_ref[...], vb_ref[...], lnv_ref, ws_ref, bsp_ref, nch)
        yb_ref[...] = (gu * s).astype(BF16)

    return pl.pallas_call(
        body, name="fwd_gmlp", grid=(L // tm,),
        in_specs=[_row(tm, GM_W), _row(tm, GM_W), _full((8, GM_W)), _full((N_GROUPS, BLK, BLK)), _full((BLK, GM_W))],
        out_specs=_row(tm, GM_W),
        out_shape=_sds((L, GM_W), BF16),
        compiler_params=_params(("arbitrary",)),
    )(u, vb, lnv, ws, bsp)


def _k_merge(x, ya, yb, ga, gb, w_a, w_b, w_o, modv, lnv, tm):
    L = x.shape[0]

    def body(x_ref, ya_ref, yb_ref, ga_ref, gb_ref, wa_ref, wb_ref, wo_ref, mod_ref, ln_ref,
             mg_ref, mix_ref, xm_ref, h2_ref):
        a = jnp.dot(ya_ref[...], wa_ref[...], preferred_element_type=F32)
        b = jnp.dot(yb_ref[...], wb_ref[...], preferred_element_type=F32)
        merged = (_sigmoid(ga_ref[...].astype(F32)) * a + _sigmoid(gb_ref[...].astype(F32)) * b).astype(BF16)
        mg_ref[...] = merged
        mix = jnp.dot(merged, wo_ref[...], preferred_element_type=F32)
        mix_ref[...] = mix.astype(BF16)
        r1 = ALPHA * x_ref[...] + mod_ref[2:3, :] * mix
        r1hat, _ = _ln(r1)
        xm = r1hat * ln_ref[0:1, :] + ln_ref[1:2, :]
        xm_ref[...] = xm
        n2, _ = _ln(xm)
        h2_ref[...] = (n2 * (1.0 + mod_ref[4:5, :]) + mod_ref[3:4, :]).astype(BF16)

    return pl.pallas_call(
        body, name="fwd_merge", grid=(L // tm,),
        in_specs=[_row(tm, D), _row(tm, Q_W), _row(tm, GM_W), _row(tm, D), _row(tm, D),
                  _resident((Q_W, D)), _resident((GM_W, D)), _resident((D, D)), _full((8, D)), _full((8, D))],
        out_specs=[_row(tm, D)] * 4,
        out_shape=[_sds((L, D), BF16), _sds((L, D), BF16), _sds((L, D), F32), _sds((L, D), BF16)],
        compiler_params=_params(("arbitrary",)),
    )(x, ya, yb, ga, gb, w_a, w_b, w_o, modv, lnv)


FFN_CH = 1408


def _k_ffn(h2, xm, tgt, w_fi, w_fo, modv, lnv, tm):
    L = h2.shape[0]

    def body(h2_ref, xm_ref, t_ref, wi_ref, wo_ref, mod_ref, ln_ref, gate_ref, up_ref, a_ref, dr2_ref, df_ref, acc_ref):
        @pl.when(pl.program_id(0) == 0)
        def _():
            acc_ref[...] = jnp.zeros_like(acc_ref)

        h2v = h2_ref[...]
        f = jnp.zeros((tm, D), F32)
        for j in range(FFN_H // FFN_CH):
            lo = j * FFN_CH
            gate = lax.dot_general(h2v, wi_ref[lo:lo + FFN_CH, :], NT, preferred_element_type=F32)
            up = lax.dot_general(h2v, wi_ref[FFN_H + lo:FFN_H + lo + FFN_CH, :], NT, preferred_element_type=F32)
            act = (gate * _sigmoid(gate) * up).astype(BF16)
            gate_ref[:, lo:lo + FFN_CH] = gate.astype(BF16)
            up_ref[:, lo:lo + FFN_CH] = up.astype(BF16)
            a_ref[:, lo:lo + FFN_CH] = act
            f = f + jnp.dot(act, wo_ref[lo:lo + FFN_CH, :], preferred_element_type=F32)
        gate2 = mod_ref[5:6, :]
        r2 = ALPHA * xm_ref[...] + gate2 * f
        r2hat, rstd = _ln(r2)
        y = r2hat * ln_ref[2:3, :] + ln_ref[3:4, :]
        err = y - t_ref[...]
        dy = err * (1.0 / D)
        dr2 = _ln_bwd(dy * ln_ref[2:3, :], r2hat, rstd)
        dr2_ref[...] = dr2
        df_ref[...] = (gate2 * dr2).astype(BF16)
        acc_ref[0:1, :] += _colsum(dy * r2hat)
        acc_ref[1:2, :] += _colsum(dy)
        acc_ref[2:3, :] += _colsum(dr2 * f)
        acc_ref[3:4, :] += _colsum(err * err) * (0.5 / D)

    return pl.pallas_call(
        body, name="fwd_ffn", grid=(L // tm,),
        in_specs=[_row(tm, D), _row(tm, D), _row(tm, D), _resident((2 * FFN_H, D)), _resident((FFN_H, D)),
                  _full((8, D)), _full((8, D))],
        out_specs=[_row(tm, FFN_H)] * 3 + [_row(tm, D), _row(tm, D), _full((8, D))],
        out_shape=[_sds((L, FFN_H), BF16)] * 3 + [_sds((L, D), F32), _sds((L, D), BF16), _sds((8, D), F32)],
        compiler_params=_params(("arbitrary",)),
    )(h2, xm, tgt, w_fi, w_fo, modv, lnv)


def _k_ffn_bwd(df, gate, up, xm, dr2, x, mix, w_fi, w_fo, modv, lnv, tm):
    L = df.shape[0]

    def body(df_ref, gate_ref, up_ref, xm_ref, dr2_ref, x_ref, mix_ref, wi_ref, wo_ref, mod_ref, ln_ref,
             dF_ref, dmix_ref, dxp_ref, acc_ref):
        @pl.when(pl.program_id(0) == 0)
        def _():
            acc_ref[...] = jnp.zeros_like(acc_ref)

        dfv = df_ref[...]
        chunks = [j * FFN_CH for j in range(FFN_H // FFN_CH)]
        das = [lax.dot_general(dfv, wo_ref[lo:lo + FFN_CH, :], NT, preferred_element_type=F32) for lo in chunks]
        n2, rstd2 = _ln(xm_ref[...])
        mixf = mix_ref[...].astype(F32)
        gate1 = mod_ref[2:3, :]
        r1hat, rstd1 = _ln(ALPHA * x_ref[...] + gate1 * mixf)
        dh2 = jnp.zeros((tm, D), F32)
        for lo, da in zip(chunks, das):
            gate = gate_ref[:, lo:lo + FFN_CH].astype(F32)
            upv = up_ref[:, lo:lo + FFN_CH].astype(F32)
            sg = _sigmoid(gate)
            d_gate = (da * upv * (sg * (1.0 + gate * (1.0 - sg)))).astype(BF16)
            d_up = (da * (gate * sg)).astype(BF16)
            dF_ref[:, lo:lo + FFN_CH] = d_gate
            dF_ref[:, FFN_H + lo:FFN_H + lo + FFN_CH] = d_up
            dh2 = dh2 + jnp.dot(d_gate, wi_ref[lo:lo + FFN_CH, :], preferred_element_type=F32)
            dh2 = dh2 + jnp.dot(d_up, wi_ref[FFN_H + lo:FFN_H + lo + FFN_CH, :], preferred_element_type=F32)
        acc_ref[0:1, :] += _colsum(dh2)
        acc_ref[1:2, :] += _colsum(dh2 * n2)
        dxm = ALPHA * dr2_ref[...] + _ln_bwd(dh2 * (1.0 + mod_ref[4:5, :]), n2, rstd2)
        acc_ref[2:3, :] += _colsum(dxm * r1hat)
        acc_ref[3:4, :] += _colsum(dxm)
        dr1 = _ln_bwd(dxm * ln_ref[0:1, :], r1hat, rstd1)
        dmix_ref[...] = (gate1 * dr1).astype(BF16)
        dxp_ref[...] = ALPHA * dr1
        acc_ref[4:5, :] += _colsum(dr1 * mixf)

    return pl.pallas_call(
        body, name="bwd_ffn", grid=(L // tm,),
        in_specs=[_row(tm, D), _row(tm, FFN_H), _row(tm, FFN_H), _row(tm, D), _row(tm, D), _row(tm, D), _row(tm, D),
                  _resident((2 * FFN_H, D)), _resident((FFN_H, D)), _full((8, D)), _full((8, D))],
        out_specs=[_row(tm, 2 * FFN_H), _row(tm, D), _row(tm, D), _full((8, D))],
        out_shape=[_sds((L, 2 * FFN_H), BF16), _sds((L, D), BF16), _sds((L, D), F32), _sds((8, D), F32)],
        compiler_params=_params(("arbitrary",)),
    )(df, gate, up, xm, dr2, x, mix, w_fi, w_fo, modv, lnv)


def _k_merge_bwd(dmix, ya, yb, ga, gb, w_a, w_b, w_o, tm, comm=None):
    L = dmix.shape[0]

    def body(dmix_ref, ya_ref, yb_ref, ga_ref, gb_ref, wa_ref, wb_ref, wo_ref,
             dA_ref, dB_ref, dga_ref, dgb_ref, dya_ref, dyb_ref):
        dmg = lax.dot_general(dmix_ref[...], wo_ref[...], NT, preferred_element_type=F32)
        a = jnp.dot(ya_ref[...], wa_ref[...], preferred_element_type=F32)
        sa = _sigmoid(ga_ref[...].astype(F32))
        dA = (dmg * sa).astype(BF16)
        dA_ref[...] = dA
        dga_ref[...] = (dmg * a * (sa * (1.0 - sa))).astype(BF16)
        dya_ref[...] = lax.dot_general(dA, wa_ref[...], NT, preferred_element_type=F32).astype(BF16)
        b = jnp.dot(yb_ref[...], wb_ref[...], preferred_element_type=F32)
        sb = _sigmoid(gb_ref[...].astype(F32))
        dB = (dmg * sb).astype(BF16)
        dB_ref[...] = dB
        dgb_ref[...] = (dmg * b * (sb * (1.0 - sb))).astype(BF16)
        dyb_ref[...] = lax.dot_general(dB, wb_ref[...], NT, preferred_element_type=F32).astype(BF16)

    return _call(
        body, name="bwd_merge", grid=(L // tm,),
        in_specs=[_row(tm, D), _row(tm, Q_W), _row(tm, GM_W), _row(tm, D), _row(tm, D),
                  _resident((Q_W, D)), _resident((GM_W, D)), _resident((D, D))],
        out_specs=[_row(tm, D)] * 4 + [_row(tm, Q_W), _row(tm, GM_W)],
        out_shape=[_sds((L, D), BF16)] * 4 + [_sds((L, Q_W), BF16), _sds((L, GM_W), BF16)],
        args=(dmix, ya, yb, ga, gb, w_a, w_b, w_o), comm=comm)


def _k_gmlp_bwd(u, vb, dyb, lnv, ws, wst, bsp):
    L = u.shape[0]
    nch = min(GMLP_CHUNKS, L // BLK)
    tm = nch * BLK

    def body(u_ref, vb_ref, dyb_ref, lnv_ref, ws_ref, wst_ref, bsp_ref, du_ref, dvb_ref, gws_ref, gbst_ref, gln_ref):
        @pl.when(pl.program_id(0) == 0)
        def _():
            gws_ref[...] = jnp.zeros_like(gws_ref)
            gbst_ref[...] = jnp.zeros_like(gbst_ref)
            gln_ref[...] = jnp.zeros_like(gln_ref)

        uf, vf, gu, tu, tv, vhat, rstd, vn, s = _gmlp_fwd_vals(u_ref[...], vb_ref[...], lnv_ref, ws_ref, bsp_ref, nch)
        dyb_f = dyb_ref[...].astype(F32)
        du_ref[...] = (dyb_f * s * _gelu_grad(uf, tu)).astype(BF16)
        ds = dyb_f * gu
        ds_b = ds.astype(BF16)
        for pr in range(N_GROUPS // 2):
            lanes = slice(pr * 128, (pr + 1) * 128)
            gw_lo = gw_hi = ds_sum = None
            for c in range(nch):
                rows = slice(c * BLK, (c + 1) * BLK)
                lo, hi = _split_pair(ds_b[rows, lanes])
                t_lo = lax.dot_general(lo, vn[rows, lanes], NT, preferred_element_type=F32)
                t_hi = lax.dot_general(hi, vn[rows, lanes], NT, preferred_element_type=F32)
                gw_lo = t_lo if c == 0 else gw_lo + t_lo
                gw_hi = t_hi if c == 0 else gw_hi + t_hi
                ds_sum = ds[rows, lanes] if c == 0 else ds_sum + ds[rows, lanes]
            gws_ref[2 * pr] += gw_lo
            gws_ref[2 * pr + 1] += gw_hi
            b_lo, b_hi = _split_pair(ds_sum)
            gbst_ref[:, 2 * pr:2 * pr + 1] += jnp.sum(b_lo, axis=1, keepdims=True)
            gbst_ref[:, 2 * pr + 1:2 * pr + 2] += jnp.sum(b_hi, axis=1, keepdims=True)
        dvn = _gmlp_spatial(wst_ref, ds_b, nch)
        gln_ref[0:1, :] += _colsum(dvn * vhat)
        gln_ref[1:2, :] += _colsum(dvn)
        dgv = _ln_bwd(dvn * lnv_ref[0:1, :], vhat, rstd)
        dvb_ref[...] = (dgv * _gelu_grad(vf, tv)).astype(BF16)

    return pl.pallas_call(
        body, name="bwd_gmlp", grid=(L // tm,),
        in_specs=[_row(tm, GM_W)] * 3 + [_full((8, GM_W)), _full((N_GROUPS, BLK, BLK)), _full((N_GROUPS, BLK, BLK)),
                                         _full((BLK, GM_W))],
        out_specs=[_row(tm, GM_W), _row(tm, GM_W), _full((N_GROUPS, BLK, BLK)), _full((BLK, N_GROUPS)), _full((8, GM_W))],
        out_shape=[_sds((L, GM_W), BF16), _sds((L, GM_W), BF16), _sds((N_GROUPS, BLK, BLK), F32),
                   _sds((BLK, N_GROUPS), F32), _sds((8, GM_W), F32)],
        compiler_params=_params(("arbitrary",)),
    )(u, vb, dyb, lnv, ws, wst, bsp)


def _k_attn_bwd(sink, q, k, v, kc, vc, dya, lse, cos, sin, bias, comm=None):
    L = q.shape[0]
    C = kc.shape[0]
    nb = L // BLK
    NK = C + 3 * BLK

    def body(sink_ref, q_ref, kp_ref, kn_ref, kx_ref, vp_ref, vn_ref, vx_ref, kc_ref, vc_ref, do_ref, lse_ref,
             cq_ref, sq_ref, ck_ref, sk_ref, bias_ref,
             dq_ref, dk_ref, dv_ref, dkc_ref, dvc_ref, dsink_ref,
             dq_scr, ck_scr, cv_scr, kp_acc, kc_acc, vp_acc, vc_acc):
        n = pl.program_id(0)

        @pl.when(n == 0)
        def _():
            for r in (kp_acc, kc_acc, vp_acc, vc_acc, dkc_ref, dvc_ref, dsink_ref):
                r[...] = jnp.zeros_like(r)

        @pl.when(n < nb)
        def _():
            band = bias_ref[0]
            sl = [slice(hk * HEAD_DIM, (hk + 1) * HEAD_DIM) for hk in range(N_KV_HEADS)]
            heads = [[hk * GQA_GROUP + g for g in range(GQA_GROUP)] for hk in range(N_KV_HEADS)]

            def scores(hk):
                kcat = jnp.concatenate([kc_ref[:, sl[hk]], kp_ref[:, sl[hk]], kn_ref[:, sl[hk]], kx_ref[:, sl[hk]]], axis=0)
                qg = jnp.concatenate([q_ref[:, h * HEAD_DIM:(h + 1) * HEAD_DIM] for h in heads[hk]], axis=0)
                s = _masked(lax.dot_general(qg, kcat, NT, preferred_element_type=F32), band, C)
                vcat = jnp.concatenate([vc_ref[:, sl[hk]], vp_ref[:, sl[hk]], vn_ref[:, sl[hk]], vx_ref[:, sl[hk]]], axis=0)
                dog = jnp.concatenate([do_ref[:, h * HEAD_DIM:(h + 1) * HEAD_DIM] for h in heads[hk]], axis=0)
                dp = lax.dot_general(dog, vcat, NT, preferred_element_type=F32)
                return kcat, qg, dog, s, dp

            def softmax_bwd(hk, s, dp):
                lse_c = jnp.concatenate([lse_ref[:, h:h + 1] for h in heads[hk]], axis=0)
                p = jnp.exp(s - lse_c)
                delta = jnp.sum(p * dp, axis=1, keepdims=True)
                ds = (p * (dp - delta)).astype(BF16)
                p_sink = jnp.exp(_sink_col(sink_ref, hk) - lse_c) * delta
                return p.astype(BF16), ds, p_sink

            def put_dq(hk, dqs, p_sink):
                for g, h in enumerate(heads[hk]):
                    dq_scr[:, h * HEAD_DIM:(h + 1) * HEAD_DIM] = dqs[g * BLK:(g + 1) * BLK, :]
                    tot = jnp.sum(p_sink[g * BLK:(g + 1) * BLK, :], axis=0, keepdims=True)
                    dsink_ref[h:h + 1, :] -= jnp.broadcast_to(tot, (1, 128))

            kcat0, qg0, dog0, s0, dp0 = scores(0)
            kcat1, qg1, dog1, s1, dp1 = scores(1)
            pb0, ds0, psink0 = softmax_bwd(0, s0, dp0)
            dqs0 = jnp.dot(ds0, kcat0, preferred_element_type=F32)
            pb1, ds1, psink1 = softmax_bwd(1, s1, dp1)
            ck_scr[:, sl[0]] = lax.dot_general(ds0, qg0, TN, preferred_element_type=F32)
            cv_scr[:, sl[0]] = lax.dot_general(pb0, dog0, TN, preferred_element_type=F32)
            put_dq(0, dqs0, psink0)
            dqs1 = jnp.dot(ds1, kcat1, preferred_element_type=F32)
            ck_scr[:, sl[1]] = lax.dot_general(ds1, qg1, TN, preferred_element_type=F32)
            put_dq(1, dqs1, psink1)
            cq, sq = cq_ref[...], sq_ref[...]
            for i in range(4):
                dq_ref[:, i * 128:(i + 1) * 128] = _unrope(dq_scr[:, i * 128:(i + 1) * 128] * Q_SCALE, cq, sq).astype(BF16)
            cv_scr[:, sl[1]] = lax.dot_general(pb1, dog1, TN, preferred_element_type=F32)
            dkc_ref[...] += ck_scr[0:C, :]
            dvc_ref[...] += cv_scr[0:C, :]

        @pl.when(n >= nb)
        def _():
            ck_scr[...] = jnp.zeros_like(ck_scr)
            cv_scr[...] = jnp.zeros_like(cv_scr)

        dk_ref[...] = _unrope(kp_acc[...] + ck_scr[C:C + BLK, :], ck_ref[...], sk_ref[...]).astype(BF16)
        dv_ref[...] = (vp_acc[...] + cv_scr[C:C + BLK, :]).astype(BF16)
        kp_acc[...] = kc_acc[...] + ck_scr[C + BLK:C + 2 * BLK, :]
        vp_acc[...] = vc_acc[...] + cv_scr[C + BLK:C + 2 * BLK, :]
        kc_acc[...] = ck_scr[C + 2 * BLK:C + 3 * BLK, :]
        vc_acc[...] = cv_scr[C + 2 * BLK:C + 3 * BLK, :]

    kv3 = _kv_specs(nb)
    cur = lambda w: pl.BlockSpec((BLK, w), lambda n: (jnp.minimum(n, nb - 1), 0))
    late = lambda w: pl.BlockSpec((BLK, w), lambda n: (jnp.maximum(n - 1, 0), 0))
    return _call(
        body, name="bwd_attn", grid=(nb + 1,),
        in_specs=[pl.BlockSpec(memory_space=pltpu.SMEM), cur(Q_W)] + kv3 + kv3
                 + [_full((C, KV_W)), _full((C, KV_W)), cur(Q_W), cur(N_Q_HEADS), cur(128), cur(128), late(128), late(128),
                    _bias_spec(nb)],
        out_specs=[cur(Q_W), late(KV_W), late(KV_W), _full((C, KV_W)), _full((C, KV_W)), _full((8, 128))],
        out_shape=[_sds((L, Q_W), BF16), _sds((L, KV_W), BF16), _sds((L, KV_W), BF16),
                   _sds((C, KV_W), F32), _sds((C, KV_W), F32), _sds((8, 128), F32)],
        scratch=[pltpu.VMEM((BLK, Q_W), F32), pltpu.VMEM((NK, KV_W), F32), pltpu.VMEM((NK, KV_W), F32)]
                + [pltpu.VMEM((BLK, KV_W), F32)] * 4,
        args=(sink, q, k, k, k, v, v, v, kc, vc, dya, lse, cos, sin, cos, sin, bias), comm=comm)


def _k_ctx_bwd(ctx, modc, hc, dkc, dvc, w_kv):
    C = ctx.shape[0]

    def body(c_ref, mod_ref, hc_ref, dkc_ref, dvc_ref, w_ref, gw_ref, dmod_ref):
        dkv = jnp.concatenate([dkc_ref[...], dvc_ref[...]], axis=1).astype(BF16)
        gw_ref[...] = lax.dot_general(dkv, hc_ref[...], TN, preferred_element_type=F32)
        dhc = jnp.dot(dkv, w_ref[...], preferred_element_type=F32)
        n, _ = _ln(c_ref[...])
        dmod_ref[...] = jnp.zeros_like(dmod_ref)
        dmod_ref[0:1, :] = _colsum(dhc)
        dmod_ref[1:2, :] = _colsum(dhc * n)

    return pl.pallas_call(
        body, name="bwd_ctx", grid=(1,),
        in_specs=[_full((C, D)), _full((8, D)), _full((C, D)), _full((C, KV_W)), _full((C, KV_W)), _full((2 * KV_W, D))],
        out_specs=[_full((2 * KV_W, D)), _full((8, D))],
        out_shape=[_sds((2 * KV_W, D), F32), _sds((8, D), F32)],
        compiler_params=_params(("arbitrary",)),
    )(ctx, modc, hc, dkc, dvc, w_kv)


def _k_in_bwd(dq, dk, dv, du, dvb, dga, dgb, x, dxp, w_in, modv, tm, comm=None):
    L = x.shape[0]
    parts = [(O_Q, Q_W), (O_K, KV_W), (O_V, KV_W), (O_U, GM_W), (O_VB, GM_W), (O_GA, D), (O_GB, D)]

    def body(dq_ref, dk_ref, dv_ref, du_ref, dvb_ref, dga_ref, dgb_ref, x_ref, dxp_ref, w_ref, mod_ref,
             dP_ref, gx_ref, acc_ref):
        @pl.when(pl.program_id(0) == 0)
        def _():
            acc_ref[...] = jnp.zeros_like(acc_ref)

        for (lo, width), r in zip(parts, (dq_ref, dk_ref, dv_ref, du_ref, dvb_ref, dga_ref, dgb_ref)):
            dP_ref[:, lo:lo + width] = r[...]
        n1, rstd1 = _ln(x_ref[...])
        dh = jnp.dot(dP_ref[...], w_ref[...], preferred_element_type=F32)
        acc_ref[0:1, :] += _colsum(dh)
        acc_ref[1:2, :] += _colsum(dh * n1)
        gx_ref[...] = dxp_ref[...] + _ln_bwd(dh * (1.0 + mod_ref[1:2, :]), n1, rstd1)

    return _call(
        body, name="bwd_in", grid=(L // tm,),
        in_specs=[_row(tm, w) for _, w in parts] + [_row(tm, D), _row(tm, D), _resident((IN_W, D)), _full((8, D))],
        out_specs=[_row(tm, IN_W), _row(tm, D), _full((8, D))],
        out_shape=[_sds((L, IN_W), BF16), _sds((L, D), F32), _sds((8, D), F32)],
        args=(dq, dk, dv, du, dvb, dga, dgb, x, dxp, w_in, modv), comm=comm)


def _wgrad(a, b, name, tk, tt, comm=None, extra=None):
    T, K = a.shape
    N = b.shape[1]
    nt = T // tt

    def body(*refs):
        a_ref, b_ref = refs[:2]
        o_ref, acc_ref = refs[-2:]
        j, t = pl.program_id(0), pl.program_id(1)

        @pl.when(t == 0)
        def _():
            acc_ref[...] = jnp.zeros_like(acc_ref)

        acc_ref[...] += lax.dot_general(a_ref[...], b_ref[...], TN, preferred_element_type=F32)

        if extra is not None:
            lo, rows = extra[0] % tk, extra[1].shape[0]

            @pl.when((t == nt - 1) & (j == extra[0] // tk))
            def _():
                acc_ref[lo:lo + rows, :] += refs[2][...]

        @pl.when(t == nt - 1)
        def _():
            o_ref[...] = acc_ref[...].astype(BF16)

    extra_specs = [] if extra is None else [pl.BlockSpec(extra[1].shape, lambda j, t: (0, 0))]
    (out,), got = _call(
        body, name=name, grid=(K // tk, nt),
        in_specs=[pl.BlockSpec((tt, tk), lambda j, t: (t, j)), pl.BlockSpec((tt, N), lambda j, t: (t, 0))] + extra_specs,
        out_specs=[pl.BlockSpec((tk, N), lambda j, t: (j, 0))],
        out_shape=[_sds((K, N), BF16)],
        scratch=[pltpu.VMEM((tk, N), F32)],
        args=(a, b) + (() if extra is None else (extra[1],)), comm=comm)
    return (out, got) if comm is not None else out


def _adamw_reduce(parts, w, m, v, name, tr):
    R, C = w.shape
    n_parts = parts.shape[0]

    def body(p_ref, w_ref, m_ref, v_ref, g_ref, d_ref, m2_ref, v2_ref):
        g = p_ref[0].astype(F32)
        for i in range(1, n_parts):
            g = g + p_ref[i].astype(F32)
        delta, m2, v2 = _adamw(w_ref[...], g, m_ref[...], v_ref[...])
        g_ref[...] = g
        d_ref[...] = delta
        m2_ref[...] = m2
        v2_ref[...] = v2

    spec = _row(tr, C)
    return pl.pallas_call(
        body, name=name, grid=(R // tr,),
        in_specs=[pl.BlockSpec((n_parts, tr, C), lambda i: (0, i, 0)), spec, spec, spec],
        out_specs=[spec] * 4,
        out_shape=[_sds((R, C), F32)] * 4,
        compiler_params=_params(("arbitrary",)),
    )(parts, w, m, v)


def _small_reduce(gath, gath_ws):
    def body(g_ref, w_ref, out_ref):
        tot = g_ref[0]
        wsum = w_ref[0]
        for i in range(1, N_DEV):
            tot = tot + g_ref[i]
            wsum = wsum + w_ref[i]
        out_ref[0:16, :] = tot
        out_ref[0:2, :] = tot[0:2, :] + tot[6:8, :]
        out_ref[15:16, :] = jnp.broadcast_to(jnp.sum(tot[15:16, :], axis=1, keepdims=True), (1, D))
        out_ref[16:SMALL_ROWS, :] = wsum

    return pl.pallas_call(
        body, name="small_reduce", grid=(1,),
        in_specs=[_full((N_DEV, 16, D)), _full((N_DEV, SMALL_ROWS - 16, D))],
        out_specs=_full((SMALL_ROWS, D)),
        out_shape=_sds((SMALL_ROWS, D), F32),
        compiler_params=_params(("arbitrary",)),
    )(gath, gath_ws)


def _small_adamw(w, g, m, v, name):
    shape = w.shape

    def body(w_ref, g_ref, m_ref, v_ref, d_ref, m2_ref, v2_ref):
        delta, m2, v2 = _adamw(w_ref[...], g_ref[...], m_ref[...], v_ref[...])
        d_ref[...] = delta
        m2_ref[...] = m2
        v2_ref[...] = v2

    return pl.pallas_call(
        body, name=name, grid=(1,),
        in_specs=[_full(shape)] * 4, out_specs=[_full(shape)] * 3,
        out_shape=[_sds(shape, F32)] * 3,
        compiler_params=_params(("arbitrary",)),
    )(w, g, m, v)


def _cctx_finish(gath, c_ctx, m, v):
    def body(g_ref, c_ref, m_ref, v_ref, gr_ref, d_ref, m2_ref, v2_ref):
        ds = g_ref[0]
        for i in range(1, N_DEV):
            ds = ds + g_ref[i]
        c = c_ref[...]
        sg = _sigmoid(c)
        g = ds * (sg * (1.0 + c * (1.0 - sg)))
        delta, m2, v2 = _adamw(c, g, m_ref[...], v_ref[...])
        gr_ref[...] = g
        d_ref[...] = delta
        m2_ref[...] = m2
        v2_ref[...] = v2

    return pl.pallas_call(
        body, name="cctx_finish", grid=(1,),
        in_specs=[_full((N_DEV, 8, D))] + [_full((8, D))] * 3, out_specs=[_full((8, D))] * 4,
        out_shape=[_sds((8, D), F32)] * 4,
        compiler_params=_params(("arbitrary",)),
    )(gath, c_ctx, m, v)


def _pad_rows(a, rows):
    return jnp.concatenate([a, jnp.zeros((rows - a.shape[0], a.shape[1]), a.dtype)], axis=0)


def _pack_small(b_ada, ln1_g, ln1_b, ln2_g, ln2_b, gm_g, gm_b, b_sp, sink, w_sp):
    rows = [b_ada.reshape(6, D), jnp.zeros((2, D), F32), ln1_g.reshape(1, D), ln1_b.reshape(1, D), ln2_g.reshape(1, D),
            ln2_b.reshape(1, D), jnp.concatenate([gm_g.reshape(1, GM_W), gm_b.reshape(1, GM_W)], axis=1),
            b_sp.reshape(1, D), _pad_rows(sink.reshape(1, N_Q_HEADS).T, D).T.reshape(1, D), jnp.zeros((1, D), F32),
            w_sp.reshape(N_GROUPS * BLK * BLK // D, D)]
    return jnp.concatenate(rows, axis=0)


def _unpack_small(p):
    return dict(b_ada=p[0:6].reshape(1, 6 * D), ln1_g=p[8:9], ln1_b=p[9:10], ln2_g=p[10:11], ln2_b=p[11:12],
                gmlp_ln_g=p[12:13, :GM_W], gmlp_ln_b=p[12:13, GM_W:], b_spatial=p[13:14].reshape(1, N_GROUPS, BLK),
                attn_sink=p[14:15, :N_Q_HEADS], w_spatial=p[16:].reshape(1, N_GROUPS, BLK, BLK))


def kernel(x, c, ctx, c_ctx, w_ada, b_ada, w_in, attn_sink, gmlp_ln_g, gmlp_ln_b, w_spatial, b_spatial, w_branch_a, w_branch_b, w_out, ln1_g, ln1_b, w_ffn_in, w_ffn_out, ln2_g, ln2_b, loss_target, m_c_ctx, m_w_ada, m_b_ada, m_w_in, m_attn_sink, m_gmlp_ln_g, m_gmlp_ln_b, m_w_spatial, m_b_spatial, m_w_branch_a, m_w_branch_b, m_w_out, m_ln1_g, m_ln1_b, m_w_ffn_in, m_w_ffn_out, m_ln2_g, m_ln2_b, v_c_ctx, v_w_ada, v_b_ada, v_w_in, v_attn_sink, v_gmlp_ln_g, v_gmlp_ln_b, v_w_spatial, v_b_spatial, v_w_branch_a, v_w_branch_b, v_w_out, v_ln1_g, v_ln1_b, v_w_ffn_in, v_w_ffn_out, v_ln2_g, v_ln2_b):
    L = x.shape[1]
    me = 4 * lax.axis_index("x") + 2 * lax.axis_index("y") + lax.axis_index("c")
    x2, tgt, ctx2 = x[0], loss_target[0], ctx[0]
    tm_in = min(512, L)
    tm = min(256, L)
    tt = min(1024, L)

    transposed = ("w_in", "w_ffn_in")
    tr = lambda kname, a: a.T if kname in transposed else a
    big = dict(w_in=w_in[0].T, w_branch_a=w_branch_a[0], w_branch_b=w_branch_b[0], w_out=w_out[0],
               w_ffn_in=w_ffn_in[0].T, w_ffn_out=w_ffn_out[0])
    col_sharded = ("w_branch_a", "w_branch_b")
    shard_bf = {k: a.astype(BF16) for k, a in big.items()}

    def assemble(kname, g):
        if kname in col_sharded:
            return g.transpose(1, 0, 2).reshape(g.shape[1], N_DEV * g.shape[2])
        return g.reshape(N_DEV * g.shape[1], g.shape[2])

    def to_blocks(kname, g):
        if kname in col_sharded:
            return g.reshape(g.shape[0], N_DEV, g.shape[1] // N_DEV).transpose(1, 0, 2)
        return g.reshape(N_DEV, g.shape[0] // N_DEV, g.shape[1])

    full = {}
    n_ada = w_ada.shape[2]
    b_my = lax.dynamic_slice(b_ada, (0, me * n_ada), (1, n_ada))
    act, mod_all, got = _prologue(_pad_rows(c, 8), _pad_rows(c_ctx[None, :], 8), w_ada[0], b_my,
                                  _Comm(gather=[shard_bf["w_in"]]))
    full["w_in"] = assemble("w_in", got[0])
    mod_all = mod_all.transpose(1, 0, 2).reshape(16, 6 * D)
    modv = _pad_rows(lax.dynamic_slice(mod_all, (me, 0), (1, 6 * D)).reshape(6, D), 8)
    modc = _pad_rows(mod_all[8].reshape(6, D), 8)

    lnv = _pad_rows(jnp.concatenate([ln1_g, ln1_b, ln2_g, ln2_b], axis=0), 8)
    gm_lnv = _pad_rows(jnp.concatenate([gmlp_ln_g, gmlp_ln_b], axis=0), 8)
    ws_b = w_spatial[0].astype(BF16)
    wst_b = ws_b.transpose(0, 2, 1)
    bsp = jnp.repeat(b_spatial[0].T, GROUP_DIM, axis=1)
    sink = attn_sink[0]
    cos, sin = _rope_tables(L)
    bias = _attn_bias()
    w_kv = full["w_in"][O_K:O_K + 2 * KV_W, :]

    (h, q, k, v, u, vb, ga, gb), got = _k_in(
        x2, modv, full["w_in"], cos, sin, tm_in,
        comm=_Comm(gather=[shard_bf[kname] for kname in ("w_branch_a", "w_branch_b", "w_out", "w_ffn_out")]))
    for kname, g in zip(("w_branch_a", "w_branch_b", "w_out", "w_ffn_out"), got):
        full[kname] = assemble(kname, g)
    hc, kc, vc = _k_ctx(ctx2, modc, w_kv)
    (ya, lse), got = _k_attn(sink, q, k, v, kc, vc, bias, comm=_Comm(gather=[shard_bf["w_ffn_in"]]))
    full["w_ffn_in"] = assemble("w_ffn_in", got[0])
    yb = _k_gmlp(u, vb, gm_lnv, ws_b, bsp)
    merged, mix, xm, h2 = _k_merge(x2, ya, yb, ga, gb, full["w_branch_a"], full["w_branch_b"], full["w_out"], modv, lnv, tm_in)
    gate, up, act_f, dr2, df, acc_f = _k_ffn(h2, xm, tgt, full["w_ffn_in"], full["w_ffn_out"], modv, lnv, tm_in)

    dF, dmix, dxp, acc_b = _k_ffn_bwd(df, gate, up, xm, dr2, x2, mix, full["w_ffn_in"], full["w_ffn_out"], modv, lnv, tm)
    blk_fi = to_blocks("w_ffn_in", _wgrad(dF, h2, "wgrad_ffn_in", 1408, tt))
    blk_fo = to_blocks("w_ffn_out", _wgrad(act_f, df, "wgrad_ffn_out", 1408, tt))
    (dA, dB, dga, dgb, dya, dyb), (rcv_fo,) = _k_merge_bwd(
        dmix, ya, yb, ga, gb, full["w_branch_a"], full["w_branch_b"], full["w_out"], tm_in, comm=_Comm(scatter=[blk_fo]))
    du, dvb, g_ws, g_bst, g_gln = _k_gmlp_bwd(u, vb, dyb, gm_lnv, ws_b, wst_b, bsp)
    (dq, dk, dv, dkc, dvc, g_sink), (gath_ws, rcv_fi) = _k_attn_bwd(
        sink, q, k, v, kc, vc, dya, lse, cos, sin, bias,
        comm=_Comm(gather=[g_ws.reshape(SMALL_ROWS - 16, D)], scatter=[blk_fi]))
    blk_a = to_blocks("w_branch_a", _wgrad(ya, dA, "wgrad_a", Q_W, tt))
    blk_b = to_blocks("w_branch_b", _wgrad(yb, dB, "wgrad_b", GM_W, tt))
    blk_o = to_blocks("w_out", _wgrad(merged, dmix, "wgrad_out", D, tt))
    (dP, grad_x, acc_i), _ = _k_in_bwd(dq, dk, dv, du, dvb, dga, dgb, x2, dxp, full["w_in"], modv, tm_in)
    g_ctx, dmodc = _k_ctx_bwd(ctx2, modc, hc, dkc, dvc, w_kv)
    gw_in, (rcv_a, rcv_b, rcv_o) = _wgrad(dP, h, "wgrad_in", 1280, tt, comm=_Comm(scatter=[blk_a, blk_b, blk_o]),
                                          extra=(O_K, g_ctx))

    dmod_x = jnp.concatenate([acc_i[0:2], acc_b[4:5], acc_b[0:2], acc_f[2:3]], axis=0)
    small = jnp.concatenate([
        dmod_x, dmodc[0:2], acc_b[2:4], acc_f[0:2],
        jnp.concatenate([g_gln[0:1], g_gln[1:2]], axis=1), g_bst.T.reshape(1, D),
        _pad_rows(g_sink[:, 0:1], D).T, acc_f[3:4]], axis=0)
    rcv_in, gath = _exchange_two_level(to_blocks("w_in", gw_in), small, "exchange_last")
    received = dict(w_in=rcv_in, w_branch_a=rcv_a, w_branch_b=rcv_b, w_out=rcv_o, w_ffn_in=rcv_fi, w_ffn_out=rcv_fo)
    moments = dict(w_in=(m_w_in, v_w_in), w_branch_a=(m_w_branch_a, v_w_branch_a), w_branch_b=(m_w_branch_b, v_w_branch_b),
                   w_out=(m_w_out, v_w_out), w_ffn_in=(m_w_ffn_in, v_w_ffn_in), w_ffn_out=(m_w_ffn_out, v_w_ffn_out))
    names = list(big)
    res = {}
    for kname in names:
        mm, vv = moments[kname]
        R = big[kname].shape[0]
        res[kname] = [tr(kname, r) for r in _adamw_reduce(
            received[kname], big[kname], tr(kname, mm[0]), tr(kname, vv[0]), "adamw_" + kname, 256 if R % 256 == 0 else R // 2)]

    tot = _small_reduce(gath, gath_ws)
    g_small = _unpack_small(tot)
    loss = tot[15, 0]

    p_w = _pack_small(b_ada, ln1_g, ln1_b, ln2_g, ln2_b, gmlp_ln_g, gmlp_ln_b, b_spatial, attn_sink, w_spatial)
    p_m = _pack_small(m_b_ada, m_ln1_g, m_ln1_b, m_ln2_g, m_ln2_b, m_gmlp_ln_g, m_gmlp_ln_b, m_b_spatial, m_attn_sink, m_w_spatial)
    p_v = _pack_small(v_b_ada, v_ln1_g, v_ln1_b, v_ln2_g, v_ln2_b, v_gmlp_ln_g, v_gmlp_ln_b, v_b_spatial, v_attn_sink, v_w_spatial)
    s_d, s_m, s_v = [_unpack_small(t) for t in _small_adamw(p_w, tot, p_m, p_v, "adamw_small")]

    dmod_rows = jnp.concatenate([gath[:, 0:6, :].reshape(N_DEV, 6 * D),
                                 jnp.concatenate([tot[6:8].reshape(1, 2 * D), jnp.zeros((1, 4 * D), F32)], axis=1),
                                 jnp.zeros((7, 6 * D), F32)], axis=0)
    dmod_my = lax.dynamic_slice(dmod_rows, (0, me * n_ada), (16, n_ada))
    g_wada, d_wada, m2_wada, v2_wada, pc = _ada_bwd(act, dmod_my, w_ada[0], m_w_ada[0], v_w_ada[0])
    pc_all = _ag_small(pc, "gather_cctx")
    cc8 = lambda a: _pad_rows(a.reshape(1, D), 8)
    g_cc, d_cc, m2_cc, v2_cc = _cctx_finish(pc_all, cc8(c_ctx), cc8(m_c_ctx), cc8(v_c_ctx))

    order = ["c_ctx", "w_ada", "b_ada", "w_in", "attn_sink", "gmlp_ln_g", "gmlp_ln_b", "w_spatial", "b_spatial",
             "w_branch_a", "w_branch_b", "w_out", "ln1_g", "ln1_b", "w_ffn_in", "w_ffn_out", "ln2_g", "ln2_b"]
    grads, deltas, new_m, new_v = {}, {}, {}, {}
    grads["c_ctx"], deltas["c_ctx"], new_m["c_ctx"], new_v["c_ctx"] = g_cc[0], d_cc[0], m2_cc[0], v2_cc[0]
    grads["w_ada"], deltas["w_ada"], new_m["w_ada"], new_v["w_ada"] = g_wada[None], d_wada[None], m2_wada[None], v2_wada[None]
    for kname in names:
        g, d, m2, v2 = res[kname]
        grads[kname], deltas[kname], new_m[kname], new_v[kname] = g[None], d[None], m2[None], v2[None]
    for kname in ("b_ada", "attn_sink", "gmlp_ln_g", "gmlp_ln_b", "w_spatial", "b_spatial", "ln1_g", "ln1_b", "ln2_g", "ln2_b"):
        grads[kname], deltas[kname], new_m[kname], new_v[kname] = g_small[kname], s_d[kname], s_m[kname], s_v[kname]
    return (loss, grad_x[None], *[grads[n] for n in order], *[deltas[n] for n in order],
            *[new_m[n] for n in order], *[new_v[n] for n in order])
```

```python
import functools
import math

import jax
import jax.numpy as jnp
import numpy as np
from jax import lax
from jax.experimental import pallas as pl
from jax.experimental.pallas import tpu as pltpu

F32 = jnp.float32
BF16 = jnp.bfloat16
MESH = pl.DeviceIdType.MESH

N_DEV = 8
D = 1024
HEAD_DIM = 64
N_Q_HEADS = 8
N_KV_HEADS = 2
GQA_GROUP = 4
BLK = 128
Q_W = 512
KV_W = 128
GM_W = 512
N_GROUPS = 8
GROUP_DIM = 64
FFN_H = 2816
IN_W = 3840
O_Q, O_K, O_V, O_U, O_VB, O_GA, O_GB = 0, 512, 640, 768, 1280, 1792, 2816
LN_EPS = 1e-5
NEG_INF = -1e30
ALPHA = 2.0 ** 0.25
ROPE_BASE = 10000.0
ROPE_PAIRS = 16
Q_SCALE = HEAD_DIM ** -0.5
GELU_K0 = math.sqrt(2.0 / math.pi)
GELU_K1 = 0.044715

ADAM_LR = 0.001
ADAM_B1 = 0.9
ADAM_B2 = 0.999
ADAM_EPS = 1e-08
ADAM_WD = 0.01
ADAM_STEP = 10

VMEM_LIMIT = 56 * 1024 * 1024
NT = (((1,), (1,)), ((), ()))
TN = (((0,), (0,)), ((), ()))


def _params(sem=None):
    return pltpu.CompilerParams(dimension_semantics=sem, vmem_limit_bytes=VMEM_LIMIT)


def _row(tm, w):
    return pl.BlockSpec((tm, w), lambda i: (i, 0))


def _full(shape):
    nd = len(shape)
    return pl.BlockSpec(shape, lambda i: (0,) * nd)


def _resident(shape):
    nd = len(shape)
    return pl.BlockSpec(shape, lambda i: (0,) * nd, pipeline_mode=pl.Buffered(1))


def _sds(shape, dt):
    return jax.ShapeDtypeStruct(shape, dt)


def _ln(xf):
    mu = jnp.mean(xf, axis=-1, keepdims=True)
    xc = xf - mu
    var = jnp.mean(xc * xc, axis=-1, keepdims=True)
    rstd = lax.rsqrt(var + LN_EPS)
    return xc * rstd, rstd


def _ln_bwd(dn, n, rstd):
    m1 = jnp.mean(dn, axis=-1, keepdims=True)
    m2 = jnp.mean(dn * n, axis=-1, keepdims=True)
    return rstd * (dn - m1 - n * m2)


def _colsum(t):
    return jnp.sum(t, axis=0, keepdims=True)


def _sigmoid(x):
    return 0.5 * jnp.tanh(0.5 * x) + 0.5


def _gelu(x):
    t = jnp.tanh(GELU_K0 * (x + GELU_K1 * (x * x * x)))
    return x * (0.5 * (1.0 + t)), t


def _gelu_grad(x, t):
    return 0.5 * (1.0 + t) + 0.5 * x * (1.0 - t * t) * (GELU_K0 * (1.0 + 3.0 * GELU_K1 * x * x))


def _swap16(t):
    lane = lax.broadcasted_iota(jnp.int32, t.shape, 1)
    return jnp.where((lane & 16) == 0, pltpu.roll(t, 112, 1), pltpu.roll(t, 16, 1))


def _rope(t, cos, sin):
    return t * cos + _swap16(t) * sin


def _unrope(t, cos, sin):
    return t * cos - _swap16(t) * sin


def _adamw(w, g, m, v):
    m2 = ADAM_B1 * m + (1.0 - ADAM_B1) * g
    v2 = ADAM_B2 * v + (1.0 - ADAM_B2) * (g * g)
    m_hat = m2 / (1.0 - ADAM_B1 ** ADAM_STEP)
    v_hat = v2 / (1.0 - ADAM_B2 ** ADAM_STEP)
    delta = -ADAM_LR * (m_hat / (jnp.sqrt(v_hat) + ADAM_EPS) + ADAM_WD * w)
    return delta, m2, v2


def _rope_tables(L):
    inv = (np.float32(ROPE_BASE) ** (-np.arange(ROPE_PAIRS, dtype=np.float32) / np.float32(ROPE_PAIRS))).astype(np.float32)
    t = np.arange(L, dtype=np.int32)
    rows = (t // 64).astype(np.float32)[:, None] * inv
    cols = (t % 64).astype(np.float32)[:, None] * inv
    cr, sr, cc, sc = np.cos(rows), np.sin(rows), np.cos(cols), np.sin(cols)
    cos = np.concatenate([cr, cr, cc, cc], axis=1)
    sin = np.concatenate([-sr, sr, -sc, sc], axis=1)
    return jnp.asarray(np.tile(cos, (1, 2)), F32), jnp.asarray(np.tile(sin, (1, 2)), F32)


def _me():
    return lax.axis_index("x"), lax.axis_index("y"), lax.axis_index("c")


def _peer(mx, my, mc, k):
    return (mx ^ ((k >> 2) & 1), my ^ ((k >> 1) & 1), mc ^ (k & 1))


def _ag_small(x, name):
    R, C = x.shape

    def body(x_ref, out_ref, send_sems, recv_sems):
        mx, my, mc = _me()
        me = 4 * mx + 2 * my + mc
        out_ref[pl.ds(me, 1)] = x_ref[...][None]
        sends = []
        for k in range(1, N_DEV):
            cp = pltpu.make_async_remote_copy(
                src_ref=x_ref, dst_ref=out_ref.at[me], send_sem=send_sems.at[k - 1], recv_sem=recv_sems.at[k - 1],
                device_id=_peer(mx, my, mc, k), device_id_type=MESH)
            cp.start()
            sends.append(cp)
        for k in range(1, N_DEV):
            pltpu.make_async_remote_copy(
                src_ref=x_ref, dst_ref=out_ref.at[me ^ k], send_sem=send_sems.at[k - 1], recv_sem=recv_sems.at[k - 1],
                device_id=(mx, my, mc), device_id_type=MESH).wait_recv()
        for cp in sends:
            cp.wait_send()

    return pl.pallas_call(
        body, name=name,
        out_shape=_sds((N_DEV, R, C), x.dtype),
        in_specs=[pl.BlockSpec(memory_space=pltpu.VMEM)],
        out_specs=pl.BlockSpec(memory_space=pltpu.VMEM),
        scratch_shapes=[pltpu.SemaphoreType.DMA((N_DEV - 1,)), pltpu.SemaphoreType.DMA((N_DEV - 1,))],
        compiler_params=pltpu.CompilerParams(vmem_limit_bytes=VMEM_LIMIT),
    )(x)


class _Comm:
    def __init__(self, gather=(), scatter=(), spread=()):
        self.kinds = ["gather"] * len(gather) + ["scatter"] * len(scatter) + ["spread"] * len(spread)
        self.args = list(gather) + list(scatter) + list(spread)
        self.n = len(self.args)

    def out_shape(self):
        return [_sds(a.shape if k == "scatter" else (N_DEV,) + a.shape, a.dtype) for k, a in zip(self.kinds, self.args)]

    def specs(self):
        return [pl.BlockSpec(memory_space=pl.ANY)] * self.n

    def scratch(self):
        return [pltpu.SemaphoreType.DMA((7 * self.n,)), pltpu.SemaphoreType.DMA((7 * self.n,)),
                pltpu.SemaphoreType.DMA((self.n,))]

    def _plan(self, x_refs, out_refs, send_sems, recv_sems, local_sems):
        mx, my, mc = _me()
        me = 4 * mx + 2 * my + mc
        here, sibling = (mx, my, mc), (mx, my, 1 - mc)
        chips = [(1 - mx, my), (mx, 1 - my), (1 - mx, 1 - my)]
        local, first, last = [], [], []
        relay = [[], [], []]
        for a, kind in enumerate(self.kinds):
            x, out = x_refs[a], out_refs[a]

            def rc(k, src, dst, to):
                return pltpu.make_async_remote_copy(
                    src_ref=src, dst_ref=dst, send_sem=send_sems.at[7 * a + k], recv_sem=recv_sems.at[7 * a + k],
                    device_id=to, device_id_type=MESH)

            if kind == "gather":
                local.append(pltpu.make_async_copy(x, out.at[me], local_sems.at[a]))
                first.append(rc(0, x, out.at[me], sibling))
                last.append(rc(0, x, out.at[me ^ 1], here))
                for j, (cx, cy) in enumerate(chips):
                    first.append(rc(1 + j, x, out.at[me], (cx, cy, mc)))
                    landed = out.at[4 * cx + 2 * cy + mc]
                    relay[j].append((rc(1 + j, x, landed, here), rc(4 + j, landed, landed, sibling)))
                    last.append(rc(4 + j, x, out.at[4 * cx + 2 * cy + 1 - mc], here))
            else:
                own = x.at[me] if kind == "scatter" else x
                local.append(pltpu.make_async_copy(own, out.at[me], local_sems.at[a]))
                for k in range(1, N_DEV):
                    src = x.at[me ^ k] if kind == "scatter" else x
                    first.append(rc(k - 1, src, out.at[me], _peer(mx, my, mc, k)))
                    last.append(rc(k - 1, own, out.at[me ^ k], here))
        return local, first, relay[0] + relay[1] + relay[2], last

    def start(self, *refs):
        local, first, _, _ = self._plan(*refs)
        for cp in local + first:
            cp.start()

    def finish(self, *refs):
        local, first, relay, last = self._plan(*refs)
        for arrival, onward in relay:
            arrival.wait_recv()
            onward.start()
        for cp in last:
            cp.wait_recv()
        for cp in first:
            cp.wait_send()
        for _, onward in relay:
            onward.wait_send()
        for cp in local:
            cp.wait()


def _call(body, *, name, grid, in_specs, out_specs, out_shape, args, scratch=(), comm=None, aliases=None):
    params = _params(("arbitrary",) * len(grid))

    def at(end):
        conds = [pl.program_id(d) == (n - 1 if end else 0) for d, n in enumerate(grid)]
        return functools.reduce(lambda p, q: p & q, conds)

    if comm is None:
        res = pl.pallas_call(
            body, name=name, grid=grid, in_specs=list(in_specs), out_specs=list(out_specs), out_shape=list(out_shape),
            scratch_shapes=list(scratch), input_output_aliases=aliases or {}, compiler_params=params)(*args)
        return list(res), []
    n_in, n_out, n_scr, cn = len(in_specs), len(out_specs), len(scratch), comm.n

    def hosted(*refs):
        ins, refs = refs[:n_in], refs[n_in:]
        cins, refs = refs[:cn], refs[cn:]
        outs, refs = refs[:n_out], refs[n_out:]
        couts, refs = refs[:cn], refs[cn:]
        scr, sems = refs[:n_scr], refs[n_scr:]

        @pl.when(at(False))
        def _():
            comm.start(cins, couts, *sems)

        body(*ins, *outs, *scr)

        @pl.when(at(True))
        def _():
            comm.finish(cins, couts, *sems)

    res = pl.pallas_call(
        hosted, name=name, grid=grid, in_specs=list(in_specs) + comm.specs(), out_specs=list(out_specs) + comm.specs(),
        out_shape=list(out_shape) + comm.out_shape(), scratch_shapes=list(scratch) + comm.scratch(),
        input_output_aliases=aliases or {}, compiler_params=params)(*args, *comm.args)
    return list(res[:n_out]), list(res[n_out:])


def _exchange_two_level(blk, small, name):
    _, R, C = blk.shape
    rows = small.shape[0]

    def body(blk_ref, small_ref, stage_ref, out_ref, gath_ref, a_scr, b_scr, t_scr, s1, r1, s3, r3, ss, rs, lsem):
        mx, my, mc = _me()
        me = 4 * mx + 2 * my + mc
        mine = 2 * mx + my
        here, sibling = (mx, my, mc), (mx, my, 1 - mc)

        def rc(src, dst, send, recv, to):
            return pltpu.make_async_remote_copy(src_ref=src, dst_ref=dst, send_sem=send, recv_sem=recv,
                                                device_id=to, device_id_type=MESH)

        own_small = pltpu.make_async_copy(small_ref, gath_ref.at[me], lsem.at[0])
        own_small.start()
        spread = [rc(small_ref, gath_ref.at[me], ss.at[k - 1], rs.at[k - 1], _peer(mx, my, mc, k)) for k in range(1, N_DEV)]
        to_sib = [rc(blk_ref.at[2 * p + 1 - mc], stage_ref.at[p], s1.at[p], r1.at[p], sibling) for p in range(4)]
        for cp in spread + to_sib:
            cp.start()
        own = [pltpu.make_async_copy(blk_ref.at[2 * p + mc], a_scr.at[p], lsem.at[1 + p]) for p in range(4)]
        for cp in own:
            cp.start()
        from_sib = []
        for p in range(4):
            rc(blk_ref.at[2 * p + 1 - mc], stage_ref.at[p], s1.at[p], r1.at[p], here).wait_recv()
            cp = pltpu.make_async_copy(stage_ref.at[p], b_scr.at[p], lsem.at[5 + p])
            cp.start()
            from_sib.append(cp)
        for cp in own + from_sib:
            cp.wait()
        t_scr[...] = (a_scr[...].astype(F32) + b_scr[...].astype(F32)).astype(BF16)
        keep = pltpu.make_async_copy(t_scr.at[mine], out_ref.at[mine], lsem.at[9])
        keep.start()
        onward = [rc(t_scr.at[mine ^ k], out_ref.at[mine], s3.at[k - 1], r3.at[k - 1], (mx ^ (k >> 1), my ^ (k & 1), mc))
                  for k in range(1, 4)]
        for cp in onward:
            cp.start()
        for k in range(1, 4):
            rc(t_scr.at[mine], out_ref.at[mine ^ k], s3.at[k - 1], r3.at[k - 1], here).wait_recv()
        for k in range(1, N_DEV):
            rc(small_ref, gath_ref.at[me ^ k], ss.at[k - 1], rs.at[k - 1], here).wait_recv()
        for cp in spread + to_sib + onward:
            cp.wait_send()
        keep.wait()
        own_small.wait()

    any_spec = pl.BlockSpec(memory_space=pl.ANY)
    dma = pltpu.SemaphoreType.DMA
    _, out, gath = pl.pallas_call(
        body, name=name,
        in_specs=[any_spec, any_spec], out_specs=[any_spec] * 3,
        out_shape=[_sds((4, R, C), BF16), _sds((4, R, C), BF16), _sds((N_DEV, rows, D), F32)],
        scratch_shapes=[pltpu.VMEM((4, R, C), BF16)] * 3
                       + [dma((4,)), dma((4,)), dma((3,)), dma((3,)), dma((N_DEV - 1,)), dma((N_DEV - 1,)), dma((10,))],
        compiler_params=pltpu.CompilerParams(vmem_limit_bytes=VMEM_LIMIT),
    )(blk, small)
    return out, gath


def _comm_only(comm, name):
    return _call(lambda: None, name=name, grid=(1,), in_specs=[], out_specs=[], out_shape=[], args=[], comm=comm)[1]


def _exchange_rows(x_ref, out_ref, send_sems, recv_sems):
    mx, my, mc = _me()
    me = 4 * mx + 2 * my + mc
    out_ref[pl.ds(me, 1)] = x_ref[...][None]
    sends = []
    for k in range(1, N_DEV):
        cp = pltpu.make_async_remote_copy(
            src_ref=x_ref, dst_ref=out_ref.at[me], send_sem=send_sems.at[k - 1], recv_sem=recv_sems.at[k - 1],
            device_id=_peer(mx, my, mc, k), device_id_type=MESH)
        cp.start()
        sends.append(cp)
    for k in range(1, N_DEV):
        pltpu.make_async_remote_copy(
            src_ref=x_ref, dst_ref=out_ref.at[me ^ k], send_sem=send_sems.at[k - 1], recv_sem=recv_sems.at[k - 1],
            device_id=(mx, my, mc), device_id_type=MESH).wait_recv()
    for cp in sends:
        cp.wait_send()


def _prologue(c8, cctx8, w_ada, b_my, comm):
    nw = w_ada.shape[1]

    def body(c_ref, cctx_ref, w_ref, b_ref, act_ref, mod_ref, cmine_scr, call_scr, mine_scr, mall_scr, s1, r1, s2, r2):
        cmine_scr[...] = c_ref[...]
        _exchange_rows(cmine_scr, call_scr, s1, r1)
        rows = [call_scr[d][0:1, :] for d in range(N_DEV)] + [cctx_ref[0:1, :], jnp.zeros((7, D), F32)]
        s = jnp.concatenate(rows, axis=0)
        act = s * _sigmoid(s)
        act_ref[...] = act
        mine_scr[...] = jnp.dot(act.astype(BF16), w_ref[...].astype(BF16), preferred_element_type=F32) + b_ref[...]
        _exchange_rows(mine_scr, mall_scr, s2, r2)
        mod_ref[...] = mall_scr[...]

    sems = [pltpu.SemaphoreType.DMA((N_DEV - 1,))] * 4
    (act, mod), got = _call(
        body, name="prologue", grid=(1,),
        in_specs=[_full((8, D)), _full((8, D)), _full((D, nw)), _full((1, nw))],
        out_specs=[_full((16, D)), _full((N_DEV, 16, nw))],
        out_shape=[_sds((16, D), F32), _sds((N_DEV, 16, nw), F32)],
        scratch=[pltpu.VMEM((8, D), F32), pltpu.VMEM((N_DEV, 8, D), F32), pltpu.VMEM((16, nw), F32),
                 pltpu.VMEM((N_DEV, 16, nw), F32)] + sems,
        args=(c8, cctx8, w_ada, b_my), comm=comm)
    return act, mod, got


def _ada_bwd(act, dmod_my, w_ada, m, v, tr=256):
    nw = w_ada.shape[1]

    def body(act_ref, dm_ref, w_ref, m_ref, v_ref, g_ref, d_ref, m2_ref, v2_ref, pc_ref):
        dm = dm_ref[...].astype(BF16)
        g = lax.dot_general(act_ref[...].astype(BF16), dm, TN, preferred_element_type=F32)
        w = w_ref[...]
        delta, m2, v2 = _adamw(w, g, m_ref[...], v_ref[...])
        g_ref[...] = g
        d_ref[...] = delta
        m2_ref[...] = m2
        v2_ref[...] = v2
        pc_ref[...] = lax.dot_general(dm[8:16, :], w.astype(BF16), NT, preferred_element_type=F32)

    wspec = _row(tr, nw)
    return pl.pallas_call(
        body, name="ada_bwd", grid=(D // tr,),
        in_specs=[pl.BlockSpec((16, tr), lambda i: (0, i)), _full((16, nw)), wspec, wspec, wspec],
        out_specs=[wspec, wspec, wspec, wspec, pl.BlockSpec((8, tr), lambda i: (0, i))],
        out_shape=[_sds((D, nw), F32)] * 4 + [_sds((8, D), F32)],
        compiler_params=_params(("arbitrary",)),
    )(act, dmod_my, w_ada, m, v)


def _k_in(x, modv, w_in, cos, sin, tm, comm=None):
    L = x.shape[0]

    def body(x_ref, mod_ref, w_ref, cos_ref, sin_ref, h_ref, q_ref, k_ref, v_ref, u_ref, vb_ref, ga_ref, gb_ref):
        n, _ = _ln(x_ref[...])
        h = (n * (1.0 + mod_ref[1:2, :]) + mod_ref[0:1, :]).astype(BF16)
        h_ref[...] = h
        c, s = cos_ref[...], sin_ref[...]

        def proj(lo, width):
            return lax.dot_general(h, w_ref[lo:lo + width, :], NT, preferred_element_type=F32)

        for i in range(4):
            q_ref[:, i * 128:(i + 1) * 128] = (_rope(proj(O_Q + i * 128, 128), c, s) * Q_SCALE).astype(BF16)
        k_ref[...] = _rope(proj(O_K, KV_W), c, s).astype(BF16)
        v_ref[...] = proj(O_V, KV_W).astype(BF16)
        u_ref[...] = proj(O_U, GM_W).astype(BF16)
        vb_ref[...] = proj(O_VB, GM_W).astype(BF16)
        ga_ref[...] = proj(O_GA, D).astype(BF16)
        gb_ref[...] = proj(O_GB, D).astype(BF16)

    widths = [D, Q_W, KV_W, KV_W, GM_W, GM_W, D, D]
    return _call(
        body, name="fwd_in", grid=(L // tm,),
        in_specs=[_row(tm, D), _full((8, D)), _resident((IN_W, D)), _row(tm, 128), _row(tm, 128)],
        out_specs=[_row(tm, w) for w in widths],
        out_shape=[_sds((L, w), BF16) for w in widths],
        args=(x, modv, w_in, cos, sin), comm=comm)


def _k_ctx(ctx, modc, w_kv):
    C = ctx.shape[0]

    def body(c_ref, mod_ref, w_ref, hc_ref, kc_ref, vc_ref):
        n, _ = _ln(c_ref[...])
        hc = (n * (1.0 + mod_ref[1:2, :]) + mod_ref[0:1, :]).astype(BF16)
        hc_ref[...] = hc
        kv = lax.dot_general(hc, w_ref[...], NT, preferred_element_type=F32)
        kc_ref[...] = kv[:, :KV_W].astype(BF16)
        vc_ref[...] = kv[:, KV_W:].astype(BF16)

    return pl.pallas_call(
        body, name="fwd_ctx", grid=(1,),
        in_specs=[_full((C, D)), _full((8, D)), _full((2 * KV_W, D))],
        out_specs=[_full((C, D)), _full((C, KV_W)), _full((C, KV_W))],
        out_shape=[_sds((C, D), BF16), _sds((C, KV_W), BF16), _sds((C, KV_W), BF16)],
        compiler_params=_params(("arbitrary",)),
    )(ctx, modc, w_kv)


def _attn_bias():
    r = (np.arange(GQA_GROUP * BLK) & (BLK - 1))[:, None]
    j = np.arange(3 * BLK)[None, :]
    band = np.abs(j - BLK - r) <= BLK
    variants = [band & (j >= BLK), band, band & (j < 2 * BLK)]
    return jnp.asarray(np.stack([np.where(v, 0.0, NEG_INF) for v in variants]), F32)


def _bias_spec(nb):
    return pl.BlockSpec((1, GQA_GROUP * BLK, 3 * BLK),
                        lambda n: (jnp.where(n == 0, 0, jnp.where(n >= nb - 1, 2, 1)), 0, 0))


def _masked(s, bias, C):
    return jnp.concatenate([s[:, :C], s[:, C:] + bias], axis=1)


def _sink_col(sink_ref, hk):
    grp = lax.broadcasted_iota(jnp.int32, (GQA_GROUP * BLK, 1), 0) >> 7
    col = jnp.full((GQA_GROUP * BLK, 1), sink_ref[hk * GQA_GROUP], F32)
    for g in range(1, GQA_GROUP):
        col = jnp.where(grp == g, sink_ref[hk * GQA_GROUP + g], col)
    return col


def _kv_specs(nb):
    prev = pl.BlockSpec((BLK, KV_W), lambda n: (jnp.clip(n - 1, 0, nb - 1), 0))
    cur = pl.BlockSpec((BLK, KV_W), lambda n: (jnp.minimum(n, nb - 1), 0))
    nxt = pl.BlockSpec((BLK, KV_W), lambda n: (jnp.minimum(n + 1, nb - 1), 0))
    return [prev, cur, nxt]


def _k_attn(sink, q, k, v, kc, vc, bias, comm=None):
    L = q.shape[0]
    C = kc.shape[0]
    nb = L // BLK

    def body(sink_ref, q_ref, kp_ref, kn_ref, kx_ref, vp_ref, vn_ref, vx_ref, kc_ref, vc_ref, bias_ref, ya_ref, lse_ref):
        band = bias_ref[0]
        kvh = range(N_KV_HEADS)
        sl = [slice(hk * HEAD_DIM, (hk + 1) * HEAD_DIM) for hk in kvh]
        kcat = [jnp.concatenate([kc_ref[:, s_], kp_ref[:, s_], kn_ref[:, s_], kx_ref[:, s_]], axis=0) for s_ in sl]
        vcat = [jnp.concatenate([vc_ref[:, s_], vp_ref[:, s_], vn_ref[:, s_], vx_ref[:, s_]], axis=0) for s_ in sl]
        qg = [jnp.concatenate(
            [q_ref[:, (hk * GQA_GROUP + g) * HEAD_DIM:(hk * GQA_GROUP + g + 1) * HEAD_DIM] for g in range(GQA_GROUP)],
            axis=0) for hk in kvh]
        s = [_masked(lax.dot_general(qg[hk], kcat[hk], NT, preferred_element_type=F32), band, C) for hk in kvh]
        for hk in kvh:
            sink_c = _sink_col(sink_ref, hk)
            m = jnp.maximum(jnp.max(s[hk], axis=1, keepdims=True), sink_c)
            p = jnp.exp(s[hk] - m)
            den = jnp.sum(p, axis=1, keepdims=True) + jnp.exp(sink_c - m)
            o = jnp.dot(p.astype(BF16), vcat[hk], preferred_element_type=F32) * (1.0 / den)
            lse = m + jnp.log(den)
            for g in range(GQA_GROUP):
                h = hk * GQA_GROUP + g
                ya_ref[:, h * HEAD_DIM:(h + 1) * HEAD_DIM] = o[g * BLK:(g + 1) * BLK, :].astype(BF16)
                lse_ref[:, h:h + 1] = lse[g * BLK:(g + 1) * BLK, :]

    kv3 = _kv_specs(nb)
    return _call(
        body, name="fwd_attn", grid=(nb,),
        in_specs=[pl.BlockSpec(memory_space=pltpu.SMEM), _row(BLK, Q_W)] + kv3 + kv3
                 + [_full((C, KV_W)), _full((C, KV_W)), _bias_spec(nb)],
        out_specs=[_row(BLK, Q_W), _row(BLK, N_Q_HEADS)],
        out_shape=[_sds((L, Q_W), BF16), _sds((L, N_Q_HEADS), F32)],
        args=(sink, q, k, k, k, v, v, v, kc, vc, bias), comm=comm)


GMLP_CHUNKS = 4


def _split_pair(t):
    low = lax.broadcasted_iota(jnp.int32, t.shape, 1) < GROUP_DIM
    zero = jnp.zeros_like(t)
    return jnp.where(low, t, zero), jnp.where(low, zero, t)


def _gmlp_spatial(w_ref, t_b, nch):
    rows = []
    for c in range(nch):
        tiles = []
        for pr in range(N_GROUPS // 2):
            lo, hi = _split_pair(t_b[c * BLK:(c + 1) * BLK, pr * 128:(pr + 1) * 128])
            tiles.append(jnp.dot(w_ref[2 * pr], lo, preferred_element_type=F32)
                         + jnp.dot(w_ref[2 * pr + 1], hi, preferred_element_type=F32))
        rows.append(jnp.concatenate(tiles, axis=1))
    return jnp.concatenate(rows, axis=0)


def _gmlp_fwd_vals(u, vb, lnv_ref, ws_ref, bsp_ref, nch):
    uf = u.astype(F32)
    vf = vb.astype(F32)
    gu, tu = _gelu(uf)
    gv, tv = _gelu(vf)
    vhat, rstd = _ln(gv)
    vn = (vhat * lnv_ref[0:1, :] + lnv_ref[1:2, :]).astype(BF16)
    s = _gmlp_spatial(ws_ref, vn, nch) + jnp.concatenate([bsp_ref[...]] * nch, axis=0)
    return uf, vf, gu, tu, tv, vhat, rstd, vn, s


def _k_gmlp(u, vb, lnv, ws, bsp):
    L = u.shape[0]
    nch = min(GMLP_CHUNKS, L // BLK)
    tm = nch * BLK

    def body(u_ref, vb_ref, lnv_ref, ws_ref, bsp_ref, yb_ref):
        _, _, gu, _, _, _, _, _, s = _gmlp_fwd_vals(u_ref[...], vb_ref[...], lnv_ref, ws_ref, bsp_ref, nch)
        yb_ref[...] = (gu * s).astype(BF16)

    return pl.pallas_call(
        body, name="fwd_gmlp", grid=(L // tm,),
        in_specs=[_row(tm, GM_W), _row(tm, GM_W), _full((8, GM_W)), _full((N_GROUPS, BLK, BLK)), _full((BLK, GM_W))],
        out_specs=_row(tm, GM_W),
        out_shape=_sds((L, GM_W), BF16),
        compiler_params=_params(("arbitrary",)),
    )(u, vb, lnv, ws, bsp)


def _k_merge(x, ya, yb, ga, gb, w_a, w_b, w_o, modv, lnv, tm):
    L = x.shape[0]

    def body(x_ref, ya_ref, yb_ref, ga_ref, gb_ref, wa_ref, wb_ref, wo_ref, mod_ref, ln_ref,
             mg_ref, mix_ref, xm_ref, h2_ref):
        a = jnp.dot(ya_ref[...], wa_ref[...], preferred_element_type=F32)
        b = jnp.dot(yb_ref[...], wb_ref[...], preferred_element_type=F32)
        merged = (_sigmoid(ga_ref[...].astype(F32)) * a + _sigmoid(gb_ref[...].astype(F32)) * b).astype(BF16)
        mg_ref[...] = merged
        mix = jnp.dot(merged, wo_ref[...], preferred_element_type=F32)
        mix_ref[...] = mix.astype(BF16)
        r1 = ALPHA * x_ref[...] + mod_ref[2:3, :] * mix
        r1hat, _ = _ln(r1)
        xm = r1hat * ln_ref[0:1, :] + ln_ref[1:2, :]
        xm_ref[...] = xm
        n2, _ = _ln(xm)
        h2_ref[...] = (n2 * (1.0 + mod_ref[4:5, :]) + mod_ref[3:4, :]).astype(BF16)

    return pl.pallas_call(
        body, name="fwd_merge", grid=(L // tm,),
        in_specs=[_row(tm, D), _row(tm, Q_W), _row(tm, GM_W), _row(tm, D), _row(tm, D),
                  _resident((Q_W, D)), _resident((GM_W, D)), _resident((D, D)), _full((8, D)), _full((8, D))],
        out_specs=[_row(tm, D)] * 4,
        out_shape=[_sds((L, D), BF16), _sds((L, D), BF16), _sds((L, D), F32), _sds((L, D), BF16)],
        compiler_params=_params(("arbitrary",)),
    )(x, ya, yb, ga, gb, w_a, w_b, w_o, modv, lnv)


FFN_CH = 1408


def _k_ffn(h2, xm, tgt, w_fi, w_fo, modv, lnv, tm):
    L = h2.shape[0]

    def body(h2_ref, xm_ref, t_ref, wi_ref, wo_ref, mod_ref, ln_ref, gate_ref, up_ref, a_ref, dr2_ref, df_ref, acc_ref):
        @pl.when(pl.program_id(0) == 0)
        def _():
            acc_ref[...] = jnp.zeros_like(acc_ref)

        h2v = h2_ref[...]
        f = jnp.zeros((tm, D), F32)
        for j in range(FFN_H // FFN_CH):
            lo = j * FFN_CH
            gate = lax.dot_general(h2v, wi_ref[lo:lo + FFN_CH, :], NT, preferred_element_type=F32)
            up = lax.dot_general(h2v, wi_ref[FFN_H + lo:FFN_H + lo + FFN_CH, :], NT, preferred_element_type=F32)
            act = (gate * _sigmoid(gate) * up).astype(BF16)
            gate_ref[:, lo:lo + FFN_CH] = gate.astype(BF16)
            up_ref[:, lo:lo + FFN_CH] = up.astype(BF16)
            a_ref[:, lo:lo + FFN_CH] = act
            f = f + jnp.dot(act, wo_ref[lo:lo + FFN_CH, :], preferred_element_type=F32)
        gate2 = mod_ref[5:6, :]
        r2 = ALPHA * xm_ref[...] + gate2 * f
        r2hat, rstd = _ln(r2)
        y = r2hat * ln_ref[2:3, :] + ln_ref[3:4, :]
        err = y - t_ref[...]
        dy = err * (1.0 / D)
        dr2 = _ln_bwd(dy * ln_ref[2:3, :], r2hat, rstd)
        dr2_ref[...] = dr2
        df_ref[...] = (gate2 * dr2).astype(BF16)
        acc_ref[0:1, :] += _colsum(dy * r2hat)
        acc_ref[1:2, :] += _colsum(dy)
        acc_ref[2:3, :] += _colsum(dr2 * f)
        acc_ref[3:4, :] += _colsum(err * err) * (0.5 / D)

    return pl.pallas_call(
        body, name="fwd_ffn", grid=(L // tm,),
        in_specs=[_row(tm, D), _row(tm, D), _row(tm, D), _resident((2 * FFN_H, D)), _resident((FFN_H, D)),
                  _full((8, D)), _full((8, D))],
        out_specs=[_row(tm, FFN_H)] * 3 + [_row(tm, D), _row(tm, D), _full((8, D))],
        out_shape=[_sds((L, FFN_H), BF16)] * 3 + [_sds((L, D), F32), _sds((L, D), BF16), _sds((8, D), F32)],
        compiler_params=_params(("arbitrary",)),
    )(h2, xm, tgt, w_fi, w_fo, modv, lnv)


def _k_ffn_bwd(df, gate, up, xm, dr2, x, mix, w_fi, w_fo, modv, lnv, tm):
    L = df.shape[0]

    def body(df_ref, gate_ref, up_ref, xm_ref, dr2_ref, x_ref, mix_ref, wi_ref, wo_ref, mod_ref, ln_ref,
             dF_ref, dmix_ref, dxp_ref, acc_ref):
        @pl.when(pl.program_id(0) == 0)
        def _():
            acc_ref[...] = jnp.zeros_like(acc_ref)

        dfv = df_ref[...]
        chunks = [j * FFN_CH for j in range(FFN_H // FFN_CH)]
        das = [lax.dot_general(dfv, wo_ref[lo:lo + FFN_CH, :], NT, preferred_element_type=F32) for lo in chunks]
        n2, rstd2 = _ln(xm_ref[...])
        mixf = mix_ref[...].astype(F32)
        gate1 = mod_ref[2:3, :]
        r1hat, rstd1 = _ln(ALPHA * x_ref[...] + gate1 * mixf)
        dh2 = jnp.zeros((tm, D), F32)
        for lo, da in zip(chunks, das):
            gate = gate_ref[:, lo:lo + FFN_CH].astype(F32)
            upv = up_ref[:, lo:lo + FFN_CH].astype(F32)
            sg = _sigmoid(gate)
            d_gate = (da * upv * (sg * (1.0 + gate * (1.0 - sg)))).astype(BF16)
            d_up = (da * (gate * sg)).astype(BF16)
            dF_ref[:, lo:lo + FFN_CH] = d_gate
            dF_ref[:, FFN_H + lo:FFN_H + lo + FFN_CH] = d_up
            dh2 = dh2 + jnp.dot(d_gate, wi_ref[lo:lo + FFN_CH, :], preferred_element_type=F32)
            dh2 = dh2 + jnp.dot(d_up, wi_ref[FFN_H + lo:FFN_H + lo + FFN_CH, :], preferred_element_type=F32)
        acc_ref[0:1, :] += _colsum(dh2)
        acc_ref[1:2, :] += _colsum(dh2 * n2)
        dxm = ALPHA * dr2_ref[...] + _ln_bwd(dh2 * (1.0 + mod_ref[4:5, :]), n2, rstd2)
        acc_ref[2:3, :] += _colsum(dxm * r1hat)
        acc_ref[3:4, :] += _colsum(dxm)
        dr1 = _ln_bwd(dxm * ln_ref[0:1, :], r1hat, rstd1)
        dmix_ref[...] = (gate1 * dr1).astype(BF16)
        dxp_ref[...] = ALPHA * dr1
        acc_ref[4:5, :] += _colsum(dr1 * mixf)

    return pl.pallas_call(
        body, name="bwd_ffn", grid=(L // tm,),
        in_specs=[_row(tm, D), _row(tm, FFN_H), _row(tm, FFN_H), _row(tm, D), _row(tm, D), _row(tm, D), _row(tm, D),
                  _resident((2 * FFN_H, D)), _resident((FFN_H, D)), _full((8, D)), _full((8, D))],
        out_specs=[_row(tm, 2 * FFN_H), _row(tm, D), _row(tm, D), _full((8, D))],
        out_shape=[_sds((L, 2 * FFN_H), BF16), _sds((L, D), BF16), _sds((L, D), F32), _sds((8, D), F32)],
        compiler_params=_params(("arbitrary",)),
    )(df, gate, up, xm, dr2, x, mix, w_fi, w_fo, modv, lnv)


def _k_merge_bwd(dmix, ya, yb, ga, gb, w_a, w_b, w_o, tm, comm=None):
    L = dmix.shape[0]

    def body(dmix_ref, ya_ref, yb_ref, ga_ref, gb_ref, wa_ref, wb_ref, wo_ref,
             dA_ref, dB_ref, dga_ref, dgb_ref, dya_ref, dyb_ref):
        dmg = lax.dot_general(dmix_ref[...], wo_ref[...], NT, preferred_element_type=F32)
        a = jnp.dot(ya_ref[...], wa_ref[...], preferred_element_type=F32)
        sa = _sigmoid(ga_ref[...].astype(F32))
        dA = (dmg * sa).astype(BF16)
        dA_ref[...] = dA
        dga_ref[...] = (dmg * a * (sa * (1.0 - sa))).astype(BF16)
        dya_ref[...] = lax.dot_general(dA, wa_ref[...], NT, preferred_element_type=F32).astype(BF16)
        b = jnp.dot(yb_ref[...], wb_ref[...], preferred_element_type=F32)
        sb = _sigmoid(gb_ref[...].astype(F32))
        dB = (dmg * sb).astype(BF16)
        dB_ref[...] = dB
        dgb_ref[...] = (dmg * b * (sb * (1.0 - sb))).astype(BF16)
        dyb_ref[...] = lax.dot_general(dB, wb_ref[...], NT, preferred_element_type=F32).astype(BF16)

    return _call(
        body, name="bwd_merge", grid=(L // tm,),
        in_specs=[_row(tm, D), _row(tm, Q_W), _row(tm, GM_W), _row(tm, D), _row(tm, D),
                  _resident((Q_W, D)), _resident((GM_W, D)), _resident((D, D))],
        out_specs=[_row(tm, D)] * 4 + [_row(tm, Q_W), _row(tm, GM_W)],
        out_shape=[_sds((L, D), BF16)] * 4 + [_sds((L, Q_W), BF16), _sds((L, GM_W), BF16)],
        args=(dmix, ya, yb, ga, gb, w_a, w_b, w_o), comm=comm)


def _k_gmlp_bwd(u, vb, dyb, lnv, ws, wst, bsp):
    L = u.shape[0]
    nch = min(GMLP_CHUNKS, L // BLK)
    tm = nch * BLK

    def body(u_ref, vb_ref, dyb_ref, lnv_ref, ws_ref, wst_ref, bsp_ref, du_ref, dvb_ref, gws_ref, gbst_ref, gln_ref):
        @pl.when(pl.program_id(0) == 0)
        def _():
            gws_ref[...] = jnp.zeros_like(gws_ref)
            gbst_ref[...] = jnp.zeros_like(gbst_ref)
            gln_ref[...] = jnp.zeros_like(gln_ref)

        uf, vf, gu, tu, tv, vhat, rstd, vn, s = _gmlp_fwd_vals(u_ref[...], vb_ref[...], lnv_ref, ws_ref, bsp_ref, nch)
        dyb_f = dyb_ref[...].astype(F32)
        du_ref[...] = (dyb_f * s * _gelu_grad(uf, tu)).astype(BF16)
        ds = dyb_f * gu
        ds_b = ds.astype(BF16)
        for pr in range(N_GROUPS // 2):
            lanes = slice(pr * 128, (pr + 1) * 128)
            gw_lo = gw_hi = ds_sum = None
            for c in range(nch):
                rows = slice(c * BLK, (c + 1) * BLK)
                lo, hi = _split_pair(ds_b[rows, lanes])
                t_lo = lax.dot_general(lo, vn[rows, lanes], NT, preferred_element_type=F32)
                t_hi = lax.dot_general(hi, vn[rows, lanes], NT, preferred_element_type=F32)
                gw_lo = t_lo if c == 0 else gw_lo + t_lo
                gw_hi = t_hi if c == 0 else gw_hi + t_hi
                ds_sum = ds[rows, lanes] if c == 0 else ds_sum + ds[rows, lanes]
            gws_ref[2 * pr] += gw_lo
            gws_ref[2 * pr + 1] += gw_hi
            b_lo, b_hi = _split_pair(ds_sum)
            gbst_ref[:, 2 * pr:2 * pr + 1] += jnp.sum(b_lo, axis=1, keepdims=True)
            gbst_ref[:, 2 * pr + 1:2 * pr + 2] += jnp.sum(b_hi, axis=1, keepdims=True)
        dvn = _gmlp_spatial(wst_ref, ds_b, nch)
        gln_ref[0:1, :] += _colsum(dvn * vhat)
        gln_ref[1:2, :] += _colsum(dvn)
        dgv = _ln_bwd(dvn * lnv_ref[0:1, :], vhat, rstd)
        dvb_ref[...] = (dgv * _gelu_grad(vf, tv)).astype(BF16)

    return pl.pallas_call(
        body, name="bwd_gmlp", grid=(L // tm,),
        in_specs=[_row(tm, GM_W)] * 3 + [_full((8, GM_W)), _full((N_GROUPS, BLK, BLK)), _full((N_GROUPS, BLK, BLK)),
                                         _full((BLK, GM_W))],
        out_specs=[_row(tm, GM_W), _row(tm, GM_W), _full((N_GROUPS, BLK, BLK)), _full((BLK, N_GROUPS)), _full((8, GM_W))],
        out_shape=[_sds((L, GM_W), BF16), _sds((L, GM_W), BF16), _sds((N_GROUPS, BLK, BLK), F32),
                   _sds((BLK, N_GROUPS), F32), _sds((8, GM_W), F32)],
        compiler_params=_params(("arbitrary",)),
    )(u, vb, dyb, lnv, ws, wst, bsp)


def _k_attn_bwd(sink, q, k, v, kc, vc, dya, lse, cos, sin, bias, comm=None):
    L = q.shape[0]
    C = kc.shape[0]
    nb = L // BLK
    NK = C + 3 * BLK

    def body(sink_ref, q_ref, kp_ref, kn_ref, kx_ref, vp_ref, vn_ref, vx_ref, kc_ref, vc_ref, do_ref, lse_ref,
             cq_ref, sq_ref, ck_ref, sk_ref, bias_ref,
             dq_ref, dk_ref, dv_ref, dkc_ref, dvc_ref, dsink_ref,
             dq_scr, ck_scr, cv_scr, kp_acc, kc_acc, vp_acc, vc_acc):
        n = pl.program_id(0)

        @pl.when(n == 0)
        def _():
            for r in (kp_acc, kc_acc, vp_acc, vc_acc, dkc_ref, dvc_ref, dsink_ref):
                r[...] = jnp.zeros_like(r)

        @pl.when(n < nb)
        def _():
            band = bias_ref[0]
            sl = [slice(hk * HEAD_DIM, (hk + 1) * HEAD_DIM) for hk in range(N_KV_HEADS)]
            heads = [[hk * GQA_GROUP + g for g in range(GQA_GROUP)] for hk in range(N_KV_HEADS)]

            def scores(hk):
                kcat = jnp.concatenate([kc_ref[:, sl[hk]], kp_ref[:, sl[hk]], kn_ref[:, sl[hk]], kx_ref[:, sl[hk]]], axis=0)
                qg = jnp.concatenate([q_ref[:, h * HEAD_DIM:(h + 1) * HEAD_DIM] for h in heads[hk]], axis=0)
                s = _masked(lax.dot_general(qg, kcat, NT, preferred_element_type=F32), band, C)
                vcat = jnp.concatenate([vc_ref[:, sl[hk]], vp_ref[:, sl[hk]], vn_ref[:, sl[hk]], vx_ref[:, sl[hk]]], axis=0)
                dog = jnp.concatenate([do_ref[:, h * HEAD_DIM:(h + 1) * HEAD_DIM] for h in heads[hk]], axis=0)
                dp = lax.dot_general(dog, vcat, NT, preferred_element_type=F32)
                return kcat, qg, dog, s, dp

            def softmax_bwd(hk, s, dp):
                lse_c = jnp.concatenate([lse_ref[:, h:h + 1] for h in heads[hk]], axis=0)
                p = jnp.exp(s - lse_c)
                delta = jnp.sum(p * dp, axis=1, keepdims=True)
                ds = (p * (dp - delta)).astype(BF16)
                p_sink = jnp.exp(_sink_col(sink_ref, hk) - lse_c) * delta
                return p.astype(BF16), ds, p_sink

            def put_dq(hk, dqs, p_sink):
                for g, h in enumerate(heads[hk]):
                    dq_scr[:, h * HEAD_DIM:(h + 1) * HEAD_DIM] = dqs[g * BLK:(g + 1) * BLK, :]
                    tot = jnp.sum(p_sink[g * BLK:(g + 1) * BLK, :], axis=0, keepdims=True)
                    dsink_ref[h:h + 1, :] -= jnp.broadcast_to(tot, (1, 128))

            kcat0, qg0, dog0, s0, dp0 = scores(0)
            kcat1, qg1, dog1, s1, dp1 = scores(1)
            pb0, ds0, psink0 = softmax_bwd(0, s0, dp0)
            dqs0 = jnp.dot(ds0, kcat0, preferred_element_type=F32)
            pb1, ds1, psink1 = softmax_bwd(1, s1, dp1)
            ck_scr[:, sl[0]] = lax.dot_general(ds0, qg0, TN, preferred_element_type=F32)
            cv_scr[:, sl[0]] = lax.dot_general(pb0, dog0, TN, preferred_element_type=F32)
            put_dq(0, dqs0, psink0)
            dqs1 = jnp.dot(ds1, kcat1, preferred_element_type=F32)
            ck_scr[:, sl[1]] = lax.dot_general(ds1, qg1, TN, preferred_element_type=F32)
            put_dq(1, dqs1, psink1)
            cq, sq = cq_ref[...], sq_ref[...]
            for i in range(4):
                dq_ref[:, i * 128:(i + 1) * 128] = _unrope(dq_scr[:, i * 128:(i + 1) * 128] * Q_SCALE, cq, sq).astype(BF16)
            cv_scr[:, sl[1]] = lax.dot_general(pb1, dog1, TN, preferred_element_type=F32)
            dkc_ref[...] += ck_scr[0:C, :]
            dvc_ref[...] += cv_scr[0:C, :]

        @pl.when(n >= nb)
        def _():
            ck_scr[...] = jnp.zeros_like(ck_scr)
            cv_scr[...] = jnp.zeros_like(cv_scr)

        dk_ref[...] = _unrope(kp_acc[...] + ck_scr[C:C + BLK, :], ck_ref[...], sk_ref[...]).astype(BF16)
        dv_ref[...] = (vp_acc[...] + cv_scr[C:C + BLK, :]).astype(BF16)
        kp_acc[...] = kc_acc[...] + ck_scr[C + BLK:C + 2 * BLK, :]
        vp_acc[...] = vc_acc[...] + cv_scr[C + BLK:C + 2 * BLK, :]
        kc_acc[...] = ck_scr[C + 2 * BLK:C + 3 * BLK, :]
        vc_acc[...] = cv_scr[C + 2 * BLK:C + 3 * BLK, :]

    kv3 = _kv_specs(nb)
    cur = lambda w: pl.BlockSpec((BLK, w), lambda n: (jnp.minimum(n, nb - 1), 0))
    late = lambda w: pl.BlockSpec((BLK, w), lambda n: (jnp.maximum(n - 1, 0), 0))
    return _call(
        body, name="bwd_attn", grid=(nb + 1,),
        in_specs=[pl.BlockSpec(memory_space=pltpu.SMEM), cur(Q_W)] + kv3 + kv3
                 + [_full((C, KV_W)), _full((C, KV_W)), cur(Q_W), cur(N_Q_HEADS), cur(128), cur(128), late(128), late(128),
                    _bias_spec(nb)],
        out_specs=[cur(Q_W), late(KV_W), late(KV_W), _full((C, KV_W)), _full((C, KV_W)), _full((8, 128))],
        out_shape=[_sds((L, Q_W), BF16), _sds((L, KV_W), BF16), _sds((L, KV_W), BF16),
                   _sds((C, KV_W), F32), _sds((C, KV_W), F32), _sds((8, 128), F32)],
        scratch=[pltpu.VMEM((BLK, Q_W), F32), pltpu.VMEM((NK, KV_W), F32), pltpu.VMEM((NK, KV_W), F32)]
                + [pltpu.VMEM((BLK, KV_W), F32)] * 4,
        args=(sink, q, k, k, k, v, v, v, kc, vc, dya, lse, cos, sin, cos, sin, bias), comm=comm)


def _k_ctx_bwd(ctx, modc, hc, dkc, dvc, w_kv):
    C = ctx.shape[0]

    def body(c_ref, mod_ref, hc_ref, dkc_ref, dvc_ref, w_ref, gw_ref, dmod_ref):
        dkv = jnp.concatenate([dkc_ref[...], dvc_ref[...]], axis=1).astype(BF16)
        gw_ref[...] = lax.dot_general(dkv, hc_ref[...], TN, preferred_element_type=F32)
        dhc = jnp.dot(dkv, w_ref[...], preferred_element_type=F32)
        n, _ = _ln(c_ref[...])
        dmod_ref[...] = jnp.zeros_like(dmod_ref)
        dmod_ref[0:1, :] = _colsum(dhc)
        dmod_ref[1:2, :] = _colsum(dhc * n)

    return pl.pallas_call(
        body, name="bwd_ctx", grid=(1,),
        in_specs=[_full((C, D)), _full((8, D)), _full((C, D)), _full((C, KV_W)), _full((C, KV_W)), _full((2 * KV_W, D))],
        out_specs=[_full((2 * KV_W, D)), _full((8, D))],
        out_shape=[_sds((2 * KV_W, D), F32), _sds((8, D), F32)],
        compiler_params=_params(("arbitrary",)),
    )(ctx, modc, hc, dkc, dvc, w_kv)


def _k_in_bwd(dq, dk, dv, du, dvb, dga, dgb, x, dxp, w_in, modv, tm, comm=None):
    L = x.shape[0]
    parts = [(O_Q, Q_W), (O_K, KV_W), (O_V, KV_W), (O_U, GM_W), (O_VB, GM_W), (O_GA, D), (O_GB, D)]

    def body(dq_ref, dk_ref, dv_ref, du_ref, dvb_ref, dga_ref, dgb_ref, x_ref, dxp_ref, w_ref, mod_ref,
             dP_ref, gx_ref, acc_ref):
        @pl.when(pl.program_id(0) == 0)
        def _():
            acc_ref[...] = jnp.zeros_like(acc_ref)

        for (lo, width), r in zip(parts, (dq_ref, dk_ref, dv_ref, du_ref, dvb_ref, dga_ref, dgb_ref)):
            dP_ref[:, lo:lo + width] = r[...]
        n1, rstd1 = _ln(x_ref[...])
        dh = jnp.dot(dP_ref[...], w_ref[...], preferred_element_type=F32)
        acc_ref[0:1, :] += _colsum(dh)
        acc_ref[1:2, :] += _colsum(dh * n1)
        gx_ref[...] = dxp_ref[...] + _ln_bwd(dh * (1.0 + mod_ref[1:2, :]), n1, rstd1)

    return _call(
        body, name="bwd_in", grid=(L // tm,),
        in_specs=[_row(tm, w) for _, w in parts] + [_row(tm, D), _row(tm, D), _resident((IN_W, D)), _full((8, D))],
        out_specs=[_row(tm, IN_W), _row(tm, D), _full((8, D))],
        out_shape=[_sds((L, IN_W), BF16), _sds((L, D), F32), _sds((8, D), F32)],
        args=(dq, dk, dv, du, dvb, dga, dgb, x, dxp, w_in, modv), comm=comm)


def _wgrad(a, b, name, tk, tt, comm=None, extra=None):
    T, K = a.shape
    N = b.shape[1]
    nt = T // tt

    def body(*refs):
        a_ref, b_ref = refs[:2]
        o_ref, acc_ref = refs[-2:]
        j, t = pl.program_id(0), pl.program_id(1)

        @pl.when(t == 0)
        def _():
            acc_ref[...] = jnp.zeros_like(acc_ref)

        acc_ref[...] += lax.dot_general(a_ref[...], b_ref[...], TN, preferred_element_type=F32)

        if extra is not None:
            lo, rows = extra[0] % tk, extra[1].shape[0]

            @pl.when((t == nt - 1) & (j == extra[0] // tk))
            def _():
                acc_ref[lo:lo + rows, :] += refs[2][...]

        @pl.when(t == nt - 1)
        def _():
            o_ref[...] = acc_ref[...].astype(BF16)

    extra_specs = [] if extra is None else [pl.BlockSpec(extra[1].shape, lambda j, t: (0, 0))]
    (out,), got = _call(
        body, name=name, grid=(K // tk, nt),
        in_specs=[pl.BlockSpec((tt, tk), lambda j, t: (t, j)), pl.BlockSpec((tt, N), lambda j, t: (t, 0))] + extra_specs,
        out_specs=[pl.BlockSpec((tk, N), lambda j, t: (j, 0))],
        out_shape=[_sds((K, N), BF16)],
        scratch=[pltpu.VMEM((tk, N), F32)],
        args=(a, b) + (() if extra is None else (extra[1],)), comm=comm)
    return (out, got) if comm is not None else out


def _adamw_reduce(parts, w, m, v, name, tr):
    R, C = w.shape
    n_parts = parts.shape[0]

    def body(p_ref, w_ref, m_ref, v_ref, g_ref, d_ref, m2_ref, v2_ref):
        g = p_ref[0].astype(F32)
        for i in range(1, n_parts):
            g = g + p_ref[i].astype(F32)
        delta, m2, v2 = _adamw(w_ref[...], g, m_ref[...], v_ref[...])
        g_ref[...] = g
        d_ref[...] = delta
        m2_ref[...] = m2
        v2_ref[...] = v2

    spec = _row(tr, C)
    return pl.pallas_call(
        body, name=name, grid=(R // tr,),
        in_specs=[pl.BlockSpec((n_parts, tr, C), lambda i: (0, i, 0)), spec, spec, spec],
        out_specs=[spec] * 4,
        out_shape=[_sds((R, C), F32)] * 4,
        compiler_params=_params(("arbitrary",)),
    )(parts, w, m, v)


SMALL_ORDER = ("b_ada", "ln1_g", "ln1_b", "ln2_g", "ln2_b", "gmlp_ln_g", "gmlp_ln_b", "b_spatial", "attn_sink")


def _small_step(gath, params):
    flat = [a for name in SMALL_ORDER for a in params[name]]

    def grad_of(tot, name):
        if name == "b_ada":
            return jnp.concatenate([tot[r:r + 1, :] for r in range(6)], axis=1)
        if name in ("ln1_g", "ln1_b", "ln2_g", "ln2_b"):
            r = 8 + ("ln1_g", "ln1_b", "ln2_g", "ln2_b").index(name)
            return tot[r:r + 1, :]
        if name == "gmlp_ln_g":
            return tot[12:13, :GM_W]
        if name == "gmlp_ln_b":
            return tot[12:13, GM_W:]
        if name == "b_spatial":
            return jnp.concatenate([tot[13:14, g * BLK:(g + 1) * BLK] for g in range(N_GROUPS)], axis=0)[None]
        return tot[14:15, :N_Q_HEADS]

    def body(*refs):
        g_ref, in_refs = refs[0], refs[1:1 + len(flat)]
        tot_ref, out_refs = refs[1 + len(flat)], refs[2 + len(flat):]
        tot = g_ref[0]
        for i in range(1, N_DEV):
            tot = tot + g_ref[i]
        tot_ref[...] = tot
        tot_ref[0:2, :] = tot[0:2, :] + tot[6:8, :]
        tot_ref[15:16, :] = jnp.broadcast_to(jnp.sum(tot[15:16, :], axis=1, keepdims=True), (1, D))
        tot = tot_ref[...]
        for k, name in enumerate(SMALL_ORDER):
            w_ref, m_ref, v_ref = in_refs[3 * k:3 * k + 3]
            g = grad_of(tot, name)
            delta, m2, v2 = _adamw(w_ref[...], g, m_ref[...], v_ref[...])
            for r, val in zip(out_refs[4 * k:4 * k + 4], (g, delta, m2, v2)):
                r[...] = val

    res = pl.pallas_call(
        body, name="small_step", grid=(1,),
        in_specs=[_full((N_DEV, 16, D))] + [_full(a.shape) for a in flat],
        out_specs=[_full((16, D))] + [_full(params[name][0].shape) for name in SMALL_ORDER for _ in range(4)],
        out_shape=[_sds((16, D), F32)] + [_sds(params[name][0].shape, F32) for name in SMALL_ORDER for _ in range(4)],
        compiler_params=_params(("arbitrary",)),
    )(gath, *flat)
    return res[0], {name: res[1 + 4 * k:5 + 4 * k] for k, name in enumerate(SMALL_ORDER)}


def _cctx_finish(gath, c_ctx, m, v):
    def body(g_ref, c_ref, m_ref, v_ref, gr_ref, d_ref, m2_ref, v2_ref):
        ds = g_ref[0]
        for i in range(1, N_DEV):
            ds = ds + g_ref[i]
        c = c_ref[...]
        sg = _sigmoid(c)
        g = ds * (sg * (1.0 + c * (1.0 - sg)))
        delta, m2, v2 = _adamw(c, g, m_ref[...], v_ref[...])
        gr_ref[...] = g
        d_ref[...] = delta
        m2_ref[...] = m2
        v2_ref[...] = v2

    return pl.pallas_call(
        body, name="cctx_finish", grid=(1,),
        in_specs=[_full((N_DEV, 8, D))] + [_full((8, D))] * 3, out_specs=[_full((8, D))] * 4,
        out_shape=[_sds((8, D), F32)] * 4,
        compiler_params=_params(("arbitrary",)),
    )(gath, c_ctx, m, v)


def _pad_rows(a, rows):
    return jnp.concatenate([a, jnp.zeros((rows - a.shape[0], a.shape[1]), a.dtype)], axis=0)


def kernel(x, c, ctx, c_ctx, w_ada, b_ada, w_in, attn_sink, gmlp_ln_g, gmlp_ln_b, w_spatial, b_spatial, w_branch_a, w_branch_b, w_out, ln1_g, ln1_b, w_ffn_in, w_ffn_out, ln2_g, ln2_b, loss_target, m_c_ctx, m_w_ada, m_b_ada, m_w_in, m_attn_sink, m_gmlp_ln_g, m_gmlp_ln_b, m_w_spatial, m_b_spatial, m_w_branch_a, m_w_branch_b, m_w_out, m_ln1_g, m_ln1_b, m_w_ffn_in, m_w_ffn_out, m_ln2_g, m_ln2_b, v_c_ctx, v_w_ada, v_b_ada, v_w_in, v_attn_sink, v_gmlp_ln_g, v_gmlp_ln_b, v_w_spatial, v_b_spatial, v_w_branch_a, v_w_branch_b, v_w_out, v_ln1_g, v_ln1_b, v_w_ffn_in, v_w_ffn_out, v_ln2_g, v_ln2_b):
    L = x.shape[1]
    me = 4 * lax.axis_index("x") + 2 * lax.axis_index("y") + lax.axis_index("c")
    x2, tgt, ctx2 = x[0], loss_target[0], ctx[0]
    tm_in = min(512, L)
    tm = min(256, L)
    tt = min(1024, L)

    transposed = ("w_in", "w_ffn_in")
    tr = lambda kname, a: a.T if kname in transposed else a
    big = dict(w_in=w_in[0].T, w_branch_a=w_branch_a[0], w_branch_b=w_branch_b[0], w_out=w_out[0],
               w_ffn_in=w_ffn_in[0].T, w_ffn_out=w_ffn_out[0])
    col_sharded = ("w_branch_a", "w_branch_b")
    shard_bf = {k: a.astype(BF16) for k, a in big.items()}

    def assemble(kname, g):
        if kname in col_sharded:
            return g.transpose(1, 0, 2).reshape(g.shape[1], N_DEV * g.shape[2])
        return g.reshape(N_DEV * g.shape[1], g.shape[2])

    def to_blocks(kname, g):
        if kname in col_sharded:
            return g.reshape(g.shape[0], N_DEV, g.shape[1] // N_DEV).transpose(1, 0, 2)
        return g.reshape(N_DEV, g.shape[0] // N_DEV, g.shape[1])

    full = {}
    n_ada = w_ada.shape[2]
    b_my = lax.dynamic_slice(b_ada, (0, me * n_ada), (1, n_ada))
    act, mod_all, got = _prologue(_pad_rows(c, 8), _pad_rows(c_ctx[None, :], 8), w_ada[0], b_my,
                                  _Comm(gather=[shard_bf["w_in"]]))
    full["w_in"] = assemble("w_in", got[0])
    mod_all = mod_all.transpose(1, 0, 2).reshape(16, 6 * D)
    modv = _pad_rows(lax.dynamic_slice(mod_all, (me, 0), (1, 6 * D)).reshape(6, D), 8)
    modc = _pad_rows(mod_all[8].reshape(6, D), 8)

    lnv = _pad_rows(jnp.concatenate([ln1_g, ln1_b, ln2_g, ln2_b], axis=0), 8)
    gm_lnv = _pad_rows(jnp.concatenate([gmlp_ln_g, gmlp_ln_b], axis=0), 8)
    ws_b = w_spatial[0].astype(BF16)
    wst_b = ws_b.transpose(0, 2, 1)
    bsp = jnp.repeat(b_spatial[0].T, GROUP_DIM, axis=1)
    sink = attn_sink[0]
    cos, sin = _rope_tables(L)
    bias = _attn_bias()
    w_kv = full["w_in"][O_K:O_K + 2 * KV_W, :]

    (h, q, k, v, u, vb, ga, gb), got = _k_in(
        x2, modv, full["w_in"], cos, sin, tm_in,
        comm=_Comm(gather=[shard_bf[kname] for kname in ("w_branch_a", "w_branch_b", "w_out", "w_ffn_out")]))
    for kname, g in zip(("w_branch_a", "w_branch_b", "w_out", "w_ffn_out"), got):
        full[kname] = assemble(kname, g)
    hc, kc, vc = _k_ctx(ctx2, modc, w_kv)
    (ya, lse), got = _k_attn(sink, q, k, v, kc, vc, bias, comm=_Comm(gather=[shard_bf["w_ffn_in"]]))
    full["w_ffn_in"] = assemble("w_ffn_in", got[0])
    yb = _k_gmlp(u, vb, gm_lnv, ws_b, bsp)
    merged, mix, xm, h2 = _k_merge(x2, ya, yb, ga, gb, full["w_branch_a"], full["w_branch_b"], full["w_out"], modv, lnv, tm_in)
    gate, up, act_f, dr2, df, acc_f = _k_ffn(h2, xm, tgt, full["w_ffn_in"], full["w_ffn_out"], modv, lnv, tm_in)

    dF, dmix, dxp, acc_b = _k_ffn_bwd(df, gate, up, xm, dr2, x2, mix, full["w_ffn_in"], full["w_ffn_out"], modv, lnv, tm)
    blk_fi = to_blocks("w_ffn_in", _wgrad(dF, h2, "wgrad_ffn_in", 1408, tt))
    blk_fo = to_blocks("w_ffn_out", _wgrad(act_f, df, "wgrad_ffn_out", 1408, tt))
    (dA, dB, dga, dgb, dya, dyb), (rcv_fo,) = _k_merge_bwd(
        dmix, ya, yb, ga, gb, full["w_branch_a"], full["w_branch_b"], full["w_out"], tm_in, comm=_Comm(scatter=[blk_fo]))
    du, dvb, g_ws, g_bst, g_gln = _k_gmlp_bwd(u, vb, dyb, gm_lnv, ws_b, wst_b, bsp)
    (dq, dk, dv, dkc, dvc, g_sink), (gath_ws, rcv_fi) = _k_attn_bwd(
        sink, q, k, v, kc, vc, dya, lse, cos, sin, bias,
        comm=_Comm(gather=[g_ws.reshape(N_GROUPS * BLK, BLK)], scatter=[blk_fi]))
    blk_a = to_blocks("w_branch_a", _wgrad(ya, dA, "wgrad_a", Q_W, tt))
    blk_b = to_blocks("w_branch_b", _wgrad(yb, dB, "wgrad_b", GM_W, tt))
    blk_o = to_blocks("w_out", _wgrad(merged, dmix, "wgrad_out", D, tt))
    (dP, grad_x, acc_i), _ = _k_in_bwd(dq, dk, dv, du, dvb, dga, dgb, x2, dxp, full["w_in"], modv, tm_in)
    g_ctx, dmodc = _k_ctx_bwd(ctx2, modc, hc, dkc, dvc, w_kv)
    gw_in, (rcv_a, rcv_b, rcv_o) = _wgrad(dP, h, "wgrad_in", 1280, tt, comm=_Comm(scatter=[blk_a, blk_b, blk_o]),
                                          extra=(O_K, g_ctx))

    dmod_x = jnp.concatenate([acc_i[0:2], acc_b[4:5], acc_b[0:2], acc_f[2:3]], axis=0)
    small = jnp.concatenate([
        dmod_x, dmodc[0:2], acc_b[2:4], acc_f[0:2],
        jnp.concatenate([g_gln[0:1], g_gln[1:2]], axis=1), g_bst.T.reshape(1, D),
        _pad_rows(g_sink[:, 0:1], D).T, acc_f[3:4]], axis=0)
    rcv_in, gath = _exchange_two_level(to_blocks("w_in", gw_in), small, "exchange_last")
    received = dict(w_in=rcv_in, w_branch_a=rcv_a, w_branch_b=rcv_b, w_out=rcv_o, w_ffn_in=rcv_fi, w_ffn_out=rcv_fo)
    moments = dict(w_in=(m_w_in, v_w_in), w_branch_a=(m_w_branch_a, v_w_branch_a), w_branch_b=(m_w_branch_b, v_w_branch_b),
                   w_out=(m_w_out, v_w_out), w_ffn_in=(m_w_ffn_in, v_w_ffn_in), w_ffn_out=(m_w_ffn_out, v_w_ffn_out))
    names = list(big)
    res = {}
    for kname in names:
        mm, vv = moments[kname]
        R = big[kname].shape[0]
        res[kname] = [tr(kname, r) for r in _adamw_reduce(
            received[kname], big[kname], tr(kname, mm[0]), tr(kname, vv[0]), "adamw_" + kname, 256 if R % 256 == 0 else R // 2)]

    ws2d = lambda a: a.reshape(N_GROUPS * BLK, BLK)
    res_ws = [r.reshape(w_spatial.shape) for r in _adamw_reduce(
        gath_ws, ws2d(w_spatial), ws2d(m_w_spatial), ws2d(v_w_spatial), "adamw_w_spatial", 256)]
    tot, res_small = _small_step(gath, dict(
        b_ada=(b_ada, m_b_ada, v_b_ada), ln1_g=(ln1_g, m_ln1_g, v_ln1_g), ln1_b=(ln1_b, m_ln1_b, v_ln1_b),
        ln2_g=(ln2_g, m_ln2_g, v_ln2_g), ln2_b=(ln2_b, m_ln2_b, v_ln2_b),
        gmlp_ln_g=(gmlp_ln_g, m_gmlp_ln_g, v_gmlp_ln_g), gmlp_ln_b=(gmlp_ln_b, m_gmlp_ln_b, v_gmlp_ln_b),
        b_spatial=(b_spatial, m_b_spatial, v_b_spatial), attn_sink=(attn_sink, m_attn_sink, v_attn_sink)))
    loss = tot[15, 0]

    dmod_rows = jnp.concatenate([gath[:, 0:6, :].reshape(N_DEV, 6 * D),
                                 jnp.concatenate([tot[6:8].reshape(1, 2 * D), jnp.zeros((1, 4 * D), F32)], axis=1),
                                 jnp.zeros((7, 6 * D), F32)], axis=0)
    dmod_my = lax.dynamic_slice(dmod_rows, (0, me * n_ada), (16, n_ada))
    g_wada, d_wada, m2_wada, v2_wada, pc = _ada_bwd(act, dmod_my, w_ada[0], m_w_ada[0], v_w_ada[0])
    pc_all = _ag_small(pc, "gather_cctx")
    cc8 = lambda a: _pad_rows(a.reshape(1, D), 8)
    g_cc, d_cc, m2_cc, v2_cc = _cctx_finish(pc_all, cc8(c_ctx), cc8(m_c_ctx), cc8(v_c_ctx))

    order = ["c_ctx", "w_ada", "b_ada", "w_in", "attn_sink", "gmlp_ln_g", "gmlp_ln_b", "w_spatial", "b_spatial",
             "w_branch_a", "w_branch_b", "w_out", "ln1_g", "ln1_b", "w_ffn_in", "w_ffn_out", "ln2_g", "ln2_b"]
    grads, deltas, new_m, new_v = {}, {}, {}, {}
    grads["c_ctx"], deltas["c_ctx"], new_m["c_ctx"], new_v["c_ctx"] = g_cc[0], d_cc[0], m2_cc[0], v2_cc[0]
    grads["w_ada"], deltas["w_ada"], new_m["w_ada"], new_v["w_ada"] = g_wada[None], d_wada[None], m2_wada[None], v2_wada[None]
    for kname in names:
        g, d, m2, v2 = res[kname]
        grads[kname], deltas[kname], new_m[kname], new_v[kname] = g[None], d[None], m2[None], v2[None]
    grads["w_spatial"], deltas["w_spatial"], new_m["w_spatial"], new_v["w_spatial"] = res_ws
    for kname in SMALL_ORDER:
        grads[kname], deltas[kname], new_m[kname], new_v[kname] = res_small[kname]
    return (loss, grad_x[None], *[grads[n] for n in order], *[deltas[n] for n in order],
            *[new_m[n] for n in order], *[new_v[n] for n in order])
```

```python
import functools
import math

import jax
import jax.numpy as jnp
import numpy as np
from jax import lax
from jax.experimental import pallas as pl
from jax.experimental.pallas import tpu as pltpu

F32 = jnp.float32
BF16 = jnp.bfloat16
MESH = pl.DeviceIdType.MESH

N_DEV = 8
D = 1024
HEAD_DIM = 64
N_Q_HEADS = 8
N_KV_HEADS = 2
GQA_GROUP = 4
BLK = 128
Q_W = 512
KV_W = 128
GM_W = 512
N_GROUPS = 8
GROUP_DIM = 64
FFN_H = 2816
IN_W = 3840
O_Q, O_K, O_V, O_U, O_VB, O_GA, O_GB = 0, 512, 640, 768, 1280, 1792, 2816
LN_EPS = 1e-5
NEG_INF = -1e30
ALPHA = 2.0 ** 0.25
ROPE_BASE = 10000.0
ROPE_PAIRS = 16
Q_SCALE = HEAD_DIM ** -0.5
GELU_K0 = math.sqrt(2.0 / math.pi)
GELU_K1 = 0.044715

ADAM_LR = 0.001
ADAM_B1 = 0.9
ADAM_B2 = 0.999
ADAM_EPS = 1e-08
ADAM_WD = 0.01
ADAM_STEP = 10

VMEM_LIMIT = 56 * 1024 * 1024
NT = (((1,), (1,)), ((), ()))
TN = (((0,), (0,)), ((), ()))


def _params(sem=None):
    return pltpu.CompilerParams(dimension_semantics=sem, vmem_limit_bytes=VMEM_LIMIT)


def _row(tm, w):
    return pl.BlockSpec((tm, w), lambda i: (i, 0))


def _full(shape):
    nd = len(shape)
    return pl.BlockSpec(shape, lambda i: (0,) * nd)


def _resident(shape):
    nd = len(shape)
    return pl.BlockSpec(shape, lambda i: (0,) * nd, pipeline_mode=pl.Buffered(1))


def _sds(shape, dt):
    return jax.ShapeDtypeStruct(shape, dt)


def _ln(xf):
    mu = jnp.mean(xf, axis=-1, keepdims=True)
    xc = xf - mu
    var = jnp.mean(xc * xc, axis=-1, keepdims=True)
    rstd = lax.rsqrt(var + LN_EPS)
    return xc * rstd, rstd


def _ln_bwd(dn, n, rstd):
    m1 = jnp.mean(dn, axis=-1, keepdims=True)
    m2 = jnp.mean(dn * n, axis=-1, keepdims=True)
    return rstd * (dn - m1 - n * m2)


def _colsum(t):
    return jnp.sum(t, axis=0, keepdims=True)


def _sigmoid(x):
    return 0.5 * jnp.tanh(0.5 * x) + 0.5


def _gelu(x):
    t = jnp.tanh(GELU_K0 * (x + GELU_K1 * (x * x * x)))
    return x * (0.5 * (1.0 + t)), t


def _gelu_grad(x, t):
    return 0.5 * (1.0 + t) + 0.5 * x * (1.0 - t * t) * (GELU_K0 * (1.0 + 3.0 * GELU_K1 * x * x))


def _swap16(t):
    lane = lax.broadcasted_iota(jnp.int32, t.shape, 1)
    return jnp.where((lane & 16) == 0, pltpu.roll(t, 112, 1), pltpu.roll(t, 16, 1))


def _rope(t, cos, sin):
    return t * cos + _swap16(t) * sin


def _unrope(t, cos, sin):
    return t * cos - _swap16(t) * sin


def _adamw(w, g, m, v):
    m2 = ADAM_B1 * m + (1.0 - ADAM_B1) * g
    v2 = ADAM_B2 * v + (1.0 - ADAM_B2) * (g * g)
    m_hat = m2 / (1.0 - ADAM_B1 ** ADAM_STEP)
    v_hat = v2 / (1.0 - ADAM_B2 ** ADAM_STEP)
    delta = -ADAM_LR * (m_hat / (jnp.sqrt(v_hat) + ADAM_EPS) + ADAM_WD * w)
    return delta, m2, v2


def _rope_tables(L):
    inv = (np.float32(ROPE_BASE) ** (-np.arange(ROPE_PAIRS, dtype=np.float32) / np.float32(ROPE_PAIRS))).astype(np.float32)
    t = np.arange(L, dtype=np.int32)
    rows = (t // 64).astype(np.float32)[:, None] * inv
    cols = (t % 64).astype(np.float32)[:, None] * inv
    cr, sr, cc, sc = np.cos(rows), np.sin(rows), np.cos(cols), np.sin(cols)
    cos = np.concatenate([cr, cr, cc, cc], axis=1)
    sin = np.concatenate([-sr, sr, -sc, sc], axis=1)
    return jnp.asarray(np.tile(cos, (1, 2)), F32), jnp.asarray(np.tile(sin, (1, 2)), F32)


def _me():
    return lax.axis_index("x"), lax.axis_index("y"), lax.axis_index("c")


def _peer(mx, my, mc, k):
    return (mx ^ ((k >> 2) & 1), my ^ ((k >> 1) & 1), mc ^ (k & 1))


def _ag_small(x, name):
    R, C = x.shape

    def body(x_ref, out_ref, send_sems, recv_sems):
        mx, my, mc = _me()
        me = 4 * mx + 2 * my + mc
        out_ref[pl.ds(me, 1)] = x_ref[...][None]
        sends = []
        for k in range(1, N_DEV):
            cp = pltpu.make_async_remote_copy(
                src_ref=x_ref, dst_ref=out_ref.at[me], send_sem=send_sems.at[k - 1], recv_sem=recv_sems.at[k - 1],
                device_id=_peer(mx, my, mc, k), device_id_type=MESH)
            cp.start()
            sends.append(cp)
        for k in range(1, N_DEV):
            pltpu.make_async_remote_copy(
                src_ref=x_ref, dst_ref=out_ref.at[me ^ k], send_sem=send_sems.at[k - 1], recv_sem=recv_sems.at[k - 1],
                device_id=(mx, my, mc), device_id_type=MESH).wait_recv()
        for cp in sends:
            cp.wait_send()

    return pl.pallas_call(
        body, name=name,
        out_shape=_sds((N_DEV, R, C), x.dtype),
        in_specs=[pl.BlockSpec(memory_space=pltpu.VMEM)],
        out_specs=pl.BlockSpec(memory_space=pltpu.VMEM),
        scratch_shapes=[pltpu.SemaphoreType.DMA((N_DEV - 1,)), pltpu.SemaphoreType.DMA((N_DEV - 1,))],
        compiler_params=pltpu.CompilerParams(vmem_limit_bytes=VMEM_LIMIT),
    )(x)


class _Comm:
    def __init__(self, gather=(), scatter=(), spread=()):
        self.kinds = ["gather"] * len(gather) + ["scatter"] * len(scatter) + ["spread"] * len(spread)
        self.args = list(gather) + list(scatter) + list(spread)
        self.n = len(self.args)

    def out_shape(self):
        return [_sds(a.shape if k == "scatter" else (N_DEV,) + a.shape, a.dtype) for k, a in zip(self.kinds, self.args)]

    def specs(self):
        return [pl.BlockSpec(memory_space=pl.ANY)] * self.n

    def scratch(self):
        return [pltpu.SemaphoreType.DMA((7 * self.n,)), pltpu.SemaphoreType.DMA((7 * self.n,)),
                pltpu.SemaphoreType.DMA((self.n,))]

    def _plan(self, x_refs, out_refs, send_sems, recv_sems, local_sems):
        mx, my, mc = _me()
        me = 4 * mx + 2 * my + mc
        here, sibling = (mx, my, mc), (mx, my, 1 - mc)
        chips = [(1 - mx, my), (mx, 1 - my), (1 - mx, 1 - my)]
        local, first, last = [], [], []
        relay = [[], [], []]
        for a, kind in enumerate(self.kinds):
            x, out = x_refs[a], out_refs[a]

            def rc(k, src, dst, to):
                return pltpu.make_async_remote_copy(
                    src_ref=src, dst_ref=dst, send_sem=send_sems.at[7 * a + k], recv_sem=recv_sems.at[7 * a + k],
                    device_id=to, device_id_type=MESH)

            if kind == "gather":
                local.append(pltpu.make_async_copy(x, out.at[me], local_sems.at[a]))
                first.append(rc(0, x, out.at[me], sibling))
                last.append(rc(0, x, out.at[me ^ 1], here))
                for j, (cx, cy) in enumerate(chips):
                    first.append(rc(1 + j, x, out.at[me], (cx, cy, mc)))
                    landed = out.at[4 * cx + 2 * cy + mc]
                    relay[j].append((rc(1 + j, x, landed, here), rc(4 + j, landed, landed, sibling)))
                    last.append(rc(4 + j, x, out.at[4 * cx + 2 * cy + 1 - mc], here))
            else:
                own = x.at[me] if kind == "scatter" else x
                local.append(pltpu.make_async_copy(own, out.at[me], local_sems.at[a]))
                for k in range(1, N_DEV):
                    src = x.at[me ^ k] if kind == "scatter" else x
                    first.append(rc(k - 1, src, out.at[me], _peer(mx, my, mc, k)))
                    last.append(rc(k - 1, own, out.at[me ^ k], here))
        return local, first, relay[0] + relay[1] + relay[2], last

    def start(self, *refs):
        local, first, _, _ = self._plan(*refs)
        for cp in local + first:
            cp.start()

    def finish(self, *refs):
        local, first, relay, last = self._plan(*refs)
        for arrival, onward in relay:
            arrival.wait_recv()
            onward.start()
        for cp in last:
            cp.wait_recv()
        for cp in first:
            cp.wait_send()
        for _, onward in relay:
            onward.wait_send()
        for cp in local:
            cp.wait()


def _call(body, *, name, grid, in_specs, out_specs, out_shape, args, scratch=(), comm=None, aliases=None):
    params = _params(("arbitrary",) * len(grid))

    def at(end):
        conds = [pl.program_id(d) == (n - 1 if end else 0) for d, n in enumerate(grid)]
        return functools.reduce(lambda p, q: p & q, conds)

    if comm is None:
        res = pl.pallas_call(
            body, name=name, grid=grid, in_specs=list(in_specs), out_specs=list(out_specs), out_shape=list(out_shape),
            scratch_shapes=list(scratch), input_output_aliases=aliases or {}, compiler_params=params)(*args)
        return list(res), []
    n_in, n_out, n_scr, cn = len(in_specs), len(out_specs), len(scratch), comm.n

    def hosted(*refs):
        ins, refs = refs[:n_in], refs[n_in:]
        cins, refs = refs[:cn], refs[cn:]
        outs, refs = refs[:n_out], refs[n_out:]
        couts, refs = refs[:cn], refs[cn:]
        scr, sems = refs[:n_scr], refs[n_scr:]

        @pl.when(at(False))
        def _():
            comm.start(cins, couts, *sems)

        body(*ins, *outs, *scr)

        @pl.when(at(True))
        def _():
            comm.finish(cins, couts, *sems)

    res = pl.pallas_call(
        hosted, name=name, grid=grid, in_specs=list(in_specs) + comm.specs(), out_specs=list(out_specs) + comm.specs(),
        out_shape=list(out_shape) + comm.out_shape(), scratch_shapes=list(scratch) + comm.scratch(),
        input_output_aliases=aliases or {}, compiler_params=params)(*args, *comm.args)
    return list(res[:n_out]), list(res[n_out:])


def _exchange_two_level(blk, small, name):
    _, R, C = blk.shape
    rows = small.shape[0]

    def body(blk_ref, small_ref, stage_ref, out_ref, gath_ref, a_scr, b_scr, t_scr, s1, r1, s3, r3, ss, rs, lsem):
        mx, my, mc = _me()
        me = 4 * mx + 2 * my + mc
        mine = 2 * mx + my
        here, sibling = (mx, my, mc), (mx, my, 1 - mc)

        def rc(src, dst, send, recv, to):
            return pltpu.make_async_remote_copy(src_ref=src, dst_ref=dst, send_sem=send, recv_sem=recv,
                                                device_id=to, device_id_type=MESH)

        own_small = pltpu.make_async_copy(small_ref, gath_ref.at[me], lsem.at[0])
        own_small.start()
        spread = [rc(small_ref, gath_ref.at[me], ss.at[k - 1], rs.at[k - 1], _peer(mx, my, mc, k)) for k in range(1, N_DEV)]
        to_sib = [rc(blk_ref.at[2 * p + 1 - mc], stage_ref.at[p], s1.at[p], r1.at[p], sibling) for p in range(4)]
        for cp in spread + to_sib:
            cp.start()
        own = [pltpu.make_async_copy(blk_ref.at[2 * p + mc], a_scr.at[p], lsem.at[1 + p]) for p in range(4)]
        for cp in own:
            cp.start()
        from_sib = []
        for p in range(4):
            rc(blk_ref.at[2 * p + 1 - mc], stage_ref.at[p], s1.at[p], r1.at[p], here).wait_recv()
            cp = pltpu.make_async_copy(stage_ref.at[p], b_scr.at[p], lsem.at[5 + p])
            cp.start()
            from_sib.append(cp)
        for cp in own + from_sib:
            cp.wait()
        t_scr[...] = (a_scr[...].astype(F32) + b_scr[...].astype(F32)).astype(BF16)
        keep = pltpu.make_async_copy(t_scr.at[mine], out_ref.at[mine], lsem.at[9])
        keep.start()
        onward = [rc(t_scr.at[mine ^ k], out_ref.at[mine], s3.at[k - 1], r3.at[k - 1], (mx ^ (k >> 1), my ^ (k & 1), mc))
                  for k in range(1, 4)]
        for cp in onward:
            cp.start()
        for k in range(1, 4):
            rc(t_scr.at[mine], out_ref.at[mine ^ k], s3.at[k - 1], r3.at[k - 1], here).wait_recv()
        for k in range(1, N_DEV):
            rc(small_ref, gath_ref.at[me ^ k], ss.at[k - 1], rs.at[k - 1], here).wait_recv()
        for cp in spread + to_sib + onward:
            cp.wait_send()
        keep.wait()
        own_small.wait()

    any_spec = pl.BlockSpec(memory_space=pl.ANY)
    dma = pltpu.SemaphoreType.DMA
    _, out, gath = pl.pallas_call(
        body, name=name,
        in_specs=[any_spec, any_spec], out_specs=[any_spec] * 3,
        out_shape=[_sds((4, R, C), BF16), _sds((4, R, C), BF16), _sds((N_DEV, rows, D), F32)],
        scratch_shapes=[pltpu.VMEM((4, R, C), BF16)] * 3
                       + [dma((4,)), dma((4,)), dma((3,)), dma((3,)), dma((N_DEV - 1,)), dma((N_DEV - 1,)), dma((10,))],
        compiler_params=pltpu.CompilerParams(vmem_limit_bytes=VMEM_LIMIT),
    )(blk, small)
    return out, gath


def _comm_only(comm, name):
    return _call(lambda: None, name=name, grid=(1,), in_specs=[], out_specs=[], out_shape=[], args=[], comm=comm)[1]


def _exchange_rows(x_ref, out_ref, send_sems, recv_sems):
    mx, my, mc = _me()
    me = 4 * mx + 2 * my + mc
    out_ref[pl.ds(me, 1)] = x_ref[...][None]
    sends = []
    for k in range(1, N_DEV):
        cp = pltpu.make_async_remote_copy(
            src_ref=x_ref, dst_ref=out_ref.at[me], send_sem=send_sems.at[k - 1], recv_sem=recv_sems.at[k - 1],
            device_id=_peer(mx, my, mc, k), device_id_type=MESH)
        cp.start()
        sends.append(cp)
    for k in range(1, N_DEV):
        pltpu.make_async_remote_copy(
            src_ref=x_ref, dst_ref=out_ref.at[me ^ k], send_sem=send_sems.at[k - 1], recv_sem=recv_sems.at[k - 1],
            device_id=(mx, my, mc), device_id_type=MESH).wait_recv()
    for cp in sends:
        cp.wait_send()


def _prologue(c8, cctx8, w_ada, b_my, comm):
    nw = w_ada.shape[1]

    def body(c_ref, cctx_ref, w_ref, b_ref, act_ref, mod_ref, cmine_scr, call_scr, mine_scr, mall_scr, s1, r1, s2, r2):
        cmine_scr[...] = c_ref[...]
        _exchange_rows(cmine_scr, call_scr, s1, r1)
        rows = [call_scr[d][0:1, :] for d in range(N_DEV)] + [cctx_ref[0:1, :], jnp.zeros((7, D), F32)]
        s = jnp.concatenate(rows, axis=0)
        act = s * _sigmoid(s)
        act_ref[...] = act
        mine_scr[...] = jnp.dot(act.astype(BF16), w_ref[...].astype(BF16), preferred_element_type=F32) + b_ref[...]
        _exchange_rows(mine_scr, mall_scr, s2, r2)
        mod_ref[...] = mall_scr[...]

    sems = [pltpu.SemaphoreType.DMA((N_DEV - 1,))] * 4
    (act, mod), got = _call(
        body, name="prologue", grid=(1,),
        in_specs=[_full((8, D)), _full((8, D)), _full((D, nw)), _full((1, nw))],
        out_specs=[_full((16, D)), _full((N_DEV, 16, nw))],
        out_shape=[_sds((16, D), F32), _sds((N_DEV, 16, nw), F32)],
        scratch=[pltpu.VMEM((8, D), F32), pltpu.VMEM((N_DEV, 8, D), F32), pltpu.VMEM((16, nw), F32),
                 pltpu.VMEM((N_DEV, 16, nw), F32)] + sems,
        args=(c8, cctx8, w_ada, b_my), comm=comm)
    return act, mod, got


def _ada_bwd(act, dmod_my, w_ada, m, v, tr=256):
    nw = w_ada.shape[1]

    def body(act_ref, dm_ref, w_ref, m_ref, v_ref, g_ref, d_ref, m2_ref, v2_ref, pc_ref):
        dm = dm_ref[...].astype(BF16)
        g = lax.dot_general(act_ref[...].astype(BF16), dm, TN, preferred_element_type=F32)
        w = w_ref[...]
        delta, m2, v2 = _adamw(w, g, m_ref[...], v_ref[...])
        g_ref[...] = g
        d_ref[...] = delta
        m2_ref[...] = m2
        v2_ref[...] = v2
        pc_ref[...] = lax.dot_general(dm[8:16, :], w.astype(BF16), NT, preferred_element_type=F32)

    wspec = _row(tr, nw)
    return pl.pallas_call(
        body, name="ada_bwd", grid=(D // tr,),
        in_specs=[pl.BlockSpec((16, tr), lambda i: (0, i)), _full((16, nw)), wspec, wspec, wspec],
        out_specs=[wspec, wspec, wspec, wspec, pl.BlockSpec((8, tr), lambda i: (0, i))],
        out_shape=[_sds((D, nw), F32)] * 4 + [_sds((8, D), F32)],
        compiler_params=_params(("arbitrary",)),
    )(act, dmod_my, w_ada, m, v)


def _k_in(x, modv, w_in, cos, sin, tm, comm=None):
    L = x.shape[0]

    def body(x_ref, mod_ref, w_ref, cos_ref, sin_ref, h_ref, q_ref, k_ref, v_ref, u_ref, vb_ref, ga_ref, gb_ref):
        n, _ = _ln(x_ref[...])
        h = (n * (1.0 + mod_ref[1:2, :]) + mod_ref[0:1, :]).astype(BF16)
        h_ref[...] = h
        c, s = cos_ref[...], sin_ref[...]

        def proj(lo, width):
            return lax.dot_general(h, w_ref[lo:lo + width, :], NT, preferred_element_type=F32)

        for i in range(4):
            q_ref[:, i * 128:(i + 1) * 128] = (_rope(proj(O_Q + i * 128, 128), c, s) * Q_SCALE).astype(BF16)
        k_ref[...] = _rope(proj(O_K, KV_W), c, s).astype(BF16)
        v_ref[...] = proj(O_V, KV_W).astype(BF16)
        u_ref[...] = proj(O_U, GM_W).astype(BF16)
        vb_ref[...] = proj(O_VB, GM_W).astype(BF16)
        ga_ref[...] = proj(O_GA, D).astype(BF16)
        gb_ref[...] = proj(O_GB, D).astype(BF16)

    widths = [D, Q_W, KV_W, KV_W, GM_W, GM_W, D, D]
    return _call(
        body, name="fwd_in", grid=(L // tm,),
        in_specs=[_row(tm, D), _full((8, D)), _resident((IN_W, D)), _row(tm, 128), _row(tm, 128)],
        out_specs=[_row(tm, w) for w in widths],
        out_shape=[_sds((L, w), BF16) for w in widths],
        args=(x, modv, w_in, cos, sin), comm=comm)


def _k_ctx(ctx, modc, w_kv):
    C = ctx.shape[0]

    def body(c_ref, mod_ref, w_ref, hc_ref, kc_ref, vc_ref):
        n, _ = _ln(c_ref[...])
        hc = (n * (1.0 + mod_ref[1:2, :]) + mod_ref[0:1, :]).astype(BF16)
        hc_ref[...] = hc
        kv = lax.dot_general(hc, w_ref[...], NT, preferred_element_type=F32)
        kc_ref[...] = kv[:, :KV_W].astype(BF16)
        vc_ref[...] = kv[:, KV_W:].astype(BF16)

    return pl.pallas_call(
        body, name="fwd_ctx", grid=(1,),
        in_specs=[_full((C, D)), _full((8, D)), _full((2 * KV_W, D))],
        out_specs=[_full((C, D)), _full((C, KV_W)), _full((C, KV_W))],
        out_shape=[_sds((C, D), BF16), _sds((C, KV_W), BF16), _sds((C, KV_W), BF16)],
        compiler_params=_params(("arbitrary",)),
    )(ctx, modc, w_kv)


def _attn_bias():
    r = (np.arange(GQA_GROUP * BLK) & (BLK - 1))[:, None]
    j = np.arange(3 * BLK)[None, :]
    band = np.abs(j - BLK - r) <= BLK
    variants = [band & (j >= BLK), band, band & (j < 2 * BLK)]
    return jnp.asarray(np.stack([np.where(v, 0.0, NEG_INF) for v in variants]), F32)


def _masked(s, bias, C):
    return jnp.concatenate([s[:, :C], s[:, C:] + bias], axis=1)


def _sink_col(sink_ref, hk):
    grp = lax.broadcasted_iota(jnp.int32, (GQA_GROUP * BLK, 1), 0) >> 7
    col = jnp.full((GQA_GROUP * BLK, 1), sink_ref[hk * GQA_GROUP], F32)
    for g in range(1, GQA_GROUP):
        col = jnp.where(grp == g, sink_ref[hk * GQA_GROUP + g], col)
    return col


def _k_attn(sink, q, k, v, kc, vc, bias, comm=None):
    L = q.shape[0]
    C = kc.shape[0]
    nb = L // BLK
    steps = nb // 2

    def body(sink_ref, q_ref, kp_ref, km_ref, kx_ref, vp_ref, vm_ref, vx_ref, kc_ref, vc_ref, bias_ref, ya_ref, lse_ref):
        i = pl.program_id(0)
        bands = [bias_ref[jnp.where(i == 0, 0, 1)], bias_ref[jnp.where(i == steps - 1, 2, 1)]]
        chains = [(qb, hk) for qb in range(2) for hk in range(N_KV_HEADS)]

        def keys(ctx_ref, p_ref, m_ref, x_ref, qb, hk):
            sl = slice(hk * HEAD_DIM, (hk + 1) * HEAD_DIM)
            band3 = ([p_ref[:, sl], m_ref[0:BLK, sl], m_ref[BLK:2 * BLK, sl]] if qb == 0
                     else [m_ref[0:BLK, sl], m_ref[BLK:2 * BLK, sl], x_ref[:, sl]])
            return jnp.concatenate([ctx_ref[:, sl]] + band3, axis=0)

        def queries(qb, hk):
            return jnp.concatenate(
                [q_ref[qb * BLK:(qb + 1) * BLK, (hk * GQA_GROUP + g) * HEAD_DIM:(hk * GQA_GROUP + g + 1) * HEAD_DIM]
                 for g in range(GQA_GROUP)], axis=0)

        s = [_masked(lax.dot_general(queries(qb, hk), keys(kc_ref, kp_ref, km_ref, kx_ref, qb, hk), NT,
                                     preferred_element_type=F32), bands[qb], C) for qb, hk in chains]
        for (qb, hk), s_ in zip(chains, s):
            sink_c = _sink_col(sink_ref, hk)
            m = jnp.maximum(jnp.max(s_, axis=1, keepdims=True), sink_c)
            p = jnp.exp(s_ - m)
            den = jnp.sum(p, axis=1, keepdims=True) + jnp.exp(sink_c - m)
            o = jnp.dot(p.astype(BF16), keys(vc_ref, vp_ref, vm_ref, vx_ref, qb, hk), preferred_element_type=F32) * (1.0 / den)
            lse = m + jnp.log(den)
            rows = slice(qb * BLK, (qb + 1) * BLK)
            for g in range(GQA_GROUP):
                h = hk * GQA_GROUP + g
                ya_ref[rows, h * HEAD_DIM:(h + 1) * HEAD_DIM] = o[g * BLK:(g + 1) * BLK, :].astype(BF16)
                lse_ref[rows, h:h + 1] = lse[g * BLK:(g + 1) * BLK, :]

    kv3 = [pl.BlockSpec((BLK, KV_W), lambda i: (jnp.maximum(2 * i - 1, 0), 0)),
           pl.BlockSpec((2 * BLK, KV_W), lambda i: (i, 0)),
           pl.BlockSpec((BLK, KV_W), lambda i: (jnp.minimum(2 * i + 2, nb - 1), 0))]
    return _call(
        body, name="fwd_attn", grid=(steps,),
        in_specs=[pl.BlockSpec(memory_space=pltpu.SMEM), _row(2 * BLK, Q_W)] + kv3 + kv3
                 + [_full((C, KV_W)), _full((C, KV_W)), _full((3, GQA_GROUP * BLK, 3 * BLK))],
        out_specs=[_row(2 * BLK, Q_W), _row(2 * BLK, N_Q_HEADS)],
        out_shape=[_sds((L, Q_W), BF16), _sds((L, N_Q_HEADS), F32)],
        args=(sink, q, k, k, k, v, v, v, kc, vc, bias), comm=comm)


GMLP_CHUNKS = 4


def _split_pair(t):
    low = lax.broadcasted_iota(jnp.int32, t.shape, 1) < GROUP_DIM
    zero = jnp.zeros_like(t)
    return jnp.where(low, t, zero), jnp.where(low, zero, t)


def _gmlp_spatial(w_ref, t_b, nch):
    rows = []
    for c in range(nch):
        tiles = []
        for pr in range(N_GROUPS // 2):
            lo, hi = _split_pair(t_b[c * BLK:(c + 1) * BLK, pr * 128:(pr + 1) * 128])
            tiles.append(jnp.dot(w_ref[2 * pr], lo, preferred_element_type=F32)
                         + jnp.dot(w_ref[2 * pr + 1], hi, preferred_element_type=F32))
        rows.append(jnp.concatenate(tiles, axis=1))
    return jnp.concatenate(rows, axis=0)


def _gmlp_fwd_vals(u, vb, lnv_ref, ws_ref, bsp_ref, nch):
    uf = u.astype(F32)
    vf = vb.astype(F32)
    gu, tu = _gelu(uf)
    gv, tv = _gelu(vf)
    vhat, rstd = _ln(gv)
    vn = (vhat * lnv_ref[0:1, :] + lnv_ref[1:2, :]).astype(BF16)
    s = _gmlp_spatial(ws_ref, vn, nch) + jnp.concatenate([bsp_ref[...]] * nch, axis=0)
    return uf, vf, gu, tu, tv, vhat, rstd, vn, s


def _k_gmlp(u, vb, lnv, ws, bsp):
    L = u.shape[0]
    nch = min(GMLP_CHUNKS, L // BLK)
    tm = nch * BLK

    def body(u_ref, vb_ref, lnv_ref, ws_ref, bsp_ref, yb_ref):
        _, _, gu, _, _, _, _, _, s = _gmlp_fwd_vals(u_ref[...], vb_ref[...], lnv_ref, ws_ref, bsp_ref, nch)
        yb_ref[...] = (gu * s).astype(BF16)

    return pl.pallas_call(
        body, name="fwd_gmlp", grid=(L // tm,),
        in_specs=[_row(tm, GM_W), _row(tm, GM_W), _full((8, GM_W)), _full((N_GROUPS, BLK, BLK)), _full((BLK, GM_W))],
        out_specs=_row(tm, GM_W),
        out_shape=_sds((L, GM_W), BF16),
        compiler_params=_params(("arbitrary",)),
    )(u, vb, lnv, ws, bsp)


def _k_merge(x, ya, yb, ga, gb, w_a, w_b, w_o, modv, lnv, tm):
    L = x.shape[0]

    def body(x_ref, ya_ref, yb_ref, ga_ref, gb_ref, wa_ref, wb_ref, wo_ref, mod_ref, ln_ref,
             mg_ref, mix_ref, xm_ref, h2_ref):
        a = jnp.dot(ya_ref[...], wa_ref[...], preferred_element_type=F32)
        b = jnp.dot(yb_ref[...], wb_ref[...], preferred_element_type=F32)
        merged = (_sigmoid(ga_ref[...].astype(F32)) * a + _sigmoid(gb_ref[...].astype(F32)) * b).astype(BF16)
        mg_ref[...] = merged
        mix = jnp.dot(merged, wo_ref[...], preferred_element_type=F32)
        mix_ref[...] = mix.astype(BF16)
        r1 = ALPHA * x_ref[...] + mod_ref[2:3, :] * mix
        r1hat, _ = _ln(r1)
        xm = r1hat * ln_ref[0:1, :] + ln_ref[1:2, :]
        xm_ref[...] = xm
        n2, _ = _ln(xm)
        h2_ref[...] = (n2 * (1.0 + mod_ref[4:5, :]) + mod_ref[3:4, :]).astype(BF16)

    return pl.pallas_call(
        body, name="fwd_merge", grid=(L // tm,),
        in_specs=[_row(tm, D), _row(tm, Q_W), _row(tm, GM_W), _row(tm, D), _row(tm, D),
                  _resident((Q_W, D)), _resident((GM_W, D)), _resident((D, D)), _full((8, D)), _full((8, D))],
        out_specs=[_row(tm, D)] * 4,
        out_shape=[_sds((L, D), BF16), _sds((L, D), BF16), _sds((L, D), F32), _sds((L, D), BF16)],
        compiler_params=_params(("arbitrary",)),
    )(x, ya, yb, ga, gb, w_a, w_b, w_o, modv, lnv)


FFN_CH = 1408


def _k_ffn(h2, xm, tgt, w_fi, w_fo, modv, lnv, tm):
    L = h2.shape[0]

    def body(h2_ref, xm_ref, t_ref, wi_ref, wo_ref, mod_ref, ln_ref, gate_ref, up_ref, a_ref, dr2_ref, df_ref, acc_ref):
        @pl.when(pl.program_id(0) == 0)
        def _():
            acc_ref[...] = jnp.zeros_like(acc_ref)

        h2v = h2_ref[...]
        f = jnp.zeros((tm, D), F32)
        for j in range(FFN_H // FFN_CH):
            lo = j * FFN_CH
            gate = lax.dot_general(h2v, wi_ref[lo:lo + FFN_CH, :], NT, preferred_element_type=F32)
            up = lax.dot_general(h2v, wi_ref[FFN_H + lo:FFN_H + lo + FFN_CH, :], NT, preferred_element_type=F32)
            act = (gate * _sigmoid(gate) * up).astype(BF16)
            gate_ref[:, lo:lo + FFN_CH] = gate.astype(BF16)
            up_ref[:, lo:lo + FFN_CH] = up.astype(BF16)
            a_ref[:, lo:lo + FFN_CH] = act
            f = f + jnp.dot(act, wo_ref[lo:lo + FFN_CH, :], preferred_element_type=F32)
        gate2 = mod_ref[5:6, :]
        r2 = ALPHA * xm_ref[...] + gate2 * f
        r2hat, rstd = _ln(r2)
        y = r2hat * ln_ref[2:3, :] + ln_ref[3:4, :]
        err = y - t_ref[...]
        dy = err * (1.0 / D)
        dr2 = _ln_bwd(dy * ln_ref[2:3, :], r2hat, rstd)
        dr2_ref[...] = dr2
        df_ref[...] = (gate2 * dr2).astype(BF16)
        acc_ref[0:1, :] += _colsum(dy * r2hat)
        acc_ref[1:2, :] += _colsum(dy)
        acc_ref[2:3, :] += _colsum(dr2 * f)
        acc_ref[3:4, :] += _colsum(err * err) * (0.5 / D)

    return pl.pallas_call(
        body, name="fwd_ffn", grid=(L // tm,),
        in_specs=[_row(tm, D), _row(tm, D), _row(tm, D), _resident((2 * FFN_H, D)), _resident((FFN_H, D)),
                  _full((8, D)), _full((8, D))],
        out_specs=[_row(tm, FFN_H)] * 3 + [_row(tm, D), _row(tm, D), _full((8, D))],
        out_shape=[_sds((L, FFN_H), BF16)] * 3 + [_sds((L, D), F32), _sds((L, D), BF16), _sds((8, D), F32)],
        compiler_params=_params(("arbitrary",)),
    )(h2, xm, tgt, w_fi, w_fo, modv, lnv)


def _k_ffn_bwd(df, gate, up, xm, dr2, x, mix, w_fi, w_fo, modv, lnv, tm):
    L = df.shape[0]

    def body(df_ref, gate_ref, up_ref, xm_ref, dr2_ref, x_ref, mix_ref, wi_ref, wo_ref, mod_ref, ln_ref,
             dF_ref, dmix_ref, dxp_ref, acc_ref):
        @pl.when(pl.program_id(0) == 0)
        def _():
            acc_ref[...] = jnp.zeros_like(acc_ref)

        dfv = df_ref[...]
        chunks = [j * FFN_CH for j in range(FFN_H // FFN_CH)]
        das = [lax.dot_general(dfv, wo_ref[lo:lo + FFN_CH, :], NT, preferred_element_type=F32) for lo in chunks]
        n2, rstd2 = _ln(xm_ref[...])
        mixf = mix_ref[...].astype(F32)
        gate1 = mod_ref[2:3, :]
        r1hat, rstd1 = _ln(ALPHA * x_ref[...] + gate1 * mixf)
        dh2 = jnp.zeros((tm, D), F32)
        for lo, da in zip(chunks, das):
            gate = gate_ref[:, lo:lo + FFN_CH].astype(F32)
            upv = up_ref[:, lo:lo + FFN_CH].astype(F32)
            sg = _sigmoid(gate)
            d_gate = (da * upv * (sg * (1.0 + gate * (1.0 - sg)))).astype(BF16)
            d_up = (da * (gate * sg)).astype(BF16)
            dF_ref[:, lo:lo + FFN_CH] = d_gate
            dF_ref[:, FFN_H + lo:FFN_H + lo + FFN_CH] = d_up
            dh2 = dh2 + jnp.dot(d_gate, wi_ref[lo:lo + FFN_CH, :], preferred_element_type=F32)
            dh2 = dh2 + jnp.dot(d_up, wi_ref[FFN_H + lo:FFN_H + lo + FFN_CH, :], preferred_element_type=F32)
        acc_ref[0:1, :] += _colsum(dh2)
        acc_ref[1:2, :] += _colsum(dh2 * n2)
        dxm = ALPHA * dr2_ref[...] + _ln_bwd(dh2 * (1.0 + mod_ref[4:5, :]), n2, rstd2)
        acc_ref[2:3, :] += _colsum(dxm * r1hat)
        acc_ref[3:4, :] += _colsum(dxm)
        dr1 = _ln_bwd(dxm * ln_ref[0:1, :], r1hat, rstd1)
        dmix_ref[...] = (gate1 * dr1).astype(BF16)
        dxp_ref[...] = ALPHA * dr1
        acc_ref[4:5, :] += _colsum(dr1 * mixf)

    return pl.pallas_call(
        body, name="bwd_ffn", grid=(L // tm,),
        in_specs=[_row(tm, D), _row(tm, FFN_H), _row(tm, FFN_H), _row(tm, D), _row(tm, D), _row(tm, D), _row(tm, D),
                  _resident((2 * FFN_H, D)), _resident((FFN_H, D)), _full((8, D)), _full((8, D))],
        out_specs=[_row(tm, 2 * FFN_H), _row(tm, D), _row(tm, D), _full((8, D))],
        out_shape=[_sds((L, 2 * FFN_H), BF16), _sds((L, D), BF16), _sds((L, D), F32), _sds((8, D), F32)],
        compiler_params=_params(("arbitrary",)),
    )(df, gate, up, xm, dr2, x, mix, w_fi, w_fo, modv, lnv)


def _k_merge_bwd(dmix, ya, yb, ga, gb, w_a, w_b, w_o, tm, comm=None):
    L = dmix.shape[0]

    def body(dmix_ref, ya_ref, yb_ref, ga_ref, gb_ref, wa_ref, wb_ref, wo_ref,
             dA_ref, dB_ref, dga_ref, dgb_ref, dya_ref, dyb_ref):
        dmg = lax.dot_general(dmix_ref[...], wo_ref[...], NT, preferred_element_type=F32)
        a = jnp.dot(ya_ref[...], wa_ref[...], preferred_element_type=F32)
        sa = _sigmoid(ga_ref[...].astype(F32))
        dA = (dmg * sa).astype(BF16)
        dA_ref[...] = dA
        dga_ref[...] = (dmg * a * (sa * (1.0 - sa))).astype(BF16)
        dya_ref[...] = lax.dot_general(dA, wa_ref[...], NT, preferred_element_type=F32).astype(BF16)
        b = jnp.dot(yb_ref[...], wb_ref[...], preferred_element_type=F32)
        sb = _sigmoid(gb_ref[...].astype(F32))
        dB = (dmg * sb).astype(BF16)
        dB_ref[...] = dB
        dgb_ref[...] = (dmg * b * (sb * (1.0 - sb))).astype(BF16)
        dyb_ref[...] = lax.dot_general(dB, wb_ref[...], NT, preferred_element_type=F32).astype(BF16)

    return _call(
        body, name="bwd_merge", grid=(L // tm,),
        in_specs=[_row(tm, D), _row(tm, Q_W), _row(tm, GM_W), _row(tm, D), _row(tm, D),
                  _resident((Q_W, D)), _resident((GM_W, D)), _resident((D, D))],
        out_specs=[_row(tm, D)] * 4 + [_row(tm, Q_W), _row(tm, GM_W)],
        out_shape=[_sds((L, D), BF16)] * 4 + [_sds((L, Q_W), BF16), _sds((L, GM_W), BF16)],
        args=(dmix, ya, yb, ga, gb, w_a, w_b, w_o), comm=comm)


def _k_gmlp_bwd(u, vb, dyb, lnv, ws, wst, bsp):
    L = u.shape[0]
    nch = min(GMLP_CHUNKS, L // BLK)
    tm = nch * BLK

    def body(u_ref, vb_ref, dyb_ref, lnv_ref, ws_ref, wst_ref, bsp_ref, du_ref, dvb_ref, gws_ref, gbst_ref, gln_ref):
        @pl.when(pl.program_id(0) == 0)
        def _():
            gws_ref[...] = jnp.zeros_like(gws_ref)
            gbst_ref[...] = jnp.zeros_like(gbst_ref)
            gln_ref[...] = jnp.zeros_like(gln_ref)

        uf, vf, gu, tu, tv, vhat, rstd, vn, s = _gmlp_fwd_vals(u_ref[...], vb_ref[...], lnv_ref, ws_ref, bsp_ref, nch)
        dyb_f = dyb_ref[...].astype(F32)
        du_ref[...] = (dyb_f * s * _gelu_grad(uf, tu)).astype(BF16)
        ds = dyb_f * gu
        ds_b = ds.astype(BF16)
        for pr in range(N_GROUPS // 2):
            lanes = slice(pr * 128, (pr + 1) * 128)
            gw_lo = gw_hi = ds_sum = None
            for c in range(nch):
                rows = slice(c * BLK, (c + 1) * BLK)
                lo, hi = _split_pair(ds_b[rows, lanes])
                t_lo = lax.dot_general(lo, vn[rows, lanes], NT, preferred_element_type=F32)
                t_hi = lax.dot_general(hi, vn[rows, lanes], NT, preferred_element_type=F32)
                gw_lo = t_lo if c == 0 else gw_lo + t_lo
                gw_hi = t_hi if c == 0 else gw_hi + t_hi
                ds_sum = ds[rows, lanes] if c == 0 else ds_sum + ds[rows, lanes]
            gws_ref[2 * pr] += gw_lo
            gws_ref[2 * pr + 1] += gw_hi
            b_lo, b_hi = _split_pair(ds_sum)
            gbst_ref[:, 2 * pr:2 * pr + 1] += jnp.sum(b_lo, axis=1, keepdims=True)
            gbst_ref[:, 2 * pr + 1:2 * pr + 2] += jnp.sum(b_hi, axis=1, keepdims=True)
        dvn = _gmlp_spatial(wst_ref, ds_b, nch)
        gln_ref[0:1, :] += _colsum(dvn * vhat)
        gln_ref[1:2, :] += _colsum(dvn)
        dgv = _ln_bwd(dvn * lnv_ref[0:1, :], vhat, rstd)
        dvb_ref[...] = (dgv * _gelu_grad(vf, tv)).astype(BF16)

    return pl.pallas_call(
        body, name="bwd_gmlp", grid=(L // tm,),
        in_specs=[_row(tm, GM_W)] * 3 + [_full((8, GM_W)), _full((N_GROUPS, BLK, BLK)), _full((N_GROUPS, BLK, BLK)),
                                         _full((BLK, GM_W))],
        out_specs=[_row(tm, GM_W), _row(tm, GM_W), _full((N_GROUPS, BLK, BLK)), _full((BLK, N_GROUPS)), _full((8, GM_W))],
        out_shape=[_sds((L, GM_W), BF16), _sds((L, GM_W), BF16), _sds((N_GROUPS, BLK, BLK), F32),
                   _sds((BLK, N_GROUPS), F32), _sds((8, GM_W), F32)],
        compiler_params=_params(("arbitrary",)),
    )(u, vb, dyb, lnv, ws, wst, bsp)


def _k_attn_bwd(sink, q, k, v, kc, vc, dya, lse, cos, sin, bias, comm=None):
    L = q.shape[0]
    C = kc.shape[0]
    nb = L // BLK
    steps = nb // 2
    NK = C + 3 * BLK
    chains = [(qb, hk) for qb in range(2) for hk in range(N_KV_HEADS)]

    def body(sink_ref, q_ref, kp_ref, km_ref, kx_ref, vp_ref, vm_ref, vx_ref, kc_ref, vc_ref, do_ref, lse_ref,
             cq_ref, sq_ref, cl_ref, sl_ref, bias_ref,
             dq_ref, dk_ref, dv_ref, dkc_ref, dvc_ref, dsink_ref,
             dq_scr, ck_scr, cv_scr, k1_acc, k2_acc, v1_acc, v2_acc):
        i = pl.program_id(0)

        @pl.when(i == 0)
        def _():
            for r in (k1_acc, k2_acc, v1_acc, v2_acc, dkc_ref, dvc_ref, dsink_ref):
                r[...] = jnp.zeros_like(r)

        @pl.when(i < steps)
        def _():
            bands = [bias_ref[jnp.where(i == 0, 0, 1)], bias_ref[jnp.where(i == steps - 1, 2, 1)]]

            def lanes(hk):
                return slice(hk * HEAD_DIM, (hk + 1) * HEAD_DIM)

            def keys(ctx_ref, p_ref, m_ref, x_ref, qb, hk):
                sl = lanes(hk)
                band3 = ([p_ref[:, sl], m_ref[0:BLK, sl], m_ref[BLK:2 * BLK, sl]] if qb == 0
                         else [m_ref[0:BLK, sl], m_ref[BLK:2 * BLK, sl], x_ref[:, sl]])
                return jnp.concatenate([ctx_ref[:, sl]] + band3, axis=0)

            def stacked(ref, qb, hk, width):
                return jnp.concatenate(
                    [ref[qb * BLK:(qb + 1) * BLK, (hk * GQA_GROUP + g) * width:(hk * GQA_GROUP + g + 1) * width]
                     for g in range(GQA_GROUP)], axis=0)

            def scores(qb, hk):
                kcat = keys(kc_ref, kp_ref, km_ref, kx_ref, qb, hk)
                qg = stacked(q_ref, qb, hk, HEAD_DIM)
                s = _masked(lax.dot_general(qg, kcat, NT, preferred_element_type=F32), bands[qb], C)
                dog = stacked(do_ref, qb, hk, HEAD_DIM)
                dp = lax.dot_general(dog, keys(vc_ref, vp_ref, vm_ref, vx_ref, qb, hk), NT, preferred_element_type=F32)
                return kcat, qg, dog, s, dp

            def softmax_bwd(qb, hk, s, dp):
                lse_c = stacked(lse_ref, qb, hk, 1)
                p = jnp.exp(s - lse_c)
                delta = jnp.sum(p * dp, axis=1, keepdims=True)
                ds = (p * (dp - delta)).astype(BF16)
                p_sink = jnp.exp(_sink_col(sink_ref, hk) - lse_c) * delta
                return p.astype(BF16), ds, p_sink

            def put_dq(qb, hk, dqs, p_sink):
                for g in range(GQA_GROUP):
                    h = hk * GQA_GROUP + g
                    dq_scr[qb * BLK:(qb + 1) * BLK, h * HEAD_DIM:(h + 1) * HEAD_DIM] = dqs[g * BLK:(g + 1) * BLK, :]
                    tot = jnp.sum(p_sink[g * BLK:(g + 1) * BLK, :], axis=0, keepdims=True)
                    dsink_ref[h:h + 1, :] -= jnp.broadcast_to(tot, (1, 128))

            sc = [scores(qb, hk) for qb, hk in chains]
            pending = None
            for (qb, hk), (kcat, qg, dog, s, dp) in zip(chains, sc):
                pb, ds, p_sink = softmax_bwd(qb, hk, s, dp)
                if pending is not None:
                    pqb, phk, pds, ppb, pqg, pdog = pending
                    ck_scr[pqb, :, lanes(phk)] = lax.dot_general(pds, pqg, TN, preferred_element_type=F32)
                    cv_scr[pqb, :, lanes(phk)] = lax.dot_general(ppb, pdog, TN, preferred_element_type=F32)
                put_dq(qb, hk, jnp.dot(ds, kcat, preferred_element_type=F32), p_sink)
                pending = (qb, hk, ds, pb, qg, dog)
            pqb, phk, pds, ppb, pqg, pdog = pending
            ck_scr[pqb, :, lanes(phk)] = lax.dot_general(pds, pqg, TN, preferred_element_type=F32)
            cq, sq = cq_ref[...], sq_ref[...]
            for j in range(4):
                dq_ref[:, j * 128:(j + 1) * 128] = _unrope(dq_scr[:, j * 128:(j + 1) * 128] * Q_SCALE, cq, sq).astype(BF16)
            cv_scr[pqb, :, lanes(phk)] = lax.dot_general(ppb, pdog, TN, preferred_element_type=F32)
            dkc_ref[...] += ck_scr[0, 0:C, :] + ck_scr[1, 0:C, :]
            dvc_ref[...] += cv_scr[0, 0:C, :] + cv_scr[1, 0:C, :]

        @pl.when(i >= steps)
        def _():
            ck_scr[...] = jnp.zeros_like(ck_scr)
            cv_scr[...] = jnp.zeros_like(cv_scr)

        def part(scr, qb, j):
            return scr[qb, C + j * BLK:C + (j + 1) * BLK, :]

        dk_ref[0:BLK, :] = _unrope(k1_acc[...] + part(ck_scr, 0, 0), cl_ref[...], sl_ref[...]).astype(BF16)
        dk_ref[BLK:2 * BLK, :] = _unrope(k2_acc[...] + part(ck_scr, 0, 1) + part(ck_scr, 1, 0),
                                         cq_ref[0:BLK, :], sq_ref[0:BLK, :]).astype(BF16)
        dv_ref[0:BLK, :] = (v1_acc[...] + part(cv_scr, 0, 0)).astype(BF16)
        dv_ref[BLK:2 * BLK, :] = (v2_acc[...] + part(cv_scr, 0, 1) + part(cv_scr, 1, 0)).astype(BF16)
        k1_acc[...] = part(ck_scr, 0, 2) + part(ck_scr, 1, 1)
        v1_acc[...] = part(cv_scr, 0, 2) + part(cv_scr, 1, 1)
        k2_acc[...] = part(ck_scr, 1, 2)
        v2_acc[...] = part(cv_scr, 1, 2)

    last = steps - 1
    kv3 = [pl.BlockSpec((BLK, KV_W), lambda i: (jnp.clip(2 * i - 1, 0, nb - 1), 0)),
           pl.BlockSpec((2 * BLK, KV_W), lambda i: (jnp.minimum(i, last), 0)),
           pl.BlockSpec((BLK, KV_W), lambda i: (jnp.minimum(2 * i + 2, nb - 1), 0))]
    cur = lambda w: pl.BlockSpec((2 * BLK, w), lambda i: (jnp.minimum(i, last), 0))
    late = lambda w: pl.BlockSpec((BLK, w), lambda i: (jnp.clip(2 * i - 1, 0, nb - 1), 0))
    out2 = lambda w: pl.BlockSpec((2 * BLK, w), lambda i: (i, 0))
    return _call(
        body, name="bwd_attn", grid=(steps + 1,),
        in_specs=[pl.BlockSpec(memory_space=pltpu.SMEM), cur(Q_W)] + kv3 + kv3
                 + [_full((C, KV_W)), _full((C, KV_W)), cur(Q_W), cur(N_Q_HEADS), cur(128), cur(128), late(128), late(128),
                    _full((3, GQA_GROUP * BLK, 3 * BLK))],
        out_specs=[cur(Q_W), out2(KV_W), out2(KV_W), _full((C, KV_W)), _full((C, KV_W)), _full((8, 128))],
        out_shape=[_sds((L, Q_W), BF16), _sds((L + 2 * BLK, KV_W), BF16), _sds((L + 2 * BLK, KV_W), BF16),
                   _sds((C, KV_W), F32), _sds((C, KV_W), F32), _sds((8, 128), F32)],
        scratch=[pltpu.VMEM((2 * BLK, Q_W), F32), pltpu.VMEM((2, NK, KV_W), F32), pltpu.VMEM((2, NK, KV_W), F32)]
                + [pltpu.VMEM((BLK, KV_W), F32)] * 4,
        args=(sink, q, k, k, k, v, v, v, kc, vc, dya, lse, cos, sin, cos, sin, bias), comm=comm)


def _k_ctx_bwd(ctx, modc, hc, dkc, dvc, w_kv):
    C = ctx.shape[0]

    def body(c_ref, mod_ref, hc_ref, dkc_ref, dvc_ref, w_ref, gw_ref, dmod_ref):
        dkv = jnp.concatenate([dkc_ref[...], dvc_ref[...]], axis=1).astype(BF16)
        gw_ref[...] = lax.dot_general(dkv, hc_ref[...], TN, preferred_element_type=F32)
        dhc = jnp.dot(dkv, w_ref[...], preferred_element_type=F32)
        n, _ = _ln(c_ref[...])
        dmod_ref[...] = jnp.zeros_like(dmod_ref)
        dmod_ref[0:1, :] = _colsum(dhc)
        dmod_ref[1:2, :] = _colsum(dhc * n)

    return pl.pallas_call(
        body, name="bwd_ctx", grid=(1,),
        in_specs=[_full((C, D)), _full((8, D)), _full((C, D)), _full((C, KV_W)), _full((C, KV_W)), _full((2 * KV_W, D))],
        out_specs=[_full((2 * KV_W, D)), _full((8, D))],
        out_shape=[_sds((2 * KV_W, D), F32), _sds((8, D), F32)],
        compiler_params=_params(("arbitrary",)),
    )(ctx, modc, hc, dkc, dvc, w_kv)


def _k_in_bwd(dq, dk, dv, du, dvb, dga, dgb, x, dxp, w_in, modv, tm, comm=None):
    L = x.shape[0]
    parts = [(O_Q, Q_W), (O_K, KV_W), (O_V, KV_W), (O_U, GM_W), (O_VB, GM_W), (O_GA, D), (O_GB, D)]

    def body(dq_ref, dk_ref, dv_ref, du_ref, dvb_ref, dga_ref, dgb_ref, x_ref, dxp_ref, w_ref, mod_ref,
             dP_ref, gx_ref, acc_ref):
        @pl.when(pl.program_id(0) == 0)
        def _():
            acc_ref[...] = jnp.zeros_like(acc_ref)

        for (lo, width), r in zip(parts, (dq_ref, dk_ref, dv_ref, du_ref, dvb_ref, dga_ref, dgb_ref)):
            dP_ref[:, lo:lo + width] = r[...]
        n1, rstd1 = _ln(x_ref[...])
        dh = jnp.dot(dP_ref[...], w_ref[...], preferred_element_type=F32)
        acc_ref[0:1, :] += _colsum(dh)
        acc_ref[1:2, :] += _colsum(dh * n1)
        gx_ref[...] = dxp_ref[...] + _ln_bwd(dh * (1.0 + mod_ref[1:2, :]), n1, rstd1)

    return _call(
        body, name="bwd_in", grid=(L // tm,),
        in_specs=[_row(tm, w) for _, w in parts] + [_row(tm, D), _row(tm, D), _resident((IN_W, D)), _full((8, D))],
        out_specs=[_row(tm, IN_W), _row(tm, D), _full((8, D))],
        out_shape=[_sds((L, IN_W), BF16), _sds((L, D), F32), _sds((8, D), F32)],
        args=(dq, dk, dv, du, dvb, dga, dgb, x, dxp, w_in, modv), comm=comm)


def _wgrad(a, b, name, tk, tt, comm=None, extra=None):
    T, K = a.shape
    N = b.shape[1]
    nt = T // tt

    def body(*refs):
        a_ref, b_ref = refs[:2]
        o_ref, acc_ref = refs[-2:]
        j, t = pl.program_id(0), pl.program_id(1)

        @pl.when(t == 0)
        def _():
            acc_ref[...] = jnp.zeros_like(acc_ref)

        acc_ref[...] += lax.dot_general(a_ref[...], b_ref[...], TN, preferred_element_type=F32)

        if extra is not None:
            lo, rows = extra[0] % tk, extra[1].shape[0]

            @pl.when((t == nt - 1) & (j == extra[0] // tk))
            def _():
                acc_ref[lo:lo + rows, :] += refs[2][...]

        @pl.when(t == nt - 1)
        def _():
            o_ref[...] = acc_ref[...].astype(BF16)

    extra_specs = [] if extra is None else [pl.BlockSpec(extra[1].shape, lambda j, t: (0, 0))]
    (out,), got = _call(
        body, name=name, grid=(K // tk, nt),
        in_specs=[pl.BlockSpec((tt, tk), lambda j, t: (t, j)), pl.BlockSpec((tt, N), lambda j, t: (t, 0))] + extra_specs,
        out_specs=[pl.BlockSpec((tk, N), lambda j, t: (j, 0))],
        out_shape=[_sds((K, N), BF16)],
        scratch=[pltpu.VMEM((tk, N), F32)],
        args=(a, b) + (() if extra is None else (extra[1],)), comm=comm)
    return (out, got) if comm is not None else out


def _adamw_reduce(parts, w, m, v, name, tr):
    R, C = w.shape
    n_parts = parts.shape[0]

    def body(p_ref, w_ref, m_ref, v_ref, g_ref, d_ref, m2_ref, v2_ref):
        g = p_ref[0].astype(F32)
        for i in range(1, n_parts):
            g = g + p_ref[i].astype(F32)
        delta, m2, v2 = _adamw(w_ref[...], g, m_ref[...], v_ref[...])
        g_ref[...] = g
        d_ref[...] = delta
        m2_ref[...] = m2
        v2_ref[...] = v2

    spec = _row(tr, C)
    return pl.pallas_call(
        body, name=name, grid=(R // tr,),
        in_specs=[pl.BlockSpec((n_parts, tr, C), lambda i: (0, i, 0)), spec, spec, spec],
        out_specs=[spec] * 4,
        out_shape=[_sds((R, C), F32)] * 4,
        compiler_params=_params(("arbitrary",)),
    )(parts, w, m, v)


SMALL_ORDER = ("b_ada", "ln1_g", "ln1_b", "ln2_g", "ln2_b", "gmlp_ln_g", "gmlp_ln_b", "b_spatial", "attn_sink")


def _small_step(gath, params):
    flat = [a for name in SMALL_ORDER for a in params[name]]

    def grad_of(tot, name):
        if name == "b_ada":
            return jnp.concatenate([tot[r:r + 1, :] for r in range(6)], axis=1)
        if name in ("ln1_g", "ln1_b", "ln2_g", "ln2_b"):
            r = 8 + ("ln1_g", "ln1_b", "ln2_g", "ln2_b").index(name)
            return tot[r:r + 1, :]
        if name == "gmlp_ln_g":
            return tot[12:13, :GM_W]
        if name == "gmlp_ln_b":
            return tot[12:13, GM_W:]
        if name == "b_spatial":
            return jnp.concatenate([tot[13:14, g * BLK:(g + 1) * BLK] for g in range(N_GROUPS)], axis=0)[None]
        return tot[14:15, :N_Q_HEADS]

    def body(*refs):
        g_ref, in_refs = refs[0], refs[1:1 + len(flat)]
        tot_ref, out_refs = refs[1 + len(flat)], refs[2 + len(flat):]
        tot = g_ref[0]
        for i in range(1, N_DEV):
            tot = tot + g_ref[i]
        tot_ref[...] = tot
        tot_ref[0:2, :] = tot[0:2, :] + tot[6:8, :]
        tot_ref[15:16, :] = jnp.broadcast_to(jnp.sum(tot[15:16, :], axis=1, keepdims=True), (1, D))
        tot = tot_ref[...]
        for k, name in enumerate(SMALL_ORDER):
            w_ref, m_ref, v_ref = in_refs[3 * k:3 * k + 3]
            g = grad_of(tot, name)
            delta, m2, v2 = _adamw(w_ref[...], g, m_ref[...], v_ref[...])
            for r, val in zip(out_refs[4 * k:4 * k + 4], (g, delta, m2, v2)):
                r[...] = val

    res = pl.pallas_call(
        body, name="small_step", grid=(1,),
        in_specs=[_full((N_DEV, 16, D))] + [_full(a.shape) for a in flat],
        out_specs=[_full((16, D))] + [_full(params[name][0].shape) for name in SMALL_ORDER for _ in range(4)],
        out_shape=[_sds((16, D), F32)] + [_sds(params[name][0].shape, F32) for name in SMALL_ORDER for _ in range(4)],
        compiler_params=_params(("arbitrary",)),
    )(gath, *flat)
    return res[0], {name: res[1 + 4 * k:5 + 4 * k] for k, name in enumerate(SMALL_ORDER)}


def _cctx_finish(gath, c_ctx, m, v):
    def body(g_ref, c_ref, m_ref, v_ref, gr_ref, d_ref, m2_ref, v2_ref):
        ds = g_ref[0]
        for i in range(1, N_DEV):
            ds = ds + g_ref[i]
        c = c_ref[...]
        sg = _sigmoid(c)
        g = ds * (sg * (1.0 + c * (1.0 - sg)))
        delta, m2, v2 = _adamw(c, g, m_ref[...], v_ref[...])
        gr_ref[...] = g
        d_ref[...] = delta
        m2_ref[...] = m2
        v2_ref[...] = v2

    return pl.pallas_call(
        body, name="cctx_finish", grid=(1,),
        in_specs=[_full((N_DEV, 8, D))] + [_full((8, D))] * 3, out_specs=[_full((8, D))] * 4,
        out_shape=[_sds((8, D), F32)] * 4,
        compiler_params=_params(("arbitrary",)),
    )(gath, c_ctx, m, v)


def _pad_rows(a, rows):
    return jnp.concatenate([a, jnp.zeros((rows - a.shape[0], a.shape[1]), a.dtype)], axis=0)


def kernel(x, c, ctx, c_ctx, w_ada, b_ada, w_in, attn_sink, gmlp_ln_g, gmlp_ln_b, w_spatial, b_spatial, w_branch_a, w_branch_b, w_out, ln1_g, ln1_b, w_ffn_in, w_ffn_out, ln2_g, ln2_b, loss_target, m_c_ctx, m_w_ada, m_b_ada, m_w_in, m_attn_sink, m_gmlp_ln_g, m_gmlp_ln_b, m_w_spatial, m_b_spatial, m_w_branch_a, m_w_branch_b, m_w_out, m_ln1_g, m_ln1_b, m_w_ffn_in, m_w_ffn_out, m_ln2_g, m_ln2_b, v_c_ctx, v_w_ada, v_b_ada, v_w_in, v_attn_sink, v_gmlp_ln_g, v_gmlp_ln_b, v_w_spatial, v_b_spatial, v_w_branch_a, v_w_branch_b, v_w_out, v_ln1_g, v_ln1_b, v_w_ffn_in, v_w_ffn_out, v_ln2_g, v_ln2_b):
    L = x.shape[1]
    me = 4 * lax.axis_index("x") + 2 * lax.axis_index("y") + lax.axis_index("c")
    x2, tgt, ctx2 = x[0], loss_target[0], ctx[0]
    tm_in = min(512, L)
    tm = min(256, L)
    tt = min(1024, L)

    transposed = ("w_in", "w_ffn_in")
    tr = lambda kname, a: a.T if kname in transposed else a
    big = dict(w_in=w_in[0].T, w_branch_a=w_branch_a[0], w_branch_b=w_branch_b[0], w_out=w_out[0],
               w_ffn_in=w_ffn_in[0].T, w_ffn_out=w_ffn_out[0])
    col_sharded = ("w_branch_a", "w_branch_b")
    shard_bf = {k: a.astype(BF16) for k, a in big.items()}

    def assemble(kname, g):
        if kname in col_sharded:
            return g.transpose(1, 0, 2).reshape(g.shape[1], N_DEV * g.shape[2])
        return g.reshape(N_DEV * g.shape[1], g.shape[2])

    def to_blocks(kname, g):
        if kname in col_sharded:
            return g.reshape(g.shape[0], N_DEV, g.shape[1] // N_DEV).transpose(1, 0, 2)
        return g.reshape(N_DEV, g.shape[0] // N_DEV, g.shape[1])

    full = {}
    n_ada = w_ada.shape[2]
    b_my = lax.dynamic_slice(b_ada, (0, me * n_ada), (1, n_ada))
    act, mod_all, got = _prologue(_pad_rows(c, 8), _pad_rows(c_ctx[None, :], 8), w_ada[0], b_my,
                                  _Comm(gather=[shard_bf["w_in"]]))
    full["w_in"] = assemble("w_in", got[0])
    mod_all = mod_all.transpose(1, 0, 2).reshape(16, 6 * D)
    modv = _pad_rows(lax.dynamic_slice(mod_all, (me, 0), (1, 6 * D)).reshape(6, D), 8)
    modc = _pad_rows(mod_all[8].reshape(6, D), 8)

    lnv = _pad_rows(jnp.concatenate([ln1_g, ln1_b, ln2_g, ln2_b], axis=0), 8)
    gm_lnv = _pad_rows(jnp.concatenate([gmlp_ln_g, gmlp_ln_b], axis=0), 8)
    ws_b = w_spatial[0].astype(BF16)
    wst_b = ws_b.transpose(0, 2, 1)
    bsp = jnp.repeat(b_spatial[0].T, GROUP_DIM, axis=1)
    sink = attn_sink[0]
    cos, sin = _rope_tables(L)
    bias = _attn_bias()
    w_kv = full["w_in"][O_K:O_K + 2 * KV_W, :]

    (h, q, k, v, u, vb, ga, gb), got = _k_in(
        x2, modv, full["w_in"], cos, sin, tm_in,
        comm=_Comm(gather=[shard_bf[kname] for kname in ("w_branch_a", "w_branch_b", "w_out", "w_ffn_out")]))
    for kname, g in zip(("w_branch_a", "w_branch_b", "w_out", "w_ffn_out"), got):
        full[kname] = assemble(kname, g)
    hc, kc, vc = _k_ctx(ctx2, modc, w_kv)
    (ya, lse), got = _k_attn(sink, q, k, v, kc, vc, bias, comm=_Comm(gather=[shard_bf["w_ffn_in"]]))
    full["w_ffn_in"] = assemble("w_ffn_in", got[0])
    yb = _k_gmlp(u, vb, gm_lnv, ws_b, bsp)
    merged, mix, xm, h2 = _k_merge(x2, ya, yb, ga, gb, full["w_branch_a"], full["w_branch_b"], full["w_out"], modv, lnv, tm_in)
    gate, up, act_f, dr2, df, acc_f = _k_ffn(h2, xm, tgt, full["w_ffn_in"], full["w_ffn_out"], modv, lnv, tm_in)

    dF, dmix, dxp, acc_b = _k_ffn_bwd(df, gate, up, xm, dr2, x2, mix, full["w_ffn_in"], full["w_ffn_out"], modv, lnv, tm)
    blk_fi = to_blocks("w_ffn_in", _wgrad(dF, h2, "wgrad_ffn_in", 1408, tt))
    blk_fo = to_blocks("w_ffn_out", _wgrad(act_f, df, "wgrad_ffn_out", 1408, tt))
    (dA, dB, dga, dgb, dya, dyb), (rcv_fo,) = _k_merge_bwd(
        dmix, ya, yb, ga, gb, full["w_branch_a"], full["w_branch_b"], full["w_out"], tm_in, comm=_Comm(scatter=[blk_fo]))
    du, dvb, g_ws, g_bst, g_gln = _k_gmlp_bwd(u, vb, dyb, gm_lnv, ws_b, wst_b, bsp)
    (dq, dk_late, dv_late, dkc, dvc, g_sink), (gath_ws, rcv_fi) = _k_attn_bwd(
        sink, q, k, v, kc, vc, dya, lse, cos, sin, bias,
        comm=_Comm(gather=[g_ws.reshape(N_GROUPS * BLK, BLK)], scatter=[blk_fi]))
    dk, dv = dk_late[BLK:BLK + L], dv_late[BLK:BLK + L]
    blk_a = to_blocks("w_branch_a", _wgrad(ya, dA, "wgrad_a", Q_W, tt))
    blk_b = to_blocks("w_branch_b", _wgrad(yb, dB, "wgrad_b", GM_W, tt))
    blk_o = to_blocks("w_out", _wgrad(merged, dmix, "wgrad_out", D, tt))
    (dP, grad_x, acc_i), _ = _k_in_bwd(dq, dk, dv, du, dvb, dga, dgb, x2, dxp, full["w_in"], modv, tm_in)
    g_ctx, dmodc = _k_ctx_bwd(ctx2, modc, hc, dkc, dvc, w_kv)
    gw_in, (rcv_a, rcv_b, rcv_o) = _wgrad(dP, h, "wgrad_in", 1280, tt, comm=_Comm(scatter=[blk_a, blk_b, blk_o]),
                                          extra=(O_K, g_ctx))

    dmod_x = jnp.concatenate([acc_i[0:2], acc_b[4:5], acc_b[0:2], acc_f[2:3]], axis=0)
    small = jnp.concatenate([
        dmod_x, dmodc[0:2], acc_b[2:4], acc_f[0:2],
        jnp.concatenate([g_gln[0:1], g_gln[1:2]], axis=1), g_bst.T.reshape(1, D),
        _pad_rows(g_sink[:, 0:1], D).T, acc_f[3:4]], axis=0)
    rcv_in, gath = _exchange_two_level(to_blocks("w_in", gw_in), small, "exchange_last")
    received = dict(w_in=rcv_in, w_branch_a=rcv_a, w_branch_b=rcv_b, w_out=rcv_o, w_ffn_in=rcv_fi, w_ffn_out=rcv_fo)
    moments = dict(w_in=(m_w_in, v_w_in), w_branch_a=(m_w_branch_a, v_w_branch_a), w_branch_b=(m_w_branch_b, v_w_branch_b),
                   w_out=(m_w_out, v_w_out), w_ffn_in=(m_w_ffn_in, v_w_ffn_in), w_ffn_out=(m_w_ffn_out, v_w_ffn_out))
    names = list(big)
    res = {}
    for kname in names:
        mm, vv = moments[kname]
        R = big[kname].shape[0]
        res[kname] = [tr(kname, r) for r in _adamw_reduce(
            received[kname], big[kname], tr(kname, mm[0]), tr(kname, vv[0]), "adamw_" + kname, 256 if R % 256 == 0 else R // 2)]

    ws2d = lambda a: a.reshape(N_GROUPS * BLK, BLK)
    res_ws = [r.reshape(w_spatial.shape) for r in _adamw_reduce(
        gath_ws, ws2d(w_spatial), ws2d(m_w_spatial), ws2d(v_w_spatial), "adamw_w_spatial", 256)]
    tot, res_small = _small_step(gath, dict(
        b_ada=(b_ada, m_b_ada, v_b_ada), ln1_g=(ln1_g, m_ln1_g, v_ln1_g), ln1_b=(ln1_b, m_ln1_b, v_ln1_b),
        ln2_g=(ln2_g, m_ln2_g, v_ln2_g), ln2_b=(ln2_b, m_ln2_b, v_ln2_b),
        gmlp_ln_g=(gmlp_ln_g, m_gmlp_ln_g, v_gmlp_ln_g), gmlp_ln_b=(gmlp_ln_b, m_gmlp_ln_b, v_gmlp_ln_b),
        b_spatial=(b_spatial, m_b_spatial, v_b_spatial), attn_sink=(attn_sink, m_attn_sink, v_attn_sink)))
    loss = tot[15, 0]

    dmod_rows = jnp.concatenate([gath[:, 0:6, :].reshape(N_DEV, 6 * D),
                                 jnp.concatenate([tot[6:8].reshape(1, 2 * D), jnp.zeros((1, 4 * D), F32)], axis=1),
                                 jnp.zeros((7, 6 * D), F32)], axis=0)
    dmod_my = lax.dynamic_slice(dmod_rows, (0, me * n_ada), (16, n_ada))
    g_wada, d_wada, m2_wada, v2_wada, pc = _ada_bwd(act, dmod_my, w_ada[0], m_w_ada[0], v_w_ada[0])
    pc_all = _ag_small(pc, "gather_cctx")
    cc8 = lambda a: _pad_rows(a.reshape(1, D), 8)
    g_cc, d_cc, m2_cc, v2_cc = _cctx_finish(pc_all, cc8(c_ctx), cc8(m_c_ctx), cc8(v_c_ctx))

    order = ["c_ctx", "w_ada", "b_ada", "w_in", "attn_sink", "gmlp_ln_g", "gmlp_ln_b", "w_spatial", "b_spatial",
             "w_branch_a", "w_branch_b", "w_out", "ln1_g", "ln1_b", "w_ffn_in", "w_ffn_out", "ln2_g", "ln2_b"]
    grads, deltas, new_m, new_v = {}, {}, {}, {}
    grads["c_ctx"], deltas["c_ctx"], new_m["c_ctx"], new_v["c_ctx"] = g_cc[0], d_cc[0], m2_cc[0], v2_cc[0]
    grads["w_ada"], deltas["w_ada"], new_m["w_ada"], new_v["w_ada"] = g_wada[None], d_wada[None], m2_wada[None], v2_wada[None]
    for kname in names:
        g, d, m2, v2 = res[kname]
        grads[kname], deltas[kname], new_m[kname], new_v[kname] = g[None], d[None], m2[None], v2[None]
    grads["w_spatial"], deltas["w_spatial"], new_m["w_spatial"], new_v["w_spatial"] = res_ws
    for kname in SMALL_ORDER:
        grads[kname], deltas[kname], new_m[kname], new_v[kname] = res_small[kname]
    return (loss, grad_x[None], *[grads[n] for n in order], *[deltas[n] for n in order],
            *[new_m[n] for n in order], *[new_v[n] for n in order])
```

```python
import functools
import math

import jax
import jax.numpy as jnp
import numpy as np
from jax import lax
from jax.experimental import pallas as pl
from jax.experimental.pallas import tpu as pltpu

F32 = jnp.float32
BF16 = jnp.bfloat16
MESH = pl.DeviceIdType.MESH

N_DEV = 8
D = 1024
HEAD_DIM = 64
N_Q_HEADS = 8
N_KV_HEADS = 2
GQA_GROUP = 4
BLK = 128
Q_W = 512
KV_W = 128
GM_W = 512
N_GROUPS = 8
GROUP_DIM = 64
FFN_H = 2816
IN_W = 3840
O_Q, O_K, O_V, O_U, O_VB, O_GA, O_GB = 0, 512, 640, 768, 1280, 1792, 2816
LN_EPS = 1e-5
NEG_INF = -1e30
ALPHA = 2.0 ** 0.25
ROPE_BASE = 10000.0
ROPE_PAIRS = 16
Q_SCALE = HEAD_DIM ** -0.5
GELU_K0 = math.sqrt(2.0 / math.pi)
GELU_K1 = 0.044715

ADAM_LR = 0.001
ADAM_B1 = 0.9
ADAM_B2 = 0.999
ADAM_EPS = 1e-08
ADAM_WD = 0.01
ADAM_STEP = 10

VMEM_LIMIT = 56 * 1024 * 1024
NT = (((1,), (1,)), ((), ()))
TN = (((0,), (0,)), ((), ()))


def _params(sem=None):
    return pltpu.CompilerParams(dimension_semantics=sem, vmem_limit_bytes=VMEM_LIMIT)


def _row(tm, w):
    return pl.BlockSpec((tm, w), lambda i: (i, 0))


def _full(shape):
    nd = len(shape)
    return pl.BlockSpec(shape, lambda i: (0,) * nd)


def _resident(shape):
    nd = len(shape)
    return pl.BlockSpec(shape, lambda i: (0,) * nd, pipeline_mode=pl.Buffered(1))


def _sds(shape, dt):
    return jax.ShapeDtypeStruct(shape, dt)


def _ln(xf):
    mu = jnp.mean(xf, axis=-1, keepdims=True)
    xc = xf - mu
    var = jnp.mean(xc * xc, axis=-1, keepdims=True)
    rstd = lax.rsqrt(var + LN_EPS)
    return xc * rstd, rstd


def _ln_bwd(dn, n, rstd):
    m1 = jnp.mean(dn, axis=-1, keepdims=True)
    m2 = jnp.mean(dn * n, axis=-1, keepdims=True)
    return rstd * (dn - m1 - n * m2)


def _colsum(t):
    return jnp.sum(t, axis=0, keepdims=True)


def _sigmoid(x):
    return 0.5 * jnp.tanh(0.5 * x) + 0.5


def _gelu(x):
    t = jnp.tanh(GELU_K0 * (x + GELU_K1 * (x * x * x)))
    return x * (0.5 * (1.0 + t)), t


def _gelu_grad(x, t):
    return 0.5 * (1.0 + t) + 0.5 * x * (1.0 - t * t) * (GELU_K0 * (1.0 + 3.0 * GELU_K1 * x * x))


def _swap16(t):
    lane = lax.broadcasted_iota(jnp.int32, t.shape, 1)
    return jnp.where((lane & 16) == 0, pltpu.roll(t, 112, 1), pltpu.roll(t, 16, 1))


def _rope(t, cos, sin):
    return t * cos + _swap16(t) * sin


def _unrope(t, cos, sin):
    return t * cos - _swap16(t) * sin


def _adamw(w, g, m, v):
    m2 = ADAM_B1 * m + (1.0 - ADAM_B1) * g
    v2 = ADAM_B2 * v + (1.0 - ADAM_B2) * (g * g)
    m_hat = m2 / (1.0 - ADAM_B1 ** ADAM_STEP)
    v_hat = v2 / (1.0 - ADAM_B2 ** ADAM_STEP)
    delta = -ADAM_LR * (m_hat / (jnp.sqrt(v_hat) + ADAM_EPS) + ADAM_WD * w)
    return delta, m2, v2


def _rope_tables(L):
    inv = (np.float32(ROPE_BASE) ** (-np.arange(ROPE_PAIRS, dtype=np.float32) / np.float32(ROPE_PAIRS))).astype(np.float32)
    t = np.arange(L, dtype=np.int32)
    rows = (t // 64).astype(np.float32)[:, None] * inv
    cols = (t % 64).astype(np.float32)[:, None] * inv
    cr, sr, cc, sc = np.cos(rows), np.sin(rows), np.cos(cols), np.sin(cols)
    cos = np.concatenate([cr, cr, cc, cc], axis=1)
    sin = np.concatenate([-sr, sr, -sc, sc], axis=1)
    return jnp.asarray(np.tile(cos, (1, 2)), F32), jnp.asarray(np.tile(sin, (1, 2)), F32)


def _me():
    return lax.axis_index("x"), lax.axis_index("y"), lax.axis_index("c")


def _peer(mx, my, mc, k):
    return (mx ^ ((k >> 2) & 1), my ^ ((k >> 1) & 1), mc ^ (k & 1))


def _ag_small(x, name):
    R, C = x.shape

    def body(x_ref, out_ref, send_sems, recv_sems):
        mx, my, mc = _me()
        me = 4 * mx + 2 * my + mc
        out_ref[pl.ds(me, 1)] = x_ref[...][None]
        sends = []
        for k in range(1, N_DEV):
            cp = pltpu.make_async_remote_copy(
                src_ref=x_ref, dst_ref=out_ref.at[me], send_sem=send_sems.at[k - 1], recv_sem=recv_sems.at[k - 1],
                device_id=_peer(mx, my, mc, k), device_id_type=MESH)
            cp.start()
            sends.append(cp)
        for k in range(1, N_DEV):
            pltpu.make_async_remote_copy(
                src_ref=x_ref, dst_ref=out_ref.at[me ^ k], send_sem=send_sems.at[k - 1], recv_sem=recv_sems.at[k - 1],
                device_id=(mx, my, mc), device_id_type=MESH).wait_recv()
        for cp in sends:
            cp.wait_send()

    return pl.pallas_call(
        body, name=name,
        out_shape=_sds((N_DEV, R, C), x.dtype),
        in_specs=[pl.BlockSpec(memory_space=pltpu.VMEM)],
        out_specs=pl.BlockSpec(memory_space=pltpu.VMEM),
        scratch_shapes=[pltpu.SemaphoreType.DMA((N_DEV - 1,)), pltpu.SemaphoreType.DMA((N_DEV - 1,))],
        compiler_params=pltpu.CompilerParams(vmem_limit_bytes=VMEM_LIMIT),
    )(x)


class _Comm:
    def __init__(self, gather=(), scatter=(), spread=()):
        self.kinds = ["gather"] * len(gather) + ["scatter"] * len(scatter) + ["spread"] * len(spread)
        self.args = list(gather) + list(scatter) + list(spread)
        self.n = len(self.args)

    def out_shape(self):
        return [_sds(a.shape if k == "scatter" else (N_DEV,) + a.shape, a.dtype) for k, a in zip(self.kinds, self.args)]

    def specs(self):
        return [pl.BlockSpec(memory_space=pl.ANY)] * self.n

    def scratch(self):
        return [pltpu.SemaphoreType.DMA((7 * self.n,)), pltpu.SemaphoreType.DMA((7 * self.n,)),
                pltpu.SemaphoreType.DMA((self.n,))]

    def _plan(self, x_refs, out_refs, send_sems, recv_sems, local_sems):
        mx, my, mc = _me()
        me = 4 * mx + 2 * my + mc
        here, sibling = (mx, my, mc), (mx, my, 1 - mc)
        chips = [(1 - mx, my), (mx, 1 - my), (1 - mx, 1 - my)]
        local, first, last = [], [], []
        relay = [[], [], []]
        for a, kind in enumerate(self.kinds):
            x, out = x_refs[a], out_refs[a]

            def rc(k, src, dst, to):
                return pltpu.make_async_remote_copy(
                    src_ref=src, dst_ref=dst, send_sem=send_sems.at[7 * a + k], recv_sem=recv_sems.at[7 * a + k],
                    device_id=to, device_id_type=MESH)

            if kind == "gather":
                local.append(pltpu.make_async_copy(x, out.at[me], local_sems.at[a]))
                first.append(rc(0, x, out.at[me], sibling))
                last.append(rc(0, x, out.at[me ^ 1], here))
                for j, (cx, cy) in enumerate(chips):
                    first.append(rc(1 + j, x, out.at[me], (cx, cy, mc)))
                    landed = out.at[4 * cx + 2 * cy + mc]
                    relay[j].append((rc(1 + j, x, landed, here), rc(4 + j, landed, landed, sibling)))
                    last.append(rc(4 + j, x, out.at[4 * cx + 2 * cy + 1 - mc], here))
            else:
                own = x.at[me] if kind == "scatter" else x
                local.append(pltpu.make_async_copy(own, out.at[me], local_sems.at[a]))
                for k in range(1, N_DEV):
                    src = x.at[me ^ k] if kind == "scatter" else x
                    first.append(rc(k - 1, src, out.at[me], _peer(mx, my, mc, k)))
                    last.append(rc(k - 1, own, out.at[me ^ k], here))
        return local, first, relay[0] + relay[1] + relay[2], last

    def start(self, *refs):
        local, first, _, _ = self._plan(*refs)
        for cp in local + first:
            cp.start()

    def finish(self, *refs):
        local, first, relay, last = self._plan(*refs)
        for arrival, onward in relay:
            arrival.wait_recv()
            onward.start()
        for cp in last:
            cp.wait_recv()
        for cp in first:
            cp.wait_send()
        for _, onward in relay:
            onward.wait_send()
        for cp in local:
            cp.wait()


def _call(body, *, name, grid, in_specs, out_specs, out_shape, args, scratch=(), comm=None, aliases=None):
    params = _params(("arbitrary",) * len(grid))

    def at(end):
        conds = [pl.program_id(d) == (n - 1 if end else 0) for d, n in enumerate(grid)]
        return functools.reduce(lambda p, q: p & q, conds)

    if comm is None:
        res = pl.pallas_call(
            body, name=name, grid=grid, in_specs=list(in_specs), out_specs=list(out_specs), out_shape=list(out_shape),
            scratch_shapes=list(scratch), input_output_aliases=aliases or {}, compiler_params=params)(*args)
        return list(res), []
    n_in, n_out, n_scr, cn = len(in_specs), len(out_specs), len(scratch), comm.n

    def hosted(*refs):
        ins, refs = refs[:n_in], refs[n_in:]
        cins, refs = refs[:cn], refs[cn:]
        outs, refs = refs[:n_out], refs[n_out:]
        couts, refs = refs[:cn], refs[cn:]
        scr, sems = refs[:n_scr], refs[n_scr:]

        @pl.when(at(False))
        def _():
            comm.start(cins, couts, *sems)

        body(*ins, *outs, *scr)

        @pl.when(at(True))
        def _():
            comm.finish(cins, couts, *sems)

    res = pl.pallas_call(
        hosted, name=name, grid=grid, in_specs=list(in_specs) + comm.specs(), out_specs=list(out_specs) + comm.specs(),
        out_shape=list(out_shape) + comm.out_shape(), scratch_shapes=list(scratch) + comm.scratch(),
        input_output_aliases=aliases or {}, compiler_params=params)(*args, *comm.args)
    return list(res[:n_out]), list(res[n_out:])


def _exchange_two_level(blk, small, name):
    _, R, C = blk.shape
    rows = small.shape[0]

    def body(blk_ref, small_ref, stage_ref, out_ref, gath_ref, a_scr, b_scr, t_scr, s1, r1, s3, r3, ss, rs, lsem):
        mx, my, mc = _me()
        me = 4 * mx + 2 * my + mc
        mine = 2 * mx + my
        here, sibling = (mx, my, mc), (mx, my, 1 - mc)

        def rc(src, dst, send, recv, to):
            return pltpu.make_async_remote_copy(src_ref=src, dst_ref=dst, send_sem=send, recv_sem=recv,
                                                device_id=to, device_id_type=MESH)

        own_small = pltpu.make_async_copy(small_ref, gath_ref.at[me], lsem.at[0])
        own_small.start()
        spread = [rc(small_ref, gath_ref.at[me], ss.at[k - 1], rs.at[k - 1], _peer(mx, my, mc, k)) for k in range(1, N_DEV)]
        to_sib = [rc(blk_ref.at[2 * p + 1 - mc], stage_ref.at[p], s1.at[p], r1.at[p], sibling) for p in range(4)]
        for cp in spread + to_sib:
            cp.start()
        own = [pltpu.make_async_copy(blk_ref.at[2 * p + mc], a_scr.at[p], lsem.at[1 + p]) for p in range(4)]
        for cp in own:
            cp.start()
        from_sib = []
        for p in range(4):
            rc(blk_ref.at[2 * p + 1 - mc], stage_ref.at[p], s1.at[p], r1.at[p], here).wait_recv()
            cp = pltpu.make_async_copy(stage_ref.at[p], b_scr.at[p], lsem.at[5 + p])
            cp.start()
            from_sib.append(cp)
        for cp in own + from_sib:
            cp.wait()
        t_scr[...] = (a_scr[...].astype(F32) + b_scr[...].astype(F32)).astype(BF16)
        keep = pltpu.make_async_copy(t_scr.at[mine], out_ref.at[mine], lsem.at[9])
        keep.start()
        onward = [rc(t_scr.at[mine ^ k], out_ref.at[mine], s3.at[k - 1], r3.at[k - 1], (mx ^ (k >> 1), my ^ (k & 1), mc))
                  for k in range(1, 4)]
        for cp in onward:
            cp.start()
        for k in range(1, 4):
            rc(t_scr.at[mine], out_ref.at[mine ^ k], s3.at[k - 1], r3.at[k - 1], here).wait_recv()
        for k in range(1, N_DEV):
            rc(small_ref, gath_ref.at[me ^ k], ss.at[k - 1], rs.at[k - 1], here).wait_recv()
        for cp in spread + to_sib + onward:
            cp.wait_send()
        keep.wait()
        own_small.wait()

    any_spec = pl.BlockSpec(memory_space=pl.ANY)
    dma = pltpu.SemaphoreType.DMA
    _, out, gath = pl.pallas_call(
        body, name=name,
        in_specs=[any_spec, any_spec], out_specs=[any_spec] * 3,
        out_shape=[_sds((4, R, C), BF16), _sds((4, R, C), BF16), _sds((N_DEV, rows, D), F32)],
        scratch_shapes=[pltpu.VMEM((4, R, C), BF16)] * 3
                       + [dma((4,)), dma((4,)), dma((3,)), dma((3,)), dma((N_DEV - 1,)), dma((N_DEV - 1,)), dma((10,))],
        compiler_params=pltpu.CompilerParams(vmem_limit_bytes=VMEM_LIMIT),
    )(blk, small)
    return out, gath


def _comm_only(comm, name):
    return _call(lambda: None, name=name, grid=(1,), in_specs=[], out_specs=[], out_shape=[], args=[], comm=comm)[1]


def _exchange_rows(x_ref, out_ref, send_sems, recv_sems):
    mx, my, mc = _me()
    me = 4 * mx + 2 * my + mc
    out_ref[pl.ds(me, 1)] = x_ref[...][None]
    sends = []
    for k in range(1, N_DEV):
        cp = pltpu.make_async_remote_copy(
            src_ref=x_ref, dst_ref=out_ref.at[me], send_sem=send_sems.at[k - 1], recv_sem=recv_sems.at[k - 1],
            device_id=_peer(mx, my, mc, k), device_id_type=MESH)
        cp.start()
        sends.append(cp)
    for k in range(1, N_DEV):
        pltpu.make_async_remote_copy(
            src_ref=x_ref, dst_ref=out_ref.at[me ^ k], send_sem=send_sems.at[k - 1], recv_sem=recv_sems.at[k - 1],
            device_id=(mx, my, mc), device_id_type=MESH).wait_recv()
    for cp in sends:
        cp.wait_send()


def _prologue(c8, cctx8, w_ada, b_my, comm):
    nw = w_ada.shape[1]

    def body(c_ref, cctx_ref, w_ref, b_ref, act_ref, mod_ref, cmine_scr, call_scr, mine_scr, mall_scr, s1, r1, s2, r2):
        cmine_scr[...] = c_ref[...]
        _exchange_rows(cmine_scr, call_scr, s1, r1)
        rows = [call_scr[d][0:1, :] for d in range(N_DEV)] + [cctx_ref[0:1, :], jnp.zeros((7, D), F32)]
        s = jnp.concatenate(rows, axis=0)
        act = s * _sigmoid(s)
        act_ref[...] = act
        mine_scr[...] = jnp.dot(act.astype(BF16), w_ref[...].astype(BF16), preferred_element_type=F32) + b_ref[...]
        _exchange_rows(mine_scr, mall_scr, s2, r2)
        mod_ref[...] = mall_scr[...]

    sems = [pltpu.SemaphoreType.DMA((N_DEV - 1,))] * 4
    (act, mod), got = _call(
        body, name="prologue", grid=(1,),
        in_specs=[_full((8, D)), _full((8, D)), _full((D, nw)), _full((1, nw))],
        out_specs=[_full((16, D)), _full((N_DEV, 16, nw))],
        out_shape=[_sds((16, D), F32), _sds((N_DEV, 16, nw), F32)],
        scratch=[pltpu.VMEM((8, D), F32), pltpu.VMEM((N_DEV, 8, D), F32), pltpu.VMEM((16, nw), F32),
                 pltpu.VMEM((N_DEV, 16, nw), F32)] + sems,
        args=(c8, cctx8, w_ada, b_my), comm=comm)
    return act, mod, got


def _ada_bwd(act, dmod_my, w_ada, m, v, tr=256):
    nw = w_ada.shape[1]

    def body(act_ref, dm_ref, w_ref, m_ref, v_ref, g_ref, d_ref, m2_ref, v2_ref, pc_ref):
        dm = dm_ref[...].astype(BF16)
        g = lax.dot_general(act_ref[...].astype(BF16), dm, TN, preferred_element_type=F32)
        w = w_ref[...]
        delta, m2, v2 = _adamw(w, g, m_ref[...], v_ref[...])
        g_ref[...] = g
        d_ref[...] = delta
        m2_ref[...] = m2
        v2_ref[...] = v2
        pc_ref[...] = lax.dot_general(dm[8:16, :], w.astype(BF16), NT, preferred_element_type=F32)

    wspec = _row(tr, nw)
    return pl.pallas_call(
        body, name="ada_bwd", grid=(D // tr,),
        in_specs=[pl.BlockSpec((16, tr), lambda i: (0, i)), _full((16, nw)), wspec, wspec, wspec],
        out_specs=[wspec, wspec, wspec, wspec, pl.BlockSpec((8, tr), lambda i: (0, i))],
        out_shape=[_sds((D, nw), F32)] * 4 + [_sds((8, D), F32)],
        compiler_params=_params(("arbitrary",)),
    )(act, dmod_my, w_ada, m, v)


def _k_in(x, modv, w_in, cos, sin, tm, comm=None):
    L = x.shape[0]

    def body(x_ref, mod_ref, w_ref, cos_ref, sin_ref, h_ref, q_ref, k_ref, v_ref, u_ref, vb_ref, ga_ref, gb_ref):
        n, _ = _ln(x_ref[...])
        h = (n * (1.0 + mod_ref[1:2, :]) + mod_ref[0:1, :]).astype(BF16)
        h_ref[...] = h
        c, s = cos_ref[...], sin_ref[...]

        def proj(lo, width):
            return lax.dot_general(h, w_ref[lo:lo + width, :], NT, preferred_element_type=F32)

        for i in range(4):
            q_ref[:, i * 128:(i + 1) * 128] = (_rope(proj(O_Q + i * 128, 128), c, s) * Q_SCALE).astype(BF16)
        k_ref[...] = _rope(proj(O_K, KV_W), c, s).astype(BF16)
        v_ref[...] = proj(O_V, KV_W).astype(BF16)
        u_ref[...] = proj(O_U, GM_W).astype(BF16)
        vb_ref[...] = proj(O_VB, GM_W).astype(BF16)
        ga_ref[...] = proj(O_GA, D).astype(BF16)
        gb_ref[...] = proj(O_GB, D).astype(BF16)

    widths = [D, Q_W, KV_W, KV_W, GM_W, GM_W, D, D]
    return _call(
        body, name="fwd_in", grid=(L // tm,),
        in_specs=[_row(tm, D), _full((8, D)), _resident((IN_W, D)), _row(tm, 128), _row(tm, 128)],
        out_specs=[_row(tm, w) for w in widths],
        out_shape=[_sds((L, w), BF16) for w in widths],
        args=(x, modv, w_in, cos, sin), comm=comm)


def _k_ctx(ctx, modc, w_kv):
    C = ctx.shape[0]

    def body(c_ref, mod_ref, w_ref, hc_ref, kc_ref, vc_ref):
        n, _ = _ln(c_ref[...])
        hc = (n * (1.0 + mod_ref[1:2, :]) + mod_ref[0:1, :]).astype(BF16)
        hc_ref[...] = hc
        kv = lax.dot_general(hc, w_ref[...], NT, preferred_element_type=F32)
        kc_ref[...] = kv[:, :KV_W].astype(BF16)
        vc_ref[...] = kv[:, KV_W:].astype(BF16)

    return pl.pallas_call(
        body, name="fwd_ctx", grid=(1,),
        in_specs=[_full((C, D)), _full((8, D)), _full((2 * KV_W, D))],
        out_specs=[_full((C, D)), _full((C, KV_W)), _full((C, KV_W))],
        out_shape=[_sds((C, D), BF16), _sds((C, KV_W), BF16), _sds((C, KV_W), BF16)],
        compiler_params=_params(("arbitrary",)),
    )(ctx, modc, w_kv)


def _attn_bias():
    r = (np.arange(GQA_GROUP * BLK) & (BLK - 1))[:, None]
    j = np.arange(3 * BLK)[None, :]
    band = np.abs(j - BLK - r) <= BLK
    variants = [band & (j >= BLK), band, band & (j < 2 * BLK)]
    return jnp.asarray(np.stack([np.where(v, 0.0, NEG_INF) for v in variants]), F32)


def _masked(s, bias, C):
    return jnp.concatenate([s[:, :C], s[:, C:] + bias], axis=1)


def _sink_col(sink_ref, hk):
    grp = lax.broadcasted_iota(jnp.int32, (GQA_GROUP * BLK, 1), 0) >> 7
    col = jnp.full((GQA_GROUP * BLK, 1), sink_ref[hk * GQA_GROUP], F32)
    for g in range(1, GQA_GROUP):
        col = jnp.where(grp == g, sink_ref[hk * GQA_GROUP + g], col)
    return col


def _k_attn(sink, q, k, v, kc, vc, bias, comm=None):
    L = q.shape[0]
    C = kc.shape[0]
    nb = L // BLK
    steps = nb // 2

    def body(sink_ref, q_ref, kp_ref, km_ref, kx_ref, vp_ref, vm_ref, vx_ref, kc_ref, vc_ref, bias_ref, ya_ref, lse_ref):
        i = pl.program_id(0)
        bands = [bias_ref[jnp.where(i == 0, 0, 1)], bias_ref[jnp.where(i == steps - 1, 2, 1)]]
        chains = [(qb, hk) for qb in range(2) for hk in range(N_KV_HEADS)]

        def keys(ctx_ref, p_ref, m_ref, x_ref, qb, hk):
            sl = slice(hk * HEAD_DIM, (hk + 1) * HEAD_DIM)
            band3 = ([p_ref[:, sl], m_ref[0:BLK, sl], m_ref[BLK:2 * BLK, sl]] if qb == 0
                     else [m_ref[0:BLK, sl], m_ref[BLK:2 * BLK, sl], x_ref[:, sl]])
            return jnp.concatenate([ctx_ref[:, sl]] + band3, axis=0)

        def queries(qb, hk):
            return jnp.concatenate(
                [q_ref[qb * BLK:(qb + 1) * BLK, (hk * GQA_GROUP + g) * HEAD_DIM:(hk * GQA_GROUP + g + 1) * HEAD_DIM]
                 for g in range(GQA_GROUP)], axis=0)

        s = [_masked(lax.dot_general(queries(qb, hk), keys(kc_ref, kp_ref, km_ref, kx_ref, qb, hk), NT,
                                     preferred_element_type=F32), bands[qb], C) for qb, hk in chains]
        for (qb, hk), s_ in zip(chains, s):
            sink_c = _sink_col(sink_ref, hk)
            m = jnp.maximum(jnp.max(s_, axis=1, keepdims=True), sink_c)
            p = jnp.exp(s_ - m)
            den = jnp.sum(p, axis=1, keepdims=True) + jnp.exp(sink_c - m)
            o = jnp.dot(p.astype(BF16), keys(vc_ref, vp_ref, vm_ref, vx_ref, qb, hk), preferred_element_type=F32) * (1.0 / den)
            lse = m + jnp.log(den)
            rows = slice(qb * BLK, (qb + 1) * BLK)
            for g in range(GQA_GROUP):
                h = hk * GQA_GROUP + g
                ya_ref[rows, h * HEAD_DIM:(h + 1) * HEAD_DIM] = o[g * BLK:(g + 1) * BLK, :].astype(BF16)
                lse_ref[rows, h:h + 1] = lse[g * BLK:(g + 1) * BLK, :]

    kv3 = [pl.BlockSpec((BLK, KV_W), lambda i: (jnp.maximum(2 * i - 1, 0), 0)),
           pl.BlockSpec((2 * BLK, KV_W), lambda i: (i, 0)),
           pl.BlockSpec((BLK, KV_W), lambda i: (jnp.minimum(2 * i + 2, nb - 1), 0))]
    return _call(
        body, name="fwd_attn", grid=(steps,),
        in_specs=[pl.BlockSpec(memory_space=pltpu.SMEM), _row(2 * BLK, Q_W)] + kv3 + kv3
                 + [_full((C, KV_W)), _full((C, KV_W)), _full((3, GQA_GROUP * BLK, 3 * BLK))],
        out_specs=[_row(2 * BLK, Q_W), _row(2 * BLK, N_Q_HEADS)],
        out_shape=[_sds((L, Q_W), BF16), _sds((L, N_Q_HEADS), F32)],
        args=(sink, q, k, k, k, v, v, v, kc, vc, bias), comm=comm)


GMLP_CHUNKS = 4


def _split_pair(t):
    low = lax.broadcasted_iota(jnp.int32, t.shape, 1) < GROUP_DIM
    zero = jnp.zeros_like(t)
    return jnp.where(low, t, zero), jnp.where(low, zero, t)


def _gmlp_spatial(w_ref, t_b, nch):
    rows = []
    for c in range(nch):
        tiles = []
        for pr in range(N_GROUPS // 2):
            lo, hi = _split_pair(t_b[c * BLK:(c + 1) * BLK, pr * 128:(pr + 1) * 128])
            tiles.append(jnp.dot(w_ref[2 * pr], lo, preferred_element_type=F32)
                         + jnp.dot(w_ref[2 * pr + 1], hi, preferred_element_type=F32))
        rows.append(jnp.concatenate(tiles, axis=1))
    return jnp.concatenate(rows, axis=0)


def _gmlp_fwd_vals(u, vb, lnv_ref, ws_ref, bsp_ref, nch):
    uf = u.astype(F32)
    vf = vb.astype(F32)
    gu, tu = _gelu(uf)
    gv, tv = _gelu(vf)
    vhat, rstd = _ln(gv)
    vn = (vhat * lnv_ref[0:1, :] + lnv_ref[1:2, :]).astype(BF16)
    s = _gmlp_spatial(ws_ref, vn, nch) + jnp.concatenate([bsp_ref[...]] * nch, axis=0)
    return uf, vf, gu, tu, tv, vhat, rstd, vn, s


def _k_gmlp(u, vb, lnv, ws, bsp):
    L = u.shape[0]
    nch = min(GMLP_CHUNKS, L // BLK)
    tm = nch * BLK

    def body(u_ref, vb_ref, lnv_ref, ws_ref, bsp_ref, yb_ref):
        _, _, gu, _, _, _, _, _, s = _gmlp_fwd_vals(u_ref[...], vb_ref[...], lnv_ref, ws_ref, bsp_ref, nch)
        yb_ref[...] = (gu * s).astype(BF16)

    return pl.pallas_call(
        body, name="fwd_gmlp", grid=(L // tm,),
        in_specs=[_row(tm, GM_W), _row(tm, GM_W), _full((8, GM_W)), _full((N_GROUPS, BLK, BLK)), _full((BLK, GM_W))],
        out_specs=_row(tm, GM_W),
        out_shape=_sds((L, GM_W), BF16),
        compiler_params=_params(("arbitrary",)),
    )(u, vb, lnv, ws, bsp)


def _k_merge(x, ya, yb, ga, gb, w_a, w_b, w_o, modv, lnv, tm):
    L = x.shape[0]

    def body(x_ref, ya_ref, yb_ref, ga_ref, gb_ref, wa_ref, wb_ref, wo_ref, mod_ref, ln_ref,
             mg_ref, mix_ref, xm_ref, h2_ref):
        a = jnp.dot(ya_ref[...], wa_ref[...], preferred_element_type=F32)
        b = jnp.dot(yb_ref[...], wb_ref[...], preferred_element_type=F32)
        merged = (_sigmoid(ga_ref[...].astype(F32)) * a + _sigmoid(gb_ref[...].astype(F32)) * b).astype(BF16)
        mg_ref[...] = merged
        mix = jnp.dot(merged, wo_ref[...], preferred_element_type=F32)
        mix_ref[...] = mix.astype(BF16)
        r1 = ALPHA * x_ref[...] + mod_ref[2:3, :] * mix
        r1hat, _ = _ln(r1)
        xm = r1hat * ln_ref[0:1, :] + ln_ref[1:2, :]
        xm_ref[...] = xm
        n2, _ = _ln(xm)
        h2_ref[...] = (n2 * (1.0 + mod_ref[4:5, :]) + mod_ref[3:4, :]).astype(BF16)

    return pl.pallas_call(
        body, name="fwd_merge", grid=(L // tm,),
        in_specs=[_row(tm, D), _row(tm, Q_W), _row(tm, GM_W), _row(tm, D), _row(tm, D),
                  _resident((Q_W, D)), _resident((GM_W, D)), _resident((D, D)), _full((8, D)), _full((8, D))],
        out_specs=[_row(tm, D)] * 4,
        out_shape=[_sds((L, D), BF16), _sds((L, D), BF16), _sds((L, D), F32), _sds((L, D), BF16)],
        compiler_params=_params(("arbitrary",)),
    )(x, ya, yb, ga, gb, w_a, w_b, w_o, modv, lnv)


FFN_CH = 1408


def _k_ffn(h2, xm, tgt, w_fi, w_fo, modv, lnv, tm):
    L = h2.shape[0]

    def body(h2_ref, xm_ref, t_ref, wi_ref, wo_ref, mod_ref, ln_ref, gate_ref, up_ref, a_ref, dr2_ref, df_ref, acc_ref):
        @pl.when(pl.program_id(0) == 0)
        def _():
            acc_ref[...] = jnp.zeros_like(acc_ref)

        h2v = h2_ref[...]
        f = jnp.zeros((tm, D), F32)
        for j in range(FFN_H // FFN_CH):
            lo = j * FFN_CH
            gate = lax.dot_general(h2v, wi_ref[lo:lo + FFN_CH, :], NT, preferred_element_type=F32)
            up = lax.dot_general(h2v, wi_ref[FFN_H + lo:FFN_H + lo + FFN_CH, :], NT, preferred_element_type=F32)
            act = (gate * _sigmoid(gate) * up).astype(BF16)
            gate_ref[:, lo:lo + FFN_CH] = gate.astype(BF16)
            up_ref[:, lo:lo + FFN_CH] = up.astype(BF16)
            a_ref[:, lo:lo + FFN_CH] = act
            f = f + jnp.dot(act, wo_ref[lo:lo + FFN_CH, :], preferred_element_type=F32)
        gate2 = mod_ref[5:6, :]
        r2 = ALPHA * xm_ref[...] + gate2 * f
        r2hat, rstd = _ln(r2)
        y = r2hat * ln_ref[2:3, :] + ln_ref[3:4, :]
        err = y - t_ref[...]
        dy = err * (1.0 / D)
        dr2 = _ln_bwd(dy * ln_ref[2:3, :], r2hat, rstd)
        dr2_ref[...] = dr2
        df_ref[...] = (gate2 * dr2).astype(BF16)
        acc_ref[0:1, :] += _colsum(dy * r2hat)
        acc_ref[1:2, :] += _colsum(dy)
        acc_ref[2:3, :] += _colsum(dr2 * f)
        acc_ref[3:4, :] += _colsum(err * err) * (0.5 / D)

    return pl.pallas_call(
        body, name="fwd_ffn", grid=(L // tm,),
        in_specs=[_row(tm, D), _row(tm, D), _row(tm, D), _resident((2 * FFN_H, D)), _resident((FFN_H, D)),
                  _full((8, D)), _full((8, D))],
        out_specs=[_row(tm, FFN_H)] * 3 + [_row(tm, D), _row(tm, D), _full((8, D))],
        out_shape=[_sds((L, FFN_H), BF16)] * 3 + [_sds((L, D), F32), _sds((L, D), BF16), _sds((8, D), F32)],
        compiler_params=_params(("arbitrary",)),
    )(h2, xm, tgt, w_fi, w_fo, modv, lnv)


def _k_ffn_bwd(df, gate, up, xm, dr2, x, mix, w_fi, w_fo, modv, lnv, tm):
    L = df.shape[0]

    def body(df_ref, gate_ref, up_ref, xm_ref, dr2_ref, x_ref, mix_ref, wi_ref, wo_ref, mod_ref, ln_ref,
             dF_ref, dmix_ref, dxp_ref, acc_ref):
        @pl.when(pl.program_id(0) == 0)
        def _():
            acc_ref[...] = jnp.zeros_like(acc_ref)

        dfv = df_ref[...]
        chunks = [j * FFN_CH for j in range(FFN_H // FFN_CH)]
        das = [lax.dot_general(dfv, wo_ref[lo:lo + FFN_CH, :], NT, preferred_element_type=F32) for lo in chunks]
        n2, rstd2 = _ln(xm_ref[...])
        mixf = mix_ref[...].astype(F32)
        gate1 = mod_ref[2:3, :]
        r1hat, rstd1 = _ln(ALPHA * x_ref[...] + gate1 * mixf)
        dh2 = jnp.zeros((tm, D), F32)
        for lo, da in zip(chunks, das):
            gate = gate_ref[:, lo:lo + FFN_CH].astype(F32)
            upv = up_ref[:, lo:lo + FFN_CH].astype(F32)
            sg = _sigmoid(gate)
            d_gate = (da * upv * (sg * (1.0 + gate * (1.0 - sg)))).astype(BF16)
            d_up = (da * (gate * sg)).astype(BF16)
            dF_ref[:, lo:lo + FFN_CH] = d_gate
            dF_ref[:, FFN_H + lo:FFN_H + lo + FFN_CH] = d_up
            dh2 = dh2 + jnp.dot(d_gate, wi_ref[lo:lo + FFN_CH, :], preferred_element_type=F32)
            dh2 = dh2 + jnp.dot(d_up, wi_ref[FFN_H + lo:FFN_H + lo + FFN_CH, :], preferred_element_type=F32)
        acc_ref[0:1, :] += _colsum(dh2)
        acc_ref[1:2, :] += _colsum(dh2 * n2)
        dxm = ALPHA * dr2_ref[...] + _ln_bwd(dh2 * (1.0 + mod_ref[4:5, :]), n2, rstd2)
        acc_ref[2:3, :] += _colsum(dxm * r1hat)
        acc_ref[3:4, :] += _colsum(dxm)
        dr1 = _ln_bwd(dxm * ln_ref[0:1, :], r1hat, rstd1)
        dmix_ref[...] = (gate1 * dr1).astype(BF16)
        dxp_ref[...] = ALPHA * dr1
        acc_ref[4:5, :] += _colsum(dr1 * mixf)

    return pl.pallas_call(
        body, name="bwd_ffn", grid=(L // tm,),
        in_specs=[_row(tm, D), _row(tm, FFN_H), _row(tm, FFN_H), _row(tm, D), _row(tm, D), _row(tm, D), _row(tm, D),
                  _resident((2 * FFN_H, D)), _resident((FFN_H, D)), _full((8, D)), _full((8, D))],
        out_specs=[_row(tm, 2 * FFN_H), _row(tm, D), _row(tm, D), _full((8, D))],
        out_shape=[_sds((L, 2 * FFN_H), BF16), _sds((L, D), BF16), _sds((L, D), F32), _sds((8, D), F32)],
        compiler_params=_params(("arbitrary",)),
    )(df, gate, up, xm, dr2, x, mix, w_fi, w_fo, modv, lnv)


def _k_merge_bwd(dmix, ya, yb, ga, gb, w_a, w_b, w_o, tm, comm=None):
    L = dmix.shape[0]

    def body(dmix_ref, ya_ref, yb_ref, ga_ref, gb_ref, wa_ref, wb_ref, wo_ref,
             dA_ref, dB_ref, dga_ref, dgb_ref, dya_ref, dyb_ref):
        dmg = lax.dot_general(dmix_ref[...], wo_ref[...], NT, preferred_element_type=F32)
        a = jnp.dot(ya_ref[...], wa_ref[...], preferred_element_type=F32)
        sa = _sigmoid(ga_ref[...].astype(F32))
        dA = (dmg * sa).astype(BF16)
        dA_ref[...] = dA
        dga_ref[...] = (dmg * a * (sa * (1.0 - sa))).astype(BF16)
        dya_ref[...] = lax.dot_general(dA, wa_ref[...], NT, preferred_element_type=F32).astype(BF16)
        b = jnp.dot(yb_ref[...], wb_ref[...], preferred_element_type=F32)
        sb = _sigmoid(gb_ref[...].astype(F32))
        dB = (dmg * sb).astype(BF16)
        dB_ref[...] = dB
        dgb_ref[...] = (dmg * b * (sb * (1.0 - sb))).astype(BF16)
        dyb_ref[...] = lax.dot_general(dB, wb_ref[...], NT, preferred_element_type=F32).astype(BF16)

    return _call(
        body, name="bwd_merge", grid=(L // tm,),
        in_specs=[_row(tm, D), _row(tm, Q_W), _row(tm, GM_W), _row(tm, D), _row(tm, D),
                  _resident((Q_W, D)), _resident((GM_W, D)), _resident((D, D))],
        out_specs=[_row(tm, D)] * 4 + [_row(tm, Q_W), _row(tm, GM_W)],
        out_shape=[_sds((L, D), BF16)] * 4 + [_sds((L, Q_W), BF16), _sds((L, GM_W), BF16)],
        args=(dmix, ya, yb, ga, gb, w_a, w_b, w_o), comm=comm)


def _k_gmlp_bwd(u, vb, dyb, lnv, ws, wst, bsp):
    L = u.shape[0]
    nch = min(GMLP_CHUNKS, L // BLK)
    tm = nch * BLK

    def body(u_ref, vb_ref, dyb_ref, lnv_ref, ws_ref, wst_ref, bsp_ref, du_ref, dvb_ref, gws_ref, gbst_ref, gln_ref):
        @pl.when(pl.program_id(0) == 0)
        def _():
            gws_ref[...] = jnp.zeros_like(gws_ref)
            gbst_ref[...] = jnp.zeros_like(gbst_ref)
            gln_ref[...] = jnp.zeros_like(gln_ref)

        uf, vf, gu, tu, tv, vhat, rstd, vn, s = _gmlp_fwd_vals(u_ref[...], vb_ref[...], lnv_ref, ws_ref, bsp_ref, nch)
        dyb_f = dyb_ref[...].astype(F32)
        du_ref[...] = (dyb_f * s * _gelu_grad(uf, tu)).astype(BF16)
        ds = dyb_f * gu
        ds_b = ds.astype(BF16)
        for pr in range(N_GROUPS // 2):
            lanes = slice(pr * 128, (pr + 1) * 128)
            gw_lo = gw_hi = ds_sum = None
            for c in range(nch):
                rows = slice(c * BLK, (c + 1) * BLK)
                lo, hi = _split_pair(ds_b[rows, lanes])
                t_lo = lax.dot_general(lo, vn[rows, lanes], NT, preferred_element_type=F32)
                t_hi = lax.dot_general(hi, vn[rows, lanes], NT, preferred_element_type=F32)
                gw_lo = t_lo if c == 0 else gw_lo + t_lo
                gw_hi = t_hi if c == 0 else gw_hi + t_hi
                ds_sum = ds[rows, lanes] if c == 0 else ds_sum + ds[rows, lanes]
            gws_ref[2 * pr] += gw_lo
            gws_ref[2 * pr + 1] += gw_hi
            b_lo, b_hi = _split_pair(ds_sum)
            gbst_ref[:, 2 * pr:2 * pr + 1] += jnp.sum(b_lo, axis=1, keepdims=True)
            gbst_ref[:, 2 * pr + 1:2 * pr + 2] += jnp.sum(b_hi, axis=1, keepdims=True)
        dvn = _gmlp_spatial(wst_ref, ds_b, nch)
        gln_ref[0:1, :] += _colsum(dvn * vhat)
        gln_ref[1:2, :] += _colsum(dvn)
        dgv = _ln_bwd(dvn * lnv_ref[0:1, :], vhat, rstd)
        dvb_ref[...] = (dgv * _gelu_grad(vf, tv)).astype(BF16)

    return pl.pallas_call(
        body, name="bwd_gmlp", grid=(L // tm,),
        in_specs=[_row(tm, GM_W)] * 3 + [_full((8, GM_W)), _full((N_GROUPS, BLK, BLK)), _full((N_GROUPS, BLK, BLK)),
                                         _full((BLK, GM_W))],
        out_specs=[_row(tm, GM_W), _row(tm, GM_W), _full((N_GROUPS, BLK, BLK)), _full((BLK, N_GROUPS)), _full((8, GM_W))],
        out_shape=[_sds((L, GM_W), BF16), _sds((L, GM_W), BF16), _sds((N_GROUPS, BLK, BLK), F32),
                   _sds((BLK, N_GROUPS), F32), _sds((8, GM_W), F32)],
        compiler_params=_params(("arbitrary",)),
    )(u, vb, dyb, lnv, ws, wst, bsp)


def _k_attn_bwd(sink, q, k, v, kc, vc, dya, lse, cos, sin, bias, comm=None):
    L = q.shape[0]
    C = kc.shape[0]
    nb = L // BLK
    steps = nb // 2
    NK = C + 3 * BLK
    chains = [(qb, hk) for qb in range(2) for hk in range(N_KV_HEADS)]

    def body(sink_ref, q_ref, kp_ref, km_ref, kx_ref, vp_ref, vm_ref, vx_ref, kc_ref, vc_ref, do_ref, lse_ref,
             cq_ref, sq_ref, cl_ref, sl_ref, bias_ref,
             dq_ref, dk_ref, dv_ref, dkc_ref, dvc_ref, dsink_ref,
             dq_scr, ck_scr, cv_scr, k1_acc, k2_acc, v1_acc, v2_acc):
        i = pl.program_id(0)

        @pl.when(i == 0)
        def _():
            for r in (k1_acc, k2_acc, v1_acc, v2_acc, dkc_ref, dvc_ref, dsink_ref):
                r[...] = jnp.zeros_like(r)

        @pl.when(i < steps)
        def _():
            bands = [bias_ref[jnp.where(i == 0, 0, 1)], bias_ref[jnp.where(i == steps - 1, 2, 1)]]

            def lanes(hk):
                return slice(hk * HEAD_DIM, (hk + 1) * HEAD_DIM)

            def keys(ctx_ref, p_ref, m_ref, x_ref, qb, hk):
                sl = lanes(hk)
                band3 = ([p_ref[:, sl], m_ref[0:BLK, sl], m_ref[BLK:2 * BLK, sl]] if qb == 0
                         else [m_ref[0:BLK, sl], m_ref[BLK:2 * BLK, sl], x_ref[:, sl]])
                return jnp.concatenate([ctx_ref[:, sl]] + band3, axis=0)

            def stacked(ref, qb, hk, width):
                return jnp.concatenate(
                    [ref[qb * BLK:(qb + 1) * BLK, (hk * GQA_GROUP + g) * width:(hk * GQA_GROUP + g + 1) * width]
                     for g in range(GQA_GROUP)], axis=0)

            def scores(qb, hk):
                kcat = keys(kc_ref, kp_ref, km_ref, kx_ref, qb, hk)
                qg = stacked(q_ref, qb, hk, HEAD_DIM)
                s = _masked(lax.dot_general(qg, kcat, NT, preferred_element_type=F32), bands[qb], C)
                dog = stacked(do_ref, qb, hk, HEAD_DIM)
                dp = lax.dot_general(dog, keys(vc_ref, vp_ref, vm_ref, vx_ref, qb, hk), NT, preferred_element_type=F32)
                return kcat, qg, dog, s, dp

            def softmax_bwd(qb, hk, s, dp):
                lse_c = stacked(lse_ref, qb, hk, 1)
                p = jnp.exp(s - lse_c)
                delta = jnp.sum(p * dp, axis=1, keepdims=True)
                ds = (p * (dp - delta)).astype(BF16)
                p_sink = jnp.exp(_sink_col(sink_ref, hk) - lse_c) * delta
                return p.astype(BF16), ds, p_sink

            def put_dq(qb, hk, dqs, p_sink):
                for g in range(GQA_GROUP):
                    h = hk * GQA_GROUP + g
                    dq_scr[qb * BLK:(qb + 1) * BLK, h * HEAD_DIM:(h + 1) * HEAD_DIM] = dqs[g * BLK:(g + 1) * BLK, :]
                    tot = jnp.sum(p_sink[g * BLK:(g + 1) * BLK, :], axis=0, keepdims=True)
                    dsink_ref[h:h + 1, :] -= jnp.broadcast_to(tot, (1, 128))

            sc = [scores(qb, hk) for qb, hk in chains]
            pending = None
            for (qb, hk), (kcat, qg, dog, s, dp) in zip(chains, sc):
                pb, ds, p_sink = softmax_bwd(qb, hk, s, dp)
                if pending is not None:
                    pqb, phk, pds, ppb, pqg, pdog = pending
                    ck_scr[pqb, :, lanes(phk)] = lax.dot_general(pds, pqg, TN, preferred_element_type=F32)
                    cv_scr[pqb, :, lanes(phk)] = lax.dot_general(ppb, pdog, TN, preferred_element_type=F32)
                put_dq(qb, hk, jnp.dot(ds, kcat, preferred_element_type=F32), p_sink)
                pending = (qb, hk, ds, pb, qg, dog)
            pqb, phk, pds, ppb, pqg, pdog = pending
            ck_scr[pqb, :, lanes(phk)] = lax.dot_general(pds, pqg, TN, preferred_element_type=F32)
            cq, sq = cq_ref[...], sq_ref[...]
            for j in range(4):
                dq_ref[:, j * 128:(j + 1) * 128] = _unrope(dq_scr[:, j * 128:(j + 1) * 128] * Q_SCALE, cq, sq).astype(BF16)
            cv_scr[pqb, :, lanes(phk)] = lax.dot_general(ppb, pdog, TN, preferred_element_type=F32)
            dkc_ref[...] += ck_scr[0, 0:C, :] + ck_scr[1, 0:C, :]
            dvc_ref[...] += cv_scr[0, 0:C, :] + cv_scr[1, 0:C, :]

        @pl.when(i >= steps)
        def _():
            ck_scr[...] = jnp.zeros_like(ck_scr)
            cv_scr[...] = jnp.zeros_like(cv_scr)

        def part(scr, qb, j):
            return scr[qb, C + j * BLK:C + (j + 1) * BLK, :]

        dk_ref[0:BLK, :] = _unrope(k1_acc[...] + part(ck_scr, 0, 0), cl_ref[...], sl_ref[...]).astype(BF16)
        dk_ref[BLK:2 * BLK, :] = _unrope(k2_acc[...] + part(ck_scr, 0, 1) + part(ck_scr, 1, 0),
                                         cq_ref[0:BLK, :], sq_ref[0:BLK, :]).astype(BF16)
        dv_ref[0:BLK, :] = (v1_acc[...] + part(cv_scr, 0, 0)).astype(BF16)
        dv_ref[BLK:2 * BLK, :] = (v2_acc[...] + part(cv_scr, 0, 1) + part(cv_scr, 1, 0)).astype(BF16)
        k1_acc[...] = part(ck_scr, 0, 2) + part(ck_scr, 1, 1)
        v1_acc[...] = part(cv_scr, 0, 2) + part(cv_scr, 1, 1)
        k2_acc[...] = part(ck_scr, 1, 2)
        v2_acc[...] = part(cv_scr, 1, 2)

    last = steps - 1
    kv3 = [pl.BlockSpec((BLK, KV_W), lambda i: (jnp.clip(2 * i - 1, 0, nb - 1), 0)),
           pl.BlockSpec((2 * BLK, KV_W), lambda i: (jnp.minimum(i, last), 0)),
           pl.BlockSpec((BLK, KV_W), lambda i: (jnp.minimum(2 * i + 2, nb - 1), 0))]
    cur = lambda w: pl.BlockSpec((2 * BLK, w), lambda i: (jnp.minimum(i, last), 0))
    late = lambda w: pl.BlockSpec((BLK, w), lambda i: (jnp.clip(2 * i - 1, 0, nb - 1), 0))
    out2 = lambda w: pl.BlockSpec((2 * BLK, w), lambda i: (i, 0))
    return _call(
        body, name="bwd_attn", grid=(steps + 1,),
        in_specs=[pl.BlockSpec(memory_space=pltpu.SMEM), cur(Q_W)] + kv3 + kv3
                 + [_full((C, KV_W)), _full((C, KV_W)), cur(Q_W), cur(N_Q_HEADS), cur(128), cur(128), late(128), late(128),
                    _full((3, GQA_GROUP * BLK, 3 * BLK))],
        out_specs=[cur(Q_W), out2(KV_W), out2(KV_W), _full((C, KV_W)), _full((C, KV_W)), _full((8, 128))],
        out_shape=[_sds((L, Q_W), BF16), _sds((L + 2 * BLK, KV_W), BF16), _sds((L + 2 * BLK, KV_W), BF16),
                   _sds((C, KV_W), F32), _sds((C, KV_W), F32), _sds((8, 128), F32)],
        scratch=[pltpu.VMEM((2 * BLK, Q_W), F32), pltpu.VMEM((2, NK, KV_W), F32), pltpu.VMEM((2, NK, KV_W), F32)]
                + [pltpu.VMEM((BLK, KV_W), F32)] * 4,
        args=(sink, q, k, k, k, v, v, v, kc, vc, dya, lse, cos, sin, cos, sin, bias), comm=comm)


def _k_ctx_bwd(ctx, modc, hc, dkc, dvc, w_kv):
    C = ctx.shape[0]

    def body(c_ref, mod_ref, hc_ref, dkc_ref, dvc_ref, w_ref, gw_ref, dmod_ref):
        dkv = jnp.concatenate([dkc_ref[...], dvc_ref[...]], axis=1).astype(BF16)
        gw_ref[...] = lax.dot_general(dkv, hc_ref[...], TN, preferred_element_type=F32)
        dhc = jnp.dot(dkv, w_ref[...], preferred_element_type=F32)
        n, _ = _ln(c_ref[...])
        dmod_ref[...] = jnp.zeros_like(dmod_ref)
        dmod_ref[0:1, :] = _colsum(dhc)
        dmod_ref[1:2, :] = _colsum(dhc * n)

    return pl.pallas_call(
        body, name="bwd_ctx", grid=(1,),
        in_specs=[_full((C, D)), _full((8, D)), _full((C, D)), _full((C, KV_W)), _full((C, KV_W)), _full((2 * KV_W, D))],
        out_specs=[_full((2 * KV_W, D)), _full((8, D))],
        out_shape=[_sds((2 * KV_W, D), F32), _sds((8, D), F32)],
        compiler_params=_params(("arbitrary",)),
    )(ctx, modc, hc, dkc, dvc, w_kv)


def _k_in_bwd(dq, dk, dv, du, dvb, dga, dgb, x, dxp, w_in, modv, tm, comm=None):
    L = x.shape[0]
    parts = [(O_Q, Q_W), (O_K, KV_W), (O_V, KV_W), (O_U, GM_W), (O_VB, GM_W), (O_GA, D), (O_GB, D)]

    def body(dq_ref, dk_ref, dv_ref, du_ref, dvb_ref, dga_ref, dgb_ref, x_ref, dxp_ref, w_ref, mod_ref,
             dP_ref, gx_ref, acc_ref):
        @pl.when(pl.program_id(0) == 0)
        def _():
            acc_ref[...] = jnp.zeros_like(acc_ref)

        for (lo, width), r in zip(parts, (dq_ref, dk_ref, dv_ref, du_ref, dvb_ref, dga_ref, dgb_ref)):
            dP_ref[:, lo:lo + width] = r[...]
        n1, rstd1 = _ln(x_ref[...])
        dh = jnp.dot(dP_ref[...], w_ref[...], preferred_element_type=F32)
        acc_ref[0:1, :] += _colsum(dh)
        acc_ref[1:2, :] += _colsum(dh * n1)
        gx_ref[...] = dxp_ref[...] + _ln_bwd(dh * (1.0 + mod_ref[1:2, :]), n1, rstd1)

    return _call(
        body, name="bwd_in", grid=(L // tm,),
        in_specs=[_row(tm, w) for _, w in parts] + [_row(tm, D), _row(tm, D), _resident((IN_W, D)), _full((8, D))],
        out_specs=[_row(tm, IN_W), _row(tm, D), _full((8, D))],
        out_shape=[_sds((L, IN_W), BF16), _sds((L, D), F32), _sds((8, D), F32)],
        args=(dq, dk, dv, du, dvb, dga, dgb, x, dxp, w_in, modv), comm=comm)


def _wgrad(a, b, name, tk, tt, comm=None, extra=None):
    T, K = a.shape
    N = b.shape[1]
    nt = T // tt

    def body(*refs):
        a_ref, b_ref = refs[:2]
        o_ref, acc_ref = refs[-2:]
        j, t = pl.program_id(0), pl.program_id(1)

        @pl.when(t == 0)
        def _():
            acc_ref[...] = jnp.zeros_like(acc_ref)

        acc_ref[...] += lax.dot_general(a_ref[...], b_ref[...], TN, preferred_element_type=F32)

        if extra is not None:
            lo, rows = extra[0] % tk, extra[1].shape[0]

            @pl.when((t == nt - 1) & (j == extra[0] // tk))
            def _():
                acc_ref[lo:lo + rows, :] += refs[2][...]

        @pl.when(t == nt - 1)
        def _():
            o_ref[...] = acc_ref[...].astype(BF16)

    extra_specs = [] if extra is None else [pl.BlockSpec(extra[1].shape, lambda j, t: (0, 0))]
    (out,), got = _call(
        body, name=name, grid=(K // tk, nt),
        in_specs=[pl.BlockSpec((tt, tk), lambda j, t: (t, j)), pl.BlockSpec((tt, N), lambda j, t: (t, 0))] + extra_specs,
        out_specs=[pl.BlockSpec((tk, N), lambda j, t: (j, 0))],
        out_shape=[_sds((K, N), BF16)],
        scratch=[pltpu.VMEM((tk, N), F32)],
        args=(a, b) + (() if extra is None else (extra[1],)), comm=comm)
    return (out, got) if comm is not None else out


def _adamw_reduce(parts, w, m, v, name, tr):
    R, C = w.shape
    n_parts = parts.shape[0]

    def body(p_ref, w_ref, m_ref, v_ref, g_ref, d_ref, m2_ref, v2_ref):
        g = p_ref[0].astype(F32)
        for i in range(1, n_parts):
            g = g + p_ref[i].astype(F32)
        delta, m2, v2 = _adamw(w_ref[...], g, m_ref[...], v_ref[...])
        g_ref[...] = g
        d_ref[...] = delta
        m2_ref[...] = m2
        v2_ref[...] = v2

    spec = _row(tr, C)
    return pl.pallas_call(
        body, name=name, grid=(R // tr,),
        in_specs=[pl.BlockSpec((n_parts, tr, C), lambda i: (0, i, 0)), spec, spec, spec],
        out_specs=[spec] * 4,
        out_shape=[_sds((R, C), F32)] * 4,
        compiler_params=_params(("arbitrary",)),
    )(parts, w, m, v)


SMALL_ORDER = ("b_ada", "ln1_g", "ln1_b", "ln2_g", "ln2_b", "gmlp_ln_g", "gmlp_ln_b", "b_spatial", "attn_sink")


def _small_step(gath, params):
    flat = [a for name in SMALL_ORDER for a in params[name]]

    def grad_of(tot, name):
        if name == "b_ada":
            return jnp.concatenate([tot[r:r + 1, :] for r in range(6)], axis=1)
        if name in ("ln1_g", "ln1_b", "ln2_g", "ln2_b"):
            r = 8 + ("ln1_g", "ln1_b", "ln2_g", "ln2_b").index(name)
            return tot[r:r + 1, :]
        if name == "gmlp_ln_g":
            return tot[12:13, :GM_W]
        if name == "gmlp_ln_b":
            return tot[12:13, GM_W:]
        if name == "b_spatial":
            return jnp.concatenate([tot[13:14, g * BLK:(g + 1) * BLK] for g in range(N_GROUPS)], axis=0)[None]
        return tot[14:15, :N_Q_HEADS]

    def body(*refs):
        g_ref, in_refs = refs[0], refs[1:1 + len(flat)]
        tot_ref, out_refs = refs[1 + len(flat)], refs[2 + len(flat):]
        tot = g_ref[0]
        for i in range(1, N_DEV):
            tot = tot + g_ref[i]
        tot_ref[...] = tot
        tot_ref[0:2, :] = tot[0:2, :] + tot[6:8, :]
        tot_ref[15:16, :] = jnp.broadcast_to(jnp.sum(tot[15:16, :], axis=1, keepdims=True), (1, D))
        tot = tot_ref[...]
        for k, name in enumerate(SMALL_ORDER):
            w_ref, m_ref, v_ref = in_refs[3 * k:3 * k + 3]
            g = grad_of(tot, name)
            delta, m2, v2 = _adamw(w_ref[...], g, m_ref[...], v_ref[...])
            for r, val in zip(out_refs[4 * k:4 * k + 4], (g, delta, m2, v2)):
                r[...] = val

    res = pl.pallas_call(
        body, name="small_step", grid=(1,),
        in_specs=[_full((N_DEV, 16, D))] + [_full(a.shape) for a in flat],
        out_specs=[_full((16, D))] + [_full(params[name][0].shape) for name in SMALL_ORDER for _ in range(4)],
        out_shape=[_sds((16, D), F32)] + [_sds(params[name][0].shape, F32) for name in SMALL_ORDER for _ in range(4)],
        compiler_params=_params(("arbitrary",)),
    )(gath, *flat)
    return res[0], {name: res[1 + 4 * k:5 + 4 * k] for k, name in enumerate(SMALL_ORDER)}


def _cctx_finish(gath, c_ctx, m, v):
    def body(g_ref, c_ref, m_ref, v_ref, gr_ref, d_ref, m2_ref, v2_ref):
        ds = g_ref[0]
        for i in range(1, N_DEV):
            ds = ds + g_ref[i]
        c = c_ref[...]
        sg = _sigmoid(c)
        g = ds * (sg * (1.0 + c * (1.0 - sg)))
        delta, m2, v2 = _adamw(c, g, m_ref[...], v_ref[...])
        gr_ref[...] = g
        d_ref[...] = delta
        m2_ref[...] = m2
        v2_ref[...] = v2

    return pl.pallas_call(
        body, name="cctx_finish", grid=(1,),
        in_specs=[_full((N_DEV, 8, D))] + [_full((8, D))] * 3, out_specs=[_full((8, D))] * 4,
        out_shape=[_sds((8, D), F32)] * 4,
        compiler_params=_params(("arbitrary",)),
    )(gath, c_ctx, m, v)


def _pad_rows(a, rows):
    return jnp.concatenate([a, jnp.zeros((rows - a.shape[0], a.shape[1]), a.dtype)], axis=0)


def kernel(x, c, ctx, c_ctx, w_ada, b_ada, w_in, attn_sink, gmlp_ln_g, gmlp_ln_b, w_spatial, b_spatial, w_branch_a, w_branch_b, w_out, ln1_g, ln1_b, w_ffn_in, w_ffn_out, ln2_g, ln2_b, loss_target, m_c_ctx, m_w_ada, m_b_ada, m_w_in, m_attn_sink, m_gmlp_ln_g, m_gmlp_ln_b, m_w_spatial, m_b_spatial, m_w_branch_a, m_w_branch_b, m_w_out, m_ln1_g, m_ln1_b, m_w_ffn_in, m_w_ffn_out, m_ln2_g, m_ln2_b, v_c_ctx, v_w_ada, v_b_ada, v_w_in, v_attn_sink, v_gmlp_ln_g, v_gmlp_ln_b, v_w_spatial, v_b_spatial, v_w_branch_a, v_w_branch_b, v_w_out, v_ln1_g, v_ln1_b, v_w_ffn_in, v_w_ffn_out, v_ln2_g, v_ln2_b):
    L = x.shape[1]
    me = 4 * lax.axis_index("x") + 2 * lax.axis_index("y") + lax.axis_index("c")
    x2, tgt, ctx2 = x[0], loss_target[0], ctx[0]
    tm_in = min(512, L)
    tm = min(256, L)
    tt = min(2048, L)

    transposed = ("w_in", "w_ffn_in")
    tr = lambda kname, a: a.T if kname in transposed else a
    big = dict(w_in=w_in[0].T, w_branch_a=w_branch_a[0], w_branch_b=w_branch_b[0], w_out=w_out[0],
               w_ffn_in=w_ffn_in[0].T, w_ffn_out=w_ffn_out[0])
    col_sharded = ("w_branch_a", "w_branch_b")
    shard_bf = {k: a.astype(BF16) for k, a in big.items()}

    def assemble(kname, g):
        if kname in col_sharded:
            return g.transpose(1, 0, 2).reshape(g.shape[1], N_DEV * g.shape[2])
        return g.reshape(N_DEV * g.shape[1], g.shape[2])

    def to_blocks(kname, g):
        if kname in col_sharded:
            return g.reshape(g.shape[0], N_DEV, g.shape[1] // N_DEV).transpose(1, 0, 2)
        return g.reshape(N_DEV, g.shape[0] // N_DEV, g.shape[1])

    full = {}
    n_ada = w_ada.shape[2]
    b_my = lax.dynamic_slice(b_ada, (0, me * n_ada), (1, n_ada))
    act, mod_all, got = _prologue(_pad_rows(c, 8), _pad_rows(c_ctx[None, :], 8), w_ada[0], b_my,
                                  _Comm(gather=[shard_bf["w_in"]]))
    full["w_in"] = assemble("w_in", got[0])
    mod_all = mod_all.transpose(1, 0, 2).reshape(16, 6 * D)
    modv = _pad_rows(lax.dynamic_slice(mod_all, (me, 0), (1, 6 * D)).reshape(6, D), 8)
    modc = _pad_rows(mod_all[8].reshape(6, D), 8)

    lnv = _pad_rows(jnp.concatenate([ln1_g, ln1_b, ln2_g, ln2_b], axis=0), 8)
    gm_lnv = _pad_rows(jnp.concatenate([gmlp_ln_g, gmlp_ln_b], axis=0), 8)
    ws_b = w_spatial[0].astype(BF16)
    wst_b = ws_b.transpose(0, 2, 1)
    bsp = jnp.repeat(b_spatial[0].T, GROUP_DIM, axis=1)
    sink = attn_sink[0]
    cos, sin = _rope_tables(L)
    bias = _attn_bias()
    w_kv = full["w_in"][O_K:O_K + 2 * KV_W, :]

    (h, q, k, v, u, vb, ga, gb), got = _k_in(
        x2, modv, full["w_in"], cos, sin, tm_in,
        comm=_Comm(gather=[shard_bf[kname] for kname in ("w_branch_a", "w_branch_b", "w_out", "w_ffn_out")]))
    for kname, g in zip(("w_branch_a", "w_branch_b", "w_out", "w_ffn_out"), got):
        full[kname] = assemble(kname, g)
    hc, kc, vc = _k_ctx(ctx2, modc, w_kv)
    (ya, lse), got = _k_attn(sink, q, k, v, kc, vc, bias, comm=_Comm(gather=[shard_bf["w_ffn_in"]]))
    full["w_ffn_in"] = assemble("w_ffn_in", got[0])
    yb = _k_gmlp(u, vb, gm_lnv, ws_b, bsp)
    merged, mix, xm, h2 = _k_merge(x2, ya, yb, ga, gb, full["w_branch_a"], full["w_branch_b"], full["w_out"], modv, lnv, tm_in)
    gate, up, act_f, dr2, df, acc_f = _k_ffn(h2, xm, tgt, full["w_ffn_in"], full["w_ffn_out"], modv, lnv, tm_in)

    dF, dmix, dxp, acc_b = _k_ffn_bwd(df, gate, up, xm, dr2, x2, mix, full["w_ffn_in"], full["w_ffn_out"], modv, lnv, tm)
    blk_fo = to_blocks("w_ffn_out", _wgrad(act_f, df, "wgrad_ffn_out", 1408, tt))
    gw_fi, (rcv_fo,) = _wgrad(dF, h2, "wgrad_ffn_in", 1408, tt, comm=_Comm(scatter=[blk_fo]))
    blk_fi = to_blocks("w_ffn_in", gw_fi)
    (dA, dB, dga, dgb, dya, dyb), _ = _k_merge_bwd(
        dmix, ya, yb, ga, gb, full["w_branch_a"], full["w_branch_b"], full["w_out"], tm_in)
    du, dvb, g_ws, g_bst, g_gln = _k_gmlp_bwd(u, vb, dyb, gm_lnv, ws_b, wst_b, bsp)
    (dq, dk_late, dv_late, dkc, dvc, g_sink), (gath_ws, rcv_fi) = _k_attn_bwd(
        sink, q, k, v, kc, vc, dya, lse, cos, sin, bias,
        comm=_Comm(gather=[g_ws.reshape(N_GROUPS * BLK, BLK)], scatter=[blk_fi]))
    dk, dv = dk_late[BLK:BLK + L], dv_late[BLK:BLK + L]
    blk_a = to_blocks("w_branch_a", _wgrad(ya, dA, "wgrad_a", Q_W, tt))
    blk_b = to_blocks("w_branch_b", _wgrad(yb, dB, "wgrad_b", GM_W, tt))
    blk_o = to_blocks("w_out", _wgrad(merged, dmix, "wgrad_out", D, tt))
    (dP, grad_x, acc_i), _ = _k_in_bwd(dq, dk, dv, du, dvb, dga, dgb, x2, dxp, full["w_in"], modv, tm_in)
    g_ctx, dmodc = _k_ctx_bwd(ctx2, modc, hc, dkc, dvc, w_kv)
    gw_in, (rcv_a, rcv_b, rcv_o) = _wgrad(dP, h, "wgrad_in", 1280, tt, comm=_Comm(scatter=[blk_a, blk_b, blk_o]),
                                          extra=(O_K, g_ctx))

    dmod_x = jnp.concatenate([acc_i[0:2], acc_b[4:5], acc_b[0:2], acc_f[2:3]], axis=0)
    small = jnp.concatenate([
        dmod_x, dmodc[0:2], acc_b[2:4], acc_f[0:2],
        jnp.concatenate([g_gln[0:1], g_gln[1:2]], axis=1), g_bst.T.reshape(1, D),
        _pad_rows(g_sink[:, 0:1], D).T, acc_f[3:4]], axis=0)
    rcv_in, gath = _exchange_two_level(to_blocks("w_in", gw_in), small, "exchange_last")
    received = dict(w_in=rcv_in, w_branch_a=rcv_a, w_branch_b=rcv_b, w_out=rcv_o, w_ffn_in=rcv_fi, w_ffn_out=rcv_fo)
    moments = dict(w_in=(m_w_in, v_w_in), w_branch_a=(m_w_branch_a, v_w_branch_a), w_branch_b=(m_w_branch_b, v_w_branch_b),
                   w_out=(m_w_out, v_w_out), w_ffn_in=(m_w_ffn_in, v_w_ffn_in), w_ffn_out=(m_w_ffn_out, v_w_ffn_out))
    names = list(big)
    res = {}
    for kname in names:
        mm, vv = moments[kname]
        R = big[kname].shape[0]
        res[kname] = [tr(kname, r) for r in _adamw_reduce(
            received[kname], big[kname], tr(kname, mm[0]), tr(kname, vv[0]), "adamw_" + kname, 256 if R % 256 == 0 else R // 2)]

    ws2d = lambda a: a.reshape(N_GROUPS * BLK, BLK)
    res_ws = [r.reshape(w_spatial.shape) for r in _adamw_reduce(
        gath_ws, ws2d(w_spatial), ws2d(m_w_spatial), ws2d(v_w_spatial), "adamw_w_spatial", 256)]
    tot, res_small = _small_step(gath, dict(
        b_ada=(b_ada, m_b_ada, v_b_ada), ln1_g=(ln1_g, m_ln1_g, v_ln1_g), ln1_b=(ln1_b, m_ln1_b, v_ln1_b),
        ln2_g=(ln2_g, m_ln2_g, v_ln2_g), ln2_b=(ln2_b, m_ln2_b, v_ln2_b),
        gmlp_ln_g=(gmlp_ln_g, m_gmlp_ln_g, v_gmlp_ln_g), gmlp_ln_b=(gmlp_ln_b, m_gmlp_ln_b, v_gmlp_ln_b),
        b_spatial=(b_spatial, m_b_spatial, v_b_spatial), attn_sink=(attn_sink, m_attn_sink, v_attn_sink)))
    loss = tot[15, 0]

    dmod_rows = jnp.concatenate([gath[:, 0:6, :].reshape(N_DEV, 6 * D),
                                 jnp.concatenate([tot[6:8].reshape(1, 2 * D), jnp.zeros((1, 4 * D), F32)], axis=1),
                                 jnp.zeros((7, 6 * D), F32)], axis=0)
    dmod_my = lax.dynamic_slice(dmod_rows, (0, me * n_ada), (16, n_ada))
    g_wada, d_wada, m2_wada, v2_wada, pc = _ada_bwd(act, dmod_my, w_ada[0], m_w_ada[0], v_w_ada[0])
    pc_all = _ag_small(pc, "gather_cctx")
    cc8 = lambda a: _pad_rows(a.reshape(1, D), 8)
    g_cc, d_cc, m2_cc, v2_cc = _cctx_finish(pc_all, cc8(c_ctx), cc8(m_c_ctx), cc8(v_c_ctx))

    order = ["c_ctx", "w_ada", "b_ada", "w_in", "attn_sink", "gmlp_ln_g", "gmlp_ln_b", "w_spatial", "b_spatial",
             "w_branch_a", "w_branch_b", "w_out", "ln1_g", "ln1_b", "w_ffn_in", "w_ffn_out", "ln2_g", "ln2_b"]
    grads, deltas, new_m, new_v = {}, {}, {}, {}
    grads["c_ctx"], deltas["c_ctx"], new_m["c_ctx"], new_v["c_ctx"] = g_cc[0], d_cc[0], m2_cc[0], v2_cc[0]
    grads["w_ada"], deltas["w_ada"], new_m["w_ada"], new_v["w_ada"] = g_wada[None], d_wada[None], m2_wada[None], v2_wada[None]
    for kname in names:
        g, d, m2, v2 = res[kname]
        grads[kname], deltas[kname], new_m[kname], new_v[kname] = g[None], d[None], m2[None], v2[None]
    grads["w_spatial"], deltas["w_spatial"], new_m["w_spatial"], new_v["w_spatial"] = res_ws
    for kname in SMALL_ORDER:
        grads[kname], deltas[kname], new_m[kname], new_v[kname] = res_small[kname]
    return (loss, grad_x[None], *[grads[n] for n in order], *[deltas[n] for n in order],
            *[new_m[n] for n in order], *[new_v[n] for n in order])
```

```python
import functools
import math

import jax
import jax.numpy as jnp
import numpy as np
from jax import lax
from jax.experimental import pallas as pl
from jax.experimental.pallas import tpu as pltpu

F32 = jnp.float32
BF16 = jnp.bfloat16
MESH = pl.DeviceIdType.MESH

N_DEV = 8
D = 1024
HEAD_DIM = 64
N_Q_HEADS = 8
N_KV_HEADS = 2
GQA_GROUP = 4
BLK = 128
Q_W = 512
KV_W = 128
GM_W = 512
N_GROUPS = 8
GROUP_DIM = 64
FFN_H = 2816
IN_W = 3840
O_Q, O_K, O_V, O_U, O_VB, O_GA, O_GB = 0, 512, 640, 768, 1280, 1792, 2816
LN_EPS = 1e-5
NEG_INF = -1e30
ALPHA = 2.0 ** 0.25
ROPE_BASE = 10000.0
ROPE_PAIRS = 16
Q_SCALE = HEAD_DIM ** -0.5
GELU_K0 = math.sqrt(2.0 / math.pi)
GELU_K1 = 0.044715

ADAM_LR = 0.001
ADAM_B1 = 0.9
ADAM_B2 = 0.999
ADAM_EPS = 1e-08
ADAM_WD = 0.01
ADAM_STEP = 10

V7X_VMEM_BYTES = 64 * 1024 * 1024
VMEM_LIMIT = V7X_VMEM_BYTES * 7 // 8
NT = (((1,), (1,)), ((), ()))
TN = (((0,), (0,)), ((), ()))


class _Tiles:
    def __init__(self, L):
        self.wide = min(512, L)
        self.narrow = min(256, L)
        self.tokens = min(2048, L)
        self.tk_in = IN_W // 3
        self.tk_ffn = FFN_H // 2


def _params(sem=None):
    return pltpu.CompilerParams(dimension_semantics=sem, vmem_limit_bytes=VMEM_LIMIT)


def _row(tm, w):
    return pl.BlockSpec((tm, w), lambda i: (i, 0))


def _full(shape):
    nd = len(shape)
    return pl.BlockSpec(shape, lambda i: (0,) * nd)


def _resident(shape):
    nd = len(shape)
    return pl.BlockSpec(shape, lambda i: (0,) * nd, pipeline_mode=pl.Buffered(1))


def _sds(shape, dt):
    return jax.ShapeDtypeStruct(shape, dt)


def _ln(xf):
    mu = jnp.mean(xf, axis=-1, keepdims=True)
    xc = xf - mu
    var = jnp.mean(xc * xc, axis=-1, keepdims=True)
    rstd = lax.rsqrt(var + LN_EPS)
    return xc * rstd, rstd


def _ln_bwd(dn, n, rstd):
    m1 = jnp.mean(dn, axis=-1, keepdims=True)
    m2 = jnp.mean(dn * n, axis=-1, keepdims=True)
    return rstd * (dn - m1 - n * m2)


def _colsum(t):
    return jnp.sum(t, axis=0, keepdims=True)


def _sigmoid(x):
    return 0.5 * jnp.tanh(0.5 * x) + 0.5


def _gelu(x):
    t = jnp.tanh(GELU_K0 * (x + GELU_K1 * (x * x * x)))
    return x * (0.5 * (1.0 + t)), t


def _gelu_grad(x, t):
    return 0.5 * (1.0 + t) + 0.5 * x * (1.0 - t * t) * (GELU_K0 * (1.0 + 3.0 * GELU_K1 * x * x))


def _swap16(t):
    lane = lax.broadcasted_iota(jnp.int32, t.shape, 1)
    return jnp.where((lane & 16) == 0, pltpu.roll(t, 112, 1), pltpu.roll(t, 16, 1))


def _rope(t, cos, sin):
    return t * cos + _swap16(t) * sin


def _unrope(t, cos, sin):
    return t * cos - _swap16(t) * sin


def _adamw(w, g, m, v):
    m2 = ADAM_B1 * m + (1.0 - ADAM_B1) * g
    v2 = ADAM_B2 * v + (1.0 - ADAM_B2) * (g * g)
    m_hat = m2 / (1.0 - ADAM_B1 ** ADAM_STEP)
    v_hat = v2 / (1.0 - ADAM_B2 ** ADAM_STEP)
    delta = -ADAM_LR * (m_hat / (jnp.sqrt(v_hat) + ADAM_EPS) + ADAM_WD * w)
    return delta, m2, v2


def _rope_tables(L):
    inv = (np.float32(ROPE_BASE) ** (-np.arange(ROPE_PAIRS, dtype=np.float32) / np.float32(ROPE_PAIRS))).astype(np.float32)
    t = np.arange(L, dtype=np.int32)
    rows = (t // 64).astype(np.float32)[:, None] * inv
    cols = (t % 64).astype(np.float32)[:, None] * inv
    cr, sr, cc, sc = np.cos(rows), np.sin(rows), np.cos(cols), np.sin(cols)
    cos = np.concatenate([cr, cr, cc, cc], axis=1)
    sin = np.concatenate([-sr, sr, -sc, sc], axis=1)
    return jnp.asarray(np.tile(cos, (1, 2)), F32), jnp.asarray(np.tile(sin, (1, 2)), F32)


def _me():
    return lax.axis_index("x"), lax.axis_index("y"), lax.axis_index("c")


def _peer(mx, my, mc, k):
    return (mx ^ ((k >> 2) & 1), my ^ ((k >> 1) & 1), mc ^ (k & 1))


def _ag_small(x, name):
    R, C = x.shape

    def body(x_ref, out_ref, send_sems, recv_sems):
        mx, my, mc = _me()
        me = 4 * mx + 2 * my + mc
        out_ref[pl.ds(me, 1)] = x_ref[...][None]
        sends = []
        for k in range(1, N_DEV):
            cp = pltpu.make_async_remote_copy(
                src_ref=x_ref, dst_ref=out_ref.at[me], send_sem=send_sems.at[k - 1], recv_sem=recv_sems.at[k - 1],
                device_id=_peer(mx, my, mc, k), device_id_type=MESH)
            cp.start()
            sends.append(cp)
        for k in range(1, N_DEV):
            pltpu.make_async_remote_copy(
                src_ref=x_ref, dst_ref=out_ref.at[me ^ k], send_sem=send_sems.at[k - 1], recv_sem=recv_sems.at[k - 1],
                device_id=(mx, my, mc), device_id_type=MESH).wait_recv()
        for cp in sends:
            cp.wait_send()

    return pl.pallas_call(
        body, name=name,
        out_shape=_sds((N_DEV, R, C), x.dtype),
        in_specs=[pl.BlockSpec(memory_space=pltpu.VMEM)],
        out_specs=pl.BlockSpec(memory_space=pltpu.VMEM),
        scratch_shapes=[pltpu.SemaphoreType.DMA((N_DEV - 1,)), pltpu.SemaphoreType.DMA((N_DEV - 1,))],
        compiler_params=pltpu.CompilerParams(vmem_limit_bytes=VMEM_LIMIT),
    )(x)


class _Comm:
    def __init__(self, gather=(), scatter=(), spread=()):
        self.kinds = ["gather"] * len(gather) + ["scatter"] * len(scatter) + ["spread"] * len(spread)
        self.args = list(gather) + list(scatter) + list(spread)
        self.n = len(self.args)

    def out_shape(self):
        return [_sds(a.shape if k == "scatter" else (N_DEV,) + a.shape, a.dtype) for k, a in zip(self.kinds, self.args)]

    def specs(self):
        return [pl.BlockSpec(memory_space=pl.ANY)] * self.n

    def scratch(self):
        return [pltpu.SemaphoreType.DMA((7 * self.n,)), pltpu.SemaphoreType.DMA((7 * self.n,)),
                pltpu.SemaphoreType.DMA((self.n,))]

    def _plan(self, x_refs, out_refs, send_sems, recv_sems, local_sems):
        mx, my, mc = _me()
        me = 4 * mx + 2 * my + mc
        here, sibling = (mx, my, mc), (mx, my, 1 - mc)
        chips = [(1 - mx, my), (mx, 1 - my), (1 - mx, 1 - my)]
        local, first, last = [], [], []
        relay = [[], [], []]
        for a, kind in enumerate(self.kinds):
            x, out = x_refs[a], out_refs[a]

            def rc(k, src, dst, to):
                return pltpu.make_async_remote_copy(
                    src_ref=src, dst_ref=dst, send_sem=send_sems.at[7 * a + k], recv_sem=recv_sems.at[7 * a + k],
                    device_id=to, device_id_type=MESH)

            if kind == "gather":
                local.append(pltpu.make_async_copy(x, out.at[me], local_sems.at[a]))
                first.append(rc(0, x, out.at[me], sibling))
                last.append(rc(0, x, out.at[me ^ 1], here))
                for j, (cx, cy) in enumerate(chips):
                    first.append(rc(1 + j, x, out.at[me], (cx, cy, mc)))
                    landed = out.at[4 * cx + 2 * cy + mc]
                    relay[j].append((rc(1 + j, x, landed, here), rc(4 + j, landed, landed, sibling)))
                    last.append(rc(4 + j, x, out.at[4 * cx + 2 * cy + 1 - mc], here))
            else:
                own = x.at[me] if kind == "scatter" else x
                local.append(pltpu.make_async_copy(own, out.at[me], local_sems.at[a]))
                for k in range(1, N_DEV):
                    src = x.at[me ^ k] if kind == "scatter" else x
                    first.append(rc(k - 1, src, out.at[me], _peer(mx, my, mc, k)))
                    last.append(rc(k - 1, own, out.at[me ^ k], here))
        return local, first, relay[0] + relay[1] + relay[2], last

    def start(self, *refs):
        local, first, _, _ = self._plan(*refs)
        for cp in local + first:
            cp.start()

    def finish(self, *refs):
        local, first, relay, last = self._plan(*refs)
        for arrival, onward in relay:
            arrival.wait_recv()
            onward.start()
        for cp in last:
            cp.wait_recv()
        for cp in first:
            cp.wait_send()
        for _, onward in relay:
            onward.wait_send()
        for cp in local:
            cp.wait()


def _call(body, *, name, grid, in_specs, out_specs, out_shape, args, scratch=(), comm=None, aliases=None):
    params = _params(("arbitrary",) * len(grid))

    def at(end):
        conds = [pl.program_id(d) == (n - 1 if end else 0) for d, n in enumerate(grid)]
        return functools.reduce(lambda p, q: p & q, conds)

    if comm is None:
        res = pl.pallas_call(
            body, name=name, grid=grid, in_specs=list(in_specs), out_specs=list(out_specs), out_shape=list(out_shape),
            scratch_shapes=list(scratch), input_output_aliases=aliases or {}, compiler_params=params)(*args)
        return list(res), []
    n_in, n_out, n_scr, cn = len(in_specs), len(out_specs), len(scratch), comm.n

    def hosted(*refs):
        ins, refs = refs[:n_in], refs[n_in:]
        cins, refs = refs[:cn], refs[cn:]
        outs, refs = refs[:n_out], refs[n_out:]
        couts, refs = refs[:cn], refs[cn:]
        scr, sems = refs[:n_scr], refs[n_scr:]

        @pl.when(at(False))
        def _():
            comm.start(cins, couts, *sems)

        body(*ins, *outs, *scr)

        @pl.when(at(True))
        def _():
            comm.finish(cins, couts, *sems)

    res = pl.pallas_call(
        hosted, name=name, grid=grid, in_specs=list(in_specs) + comm.specs(), out_specs=list(out_specs) + comm.specs(),
        out_shape=list(out_shape) + comm.out_shape(), scratch_shapes=list(scratch) + comm.scratch(),
        input_output_aliases=aliases or {}, compiler_params=params)(*args, *comm.args)
    return list(res[:n_out]), list(res[n_out:])


def _exchange_two_level(blk, small, name):
    _, R, C = blk.shape
    rows = small.shape[0]

    def body(blk_ref, small_ref, stage_ref, out_ref, gath_ref, a_scr, b_scr, t_scr, s1, r1, s3, r3, ss, rs, lsem):
        mx, my, mc = _me()
        me = 4 * mx + 2 * my + mc
        mine = 2 * mx + my
        here, sibling = (mx, my, mc), (mx, my, 1 - mc)

        def rc(src, dst, send, recv, to):
            return pltpu.make_async_remote_copy(src_ref=src, dst_ref=dst, send_sem=send, recv_sem=recv,
                                                device_id=to, device_id_type=MESH)

        own_small = pltpu.make_async_copy(small_ref, gath_ref.at[me], lsem.at[0])
        own_small.start()
        spread = [rc(small_ref, gath_ref.at[me], ss.at[k - 1], rs.at[k - 1], _peer(mx, my, mc, k)) for k in range(1, N_DEV)]
        to_sib = [rc(blk_ref.at[2 * p + 1 - mc], stage_ref.at[p], s1.at[p], r1.at[p], sibling) for p in range(4)]
        for cp in spread + to_sib:
            cp.start()
        own = [pltpu.make_async_copy(blk_ref.at[2 * p + mc], a_scr.at[p], lsem.at[1 + p]) for p in range(4)]
        for cp in own:
            cp.start()
        from_sib = []
        for p in range(4):
            rc(blk_ref.at[2 * p + 1 - mc], stage_ref.at[p], s1.at[p], r1.at[p], here).wait_recv()
            cp = pltpu.make_async_copy(stage_ref.at[p], b_scr.at[p], lsem.at[5 + p])
            cp.start()
            from_sib.append(cp)
        for cp in own + from_sib:
            cp.wait()
        t_scr[...] = (a_scr[...].astype(F32) + b_scr[...].astype(F32)).astype(BF16)
        keep = pltpu.make_async_copy(t_scr.at[mine], out_ref.at[mine], lsem.at[9])
        keep.start()
        onward = [rc(t_scr.at[mine ^ k], out_ref.at[mine], s3.at[k - 1], r3.at[k - 1], (mx ^ (k >> 1), my ^ (k & 1), mc))
                  for k in range(1, 4)]
        for cp in onward:
            cp.start()
        for k in range(1, 4):
            rc(t_scr.at[mine], out_ref.at[mine ^ k], s3.at[k - 1], r3.at[k - 1], here).wait_recv()
        for k in range(1, N_DEV):
            rc(small_ref, gath_ref.at[me ^ k], ss.at[k - 1], rs.at[k - 1], here).wait_recv()
        for cp in spread + to_sib + onward:
            cp.wait_send()
        keep.wait()
        own_small.wait()

    any_spec = pl.BlockSpec(memory_space=pl.ANY)
    dma = pltpu.SemaphoreType.DMA
    _, out, gath = pl.pallas_call(
        body, name=name,
        in_specs=[any_spec, any_spec], out_specs=[any_spec] * 3,
        out_shape=[_sds((4, R, C), BF16), _sds((4, R, C), BF16), _sds((N_DEV, rows, D), F32)],
        scratch_shapes=[pltpu.VMEM((4, R, C), BF16)] * 3
                       + [dma((4,)), dma((4,)), dma((3,)), dma((3,)), dma((N_DEV - 1,)), dma((N_DEV - 1,)), dma((10,))],
        compiler_params=pltpu.CompilerParams(vmem_limit_bytes=VMEM_LIMIT),
    )(blk, small)
    return out, gath


def _comm_only(comm, name):
    return _call(lambda: None, name=name, grid=(1,), in_specs=[], out_specs=[], out_shape=[], args=[], comm=comm)[1]


def _exchange_rows(x_ref, out_ref, send_sems, recv_sems):
    mx, my, mc = _me()
    me = 4 * mx + 2 * my + mc
    out_ref[pl.ds(me, 1)] = x_ref[...][None]
    sends = []
    for k in range(1, N_DEV):
        cp = pltpu.make_async_remote_copy(
            src_ref=x_ref, dst_ref=out_ref.at[me], send_sem=send_sems.at[k - 1], recv_sem=recv_sems.at[k - 1],
            device_id=_peer(mx, my, mc, k), device_id_type=MESH)
        cp.start()
        sends.append(cp)
    for k in range(1, N_DEV):
        pltpu.make_async_remote_copy(
            src_ref=x_ref, dst_ref=out_ref.at[me ^ k], send_sem=send_sems.at[k - 1], recv_sem=recv_sems.at[k - 1],
            device_id=(mx, my, mc), device_id_type=MESH).wait_recv()
    for cp in sends:
        cp.wait_send()


def _prologue(c8, cctx8, w_ada, b_my, comm):
    nw = w_ada.shape[1]

    def body(c_ref, cctx_ref, w_ref, b_ref, act_ref, mod_ref, cmine_scr, call_scr, mine_scr, mall_scr, s1, r1, s2, r2):
        cmine_scr[...] = c_ref[...]
        _exchange_rows(cmine_scr, call_scr, s1, r1)
        rows = [call_scr[d][0:1, :] for d in range(N_DEV)] + [cctx_ref[0:1, :], jnp.zeros((7, D), F32)]
        s = jnp.concatenate(rows, axis=0)
        act = s * _sigmoid(s)
        act_ref[...] = act
        mine_scr[...] = jnp.dot(act.astype(BF16), w_ref[...].astype(BF16), preferred_element_type=F32) + b_ref[...]
        _exchange_rows(mine_scr, mall_scr, s2, r2)
        mod_ref[...] = mall_scr[...]

    sems = [pltpu.SemaphoreType.DMA((N_DEV - 1,))] * 4
    (act, mod), got = _call(
        body, name="prologue", grid=(1,),
        in_specs=[_full((8, D)), _full((8, D)), _full((D, nw)), _full((1, nw))],
        out_specs=[_full((16, D)), _full((N_DEV, 16, nw))],
        out_shape=[_sds((16, D), F32), _sds((N_DEV, 16, nw), F32)],
        scratch=[pltpu.VMEM((8, D), F32), pltpu.VMEM((N_DEV, 8, D), F32), pltpu.VMEM((16, nw), F32),
                 pltpu.VMEM((N_DEV, 16, nw), F32)] + sems,
        args=(c8, cctx8, w_ada, b_my), comm=comm)
    return act, mod, got


def _ada_bwd(act, dmod_my, w_ada, m, v, tr=256):
    nw = w_ada.shape[1]

    def body(act_ref, dm_ref, w_ref, m_ref, v_ref, g_ref, d_ref, m2_ref, v2_ref, pc_ref):
        dm = dm_ref[...].astype(BF16)
        g = lax.dot_general(act_ref[...].astype(BF16), dm, TN, preferred_element_type=F32)
        w = w_ref[...]
        delta, m2, v2 = _adamw(w, g, m_ref[...], v_ref[...])
        g_ref[...] = g
        d_ref[...] = delta
        m2_ref[...] = m2
        v2_ref[...] = v2
        pc_ref[...] = lax.dot_general(dm[8:16, :], w.astype(BF16), NT, preferred_element_type=F32)

    wspec = _row(tr, nw)
    return pl.pallas_call(
        body, name="ada_bwd", grid=(D // tr,),
        in_specs=[pl.BlockSpec((16, tr), lambda i: (0, i)), _full((16, nw)), wspec, wspec, wspec],
        out_specs=[wspec, wspec, wspec, wspec, pl.BlockSpec((8, tr), lambda i: (0, i))],
        out_shape=[_sds((D, nw), F32)] * 4 + [_sds((8, D), F32)],
        compiler_params=_params(("arbitrary",)),
    )(act, dmod_my, w_ada, m, v)


def _k_in(x, modv, w_in, cos, sin, tm, comm=None):
    L = x.shape[0]

    def body(x_ref, mod_ref, w_ref, cos_ref, sin_ref, h_ref, q_ref, k_ref, v_ref, u_ref, vb_ref, ga_ref, gb_ref):
        n, _ = _ln(x_ref[...])
        h = (n * (1.0 + mod_ref[1:2, :]) + mod_ref[0:1, :]).astype(BF16)
        h_ref[...] = h
        c, s = cos_ref[...], sin_ref[...]

        def proj(lo, width):
            return lax.dot_general(h, w_ref[lo:lo + width, :], NT, preferred_element_type=F32)

        for i in range(4):
            q_ref[:, i * 128:(i + 1) * 128] = (_rope(proj(O_Q + i * 128, 128), c, s) * Q_SCALE).astype(BF16)
        k_ref[...] = _rope(proj(O_K, KV_W), c, s).astype(BF16)
        v_ref[...] = proj(O_V, KV_W).astype(BF16)
        u_ref[...] = proj(O_U, GM_W).astype(BF16)
        vb_ref[...] = proj(O_VB, GM_W).astype(BF16)
        ga_ref[...] = proj(O_GA, D).astype(BF16)
        gb_ref[...] = proj(O_GB, D).astype(BF16)

    widths = [D, Q_W, KV_W, KV_W, GM_W, GM_W, D, D]
    return _call(
        body, name="fwd_in", grid=(L // tm,),
        in_specs=[_row(tm, D), _full((8, D)), _resident((IN_W, D)), _row(tm, 128), _row(tm, 128)],
        out_specs=[_row(tm, w) for w in widths],
        out_shape=[_sds((L, w), BF16) for w in widths],
        args=(x, modv, w_in, cos, sin), comm=comm)


def _k_ctx(ctx, modc, w_kv):
    C = ctx.shape[0]

    def body(c_ref, mod_ref, w_ref, hc_ref, kc_ref, vc_ref):
        n, _ = _ln(c_ref[...])
        hc = (n * (1.0 + mod_ref[1:2, :]) + mod_ref[0:1, :]).astype(BF16)
        hc_ref[...] = hc
        kv = lax.dot_general(hc, w_ref[...], NT, preferred_element_type=F32)
        kc_ref[...] = kv[:, :KV_W].astype(BF16)
        vc_ref[...] = kv[:, KV_W:].astype(BF16)

    return pl.pallas_call(
        body, name="fwd_ctx", grid=(1,),
        in_specs=[_full((C, D)), _full((8, D)), _full((2 * KV_W, D))],
        out_specs=[_full((C, D)), _full((C, KV_W)), _full((C, KV_W))],
        out_shape=[_sds((C, D), BF16), _sds((C, KV_W), BF16), _sds((C, KV_W), BF16)],
        compiler_params=_params(("arbitrary",)),
    )(ctx, modc, w_kv)


def _attn_bias():
    r = (np.arange(GQA_GROUP * BLK) & (BLK - 1))[:, None]
    j = np.arange(3 * BLK)[None, :]
    band = np.abs(j - BLK - r) <= BLK
    variants = [band & (j >= BLK), band, band & (j < 2 * BLK)]
    return jnp.asarray(np.stack([np.where(v, 0.0, NEG_INF) for v in variants]), F32)


def _masked(s, bias, C):
    return jnp.concatenate([s[:, :C], s[:, C:] + bias], axis=1)


def _sink_col(sink_ref, hk):
    grp = lax.broadcasted_iota(jnp.int32, (GQA_GROUP * BLK, 1), 0) >> 7
    col = jnp.full((GQA_GROUP * BLK, 1), sink_ref[hk * GQA_GROUP], F32)
    for g in range(1, GQA_GROUP):
        col = jnp.where(grp == g, sink_ref[hk * GQA_GROUP + g], col)
    return col


def _k_attn(sink, q, k, v, kc, vc, bias, comm=None):
    L = q.shape[0]
    C = kc.shape[0]
    nb = L // BLK
    steps = nb // 2

    def body(sink_ref, q_ref, kp_ref, km_ref, kx_ref, vp_ref, vm_ref, vx_ref, kc_ref, vc_ref, bias_ref, ya_ref, lse_ref):
        i = pl.program_id(0)
        bands = [bias_ref[jnp.where(i == 0, 0, 1)], bias_ref[jnp.where(i == steps - 1, 2, 1)]]
        chains = [(qb, hk) for qb in range(2) for hk in range(N_KV_HEADS)]

        def keys(ctx_ref, p_ref, m_ref, x_ref, qb, hk):
            sl = slice(hk * HEAD_DIM, (hk + 1) * HEAD_DIM)
            band3 = ([p_ref[:, sl], m_ref[0:BLK, sl], m_ref[BLK:2 * BLK, sl]] if qb == 0
                     else [m_ref[0:BLK, sl], m_ref[BLK:2 * BLK, sl], x_ref[:, sl]])
            return jnp.concatenate([ctx_ref[:, sl]] + band3, axis=0)

        def queries(qb, hk):
            return jnp.concatenate(
                [q_ref[qb * BLK:(qb + 1) * BLK, (hk * GQA_GROUP + g) * HEAD_DIM:(hk * GQA_GROUP + g + 1) * HEAD_DIM]
                 for g in range(GQA_GROUP)], axis=0)

        s = [_masked(lax.dot_general(queries(qb, hk), keys(kc_ref, kp_ref, km_ref, kx_ref, qb, hk), NT,
                                     preferred_element_type=F32), bands[qb], C) for qb, hk in chains]
        for (qb, hk), s_ in zip(chains, s):
            sink_c = _sink_col(sink_ref, hk)
            m = jnp.maximum(jnp.max(s_, axis=1, keepdims=True), sink_c)
            p = jnp.exp(s_ - m)
            den = jnp.sum(p, axis=1, keepdims=True) + jnp.exp(sink_c - m)
            o = jnp.dot(p.astype(BF16), keys(vc_ref, vp_ref, vm_ref, vx_ref, qb, hk), preferred_element_type=F32) * (1.0 / den)
            lse = m + jnp.log(den)
            rows = slice(qb * BLK, (qb + 1) * BLK)
            for g in range(GQA_GROUP):
                h = hk * GQA_GROUP + g
                ya_ref[rows, h * HEAD_DIM:(h + 1) * HEAD_DIM] = o[g * BLK:(g + 1) * BLK, :].astype(BF16)
                lse_ref[rows, h:h + 1] = lse[g * BLK:(g + 1) * BLK, :]

    kv3 = [pl.BlockSpec((BLK, KV_W), lambda i: (jnp.maximum(2 * i - 1, 0), 0)),
           pl.BlockSpec((2 * BLK, KV_W), lambda i: (i, 0)),
           pl.BlockSpec((BLK, KV_W), lambda i: (jnp.minimum(2 * i + 2, nb - 1), 0))]
    return _call(
        body, name="fwd_attn", grid=(steps,),
        in_specs=[pl.BlockSpec(memory_space=pltpu.SMEM), _row(2 * BLK, Q_W)] + kv3 + kv3
                 + [_full((C, KV_W)), _full((C, KV_W)), _full((3, GQA_GROUP * BLK, 3 * BLK))],
        out_specs=[_row(2 * BLK, Q_W), _row(2 * BLK, N_Q_HEADS)],
        out_shape=[_sds((L, Q_W), BF16), _sds((L, N_Q_HEADS), F32)],
        args=(sink, q, k, k, k, v, v, v, kc, vc, bias), comm=comm)


GMLP_CHUNKS = 4


def _split_pair(t):
    low = lax.broadcasted_iota(jnp.int32, t.shape, 1) < GROUP_DIM
    zero = jnp.zeros_like(t)
    return jnp.where(low, t, zero), jnp.where(low, zero, t)


def _gmlp_spatial(w_ref, t_b, nch):
    rows = []
    for c in range(nch):
        tiles = []
        for pr in range(N_GROUPS // 2):
            lo, hi = _split_pair(t_b[c * BLK:(c + 1) * BLK, pr * 128:(pr + 1) * 128])
            tiles.append(jnp.dot(w_ref[2 * pr], lo, preferred_element_type=F32)
                         + jnp.dot(w_ref[2 * pr + 1], hi, preferred_element_type=F32))
        rows.append(jnp.concatenate(tiles, axis=1))
    return jnp.concatenate(rows, axis=0)


def _gmlp_fwd_vals(u, vb, lnv_ref, ws_ref, bsp_ref, nch):
    uf = u.astype(F32)
    vf = vb.astype(F32)
    gu, tu = _gelu(uf)
    gv, tv = _gelu(vf)
    vhat, rstd = _ln(gv)
    vn = (vhat * lnv_ref[0:1, :] + lnv_ref[1:2, :]).astype(BF16)
    s = _gmlp_spatial(ws_ref, vn, nch) + jnp.concatenate([bsp_ref[...]] * nch, axis=0)
    return uf, vf, gu, tu, tv, vhat, rstd, vn, s


def _k_gmlp(u, vb, lnv, ws, bsp):
    L = u.shape[0]
    nch = min(GMLP_CHUNKS, L // BLK)
    tm = nch * BLK

    def body(u_ref, vb_ref, lnv_ref, ws_ref, bsp_ref, yb_ref):
        _, _, gu, _, _, _, _, _, s = _gmlp_fwd_vals(u_ref[...], vb_ref[...], lnv_ref, ws_ref, bsp_ref, nch)
        yb_ref[...] = (gu * s).astype(BF16)

    return pl.pallas_call(
        body, name="fwd_gmlp", grid=(L // tm,),
        in_specs=[_row(tm, GM_W), _row(tm, GM_W), _full((8, GM_W)), _full((N_GROUPS, BLK, BLK)), _full((BLK, GM_W))],
        out_specs=_row(tm, GM_W),
        out_shape=_sds((L, GM_W), BF16),
        compiler_params=_params(("arbitrary",)),
    )(u, vb, lnv, ws, bsp)


def _k_merge(x, ya, yb, ga, gb, w_a, w_b, w_o, modv, lnv, tm):
    L = x.shape[0]

    def body(x_ref, ya_ref, yb_ref, ga_ref, gb_ref, wa_ref, wb_ref, wo_ref, mod_ref, ln_ref,
             mg_ref, mix_ref, xm_ref, h2_ref, a_ref, b_ref):
        a = jnp.dot(ya_ref[...], wa_ref[...], preferred_element_type=F32)
        b = jnp.dot(yb_ref[...], wb_ref[...], preferred_element_type=F32)
        a_ref[...] = a.astype(BF16)
        b_ref[...] = b.astype(BF16)
        merged =(_sigmoid(ga_ref[...].astype(F32)) * a + _sigmoid(gb_ref[...].astype(F32)) * b).astype(BF16)
        mg_ref[...] = merged
        mix = jnp.dot(merged, wo_ref[...], preferred_element_type=F32)
        mix_ref[...] = mix.astype(BF16)
        r1 = ALPHA * x_ref[...] + mod_ref[2:3, :] * mix
        r1hat, _ = _ln(r1)
        xm = r1hat * ln_ref[0:1, :] + ln_ref[1:2, :]
        xm_ref[...] = xm
        n2, _ = _ln(xm)
        h2_ref[...] = (n2 * (1.0 + mod_ref[4:5, :]) + mod_ref[3:4, :]).astype(BF16)

    return pl.pallas_call(
        body, name="fwd_merge", grid=(L // tm,),
        in_specs=[_row(tm, D), _row(tm, Q_W), _row(tm, GM_W), _row(tm, D), _row(tm, D),
                  _resident((Q_W, D)), _resident((GM_W, D)), _resident((D, D)), _full((8, D)), _full((8, D))],
        out_specs=[_row(tm, D)] * 6,
        out_shape=[_sds((L, D), BF16), _sds((L, D), BF16), _sds((L, D), F32), _sds((L, D), BF16),
                   _sds((L, D), BF16), _sds((L, D), BF16)],
        compiler_params=_params(("arbitrary",)),
    )(x, ya, yb, ga, gb, w_a, w_b, w_o, modv, lnv)


FFN_CH = FFN_H // 2


def _k_ffn(h2, xm, tgt, w_fi, w_fo, modv, lnv, tm):
    L = h2.shape[0]

    def body(h2_ref, xm_ref, t_ref, wi_ref, wo_ref, mod_ref, ln_ref, gate_ref, up_ref, a_ref, dr2_ref, df_ref, acc_ref):
        @pl.when(pl.program_id(0) == 0)
        def _():
            acc_ref[...] = jnp.zeros_like(acc_ref)

        h2v = h2_ref[...]
        f = jnp.zeros((tm, D), F32)
        for j in range(FFN_H // FFN_CH):
            lo = j * FFN_CH
            gate = lax.dot_general(h2v, wi_ref[lo:lo + FFN_CH, :], NT, preferred_element_type=F32)
            up = lax.dot_general(h2v, wi_ref[FFN_H + lo:FFN_H + lo + FFN_CH, :], NT, preferred_element_type=F32)
            act = (gate * _sigmoid(gate) * up).astype(BF16)
            gate_ref[:, lo:lo + FFN_CH] = gate.astype(BF16)
            up_ref[:, lo:lo + FFN_CH] = up.astype(BF16)
            a_ref[:, lo:lo + FFN_CH] = act
            f = f + jnp.dot(act, wo_ref[lo:lo + FFN_CH, :], preferred_element_type=F32)
        gate2 = mod_ref[5:6, :]
        r2 = ALPHA * xm_ref[...] + gate2 * f
        r2hat, rstd = _ln(r2)
        y = r2hat * ln_ref[2:3, :] + ln_ref[3:4, :]
        err = y - t_ref[...]
        dy = err * (1.0 / D)
        dr2 = _ln_bwd(dy * ln_ref[2:3, :], r2hat, rstd)
        dr2_ref[...] = dr2
        df_ref[...] = (gate2 * dr2).astype(BF16)
        acc_ref[0:1, :] += _colsum(dy * r2hat)
        acc_ref[1:2, :] += _colsum(dy)
        acc_ref[2:3, :] += _colsum(dr2 * f)
        acc_ref[3:4, :] += _colsum(err * err) * (0.5 / D)

    return pl.pallas_call(
        body, name="fwd_ffn", grid=(L // tm,),
        in_specs=[_row(tm, D), _row(tm, D), _row(tm, D), _resident((2 * FFN_H, D)), _resident((FFN_H, D)),
                  _full((8, D)), _full((8, D))],
        out_specs=[_row(tm, FFN_H)] * 3 + [_row(tm, D), _row(tm, D), _full((8, D))],
        out_shape=[_sds((L, FFN_H), BF16)] * 3 + [_sds((L, D), F32), _sds((L, D), BF16), _sds((8, D), F32)],
        compiler_params=_params(("arbitrary",)),
    )(h2, xm, tgt, w_fi, w_fo, modv, lnv)


def _k_ffn_bwd(df, gate, up, xm, dr2, x, mix, w_fi, w_fo, modv, lnv, tm):
    L = df.shape[0]

    def body(df_ref, gate_ref, up_ref, xm_ref, dr2_ref, x_ref, mix_ref, wi_ref, wo_ref, mod_ref, ln_ref,
             dF_ref, dmix_ref, dxp_ref, acc_ref):
        @pl.when(pl.program_id(0) == 0)
        def _():
            acc_ref[...] = jnp.zeros_like(acc_ref)

        dfv = df_ref[...]
        chunks = [j * FFN_CH for j in range(FFN_H // FFN_CH)]
        das = [lax.dot_general(dfv, wo_ref[lo:lo + FFN_CH, :], NT, preferred_element_type=F32) for lo in chunks]
        n2, rstd2 = _ln(xm_ref[...])
        mixf = mix_ref[...].astype(F32)
        gate1 = mod_ref[2:3, :]
        r1hat, rstd1 = _ln(ALPHA * x_ref[...] + gate1 * mixf)
        dh2 = jnp.zeros((tm, D), F32)
        for lo, da in zip(chunks, das):
            gate = gate_ref[:, lo:lo + FFN_CH].astype(F32)
            upv = up_ref[:, lo:lo + FFN_CH].astype(F32)
            sg = _sigmoid(gate)
            d_gate = (da * upv * (sg * (1.0 + gate * (1.0 - sg)))).astype(BF16)
            d_up = (da * (gate * sg)).astype(BF16)
            dF_ref[:, lo:lo + FFN_CH] = d_gate
            dF_ref[:, FFN_H + lo:FFN_H + lo + FFN_CH] = d_up
            dh2 = dh2 + jnp.dot(d_gate, wi_ref[lo:lo + FFN_CH, :], preferred_element_type=F32)
            dh2 = dh2 + jnp.dot(d_up, wi_ref[FFN_H + lo:FFN_H + lo + FFN_CH, :], preferred_element_type=F32)
        acc_ref[0:1, :] += _colsum(dh2)
        acc_ref[1:2, :] += _colsum(dh2 * n2)
        dxm = ALPHA * dr2_ref[...] + _ln_bwd(dh2 * (1.0 + mod_ref[4:5, :]), n2, rstd2)
        acc_ref[2:3, :] += _colsum(dxm * r1hat)
        acc_ref[3:4, :] += _colsum(dxm)
        dr1 = _ln_bwd(dxm * ln_ref[0:1, :], r1hat, rstd1)
        dmix_ref[...] = (gate1 * dr1).astype(BF16)
        dxp_ref[...] = ALPHA * dr1
        acc_ref[4:5, :] += _colsum(dr1 * mixf)

    return pl.pallas_call(
        body, name="bwd_ffn", grid=(L // tm,),
        in_specs=[_row(tm, D), _row(tm, FFN_H), _row(tm, FFN_H), _row(tm, D), _row(tm, D), _row(tm, D), _row(tm, D),
                  _resident((2 * FFN_H, D)), _resident((FFN_H, D)), _full((8, D)), _full((8, D))],
        out_specs=[_row(tm, 2 * FFN_H), _row(tm, D), _row(tm, D), _full((8, D))],
        out_shape=[_sds((L, 2 * FFN_H), BF16), _sds((L, D), BF16), _sds((L, D), F32), _sds((8, D), F32)],
        compiler_params=_params(("arbitrary",)),
    )(df, gate, up, xm, dr2, x, mix, w_fi, w_fo, modv, lnv)


def _k_merge_bwd(dmix, a_br, b_br, ga, gb, w_a, w_b, w_o, tm, comm=None):
    L = dmix.shape[0]

    def body(dmix_ref, a_ref, b_ref, ga_ref, gb_ref, wa_ref, wb_ref, wo_ref,
             dA_ref, dB_ref, dga_ref, dgb_ref, dya_ref, dyb_ref):
        dmg = lax.dot_general(dmix_ref[...], wo_ref[...], NT, preferred_element_type=F32)
        a = a_ref[...].astype(F32)
        sa = _sigmoid(ga_ref[...].astype(F32))
        dA = (dmg * sa).astype(BF16)
        dA_ref[...] = dA
        dga_ref[...] = (dmg * a * (sa * (1.0 - sa))).astype(BF16)
        dya_ref[...] = lax.dot_general(dA, wa_ref[...], NT, preferred_element_type=F32).astype(BF16)
        b = b_ref[...].astype(F32)
        sb = _sigmoid(gb_ref[...].astype(F32))
        dB = (dmg * sb).astype(BF16)
        dB_ref[...] = dB
        dgb_ref[...] = (dmg * b * (sb * (1.0 - sb))).astype(BF16)
        dyb_ref[...] = lax.dot_general(dB, wb_ref[...], NT, preferred_element_type=F32).astype(BF16)

    return _call(
        body, name="bwd_merge", grid=(L // tm,),
        in_specs=[_row(tm, D)] * 5 + [_resident((Q_W, D)), _resident((GM_W, D)), _resident((D, D))],
        out_specs=[_row(tm, D)] * 4 + [_row(tm, Q_W), _row(tm, GM_W)],
        out_shape=[_sds((L, D), BF16)] * 4 + [_sds((L, Q_W), BF16), _sds((L, GM_W), BF16)],
        args=(dmix, a_br, b_br, ga, gb, w_a, w_b, w_o), comm=comm)


def _k_gmlp_bwd(u, vb, dyb, lnv, ws, wst, bsp):
    L = u.shape[0]
    nch = min(GMLP_CHUNKS, L // BLK)
    tm = nch * BLK

    def body(u_ref, vb_ref, dyb_ref, lnv_ref, ws_ref, wst_ref, bsp_ref, du_ref, dvb_ref, gws_ref, gbst_ref, gln_ref):
        @pl.when(pl.program_id(0) == 0)
        def _():
            gws_ref[...] = jnp.zeros_like(gws_ref)
            gbst_ref[...] = jnp.zeros_like(gbst_ref)
            gln_ref[...] = jnp.zeros_like(gln_ref)

        uf, vf, gu, tu, tv, vhat, rstd, vn, s = _gmlp_fwd_vals(u_ref[...], vb_ref[...], lnv_ref, ws_ref, bsp_ref, nch)
        dyb_f = dyb_ref[...].astype(F32)
        du_ref[...] = (dyb_f * s * _gelu_grad(uf, tu)).astype(BF16)
        ds = dyb_f * gu
        ds_b = ds.astype(BF16)
        for pr in range(N_GROUPS // 2):
            lanes = slice(pr * 128, (pr + 1) * 128)
            gw_lo = gw_hi = ds_sum = None
            for c in range(nch):
                rows = slice(c * BLK, (c + 1) * BLK)
                lo, hi = _split_pair(ds_b[rows, lanes])
                t_lo = lax.dot_general(lo, vn[rows, lanes], NT, preferred_element_type=F32)
                t_hi = lax.dot_general(hi, vn[rows, lanes], NT, preferred_element_type=F32)
                gw_lo = t_lo if c == 0 else gw_lo + t_lo
                gw_hi = t_hi if c == 0 else gw_hi + t_hi
                ds_sum = ds[rows, lanes] if c == 0 else ds_sum + ds[rows, lanes]
            gws_ref[2 * pr] += gw_lo
            gws_ref[2 * pr + 1] += gw_hi
            b_lo, b_hi = _split_pair(ds_sum)
            gbst_ref[:, 2 * pr:2 * pr + 1] += jnp.sum(b_lo, axis=1, keepdims=True)
            gbst_ref[:, 2 * pr + 1:2 * pr + 2] += jnp.sum(b_hi, axis=1, keepdims=True)
        dvn = _gmlp_spatial(wst_ref, ds_b, nch)
        gln_ref[0:1, :] += _colsum(dvn * vhat)
        gln_ref[1:2, :] += _colsum(dvn)
        dgv = _ln_bwd(dvn * lnv_ref[0:1, :], vhat, rstd)
        dvb_ref[...] = (dgv * _gelu_grad(vf, tv)).astype(BF16)

    return pl.pallas_call(
        body, name="bwd_gmlp", grid=(L // tm,),
        in_specs=[_row(tm, GM_W)] * 3 + [_full((8, GM_W)), _full((N_GROUPS, BLK, BLK)), _full((N_GROUPS, BLK, BLK)),
                                         _full((BLK, GM_W))],
        out_specs=[_row(tm, GM_W), _row(tm, GM_W), _full((N_GROUPS, BLK, BLK)), _full((BLK, N_GROUPS)), _full((8, GM_W))],
        out_shape=[_sds((L, GM_W), BF16), _sds((L, GM_W), BF16), _sds((N_GROUPS, BLK, BLK), F32),
                   _sds((BLK, N_GROUPS), F32), _sds((8, GM_W), F32)],
        compiler_params=_params(("arbitrary",)),
    )(u, vb, dyb, lnv, ws, wst, bsp)


def _k_attn_bwd(sink, q, k, v, kc, vc, dya, lse, cos, sin, bias, comm=None):
    L = q.shape[0]
    C = kc.shape[0]
    nb = L // BLK
    steps = nb // 2
    NK = C + 3 * BLK
    chains = [(qb, hk) for qb in range(2) for hk in range(N_KV_HEADS)]

    def body(sink_ref, q_ref, kp_ref, km_ref, kx_ref, vp_ref, vm_ref, vx_ref, kc_ref, vc_ref, do_ref, lse_ref,
             cq_ref, sq_ref, cl_ref, sl_ref, bias_ref,
             dq_ref, dk_ref, dv_ref, dkc_ref, dvc_ref, dsink_ref,
             dq_scr, ck_scr, cv_scr, k1_acc, k2_acc, v1_acc, v2_acc):
        i = pl.program_id(0)

        @pl.when(i == 0)
        def _():
            for r in (k1_acc, k2_acc, v1_acc, v2_acc, dkc_ref, dvc_ref, dsink_ref):
                r[...] = jnp.zeros_like(r)

        @pl.when(i < steps)
        def _():
            bands = [bias_ref[jnp.where(i == 0, 0, 1)], bias_ref[jnp.where(i == steps - 1, 2, 1)]]

            def lanes(hk):
                return slice(hk * HEAD_DIM, (hk + 1) * HEAD_DIM)

            def keys(ctx_ref, p_ref, m_ref, x_ref, qb, hk):
                sl = lanes(hk)
                band3 = ([p_ref[:, sl], m_ref[0:BLK, sl], m_ref[BLK:2 * BLK, sl]] if qb == 0
                         else [m_ref[0:BLK, sl], m_ref[BLK:2 * BLK, sl], x_ref[:, sl]])
                return jnp.concatenate([ctx_ref[:, sl]] + band3, axis=0)

            def stacked(ref, qb, hk, width):
                return jnp.concatenate(
                    [ref[qb * BLK:(qb + 1) * BLK, (hk * GQA_GROUP + g) * width:(hk * GQA_GROUP + g + 1) * width]
                     for g in range(GQA_GROUP)], axis=0)

            def scores(qb, hk):
                kcat = keys(kc_ref, kp_ref, km_ref, kx_ref, qb, hk)
                qg = stacked(q_ref, qb, hk, HEAD_DIM)
                s = _masked(lax.dot_general(qg, kcat, NT, preferred_element_type=F32), bands[qb], C)
                dog = stacked(do_ref, qb, hk, HEAD_DIM)
                dp = lax.dot_general(dog, keys(vc_ref, vp_ref, vm_ref, vx_ref, qb, hk), NT, preferred_element_type=F32)
                return kcat, qg, dog, s, dp

            def softmax_bwd(qb, hk, s, dp):
                lse_c = stacked(lse_ref, qb, hk, 1)
                p = jnp.exp(s - lse_c)
                delta = jnp.sum(p * dp, axis=1, keepdims=True)
                ds = (p * (dp - delta)).astype(BF16)
                p_sink = jnp.exp(_sink_col(sink_ref, hk) - lse_c) * delta
                return p.astype(BF16), ds, p_sink

            def put_dq(qb, hk, dqs, p_sink):
                for g in range(GQA_GROUP):
                    h = hk * GQA_GROUP + g
                    dq_scr[qb * BLK:(qb + 1) * BLK, h * HEAD_DIM:(h + 1) * HEAD_DIM] = dqs[g * BLK:(g + 1) * BLK, :]
                    tot = jnp.sum(p_sink[g * BLK:(g + 1) * BLK, :], axis=0, keepdims=True)
                    dsink_ref[h:h + 1, :] -= jnp.broadcast_to(tot, (1, 128))

            sc = [scores(qb, hk) for qb, hk in chains]
            pending = None
            for (qb, hk), (kcat, qg, dog, s, dp) in zip(chains, sc):
                pb, ds, p_sink = softmax_bwd(qb, hk, s, dp)
                if pending is not None:
                    pqb, phk, pds, ppb, pqg, pdog = pending
                    ck_scr[pqb, :, lanes(phk)] = lax.dot_general(pds, pqg, TN, preferred_element_type=F32)
                    cv_scr[pqb, :, lanes(phk)] = lax.dot_general(ppb, pdog, TN, preferred_element_type=F32)
                put_dq(qb, hk, jnp.dot(ds, kcat, preferred_element_type=F32), p_sink)
                pending = (qb, hk, ds, pb, qg, dog)
            pqb, phk, pds, ppb, pqg, pdog = pending
            ck_scr[pqb, :, lanes(phk)] = lax.dot_general(pds, pqg, TN, preferred_element_type=F32)
            cq, sq = cq_ref[...], sq_ref[...]
            for j in range(4):
                dq_ref[:, j * 128:(j + 1) * 128] = _unrope(dq_scr[:, j * 128:(j + 1) * 128] * Q_SCALE, cq, sq).astype(BF16)
            cv_scr[pqb, :, lanes(phk)] = lax.dot_general(ppb, pdog, TN, preferred_element_type=F32)
            dkc_ref[...] += ck_scr[0, 0:C, :] + ck_scr[1, 0:C, :]
            dvc_ref[...] += cv_scr[0, 0:C, :] + cv_scr[1, 0:C, :]

        @pl.when(i >= steps)
        def _():
            ck_scr[...] = jnp.zeros_like(ck_scr)
            cv_scr[...] = jnp.zeros_like(cv_scr)

        def part(scr, qb, j):
            return scr[qb, C + j * BLK:C + (j + 1) * BLK, :]

        dk_ref[0:BLK, :] = _unrope(k1_acc[...] + part(ck_scr, 0, 0), cl_ref[...], sl_ref[...]).astype(BF16)
        dk_ref[BLK:2 * BLK, :] = _unrope(k2_acc[...] + part(ck_scr, 0, 1) + part(ck_scr, 1, 0),
                                         cq_ref[0:BLK, :], sq_ref[0:BLK, :]).astype(BF16)
        dv_ref[0:BLK, :] = (v1_acc[...] + part(cv_scr, 0, 0)).astype(BF16)
        dv_ref[BLK:2 * BLK, :] = (v2_acc[...] + part(cv_scr, 0, 1) + part(cv_scr, 1, 0)).astype(BF16)
        k1_acc[...] = part(ck_scr, 0, 2) + part(ck_scr, 1, 1)
        v1_acc[...] = part(cv_scr, 0, 2) + part(cv_scr, 1, 1)
        k2_acc[...] = part(ck_scr, 1, 2)
        v2_acc[...] = part(cv_scr, 1, 2)

    last = steps - 1
    kv3 = [pl.BlockSpec((BLK, KV_W), lambda i: (jnp.clip(2 * i - 1, 0, nb - 1), 0)),
           pl.BlockSpec((2 * BLK, KV_W), lambda i: (jnp.minimum(i, last), 0)),
           pl.BlockSpec((BLK, KV_W), lambda i: (jnp.minimum(2 * i + 2, nb - 1), 0))]
    cur = lambda w: pl.BlockSpec((2 * BLK, w), lambda i: (jnp.minimum(i, last), 0))
    late = lambda w: pl.BlockSpec((BLK, w), lambda i: (jnp.clip(2 * i - 1, 0, nb - 1), 0))
    out2 = lambda w: pl.BlockSpec((2 * BLK, w), lambda i: (i, 0))
    return _call(
        body, name="bwd_attn", grid=(steps + 1,),
        in_specs=[pl.BlockSpec(memory_space=pltpu.SMEM), cur(Q_W)] + kv3 + kv3
                 + [_full((C, KV_W)), _full((C, KV_W)), cur(Q_W), cur(N_Q_HEADS), cur(128), cur(128), late(128), late(128),
                    _full((3, GQA_GROUP * BLK, 3 * BLK))],
        out_specs=[cur(Q_W), out2(KV_W), out2(KV_W), _full((C, KV_W)), _full((C, KV_W)), _full((8, 128))],
        out_shape=[_sds((L, Q_W), BF16), _sds((L + 2 * BLK, KV_W), BF16), _sds((L + 2 * BLK, KV_W), BF16),
                   _sds((C, KV_W), F32), _sds((C, KV_W), F32), _sds((8, 128), F32)],
        scratch=[pltpu.VMEM((2 * BLK, Q_W), F32), pltpu.VMEM((2, NK, KV_W), F32), pltpu.VMEM((2, NK, KV_W), F32)]
                + [pltpu.VMEM((BLK, KV_W), F32)] * 4,
        args=(sink, q, k, k, k, v, v, v, kc, vc, dya, lse, cos, sin, cos, sin, bias), comm=comm)


def _k_ctx_bwd(ctx, modc, hc, dkc, dvc, w_kv):
    C = ctx.shape[0]

    def body(c_ref, mod_ref, hc_ref, dkc_ref, dvc_ref, w_ref, gw_ref, dmod_ref):
        dkv = jnp.concatenate([dkc_ref[...], dvc_ref[...]], axis=1).astype(BF16)
        gw_ref[...] = lax.dot_general(dkv, hc_ref[...], TN, preferred_element_type=F32)
        dhc = jnp.dot(dkv, w_ref[...], preferred_element_type=F32)
        n, _ = _ln(c_ref[...])
        dmod_ref[...] = jnp.zeros_like(dmod_ref)
        dmod_ref[0:1, :] = _colsum(dhc)
        dmod_ref[1:2, :] = _colsum(dhc * n)

    return pl.pallas_call(
        body, name="bwd_ctx", grid=(1,),
        in_specs=[_full((C, D)), _full((8, D)), _full((C, D)), _full((C, KV_W)), _full((C, KV_W)), _full((2 * KV_W, D))],
        out_specs=[_full((2 * KV_W, D)), _full((8, D))],
        out_shape=[_sds((2 * KV_W, D), F32), _sds((8, D), F32)],
        compiler_params=_params(("arbitrary",)),
    )(ctx, modc, hc, dkc, dvc, w_kv)


def _k_in_bwd(dq, dk, dv, du, dvb, dga, dgb, x, dxp, w_in, modv, tm, comm=None):
    L = x.shape[0]
    parts = [(O_Q, Q_W), (O_K, KV_W), (O_V, KV_W), (O_U, GM_W), (O_VB, GM_W), (O_GA, D), (O_GB, D)]

    def body(dq_ref, dk_ref, dv_ref, du_ref, dvb_ref, dga_ref, dgb_ref, x_ref, dxp_ref, w_ref, mod_ref,
             dP_ref, gx_ref, acc_ref):
        @pl.when(pl.program_id(0) == 0)
        def _():
            acc_ref[...] = jnp.zeros_like(acc_ref)

        for (lo, width), r in zip(parts, (dq_ref, dk_ref, dv_ref, du_ref, dvb_ref, dga_ref, dgb_ref)):
            dP_ref[:, lo:lo + width] = r[...]
        n1, rstd1 = _ln(x_ref[...])
        dh = jnp.dot(dP_ref[...], w_ref[...], preferred_element_type=F32)
        acc_ref[0:1, :] += _colsum(dh)
        acc_ref[1:2, :] += _colsum(dh * n1)
        gx_ref[...] = dxp_ref[...] + _ln_bwd(dh * (1.0 + mod_ref[1:2, :]), n1, rstd1)

    return _call(
        body, name="bwd_in", grid=(L // tm,),
        in_specs=[_row(tm, w) for _, w in parts] + [_row(tm, D), _row(tm, D), _resident((IN_W, D)), _full((8, D))],
        out_specs=[_row(tm, IN_W), _row(tm, D), _full((8, D))],
        out_shape=[_sds((L, IN_W), BF16), _sds((L, D), F32), _sds((8, D), F32)],
        args=(dq, dk, dv, du, dvb, dga, dgb, x, dxp, w_in, modv), comm=comm)


def _wgrad(a, b, name, tk, tt, comm=None, extra=None):
    T, K = a.shape
    N = b.shape[1]
    nt = T // tt

    def body(*refs):
        a_ref, b_ref = refs[:2]
        o_ref, acc_ref = refs[-2:]
        j, t = pl.program_id(0), pl.program_id(1)

        @pl.when(t == 0)
        def _():
            acc_ref[...] = jnp.zeros_like(acc_ref)

        acc_ref[...] += lax.dot_general(a_ref[...], b_ref[...], TN, preferred_element_type=F32)

        if extra is not None:
            lo, rows = extra[0] % tk, extra[1].shape[0]

            @pl.when((t == nt - 1) & (j == extra[0] // tk))
            def _():
                acc_ref[lo:lo + rows, :] += refs[2][...]

        @pl.when(t == nt - 1)
        def _():
            o_ref[...] = acc_ref[...].astype(BF16)

    extra_specs = [] if extra is None else [pl.BlockSpec(extra[1].shape, lambda j, t: (0, 0))]
    (out,), got = _call(
        body, name=name, grid=(K // tk, nt),
        in_specs=[pl.BlockSpec((tt, tk), lambda j, t: (t, j)), pl.BlockSpec((tt, N), lambda j, t: (t, 0))] + extra_specs,
        out_specs=[pl.BlockSpec((tk, N), lambda j, t: (j, 0))],
        out_shape=[_sds((K, N), BF16)],
        scratch=[pltpu.VMEM((tk, N), F32)],
        args=(a, b) + (() if extra is None else (extra[1],)), comm=comm)
    return (out, got) if comm is not None else out


def _adamw_reduce(parts, w, m, v, name, tr):
    R, C = w.shape
    n_parts = parts.shape[0]

    def body(p_ref, w_ref, m_ref, v_ref, g_ref, d_ref, m2_ref, v2_ref):
        g = p_ref[0].astype(F32)
        for i in range(1, n_parts):
            g = g + p_ref[i].astype(F32)
        delta, m2, v2 = _adamw(w_ref[...], g, m_ref[...], v_ref[...])
        g_ref[...] = g
        d_ref[...] = delta
        m2_ref[...] = m2
        v2_ref[...] = v2

    spec = _row(tr, C)
    return pl.pallas_call(
        body, name=name, grid=(R // tr,),
        in_specs=[pl.BlockSpec((n_parts, tr, C), lambda i: (0, i, 0)), spec, spec, spec],
        out_specs=[spec] * 4,
        out_shape=[_sds((R, C), F32)] * 4,
        compiler_params=_params(("arbitrary",)),
    )(parts, w, m, v)


SMALL_ORDER = ("b_ada", "ln1_g", "ln1_b", "ln2_g", "ln2_b", "gmlp_ln_g", "gmlp_ln_b", "b_spatial", "attn_sink")


def _small_step(gath, params):
    flat = [a for name in SMALL_ORDER for a in params[name]]

    def grad_of(tot, name):
        if name == "b_ada":
            return jnp.concatenate([tot[r:r + 1, :] for r in range(6)], axis=1)
        if name in ("ln1_g", "ln1_b", "ln2_g", "ln2_b"):
            r = 8 + ("ln1_g", "ln1_b", "ln2_g", "ln2_b").index(name)
            return tot[r:r + 1, :]
        if name == "gmlp_ln_g":
            return tot[12:13, :GM_W]
        if name == "gmlp_ln_b":
            return tot[12:13, GM_W:]
        if name == "b_spatial":
            return jnp.concatenate([tot[13:14, g * BLK:(g + 1) * BLK] for g in range(N_GROUPS)], axis=0)[None]
        return tot[14:15, :N_Q_HEADS]

    def body(*refs):
        g_ref, in_refs = refs[0], refs[1:1 + len(flat)]
        tot_ref, out_refs = refs[1 + len(flat)], refs[2 + len(flat):]
        tot = g_ref[0]
        for i in range(1, N_DEV):
            tot = tot + g_ref[i]
        tot_ref[...] = tot
        tot_ref[0:2, :] = tot[0:2, :] + tot[6:8, :]
        tot_ref[15:16, :] = jnp.broadcast_to(jnp.sum(tot[15:16, :], axis=1, keepdims=True), (1, D))
        tot = tot_ref[...]
        for k, name in enumerate(SMALL_ORDER):
            w_ref, m_ref, v_ref = in_refs[3 * k:3 * k + 3]
            g = grad_of(tot, name)
            delta, m2, v2 = _adamw(w_ref[...], g, m_ref[...], v_ref[...])
            for r, val in zip(out_refs[4 * k:4 * k + 4], (g, delta, m2, v2)):
                r[...] = val

    res = pl.pallas_call(
        body, name="small_step", grid=(1,),
        in_specs=[_full((N_DEV, 16, D))] + [_full(a.shape) for a in flat],
        out_specs=[_full((16, D))] + [_full(params[name][0].shape) for name in SMALL_ORDER for _ in range(4)],
        out_shape=[_sds((16, D), F32)] + [_sds(params[name][0].shape, F32) for name in SMALL_ORDER for _ in range(4)],
        compiler_params=_params(("arbitrary",)),
    )(gath, *flat)
    return res[0], {name: res[1 + 4 * k:5 + 4 * k] for k, name in enumerate(SMALL_ORDER)}


def _cctx_finish(gath, c_ctx, m, v):
    def body(g_ref, c_ref, m_ref, v_ref, gr_ref, d_ref, m2_ref, v2_ref):
        ds = g_ref[0]
        for i in range(1, N_DEV):
            ds = ds + g_ref[i]
        c = c_ref[...]
        sg = _sigmoid(c)
        g = ds * (sg * (1.0 + c * (1.0 - sg)))
        delta, m2, v2 = _adamw(c, g, m_ref[...], v_ref[...])
        gr_ref[...] = g
        d_ref[...] = delta
        m2_ref[...] = m2
        v2_ref[...] = v2

    return pl.pallas_call(
        body, name="cctx_finish", grid=(1,),
        in_specs=[_full((N_DEV, 8, D))] + [_full((8, D))] * 3, out_specs=[_full((8, D))] * 4,
        out_shape=[_sds((8, D), F32)] * 4,
        compiler_params=_params(("arbitrary",)),
    )(gath, c_ctx, m, v)


def _pad_rows(a, rows):
    return jnp.concatenate([a, jnp.zeros((rows - a.shape[0], a.shape[1]), a.dtype)], axis=0)


def kernel(x, c, ctx, c_ctx, w_ada, b_ada, w_in, attn_sink, gmlp_ln_g, gmlp_ln_b, w_spatial, b_spatial, w_branch_a, w_branch_b, w_out, ln1_g, ln1_b, w_ffn_in, w_ffn_out, ln2_g, ln2_b, loss_target, m_c_ctx, m_w_ada, m_b_ada, m_w_in, m_attn_sink, m_gmlp_ln_g, m_gmlp_ln_b, m_w_spatial, m_b_spatial, m_w_branch_a, m_w_branch_b, m_w_out, m_ln1_g, m_ln1_b, m_w_ffn_in, m_w_ffn_out, m_ln2_g, m_ln2_b, v_c_ctx, v_w_ada, v_b_ada, v_w_in, v_attn_sink, v_gmlp_ln_g, v_gmlp_ln_b, v_w_spatial, v_b_spatial, v_w_branch_a, v_w_branch_b, v_w_out, v_ln1_g, v_ln1_b, v_w_ffn_in, v_w_ffn_out, v_ln2_g, v_ln2_b):
    L = x.shape[1]
    me = 4 * lax.axis_index("x") + 2 * lax.axis_index("y") + lax.axis_index("c")
    x2, tgt, ctx2 = x[0], loss_target[0], ctx[0]
    tiles = _Tiles(L)
    tm_in, tm, tt = tiles.wide, tiles.narrow, tiles.tokens

    transposed = ("w_in", "w_ffn_in")
    tr = lambda kname, a: a.T if kname in transposed else a
    big = dict(w_in=w_in[0].T, w_branch_a=w_branch_a[0], w_branch_b=w_branch_b[0], w_out=w_out[0],
               w_ffn_in=w_ffn_in[0].T, w_ffn_out=w_ffn_out[0])
    col_sharded = ("w_branch_a", "w_branch_b")
    shard_bf = {k: a.astype(BF16) for k, a in big.items()}

    def assemble(kname, g):
        if kname in col_sharded:
            return g.transpose(1, 0, 2).reshape(g.shape[1], N_DEV * g.shape[2])
        return g.reshape(N_DEV * g.shape[1], g.shape[2])

    def to_blocks(kname, g):
        if kname in col_sharded:
            return g.reshape(g.shape[0], N_DEV, g.shape[1] // N_DEV).transpose(1, 0, 2)
        return g.reshape(N_DEV, g.shape[0] // N_DEV, g.shape[1])

    full = {}
    n_ada = w_ada.shape[2]
    b_my = lax.dynamic_slice(b_ada, (0, me * n_ada), (1, n_ada))
    act, mod_all, got = _prologue(_pad_rows(c, 8), _pad_rows(c_ctx[None, :], 8), w_ada[0], b_my,
                                  _Comm(gather=[shard_bf["w_in"]]))
    full["w_in"] = assemble("w_in", got[0])
    mod_all = mod_all.transpose(1, 0, 2).reshape(16, 6 * D)
    modv = _pad_rows(lax.dynamic_slice(mod_all, (me, 0), (1, 6 * D)).reshape(6, D), 8)
    modc = _pad_rows(mod_all[8].reshape(6, D), 8)

    lnv = _pad_rows(jnp.concatenate([ln1_g, ln1_b, ln2_g, ln2_b], axis=0), 8)
    gm_lnv = _pad_rows(jnp.concatenate([gmlp_ln_g, gmlp_ln_b], axis=0), 8)
    ws_b = w_spatial[0].astype(BF16)
    wst_b = ws_b.transpose(0, 2, 1)
    bsp = jnp.repeat(b_spatial[0].T, GROUP_DIM, axis=1)
    sink = attn_sink[0]
    cos, sin = _rope_tables(L)
    bias = _attn_bias()
    w_kv = full["w_in"][O_K:O_K + 2 * KV_W, :]

    (h, q, k, v, u, vb, ga, gb), got = _k_in(
        x2, modv, full["w_in"], cos, sin, tm_in,
        comm=_Comm(gather=[shard_bf[kname] for kname in ("w_branch_a", "w_branch_b", "w_out", "w_ffn_out")]))
    for kname, g in zip(("w_branch_a", "w_branch_b", "w_out", "w_ffn_out"), got):
        full[kname] = assemble(kname, g)
    hc, kc, vc = _k_ctx(ctx2, modc, w_kv)
    (ya, lse), got = _k_attn(sink, q, k, v, kc, vc, bias, comm=_Comm(gather=[shard_bf["w_ffn_in"]]))
    full["w_ffn_in"] = assemble("w_ffn_in", got[0])
    yb = _k_gmlp(u, vb, gm_lnv, ws_b, bsp)
    merged, mix, xm, h2, a_br, b_br = _k_merge(x2, ya, yb, ga, gb, full["w_branch_a"], full["w_branch_b"], full["w_out"], modv, lnv, tm_in)
    gate, up, act_f, dr2, df, acc_f = _k_ffn(h2, xm, tgt, full["w_ffn_in"], full["w_ffn_out"], modv, lnv, tm_in)

    dF, dmix, dxp, acc_b = _k_ffn_bwd(df, gate, up, xm, dr2, x2, mix, full["w_ffn_in"], full["w_ffn_out"], modv, lnv, tm)
    blk_fo = to_blocks("w_ffn_out", _wgrad(act_f, df, "wgrad_ffn_out", tiles.tk_ffn, tt))
    gw_fi, (rcv_fo,) = _wgrad(dF, h2, "wgrad_ffn_in", tiles.tk_ffn, tt, comm=_Comm(scatter=[blk_fo]))
    blk_fi = to_blocks("w_ffn_in", gw_fi)
    (dA, dB, dga, dgb, dya, dyb), _ = _k_merge_bwd(
        dmix, a_br, b_br, ga, gb, full["w_branch_a"], full["w_branch_b"], full["w_out"], tm_in)
    du, dvb, g_ws, g_bst, g_gln = _k_gmlp_bwd(u, vb, dyb, gm_lnv, ws_b, wst_b, bsp)
    (dq, dk_late, dv_late, dkc, dvc, g_sink), (gath_ws, rcv_fi) = _k_attn_bwd(
        sink, q, k, v, kc, vc, dya, lse, cos, sin, bias,
        comm=_Comm(gather=[g_ws.reshape(N_GROUPS * BLK, BLK)], scatter=[blk_fi]))
    dk, dv = dk_late[BLK:BLK + L], dv_late[BLK:BLK + L]
    blk_a = to_blocks("w_branch_a", _wgrad(ya, dA, "wgrad_a", Q_W, tt))
    blk_b = to_blocks("w_branch_b", _wgrad(yb, dB, "wgrad_b", GM_W, tt))
    blk_o = to_blocks("w_out", _wgrad(merged, dmix, "wgrad_out", D, tt))
    (dP, grad_x, acc_i), _ = _k_in_bwd(dq, dk, dv, du, dvb, dga, dgb, x2, dxp, full["w_in"], modv, tm_in)
    g_ctx, dmodc = _k_ctx_bwd(ctx2, modc, hc, dkc, dvc, w_kv)
    gw_in, (rcv_a, rcv_b, rcv_o) = _wgrad(dP, h, "wgrad_in", tiles.tk_in, tt, comm=_Comm(scatter=[blk_a, blk_b, blk_o]),
                                          extra=(O_K, g_ctx))

    dmod_x = jnp.concatenate([acc_i[0:2], acc_b[4:5], acc_b[0:2], acc_f[2:3]], axis=0)
    small = jnp.concatenate([
        dmod_x, dmodc[0:2], acc_b[2:4], acc_f[0:2],
        jnp.concatenate([g_gln[0:1], g_gln[1:2]], axis=1), g_bst.T.reshape(1, D),
        _pad_rows(g_sink[:, 0:1], D).T, acc_f[3:4]], axis=0)
    rcv_in, gath = _exchange_two_level(to_blocks("w_in", gw_in), small, "exchange_last")
    received = dict(w_in=rcv_in, w_branch_a=rcv_a, w_branch_b=rcv_b, w_out=rcv_o, w_ffn_in=rcv_fi, w_ffn_out=rcv_fo)
    moments = dict(w_in=(m_w_in, v_w_in), w_branch_a=(m_w_branch_a, v_w_branch_a), w_branch_b=(m_w_branch_b, v_w_branch_b),
                   w_out=(m_w_out, v_w_out), w_ffn_in=(m_w_ffn_in, v_w_ffn_in), w_ffn_out=(m_w_ffn_out, v_w_ffn_out))
    names = list(big)
    res = {}
    for kname in names:
        mm, vv = moments[kname]
        R = big[kname].shape[0]
        res[kname] = [tr(kname, r) for r in _adamw_reduce(
            received[kname], big[kname], tr(kname, mm[0]), tr(kname, vv[0]), "adamw_" + kname, 256 if R % 256 == 0 else R // 2)]

    ws2d = lambda a: a.reshape(N_GROUPS * BLK, BLK)
    res_ws = [r.reshape(w_spatial.shape) for r in _adamw_reduce(
        gath_ws, ws2d(w_spatial), ws2d(m_w_spatial), ws2d(v_w_spatial), "adamw_w_spatial", 256)]
    tot, res_small = _small_step(gath, dict(
        b_ada=(b_ada, m_b_ada, v_b_ada), ln1_g=(ln1_g, m_ln1_g, v_ln1_g), ln1_b=(ln1_b, m_ln1_b, v_ln1_b),
        ln2_g=(ln2_g, m_ln2_g, v_ln2_g), ln2_b=(ln2_b, m_ln2_b, v_ln2_b),
        gmlp_ln_g=(gmlp_ln_g, m_gmlp_ln_g, v_gmlp_ln_g), gmlp_ln_b=(gmlp_ln_b, m_gmlp_ln_b, v_gmlp_ln_b),
        b_spatial=(b_spatial, m_b_spatial, v_b_spatial), attn_sink=(attn_sink, m_attn_sink, v_attn_sink)))
    loss = tot[15, 0]

    dmod_rows = jnp.concatenate([gath[:, 0:6, :].reshape(N_DEV, 6 * D),
                                 jnp.concatenate([tot[6:8].reshape(1, 2 * D), jnp.zeros((1, 4 * D), F32)], axis=1),
                                 jnp.zeros((7, 6 * D), F32)], axis=0)
    dmod_my = lax.dynamic_slice(dmod_rows, (0, me * n_ada), (16, n_ada))
    g_wada, d_wada, m2_wada, v2_wada, pc = _ada_bwd(act, dmod_my, w_ada[0], m_w_ada[0], v_w_ada[0])
    pc_all = _ag_small(pc, "gather_cctx")
    cc8 = lambda a: _pad_rows(a.reshape(1, D), 8)
    g_cc, d_cc, m2_cc, v2_cc = _cctx_finish(pc_all, cc8(c_ctx), cc8(m_c_ctx), cc8(v_c_ctx))

    order = ["c_ctx", "w_ada", "b_ada", "w_in", "attn_sink", "gmlp_ln_g", "gmlp_ln_b", "w_spatial", "b_spatial",
             "w_branch_a", "w_branch_b", "w_out", "ln1_g", "ln1_b", "w_ffn_in", "w_ffn_out", "ln2_g", "ln2_b"]
    grads, deltas, new_m, new_v = {}, {}, {}, {}
    grads["c_ctx"], deltas["c_ctx"], new_m["c_ctx"], new_v["c_ctx"] = g_cc[0], d_cc[0], m2_cc[0], v2_cc[0]
    grads["w_ada"], deltas["w_ada"], new_m["w_ada"], new_v["w_ada"] = g_wada[None], d_wada[None], m2_wada[None], v2_wada[None]
    for kname in names:
        g, d, m2, v2 = res[kname]
        grads[kname], deltas[kname], new_m[kname], new_v[kname] = g[None], d[None], m2[None], v2[None]
    grads["w_spatial"], deltas["w_spatial"], new_m["w_spatial"], new_v["w_spatial"] = res_ws
    for kname in SMALL_ORDER:
        grads[kname], deltas[kname], new_m[kname], new_v[kname] = res_small[kname]
    return (loss, grad_x[None], *[grads[n] for n in order], *[deltas[n] for n in order],
            *[new_m[n] for n in order], *[new_v[n] for n in order])
```

```python
import functools
import math

import jax
import jax.numpy as jnp
import numpy as np
from jax import lax
from jax.experimental import pallas as pl
from jax.experimental.pallas import tpu as pltpu

F32 = jnp.float32
BF16 = jnp.bfloat16
MESH = pl.DeviceIdType.MESH

N_DEV = 8
D = 1024
HEAD_DIM = 64
N_Q_HEADS = 8
N_KV_HEADS = 2
GQA_GROUP = 4
BLK = 128
Q_W = 512
KV_W = 128
GM_W = 512
N_GROUPS = 8
GROUP_DIM = 64
FFN_H = 2816
IN_W = 3840
O_Q, O_K, O_V, O_U, O_VB, O_GA, O_GB = 0, 512, 640, 768, 1280, 1792, 2816
LN_EPS = 1e-5
NEG_INF = -1e30
ALPHA = 2.0 ** 0.25
ROPE_BASE = 10000.0
ROPE_PAIRS = 16
Q_SCALE = HEAD_DIM ** -0.5
GELU_K0 = math.sqrt(2.0 / math.pi)
GELU_K1 = 0.044715

ADAM_LR = 0.001
ADAM_B1 = 0.9
ADAM_B2 = 0.999
ADAM_EPS = 1e-08
ADAM_WD = 0.01
ADAM_STEP = 10

V7X_VMEM_BYTES = 64 * 1024 * 1024
VMEM_LIMIT = V7X_VMEM_BYTES * 7 // 8
NT = (((1,), (1,)), ((), ()))
TN = (((0,), (0,)), ((), ()))


class _Tiles:
    def __init__(self, L):
        self.wide = min(512, L)
        self.narrow = min(256, L)
        self.tokens = min(2048, L)
        self.tk_in = IN_W // 3
        self.tk_ffn = FFN_H // 2


def _params(sem=None):
    return pltpu.CompilerParams(dimension_semantics=sem, vmem_limit_bytes=VMEM_LIMIT)


def _row(tm, w):
    return pl.BlockSpec((tm, w), lambda i: (i, 0))


def _full(shape):
    nd = len(shape)
    return pl.BlockSpec(shape, lambda i: (0,) * nd)


def _resident(shape):
    nd = len(shape)
    return pl.BlockSpec(shape, lambda i: (0,) * nd, pipeline_mode=pl.Buffered(1))


def _sds(shape, dt):
    return jax.ShapeDtypeStruct(shape, dt)


def _ln(xf):
    mu = jnp.mean(xf, axis=-1, keepdims=True)
    xc = xf - mu
    var = jnp.mean(xc * xc, axis=-1, keepdims=True)
    rstd = lax.rsqrt(var + LN_EPS)
    return xc * rstd, rstd


def _ln_bwd(dn, n, rstd):
    m1 = jnp.mean(dn, axis=-1, keepdims=True)
    m2 = jnp.mean(dn * n, axis=-1, keepdims=True)
    return rstd * (dn - m1 - n * m2)


def _colsum(t):
    return jnp.sum(t, axis=0, keepdims=True)


def _sigmoid(x):
    return 0.5 * jnp.tanh(0.5 * x) + 0.5


def _gelu(x):
    t = jnp.tanh(GELU_K0 * (x + GELU_K1 * (x * x * x)))
    return x * (0.5 * (1.0 + t)), t


def _gelu_grad(x, t):
    return 0.5 * (1.0 + t) + 0.5 * x * (1.0 - t * t) * (GELU_K0 * (1.0 + 3.0 * GELU_K1 * x * x))


def _swap16(t):
    lane = lax.broadcasted_iota(jnp.int32, t.shape, 1)
    return jnp.where((lane & 16) == 0, pltpu.roll(t, 112, 1), pltpu.roll(t, 16, 1))


def _rope(t, cos, sin):
    return t * cos + _swap16(t) * sin


def _unrope(t, cos, sin):
    return t * cos - _swap16(t) * sin


def _adamw(w, g, m, v):
    m2 = ADAM_B1 * m + (1.0 - ADAM_B1) * g
    v2 = ADAM_B2 * v + (1.0 - ADAM_B2) * (g * g)
    m_hat = m2 / (1.0 - ADAM_B1 ** ADAM_STEP)
    v_hat = v2 / (1.0 - ADAM_B2 ** ADAM_STEP)
    delta = -ADAM_LR * (m_hat / (jnp.sqrt(v_hat) + ADAM_EPS) + ADAM_WD * w)
    return delta, m2, v2


def _rope_tables(L):
    inv = (np.float32(ROPE_BASE) ** (-np.arange(ROPE_PAIRS, dtype=np.float32) / np.float32(ROPE_PAIRS))).astype(np.float32)
    t = np.arange(L, dtype=np.int32)
    rows = (t // 64).astype(np.float32)[:, None] * inv
    cols = (t % 64).astype(np.float32)[:, None] * inv
    cr, sr, cc, sc = np.cos(rows), np.sin(rows), np.cos(cols), np.sin(cols)
    cos = np.concatenate([cr, cr, cc, cc], axis=1)
    sin = np.concatenate([-sr, sr, -sc, sc], axis=1)
    return jnp.asarray(np.tile(cos, (1, 2)), F32), jnp.asarray(np.tile(sin, (1, 2)), F32)


def _me():
    return lax.axis_index("x"), lax.axis_index("y"), lax.axis_index("c")


def _peer(mx, my, mc, k):
    return (mx ^ ((k >> 2) & 1), my ^ ((k >> 1) & 1), mc ^ (k & 1))


class _Comm:
    def __init__(self, gather=(), scatter=(), spread=()):
        self.kinds = ["gather"] * len(gather) + ["scatter"] * len(scatter) + ["spread"] * len(spread)
        self.args = list(gather) + list(scatter) + list(spread)
        self.n = len(self.args)

    def out_shape(self):
        return [_sds(a.shape if k == "scatter" else (N_DEV,) + a.shape, a.dtype) for k, a in zip(self.kinds, self.args)]

    def specs(self):
        return [pl.BlockSpec(memory_space=pl.ANY)] * self.n

    def scratch(self):
        return [pltpu.SemaphoreType.DMA((7 * self.n,)), pltpu.SemaphoreType.DMA((7 * self.n,)),
                pltpu.SemaphoreType.DMA((self.n,))]

    def _plan(self, x_refs, out_refs, send_sems, recv_sems, local_sems):
        mx, my, mc = _me()
        me = 4 * mx + 2 * my + mc
        here, sibling = (mx, my, mc), (mx, my, 1 - mc)
        chips = [(1 - mx, my), (mx, 1 - my), (1 - mx, 1 - my)]
        local, first, last = [], [], []
        relay = [[], [], []]
        for a, kind in enumerate(self.kinds):
            x, out = x_refs[a], out_refs[a]

            def rc(k, src, dst, to):
                return pltpu.make_async_remote_copy(
                    src_ref=src, dst_ref=dst, send_sem=send_sems.at[7 * a + k], recv_sem=recv_sems.at[7 * a + k],
                    device_id=to, device_id_type=MESH)

            if kind == "gather":
                local.append(pltpu.make_async_copy(x, out.at[me], local_sems.at[a]))
                first.append(rc(0, x, out.at[me], sibling))
                last.append(rc(0, x, out.at[me ^ 1], here))
                for j, (cx, cy) in enumerate(chips):
                    first.append(rc(1 + j, x, out.at[me], (cx, cy, mc)))
                    landed = out.at[4 * cx + 2 * cy + mc]
                    relay[j].append((rc(1 + j, x, landed, here), rc(4 + j, landed, landed, sibling)))
                    last.append(rc(4 + j, x, out.at[4 * cx + 2 * cy + 1 - mc], here))
            else:
                own = x.at[me] if kind == "scatter" else x
                local.append(pltpu.make_async_copy(own, out.at[me], local_sems.at[a]))
                for k in range(1, N_DEV):
                    src = x.at[me ^ k] if kind == "scatter" else x
                    first.append(rc(k - 1, src, out.at[me], _peer(mx, my, mc, k)))
                    last.append(rc(k - 1, own, out.at[me ^ k], here))
        return local, first, relay[0] + relay[1] + relay[2], last

    def start(self, *refs):
        local, first, _, _ = self._plan(*refs)
        for cp in local + first:
            cp.start()

    def finish(self, *refs):
        local, first, relay, last = self._plan(*refs)
        for arrival, onward in relay:
            arrival.wait_recv()
            onward.start()
        for cp in last:
            cp.wait_recv()
        for cp in first:
            cp.wait_send()
        for _, onward in relay:
            onward.wait_send()
        for cp in local:
            cp.wait()


def _call(body, *, name, grid, in_specs, out_specs, out_shape, args, scratch=(), comm=None, aliases=None):
    params = _params(("arbitrary",) * len(grid))

    def at(end):
        conds = [pl.program_id(d) == (n - 1 if end else 0) for d, n in enumerate(grid)]
        return functools.reduce(lambda p, q: p & q, conds)

    if comm is None:
        res = pl.pallas_call(
            body, name=name, grid=grid, in_specs=list(in_specs), out_specs=list(out_specs), out_shape=list(out_shape),
            scratch_shapes=list(scratch), input_output_aliases=aliases or {}, compiler_params=params)(*args)
        return list(res), []
    n_in, n_out, n_scr, cn = len(in_specs), len(out_specs), len(scratch), comm.n

    def hosted(*refs):
        ins, refs = refs[:n_in], refs[n_in:]
        cins, refs = refs[:cn], refs[cn:]
        outs, refs = refs[:n_out], refs[n_out:]
        couts, refs = refs[:cn], refs[cn:]
        scr, sems = refs[:n_scr], refs[n_scr:]

        @pl.when(at(False))
        def _():
            comm.start(cins, couts, *sems)

        body(*ins, *outs, *scr)

        @pl.when(at(True))
        def _():
            comm.finish(cins, couts, *sems)

    res = pl.pallas_call(
        hosted, name=name, grid=grid, in_specs=list(in_specs) + comm.specs(), out_specs=list(out_specs) + comm.specs(),
        out_shape=list(out_shape) + comm.out_shape(), scratch_shapes=list(scratch) + comm.scratch(),
        input_output_aliases=aliases or {}, compiler_params=params)(*args, *comm.args)
    return list(res[:n_out]), list(res[n_out:])


def _exchange_two_level(blk, small, name):
    _, R, C = blk.shape
    rows = small.shape[0]

    def body(blk_ref, small_ref, stage_ref, out_ref, gath_ref, a_scr, b_scr, t_scr, s1, r1, s3, r3, ss, rs, lsem):
        mx, my, mc = _me()
        me = 4 * mx + 2 * my + mc
        mine = 2 * mx + my
        here, sibling = (mx, my, mc), (mx, my, 1 - mc)

        def rc(src, dst, send, recv, to):
            return pltpu.make_async_remote_copy(src_ref=src, dst_ref=dst, send_sem=send, recv_sem=recv,
                                                device_id=to, device_id_type=MESH)

        own_small = pltpu.make_async_copy(small_ref, gath_ref.at[me], lsem.at[0])
        own_small.start()
        spread = [rc(small_ref, gath_ref.at[me], ss.at[k - 1], rs.at[k - 1], _peer(mx, my, mc, k)) for k in range(1, N_DEV)]
        to_sib = [rc(blk_ref.at[2 * p + 1 - mc], stage_ref.at[p], s1.at[p], r1.at[p], sibling) for p in range(4)]
        for cp in spread + to_sib:
            cp.start()
        own = [pltpu.make_async_copy(blk_ref.at[2 * p + mc], a_scr.at[p], lsem.at[1 + p]) for p in range(4)]
        for cp in own:
            cp.start()
        from_sib = []
        for p in range(4):
            rc(blk_ref.at[2 * p + 1 - mc], stage_ref.at[p], s1.at[p], r1.at[p], here).wait_recv()
            cp = pltpu.make_async_copy(stage_ref.at[p], b_scr.at[p], lsem.at[5 + p])
            cp.start()
            from_sib.append(cp)
        for cp in own + from_sib:
            cp.wait()
        t_scr[...] = (a_scr[...].astype(F32) + b_scr[...].astype(F32)).astype(BF16)
        keep = pltpu.make_async_copy(t_scr.at[mine], out_ref.at[mine], lsem.at[9])
        keep.start()
        onward = [rc(t_scr.at[mine ^ k], out_ref.at[mine], s3.at[k - 1], r3.at[k - 1], (mx ^ (k >> 1), my ^ (k & 1), mc))
                  for k in range(1, 4)]
        for cp in onward:
            cp.start()
        for k in range(1, 4):
            rc(t_scr.at[mine], out_ref.at[mine ^ k], s3.at[k - 1], r3.at[k - 1], here).wait_recv()
        for k in range(1, N_DEV):
            rc(small_ref, gath_ref.at[me ^ k], ss.at[k - 1], rs.at[k - 1], here).wait_recv()
        for cp in spread + to_sib + onward:
            cp.wait_send()
        keep.wait()
        own_small.wait()

    any_spec = pl.BlockSpec(memory_space=pl.ANY)
    dma = pltpu.SemaphoreType.DMA
    _, out, gath = pl.pallas_call(
        body, name=name,
        in_specs=[any_spec, any_spec], out_specs=[any_spec] * 3,
        out_shape=[_sds((4, R, C), BF16), _sds((4, R, C), BF16), _sds((N_DEV, rows, D), F32)],
        scratch_shapes=[pltpu.VMEM((4, R, C), BF16)] * 3
                       + [dma((4,)), dma((4,)), dma((3,)), dma((3,)), dma((N_DEV - 1,)), dma((N_DEV - 1,)), dma((10,))],
        compiler_params=pltpu.CompilerParams(vmem_limit_bytes=VMEM_LIMIT),
    )(blk, small)
    return out, gath


def _exchange_rows(x_ref, out_ref, send_sems, recv_sems):
    mx, my, mc = _me()
    me = 4 * mx + 2 * my + mc
    out_ref[pl.ds(me, 1)] = x_ref[...][None]
    sends = []
    for k in range(1, N_DEV):
        cp = pltpu.make_async_remote_copy(
            src_ref=x_ref, dst_ref=out_ref.at[me], send_sem=send_sems.at[k - 1], recv_sem=recv_sems.at[k - 1],
            device_id=_peer(mx, my, mc, k), device_id_type=MESH)
        cp.start()
        sends.append(cp)
    for k in range(1, N_DEV):
        pltpu.make_async_remote_copy(
            src_ref=x_ref, dst_ref=out_ref.at[me ^ k], send_sem=send_sems.at[k - 1], recv_sem=recv_sems.at[k - 1],
            device_id=(mx, my, mc), device_id_type=MESH).wait_recv()
    for cp in sends:
        cp.wait_send()


def _prologue(c8, cctx8, w_ada, b_my, comm):
    nw = w_ada.shape[1]

    def body(c_ref, cctx_ref, w_ref, b_ref, act_ref, mod_ref, cmine_scr, call_scr, mine_scr, mall_scr, s1, r1, s2, r2):
        cmine_scr[...] = c_ref[...]
        _exchange_rows(cmine_scr, call_scr, s1, r1)
        rows = [call_scr[d][0:1, :] for d in range(N_DEV)] + [cctx_ref[0:1, :], jnp.zeros((7, D), F32)]
        s = jnp.concatenate(rows, axis=0)
        act = s * _sigmoid(s)
        act_ref[...] = act
        mine_scr[...] = jnp.dot(act.astype(BF16), w_ref[...].astype(BF16), preferred_element_type=F32) + b_ref[...]
        _exchange_rows(mine_scr, mall_scr, s2, r2)
        mod_ref[...] = mall_scr[...]

    sems = [pltpu.SemaphoreType.DMA((N_DEV - 1,))] * 4
    (act, mod), got = _call(
        body, name="prologue", grid=(1,),
        in_specs=[_full((8, D)), _full((8, D)), _full((D, nw)), _full((1, nw))],
        out_specs=[_full((16, D)), _full((N_DEV, 16, nw))],
        out_shape=[_sds((16, D), F32), _sds((N_DEV, 16, nw), F32)],
        scratch=[pltpu.VMEM((8, D), F32), pltpu.VMEM((N_DEV, 8, D), F32), pltpu.VMEM((16, nw), F32),
                 pltpu.VMEM((N_DEV, 16, nw), F32)] + sems,
        args=(c8, cctx8, w_ada, b_my), comm=comm)
    return act, mod, got


def _cctx_partial(dmod_my, w_ada):
    nw = w_ada.shape[1]

    def body(dm_ref, w_ref, pc_ref):
        pc_ref[...] = lax.dot_general(dm_ref[8:16, :].astype(BF16), w_ref[...].astype(BF16), NT,
                                      preferred_element_type=F32)

    return pl.pallas_call(
        body, name="cctx_partial", grid=(1,),
        in_specs=[_full((16, nw)), _full((D, nw))], out_specs=_full((8, D)), out_shape=_sds((8, D), F32),
        compiler_params=_params(("arbitrary",)),
    )(dmod_my, w_ada)


def _ada_bwd(act, dmod_my, w_ada, m, v, comm, tr=256):
    nw = w_ada.shape[1]

    def body(act_ref, dm_ref, w_ref, m_ref, v_ref, g_ref, d_ref, m2_ref, v2_ref):
        g = lax.dot_general(act_ref[...].astype(BF16), dm_ref[...].astype(BF16), TN, preferred_element_type=F32)
        delta, m2, v2 = _adamw(w_ref[...], g, m_ref[...], v_ref[...])
        g_ref[...] = g
        d_ref[...] = delta
        m2_ref[...] = m2
        v2_ref[...] = v2

    wspec = _row(tr, nw)
    return _call(
        body, name="ada_bwd", grid=(D // tr,),
        in_specs=[pl.BlockSpec((16, tr), lambda i: (0, i)), _full((16, nw)), wspec, wspec, wspec],
        out_specs=[wspec] * 4, out_shape=[_sds((D, nw), F32)] * 4,
        args=(act, dmod_my, w_ada, m, v), comm=comm)


def _k_in(x, modv, w_in, cos, sin, tm, comm=None):
    L = x.shape[0]

    def body(x_ref, mod_ref, w_ref, cos_ref, sin_ref, h_ref, q_ref, k_ref, v_ref, u_ref, vb_ref, ga_ref, gb_ref):
        n, _ = _ln(x_ref[...])
        h = (n * (1.0 + mod_ref[1:2, :]) + mod_ref[0:1, :]).astype(BF16)
        h_ref[...] = h
        c, s = cos_ref[...], sin_ref[...]

        def proj(lo, width):
            return lax.dot_general(h, w_ref[lo:lo + width, :], NT, preferred_element_type=F32)

        for i in range(4):
            q_ref[:, i * 128:(i + 1) * 128] = (_rope(proj(O_Q + i * 128, 128), c, s) * Q_SCALE).astype(BF16)
        k_ref[...] = _rope(proj(O_K, KV_W), c, s).astype(BF16)
        v_ref[...] = proj(O_V, KV_W).astype(BF16)
        u_ref[...] = proj(O_U, GM_W).astype(BF16)
        vb_ref[...] = proj(O_VB, GM_W).astype(BF16)
        ga_ref[...] = proj(O_GA, D).astype(BF16)
        gb_ref[...] = proj(O_GB, D).astype(BF16)

    widths = [D, Q_W, KV_W, KV_W, GM_W, GM_W, D, D]
    return _call(
        body, name="fwd_in", grid=(L // tm,),
        in_specs=[_row(tm, D), _full((8, D)), _resident((IN_W, D)), _row(tm, 128), _row(tm, 128)],
        out_specs=[_row(tm, w) for w in widths],
        out_shape=[_sds((L, w), BF16) for w in widths],
        args=(x, modv, w_in, cos, sin), comm=comm)


def _k_ctx(ctx, modc, w_kv):
    C = ctx.shape[0]

    def body(c_ref, mod_ref, w_ref, hc_ref, kc_ref, vc_ref):
        n, _ = _ln(c_ref[...])
        hc = (n * (1.0 + mod_ref[1:2, :]) + mod_ref[0:1, :]).astype(BF16)
        hc_ref[...] = hc
        kv = lax.dot_general(hc, w_ref[...], NT, preferred_element_type=F32)
        kc_ref[...] = kv[:, :KV_W].astype(BF16)
        vc_ref[...] = kv[:, KV_W:].astype(BF16)

    return pl.pallas_call(
        body, name="fwd_ctx", grid=(1,),
        in_specs=[_full((C, D)), _full((8, D)), _full((2 * KV_W, D))],
        out_specs=[_full((C, D)), _full((C, KV_W)), _full((C, KV_W))],
        out_shape=[_sds((C, D), BF16), _sds((C, KV_W), BF16), _sds((C, KV_W), BF16)],
        compiler_params=_params(("arbitrary",)),
    )(ctx, modc, w_kv)


def _attn_bias():
    r = (np.arange(GQA_GROUP * BLK) & (BLK - 1))[:, None]
    j = np.arange(3 * BLK)[None, :]
    band = np.abs(j - BLK - r) <= BLK
    variants = [band & (j >= BLK), band, band & (j < 2 * BLK)]
    return jnp.asarray(np.stack([np.where(v, 0.0, NEG_INF) for v in variants]), F32)


def _masked(s, bias, C):
    return jnp.concatenate([s[:, :C], s[:, C:] + bias], axis=1)


def _sink_col(sink_ref, hk):
    grp = lax.broadcasted_iota(jnp.int32, (GQA_GROUP * BLK, 1), 0) >> 7
    col = jnp.full((GQA_GROUP * BLK, 1), sink_ref[hk * GQA_GROUP], F32)
    for g in range(1, GQA_GROUP):
        col = jnp.where(grp == g, sink_ref[hk * GQA_GROUP + g], col)
    return col


def _k_attn(sink, q, k, v, kc, vc, bias, comm=None):
    L = q.shape[0]
    C = kc.shape[0]
    nb = L // BLK
    steps = nb // 2

    def body(sink_ref, q_ref, kp_ref, km_ref, kx_ref, vp_ref, vm_ref, vx_ref, kc_ref, vc_ref, bias_ref, ya_ref, lse_ref):
        i = pl.program_id(0)
        bands = [bias_ref[jnp.where(i == 0, 0, 1)], bias_ref[jnp.where(i == steps - 1, 2, 1)]]
        chains = [(qb, hk) for qb in range(2) for hk in range(N_KV_HEADS)]

        def keys(ctx_ref, p_ref, m_ref, x_ref, qb, hk):
            sl = slice(hk * HEAD_DIM, (hk + 1) * HEAD_DIM)
            band3 = ([p_ref[:, sl], m_ref[0:BLK, sl], m_ref[BLK:2 * BLK, sl]] if qb == 0
                     else [m_ref[0:BLK, sl], m_ref[BLK:2 * BLK, sl], x_ref[:, sl]])
            return jnp.concatenate([ctx_ref[:, sl]] + band3, axis=0)

        def queries(qb, hk):
            return jnp.concatenate(
                [q_ref[qb * BLK:(qb + 1) * BLK, (hk * GQA_GROUP + g) * HEAD_DIM:(hk * GQA_GROUP + g + 1) * HEAD_DIM]
                 for g in range(GQA_GROUP)], axis=0)

        s = [_masked(lax.dot_general(queries(qb, hk), keys(kc_ref, kp_ref, km_ref, kx_ref, qb, hk), NT,
                                     preferred_element_type=F32), bands[qb], C) for qb, hk in chains]
        for (qb, hk), s_ in zip(chains, s):
            sink_c = _sink_col(sink_ref, hk)
            m = jnp.maximum(jnp.max(s_, axis=1, keepdims=True), sink_c)
            p = jnp.exp(s_ - m)
            den = jnp.sum(p, axis=1, keepdims=True) + jnp.exp(sink_c - m)
            o = jnp.dot(p.astype(BF16), keys(vc_ref, vp_ref, vm_ref, vx_ref, qb, hk), preferred_element_type=F32) * (1.0 / den)
            lse = m + jnp.log(den)
            rows = slice(qb * BLK, (qb + 1) * BLK)
            for g in range(GQA_GROUP):
                h = hk * GQA_GROUP + g
                ya_ref[rows, h * HEAD_DIM:(h + 1) * HEAD_DIM] = o[g * BLK:(g + 1) * BLK, :].astype(BF16)
                lse_ref[rows, h:h + 1] = lse[g * BLK:(g + 1) * BLK, :]

    kv3 = [pl.BlockSpec((BLK, KV_W), lambda i: (jnp.maximum(2 * i - 1, 0), 0)),
           pl.BlockSpec((2 * BLK, KV_W), lambda i: (i, 0)),
           pl.BlockSpec((BLK, KV_W), lambda i: (jnp.minimum(2 * i + 2, nb - 1), 0))]
    return _call(
        body, name="fwd_attn", grid=(steps,),
        in_specs=[pl.BlockSpec(memory_space=pltpu.SMEM), _row(2 * BLK, Q_W)] + kv3 + kv3
                 + [_full((C, KV_W)), _full((C, KV_W)), _full((3, GQA_GROUP * BLK, 3 * BLK))],
        out_specs=[_row(2 * BLK, Q_W), _row(2 * BLK, N_Q_HEADS)],
        out_shape=[_sds((L, Q_W), BF16), _sds((L, N_Q_HEADS), F32)],
        args=(sink, q, k, k, k, v, v, v, kc, vc, bias), comm=comm)


GMLP_CHUNKS = 4


def _split_pair(t):
    low = lax.broadcasted_iota(jnp.int32, t.shape, 1) < GROUP_DIM
    zero = jnp.zeros_like(t)
    return jnp.where(low, t, zero), jnp.where(low, zero, t)


def _gmlp_spatial(w_ref, t_b, nch):
    rows = []
    for c in range(nch):
        tiles = []
        for pr in range(N_GROUPS // 2):
            lo, hi = _split_pair(t_b[c * BLK:(c + 1) * BLK, pr * 128:(pr + 1) * 128])
            tiles.append(jnp.dot(w_ref[2 * pr], lo, preferred_element_type=F32)
                         + jnp.dot(w_ref[2 * pr + 1], hi, preferred_element_type=F32))
        rows.append(jnp.concatenate(tiles, axis=1))
    return jnp.concatenate(rows, axis=0)


def _gmlp_fwd_vals(u, vb, lnv_ref, ws_ref, bsp_ref, nch):
    uf = u.astype(F32)
    vf = vb.astype(F32)
    gu, tu = _gelu(uf)
    gv, tv = _gelu(vf)
    vhat, rstd = _ln(gv)
    vn = (vhat * lnv_ref[0:1, :] + lnv_ref[1:2, :]).astype(BF16)
    s = _gmlp_spatial(ws_ref, vn, nch) + jnp.concatenate([bsp_ref[...]] * nch, axis=0)
    return uf, vf, gu, tu, tv, vhat, rstd, vn, s


def _k_gmlp(u, vb, lnv, ws, bsp):
    L = u.shape[0]
    nch = min(GMLP_CHUNKS, L // BLK)
    tm = nch * BLK

    def body(u_ref, vb_ref, lnv_ref, ws_ref, bsp_ref, yb_ref):
        _, _, gu, _, _, _, _, _, s = _gmlp_fwd_vals(u_ref[...], vb_ref[...], lnv_ref, ws_ref, bsp_ref, nch)
        yb_ref[...] = (gu * s).astype(BF16)

    return pl.pallas_call(
        body, name="fwd_gmlp", grid=(L // tm,),
        in_specs=[_row(tm, GM_W), _row(tm, GM_W), _full((8, GM_W)), _full((N_GROUPS, BLK, BLK)), _full((BLK, GM_W))],
        out_specs=_row(tm, GM_W),
        out_shape=_sds((L, GM_W), BF16),
        compiler_params=_params(("arbitrary",)),
    )(u, vb, lnv, ws, bsp)


def _k_merge(x, ya, yb, ga, gb, w_a, w_b, w_o, modv, lnv, tm):
    L = x.shape[0]

    def body(x_ref, ya_ref, yb_ref, ga_ref, gb_ref, wa_ref, wb_ref, wo_ref, mod_ref, ln_ref,
             mg_ref, mix_ref, xm_ref, h2_ref):
        a = jnp.dot(ya_ref[...], wa_ref[...], preferred_element_type=F32)
        b = jnp.dot(yb_ref[...], wb_ref[...], preferred_element_type=F32)
        merged = (_sigmoid(ga_ref[...].astype(F32)) * a + _sigmoid(gb_ref[...].astype(F32)) * b).astype(BF16)
        mg_ref[...] = merged
        mix = jnp.dot(merged, wo_ref[...], preferred_element_type=F32)
        mix_ref[...] = mix.astype(BF16)
        r1 = ALPHA * x_ref[...] + mod_ref[2:3, :] * mix
        r1hat, _ = _ln(r1)
        xm = r1hat * ln_ref[0:1, :] + ln_ref[1:2, :]
        xm_ref[...] = xm
        n2, _ = _ln(xm)
        h2_ref[...] = (n2 * (1.0 + mod_ref[4:5, :]) + mod_ref[3:4, :]).astype(BF16)

    return pl.pallas_call(
        body, name="fwd_merge", grid=(L // tm,),
        in_specs=[_row(tm, D), _row(tm, Q_W), _row(tm, GM_W), _row(tm, D), _row(tm, D),
                  _resident((Q_W, D)), _resident((GM_W, D)), _resident((D, D)), _full((8, D)), _full((8, D))],
        out_specs=[_row(tm, D)] * 4,
        out_shape=[_sds((L, D), BF16), _sds((L, D), BF16), _sds((L, D), F32), _sds((L, D), BF16)],
        compiler_params=_params(("arbitrary",)),
    )(x, ya, yb, ga, gb, w_a, w_b, w_o, modv, lnv)


FFN_CH = FFN_H // 2


def _k_ffn(h2, xm, tgt, w_fi, w_fo, modv, lnv, tm):
    L = h2.shape[0]

    def body(h2_ref, xm_ref, t_ref, wi_ref, wo_ref, mod_ref, ln_ref, gate_ref, up_ref, a_ref, dr2_ref, df_ref, acc_ref):
        @pl.when(pl.program_id(0) == 0)
        def _():
            acc_ref[...] = jnp.zeros_like(acc_ref)

        h2v = h2_ref[...]
        f = jnp.zeros((tm, D), F32)
        for j in range(FFN_H // FFN_CH):
            lo = j * FFN_CH
            gate = lax.dot_general(h2v, wi_ref[lo:lo + FFN_CH, :], NT, preferred_element_type=F32)
            up = lax.dot_general(h2v, wi_ref[FFN_H + lo:FFN_H + lo + FFN_CH, :], NT, preferred_element_type=F32)
            act = (gate * _sigmoid(gate) * up).astype(BF16)
            gate_ref[:, lo:lo + FFN_CH] = gate.astype(BF16)
            up_ref[:, lo:lo + FFN_CH] = up.astype(BF16)
            a_ref[:, lo:lo + FFN_CH] = act
            f = f + jnp.dot(act, wo_ref[lo:lo + FFN_CH, :], preferred_element_type=F32)
        gate2 = mod_ref[5:6, :]
        r2 = ALPHA * xm_ref[...] + gate2 * f
        r2hat, rstd = _ln(r2)
        y = r2hat * ln_ref[2:3, :] + ln_ref[3:4, :]
        err = y - t_ref[...]
        dy = err * (1.0 / D)
        dr2 = _ln_bwd(dy * ln_ref[2:3, :], r2hat, rstd)
        dr2_ref[...] = dr2
        df_ref[...] = (gate2 * dr2).astype(BF16)
        acc_ref[0:1, :] += _colsum(dy * r2hat)
        acc_ref[1:2, :] += _colsum(dy)
        acc_ref[2:3, :] += _colsum(dr2 * f)
        acc_ref[3:4, :] += _colsum(err * err) * (0.5 / D)

    return pl.pallas_call(
        body, name="fwd_ffn", grid=(L // tm,),
        in_specs=[_row(tm, D), _row(tm, D), _row(tm, D), _resident((2 * FFN_H, D)), _resident((FFN_H, D)),
                  _full((8, D)), _full((8, D))],
        out_specs=[_row(tm, FFN_H)] * 3 + [_row(tm, D), _row(tm, D), _full((8, D))],
        out_shape=[_sds((L, FFN_H), BF16)] * 3 + [_sds((L, D), F32), _sds((L, D), BF16), _sds((8, D), F32)],
        compiler_params=_params(("arbitrary",)),
    )(h2, xm, tgt, w_fi, w_fo, modv, lnv)


def _k_ffn_bwd(df, gate, up, xm, dr2, x, mix, w_fi, w_fo, modv, lnv, tm):
    L = df.shape[0]

    def body(df_ref, gate_ref, up_ref, xm_ref, dr2_ref, x_ref, mix_ref, wi_ref, wo_ref, mod_ref, ln_ref,
             dF_ref, dmix_ref, dxp_ref, acc_ref):
        @pl.when(pl.program_id(0) == 0)
        def _():
            acc_ref[...] = jnp.zeros_like(acc_ref)

        dfv = df_ref[...]
        chunks = [j * FFN_CH for j in range(FFN_H // FFN_CH)]
        das = [lax.dot_general(dfv, wo_ref[lo:lo + FFN_CH, :], NT, preferred_element_type=F32) for lo in chunks]
        n2, rstd2 = _ln(xm_ref[...])
        mixf = mix_ref[...].astype(F32)
        gate1 = mod_ref[2:3, :]
        r1hat, rstd1 = _ln(ALPHA * x_ref[...] + gate1 * mixf)
        dh2 = jnp.zeros((tm, D), F32)
        for lo, da in zip(chunks, das):
            gate = gate_ref[:, lo:lo + FFN_CH].astype(F32)
            upv = up_ref[:, lo:lo + FFN_CH].astype(F32)
            sg = _sigmoid(gate)
            d_gate = (da * upv * (sg * (1.0 + gate * (1.0 - sg)))).astype(BF16)
            d_up = (da * (gate * sg)).astype(BF16)
            dF_ref[:, lo:lo + FFN_CH] = d_gate
            dF_ref[:, FFN_H + lo:FFN_H + lo + FFN_CH] = d_up
            dh2 = dh2 + jnp.dot(d_gate, wi_ref[lo:lo + FFN_CH, :], preferred_element_type=F32)
            dh2 = dh2 + jnp.dot(d_up, wi_ref[FFN_H + lo:FFN_H + lo + FFN_CH, :], preferred_element_type=F32)
        acc_ref[0:1, :] += _colsum(dh2)
        acc_ref[1:2, :] += _colsum(dh2 * n2)
        dxm = ALPHA * dr2_ref[...] + _ln_bwd(dh2 * (1.0 + mod_ref[4:5, :]), n2, rstd2)
        acc_ref[2:3, :] += _colsum(dxm * r1hat)
        acc_ref[3:4, :] += _colsum(dxm)
        dr1 = _ln_bwd(dxm * ln_ref[0:1, :], r1hat, rstd1)
        dmix_ref[...] = (gate1 * dr1).astype(BF16)
        dxp_ref[...] = ALPHA * dr1
        acc_ref[4:5, :] += _colsum(dr1 * mixf)

    return pl.pallas_call(
        body, name="bwd_ffn", grid=(L // tm,),
        in_specs=[_row(tm, D), _row(tm, FFN_H), _row(tm, FFN_H), _row(tm, D), _row(tm, D), _row(tm, D), _row(tm, D),
                  _resident((2 * FFN_H, D)), _resident((FFN_H, D)), _full((8, D)), _full((8, D))],
        out_specs=[_row(tm, 2 * FFN_H), _row(tm, D), _row(tm, D), _full((8, D))],
        out_shape=[_sds((L, 2 * FFN_H), BF16), _sds((L, D), BF16), _sds((L, D), F32), _sds((8, D), F32)],
        compiler_params=_params(("arbitrary",)),
    )(df, gate, up, xm, dr2, x, mix, w_fi, w_fo, modv, lnv)


def _k_merge_bwd(dmix, ya, yb, ga, gb, w_a, w_b, w_o, tm, comm=None):
    L = dmix.shape[0]

    def body(dmix_ref, ya_ref, yb_ref, ga_ref, gb_ref, wa_ref, wb_ref, wo_ref,
             dA_ref, dB_ref, dga_ref, dgb_ref, dya_ref, dyb_ref):
        dmg = lax.dot_general(dmix_ref[...], wo_ref[...], NT, preferred_element_type=F32)
        a = jnp.dot(ya_ref[...], wa_ref[...], preferred_element_type=F32)
        sa = _sigmoid(ga_ref[...].astype(F32))
        dA = (dmg * sa).astype(BF16)
        dA_ref[...] = dA
        dga_ref[...] = (dmg * a * (sa * (1.0 - sa))).astype(BF16)
        dya_ref[...] = lax.dot_general(dA, wa_ref[...], NT, preferred_element_type=F32).astype(BF16)
        b = jnp.dot(yb_ref[...], wb_ref[...], preferred_element_type=F32)
        sb = _sigmoid(gb_ref[...].astype(F32))
        dB = (dmg * sb).astype(BF16)
        dB_ref[...] = dB
        dgb_ref[...] = (dmg * b * (sb * (1.0 - sb))).astype(BF16)
        dyb_ref[...] = lax.dot_general(dB, wb_ref[...], NT, preferred_element_type=F32).astype(BF16)

    return _call(
        body, name="bwd_merge", grid=(L // tm,),
        in_specs=[_row(tm, D), _row(tm, Q_W), _row(tm, GM_W), _row(tm, D), _row(tm, D),
                  _resident((Q_W, D)), _resident((GM_W, D)), _resident((D, D))],
        out_specs=[_row(tm, D)] * 4 + [_row(tm, Q_W), _row(tm, GM_W)],
        out_shape=[_sds((L, D), BF16)] * 4 + [_sds((L, Q_W), BF16), _sds((L, GM_W), BF16)],
        args=(dmix, ya, yb, ga, gb, w_a, w_b, w_o), comm=comm)


def _k_gmlp_bwd(u, vb, dyb, lnv, ws, wst, bsp):
    L = u.shape[0]
    nch = min(GMLP_CHUNKS, L // BLK)
    tm = nch * BLK

    def body(u_ref, vb_ref, dyb_ref, lnv_ref, ws_ref, wst_ref, bsp_ref, du_ref, dvb_ref, gws_ref, gbst_ref, gln_ref):
        @pl.when(pl.program_id(0) == 0)
        def _():
            gws_ref[...] = jnp.zeros_like(gws_ref)
            gbst_ref[...] = jnp.zeros_like(gbst_ref)
            gln_ref[...] = jnp.zeros_like(gln_ref)

        uf, vf, gu, tu, tv, vhat, rstd, vn, s = _gmlp_fwd_vals(u_ref[...], vb_ref[...], lnv_ref, ws_ref, bsp_ref, nch)
        dyb_f = dyb_ref[...].astype(F32)
        du_ref[...] = (dyb_f * s * _gelu_grad(uf, tu)).astype(BF16)
        ds = dyb_f * gu
        ds_b = ds.astype(BF16)
        for pr in range(N_GROUPS // 2):
            lanes = slice(pr * 128, (pr + 1) * 128)
            gw_lo = gw_hi = ds_sum = None
            for c in range(nch):
                rows = slice(c * BLK, (c + 1) * BLK)
                lo, hi = _split_pair(ds_b[rows, lanes])
                t_lo = lax.dot_general(lo, vn[rows, lanes], NT, preferred_element_type=F32)
                t_hi = lax.dot_general(hi, vn[rows, lanes], NT, preferred_element_type=F32)
                gw_lo = t_lo if c == 0 else gw_lo + t_lo
                gw_hi = t_hi if c == 0 else gw_hi + t_hi
                ds_sum = ds[rows, lanes] if c == 0 else ds_sum + ds[rows, lanes]
            gws_ref[2 * pr] += gw_lo
            gws_ref[2 * pr + 1] += gw_hi
            b_lo, b_hi = _split_pair(ds_sum)
            gbst_ref[:, 2 * pr:2 * pr + 1] += jnp.sum(b_lo, axis=1, keepdims=True)
            gbst_ref[:, 2 * pr + 1:2 * pr + 2] += jnp.sum(b_hi, axis=1, keepdims=True)
        dvn = _gmlp_spatial(wst_ref, ds_b, nch)
        gln_ref[0:1, :] += _colsum(dvn * vhat)
        gln_ref[1:2, :] += _colsum(dvn)
        dgv = _ln_bwd(dvn * lnv_ref[0:1, :], vhat, rstd)
        dvb_ref[...] = (dgv * _gelu_grad(vf, tv)).astype(BF16)

    return pl.pallas_call(
        body, name="bwd_gmlp", grid=(L // tm,),
        in_specs=[_row(tm, GM_W)] * 3 + [_full((8, GM_W)), _full((N_GROUPS, BLK, BLK)), _full((N_GROUPS, BLK, BLK)),
                                         _full((BLK, GM_W))],
        out_specs=[_row(tm, GM_W), _row(tm, GM_W), _full((N_GROUPS, BLK, BLK)), _full((BLK, N_GROUPS)), _full((8, GM_W))],
        out_shape=[_sds((L, GM_W), BF16), _sds((L, GM_W), BF16), _sds((N_GROUPS, BLK, BLK), F32),
                   _sds((BLK, N_GROUPS), F32), _sds((8, GM_W), F32)],
        compiler_params=_params(("arbitrary",)),
    )(u, vb, dyb, lnv, ws, wst, bsp)


def _k_attn_bwd(sink, q, k, v, kc, vc, dya, lse, cos, sin, bias, comm=None):
    L = q.shape[0]
    C = kc.shape[0]
    nb = L // BLK
    steps = nb // 2
    NK = C + 3 * BLK
    chains = [(qb, hk) for qb in range(2) for hk in range(N_KV_HEADS)]

    def body(sink_ref, q_ref, kp_ref, km_ref, kx_ref, vp_ref, vm_ref, vx_ref, kc_ref, vc_ref, do_ref, lse_ref,
             cq_ref, sq_ref, cl_ref, sl_ref, bias_ref,
             dq_ref, dk_ref, dv_ref, dkc_ref, dvc_ref, dsink_ref,
             dq_scr, ck_scr, cv_scr, k1_acc, k2_acc, v1_acc, v2_acc):
        i = pl.program_id(0)

        @pl.when(i == 0)
        def _():
            for r in (k1_acc, k2_acc, v1_acc, v2_acc, dkc_ref, dvc_ref, dsink_ref):
                r[...] = jnp.zeros_like(r)

        @pl.when(i < steps)
        def _():
            bands = [bias_ref[jnp.where(i == 0, 0, 1)], bias_ref[jnp.where(i == steps - 1, 2, 1)]]

            def lanes(hk):
                return slice(hk * HEAD_DIM, (hk + 1) * HEAD_DIM)

            def keys(ctx_ref, p_ref, m_ref, x_ref, qb, hk):
                sl = lanes(hk)
                band3 = ([p_ref[:, sl], m_ref[0:BLK, sl], m_ref[BLK:2 * BLK, sl]] if qb == 0
                         else [m_ref[0:BLK, sl], m_ref[BLK:2 * BLK, sl], x_ref[:, sl]])
                return jnp.concatenate([ctx_ref[:, sl]] + band3, axis=0)

            def stacked(ref, qb, hk, width):
                return jnp.concatenate(
                    [ref[qb * BLK:(qb + 1) * BLK, (hk * GQA_GROUP + g) * width:(hk * GQA_GROUP + g + 1) * width]
                     for g in range(GQA_GROUP)], axis=0)

            def scores(qb, hk):
                kcat = keys(kc_ref, kp_ref, km_ref, kx_ref, qb, hk)
                qg = stacked(q_ref, qb, hk, HEAD_DIM)
                s = _masked(lax.dot_general(qg, kcat, NT, preferred_element_type=F32), bands[qb], C)
                dog = stacked(do_ref, qb, hk, HEAD_DIM)
                dp = lax.dot_general(dog, keys(vc_ref, vp_ref, vm_ref, vx_ref, qb, hk), NT, preferred_element_type=F32)
                return kcat, qg, dog, s, dp

            def softmax_bwd(qb, hk, s, dp):
                lse_c = stacked(lse_ref, qb, hk, 1)
                p = jnp.exp(s - lse_c)
                delta = jnp.sum(p * dp, axis=1, keepdims=True)
                ds = (p * (dp - delta)).astype(BF16)
                p_sink = jnp.exp(_sink_col(sink_ref, hk) - lse_c) * delta
                return p.astype(BF16), ds, p_sink

            def put_dq(qb, hk, dqs, p_sink):
                for g in range(GQA_GROUP):
                    h = hk * GQA_GROUP + g
                    dq_scr[qb * BLK:(qb + 1) * BLK, h * HEAD_DIM:(h + 1) * HEAD_DIM] = dqs[g * BLK:(g + 1) * BLK, :]
                    tot = jnp.sum(p_sink[g * BLK:(g + 1) * BLK, :], axis=0, keepdims=True)
                    dsink_ref[h:h + 1, :] -= jnp.broadcast_to(tot, (1, 128))

            sc = [scores(qb, hk) for qb, hk in chains]
            pending = None
            for (qb, hk), (kcat, qg, dog, s, dp) in zip(chains, sc):
                pb, ds, p_sink = softmax_bwd(qb, hk, s, dp)
                if pending is not None:
                    pqb, phk, pds, ppb, pqg, pdog = pending
                    ck_scr[pqb, :, lanes(phk)] = lax.dot_general(pds, pqg, TN, preferred_element_type=F32)
                    cv_scr[pqb, :, lanes(phk)] = lax.dot_general(ppb, pdog, TN, preferred_element_type=F32)
                put_dq(qb, hk, jnp.dot(ds, kcat, preferred_element_type=F32), p_sink)
                pending = (qb, hk, ds, pb, qg, dog)
            pqb, phk, pds, ppb, pqg, pdog = pending
            ck_scr[pqb, :, lanes(phk)] = lax.dot_general(pds, pqg, TN, preferred_element_type=F32)
            cq, sq = cq_ref[...], sq_ref[...]
            for j in range(4):
                dq_ref[:, j * 128:(j + 1) * 128] = _unrope(dq_scr[:, j * 128:(j + 1) * 128] * Q_SCALE, cq, sq).astype(BF16)
            cv_scr[pqb, :, lanes(phk)] = lax.dot_general(ppb, pdog, TN, preferred_element_type=F32)
            dkc_ref[...] += ck_scr[0, 0:C, :] + ck_scr[1, 0:C, :]
            dvc_ref[...] += cv_scr[0, 0:C, :] + cv_scr[1, 0:C, :]

        @pl.when(i >= steps)
        def _():
            ck_scr[...] = jnp.zeros_like(ck_scr)
            cv_scr[...] = jnp.zeros_like(cv_scr)

        def part(scr, qb, j):
            return scr[qb, C + j * BLK:C + (j + 1) * BLK, :]

        dk_ref[0:BLK, :] = _unrope(k1_acc[...] + part(ck_scr, 0, 0), cl_ref[...], sl_ref[...]).astype(BF16)
        dk_ref[BLK:2 * BLK, :] = _unrope(k2_acc[...] + part(ck_scr, 0, 1) + part(ck_scr, 1, 0),
                                         cq_ref[0:BLK, :], sq_ref[0:BLK, :]).astype(BF16)
        dv_ref[0:BLK, :] = (v1_acc[...] + part(cv_scr, 0, 0)).astype(BF16)
        dv_ref[BLK:2 * BLK, :] = (v2_acc[...] + part(cv_scr, 0, 1) + part(cv_scr, 1, 0)).astype(BF16)
        k1_acc[...] = part(ck_scr, 0, 2) + part(ck_scr, 1, 1)
        v1_acc[...] = part(cv_scr, 0, 2) + part(cv_scr, 1, 1)
        k2_acc[...] = part(ck_scr, 1, 2)
        v2_acc[...] = part(cv_scr, 1, 2)

    last = steps - 1
    kv3 = [pl.BlockSpec((BLK, KV_W), lambda i: (jnp.clip(2 * i - 1, 0, nb - 1), 0)),
           pl.BlockSpec((2 * BLK, KV_W), lambda i: (jnp.minimum(i, last), 0)),
           pl.BlockSpec((BLK, KV_W), lambda i: (jnp.minimum(2 * i + 2, nb - 1), 0))]
    cur = lambda w: pl.BlockSpec((2 * BLK, w), lambda i: (jnp.minimum(i, last), 0))
    late = lambda w: pl.BlockSpec((BLK, w), lambda i: (jnp.clip(2 * i - 1, 0, nb - 1), 0))
    out2 = lambda w: pl.BlockSpec((2 * BLK, w), lambda i: (i, 0))
    return _call(
        body, name="bwd_attn", grid=(steps + 1,),
        in_specs=[pl.BlockSpec(memory_space=pltpu.SMEM), cur(Q_W)] + kv3 + kv3
                 + [_full((C, KV_W)), _full((C, KV_W)), cur(Q_W), cur(N_Q_HEADS), cur(128), cur(128), late(128), late(128),
                    _full((3, GQA_GROUP * BLK, 3 * BLK))],
        out_specs=[cur(Q_W), out2(KV_W), out2(KV_W), _full((C, KV_W)), _full((C, KV_W)), _full((8, 128))],
        out_shape=[_sds((L, Q_W), BF16), _sds((L + 2 * BLK, KV_W), BF16), _sds((L + 2 * BLK, KV_W), BF16),
                   _sds((C, KV_W), F32), _sds((C, KV_W), F32), _sds((8, 128), F32)],
        scratch=[pltpu.VMEM((2 * BLK, Q_W), F32), pltpu.VMEM((2, NK, KV_W), F32), pltpu.VMEM((2, NK, KV_W), F32)]
                + [pltpu.VMEM((BLK, KV_W), F32)] * 4,
        args=(sink, q, k, k, k, v, v, v, kc, vc, dya, lse, cos, sin, cos, sin, bias), comm=comm)


def _k_ctx_bwd(ctx, modc, hc, dkc, dvc, w_kv):
    C = ctx.shape[0]

    def body(c_ref, mod_ref, hc_ref, dkc_ref, dvc_ref, w_ref, gw_ref, dmod_ref):
        dkv = jnp.concatenate([dkc_ref[...], dvc_ref[...]], axis=1).astype(BF16)
        gw_ref[...] = lax.dot_general(dkv, hc_ref[...], TN, preferred_element_type=F32)
        dhc = jnp.dot(dkv, w_ref[...], preferred_element_type=F32)
        n, _ = _ln(c_ref[...])
        dmod_ref[...] = jnp.zeros_like(dmod_ref)
        dmod_ref[0:1, :] = _colsum(dhc)
        dmod_ref[1:2, :] = _colsum(dhc * n)

    return pl.pallas_call(
        body, name="bwd_ctx", grid=(1,),
        in_specs=[_full((C, D)), _full((8, D)), _full((C, D)), _full((C, KV_W)), _full((C, KV_W)), _full((2 * KV_W, D))],
        out_specs=[_full((2 * KV_W, D)), _full((8, D))],
        out_shape=[_sds((2 * KV_W, D), F32), _sds((8, D), F32)],
        compiler_params=_params(("arbitrary",)),
    )(ctx, modc, hc, dkc, dvc, w_kv)


def _k_in_bwd(dq, dk, dv, du, dvb, dga, dgb, x, dxp, w_in, modv, tm, comm=None):
    L = x.shape[0]
    parts = [(O_Q, Q_W), (O_K, KV_W), (O_V, KV_W), (O_U, GM_W), (O_VB, GM_W), (O_GA, D), (O_GB, D)]

    def body(dq_ref, dk_ref, dv_ref, du_ref, dvb_ref, dga_ref, dgb_ref, x_ref, dxp_ref, w_ref, mod_ref,
             dP_ref, gx_ref, acc_ref):
        @pl.when(pl.program_id(0) == 0)
        def _():
            acc_ref[...] = jnp.zeros_like(acc_ref)

        for (lo, width), r in zip(parts, (dq_ref, dk_ref, dv_ref, du_ref, dvb_ref, dga_ref, dgb_ref)):
            dP_ref[:, lo:lo + width] = r[...]
        n1, rstd1 = _ln(x_ref[...])
        dh = jnp.dot(dP_ref[...], w_ref[...], preferred_element_type=F32)
        acc_ref[0:1, :] += _colsum(dh)
        acc_ref[1:2, :] += _colsum(dh * n1)
        gx_ref[...] = dxp_ref[...] + _ln_bwd(dh * (1.0 + mod_ref[1:2, :]), n1, rstd1)

    return _call(
        body, name="bwd_in", grid=(L // tm,),
        in_specs=[_row(tm, w) for _, w in parts] + [_row(tm, D), _row(tm, D), _resident((IN_W, D)), _full((8, D))],
        out_specs=[_row(tm, IN_W), _row(tm, D), _full((8, D))],
        out_shape=[_sds((L, IN_W), BF16), _sds((L, D), F32), _sds((8, D), F32)],
        args=(dq, dk, dv, du, dvb, dga, dgb, x, dxp, w_in, modv), comm=comm)


def _wgrad(a, b, name, tk, tt, comm=None, extra=None):
    T, K = a.shape
    N = b.shape[1]
    nt = T // tt

    def body(*refs):
        a_ref, b_ref = refs[:2]
        o_ref, acc_ref = refs[-2:]
        j, t = pl.program_id(0), pl.program_id(1)

        @pl.when(t == 0)
        def _():
            acc_ref[...] = jnp.zeros_like(acc_ref)

        acc_ref[...] += lax.dot_general(a_ref[...], b_ref[...], TN, preferred_element_type=F32)

        if extra is not None:
            lo, rows = extra[0] % tk, extra[1].shape[0]

            @pl.when((t == nt - 1) & (j == extra[0] // tk))
            def _():
                acc_ref[lo:lo + rows, :] += refs[2][...]

        @pl.when(t == nt - 1)
        def _():
            o_ref[...] = acc_ref[...].astype(BF16)

    extra_specs = [] if extra is None else [pl.BlockSpec(extra[1].shape, lambda j, t: (0, 0))]
    (out,), got = _call(
        body, name=name, grid=(K // tk, nt),
        in_specs=[pl.BlockSpec((tt, tk), lambda j, t: (t, j)), pl.BlockSpec((tt, N), lambda j, t: (t, 0))] + extra_specs,
        out_specs=[pl.BlockSpec((tk, N), lambda j, t: (j, 0))],
        out_shape=[_sds((K, N), BF16)],
        scratch=[pltpu.VMEM((tk, N), F32)],
        args=(a, b) + (() if extra is None else (extra[1],)), comm=comm)
    return (out, got) if comm is not None else out


def _adamw_reduce(parts, w, m, v, name, tr):
    R, C = w.shape
    n_parts = parts.shape[0]

    def body(p_ref, w_ref, m_ref, v_ref, g_ref, d_ref, m2_ref, v2_ref):
        g = p_ref[0].astype(F32)
        for i in range(1, n_parts):
            g = g + p_ref[i].astype(F32)
        delta, m2, v2 = _adamw(w_ref[...], g, m_ref[...], v_ref[...])
        g_ref[...] = g
        d_ref[...] = delta
        m2_ref[...] = m2
        v2_ref[...] = v2

    spec = _row(tr, C)
    return pl.pallas_call(
        body, name=name, grid=(R // tr,),
        in_specs=[pl.BlockSpec((n_parts, tr, C), lambda i: (0, i, 0)), spec, spec, spec],
        out_specs=[spec] * 4,
        out_shape=[_sds((R, C), F32)] * 4,
        compiler_params=_params(("arbitrary",)),
    )(parts, w, m, v)


SMALL_ORDER = ("b_ada", "ln1_g", "ln1_b", "ln2_g", "ln2_b", "gmlp_ln_g", "gmlp_ln_b", "b_spatial", "attn_sink")


def _small_step(gath, params):
    flat = [a for name in SMALL_ORDER for a in params[name]]

    def grad_of(tot, name):
        if name == "b_ada":
            return jnp.concatenate([tot[r:r + 1, :] for r in range(6)], axis=1)
        if name in ("ln1_g", "ln1_b", "ln2_g", "ln2_b"):
            r = 8 + ("ln1_g", "ln1_b", "ln2_g", "ln2_b").index(name)
            return tot[r:r + 1, :]
        if name == "gmlp_ln_g":
            return tot[12:13, :GM_W]
        if name == "gmlp_ln_b":
            return tot[12:13, GM_W:]
        if name == "b_spatial":
            return jnp.concatenate([tot[13:14, g * BLK:(g + 1) * BLK] for g in range(N_GROUPS)], axis=0)[None]
        return tot[14:15, :N_Q_HEADS]

    def body(*refs):
        g_ref, in_refs = refs[0], refs[1:1 + len(flat)]
        tot_ref, out_refs = refs[1 + len(flat)], refs[2 + len(flat):]
        tot = g_ref[0]
        for i in range(1, N_DEV):
            tot = tot + g_ref[i]
        tot_ref[...] = tot
        tot_ref[0:2, :] = tot[0:2, :] + tot[6:8, :]
        tot_ref[15:16, :] = jnp.broadcast_to(jnp.sum(tot[15:16, :], axis=1, keepdims=True), (1, D))
        tot = tot_ref[...]
        for k, name in enumerate(SMALL_ORDER):
            w_ref, m_ref, v_ref = in_refs[3 * k:3 * k + 3]
            g = grad_of(tot, name)
            delta, m2, v2 = _adamw(w_ref[...], g, m_ref[...], v_ref[...])
            for r, val in zip(out_refs[4 * k:4 * k + 4], (g, delta, m2, v2)):
                r[...] = val

    res = pl.pallas_call(
        body, name="small_step", grid=(1,),
        in_specs=[_full((N_DEV, 16, D))] + [_full(a.shape) for a in flat],
        out_specs=[_full((16, D))] + [_full(params[name][0].shape) for name in SMALL_ORDER for _ in range(4)],
        out_shape=[_sds((16, D), F32)] + [_sds(params[name][0].shape, F32) for name in SMALL_ORDER for _ in range(4)],
        compiler_params=_params(("arbitrary",)),
    )(gath, *flat)
    return res[0], {name: res[1 + 4 * k:5 + 4 * k] for k, name in enumerate(SMALL_ORDER)}


def _cctx_finish(gath, c_ctx, m, v):
    def body(g_ref, c_ref, m_ref, v_ref, gr_ref, d_ref, m2_ref, v2_ref):
        ds = g_ref[0]
        for i in range(1, N_DEV):
            ds = ds + g_ref[i]
        c = c_ref[...]
        sg = _sigmoid(c)
        g = ds * (sg * (1.0 + c * (1.0 - sg)))
        delta, m2, v2 = _adamw(c, g, m_ref[...], v_ref[...])
        gr_ref[...] = g
        d_ref[...] = delta
        m2_ref[...] = m2
        v2_ref[...] = v2

    return pl.pallas_call(
        body, name="cctx_finish", grid=(1,),
        in_specs=[_full((N_DEV, 8, D))] + [_full((8, D))] * 3, out_specs=[_full((8, D))] * 4,
        out_shape=[_sds((8, D), F32)] * 4,
        compiler_params=_params(("arbitrary",)),
    )(gath, c_ctx, m, v)


def _pad_rows(a, rows):
    return jnp.concatenate([a, jnp.zeros((rows - a.shape[0], a.shape[1]), a.dtype)], axis=0)


def kernel(x, c, ctx, c_ctx, w_ada, b_ada, w_in, attn_sink, gmlp_ln_g, gmlp_ln_b, w_spatial, b_spatial, w_branch_a, w_branch_b, w_out, ln1_g, ln1_b, w_ffn_in, w_ffn_out, ln2_g, ln2_b, loss_target, m_c_ctx, m_w_ada, m_b_ada, m_w_in, m_attn_sink, m_gmlp_ln_g, m_gmlp_ln_b, m_w_spatial, m_b_spatial, m_w_branch_a, m_w_branch_b, m_w_out, m_ln1_g, m_ln1_b, m_w_ffn_in, m_w_ffn_out, m_ln2_g, m_ln2_b, v_c_ctx, v_w_ada, v_b_ada, v_w_in, v_attn_sink, v_gmlp_ln_g, v_gmlp_ln_b, v_w_spatial, v_b_spatial, v_w_branch_a, v_w_branch_b, v_w_out, v_ln1_g, v_ln1_b, v_w_ffn_in, v_w_ffn_out, v_ln2_g, v_ln2_b):
    L = x.shape[1]
    me = 4 * lax.axis_index("x") + 2 * lax.axis_index("y") + lax.axis_index("c")
    x2, tgt, ctx2 = x[0], loss_target[0], ctx[0]
    tiles = _Tiles(L)
    tm_in, tm, tt = tiles.wide, tiles.narrow, tiles.tokens

    transposed = ("w_in", "w_ffn_in")
    tr = lambda kname, a: a.T if kname in transposed else a
    big = dict(w_in=w_in[0].T, w_branch_a=w_branch_a[0], w_branch_b=w_branch_b[0], w_out=w_out[0],
               w_ffn_in=w_ffn_in[0].T, w_ffn_out=w_ffn_out[0])
    col_sharded = ("w_branch_a", "w_branch_b")
    shard_bf = {k: a.astype(BF16) for k, a in big.items()}

    def assemble(kname, g):
        if kname in col_sharded:
            return g.transpose(1, 0, 2).reshape(g.shape[1], N_DEV * g.shape[2])
        return g.reshape(N_DEV * g.shape[1], g.shape[2])

    def to_blocks(kname, g):
        if kname in col_sharded:
            return g.reshape(g.shape[0], N_DEV, g.shape[1] // N_DEV).transpose(1, 0, 2)
        return g.reshape(N_DEV, g.shape[0] // N_DEV, g.shape[1])

    full = {}
    n_ada = w_ada.shape[2]
    b_my = lax.dynamic_slice(b_ada, (0, me * n_ada), (1, n_ada))
    act, mod_all, got = _prologue(_pad_rows(c, 8), _pad_rows(c_ctx[None, :], 8), w_ada[0], b_my,
                                  _Comm(gather=[shard_bf["w_in"]]))
    full["w_in"] = assemble("w_in", got[0])
    mod_all = mod_all.transpose(1, 0, 2).reshape(16, 6 * D)
    modv = _pad_rows(lax.dynamic_slice(mod_all, (me, 0), (1, 6 * D)).reshape(6, D), 8)
    modc = _pad_rows(mod_all[8].reshape(6, D), 8)

    lnv = _pad_rows(jnp.concatenate([ln1_g, ln1_b, ln2_g, ln2_b], axis=0), 8)
    gm_lnv = _pad_rows(jnp.concatenate([gmlp_ln_g, gmlp_ln_b], axis=0), 8)
    ws_b = w_spatial[0].astype(BF16)
    wst_b = ws_b.transpose(0, 2, 1)
    bsp = jnp.repeat(b_spatial[0].T, GROUP_DIM, axis=1)
    sink = attn_sink[0]
    cos, sin = _rope_tables(L)
    bias = _attn_bias()
    w_kv = full["w_in"][O_K:O_K + 2 * KV_W, :]

    (h, q, k, v, u, vb, ga, gb), got = _k_in(
        x2, modv, full["w_in"], cos, sin, tm_in,
        comm=_Comm(gather=[shard_bf[kname] for kname in ("w_branch_a", "w_branch_b", "w_out", "w_ffn_out")]))
    for kname, g in zip(("w_branch_a", "w_branch_b", "w_out", "w_ffn_out"), got):
        full[kname] = assemble(kname, g)
    hc, kc, vc = _k_ctx(ctx2, modc, w_kv)
    (ya, lse), got = _k_attn(sink, q, k, v, kc, vc, bias, comm=_Comm(gather=[shard_bf["w_ffn_in"]]))
    full["w_ffn_in"] = assemble("w_ffn_in", got[0])
    yb = _k_gmlp(u, vb, gm_lnv, ws_b, bsp)
    merged, mix, xm, h2 = _k_merge(x2, ya, yb, ga, gb, full["w_branch_a"], full["w_branch_b"], full["w_out"], modv, lnv, tm_in)
    gate, up, act_f, dr2, df, acc_f = _k_ffn(h2, xm, tgt, full["w_ffn_in"], full["w_ffn_out"], modv, lnv, tm_in)

    dF, dmix, dxp, acc_b = _k_ffn_bwd(df, gate, up, xm, dr2, x2, mix, full["w_ffn_in"], full["w_ffn_out"], modv, lnv, tm)
    blk_fo = to_blocks("w_ffn_out", _wgrad(act_f, df, "wgrad_ffn_out", tiles.tk_ffn, tt))
    gw_fi, (rcv_fo,) = _wgrad(dF, h2, "wgrad_ffn_in", tiles.tk_ffn, tt, comm=_Comm(scatter=[blk_fo]))
    blk_fi = to_blocks("w_ffn_in", gw_fi)
    (dA, dB, dga, dgb, dya, dyb), _ = _k_merge_bwd(
        dmix, ya, yb, ga, gb, full["w_branch_a"], full["w_branch_b"], full["w_out"], tm_in)
    du, dvb, g_ws, g_bst, g_gln = _k_gmlp_bwd(u, vb, dyb, gm_lnv, ws_b, wst_b, bsp)
    (dq, dk_late, dv_late, dkc, dvc, g_sink), (gath_ws, rcv_fi) = _k_attn_bwd(
        sink, q, k, v, kc, vc, dya, lse, cos, sin, bias,
        comm=_Comm(gather=[g_ws.reshape(N_GROUPS * BLK, BLK)], scatter=[blk_fi]))
    dk, dv = dk_late[BLK:BLK + L], dv_late[BLK:BLK + L]
    blk_a = to_blocks("w_branch_a", _wgrad(ya, dA, "wgrad_a", Q_W, tt))
    blk_b = to_blocks("w_branch_b", _wgrad(yb, dB, "wgrad_b", GM_W, tt))
    blk_o = to_blocks("w_out", _wgrad(merged, dmix, "wgrad_out", D, tt))
    (dP, grad_x, acc_i), _ = _k_in_bwd(dq, dk, dv, du, dvb, dga, dgb, x2, dxp, full["w_in"], modv, tm_in)
    g_ctx, dmodc = _k_ctx_bwd(ctx2, modc, hc, dkc, dvc, w_kv)
    gw_in, (rcv_a, rcv_b, rcv_o) = _wgrad(dP, h, "wgrad_in", tiles.tk_in, tt, comm=_Comm(scatter=[blk_a, blk_b, blk_o]),
                                          extra=(O_K, g_ctx))

    dmod_x = jnp.concatenate([acc_i[0:2], acc_b[4:5], acc_b[0:2], acc_f[2:3]], axis=0)
    small = jnp.concatenate([
        dmod_x, dmodc[0:2], acc_b[2:4], acc_f[0:2],
        jnp.concatenate([g_gln[0:1], g_gln[1:2]], axis=1), g_bst.T.reshape(1, D),
        _pad_rows(g_sink[:, 0:1], D).T, acc_f[3:4]], axis=0)
    rcv_in, gath = _exchange_two_level(to_blocks("w_in", gw_in), small, "exchange_last")
    received = dict(w_in=rcv_in, w_branch_a=rcv_a, w_branch_b=rcv_b, w_out=rcv_o, w_ffn_in=rcv_fi, w_ffn_out=rcv_fo)
    moments = dict(w_in=(m_w_in, v_w_in), w_branch_a=(m_w_branch_a, v_w_branch_a), w_branch_b=(m_w_branch_b, v_w_branch_b),
                   w_out=(m_w_out, v_w_out), w_ffn_in=(m_w_ffn_in, v_w_ffn_in), w_ffn_out=(m_w_ffn_out, v_w_ffn_out))
    names = list(big)
    res = {}
    for kname in names:
        mm, vv = moments[kname]
        R = big[kname].shape[0]
        res[kname] = [tr(kname, r) for r in _adamw_reduce(
            received[kname], big[kname], tr(kname, mm[0]), tr(kname, vv[0]), "adamw_" + kname, 256 if R % 256 == 0 else R // 2)]

    ws2d = lambda a: a.reshape(N_GROUPS * BLK, BLK)
    res_ws = [r.reshape(w_spatial.shape) for r in _adamw_reduce(
        gath_ws, ws2d(w_spatial), ws2d(m_w_spatial), ws2d(v_w_spatial), "adamw_w_spatial", 256)]
    tot, res_small = _small_step(gath, dict(
        b_ada=(b_ada, m_b_ada, v_b_ada), ln1_g=(ln1_g, m_ln1_g, v_ln1_g), ln1_b=(ln1_b, m_ln1_b, v_ln1_b),
        ln2_g=(ln2_g, m_ln2_g, v_ln2_g), ln2_b=(ln2_b, m_ln2_b, v_ln2_b),
        gmlp_ln_g=(gmlp_ln_g, m_gmlp_ln_g, v_gmlp_ln_g), gmlp_ln_b=(gmlp_ln_b, m_gmlp_ln_b, v_gmlp_ln_b),
        b_spatial=(b_spatial, m_b_spatial, v_b_spatial), attn_sink=(attn_sink, m_attn_sink, v_attn_sink)))
    loss = tot[15, 0]

    dmod_rows = jnp.concatenate([gath[:, 0:6, :].reshape(N_DEV, 6 * D),
                                 jnp.concatenate([tot[6:8].reshape(1, 2 * D), jnp.zeros((1, 4 * D), F32)], axis=1),
                                 jnp.zeros((7, 6 * D), F32)], axis=0)
    dmod_my = lax.dynamic_slice(dmod_rows, (0, me * n_ada), (16, n_ada))
    pc = _cctx_partial(dmod_my, w_ada[0])
    (g_wada, d_wada, m2_wada, v2_wada), (pc_all,) = _ada_bwd(
        act, dmod_my, w_ada[0], m_w_ada[0], v_w_ada[0], _Comm(spread=[pc]))
    cc8 = lambda a: _pad_rows(a.reshape(1, D), 8)
    g_cc, d_cc, m2_cc, v2_cc = _cctx_finish(pc_all, cc8(c_ctx), cc8(m_c_ctx), cc8(v_c_ctx))

    order = ["c_ctx", "w_ada", "b_ada", "w_in", "attn_sink", "gmlp_ln_g", "gmlp_ln_b", "w_spatial", "b_spatial",
             "w_branch_a", "w_branch_b", "w_out", "ln1_g", "ln1_b", "w_ffn_in", "w_ffn_out", "ln2_g", "ln2_b"]
    grads, deltas, new_m, new_v = {}, {}, {}, {}
    grads["c_ctx"], deltas["c_ctx"], new_m["c_ctx"], new_v["c_ctx"] = g_cc[0], d_cc[0], m2_cc[0], v2_cc[0]
    grads["w_ada"], deltas["w_ada"], new_m["w_ada"], new_v["w_ada"] = g_wada[None], d_wada[None], m2_wada[None], v2_wada[None]
    for kname in names:
        g, d, m2, v2 = res[kname]
        grads[kname], deltas[kname], new_m[kname], new_v[kname] = g[None], d[None], m2[None], v2[None]
    grads["w_spatial"], deltas["w_spatial"], new_m["w_spatial"], new_v["w_spatial"] = res_ws
    for kname in SMALL_ORDER:
        grads[kname], deltas[kname], new_m[kname], new_v[kname] = res_small[kname]
    return (loss, grad_x[None], *[grads[n] for n in order], *[deltas[n] for n in order],
            *[new_m[n] for n in order], *[new_v[n] for n in order])
```

```python
import functools
import math

import jax
import jax.numpy as jnp
import numpy as np
from jax import lax
from jax.experimental import pallas as pl
from jax.experimental.pallas import tpu as pltpu

F32 = jnp.float32
BF16 = jnp.bfloat16
MESH = pl.DeviceIdType.MESH

N_DEV = 8
D = 1024
HEAD_DIM = 64
N_Q_HEADS = 8
N_KV_HEADS = 2
GQA_GROUP = 4
BLK = 128
Q_W = 512
KV_W = 128
GM_W = 512
N_GROUPS = 8
GROUP_DIM = 64
FFN_H = 2816
IN_W = 3840
O_Q, O_K, O_V, O_U, O_VB, O_GA, O_GB = 0, 512, 640, 768, 1280, 1792, 2816
LN_EPS = 1e-5
NEG_INF = -1e30
ALPHA = 2.0 ** 0.25
ROPE_BASE = 10000.0
ROPE_PAIRS = 16
Q_SCALE = HEAD_DIM ** -0.5
GELU_K0 = math.sqrt(2.0 / math.pi)
GELU_K1 = 0.044715

ADAM_LR = 0.001
ADAM_B1 = 0.9
ADAM_B2 = 0.999
ADAM_EPS = 1e-08
ADAM_WD = 0.01
ADAM_STEP = 10

V7X_VMEM_BYTES = 64 * 1024 * 1024
VMEM_LIMIT = V7X_VMEM_BYTES * 7 // 8
NT = (((1,), (1,)), ((), ()))
TN = (((0,), (0,)), ((), ()))


class _Tiles:
    def __init__(self, L):
        self.wide = min(512, L)
        self.narrow = min(256, L)
        self.tokens = min(2048, L)
        self.tk_in = IN_W // 3
        self.tk_ffn = FFN_H // 2


def _params(sem=None):
    return pltpu.CompilerParams(dimension_semantics=sem, vmem_limit_bytes=VMEM_LIMIT)


def _row(tm, w):
    return pl.BlockSpec((tm, w), lambda i: (i, 0))


def _full(shape):
    nd = len(shape)
    return pl.BlockSpec(shape, lambda i: (0,) * nd)


def _resident(shape):
    nd = len(shape)
    return pl.BlockSpec(shape, lambda i: (0,) * nd, pipeline_mode=pl.Buffered(1))


def _sds(shape, dt):
    return jax.ShapeDtypeStruct(shape, dt)


def _ln(xf):
    mu = jnp.mean(xf, axis=-1, keepdims=True)
    xc = xf - mu
    var = jnp.mean(xc * xc, axis=-1, keepdims=True)
    rstd = lax.rsqrt(var + LN_EPS)
    return xc * rstd, rstd


def _ln_bwd(dn, n, rstd):
    m1 = jnp.mean(dn, axis=-1, keepdims=True)
    m2 = jnp.mean(dn * n, axis=-1, keepdims=True)
    return rstd * (dn - m1 - n * m2)


def _colsum(t):
    return jnp.sum(t, axis=0, keepdims=True)


def _sigmoid(x):
    return 0.5 * jnp.tanh(0.5 * x) + 0.5


def _gelu(x):
    t = jnp.tanh(GELU_K0 * (x + GELU_K1 * (x * x * x)))
    return x * (0.5 * (1.0 + t)), t


def _gelu_grad(x, t):
    return 0.5 * (1.0 + t) + 0.5 * x * (1.0 - t * t) * (GELU_K0 * (1.0 + 3.0 * GELU_K1 * x * x))


def _swap16(t):
    lane = lax.broadcasted_iota(jnp.int32, t.shape, 1)
    return jnp.where((lane & 16) == 0, pltpu.roll(t, 112, 1), pltpu.roll(t, 16, 1))


def _rope(t, cos, sin):
    return t * cos + _swap16(t) * sin


def _unrope(t, cos, sin):
    return t * cos - _swap16(t) * sin


def _adamw(w, g, m, v):
    m2 = ADAM_B1 * m + (1.0 - ADAM_B1) * g
    v2 = ADAM_B2 * v + (1.0 - ADAM_B2) * (g * g)
    m_hat = m2 / (1.0 - ADAM_B1 ** ADAM_STEP)
    v_hat = v2 / (1.0 - ADAM_B2 ** ADAM_STEP)
    delta = -ADAM_LR * (m_hat / (jnp.sqrt(v_hat) + ADAM_EPS) + ADAM_WD * w)
    return delta, m2, v2


def _rope_tables(L):
    inv = (np.float32(ROPE_BASE) ** (-np.arange(ROPE_PAIRS, dtype=np.float32) / np.float32(ROPE_PAIRS))).astype(np.float32)
    t = np.arange(L, dtype=np.int32)
    rows = (t // 64).astype(np.float32)[:, None] * inv
    cols = (t % 64).astype(np.float32)[:, None] * inv
    cr, sr, cc, sc = np.cos(rows), np.sin(rows), np.cos(cols), np.sin(cols)
    cos = np.concatenate([cr, cr, cc, cc], axis=1)
    sin = np.concatenate([-sr, sr, -sc, sc], axis=1)
    return jnp.asarray(np.tile(cos, (1, 2)), F32), jnp.asarray(np.tile(sin, (1, 2)), F32)


def _me():
    return lax.axis_index("x"), lax.axis_index("y"), lax.axis_index("c")


def _peer(mx, my, mc, k):
    return (mx ^ ((k >> 2) & 1), my ^ ((k >> 1) & 1), mc ^ (k & 1))


class _Comm:
    def __init__(self, gather=(), scatter=(), spread=()):
        self.kinds = ["gather"] * len(gather) + ["scatter"] * len(scatter) + ["spread"] * len(spread)
        self.args = list(gather) + list(scatter) + list(spread)
        self.n = len(self.args)

    def out_shape(self):
        return [_sds(a.shape if k == "scatter" else (N_DEV,) + a.shape, a.dtype) for k, a in zip(self.kinds, self.args)]

    def specs(self):
        return [pl.BlockSpec(memory_space=pl.ANY)] * self.n

    def scratch(self):
        return [pltpu.SemaphoreType.DMA((7 * self.n,)), pltpu.SemaphoreType.DMA((7 * self.n,)),
                pltpu.SemaphoreType.DMA((self.n,))]

    def _plan(self, x_refs, out_refs, send_sems, recv_sems, local_sems):
        mx, my, mc = _me()
        me = 4 * mx + 2 * my + mc
        here, sibling = (mx, my, mc), (mx, my, 1 - mc)
        chips = [(1 - mx, my), (mx, 1 - my), (1 - mx, 1 - my)]
        local, first, last = [], [], []
        relay = [[], [], []]
        for a, kind in enumerate(self.kinds):
            x, out = x_refs[a], out_refs[a]

            def rc(k, src, dst, to):
                return pltpu.make_async_remote_copy(
                    src_ref=src, dst_ref=dst, send_sem=send_sems.at[7 * a + k], recv_sem=recv_sems.at[7 * a + k],
                    device_id=to, device_id_type=MESH)

            if kind == "gather":
                local.append(pltpu.make_async_copy(x, out.at[me], local_sems.at[a]))
                first.append(rc(0, x, out.at[me], sibling))
                last.append(rc(0, x, out.at[me ^ 1], here))
                for j, (cx, cy) in enumerate(chips):
                    first.append(rc(1 + j, x, out.at[me], (cx, cy, mc)))
                    landed = out.at[4 * cx + 2 * cy + mc]
                    relay[j].append((rc(1 + j, x, landed, here), rc(4 + j, landed, landed, sibling)))
                    last.append(rc(4 + j, x, out.at[4 * cx + 2 * cy + 1 - mc], here))
            else:
                own = x.at[me] if kind == "scatter" else x
                local.append(pltpu.make_async_copy(own, out.at[me], local_sems.at[a]))
                for k in range(1, N_DEV):
                    src = x.at[me ^ k] if kind == "scatter" else x
                    first.append(rc(k - 1, src, out.at[me], _peer(mx, my, mc, k)))
                    last.append(rc(k - 1, own, out.at[me ^ k], here))
        return local, first, relay[0] + relay[1] + relay[2], last

    def start(self, *refs):
        local, first, _, _ = self._plan(*refs)
        for cp in local + first:
            cp.start()

    def finish(self, *refs):
        local, first, relay, last = self._plan(*refs)
        for arrival, onward in relay:
            arrival.wait_recv()
            onward.start()
        for cp in last:
            cp.wait_recv()
        for cp in first:
            cp.wait_send()
        for _, onward in relay:
            onward.wait_send()
        for cp in local:
            cp.wait()


def _call(body, *, name, grid, in_specs, out_specs, out_shape, args, scratch=(), comm=None, aliases=None):
    params = _params(("arbitrary",) * len(grid))

    def at(end):
        conds = [pl.program_id(d) == (n - 1 if end else 0) for d, n in enumerate(grid)]
        return functools.reduce(lambda p, q: p & q, conds)

    if comm is None:
        res = pl.pallas_call(
            body, name=name, grid=grid, in_specs=list(in_specs), out_specs=list(out_specs), out_shape=list(out_shape),
            scratch_shapes=list(scratch), input_output_aliases=aliases or {}, compiler_params=params)(*args)
        return list(res), []
    n_in, n_out, n_scr, cn = len(in_specs), len(out_specs), len(scratch), comm.n

    def hosted(*refs):
        ins, refs = refs[:n_in], refs[n_in:]
        cins, refs = refs[:cn], refs[cn:]
        outs, refs = refs[:n_out], refs[n_out:]
        couts, refs = refs[:cn], refs[cn:]
        scr, sems = refs[:n_scr], refs[n_scr:]

        @pl.when(at(False))
        def _():
            comm.start(cins, couts, *sems)

        body(*ins, *outs, *scr)

        @pl.when(at(True))
        def _():
            comm.finish(cins, couts, *sems)

    res = pl.pallas_call(
        hosted, name=name, grid=grid, in_specs=list(in_specs) + comm.specs(), out_specs=list(out_specs) + comm.specs(),
        out_shape=list(out_shape) + comm.out_shape(), scratch_shapes=list(scratch) + comm.scratch(),
        input_output_aliases=aliases or {}, compiler_params=params)(*args, *comm.args)
    return list(res[:n_out]), list(res[n_out:])


def _exchange_two_level(blk, small, name):
    _, R, C = blk.shape
    rows = small.shape[0]

    def body(blk_ref, small_ref, stage_ref, out_ref, gath_ref, a_scr, b_scr, t_scr, s1, r1, s3, r3, ss, rs, lsem):
        mx, my, mc = _me()
        me = 4 * mx + 2 * my + mc
        mine = 2 * mx + my
        here, sibling = (mx, my, mc), (mx, my, 1 - mc)

        def rc(src, dst, send, recv, to):
            return pltpu.make_async_remote_copy(src_ref=src, dst_ref=dst, send_sem=send, recv_sem=recv,
                                                device_id=to, device_id_type=MESH)

        own_small = pltpu.make_async_copy(small_ref, gath_ref.at[me], lsem.at[0])
        own_small.start()
        spread = [rc(small_ref, gath_ref.at[me], ss.at[k - 1], rs.at[k - 1], _peer(mx, my, mc, k)) for k in range(1, N_DEV)]
        to_sib = [rc(blk_ref.at[2 * p + 1 - mc], stage_ref.at[p], s1.at[p], r1.at[p], sibling) for p in range(4)]
        for cp in spread + to_sib:
            cp.start()
        own = [pltpu.make_async_copy(blk_ref.at[2 * p + mc], a_scr.at[p], lsem.at[1 + p]) for p in range(4)]
        for cp in own:
            cp.start()
        from_sib = []
        for p in range(4):
            rc(blk_ref.at[2 * p + 1 - mc], stage_ref.at[p], s1.at[p], r1.at[p], here).wait_recv()
            cp = pltpu.make_async_copy(stage_ref.at[p], b_scr.at[p], lsem.at[5 + p])
            cp.start()
            from_sib.append(cp)
        for cp in own + from_sib:
            cp.wait()
        t_scr[...] = (a_scr[...].astype(F32) + b_scr[...].astype(F32)).astype(BF16)
        keep = pltpu.make_async_copy(t_scr.at[mine], out_ref.at[mine], lsem.at[9])
        keep.start()
        onward = [rc(t_scr.at[mine ^ k], out_ref.at[mine], s3.at[k - 1], r3.at[k - 1], (mx ^ (k >> 1), my ^ (k & 1), mc))
                  for k in range(1, 4)]
        for cp in onward:
            cp.start()
        for k in range(1, 4):
            rc(t_scr.at[mine], out_ref.at[mine ^ k], s3.at[k - 1], r3.at[k - 1], here).wait_recv()
        for k in range(1, N_DEV):
            rc(small_ref, gath_ref.at[me ^ k], ss.at[k - 1], rs.at[k - 1], here).wait_recv()
        for cp in spread + to_sib + onward:
            cp.wait_send()
        keep.wait()
        own_small.wait()

    any_spec = pl.BlockSpec(memory_space=pl.ANY)
    dma = pltpu.SemaphoreType.DMA
    _, out, gath = pl.pallas_call(
        body, name=name,
        in_specs=[any_spec, any_spec], out_specs=[any_spec] * 3,
        out_shape=[_sds((4, R, C), BF16), _sds((4, R, C), BF16), _sds((N_DEV, rows, D), F32)],
        scratch_shapes=[pltpu.VMEM((4, R, C), BF16)] * 3
                       + [dma((4,)), dma((4,)), dma((3,)), dma((3,)), dma((N_DEV - 1,)), dma((N_DEV - 1,)), dma((10,))],
        compiler_params=pltpu.CompilerParams(vmem_limit_bytes=VMEM_LIMIT),
    )(blk, small)
    return out, gath


def _exchange_rows(x_ref, out_ref, send_sems, recv_sems):
    mx, my, mc = _me()
    me = 4 * mx + 2 * my + mc
    out_ref[pl.ds(me, 1)] = x_ref[...][None]
    sends = []
    for k in range(1, N_DEV):
        cp = pltpu.make_async_remote_copy(
            src_ref=x_ref, dst_ref=out_ref.at[me], send_sem=send_sems.at[k - 1], recv_sem=recv_sems.at[k - 1],
            device_id=_peer(mx, my, mc, k), device_id_type=MESH)
        cp.start()
        sends.append(cp)
    for k in range(1, N_DEV):
        pltpu.make_async_remote_copy(
            src_ref=x_ref, dst_ref=out_ref.at[me ^ k], send_sem=send_sems.at[k - 1], recv_sem=recv_sems.at[k - 1],
            device_id=(mx, my, mc), device_id_type=MESH).wait_recv()
    for cp in sends:
        cp.wait_send()


def _prologue(c8, cctx8, w_ada, b_my, comm):
    nw = w_ada.shape[1]

    def body(c_ref, cctx_ref, w_ref, b_ref, act_ref, mod_ref, cmine_scr, call_scr, mine_scr, mall_scr, s1, r1, s2, r2):
        cmine_scr[...] = c_ref[...]
        _exchange_rows(cmine_scr, call_scr, s1, r1)
        rows = [call_scr[d][0:1, :] for d in range(N_DEV)] + [cctx_ref[0:1, :], jnp.zeros((7, D), F32)]
        s = jnp.concatenate(rows, axis=0)
        act = s * _sigmoid(s)
        act_ref[...] = act
        mine_scr[...] = jnp.dot(act.astype(BF16), w_ref[...].astype(BF16), preferred_element_type=F32) + b_ref[...]
        _exchange_rows(mine_scr, mall_scr, s2, r2)
        mod_ref[...] = mall_scr[...]

    sems = [pltpu.SemaphoreType.DMA((N_DEV - 1,))] * 4
    (act, mod), got = _call(
        body, name="prologue", grid=(1,),
        in_specs=[_full((8, D)), _full((8, D)), _full((D, nw)), _full((1, nw))],
        out_specs=[_full((16, D)), _full((N_DEV, 16, nw))],
        out_shape=[_sds((16, D), F32), _sds((N_DEV, 16, nw), F32)],
        scratch=[pltpu.VMEM((8, D), F32), pltpu.VMEM((N_DEV, 8, D), F32), pltpu.VMEM((16, nw), F32),
                 pltpu.VMEM((N_DEV, 16, nw), F32)] + sems,
        args=(c8, cctx8, w_ada, b_my), comm=comm)
    return act, mod, got


def _gather_rows(x, name):
    def body(x_ref, out_ref, send_sems, recv_sems):
        _exchange_rows(x_ref, out_ref, send_sems, recv_sems)

    return pl.pallas_call(
        body, name=name,
        out_shape=_sds((N_DEV,) + x.shape, x.dtype),
        in_specs=[pl.BlockSpec(memory_space=pltpu.VMEM)],
        out_specs=pl.BlockSpec(memory_space=pltpu.VMEM),
        scratch_shapes=[pltpu.SemaphoreType.DMA((N_DEV - 1,)), pltpu.SemaphoreType.DMA((N_DEV - 1,))],
        compiler_params=pltpu.CompilerParams(vmem_limit_bytes=VMEM_LIMIT),
    )(x)


def _ada_bwd(act, dmod_my, w_ada, m, v, tr=256):
    nw = w_ada.shape[1]

    def body(act_ref, dm_ref, w_ref, m_ref, v_ref, g_ref, d_ref, m2_ref, v2_ref, pc_ref):
        dm = dm_ref[...].astype(BF16)
        g = lax.dot_general(act_ref[...].astype(BF16), dm, TN, preferred_element_type=F32)
        w = w_ref[...]
        delta, m2, v2 = _adamw(w, g, m_ref[...], v_ref[...])
        g_ref[...] = g
        d_ref[...] = delta
        m2_ref[...] = m2
        v2_ref[...] = v2
        pc_ref[...] = lax.dot_general(dm[8:16, :], w.astype(BF16), NT, preferred_element_type=F32)

    wspec = _row(tr, nw)
    return pl.pallas_call(
        body, name="ada_bwd", grid=(D // tr,),
        in_specs=[pl.BlockSpec((16, tr), lambda i: (0, i)), _full((16, nw)), wspec, wspec, wspec],
        out_specs=[wspec, wspec, wspec, wspec, pl.BlockSpec((8, tr), lambda i: (0, i))],
        out_shape=[_sds((D, nw), F32)] * 4 + [_sds((8, D), F32)],
        compiler_params=_params(("arbitrary",)),
    )(act, dmod_my, w_ada, m, v)


def _k_in(x, modv, w_in, cos, sin, tm, comm=None):
    L = x.shape[0]

    def body(x_ref, mod_ref, w_ref, cos_ref, sin_ref, h_ref, q_ref, k_ref, v_ref, u_ref, vb_ref, ga_ref, gb_ref):
        n, _ = _ln(x_ref[...])
        h = (n * (1.0 + mod_ref[1:2, :]) + mod_ref[0:1, :]).astype(BF16)
        h_ref[...] = h
        c, s = cos_ref[...], sin_ref[...]

        def proj(lo, width):
            return lax.dot_general(h, w_ref[lo:lo + width, :], NT, preferred_element_type=F32)

        for i in range(4):
            q_ref[:, i * 128:(i + 1) * 128] = (_rope(proj(O_Q + i * 128, 128), c, s) * Q_SCALE).astype(BF16)
        k_ref[...] = _rope(proj(O_K, KV_W), c, s).astype(BF16)
        v_ref[...] = proj(O_V, KV_W).astype(BF16)
        u_ref[...] = proj(O_U, GM_W).astype(BF16)
        vb_ref[...] = proj(O_VB, GM_W).astype(BF16)
        ga_ref[...] = proj(O_GA, D).astype(BF16)
        gb_ref[...] = proj(O_GB, D).astype(BF16)

    widths = [D, Q_W, KV_W, KV_W, GM_W, GM_W, D, D]
    return _call(
        body, name="fwd_in", grid=(L // tm,),
        in_specs=[_row(tm, D), _full((8, D)), _resident((IN_W, D)), _row(tm, 128), _row(tm, 128)],
        out_specs=[_row(tm, w) for w in widths],
        out_shape=[_sds((L, w), BF16) for w in widths],
        args=(x, modv, w_in, cos, sin), comm=comm)


def _k_ctx(ctx, modc, w_kv):
    C = ctx.shape[0]

    def body(c_ref, mod_ref, w_ref, hc_ref, kc_ref, vc_ref):
        n, _ = _ln(c_ref[...])
        hc = (n * (1.0 + mod_ref[1:2, :]) + mod_ref[0:1, :]).astype(BF16)
        hc_ref[...] = hc
        kv = lax.dot_general(hc, w_ref[...], NT, preferred_element_type=F32)
        kc_ref[...] = kv[:, :KV_W].astype(BF16)
        vc_ref[...] = kv[:, KV_W:].astype(BF16)

    return pl.pallas_call(
        body, name="fwd_ctx", grid=(1,),
        in_specs=[_full((C, D)), _full((8, D)), _full((2 * KV_W, D))],
        out_specs=[_full((C, D)), _full((C, KV_W)), _full((C, KV_W))],
        out_shape=[_sds((C, D), BF16), _sds((C, KV_W), BF16), _sds((C, KV_W), BF16)],
        compiler_params=_params(("arbitrary",)),
    )(ctx, modc, w_kv)


def _attn_bias():
    r = (np.arange(GQA_GROUP * BLK) & (BLK - 1))[:, None]
    j = np.arange(3 * BLK)[None, :]
    band = np.abs(j - BLK - r) <= BLK
    variants = [band & (j >= BLK), band, band & (j < 2 * BLK)]
    return jnp.asarray(np.stack([np.where(v, 0.0, NEG_INF) for v in variants]), F32)


def _masked(s, bias, C):
    return jnp.concatenate([s[:, :C], s[:, C:] + bias], axis=1)


def _sink_col(sink_ref, hk):
    grp = lax.broadcasted_iota(jnp.int32, (GQA_GROUP * BLK, 1), 0) >> 7
    col = jnp.full((GQA_GROUP * BLK, 1), sink_ref[hk * GQA_GROUP], F32)
    for g in range(1, GQA_GROUP):
        col = jnp.where(grp == g, sink_ref[hk * GQA_GROUP + g], col)
    return col


ATTN_FWD_BLOCKS = 4


def _k_attn(sink, q, k, v, kc, vc, bias, comm=None):
    L = q.shape[0]
    C = kc.shape[0]
    nb = L // BLK
    nq = min(ATTN_FWD_BLOCKS, nb)
    steps = nb // nq

    def body(sink_ref, q_ref, kp_ref, km_ref, kx_ref, vp_ref, vm_ref, vx_ref, kc_ref, vc_ref, bias_ref, ya_ref, lse_ref):
        i = pl.program_id(0)
        chains = [(qb, hk) for qb in range(nq) for hk in range(N_KV_HEADS)]

        def band(qb):
            first = jnp.where(i == 0, 0, 1) if qb == 0 else 1
            return bias_ref[jnp.where(i == steps - 1, 2, first) if qb == nq - 1 else first]

        def keys(ctx_ref, p_ref, m_ref, x_ref, qb, hk):
            sl = slice(hk * HEAD_DIM, (hk + 1) * HEAD_DIM)
            blocks = [p_ref[:, sl]] + [m_ref[j * BLK:(j + 1) * BLK, sl] for j in range(nq)] + [x_ref[:, sl]]
            return jnp.concatenate([ctx_ref[:, sl]] + blocks[qb:qb + 3], axis=0)

        def queries(qb, hk):
            return jnp.concatenate(
                [q_ref[qb * BLK:(qb + 1) * BLK, (hk * GQA_GROUP + g) * HEAD_DIM:(hk * GQA_GROUP + g + 1) * HEAD_DIM]
                 for g in range(GQA_GROUP)], axis=0)

        def scores(qb, hk):
            return _masked(lax.dot_general(queries(qb, hk), keys(kc_ref, kp_ref, km_ref, kx_ref, qb, hk), NT,
                                           preferred_element_type=F32), band(qb), C)

        ahead = 2
        s = [scores(*c) for c in chains[:ahead]]
        for n, (qb, hk) in enumerate(chains):
            if n + ahead < len(chains):
                s.append(scores(*chains[n + ahead]))
            s_ = s[n]
            sink_c = _sink_col(sink_ref, hk)
            m = jnp.maximum(jnp.max(s_, axis=1, keepdims=True), sink_c)
            p = jnp.exp(s_ - m)
            den = jnp.sum(p, axis=1, keepdims=True) + jnp.exp(sink_c - m)
            o = jnp.dot(p.astype(BF16), keys(vc_ref, vp_ref, vm_ref, vx_ref, qb, hk), preferred_element_type=F32) * (1.0 / den)
            lse = m + jnp.log(den)
            rows = slice(qb * BLK, (qb + 1) * BLK)
            for g in range(GQA_GROUP):
                h = hk * GQA_GROUP + g
                ya_ref[rows, h * HEAD_DIM:(h + 1) * HEAD_DIM] = o[g * BLK:(g + 1) * BLK, :].astype(BF16)
                lse_ref[rows, h:h + 1] = lse[g * BLK:(g + 1) * BLK, :]

    kv3 = [pl.BlockSpec((BLK, KV_W), lambda i: (jnp.maximum(nq * i - 1, 0), 0)),
           pl.BlockSpec((nq * BLK, KV_W), lambda i: (i, 0)),
           pl.BlockSpec((BLK, KV_W), lambda i: (jnp.minimum(nq * i + nq, nb - 1), 0))]
    return _call(
        body, name="fwd_attn", grid=(steps,),
        in_specs=[pl.BlockSpec(memory_space=pltpu.SMEM), _row(nq * BLK, Q_W)] + kv3 + kv3
                 + [_full((C, KV_W)), _full((C, KV_W)), _full((3, GQA_GROUP * BLK, 3 * BLK))],
        out_specs=[_row(nq * BLK, Q_W), _row(nq * BLK, N_Q_HEADS)],
        out_shape=[_sds((L, Q_W), BF16), _sds((L, N_Q_HEADS), F32)],
        args=(sink, q, k, k, k, v, v, v, kc, vc, bias), comm=comm)


GMLP_CHUNKS = 4


def _split_pair(t):
    low = lax.broadcasted_iota(jnp.int32, t.shape, 1) < GROUP_DIM
    zero = jnp.zeros_like(t)
    return jnp.where(low, t, zero), jnp.where(low, zero, t)


def _gmlp_spatial(w_ref, t_b, nch):
    rows = []
    for c in range(nch):
        tiles = []
        for pr in range(N_GROUPS // 2):
            lo, hi = _split_pair(t_b[c * BLK:(c + 1) * BLK, pr * 128:(pr + 1) * 128])
            tiles.append(jnp.dot(w_ref[2 * pr], lo, preferred_element_type=F32)
                         + jnp.dot(w_ref[2 * pr + 1], hi, preferred_element_type=F32))
        rows.append(jnp.concatenate(tiles, axis=1))
    return jnp.concatenate(rows, axis=0)


def _gmlp_fwd_vals(u, vb, lnv_ref, ws_ref, bsp_ref, nch):
    uf = u.astype(F32)
    vf = vb.astype(F32)
    gu, tu = _gelu(uf)
    gv, tv = _gelu(vf)
    vhat, rstd = _ln(gv)
    vn = (vhat * lnv_ref[0:1, :] + lnv_ref[1:2, :]).astype(BF16)
    s = _gmlp_spatial(ws_ref, vn, nch) + jnp.concatenate([bsp_ref[...]] * nch, axis=0)
    return uf, vf, gu, tu, tv, vhat, rstd, vn, s


def _k_gmlp(u, vb, lnv, ws, bsp):
    L = u.shape[0]
    nch = min(GMLP_CHUNKS, L // BLK)
    tm = nch * BLK

    def body(u_ref, vb_ref, lnv_ref, ws_ref, bsp_ref, yb_ref):
        _, _, gu, _, _, _, _, _, s = _gmlp_fwd_vals(u_ref[...], vb_ref[...], lnv_ref, ws_ref, bsp_ref, nch)
        yb_ref[...] = (gu * s).astype(BF16)

    return pl.pallas_call(
        body, name="fwd_gmlp", grid=(L // tm,),
        in_specs=[_row(tm, GM_W), _row(tm, GM_W), _full((8, GM_W)), _full((N_GROUPS, BLK, BLK)), _full((BLK, GM_W))],
        out_specs=_row(tm, GM_W),
        out_shape=_sds((L, GM_W), BF16),
        compiler_params=_params(("arbitrary",)),
    )(u, vb, lnv, ws, bsp)


def _k_merge(x, ya, yb, ga, gb, w_a, w_b, w_o, modv, lnv, tm):
    L = x.shape[0]

    def body(x_ref, ya_ref, yb_ref, ga_ref, gb_ref, wa_ref, wb_ref, wo_ref, mod_ref, ln_ref,
             mg_ref, mix_ref, xm_ref, h2_ref):
        a = jnp.dot(ya_ref[...], wa_ref[...], preferred_element_type=F32)
        b = jnp.dot(yb_ref[...], wb_ref[...], preferred_element_type=F32)
        merged = (_sigmoid(ga_ref[...].astype(F32)) * a + _sigmoid(gb_ref[...].astype(F32)) * b).astype(BF16)
        mg_ref[...] = merged
        mix = jnp.dot(merged, wo_ref[...], preferred_element_type=F32)
        mix_ref[...] = mix.astype(BF16)
        r1 = ALPHA * x_ref[...] + mod_ref[2:3, :] * mix
        r1hat, _ = _ln(r1)
        xm = r1hat * ln_ref[0:1, :] + ln_ref[1:2, :]
        xm_ref[...] = xm
        n2, _ = _ln(xm)
        h2_ref[...] = (n2 * (1.0 + mod_ref[4:5, :]) + mod_ref[3:4, :]).astype(BF16)

    return pl.pallas_call(
        body, name="fwd_merge", grid=(L // tm,),
        in_specs=[_row(tm, D), _row(tm, Q_W), _row(tm, GM_W), _row(tm, D), _row(tm, D),
                  _resident((Q_W, D)), _resident((GM_W, D)), _resident((D, D)), _full((8, D)), _full((8, D))],
        out_specs=[_row(tm, D)] * 4,
        out_shape=[_sds((L, D), BF16), _sds((L, D), BF16), _sds((L, D), F32), _sds((L, D), BF16)],
        compiler_params=_params(("arbitrary",)),
    )(x, ya, yb, ga, gb, w_a, w_b, w_o, modv, lnv)


FFN_CH = FFN_H // 2


def _k_ffn(h2, xm, tgt, w_fi, w_fo, modv, lnv, tm):
    L = h2.shape[0]

    def body(h2_ref, xm_ref, t_ref, wi_ref, wo_ref, mod_ref, ln_ref, gate_ref, up_ref, a_ref, dr2_ref, df_ref, acc_ref):
        @pl.when(pl.program_id(0) == 0)
        def _():
            acc_ref[...] = jnp.zeros_like(acc_ref)

        h2v = h2_ref[...]
        f = jnp.zeros((tm, D), F32)
        for j in range(FFN_H // FFN_CH):
            lo = j * FFN_CH
            gate = lax.dot_general(h2v, wi_ref[lo:lo + FFN_CH, :], NT, preferred_element_type=F32)
            up = lax.dot_general(h2v, wi_ref[FFN_H + lo:FFN_H + lo + FFN_CH, :], NT, preferred_element_type=F32)
            act = (gate * _sigmoid(gate) * up).astype(BF16)
            gate_ref[:, lo:lo + FFN_CH] = gate.astype(BF16)
            up_ref[:, lo:lo + FFN_CH] = up.astype(BF16)
            a_ref[:, lo:lo + FFN_CH] = act
            f = f + jnp.dot(act, wo_ref[lo:lo + FFN_CH, :], preferred_element_type=F32)
        gate2 = mod_ref[5:6, :]
        r2 = ALPHA * xm_ref[...] + gate2 * f
        r2hat, rstd = _ln(r2)
        y = r2hat * ln_ref[2:3, :] + ln_ref[3:4, :]
        err = y - t_ref[...]
        dy = err * (1.0 / D)
        dr2 = _ln_bwd(dy * ln_ref[2:3, :], r2hat, rstd)
        dr2_ref[...] = dr2
        df_ref[...] = (gate2 * dr2).astype(BF16)
        acc_ref[0:1, :] += _colsum(dy * r2hat)
        acc_ref[1:2, :] += _colsum(dy)
        acc_ref[2:3, :] += _colsum(dr2 * f)
        acc_ref[3:4, :] += _colsum(err * err) * (0.5 / D)

    return pl.pallas_call(
        body, name="fwd_ffn", grid=(L // tm,),
        in_specs=[_row(tm, D), _row(tm, D), _row(tm, D), _resident((2 * FFN_H, D)), _resident((FFN_H, D)),
                  _full((8, D)), _full((8, D))],
        out_specs=[_row(tm, FFN_H)] * 3 + [_row(tm, D), _row(tm, D), _full((8, D))],
        out_shape=[_sds((L, FFN_H), BF16)] * 3 + [_sds((L, D), F32), _sds((L, D), BF16), _sds((8, D), F32)],
        compiler_params=_params(("arbitrary",)),
    )(h2, xm, tgt, w_fi, w_fo, modv, lnv)


def _k_ffn_bwd(df, gate, up, xm, dr2, x, mix, w_fi, w_fo, modv, lnv, tm):
    L = df.shape[0]

    def body(df_ref, gate_ref, up_ref, xm_ref, dr2_ref, x_ref, mix_ref, wi_ref, wo_ref, mod_ref, ln_ref,
             dF_ref, dmix_ref, dxp_ref, acc_ref):
        @pl.when(pl.program_id(0) == 0)
        def _():
            acc_ref[...] = jnp.zeros_like(acc_ref)

        dfv = df_ref[...]
        chunks = [j * FFN_CH for j in range(FFN_H // FFN_CH)]
        das = [lax.dot_general(dfv, wo_ref[lo:lo + FFN_CH, :], NT, preferred_element_type=F32) for lo in chunks]
        n2, rstd2 = _ln(xm_ref[...])
        mixf = mix_ref[...].astype(F32)
        gate1 = mod_ref[2:3, :]
        r1hat, rstd1 = _ln(ALPHA * x_ref[...] + gate1 * mixf)
        dh2 = jnp.zeros((tm, D), F32)
        for lo, da in zip(chunks, das):
            gate = gate_ref[:, lo:lo + FFN_CH].astype(F32)
            upv = up_ref[:, lo:lo + FFN_CH].astype(F32)
            sg = _sigmoid(gate)
            d_gate = (da * upv * (sg * (1.0 + gate * (1.0 - sg)))).astype(BF16)
            d_up = (da * (gate * sg)).astype(BF16)
            dF_ref[:, lo:lo + FFN_CH] = d_gate
            dF_ref[:, FFN_H + lo:FFN_H + lo + FFN_CH] = d_up
            dh2 = dh2 + jnp.dot(d_gate, wi_ref[lo:lo + FFN_CH, :], preferred_element_type=F32)
            dh2 = dh2 + jnp.dot(d_up, wi_ref[FFN_H + lo:FFN_H + lo + FFN_CH, :], preferred_element_type=F32)
        acc_ref[0:1, :] += _colsum(dh2)
        acc_ref[1:2, :] += _colsum(dh2 * n2)
        dxm = ALPHA * dr2_ref[...] + _ln_bwd(dh2 * (1.0 + mod_ref[4:5, :]), n2, rstd2)
        acc_ref[2:3, :] += _colsum(dxm * r1hat)
        acc_ref[3:4, :] += _colsum(dxm)
        dr1 = _ln_bwd(dxm * ln_ref[0:1, :], r1hat, rstd1)
        dmix_ref[...] = (gate1 * dr1).astype(BF16)
        dxp_ref[...] = ALPHA * dr1
        acc_ref[4:5, :] += _colsum(dr1 * mixf)

    return pl.pallas_call(
        body, name="bwd_ffn", grid=(L // tm,),
        in_specs=[_row(tm, D), _row(tm, FFN_H), _row(tm, FFN_H), _row(tm, D), _row(tm, D), _row(tm, D), _row(tm, D),
                  _resident((2 * FFN_H, D)), _resident((FFN_H, D)), _full((8, D)), _full((8, D))],
        out_specs=[_row(tm, 2 * FFN_H), _row(tm, D), _row(tm, D), _full((8, D))],
        out_shape=[_sds((L, 2 * FFN_H), BF16), _sds((L, D), BF16), _sds((L, D), F32), _sds((8, D), F32)],
        compiler_params=_params(("arbitrary",)),
    )(df, gate, up, xm, dr2, x, mix, w_fi, w_fo, modv, lnv)


def _k_merge_bwd(dmix, ya, yb, ga, gb, w_a, w_b, w_o, tm, comm=None):
    L = dmix.shape[0]

    def body(dmix_ref, ya_ref, yb_ref, ga_ref, gb_ref, wa_ref, wb_ref, wo_ref,
             dA_ref, dB_ref, dga_ref, dgb_ref, dya_ref, dyb_ref):
        dmg = lax.dot_general(dmix_ref[...], wo_ref[...], NT, preferred_element_type=F32)
        a = jnp.dot(ya_ref[...], wa_ref[...], preferred_element_type=F32)
        sa = _sigmoid(ga_ref[...].astype(F32))
        dA = (dmg * sa).astype(BF16)
        dA_ref[...] = dA
        dga_ref[...] = (dmg * a * (sa * (1.0 - sa))).astype(BF16)
        dya_ref[...] = lax.dot_general(dA, wa_ref[...], NT, preferred_element_type=F32).astype(BF16)
        b = jnp.dot(yb_ref[...], wb_ref[...], preferred_element_type=F32)
        sb = _sigmoid(gb_ref[...].astype(F32))
        dB = (dmg * sb).astype(BF16)
        dB_ref[...] = dB
        dgb_ref[...] = (dmg * b * (sb * (1.0 - sb))).astype(BF16)
        dyb_ref[...] = lax.dot_general(dB, wb_ref[...], NT, preferred_element_type=F32).astype(BF16)

    return _call(
        body, name="bwd_merge", grid=(L // tm,),
        in_specs=[_row(tm, D), _row(tm, Q_W), _row(tm, GM_W), _row(tm, D), _row(tm, D),
                  _resident((Q_W, D)), _resident((GM_W, D)), _resident((D, D))],
        out_specs=[_row(tm, D)] * 4 + [_row(tm, Q_W), _row(tm, GM_W)],
        out_shape=[_sds((L, D), BF16)] * 4 + [_sds((L, Q_W), BF16), _sds((L, GM_W), BF16)],
        args=(dmix, ya, yb, ga, gb, w_a, w_b, w_o), comm=comm)


def _k_gmlp_bwd(u, vb, dyb, lnv, ws, wst, bsp):
    L = u.shape[0]
    nch = min(GMLP_CHUNKS, L // BLK)
    tm = nch * BLK

    def body(u_ref, vb_ref, dyb_ref, lnv_ref, ws_ref, wst_ref, bsp_ref, du_ref, dvb_ref, gws_ref, gbst_ref, gln_ref):
        @pl.when(pl.program_id(0) == 0)
        def _():
            gws_ref[...] = jnp.zeros_like(gws_ref)
            gbst_ref[...] = jnp.zeros_like(gbst_ref)
            gln_ref[...] = jnp.zeros_like(gln_ref)

        uf, vf, gu, tu, tv, vhat, rstd, vn, s = _gmlp_fwd_vals(u_ref[...], vb_ref[...], lnv_ref, ws_ref, bsp_ref, nch)
        dyb_f = dyb_ref[...].astype(F32)
        du_ref[...] = (dyb_f * s * _gelu_grad(uf, tu)).astype(BF16)
        ds = dyb_f * gu
        ds_b = ds.astype(BF16)
        for pr in range(N_GROUPS // 2):
            lanes = slice(pr * 128, (pr + 1) * 128)
            gw_lo = gw_hi = ds_sum = None
            for c in range(nch):
                rows = slice(c * BLK, (c + 1) * BLK)
                lo, hi = _split_pair(ds_b[rows, lanes])
                t_lo = lax.dot_general(lo, vn[rows, lanes], NT, preferred_element_type=F32)
                t_hi = lax.dot_general(hi, vn[rows, lanes], NT, preferred_element_type=F32)
                gw_lo = t_lo if c == 0 else gw_lo + t_lo
                gw_hi = t_hi if c == 0 else gw_hi + t_hi
                ds_sum = ds[rows, lanes] if c == 0 else ds_sum + ds[rows, lanes]
            gws_ref[2 * pr] += gw_lo
            gws_ref[2 * pr + 1] += gw_hi
            b_lo, b_hi = _split_pair(ds_sum)
            gbst_ref[:, 2 * pr:2 * pr + 1] += jnp.sum(b_lo, axis=1, keepdims=True)
            gbst_ref[:, 2 * pr + 1:2 * pr + 2] += jnp.sum(b_hi, axis=1, keepdims=True)
        dvn = _gmlp_spatial(wst_ref, ds_b, nch)
        gln_ref[0:1, :] += _colsum(dvn * vhat)
        gln_ref[1:2, :] += _colsum(dvn)
        dgv = _ln_bwd(dvn * lnv_ref[0:1, :], vhat, rstd)
        dvb_ref[...] = (dgv * _gelu_grad(vf, tv)).astype(BF16)

    return pl.pallas_call(
        body, name="bwd_gmlp", grid=(L // tm,),
        in_specs=[_row(tm, GM_W)] * 3 + [_full((8, GM_W)), _full((N_GROUPS, BLK, BLK)), _full((N_GROUPS, BLK, BLK)),
                                         _full((BLK, GM_W))],
        out_specs=[_row(tm, GM_W), _row(tm, GM_W), _full((N_GROUPS, BLK, BLK)), _full((BLK, N_GROUPS)), _full((8, GM_W))],
        out_shape=[_sds((L, GM_W), BF16), _sds((L, GM_W), BF16), _sds((N_GROUPS, BLK, BLK), F32),
                   _sds((BLK, N_GROUPS), F32), _sds((8, GM_W), F32)],
        compiler_params=_params(("arbitrary",)),
    )(u, vb, dyb, lnv, ws, wst, bsp)


ATTN_BWD_BLOCKS = 2


def _k_attn_bwd(sink, q, k, v, kc, vc, dya, lse, cos, sin, bias, comm=None):
    L = q.shape[0]
    C = kc.shape[0]
    nb = L // BLK
    nq = min(ATTN_BWD_BLOCKS, nb)
    steps = nb // nq
    NK = C + 3 * BLK
    chains = [(qb, hk) for qb in range(nq) for hk in range(N_KV_HEADS)]

    def body(sink_ref, q_ref, kp_ref, km_ref, kx_ref, vp_ref, vm_ref, vx_ref, kc_ref, vc_ref, do_ref, lse_ref,
             cq_ref, sq_ref, cl_ref, sl_ref, bias_ref,
             dq_ref, dk_ref, dv_ref, dkc_ref, dvc_ref, dsink_ref,
             dq_scr, ck_scr, cv_scr, k1_acc, k2_acc, v1_acc, v2_acc):
        i = pl.program_id(0)

        @pl.when(i == 0)
        def _():
            for r in (k1_acc, k2_acc, v1_acc, v2_acc, dkc_ref, dvc_ref, dsink_ref):
                r[...] = jnp.zeros_like(r)

        @pl.when(i < steps)
        def _():
            def band(qb):
                first = jnp.where(i == 0, 0, 1) if qb == 0 else 1
                return bias_ref[jnp.where(i == steps - 1, 2, first) if qb == nq - 1 else first]

            def lanes(hk):
                return slice(hk * HEAD_DIM, (hk + 1) * HEAD_DIM)

            def keys(ctx_ref, p_ref, m_ref, x_ref, qb, hk):
                sl = lanes(hk)
                blocks = [p_ref[:, sl]] + [m_ref[j * BLK:(j + 1) * BLK, sl] for j in range(nq)] + [x_ref[:, sl]]
                return jnp.concatenate([ctx_ref[:, sl]] + blocks[qb:qb + 3], axis=0)

            def stacked(ref, qb, hk, width):
                return jnp.concatenate(
                    [ref[qb * BLK:(qb + 1) * BLK, (hk * GQA_GROUP + g) * width:(hk * GQA_GROUP + g + 1) * width]
                     for g in range(GQA_GROUP)], axis=0)

            def scores(qb, hk):
                kcat = keys(kc_ref, kp_ref, km_ref, kx_ref, qb, hk)
                qg = stacked(q_ref, qb, hk, HEAD_DIM)
                s = _masked(lax.dot_general(qg, kcat, NT, preferred_element_type=F32), band(qb), C)
                dog = stacked(do_ref, qb, hk, HEAD_DIM)
                dp = lax.dot_general(dog, keys(vc_ref, vp_ref, vm_ref, vx_ref, qb, hk), NT, preferred_element_type=F32)
                return kcat, qg, dog, s, dp

            def softmax_bwd(qb, hk, s, dp):
                lse_c = stacked(lse_ref, qb, hk, 1)
                p = jnp.exp(s - lse_c)
                delta = jnp.sum(p * dp, axis=1, keepdims=True)
                ds = (p * (dp - delta)).astype(BF16)
                p_sink = jnp.exp(_sink_col(sink_ref, hk) - lse_c) * delta
                return p.astype(BF16), ds, p_sink

            def put_dq(qb, hk, dqs, p_sink):
                for g in range(GQA_GROUP):
                    h = hk * GQA_GROUP + g
                    dq_scr[qb * BLK:(qb + 1) * BLK, h * HEAD_DIM:(h + 1) * HEAD_DIM] = dqs[g * BLK:(g + 1) * BLK, :]
                    tot = jnp.sum(p_sink[g * BLK:(g + 1) * BLK, :], axis=0, keepdims=True)
                    dsink_ref[h:h + 1, :] -= jnp.broadcast_to(tot, (1, 128))

            ahead = 4
            sc = [scores(*c) for c in chains[:ahead]]
            pending = None
            for n, (qb, hk) in enumerate(chains):
                if n + ahead < len(chains):
                    sc.append(scores(*chains[n + ahead]))
                kcat, qg, dog, s, dp = sc[n]
                pb, ds, p_sink = softmax_bwd(qb, hk, s, dp)
                if pending is not None:
                    pqb, phk, pds, ppb, pqg, pdog = pending
                    ck_scr[pqb, :, lanes(phk)] = lax.dot_general(pds, pqg, TN, preferred_element_type=F32)
                    cv_scr[pqb, :, lanes(phk)] = lax.dot_general(ppb, pdog, TN, preferred_element_type=F32)
                put_dq(qb, hk, jnp.dot(ds, kcat, preferred_element_type=F32), p_sink)
                pending = (qb, hk, ds, pb, qg, dog)
            pqb, phk, pds, ppb, pqg, pdog = pending
            ck_scr[pqb, :, lanes(phk)] = lax.dot_general(pds, pqg, TN, preferred_element_type=F32)
            cq, sq = cq_ref[...], sq_ref[...]
            for j in range(4):
                dq_ref[:, j * 128:(j + 1) * 128] = _unrope(dq_scr[:, j * 128:(j + 1) * 128] * Q_SCALE, cq, sq).astype(BF16)
            cv_scr[pqb, :, lanes(phk)] = lax.dot_general(ppb, pdog, TN, preferred_element_type=F32)
            dkc_ref[...] += functools.reduce(lambda a, b: a + b, [ck_scr[qb, 0:C, :] for qb in range(nq)])
            dvc_ref[...] += functools.reduce(lambda a, b: a + b, [cv_scr[qb, 0:C, :] for qb in range(nq)])

        @pl.when(i >= steps)
        def _():
            ck_scr[...] = jnp.zeros_like(ck_scr)
            cv_scr[...] = jnp.zeros_like(cv_scr)

        def slot(scr, r, carried):
            parts = [scr[qb, C + (r - qb) * BLK:C + (r - qb + 1) * BLK, :] for qb in range(nq) if 0 <= r - qb <= 2]
            total = functools.reduce(lambda a, b: a + b, parts)
            return total if carried is None else carried[...] + total

        for r in range(nq):
            rows = slice(r * BLK, (r + 1) * BLK)
            carried_k, carried_v = ((k1_acc, v1_acc), (k2_acc, v2_acc), (None, None))[min(r, 2)]
            tables = (cl_ref[...], sl_ref[...]) if r == 0 else (cq_ref[(r - 1) * BLK:r * BLK, :], sq_ref[(r - 1) * BLK:r * BLK, :])
            dk_ref[rows, :] = _unrope(slot(ck_scr, r, carried_k), *tables).astype(BF16)
            dv_ref[rows, :] = slot(cv_scr, r, carried_v).astype(BF16)
        k1_acc[...] = slot(ck_scr, nq, None)
        v1_acc[...] = slot(cv_scr, nq, None)
        k2_acc[...] = slot(ck_scr, nq + 1, None)
        v2_acc[...] = slot(cv_scr, nq + 1, None)

    last = steps - 1
    kv3 = [pl.BlockSpec((BLK, KV_W), lambda i: (jnp.clip(nq * i - 1, 0, nb - 1), 0)),
           pl.BlockSpec((nq * BLK, KV_W), lambda i: (jnp.minimum(i, last), 0)),
           pl.BlockSpec((BLK, KV_W), lambda i: (jnp.minimum(nq * i + nq, nb - 1), 0))]
    cur = lambda w: pl.BlockSpec((nq * BLK, w), lambda i: (jnp.minimum(i, last), 0))
    late = lambda w: pl.BlockSpec((BLK, w), lambda i: (jnp.clip(nq * i - 1, 0, nb - 1), 0))
    out2 = lambda w: pl.BlockSpec((nq * BLK, w), lambda i: (i, 0))
    return _call(
        body, name="bwd_attn", grid=(steps + 1,),
        in_specs=[pl.BlockSpec(memory_space=pltpu.SMEM), cur(Q_W)] + kv3 + kv3
                 + [_full((C, KV_W)), _full((C, KV_W)), cur(Q_W), cur(N_Q_HEADS), cur(128), cur(128), late(128), late(128),
                    _full((3, GQA_GROUP * BLK, 3 * BLK))],
        out_specs=[cur(Q_W), out2(KV_W), out2(KV_W), _full((C, KV_W)), _full((C, KV_W)), _full((8, 128))],
        out_shape=[_sds((L, Q_W), BF16), _sds((L + nq * BLK, KV_W), BF16), _sds((L + nq * BLK, KV_W), BF16),
                   _sds((C, KV_W), F32), _sds((C, KV_W), F32), _sds((8, 128), F32)],
        scratch=[pltpu.VMEM((nq * BLK, Q_W), F32), pltpu.VMEM((nq, NK, KV_W), F32), pltpu.VMEM((nq, NK, KV_W), F32)]
                + [pltpu.VMEM((BLK, KV_W), F32)] * 4,
        args=(sink, q, k, k, k, v, v, v, kc, vc, dya, lse, cos, sin, cos, sin, bias), comm=comm)


def _k_ctx_bwd(ctx, modc, hc, dkc, dvc, w_kv):
    C = ctx.shape[0]

    def body(c_ref, mod_ref, hc_ref, dkc_ref, dvc_ref, w_ref, gw_ref, dmod_ref):
        dkv = jnp.concatenate([dkc_ref[...], dvc_ref[...]], axis=1).astype(BF16)
        gw_ref[...] = lax.dot_general(dkv, hc_ref[...], TN, preferred_element_type=F32)
        dhc = jnp.dot(dkv, w_ref[...], preferred_element_type=F32)
        n, _ = _ln(c_ref[...])
        dmod_ref[...] = jnp.zeros_like(dmod_ref)
        dmod_ref[0:1, :] = _colsum(dhc)
        dmod_ref[1:2, :] = _colsum(dhc * n)

    return pl.pallas_call(
        body, name="bwd_ctx", grid=(1,),
        in_specs=[_full((C, D)), _full((8, D)), _full((C, D)), _full((C, KV_W)), _full((C, KV_W)), _full((2 * KV_W, D))],
        out_specs=[_full((2 * KV_W, D)), _full((8, D))],
        out_shape=[_sds((2 * KV_W, D), F32), _sds((8, D), F32)],
        compiler_params=_params(("arbitrary",)),
    )(ctx, modc, hc, dkc, dvc, w_kv)


def _k_in_bwd(dq, dk, dv, du, dvb, dga, dgb, x, dxp, w_in, modv, tm, comm=None):
    L = x.shape[0]
    parts = [(O_Q, Q_W), (O_K, KV_W), (O_V, KV_W), (O_U, GM_W), (O_VB, GM_W), (O_GA, D), (O_GB, D)]

    def body(dq_ref, dk_ref, dv_ref, du_ref, dvb_ref, dga_ref, dgb_ref, x_ref, dxp_ref, w_ref, mod_ref,
             dP_ref, gx_ref, acc_ref):
        @pl.when(pl.program_id(0) == 0)
        def _():
            acc_ref[...] = jnp.zeros_like(acc_ref)

        for (lo, width), r in zip(parts, (dq_ref, dk_ref, dv_ref, du_ref, dvb_ref, dga_ref, dgb_ref)):
            dP_ref[:, lo:lo + width] = r[...]
        n1, rstd1 = _ln(x_ref[...])
        dh = jnp.dot(dP_ref[...], w_ref[...], preferred_element_type=F32)
        acc_ref[0:1, :] += _colsum(dh)
        acc_ref[1:2, :] += _colsum(dh * n1)
        gx_ref[...] = dxp_ref[...] + _ln_bwd(dh * (1.0 + mod_ref[1:2, :]), n1, rstd1)

    return _call(
        body, name="bwd_in", grid=(L // tm,),
        in_specs=[_row(tm, w) for _, w in parts] + [_row(tm, D), _row(tm, D), _resident((IN_W, D)), _full((8, D))],
        out_specs=[_row(tm, IN_W), _row(tm, D), _full((8, D))],
        out_shape=[_sds((L, IN_W), BF16), _sds((L, D), F32), _sds((8, D), F32)],
        args=(dq, dk, dv, du, dvb, dga, dgb, x, dxp, w_in, modv), comm=comm)


def _wgrad(a, b, name, tk, tt, comm=None, extra=None):
    T, K = a.shape
    N = b.shape[1]
    nt = T // tt

    def body(*refs):
        a_ref, b_ref = refs[:2]
        o_ref, acc_ref = refs[-2:]
        j, t = pl.program_id(0), pl.program_id(1)

        @pl.when(t == 0)
        def _():
            acc_ref[...] = jnp.zeros_like(acc_ref)

        acc_ref[...] += lax.dot_general(a_ref[...], b_ref[...], TN, preferred_element_type=F32)

        if extra is not None:
            lo, rows = extra[0] % tk, extra[1].shape[0]

            @pl.when((t == nt - 1) & (j == extra[0] // tk))
            def _():
                acc_ref[lo:lo + rows, :] += refs[2][...]

        @pl.when(t == nt - 1)
        def _():
            o_ref[...] = acc_ref[...].astype(BF16)

    extra_specs = [] if extra is None else [pl.BlockSpec(extra[1].shape, lambda j, t: (0, 0))]
    (out,), got = _call(
        body, name=name, grid=(K // tk, nt),
        in_specs=[pl.BlockSpec((tt, tk), lambda j, t: (t, j)), pl.BlockSpec((tt, N), lambda j, t: (t, 0))] + extra_specs,
        out_specs=[pl.BlockSpec((tk, N), lambda j, t: (j, 0))],
        out_shape=[_sds((K, N), BF16)],
        scratch=[pltpu.VMEM((tk, N), F32)],
        args=(a, b) + (() if extra is None else (extra[1],)), comm=comm)
    return (out, got) if comm is not None else out


def _adamw_reduce(parts, w, m, v, name, tr):
    R, C = w.shape
    n_parts = parts.shape[0]

    def body(p_ref, w_ref, m_ref, v_ref, g_ref, d_ref, m2_ref, v2_ref):
        g = p_ref[0].astype(F32)
        for i in range(1, n_parts):
            g = g + p_ref[i].astype(F32)
        delta, m2, v2 = _adamw(w_ref[...], g, m_ref[...], v_ref[...])
        g_ref[...] = g
        d_ref[...] = delta
        m2_ref[...] = m2
        v2_ref[...] = v2

    spec = _row(tr, C)
    return pl.pallas_call(
        body, name=name, grid=(R // tr,),
        in_specs=[pl.BlockSpec((n_parts, tr, C), lambda i: (0, i, 0)), spec, spec, spec],
        out_specs=[spec] * 4,
        out_shape=[_sds((R, C), F32)] * 4,
        compiler_params=_params(("arbitrary",)),
    )(parts, w, m, v)


SMALL_ORDER = ("b_ada", "ln1_g", "ln1_b", "ln2_g", "ln2_b", "gmlp_ln_g", "gmlp_ln_b", "b_spatial", "attn_sink")


def _small_step(gath, params):
    flat = [a for name in SMALL_ORDER for a in params[name]]

    def grad_of(tot, name):
        if name == "b_ada":
            return jnp.concatenate([tot[r:r + 1, :] for r in range(6)], axis=1)
        if name in ("ln1_g", "ln1_b", "ln2_g", "ln2_b"):
            r = 8 + ("ln1_g", "ln1_b", "ln2_g", "ln2_b").index(name)
            return tot[r:r + 1, :]
        if name == "gmlp_ln_g":
            return tot[12:13, :GM_W]
        if name == "gmlp_ln_b":
            return tot[12:13, GM_W:]
        if name == "b_spatial":
            return jnp.concatenate([tot[13:14, g * BLK:(g + 1) * BLK] for g in range(N_GROUPS)], axis=0)[None]
        return tot[14:15, :N_Q_HEADS]

    def body(*refs):
        g_ref, in_refs = refs[0], refs[1:1 + len(flat)]
        tot_ref, out_refs = refs[1 + len(flat)], refs[2 + len(flat):]
        tot = g_ref[0]
        for i in range(1, N_DEV):
            tot = tot + g_ref[i]
        tot_ref[...] = tot
        tot_ref[0:2, :] = tot[0:2, :] + tot[6:8, :]
        tot_ref[15:16, :] = jnp.broadcast_to(jnp.sum(tot[15:16, :], axis=1, keepdims=True), (1, D))
        tot = tot_ref[...]
        for k, name in enumerate(SMALL_ORDER):
            w_ref, m_ref, v_ref = in_refs[3 * k:3 * k + 3]
            g = grad_of(tot, name)
            delta, m2, v2 = _adamw(w_ref[...], g, m_ref[...], v_ref[...])
            for r, val in zip(out_refs[4 * k:4 * k + 4], (g, delta, m2, v2)):
                r[...] = val

    res = pl.pallas_call(
        body, name="small_step", grid=(1,),
        in_specs=[_full((N_DEV, 16, D))] + [_full(a.shape) for a in flat],
        out_specs=[_full((16, D))] + [_full(params[name][0].shape) for name in SMALL_ORDER for _ in range(4)],
        out_shape=[_sds((16, D), F32)] + [_sds(params[name][0].shape, F32) for name in SMALL_ORDER for _ in range(4)],
        compiler_params=_params(("arbitrary",)),
    )(gath, *flat)
    return res[0], {name: res[1 + 4 * k:5 + 4 * k] for k, name in enumerate(SMALL_ORDER)}


def _cctx_finish(gath, c_ctx, m, v):
    def body(g_ref, c_ref, m_ref, v_ref, gr_ref, d_ref, m2_ref, v2_ref):
        ds = g_ref[0]
        for i in range(1, N_DEV):
            ds = ds + g_ref[i]
        c = c_ref[...]
        sg = _sigmoid(c)
        g = ds * (sg * (1.0 + c * (1.0 - sg)))
        delta, m2, v2 = _adamw(c, g, m_ref[...], v_ref[...])
        gr_ref[...] = g
        d_ref[...] = delta
        m2_ref[...] = m2
        v2_ref[...] = v2

    return pl.pallas_call(
        body, name="cctx_finish", grid=(1,),
        in_specs=[_full((N_DEV, 8, D))] + [_full((8, D))] * 3, out_specs=[_full((8, D))] * 4,
        out_shape=[_sds((8, D), F32)] * 4,
        compiler_params=_params(("arbitrary",)),
    )(gath, c_ctx, m, v)


def _pad_rows(a, rows):
    return jnp.concatenate([a, jnp.zeros((rows - a.shape[0], a.shape[1]), a.dtype)], axis=0)


def kernel(x, c, ctx, c_ctx, w_ada, b_ada, w_in, attn_sink, gmlp_ln_g, gmlp_ln_b, w_spatial, b_spatial, w_branch_a, w_branch_b, w_out, ln1_g, ln1_b, w_ffn_in, w_ffn_out, ln2_g, ln2_b, loss_target, m_c_ctx, m_w_ada, m_b_ada, m_w_in, m_attn_sink, m_gmlp_ln_g, m_gmlp_ln_b, m_w_spatial, m_b_spatial, m_w_branch_a, m_w_branch_b, m_w_out, m_ln1_g, m_ln1_b, m_w_ffn_in, m_w_ffn_out, m_ln2_g, m_ln2_b, v_c_ctx, v_w_ada, v_b_ada, v_w_in, v_attn_sink, v_gmlp_ln_g, v_gmlp_ln_b, v_w_spatial, v_b_spatial, v_w_branch_a, v_w_branch_b, v_w_out, v_ln1_g, v_ln1_b, v_w_ffn_in, v_w_ffn_out, v_ln2_g, v_ln2_b):
    L = x.shape[1]
    me = 4 * lax.axis_index("x") + 2 * lax.axis_index("y") + lax.axis_index("c")
    x2, tgt, ctx2 = x[0], loss_target[0], ctx[0]
    tiles = _Tiles(L)
    tm_in, tm, tt = tiles.wide, tiles.narrow, tiles.tokens

    transposed = ("w_in", "w_ffn_in")
    tr = lambda kname, a: a.T if kname in transposed else a
    big = dict(w_in=w_in[0].T, w_branch_a=w_branch_a[0], w_branch_b=w_branch_b[0], w_out=w_out[0],
               w_ffn_in=w_ffn_in[0].T, w_ffn_out=w_ffn_out[0])
    col_sharded = ("w_branch_a", "w_branch_b")
    shard_bf = {k: a.astype(BF16) for k, a in big.items()}

    def assemble(kname, g):
        if kname in col_sharded:
            return g.transpose(1, 0, 2).reshape(g.shape[1], N_DEV * g.shape[2])
        return g.reshape(N_DEV * g.shape[1], g.shape[2])

    def to_blocks(kname, g):
        if kname in col_sharded:
            return g.reshape(g.shape[0], N_DEV, g.shape[1] // N_DEV).transpose(1, 0, 2)
        return g.reshape(N_DEV, g.shape[0] // N_DEV, g.shape[1])

    full = {}
    n_ada = w_ada.shape[2]
    b_my = lax.dynamic_slice(b_ada, (0, me * n_ada), (1, n_ada))
    act, mod_all, got = _prologue(_pad_rows(c, 8), _pad_rows(c_ctx[None, :], 8), w_ada[0], b_my,
                                  _Comm(gather=[shard_bf["w_in"]]))
    full["w_in"] = assemble("w_in", got[0])
    mod_all = mod_all.transpose(1, 0, 2).reshape(16, 6 * D)
    modv = _pad_rows(lax.dynamic_slice(mod_all, (me, 0), (1, 6 * D)).reshape(6, D), 8)
    modc = _pad_rows(mod_all[8].reshape(6, D), 8)

    lnv = _pad_rows(jnp.concatenate([ln1_g, ln1_b, ln2_g, ln2_b], axis=0), 8)
    gm_lnv = _pad_rows(jnp.concatenate([gmlp_ln_g, gmlp_ln_b], axis=0), 8)
    ws_b = w_spatial[0].astype(BF16)
    wst_b = ws_b.transpose(0, 2, 1)
    bsp = jnp.repeat(b_spatial[0].T, GROUP_DIM, axis=1)
    sink = attn_sink[0]
    cos, sin = _rope_tables(L)
    bias = _attn_bias()
    w_kv = full["w_in"][O_K:O_K + 2 * KV_W, :]

    (h, q, k, v, u, vb, ga, gb), got = _k_in(
        x2, modv, full["w_in"], cos, sin, tm_in,
        comm=_Comm(gather=[shard_bf[kname] for kname in ("w_branch_a", "w_branch_b", "w_out", "w_ffn_out")]))
    for kname, g in zip(("w_branch_a", "w_branch_b", "w_out", "w_ffn_out"), got):
        full[kname] = assemble(kname, g)
    hc, kc, vc = _k_ctx(ctx2, modc, w_kv)
    (ya, lse), got = _k_attn(sink, q, k, v, kc, vc, bias, comm=_Comm(gather=[shard_bf["w_ffn_in"]]))
    full["w_ffn_in"] = assemble("w_ffn_in", got[0])
    yb = _k_gmlp(u, vb, gm_lnv, ws_b, bsp)
    merged, mix, xm, h2 = _k_merge(x2, ya, yb, ga, gb, full["w_branch_a"], full["w_branch_b"], full["w_out"], modv, lnv, tm_in)
    gate, up, act_f, dr2, df, acc_f = _k_ffn(h2, xm, tgt, full["w_ffn_in"], full["w_ffn_out"], modv, lnv, tm_in)

    dF, dmix, dxp, acc_b = _k_ffn_bwd(df, gate, up, xm, dr2, x2, mix, full["w_ffn_in"], full["w_ffn_out"], modv, lnv, tm)
    blk_fo = to_blocks("w_ffn_out", _wgrad(act_f, df, "wgrad_ffn_out", tiles.tk_ffn, tt))
    gw_fi, (rcv_fo,) = _wgrad(dF, h2, "wgrad_ffn_in", tiles.tk_ffn, tt, comm=_Comm(scatter=[blk_fo]))
    blk_fi = to_blocks("w_ffn_in", gw_fi)
    (dA, dB, dga, dgb, dya, dyb), _ = _k_merge_bwd(
        dmix, ya, yb, ga, gb, full["w_branch_a"], full["w_branch_b"], full["w_out"], tm_in)
    du, dvb, g_ws, g_bst, g_gln = _k_gmlp_bwd(u, vb, dyb, gm_lnv, ws_b, wst_b, bsp)
    (dq, dk_late, dv_late, dkc, dvc, g_sink), (gath_ws, rcv_fi) = _k_attn_bwd(
        sink, q, k, v, kc, vc, dya, lse, cos, sin, bias,
        comm=_Comm(gather=[g_ws.reshape(N_GROUPS * BLK, BLK)], scatter=[blk_fi]))
    dk, dv = dk_late[BLK:BLK + L], dv_late[BLK:BLK + L]
    blk_a = to_blocks("w_branch_a", _wgrad(ya, dA, "wgrad_a", Q_W, tt))
    blk_b = to_blocks("w_branch_b", _wgrad(yb, dB, "wgrad_b", GM_W, tt))
    blk_o = to_blocks("w_out", _wgrad(merged, dmix, "wgrad_out", D, tt))
    (dP, grad_x, acc_i), _ = _k_in_bwd(dq, dk, dv, du, dvb, dga, dgb, x2, dxp, full["w_in"], modv, tm_in)
    g_ctx, dmodc = _k_ctx_bwd(ctx2, modc, hc, dkc, dvc, w_kv)
    gw_in, (rcv_a, rcv_b, rcv_o) = _wgrad(dP, h, "wgrad_in", tiles.tk_in, tt, comm=_Comm(scatter=[blk_a, blk_b, blk_o]),
                                          extra=(O_K, g_ctx))

    dmod_x = jnp.concatenate([acc_i[0:2], acc_b[4:5], acc_b[0:2], acc_f[2:3]], axis=0)
    small = jnp.concatenate([
        dmod_x, dmodc[0:2], acc_b[2:4], acc_f[0:2],
        jnp.concatenate([g_gln[0:1], g_gln[1:2]], axis=1), g_bst.T.reshape(1, D),
        _pad_rows(g_sink[:, 0:1], D).T, acc_f[3:4]], axis=0)
    rcv_in, gath = _exchange_two_level(to_blocks("w_in", gw_in), small, "exchange_last")
    received = dict(w_in=rcv_in, w_branch_a=rcv_a, w_branch_b=rcv_b, w_out=rcv_o, w_ffn_in=rcv_fi, w_ffn_out=rcv_fo)
    moments = dict(w_in=(m_w_in, v_w_in), w_branch_a=(m_w_branch_a, v_w_branch_a), w_branch_b=(m_w_branch_b, v_w_branch_b),
                   w_out=(m_w_out, v_w_out), w_ffn_in=(m_w_ffn_in, v_w_ffn_in), w_ffn_out=(m_w_ffn_out, v_w_ffn_out))
    names = list(big)
    res = {}
    for kname in names:
        mm, vv = moments[kname]
        R = big[kname].shape[0]
        res[kname] = [tr(kname, r) for r in _adamw_reduce(
            received[kname], big[kname], tr(kname, mm[0]), tr(kname, vv[0]), "adamw_" + kname, 256 if R % 256 == 0 else R // 2)]

    ws2d = lambda a: a.reshape(N_GROUPS * BLK, BLK)
    res_ws = [r.reshape(w_spatial.shape) for r in _adamw_reduce(
        gath_ws, ws2d(w_spatial), ws2d(m_w_spatial), ws2d(v_w_spatial), "adamw_w_spatial", 256)]
    tot, res_small = _small_step(gath, dict(
        b_ada=(b_ada, m_b_ada, v_b_ada), ln1_g=(ln1_g, m_ln1_g, v_ln1_g), ln1_b=(ln1_b, m_ln1_b, v_ln1_b),
        ln2_g=(ln2_g, m_ln2_g, v_ln2_g), ln2_b=(ln2_b, m_ln2_b, v_ln2_b),
        gmlp_ln_g=(gmlp_ln_g, m_gmlp_ln_g, v_gmlp_ln_g), gmlp_ln_b=(gmlp_ln_b, m_gmlp_ln_b, v_gmlp_ln_b),
        b_spatial=(b_spatial, m_b_spatial, v_b_spatial), attn_sink=(attn_sink, m_attn_sink, v_attn_sink)))
    loss = tot[15, 0]

    dmod_rows = jnp.concatenate([gath[:, 0:6, :].reshape(N_DEV, 6 * D),
                                 jnp.concatenate([tot[6:8].reshape(1, 2 * D), jnp.zeros((1, 4 * D), F32)], axis=1),
                                 jnp.zeros((7, 6 * D), F32)], axis=0)
    dmod_my = lax.dynamic_slice(dmod_rows, (0, me * n_ada), (16, n_ada))
    g_wada, d_wada, m2_wada, v2_wada, pc = _ada_bwd(act, dmod_my, w_ada[0], m_w_ada[0], v_w_ada[0])
    pc_all = _gather_rows(pc, "gather_cctx")
    cc8 = lambda a: _pad_rows(a.reshape(1, D), 8)
    g_cc, d_cc, m2_cc, v2_cc = _cctx_finish(pc_all, cc8(c_ctx), cc8(m_c_ctx), cc8(v_c_ctx))

    order = ["c_ctx", "w_ada", "b_ada", "w_in", "attn_sink", "gmlp_ln_g", "gmlp_ln_b", "w_spatial", "b_spatial",
             "w_branch_a", "w_branch_b", "w_out", "ln1_g", "ln1_b", "w_ffn_in", "w_ffn_out", "ln2_g", "ln2_b"]
    grads, deltas, new_m, new_v = {}, {}, {}, {}
    grads["c_ctx"], deltas["c_ctx"], new_m["c_ctx"], new_v["c_ctx"] = g_cc[0], d_cc[0], m2_cc[0], v2_cc[0]
    grads["w_ada"], deltas["w_ada"], new_m["w_ada"], new_v["w_ada"] = g_wada[None], d_wada[None], m2_wada[None], v2_wada[None]
    for kname in names:
        g, d, m2, v2 = res[kname]
        grads[kname], deltas[kname], new_m[kname], new_v[kname] = g[None], d[None], m2[None], v2[None]
    grads["w_spatial"], deltas["w_spatial"], new_m["w_spatial"], new_v["w_spatial"] = res_ws
    for kname in SMALL_ORDER:
        grads[kname], deltas[kname], new_m[kname], new_v[kname] = res_small[kname]
    return (loss, grad_x[None], *[grads[n] for n in order], *[deltas[n] for n in order],
            *[new_m[n] for n in order], *[new_v[n] for n in order])
```

```python
import functools
import math

import jax
import jax.numpy as jnp
import numpy as np
from jax import lax
from jax.experimental import pallas as pl
from jax.experimental.pallas import tpu as pltpu

F32 = jnp.float32
BF16 = jnp.bfloat16
MESH = pl.DeviceIdType.MESH

N_DEV = 8
D = 1024
HEAD_DIM = 64
N_Q_HEADS = 8
N_KV_HEADS = 2
GQA_GROUP = 4
BLK = 128
Q_W = 512
KV_W = 128
GM_W = 512
N_GROUPS = 8
GROUP_DIM = 64
FFN_H = 2816
IN_W = 3840
O_Q, O_K, O_V, O_U, O_VB, O_GA, O_GB = 0, 512, 640, 768, 1280, 1792, 2816
LN_EPS = 1e-5
NEG_INF = -1e30
ALPHA = 2.0 ** 0.25
ROPE_BASE = 10000.0
ROPE_PAIRS = 16
Q_SCALE = HEAD_DIM ** -0.5
GELU_K0 = math.sqrt(2.0 / math.pi)
GELU_K1 = 0.044715

ADAM_LR = 0.001
ADAM_B1 = 0.9
ADAM_B2 = 0.999
ADAM_EPS = 1e-08
ADAM_WD = 0.01
ADAM_STEP = 10

V7X_VMEM_BYTES = 64 * 1024 * 1024
VMEM_LIMIT = V7X_VMEM_BYTES * 7 // 8
NT = (((1,), (1,)), ((), ()))
TN = (((0,), (0,)), ((), ()))


class _Tiles:
    def __init__(self, L):
        self.wide = min(512, L)
        self.narrow = min(256, L)
        self.tokens = min(2048, L)
        self.tk_in = IN_W // 3
        self.tk_ffn = FFN_H // 2


def _params(sem=None):
    return pltpu.CompilerParams(dimension_semantics=sem, vmem_limit_bytes=VMEM_LIMIT)


def _row(tm, w):
    return pl.BlockSpec((tm, w), lambda i: (i, 0))


def _full(shape):
    nd = len(shape)
    return pl.BlockSpec(shape, lambda i: (0,) * nd)


def _resident(shape):
    nd = len(shape)
    return pl.BlockSpec(shape, lambda i: (0,) * nd, pipeline_mode=pl.Buffered(1))


def _sds(shape, dt):
    return jax.ShapeDtypeStruct(shape, dt)


def _ln(xf):
    mu = jnp.mean(xf, axis=-1, keepdims=True)
    xc = xf - mu
    var = jnp.mean(xc * xc, axis=-1, keepdims=True)
    rstd = lax.rsqrt(var + LN_EPS)
    return xc * rstd, rstd


def _ln_bwd(dn, n, rstd):
    m1 = jnp.mean(dn, axis=-1, keepdims=True)
    m2 = jnp.mean(dn * n, axis=-1, keepdims=True)
    return rstd * (dn - m1 - n * m2)


def _colsum(t):
    return jnp.sum(t, axis=0, keepdims=True)


def _sigmoid(x):
    return 0.5 * jnp.tanh(0.5 * x) + 0.5


def _gelu(x):
    t = jnp.tanh(GELU_K0 * (x + GELU_K1 * (x * x * x)))
    return x * (0.5 * (1.0 + t)), t


def _gelu_grad(x, t):
    return 0.5 * (1.0 + t) + 0.5 * x * (1.0 - t * t) * (GELU_K0 * (1.0 + 3.0 * GELU_K1 * x * x))


def _swap16(t):
    lane = lax.broadcasted_iota(jnp.int32, t.shape, 1)
    return jnp.where((lane & 16) == 0, pltpu.roll(t, 112, 1), pltpu.roll(t, 16, 1))


def _rope(t, cos, sin):
    return t * cos + _swap16(t) * sin


def _unrope(t, cos, sin):
    return t * cos - _swap16(t) * sin


def _adamw(w, g, m, v):
    m2 = ADAM_B1 * m + (1.0 - ADAM_B1) * g
    v2 = ADAM_B2 * v + (1.0 - ADAM_B2) * (g * g)
    m_hat = m2 / (1.0 - ADAM_B1 ** ADAM_STEP)
    v_hat = v2 / (1.0 - ADAM_B2 ** ADAM_STEP)
    delta = -ADAM_LR * (m_hat / (jnp.sqrt(v_hat) + ADAM_EPS) + ADAM_WD * w)
    return delta, m2, v2


def _rope_tables(L):
    inv = (np.float32(ROPE_BASE) ** (-np.arange(ROPE_PAIRS, dtype=np.float32) / np.float32(ROPE_PAIRS))).astype(np.float32)
    t = np.arange(L, dtype=np.int32)
    rows = (t // 64).astype(np.float32)[:, None] * inv
    cols = (t % 64).astype(np.float32)[:, None] * inv
    cr, sr, cc, sc = np.cos(rows), np.sin(rows), np.cos(cols), np.sin(cols)
    cos = np.concatenate([cr, cr, cc, cc], axis=1)
    sin = np.concatenate([-sr, sr, -sc, sc], axis=1)
    return jnp.asarray(np.tile(cos, (1, 2)), F32), jnp.asarray(np.tile(sin, (1, 2)), F32)


def _me():
    return lax.axis_index("x"), lax.axis_index("y"), lax.axis_index("c")


def _peer(mx, my, mc, k):
    return (mx ^ ((k >> 2) & 1), my ^ ((k >> 1) & 1), mc ^ (k & 1))


class _Comm:
    def __init__(self, gather=(), scatter=(), spread=()):
        self.kinds = ["gather"] * len(gather) + ["scatter"] * len(scatter) + ["spread"] * len(spread)
        self.args = list(gather) + list(scatter) + list(spread)
        self.n = len(self.args)

    def out_shape(self):
        return [_sds(a.shape if k == "scatter" else (N_DEV,) + a.shape, a.dtype) for k, a in zip(self.kinds, self.args)]

    def specs(self):
        return [pl.BlockSpec(memory_space=pl.ANY)] * self.n

    def scratch(self):
        return [pltpu.SemaphoreType.DMA((7 * self.n,)), pltpu.SemaphoreType.DMA((7 * self.n,)),
                pltpu.SemaphoreType.DMA((self.n,))]

    def _plan(self, x_refs, out_refs, send_sems, recv_sems, local_sems):
        mx, my, mc = _me()
        me = 4 * mx + 2 * my + mc
        here, sibling = (mx, my, mc), (mx, my, 1 - mc)
        chips = [(1 - mx, my), (mx, 1 - my), (1 - mx, 1 - my)]
        local, first, last = [], [], []
        relay = [[], [], []]
        for a, kind in enumerate(self.kinds):
            x, out = x_refs[a], out_refs[a]

            def rc(k, src, dst, to):
                return pltpu.make_async_remote_copy(
                    src_ref=src, dst_ref=dst, send_sem=send_sems.at[7 * a + k], recv_sem=recv_sems.at[7 * a + k],
                    device_id=to, device_id_type=MESH)

            if kind == "gather":
                local.append(pltpu.make_async_copy(x, out.at[me], local_sems.at[a]))
                first.append(rc(0, x, out.at[me], sibling))
                last.append(rc(0, x, out.at[me ^ 1], here))
                for j, (cx, cy) in enumerate(chips):
                    first.append(rc(1 + j, x, out.at[me], (cx, cy, mc)))
                    landed = out.at[4 * cx + 2 * cy + mc]
                    relay[j].append((rc(1 + j, x, landed, here), rc(4 + j, landed, landed, sibling)))
                    last.append(rc(4 + j, x, out.at[4 * cx + 2 * cy + 1 - mc], here))
            else:
                own = x.at[me] if kind == "scatter" else x
                local.append(pltpu.make_async_copy(own, out.at[me], local_sems.at[a]))
                for k in range(1, N_DEV):
                    src = x.at[me ^ k] if kind == "scatter" else x
                    first.append(rc(k - 1, src, out.at[me], _peer(mx, my, mc, k)))
                    last.append(rc(k - 1, own, out.at[me ^ k], here))
        return local, first, relay[0] + relay[1] + relay[2], last

    def start(self, *refs):
        local, first, _, _ = self._plan(*refs)
        for cp in local + first:
            cp.start()

    def finish(self, *refs):
        local, first, relay, last = self._plan(*refs)
        for arrival, onward in relay:
            arrival.wait_recv()
            onward.start()
        for cp in last:
            cp.wait_recv()
        for cp in first:
            cp.wait_send()
        for _, onward in relay:
            onward.wait_send()
        for cp in local:
            cp.wait()


def _call(body, *, name, grid, in_specs, out_specs, out_shape, args, scratch=(), comm=None, aliases=None):
    params = _params(("arbitrary",) * len(grid))

    def at(end):
        conds = [pl.program_id(d) == (n - 1 if end else 0) for d, n in enumerate(grid)]
        return functools.reduce(lambda p, q: p & q, conds)

    if comm is None:
        res = pl.pallas_call(
            body, name=name, grid=grid, in_specs=list(in_specs), out_specs=list(out_specs), out_shape=list(out_shape),
            scratch_shapes=list(scratch), input_output_aliases=aliases or {}, compiler_params=params)(*args)
        return list(res), []
    n_in, n_out, n_scr, cn = len(in_specs), len(out_specs), len(scratch), comm.n

    def hosted(*refs):
        ins, refs = refs[:n_in], refs[n_in:]
        cins, refs = refs[:cn], refs[cn:]
        outs, refs = refs[:n_out], refs[n_out:]
        couts, refs = refs[:cn], refs[cn:]
        scr, sems = refs[:n_scr], refs[n_scr:]

        @pl.when(at(False))
        def _():
            comm.start(cins, couts, *sems)

        body(*ins, *outs, *scr)

        @pl.when(at(True))
        def _():
            comm.finish(cins, couts, *sems)

    res = pl.pallas_call(
        hosted, name=name, grid=grid, in_specs=list(in_specs) + comm.specs(), out_specs=list(out_specs) + comm.specs(),
        out_shape=list(out_shape) + comm.out_shape(), scratch_shapes=list(scratch) + comm.scratch(),
        input_output_aliases=aliases or {}, compiler_params=params)(*args, *comm.args)
    return list(res[:n_out]), list(res[n_out:])


def _exchange_two_level(blk, small, name):
    _, R, C = blk.shape
    rows = small.shape[0]

    def body(blk_ref, small_ref, stage_ref, out_ref, gath_ref, a_scr, b_scr, t_scr, s1, r1, s3, r3, ss, rs, lsem):
        mx, my, mc = _me()
        me = 4 * mx + 2 * my + mc
        mine = 2 * mx + my
        here, sibling = (mx, my, mc), (mx, my, 1 - mc)

        def rc(src, dst, send, recv, to):
            return pltpu.make_async_remote_copy(src_ref=src, dst_ref=dst, send_sem=send, recv_sem=recv,
                                                device_id=to, device_id_type=MESH)

        own_small = pltpu.make_async_copy(small_ref, gath_ref.at[me], lsem.at[0])
        own_small.start()
        spread = [rc(small_ref, gath_ref.at[me], ss.at[k - 1], rs.at[k - 1], _peer(mx, my, mc, k)) for k in range(1, N_DEV)]
        to_sib = [rc(blk_ref.at[2 * p + 1 - mc], stage_ref.at[p], s1.at[p], r1.at[p], sibling) for p in range(4)]
        for cp in spread + to_sib:
            cp.start()
        own = [pltpu.make_async_copy(blk_ref.at[2 * p + mc], a_scr.at[p], lsem.at[1 + p]) for p in range(4)]
        for cp in own:
            cp.start()
        from_sib = []
        for p in range(4):
            rc(blk_ref.at[2 * p + 1 - mc], stage_ref.at[p], s1.at[p], r1.at[p], here).wait_recv()
            cp = pltpu.make_async_copy(stage_ref.at[p], b_scr.at[p], lsem.at[5 + p])
            cp.start()
            from_sib.append(cp)
        for cp in own + from_sib:
            cp.wait()
        t_scr[...] = (a_scr[...].astype(F32) + b_scr[...].astype(F32)).astype(BF16)
        keep = pltpu.make_async_copy(t_scr.at[mine], out_ref.at[mine], lsem.at[9])
        keep.start()
        onward = [rc(t_scr.at[mine ^ k], out_ref.at[mine], s3.at[k - 1], r3.at[k - 1], (mx ^ (k >> 1), my ^ (k & 1), mc))
                  for k in range(1, 4)]
        for cp in onward:
            cp.start()
        for k in range(1, 4):
            rc(t_scr.at[mine], out_ref.at[mine ^ k], s3.at[k - 1], r3.at[k - 1], here).wait_recv()
        for k in range(1, N_DEV):
            rc(small_ref, gath_ref.at[me ^ k], ss.at[k - 1], rs.at[k - 1], here).wait_recv()
        for cp in spread + to_sib + onward:
            cp.wait_send()
        keep.wait()
        own_small.wait()

    any_spec = pl.BlockSpec(memory_space=pl.ANY)
    dma = pltpu.SemaphoreType.DMA
    _, out, gath = pl.pallas_call(
        body, name=name,
        in_specs=[any_spec, any_spec], out_specs=[any_spec] * 3,
        out_shape=[_sds((4, R, C), BF16), _sds((4, R, C), BF16), _sds((N_DEV, rows, D), F32)],
        scratch_shapes=[pltpu.VMEM((4, R, C), BF16)] * 3
                       + [dma((4,)), dma((4,)), dma((3,)), dma((3,)), dma((N_DEV - 1,)), dma((N_DEV - 1,)), dma((10,))],
        compiler_params=pltpu.CompilerParams(vmem_limit_bytes=VMEM_LIMIT),
    )(blk, small)
    return out, gath


def _exchange_rows(x_ref, out_ref, send_sems, recv_sems):
    mx, my, mc = _me()
    me = 4 * mx + 2 * my + mc
    out_ref[pl.ds(me, 1)] = x_ref[...][None]
    sends = []
    for k in range(1, N_DEV):
        cp = pltpu.make_async_remote_copy(
            src_ref=x_ref, dst_ref=out_ref.at[me], send_sem=send_sems.at[k - 1], recv_sem=recv_sems.at[k - 1],
            device_id=_peer(mx, my, mc, k), device_id_type=MESH)
        cp.start()
        sends.append(cp)
    for k in range(1, N_DEV):
        pltpu.make_async_remote_copy(
            src_ref=x_ref, dst_ref=out_ref.at[me ^ k], send_sem=send_sems.at[k - 1], recv_sem=recv_sems.at[k - 1],
            device_id=(mx, my, mc), device_id_type=MESH).wait_recv()
    for cp in sends:
        cp.wait_send()


def _prologue(c8, cctx8, w_ada, b_my, comm):
    nw = w_ada.shape[1]

    def body(c_ref, cctx_ref, w_ref, b_ref, act_ref, mod_ref, cmine_scr, call_scr, mine_scr, mall_scr, s1, r1, s2, r2):
        cmine_scr[...] = c_ref[...]
        _exchange_rows(cmine_scr, call_scr, s1, r1)
        rows = [call_scr[d][0:1, :] for d in range(N_DEV)] + [cctx_ref[0:1, :], jnp.zeros((7, D), F32)]
        s = jnp.concatenate(rows, axis=0)
        act = s * _sigmoid(s)
        act_ref[...] = act
        mine_scr[...] = jnp.dot(act.astype(BF16), w_ref[...].astype(BF16), preferred_element_type=F32) + b_ref[...]
        _exchange_rows(mine_scr, mall_scr, s2, r2)
        mod_ref[...] = mall_scr[...]

    sems = [pltpu.SemaphoreType.DMA((N_DEV - 1,))] * 4
    (act, mod), got = _call(
        body, name="prologue", grid=(1,),
        in_specs=[_full((8, D)), _full((8, D)), _full((D, nw)), _full((1, nw))],
        out_specs=[_full((16, D)), _full((N_DEV, 16, nw))],
        out_shape=[_sds((16, D), F32), _sds((N_DEV, 16, nw), F32)],
        scratch=[pltpu.VMEM((8, D), F32), pltpu.VMEM((N_DEV, 8, D), F32), pltpu.VMEM((16, nw), F32),
                 pltpu.VMEM((N_DEV, 16, nw), F32)] + sems,
        args=(c8, cctx8, w_ada, b_my), comm=comm)
    return act, mod, got


def _gather_rows(x, name):
    def body(x_ref, out_ref, send_sems, recv_sems):
        _exchange_rows(x_ref, out_ref, send_sems, recv_sems)

    return pl.pallas_call(
        body, name=name,
        out_shape=_sds((N_DEV,) + x.shape, x.dtype),
        in_specs=[pl.BlockSpec(memory_space=pltpu.VMEM)],
        out_specs=pl.BlockSpec(memory_space=pltpu.VMEM),
        scratch_shapes=[pltpu.SemaphoreType.DMA((N_DEV - 1,)), pltpu.SemaphoreType.DMA((N_DEV - 1,))],
        compiler_params=pltpu.CompilerParams(vmem_limit_bytes=VMEM_LIMIT),
    )(x)


def _ada_bwd(act, dmod_my, w_ada, m, v, tr=256):
    nw = w_ada.shape[1]

    def body(act_ref, dm_ref, w_ref, m_ref, v_ref, g_ref, d_ref, m2_ref, v2_ref, pc_ref):
        dm = dm_ref[...].astype(BF16)
        g = lax.dot_general(act_ref[...].astype(BF16), dm, TN, preferred_element_type=F32)
        w = w_ref[...]
        delta, m2, v2 = _adamw(w, g, m_ref[...], v_ref[...])
        g_ref[...] = g
        d_ref[...] = delta
        m2_ref[...] = m2
        v2_ref[...] = v2
        pc_ref[...] = lax.dot_general(dm[8:16, :], w.astype(BF16), NT, preferred_element_type=F32)

    wspec = _row(tr, nw)
    return pl.pallas_call(
        body, name="ada_bwd", grid=(D // tr,),
        in_specs=[pl.BlockSpec((16, tr), lambda i: (0, i)), _full((16, nw)), wspec, wspec, wspec],
        out_specs=[wspec, wspec, wspec, wspec, pl.BlockSpec((8, tr), lambda i: (0, i))],
        out_shape=[_sds((D, nw), F32)] * 4 + [_sds((8, D), F32)],
        compiler_params=_params(("arbitrary",)),
    )(act, dmod_my, w_ada, m, v)


def _k_in(x, modv, w_in, cos, sin, tm, comm=None):
    L = x.shape[0]

    def body(x_ref, mod_ref, w_ref, cos_ref, sin_ref, h_ref, q_ref, k_ref, v_ref, u_ref, vb_ref, ga_ref, gb_ref):
        n, _ = _ln(x_ref[...])
        h = (n * (1.0 + mod_ref[1:2, :]) + mod_ref[0:1, :]).astype(BF16)
        h_ref[...] = h
        c, s = cos_ref[...], sin_ref[...]

        def proj(lo, width):
            return lax.dot_general(h, w_ref[lo:lo + width, :], NT, preferred_element_type=F32)

        for i in range(4):
            q_ref[:, i * 128:(i + 1) * 128] = (_rope(proj(O_Q + i * 128, 128), c, s) * Q_SCALE).astype(BF16)
        k_ref[...] = _rope(proj(O_K, KV_W), c, s).astype(BF16)
        v_ref[...] = proj(O_V, KV_W).astype(BF16)
        u_ref[...] = proj(O_U, GM_W).astype(BF16)
        vb_ref[...] = proj(O_VB, GM_W).astype(BF16)
        ga_ref[...] = proj(O_GA, D).astype(BF16)
        gb_ref[...] = proj(O_GB, D).astype(BF16)

    widths = [D, Q_W, KV_W, KV_W, GM_W, GM_W, D, D]
    return _call(
        body, name="fwd_in", grid=(L // tm,),
        in_specs=[_row(tm, D), _full((8, D)), _resident((IN_W, D)), _row(tm, 128), _row(tm, 128)],
        out_specs=[_row(tm, w) for w in widths],
        out_shape=[_sds((L, w), BF16) for w in widths],
        args=(x, modv, w_in, cos, sin), comm=comm)


def _k_ctx(ctx, modc, w_kv):
    C = ctx.shape[0]

    def body(c_ref, mod_ref, w_ref, hc_ref, kc_ref, vc_ref):
        n, _ = _ln(c_ref[...])
        hc = (n * (1.0 + mod_ref[1:2, :]) + mod_ref[0:1, :]).astype(BF16)
        hc_ref[...] = hc
        kv = lax.dot_general(hc, w_ref[...], NT, preferred_element_type=F32)
        kc_ref[...] = kv[:, :KV_W].astype(BF16)
        vc_ref[...] = kv[:, KV_W:].astype(BF16)

    return pl.pallas_call(
        body, name="fwd_ctx", grid=(1,),
        in_specs=[_full((C, D)), _full((8, D)), _full((2 * KV_W, D))],
        out_specs=[_full((C, D)), _full((C, KV_W)), _full((C, KV_W))],
        out_shape=[_sds((C, D), BF16), _sds((C, KV_W), BF16), _sds((C, KV_W), BF16)],
        compiler_params=_params(("arbitrary",)),
    )(ctx, modc, w_kv)


def _attn_bias():
    r = (np.arange(GQA_GROUP * BLK) & (BLK - 1))[:, None]
    j = np.arange(3 * BLK)[None, :]
    band = np.abs(j - BLK - r) <= BLK
    variants = [band & (j >= BLK), band, band & (j < 2 * BLK)]
    return jnp.asarray(np.stack([np.where(v, 0.0, NEG_INF) for v in variants]), F32)


def _masked(s, bias, C):
    return jnp.concatenate([s[:, :C], s[:, C:] + bias], axis=1)


def _sink_col(sink_ref, hk):
    grp = lax.broadcasted_iota(jnp.int32, (GQA_GROUP * BLK, 1), 0) >> 7
    col = jnp.full((GQA_GROUP * BLK, 1), sink_ref[hk * GQA_GROUP], F32)
    for g in range(1, GQA_GROUP):
        col = jnp.where(grp == g, sink_ref[hk * GQA_GROUP + g], col)
    return col


ATTN_FWD_BLOCKS = 4


def _k_attn(sink, q, k, v, kc, vc, bias, comm=None):
    L = q.shape[0]
    C = kc.shape[0]
    nb = L // BLK
    nq = min(ATTN_FWD_BLOCKS, nb)
    steps = nb // nq

    def body(sink_ref, q_ref, kp_ref, km_ref, kx_ref, vp_ref, vm_ref, vx_ref, kc_ref, vc_ref, bias_ref, ya_ref, lse_ref):
        i = pl.program_id(0)
        chains = [(qb, hk) for qb in range(nq) for hk in range(N_KV_HEADS)]

        def band(qb):
            first = jnp.where(i == 0, 0, 1) if qb == 0 else 1
            return bias_ref[jnp.where(i == steps - 1, 2, first) if qb == nq - 1 else first]

        def keys(ctx_ref, p_ref, m_ref, x_ref, qb, hk):
            sl = slice(hk * HEAD_DIM, (hk + 1) * HEAD_DIM)
            blocks = [p_ref[:, sl]] + [m_ref[j * BLK:(j + 1) * BLK, sl] for j in range(nq)] + [x_ref[:, sl]]
            return jnp.concatenate([ctx_ref[:, sl]] + blocks[qb:qb + 3], axis=0)

        def queries(qb, hk):
            return jnp.concatenate(
                [q_ref[qb * BLK:(qb + 1) * BLK, (hk * GQA_GROUP + g) * HEAD_DIM:(hk * GQA_GROUP + g + 1) * HEAD_DIM]
                 for g in range(GQA_GROUP)], axis=0)

        def scores(qb, hk):
            return _masked(lax.dot_general(queries(qb, hk), keys(kc_ref, kp_ref, km_ref, kx_ref, qb, hk), NT,
                                           preferred_element_type=F32), band(qb), C)

        ahead = 2
        s = [scores(*c) for c in chains[:ahead]]
        for n, (qb, hk) in enumerate(chains):
            if n + ahead < len(chains):
                s.append(scores(*chains[n + ahead]))
            s_ = s[n]
            sink_c = _sink_col(sink_ref, hk)
            m = jnp.maximum(jnp.max(s_, axis=1, keepdims=True), sink_c)
            p = jnp.exp(s_ - m)
            den = jnp.sum(p, axis=1, keepdims=True) + jnp.exp(sink_c - m)
            o = jnp.dot(p.astype(BF16), keys(vc_ref, vp_ref, vm_ref, vx_ref, qb, hk), preferred_element_type=F32) * (1.0 / den)
            lse = m + jnp.log(den)
            rows = slice(qb * BLK, (qb + 1) * BLK)
            for g in range(GQA_GROUP):
                h = hk * GQA_GROUP + g
                ya_ref[rows, h * HEAD_DIM:(h + 1) * HEAD_DIM] = o[g * BLK:(g + 1) * BLK, :].astype(BF16)
                lse_ref[rows, h:h + 1] = lse[g * BLK:(g + 1) * BLK, :]

    kv3 = [pl.BlockSpec((BLK, KV_W), lambda i: (jnp.maximum(nq * i - 1, 0), 0)),
           pl.BlockSpec((nq * BLK, KV_W), lambda i: (i, 0)),
           pl.BlockSpec((BLK, KV_W), lambda i: (jnp.minimum(nq * i + nq, nb - 1), 0))]
    return _call(
        body, name="fwd_attn", grid=(steps,),
        in_specs=[pl.BlockSpec(memory_space=pltpu.SMEM), _row(nq * BLK, Q_W)] + kv3 + kv3
                 + [_full((C, KV_W)), _full((C, KV_W)), _full((3, GQA_GROUP * BLK, 3 * BLK))],
        out_specs=[_row(nq * BLK, Q_W), _row(nq * BLK, N_Q_HEADS)],
        out_shape=[_sds((L, Q_W), BF16), _sds((L, N_Q_HEADS), F32)],
        args=(sink, q, k, k, k, v, v, v, kc, vc, bias), comm=comm)


GMLP_CHUNKS = 4


def _split_pair(t):
    low = lax.broadcasted_iota(jnp.int32, t.shape, 1) < GROUP_DIM
    zero = jnp.zeros_like(t)
    return jnp.where(low, t, zero), jnp.where(low, zero, t)


def _gmlp_spatial(w_ref, t_b, nch):
    rows = []
    for c in range(nch):
        tiles = []
        for pr in range(N_GROUPS // 2):
            lo, hi = _split_pair(t_b[c * BLK:(c + 1) * BLK, pr * 128:(pr + 1) * 128])
            tiles.append(jnp.dot(w_ref[2 * pr], lo, preferred_element_type=F32)
                         + jnp.dot(w_ref[2 * pr + 1], hi, preferred_element_type=F32))
        rows.append(jnp.concatenate(tiles, axis=1))
    return jnp.concatenate(rows, axis=0)


def _gmlp_fwd_vals(u, vb, lnv_ref, ws_ref, bsp_ref, nch):
    uf = u.astype(F32)
    vf = vb.astype(F32)
    gu, tu = _gelu(uf)
    gv, tv = _gelu(vf)
    vhat, rstd = _ln(gv)
    vn = (vhat * lnv_ref[0:1, :] + lnv_ref[1:2, :]).astype(BF16)
    s = _gmlp_spatial(ws_ref, vn, nch) + jnp.concatenate([bsp_ref[...]] * nch, axis=0)
    return uf, vf, gu, tu, tv, vhat, rstd, vn, s


def _k_gmlp(u, vb, lnv, ws, bsp):
    L = u.shape[0]
    nch = min(GMLP_CHUNKS, L // BLK)
    tm = nch * BLK

    def body(u_ref, vb_ref, lnv_ref, ws_ref, bsp_ref, yb_ref):
        _, _, gu, _, _, _, _, _, s = _gmlp_fwd_vals(u_ref[...], vb_ref[...], lnv_ref, ws_ref, bsp_ref, nch)
        yb_ref[...] = (gu * s).astype(BF16)

    return pl.pallas_call(
        body, name="fwd_gmlp", grid=(L // tm,),
        in_specs=[_row(tm, GM_W), _row(tm, GM_W), _full((8, GM_W)), _full((N_GROUPS, BLK, BLK)), _full((BLK, GM_W))],
        out_specs=_row(tm, GM_W),
        out_shape=_sds((L, GM_W), BF16),
        compiler_params=_params(("arbitrary",)),
    )(u, vb, lnv, ws, bsp)


def _k_merge(x, ya, yb, ga, gb, w_a, w_b, w_o, modv, lnv, tm):
    L = x.shape[0]

    def body(x_ref, ya_ref, yb_ref, ga_ref, gb_ref, wa_ref, wb_ref, wo_ref, mod_ref, ln_ref,
             mg_ref, mix_ref, xm_ref, h2_ref):
        a = jnp.dot(ya_ref[...], wa_ref[...], preferred_element_type=F32)
        b = jnp.dot(yb_ref[...], wb_ref[...], preferred_element_type=F32)
        merged = (_sigmoid(ga_ref[...].astype(F32)) * a + _sigmoid(gb_ref[...].astype(F32)) * b).astype(BF16)
        mg_ref[...] = merged
        mix = jnp.dot(merged, wo_ref[...], preferred_element_type=F32)
        mix_ref[...] = mix.astype(BF16)
        r1 = ALPHA * x_ref[...] + mod_ref[2:3, :] * mix
        r1hat, _ = _ln(r1)
        xm = r1hat * ln_ref[0:1, :] + ln_ref[1:2, :]
        xm_ref[...] = xm
        n2, _ = _ln(xm)
        h2_ref[...] = (n2 * (1.0 + mod_ref[4:5, :]) + mod_ref[3:4, :]).astype(BF16)

    return pl.pallas_call(
        body, name="fwd_merge", grid=(L // tm,),
        in_specs=[_row(tm, D), _row(tm, Q_W), _row(tm, GM_W), _row(tm, D), _row(tm, D),
                  _resident((Q_W, D)), _resident((GM_W, D)), _resident((D, D)), _full((8, D)), _full((8, D))],
        out_specs=[_row(tm, D)] * 4,
        out_shape=[_sds((L, D), BF16), _sds((L, D), BF16), _sds((L, D), F32), _sds((L, D), BF16)],
        compiler_params=_params(("arbitrary",)),
    )(x, ya, yb, ga, gb, w_a, w_b, w_o, modv, lnv)


FFN_CH = FFN_H // 2


def _k_ffn(h2, xm, tgt, w_fi, w_fo, modv, lnv, tm):
    L = h2.shape[0]

    def body(h2_ref, xm_ref, t_ref, wi_ref, wo_ref, mod_ref, ln_ref, gate_ref, up_ref, a_ref, dr2_ref, df_ref, acc_ref):
        @pl.when(pl.program_id(0) == 0)
        def _():
            acc_ref[...] = jnp.zeros_like(acc_ref)

        h2v = h2_ref[...]
        f = jnp.zeros((tm, D), F32)
        for j in range(FFN_H // FFN_CH):
            lo = j * FFN_CH
            gate = lax.dot_general(h2v, wi_ref[lo:lo + FFN_CH, :], NT, preferred_element_type=F32)
            up = lax.dot_general(h2v, wi_ref[FFN_H + lo:FFN_H + lo + FFN_CH, :], NT, preferred_element_type=F32)
            act = (gate * _sigmoid(gate) * up).astype(BF16)
            gate_ref[:, lo:lo + FFN_CH] = gate.astype(BF16)
            up_ref[:, lo:lo + FFN_CH] = up.astype(BF16)
            a_ref[:, lo:lo + FFN_CH] = act
            f = f + jnp.dot(act, wo_ref[lo:lo + FFN_CH, :], preferred_element_type=F32)
        gate2 = mod_ref[5:6, :]
        r2 = ALPHA * xm_ref[...] + gate2 * f
        r2hat, rstd = _ln(r2)
        y = r2hat * ln_ref[2:3, :] + ln_ref[3:4, :]
        err = y - t_ref[...]
        dy = err * (1.0 / D)
        dr2 = _ln_bwd(dy * ln_ref[2:3, :], r2hat, rstd)
        dr2_ref[...] = dr2
        df_ref[...] = (gate2 * dr2).astype(BF16)
        acc_ref[0:1, :] += _colsum(dy * r2hat)
        acc_ref[1:2, :] += _colsum(dy)
        acc_ref[2:3, :] += _colsum(dr2 * f)
        acc_ref[3:4, :] += _colsum(err * err) * (0.5 / D)

    return pl.pallas_call(
        body, name="fwd_ffn", grid=(L // tm,),
        in_specs=[_row(tm, D), _row(tm, D), _row(tm, D), _resident((2 * FFN_H, D)), _resident((FFN_H, D)),
                  _full((8, D)), _full((8, D))],
        out_specs=[_row(tm, FFN_H)] * 3 + [_row(tm, D), _row(tm, D), _full((8, D))],
        out_shape=[_sds((L, FFN_H), BF16)] * 3 + [_sds((L, D), F32), _sds((L, D), BF16), _sds((8, D), F32)],
        compiler_params=_params(("arbitrary",)),
    )(h2, xm, tgt, w_fi, w_fo, modv, lnv)


def _k_ffn_bwd(df, gate, up, xm, dr2, x, mix, w_fi, w_fo, modv, lnv, tm):
    L = df.shape[0]

    def body(df_ref, gate_ref, up_ref, xm_ref, dr2_ref, x_ref, mix_ref, wi_ref, wo_ref, mod_ref, ln_ref,
             dF_ref, dmix_ref, dxp_ref, acc_ref):
        @pl.when(pl.program_id(0) == 0)
        def _():
            acc_ref[...] = jnp.zeros_like(acc_ref)

        dfv = df_ref[...]
        chunks = [j * FFN_CH for j in range(FFN_H // FFN_CH)]
        das = [lax.dot_general(dfv, wo_ref[lo:lo + FFN_CH, :], NT, preferred_element_type=F32) for lo in chunks]
        n2, rstd2 = _ln(xm_ref[...])
        mixf = mix_ref[...].astype(F32)
        gate1 = mod_ref[2:3, :]
        r1hat, rstd1 = _ln(ALPHA * x_ref[...] + gate1 * mixf)
        dh2 = jnp.zeros((tm, D), F32)
        for lo, da in zip(chunks, das):
            gate = gate_ref[:, lo:lo + FFN_CH].astype(F32)
            upv = up_ref[:, lo:lo + FFN_CH].astype(F32)
            sg = _sigmoid(gate)
            d_gate = (da * upv * (sg * (1.0 + gate * (1.0 - sg)))).astype(BF16)
            d_up = (da * (gate * sg)).astype(BF16)
            dF_ref[:, lo:lo + FFN_CH] = d_gate
            dF_ref[:, FFN_H + lo:FFN_H + lo + FFN_CH] = d_up
            dh2 = dh2 + jnp.dot(d_gate, wi_ref[lo:lo + FFN_CH, :], preferred_element_type=F32)
            dh2 = dh2 + jnp.dot(d_up, wi_ref[FFN_H + lo:FFN_H + lo + FFN_CH, :], preferred_element_type=F32)
        acc_ref[0:1, :] += _colsum(dh2)
        acc_ref[1:2, :] += _colsum(dh2 * n2)
        dxm = ALPHA * dr2_ref[...] + _ln_bwd(dh2 * (1.0 + mod_ref[4:5, :]), n2, rstd2)
        acc_ref[2:3, :] += _colsum(dxm * r1hat)
        acc_ref[3:4, :] += _colsum(dxm)
        dr1 = _ln_bwd(dxm * ln_ref[0:1, :], r1hat, rstd1)
        dmix_ref[...] = (gate1 * dr1).astype(BF16)
        dxp_ref[...] = ALPHA * dr1
        acc_ref[4:5, :] += _colsum(dr1 * mixf)

    return pl.pallas_call(
        body, name="bwd_ffn", grid=(L // tm,),
        in_specs=[_row(tm, D), _row(tm, FFN_H), _row(tm, FFN_H), _row(tm, D), _row(tm, D), _row(tm, D), _row(tm, D),
                  _resident((2 * FFN_H, D)), _resident((FFN_H, D)), _full((8, D)), _full((8, D))],
        out_specs=[_row(tm, 2 * FFN_H), _row(tm, D), _row(tm, D), _full((8, D))],
        out_shape=[_sds((L, 2 * FFN_H), BF16), _sds((L, D), BF16), _sds((L, D), F32), _sds((8, D), F32)],
        compiler_params=_params(("arbitrary",)),
    )(df, gate, up, xm, dr2, x, mix, w_fi, w_fo, modv, lnv)


def _k_merge_bwd(dmix, merged, ya, yb, ga, gb, w_a, w_b, w_o, tm):
    L = dmix.shape[0]
    n = L // tm

    def body(dmix_ref, mg_ref, ya_ref, yb_ref, ga_ref, gb_ref, wa_ref, wb_ref, wo_ref,
             dga_ref, dgb_ref, dya_ref, dyb_ref, gwa_ref, gwb_ref, gwo_ref, acc_a, acc_b, acc_o):
        i = pl.program_id(0)

        @pl.when(i == 0)
        def _():
            for r in (acc_a, acc_b, acc_o):
                r[...] = jnp.zeros_like(r)

        dmixv = dmix_ref[...]
        dmg = lax.dot_general(dmixv, wo_ref[...], NT, preferred_element_type=F32)
        acc_o[...] += lax.dot_general(mg_ref[...], dmixv, TN, preferred_element_type=F32)
        ya = ya_ref[...]
        a = jnp.dot(ya, wa_ref[...], preferred_element_type=F32)
        sa = _sigmoid(ga_ref[...].astype(F32))
        dA = (dmg * sa).astype(BF16)
        dga_ref[...] = (dmg * a * (sa * (1.0 - sa))).astype(BF16)
        dya_ref[...] = lax.dot_general(dA, wa_ref[...], NT, preferred_element_type=F32).astype(BF16)
        acc_a[...] += lax.dot_general(ya, dA, TN, preferred_element_type=F32)
        yb = yb_ref[...]
        b = jnp.dot(yb, wb_ref[...], preferred_element_type=F32)
        sb = _sigmoid(gb_ref[...].astype(F32))
        dB = (dmg * sb).astype(BF16)
        dgb_ref[...] = (dmg * b * (sb * (1.0 - sb))).astype(BF16)
        dyb_ref[...] = lax.dot_general(dB, wb_ref[...], NT, preferred_element_type=F32).astype(BF16)
        acc_b[...] += lax.dot_general(yb, dB, TN, preferred_element_type=F32)

        @pl.when(i == n - 1)
        def _():
            gwa_ref[...] = acc_a[...].astype(BF16)
            gwb_ref[...] = acc_b[...].astype(BF16)
            gwo_ref[...] = acc_o[...].astype(BF16)

    return pl.pallas_call(
        body, name="bwd_merge", grid=(n,),
        in_specs=[_row(tm, D), _row(tm, D), _row(tm, Q_W), _row(tm, GM_W), _row(tm, D), _row(tm, D),
                  _resident((Q_W, D)), _resident((GM_W, D)), _resident((D, D))],
        out_specs=[_row(tm, D), _row(tm, D), _row(tm, Q_W), _row(tm, GM_W), _full((Q_W, D)), _full((GM_W, D)), _full((D, D))],
        out_shape=[_sds((L, D), BF16), _sds((L, D), BF16), _sds((L, Q_W), BF16), _sds((L, GM_W), BF16),
                   _sds((Q_W, D), BF16), _sds((GM_W, D), BF16), _sds((D, D), BF16)],
        scratch_shapes=[pltpu.VMEM((Q_W, D), F32), pltpu.VMEM((GM_W, D), F32), pltpu.VMEM((D, D), F32)],
        compiler_params=_params(("arbitrary",)),
    )(dmix, merged, ya, yb, ga, gb, w_a, w_b, w_o)


def _k_gmlp_bwd(u, vb, dyb, lnv, ws, wst, bsp):
    L = u.shape[0]
    nch = min(GMLP_CHUNKS, L // BLK)
    tm = nch * BLK

    def body(u_ref, vb_ref, dyb_ref, lnv_ref, ws_ref, wst_ref, bsp_ref, du_ref, dvb_ref, gws_ref, gbst_ref, gln_ref):
        @pl.when(pl.program_id(0) == 0)
        def _():
            gws_ref[...] = jnp.zeros_like(gws_ref)
            gbst_ref[...] = jnp.zeros_like(gbst_ref)
            gln_ref[...] = jnp.zeros_like(gln_ref)

        uf, vf, gu, tu, tv, vhat, rstd, vn, s = _gmlp_fwd_vals(u_ref[...], vb_ref[...], lnv_ref, ws_ref, bsp_ref, nch)
        dyb_f = dyb_ref[...].astype(F32)
        du_ref[...] = (dyb_f * s * _gelu_grad(uf, tu)).astype(BF16)
        ds = dyb_f * gu
        ds_b = ds.astype(BF16)
        for pr in range(N_GROUPS // 2):
            lanes = slice(pr * 128, (pr + 1) * 128)
            gw_lo = gw_hi = ds_sum = None
            for c in range(nch):
                rows = slice(c * BLK, (c + 1) * BLK)
                lo, hi = _split_pair(ds_b[rows, lanes])
                t_lo = lax.dot_general(lo, vn[rows, lanes], NT, preferred_element_type=F32)
                t_hi = lax.dot_general(hi, vn[rows, lanes], NT, preferred_element_type=F32)
                gw_lo = t_lo if c == 0 else gw_lo + t_lo
                gw_hi = t_hi if c == 0 else gw_hi + t_hi
                ds_sum = ds[rows, lanes] if c == 0 else ds_sum + ds[rows, lanes]
            gws_ref[2 * pr] += gw_lo
            gws_ref[2 * pr + 1] += gw_hi
            b_lo, b_hi = _split_pair(ds_sum)
            gbst_ref[:, 2 * pr:2 * pr + 1] += jnp.sum(b_lo, axis=1, keepdims=True)
            gbst_ref[:, 2 * pr + 1:2 * pr + 2] += jnp.sum(b_hi, axis=1, keepdims=True)
        dvn = _gmlp_spatial(wst_ref, ds_b, nch)
        gln_ref[0:1, :] += _colsum(dvn * vhat)
        gln_ref[1:2, :] += _colsum(dvn)
        dgv = _ln_bwd(dvn * lnv_ref[0:1, :], vhat, rstd)
        dvb_ref[...] = (dgv * _gelu_grad(vf, tv)).astype(BF16)

    return pl.pallas_call(
        body, name="bwd_gmlp", grid=(L // tm,),
        in_specs=[_row(tm, GM_W)] * 3 + [_full((8, GM_W)), _full((N_GROUPS, BLK, BLK)), _full((N_GROUPS, BLK, BLK)),
                                         _full((BLK, GM_W))],
        out_specs=[_row(tm, GM_W), _row(tm, GM_W), _full((N_GROUPS, BLK, BLK)), _full((BLK, N_GROUPS)), _full((8, GM_W))],
        out_shape=[_sds((L, GM_W), BF16), _sds((L, GM_W), BF16), _sds((N_GROUPS, BLK, BLK), F32),
                   _sds((BLK, N_GROUPS), F32), _sds((8, GM_W), F32)],
        compiler_params=_params(("arbitrary",)),
    )(u, vb, dyb, lnv, ws, wst, bsp)


ATTN_BWD_BLOCKS = 2


def _k_attn_bwd(sink, q, k, v, kc, vc, dya, lse, cos, sin, bias, comm=None):
    L = q.shape[0]
    C = kc.shape[0]
    nb = L // BLK
    nq = min(ATTN_BWD_BLOCKS, nb)
    steps = nb // nq
    NK = C + 3 * BLK
    chains = [(qb, hk) for qb in range(nq) for hk in range(N_KV_HEADS)]

    def body(sink_ref, q_ref, kp_ref, km_ref, kx_ref, vp_ref, vm_ref, vx_ref, kc_ref, vc_ref, do_ref, lse_ref,
             cq_ref, sq_ref, cl_ref, sl_ref, bias_ref,
             dq_ref, dk_ref, dv_ref, dkc_ref, dvc_ref, dsink_ref,
             dq_scr, ck_scr, cv_scr, k1_acc, k2_acc, v1_acc, v2_acc):
        i = pl.program_id(0)

        @pl.when(i == 0)
        def _():
            for r in (k1_acc, k2_acc, v1_acc, v2_acc, dkc_ref, dvc_ref, dsink_ref):
                r[...] = jnp.zeros_like(r)

        @pl.when(i < steps)
        def _():
            def band(qb):
                first = jnp.where(i == 0, 0, 1) if qb == 0 else 1
                return bias_ref[jnp.where(i == steps - 1, 2, first) if qb == nq - 1 else first]

            def lanes(hk):
                return slice(hk * HEAD_DIM, (hk + 1) * HEAD_DIM)

            def keys(ctx_ref, p_ref, m_ref, x_ref, qb, hk):
                sl = lanes(hk)
                blocks = [p_ref[:, sl]] + [m_ref[j * BLK:(j + 1) * BLK, sl] for j in range(nq)] + [x_ref[:, sl]]
                return jnp.concatenate([ctx_ref[:, sl]] + blocks[qb:qb + 3], axis=0)

            def stacked(ref, qb, hk, width):
                return jnp.concatenate(
                    [ref[qb * BLK:(qb + 1) * BLK, (hk * GQA_GROUP + g) * width:(hk * GQA_GROUP + g + 1) * width]
                     for g in range(GQA_GROUP)], axis=0)

            def scores(qb, hk):
                kcat = keys(kc_ref, kp_ref, km_ref, kx_ref, qb, hk)
                qg = stacked(q_ref, qb, hk, HEAD_DIM)
                s = _masked(lax.dot_general(qg, kcat, NT, preferred_element_type=F32), band(qb), C)
                dog = stacked(do_ref, qb, hk, HEAD_DIM)
                dp = lax.dot_general(dog, keys(vc_ref, vp_ref, vm_ref, vx_ref, qb, hk), NT, preferred_element_type=F32)
                return kcat, qg, dog, s, dp

            def softmax_bwd(qb, hk, s, dp):
                lse_c = stacked(lse_ref, qb, hk, 1)
                p = jnp.exp(s - lse_c)
                delta = jnp.sum(p * dp, axis=1, keepdims=True)
                ds = (p * (dp - delta)).astype(BF16)
                p_sink = jnp.exp(_sink_col(sink_ref, hk) - lse_c) * delta
                return p.astype(BF16), ds, p_sink

            def put_dq(qb, hk, dqs, p_sink):
                for g in range(GQA_GROUP):
                    h = hk * GQA_GROUP + g
                    dq_scr[qb * BLK:(qb + 1) * BLK, h * HEAD_DIM:(h + 1) * HEAD_DIM] = dqs[g * BLK:(g + 1) * BLK, :]
                    tot = jnp.sum(p_sink[g * BLK:(g + 1) * BLK, :], axis=0, keepdims=True)
                    dsink_ref[h:h + 1, :] -= jnp.broadcast_to(tot, (1, 128))

            ahead = 4
            sc = [scores(*c) for c in chains[:ahead]]
            pending = None
            for n, (qb, hk) in enumerate(chains):
                if n + ahead < len(chains):
                    sc.append(scores(*chains[n + ahead]))
                kcat, qg, dog, s, dp = sc[n]
                pb, ds, p_sink = softmax_bwd(qb, hk, s, dp)
                if pending is not None:
                    pqb, phk, pds, ppb, pqg, pdog = pending
                    ck_scr[pqb, :, lanes(phk)] = lax.dot_general(pds, pqg, TN, preferred_element_type=F32)
                    cv_scr[pqb, :, lanes(phk)] = lax.dot_general(ppb, pdog, TN, preferred_element_type=F32)
                put_dq(qb, hk, jnp.dot(ds, kcat, preferred_element_type=F32), p_sink)
                pending = (qb, hk, ds, pb, qg, dog)
            pqb, phk, pds, ppb, pqg, pdog = pending
            ck_scr[pqb, :, lanes(phk)] = lax.dot_general(pds, pqg, TN, preferred_element_type=F32)
            cq, sq = cq_ref[...], sq_ref[...]
            for j in range(4):
                dq_ref[:, j * 128:(j + 1) * 128] = _unrope(dq_scr[:, j * 128:(j + 1) * 128] * Q_SCALE, cq, sq).astype(BF16)
            cv_scr[pqb, :, lanes(phk)] = lax.dot_general(ppb, pdog, TN, preferred_element_type=F32)
            dkc_ref[...] += functools.reduce(lambda a, b: a + b, [ck_scr[qb, 0:C, :] for qb in range(nq)])
            dvc_ref[...] += functools.reduce(lambda a, b: a + b, [cv_scr[qb, 0:C, :] for qb in range(nq)])

        @pl.when(i >= steps)
        def _():
            ck_scr[...] = jnp.zeros_like(ck_scr)
            cv_scr[...] = jnp.zeros_like(cv_scr)

        def slot(scr, r, carried):
            parts = [scr[qb, C + (r - qb) * BLK:C + (r - qb + 1) * BLK, :] for qb in range(nq) if 0 <= r - qb <= 2]
            total = functools.reduce(lambda a, b: a + b, parts)
            return total if carried is None else carried[...] + total

        for r in range(nq):
            rows = slice(r * BLK, (r + 1) * BLK)
            carried_k, carried_v = ((k1_acc, v1_acc), (k2_acc, v2_acc), (None, None))[min(r, 2)]
            tables = (cl_ref[...], sl_ref[...]) if r == 0 else (cq_ref[(r - 1) * BLK:r * BLK, :], sq_ref[(r - 1) * BLK:r * BLK, :])
            dk_ref[rows, :] = _unrope(slot(ck_scr, r, carried_k), *tables).astype(BF16)
            dv_ref[rows, :] = slot(cv_scr, r, carried_v).astype(BF16)
        k1_acc[...] = slot(ck_scr, nq, None)
        v1_acc[...] = slot(cv_scr, nq, None)
        k2_acc[...] = slot(ck_scr, nq + 1, None)
        v2_acc[...] = slot(cv_scr, nq + 1, None)

    last = steps - 1
    kv3 = [pl.BlockSpec((BLK, KV_W), lambda i: (jnp.clip(nq * i - 1, 0, nb - 1), 0)),
           pl.BlockSpec((nq * BLK, KV_W), lambda i: (jnp.minimum(i, last), 0)),
           pl.BlockSpec((BLK, KV_W), lambda i: (jnp.minimum(nq * i + nq, nb - 1), 0))]
    cur = lambda w: pl.BlockSpec((nq * BLK, w), lambda i: (jnp.minimum(i, last), 0))
    late = lambda w: pl.BlockSpec((BLK, w), lambda i: (jnp.clip(nq * i - 1, 0, nb - 1), 0))
    out2 = lambda w: pl.BlockSpec((nq * BLK, w), lambda i: (i, 0))
    return _call(
        body, name="bwd_attn", grid=(steps + 1,),
        in_specs=[pl.BlockSpec(memory_space=pltpu.SMEM), cur(Q_W)] + kv3 + kv3
                 + [_full((C, KV_W)), _full((C, KV_W)), cur(Q_W), cur(N_Q_HEADS), cur(128), cur(128), late(128), late(128),
                    _full((3, GQA_GROUP * BLK, 3 * BLK))],
        out_specs=[cur(Q_W), out2(KV_W), out2(KV_W), _full((C, KV_W)), _full((C, KV_W)), _full((8, 128))],
        out_shape=[_sds((L, Q_W), BF16), _sds((L + nq * BLK, KV_W), BF16), _sds((L + nq * BLK, KV_W), BF16),
                   _sds((C, KV_W), F32), _sds((C, KV_W), F32), _sds((8, 128), F32)],
        scratch=[pltpu.VMEM((nq * BLK, Q_W), F32), pltpu.VMEM((nq, NK, KV_W), F32), pltpu.VMEM((nq, NK, KV_W), F32)]
                + [pltpu.VMEM((BLK, KV_W), F32)] * 4,
        args=(sink, q, k, k, k, v, v, v, kc, vc, dya, lse, cos, sin, cos, sin, bias), comm=comm)


def _k_ctx_bwd(ctx, modc, hc, dkc, dvc, w_kv):
    C = ctx.shape[0]

    def body(c_ref, mod_ref, hc_ref, dkc_ref, dvc_ref, w_ref, gw_ref, dmod_ref):
        dkv = jnp.concatenate([dkc_ref[...], dvc_ref[...]], axis=1).astype(BF16)
        gw_ref[...] = lax.dot_general(dkv, hc_ref[...], TN, preferred_element_type=F32)
        dhc = jnp.dot(dkv, w_ref[...], preferred_element_type=F32)
        n, _ = _ln(c_ref[...])
        dmod_ref[...] = jnp.zeros_like(dmod_ref)
        dmod_ref[0:1, :] = _colsum(dhc)
        dmod_ref[1:2, :] = _colsum(dhc * n)

    return pl.pallas_call(
        body, name="bwd_ctx", grid=(1,),
        in_specs=[_full((C, D)), _full((8, D)), _full((C, D)), _full((C, KV_W)), _full((C, KV_W)), _full((2 * KV_W, D))],
        out_specs=[_full((2 * KV_W, D)), _full((8, D))],
        out_shape=[_sds((2 * KV_W, D), F32), _sds((8, D), F32)],
        compiler_params=_params(("arbitrary",)),
    )(ctx, modc, hc, dkc, dvc, w_kv)


def _k_in_bwd(dq, dk, dv, du, dvb, dga, dgb, x, dxp, w_in, modv, tm, comm=None):
    L = x.shape[0]
    parts = [(O_Q, Q_W), (O_K, KV_W), (O_V, KV_W), (O_U, GM_W), (O_VB, GM_W), (O_GA, D), (O_GB, D)]

    def body(dq_ref, dk_ref, dv_ref, du_ref, dvb_ref, dga_ref, dgb_ref, x_ref, dxp_ref, w_ref, mod_ref,
             dP_ref, gx_ref, acc_ref):
        @pl.when(pl.program_id(0) == 0)
        def _():
            acc_ref[...] = jnp.zeros_like(acc_ref)

        for (lo, width), r in zip(parts, (dq_ref, dk_ref, dv_ref, du_ref, dvb_ref, dga_ref, dgb_ref)):
            dP_ref[:, lo:lo + width] = r[...]
        n1, rstd1 = _ln(x_ref[...])
        dh = jnp.dot(dP_ref[...], w_ref[...], preferred_element_type=F32)
        acc_ref[0:1, :] += _colsum(dh)
        acc_ref[1:2, :] += _colsum(dh * n1)
        gx_ref[...] = dxp_ref[...] + _ln_bwd(dh * (1.0 + mod_ref[1:2, :]), n1, rstd1)

    return _call(
        body, name="bwd_in", grid=(L // tm,),
        in_specs=[_row(tm, w) for _, w in parts] + [_row(tm, D), _row(tm, D), _resident((IN_W, D)), _full((8, D))],
        out_specs=[_row(tm, IN_W), _row(tm, D), _full((8, D))],
        out_shape=[_sds((L, IN_W), BF16), _sds((L, D), F32), _sds((8, D), F32)],
        args=(dq, dk, dv, du, dvb, dga, dgb, x, dxp, w_in, modv), comm=comm)


def _wgrad(a, b, name, tk, tt, comm=None, extra=None):
    T, K = a.shape
    N = b.shape[1]
    nt = T // tt

    def body(*refs):
        a_ref, b_ref = refs[:2]
        o_ref, acc_ref = refs[-2:]
        j, t = pl.program_id(0), pl.program_id(1)

        @pl.when(t == 0)
        def _():
            acc_ref[...] = jnp.zeros_like(acc_ref)

        acc_ref[...] += lax.dot_general(a_ref[...], b_ref[...], TN, preferred_element_type=F32)

        if extra is not None:
            lo, rows = extra[0] % tk, extra[1].shape[0]

            @pl.when((t == nt - 1) & (j == extra[0] // tk))
            def _():
                acc_ref[lo:lo + rows, :] += refs[2][...]

        @pl.when(t == nt - 1)
        def _():
            o_ref[...] = acc_ref[...].astype(BF16)

    extra_specs = [] if extra is None else [pl.BlockSpec(extra[1].shape, lambda j, t: (0, 0))]
    (out,), got = _call(
        body, name=name, grid=(K // tk, nt),
        in_specs=[pl.BlockSpec((tt, tk), lambda j, t: (t, j)), pl.BlockSpec((tt, N), lambda j, t: (t, 0))] + extra_specs,
        out_specs=[pl.BlockSpec((tk, N), lambda j, t: (j, 0))],
        out_shape=[_sds((K, N), BF16)],
        scratch=[pltpu.VMEM((tk, N), F32)],
        args=(a, b) + (() if extra is None else (extra[1],)), comm=comm)
    return (out, got) if comm is not None else out


def _adamw_reduce(parts, w, m, v, name, tr):
    R, C = w.shape
    n_parts = parts.shape[0]

    def body(p_ref, w_ref, m_ref, v_ref, g_ref, d_ref, m2_ref, v2_ref):
        g = p_ref[0].astype(F32)
        for i in range(1, n_parts):
            g = g + p_ref[i].astype(F32)
        delta, m2, v2 = _adamw(w_ref[...], g, m_ref[...], v_ref[...])
        g_ref[...] = g
        d_ref[...] = delta
        m2_ref[...] = m2
        v2_ref[...] = v2

    spec = _row(tr, C)
    return pl.pallas_call(
        body, name=name, grid=(R // tr,),
        in_specs=[pl.BlockSpec((n_parts, tr, C), lambda i: (0, i, 0)), spec, spec, spec],
        out_specs=[spec] * 4,
        out_shape=[_sds((R, C), F32)] * 4,
        compiler_params=_params(("arbitrary",)),
    )(parts, w, m, v)


SMALL_ORDER = ("b_ada", "ln1_g", "ln1_b", "ln2_g", "ln2_b", "gmlp_ln_g", "gmlp_ln_b", "b_spatial", "attn_sink")


def _small_step(gath, params):
    flat = [a for name in SMALL_ORDER for a in params[name]]

    def grad_of(tot, name):
        if name == "b_ada":
            return jnp.concatenate([tot[r:r + 1, :] for r in range(6)], axis=1)
        if name in ("ln1_g", "ln1_b", "ln2_g", "ln2_b"):
            r = 8 + ("ln1_g", "ln1_b", "ln2_g", "ln2_b").index(name)
            return tot[r:r + 1, :]
        if name == "gmlp_ln_g":
            return tot[12:13, :GM_W]
        if name == "gmlp_ln_b":
            return tot[12:13, GM_W:]
        if name == "b_spatial":
            return jnp.concatenate([tot[13:14, g * BLK:(g + 1) * BLK] for g in range(N_GROUPS)], axis=0)[None]
        return tot[14:15, :N_Q_HEADS]

    def body(*refs):
        g_ref, in_refs = refs[0], refs[1:1 + len(flat)]
        tot_ref, out_refs = refs[1 + len(flat)], refs[2 + len(flat):]
        tot = g_ref[0]
        for i in range(1, N_DEV):
            tot = tot + g_ref[i]
        tot_ref[...] = tot
        tot_ref[0:2, :] = tot[0:2, :] + tot[6:8, :]
        tot_ref[15:16, :] = jnp.broadcast_to(jnp.sum(tot[15:16, :], axis=1, keepdims=True), (1, D))
        tot = tot_ref[...]
        for k, name in enumerate(SMALL_ORDER):
            w_ref, m_ref, v_ref = in_refs[3 * k:3 * k + 3]
            g = grad_of(tot, name)
            delta, m2, v2 = _adamw(w_ref[...], g, m_ref[...], v_ref[...])
            for r, val in zip(out_refs[4 * k:4 * k + 4], (g, delta, m2, v2)):
                r[...] = val

    res = pl.pallas_call(
        body, name="small_step", grid=(1,),
        in_specs=[_full((N_DEV, 16, D))] + [_full(a.shape) for a in flat],
        out_specs=[_full((16, D))] + [_full(params[name][0].shape) for name in SMALL_ORDER for _ in range(4)],
        out_shape=[_sds((16, D), F32)] + [_sds(params[name][0].shape, F32) for name in SMALL_ORDER for _ in range(4)],
        compiler_params=_params(("arbitrary",)),
    )(gath, *flat)
    return res[0], {name: res[1 + 4 * k:5 + 4 * k] for k, name in enumerate(SMALL_ORDER)}


def _cctx_finish(gath, c_ctx, m, v):
    def body(g_ref, c_ref, m_ref, v_ref, gr_ref, d_ref, m2_ref, v2_ref):
        ds = g_ref[0]
        for i in range(1, N_DEV):
            ds = ds + g_ref[i]
        c = c_ref[...]
        sg = _sigmoid(c)
        g = ds * (sg * (1.0 + c * (1.0 - sg)))
        delta, m2, v2 = _adamw(c, g, m_ref[...], v_ref[...])
        gr_ref[...] = g
        d_ref[...] = delta
        m2_ref[...] = m2
        v2_ref[...] = v2

    return pl.pallas_call(
        body, name="cctx_finish", grid=(1,),
        in_specs=[_full((N_DEV, 8, D))] + [_full((8, D))] * 3, out_specs=[_full((8, D))] * 4,
        out_shape=[_sds((8, D), F32)] * 4,
        compiler_params=_params(("arbitrary",)),
    )(gath, c_ctx, m, v)


def _pad_rows(a, rows):
    return jnp.concatenate([a, jnp.zeros((rows - a.shape[0], a.shape[1]), a.dtype)], axis=0)


def kernel(x, c, ctx, c_ctx, w_ada, b_ada, w_in, attn_sink, gmlp_ln_g, gmlp_ln_b, w_spatial, b_spatial, w_branch_a, w_branch_b, w_out, ln1_g, ln1_b, w_ffn_in, w_ffn_out, ln2_g, ln2_b, loss_target, m_c_ctx, m_w_ada, m_b_ada, m_w_in, m_attn_sink, m_gmlp_ln_g, m_gmlp_ln_b, m_w_spatial, m_b_spatial, m_w_branch_a, m_w_branch_b, m_w_out, m_ln1_g, m_ln1_b, m_w_ffn_in, m_w_ffn_out, m_ln2_g, m_ln2_b, v_c_ctx, v_w_ada, v_b_ada, v_w_in, v_attn_sink, v_gmlp_ln_g, v_gmlp_ln_b, v_w_spatial, v_b_spatial, v_w_branch_a, v_w_branch_b, v_w_out, v_ln1_g, v_ln1_b, v_w_ffn_in, v_w_ffn_out, v_ln2_g, v_ln2_b):
    L = x.shape[1]
    me = 4 * lax.axis_index("x") + 2 * lax.axis_index("y") + lax.axis_index("c")
    x2, tgt, ctx2 = x[0], loss_target[0], ctx[0]
    tiles = _Tiles(L)
    tm_in, tm, tt = tiles.wide, tiles.narrow, tiles.tokens

    transposed = ("w_in", "w_ffn_in")
    tr = lambda kname, a: a.T if kname in transposed else a
    big = dict(w_in=w_in[0].T, w_branch_a=w_branch_a[0], w_branch_b=w_branch_b[0], w_out=w_out[0],
               w_ffn_in=w_ffn_in[0].T, w_ffn_out=w_ffn_out[0])
    col_sharded = ("w_branch_a", "w_branch_b")
    shard_bf = {k: a.astype(BF16) for k, a in big.items()}

    def assemble(kname, g):
        if kname in col_sharded:
            return g.transpose(1, 0, 2).reshape(g.shape[1], N_DEV * g.shape[2])
        return g.reshape(N_DEV * g.shape[1], g.shape[2])

    def to_blocks(kname, g):
        if kname in col_sharded:
            return g.reshape(g.shape[0], N_DEV, g.shape[1] // N_DEV).transpose(1, 0, 2)
        return g.reshape(N_DEV, g.shape[0] // N_DEV, g.shape[1])

    full = {}
    n_ada = w_ada.shape[2]
    b_my = lax.dynamic_slice(b_ada, (0, me * n_ada), (1, n_ada))
    act, mod_all, got = _prologue(_pad_rows(c, 8), _pad_rows(c_ctx[None, :], 8), w_ada[0], b_my,
                                  _Comm(gather=[shard_bf["w_in"]]))
    full["w_in"] = assemble("w_in", got[0])
    mod_all = mod_all.transpose(1, 0, 2).reshape(16, 6 * D)
    modv = _pad_rows(lax.dynamic_slice(mod_all, (me, 0), (1, 6 * D)).reshape(6, D), 8)
    modc = _pad_rows(mod_all[8].reshape(6, D), 8)

    lnv = _pad_rows(jnp.concatenate([ln1_g, ln1_b, ln2_g, ln2_b], axis=0), 8)
    gm_lnv = _pad_rows(jnp.concatenate([gmlp_ln_g, gmlp_ln_b], axis=0), 8)
    ws_b = w_spatial[0].astype(BF16)
    wst_b = ws_b.transpose(0, 2, 1)
    bsp = jnp.repeat(b_spatial[0].T, GROUP_DIM, axis=1)
    sink = attn_sink[0]
    cos, sin = _rope_tables(L)
    bias = _attn_bias()
    w_kv = full["w_in"][O_K:O_K + 2 * KV_W, :]

    (h, q, k, v, u, vb, ga, gb), got = _k_in(
        x2, modv, full["w_in"], cos, sin, tm_in,
        comm=_Comm(gather=[shard_bf[kname] for kname in ("w_branch_a", "w_branch_b", "w_out", "w_ffn_out")]))
    for kname, g in zip(("w_branch_a", "w_branch_b", "w_out", "w_ffn_out"), got):
        full[kname] = assemble(kname, g)
    hc, kc, vc = _k_ctx(ctx2, modc, w_kv)
    (ya, lse), got = _k_attn(sink, q, k, v, kc, vc, bias, comm=_Comm(gather=[shard_bf["w_ffn_in"]]))
    full["w_ffn_in"] = assemble("w_ffn_in", got[0])
    yb = _k_gmlp(u, vb, gm_lnv, ws_b, bsp)
    merged, mix, xm, h2 = _k_merge(x2, ya, yb, ga, gb, full["w_branch_a"], full["w_branch_b"], full["w_out"], modv, lnv, tm_in)
    gate, up, act_f, dr2, df, acc_f = _k_ffn(h2, xm, tgt, full["w_ffn_in"], full["w_ffn_out"], modv, lnv, tm_in)

    dF, dmix, dxp, acc_b = _k_ffn_bwd(df, gate, up, xm, dr2, x2, mix, full["w_ffn_in"], full["w_ffn_out"], modv, lnv, tm)
    blk_fo = to_blocks("w_ffn_out", _wgrad(act_f, df, "wgrad_ffn_out", tiles.tk_ffn, tt))
    gw_fi, (rcv_fo,) = _wgrad(dF, h2, "wgrad_ffn_in", tiles.tk_ffn, tt, comm=_Comm(scatter=[blk_fo]))
    blk_fi = to_blocks("w_ffn_in", gw_fi)
    dga, dgb, dya, dyb, gw_a, gw_b, gw_o = _k_merge_bwd(
        dmix, merged, ya, yb, ga, gb, full["w_branch_a"], full["w_branch_b"], full["w_out"], tm_in)
    du, dvb, g_ws, g_bst, g_gln = _k_gmlp_bwd(u, vb, dyb, gm_lnv, ws_b, wst_b, bsp)
    (dq, dk_late, dv_late, dkc, dvc, g_sink), (gath_ws, rcv_fi) = _k_attn_bwd(
        sink, q, k, v, kc, vc, dya, lse, cos, sin, bias,
        comm=_Comm(gather=[g_ws.reshape(N_GROUPS * BLK, BLK)], scatter=[blk_fi]))
    dk, dv = dk_late[BLK:BLK + L], dv_late[BLK:BLK + L]
    blk_a, blk_b, blk_o = to_blocks("w_branch_a", gw_a), to_blocks("w_branch_b", gw_b), to_blocks("w_out", gw_o)
    (dP, grad_x, acc_i), _ = _k_in_bwd(dq, dk, dv, du, dvb, dga, dgb, x2, dxp, full["w_in"], modv, tm_in)
    g_ctx, dmodc = _k_ctx_bwd(ctx2, modc, hc, dkc, dvc, w_kv)
    gw_in, (rcv_a, rcv_b, rcv_o) = _wgrad(dP, h, "wgrad_in", tiles.tk_in, tt, comm=_Comm(scatter=[blk_a, blk_b, blk_o]),
                                          extra=(O_K, g_ctx))

    dmod_x = jnp.concatenate([acc_i[0:2], acc_b[4:5], acc_b[0:2], acc_f[2:3]], axis=0)
    small = jnp.concatenate([
        dmod_x, dmodc[0:2], acc_b[2:4], acc_f[0:2],
        jnp.concatenate([g_gln[0:1], g_gln[1:2]], axis=1), g_bst.T.reshape(1, D),
        _pad_rows(g_sink[:, 0:1], D).T, acc_f[3:4]], axis=0)
    rcv_in, gath = _exchange_two_level(to_blocks("w_in", gw_in), small, "exchange_last")
    received = dict(w_in=rcv_in, w_branch_a=rcv_a, w_branch_b=rcv_b, w_out=rcv_o, w_ffn_in=rcv_fi, w_ffn_out=rcv_fo)
    moments = dict(w_in=(m_w_in, v_w_in), w_branch_a=(m_w_branch_a, v_w_branch_a), w_branch_b=(m_w_branch_b, v_w_branch_b),
                   w_out=(m_w_out, v_w_out), w_ffn_in=(m_w_ffn_in, v_w_ffn_in), w_ffn_out=(m_w_ffn_out, v_w_ffn_out))
    names = list(big)
    res = {}
    for kname in names:
        mm, vv = moments[kname]
        R = big[kname].shape[0]
        res[kname] = [tr(kname, r) for r in _adamw_reduce(
            received[kname], big[kname], tr(kname, mm[0]), tr(kname, vv[0]), "adamw_" + kname, 256 if R % 256 == 0 else R // 2)]

    ws2d = lambda a: a.reshape(N_GROUPS * BLK, BLK)
    res_ws = [r.reshape(w_spatial.shape) for r in _adamw_reduce(
        gath_ws, ws2d(w_spatial), ws2d(m_w_spatial), ws2d(v_w_spatial), "adamw_w_spatial", 256)]
    tot, res_small = _small_step(gath, dict(
        b_ada=(b_ada, m_b_ada, v_b_ada), ln1_g=(ln1_g, m_ln1_g, v_ln1_g), ln1_b=(ln1_b, m_ln1_b, v_ln1_b),
        ln2_g=(ln2_g, m_ln2_g, v_ln2_g), ln2_b=(ln2_b, m_ln2_b, v_ln2_b),
        gmlp_ln_g=(gmlp_ln_g, m_gmlp_ln_g, v_gmlp_ln_g), gmlp_ln_b=(gmlp_ln_b, m_gmlp_ln_b, v_gmlp_ln_b),
        b_spatial=(b_spatial, m_b_spatial, v_b_spatial), attn_sink=(attn_sink, m_attn_sink, v_attn_sink)))
    loss = tot[15, 0]

    dmod_rows = jnp.concatenate([gath[:, 0:6, :].reshape(N_DEV, 6 * D),
                                 jnp.concatenate([tot[6:8].reshape(1, 2 * D), jnp.zeros((1, 4 * D), F32)], axis=1),
                                 jnp.zeros((7, 6 * D), F32)], axis=0)
    dmod_my = lax.dynamic_slice(dmod_rows, (0, me * n_ada), (16, n_ada))
    g_wada, d_wada, m2_wada, v2_wada, pc = _ada_bwd(act, dmod_my, w_ada[0], m_w_ada[0], v_w_ada[0])
    pc_all = _gather_rows(pc, "gather_cctx")
    cc8 = lambda a: _pad_rows(a.reshape(1, D), 8)
    g_cc, d_cc, m2_cc, v2_cc = _cctx_finish(pc_all, cc8(c_ctx), cc8(m_c_ctx), cc8(v_c_ctx))

    order = ["c_ctx", "w_ada", "b_ada", "w_in", "attn_sink", "gmlp_ln_g", "gmlp_ln_b", "w_spatial", "b_spatial",
             "w_branch_a", "w_branch_b", "w_out", "ln1_g", "ln1_b", "w_ffn_in", "w_ffn_out", "ln2_g", "ln2_b"]
    grads, deltas, new_m, new_v = {}, {}, {}, {}
    grads["c_ctx"], deltas["c_ctx"], new_m["c_ctx"], new_v["c_ctx"] = g_cc[0], d_cc[0], m2_cc[0], v2_cc[0]
    grads["w_ada"], deltas["w_ada"], new_m["w_ada"], new_v["w_ada"] = g_wada[None], d_wada[None], m2_wada[None], v2_wada[None]
    for kname in names:
        g, d, m2, v2 = res[kname]
        grads[kname], deltas[kname], new_m[kname], new_v[kname] = g[None], d[None], m2[None], v2[None]
    grads["w_spatial"], deltas["w_spatial"], new_m["w_spatial"], new_v["w_spatial"] = res_ws
    for kname in SMALL_ORDER:
        grads[kname], deltas[kname], new_m[kname], new_v[kname] = res_small[kname]
    return (loss, grad_x[None], *[grads[n] for n in order], *[deltas[n] for n in order],
            *[new_m[n] for n in order], *[new_v[n] for n in order])
```

```python
import functools
import math

import jax
import jax.numpy as jnp
import numpy as np
from jax import lax
from jax.experimental import pallas as pl
from jax.experimental.pallas import tpu as pltpu

F32 = jnp.float32
BF16 = jnp.bfloat16
MESH = pl.DeviceIdType.MESH

N_DEV = 8
D = 1024
HEAD_DIM = 64
N_Q_HEADS = 8
N_KV_HEADS = 2
GQA_GROUP = 4
BLK = 128
Q_W = 512
KV_W = 128
GM_W = 512
N_GROUPS = 8
GROUP_DIM = 64
FFN_H = 2816
IN_W = 3840
O_Q, O_K, O_V, O_U, O_VB, O_GA, O_GB = 0, 512, 640, 768, 1280, 1792, 2816
LN_EPS = 1e-5
NEG_INF = -1e30
ALPHA = 2.0 ** 0.25
ROPE_BASE = 10000.0
ROPE_PAIRS = 16
Q_SCALE = HEAD_DIM ** -0.5
GELU_K0 = math.sqrt(2.0 / math.pi)
GELU_K1 = 0.044715

ADAM_LR = 0.001
ADAM_B1 = 0.9
ADAM_B2 = 0.999
ADAM_EPS = 1e-08
ADAM_WD = 0.01
ADAM_STEP = 10

V7X_VMEM_BYTES = 64 * 1024 * 1024
VMEM_LIMIT = V7X_VMEM_BYTES * 7 // 8
NT = (((1,), (1,)), ((), ()))
TN = (((0,), (0,)), ((), ()))


class _Tiles:
    def __init__(self, L):
        self.wide = min(512, L)
        self.narrow = min(256, L)
        self.tokens = min(2048, L)
        self.tk_in = IN_W // 3
        self.tk_ffn = FFN_H // 2


def _params(sem=None):
    return pltpu.CompilerParams(dimension_semantics=sem, vmem_limit_bytes=VMEM_LIMIT)


def _row(tm, w):
    return pl.BlockSpec((tm, w), lambda i: (i, 0))


def _full(shape):
    nd = len(shape)
    return pl.BlockSpec(shape, lambda i: (0,) * nd)


def _resident(shape):
    nd = len(shape)
    return pl.BlockSpec(shape, lambda i: (0,) * nd, pipeline_mode=pl.Buffered(1))


def _sds(shape, dt):
    return jax.ShapeDtypeStruct(shape, dt)


def _ln(xf):
    mu = jnp.mean(xf, axis=-1, keepdims=True)
    xc = xf - mu
    var = jnp.mean(xc * xc, axis=-1, keepdims=True)
    rstd = lax.rsqrt(var + LN_EPS)
    return xc * rstd, rstd


def _ln_bwd(dn, n, rstd):
    m1 = jnp.mean(dn, axis=-1, keepdims=True)
    m2 = jnp.mean(dn * n, axis=-1, keepdims=True)
    return rstd * (dn - m1 - n * m2)


def _colsum(t):
    return jnp.sum(t, axis=0, keepdims=True)


def _sigmoid(x):
    return 0.5 * jnp.tanh(0.5 * x) + 0.5


def _gelu(x):
    t = jnp.tanh(GELU_K0 * (x + GELU_K1 * (x * x * x)))
    return x * (0.5 * (1.0 + t)), t


def _gelu_grad(x, t):
    return 0.5 * (1.0 + t) + 0.5 * x * (1.0 - t * t) * (GELU_K0 * (1.0 + 3.0 * GELU_K1 * x * x))


def _swap16(t):
    lane = lax.broadcasted_iota(jnp.int32, t.shape, 1)
    return jnp.where((lane & 16) == 0, pltpu.roll(t, 112, 1), pltpu.roll(t, 16, 1))


def _rope(t, cos, sin):
    return t * cos + _swap16(t) * sin


def _unrope(t, cos, sin):
    return t * cos - _swap16(t) * sin


def _adamw(w, g, m, v):
    m2 = ADAM_B1 * m + (1.0 - ADAM_B1) * g
    v2 = ADAM_B2 * v + (1.0 - ADAM_B2) * (g * g)
    m_hat = m2 / (1.0 - ADAM_B1 ** ADAM_STEP)
    v_hat = v2 / (1.0 - ADAM_B2 ** ADAM_STEP)
    delta = -ADAM_LR * (m_hat / (jnp.sqrt(v_hat) + ADAM_EPS) + ADAM_WD * w)
    return delta, m2, v2


def _rope_tables(L):
    inv = (np.float32(ROPE_BASE) ** (-np.arange(ROPE_PAIRS, dtype=np.float32) / np.float32(ROPE_PAIRS))).astype(np.float32)
    t = np.arange(L, dtype=np.int32)
    rows = (t // 64).astype(np.float32)[:, None] * inv
    cols = (t % 64).astype(np.float32)[:, None] * inv
    cr, sr, cc, sc = np.cos(rows), np.sin(rows), np.cos(cols), np.sin(cols)
    cos = np.concatenate([cr, cr, cc, cc], axis=1)
    sin = np.concatenate([-sr, sr, -sc, sc], axis=1)
    return jnp.asarray(np.tile(cos, (1, 2)), F32), jnp.asarray(np.tile(sin, (1, 2)), F32)


def _me():
    return lax.axis_index("x"), lax.axis_index("y"), lax.axis_index("c")


def _peer(mx, my, mc, k):
    return (mx ^ ((k >> 2) & 1), my ^ ((k >> 1) & 1), mc ^ (k & 1))


class _Comm:
    def __init__(self, gather=(), scatter=(), spread=()):
        self.kinds = ["gather"] * len(gather) + ["scatter"] * len(scatter) + ["spread"] * len(spread)
        self.args = list(gather) + list(scatter) + list(spread)
        self.n = len(self.args)

    def out_shape(self):
        return [_sds(a.shape if k == "scatter" else (N_DEV,) + a.shape, a.dtype) for k, a in zip(self.kinds, self.args)]

    def specs(self):
        return [pl.BlockSpec(memory_space=pl.ANY)] * self.n

    def scratch(self):
        return [pltpu.SemaphoreType.DMA((7 * self.n,)), pltpu.SemaphoreType.DMA((7 * self.n,)),
                pltpu.SemaphoreType.DMA((self.n,))]

    def _plan(self, x_refs, out_refs, send_sems, recv_sems, local_sems):
        mx, my, mc = _me()
        me = 4 * mx + 2 * my + mc
        here, sibling = (mx, my, mc), (mx, my, 1 - mc)
        chips = [(1 - mx, my), (mx, 1 - my), (1 - mx, 1 - my)]
        local, first, last = [], [], []
        relay = [[], [], []]
        for a, kind in enumerate(self.kinds):
            x, out = x_refs[a], out_refs[a]

            def rc(k, src, dst, to):
                return pltpu.make_async_remote_copy(
                    src_ref=src, dst_ref=dst, send_sem=send_sems.at[7 * a + k], recv_sem=recv_sems.at[7 * a + k],
                    device_id=to, device_id_type=MESH)

            if kind == "gather":
                local.append(pltpu.make_async_copy(x, out.at[me], local_sems.at[a]))
                first.append(rc(0, x, out.at[me], sibling))
                last.append(rc(0, x, out.at[me ^ 1], here))
                for j, (cx, cy) in enumerate(chips):
                    first.append(rc(1 + j, x, out.at[me], (cx, cy, mc)))
                    landed = out.at[4 * cx + 2 * cy + mc]
                    relay[j].append((rc(1 + j, x, landed, here), rc(4 + j, landed, landed, sibling)))
                    last.append(rc(4 + j, x, out.at[4 * cx + 2 * cy + 1 - mc], here))
            else:
                own = x.at[me] if kind == "scatter" else x
                local.append(pltpu.make_async_copy(own, out.at[me], local_sems.at[a]))
                for k in range(1, N_DEV):
                    src = x.at[me ^ k] if kind == "scatter" else x
                    first.append(rc(k - 1, src, out.at[me], _peer(mx, my, mc, k)))
                    last.append(rc(k - 1, own, out.at[me ^ k], here))
        return local, first, relay[0] + relay[1] + relay[2], last

    def start(self, *refs):
        local, first, _, _ = self._plan(*refs)
        for cp in local + first:
            cp.start()

    def relay(self, *refs):
        _, _, relay, _ = self._plan(*refs)
        for arrival, onward in relay:
            arrival.wait_recv()
            onward.start()

    def finish(self, *refs):
        local, first, relay, last = self._plan(*refs)
        for cp in last:
            cp.wait_recv()
        for cp in first:
            cp.wait_send()
        for _, onward in relay:
            onward.wait_send()
        for cp in local:
            cp.wait()


def _call(body, *, name, grid, in_specs, out_specs, out_shape, args, scratch=(), comm=None, aliases=None):
    params = _params(("arbitrary",) * len(grid))
    total = math.prod(grid)

    def at(step):
        flat = functools.reduce(lambda acc, dn: acc * dn[1] + pl.program_id(dn[0]), enumerate(grid), 0)
        return flat == step

    if comm is None:
        res = pl.pallas_call(
            body, name=name, grid=grid, in_specs=list(in_specs), out_specs=list(out_specs), out_shape=list(out_shape),
            scratch_shapes=list(scratch), input_output_aliases=aliases or {}, compiler_params=params)(*args)
        return list(res), []
    n_in, n_out, n_scr, cn = len(in_specs), len(out_specs), len(scratch), comm.n

    def hosted(*refs):
        ins, refs = refs[:n_in], refs[n_in:]
        cins, refs = refs[:cn], refs[cn:]
        outs, refs = refs[:n_out], refs[n_out:]
        couts, refs = refs[:cn], refs[cn:]
        scr, sems = refs[:n_scr], refs[n_scr:]

        @pl.when(at(0))
        def _():
            comm.start(cins, couts, *sems)

        body(*ins, *outs, *scr)

        @pl.when(at((3 * total) // 4 if total >= 4 else total - 1))
        def _():
            comm.relay(cins, couts, *sems)

        @pl.when(at(total - 1))
        def _():
            comm.finish(cins, couts, *sems)

    res = pl.pallas_call(
        hosted, name=name, grid=grid, in_specs=list(in_specs) + comm.specs(), out_specs=list(out_specs) + comm.specs(),
        out_shape=list(out_shape) + comm.out_shape(), scratch_shapes=list(scratch) + comm.scratch(),
        input_output_aliases=aliases or {}, compiler_params=params)(*args, *comm.args)
    return list(res[:n_out]), list(res[n_out:])


def _exchange_two_level(blk, small, name):
    _, R, C = blk.shape
    rows = small.shape[0]

    def body(blk_ref, small_ref, stage_ref, out_ref, gath_ref, a_scr, b_scr, t_scr, s1, r1, s3, r3, ss, rs, lsem):
        mx, my, mc = _me()
        me = 4 * mx + 2 * my + mc
        mine = 2 * mx + my
        here, sibling = (mx, my, mc), (mx, my, 1 - mc)

        def rc(src, dst, send, recv, to):
            return pltpu.make_async_remote_copy(src_ref=src, dst_ref=dst, send_sem=send, recv_sem=recv,
                                                device_id=to, device_id_type=MESH)

        own_small = pltpu.make_async_copy(small_ref, gath_ref.at[me], lsem.at[0])
        own_small.start()
        spread = [rc(small_ref, gath_ref.at[me], ss.at[k - 1], rs.at[k - 1], _peer(mx, my, mc, k)) for k in range(1, N_DEV)]
        to_sib = [rc(blk_ref.at[2 * p + 1 - mc], stage_ref.at[p], s1.at[p], r1.at[p], sibling) for p in range(4)]
        for cp in spread + to_sib:
            cp.start()
        own = [pltpu.make_async_copy(blk_ref.at[2 * p + mc], a_scr.at[p], lsem.at[1 + p]) for p in range(4)]
        for cp in own:
            cp.start()
        from_sib = []
        for p in range(4):
            rc(blk_ref.at[2 * p + 1 - mc], stage_ref.at[p], s1.at[p], r1.at[p], here).wait_recv()
            cp = pltpu.make_async_copy(stage_ref.at[p], b_scr.at[p], lsem.at[5 + p])
            cp.start()
            from_sib.append(cp)
        for cp in own + from_sib:
            cp.wait()
        t_scr[...] = (a_scr[...].astype(F32) + b_scr[...].astype(F32)).astype(BF16)
        keep = pltpu.make_async_copy(t_scr.at[mine], out_ref.at[mine], lsem.at[9])
        keep.start()
        onward = [rc(t_scr.at[mine ^ k], out_ref.at[mine], s3.at[k - 1], r3.at[k - 1], (mx ^ (k >> 1), my ^ (k & 1), mc))
                  for k in range(1, 4)]
        for cp in onward:
            cp.start()
        for k in range(1, 4):
            rc(t_scr.at[mine], out_ref.at[mine ^ k], s3.at[k - 1], r3.at[k - 1], here).wait_recv()
        for k in range(1, N_DEV):
            rc(small_ref, gath_ref.at[me ^ k], ss.at[k - 1], rs.at[k - 1], here).wait_recv()
        for cp in spread + to_sib + onward:
            cp.wait_send()
        keep.wait()
        own_small.wait()

    any_spec = pl.BlockSpec(memory_space=pl.ANY)
    dma = pltpu.SemaphoreType.DMA
    _, out, gath = pl.pallas_call(
        body, name=name,
        in_specs=[any_spec, any_spec], out_specs=[any_spec] * 3,
        out_shape=[_sds((4, R, C), BF16), _sds((4, R, C), BF16), _sds((N_DEV, rows, D), F32)],
        scratch_shapes=[pltpu.VMEM((4, R, C), BF16)] * 3
                       + [dma((4,)), dma((4,)), dma((3,)), dma((3,)), dma((N_DEV - 1,)), dma((N_DEV - 1,)), dma((10,))],
        compiler_params=pltpu.CompilerParams(vmem_limit_bytes=VMEM_LIMIT),
    )(blk, small)
    return out, gath


def _exchange_rows(x_ref, out_ref, send_sems, recv_sems):
    mx, my, mc = _me()
    me = 4 * mx + 2 * my + mc
    out_ref[pl.ds(me, 1)] = x_ref[...][None]
    sends = []
    for k in range(1, N_DEV):
        cp = pltpu.make_async_remote_copy(
            src_ref=x_ref, dst_ref=out_ref.at[me], send_sem=send_sems.at[k - 1], recv_sem=recv_sems.at[k - 1],
            device_id=_peer(mx, my, mc, k), device_id_type=MESH)
        cp.start()
        sends.append(cp)
    for k in range(1, N_DEV):
        pltpu.make_async_remote_copy(
            src_ref=x_ref, dst_ref=out_ref.at[me ^ k], send_sem=send_sems.at[k - 1], recv_sem=recv_sems.at[k - 1],
            device_id=(mx, my, mc), device_id_type=MESH).wait_recv()
    for cp in sends:
        cp.wait_send()


def _prologue(c8, cctx8, w_ada, b_my, comm):
    nw = w_ada.shape[1]

    def body(c_ref, cctx_ref, w_ref, b_ref, act_ref, mod_ref, cmine_scr, call_scr, mine_scr, mall_scr, s1, r1, s2, r2):
        cmine_scr[...] = c_ref[...]
        _exchange_rows(cmine_scr, call_scr, s1, r1)
        rows = [call_scr[d][0:1, :] for d in range(N_DEV)] + [cctx_ref[0:1, :], jnp.zeros((7, D), F32)]
        s = jnp.concatenate(rows, axis=0)
        act = s * _sigmoid(s)
        act_ref[...] = act
        mine_scr[...] = jnp.dot(act.astype(BF16), w_ref[...].astype(BF16), preferred_element_type=F32) + b_ref[...]
        _exchange_rows(mine_scr, mall_scr, s2, r2)
        mod_ref[...] = mall_scr[...]

    sems = [pltpu.SemaphoreType.DMA((N_DEV - 1,))] * 4
    (act, mod), got = _call(
        body, name="prologue", grid=(1,),
        in_specs=[_full((8, D)), _full((8, D)), _full((D, nw)), _full((1, nw))],
        out_specs=[_full((16, D)), _full((N_DEV, 16, nw))],
        out_shape=[_sds((16, D), F32), _sds((N_DEV, 16, nw), F32)],
        scratch=[pltpu.VMEM((8, D), F32), pltpu.VMEM((N_DEV, 8, D), F32), pltpu.VMEM((16, nw), F32),
                 pltpu.VMEM((N_DEV, 16, nw), F32)] + sems,
        args=(c8, cctx8, w_ada, b_my), comm=comm)
    return act, mod, got


def _gather_rows(x, name):
    def body(x_ref, out_ref, send_sems, recv_sems):
        _exchange_rows(x_ref, out_ref, send_sems, recv_sems)

    return pl.pallas_call(
        body, name=name,
        out_shape=_sds((N_DEV,) + x.shape, x.dtype),
        in_specs=[pl.BlockSpec(memory_space=pltpu.VMEM)],
        out_specs=pl.BlockSpec(memory_space=pltpu.VMEM),
        scratch_shapes=[pltpu.SemaphoreType.DMA((N_DEV - 1,)), pltpu.SemaphoreType.DMA((N_DEV - 1,))],
        compiler_params=pltpu.CompilerParams(vmem_limit_bytes=VMEM_LIMIT),
    )(x)


def _ada_bwd(act, dmod_my, w_ada, m, v, tr=256):
    nw = w_ada.shape[1]

    def body(act_ref, dm_ref, w_ref, m_ref, v_ref, g_ref, d_ref, m2_ref, v2_ref, pc_ref):
        dm = dm_ref[...].astype(BF16)
        g = lax.dot_general(act_ref[...].astype(BF16), dm, TN, preferred_element_type=F32)
        w = w_ref[...]
        delta, m2, v2 = _adamw(w, g, m_ref[...], v_ref[...])
        g_ref[...] = g
        d_ref[...] = delta
        m2_ref[...] = m2
        v2_ref[...] = v2
        pc_ref[...] = lax.dot_general(dm[8:16, :], w.astype(BF16), NT, preferred_element_type=F32)

    wspec = _row(tr, nw)
    return pl.pallas_call(
        body, name="ada_bwd", grid=(D // tr,),
        in_specs=[pl.BlockSpec((16, tr), lambda i: (0, i)), _full((16, nw)), wspec, wspec, wspec],
        out_specs=[wspec, wspec, wspec, wspec, pl.BlockSpec((8, tr), lambda i: (0, i))],
        out_shape=[_sds((D, nw), F32)] * 4 + [_sds((8, D), F32)],
        compiler_params=_params(("arbitrary",)),
    )(act, dmod_my, w_ada, m, v)


def _k_in(x, modv, w_in, cos, sin, tm, comm=None):
    L = x.shape[0]

    def body(x_ref, mod_ref, w_ref, cos_ref, sin_ref, h_ref, q_ref, k_ref, v_ref, u_ref, vb_ref, ga_ref, gb_ref):
        n, _ = _ln(x_ref[...])
        h = (n * (1.0 + mod_ref[1:2, :]) + mod_ref[0:1, :]).astype(BF16)
        h_ref[...] = h
        c, s = cos_ref[...], sin_ref[...]

        def proj(lo, width):
            return lax.dot_general(h, w_ref[lo:lo + width, :], NT, preferred_element_type=F32)

        for i in range(4):
            q_ref[:, i * 128:(i + 1) * 128] = (_rope(proj(O_Q + i * 128, 128), c, s) * Q_SCALE).astype(BF16)
        k_ref[...] = _rope(proj(O_K, KV_W), c, s).astype(BF16)
        v_ref[...] = proj(O_V, KV_W).astype(BF16)
        u_ref[...] = proj(O_U, GM_W).astype(BF16)
        vb_ref[...] = proj(O_VB, GM_W).astype(BF16)
        ga_ref[...] = proj(O_GA, D).astype(BF16)
        gb_ref[...] = proj(O_GB, D).astype(BF16)

    widths = [D, Q_W, KV_W, KV_W, GM_W, GM_W, D, D]
    return _call(
        body, name="fwd_in", grid=(L // tm,),
        in_specs=[_row(tm, D), _full((8, D)), _resident((IN_W, D)), _row(tm, 128), _row(tm, 128)],
        out_specs=[_row(tm, w) for w in widths],
        out_shape=[_sds((L, w), BF16) for w in widths],
        args=(x, modv, w_in, cos, sin), comm=comm)


def _k_ctx(ctx, modc, w_kv):
    C = ctx.shape[0]

    def body(c_ref, mod_ref, w_ref, hc_ref, kc_ref, vc_ref):
        n, _ = _ln(c_ref[...])
        hc = (n * (1.0 + mod_ref[1:2, :]) + mod_ref[0:1, :]).astype(BF16)
        hc_ref[...] = hc
        kv = lax.dot_general(hc, w_ref[...], NT, preferred_element_type=F32)
        kc_ref[...] = kv[:, :KV_W].astype(BF16)
        vc_ref[...] = kv[:, KV_W:].astype(BF16)

    return pl.pallas_call(
        body, name="fwd_ctx", grid=(1,),
        in_specs=[_full((C, D)), _full((8, D)), _full((2 * KV_W, D))],
        out_specs=[_full((C, D)), _full((C, KV_W)), _full((C, KV_W))],
        out_shape=[_sds((C, D), BF16), _sds((C, KV_W), BF16), _sds((C, KV_W), BF16)],
        compiler_params=_params(("arbitrary",)),
    )(ctx, modc, w_kv)


def _attn_bias():
    r = (np.arange(GQA_GROUP * BLK) & (BLK - 1))[:, None]
    j = np.arange(3 * BLK)[None, :]
    band = np.abs(j - BLK - r) <= BLK
    variants = [band & (j >= BLK), band, band & (j < 2 * BLK)]
    return jnp.asarray(np.stack([np.where(v, 0.0, NEG_INF) for v in variants]), F32)


def _masked(s, bias, C):
    return jnp.concatenate([s[:, :C], s[:, C:] + bias], axis=1)


def _sink_col(sink_ref, hk):
    grp = lax.broadcasted_iota(jnp.int32, (GQA_GROUP * BLK, 1), 0) >> 7
    col = jnp.full((GQA_GROUP * BLK, 1), sink_ref[hk * GQA_GROUP], F32)
    for g in range(1, GQA_GROUP):
        col = jnp.where(grp == g, sink_ref[hk * GQA_GROUP + g], col)
    return col


ATTN_FWD_BLOCKS = 4


def _k_attn(sink, q, k, v, kc, vc, bias, comm=None):
    L = q.shape[0]
    C = kc.shape[0]
    nb = L // BLK
    nq = min(ATTN_FWD_BLOCKS, nb)
    steps = nb // nq

    def body(sink_ref, q_ref, kp_ref, km_ref, kx_ref, vp_ref, vm_ref, vx_ref, kc_ref, vc_ref, bias_ref, ya_ref, lse_ref):
        i = pl.program_id(0)
        chains = [(qb, hk) for qb in range(nq) for hk in range(N_KV_HEADS)]

        def band(qb):
            first = jnp.where(i == 0, 0, 1) if qb == 0 else 1
            return bias_ref[jnp.where(i == steps - 1, 2, first) if qb == nq - 1 else first]

        def keys(ctx_ref, p_ref, m_ref, x_ref, qb, hk):
            sl = slice(hk * HEAD_DIM, (hk + 1) * HEAD_DIM)
            blocks = [p_ref[:, sl]] + [m_ref[j * BLK:(j + 1) * BLK, sl] for j in range(nq)] + [x_ref[:, sl]]
            return jnp.concatenate([ctx_ref[:, sl]] + blocks[qb:qb + 3], axis=0)

        def queries(qb, hk):
            return jnp.concatenate(
                [q_ref[qb * BLK:(qb + 1) * BLK, (hk * GQA_GROUP + g) * HEAD_DIM:(hk * GQA_GROUP + g + 1) * HEAD_DIM]
                 for g in range(GQA_GROUP)], axis=0)

        def scores(qb, hk):
            return _masked(lax.dot_general(queries(qb, hk), keys(kc_ref, kp_ref, km_ref, kx_ref, qb, hk), NT,
                                           preferred_element_type=F32), band(qb), C)

        ahead = 2
        s = [scores(*c) for c in chains[:ahead]]
        for n, (qb, hk) in enumerate(chains):
            if n + ahead < len(chains):
                s.append(scores(*chains[n + ahead]))
            s_ = s[n]
            sink_c = _sink_col(sink_ref, hk)
            m = jnp.maximum(jnp.max(s_, axis=1, keepdims=True), sink_c)
            p = jnp.exp(s_ - m)
            den = jnp.sum(p, axis=1, keepdims=True) + jnp.exp(sink_c - m)
            o = jnp.dot(p.astype(BF16), keys(vc_ref, vp_ref, vm_ref, vx_ref, qb, hk), preferred_element_type=F32) * (1.0 / den)
            lse = m + jnp.log(den)
            rows = slice(qb * BLK, (qb + 1) * BLK)
            for g in range(GQA_GROUP):
                h = hk * GQA_GROUP + g
                ya_ref[rows, h * HEAD_DIM:(h + 1) * HEAD_DIM] = o[g * BLK:(g + 1) * BLK, :].astype(BF16)
                lse_ref[rows, h:h + 1] = lse[g * BLK:(g + 1) * BLK, :]

    kv3 = [pl.BlockSpec((BLK, KV_W), lambda i: (jnp.maximum(nq * i - 1, 0), 0)),
           pl.BlockSpec((nq * BLK, KV_W), lambda i: (i, 0)),
           pl.BlockSpec((BLK, KV_W), lambda i: (jnp.minimum(nq * i + nq, nb - 1), 0))]
    return _call(
        body, name="fwd_attn", grid=(steps,),
        in_specs=[pl.BlockSpec(memory_space=pltpu.SMEM), _row(nq * BLK, Q_W)] + kv3 + kv3
                 + [_full((C, KV_W)), _full((C, KV_W)), _full((3, GQA_GROUP * BLK, 3 * BLK))],
        out_specs=[_row(nq * BLK, Q_W), _row(nq * BLK, N_Q_HEADS)],
        out_shape=[_sds((L, Q_W), BF16), _sds((L, N_Q_HEADS), F32)],
        args=(sink, q, k, k, k, v, v, v, kc, vc, bias), comm=comm)


GMLP_CHUNKS = 4


def _split_pair(t):
    low = lax.broadcasted_iota(jnp.int32, t.shape, 1) < GROUP_DIM
    zero = jnp.zeros_like(t)
    return jnp.where(low, t, zero), jnp.where(low, zero, t)


def _gmlp_spatial(w_ref, t_b, nch):
    rows = []
    for c in range(nch):
        tiles = []
        for pr in range(N_GROUPS // 2):
            lo, hi = _split_pair(t_b[c * BLK:(c + 1) * BLK, pr * 128:(pr + 1) * 128])
            tiles.append(jnp.dot(w_ref[2 * pr], lo, preferred_element_type=F32)
                         + jnp.dot(w_ref[2 * pr + 1], hi, preferred_element_type=F32))
        rows.append(jnp.concatenate(tiles, axis=1))
    return jnp.concatenate(rows, axis=0)


def _gmlp_fwd_vals(u, vb, lnv_ref, ws_ref, bsp_ref, nch):
    uf = u.astype(F32)
    vf = vb.astype(F32)
    gu, tu = _gelu(uf)
    gv, tv = _gelu(vf)
    vhat, rstd = _ln(gv)
    vn = (vhat * lnv_ref[0:1, :] + lnv_ref[1:2, :]).astype(BF16)
    s = _gmlp_spatial(ws_ref, vn, nch) + jnp.concatenate([bsp_ref[...]] * nch, axis=0)
    return uf, vf, gu, tu, tv, vhat, rstd, vn, s


def _k_gmlp(u, vb, lnv, ws, bsp):
    L = u.shape[0]
    nch = min(GMLP_CHUNKS, L // BLK)
    tm = nch * BLK

    def body(u_ref, vb_ref, lnv_ref, ws_ref, bsp_ref, yb_ref):
        _, _, gu, _, _, _, _, _, s = _gmlp_fwd_vals(u_ref[...], vb_ref[...], lnv_ref, ws_ref, bsp_ref, nch)
        yb_ref[...] = (gu * s).astype(BF16)

    return pl.pallas_call(
        body, name="fwd_gmlp", grid=(L // tm,),
        in_specs=[_row(tm, GM_W), _row(tm, GM_W), _full((8, GM_W)), _full((N_GROUPS, BLK, BLK)), _full((BLK, GM_W))],
        out_specs=_row(tm, GM_W),
        out_shape=_sds((L, GM_W), BF16),
        compiler_params=_params(("arbitrary",)),
    )(u, vb, lnv, ws, bsp)


def _k_merge(x, ya, yb, ga, gb, w_a, w_b, w_o, modv, lnv, tm):
    L = x.shape[0]

    def body(x_ref, ya_ref, yb_ref, ga_ref, gb_ref, wa_ref, wb_ref, wo_ref, mod_ref, ln_ref,
             mg_ref, mix_ref, xm_ref, h2_ref):
        a = jnp.dot(ya_ref[...], wa_ref[...], preferred_element_type=F32)
        b = jnp.dot(yb_ref[...], wb_ref[...], preferred_element_type=F32)
        merged = (_sigmoid(ga_ref[...].astype(F32)) * a + _sigmoid(gb_ref[...].astype(F32)) * b).astype(BF16)
        mg_ref[...] = merged
        mix = jnp.dot(merged, wo_ref[...], preferred_element_type=F32)
        mix_ref[...] = mix.astype(BF16)
        r1 = ALPHA * x_ref[...] + mod_ref[2:3, :] * mix
        r1hat, _ = _ln(r1)
        xm = r1hat * ln_ref[0:1, :] + ln_ref[1:2, :]
        xm_ref[...] = xm
        n2, _ = _ln(xm)
        h2_ref[...] = (n2 * (1.0 + mod_ref[4:5, :]) + mod_ref[3:4, :]).astype(BF16)

    return pl.pallas_call(
        body, name="fwd_merge", grid=(L // tm,),
        in_specs=[_row(tm, D), _row(tm, Q_W), _row(tm, GM_W), _row(tm, D), _row(tm, D),
                  _resident((Q_W, D)), _resident((GM_W, D)), _resident((D, D)), _full((8, D)), _full((8, D))],
        out_specs=[_row(tm, D)] * 4,
        out_shape=[_sds((L, D), BF16), _sds((L, D), BF16), _sds((L, D), F32), _sds((L, D), BF16)],
        compiler_params=_params(("arbitrary",)),
    )(x, ya, yb, ga, gb, w_a, w_b, w_o, modv, lnv)


FFN_CH = FFN_H // 2


def _k_ffn(h2, xm, tgt, w_fi, w_fo, modv, lnv, tm):
    L = h2.shape[0]

    def body(h2_ref, xm_ref, t_ref, wi_ref, wo_ref, mod_ref, ln_ref, gate_ref, up_ref, a_ref, dr2_ref, df_ref, acc_ref):
        @pl.when(pl.program_id(0) == 0)
        def _():
            acc_ref[...] = jnp.zeros_like(acc_ref)

        h2v = h2_ref[...]
        f = jnp.zeros((tm, D), F32)
        for j in range(FFN_H // FFN_CH):
            lo = j * FFN_CH
            gate = lax.dot_general(h2v, wi_ref[lo:lo + FFN_CH, :], NT, preferred_element_type=F32)
            up = lax.dot_general(h2v, wi_ref[FFN_H + lo:FFN_H + lo + FFN_CH, :], NT, preferred_element_type=F32)
            act = (gate * _sigmoid(gate) * up).astype(BF16)
            gate_ref[:, lo:lo + FFN_CH] = gate.astype(BF16)
            up_ref[:, lo:lo + FFN_CH] = up.astype(BF16)
            a_ref[:, lo:lo + FFN_CH] = act
            f = f + jnp.dot(act, wo_ref[lo:lo + FFN_CH, :], preferred_element_type=F32)
        gate2 = mod_ref[5:6, :]
        r2 = ALPHA * xm_ref[...] + gate2 * f
        r2hat, rstd = _ln(r2)
        y = r2hat * ln_ref[2:3, :] + ln_ref[3:4, :]
        err = y - t_ref[...]
        dy = err * (1.0 / D)
        dr2 = _ln_bwd(dy * ln_ref[2:3, :], r2hat, rstd)
        dr2_ref[...] = dr2
        df_ref[...] = (gate2 * dr2).astype(BF16)
        acc_ref[0:1, :] += _colsum(dy * r2hat)
        acc_ref[1:2, :] += _colsum(dy)
        acc_ref[2:3, :] += _colsum(dr2 * f)
        acc_ref[3:4, :] += _colsum(err * err) * (0.5 / D)

    return pl.pallas_call(
        body, name="fwd_ffn", grid=(L // tm,),
        in_specs=[_row(tm, D), _row(tm, D), _row(tm, D), _resident((2 * FFN_H, D)), _resident((FFN_H, D)),
                  _full((8, D)), _full((8, D))],
        out_specs=[_row(tm, FFN_H)] * 3 + [_row(tm, D), _row(tm, D), _full((8, D))],
        out_shape=[_sds((L, FFN_H), BF16)] * 3 + [_sds((L, D), F32), _sds((L, D), BF16), _sds((8, D), F32)],
        compiler_params=_params(("arbitrary",)),
    )(h2, xm, tgt, w_fi, w_fo, modv, lnv)


FFN_CH_BWD = FFN_CH


def _k_ffn_bwd(df, gate, up, xm, dr2, x, mix, w_fi, w_fo, modv, lnv, tm):
    L = df.shape[0]

    def body(df_ref, gate_ref, up_ref, xm_ref, dr2_ref, x_ref, mix_ref, wi_ref, wo_ref, mod_ref, ln_ref,
             dF_ref, dmix_ref, dxp_ref, acc_ref):
        @pl.when(pl.program_id(0) == 0)
        def _():
            acc_ref[...] = jnp.zeros_like(acc_ref)

        dfv = df_ref[...]
        ch = FFN_CH_BWD
        chunks = [j * ch for j in range(FFN_H // ch)]
        das = [lax.dot_general(dfv, wo_ref[lo:lo + ch, :], NT, preferred_element_type=F32) for lo in chunks]
        n2, rstd2 = _ln(xm_ref[...])
        mixf = mix_ref[...].astype(F32)
        gate1 = mod_ref[2:3, :]
        r1hat, rstd1 = _ln(ALPHA * x_ref[...] + gate1 * mixf)
        dh2 = jnp.zeros((tm, D), F32)
        for lo, da in zip(chunks, das):
            gate = gate_ref[:, lo:lo + ch].astype(F32)
            upv = up_ref[:, lo:lo + ch].astype(F32)
            sg = _sigmoid(gate)
            d_gate = (da * upv * (sg * (1.0 + gate * (1.0 - sg)))).astype(BF16)
            d_up = (da * (gate * sg)).astype(BF16)
            dF_ref[:, lo:lo + ch] = d_gate
            dF_ref[:, FFN_H + lo:FFN_H + lo + ch] = d_up
            dh2 = dh2 + jnp.dot(d_gate, wi_ref[lo:lo + ch, :], preferred_element_type=F32)
            dh2 = dh2 + jnp.dot(d_up, wi_ref[FFN_H + lo:FFN_H + lo + ch, :], preferred_element_type=F32)
        acc_ref[0:1, :] += _colsum(dh2)
        acc_ref[1:2, :] += _colsum(dh2 * n2)
        dxm = ALPHA * dr2_ref[...] + _ln_bwd(dh2 * (1.0 + mod_ref[4:5, :]), n2, rstd2)
        acc_ref[2:3, :] += _colsum(dxm * r1hat)
        acc_ref[3:4, :] += _colsum(dxm)
        dr1 = _ln_bwd(dxm * ln_ref[0:1, :], r1hat, rstd1)
        dmix_ref[...] = (gate1 * dr1).astype(BF16)
        dxp_ref[...] = ALPHA * dr1
        acc_ref[4:5, :] += _colsum(dr1 * mixf)

    return pl.pallas_call(
        body, name="bwd_ffn", grid=(L // tm,),
        in_specs=[_row(tm, D), _row(tm, FFN_H), _row(tm, FFN_H), _row(tm, D), _row(tm, D), _row(tm, D), _row(tm, D),
                  _resident((2 * FFN_H, D)), _resident((FFN_H, D)), _full((8, D)), _full((8, D))],
        out_specs=[_row(tm, 2 * FFN_H), _row(tm, D), _row(tm, D), _full((8, D))],
        out_shape=[_sds((L, 2 * FFN_H), BF16), _sds((L, D), BF16), _sds((L, D), F32), _sds((8, D), F32)],
        compiler_params=_params(("arbitrary",)),
    )(df, gate, up, xm, dr2, x, mix, w_fi, w_fo, modv, lnv)


def _k_merge_bwd(dmix, merged, ya, yb, ga, gb, w_a, w_b, w_o, tm):
    L = dmix.shape[0]
    n = L // tm

    def body(dmix_ref, mg_ref, ya_ref, yb_ref, ga_ref, gb_ref, wa_ref, wb_ref, wo_ref,
             dga_ref, dgb_ref, dya_ref, dyb_ref, gwa_ref, gwb_ref, gwo_ref, acc_a, acc_b, acc_o):
        i = pl.program_id(0)

        @pl.when(i == 0)
        def _():
            for r in (acc_a, acc_b, acc_o):
                r[...] = jnp.zeros_like(r)

        dmixv = dmix_ref[...]
        dmg = lax.dot_general(dmixv, wo_ref[...], NT, preferred_element_type=F32)
        acc_o[...] += lax.dot_general(mg_ref[...], dmixv, TN, preferred_element_type=F32)
        ya = ya_ref[...]
        a = jnp.dot(ya, wa_ref[...], preferred_element_type=F32)
        sa = _sigmoid(ga_ref[...].astype(F32))
        dA = (dmg * sa).astype(BF16)
        dga_ref[...] = (dmg * a * (sa * (1.0 - sa))).astype(BF16)
        dya_ref[...] = lax.dot_general(dA, wa_ref[...], NT, preferred_element_type=F32).astype(BF16)
        acc_a[...] += lax.dot_general(ya, dA, TN, preferred_element_type=F32)
        yb = yb_ref[...]
        b = jnp.dot(yb, wb_ref[...], preferred_element_type=F32)
        sb = _sigmoid(gb_ref[...].astype(F32))
        dB = (dmg * sb).astype(BF16)
        dgb_ref[...] = (dmg * b * (sb * (1.0 - sb))).astype(BF16)
        dyb_ref[...] = lax.dot_general(dB, wb_ref[...], NT, preferred_element_type=F32).astype(BF16)
        acc_b[...] += lax.dot_general(yb, dB, TN, preferred_element_type=F32)

        @pl.when(i == n - 1)
        def _():
            gwa_ref[...] = acc_a[...].astype(BF16)
            gwb_ref[...] = acc_b[...].astype(BF16)
            gwo_ref[...] = acc_o[...].astype(BF16)

    return pl.pallas_call(
        body, name="bwd_merge", grid=(n,),
        in_specs=[_row(tm, D), _row(tm, D), _row(tm, Q_W), _row(tm, GM_W), _row(tm, D), _row(tm, D),
                  _resident((Q_W, D)), _resident((GM_W, D)), _resident((D, D))],
        out_specs=[_row(tm, D), _row(tm, D), _row(tm, Q_W), _row(tm, GM_W), _full((Q_W, D)), _full((GM_W, D)), _full((D, D))],
        out_shape=[_sds((L, D), BF16), _sds((L, D), BF16), _sds((L, Q_W), BF16), _sds((L, GM_W), BF16),
                   _sds((Q_W, D), BF16), _sds((GM_W, D), BF16), _sds((D, D), BF16)],
        scratch_shapes=[pltpu.VMEM((Q_W, D), F32), pltpu.VMEM((GM_W, D), F32), pltpu.VMEM((D, D), F32)],
        compiler_params=_params(("arbitrary",)),
    )(dmix, merged, ya, yb, ga, gb, w_a, w_b, w_o)


def _k_gmlp_bwd(u, vb, dyb, lnv, ws, wst, bsp):
    L = u.shape[0]
    nch = min(GMLP_CHUNKS, L // BLK)
    tm = nch * BLK

    def body(u_ref, vb_ref, dyb_ref, lnv_ref, ws_ref, wst_ref, bsp_ref, du_ref, dvb_ref, gws_ref, gbst_ref, gln_ref):
        @pl.when(pl.program_id(0) == 0)
        def _():
            gws_ref[...] = jnp.zeros_like(gws_ref)
            gbst_ref[...] = jnp.zeros_like(gbst_ref)
            gln_ref[...] = jnp.zeros_like(gln_ref)

        uf, vf, gu, tu, tv, vhat, rstd, vn, s = _gmlp_fwd_vals(u_ref[...], vb_ref[...], lnv_ref, ws_ref, bsp_ref, nch)
        dyb_f = dyb_ref[...].astype(F32)
        du_ref[...] = (dyb_f * s * _gelu_grad(uf, tu)).astype(BF16)
        ds = dyb_f * gu
        ds_b = ds.astype(BF16)
        for pr in range(N_GROUPS // 2):
            lanes = slice(pr * 128, (pr + 1) * 128)
            gw_lo = gw_hi = ds_sum = None
            for c in range(nch):
                rows = slice(c * BLK, (c + 1) * BLK)
                lo, hi = _split_pair(ds_b[rows, lanes])
                t_lo = lax.dot_general(lo, vn[rows, lanes], NT, preferred_element_type=F32)
                t_hi = lax.dot_general(hi, vn[rows, lanes], NT, preferred_element_type=F32)
                gw_lo = t_lo if c == 0 else gw_lo + t_lo
                gw_hi = t_hi if c == 0 else gw_hi + t_hi
                ds_sum = ds[rows, lanes] if c == 0 else ds_sum + ds[rows, lanes]
            gws_ref[2 * pr] += gw_lo
            gws_ref[2 * pr + 1] += gw_hi
            b_lo, b_hi = _split_pair(ds_sum)
            gbst_ref[:, 2 * pr:2 * pr + 1] += jnp.sum(b_lo, axis=1, keepdims=True)
            gbst_ref[:, 2 * pr + 1:2 * pr + 2] += jnp.sum(b_hi, axis=1, keepdims=True)
        dvn = _gmlp_spatial(wst_ref, ds_b, nch)
        gln_ref[0:1, :] += _colsum(dvn * vhat)
        gln_ref[1:2, :] += _colsum(dvn)
        dgv = _ln_bwd(dvn * lnv_ref[0:1, :], vhat, rstd)
        dvb_ref[...] = (dgv * _gelu_grad(vf, tv)).astype(BF16)

    return pl.pallas_call(
        body, name="bwd_gmlp", grid=(L // tm,),
        in_specs=[_row(tm, GM_W)] * 3 + [_full((8, GM_W)), _full((N_GROUPS, BLK, BLK)), _full((N_GROUPS, BLK, BLK)),
                                         _full((BLK, GM_W))],
        out_specs=[_row(tm, GM_W), _row(tm, GM_W), _full((N_GROUPS, BLK, BLK)), _full((BLK, N_GROUPS)), _full((8, GM_W))],
        out_shape=[_sds((L, GM_W), BF16), _sds((L, GM_W), BF16), _sds((N_GROUPS, BLK, BLK), F32),
                   _sds((BLK, N_GROUPS), F32), _sds((8, GM_W), F32)],
        compiler_params=_params(("arbitrary",)),
    )(u, vb, dyb, lnv, ws, wst, bsp)


ATTN_BWD_BLOCKS = 2


def _k_attn_bwd(sink, q, k, v, kc, vc, dya, lse, cos, sin, bias, comm=None):
    L = q.shape[0]
    C = kc.shape[0]
    nb = L // BLK
    nq = min(ATTN_BWD_BLOCKS, nb)
    steps = nb // nq
    NK = C + 3 * BLK
    chains = [(qb, hk) for qb in range(nq) for hk in range(N_KV_HEADS)]

    def body(sink_ref, q_ref, kp_ref, km_ref, kx_ref, vp_ref, vm_ref, vx_ref, kc_ref, vc_ref, do_ref, lse_ref,
             cq_ref, sq_ref, cl_ref, sl_ref, bias_ref,
             dq_ref, dk_ref, dv_ref, dkc_ref, dvc_ref, dsink_ref,
             dq_scr, ck_scr, cv_scr, k1_acc, k2_acc, v1_acc, v2_acc):
        i = pl.program_id(0)

        @pl.when(i == 0)
        def _():
            for r in (k1_acc, k2_acc, v1_acc, v2_acc, dkc_ref, dvc_ref, dsink_ref):
                r[...] = jnp.zeros_like(r)

        @pl.when(i < steps)
        def _():
            def band(qb):
                first = jnp.where(i == 0, 0, 1) if qb == 0 else 1
                return bias_ref[jnp.where(i == steps - 1, 2, first) if qb == nq - 1 else first]

            def lanes(hk):
                return slice(hk * HEAD_DIM, (hk + 1) * HEAD_DIM)

            def keys(ctx_ref, p_ref, m_ref, x_ref, qb, hk):
                sl = lanes(hk)
                blocks = [p_ref[:, sl]] + [m_ref[j * BLK:(j + 1) * BLK, sl] for j in range(nq)] + [x_ref[:, sl]]
                return jnp.concatenate([ctx_ref[:, sl]] + blocks[qb:qb + 3], axis=0)

            def stacked(ref, qb, hk, width):
                return jnp.concatenate(
                    [ref[qb * BLK:(qb + 1) * BLK, (hk * GQA_GROUP + g) * width:(hk * GQA_GROUP + g + 1) * width]
                     for g in range(GQA_GROUP)], axis=0)

            def scores(qb, hk):
                kcat = keys(kc_ref, kp_ref, km_ref, kx_ref, qb, hk)
                qg = stacked(q_ref, qb, hk, HEAD_DIM)
                s = _masked(lax.dot_general(qg, kcat, NT, preferred_element_type=F32), band(qb), C)
                dog = stacked(do_ref, qb, hk, HEAD_DIM)
                dp = lax.dot_general(dog, keys(vc_ref, vp_ref, vm_ref, vx_ref, qb, hk), NT, preferred_element_type=F32)
                return kcat, qg, dog, s, dp

            def softmax_bwd(qb, hk, s, dp):
                lse_c = stacked(lse_ref, qb, hk, 1)
                p = jnp.exp(s - lse_c)
                delta = jnp.sum(p * dp, axis=1, keepdims=True)
                ds = (p * (dp - delta)).astype(BF16)
                p_sink = jnp.exp(_sink_col(sink_ref, hk) - lse_c) * delta
                return p.astype(BF16), ds, p_sink

            def put_dq(qb, hk, dqs, p_sink):
                for g in range(GQA_GROUP):
                    h = hk * GQA_GROUP + g
                    dq_scr[qb * BLK:(qb + 1) * BLK, h * HEAD_DIM:(h + 1) * HEAD_DIM] = dqs[g * BLK:(g + 1) * BLK, :]
                    tot = jnp.sum(p_sink[g * BLK:(g + 1) * BLK, :], axis=0, keepdims=True)
                    dsink_ref[h:h + 1, :] -= jnp.broadcast_to(tot, (1, 128))

            ahead = 4
            sc = [scores(*c) for c in chains[:ahead]]
            pending = None
            for n, (qb, hk) in enumerate(chains):
                if n + ahead < len(chains):
                    sc.append(scores(*chains[n + ahead]))
                kcat, qg, dog, s, dp = sc[n]
                pb, ds, p_sink = softmax_bwd(qb, hk, s, dp)
                if pending is not None:
                    pqb, phk, pds, ppb, pqg, pdog = pending
                    ck_scr[pqb, :, lanes(phk)] = lax.dot_general(pds, pqg, TN, preferred_element_type=F32)
                    cv_scr[pqb, :, lanes(phk)] = lax.dot_general(ppb, pdog, TN, preferred_element_type=F32)
                put_dq(qb, hk, jnp.dot(ds, kcat, preferred_element_type=F32), p_sink)
                pending = (qb, hk, ds, pb, qg, dog)
            pqb, phk, pds, ppb, pqg, pdog = pending
            ck_scr[pqb, :, lanes(phk)] = lax.dot_general(pds, pqg, TN, preferred_element_type=F32)
            cq, sq = cq_ref[...], sq_ref[...]
            for j in range(4):
                dq_ref[:, j * 128:(j + 1) * 128] = _unrope(dq_scr[:, j * 128:(j + 1) * 128] * Q_SCALE, cq, sq).astype(BF16)
            cv_scr[pqb, :, lanes(phk)] = lax.dot_general(ppb, pdog, TN, preferred_element_type=F32)
            dkc_ref[...] += functools.reduce(lambda a, b: a + b, [ck_scr[qb, 0:C, :] for qb in range(nq)])
            dvc_ref[...] += functools.reduce(lambda a, b: a + b, [cv_scr[qb, 0:C, :] for qb in range(nq)])

        @pl.when(i >= steps)
        def _():
            ck_scr[...] = jnp.zeros_like(ck_scr)
            cv_scr[...] = jnp.zeros_like(cv_scr)

        def slot(scr, r, carried):
            parts = [scr[qb, C + (r - qb) * BLK:C + (r - qb + 1) * BLK, :] for qb in range(nq) if 0 <= r - qb <= 2]
            total = functools.reduce(lambda a, b: a + b, parts)
            return total if carried is None else carried[...] + total

        for r in range(nq):
            rows = slice(r * BLK, (r + 1) * BLK)
            carried_k, carried_v = ((k1_acc, v1_acc), (k2_acc, v2_acc), (None, None))[min(r, 2)]
            tables = (cl_ref[...], sl_ref[...]) if r == 0 else (cq_ref[(r - 1) * BLK:r * BLK, :], sq_ref[(r - 1) * BLK:r * BLK, :])
            dk_ref[rows, :] = _unrope(slot(ck_scr, r, carried_k), *tables).astype(BF16)
            dv_ref[rows, :] = slot(cv_scr, r, carried_v).astype(BF16)
        k1_acc[...] = slot(ck_scr, nq, None)
        v1_acc[...] = slot(cv_scr, nq, None)
        k2_acc[...] = slot(ck_scr, nq + 1, None)
        v2_acc[...] = slot(cv_scr, nq + 1, None)

    last = steps - 1
    kv3 = [pl.BlockSpec((BLK, KV_W), lambda i: (jnp.clip(nq * i - 1, 0, nb - 1), 0)),
           pl.BlockSpec((nq * BLK, KV_W), lambda i: (jnp.minimum(i, last), 0)),
           pl.BlockSpec((BLK, KV_W), lambda i: (jnp.minimum(nq * i + nq, nb - 1), 0))]
    cur = lambda w: pl.BlockSpec((nq * BLK, w), lambda i: (jnp.minimum(i, last), 0))
    late = lambda w: pl.BlockSpec((BLK, w), lambda i: (jnp.clip(nq * i - 1, 0, nb - 1), 0))
    out2 = lambda w: pl.BlockSpec((nq * BLK, w), lambda i: (i, 0))
    return _call(
        body, name="bwd_attn", grid=(steps + 1,),
        in_specs=[pl.BlockSpec(memory_space=pltpu.SMEM), cur(Q_W)] + kv3 + kv3
                 + [_full((C, KV_W)), _full((C, KV_W)), cur(Q_W), cur(N_Q_HEADS), cur(128), cur(128), late(128), late(128),
                    _full((3, GQA_GROUP * BLK, 3 * BLK))],
        out_specs=[cur(Q_W), out2(KV_W), out2(KV_W), _full((C, KV_W)), _full((C, KV_W)), _full((8, 128))],
        out_shape=[_sds((L, Q_W), BF16), _sds((L + nq * BLK, KV_W), BF16), _sds((L + nq * BLK, KV_W), BF16),
                   _sds((C, KV_W), F32), _sds((C, KV_W), F32), _sds((8, 128), F32)],
        scratch=[pltpu.VMEM((nq * BLK, Q_W), F32), pltpu.VMEM((nq, NK, KV_W), F32), pltpu.VMEM((nq, NK, KV_W), F32)]
                + [pltpu.VMEM((BLK, KV_W), F32)] * 4,
        args=(sink, q, k, k, k, v, v, v, kc, vc, dya, lse, cos, sin, cos, sin, bias), comm=comm)


def _k_ctx_bwd(ctx, modc, hc, dkc, dvc, w_kv):
    C = ctx.shape[0]

    def body(c_ref, mod_ref, hc_ref, dkc_ref, dvc_ref, w_ref, gw_ref, dmod_ref):
        dkv = jnp.concatenate([dkc_ref[...], dvc_ref[...]], axis=1).astype(BF16)
        gw_ref[...] = lax.dot_general(dkv, hc_ref[...], TN, preferred_element_type=F32)
        dhc = jnp.dot(dkv, w_ref[...], preferred_element_type=F32)
        n, _ = _ln(c_ref[...])
        dmod_ref[...] = jnp.zeros_like(dmod_ref)
        dmod_ref[0:1, :] = _colsum(dhc)
        dmod_ref[1:2, :] = _colsum(dhc * n)

    return pl.pallas_call(
        body, name="bwd_ctx", grid=(1,),
        in_specs=[_full((C, D)), _full((8, D)), _full((C, D)), _full((C, KV_W)), _full((C, KV_W)), _full((2 * KV_W, D))],
        out_specs=[_full((2 * KV_W, D)), _full((8, D))],
        out_shape=[_sds((2 * KV_W, D), F32), _sds((8, D), F32)],
        compiler_params=_params(("arbitrary",)),
    )(ctx, modc, hc, dkc, dvc, w_kv)


def _k_in_bwd(dq, dk, dv, du, dvb, dga, dgb, x, dxp, w_in, modv, tm, comm=None):
    L = x.shape[0]
    parts = [(O_Q, Q_W), (O_K, KV_W), (O_V, KV_W), (O_U, GM_W), (O_VB, GM_W), (O_GA, D), (O_GB, D)]

    def body(dq_ref, dk_ref, dv_ref, du_ref, dvb_ref, dga_ref, dgb_ref, x_ref, dxp_ref, w_ref, mod_ref,
             dP_ref, gx_ref, acc_ref):
        @pl.when(pl.program_id(0) == 0)
        def _():
            acc_ref[...] = jnp.zeros_like(acc_ref)

        for (lo, width), r in zip(parts, (dq_ref, dk_ref, dv_ref, du_ref, dvb_ref, dga_ref, dgb_ref)):
            dP_ref[:, lo:lo + width] = r[...]
        n1, rstd1 = _ln(x_ref[...])
        dh = jnp.dot(dP_ref[...], w_ref[...], preferred_element_type=F32)
        acc_ref[0:1, :] += _colsum(dh)
        acc_ref[1:2, :] += _colsum(dh * n1)
        gx_ref[...] = dxp_ref[...] + _ln_bwd(dh * (1.0 + mod_ref[1:2, :]), n1, rstd1)

    return _call(
        body, name="bwd_in", grid=(L // tm,),
        in_specs=[_row(tm, w) for _, w in parts] + [_row(tm, D), _row(tm, D), _resident((IN_W, D)), _full((8, D))],
        out_specs=[_row(tm, IN_W), _row(tm, D), _full((8, D))],
        out_shape=[_sds((L, IN_W), BF16), _sds((L, D), F32), _sds((8, D), F32)],
        args=(dq, dk, dv, du, dvb, dga, dgb, x, dxp, w_in, modv), comm=comm)


def _wgrad(a, b, name, tk, tt, comm=None, extra=None):
    T, K = a.shape
    N = b.shape[1]
    nt = T // tt

    def body(*refs):
        a_ref, b_ref = refs[:2]
        o_ref, acc_ref = refs[-2:]
        j, t = pl.program_id(0), pl.program_id(1)

        @pl.when(t == 0)
        def _():
            acc_ref[...] = jnp.zeros_like(acc_ref)

        acc_ref[...] += lax.dot_general(a_ref[...], b_ref[...], TN, preferred_element_type=F32)

        if extra is not None:
            lo, rows = extra[0] % tk, extra[1].shape[0]

            @pl.when((t == nt - 1) & (j == extra[0] // tk))
            def _():
                acc_ref[lo:lo + rows, :] += refs[2][...]

        @pl.when(t == nt - 1)
        def _():
            o_ref[...] = acc_ref[...].astype(BF16)

    extra_specs = [] if extra is None else [pl.BlockSpec(extra[1].shape, lambda j, t: (0, 0))]
    (out,), got = _call(
        body, name=name, grid=(K // tk, nt),
        in_specs=[pl.BlockSpec((tt, tk), lambda j, t: (t, j)), pl.BlockSpec((tt, N), lambda j, t: (t, 0))] + extra_specs,
        out_specs=[pl.BlockSpec((tk, N), lambda j, t: (j, 0))],
        out_shape=[_sds((K, N), BF16)],
        scratch=[pltpu.VMEM((tk, N), F32)],
        args=(a, b) + (() if extra is None else (extra[1],)), comm=comm)
    return (out, got) if comm is not None else out


def _adamw_reduce(parts, w, m, v, name, tr):
    R, C = w.shape
    n_parts = parts.shape[0]

    def body(p_ref, w_ref, m_ref, v_ref, g_ref, d_ref, m2_ref, v2_ref):
        g = p_ref[0].astype(F32)
        for i in range(1, n_parts):
            g = g + p_ref[i].astype(F32)
        delta, m2, v2 = _adamw(w_ref[...], g, m_ref[...], v_ref[...])
        g_ref[...] = g
        d_ref[...] = delta
        m2_ref[...] = m2
        v2_ref[...] = v2

    spec = _row(tr, C)
    return pl.pallas_call(
        body, name=name, grid=(R // tr,),
        in_specs=[pl.BlockSpec((n_parts, tr, C), lambda i: (0, i, 0)), spec, spec, spec],
        out_specs=[spec] * 4,
        out_shape=[_sds((R, C), F32)] * 4,
        compiler_params=_params(("arbitrary",)),
    )(parts, w, m, v)


SMALL_ORDER = ("b_ada", "ln1_g", "ln1_b", "ln2_g", "ln2_b", "gmlp_ln_g", "gmlp_ln_b", "b_spatial", "attn_sink")


def _small_step(gath, params):
    flat = [a for name in SMALL_ORDER for a in params[name]]

    def grad_of(tot, name):
        if name == "b_ada":
            return jnp.concatenate([tot[r:r + 1, :] for r in range(6)], axis=1)
        if name in ("ln1_g", "ln1_b", "ln2_g", "ln2_b"):
            r = 8 + ("ln1_g", "ln1_b", "ln2_g", "ln2_b").index(name)
            return tot[r:r + 1, :]
        if name == "gmlp_ln_g":
            return tot[12:13, :GM_W]
        if name == "gmlp_ln_b":
            return tot[12:13, GM_W:]
        if name == "b_spatial":
            return jnp.concatenate([tot[13:14, g * BLK:(g + 1) * BLK] for g in range(N_GROUPS)], axis=0)[None]
        return tot[14:15, :N_Q_HEADS]

    def body(*refs):
        g_ref, in_refs = refs[0], refs[1:1 + len(flat)]
        tot_ref, out_refs = refs[1 + len(flat)], refs[2 + len(flat):]
        tot = g_ref[0]
        for i in range(1, N_DEV):
            tot = tot + g_ref[i]
        tot_ref[...] = tot
        tot_ref[0:2, :] = tot[0:2, :] + tot[6:8, :]
        tot_ref[15:16, :] = jnp.broadcast_to(jnp.sum(tot[15:16, :], axis=1, keepdims=True), (1, D))
        tot = tot_ref[...]
        for k, name in enumerate(SMALL_ORDER):
            w_ref, m_ref, v_ref = in_refs[3 * k:3 * k + 3]
            g = grad_of(tot, name)
            delta, m2, v2 = _adamw(w_ref[...], g, m_ref[...], v_ref[...])
            for r, val in zip(out_refs[4 * k:4 * k + 4], (g, delta, m2, v2)):
                r[...] = val

    res = pl.pallas_call(
        body, name="small_step", grid=(1,),
        in_specs=[_full((N_DEV, 16, D))] + [_full(a.shape) for a in flat],
        out_specs=[_full((16, D))] + [_full(params[name][0].shape) for name in SMALL_ORDER for _ in range(4)],
        out_shape=[_sds((16, D), F32)] + [_sds(params[name][0].shape, F32) for name in SMALL_ORDER for _ in range(4)],
        compiler_params=_params(("arbitrary",)),
    )(gath, *flat)
    return res[0], {name: res[1 + 4 * k:5 + 4 * k] for k, name in enumerate(SMALL_ORDER)}


def _cctx_finish(gath, c_ctx, m, v):
    def body(g_ref, c_ref, m_ref, v_ref, gr_ref, d_ref, m2_ref, v2_ref):
        ds = g_ref[0]
        for i in range(1, N_DEV):
            ds = ds + g_ref[i]
        c = c_ref[...]
        sg = _sigmoid(c)
        g = ds * (sg * (1.0 + c * (1.0 - sg)))
        delta, m2, v2 = _adamw(c, g, m_ref[...], v_ref[...])
        gr_ref[...] = g
        d_ref[...] = delta
        m2_ref[...] = m2
        v2_ref[...] = v2

    return pl.pallas_call(
        body, name="cctx_finish", grid=(1,),
        in_specs=[_full((N_DEV, 8, D))] + [_full((8, D))] * 3, out_specs=[_full((8, D))] * 4,
        out_shape=[_sds((8, D), F32)] * 4,
        compiler_params=_params(("arbitrary",)),
    )(gath, c_ctx, m, v)


def _pad_rows(a, rows):
    return jnp.concatenate([a, jnp.zeros((rows - a.shape[0], a.shape[1]), a.dtype)], axis=0)


def kernel(x, c, ctx, c_ctx, w_ada, b_ada, w_in, attn_sink, gmlp_ln_g, gmlp_ln_b, w_spatial, b_spatial, w_branch_a, w_branch_b, w_out, ln1_g, ln1_b, w_ffn_in, w_ffn_out, ln2_g, ln2_b, loss_target, m_c_ctx, m_w_ada, m_b_ada, m_w_in, m_attn_sink, m_gmlp_ln_g, m_gmlp_ln_b, m_w_spatial, m_b_spatial, m_w_branch_a, m_w_branch_b, m_w_out, m_ln1_g, m_ln1_b, m_w_ffn_in, m_w_ffn_out, m_ln2_g, m_ln2_b, v_c_ctx, v_w_ada, v_b_ada, v_w_in, v_attn_sink, v_gmlp_ln_g, v_gmlp_ln_b, v_w_spatial, v_b_spatial, v_w_branch_a, v_w_branch_b, v_w_out, v_ln1_g, v_ln1_b, v_w_ffn_in, v_w_ffn_out, v_ln2_g, v_ln2_b):
    L = x.shape[1]
    me = 4 * lax.axis_index("x") + 2 * lax.axis_index("y") + lax.axis_index("c")
    x2, tgt, ctx2 = x[0], loss_target[0], ctx[0]
    tiles = _Tiles(L)
    tm_in, tm, tt = tiles.wide, tiles.narrow, tiles.tokens

    transposed = ("w_in", "w_ffn_in")
    tr = lambda kname, a: a.T if kname in transposed else a
    big = dict(w_in=w_in[0].T, w_branch_a=w_branch_a[0], w_branch_b=w_branch_b[0], w_out=w_out[0],
               w_ffn_in=w_ffn_in[0].T, w_ffn_out=w_ffn_out[0])
    col_sharded = ("w_branch_a", "w_branch_b")
    shard_bf = {k: a.astype(BF16) for k, a in big.items()}

    def assemble(kname, g):
        if kname in col_sharded:
            return g.transpose(1, 0, 2).reshape(g.shape[1], N_DEV * g.shape[2])
        return g.reshape(N_DEV * g.shape[1], g.shape[2])

    def to_blocks(kname, g):
        if kname in col_sharded:
            return g.reshape(g.shape[0], N_DEV, g.shape[1] // N_DEV).transpose(1, 0, 2)
        return g.reshape(N_DEV, g.shape[0] // N_DEV, g.shape[1])

    full = {}
    n_ada = w_ada.shape[2]
    b_my = lax.dynamic_slice(b_ada, (0, me * n_ada), (1, n_ada))
    act, mod_all, got = _prologue(_pad_rows(c, 8), _pad_rows(c_ctx[None, :], 8), w_ada[0], b_my,
                                  _Comm(gather=[shard_bf["w_in"]]))
    full["w_in"] = assemble("w_in", got[0])
    mod_all = mod_all.transpose(1, 0, 2).reshape(16, 6 * D)
    modv = _pad_rows(lax.dynamic_slice(mod_all, (me, 0), (1, 6 * D)).reshape(6, D), 8)
    modc = _pad_rows(mod_all[8].reshape(6, D), 8)

    lnv = _pad_rows(jnp.concatenate([ln1_g, ln1_b, ln2_g, ln2_b], axis=0), 8)
    gm_lnv = _pad_rows(jnp.concatenate([gmlp_ln_g, gmlp_ln_b], axis=0), 8)
    ws_b = w_spatial[0].astype(BF16)
    wst_b = ws_b.transpose(0, 2, 1)
    bsp = jnp.repeat(b_spatial[0].T, GROUP_DIM, axis=1)
    sink = attn_sink[0]
    cos, sin = _rope_tables(L)
    bias = _attn_bias()
    w_kv = full["w_in"][O_K:O_K + 2 * KV_W, :]

    (h, q, k, v, u, vb, ga, gb), got = _k_in(
        x2, modv, full["w_in"], cos, sin, tm_in,
        comm=_Comm(gather=[shard_bf[kname] for kname in ("w_branch_a", "w_branch_b", "w_out", "w_ffn_out")]))
    for kname, g in zip(("w_branch_a", "w_branch_b", "w_out", "w_ffn_out"), got):
        full[kname] = assemble(kname, g)
    hc, kc, vc = _k_ctx(ctx2, modc, w_kv)
    (ya, lse), got = _k_attn(sink, q, k, v, kc, vc, bias, comm=_Comm(gather=[shard_bf["w_ffn_in"]]))
    full["w_ffn_in"] = assemble("w_ffn_in", got[0])
    yb = _k_gmlp(u, vb, gm_lnv, ws_b, bsp)
    merged, mix, xm, h2 = _k_merge(x2, ya, yb, ga, gb, full["w_branch_a"], full["w_branch_b"], full["w_out"], modv, lnv, tm_in)
    gate, up, act_f, dr2, df, acc_f = _k_ffn(h2, xm, tgt, full["w_ffn_in"], full["w_ffn_out"], modv, lnv, tm_in)

    dF, dmix, dxp, acc_b = _k_ffn_bwd(df, gate, up, xm, dr2, x2, mix, full["w_ffn_in"], full["w_ffn_out"], modv, lnv, tm)
    blk_fo = to_blocks("w_ffn_out", _wgrad(act_f, df, "wgrad_ffn_out", tiles.tk_ffn, tt))
    gw_fi, (rcv_fo,) = _wgrad(dF, h2, "wgrad_ffn_in", tiles.tk_ffn, tt, comm=_Comm(scatter=[blk_fo]))
    blk_fi = to_blocks("w_ffn_in", gw_fi)
    dga, dgb, dya, dyb, gw_a, gw_b, gw_o = _k_merge_bwd(
        dmix, merged, ya, yb, ga, gb, full["w_branch_a"], full["w_branch_b"], full["w_out"], tm_in)
    du, dvb, g_ws, g_bst, g_gln = _k_gmlp_bwd(u, vb, dyb, gm_lnv, ws_b, wst_b, bsp)
    (dq, dk_late, dv_late, dkc, dvc, g_sink), (gath_ws, rcv_fi) = _k_attn_bwd(
        sink, q, k, v, kc, vc, dya, lse, cos, sin, bias,
        comm=_Comm(gather=[g_ws.reshape(N_GROUPS * BLK, BLK)], scatter=[blk_fi]))
    dk, dv = dk_late[BLK:BLK + L], dv_late[BLK:BLK + L]
    blk_a, blk_b, blk_o = to_blocks("w_branch_a", gw_a), to_blocks("w_branch_b", gw_b), to_blocks("w_out", gw_o)
    (dP, grad_x, acc_i), _ = _k_in_bwd(dq, dk, dv, du, dvb, dga, dgb, x2, dxp, full["w_in"], modv, tm_in)
    g_ctx, dmodc = _k_ctx_bwd(ctx2, modc, hc, dkc, dvc, w_kv)
    gw_in, (rcv_a, rcv_b, rcv_o) = _wgrad(dP, h, "wgrad_in", tiles.tk_in, tt, comm=_Comm(scatter=[blk_a, blk_b, blk_o]),
                                          extra=(O_K, g_ctx))

    dmod_x = jnp.concatenate([acc_i[0:2], acc_b[4:5], acc_b[0:2], acc_f[2:3]], axis=0)
    small = jnp.concatenate([
        dmod_x, dmodc[0:2], acc_b[2:4], acc_f[0:2],
        jnp.concatenate([g_gln[0:1], g_gln[1:2]], axis=1), g_bst.T.reshape(1, D),
        _pad_rows(g_sink[:, 0:1], D).T, acc_f[3:4]], axis=0)
    rcv_in, gath = _exchange_two_level(to_blocks("w_in", gw_in), small, "exchange_last")
    received = dict(w_in=rcv_in, w_branch_a=rcv_a, w_branch_b=rcv_b, w_out=rcv_o, w_ffn_in=rcv_fi, w_ffn_out=rcv_fo)
    moments = dict(w_in=(m_w_in, v_w_in), w_branch_a=(m_w_branch_a, v_w_branch_a), w_branch_b=(m_w_branch_b, v_w_branch_b),
                   w_out=(m_w_out, v_w_out), w_ffn_in=(m_w_ffn_in, v_w_ffn_in), w_ffn_out=(m_w_ffn_out, v_w_ffn_out))
    names = list(big)
    res = {}
    for kname in names:
        mm, vv = moments[kname]
        R = big[kname].shape[0]
        res[kname] = [tr(kname, r) for r in _adamw_reduce(
            received[kname], big[kname], tr(kname, mm[0]), tr(kname, vv[0]), "adamw_" + kname, 256 if R % 256 == 0 else R // 2)]

    ws2d = lambda a: a.reshape(N_GROUPS * BLK, BLK)
    res_ws = [r.reshape(w_spatial.shape) for r in _adamw_reduce(
        gath_ws, ws2d(w_spatial), ws2d(m_w_spatial), ws2d(v_w_spatial), "adamw_w_spatial", 256)]
    tot, res_small = _small_step(gath, dict(
        b_ada=(b_ada, m_b_ada, v_b_ada), ln1_g=(ln1_g, m_ln1_g, v_ln1_g), ln1_b=(ln1_b, m_ln1_b, v_ln1_b),
        ln2_g=(ln2_g, m_ln2_g, v_ln2_g), ln2_b=(ln2_b, m_ln2_b, v_ln2_b),
        gmlp_ln_g=(gmlp_ln_g, m_gmlp_ln_g, v_gmlp_ln_g), gmlp_ln_b=(gmlp_ln_b, m_gmlp_ln_b, v_gmlp_ln_b),
        b_spatial=(b_spatial, m_b_spatial, v_b_spatial), attn_sink=(attn_sink, m_attn_sink, v_attn_sink)))
    loss = tot[15, 0]

    dmod_rows = jnp.concatenate([gath[:, 0:6, :].reshape(N_DEV, 6 * D),
                                 jnp.concatenate([tot[6:8].reshape(1, 2 * D), jnp.zeros((1, 4 * D), F32)], axis=1),
                                 jnp.zeros((7, 6 * D), F32)], axis=0)
    dmod_my = lax.dynamic_slice(dmod_rows, (0, me * n_ada), (16, n_ada))
    g_wada, d_wada, m2_wada, v2_wada, pc = _ada_bwd(act, dmod_my, w_ada[0], m_w_ada[0], v_w_ada[0])
    pc_all = _gather_rows(pc, "gather_cctx")
    cc8 = lambda a: _pad_rows(a.reshape(1, D), 8)
    g_cc, d_cc, m2_cc, v2_cc = _cctx_finish(pc_all, cc8(c_ctx), cc8(m_c_ctx), cc8(v_c_ctx))

    order = ["c_ctx", "w_ada", "b_ada", "w_in", "attn_sink", "gmlp_ln_g", "gmlp_ln_b", "w_spatial", "b_spatial",
             "w_branch_a", "w_branch_b", "w_out", "ln1_g", "ln1_b", "w_ffn_in", "w_ffn_out", "ln2_g", "ln2_b"]
    grads, deltas, new_m, new_v = {}, {}, {}, {}
    grads["c_ctx"], deltas["c_ctx"], new_m["c_ctx"], new_v["c_ctx"] = g_cc[0], d_cc[0], m2_cc[0], v2_cc[0]
    grads["w_ada"], deltas["w_ada"], new_m["w_ada"], new_v["w_ada"] = g_wada[None], d_wada[None], m2_wada[None], v2_wada[None]
    for kname in names:
        g, d, m2, v2 = res[kname]
        grads[kname], deltas[kname], new_m[kname], new_v[kname] = g[None], d[None], m2[None], v2[None]
    grads["w_spatial"], deltas["w_spatial"], new_m["w_spatial"], new_v["w_spatial"] = res_ws
    for kname in SMALL_ORDER:
        grads[kname], deltas[kname], new_m[kname], new_v[kname] = res_small[kname]
    return (loss, grad_x[None], *[grads[n] for n in order], *[deltas[n] for n in order],
            *[new_m[n] for n in order], *[new_v[n] for n in order])
```

```python
import functools
import math

import jax
import jax.numpy as jnp
import numpy as np
from jax import lax
from jax.experimental import pallas as pl
from jax.experimental.pallas import tpu as pltpu

F32 = jnp.float32
BF16 = jnp.bfloat16
MESH = pl.DeviceIdType.MESH

N_DEV = 8
D = 1024
HEAD_DIM = 64
N_Q_HEADS = 8
N_KV_HEADS = 2
GQA_GROUP = 4
BLK = 128
Q_W = 512
KV_W = 128
GM_W = 512
N_GROUPS = 8
GROUP_DIM = 64
FFN_H = 2816
IN_W = 3840
O_Q, O_K, O_V, O_U, O_VB, O_GA, O_GB = 0, 512, 640, 768, 1280, 1792, 2816
LN_EPS = 1e-5
NEG_INF = -1e30
ALPHA = 2.0 ** 0.25
ROPE_BASE = 10000.0
ROPE_PAIRS = 16
Q_SCALE = HEAD_DIM ** -0.5
GELU_K0 = math.sqrt(2.0 / math.pi)
GELU_K1 = 0.044715

ADAM_LR = 0.001
ADAM_B1 = 0.9
ADAM_B2 = 0.999
ADAM_EPS = 1e-08
ADAM_WD = 0.01
ADAM_STEP = 10

V7X_VMEM_BYTES = 64 * 1024 * 1024
VMEM_LIMIT = V7X_VMEM_BYTES * 7 // 8
NT = (((1,), (1,)), ((), ()))
TN = (((0,), (0,)), ((), ()))


class _Tiles:
    def __init__(self, L):
        self.wide = min(512, L)
        self.narrow = min(256, L)
        self.tokens = min(2048, L)
        self.tk_in = IN_W // 3
        self.tk_ffn = FFN_H // 2


def _params(sem=None):
    return pltpu.CompilerParams(dimension_semantics=sem, vmem_limit_bytes=VMEM_LIMIT)


def _row(tm, w):
    return pl.BlockSpec((tm, w), lambda i: (i, 0))


def _full(shape):
    nd = len(shape)
    return pl.BlockSpec(shape, lambda i: (0,) * nd)


def _resident(shape):
    nd = len(shape)
    return pl.BlockSpec(shape, lambda i: (0,) * nd, pipeline_mode=pl.Buffered(1))


def _sds(shape, dt):
    return jax.ShapeDtypeStruct(shape, dt)


def _ln(xf):
    mu = jnp.mean(xf, axis=-1, keepdims=True)
    xc = xf - mu
    var = jnp.mean(xc * xc, axis=-1, keepdims=True)
    rstd = lax.rsqrt(var + LN_EPS)
    return xc * rstd, rstd


def _ln_bwd(dn, n, rstd):
    m1 = jnp.mean(dn, axis=-1, keepdims=True)
    m2 = jnp.mean(dn * n, axis=-1, keepdims=True)
    return rstd * (dn - m1 - n * m2)


def _colsum(t):
    return jnp.sum(t, axis=0, keepdims=True)


def _sigmoid(x):
    return 0.5 * jnp.tanh(0.5 * x) + 0.5


def _gelu(x):
    t = jnp.tanh(GELU_K0 * (x + GELU_K1 * (x * x * x)))
    return x * (0.5 * (1.0 + t)), t


def _gelu_grad(x, t):
    return 0.5 * (1.0 + t) + 0.5 * x * (1.0 - t * t) * (GELU_K0 * (1.0 + 3.0 * GELU_K1 * x * x))


def _swap16(t):
    lane = lax.broadcasted_iota(jnp.int32, t.shape, 1)
    return jnp.where((lane & 16) == 0, pltpu.roll(t, 112, 1), pltpu.roll(t, 16, 1))


def _rope(t, cos, sin):
    return t * cos + _swap16(t) * sin


def _unrope(t, cos, sin):
    return t * cos - _swap16(t) * sin


def _adamw(w, g, m, v):
    m2 = ADAM_B1 * m + (1.0 - ADAM_B1) * g
    v2 = ADAM_B2 * v + (1.0 - ADAM_B2) * (g * g)
    m_hat = m2 / (1.0 - ADAM_B1 ** ADAM_STEP)
    v_hat = v2 / (1.0 - ADAM_B2 ** ADAM_STEP)
    delta = -ADAM_LR * (m_hat / (jnp.sqrt(v_hat) + ADAM_EPS) + ADAM_WD * w)
    return delta, m2, v2


def _rope_tables(L):
    inv = (np.float32(ROPE_BASE) ** (-np.arange(ROPE_PAIRS, dtype=np.float32) / np.float32(ROPE_PAIRS))).astype(np.float32)
    t = np.arange(L, dtype=np.int32)
    rows = (t // 64).astype(np.float32)[:, None] * inv
    cols = (t % 64).astype(np.float32)[:, None] * inv
    cr, sr, cc, sc = np.cos(rows), np.sin(rows), np.cos(cols), np.sin(cols)
    cos = np.concatenate([cr, cr, cc, cc], axis=1)
    sin = np.concatenate([-sr, sr, -sc, sc], axis=1)
    return jnp.asarray(np.tile(cos, (1, 2)), F32), jnp.asarray(np.tile(sin, (1, 2)), F32)


def _me():
    return lax.axis_index("x"), lax.axis_index("y"), lax.axis_index("c")


def _peer(mx, my, mc, k):
    return (mx ^ ((k >> 2) & 1), my ^ ((k >> 1) & 1), mc ^ (k & 1))


class _Comm:
    def __init__(self, gather=(), scatter=(), spread=()):
        self.kinds = ["gather"] * len(gather) + ["scatter"] * len(scatter) + ["spread"] * len(spread)
        self.args = list(gather) + list(scatter) + list(spread)
        self.n = len(self.args)

    def out_shape(self):
        return [_sds(a.shape if k == "scatter" else (N_DEV,) + a.shape, a.dtype) for k, a in zip(self.kinds, self.args)]

    def specs(self):
        return [pl.BlockSpec(memory_space=pl.ANY)] * self.n

    def scratch(self):
        return [pltpu.SemaphoreType.DMA((7 * self.n,)), pltpu.SemaphoreType.DMA((7 * self.n,)),
                pltpu.SemaphoreType.DMA((self.n,))]

    def _plan(self, x_refs, out_refs, send_sems, recv_sems, local_sems):
        mx, my, mc = _me()
        me = 4 * mx + 2 * my + mc
        here, sibling = (mx, my, mc), (mx, my, 1 - mc)
        chips = [(1 - mx, my), (mx, 1 - my), (1 - mx, 1 - my)]
        local, first, last = [], [], []
        relay = [[], [], []]
        for a, kind in enumerate(self.kinds):
            x, out = x_refs[a], out_refs[a]

            def rc(k, src, dst, to):
                return pltpu.make_async_remote_copy(
                    src_ref=src, dst_ref=dst, send_sem=send_sems.at[7 * a + k], recv_sem=recv_sems.at[7 * a + k],
                    device_id=to, device_id_type=MESH)

            if kind == "gather":
                local.append(pltpu.make_async_copy(x, out.at[me], local_sems.at[a]))
                first.append(rc(0, x, out.at[me], sibling))
                last.append(rc(0, x, out.at[me ^ 1], here))
                for j, (cx, cy) in enumerate(chips):
                    first.append(rc(1 + j, x, out.at[me], (cx, cy, mc)))
                    landed = out.at[4 * cx + 2 * cy + mc]
                    relay[j].append((rc(1 + j, x, landed, here), rc(4 + j, landed, landed, sibling)))
                    last.append(rc(4 + j, x, out.at[4 * cx + 2 * cy + 1 - mc], here))
            else:
                own = x.at[me] if kind == "scatter" else x
                local.append(pltpu.make_async_copy(own, out.at[me], local_sems.at[a]))
                for k in range(1, N_DEV):
                    src = x.at[me ^ k] if kind == "scatter" else x
                    first.append(rc(k - 1, src, out.at[me], _peer(mx, my, mc, k)))
                    last.append(rc(k - 1, own, out.at[me ^ k], here))
        return local, first, relay[0] + relay[1] + relay[2], last

    def start(self, *refs):
        local, first, _, _ = self._plan(*refs)
        for cp in local + first:
            cp.start()

    def relay(self, *refs):
        _, _, relay, _ = self._plan(*refs)
        for arrival, onward in relay:
            arrival.wait_recv()
            onward.start()

    def finish(self, *refs):
        local, first, relay, last = self._plan(*refs)
        for cp in last:
            cp.wait_recv()
        for cp in first:
            cp.wait_send()
        for _, onward in relay:
            onward.wait_send()
        for cp in local:
            cp.wait()


def _call(body, *, name, grid, in_specs, out_specs, out_shape, args, scratch=(), comm=None, aliases=None):
    params = _params(("arbitrary",) * len(grid))
    total = math.prod(grid)

    def at(step):
        flat = functools.reduce(lambda acc, dn: acc * dn[1] + pl.program_id(dn[0]), enumerate(grid), 0)
        return flat == step

    if comm is None:
        res = pl.pallas_call(
            body, name=name, grid=grid, in_specs=list(in_specs), out_specs=list(out_specs), out_shape=list(out_shape),
            scratch_shapes=list(scratch), input_output_aliases=aliases or {}, compiler_params=params)(*args)
        return list(res), []
    n_in, n_out, n_scr, cn = len(in_specs), len(out_specs), len(scratch), comm.n

    def hosted(*refs):
        ins, refs = refs[:n_in], refs[n_in:]
        cins, refs = refs[:cn], refs[cn:]
        outs, refs = refs[:n_out], refs[n_out:]
        couts, refs = refs[:cn], refs[cn:]
        scr, sems = refs[:n_scr], refs[n_scr:]

        @pl.when(at(0))
        def _():
            comm.start(cins, couts, *sems)

        body(*ins, *outs, *scr)

        @pl.when(at((3 * total) // 4 if total >= 4 else total - 1))
        def _():
            comm.relay(cins, couts, *sems)

        @pl.when(at(total - 1))
        def _():
            comm.finish(cins, couts, *sems)

    res = pl.pallas_call(
        hosted, name=name, grid=grid, in_specs=list(in_specs) + comm.specs(), out_specs=list(out_specs) + comm.specs(),
        out_shape=list(out_shape) + comm.out_shape(), scratch_shapes=list(scratch) + comm.scratch(),
        input_output_aliases=aliases or {}, compiler_params=params)(*args, *comm.args)
    return list(res[:n_out]), list(res[n_out:])


def _exchange_two_level(blk, small, name):
    _, R, C = blk.shape
    rows = small.shape[0]

    def body(blk_ref, small_ref, stage_ref, out_ref, gath_ref, a_scr, b_scr, t_scr, s1, r1, s3, r3, ss, rs, lsem):
        mx, my, mc = _me()
        me = 4 * mx + 2 * my + mc
        mine = 2 * mx + my
        here, sibling = (mx, my, mc), (mx, my, 1 - mc)

        def rc(src, dst, send, recv, to):
            return pltpu.make_async_remote_copy(src_ref=src, dst_ref=dst, send_sem=send, recv_sem=recv,
                                                device_id=to, device_id_type=MESH)

        own_small = pltpu.make_async_copy(small_ref, gath_ref.at[me], lsem.at[0])
        own_small.start()
        spread = [rc(small_ref, gath_ref.at[me], ss.at[k - 1], rs.at[k - 1], _peer(mx, my, mc, k)) for k in range(1, N_DEV)]
        to_sib = [rc(blk_ref.at[2 * p + 1 - mc], stage_ref.at[p], s1.at[p], r1.at[p], sibling) for p in range(4)]
        for cp in spread + to_sib:
            cp.start()
        own = [pltpu.make_async_copy(blk_ref.at[2 * p + mc], a_scr.at[p], lsem.at[1 + p]) for p in range(4)]
        for cp in own:
            cp.start()
        from_sib = []
        for p in range(4):
            rc(blk_ref.at[2 * p + 1 - mc], stage_ref.at[p], s1.at[p], r1.at[p], here).wait_recv()
            cp = pltpu.make_async_copy(stage_ref.at[p], b_scr.at[p], lsem.at[5 + p])
            cp.start()
            from_sib.append(cp)
        for cp in own + from_sib:
            cp.wait()
        t_scr[...] = (a_scr[...].astype(F32) + b_scr[...].astype(F32)).astype(BF16)
        keep = pltpu.make_async_copy(t_scr.at[mine], out_ref.at[mine], lsem.at[9])
        keep.start()
        onward = [rc(t_scr.at[mine ^ k], out_ref.at[mine], s3.at[k - 1], r3.at[k - 1], (mx ^ (k >> 1), my ^ (k & 1), mc))
                  for k in range(1, 4)]
        for cp in onward:
            cp.start()
        for k in range(1, 4):
            rc(t_scr.at[mine], out_ref.at[mine ^ k], s3.at[k - 1], r3.at[k - 1], here).wait_recv()
        for k in range(1, N_DEV):
            rc(small_ref, gath_ref.at[me ^ k], ss.at[k - 1], rs.at[k - 1], here).wait_recv()
        for cp in spread + to_sib + onward:
            cp.wait_send()
        keep.wait()
        own_small.wait()

    any_spec = pl.BlockSpec(memory_space=pl.ANY)
    dma = pltpu.SemaphoreType.DMA
    _, out, gath = pl.pallas_call(
        body, name=name,
        in_specs=[any_spec, any_spec], out_specs=[any_spec] * 3,
        out_shape=[_sds((4, R, C), BF16), _sds((4, R, C), BF16), _sds((N_DEV, rows, D), F32)],
        scratch_shapes=[pltpu.VMEM((4, R, C), BF16)] * 3
                       + [dma((4,)), dma((4,)), dma((3,)), dma((3,)), dma((N_DEV - 1,)), dma((N_DEV - 1,)), dma((10,))],
        compiler_params=pltpu.CompilerParams(vmem_limit_bytes=VMEM_LIMIT),
    )(blk, small)
    return out, gath


def _exchange_rows(x_ref, out_ref, send_sems, recv_sems, between=None):
    mx, my, mc = _me()
    me = 4 * mx + 2 * my + mc
    out_ref[pl.ds(me, 1)] = x_ref[...][None]
    sends = []
    for k in range(1, N_DEV):
        cp = pltpu.make_async_remote_copy(
            src_ref=x_ref, dst_ref=out_ref.at[me], send_sem=send_sems.at[k - 1], recv_sem=recv_sems.at[k - 1],
            device_id=_peer(mx, my, mc, k), device_id_type=MESH)
        cp.start()
        sends.append(cp)
    if between is not None:
        between()
    for k in range(1, N_DEV):
        pltpu.make_async_remote_copy(
            src_ref=x_ref, dst_ref=out_ref.at[me ^ k], send_sem=send_sems.at[k - 1], recv_sem=recv_sems.at[k - 1],
            device_id=(mx, my, mc), device_id_type=MESH).wait_recv()
    for cp in sends:
        cp.wait_send()


def _prologue(c8, cctx8, w_ada, b_my, comm):
    nw = w_ada.shape[1]

    cn = comm.n

    def body(*refs):
        c_ref, cctx_ref, w_ref, b_ref = refs[:4]
        cins, refs = refs[4:4 + cn], refs[4 + cn:]
        act_ref, mod_ref = refs[:2]
        couts, refs = refs[2:2 + cn], refs[2 + cn:]
        cmine_scr, call_scr, mine_scr, mall_scr, s1, r1, s2, r2 = refs[:8]
        csems = refs[8:]
        cmine_scr[...] = c_ref[...]
        _exchange_rows(cmine_scr, call_scr, s1, r1)
        rows = [call_scr[d][0:1, :] for d in range(N_DEV)] + [cctx_ref[0:1, :], jnp.zeros((7, D), F32)]
        s = jnp.concatenate(rows, axis=0)
        act = s * _sigmoid(s)
        act_ref[...] = act
        mine_scr[...] = jnp.dot(act.astype(BF16), w_ref[...].astype(BF16), preferred_element_type=F32) + b_ref[...]
        _exchange_rows(mine_scr, mall_scr, s2, r2, between=lambda: comm.start(cins, couts, *csems))
        mod_ref[...] = mall_scr[...]
        comm.relay(cins, couts, *csems)
        comm.finish(cins, couts, *csems)

    sems = [pltpu.SemaphoreType.DMA((N_DEV - 1,))] * 4
    res = pl.pallas_call(
        body, name="prologue", grid=(1,),
        in_specs=[_full((8, D)), _full((8, D)), _full((D, nw)), _full((1, nw))] + comm.specs(),
        out_specs=[_full((16, D)), _full((N_DEV, 16, nw))] + comm.specs(),
        out_shape=[_sds((16, D), F32), _sds((N_DEV, 16, nw), F32)] + comm.out_shape(),
        scratch_shapes=[pltpu.VMEM((8, D), F32), pltpu.VMEM((N_DEV, 8, D), F32), pltpu.VMEM((16, nw), F32),
                        pltpu.VMEM((N_DEV, 16, nw), F32)] + sems + comm.scratch(),
        compiler_params=_params(("arbitrary",)),
    )(c8, cctx8, w_ada, b_my, *comm.args)
    return res[0], res[1], list(res[2:])


def _gather_rows(x, name):
    def body(x_ref, out_ref, send_sems, recv_sems):
        _exchange_rows(x_ref, out_ref, send_sems, recv_sems)

    return pl.pallas_call(
        body, name=name,
        out_shape=_sds((N_DEV,) + x.shape, x.dtype),
        in_specs=[pl.BlockSpec(memory_space=pltpu.VMEM)],
        out_specs=pl.BlockSpec(memory_space=pltpu.VMEM),
        scratch_shapes=[pltpu.SemaphoreType.DMA((N_DEV - 1,)), pltpu.SemaphoreType.DMA((N_DEV - 1,))],
        compiler_params=pltpu.CompilerParams(vmem_limit_bytes=VMEM_LIMIT),
    )(x)


def _ada_bwd(act, dmod_my, w_ada, m, v, tr=256):
    nw = w_ada.shape[1]

    def body(act_ref, dm_ref, w_ref, m_ref, v_ref, g_ref, d_ref, m2_ref, v2_ref, pc_ref):
        dm = dm_ref[...].astype(BF16)
        g = lax.dot_general(act_ref[...].astype(BF16), dm, TN, preferred_element_type=F32)
        w = w_ref[...]
        delta, m2, v2 = _adamw(w, g, m_ref[...], v_ref[...])
        g_ref[...] = g
        d_ref[...] = delta
        m2_ref[...] = m2
        v2_ref[...] = v2
        pc_ref[...] = lax.dot_general(dm[8:16, :], w.astype(BF16), NT, preferred_element_type=F32)

    wspec = _row(tr, nw)
    return pl.pallas_call(
        body, name="ada_bwd", grid=(D // tr,),
        in_specs=[pl.BlockSpec((16, tr), lambda i: (0, i)), _full((16, nw)), wspec, wspec, wspec],
        out_specs=[wspec, wspec, wspec, wspec, pl.BlockSpec((8, tr), lambda i: (0, i))],
        out_shape=[_sds((D, nw), F32)] * 4 + [_sds((8, D), F32)],
        compiler_params=_params(("arbitrary",)),
    )(act, dmod_my, w_ada, m, v)


def _k_in(x, modv, w_in, cos, sin, tm, comm=None):
    L = x.shape[0]

    def body(x_ref, mod_ref, w_ref, cos_ref, sin_ref, h_ref, q_ref, k_ref, v_ref, u_ref, vb_ref, ga_ref, gb_ref):
        n, _ = _ln(x_ref[...])
        h = (n * (1.0 + mod_ref[1:2, :]) + mod_ref[0:1, :]).astype(BF16)
        h_ref[...] = h
        c, s = cos_ref[...], sin_ref[...]

        def proj(lo, width):
            return lax.dot_general(h, w_ref[lo:lo + width, :], NT, preferred_element_type=F32)

        for i in range(4):
            q_ref[:, i * 128:(i + 1) * 128] = (_rope(proj(O_Q + i * 128, 128), c, s) * Q_SCALE).astype(BF16)
        k_ref[...] = _rope(proj(O_K, KV_W), c, s).astype(BF16)
        v_ref[...] = proj(O_V, KV_W).astype(BF16)
        u_ref[...] = proj(O_U, GM_W).astype(BF16)
        vb_ref[...] = proj(O_VB, GM_W).astype(BF16)
        ga_ref[...] = proj(O_GA, D).astype(BF16)
        gb_ref[...] = proj(O_GB, D).astype(BF16)

    widths = [D, Q_W, KV_W, KV_W, GM_W, GM_W, D, D]
    return _call(
        body, name="fwd_in", grid=(L // tm,),
        in_specs=[_row(tm, D), _full((8, D)), _resident((IN_W, D)), _row(tm, 128), _row(tm, 128)],
        out_specs=[_row(tm, w) for w in widths],
        out_shape=[_sds((L, w), BF16) for w in widths],
        args=(x, modv, w_in, cos, sin), comm=comm)


def _k_ctx(ctx, modc, w_kv):
    C = ctx.shape[0]

    def body(c_ref, mod_ref, w_ref, hc_ref, kc_ref, vc_ref):
        n, _ = _ln(c_ref[...])
        hc = (n * (1.0 + mod_ref[1:2, :]) + mod_ref[0:1, :]).astype(BF16)
        hc_ref[...] = hc
        kv = lax.dot_general(hc, w_ref[...], NT, preferred_element_type=F32)
        kc_ref[...] = kv[:, :KV_W].astype(BF16)
        vc_ref[...] = kv[:, KV_W:].astype(BF16)

    return pl.pallas_call(
        body, name="fwd_ctx", grid=(1,),
        in_specs=[_full((C, D)), _full((8, D)), _full((2 * KV_W, D))],
        out_specs=[_full((C, D)), _full((C, KV_W)), _full((C, KV_W))],
        out_shape=[_sds((C, D), BF16), _sds((C, KV_W), BF16), _sds((C, KV_W), BF16)],
        compiler_params=_params(("arbitrary",)),
    )(ctx, modc, w_kv)


def _attn_bias():
    r = (np.arange(GQA_GROUP * BLK) & (BLK - 1))[:, None]
    j = np.arange(3 * BLK)[None, :]
    band = np.abs(j - BLK - r) <= BLK
    variants = [band & (j >= BLK), band, band & (j < 2 * BLK)]
    return jnp.asarray(np.stack([np.where(v, 0.0, NEG_INF) for v in variants]), F32)


def _masked(s, bias, C):
    return jnp.concatenate([s[:, :C], s[:, C:] + bias], axis=1)


def _sink_col(sink_ref, hk):
    grp = lax.broadcasted_iota(jnp.int32, (GQA_GROUP * BLK, 1), 0) >> 7
    col = jnp.full((GQA_GROUP * BLK, 1), sink_ref[hk * GQA_GROUP], F32)
    for g in range(1, GQA_GROUP):
        col = jnp.where(grp == g, sink_ref[hk * GQA_GROUP + g], col)
    return col


ATTN_FWD_BLOCKS = 4


def _k_attn(sink, q, k, v, kc, vc, bias, comm=None):
    L = q.shape[0]
    C = kc.shape[0]
    nb = L // BLK
    nq = min(ATTN_FWD_BLOCKS, nb)
    steps = nb // nq

    def body(sink_ref, q_ref, kp_ref, km_ref, kx_ref, vp_ref, vm_ref, vx_ref, kc_ref, vc_ref, bias_ref, ya_ref, lse_ref):
        i = pl.program_id(0)
        chains = [(qb, hk) for qb in range(nq) for hk in range(N_KV_HEADS)]

        def band(qb):
            first = jnp.where(i == 0, 0, 1) if qb == 0 else 1
            return bias_ref[jnp.where(i == steps - 1, 2, first) if qb == nq - 1 else first]

        def keys(ctx_ref, p_ref, m_ref, x_ref, qb, hk):
            sl = slice(hk * HEAD_DIM, (hk + 1) * HEAD_DIM)
            blocks = [p_ref[:, sl]] + [m_ref[j * BLK:(j + 1) * BLK, sl] for j in range(nq)] + [x_ref[:, sl]]
            return jnp.concatenate([ctx_ref[:, sl]] + blocks[qb:qb + 3], axis=0)

        def queries(qb, hk):
            return jnp.concatenate(
                [q_ref[qb * BLK:(qb + 1) * BLK, (hk * GQA_GROUP + g) * HEAD_DIM:(hk * GQA_GROUP + g + 1) * HEAD_DIM]
                 for g in range(GQA_GROUP)], axis=0)

        def scores(qb, hk):
            return _masked(lax.dot_general(queries(qb, hk), keys(kc_ref, kp_ref, km_ref, kx_ref, qb, hk), NT,
                                           preferred_element_type=F32), band(qb), C)

        ahead = 2
        s = [scores(*c) for c in chains[:ahead]]
        for n, (qb, hk) in enumerate(chains):
            if n + ahead < len(chains):
                s.append(scores(*chains[n + ahead]))
            s_ = s[n]
            sink_c = _sink_col(sink_ref, hk)
            m = jnp.maximum(jnp.max(s_, axis=1, keepdims=True), sink_c)
            p = jnp.exp(s_ - m)
            den = jnp.sum(p, axis=1, keepdims=True) + jnp.exp(sink_c - m)
            o = jnp.dot(p.astype(BF16), keys(vc_ref, vp_ref, vm_ref, vx_ref, qb, hk), preferred_element_type=F32) * (1.0 / den)
            lse = m + jnp.log(den)
            rows = slice(qb * BLK, (qb + 1) * BLK)
            for g in range(GQA_GROUP):
                h = hk * GQA_GROUP + g
                ya_ref[rows, h * HEAD_DIM:(h + 1) * HEAD_DIM] = o[g * BLK:(g + 1) * BLK, :].astype(BF16)
                lse_ref[rows, h:h + 1] = lse[g * BLK:(g + 1) * BLK, :]

    kv3 = [pl.BlockSpec((BLK, KV_W), lambda i: (jnp.maximum(nq * i - 1, 0), 0)),
           pl.BlockSpec((nq * BLK, KV_W), lambda i: (i, 0)),
           pl.BlockSpec((BLK, KV_W), lambda i: (jnp.minimum(nq * i + nq, nb - 1), 0))]
    return _call(
        body, name="fwd_attn", grid=(steps,),
        in_specs=[pl.BlockSpec(memory_space=pltpu.SMEM), _row(nq * BLK, Q_W)] + kv3 + kv3
                 + [_full((C, KV_W)), _full((C, KV_W)), _full((3, GQA_GROUP * BLK, 3 * BLK))],
        out_specs=[_row(nq * BLK, Q_W), _row(nq * BLK, N_Q_HEADS)],
        out_shape=[_sds((L, Q_W), BF16), _sds((L, N_Q_HEADS), F32)],
        args=(sink, q, k, k, k, v, v, v, kc, vc, bias), comm=comm)


GMLP_CHUNKS = 4


def _split_pair(t):
    low = lax.broadcasted_iota(jnp.int32, t.shape, 1) < GROUP_DIM
    zero = jnp.zeros_like(t)
    return jnp.where(low, t, zero), jnp.where(low, zero, t)


def _gmlp_spatial(w_ref, t_b, nch):
    rows = []
    for c in range(nch):
        tiles = []
        for pr in range(N_GROUPS // 2):
            lo, hi = _split_pair(t_b[c * BLK:(c + 1) * BLK, pr * 128:(pr + 1) * 128])
            tiles.append(jnp.dot(w_ref[2 * pr], lo, preferred_element_type=F32)
                         + jnp.dot(w_ref[2 * pr + 1], hi, preferred_element_type=F32))
        rows.append(jnp.concatenate(tiles, axis=1))
    return jnp.concatenate(rows, axis=0)


def _gmlp_fwd_vals(u, vb, lnv_ref, ws_ref, bsp_ref, nch):
    uf = u.astype(F32)
    vf = vb.astype(F32)
    gu, tu = _gelu(uf)
    gv, tv = _gelu(vf)
    vhat, rstd = _ln(gv)
    vn = (vhat * lnv_ref[0:1, :] + lnv_ref[1:2, :]).astype(BF16)
    s = _gmlp_spatial(ws_ref, vn, nch) + jnp.concatenate([bsp_ref[...]] * nch, axis=0)
    return uf, vf, gu, tu, tv, vhat, rstd, vn, s


def _k_gmlp(u, vb, lnv, ws, bsp):
    L = u.shape[0]
    nch = min(GMLP_CHUNKS, L // BLK)
    tm = nch * BLK

    def body(u_ref, vb_ref, lnv_ref, ws_ref, bsp_ref, yb_ref):
        _, _, gu, _, _, _, _, _, s = _gmlp_fwd_vals(u_ref[...], vb_ref[...], lnv_ref, ws_ref, bsp_ref, nch)
        yb_ref[...] = (gu * s).astype(BF16)

    return pl.pallas_call(
        body, name="fwd_gmlp", grid=(L // tm,),
        in_specs=[_row(tm, GM_W), _row(tm, GM_W), _full((8, GM_W)), _full((N_GROUPS, BLK, BLK)), _full((BLK, GM_W))],
        out_specs=_row(tm, GM_W),
        out_shape=_sds((L, GM_W), BF16),
        compiler_params=_params(("arbitrary",)),
    )(u, vb, lnv, ws, bsp)


def _k_merge(x, ya, yb, ga, gb, w_a, w_b, w_o, modv, lnv, tm):
    L = x.shape[0]

    def body(x_ref, ya_ref, yb_ref, ga_ref, gb_ref, wa_ref, wb_ref, wo_ref, mod_ref, ln_ref,
             mg_ref, mix_ref, xm_ref, h2_ref):
        a = jnp.dot(ya_ref[...], wa_ref[...], preferred_element_type=F32)
        b = jnp.dot(yb_ref[...], wb_ref[...], preferred_element_type=F32)
        merged = (_sigmoid(ga_ref[...].astype(F32)) * a + _sigmoid(gb_ref[...].astype(F32)) * b).astype(BF16)
        mg_ref[...] = merged
        mix = jnp.dot(merged, wo_ref[...], preferred_element_type=F32)
        mix_ref[...] = mix.astype(BF16)
        r1 = ALPHA * x_ref[...] + mod_ref[2:3, :] * mix
        r1hat, _ = _ln(r1)
        xm = r1hat * ln_ref[0:1, :] + ln_ref[1:2, :]
        xm_ref[...] = xm
        n2, _ = _ln(xm)
        h2_ref[...] = (n2 * (1.0 + mod_ref[4:5, :]) + mod_ref[3:4, :]).astype(BF16)

    return pl.pallas_call(
        body, name="fwd_merge", grid=(L // tm,),
        in_specs=[_row(tm, D), _row(tm, Q_W), _row(tm, GM_W), _row(tm, D), _row(tm, D),
                  _resident((Q_W, D)), _resident((GM_W, D)), _resident((D, D)), _full((8, D)), _full((8, D))],
        out_specs=[_row(tm, D)] * 4,
        out_shape=[_sds((L, D), BF16), _sds((L, D), BF16), _sds((L, D), F32), _sds((L, D), BF16)],
        compiler_params=_params(("arbitrary",)),
    )(x, ya, yb, ga, gb, w_a, w_b, w_o, modv, lnv)


FFN_CH = FFN_H // 2


def _k_ffn(h2, xm, tgt, w_fi, w_fo, modv, lnv, tm):
    L = h2.shape[0]

    def body(h2_ref, xm_ref, t_ref, wi_ref, wo_ref, mod_ref, ln_ref, gate_ref, up_ref, a_ref, dr2_ref, df_ref, acc_ref):
        @pl.when(pl.program_id(0) == 0)
        def _():
            acc_ref[...] = jnp.zeros_like(acc_ref)

        h2v = h2_ref[...]
        f = jnp.zeros((tm, D), F32)
        for j in range(FFN_H // FFN_CH):
            lo = j * FFN_CH
            gate = lax.dot_general(h2v, wi_ref[lo:lo + FFN_CH, :], NT, preferred_element_type=F32)
            up = lax.dot_general(h2v, wi_ref[FFN_H + lo:FFN_H + lo + FFN_CH, :], NT, preferred_element_type=F32)
            act = (gate * _sigmoid(gate) * up).astype(BF16)
            gate_ref[:, lo:lo + FFN_CH] = gate.astype(BF16)
            up_ref[:, lo:lo + FFN_CH] = up.astype(BF16)
            a_ref[:, lo:lo + FFN_CH] = act
            f = f + jnp.dot(act, wo_ref[lo:lo + FFN_CH, :], preferred_element_type=F32)
        gate2 = mod_ref[5:6, :]
        r2 = ALPHA * xm_ref[...] + gate2 * f
        r2hat, rstd = _ln(r2)
        y = r2hat * ln_ref[2:3, :] + ln_ref[3:4, :]
        err = y - t_ref[...]
        dy = err * (1.0 / D)
        dr2 = _ln_bwd(dy * ln_ref[2:3, :], r2hat, rstd)
        dr2_ref[...] = dr2
        df_ref[...] = (gate2 * dr2).astype(BF16)
        acc_ref[0:1, :] += _colsum(dy * r2hat)
        acc_ref[1:2, :] += _colsum(dy)
        acc_ref[2:3, :] += _colsum(dr2 * f)
        acc_ref[3:4, :] += _colsum(err * err) * (0.5 / D)

    return pl.pallas_call(
        body, name="fwd_ffn", grid=(L // tm,),
        in_specs=[_row(tm, D), _row(tm, D), _row(tm, D), _resident((2 * FFN_H, D)), _resident((FFN_H, D)),
                  _full((8, D)), _full((8, D))],
        out_specs=[_row(tm, FFN_H)] * 3 + [_row(tm, D), _row(tm, D), _full((8, D))],
        out_shape=[_sds((L, FFN_H), BF16)] * 3 + [_sds((L, D), F32), _sds((L, D), BF16), _sds((8, D), F32)],
        compiler_params=_params(("arbitrary",)),
    )(h2, xm, tgt, w_fi, w_fo, modv, lnv)


FFN_CH_BWD = FFN_CH


def _k_ffn_bwd(df, gate, up, xm, dr2, x, mix, w_fi, w_fo, modv, lnv, tm):
    L = df.shape[0]

    def body(df_ref, gate_ref, up_ref, xm_ref, dr2_ref, x_ref, mix_ref, wi_ref, wo_ref, mod_ref, ln_ref,
             dF_ref, dmix_ref, dxp_ref, acc_ref):
        @pl.when(pl.program_id(0) == 0)
        def _():
            acc_ref[...] = jnp.zeros_like(acc_ref)

        dfv = df_ref[...]
        ch = FFN_CH_BWD
        chunks = [j * ch for j in range(FFN_H // ch)]
        das = [lax.dot_general(dfv, wo_ref[lo:lo + ch, :], NT, preferred_element_type=F32) for lo in chunks]
        n2, rstd2 = _ln(xm_ref[...])
        mixf = mix_ref[...].astype(F32)
        gate1 = mod_ref[2:3, :]
        r1hat, rstd1 = _ln(ALPHA * x_ref[...] + gate1 * mixf)
        dh2 = jnp.zeros((tm, D), F32)
        for lo, da in zip(chunks, das):
            gate = gate_ref[:, lo:lo + ch].astype(F32)
            upv = up_ref[:, lo:lo + ch].astype(F32)
            sg = _sigmoid(gate)
            d_gate = (da * upv * (sg * (1.0 + gate * (1.0 - sg)))).astype(BF16)
            d_up = (da * (gate * sg)).astype(BF16)
            dF_ref[:, lo:lo + ch] = d_gate
            dF_ref[:, FFN_H + lo:FFN_H + lo + ch] = d_up
            dh2 = dh2 + jnp.dot(d_gate, wi_ref[lo:lo + ch, :], preferred_element_type=F32)
            dh2 = dh2 + jnp.dot(d_up, wi_ref[FFN_H + lo:FFN_H + lo + ch, :], preferred_element_type=F32)
        acc_ref[0:1, :] += _colsum(dh2)
        acc_ref[1:2, :] += _colsum(dh2 * n2)
        dxm = ALPHA * dr2_ref[...] + _ln_bwd(dh2 * (1.0 + mod_ref[4:5, :]), n2, rstd2)
        acc_ref[2:3, :] += _colsum(dxm * r1hat)
        acc_ref[3:4, :] += _colsum(dxm)
        dr1 = _ln_bwd(dxm * ln_ref[0:1, :], r1hat, rstd1)
        dmix_ref[...] = (gate1 * dr1).astype(BF16)
        dxp_ref[...] = ALPHA * dr1
        acc_ref[4:5, :] += _colsum(dr1 * mixf)

    return pl.pallas_call(
        body, name="bwd_ffn", grid=(L // tm,),
        in_specs=[_row(tm, D), _row(tm, FFN_H), _row(tm, FFN_H), _row(tm, D), _row(tm, D), _row(tm, D), _row(tm, D),
                  _resident((2 * FFN_H, D)), _resident((FFN_H, D)), _full((8, D)), _full((8, D))],
        out_specs=[_row(tm, 2 * FFN_H), _row(tm, D), _row(tm, D), _full((8, D))],
        out_shape=[_sds((L, 2 * FFN_H), BF16), _sds((L, D), BF16), _sds((L, D), F32), _sds((8, D), F32)],
        compiler_params=_params(("arbitrary",)),
    )(df, gate, up, xm, dr2, x, mix, w_fi, w_fo, modv, lnv)


def _k_merge_bwd(dmix, merged, ya, yb, ga, gb, w_a, w_b, w_o, tm):
    L = dmix.shape[0]
    n = L // tm

    def body(dmix_ref, mg_ref, ya_ref, yb_ref, ga_ref, gb_ref, wa_ref, wb_ref, wo_ref,
             dga_ref, dgb_ref, dya_ref, dyb_ref, gwa_ref, gwb_ref, gwo_ref, acc_a, acc_b, acc_o):
        i = pl.program_id(0)

        @pl.when(i == 0)
        def _():
            for r in (acc_a, acc_b, acc_o):
                r[...] = jnp.zeros_like(r)

        dmixv = dmix_ref[...]
        dmg = lax.dot_general(dmixv, wo_ref[...], NT, preferred_element_type=F32)
        acc_o[...] += lax.dot_general(mg_ref[...], dmixv, TN, preferred_element_type=F32)
        ya = ya_ref[...]
        a = jnp.dot(ya, wa_ref[...], preferred_element_type=F32)
        sa = _sigmoid(ga_ref[...].astype(F32))
        dA = (dmg * sa).astype(BF16)
        dga_ref[...] = (dmg * a * (sa * (1.0 - sa))).astype(BF16)
        dya_ref[...] = lax.dot_general(dA, wa_ref[...], NT, preferred_element_type=F32).astype(BF16)
        acc_a[...] += lax.dot_general(ya, dA, TN, preferred_element_type=F32)
        yb = yb_ref[...]
        b = jnp.dot(yb, wb_ref[...], preferred_element_type=F32)
        sb = _sigmoid(gb_ref[...].astype(F32))
        dB = (dmg * sb).astype(BF16)
        dgb_ref[...] = (dmg * b * (sb * (1.0 - sb))).astype(BF16)
        dyb_ref[...] = lax.dot_general(dB, wb_ref[...], NT, preferred_element_type=F32).astype(BF16)
        acc_b[...] += lax.dot_general(yb, dB, TN, preferred_element_type=F32)

        @pl.when(i == n - 1)
        def _():
            gwa_ref[...] = acc_a[...].astype(BF16)
            gwb_ref[...] = acc_b[...].astype(BF16)
            gwo_ref[...] = acc_o[...].astype(BF16)

    return pl.pallas_call(
        body, name="bwd_merge", grid=(n,),
        in_specs=[_row(tm, D), _row(tm, D), _row(tm, Q_W), _row(tm, GM_W), _row(tm, D), _row(tm, D),
                  _resident((Q_W, D)), _resident((GM_W, D)), _resident((D, D))],
        out_specs=[_row(tm, D), _row(tm, D), _row(tm, Q_W), _row(tm, GM_W), _full((Q_W, D)), _full((GM_W, D)), _full((D, D))],
        out_shape=[_sds((L, D), BF16), _sds((L, D), BF16), _sds((L, Q_W), BF16), _sds((L, GM_W), BF16),
                   _sds((Q_W, D), BF16), _sds((GM_W, D), BF16), _sds((D, D), BF16)],
        scratch_shapes=[pltpu.VMEM((Q_W, D), F32), pltpu.VMEM((GM_W, D), F32), pltpu.VMEM((D, D), F32)],
        compiler_params=_params(("arbitrary",)),
    )(dmix, merged, ya, yb, ga, gb, w_a, w_b, w_o)


def _k_gmlp_bwd(u, vb, dyb, lnv, ws, wst, bsp):
    L = u.shape[0]
    nch = min(GMLP_CHUNKS, L // BLK)
    tm = nch * BLK

    def body(u_ref, vb_ref, dyb_ref, lnv_ref, ws_ref, wst_ref, bsp_ref, du_ref, dvb_ref, gws_ref, gbst_ref, gln_ref):
        @pl.when(pl.program_id(0) == 0)
        def _():
            gws_ref[...] = jnp.zeros_like(gws_ref)
            gbst_ref[...] = jnp.zeros_like(gbst_ref)
            gln_ref[...] = jnp.zeros_like(gln_ref)

        uf, vf, gu, tu, tv, vhat, rstd, vn, s = _gmlp_fwd_vals(u_ref[...], vb_ref[...], lnv_ref, ws_ref, bsp_ref, nch)
        dyb_f = dyb_ref[...].astype(F32)
        du_ref[...] = (dyb_f * s * _gelu_grad(uf, tu)).astype(BF16)
        ds = dyb_f * gu
        ds_b = ds.astype(BF16)
        for pr in range(N_GROUPS // 2):
            lanes = slice(pr * 128, (pr + 1) * 128)
            gw_lo = gw_hi = ds_sum = None
            for c in range(nch):
                rows = slice(c * BLK, (c + 1) * BLK)
                lo, hi = _split_pair(ds_b[rows, lanes])
                t_lo = lax.dot_general(lo, vn[rows, lanes], NT, preferred_element_type=F32)
                t_hi = lax.dot_general(hi, vn[rows, lanes], NT, preferred_element_type=F32)
                gw_lo = t_lo if c == 0 else gw_lo + t_lo
                gw_hi = t_hi if c == 0 else gw_hi + t_hi
                ds_sum = ds[rows, lanes] if c == 0 else ds_sum + ds[rows, lanes]
            gws_ref[2 * pr] += gw_lo
            gws_ref[2 * pr + 1] += gw_hi
            b_lo, b_hi = _split_pair(ds_sum)
            gbst_ref[:, 2 * pr:2 * pr + 1] += jnp.sum(b_lo, axis=1, keepdims=True)
            gbst_ref[:, 2 * pr + 1:2 * pr + 2] += jnp.sum(b_hi, axis=1, keepdims=True)
        dvn = _gmlp_spatial(wst_ref, ds_b, nch)
        gln_ref[0:1, :] += _colsum(dvn * vhat)
        gln_ref[1:2, :] += _colsum(dvn)
        dgv = _ln_bwd(dvn * lnv_ref[0:1, :], vhat, rstd)
        dvb_ref[...] = (dgv * _gelu_grad(vf, tv)).astype(BF16)

    return pl.pallas_call(
        body, name="bwd_gmlp", grid=(L // tm,),
        in_specs=[_row(tm, GM_W)] * 3 + [_full((8, GM_W)), _full((N_GROUPS, BLK, BLK)), _full((N_GROUPS, BLK, BLK)),
                                         _full((BLK, GM_W))],
        out_specs=[_row(tm, GM_W), _row(tm, GM_W), _full((N_GROUPS, BLK, BLK)), _full((BLK, N_GROUPS)), _full((8, GM_W))],
        out_shape=[_sds((L, GM_W), BF16), _sds((L, GM_W), BF16), _sds((N_GROUPS, BLK, BLK), F32),
                   _sds((BLK, N_GROUPS), F32), _sds((8, GM_W), F32)],
        compiler_params=_params(("arbitrary",)),
    )(u, vb, dyb, lnv, ws, wst, bsp)


ATTN_BWD_BLOCKS = 2


def _k_attn_bwd(sink, q, k, v, kc, vc, dya, lse, cos, sin, bias, comm=None):
    L = q.shape[0]
    C = kc.shape[0]
    nb = L // BLK
    nq = min(ATTN_BWD_BLOCKS, nb)
    steps = nb // nq
    NK = C + 3 * BLK
    chains = [(qb, hk) for qb in range(nq) for hk in range(N_KV_HEADS)]

    def body(sink_ref, q_ref, kp_ref, km_ref, kx_ref, vp_ref, vm_ref, vx_ref, kc_ref, vc_ref, do_ref, lse_ref,
             cq_ref, sq_ref, cl_ref, sl_ref, bias_ref,
             dq_ref, dk_ref, dv_ref, dkc_ref, dvc_ref, dsink_ref,
             dq_scr, ck_scr, cv_scr, k1_acc, k2_acc, v1_acc, v2_acc):
        i = pl.program_id(0)

        @pl.when(i == 0)
        def _():
            for r in (k1_acc, k2_acc, v1_acc, v2_acc, dkc_ref, dvc_ref, dsink_ref):
                r[...] = jnp.zeros_like(r)

        @pl.when(i < steps)
        def _():
            def band(qb):
                first = jnp.where(i == 0, 0, 1) if qb == 0 else 1
                return bias_ref[jnp.where(i == steps - 1, 2, first) if qb == nq - 1 else first]

            def lanes(hk):
                return slice(hk * HEAD_DIM, (hk + 1) * HEAD_DIM)

            def keys(ctx_ref, p_ref, m_ref, x_ref, qb, hk):
                sl = lanes(hk)
                blocks = [p_ref[:, sl]] + [m_ref[j * BLK:(j + 1) * BLK, sl] for j in range(nq)] + [x_ref[:, sl]]
                return jnp.concatenate([ctx_ref[:, sl]] + blocks[qb:qb + 3], axis=0)

            def stacked(ref, qb, hk, width):
                return jnp.concatenate(
                    [ref[qb * BLK:(qb + 1) * BLK, (hk * GQA_GROUP + g) * width:(hk * GQA_GROUP + g + 1) * width]
                     for g in range(GQA_GROUP)], axis=0)

            def scores(qb, hk):
                kcat = keys(kc_ref, kp_ref, km_ref, kx_ref, qb, hk)
                qg = stacked(q_ref, qb, hk, HEAD_DIM)
                s = _masked(lax.dot_general(qg, kcat, NT, preferred_element_type=F32), band(qb), C)
                dog = stacked(do_ref, qb, hk, HEAD_DIM)
                dp = lax.dot_general(dog, keys(vc_ref, vp_ref, vm_ref, vx_ref, qb, hk), NT, preferred_element_type=F32)
                return kcat, qg, dog, s, dp

            def softmax_bwd(qb, hk, s, dp):
                lse_c = stacked(lse_ref, qb, hk, 1)
                p = jnp.exp(s - lse_c)
                delta = jnp.sum(p * dp, axis=1, keepdims=True)
                ds = (p * (dp - delta)).astype(BF16)
                p_sink = jnp.exp(_sink_col(sink_ref, hk) - lse_c) * delta
                return p.astype(BF16), ds, p_sink

            def put_dq(qb, hk, dqs, p_sink):
                for g in range(GQA_GROUP):
                    h = hk * GQA_GROUP + g
                    dq_scr[qb * BLK:(qb + 1) * BLK, h * HEAD_DIM:(h + 1) * HEAD_DIM] = dqs[g * BLK:(g + 1) * BLK, :]
                    tot = jnp.sum(p_sink[g * BLK:(g + 1) * BLK, :], axis=0, keepdims=True)
                    dsink_ref[h:h + 1, :] -= jnp.broadcast_to(tot, (1, 128))

            ahead = 4
            sc = [scores(*c) for c in chains[:ahead]]
            pending = None
            for n, (qb, hk) in enumerate(chains):
                if n + ahead < len(chains):
                    sc.append(scores(*chains[n + ahead]))
                kcat, qg, dog, s, dp = sc[n]
                pb, ds, p_sink = softmax_bwd(qb, hk, s, dp)
                if pending is not None:
                    pqb, phk, pds, ppb, pqg, pdog = pending
                    ck_scr[pqb, :, lanes(phk)] = lax.dot_general(pds, pqg, TN, preferred_element_type=F32)
                    cv_scr[pqb, :, lanes(phk)] = lax.dot_general(ppb, pdog, TN, preferred_element_type=F32)
                put_dq(qb, hk, jnp.dot(ds, kcat, preferred_element_type=F32), p_sink)
                pending = (qb, hk, ds, pb, qg, dog)
            pqb, phk, pds, ppb, pqg, pdog = pending
            ck_scr[pqb, :, lanes(phk)] = lax.dot_general(pds, pqg, TN, preferred_element_type=F32)
            cq, sq = cq_ref[...], sq_ref[...]
            for j in range(4):
                dq_ref[:, j * 128:(j + 1) * 128] = _unrope(dq_scr[:, j * 128:(j + 1) * 128] * Q_SCALE, cq, sq).astype(BF16)
            cv_scr[pqb, :, lanes(phk)] = lax.dot_general(ppb, pdog, TN, preferred_element_type=F32)
            dkc_ref[...] += functools.reduce(lambda a, b: a + b, [ck_scr[qb, 0:C, :] for qb in range(nq)])
            dvc_ref[...] += functools.reduce(lambda a, b: a + b, [cv_scr[qb, 0:C, :] for qb in range(nq)])

        @pl.when(i >= steps)
        def _():
            ck_scr[...] = jnp.zeros_like(ck_scr)
            cv_scr[...] = jnp.zeros_like(cv_scr)

        def slot(scr, r, carried):
            parts = [scr[qb, C + (r - qb) * BLK:C + (r - qb + 1) * BLK, :] for qb in range(nq) if 0 <= r - qb <= 2]
            total = functools.reduce(lambda a, b: a + b, parts)
            return total if carried is None else carried[...] + total

        for r in range(nq):
            rows = slice(r * BLK, (r + 1) * BLK)
            carried_k, carried_v = ((k1_acc, v1_acc), (k2_acc, v2_acc), (None, None))[min(r, 2)]
            tables = (cl_ref[...], sl_ref[...]) if r == 0 else (cq_ref[(r - 1) * BLK:r * BLK, :], sq_ref[(r - 1) * BLK:r * BLK, :])
            dk_ref[rows, :] = _unrope(slot(ck_scr, r, carried_k), *tables).astype(BF16)
            dv_ref[rows, :] = slot(cv_scr, r, carried_v).astype(BF16)
        k1_acc[...] = slot(ck_scr, nq, None)
        v1_acc[...] = slot(cv_scr, nq, None)
        k2_acc[...] = slot(ck_scr, nq + 1, None)
        v2_acc[...] = slot(cv_scr, nq + 1, None)

    last = steps - 1
    kv3 = [pl.BlockSpec((BLK, KV_W), lambda i: (jnp.clip(nq * i - 1, 0, nb - 1), 0)),
           pl.BlockSpec((nq * BLK, KV_W), lambda i: (jnp.minimum(i, last), 0)),
           pl.BlockSpec((BLK, KV_W), lambda i: (jnp.minimum(nq * i + nq, nb - 1), 0))]
    cur = lambda w: pl.BlockSpec((nq * BLK, w), lambda i: (jnp.minimum(i, last), 0))
    late = lambda w: pl.BlockSpec((BLK, w), lambda i: (jnp.clip(nq * i - 1, 0, nb - 1), 0))
    out2 = lambda w: pl.BlockSpec((nq * BLK, w), lambda i: (i, 0))
    return _call(
        body, name="bwd_attn", grid=(steps + 1,),
        in_specs=[pl.BlockSpec(memory_space=pltpu.SMEM), cur(Q_W)] + kv3 + kv3
                 + [_full((C, KV_W)), _full((C, KV_W)), cur(Q_W), cur(N_Q_HEADS), cur(128), cur(128), late(128), late(128),
                    _full((3, GQA_GROUP * BLK, 3 * BLK))],
        out_specs=[cur(Q_W), out2(KV_W), out2(KV_W), _full((C, KV_W)), _full((C, KV_W)), _full((8, 128))],
        out_shape=[_sds((L, Q_W), BF16), _sds((L + nq * BLK, KV_W), BF16), _sds((L + nq * BLK, KV_W), BF16),
                   _sds((C, KV_W), F32), _sds((C, KV_W), F32), _sds((8, 128), F32)],
        scratch=[pltpu.VMEM((nq * BLK, Q_W), F32), pltpu.VMEM((nq, NK, KV_W), F32), pltpu.VMEM((nq, NK, KV_W), F32)]
                + [pltpu.VMEM((BLK, KV_W), F32)] * 4,
        args=(sink, q, k, k, k, v, v, v, kc, vc, dya, lse, cos, sin, cos, sin, bias), comm=comm)


def _k_ctx_bwd(ctx, modc, hc, dkc, dvc, w_kv):
    C = ctx.shape[0]

    def body(c_ref, mod_ref, hc_ref, dkc_ref, dvc_ref, w_ref, gw_ref, dmod_ref):
        dkv = jnp.concatenate([dkc_ref[...], dvc_ref[...]], axis=1).astype(BF16)
        gw_ref[...] = lax.dot_general(dkv, hc_ref[...], TN, preferred_element_type=F32)
        dhc = jnp.dot(dkv, w_ref[...], preferred_element_type=F32)
        n, _ = _ln(c_ref[...])
        dmod_ref[...] = jnp.zeros_like(dmod_ref)
        dmod_ref[0:1, :] = _colsum(dhc)
        dmod_ref[1:2, :] = _colsum(dhc * n)

    return pl.pallas_call(
        body, name="bwd_ctx", grid=(1,),
        in_specs=[_full((C, D)), _full((8, D)), _full((C, D)), _full((C, KV_W)), _full((C, KV_W)), _full((2 * KV_W, D))],
        out_specs=[_full((2 * KV_W, D)), _full((8, D))],
        out_shape=[_sds((2 * KV_W, D), F32), _sds((8, D), F32)],
        compiler_params=_params(("arbitrary",)),
    )(ctx, modc, hc, dkc, dvc, w_kv)


def _k_in_bwd(dq, dk, dv, du, dvb, dga, dgb, x, dxp, w_in, modv, tm, comm=None):
    L = x.shape[0]
    parts = [(O_Q, Q_W), (O_K, KV_W), (O_V, KV_W), (O_U, GM_W), (O_VB, GM_W), (O_GA, D), (O_GB, D)]

    def body(dq_ref, dk_ref, dv_ref, du_ref, dvb_ref, dga_ref, dgb_ref, x_ref, dxp_ref, w_ref, mod_ref,
             dP_ref, gx_ref, acc_ref):
        @pl.when(pl.program_id(0) == 0)
        def _():
            acc_ref[...] = jnp.zeros_like(acc_ref)

        for (lo, width), r in zip(parts, (dq_ref, dk_ref, dv_ref, du_ref, dvb_ref, dga_ref, dgb_ref)):
            dP_ref[:, lo:lo + width] = r[...]
        n1, rstd1 = _ln(x_ref[...])
        dh = jnp.dot(dP_ref[...], w_ref[...], preferred_element_type=F32)
        acc_ref[0:1, :] += _colsum(dh)
        acc_ref[1:2, :] += _colsum(dh * n1)
        gx_ref[...] = dxp_ref[...] + _ln_bwd(dh * (1.0 + mod_ref[1:2, :]), n1, rstd1)

    return _call(
        body, name="bwd_in", grid=(L // tm,),
        in_specs=[_row(tm, w) for _, w in parts] + [_row(tm, D), _row(tm, D), _resident((IN_W, D)), _full((8, D))],
        out_specs=[_row(tm, IN_W), _row(tm, D), _full((8, D))],
        out_shape=[_sds((L, IN_W), BF16), _sds((L, D), F32), _sds((8, D), F32)],
        args=(dq, dk, dv, du, dvb, dga, dgb, x, dxp, w_in, modv), comm=comm)


def _wgrad(a, b, name, tk, tt, comm=None, extra=None):
    T, K = a.shape
    N = b.shape[1]
    nt = T // tt

    def body(*refs):
        a_ref, b_ref = refs[:2]
        o_ref, acc_ref = refs[-2:]
        j, t = pl.program_id(0), pl.program_id(1)

        @pl.when(t == 0)
        def _():
            acc_ref[...] = jnp.zeros_like(acc_ref)

        acc_ref[...] += lax.dot_general(a_ref[...], b_ref[...], TN, preferred_element_type=F32)

        if extra is not None:
            lo, rows = extra[0] % tk, extra[1].shape[0]

            @pl.when((t == nt - 1) & (j == extra[0] // tk))
            def _():
                acc_ref[lo:lo + rows, :] += refs[2][...]

        @pl.when(t == nt - 1)
        def _():
            o_ref[...] = acc_ref[...].astype(BF16)

    extra_specs = [] if extra is None else [pl.BlockSpec(extra[1].shape, lambda j, t: (0, 0))]
    (out,), got = _call(
        body, name=name, grid=(K // tk, nt),
        in_specs=[pl.BlockSpec((tt, tk), lambda j, t: (t, j)), pl.BlockSpec((tt, N), lambda j, t: (t, 0))] + extra_specs,
        out_specs=[pl.BlockSpec((tk, N), lambda j, t: (j, 0))],
        out_shape=[_sds((K, N), BF16)],
        scratch=[pltpu.VMEM((tk, N), F32)],
        args=(a, b) + (() if extra is None else (extra[1],)), comm=comm)
    return (out, got) if comm is not None else out


def _adamw_reduce(parts, w, m, v, name, tr):
    R, C = w.shape
    n_parts = parts.shape[0]

    def body(p_ref, w_ref, m_ref, v_ref, g_ref, d_ref, m2_ref, v2_ref):
        g = p_ref[0].astype(F32)
        for i in range(1, n_parts):
            g = g + p_ref[i].astype(F32)
        delta, m2, v2 = _adamw(w_ref[...], g, m_ref[...], v_ref[...])
        g_ref[...] = g
        d_ref[...] = delta
        m2_ref[...] = m2
        v2_ref[...] = v2

    spec = _row(tr, C)
    return pl.pallas_call(
        body, name=name, grid=(R // tr,),
        in_specs=[pl.BlockSpec((n_parts, tr, C), lambda i: (0, i, 0)), spec, spec, spec],
        out_specs=[spec] * 4,
        out_shape=[_sds((R, C), F32)] * 4,
        compiler_params=_params(("arbitrary",)),
    )(parts, w, m, v)


SMALL_ORDER = ("b_ada", "ln1_g", "ln1_b", "ln2_g", "ln2_b", "gmlp_ln_g", "gmlp_ln_b", "b_spatial", "attn_sink")


def _small_step(gath, params):
    flat = [a for name in SMALL_ORDER for a in params[name]]

    def grad_of(tot, name):
        if name == "b_ada":
            return jnp.concatenate([tot[r:r + 1, :] for r in range(6)], axis=1)
        if name in ("ln1_g", "ln1_b", "ln2_g", "ln2_b"):
            r = 8 + ("ln1_g", "ln1_b", "ln2_g", "ln2_b").index(name)
            return tot[r:r + 1, :]
        if name == "gmlp_ln_g":
            return tot[12:13, :GM_W]
        if name == "gmlp_ln_b":
            return tot[12:13, GM_W:]
        if name == "b_spatial":
            return jnp.concatenate([tot[13:14, g * BLK:(g + 1) * BLK] for g in range(N_GROUPS)], axis=0)[None]
        return tot[14:15, :N_Q_HEADS]

    def body(*refs):
        g_ref, in_refs = refs[0], refs[1:1 + len(flat)]
        tot_ref, out_refs = refs[1 + len(flat)], refs[2 + len(flat):]
        tot = g_ref[0]
        for i in range(1, N_DEV):
            tot = tot + g_ref[i]
        tot_ref[...] = tot
        tot_ref[0:2, :] = tot[0:2, :] + tot[6:8, :]
        tot_ref[15:16, :] = jnp.broadcast_to(jnp.sum(tot[15:16, :], axis=1, keepdims=True), (1, D))
        tot = tot_ref[...]
        for k, name in enumerate(SMALL_ORDER):
            w_ref, m_ref, v_ref = in_refs[3 * k:3 * k + 3]
            g = grad_of(tot, name)
            delta, m2, v2 = _adamw(w_ref[...], g, m_ref[...], v_ref[...])
            for r, val in zip(out_refs[4 * k:4 * k + 4], (g, delta, m2, v2)):
                r[...] = val

    res = pl.pallas_call(
        body, name="small_step", grid=(1,),
        in_specs=[_full((N_DEV, 16, D))] + [_full(a.shape) for a in flat],
        out_specs=[_full((16, D))] + [_full(params[name][0].shape) for name in SMALL_ORDER for _ in range(4)],
        out_shape=[_sds((16, D), F32)] + [_sds(params[name][0].shape, F32) for name in SMALL_ORDER for _ in range(4)],
        compiler_params=_params(("arbitrary",)),
    )(gath, *flat)
    return res[0], {name: res[1 + 4 * k:5 + 4 * k] for k, name in enumerate(SMALL_ORDER)}


def _cctx_finish(gath, c_ctx, m, v):
    def body(g_ref, c_ref, m_ref, v_ref, gr_ref, d_ref, m2_ref, v2_ref):
        ds = g_ref[0]
        for i in range(1, N_DEV):
            ds = ds + g_ref[i]
        c = c_ref[...]
        sg = _sigmoid(c)
        g = ds * (sg * (1.0 + c * (1.0 - sg)))
        delta, m2, v2 = _adamw(c, g, m_ref[...], v_ref[...])
        gr_ref[...] = g
        d_ref[...] = delta
        m2_ref[...] = m2
        v2_ref[...] = v2

    return pl.pallas_call(
        body, name="cctx_finish", grid=(1,),
        in_specs=[_full((N_DEV, 8, D))] + [_full((8, D))] * 3, out_specs=[_full((8, D))] * 4,
        out_shape=[_sds((8, D), F32)] * 4,
        compiler_params=_params(("arbitrary",)),
    )(gath, c_ctx, m, v)


def _pad_rows(a, rows):
    return jnp.concatenate([a, jnp.zeros((rows - a.shape[0], a.shape[1]), a.dtype)], axis=0)


def kernel(x, c, ctx, c_ctx, w_ada, b_ada, w_in, attn_sink, gmlp_ln_g, gmlp_ln_b, w_spatial, b_spatial, w_branch_a, w_branch_b, w_out, ln1_g, ln1_b, w_ffn_in, w_ffn_out, ln2_g, ln2_b, loss_target, m_c_ctx, m_w_ada, m_b_ada, m_w_in, m_attn_sink, m_gmlp_ln_g, m_gmlp_ln_b, m_w_spatial, m_b_spatial, m_w_branch_a, m_w_branch_b, m_w_out, m_ln1_g, m_ln1_b, m_w_ffn_in, m_w_ffn_out, m_ln2_g, m_ln2_b, v_c_ctx, v_w_ada, v_b_ada, v_w_in, v_attn_sink, v_gmlp_ln_g, v_gmlp_ln_b, v_w_spatial, v_b_spatial, v_w_branch_a, v_w_branch_b, v_w_out, v_ln1_g, v_ln1_b, v_w_ffn_in, v_w_ffn_out, v_ln2_g, v_ln2_b):
    L = x.shape[1]
    me = 4 * lax.axis_index("x") + 2 * lax.axis_index("y") + lax.axis_index("c")
    x2, tgt, ctx2 = x[0], loss_target[0], ctx[0]
    tiles = _Tiles(L)
    tm_in, tm, tt = tiles.wide, tiles.narrow, tiles.tokens

    transposed = ("w_in", "w_ffn_in")
    tr = lambda kname, a: a.T if kname in transposed else a
    big = dict(w_in=w_in[0].T, w_branch_a=w_branch_a[0], w_branch_b=w_branch_b[0], w_out=w_out[0],
               w_ffn_in=w_ffn_in[0].T, w_ffn_out=w_ffn_out[0])
    col_sharded = ("w_branch_a", "w_branch_b")
    shard_bf = {k: a.astype(BF16) for k, a in big.items()}

    def assemble(kname, g):
        if kname in col_sharded:
            return g.transpose(1, 0, 2).reshape(g.shape[1], N_DEV * g.shape[2])
        return g.reshape(N_DEV * g.shape[1], g.shape[2])

    def to_blocks(kname, g):
        if kname in col_sharded:
            return g.reshape(g.shape[0], N_DEV, g.shape[1] // N_DEV).transpose(1, 0, 2)
        return g.reshape(N_DEV, g.shape[0] // N_DEV, g.shape[1])

    full = {}
    n_ada = w_ada.shape[2]
    b_my = lax.dynamic_slice(b_ada, (0, me * n_ada), (1, n_ada))
    act, mod_all, got = _prologue(_pad_rows(c, 8), _pad_rows(c_ctx[None, :], 8), w_ada[0], b_my,
                                  _Comm(gather=[shard_bf["w_in"]]))
    full["w_in"] = assemble("w_in", got[0])
    mod_all = mod_all.transpose(1, 0, 2).reshape(16, 6 * D)
    modv = _pad_rows(lax.dynamic_slice(mod_all, (me, 0), (1, 6 * D)).reshape(6, D), 8)
    modc = _pad_rows(mod_all[8].reshape(6, D), 8)

    lnv = _pad_rows(jnp.concatenate([ln1_g, ln1_b, ln2_g, ln2_b], axis=0), 8)
    gm_lnv = _pad_rows(jnp.concatenate([gmlp_ln_g, gmlp_ln_b], axis=0), 8)
    ws_b = w_spatial[0].astype(BF16)
    wst_b = ws_b.transpose(0, 2, 1)
    bsp = jnp.repeat(b_spatial[0].T, GROUP_DIM, axis=1)
    sink = attn_sink[0]
    cos, sin = _rope_tables(L)
    bias = _attn_bias()
    w_kv = full["w_in"][O_K:O_K + 2 * KV_W, :]

    (h, q, k, v, u, vb, ga, gb), got = _k_in(
        x2, modv, full["w_in"], cos, sin, tm_in,
        comm=_Comm(gather=[shard_bf[kname] for kname in ("w_branch_a", "w_branch_b", "w_out", "w_ffn_out")]))
    for kname, g in zip(("w_branch_a", "w_branch_b", "w_out", "w_ffn_out"), got):
        full[kname] = assemble(kname, g)
    hc, kc, vc = _k_ctx(ctx2, modc, w_kv)
    (ya, lse), got = _k_attn(sink, q, k, v, kc, vc, bias, comm=_Comm(gather=[shard_bf["w_ffn_in"]]))
    full["w_ffn_in"] = assemble("w_ffn_in", got[0])
    yb = _k_gmlp(u, vb, gm_lnv, ws_b, bsp)
    merged, mix, xm, h2 = _k_merge(x2, ya, yb, ga, gb, full["w_branch_a"], full["w_branch_b"], full["w_out"], modv, lnv, tm_in)
    gate, up, act_f, dr2, df, acc_f = _k_ffn(h2, xm, tgt, full["w_ffn_in"], full["w_ffn_out"], modv, lnv, tm_in)

    dF, dmix, dxp, acc_b = _k_ffn_bwd(df, gate, up, xm, dr2, x2, mix, full["w_ffn_in"], full["w_ffn_out"], modv, lnv, tm)
    blk_fo = to_blocks("w_ffn_out", _wgrad(act_f, df, "wgrad_ffn_out", tiles.tk_ffn, tt))
    gw_fi, (rcv_fo,) = _wgrad(dF, h2, "wgrad_ffn_in", tiles.tk_ffn, tt, comm=_Comm(scatter=[blk_fo]))
    blk_fi = to_blocks("w_ffn_in", gw_fi)
    dga, dgb, dya, dyb, gw_a, gw_b, gw_o = _k_merge_bwd(
        dmix, merged, ya, yb, ga, gb, full["w_branch_a"], full["w_branch_b"], full["w_out"], tm_in)
    du, dvb, g_ws, g_bst, g_gln = _k_gmlp_bwd(u, vb, dyb, gm_lnv, ws_b, wst_b, bsp)
    (dq, dk_late, dv_late, dkc, dvc, g_sink), (gath_ws, rcv_fi) = _k_attn_bwd(
        sink, q, k, v, kc, vc, dya, lse, cos, sin, bias,
        comm=_Comm(gather=[g_ws.reshape(N_GROUPS * BLK, BLK)], scatter=[blk_fi]))
    dk, dv = dk_late[BLK:BLK + L], dv_late[BLK:BLK + L]
    blk_a, blk_b, blk_o = to_blocks("w_branch_a", gw_a), to_blocks("w_branch_b", gw_b), to_blocks("w_out", gw_o)
    (dP, grad_x, acc_i), _ = _k_in_bwd(dq, dk, dv, du, dvb, dga, dgb, x2, dxp, full["w_in"], modv, tm_in)
    g_ctx, dmodc = _k_ctx_bwd(ctx2, modc, hc, dkc, dvc, w_kv)
    gw_in, (rcv_a, rcv_b, rcv_o) = _wgrad(dP, h, "wgrad_in", tiles.tk_in, tt, comm=_Comm(scatter=[blk_a, blk_b, blk_o]),
                                          extra=(O_K, g_ctx))

    dmod_x = jnp.concatenate([acc_i[0:2], acc_b[4:5], acc_b[0:2], acc_f[2:3]], axis=0)
    small = jnp.concatenate([
        dmod_x, dmodc[0:2], acc_b[2:4], acc_f[0:2],
        jnp.concatenate([g_gln[0:1], g_gln[1:2]], axis=1), g_bst.T.reshape(1, D),
        _pad_rows(g_sink[:, 0:1], D).T, acc_f[3:4]], axis=0)
    rcv_in, gath = _exchange_two_level(to_blocks("w_in", gw_in), small, "exchange_last")
    received = dict(w_in=rcv_in, w_branch_a=rcv_a, w_branch_b=rcv_b, w_out=rcv_o, w_ffn_in=rcv_fi, w_ffn_out=rcv_fo)
    moments = dict(w_in=(m_w_in, v_w_in), w_branch_a=(m_w_branch_a, v_w_branch_a), w_branch_b=(m_w_branch_b, v_w_branch_b),
                   w_out=(m_w_out, v_w_out), w_ffn_in=(m_w_ffn_in, v_w_ffn_in), w_ffn_out=(m_w_ffn_out, v_w_ffn_out))
    names = list(big)
    res = {}
    for kname in names:
        mm, vv = moments[kname]
        R = big[kname].shape[0]
        res[kname] = [tr(kname, r) for r in _adamw_reduce(
            received[kname], big[kname], tr(kname, mm[0]), tr(kname, vv[0]), "adamw_" + kname, 256 if R % 256 == 0 else R // 2)]

    ws2d = lambda a: a.reshape(N_GROUPS * BLK, BLK)
    res_ws = [r.reshape(w_spatial.shape) for r in _adamw_reduce(
        gath_ws, ws2d(w_spatial), ws2d(m_w_spatial), ws2d(v_w_spatial), "adamw_w_spatial", 256)]
    tot, res_small = _small_step(gath, dict(
        b_ada=(b_ada, m_b_ada, v_b_ada), ln1_g=(ln1_g, m_ln1_g, v_ln1_g), ln1_b=(ln1_b, m_ln1_b, v_ln1_b),
        ln2_g=(ln2_g, m_ln2_g, v_ln2_g), ln2_b=(ln2_b, m_ln2_b, v_ln2_b),
        gmlp_ln_g=(gmlp_ln_g, m_gmlp_ln_g, v_gmlp_ln_g), gmlp_ln_b=(gmlp_ln_b, m_gmlp_ln_b, v_gmlp_ln_b),
        b_spatial=(b_spatial, m_b_spatial, v_b_spatial), attn_sink=(attn_sink, m_attn_sink, v_attn_sink)))
    loss = tot[15, 0]

    dmod_rows = jnp.concatenate([gath[:, 0:6, :].reshape(N_DEV, 6 * D),
                                 jnp.concatenate([tot[6:8].reshape(1, 2 * D), jnp.zeros((1, 4 * D), F32)], axis=1),
                                 jnp.zeros((7, 6 * D), F32)], axis=0)
    dmod_my = lax.dynamic_slice(dmod_rows, (0, me * n_ada), (16, n_ada))
    g_wada, d_wada, m2_wada, v2_wada, pc = _ada_bwd(act, dmod_my, w_ada[0], m_w_ada[0], v_w_ada[0])
    pc_all = _gather_rows(pc, "gather_cctx")
    cc8 = lambda a: _pad_rows(a.reshape(1, D), 8)
    g_cc, d_cc, m2_cc, v2_cc = _cctx_finish(pc_all, cc8(c_ctx), cc8(m_c_ctx), cc8(v_c_ctx))

    order = ["c_ctx", "w_ada", "b_ada", "w_in", "attn_sink", "gmlp_ln_g", "gmlp_ln_b", "w_spatial", "b_spatial",
             "w_branch_a", "w_branch_b", "w_out", "ln1_g", "ln1_b", "w_ffn_in", "w_ffn_out", "ln2_g", "ln2_b"]
    grads, deltas, new_m, new_v = {}, {}, {}, {}
    grads["c_ctx"], deltas["c_ctx"], new_m["c_ctx"], new_v["c_ctx"] = g_cc[0], d_cc[0], m2_cc[0], v2_cc[0]
    grads["w_ada"], deltas["w_ada"], new_m["w_ada"], new_v["w_ada"] = g_wada[None], d_wada[None], m2_wada[None], v2_wada[None]
    for kname in names:
        g, d, m2, v2 = res[kname]
        grads[kname], deltas[kname], new_m[kname], new_v[kname] = g[None], d[None], m2[None], v2[None]
    grads["w_spatial"], deltas["w_spatial"], new_m["w_spatial"], new_v["w_spatial"] = res_ws
    for kname in SMALL_ORDER:
        grads[kname], deltas[kname], new_m[kname], new_v[kname] = res_small[kname]
    return (loss, grad_x[None], *[grads[n] for n in order], *[deltas[n] for n in order],
            *[new_m[n] for n in order], *[new_v[n] for n in order])
```

```python
import functools
import math

import jax
import jax.numpy as jnp
import numpy as np
from jax import lax
from jax.experimental import pallas as pl
from jax.experimental.pallas import tpu as pltpu

F32 = jnp.float32
BF16 = jnp.bfloat16
MESH = pl.DeviceIdType.MESH

N_DEV = 8
D = 1024
HEAD_DIM = 64
N_Q_HEADS = 8
N_KV_HEADS = 2
GQA_GROUP = 4
BLK = 128
Q_W = 512
KV_W = 128
GM_W = 512
N_GROUPS = 8
GROUP_DIM = 64
FFN_H = 2816
IN_W = 3840
O_Q, O_K, O_V, O_U, O_VB, O_GA, O_GB = 0, 512, 640, 768, 1280, 1792, 2816
LN_EPS = 1e-5
NEG_INF = -1e30
ALPHA = 2.0 ** 0.25
ROPE_BASE = 10000.0
ROPE_PAIRS = 16
Q_SCALE = HEAD_DIM ** -0.5
GELU_K0 = math.sqrt(2.0 / math.pi)
GELU_K1 = 0.044715

ADAM_LR = 0.001
ADAM_B1 = 0.9
ADAM_B2 = 0.999
ADAM_EPS = 1e-08
ADAM_WD = 0.01
ADAM_STEP = 10

V7X_VMEM_BYTES = 64 * 1024 * 1024
VMEM_LIMIT = V7X_VMEM_BYTES * 7 // 8
NT = (((1,), (1,)), ((), ()))
TN = (((0,), (0,)), ((), ()))


class _Tiles:
    def __init__(self, L):
        self.wide = min(512, L)
        self.narrow = min(256, L)
        self.tokens = min(2048, L)
        self.tk_in = IN_W // 3
        self.tk_ffn = FFN_H // 2


def _params(sem=None):
    return pltpu.CompilerParams(dimension_semantics=sem, vmem_limit_bytes=VMEM_LIMIT)


def _row(tm, w):
    return pl.BlockSpec((tm, w), lambda i: (i, 0))


def _full(shape):
    nd = len(shape)
    return pl.BlockSpec(shape, lambda i: (0,) * nd)


def _resident(shape):
    nd = len(shape)
    return pl.BlockSpec(shape, lambda i: (0,) * nd, pipeline_mode=pl.Buffered(1))


def _sds(shape, dt):
    return jax.ShapeDtypeStruct(shape, dt)


def _ln(xf):
    mu = jnp.mean(xf, axis=-1, keepdims=True)
    xc = xf - mu
    var = jnp.mean(xc * xc, axis=-1, keepdims=True)
    rstd = lax.rsqrt(var + LN_EPS)
    return xc * rstd, rstd


def _ln_bwd(dn, n, rstd):
    m1 = jnp.mean(dn, axis=-1, keepdims=True)
    m2 = jnp.mean(dn * n, axis=-1, keepdims=True)
    return rstd * (dn - m1 - n * m2)


def _colsum(t):
    return jnp.sum(t, axis=0, keepdims=True)


def _sigmoid(x):
    return 0.5 * jnp.tanh(0.5 * x) + 0.5


def _gelu(x):
    t = jnp.tanh(GELU_K0 * (x + GELU_K1 * (x * x * x)))
    return x * (0.5 * (1.0 + t)), t


def _gelu_grad(x, t):
    return 0.5 * (1.0 + t) + 0.5 * x * (1.0 - t * t) * (GELU_K0 * (1.0 + 3.0 * GELU_K1 * x * x))


def _swap16(t):
    lane = lax.broadcasted_iota(jnp.int32, t.shape, 1)
    return jnp.where((lane & 16) == 0, pltpu.roll(t, 112, 1), pltpu.roll(t, 16, 1))


def _rope(t, cos, sin):
    return t * cos + _swap16(t) * sin


def _unrope(t, cos, sin):
    return t * cos - _swap16(t) * sin


def _adamw(w, g, m, v):
    m2 = ADAM_B1 * m + (1.0 - ADAM_B1) * g
    v2 = ADAM_B2 * v + (1.0 - ADAM_B2) * (g * g)
    m_hat = m2 / (1.0 - ADAM_B1 ** ADAM_STEP)
    v_hat = v2 / (1.0 - ADAM_B2 ** ADAM_STEP)
    delta = -ADAM_LR * (m_hat / (jnp.sqrt(v_hat) + ADAM_EPS) + ADAM_WD * w)
    return delta, m2, v2


def _rope_tables(L):
    inv = (np.float32(ROPE_BASE) ** (-np.arange(ROPE_PAIRS, dtype=np.float32) / np.float32(ROPE_PAIRS))).astype(np.float32)
    t = np.arange(L, dtype=np.int32)
    rows = (t // 64).astype(np.float32)[:, None] * inv
    cols = (t % 64).astype(np.float32)[:, None] * inv
    cr, sr, cc, sc = np.cos(rows), np.sin(rows), np.cos(cols), np.sin(cols)
    cos = np.concatenate([cr, cr, cc, cc], axis=1)
    sin = np.concatenate([-sr, sr, -sc, sc], axis=1)
    return jnp.asarray(np.tile(cos, (1, 2)), F32), jnp.asarray(np.tile(sin, (1, 2)), F32)


def _me():
    return lax.axis_index("x"), lax.axis_index("y"), lax.axis_index("c")


def _peer(mx, my, mc, k):
    return (mx ^ ((k >> 2) & 1), my ^ ((k >> 1) & 1), mc ^ (k & 1))


class _Comm:
    def __init__(self, gather=(), scatter=(), spread=()):
        self.kinds = ["gather"] * len(gather) + ["scatter"] * len(scatter) + ["spread"] * len(spread)
        self.args = list(gather) + list(scatter) + list(spread)
        self.n = len(self.args)

    def out_shape(self):
        return [_sds(a.shape if k == "scatter" else (N_DEV,) + a.shape, a.dtype) for k, a in zip(self.kinds, self.args)]

    def specs(self):
        return [pl.BlockSpec(memory_space=pl.ANY)] * self.n

    def scratch(self):
        return [pltpu.SemaphoreType.DMA((7 * self.n,)), pltpu.SemaphoreType.DMA((7 * self.n,)),
                pltpu.SemaphoreType.DMA((self.n,))]

    def _plan(self, x_refs, out_refs, send_sems, recv_sems, local_sems):
        mx, my, mc = _me()
        me = 4 * mx + 2 * my + mc
        here, sibling = (mx, my, mc), (mx, my, 1 - mc)
        chips = [(1 - mx, my), (mx, 1 - my), (1 - mx, 1 - my)]
        local, first, last = [], [], []
        relay = [[], [], []]
        for a, kind in enumerate(self.kinds):
            x, out = x_refs[a], out_refs[a]

            def rc(k, src, dst, to):
                return pltpu.make_async_remote_copy(
                    src_ref=src, dst_ref=dst, send_sem=send_sems.at[7 * a + k], recv_sem=recv_sems.at[7 * a + k],
                    device_id=to, device_id_type=MESH)

            if kind == "gather":
                local.append(pltpu.make_async_copy(x, out.at[me], local_sems.at[a]))
                first.append(rc(0, x, out.at[me], sibling))
                last.append(rc(0, x, out.at[me ^ 1], here))
                for j, (cx, cy) in enumerate(chips):
                    first.append(rc(1 + j, x, out.at[me], (cx, cy, mc)))
                    landed = out.at[4 * cx + 2 * cy + mc]
                    relay[j].append((rc(1 + j, x, landed, here), rc(4 + j, landed, landed, sibling)))
                    last.append(rc(4 + j, x, out.at[4 * cx + 2 * cy + 1 - mc], here))
            else:
                own = x.at[me] if kind == "scatter" else x
                local.append(pltpu.make_async_copy(own, out.at[me], local_sems.at[a]))
                for k in range(1, N_DEV):
                    src = x.at[me ^ k] if kind == "scatter" else x
                    first.append(rc(k - 1, src, out.at[me], _peer(mx, my, mc, k)))
                    last.append(rc(k - 1, own, out.at[me ^ k], here))
        return local, first, relay[0] + relay[1] + relay[2], last

    def start(self, *refs):
        local, first, _, _ = self._plan(*refs)
        for cp in local + first:
            cp.start()

    def relay(self, *refs):
        _, _, relay, _ = self._plan(*refs)
        for arrival, onward in relay:
            arrival.wait_recv()
            onward.start()

    def finish(self, *refs):
        local, first, relay, last = self._plan(*refs)
        for cp in last:
            cp.wait_recv()
        for cp in first:
            cp.wait_send()
        for _, onward in relay:
            onward.wait_send()
        for cp in local:
            cp.wait()


def _call(body, *, name, grid, in_specs, out_specs, out_shape, args, scratch=(), comm=None, aliases=None):
    params = _params(("arbitrary",) * len(grid))
    total = math.prod(grid)

    def at(step):
        flat = functools.reduce(lambda acc, dn: acc * dn[1] + pl.program_id(dn[0]), enumerate(grid), 0)
        return flat == step

    if comm is None:
        res = pl.pallas_call(
            body, name=name, grid=grid, in_specs=list(in_specs), out_specs=list(out_specs), out_shape=list(out_shape),
            scratch_shapes=list(scratch), input_output_aliases=aliases or {}, compiler_params=params)(*args)
        return list(res), []
    n_in, n_out, n_scr, cn = len(in_specs), len(out_specs), len(scratch), comm.n

    def hosted(*refs):
        ins, refs = refs[:n_in], refs[n_in:]
        cins, refs = refs[:cn], refs[cn:]
        outs, refs = refs[:n_out], refs[n_out:]
        couts, refs = refs[:cn], refs[cn:]
        scr, sems = refs[:n_scr], refs[n_scr:]

        @pl.when(at(0))
        def _():
            comm.start(cins, couts, *sems)

        body(*ins, *outs, *scr)

        @pl.when(at((3 * total) // 4 if total >= 4 else total - 1))
        def _():
            comm.relay(cins, couts, *sems)

        @pl.when(at(total - 1))
        def _():
            comm.finish(cins, couts, *sems)

    res = pl.pallas_call(
        hosted, name=name, grid=grid, in_specs=list(in_specs) + comm.specs(), out_specs=list(out_specs) + comm.specs(),
        out_shape=list(out_shape) + comm.out_shape(), scratch_shapes=list(scratch) + comm.scratch(),
        input_output_aliases=aliases or {}, compiler_params=params)(*args, *comm.args)
    return list(res[:n_out]), list(res[n_out:])


def _exchange_two_level(blk, small, name):
    _, R, C = blk.shape
    rows = small.shape[0]

    def body(blk_ref, small_ref, stage_ref, out_ref, gath_ref, a_scr, b_scr, t_scr, s1, r1, s3, r3, ss, rs, lsem):
        mx, my, mc = _me()
        me = 4 * mx + 2 * my + mc
        mine = 2 * mx + my
        here, sibling = (mx, my, mc), (mx, my, 1 - mc)

        def rc(src, dst, send, recv, to):
            return pltpu.make_async_remote_copy(src_ref=src, dst_ref=dst, send_sem=send, recv_sem=recv,
                                                device_id=to, device_id_type=MESH)

        own_small = pltpu.make_async_copy(small_ref, gath_ref.at[me], lsem.at[0])
        own_small.start()
        spread = [rc(small_ref, gath_ref.at[me], ss.at[k - 1], rs.at[k - 1], _peer(mx, my, mc, k)) for k in range(1, N_DEV)]
        order = (1, 2, 3, 0)
        to_sib = [rc(blk_ref.at[2 * (mine ^ k) + 1 - mc], stage_ref.at[k], s1.at[k], r1.at[k], sibling) for k in order]
        for cp in spread + to_sib:
            cp.start()
        own = {k: pltpu.make_async_copy(blk_ref.at[2 * (mine ^ k) + mc], a_scr.at[k], lsem.at[1 + k]) for k in order}
        for k in order:
            own[k].start()
        onward = []
        for k in order:
            rc(blk_ref.at[0], stage_ref.at[k], s1.at[k], r1.at[k], here).wait_recv()
            landed = pltpu.make_async_copy(stage_ref.at[k], b_scr.at[k], lsem.at[5 + k])
            landed.start()
            landed.wait()
            own[k].wait()
            t_scr[k] = (a_scr[k].astype(F32) + b_scr[k].astype(F32)).astype(BF16)
            if k > 0:
                cp = rc(t_scr.at[k], out_ref.at[mine], s3.at[k - 1], r3.at[k - 1], (mx ^ (k >> 1), my ^ (k & 1), mc))
                cp.start()
                onward.append(cp)
        keep = pltpu.make_async_copy(t_scr.at[0], out_ref.at[mine], lsem.at[9])
        keep.start()
        for k in range(1, 4):
            rc(t_scr.at[0], out_ref.at[mine ^ k], s3.at[k - 1], r3.at[k - 1], here).wait_recv()
        for k in range(1, N_DEV):
            rc(small_ref, gath_ref.at[me ^ k], ss.at[k - 1], rs.at[k - 1], here).wait_recv()
        for cp in spread + to_sib + onward:
            cp.wait_send()
        keep.wait()
        own_small.wait()

    any_spec = pl.BlockSpec(memory_space=pl.ANY)
    dma = pltpu.SemaphoreType.DMA
    _, out, gath = pl.pallas_call(
        body, name=name,
        in_specs=[any_spec, any_spec], out_specs=[any_spec] * 3,
        out_shape=[_sds((4, R, C), BF16), _sds((4, R, C), BF16), _sds((N_DEV, rows, D), F32)],
        scratch_shapes=[pltpu.VMEM((4, R, C), BF16)] * 3
                       + [dma((4,)), dma((4,)), dma((3,)), dma((3,)), dma((N_DEV - 1,)), dma((N_DEV - 1,)), dma((10,))],
        compiler_params=pltpu.CompilerParams(vmem_limit_bytes=VMEM_LIMIT),
    )(blk, small)
    return out, gath


def _exchange_rows(x_ref, out_ref, send_sems, recv_sems, between=None):
    mx, my, mc = _me()
    me = 4 * mx + 2 * my + mc
    out_ref[pl.ds(me, 1)] = x_ref[...][None]
    sends = []
    for k in range(1, N_DEV):
        cp = pltpu.make_async_remote_copy(
            src_ref=x_ref, dst_ref=out_ref.at[me], send_sem=send_sems.at[k - 1], recv_sem=recv_sems.at[k - 1],
            device_id=_peer(mx, my, mc, k), device_id_type=MESH)
        cp.start()
        sends.append(cp)
    if between is not None:
        between()
    for k in range(1, N_DEV):
        pltpu.make_async_remote_copy(
            src_ref=x_ref, dst_ref=out_ref.at[me ^ k], send_sem=send_sems.at[k - 1], recv_sem=recv_sems.at[k - 1],
            device_id=(mx, my, mc), device_id_type=MESH).wait_recv()
    for cp in sends:
        cp.wait_send()


def _prologue(c8, cctx8, w_ada, b_my, comm):
    nw = w_ada.shape[1]

    cn = comm.n

    def body(*refs):
        c_ref, cctx_ref, w_ref, b_ref = refs[:4]
        cins, refs = refs[4:4 + cn], refs[4 + cn:]
        act_ref, mod_ref = refs[:2]
        couts, refs = refs[2:2 + cn], refs[2 + cn:]
        cmine_scr, call_scr, mine_scr, mall_scr, s1, r1, s2, r2 = refs[:8]
        csems = refs[8:]
        cmine_scr[...] = c_ref[...]
        _exchange_rows(cmine_scr, call_scr, s1, r1)
        rows = [call_scr[d][0:1, :] for d in range(N_DEV)] + [cctx_ref[0:1, :], jnp.zeros((7, D), F32)]
        s = jnp.concatenate(rows, axis=0)
        act = s * _sigmoid(s)
        act_ref[...] = act
        mine_scr[...] = jnp.dot(act.astype(BF16), w_ref[...].astype(BF16), preferred_element_type=F32) + b_ref[...]
        _exchange_rows(mine_scr, mall_scr, s2, r2, between=lambda: comm.start(cins, couts, *csems))
        mod_ref[...] = mall_scr[...]
        comm.relay(cins, couts, *csems)
        comm.finish(cins, couts, *csems)

    sems = [pltpu.SemaphoreType.DMA((N_DEV - 1,))] * 4
    res = pl.pallas_call(
        body, name="prologue", grid=(1,),
        in_specs=[_full((8, D)), _full((8, D)), _full((D, nw)), _full((1, nw))] + comm.specs(),
        out_specs=[_full((16, D)), _full((N_DEV, 16, nw))] + comm.specs(),
        out_shape=[_sds((16, D), F32), _sds((N_DEV, 16, nw), F32)] + comm.out_shape(),
        scratch_shapes=[pltpu.VMEM((8, D), F32), pltpu.VMEM((N_DEV, 8, D), F32), pltpu.VMEM((16, nw), F32),
                        pltpu.VMEM((N_DEV, 16, nw), F32)] + sems + comm.scratch(),
        compiler_params=_params(("arbitrary",)),
    )(c8, cctx8, w_ada, b_my, *comm.args)
    return res[0], res[1], list(res[2:])


def _gather_rows(x, name):
    def body(x_ref, out_ref, send_sems, recv_sems):
        _exchange_rows(x_ref, out_ref, send_sems, recv_sems)

    return pl.pallas_call(
        body, name=name,
        out_shape=_sds((N_DEV,) + x.shape, x.dtype),
        in_specs=[pl.BlockSpec(memory_space=pltpu.VMEM)],
        out_specs=pl.BlockSpec(memory_space=pltpu.VMEM),
        scratch_shapes=[pltpu.SemaphoreType.DMA((N_DEV - 1,)), pltpu.SemaphoreType.DMA((N_DEV - 1,))],
        compiler_params=pltpu.CompilerParams(vmem_limit_bytes=VMEM_LIMIT),
    )(x)


def _ada_bwd(act, dmod_my, w_ada, m, v, tr=256):
    nw = w_ada.shape[1]

    def body(act_ref, dm_ref, w_ref, m_ref, v_ref, g_ref, d_ref, m2_ref, v2_ref, pc_ref):
        dm = dm_ref[...].astype(BF16)
        g = lax.dot_general(act_ref[...].astype(BF16), dm, TN, preferred_element_type=F32)
        w = w_ref[...]
        delta, m2, v2 = _adamw(w, g, m_ref[...], v_ref[...])
        g_ref[...] = g
        d_ref[...] = delta
        m2_ref[...] = m2
        v2_ref[...] = v2
        pc_ref[...] = lax.dot_general(dm[8:16, :], w.astype(BF16), NT, preferred_element_type=F32)

    wspec = _row(tr, nw)
    return pl.pallas_call(
        body, name="ada_bwd", grid=(D // tr,),
        in_specs=[pl.BlockSpec((16, tr), lambda i: (0, i)), _full((16, nw)), wspec, wspec, wspec],
        out_specs=[wspec, wspec, wspec, wspec, pl.BlockSpec((8, tr), lambda i: (0, i))],
        out_shape=[_sds((D, nw), F32)] * 4 + [_sds((8, D), F32)],
        compiler_params=_params(("arbitrary",)),
    )(act, dmod_my, w_ada, m, v)


def _k_in(x, modv, w_in, cos, sin, tm, comm=None):
    L = x.shape[0]

    def body(x_ref, mod_ref, w_ref, cos_ref, sin_ref, h_ref, q_ref, k_ref, v_ref, u_ref, vb_ref, ga_ref, gb_ref):
        n, _ = _ln(x_ref[...])
        h = (n * (1.0 + mod_ref[1:2, :]) + mod_ref[0:1, :]).astype(BF16)
        h_ref[...] = h
        c, s = cos_ref[...], sin_ref[...]

        def proj(lo, width):
            return lax.dot_general(h, w_ref[lo:lo + width, :], NT, preferred_element_type=F32)

        for i in range(4):
            q_ref[:, i * 128:(i + 1) * 128] = (_rope(proj(O_Q + i * 128, 128), c, s) * Q_SCALE).astype(BF16)
        k_ref[...] = _rope(proj(O_K, KV_W), c, s).astype(BF16)
        v_ref[...] = proj(O_V, KV_W).astype(BF16)
        u_ref[...] = proj(O_U, GM_W).astype(BF16)
        vb_ref[...] = proj(O_VB, GM_W).astype(BF16)
        ga_ref[...] = proj(O_GA, D).astype(BF16)
        gb_ref[...] = proj(O_GB, D).astype(BF16)

    widths = [D, Q_W, KV_W, KV_W, GM_W, GM_W, D, D]
    return _call(
        body, name="fwd_in", grid=(L // tm,),
        in_specs=[_row(tm, D), _full((8, D)), _resident((IN_W, D)), _row(tm, 128), _row(tm, 128)],
        out_specs=[_row(tm, w) for w in widths],
        out_shape=[_sds((L, w), BF16) for w in widths],
        args=(x, modv, w_in, cos, sin), comm=comm)


def _k_ctx(ctx, modc, w_kv):
    C = ctx.shape[0]

    def body(c_ref, mod_ref, w_ref, hc_ref, kc_ref, vc_ref):
        n, _ = _ln(c_ref[...])
        hc = (n * (1.0 + mod_ref[1:2, :]) + mod_ref[0:1, :]).astype(BF16)
        hc_ref[...] = hc
        kv = lax.dot_general(hc, w_ref[...], NT, preferred_element_type=F32)
        kc_ref[...] = kv[:, :KV_W].astype(BF16)
        vc_ref[...] = kv[:, KV_W:].astype(BF16)

    return pl.pallas_call(
        body, name="fwd_ctx", grid=(1,),
        in_specs=[_full((C, D)), _full((8, D)), _full((2 * KV_W, D))],
        out_specs=[_full((C, D)), _full((C, KV_W)), _full((C, KV_W))],
        out_shape=[_sds((C, D), BF16), _sds((C, KV_W), BF16), _sds((C, KV_W), BF16)],
        compiler_params=_params(("arbitrary",)),
    )(ctx, modc, w_kv)


def _attn_bias():
    r = (np.arange(GQA_GROUP * BLK) & (BLK - 1))[:, None]
    j = np.arange(3 * BLK)[None, :]
    band = np.abs(j - BLK - r) <= BLK
    variants = [band & (j >= BLK), band, band & (j < 2 * BLK)]
    return jnp.asarray(np.stack([np.where(v, 0.0, NEG_INF) for v in variants]), F32)


def _masked(s, bias, C):
    return jnp.concatenate([s[:, :C], s[:, C:] + bias], axis=1)


def _sink_col(sink_ref, hk):
    grp = lax.broadcasted_iota(jnp.int32, (GQA_GROUP * BLK, 1), 0) >> 7
    col = jnp.full((GQA_GROUP * BLK, 1), sink_ref[hk * GQA_GROUP], F32)
    for g in range(1, GQA_GROUP):
        col = jnp.where(grp == g, sink_ref[hk * GQA_GROUP + g], col)
    return col


ATTN_FWD_BLOCKS = 4


def _k_attn(sink, q, k, v, kc, vc, bias, comm=None):
    L = q.shape[0]
    C = kc.shape[0]
    nb = L // BLK
    nq = min(ATTN_FWD_BLOCKS, nb)
    steps = nb // nq

    def body(sink_ref, q_ref, kp_ref, km_ref, kx_ref, vp_ref, vm_ref, vx_ref, kc_ref, vc_ref, bias_ref, ya_ref, lse_ref):
        i = pl.program_id(0)
        chains = [(qb, hk) for qb in range(nq) for hk in range(N_KV_HEADS)]

        def band(qb):
            first = jnp.where(i == 0, 0, 1) if qb == 0 else 1
            return bias_ref[jnp.where(i == steps - 1, 2, first) if qb == nq - 1 else first]

        def keys(ctx_ref, p_ref, m_ref, x_ref, qb, hk):
            sl = slice(hk * HEAD_DIM, (hk + 1) * HEAD_DIM)
            blocks = [p_ref[:, sl]] + [m_ref[j * BLK:(j + 1) * BLK, sl] for j in range(nq)] + [x_ref[:, sl]]
            return jnp.concatenate([ctx_ref[:, sl]] + blocks[qb:qb + 3], axis=0)

        def queries(qb, hk):
            return jnp.concatenate(
                [q_ref[qb * BLK:(qb + 1) * BLK, (hk * GQA_GROUP + g) * HEAD_DIM:(hk * GQA_GROUP + g + 1) * HEAD_DIM]
                 for g in range(GQA_GROUP)], axis=0)

        def scores(qb, hk):
            return _masked(lax.dot_general(queries(qb, hk), keys(kc_ref, kp_ref, km_ref, kx_ref, qb, hk), NT,
                                           preferred_element_type=F32), band(qb), C)

        ahead = 2
        s = [scores(*c) for c in chains[:ahead]]
        for n, (qb, hk) in enumerate(chains):
            if n + ahead < len(chains):
                s.append(scores(*chains[n + ahead]))
            s_ = s[n]
            sink_c = _sink_col(sink_ref, hk)
            m = jnp.maximum(jnp.max(s_, axis=1, keepdims=True), sink_c)
            p = jnp.exp(s_ - m)
            den = jnp.sum(p, axis=1, keepdims=True) + jnp.exp(sink_c - m)
            o = jnp.dot(p.astype(BF16), keys(vc_ref, vp_ref, vm_ref, vx_ref, qb, hk), preferred_element_type=F32) * (1.0 / den)
            lse = m + jnp.log(den)
            rows = slice(qb * BLK, (qb + 1) * BLK)
            for g in range(GQA_GROUP):
                h = hk * GQA_GROUP + g
                ya_ref[rows, h * HEAD_DIM:(h + 1) * HEAD_DIM] = o[g * BLK:(g + 1) * BLK, :].astype(BF16)
                lse_ref[rows, h:h + 1] = lse[g * BLK:(g + 1) * BLK, :]

    kv3 = [pl.BlockSpec((BLK, KV_W), lambda i: (jnp.maximum(nq * i - 1, 0), 0)),
           pl.BlockSpec((nq * BLK, KV_W), lambda i: (i, 0)),
           pl.BlockSpec((BLK, KV_W), lambda i: (jnp.minimum(nq * i + nq, nb - 1), 0))]
    return _call(
        body, name="fwd_attn", grid=(steps,),
        in_specs=[pl.BlockSpec(memory_space=pltpu.SMEM), _row(nq * BLK, Q_W)] + kv3 + kv3
                 + [_full((C, KV_W)), _full((C, KV_W)), _full((3, GQA_GROUP * BLK, 3 * BLK))],
        out_specs=[_row(nq * BLK, Q_W), _row(nq * BLK, N_Q_HEADS)],
        out_shape=[_sds((L, Q_W), BF16), _sds((L, N_Q_HEADS), F32)],
        args=(sink, q, k, k, k, v, v, v, kc, vc, bias), comm=comm)


GMLP_CHUNKS = 4


def _split_pair(t):
    low = lax.broadcasted_iota(jnp.int32, t.shape, 1) < GROUP_DIM
    zero = jnp.zeros_like(t)
    return jnp.where(low, t, zero), jnp.where(low, zero, t)


def _gmlp_spatial(w_ref, t_b, nch):
    rows = []
    for c in range(nch):
        tiles = []
        for pr in range(N_GROUPS // 2):
            lo, hi = _split_pair(t_b[c * BLK:(c + 1) * BLK, pr * 128:(pr + 1) * 128])
            tiles.append(jnp.dot(w_ref[2 * pr], lo, preferred_element_type=F32)
                         + jnp.dot(w_ref[2 * pr + 1], hi, preferred_element_type=F32))
        rows.append(jnp.concatenate(tiles, axis=1))
    return jnp.concatenate(rows, axis=0)


def _gmlp_fwd_vals(u, vb, lnv_ref, ws_ref, bsp_ref, nch):
    uf = u.astype(F32)
    vf = vb.astype(F32)
    gu, tu = _gelu(uf)
    gv, tv = _gelu(vf)
    vhat, rstd = _ln(gv)
    vn = (vhat * lnv_ref[0:1, :] + lnv_ref[1:2, :]).astype(BF16)
    s = _gmlp_spatial(ws_ref, vn, nch) + jnp.concatenate([bsp_ref[...]] * nch, axis=0)
    return uf, vf, gu, tu, tv, vhat, rstd, vn, s


def _k_gmlp(u, vb, lnv, ws, bsp):
    L = u.shape[0]
    nch = min(GMLP_CHUNKS, L // BLK)
    tm = nch * BLK

    def body(u_ref, vb_ref, lnv_ref, ws_ref, bsp_ref, yb_ref):
        _, _, gu, _, _, _, _, _, s = _gmlp_fwd_vals(u_ref[...], vb_ref[...], lnv_ref, ws_ref, bsp_ref, nch)
        yb_ref[...] = (gu * s).astype(BF16)

    return pl.pallas_call(
        body, name="fwd_gmlp", grid=(L // tm,),
        in_specs=[_row(tm, GM_W), _row(tm, GM_W), _full((8, GM_W)), _full((N_GROUPS, BLK, BLK)), _full((BLK, GM_W))],
        out_specs=_row(tm, GM_W),
        out_shape=_sds((L, GM_W), BF16),
        compiler_params=_params(("arbitrary",)),
    )(u, vb, lnv, ws, bsp)


def _k_merge(x, ya, yb, ga, gb, w_a, w_b, w_o, modv, lnv, tm):
    L = x.shape[0]

    def body(x_ref, ya_ref, yb_ref, ga_ref, gb_ref, wa_ref, wb_ref, wo_ref, mod_ref, ln_ref,
             mg_ref, mix_ref, xm_ref, h2_ref):
        a = jnp.dot(ya_ref[...], wa_ref[...], preferred_element_type=F32)
        b = jnp.dot(yb_ref[...], wb_ref[...], preferred_element_type=F32)
        merged = (_sigmoid(ga_ref[...].astype(F32)) * a + _sigmoid(gb_ref[...].astype(F32)) * b).astype(BF16)
        mg_ref[...] = merged
        mix = jnp.dot(merged, wo_ref[...], preferred_element_type=F32)
        mix_ref[...] = mix.astype(BF16)
        r1 = ALPHA * x_ref[...] + mod_ref[2:3, :] * mix
        r1hat, _ = _ln(r1)
        xm = r1hat * ln_ref[0:1, :] + ln_ref[1:2, :]
        xm_ref[...] = xm
        n2, _ = _ln(xm)
        h2_ref[...] = (n2 * (1.0 + mod_ref[4:5, :]) + mod_ref[3:4, :]).astype(BF16)

    return pl.pallas_call(
        body, name="fwd_merge", grid=(L // tm,),
        in_specs=[_row(tm, D), _row(tm, Q_W), _row(tm, GM_W), _row(tm, D), _row(tm, D),
                  _resident((Q_W, D)), _resident((GM_W, D)), _resident((D, D)), _full((8, D)), _full((8, D))],
        out_specs=[_row(tm, D)] * 4,
        out_shape=[_sds((L, D), BF16), _sds((L, D), BF16), _sds((L, D), F32), _sds((L, D), BF16)],
        compiler_params=_params(("arbitrary",)),
    )(x, ya, yb, ga, gb, w_a, w_b, w_o, modv, lnv)


FFN_CH = FFN_H // 2


def _k_ffn(h2, xm, tgt, w_fi, w_fo, modv, lnv, tm):
    L = h2.shape[0]

    def body(h2_ref, xm_ref, t_ref, wi_ref, wo_ref, mod_ref, ln_ref, gate_ref, up_ref, a_ref, dr2_ref, df_ref, acc_ref):
        @pl.when(pl.program_id(0) == 0)
        def _():
            acc_ref[...] = jnp.zeros_like(acc_ref)

        h2v = h2_ref[...]
        f = jnp.zeros((tm, D), F32)
        for j in range(FFN_H // FFN_CH):
            lo = j * FFN_CH
            gate = lax.dot_general(h2v, wi_ref[lo:lo + FFN_CH, :], NT, preferred_element_type=F32)
            up = lax.dot_general(h2v, wi_ref[FFN_H + lo:FFN_H + lo + FFN_CH, :], NT, preferred_element_type=F32)
            act = (gate * _sigmoid(gate) * up).astype(BF16)
            gate_ref[:, lo:lo + FFN_CH] = gate.astype(BF16)
            up_ref[:, lo:lo + FFN_CH] = up.astype(BF16)
            a_ref[:, lo:lo + FFN_CH] = act
            f = f + jnp.dot(act, wo_ref[lo:lo + FFN_CH, :], preferred_element_type=F32)
        gate2 = mod_ref[5:6, :]
        r2 = ALPHA * xm_ref[...] + gate2 * f
        r2hat, rstd = _ln(r2)
        y = r2hat * ln_ref[2:3, :] + ln_ref[3:4, :]
        err = y - t_ref[...]
        dy = err * (1.0 / D)
        dr2 = _ln_bwd(dy * ln_ref[2:3, :], r2hat, rstd)
        dr2_ref[...] = dr2
        df_ref[...] = (gate2 * dr2).astype(BF16)
        acc_ref[0:1, :] += _colsum(dy * r2hat)
        acc_ref[1:2, :] += _colsum(dy)
        acc_ref[2:3, :] += _colsum(dr2 * f)
        acc_ref[3:4, :] += _colsum(err * err) * (0.5 / D)

    return pl.pallas_call(
        body, name="fwd_ffn", grid=(L // tm,),
        in_specs=[_row(tm, D), _row(tm, D), _row(tm, D), _resident((2 * FFN_H, D)), _resident((FFN_H, D)),
                  _full((8, D)), _full((8, D))],
        out_specs=[_row(tm, FFN_H)] * 3 + [_row(tm, D), _row(tm, D), _full((8, D))],
        out_shape=[_sds((L, FFN_H), BF16)] * 3 + [_sds((L, D), F32), _sds((L, D), BF16), _sds((8, D), F32)],
        compiler_params=_params(("arbitrary",)),
    )(h2, xm, tgt, w_fi, w_fo, modv, lnv)


FFN_CH_BWD = FFN_CH


def _k_ffn_bwd(df, gate, up, xm, dr2, x, mix, w_fi, w_fo, modv, lnv, tm):
    L = df.shape[0]

    def body(df_ref, gate_ref, up_ref, xm_ref, dr2_ref, x_ref, mix_ref, wi_ref, wo_ref, mod_ref, ln_ref,
             dF_ref, dmix_ref, dxp_ref, acc_ref):
        @pl.when(pl.program_id(0) == 0)
        def _():
            acc_ref[...] = jnp.zeros_like(acc_ref)

        dfv = df_ref[...]
        ch = FFN_CH_BWD
        chunks = [j * ch for j in range(FFN_H // ch)]
        das = [lax.dot_general(dfv, wo_ref[lo:lo + ch, :], NT, preferred_element_type=F32) for lo in chunks]
        n2, rstd2 = _ln(xm_ref[...])
        mixf = mix_ref[...].astype(F32)
        gate1 = mod_ref[2:3, :]
        r1hat, rstd1 = _ln(ALPHA * x_ref[...] + gate1 * mixf)
        dh2 = jnp.zeros((tm, D), F32)
        for lo, da in zip(chunks, das):
            gate = gate_ref[:, lo:lo + ch].astype(F32)
            upv = up_ref[:, lo:lo + ch].astype(F32)
            sg = _sigmoid(gate)
            d_gate = (da * upv * (sg * (1.0 + gate * (1.0 - sg)))).astype(BF16)
            d_up = (da * (gate * sg)).astype(BF16)
            dF_ref[:, lo:lo + ch] = d_gate
            dF_ref[:, FFN_H + lo:FFN_H + lo + ch] = d_up
            dh2 = dh2 + jnp.dot(d_gate, wi_ref[lo:lo + ch, :], preferred_element_type=F32)
            dh2 = dh2 + jnp.dot(d_up, wi_ref[FFN_H + lo:FFN_H + lo + ch, :], preferred_element_type=F32)
        acc_ref[0:1, :] += _colsum(dh2)
        acc_ref[1:2, :] += _colsum(dh2 * n2)
        dxm = ALPHA * dr2_ref[...] + _ln_bwd(dh2 * (1.0 + mod_ref[4:5, :]), n2, rstd2)
        acc_ref[2:3, :] += _colsum(dxm * r1hat)
        acc_ref[3:4, :] += _colsum(dxm)
        dr1 = _ln_bwd(dxm * ln_ref[0:1, :], r1hat, rstd1)
        dmix_ref[...] = (gate1 * dr1).astype(BF16)
        dxp_ref[...] = ALPHA * dr1
        acc_ref[4:5, :] += _colsum(dr1 * mixf)

    return pl.pallas_call(
        body, name="bwd_ffn", grid=(L // tm,),
        in_specs=[_row(tm, D), _row(tm, FFN_H), _row(tm, FFN_H), _row(tm, D), _row(tm, D), _row(tm, D), _row(tm, D),
                  _resident((2 * FFN_H, D)), _resident((FFN_H, D)), _full((8, D)), _full((8, D))],
        out_specs=[_row(tm, 2 * FFN_H), _row(tm, D), _row(tm, D), _full((8, D))],
        out_shape=[_sds((L, 2 * FFN_H), BF16), _sds((L, D), BF16), _sds((L, D), F32), _sds((8, D), F32)],
        compiler_params=_params(("arbitrary",)),
    )(df, gate, up, xm, dr2, x, mix, w_fi, w_fo, modv, lnv)


def _k_merge_bwd(dmix, merged, ya, yb, ga, gb, w_a, w_b, w_o, tm):
    L = dmix.shape[0]
    n = L // tm

    def body(dmix_ref, mg_ref, ya_ref, yb_ref, ga_ref, gb_ref, wa_ref, wb_ref, wo_ref,
             dga_ref, dgb_ref, dya_ref, dyb_ref, gwa_ref, gwb_ref, gwo_ref, acc_a, acc_b, acc_o):
        i = pl.program_id(0)

        @pl.when(i == 0)
        def _():
            for r in (acc_a, acc_b, acc_o):
                r[...] = jnp.zeros_like(r)

        dmixv = dmix_ref[...]
        dmg = lax.dot_general(dmixv, wo_ref[...], NT, preferred_element_type=F32)
        acc_o[...] += lax.dot_general(mg_ref[...], dmixv, TN, preferred_element_type=F32)
        ya = ya_ref[...]
        a = jnp.dot(ya, wa_ref[...], preferred_element_type=F32)
        sa = _sigmoid(ga_ref[...].astype(F32))
        dA = (dmg * sa).astype(BF16)
        dga_ref[...] = (dmg * a * (sa * (1.0 - sa))).astype(BF16)
        dya_ref[...] = lax.dot_general(dA, wa_ref[...], NT, preferred_element_type=F32).astype(BF16)
        acc_a[...] += lax.dot_general(ya, dA, TN, preferred_element_type=F32)
        yb = yb_ref[...]
        b = jnp.dot(yb, wb_ref[...], preferred_element_type=F32)
        sb = _sigmoid(gb_ref[...].astype(F32))
        dB = (dmg * sb).astype(BF16)
        dgb_ref[...] = (dmg * b * (sb * (1.0 - sb))).astype(BF16)
        dyb_ref[...] = lax.dot_general(dB, wb_ref[...], NT, preferred_element_type=F32).astype(BF16)
        acc_b[...] += lax.dot_general(yb, dB, TN, preferred_element_type=F32)

        @pl.when(i == n - 1)
        def _():
            gwa_ref[...] = acc_a[...].astype(BF16)
            gwb_ref[...] = acc_b[...].astype(BF16)
            gwo_ref[...] = acc_o[...].astype(BF16)

    return pl.pallas_call(
        body, name="bwd_merge", grid=(n,),
        in_specs=[_row(tm, D), _row(tm, D), _row(tm, Q_W), _row(tm, GM_W), _row(tm, D), _row(tm, D),
                  _resident((Q_W, D)), _resident((GM_W, D)), _resident((D, D))],
        out_specs=[_row(tm, D), _row(tm, D), _row(tm, Q_W), _row(tm, GM_W), _full((Q_W, D)), _full((GM_W, D)), _full((D, D))],
        out_shape=[_sds((L, D), BF16), _sds((L, D), BF16), _sds((L, Q_W), BF16), _sds((L, GM_W), BF16),
                   _sds((Q_W, D), BF16), _sds((GM_W, D), BF16), _sds((D, D), BF16)],
        scratch_shapes=[pltpu.VMEM((Q_W, D), F32), pltpu.VMEM((GM_W, D), F32), pltpu.VMEM((D, D), F32)],
        compiler_params=_params(("arbitrary",)),
    )(dmix, merged, ya, yb, ga, gb, w_a, w_b, w_o)


def _k_gmlp_bwd(u, vb, dyb, lnv, ws, wst, bsp):
    L = u.shape[0]
    nch = min(GMLP_CHUNKS, L // BLK)
    tm = nch * BLK

    def body(u_ref, vb_ref, dyb_ref, lnv_ref, ws_ref, wst_ref, bsp_ref, du_ref, dvb_ref, gws_ref, gbst_ref, gln_ref):
        @pl.when(pl.program_id(0) == 0)
        def _():
            gws_ref[...] = jnp.zeros_like(gws_ref)
            gbst_ref[...] = jnp.zeros_like(gbst_ref)
            gln_ref[...] = jnp.zeros_like(gln_ref)

        uf, vf, gu, tu, tv, vhat, rstd, vn, s = _gmlp_fwd_vals(u_ref[...], vb_ref[...], lnv_ref, ws_ref, bsp_ref, nch)
        dyb_f = dyb_ref[...].astype(F32)
        du_ref[...] = (dyb_f * s * _gelu_grad(uf, tu)).astype(BF16)
        ds = dyb_f * gu
        ds_b = ds.astype(BF16)
        for pr in range(N_GROUPS // 2):
            lanes = slice(pr * 128, (pr + 1) * 128)
            gw_lo = gw_hi = ds_sum = None
            for c in range(nch):
                rows = slice(c * BLK, (c + 1) * BLK)
                lo, hi = _split_pair(ds_b[rows, lanes])
                t_lo = lax.dot_general(lo, vn[rows, lanes], NT, preferred_element_type=F32)
                t_hi = lax.dot_general(hi, vn[rows, lanes], NT, preferred_element_type=F32)
                gw_lo = t_lo if c == 0 else gw_lo + t_lo
                gw_hi = t_hi if c == 0 else gw_hi + t_hi
                ds_sum = ds[rows, lanes] if c == 0 else ds_sum + ds[rows, lanes]
            gws_ref[2 * pr] += gw_lo
            gws_ref[2 * pr + 1] += gw_hi
            b_lo, b_hi = _split_pair(ds_sum)
            gbst_ref[:, 2 * pr:2 * pr + 1] += jnp.sum(b_lo, axis=1, keepdims=True)
            gbst_ref[:, 2 * pr + 1:2 * pr + 2] += jnp.sum(b_hi, axis=1, keepdims=True)
        dvn = _gmlp_spatial(wst_ref, ds_b, nch)
        gln_ref[0:1, :] += _colsum(dvn * vhat)
        gln_ref[1:2, :] += _colsum(dvn)
        dgv = _ln_bwd(dvn * lnv_ref[0:1, :], vhat, rstd)
        dvb_ref[...] = (dgv * _gelu_grad(vf, tv)).astype(BF16)

    return pl.pallas_call(
        body, name="bwd_gmlp", grid=(L // tm,),
        in_specs=[_row(tm, GM_W)] * 3 + [_full((8, GM_W)), _full((N_GROUPS, BLK, BLK)), _full((N_GROUPS, BLK, BLK)),
                                         _full((BLK, GM_W))],
        out_specs=[_row(tm, GM_W), _row(tm, GM_W), _full((N_GROUPS, BLK, BLK)), _full((BLK, N_GROUPS)), _full((8, GM_W))],
        out_shape=[_sds((L, GM_W), BF16), _sds((L, GM_W), BF16), _sds((N_GROUPS, BLK, BLK), F32),
                   _sds((BLK, N_GROUPS), F32), _sds((8, GM_W), F32)],
        compiler_params=_params(("arbitrary",)),
    )(u, vb, dyb, lnv, ws, wst, bsp)


ATTN_BWD_BLOCKS = 2


def _k_attn_bwd(sink, q, k, v, kc, vc, dya, lse, cos, sin, bias, comm=None):
    L = q.shape[0]
    C = kc.shape[0]
    nb = L // BLK
    nq = min(ATTN_BWD_BLOCKS, nb)
    steps = nb // nq
    NK = C + 3 * BLK
    chains = [(qb, hk) for qb in range(nq) for hk in range(N_KV_HEADS)]

    def body(sink_ref, q_ref, kp_ref, km_ref, kx_ref, vp_ref, vm_ref, vx_ref, kc_ref, vc_ref, do_ref, lse_ref,
             cq_ref, sq_ref, cl_ref, sl_ref, bias_ref,
             dq_ref, dk_ref, dv_ref, dkc_ref, dvc_ref, dsink_ref,
             dq_scr, ck_scr, cv_scr, k1_acc, k2_acc, v1_acc, v2_acc):
        i = pl.program_id(0)

        @pl.when(i == 0)
        def _():
            for r in (k1_acc, k2_acc, v1_acc, v2_acc, dkc_ref, dvc_ref, dsink_ref):
                r[...] = jnp.zeros_like(r)

        @pl.when(i < steps)
        def _():
            def band(qb):
                first = jnp.where(i == 0, 0, 1) if qb == 0 else 1
                return bias_ref[jnp.where(i == steps - 1, 2, first) if qb == nq - 1 else first]

            def lanes(hk):
                return slice(hk * HEAD_DIM, (hk + 1) * HEAD_DIM)

            def keys(ctx_ref, p_ref, m_ref, x_ref, qb, hk):
                sl = lanes(hk)
                blocks = [p_ref[:, sl]] + [m_ref[j * BLK:(j + 1) * BLK, sl] for j in range(nq)] + [x_ref[:, sl]]
                return jnp.concatenate([ctx_ref[:, sl]] + blocks[qb:qb + 3], axis=0)

            def stacked(ref, qb, hk, width):
                return jnp.concatenate(
                    [ref[qb * BLK:(qb + 1) * BLK, (hk * GQA_GROUP + g) * width:(hk * GQA_GROUP + g + 1) * width]
                     for g in range(GQA_GROUP)], axis=0)

            def scores(qb, hk):
                kcat = keys(kc_ref, kp_ref, km_ref, kx_ref, qb, hk)
                qg = stacked(q_ref, qb, hk, HEAD_DIM)
                s = _masked(lax.dot_general(qg, kcat, NT, preferred_element_type=F32), band(qb), C)
                dog = stacked(do_ref, qb, hk, HEAD_DIM)
                dp = lax.dot_general(dog, keys(vc_ref, vp_ref, vm_ref, vx_ref, qb, hk), NT, preferred_element_type=F32)
                return kcat, qg, dog, s, dp

            def softmax_bwd(qb, hk, s, dp):
                lse_c = stacked(lse_ref, qb, hk, 1)
                p = jnp.exp(s - lse_c)
                delta = jnp.sum(p * dp, axis=1, keepdims=True)
                ds = (p * (dp - delta)).astype(BF16)
                p_sink = jnp.exp(_sink_col(sink_ref, hk) - lse_c) * delta
                return p.astype(BF16), ds, p_sink

            def put_dq(qb, hk, dqs, p_sink):
                for g in range(GQA_GROUP):
                    h = hk * GQA_GROUP + g
                    dq_scr[qb * BLK:(qb + 1) * BLK, h * HEAD_DIM:(h + 1) * HEAD_DIM] = dqs[g * BLK:(g + 1) * BLK, :]
                    tot = jnp.sum(p_sink[g * BLK:(g + 1) * BLK, :], axis=0, keepdims=True)
                    dsink_ref[h:h + 1, :] -= jnp.broadcast_to(tot, (1, 128))

            ahead = 4
            sc = [scores(*c) for c in chains[:ahead]]
            pending = None
            for n, (qb, hk) in enumerate(chains):
                if n + ahead < len(chains):
                    sc.append(scores(*chains[n + ahead]))
                kcat, qg, dog, s, dp = sc[n]
                pb, ds, p_sink = softmax_bwd(qb, hk, s, dp)
                if pending is not None:
                    pqb, phk, pds, ppb, pqg, pdog = pending
                    ck_scr[pqb, :, lanes(phk)] = lax.dot_general(pds, pqg, TN, preferred_element_type=F32)
                    cv_scr[pqb, :, lanes(phk)] = lax.dot_general(ppb, pdog, TN, preferred_element_type=F32)
                put_dq(qb, hk, jnp.dot(ds, kcat, preferred_element_type=F32), p_sink)
                pending = (qb, hk, ds, pb, qg, dog)
            pqb, phk, pds, ppb, pqg, pdog = pending
            ck_scr[pqb, :, lanes(phk)] = lax.dot_general(pds, pqg, TN, preferred_element_type=F32)
            cq, sq = cq_ref[...], sq_ref[...]
            for j in range(4):
                dq_ref[:, j * 128:(j + 1) * 128] = _unrope(dq_scr[:, j * 128:(j + 1) * 128] * Q_SCALE, cq, sq).astype(BF16)
            cv_scr[pqb, :, lanes(phk)] = lax.dot_general(ppb, pdog, TN, preferred_element_type=F32)
            dkc_ref[...] += functools.reduce(lambda a, b: a + b, [ck_scr[qb, 0:C, :] for qb in range(nq)])
            dvc_ref[...] += functools.reduce(lambda a, b: a + b, [cv_scr[qb, 0:C, :] for qb in range(nq)])

        @pl.when(i >= steps)
        def _():
            ck_scr[...] = jnp.zeros_like(ck_scr)
            cv_scr[...] = jnp.zeros_like(cv_scr)

        def slot(scr, r, carried):
            parts = [scr[qb, C + (r - qb) * BLK:C + (r - qb + 1) * BLK, :] for qb in range(nq) if 0 <= r - qb <= 2]
            total = functools.reduce(lambda a, b: a + b, parts)
            return total if carried is None else carried[...] + total

        for r in range(nq):
            rows = slice(r * BLK, (r + 1) * BLK)
            carried_k, carried_v = ((k1_acc, v1_acc), (k2_acc, v2_acc), (None, None))[min(r, 2)]
            tables = (cl_ref[...], sl_ref[...]) if r == 0 else (cq_ref[(r - 1) * BLK:r * BLK, :], sq_ref[(r - 1) * BLK:r * BLK, :])
            dk_ref[rows, :] = _unrope(slot(ck_scr, r, carried_k), *tables).astype(BF16)
            dv_ref[rows, :] = slot(cv_scr, r, carried_v).astype(BF16)
        k1_acc[...] = slot(ck_scr, nq, None)
        v1_acc[...] = slot(cv_scr, nq, None)
        k2_acc[...] = slot(ck_scr, nq + 1, None)
        v2_acc[...] = slot(cv_scr, nq + 1, None)

    last = steps - 1
    kv3 = [pl.BlockSpec((BLK, KV_W), lambda i: (jnp.clip(nq * i - 1, 0, nb - 1), 0)),
           pl.BlockSpec((nq * BLK, KV_W), lambda i: (jnp.minimum(i, last), 0)),
           pl.BlockSpec((BLK, KV_W), lambda i: (jnp.minimum(nq * i + nq, nb - 1), 0))]
    cur = lambda w: pl.BlockSpec((nq * BLK, w), lambda i: (jnp.minimum(i, last), 0))
    late = lambda w: pl.BlockSpec((BLK, w), lambda i: (jnp.clip(nq * i - 1, 0, nb - 1), 0))
    out2 = lambda w: pl.BlockSpec((nq * BLK, w), lambda i: (i, 0))
    return _call(
        body, name="bwd_attn", grid=(steps + 1,),
        in_specs=[pl.BlockSpec(memory_space=pltpu.SMEM), cur(Q_W)] + kv3 + kv3
                 + [_full((C, KV_W)), _full((C, KV_W)), cur(Q_W), cur(N_Q_HEADS), cur(128), cur(128), late(128), late(128),
                    _full((3, GQA_GROUP * BLK, 3 * BLK))],
        out_specs=[cur(Q_W), out2(KV_W), out2(KV_W), _full((C, KV_W)), _full((C, KV_W)), _full((8, 128))],
        out_shape=[_sds((L, Q_W), BF16), _sds((L + nq * BLK, KV_W), BF16), _sds((L + nq * BLK, KV_W), BF16),
                   _sds((C, KV_W), F32), _sds((C, KV_W), F32), _sds((8, 128), F32)],
        scratch=[pltpu.VMEM((nq * BLK, Q_W), F32), pltpu.VMEM((nq, NK, KV_W), F32), pltpu.VMEM((nq, NK, KV_W), F32)]
                + [pltpu.VMEM((BLK, KV_W), F32)] * 4,
        args=(sink, q, k, k, k, v, v, v, kc, vc, dya, lse, cos, sin, cos, sin, bias), comm=comm)


def _k_ctx_bwd(ctx, modc, hc, dkc, dvc, w_kv):
    C = ctx.shape[0]

    def body(c_ref, mod_ref, hc_ref, dkc_ref, dvc_ref, w_ref, gw_ref, dmod_ref):
        dkv = jnp.concatenate([dkc_ref[...], dvc_ref[...]], axis=1).astype(BF16)
        gw_ref[...] = lax.dot_general(dkv, hc_ref[...], TN, preferred_element_type=F32)
        dhc = jnp.dot(dkv, w_ref[...], preferred_element_type=F32)
        n, _ = _ln(c_ref[...])
        dmod_ref[...] = jnp.zeros_like(dmod_ref)
        dmod_ref[0:1, :] = _colsum(dhc)
        dmod_ref[1:2, :] = _colsum(dhc * n)

    return pl.pallas_call(
        body, name="bwd_ctx", grid=(1,),
        in_specs=[_full((C, D)), _full((8, D)), _full((C, D)), _full((C, KV_W)), _full((C, KV_W)), _full((2 * KV_W, D))],
        out_specs=[_full((2 * KV_W, D)), _full((8, D))],
        out_shape=[_sds((2 * KV_W, D), F32), _sds((8, D), F32)],
        compiler_params=_params(("arbitrary",)),
    )(ctx, modc, hc, dkc, dvc, w_kv)


def _k_in_bwd(dq, dk, dv, du, dvb, dga, dgb, x, dxp, w_in, modv, tm, comm=None):
    L = x.shape[0]
    parts = [(O_Q, Q_W), (O_K, KV_W), (O_V, KV_W), (O_U, GM_W), (O_VB, GM_W), (O_GA, D), (O_GB, D)]

    def body(dq_ref, dk_ref, dv_ref, du_ref, dvb_ref, dga_ref, dgb_ref, x_ref, dxp_ref, w_ref, mod_ref,
             dP_ref, gx_ref, acc_ref):
        @pl.when(pl.program_id(0) == 0)
        def _():
            acc_ref[...] = jnp.zeros_like(acc_ref)

        for (lo, width), r in zip(parts, (dq_ref, dk_ref, dv_ref, du_ref, dvb_ref, dga_ref, dgb_ref)):
            dP_ref[:, lo:lo + width] = r[...]
        n1, rstd1 = _ln(x_ref[...])
        dh = jnp.dot(dP_ref[...], w_ref[...], preferred_element_type=F32)
        acc_ref[0:1, :] += _colsum(dh)
        acc_ref[1:2, :] += _colsum(dh * n1)
        gx_ref[...] = dxp_ref[...] + _ln_bwd(dh * (1.0 + mod_ref[1:2, :]), n1, rstd1)

    return _call(
        body, name="bwd_in", grid=(L // tm,),
        in_specs=[_row(tm, w) for _, w in parts] + [_row(tm, D), _row(tm, D), _resident((IN_W, D)), _full((8, D))],
        out_specs=[_row(tm, IN_W), _row(tm, D), _full((8, D))],
        out_shape=[_sds((L, IN_W), BF16), _sds((L, D), F32), _sds((8, D), F32)],
        args=(dq, dk, dv, du, dvb, dga, dgb, x, dxp, w_in, modv), comm=comm)


def _wgrad(a, b, name, tk, tt, comm=None, extra=None):
    T, K = a.shape
    N = b.shape[1]
    nt = T // tt

    def body(*refs):
        a_ref, b_ref = refs[:2]
        o_ref, acc_ref = refs[-2:]
        j, t = pl.program_id(0), pl.program_id(1)

        @pl.when(t == 0)
        def _():
            acc_ref[...] = jnp.zeros_like(acc_ref)

        acc_ref[...] += lax.dot_general(a_ref[...], b_ref[...], TN, preferred_element_type=F32)

        if extra is not None:
            lo, rows = extra[0] % tk, extra[1].shape[0]

            @pl.when((t == nt - 1) & (j == extra[0] // tk))
            def _():
                acc_ref[lo:lo + rows, :] += refs[2][...]

        @pl.when(t == nt - 1)
        def _():
            o_ref[...] = acc_ref[...].astype(BF16)

    extra_specs = [] if extra is None else [pl.BlockSpec(extra[1].shape, lambda j, t: (0, 0))]
    (out,), got = _call(
        body, name=name, grid=(K // tk, nt),
        in_specs=[pl.BlockSpec((tt, tk), lambda j, t: (t, j)), pl.BlockSpec((tt, N), lambda j, t: (t, 0))] + extra_specs,
        out_specs=[pl.BlockSpec((tk, N), lambda j, t: (j, 0))],
        out_shape=[_sds((K, N), BF16)],
        scratch=[pltpu.VMEM((tk, N), F32)],
        args=(a, b) + (() if extra is None else (extra[1],)), comm=comm)
    return (out, got) if comm is not None else out


def _adamw_reduce(parts, w, m, v, name, tr):
    R, C = w.shape
    n_parts = parts.shape[0]

    def body(p_ref, w_ref, m_ref, v_ref, g_ref, d_ref, m2_ref, v2_ref):
        g = p_ref[0].astype(F32)
        for i in range(1, n_parts):
            g = g + p_ref[i].astype(F32)
        delta, m2, v2 = _adamw(w_ref[...], g, m_ref[...], v_ref[...])
        g_ref[...] = g
        d_ref[...] = delta
        m2_ref[...] = m2
        v2_ref[...] = v2

    spec = _row(tr, C)
    return pl.pallas_call(
        body, name=name, grid=(R // tr,),
        in_specs=[pl.BlockSpec((n_parts, tr, C), lambda i: (0, i, 0)), spec, spec, spec],
        out_specs=[spec] * 4,
        out_shape=[_sds((R, C), F32)] * 4,
        compiler_params=_params(("arbitrary",)),
    )(parts, w, m, v)


SMALL_ORDER = ("b_ada", "ln1_g", "ln1_b", "ln2_g", "ln2_b", "gmlp_ln_g", "gmlp_ln_b", "b_spatial", "attn_sink")


def _small_step(gath, params):
    flat = [a for name in SMALL_ORDER for a in params[name]]

    def grad_of(tot, name):
        if name == "b_ada":
            return jnp.concatenate([tot[r:r + 1, :] for r in range(6)], axis=1)
        if name in ("ln1_g", "ln1_b", "ln2_g", "ln2_b"):
            r = 8 + ("ln1_g", "ln1_b", "ln2_g", "ln2_b").index(name)
            return tot[r:r + 1, :]
        if name == "gmlp_ln_g":
            return tot[12:13, :GM_W]
        if name == "gmlp_ln_b":
            return tot[12:13, GM_W:]
        if name == "b_spatial":
            return jnp.concatenate([tot[13:14, g * BLK:(g + 1) * BLK] for g in range(N_GROUPS)], axis=0)[None]
        return tot[14:15, :N_Q_HEADS]

    def body(*refs):
        g_ref, in_refs = refs[0], refs[1:1 + len(flat)]
        tot_ref, out_refs = refs[1 + len(flat)], refs[2 + len(flat):]
        tot = g_ref[0]
        for i in range(1, N_DEV):
            tot = tot + g_ref[i]
        tot_ref[...] = tot
        tot_ref[0:2, :] = tot[0:2, :] + tot[6:8, :]
        tot_ref[15:16, :] = jnp.broadcast_to(jnp.sum(tot[15:16, :], axis=1, keepdims=True), (1, D))
        tot = tot_ref[...]
        for k, name in enumerate(SMALL_ORDER):
            w_ref, m_ref, v_ref = in_refs[3 * k:3 * k + 3]
            g = grad_of(tot, name)
            delta, m2, v2 = _adamw(w_ref[...], g, m_ref[...], v_ref[...])
            for r, val in zip(out_refs[4 * k:4 * k + 4], (g, delta, m2, v2)):
                r[...] = val

    res = pl.pallas_call(
        body, name="small_step", grid=(1,),
        in_specs=[_full((N_DEV, 16, D))] + [_full(a.shape) for a in flat],
        out_specs=[_full((16, D))] + [_full(params[name][0].shape) for name in SMALL_ORDER for _ in range(4)],
        out_shape=[_sds((16, D), F32)] + [_sds(params[name][0].shape, F32) for name in SMALL_ORDER for _ in range(4)],
        compiler_params=_params(("arbitrary",)),
    )(gath, *flat)
    return res[0], {name: res[1 + 4 * k:5 + 4 * k] for k, name in enumerate(SMALL_ORDER)}


def _cctx_finish(gath, c_ctx, m, v):
    def body(g_ref, c_ref, m_ref, v_ref, gr_ref, d_ref, m2_ref, v2_ref):
        ds = g_ref[0]
        for i in range(1, N_DEV):
            ds = ds + g_ref[i]
        c = c_ref[...]
        sg = _sigmoid(c)
        g = ds * (sg * (1.0 + c * (1.0 - sg)))
        delta, m2, v2 = _adamw(c, g, m_ref[...], v_ref[...])
        gr_ref[...] = g
        d_ref[...] = delta
        m2_ref[...] = m2
        v2_ref[...] = v2

    return pl.pallas_call(
        body, name="cctx_finish", grid=(1,),
        in_specs=[_full((N_DEV, 8, D))] + [_full((8, D))] * 3, out_specs=[_full((8, D))] * 4,
        out_shape=[_sds((8, D), F32)] * 4,
        compiler_params=_params(("arbitrary",)),
    )(gath, c_ctx, m, v)


def _pad_rows(a, rows):
    return jnp.concatenate([a, jnp.zeros((rows - a.shape[0], a.shape[1]), a.dtype)], axis=0)


def kernel(x, c, ctx, c_ctx, w_ada, b_ada, w_in, attn_sink, gmlp_ln_g, gmlp_ln_b, w_spatial, b_spatial, w_branch_a, w_branch_b, w_out, ln1_g, ln1_b, w_ffn_in, w_ffn_out, ln2_g, ln2_b, loss_target, m_c_ctx, m_w_ada, m_b_ada, m_w_in, m_attn_sink, m_gmlp_ln_g, m_gmlp_ln_b, m_w_spatial, m_b_spatial, m_w_branch_a, m_w_branch_b, m_w_out, m_ln1_g, m_ln1_b, m_w_ffn_in, m_w_ffn_out, m_ln2_g, m_ln2_b, v_c_ctx, v_w_ada, v_b_ada, v_w_in, v_attn_sink, v_gmlp_ln_g, v_gmlp_ln_b, v_w_spatial, v_b_spatial, v_w_branch_a, v_w_branch_b, v_w_out, v_ln1_g, v_ln1_b, v_w_ffn_in, v_w_ffn_out, v_ln2_g, v_ln2_b):
    L = x.shape[1]
    me = 4 * lax.axis_index("x") + 2 * lax.axis_index("y") + lax.axis_index("c")
    x2, tgt, ctx2 = x[0], loss_target[0], ctx[0]
    tiles = _Tiles(L)
    tm_in, tm, tt = tiles.wide, tiles.narrow, tiles.tokens

    transposed = ("w_in", "w_ffn_in")
    tr = lambda kname, a: a.T if kname in transposed else a
    big = dict(w_in=w_in[0].T, w_branch_a=w_branch_a[0], w_branch_b=w_branch_b[0], w_out=w_out[0],
               w_ffn_in=w_ffn_in[0].T, w_ffn_out=w_ffn_out[0])
    col_sharded = ("w_branch_a", "w_branch_b")
    shard_bf = {k: a.astype(BF16) for k, a in big.items()}

    def assemble(kname, g):
        if kname in col_sharded:
            return g.transpose(1, 0, 2).reshape(g.shape[1], N_DEV * g.shape[2])
        return g.reshape(N_DEV * g.shape[1], g.shape[2])

    def to_blocks(kname, g):
        if kname in col_sharded:
            return g.reshape(g.shape[0], N_DEV, g.shape[1] // N_DEV).transpose(1, 0, 2)
        return g.reshape(N_DEV, g.shape[0] // N_DEV, g.shape[1])

    full = {}
    n_ada = w_ada.shape[2]
    b_my = lax.dynamic_slice(b_ada, (0, me * n_ada), (1, n_ada))
    act, mod_all, got = _prologue(_pad_rows(c, 8), _pad_rows(c_ctx[None, :], 8), w_ada[0], b_my,
                                  _Comm(gather=[shard_bf["w_in"]]))
    full["w_in"] = assemble("w_in", got[0])
    mod_all = mod_all.transpose(1, 0, 2).reshape(16, 6 * D)
    modv = _pad_rows(lax.dynamic_slice(mod_all, (me, 0), (1, 6 * D)).reshape(6, D), 8)
    modc = _pad_rows(mod_all[8].reshape(6, D), 8)

    lnv = _pad_rows(jnp.concatenate([ln1_g, ln1_b, ln2_g, ln2_b], axis=0), 8)
    gm_lnv = _pad_rows(jnp.concatenate([gmlp_ln_g, gmlp_ln_b], axis=0), 8)
    ws_b = w_spatial[0].astype(BF16)
    wst_b = ws_b.transpose(0, 2, 1)
    bsp = jnp.repeat(b_spatial[0].T, GROUP_DIM, axis=1)
    sink = attn_sink[0]
    cos, sin = _rope_tables(L)
    bias = _attn_bias()
    w_kv = full["w_in"][O_K:O_K + 2 * KV_W, :]

    (h, q, k, v, u, vb, ga, gb), got = _k_in(
        x2, modv, full["w_in"], cos, sin, tm_in,
        comm=_Comm(gather=[shard_bf[kname] for kname in ("w_branch_a", "w_branch_b", "w_out", "w_ffn_out")]))
    for kname, g in zip(("w_branch_a", "w_branch_b", "w_out", "w_ffn_out"), got):
        full[kname] = assemble(kname, g)
    hc, kc, vc = _k_ctx(ctx2, modc, w_kv)
    (ya, lse), got = _k_attn(sink, q, k, v, kc, vc, bias, comm=_Comm(gather=[shard_bf["w_ffn_in"]]))
    full["w_ffn_in"] = assemble("w_ffn_in", got[0])
    yb = _k_gmlp(u, vb, gm_lnv, ws_b, bsp)
    merged, mix, xm, h2 = _k_merge(x2, ya, yb, ga, gb, full["w_branch_a"], full["w_branch_b"], full["w_out"], modv, lnv, tm_in)
    gate, up, act_f, dr2, df, acc_f = _k_ffn(h2, xm, tgt, full["w_ffn_in"], full["w_ffn_out"], modv, lnv, tm_in)

    dF, dmix, dxp, acc_b = _k_ffn_bwd(df, gate, up, xm, dr2, x2, mix, full["w_ffn_in"], full["w_ffn_out"], modv, lnv, tm)
    blk_fo = to_blocks("w_ffn_out", _wgrad(act_f, df, "wgrad_ffn_out", tiles.tk_ffn, tt))
    gw_fi, (rcv_fo,) = _wgrad(dF, h2, "wgrad_ffn_in", tiles.tk_ffn, tt, comm=_Comm(scatter=[blk_fo]))
    blk_fi = to_blocks("w_ffn_in", gw_fi)
    dga, dgb, dya, dyb, gw_a, gw_b, gw_o = _k_merge_bwd(
        dmix, merged, ya, yb, ga, gb, full["w_branch_a"], full["w_branch_b"], full["w_out"], tm_in)
    du, dvb, g_ws, g_bst, g_gln = _k_gmlp_bwd(u, vb, dyb, gm_lnv, ws_b, wst_b, bsp)
    (dq, dk_late, dv_late, dkc, dvc, g_sink), (gath_ws, rcv_fi) = _k_attn_bwd(
        sink, q, k, v, kc, vc, dya, lse, cos, sin, bias,
        comm=_Comm(gather=[g_ws.reshape(N_GROUPS * BLK, BLK)], scatter=[blk_fi]))
    dk, dv = dk_late[BLK:BLK + L], dv_late[BLK:BLK + L]
    blk_a, blk_b, blk_o = to_blocks("w_branch_a", gw_a), to_blocks("w_branch_b", gw_b), to_blocks("w_out", gw_o)
    (dP, grad_x, acc_i), _ = _k_in_bwd(dq, dk, dv, du, dvb, dga, dgb, x2, dxp, full["w_in"], modv, tm_in)
    g_ctx, dmodc = _k_ctx_bwd(ctx2, modc, hc, dkc, dvc, w_kv)
    gw_in, (rcv_a, rcv_b, rcv_o) = _wgrad(dP, h, "wgrad_in", tiles.tk_in, tt, comm=_Comm(scatter=[blk_a, blk_b, blk_o]),
                                          extra=(O_K, g_ctx))

    dmod_x = jnp.concatenate([acc_i[0:2], acc_b[4:5], acc_b[0:2], acc_f[2:3]], axis=0)
    small = jnp.concatenate([
        dmod_x, dmodc[0:2], acc_b[2:4], acc_f[0:2],
        jnp.concatenate([g_gln[0:1], g_gln[1:2]], axis=1), g_bst.T.reshape(1, D),
        _pad_rows(g_sink[:, 0:1], D).T, acc_f[3:4]], axis=0)
    rcv_in, gath = _exchange_two_level(to_blocks("w_in", gw_in), small, "exchange_last")
    received = dict(w_in=rcv_in, w_branch_a=rcv_a, w_branch_b=rcv_b, w_out=rcv_o, w_ffn_in=rcv_fi, w_ffn_out=rcv_fo)
    moments = dict(w_in=(m_w_in, v_w_in), w_branch_a=(m_w_branch_a, v_w_branch_a), w_branch_b=(m_w_branch_b, v_w_branch_b),
                   w_out=(m_w_out, v_w_out), w_ffn_in=(m_w_ffn_in, v_w_ffn_in), w_ffn_out=(m_w_ffn_out, v_w_ffn_out))
    names = list(big)
    res = {}
    for kname in names:
        mm, vv = moments[kname]
        R = big[kname].shape[0]
        res[kname] = [tr(kname, r) for r in _adamw_reduce(
            received[kname], big[kname], tr(kname, mm[0]), tr(kname, vv[0]), "adamw_" + kname, 256 if R % 256 == 0 else R // 2)]

    ws2d = lambda a: a.reshape(N_GROUPS * BLK, BLK)
    res_ws = [r.reshape(w_spatial.shape) for r in _adamw_reduce(
        gath_ws, ws2d(w_spatial), ws2d(m_w_spatial), ws2d(v_w_spatial), "adamw_w_spatial", 256)]
    tot, res_small = _small_step(gath, dict(
        b_ada=(b_ada, m_b_ada, v_b_ada), ln1_g=(ln1_g, m_ln1_g, v_ln1_g), ln1_b=(ln1_b, m_ln1_b, v_ln1_b),
        ln2_g=(ln2_g, m_ln2_g, v_ln2_g), ln2_b=(ln2_b, m_ln2_b, v_ln2_b),
        gmlp_ln_g=(gmlp_ln_g, m_gmlp_ln_g, v_gmlp_ln_g), gmlp_ln_b=(gmlp_ln_b, m_gmlp_ln_b, v_gmlp_ln_b),
        b_spatial=(b_spatial, m_b_spatial, v_b_spatial), attn_sink=(attn_sink, m_attn_sink, v_attn_sink)))
    loss = tot[15, 0]

    dmod_rows = jnp.concatenate([gath[:, 0:6, :].reshape(N_DEV, 6 * D),
                                 jnp.concatenate([tot[6:8].reshape(1, 2 * D), jnp.zeros((1, 4 * D), F32)], axis=1),
                                 jnp.zeros((7, 6 * D), F32)], axis=0)
    dmod_my = lax.dynamic_slice(dmod_rows, (0, me * n_ada), (16, n_ada))
    g_wada, d_wada, m2_wada, v2_wada, pc = _ada_bwd(act, dmod_my, w_ada[0], m_w_ada[0], v_w_ada[0])
    pc_all = _gather_rows(pc, "gather_cctx")
    cc8 = lambda a: _pad_rows(a.reshape(1, D), 8)
    g_cc, d_cc, m2_cc, v2_cc = _cctx_finish(pc_all, cc8(c_ctx), cc8(m_c_ctx), cc8(v_c_ctx))

    order = ["c_ctx", "w_ada", "b_ada", "w_in", "attn_sink", "gmlp_ln_g", "gmlp_ln_b", "w_spatial", "b_spatial",
             "w_branch_a", "w_branch_b", "w_out", "ln1_g", "ln1_b", "w_ffn_in", "w_ffn_out", "ln2_g", "ln2_b"]
    grads, deltas, new_m, new_v = {}, {}, {}, {}
    grads["c_ctx"], deltas["c_ctx"], new_m["c_ctx"], new_v["c_ctx"] = g_cc[0], d_cc[0], m2_cc[0], v2_cc[0]
    grads["w_ada"], deltas["w_ada"], new_m["w_ada"], new_v["w_ada"] = g_wada[None], d_wada[None], m2_wada[None], v2_wada[None]
    for kname in names:
        g, d, m2, v2 = res[kname]
        grads[kname], deltas[kname], new_m[kname], new_v[kname] = g[None], d[None], m2[None], v2[None]
    grads["w_spatial"], deltas["w_spatial"], new_m["w_spatial"], new_v["w_spatial"] = res_ws
    for kname in SMALL_ORDER:
        grads[kname], deltas[kname], new_m[kname], new_v[kname] = res_small[kname]
    return (loss, grad_x[None], *[grads[n] for n in order], *[deltas[n] for n in order],
            *[new_m[n] for n in order], *[new_v[n] for n in order])
```

```python
import functools
import math

import jax
import jax.numpy as jnp
import numpy as np
from jax import lax
from jax.experimental import pallas as pl
from jax.experimental.pallas import tpu as pltpu

F32 = jnp.float32
BF16 = jnp.bfloat16
MESH = pl.DeviceIdType.MESH

N_DEV = 8
D = 1024
HEAD_DIM = 64
N_Q_HEADS = 8
N_KV_HEADS = 2
GQA_GROUP = 4
BLK = 128
Q_W = 512
KV_W = 128
GM_W = 512
N_GROUPS = 8
GROUP_DIM = 64
FFN_H = 2816
IN_W = 3840
O_Q, O_K, O_V, O_U, O_VB, O_GA, O_GB = 0, 512, 640, 768, 1280, 1792, 2816
LN_EPS = 1e-5
NEG_INF = -1e30
ALPHA = 2.0 ** 0.25
ROPE_BASE = 10000.0
ROPE_PAIRS = 16
Q_SCALE = HEAD_DIM ** -0.5
GELU_K0 = math.sqrt(2.0 / math.pi)
GELU_K1 = 0.044715

ADAM_LR = 0.001
ADAM_B1 = 0.9
ADAM_B2 = 0.999
ADAM_EPS = 1e-08
ADAM_WD = 0.01
ADAM_STEP = 10

V7X_VMEM_BYTES = 64 * 1024 * 1024
VMEM_LIMIT = V7X_VMEM_BYTES * 7 // 8
NT = (((1,), (1,)), ((), ()))
TN = (((0,), (0,)), ((), ()))


class _Tiles:
    def __init__(self, L):
        self.widest = min(1024, L)
        self.wide = min(512, L)
        self.narrow = min(256, L)
        self.tokens = min(2048, L)
        self.tk_in = IN_W // 3
        self.tk_ffn = FFN_H // 2


def _params(sem=None):
    return pltpu.CompilerParams(dimension_semantics=sem, vmem_limit_bytes=VMEM_LIMIT)


def _row(tm, w):
    return pl.BlockSpec((tm, w), lambda i: (i, 0))


def _full(shape):
    nd = len(shape)
    return pl.BlockSpec(shape, lambda i: (0,) * nd)


def _resident(shape):
    nd = len(shape)
    return pl.BlockSpec(shape, lambda i: (0,) * nd, pipeline_mode=pl.Buffered(1))


def _sds(shape, dt):
    return jax.ShapeDtypeStruct(shape, dt)


def _ln(xf):
    mu = jnp.mean(xf, axis=-1, keepdims=True)
    xc = xf - mu
    var = jnp.mean(xc * xc, axis=-1, keepdims=True)
    rstd = lax.rsqrt(var + LN_EPS)
    return xc * rstd, rstd


def _ln_bwd(dn, n, rstd):
    m1 = jnp.mean(dn, axis=-1, keepdims=True)
    m2 = jnp.mean(dn * n, axis=-1, keepdims=True)
    return rstd * (dn - m1 - n * m2)


def _colsum(t):
    return jnp.sum(t, axis=0, keepdims=True)


def _sigmoid(x):
    return 0.5 * jnp.tanh(0.5 * x) + 0.5


def _gelu(x):
    t = jnp.tanh(GELU_K0 * (x + GELU_K1 * (x * x * x)))
    return x * (0.5 * (1.0 + t)), t


def _gelu_grad(x, t):
    return 0.5 * (1.0 + t) + 0.5 * x * (1.0 - t * t) * (GELU_K0 * (1.0 + 3.0 * GELU_K1 * x * x))


def _swap16(t):
    lane = lax.broadcasted_iota(jnp.int32, t.shape, 1)
    return jnp.where((lane & 16) == 0, pltpu.roll(t, 112, 1), pltpu.roll(t, 16, 1))


def _rope(t, cos, sin):
    return t * cos + _swap16(t) * sin


def _unrope(t, cos, sin):
    return t * cos - _swap16(t) * sin


def _adamw(w, g, m, v):
    m2 = ADAM_B1 * m + (1.0 - ADAM_B1) * g
    v2 = ADAM_B2 * v + (1.0 - ADAM_B2) * (g * g)
    m_hat = m2 / (1.0 - ADAM_B1 ** ADAM_STEP)
    v_hat = v2 / (1.0 - ADAM_B2 ** ADAM_STEP)
    delta = -ADAM_LR * (m_hat / (jnp.sqrt(v_hat) + ADAM_EPS) + ADAM_WD * w)
    return delta, m2, v2


def _rope_tables(L):
    inv = (np.float32(ROPE_BASE) ** (-np.arange(ROPE_PAIRS, dtype=np.float32) / np.float32(ROPE_PAIRS))).astype(np.float32)
    t = np.arange(L, dtype=np.int32)
    rows = (t // 64).astype(np.float32)[:, None] * inv
    cols = (t % 64).astype(np.float32)[:, None] * inv
    cr, sr, cc, sc = np.cos(rows), np.sin(rows), np.cos(cols), np.sin(cols)
    cos = np.concatenate([cr, cr, cc, cc], axis=1)
    sin = np.concatenate([-sr, sr, -sc, sc], axis=1)
    return jnp.asarray(np.tile(cos, (1, 2)), F32), jnp.asarray(np.tile(sin, (1, 2)), F32)


def _me():
    return lax.axis_index("x"), lax.axis_index("y"), lax.axis_index("c")


def _peer(mx, my, mc, k):
    return (mx ^ ((k >> 2) & 1), my ^ ((k >> 1) & 1), mc ^ (k & 1))


class _Comm:
    def __init__(self, gather=(), scatter=(), spread=()):
        self.kinds = ["gather"] * len(gather) + ["scatter"] * len(scatter) + ["spread"] * len(spread)
        self.args = list(gather) + list(scatter) + list(spread)
        self.n = len(self.args)

    def out_shape(self):
        return [_sds(a.shape if k == "scatter" else (N_DEV,) + a.shape, a.dtype) for k, a in zip(self.kinds, self.args)]

    def specs(self):
        return [pl.BlockSpec(memory_space=pl.ANY)] * self.n

    def scratch(self):
        return [pltpu.SemaphoreType.DMA((7 * self.n,)), pltpu.SemaphoreType.DMA((7 * self.n,)),
                pltpu.SemaphoreType.DMA((self.n,))]

    def _plan(self, x_refs, out_refs, send_sems, recv_sems, local_sems):
        mx, my, mc = _me()
        me = 4 * mx + 2 * my + mc
        here, sibling = (mx, my, mc), (mx, my, 1 - mc)
        chips = [(1 - mx, my), (mx, 1 - my), (1 - mx, 1 - my)]
        local, first, last = [], [], []
        relay = [[], [], []]
        for a, kind in enumerate(self.kinds):
            x, out = x_refs[a], out_refs[a]

            def rc(k, src, dst, to):
                return pltpu.make_async_remote_copy(
                    src_ref=src, dst_ref=dst, send_sem=send_sems.at[7 * a + k], recv_sem=recv_sems.at[7 * a + k],
                    device_id=to, device_id_type=MESH)

            if kind == "gather":
                local.append(pltpu.make_async_copy(x, out.at[me], local_sems.at[a]))
                first.append(rc(0, x, out.at[me], sibling))
                last.append(rc(0, x, out.at[me ^ 1], here))
                for j, (cx, cy) in enumerate(chips):
                    first.append(rc(1 + j, x, out.at[me], (cx, cy, mc)))
                    landed = out.at[4 * cx + 2 * cy + mc]
                    relay[j].append((rc(1 + j, x, landed, here), rc(4 + j, landed, landed, sibling)))
                    last.append(rc(4 + j, x, out.at[4 * cx + 2 * cy + 1 - mc], here))
            else:
                own = x.at[me] if kind == "scatter" else x
                local.append(pltpu.make_async_copy(own, out.at[me], local_sems.at[a]))
                for k in range(1, N_DEV):
                    src = x.at[me ^ k] if kind == "scatter" else x
                    first.append(rc(k - 1, src, out.at[me], _peer(mx, my, mc, k)))
                    last.append(rc(k - 1, own, out.at[me ^ k], here))
        return local, first, relay[0] + relay[1] + relay[2], last

    def start(self, *refs):
        local, first, _, _ = self._plan(*refs)
        for cp in local + first:
            cp.start()

    def relay(self, *refs):
        _, _, relay, _ = self._plan(*refs)
        for arrival, onward in relay:
            arrival.wait_recv()
            onward.start()

    def finish(self, *refs):
        local, first, relay, last = self._plan(*refs)
        for cp in last:
            cp.wait_recv()
        for cp in first:
            cp.wait_send()
        for _, onward in relay:
            onward.wait_send()
        for cp in local:
            cp.wait()


def _call(body, *, name, grid, in_specs, out_specs, out_shape, args, scratch=(), comm=None, aliases=None):
    params = _params(("arbitrary",) * len(grid))
    total = math.prod(grid)

    def at(step):
        flat = functools.reduce(lambda acc, dn: acc * dn[1] + pl.program_id(dn[0]), enumerate(grid), 0)
        return flat == step

    if comm is None:
        res = pl.pallas_call(
            body, name=name, grid=grid, in_specs=list(in_specs), out_specs=list(out_specs), out_shape=list(out_shape),
            scratch_shapes=list(scratch), input_output_aliases=aliases or {}, compiler_params=params)(*args)
        return list(res), []
    n_in, n_out, n_scr, cn = len(in_specs), len(out_specs), len(scratch), comm.n

    def hosted(*refs):
        ins, refs = refs[:n_in], refs[n_in:]
        cins, refs = refs[:cn], refs[cn:]
        outs, refs = refs[:n_out], refs[n_out:]
        couts, refs = refs[:cn], refs[cn:]
        scr, sems = refs[:n_scr], refs[n_scr:]

        @pl.when(at(0))
        def _():
            comm.start(cins, couts, *sems)

        body(*ins, *outs, *scr)

        @pl.when(at((3 * total) // 4 if total >= 4 else total - 1))
        def _():
            comm.relay(cins, couts, *sems)

        @pl.when(at(total - 1))
        def _():
            comm.finish(cins, couts, *sems)

    res = pl.pallas_call(
        hosted, name=name, grid=grid, in_specs=list(in_specs) + comm.specs(), out_specs=list(out_specs) + comm.specs(),
        out_shape=list(out_shape) + comm.out_shape(), scratch_shapes=list(scratch) + comm.scratch(),
        input_output_aliases=aliases or {}, compiler_params=params)(*args, *comm.args)
    return list(res[:n_out]), list(res[n_out:])


def _exchange_two_level(blk, small, name):
    _, R, C = blk.shape
    rows = small.shape[0]

    def body(blk_ref, small_ref, stage_ref, out_ref, gath_ref, a_scr, b_scr, t_scr, s1, r1, s3, r3, ss, rs, lsem):
        mx, my, mc = _me()
        me = 4 * mx + 2 * my + mc
        mine = 2 * mx + my
        here, sibling = (mx, my, mc), (mx, my, 1 - mc)

        def rc(src, dst, send, recv, to):
            return pltpu.make_async_remote_copy(src_ref=src, dst_ref=dst, send_sem=send, recv_sem=recv,
                                                device_id=to, device_id_type=MESH)

        own_small = pltpu.make_async_copy(small_ref, gath_ref.at[me], lsem.at[0])
        own_small.start()
        spread = [rc(small_ref, gath_ref.at[me], ss.at[k - 1], rs.at[k - 1], _peer(mx, my, mc, k)) for k in range(1, N_DEV)]
        order = (1, 2, 3, 0)
        to_sib = [rc(blk_ref.at[2 * (mine ^ k) + 1 - mc], stage_ref.at[k], s1.at[k], r1.at[k], sibling) for k in order]
        for cp in spread + to_sib:
            cp.start()
        own = {k: pltpu.make_async_copy(blk_ref.at[2 * (mine ^ k) + mc], a_scr.at[k], lsem.at[1 + k]) for k in order}
        for k in order:
            own[k].start()
        onward = []
        for k in order:
            rc(blk_ref.at[0], stage_ref.at[k], s1.at[k], r1.at[k], here).wait_recv()
            landed = pltpu.make_async_copy(stage_ref.at[k], b_scr.at[k], lsem.at[5 + k])
            landed.start()
            landed.wait()
            own[k].wait()
            t_scr[k] = (a_scr[k].astype(F32) + b_scr[k].astype(F32)).astype(BF16)
            if k > 0:
                cp = rc(t_scr.at[k], out_ref.at[mine], s3.at[k - 1], r3.at[k - 1], (mx ^ (k >> 1), my ^ (k & 1), mc))
                cp.start()
                onward.append(cp)
        keep = pltpu.make_async_copy(t_scr.at[0], out_ref.at[mine], lsem.at[9])
        keep.start()
        for k in range(1, 4):
            rc(t_scr.at[0], out_ref.at[mine ^ k], s3.at[k - 1], r3.at[k - 1], here).wait_recv()
        for k in range(1, N_DEV):
            rc(small_ref, gath_ref.at[me ^ k], ss.at[k - 1], rs.at[k - 1], here).wait_recv()
        for cp in spread + to_sib + onward:
            cp.wait_send()
        keep.wait()
        own_small.wait()

    any_spec = pl.BlockSpec(memory_space=pl.ANY)
    dma = pltpu.SemaphoreType.DMA
    _, out, gath = pl.pallas_call(
        body, name=name,
        in_specs=[any_spec, any_spec], out_specs=[any_spec] * 3,
        out_shape=[_sds((4, R, C), BF16), _sds((4, R, C), BF16), _sds((N_DEV, rows, D), F32)],
        scratch_shapes=[pltpu.VMEM((4, R, C), BF16)] * 3
                       + [dma((4,)), dma((4,)), dma((3,)), dma((3,)), dma((N_DEV - 1,)), dma((N_DEV - 1,)), dma((10,))],
        compiler_params=pltpu.CompilerParams(vmem_limit_bytes=VMEM_LIMIT),
    )(blk, small)
    return out, gath


def _exchange_rows(x_ref, out_ref, send_sems, recv_sems, between=None):
    mx, my, mc = _me()
    me = 4 * mx + 2 * my + mc
    out_ref[pl.ds(me, 1)] = x_ref[...][None]
    sends = []
    for k in range(1, N_DEV):
        cp = pltpu.make_async_remote_copy(
            src_ref=x_ref, dst_ref=out_ref.at[me], send_sem=send_sems.at[k - 1], recv_sem=recv_sems.at[k - 1],
            device_id=_peer(mx, my, mc, k), device_id_type=MESH)
        cp.start()
        sends.append(cp)
    if between is not None:
        between()
    for k in range(1, N_DEV):
        pltpu.make_async_remote_copy(
            src_ref=x_ref, dst_ref=out_ref.at[me ^ k], send_sem=send_sems.at[k - 1], recv_sem=recv_sems.at[k - 1],
            device_id=(mx, my, mc), device_id_type=MESH).wait_recv()
    for cp in sends:
        cp.wait_send()


def _prologue(c8, cctx8, w_ada, b_my, comm):
    nw = w_ada.shape[1]

    cn = comm.n

    def body(*refs):
        c_ref, cctx_ref, w_ref, b_ref = refs[:4]
        cins, refs = refs[4:4 + cn], refs[4 + cn:]
        act_ref, mod_ref = refs[:2]
        couts, refs = refs[2:2 + cn], refs[2 + cn:]
        cmine_scr, call_scr, mine_scr, mall_scr, s1, r1, s2, r2 = refs[:8]
        csems = refs[8:]
        cmine_scr[...] = c_ref[...]
        _exchange_rows(cmine_scr, call_scr, s1, r1)
        rows = [call_scr[d][0:1, :] for d in range(N_DEV)] + [cctx_ref[0:1, :], jnp.zeros((7, D), F32)]
        s = jnp.concatenate(rows, axis=0)
        act = s * _sigmoid(s)
        act_ref[...] = act
        mine_scr[...] = jnp.dot(act.astype(BF16), w_ref[...].astype(BF16), preferred_element_type=F32) + b_ref[...]
        _exchange_rows(mine_scr, mall_scr, s2, r2, between=lambda: comm.start(cins, couts, *csems))
        mod_ref[...] = mall_scr[...]
        comm.relay(cins, couts, *csems)
        comm.finish(cins, couts, *csems)

    sems = [pltpu.SemaphoreType.DMA((N_DEV - 1,))] * 4
    res = pl.pallas_call(
        body, name="prologue", grid=(1,),
        in_specs=[_full((8, D)), _full((8, D)), _full((D, nw)), _full((1, nw))] + comm.specs(),
        out_specs=[_full((16, D)), _full((N_DEV, 16, nw))] + comm.specs(),
        out_shape=[_sds((16, D), F32), _sds((N_DEV, 16, nw), F32)] + comm.out_shape(),
        scratch_shapes=[pltpu.VMEM((8, D), F32), pltpu.VMEM((N_DEV, 8, D), F32), pltpu.VMEM((16, nw), F32),
                        pltpu.VMEM((N_DEV, 16, nw), F32)] + sems + comm.scratch(),
        compiler_params=_params(("arbitrary",)),
    )(c8, cctx8, w_ada, b_my, *comm.args)
    return res[0], res[1], list(res[2:])


def _gather_rows(x, name):
    def body(x_ref, out_ref, send_sems, recv_sems):
        _exchange_rows(x_ref, out_ref, send_sems, recv_sems)

    return pl.pallas_call(
        body, name=name,
        out_shape=_sds((N_DEV,) + x.shape, x.dtype),
        in_specs=[pl.BlockSpec(memory_space=pltpu.VMEM)],
        out_specs=pl.BlockSpec(memory_space=pltpu.VMEM),
        scratch_shapes=[pltpu.SemaphoreType.DMA((N_DEV - 1,)), pltpu.SemaphoreType.DMA((N_DEV - 1,))],
        compiler_params=pltpu.CompilerParams(vmem_limit_bytes=VMEM_LIMIT),
    )(x)


def _ada_bwd(act, dmod_my, w_ada, m, v, tr=256):
    nw = w_ada.shape[1]

    def body(act_ref, dm_ref, w_ref, m_ref, v_ref, g_ref, d_ref, m2_ref, v2_ref, pc_ref):
        dm = dm_ref[...].astype(BF16)
        g = lax.dot_general(act_ref[...].astype(BF16), dm, TN, preferred_element_type=F32)
        w = w_ref[...]
        delta, m2, v2 = _adamw(w, g, m_ref[...], v_ref[...])
        g_ref[...] = g
        d_ref[...] = delta
        m2_ref[...] = m2
        v2_ref[...] = v2
        pc_ref[...] = lax.dot_general(dm[8:16, :], w.astype(BF16), NT, preferred_element_type=F32)

    wspec = _row(tr, nw)
    return pl.pallas_call(
        body, name="ada_bwd", grid=(D // tr,),
        in_specs=[pl.BlockSpec((16, tr), lambda i: (0, i)), _full((16, nw)), wspec, wspec, wspec],
        out_specs=[wspec, wspec, wspec, wspec, pl.BlockSpec((8, tr), lambda i: (0, i))],
        out_shape=[_sds((D, nw), F32)] * 4 + [_sds((8, D), F32)],
        compiler_params=_params(("arbitrary",)),
    )(act, dmod_my, w_ada, m, v)


def _k_in(x, modv, w_in, cos, sin, tm, comm=None):
    L = x.shape[0]

    def body(x_ref, mod_ref, w_ref, cos_ref, sin_ref, h_ref, q_ref, k_ref, v_ref, u_ref, vb_ref, ga_ref, gb_ref):
        n, _ = _ln(x_ref[...])
        h = (n * (1.0 + mod_ref[1:2, :]) + mod_ref[0:1, :]).astype(BF16)
        h_ref[...] = h
        c, s = cos_ref[...], sin_ref[...]

        def proj(lo, width):
            return lax.dot_general(h, w_ref[lo:lo + width, :], NT, preferred_element_type=F32)

        for i in range(4):
            q_ref[:, i * 128:(i + 1) * 128] = (_rope(proj(O_Q + i * 128, 128), c, s) * Q_SCALE).astype(BF16)
        k_ref[...] = _rope(proj(O_K, KV_W), c, s).astype(BF16)
        v_ref[...] = proj(O_V, KV_W).astype(BF16)
        u_ref[...] = proj(O_U, GM_W).astype(BF16)
        vb_ref[...] = proj(O_VB, GM_W).astype(BF16)
        ga_ref[...] = proj(O_GA, D).astype(BF16)
        gb_ref[...] = proj(O_GB, D).astype(BF16)

    widths = [D, Q_W, KV_W, KV_W, GM_W, GM_W, D, D]
    return _call(
        body, name="fwd_in", grid=(L // tm,),
        in_specs=[_row(tm, D), _full((8, D)), _resident((IN_W, D)), _row(tm, 128), _row(tm, 128)],
        out_specs=[_row(tm, w) for w in widths],
        out_shape=[_sds((L, w), BF16) for w in widths],
        args=(x, modv, w_in, cos, sin), comm=comm)


def _k_ctx(ctx, modc, w_kv):
    C = ctx.shape[0]

    def body(c_ref, mod_ref, w_ref, hc_ref, kc_ref, vc_ref):
        n, _ = _ln(c_ref[...])
        hc = (n * (1.0 + mod_ref[1:2, :]) + mod_ref[0:1, :]).astype(BF16)
        hc_ref[...] = hc
        kv = lax.dot_general(hc, w_ref[...], NT, preferred_element_type=F32)
        kc_ref[...] = kv[:, :KV_W].astype(BF16)
        vc_ref[...] = kv[:, KV_W:].astype(BF16)

    return pl.pallas_call(
        body, name="fwd_ctx", grid=(1,),
        in_specs=[_full((C, D)), _full((8, D)), _full((2 * KV_W, D))],
        out_specs=[_full((C, D)), _full((C, KV_W)), _full((C, KV_W))],
        out_shape=[_sds((C, D), BF16), _sds((C, KV_W), BF16), _sds((C, KV_W), BF16)],
        compiler_params=_params(("arbitrary",)),
    )(ctx, modc, w_kv)


def _attn_bias():
    r = (np.arange(GQA_GROUP * BLK) & (BLK - 1))[:, None]
    j = np.arange(3 * BLK)[None, :]
    band = np.abs(j - BLK - r) <= BLK
    variants = [band & (j >= BLK), band, band & (j < 2 * BLK)]
    return jnp.asarray(np.stack([np.where(v, 0.0, NEG_INF) for v in variants]), F32)


def _masked(s, bias, C):
    return jnp.concatenate([s[:, :C], s[:, C:] + bias], axis=1)


def _sink_col(sink_ref, hk):
    grp = lax.broadcasted_iota(jnp.int32, (GQA_GROUP * BLK, 1), 0) >> 7
    col = jnp.full((GQA_GROUP * BLK, 1), sink_ref[hk * GQA_GROUP], F32)
    for g in range(1, GQA_GROUP):
        col = jnp.where(grp == g, sink_ref[hk * GQA_GROUP + g], col)
    return col


ATTN_FWD_BLOCKS = 4


def _k_attn(sink, q, k, v, kc, vc, bias, comm=None):
    L = q.shape[0]
    C = kc.shape[0]
    nb = L // BLK
    nq = min(ATTN_FWD_BLOCKS, nb)
    steps = nb // nq

    def body(sink_ref, q_ref, kp_ref, km_ref, kx_ref, vp_ref, vm_ref, vx_ref, kc_ref, vc_ref, bias_ref, ya_ref, lse_ref):
        i = pl.program_id(0)
        chains = [(qb, hk) for qb in range(nq) for hk in range(N_KV_HEADS)]

        def band(qb):
            first = jnp.where(i == 0, 0, 1) if qb == 0 else 1
            return bias_ref[jnp.where(i == steps - 1, 2, first) if qb == nq - 1 else first]

        def keys(ctx_ref, p_ref, m_ref, x_ref, qb, hk):
            sl = slice(hk * HEAD_DIM, (hk + 1) * HEAD_DIM)
            blocks = [p_ref[:, sl]] + [m_ref[j * BLK:(j + 1) * BLK, sl] for j in range(nq)] + [x_ref[:, sl]]
            return jnp.concatenate([ctx_ref[:, sl]] + blocks[qb:qb + 3], axis=0)

        def queries(qb, hk):
            return jnp.concatenate(
                [q_ref[qb * BLK:(qb + 1) * BLK, (hk * GQA_GROUP + g) * HEAD_DIM:(hk * GQA_GROUP + g + 1) * HEAD_DIM]
                 for g in range(GQA_GROUP)], axis=0)

        def scores(qb, hk):
            return _masked(lax.dot_general(queries(qb, hk), keys(kc_ref, kp_ref, km_ref, kx_ref, qb, hk), NT,
                                           preferred_element_type=F32), band(qb), C)

        ahead = 2
        s = [scores(*c) for c in chains[:ahead]]
        for n, (qb, hk) in enumerate(chains):
            if n + ahead < len(chains):
                s.append(scores(*chains[n + ahead]))
            s_ = s[n]
            sink_c = _sink_col(sink_ref, hk)
            m = jnp.maximum(jnp.max(s_, axis=1, keepdims=True), sink_c)
            p = jnp.exp(s_ - m)
            den = jnp.sum(p, axis=1, keepdims=True) + jnp.exp(sink_c - m)
            o = jnp.dot(p.astype(BF16), keys(vc_ref, vp_ref, vm_ref, vx_ref, qb, hk), preferred_element_type=F32) * (1.0 / den)
            lse = m + jnp.log(den)
            rows = slice(qb * BLK, (qb + 1) * BLK)
            for g in range(GQA_GROUP):
                h = hk * GQA_GROUP + g
                ya_ref[rows, h * HEAD_DIM:(h + 1) * HEAD_DIM] = o[g * BLK:(g + 1) * BLK, :].astype(BF16)
                lse_ref[rows, h:h + 1] = lse[g * BLK:(g + 1) * BLK, :]

    kv3 = [pl.BlockSpec((BLK, KV_W), lambda i: (jnp.maximum(nq * i - 1, 0), 0)),
           pl.BlockSpec((nq * BLK, KV_W), lambda i: (i, 0)),
           pl.BlockSpec((BLK, KV_W), lambda i: (jnp.minimum(nq * i + nq, nb - 1), 0))]
    return _call(
        body, name="fwd_attn", grid=(steps,),
        in_specs=[pl.BlockSpec(memory_space=pltpu.SMEM), _row(nq * BLK, Q_W)] + kv3 + kv3
                 + [_full((C, KV_W)), _full((C, KV_W)), _full((3, GQA_GROUP * BLK, 3 * BLK))],
        out_specs=[_row(nq * BLK, Q_W), _row(nq * BLK, N_Q_HEADS)],
        out_shape=[_sds((L, Q_W), BF16), _sds((L, N_Q_HEADS), F32)],
        args=(sink, q, k, k, k, v, v, v, kc, vc, bias), comm=comm)


GMLP_CHUNKS = 4


def _split_pair(t):
    low = lax.broadcasted_iota(jnp.int32, t.shape, 1) < GROUP_DIM
    zero = jnp.zeros_like(t)
    return jnp.where(low, t, zero), jnp.where(low, zero, t)


def _gmlp_spatial(w_ref, t_b, nch):
    rows = []
    for c in range(nch):
        tiles = []
        for pr in range(N_GROUPS // 2):
            lo, hi = _split_pair(t_b[c * BLK:(c + 1) * BLK, pr * 128:(pr + 1) * 128])
            tiles.append(jnp.dot(w_ref[2 * pr], lo, preferred_element_type=F32)
                         + jnp.dot(w_ref[2 * pr + 1], hi, preferred_element_type=F32))
        rows.append(jnp.concatenate(tiles, axis=1))
    return jnp.concatenate(rows, axis=0)


def _gmlp_fwd_vals(u, vb, lnv_ref, ws_ref, bsp_ref, nch):
    uf = u.astype(F32)
    vf = vb.astype(F32)
    gu, tu = _gelu(uf)
    gv, tv = _gelu(vf)
    vhat, rstd = _ln(gv)
    vn = (vhat * lnv_ref[0:1, :] + lnv_ref[1:2, :]).astype(BF16)
    s = _gmlp_spatial(ws_ref, vn, nch) + jnp.concatenate([bsp_ref[...]] * nch, axis=0)
    return uf, vf, gu, tu, tv, vhat, rstd, vn, s


def _k_gmlp(u, vb, lnv, ws, bsp):
    L = u.shape[0]
    nch = min(GMLP_CHUNKS, L // BLK)
    tm = nch * BLK

    def body(u_ref, vb_ref, lnv_ref, ws_ref, bsp_ref, yb_ref):
        _, _, gu, _, _, _, _, _, s = _gmlp_fwd_vals(u_ref[...], vb_ref[...], lnv_ref, ws_ref, bsp_ref, nch)
        yb_ref[...] = (gu * s).astype(BF16)

    return pl.pallas_call(
        body, name="fwd_gmlp", grid=(L // tm,),
        in_specs=[_row(tm, GM_W), _row(tm, GM_W), _full((8, GM_W)), _full((N_GROUPS, BLK, BLK)), _full((BLK, GM_W))],
        out_specs=_row(tm, GM_W),
        out_shape=_sds((L, GM_W), BF16),
        compiler_params=_params(("arbitrary",)),
    )(u, vb, lnv, ws, bsp)


def _k_merge(x, ya, yb, ga, gb, w_a, w_b, w_o, modv, lnv, tm):
    L = x.shape[0]

    def body(x_ref, ya_ref, yb_ref, ga_ref, gb_ref, wa_ref, wb_ref, wo_ref, mod_ref, ln_ref,
             mg_ref, mix_ref, xm_ref, h2_ref):
        a = jnp.dot(ya_ref[...], wa_ref[...], preferred_element_type=F32)
        b = jnp.dot(yb_ref[...], wb_ref[...], preferred_element_type=F32)
        merged = (_sigmoid(ga_ref[...].astype(F32)) * a + _sigmoid(gb_ref[...].astype(F32)) * b).astype(BF16)
        mg_ref[...] = merged
        mix = jnp.dot(merged, wo_ref[...], preferred_element_type=F32)
        mix_ref[...] = mix.astype(BF16)
        r1 = ALPHA * x_ref[...] + mod_ref[2:3, :] * mix
        r1hat, _ = _ln(r1)
        xm = r1hat * ln_ref[0:1, :] + ln_ref[1:2, :]
        xm_ref[...] = xm
        n2, _ = _ln(xm)
        h2_ref[...] = (n2 * (1.0 + mod_ref[4:5, :]) + mod_ref[3:4, :]).astype(BF16)

    return pl.pallas_call(
        body, name="fwd_merge", grid=(L // tm,),
        in_specs=[_row(tm, D), _row(tm, Q_W), _row(tm, GM_W), _row(tm, D), _row(tm, D),
                  _resident((Q_W, D)), _resident((GM_W, D)), _resident((D, D)), _full((8, D)), _full((8, D))],
        out_specs=[_row(tm, D)] * 4,
        out_shape=[_sds((L, D), BF16), _sds((L, D), BF16), _sds((L, D), F32), _sds((L, D), BF16)],
        compiler_params=_params(("arbitrary",)),
    )(x, ya, yb, ga, gb, w_a, w_b, w_o, modv, lnv)


FFN_CH = FFN_H // 2


def _k_ffn(h2, xm, tgt, w_fi, w_fo, modv, lnv, tm):
    L = h2.shape[0]

    def body(h2_ref, xm_ref, t_ref, wi_ref, wo_ref, mod_ref, ln_ref, gate_ref, up_ref, a_ref, dr2_ref, df_ref, acc_ref):
        @pl.when(pl.program_id(0) == 0)
        def _():
            acc_ref[...] = jnp.zeros_like(acc_ref)

        h2v = h2_ref[...]
        f = jnp.zeros((tm, D), F32)
        for j in range(FFN_H // FFN_CH):
            lo = j * FFN_CH
            gate = lax.dot_general(h2v, wi_ref[lo:lo + FFN_CH, :], NT, preferred_element_type=F32)
            up = lax.dot_general(h2v, wi_ref[FFN_H + lo:FFN_H + lo + FFN_CH, :], NT, preferred_element_type=F32)
            act = (gate * _sigmoid(gate) * up).astype(BF16)
            gate_ref[:, lo:lo + FFN_CH] = gate.astype(BF16)
            up_ref[:, lo:lo + FFN_CH] = up.astype(BF16)
            a_ref[:, lo:lo + FFN_CH] = act
            f = f + jnp.dot(act, wo_ref[lo:lo + FFN_CH, :], preferred_element_type=F32)
        gate2 = mod_ref[5:6, :]
        r2 = ALPHA * xm_ref[...] + gate2 * f
        r2hat, rstd = _ln(r2)
        y = r2hat * ln_ref[2:3, :] + ln_ref[3:4, :]
        err = y - t_ref[...]
        dy = err * (1.0 / D)
        dr2 = _ln_bwd(dy * ln_ref[2:3, :], r2hat, rstd)
        dr2_ref[...] = dr2
        df_ref[...] = (gate2 * dr2).astype(BF16)
        acc_ref[0:1, :] += _colsum(dy * r2hat)
        acc_ref[1:2, :] += _colsum(dy)
        acc_ref[2:3, :] += _colsum(dr2 * f)
        acc_ref[3:4, :] += _colsum(err * err) * (0.5 / D)

    return pl.pallas_call(
        body, name="fwd_ffn", grid=(L // tm,),
        in_specs=[_row(tm, D), _row(tm, D), _row(tm, D), _resident((2 * FFN_H, D)), _resident((FFN_H, D)),
                  _full((8, D)), _full((8, D))],
        out_specs=[_row(tm, FFN_H)] * 3 + [_row(tm, D), _row(tm, D), _full((8, D))],
        out_shape=[_sds((L, FFN_H), BF16)] * 3 + [_sds((L, D), F32), _sds((L, D), BF16), _sds((8, D), F32)],
        compiler_params=_params(("arbitrary",)),
    )(h2, xm, tgt, w_fi, w_fo, modv, lnv)


FFN_CH_BWD = FFN_CH


def _k_ffn_bwd(df, gate, up, xm, dr2, x, mix, w_fi, w_fo, modv, lnv, tm):
    L = df.shape[0]

    def body(df_ref, gate_ref, up_ref, xm_ref, dr2_ref, x_ref, mix_ref, wi_ref, wo_ref, mod_ref, ln_ref,
             dF_ref, dmix_ref, dxp_ref, acc_ref):
        @pl.when(pl.program_id(0) == 0)
        def _():
            acc_ref[...] = jnp.zeros_like(acc_ref)

        dfv = df_ref[...]
        ch = FFN_CH_BWD
        chunks = [j * ch for j in range(FFN_H // ch)]
        das = [lax.dot_general(dfv, wo_ref[lo:lo + ch, :], NT, preferred_element_type=F32) for lo in chunks]
        n2, rstd2 = _ln(xm_ref[...])
        mixf = mix_ref[...].astype(F32)
        gate1 = mod_ref[2:3, :]
        r1hat, rstd1 = _ln(ALPHA * x_ref[...] + gate1 * mixf)
        dh2 = jnp.zeros((tm, D), F32)
        for lo, da in zip(chunks, das):
            gate = gate_ref[:, lo:lo + ch].astype(F32)
            upv = up_ref[:, lo:lo + ch].astype(F32)
            sg = _sigmoid(gate)
            d_gate = (da * upv * (sg * (1.0 + gate * (1.0 - sg)))).astype(BF16)
            d_up = (da * (gate * sg)).astype(BF16)
            dF_ref[:, lo:lo + ch] = d_gate
            dF_ref[:, FFN_H + lo:FFN_H + lo + ch] = d_up
            dh2 = dh2 + jnp.dot(d_gate, wi_ref[lo:lo + ch, :], preferred_element_type=F32)
            dh2 = dh2 + jnp.dot(d_up, wi_ref[FFN_H + lo:FFN_H + lo + ch, :], preferred_element_type=F32)
        acc_ref[0:1, :] += _colsum(dh2)
        acc_ref[1:2, :] += _colsum(dh2 * n2)
        dxm = ALPHA * dr2_ref[...] + _ln_bwd(dh2 * (1.0 + mod_ref[4:5, :]), n2, rstd2)
        acc_ref[2:3, :] += _colsum(dxm * r1hat)
        acc_ref[3:4, :] += _colsum(dxm)
        dr1 = _ln_bwd(dxm * ln_ref[0:1, :], r1hat, rstd1)
        dmix_ref[...] = (gate1 * dr1).astype(BF16)
        dxp_ref[...] = ALPHA * dr1
        acc_ref[4:5, :] += _colsum(dr1 * mixf)

    return pl.pallas_call(
        body, name="bwd_ffn", grid=(L // tm,),
        in_specs=[_row(tm, D), _row(tm, FFN_H), _row(tm, FFN_H), _row(tm, D), _row(tm, D), _row(tm, D), _row(tm, D),
                  _resident((2 * FFN_H, D)), _resident((FFN_H, D)), _full((8, D)), _full((8, D))],
        out_specs=[_row(tm, 2 * FFN_H), _row(tm, D), _row(tm, D), _full((8, D))],
        out_shape=[_sds((L, 2 * FFN_H), BF16), _sds((L, D), BF16), _sds((L, D), F32), _sds((8, D), F32)],
        compiler_params=_params(("arbitrary",)),
    )(df, gate, up, xm, dr2, x, mix, w_fi, w_fo, modv, lnv)


def _k_merge_bwd(dmix, merged, ya, yb, ga, gb, w_a, w_b, w_o, tm):
    L = dmix.shape[0]
    n = L // tm

    def body(dmix_ref, mg_ref, ya_ref, yb_ref, ga_ref, gb_ref, wa_ref, wb_ref, wo_ref,
             dga_ref, dgb_ref, dya_ref, dyb_ref, gwa_ref, gwb_ref, gwo_ref, acc_a, acc_b, acc_o):
        i = pl.program_id(0)

        @pl.when(i == 0)
        def _():
            for r in (acc_a, acc_b, acc_o):
                r[...] = jnp.zeros_like(r)

        dmixv = dmix_ref[...]
        dmg = lax.dot_general(dmixv, wo_ref[...], NT, preferred_element_type=F32)
        acc_o[...] += lax.dot_general(mg_ref[...], dmixv, TN, preferred_element_type=F32)
        ya = ya_ref[...]
        a = jnp.dot(ya, wa_ref[...], preferred_element_type=F32)
        sa = _sigmoid(ga_ref[...].astype(F32))
        dA = (dmg * sa).astype(BF16)
        dga_ref[...] = (dmg * a * (sa * (1.0 - sa))).astype(BF16)
        dya_ref[...] = lax.dot_general(dA, wa_ref[...], NT, preferred_element_type=F32).astype(BF16)
        acc_a[...] += lax.dot_general(ya, dA, TN, preferred_element_type=F32)
        yb = yb_ref[...]
        b = jnp.dot(yb, wb_ref[...], preferred_element_type=F32)
        sb = _sigmoid(gb_ref[...].astype(F32))
        dB = (dmg * sb).astype(BF16)
        dgb_ref[...] = (dmg * b * (sb * (1.0 - sb))).astype(BF16)
        dyb_ref[...] = lax.dot_general(dB, wb_ref[...], NT, preferred_element_type=F32).astype(BF16)
        acc_b[...] += lax.dot_general(yb, dB, TN, preferred_element_type=F32)

        @pl.when(i == n - 1)
        def _():
            gwa_ref[...] = acc_a[...].astype(BF16)
            gwb_ref[...] = acc_b[...].astype(BF16)
            gwo_ref[...] = acc_o[...].astype(BF16)

    return pl.pallas_call(
        body, name="bwd_merge", grid=(n,),
        in_specs=[_row(tm, D), _row(tm, D), _row(tm, Q_W), _row(tm, GM_W), _row(tm, D), _row(tm, D),
                  _resident((Q_W, D)), _resident((GM_W, D)), _resident((D, D))],
        out_specs=[_row(tm, D), _row(tm, D), _row(tm, Q_W), _row(tm, GM_W), _full((Q_W, D)), _full((GM_W, D)), _full((D, D))],
        out_shape=[_sds((L, D), BF16), _sds((L, D), BF16), _sds((L, Q_W), BF16), _sds((L, GM_W), BF16),
                   _sds((Q_W, D), BF16), _sds((GM_W, D), BF16), _sds((D, D), BF16)],
        scratch_shapes=[pltpu.VMEM((Q_W, D), F32), pltpu.VMEM((GM_W, D), F32), pltpu.VMEM((D, D), F32)],
        compiler_params=_params(("arbitrary",)),
    )(dmix, merged, ya, yb, ga, gb, w_a, w_b, w_o)


def _k_gmlp_bwd(u, vb, dyb, lnv, ws, wst, bsp):
    L = u.shape[0]
    nch = min(GMLP_CHUNKS, L // BLK)
    tm = nch * BLK

    def body(u_ref, vb_ref, dyb_ref, lnv_ref, ws_ref, wst_ref, bsp_ref, du_ref, dvb_ref, gws_ref, gbst_ref, gln_ref):
        @pl.when(pl.program_id(0) == 0)
        def _():
            gws_ref[...] = jnp.zeros_like(gws_ref)
            gbst_ref[...] = jnp.zeros_like(gbst_ref)
            gln_ref[...] = jnp.zeros_like(gln_ref)

        uf, vf, gu, tu, tv, vhat, rstd, vn, s = _gmlp_fwd_vals(u_ref[...], vb_ref[...], lnv_ref, ws_ref, bsp_ref, nch)
        dyb_f = dyb_ref[...].astype(F32)
        du_ref[...] = (dyb_f * s * _gelu_grad(uf, tu)).astype(BF16)
        ds = dyb_f * gu
        ds_b = ds.astype(BF16)
        for pr in range(N_GROUPS // 2):
            lanes = slice(pr * 128, (pr + 1) * 128)
            gw_lo = gw_hi = ds_sum = None
            for c in range(nch):
                rows = slice(c * BLK, (c + 1) * BLK)
                lo, hi = _split_pair(ds_b[rows, lanes])
                t_lo = lax.dot_general(lo, vn[rows, lanes], NT, preferred_element_type=F32)
                t_hi = lax.dot_general(hi, vn[rows, lanes], NT, preferred_element_type=F32)
                gw_lo = t_lo if c == 0 else gw_lo + t_lo
                gw_hi = t_hi if c == 0 else gw_hi + t_hi
                ds_sum = ds[rows, lanes] if c == 0 else ds_sum + ds[rows, lanes]
            gws_ref[2 * pr] += gw_lo
            gws_ref[2 * pr + 1] += gw_hi
            b_lo, b_hi = _split_pair(ds_sum)
            gbst_ref[:, 2 * pr:2 * pr + 1] += jnp.sum(b_lo, axis=1, keepdims=True)
            gbst_ref[:, 2 * pr + 1:2 * pr + 2] += jnp.sum(b_hi, axis=1, keepdims=True)
        dvn = _gmlp_spatial(wst_ref, ds_b, nch)
        gln_ref[0:1, :] += _colsum(dvn * vhat)
        gln_ref[1:2, :] += _colsum(dvn)
        dgv = _ln_bwd(dvn * lnv_ref[0:1, :], vhat, rstd)
        dvb_ref[...] = (dgv * _gelu_grad(vf, tv)).astype(BF16)

    return pl.pallas_call(
        body, name="bwd_gmlp", grid=(L // tm,),
        in_specs=[_row(tm, GM_W)] * 3 + [_full((8, GM_W)), _full((N_GROUPS, BLK, BLK)), _full((N_GROUPS, BLK, BLK)),
                                         _full((BLK, GM_W))],
        out_specs=[_row(tm, GM_W), _row(tm, GM_W), _full((N_GROUPS, BLK, BLK)), _full((BLK, N_GROUPS)), _full((8, GM_W))],
        out_shape=[_sds((L, GM_W), BF16), _sds((L, GM_W), BF16), _sds((N_GROUPS, BLK, BLK), F32),
                   _sds((BLK, N_GROUPS), F32), _sds((8, GM_W), F32)],
        compiler_params=_params(("arbitrary",)),
    )(u, vb, dyb, lnv, ws, wst, bsp)


ATTN_BWD_BLOCKS = 2


def _k_attn_bwd(sink, q, k, v, kc, vc, dya, lse, cos, sin, bias, comm=None):
    L = q.shape[0]
    C = kc.shape[0]
    nb = L // BLK
    nq = min(ATTN_BWD_BLOCKS, nb)
    steps = nb // nq
    NK = C + 3 * BLK
    chains = [(qb, hk) for qb in range(nq) for hk in range(N_KV_HEADS)]

    def body(sink_ref, q_ref, kp_ref, km_ref, kx_ref, vp_ref, vm_ref, vx_ref, kc_ref, vc_ref, do_ref, lse_ref,
             cq_ref, sq_ref, cl_ref, sl_ref, bias_ref,
             dq_ref, dk_ref, dv_ref, dkc_ref, dvc_ref, dsink_ref,
             dq_scr, ck_scr, cv_scr, k1_acc, k2_acc, v1_acc, v2_acc):
        i = pl.program_id(0)

        @pl.when(i == 0)
        def _():
            for r in (k1_acc, k2_acc, v1_acc, v2_acc, dkc_ref, dvc_ref, dsink_ref):
                r[...] = jnp.zeros_like(r)

        @pl.when(i < steps)
        def _():
            def band(qb):
                first = jnp.where(i == 0, 0, 1) if qb == 0 else 1
                return bias_ref[jnp.where(i == steps - 1, 2, first) if qb == nq - 1 else first]

            def lanes(hk):
                return slice(hk * HEAD_DIM, (hk + 1) * HEAD_DIM)

            def keys(ctx_ref, p_ref, m_ref, x_ref, qb, hk):
                sl = lanes(hk)
                blocks = [p_ref[:, sl]] + [m_ref[j * BLK:(j + 1) * BLK, sl] for j in range(nq)] + [x_ref[:, sl]]
                return jnp.concatenate([ctx_ref[:, sl]] + blocks[qb:qb + 3], axis=0)

            def stacked(ref, qb, hk, width):
                return jnp.concatenate(
                    [ref[qb * BLK:(qb + 1) * BLK, (hk * GQA_GROUP + g) * width:(hk * GQA_GROUP + g + 1) * width]
                     for g in range(GQA_GROUP)], axis=0)

            def scores(qb, hk):
                kcat = keys(kc_ref, kp_ref, km_ref, kx_ref, qb, hk)
                qg = stacked(q_ref, qb, hk, HEAD_DIM)
                s = _masked(lax.dot_general(qg, kcat, NT, preferred_element_type=F32), band(qb), C)
                dog = stacked(do_ref, qb, hk, HEAD_DIM)
                dp = lax.dot_general(dog, keys(vc_ref, vp_ref, vm_ref, vx_ref, qb, hk), NT, preferred_element_type=F32)
                return kcat, qg, dog, s, dp

            def softmax_bwd(qb, hk, s, dp):
                lse_c = stacked(lse_ref, qb, hk, 1)
                p = jnp.exp(s - lse_c)
                delta = jnp.sum(p * dp, axis=1, keepdims=True)
                ds = (p * (dp - delta)).astype(BF16)
                p_sink = jnp.exp(_sink_col(sink_ref, hk) - lse_c) * delta
                return p.astype(BF16), ds, p_sink

            def put_dq(qb, hk, dqs, p_sink):
                for g in range(GQA_GROUP):
                    h = hk * GQA_GROUP + g
                    dq_scr[qb * BLK:(qb + 1) * BLK, h * HEAD_DIM:(h + 1) * HEAD_DIM] = dqs[g * BLK:(g + 1) * BLK, :]
                    tot = jnp.sum(p_sink[g * BLK:(g + 1) * BLK, :], axis=0, keepdims=True)
                    dsink_ref[h:h + 1, :] -= jnp.broadcast_to(tot, (1, 128))

            ahead = 4
            sc = [scores(*c) for c in chains[:ahead]]
            pending = None
            for n, (qb, hk) in enumerate(chains):
                if n + ahead < len(chains):
                    sc.append(scores(*chains[n + ahead]))
                kcat, qg, dog, s, dp = sc[n]
                pb, ds, p_sink = softmax_bwd(qb, hk, s, dp)
                if pending is not None:
                    pqb, phk, pds, ppb, pqg, pdog = pending
                    ck_scr[pqb, :, lanes(phk)] = lax.dot_general(pds, pqg, TN, preferred_element_type=F32)
                    cv_scr[pqb, :, lanes(phk)] = lax.dot_general(ppb, pdog, TN, preferred_element_type=F32)
                put_dq(qb, hk, jnp.dot(ds, kcat, preferred_element_type=F32), p_sink)
                pending = (qb, hk, ds, pb, qg, dog)
            pqb, phk, pds, ppb, pqg, pdog = pending
            ck_scr[pqb, :, lanes(phk)] = lax.dot_general(pds, pqg, TN, preferred_element_type=F32)
            cq, sq = cq_ref[...], sq_ref[...]
            for j in range(4):
                dq_ref[:, j * 128:(j + 1) * 128] = _unrope(dq_scr[:, j * 128:(j + 1) * 128] * Q_SCALE, cq, sq).astype(BF16)
            cv_scr[pqb, :, lanes(phk)] = lax.dot_general(ppb, pdog, TN, preferred_element_type=F32)
            dkc_ref[...] += functools.reduce(lambda a, b: a + b, [ck_scr[qb, 0:C, :] for qb in range(nq)])
            dvc_ref[...] += functools.reduce(lambda a, b: a + b, [cv_scr[qb, 0:C, :] for qb in range(nq)])

        @pl.when(i >= steps)
        def _():
            ck_scr[...] = jnp.zeros_like(ck_scr)
            cv_scr[...] = jnp.zeros_like(cv_scr)

        def slot(scr, r, carried):
            parts = [scr[qb, C + (r - qb) * BLK:C + (r - qb + 1) * BLK, :] for qb in range(nq) if 0 <= r - qb <= 2]
            total = functools.reduce(lambda a, b: a + b, parts)
            return total if carried is None else carried[...] + total

        for r in range(nq):
            rows = slice(r * BLK, (r + 1) * BLK)
            carried_k, carried_v = ((k1_acc, v1_acc), (k2_acc, v2_acc), (None, None))[min(r, 2)]
            tables = (cl_ref[...], sl_ref[...]) if r == 0 else (cq_ref[(r - 1) * BLK:r * BLK, :], sq_ref[(r - 1) * BLK:r * BLK, :])
            dk_ref[rows, :] = _unrope(slot(ck_scr, r, carried_k), *tables).astype(BF16)
            dv_ref[rows, :] = slot(cv_scr, r, carried_v).astype(BF16)
        k1_acc[...] = slot(ck_scr, nq, None)
        v1_acc[...] = slot(cv_scr, nq, None)
        k2_acc[...] = slot(ck_scr, nq + 1, None)
        v2_acc[...] = slot(cv_scr, nq + 1, None)

    last = steps - 1
    kv3 = [pl.BlockSpec((BLK, KV_W), lambda i: (jnp.clip(nq * i - 1, 0, nb - 1), 0)),
           pl.BlockSpec((nq * BLK, KV_W), lambda i: (jnp.minimum(i, last), 0)),
           pl.BlockSpec((BLK, KV_W), lambda i: (jnp.minimum(nq * i + nq, nb - 1), 0))]
    cur = lambda w: pl.BlockSpec((nq * BLK, w), lambda i: (jnp.minimum(i, last), 0))
    late = lambda w: pl.BlockSpec((BLK, w), lambda i: (jnp.clip(nq * i - 1, 0, nb - 1), 0))
    out2 = lambda w: pl.BlockSpec((nq * BLK, w), lambda i: (i, 0))
    return _call(
        body, name="bwd_attn", grid=(steps + 1,),
        in_specs=[pl.BlockSpec(memory_space=pltpu.SMEM), cur(Q_W)] + kv3 + kv3
                 + [_full((C, KV_W)), _full((C, KV_W)), cur(Q_W), cur(N_Q_HEADS), cur(128), cur(128), late(128), late(128),
                    _full((3, GQA_GROUP * BLK, 3 * BLK))],
        out_specs=[cur(Q_W), out2(KV_W), out2(KV_W), _full((C, KV_W)), _full((C, KV_W)), _full((8, 128))],
        out_shape=[_sds((L, Q_W), BF16), _sds((L + nq * BLK, KV_W), BF16), _sds((L + nq * BLK, KV_W), BF16),
                   _sds((C, KV_W), F32), _sds((C, KV_W), F32), _sds((8, 128), F32)],
        scratch=[pltpu.VMEM((nq * BLK, Q_W), F32), pltpu.VMEM((nq, NK, KV_W), F32), pltpu.VMEM((nq, NK, KV_W), F32)]
                + [pltpu.VMEM((BLK, KV_W), F32)] * 4,
        args=(sink, q, k, k, k, v, v, v, kc, vc, dya, lse, cos, sin, cos, sin, bias), comm=comm)


def _k_ctx_bwd(ctx, modc, hc, dkc, dvc, w_kv):
    C = ctx.shape[0]

    def body(c_ref, mod_ref, hc_ref, dkc_ref, dvc_ref, w_ref, gw_ref, dmod_ref):
        dkv = jnp.concatenate([dkc_ref[...], dvc_ref[...]], axis=1).astype(BF16)
        gw_ref[...] = lax.dot_general(dkv, hc_ref[...], TN, preferred_element_type=F32)
        dhc = jnp.dot(dkv, w_ref[...], preferred_element_type=F32)
        n, _ = _ln(c_ref[...])
        dmod_ref[...] = jnp.zeros_like(dmod_ref)
        dmod_ref[0:1, :] = _colsum(dhc)
        dmod_ref[1:2, :] = _colsum(dhc * n)

    return pl.pallas_call(
        body, name="bwd_ctx", grid=(1,),
        in_specs=[_full((C, D)), _full((8, D)), _full((C, D)), _full((C, KV_W)), _full((C, KV_W)), _full((2 * KV_W, D))],
        out_specs=[_full((2 * KV_W, D)), _full((8, D))],
        out_shape=[_sds((2 * KV_W, D), F32), _sds((8, D), F32)],
        compiler_params=_params(("arbitrary",)),
    )(ctx, modc, hc, dkc, dvc, w_kv)


def _k_in_bwd(dq, dk, dv, du, dvb, dga, dgb, x, dxp, w_in, modv, tm, comm=None):
    L = x.shape[0]
    parts = [(O_Q, Q_W), (O_K, KV_W), (O_V, KV_W), (O_U, GM_W), (O_VB, GM_W), (O_GA, D), (O_GB, D)]

    def body(dq_ref, dk_ref, dv_ref, du_ref, dvb_ref, dga_ref, dgb_ref, x_ref, dxp_ref, w_ref, mod_ref,
             dP_ref, gx_ref, acc_ref):
        @pl.when(pl.program_id(0) == 0)
        def _():
            acc_ref[...] = jnp.zeros_like(acc_ref)

        for (lo, width), r in zip(parts, (dq_ref, dk_ref, dv_ref, du_ref, dvb_ref, dga_ref, dgb_ref)):
            dP_ref[:, lo:lo + width] = r[...]
        n1, rstd1 = _ln(x_ref[...])
        dh = jnp.dot(dP_ref[...], w_ref[...], preferred_element_type=F32)
        acc_ref[0:1, :] += _colsum(dh)
        acc_ref[1:2, :] += _colsum(dh * n1)
        gx_ref[...] = dxp_ref[...] + _ln_bwd(dh * (1.0 + mod_ref[1:2, :]), n1, rstd1)

    return _call(
        body, name="bwd_in", grid=(L // tm,),
        in_specs=[_row(tm, w) for _, w in parts] + [_row(tm, D), _row(tm, D), _resident((IN_W, D)), _full((8, D))],
        out_specs=[_row(tm, IN_W), _row(tm, D), _full((8, D))],
        out_shape=[_sds((L, IN_W), BF16), _sds((L, D), F32), _sds((8, D), F32)],
        args=(dq, dk, dv, du, dvb, dga, dgb, x, dxp, w_in, modv), comm=comm)


def _wgrad(a, b, name, tk, tt, comm=None, extra=None):
    T, K = a.shape
    N = b.shape[1]
    nt = T // tt

    def body(*refs):
        a_ref, b_ref = refs[:2]
        o_ref, acc_ref = refs[-2:]
        j, t = pl.program_id(0), pl.program_id(1)

        @pl.when(t == 0)
        def _():
            acc_ref[...] = jnp.zeros_like(acc_ref)

        acc_ref[...] += lax.dot_general(a_ref[...], b_ref[...], TN, preferred_element_type=F32)

        if extra is not None:
            lo, rows = extra[0] % tk, extra[1].shape[0]

            @pl.when((t == nt - 1) & (j == extra[0] // tk))
            def _():
                acc_ref[lo:lo + rows, :] += refs[2][...]

        @pl.when(t == nt - 1)
        def _():
            o_ref[...] = acc_ref[...].astype(BF16)

    extra_specs = [] if extra is None else [pl.BlockSpec(extra[1].shape, lambda j, t: (0, 0))]
    (out,), got = _call(
        body, name=name, grid=(K // tk, nt),
        in_specs=[pl.BlockSpec((tt, tk), lambda j, t: (t, j)), pl.BlockSpec((tt, N), lambda j, t: (t, 0))] + extra_specs,
        out_specs=[pl.BlockSpec((tk, N), lambda j, t: (j, 0))],
        out_shape=[_sds((K, N), BF16)],
        scratch=[pltpu.VMEM((tk, N), F32)],
        args=(a, b) + (() if extra is None else (extra[1],)), comm=comm)
    return (out, got) if comm is not None else out


def _adamw_reduce(parts, w, m, v, name, tr):
    R, C = w.shape
    n_parts = parts.shape[0]

    def body(p_ref, w_ref, m_ref, v_ref, g_ref, d_ref, m2_ref, v2_ref):
        g = p_ref[0].astype(F32)
        for i in range(1, n_parts):
            g = g + p_ref[i].astype(F32)
        delta, m2, v2 = _adamw(w_ref[...], g, m_ref[...], v_ref[...])
        g_ref[...] = g
        d_ref[...] = delta
        m2_ref[...] = m2
        v2_ref[...] = v2

    spec = _row(tr, C)
    return pl.pallas_call(
        body, name=name, grid=(R // tr,),
        in_specs=[pl.BlockSpec((n_parts, tr, C), lambda i: (0, i, 0)), spec, spec, spec],
        out_specs=[spec] * 4,
        out_shape=[_sds((R, C), F32)] * 4,
        compiler_params=_params(("arbitrary",)),
    )(parts, w, m, v)


SMALL_ORDER = ("b_ada", "ln1_g", "ln1_b", "ln2_g", "ln2_b", "gmlp_ln_g", "gmlp_ln_b", "b_spatial", "attn_sink")


def _small_step(gath, params):
    flat = [a for name in SMALL_ORDER for a in params[name]]

    def grad_of(tot, name):
        if name == "b_ada":
            return jnp.concatenate([tot[r:r + 1, :] for r in range(6)], axis=1)
        if name in ("ln1_g", "ln1_b", "ln2_g", "ln2_b"):
            r = 8 + ("ln1_g", "ln1_b", "ln2_g", "ln2_b").index(name)
            return tot[r:r + 1, :]
        if name == "gmlp_ln_g":
            return tot[12:13, :GM_W]
        if name == "gmlp_ln_b":
            return tot[12:13, GM_W:]
        if name == "b_spatial":
            return jnp.concatenate([tot[13:14, g * BLK:(g + 1) * BLK] for g in range(N_GROUPS)], axis=0)[None]
        return tot[14:15, :N_Q_HEADS]

    def body(*refs):
        g_ref, in_refs = refs[0], refs[1:1 + len(flat)]
        tot_ref, out_refs = refs[1 + len(flat)], refs[2 + len(flat):]
        tot = g_ref[0]
        for i in range(1, N_DEV):
            tot = tot + g_ref[i]
        tot_ref[...] = tot
        tot_ref[0:2, :] = tot[0:2, :] + tot[6:8, :]
        tot_ref[15:16, :] = jnp.broadcast_to(jnp.sum(tot[15:16, :], axis=1, keepdims=True), (1, D))
        tot = tot_ref[...]
        for k, name in enumerate(SMALL_ORDER):
            w_ref, m_ref, v_ref = in_refs[3 * k:3 * k + 3]
            g = grad_of(tot, name)
            delta, m2, v2 = _adamw(w_ref[...], g, m_ref[...], v_ref[...])
            for r, val in zip(out_refs[4 * k:4 * k + 4], (g, delta, m2, v2)):
                r[...] = val

    res = pl.pallas_call(
        body, name="small_step", grid=(1,),
        in_specs=[_full((N_DEV, 16, D))] + [_full(a.shape) for a in flat],
        out_specs=[_full((16, D))] + [_full(params[name][0].shape) for name in SMALL_ORDER for _ in range(4)],
        out_shape=[_sds((16, D), F32)] + [_sds(params[name][0].shape, F32) for name in SMALL_ORDER for _ in range(4)],
        compiler_params=_params(("arbitrary",)),
    )(gath, *flat)
    return res[0], {name: res[1 + 4 * k:5 + 4 * k] for k, name in enumerate(SMALL_ORDER)}


def _cctx_finish(gath, c_ctx, m, v):
    def body(g_ref, c_ref, m_ref, v_ref, gr_ref, d_ref, m2_ref, v2_ref):
        ds = g_ref[0]
        for i in range(1, N_DEV):
            ds = ds + g_ref[i]
        c = c_ref[...]
        sg = _sigmoid(c)
        g = ds * (sg * (1.0 + c * (1.0 - sg)))
        delta, m2, v2 = _adamw(c, g, m_ref[...], v_ref[...])
        gr_ref[...] = g
        d_ref[...] = delta
        m2_ref[...] = m2
        v2_ref[...] = v2

    return pl.pallas_call(
        body, name="cctx_finish", grid=(1,),
        in_specs=[_full((N_DEV, 8, D))] + [_full((8, D))] * 3, out_specs=[_full((8, D))] * 4,
        out_shape=[_sds((8, D), F32)] * 4,
        compiler_params=_params(("arbitrary",)),
    )(gath, c_ctx, m, v)


def _pad_rows(a, rows):
    return jnp.concatenate([a, jnp.zeros((rows - a.shape[0], a.shape[1]), a.dtype)], axis=0)


def kernel(x, c, ctx, c_ctx, w_ada, b_ada, w_in, attn_sink, gmlp_ln_g, gmlp_ln_b, w_spatial, b_spatial, w_branch_a, w_branch_b, w_out, ln1_g, ln1_b, w_ffn_in, w_ffn_out, ln2_g, ln2_b, loss_target, m_c_ctx, m_w_ada, m_b_ada, m_w_in, m_attn_sink, m_gmlp_ln_g, m_gmlp_ln_b, m_w_spatial, m_b_spatial, m_w_branch_a, m_w_branch_b, m_w_out, m_ln1_g, m_ln1_b, m_w_ffn_in, m_w_ffn_out, m_ln2_g, m_ln2_b, v_c_ctx, v_w_ada, v_b_ada, v_w_in, v_attn_sink, v_gmlp_ln_g, v_gmlp_ln_b, v_w_spatial, v_b_spatial, v_w_branch_a, v_w_branch_b, v_w_out, v_ln1_g, v_ln1_b, v_w_ffn_in, v_w_ffn_out, v_ln2_g, v_ln2_b):
    L = x.shape[1]
    me = 4 * lax.axis_index("x") + 2 * lax.axis_index("y") + lax.axis_index("c")
    x2, tgt, ctx2 = x[0], loss_target[0], ctx[0]
    tiles = _Tiles(L)
    tm_in, tm, tt = tiles.wide, tiles.narrow, tiles.tokens

    transposed = ("w_in", "w_ffn_in")
    tr = lambda kname, a: a.T if kname in transposed else a
    big = dict(w_in=w_in[0].T, w_branch_a=w_branch_a[0], w_branch_b=w_branch_b[0], w_out=w_out[0],
               w_ffn_in=w_ffn_in[0].T, w_ffn_out=w_ffn_out[0])
    col_sharded = ("w_branch_a", "w_branch_b")
    shard_bf = {k: a.astype(BF16) for k, a in big.items()}

    def assemble(kname, g):
        if kname in col_sharded:
            return g.transpose(1, 0, 2).reshape(g.shape[1], N_DEV * g.shape[2])
        return g.reshape(N_DEV * g.shape[1], g.shape[2])

    def to_blocks(kname, g):
        if kname in col_sharded:
            return g.reshape(g.shape[0], N_DEV, g.shape[1] // N_DEV).transpose(1, 0, 2)
        return g.reshape(N_DEV, g.shape[0] // N_DEV, g.shape[1])

    full = {}
    n_ada = w_ada.shape[2]
    b_my = lax.dynamic_slice(b_ada, (0, me * n_ada), (1, n_ada))
    act, mod_all, got = _prologue(_pad_rows(c, 8), _pad_rows(c_ctx[None, :], 8), w_ada[0], b_my,
                                  _Comm(gather=[shard_bf["w_in"]]))
    full["w_in"] = assemble("w_in", got[0])
    mod_all = mod_all.transpose(1, 0, 2).reshape(16, 6 * D)
    modv = _pad_rows(lax.dynamic_slice(mod_all, (me, 0), (1, 6 * D)).reshape(6, D), 8)
    modc = _pad_rows(mod_all[8].reshape(6, D), 8)

    lnv = _pad_rows(jnp.concatenate([ln1_g, ln1_b, ln2_g, ln2_b], axis=0), 8)
    gm_lnv = _pad_rows(jnp.concatenate([gmlp_ln_g, gmlp_ln_b], axis=0), 8)
    ws_b = w_spatial[0].astype(BF16)
    wst_b = ws_b.transpose(0, 2, 1)
    bsp = jnp.repeat(b_spatial[0].T, GROUP_DIM, axis=1)
    sink = attn_sink[0]
    cos, sin = _rope_tables(L)
    bias = _attn_bias()
    w_kv = full["w_in"][O_K:O_K + 2 * KV_W, :]

    (h, q, k, v, u, vb, ga, gb), got = _k_in(
        x2, modv, full["w_in"], cos, sin, tiles.widest,
        comm=_Comm(gather=[shard_bf[kname] for kname in ("w_branch_a", "w_branch_b", "w_out", "w_ffn_out")]))
    for kname, g in zip(("w_branch_a", "w_branch_b", "w_out", "w_ffn_out"), got):
        full[kname] = assemble(kname, g)
    hc, kc, vc = _k_ctx(ctx2, modc, w_kv)
    (ya, lse), got = _k_attn(sink, q, k, v, kc, vc, bias, comm=_Comm(gather=[shard_bf["w_ffn_in"]]))
    full["w_ffn_in"] = assemble("w_ffn_in", got[0])
    yb = _k_gmlp(u, vb, gm_lnv, ws_b, bsp)
    merged, mix, xm, h2 = _k_merge(x2, ya, yb, ga, gb, full["w_branch_a"], full["w_branch_b"], full["w_out"], modv, lnv,
                                   tiles.widest)
    gate, up, act_f, dr2, df, acc_f = _k_ffn(h2, xm, tgt, full["w_ffn_in"], full["w_ffn_out"], modv, lnv, tm_in)

    dF, dmix, dxp, acc_b = _k_ffn_bwd(df, gate, up, xm, dr2, x2, mix, full["w_ffn_in"], full["w_ffn_out"], modv, lnv, tm)
    blk_fo = to_blocks("w_ffn_out", _wgrad(act_f, df, "wgrad_ffn_out", tiles.tk_ffn, tt))
    gw_fi, (rcv_fo,) = _wgrad(dF, h2, "wgrad_ffn_in", tiles.tk_ffn, tt, comm=_Comm(scatter=[blk_fo]))
    blk_fi = to_blocks("w_ffn_in", gw_fi)
    dga, dgb, dya, dyb, gw_a, gw_b, gw_o = _k_merge_bwd(
        dmix, merged, ya, yb, ga, gb, full["w_branch_a"], full["w_branch_b"], full["w_out"], tm_in)
    du, dvb, g_ws, g_bst, g_gln = _k_gmlp_bwd(u, vb, dyb, gm_lnv, ws_b, wst_b, bsp)
    (dq, dk_late, dv_late, dkc, dvc, g_sink), (gath_ws, rcv_fi) = _k_attn_bwd(
        sink, q, k, v, kc, vc, dya, lse, cos, sin, bias,
        comm=_Comm(gather=[g_ws.reshape(N_GROUPS * BLK, BLK)], scatter=[blk_fi]))
    dk, dv = dk_late[BLK:BLK + L], dv_late[BLK:BLK + L]
    blk_a, blk_b, blk_o = to_blocks("w_branch_a", gw_a), to_blocks("w_branch_b", gw_b), to_blocks("w_out", gw_o)
    (dP, grad_x, acc_i), _ = _k_in_bwd(dq, dk, dv, du, dvb, dga, dgb, x2, dxp, full["w_in"], modv, tm_in)
    g_ctx, dmodc = _k_ctx_bwd(ctx2, modc, hc, dkc, dvc, w_kv)
    gw_in, (rcv_a, rcv_b, rcv_o) = _wgrad(dP, h, "wgrad_in", tiles.tk_in, tt, comm=_Comm(scatter=[blk_a, blk_b, blk_o]),
                                          extra=(O_K, g_ctx))

    dmod_x = jnp.concatenate([acc_i[0:2], acc_b[4:5], acc_b[0:2], acc_f[2:3]], axis=0)
    small = jnp.concatenate([
        dmod_x, dmodc[0:2], acc_b[2:4], acc_f[0:2],
        jnp.concatenate([g_gln[0:1], g_gln[1:2]], axis=1), g_bst.T.reshape(1, D),
        _pad_rows(g_sink[:, 0:1], D).T, acc_f[3:4]], axis=0)
    rcv_in, gath = _exchange_two_level(to_blocks("w_in", gw_in), small, "exchange_last")
    received = dict(w_in=rcv_in, w_branch_a=rcv_a, w_branch_b=rcv_b, w_out=rcv_o, w_ffn_in=rcv_fi, w_ffn_out=rcv_fo)
    moments = dict(w_in=(m_w_in, v_w_in), w_branch_a=(m_w_branch_a, v_w_branch_a), w_branch_b=(m_w_branch_b, v_w_branch_b),
                   w_out=(m_w_out, v_w_out), w_ffn_in=(m_w_ffn_in, v_w_ffn_in), w_ffn_out=(m_w_ffn_out, v_w_ffn_out))
    names = list(big)
    res = {}
    for kname in names:
        mm, vv = moments[kname]
        R = big[kname].shape[0]
        res[kname] = [tr(kname, r) for r in _adamw_reduce(
            received[kname], big[kname], tr(kname, mm[0]), tr(kname, vv[0]), "adamw_" + kname, 256 if R % 256 == 0 else R // 2)]

    ws2d = lambda a: a.reshape(N_GROUPS * BLK, BLK)
    res_ws = [r.reshape(w_spatial.shape) for r in _adamw_reduce(
        gath_ws, ws2d(w_spatial), ws2d(m_w_spatial), ws2d(v_w_spatial), "adamw_w_spatial", 256)]
    tot, res_small = _small_step(gath, dict(
        b_ada=(b_ada, m_b_ada, v_b_ada), ln1_g=(ln1_g, m_ln1_g, v_ln1_g), ln1_b=(ln1_b, m_ln1_b, v_ln1_b),
        ln2_g=(ln2_g, m_ln2_g, v_ln2_g), ln2_b=(ln2_b, m_ln2_b, v_ln2_b),
        gmlp_ln_g=(gmlp_ln_g, m_gmlp_ln_g, v_gmlp_ln_g), gmlp_ln_b=(gmlp_ln_b, m_gmlp_ln_b, v_gmlp_ln_b),
        b_spatial=(b_spatial, m_b_spatial, v_b_spatial), attn_sink=(attn_sink, m_attn_sink, v_attn_sink)))
    loss = tot[15, 0]

    dmod_rows = jnp.concatenate([gath[:, 0:6, :].reshape(N_DEV, 6 * D),
                                 jnp.concatenate([tot[6:8].reshape(1, 2 * D), jnp.zeros((1, 4 * D), F32)], axis=1),
                                 jnp.zeros((7, 6 * D), F32)], axis=0)
    dmod_my = lax.dynamic_slice(dmod_rows, (0, me * n_ada), (16, n_ada))
    g_wada, d_wada, m2_wada, v2_wada, pc = _ada_bwd(act, dmod_my, w_ada[0], m_w_ada[0], v_w_ada[0])
    pc_all = _gather_rows(pc, "gather_cctx")
    cc8 = lambda a: _pad_rows(a.reshape(1, D), 8)
    g_cc, d_cc, m2_cc, v2_cc = _cctx_finish(pc_all, cc8(c_ctx), cc8(m_c_ctx), cc8(v_c_ctx))

    order = ["c_ctx", "w_ada", "b_ada", "w_in", "attn_sink", "gmlp_ln_g", "gmlp_ln_b", "w_spatial", "b_spatial",
             "w_branch_a", "w_branch_b", "w_out", "ln1_g", "ln1_b", "w_ffn_in", "w_ffn_out", "ln2_g", "ln2_b"]
    grads, deltas, new_m, new_v = {}, {}, {}, {}
    grads["c_ctx"], deltas["c_ctx"], new_m["c_ctx"], new_v["c_ctx"] = g_cc[0], d_cc[0], m2_cc[0], v2_cc[0]
    grads["w_ada"], deltas["w_ada"], new_m["w_ada"], new_v["w_ada"] = g_wada[None], d_wada[None], m2_wada[None], v2_wada[None]
    for kname in names:
        g, d, m2, v2 = res[kname]
        grads[kname], deltas[kname], new_m[kname], new_v[kname] = g[None], d[None], m2[None], v2[None]
    grads["w_spatial"], deltas["w_spatial"], new_m["w_spatial"], new_v["w_spatial"] = res_ws
    for kname in SMALL_ORDER:
        grads[kname], deltas[kname], new_m[kname], new_v[kname] = res_small[kname]
    return (loss, grad_x[None], *[grads[n] for n in order], *[deltas[n] for n in order],
            *[new_m[n] for n in order], *[new_v[n] for n in order])
```

```python
import functools
import math

import jax
import jax.numpy as jnp
import numpy as np
from jax import lax
from jax.experimental import pallas as pl
from jax.experimental.pallas import tpu as pltpu

F32 = jnp.float32
BF16 = jnp.bfloat16
MESH = pl.DeviceIdType.MESH

N_DEV = 8
D = 1024
HEAD_DIM = 64
N_Q_HEADS = 8
N_KV_HEADS = 2
GQA_GROUP = 4
BLK = 128
Q_W = 512
KV_W = 128
GM_W = 512
N_GROUPS = 8
GROUP_DIM = 64
FFN_H = 2816
IN_W = 3840
O_Q, O_K, O_V, O_U, O_VB, O_GA, O_GB = 0, 512, 640, 768, 1280, 1792, 2816
LN_EPS = 1e-5
NEG_INF = -1e30
ALPHA = 2.0 ** 0.25
ROPE_BASE = 10000.0
ROPE_PAIRS = 16
Q_SCALE = HEAD_DIM ** -0.5
GELU_K0 = math.sqrt(2.0 / math.pi)
GELU_K1 = 0.044715

ADAM_LR = 0.001
ADAM_B1 = 0.9
ADAM_B2 = 0.999
ADAM_EPS = 1e-08
ADAM_WD = 0.01
ADAM_STEP = 10

V7X_VMEM_BYTES = 64 * 1024 * 1024
VMEM_LIMIT = V7X_VMEM_BYTES * 7 // 8
NT = (((1,), (1,)), ((), ()))
TN = (((0,), (0,)), ((), ()))


class _Tiles:
    def __init__(self, L):
        self.wide = min(512, L)
        self.narrow = min(256, L)
        self.tokens = min(2048, L)
        self.tk_in = IN_W // 3
        self.tk_ffn = FFN_H // 2


def _params(sem=None):
    return pltpu.CompilerParams(dimension_semantics=sem, vmem_limit_bytes=VMEM_LIMIT)


def _row(tm, w):
    return pl.BlockSpec((tm, w), lambda i: (i, 0))


def _full(shape):
    nd = len(shape)
    return pl.BlockSpec(shape, lambda i: (0,) * nd)


def _resident(shape):
    nd = len(shape)
    return pl.BlockSpec(shape, lambda i: (0,) * nd, pipeline_mode=pl.Buffered(1))


def _sds(shape, dt):
    return jax.ShapeDtypeStruct(shape, dt)


def _ln(xf):
    mu = jnp.mean(xf, axis=-1, keepdims=True)
    xc = xf - mu
    var = jnp.mean(xc * xc, axis=-1, keepdims=True)
    rstd = lax.rsqrt(var + LN_EPS)
    return xc * rstd, rstd


def _ln_bwd(dn, n, rstd):
    m1 = jnp.mean(dn, axis=-1, keepdims=True)
    m2 = jnp.mean(dn * n, axis=-1, keepdims=True)
    return rstd * (dn - m1 - n * m2)


def _colsum(t):
    return jnp.sum(t, axis=0, keepdims=True)


def _sigmoid(x):
    return 0.5 * jnp.tanh(0.5 * x) + 0.5


def _gelu(x):
    t = jnp.tanh(GELU_K0 * (x + GELU_K1 * (x * x * x)))
    return x * (0.5 * (1.0 + t)), t


def _gelu_grad(x, t):
    return 0.5 * (1.0 + t) + 0.5 * x * (1.0 - t * t) * (GELU_K0 * (1.0 + 3.0 * GELU_K1 * x * x))


def _swap16(t):
    lane = lax.broadcasted_iota(jnp.int32, t.shape, 1)
    return jnp.where((lane & 16) == 0, pltpu.roll(t, 112, 1), pltpu.roll(t, 16, 1))


def _rope(t, cos, sin):
    return t * cos + _swap16(t) * sin


def _unrope(t, cos, sin):
    return t * cos - _swap16(t) * sin


def _adamw(w, g, m, v):
    m2 = ADAM_B1 * m + (1.0 - ADAM_B1) * g
    v2 = ADAM_B2 * v + (1.0 - ADAM_B2) * (g * g)
    m_hat = m2 / (1.0 - ADAM_B1 ** ADAM_STEP)
    v_hat = v2 / (1.0 - ADAM_B2 ** ADAM_STEP)
    delta = -ADAM_LR * (m_hat / (jnp.sqrt(v_hat) + ADAM_EPS) + ADAM_WD * w)
    return delta, m2, v2


def _rope_tables(L):
    inv = (np.float32(ROPE_BASE) ** (-np.arange(ROPE_PAIRS, dtype=np.float32) / np.float32(ROPE_PAIRS))).astype(np.float32)
    t = np.arange(L, dtype=np.int32)
    rows = (t // 64).astype(np.float32)[:, None] * inv
    cols = (t % 64).astype(np.float32)[:, None] * inv
    cr, sr, cc, sc = np.cos(rows), np.sin(rows), np.cos(cols), np.sin(cols)
    cos = np.concatenate([cr, cr, cc, cc], axis=1)
    sin = np.concatenate([-sr, sr, -sc, sc], axis=1)
    return jnp.asarray(np.tile(cos, (1, 2)), F32), jnp.asarray(np.tile(sin, (1, 2)), F32)


def _me():
    return lax.axis_index("x"), lax.axis_index("y"), lax.axis_index("c")


def _peer(mx, my, mc, k):
    return (mx ^ ((k >> 2) & 1), my ^ ((k >> 1) & 1), mc ^ (k & 1))


class _Comm:
    def __init__(self, gather=(), scatter=(), spread=()):
        self.kinds = ["gather"] * len(gather) + ["scatter"] * len(scatter) + ["spread"] * len(spread)
        self.args = list(gather) + list(scatter) + list(spread)
        self.n = len(self.args)

    def out_shape(self):
        return [_sds(a.shape if k == "scatter" else (N_DEV,) + a.shape, a.dtype) for k, a in zip(self.kinds, self.args)]

    def specs(self):
        return [pl.BlockSpec(memory_space=pl.ANY)] * self.n

    def scratch(self):
        return [pltpu.SemaphoreType.DMA((7 * self.n,)), pltpu.SemaphoreType.DMA((7 * self.n,)),
                pltpu.SemaphoreType.DMA((self.n,))]

    def _plan(self, x_refs, out_refs, send_sems, recv_sems, local_sems):
        mx, my, mc = _me()
        me = 4 * mx + 2 * my + mc
        here, sibling = (mx, my, mc), (mx, my, 1 - mc)
        chips = [(1 - mx, my), (mx, 1 - my), (1 - mx, 1 - my)]
        local, first, last = [], [], []
        relay = [[], [], []]
        for a, kind in enumerate(self.kinds):
            x, out = x_refs[a], out_refs[a]

            def rc(k, src, dst, to):
                return pltpu.make_async_remote_copy(
                    src_ref=src, dst_ref=dst, send_sem=send_sems.at[7 * a + k], recv_sem=recv_sems.at[7 * a + k],
                    device_id=to, device_id_type=MESH)

            if kind == "gather":
                local.append(pltpu.make_async_copy(x, out.at[me], local_sems.at[a]))
                first.append(rc(0, x, out.at[me], sibling))
                last.append(rc(0, x, out.at[me ^ 1], here))
                for j, (cx, cy) in enumerate(chips):
                    first.append(rc(1 + j, x, out.at[me], (cx, cy, mc)))
                    landed = out.at[4 * cx + 2 * cy + mc]
                    relay[j].append((rc(1 + j, x, landed, here), rc(4 + j, landed, landed, sibling)))
                    last.append(rc(4 + j, x, out.at[4 * cx + 2 * cy + 1 - mc], here))
            else:
                own = x.at[me] if kind == "scatter" else x
                local.append(pltpu.make_async_copy(own, out.at[me], local_sems.at[a]))
                for k in range(1, N_DEV):
                    src = x.at[me ^ k] if kind == "scatter" else x
                    first.append(rc(k - 1, src, out.at[me], _peer(mx, my, mc, k)))
                    last.append(rc(k - 1, own, out.at[me ^ k], here))
        return local, first, relay[0] + relay[1] + relay[2], last

    def start(self, *refs):
        local, first, _, _ = self._plan(*refs)
        for cp in local + first:
            cp.start()

    def relay(self, *refs):
        _, _, relay, _ = self._plan(*refs)
        for arrival, onward in relay:
            arrival.wait_recv()
            onward.start()

    def finish(self, *refs):
        local, first, relay, last = self._plan(*refs)
        for cp in last:
            cp.wait_recv()
        for cp in first:
            cp.wait_send()
        for _, onward in relay:
            onward.wait_send()
        for cp in local:
            cp.wait()


def _call(body, *, name, grid, in_specs, out_specs, out_shape, args, scratch=(), comm=None, aliases=None):
    params = _params(("arbitrary",) * len(grid))
    total = math.prod(grid)

    def at(step):
        flat = functools.reduce(lambda acc, dn: acc * dn[1] + pl.program_id(dn[0]), enumerate(grid), 0)
        return flat == step

    if comm is None:
        res = pl.pallas_call(
            body, name=name, grid=grid, in_specs=list(in_specs), out_specs=list(out_specs), out_shape=list(out_shape),
            scratch_shapes=list(scratch), input_output_aliases=aliases or {}, compiler_params=params)(*args)
        return list(res), []
    n_in, n_out, n_scr, cn = len(in_specs), len(out_specs), len(scratch), comm.n

    def hosted(*refs):
        ins, refs = refs[:n_in], refs[n_in:]
        cins, refs = refs[:cn], refs[cn:]
        outs, refs = refs[:n_out], refs[n_out:]
        couts, refs = refs[:cn], refs[cn:]
        scr, sems = refs[:n_scr], refs[n_scr:]

        @pl.when(at(0))
        def _():
            comm.start(cins, couts, *sems)

        body(*ins, *outs, *scr)

        @pl.when(at((3 * total) // 4 if total >= 4 else total - 1))
        def _():
            comm.relay(cins, couts, *sems)

        @pl.when(at(total - 1))
        def _():
            comm.finish(cins, couts, *sems)

    res = pl.pallas_call(
        hosted, name=name, grid=grid, in_specs=list(in_specs) + comm.specs(), out_specs=list(out_specs) + comm.specs(),
        out_shape=list(out_shape) + comm.out_shape(), scratch_shapes=list(scratch) + comm.scratch(),
        input_output_aliases=aliases or {}, compiler_params=params)(*args, *comm.args)
    return list(res[:n_out]), list(res[n_out:])


def _exchange_two_level(blk, small, name):
    _, R, C = blk.shape
    rows = small.shape[0]

    def body(blk_ref, small_ref, stage_ref, out_ref, gath_ref, a_scr, b_scr, t_scr, s1, r1, s3, r3, ss, rs, lsem):
        mx, my, mc = _me()
        me = 4 * mx + 2 * my + mc
        mine = 2 * mx + my
        here, sibling = (mx, my, mc), (mx, my, 1 - mc)

        def rc(src, dst, send, recv, to):
            return pltpu.make_async_remote_copy(src_ref=src, dst_ref=dst, send_sem=send, recv_sem=recv,
                                                device_id=to, device_id_type=MESH)

        own_small = pltpu.make_async_copy(small_ref, gath_ref.at[me], lsem.at[0])
        own_small.start()
        spread = [rc(small_ref, gath_ref.at[me], ss.at[k - 1], rs.at[k - 1], _peer(mx, my, mc, k)) for k in range(1, N_DEV)]
        order = (1, 2, 3, 0)
        to_sib = [rc(blk_ref.at[2 * (mine ^ k) + 1 - mc], stage_ref.at[k], s1.at[k], r1.at[k], sibling) for k in order]
        for cp in spread + to_sib:
            cp.start()
        own = {k: pltpu.make_async_copy(blk_ref.at[2 * (mine ^ k) + mc], a_scr.at[k], lsem.at[1 + k]) for k in order}
        for k in order:
            own[k].start()
        onward = []
        for k in order:
            rc(blk_ref.at[0], stage_ref.at[k], s1.at[k], r1.at[k], here).wait_recv()
            landed = pltpu.make_async_copy(stage_ref.at[k], b_scr.at[k], lsem.at[5 + k])
            landed.start()
            landed.wait()
            own[k].wait()
            t_scr[k] = (a_scr[k].astype(F32) + b_scr[k].astype(F32)).astype(BF16)
            if k > 0:
                cp = rc(t_scr.at[k], out_ref.at[mine], s3.at[k - 1], r3.at[k - 1], (mx ^ (k >> 1), my ^ (k & 1), mc))
                cp.start()
                onward.append(cp)
        keep = pltpu.make_async_copy(t_scr.at[0], out_ref.at[mine], lsem.at[9])
        keep.start()
        for k in range(1, 4):
            rc(t_scr.at[0], out_ref.at[mine ^ k], s3.at[k - 1], r3.at[k - 1], here).wait_recv()
        for k in range(1, N_DEV):
            rc(small_ref, gath_ref.at[me ^ k], ss.at[k - 1], rs.at[k - 1], here).wait_recv()
        for cp in spread + to_sib + onward:
            cp.wait_send()
        keep.wait()
        own_small.wait()

    any_spec = pl.BlockSpec(memory_space=pl.ANY)
    dma = pltpu.SemaphoreType.DMA
    _, out, gath = pl.pallas_call(
        body, name=name,
        in_specs=[any_spec, any_spec], out_specs=[any_spec] * 3,
        out_shape=[_sds((4, R, C), BF16), _sds((4, R, C), BF16), _sds((N_DEV, rows, D), F32)],
        scratch_shapes=[pltpu.VMEM((4, R, C), BF16)] * 3
                       + [dma((4,)), dma((4,)), dma((3,)), dma((3,)), dma((N_DEV - 1,)), dma((N_DEV - 1,)), dma((10,))],
        compiler_params=pltpu.CompilerParams(vmem_limit_bytes=VMEM_LIMIT),
    )(blk, small)
    return out, gath


def _exchange_rows(x_ref, out_ref, send_sems, recv_sems, between=None):
    mx, my, mc = _me()
    me = 4 * mx + 2 * my + mc
    out_ref[pl.ds(me, 1)] = x_ref[...][None]
    sends = []
    for k in range(1, N_DEV):
        cp = pltpu.make_async_remote_copy(
            src_ref=x_ref, dst_ref=out_ref.at[me], send_sem=send_sems.at[k - 1], recv_sem=recv_sems.at[k - 1],
            device_id=_peer(mx, my, mc, k), device_id_type=MESH)
        cp.start()
        sends.append(cp)
    if between is not None:
        between()
    for k in range(1, N_DEV):
        pltpu.make_async_remote_copy(
            src_ref=x_ref, dst_ref=out_ref.at[me ^ k], send_sem=send_sems.at[k - 1], recv_sem=recv_sems.at[k - 1],
            device_id=(mx, my, mc), device_id_type=MESH).wait_recv()
    for cp in sends:
        cp.wait_send()


def _prologue(c8, cctx8, w_ada, b_my, comm):
    nw = w_ada.shape[1]

    cn = comm.n

    def body(*refs):
        c_ref, cctx_ref, w_ref, b_ref = refs[:4]
        cins, refs = refs[4:4 + cn], refs[4 + cn:]
        act_ref, mod_ref = refs[:2]
        couts, refs = refs[2:2 + cn], refs[2 + cn:]
        cmine_scr, call_scr, mine_scr, mall_scr, s1, r1, s2, r2 = refs[:8]
        csems = refs[8:]
        cmine_scr[...] = c_ref[...]
        _exchange_rows(cmine_scr, call_scr, s1, r1)
        rows = [call_scr[d][0:1, :] for d in range(N_DEV)] + [cctx_ref[0:1, :], jnp.zeros((7, D), F32)]
        s = jnp.concatenate(rows, axis=0)
        act = s * _sigmoid(s)
        act_ref[...] = act
        mine_scr[...] = jnp.dot(act.astype(BF16), w_ref[...].astype(BF16), preferred_element_type=F32) + b_ref[...]
        _exchange_rows(mine_scr, mall_scr, s2, r2, between=lambda: comm.start(cins, couts, *csems))
        mod_ref[...] = mall_scr[...]
        comm.relay(cins, couts, *csems)
        comm.finish(cins, couts, *csems)

    sems = [pltpu.SemaphoreType.DMA((N_DEV - 1,))] * 4
    res = pl.pallas_call(
        body, name="prologue", grid=(1,),
        in_specs=[_full((8, D)), _full((8, D)), _full((D, nw)), _full((1, nw))] + comm.specs(),
        out_specs=[_full((16, D)), _full((N_DEV, 16, nw))] + comm.specs(),
        out_shape=[_sds((16, D), F32), _sds((N_DEV, 16, nw), F32)] + comm.out_shape(),
        scratch_shapes=[pltpu.VMEM((8, D), F32), pltpu.VMEM((N_DEV, 8, D), F32), pltpu.VMEM((16, nw), F32),
                        pltpu.VMEM((N_DEV, 16, nw), F32)] + sems + comm.scratch(),
        compiler_params=_params(("arbitrary",)),
    )(c8, cctx8, w_ada, b_my, *comm.args)
    return res[0], res[1], list(res[2:])


def _gather_rows(x, name):
    def body(x_ref, out_ref, send_sems, recv_sems):
        _exchange_rows(x_ref, out_ref, send_sems, recv_sems)

    return pl.pallas_call(
        body, name=name,
        out_shape=_sds((N_DEV,) + x.shape, x.dtype),
        in_specs=[pl.BlockSpec(memory_space=pltpu.VMEM)],
        out_specs=pl.BlockSpec(memory_space=pltpu.VMEM),
        scratch_shapes=[pltpu.SemaphoreType.DMA((N_DEV - 1,)), pltpu.SemaphoreType.DMA((N_DEV - 1,))],
        compiler_params=pltpu.CompilerParams(vmem_limit_bytes=VMEM_LIMIT),
    )(x)


def _ada_bwd(act, dmod_my, w_ada, m, v, tr=256):
    nw = w_ada.shape[1]

    def body(act_ref, dm_ref, w_ref, m_ref, v_ref, g_ref, d_ref, m2_ref, v2_ref, pc_ref):
        dm = dm_ref[...].astype(BF16)
        g = lax.dot_general(act_ref[...].astype(BF16), dm, TN, preferred_element_type=F32)
        w = w_ref[...]
        delta, m2, v2 = _adamw(w, g, m_ref[...], v_ref[...])
        g_ref[...] = g
        d_ref[...] = delta
        m2_ref[...] = m2
        v2_ref[...] = v2
        pc_ref[...] = lax.dot_general(dm[8:16, :], w.astype(BF16), NT, preferred_element_type=F32)

    wspec = _row(tr, nw)
    return pl.pallas_call(
        body, name="ada_bwd", grid=(D // tr,),
        in_specs=[pl.BlockSpec((16, tr), lambda i: (0, i)), _full((16, nw)), wspec, wspec, wspec],
        out_specs=[wspec, wspec, wspec, wspec, pl.BlockSpec((8, tr), lambda i: (0, i))],
        out_shape=[_sds((D, nw), F32)] * 4 + [_sds((8, D), F32)],
        compiler_params=_params(("arbitrary",)),
    )(act, dmod_my, w_ada, m, v)


def _k_in(x, modv, w_in, cos, sin, tm, comm=None):
    L = x.shape[0]

    def body(x_ref, mod_ref, w_ref, cos_ref, sin_ref, h_ref, q_ref, k_ref, v_ref, u_ref, vb_ref, ga_ref, gb_ref):
        n, _ = _ln(x_ref[...])
        h = (n * (1.0 + mod_ref[1:2, :]) + mod_ref[0:1, :]).astype(BF16)
        h_ref[...] = h
        c, s = cos_ref[...], sin_ref[...]

        def proj(lo, width):
            return lax.dot_general(h, w_ref[lo:lo + width, :], NT, preferred_element_type=F32)

        for i in range(2):
            qh = proj(O_Q + i * 256, 256)
            for j in range(2):
                q_ref[:, i * 256 + j * 128:i * 256 + (j + 1) * 128] = (
                    _rope(qh[:, j * 128:(j + 1) * 128], c, s) * Q_SCALE).astype(BF16)
        kv = proj(O_K, 2 * KV_W)
        k_ref[...] = _rope(kv[:, :KV_W], c, s).astype(BF16)
        v_ref[...] = kv[:, KV_W:].astype(BF16)
        u_ref[...] = proj(O_U, GM_W).astype(BF16)
        vb_ref[...] = proj(O_VB, GM_W).astype(BF16)
        ga_ref[...] = proj(O_GA, D).astype(BF16)
        gb_ref[...] = proj(O_GB, D).astype(BF16)

    widths = [D, Q_W, KV_W, KV_W, GM_W, GM_W, D, D]
    return _call(
        body, name="fwd_in", grid=(L // tm,),
        in_specs=[_row(tm, D), _full((8, D)), _resident((IN_W, D)), _row(tm, 128), _row(tm, 128)],
        out_specs=[_row(tm, w) for w in widths],
        out_shape=[_sds((L, w), BF16) for w in widths],
        args=(x, modv, w_in, cos, sin), comm=comm)


def _k_ctx(ctx, modc, w_kv):
    C = ctx.shape[0]

    def body(c_ref, mod_ref, w_ref, hc_ref, kc_ref, vc_ref):
        n, _ = _ln(c_ref[...])
        hc = (n * (1.0 + mod_ref[1:2, :]) + mod_ref[0:1, :]).astype(BF16)
        hc_ref[...] = hc
        kv = lax.dot_general(hc, w_ref[...], NT, preferred_element_type=F32)
        kc_ref[...] = kv[:, :KV_W].astype(BF16)
        vc_ref[...] = kv[:, KV_W:].astype(BF16)

    return pl.pallas_call(
        body, name="fwd_ctx", grid=(1,),
        in_specs=[_full((C, D)), _full((8, D)), _full((2 * KV_W, D))],
        out_specs=[_full((C, D)), _full((C, KV_W)), _full((C, KV_W))],
        out_shape=[_sds((C, D), BF16), _sds((C, KV_W), BF16), _sds((C, KV_W), BF16)],
        compiler_params=_params(("arbitrary",)),
    )(ctx, modc, w_kv)


def _attn_bias():
    r = (np.arange(GQA_GROUP * BLK) & (BLK - 1))[:, None]
    j = np.arange(3 * BLK)[None, :]
    band = np.abs(j - BLK - r) <= BLK
    variants = [band & (j >= BLK), band, band & (j < 2 * BLK)]
    return jnp.asarray(np.stack([np.where(v, 0.0, NEG_INF) for v in variants]), F32)


def _masked(s, bias, C):
    return jnp.concatenate([s[:, :C], s[:, C:] + bias], axis=1)


def _sink_col(sink_ref, hk):
    grp = lax.broadcasted_iota(jnp.int32, (GQA_GROUP * BLK, 1), 0) >> 7
    col = jnp.full((GQA_GROUP * BLK, 1), sink_ref[hk * GQA_GROUP], F32)
    for g in range(1, GQA_GROUP):
        col = jnp.where(grp == g, sink_ref[hk * GQA_GROUP + g], col)
    return col


ATTN_FWD_BLOCKS = 4


def _k_attn(sink, q, k, v, kc, vc, bias, comm=None):
    L = q.shape[0]
    C = kc.shape[0]
    nb = L // BLK
    nq = min(ATTN_FWD_BLOCKS, nb)
    steps = nb // nq

    def body(sink_ref, q_ref, kp_ref, km_ref, kx_ref, vp_ref, vm_ref, vx_ref, kc_ref, vc_ref, bias_ref, ya_ref, lse_ref):
        i = pl.program_id(0)
        chains = [(qb, hk) for qb in range(nq) for hk in range(N_KV_HEADS)]

        def band(qb):
            first = jnp.where(i == 0, 0, 1) if qb == 0 else 1
            return bias_ref[jnp.where(i == steps - 1, 2, first) if qb == nq - 1 else first]

        def keys(ctx_ref, p_ref, m_ref, x_ref, qb, hk):
            sl = slice(hk * HEAD_DIM, (hk + 1) * HEAD_DIM)
            blocks = [p_ref[:, sl]] + [m_ref[j * BLK:(j + 1) * BLK, sl] for j in range(nq)] + [x_ref[:, sl]]
            return jnp.concatenate([ctx_ref[:, sl]] + blocks[qb:qb + 3], axis=0)

        def queries(qb, hk):
            return jnp.concatenate(
                [q_ref[qb * BLK:(qb + 1) * BLK, (hk * GQA_GROUP + g) * HEAD_DIM:(hk * GQA_GROUP + g + 1) * HEAD_DIM]
                 for g in range(GQA_GROUP)], axis=0)

        def scores(qb, hk):
            return _masked(lax.dot_general(queries(qb, hk), keys(kc_ref, kp_ref, km_ref, kx_ref, qb, hk), NT,
                                           preferred_element_type=F32), band(qb), C)

        ahead = 2
        s = [scores(*c) for c in chains[:ahead]]
        for n, (qb, hk) in enumerate(chains):
            if n + ahead < len(chains):
                s.append(scores(*chains[n + ahead]))
            s_ = s[n]
            sink_c = _sink_col(sink_ref, hk)
            m = jnp.maximum(jnp.max(s_, axis=1, keepdims=True), sink_c)
            p = jnp.exp(s_ - m)
            den = jnp.sum(p, axis=1, keepdims=True) + jnp.exp(sink_c - m)
            o = jnp.dot(p.astype(BF16), keys(vc_ref, vp_ref, vm_ref, vx_ref, qb, hk), preferred_element_type=F32) * (1.0 / den)
            lse = m + jnp.log(den)
            rows = slice(qb * BLK, (qb + 1) * BLK)
            for g in range(GQA_GROUP):
                h = hk * GQA_GROUP + g
                ya_ref[rows, h * HEAD_DIM:(h + 1) * HEAD_DIM] = o[g * BLK:(g + 1) * BLK, :].astype(BF16)
                lse_ref[rows, h:h + 1] = lse[g * BLK:(g + 1) * BLK, :]

    kv3 = [pl.BlockSpec((BLK, KV_W), lambda i: (jnp.maximum(nq * i - 1, 0), 0)),
           pl.BlockSpec((nq * BLK, KV_W), lambda i: (i, 0)),
           pl.BlockSpec((BLK, KV_W), lambda i: (jnp.minimum(nq * i + nq, nb - 1), 0))]
    return _call(
        body, name="fwd_attn", grid=(steps,),
        in_specs=[pl.BlockSpec(memory_space=pltpu.SMEM), _row(nq * BLK, Q_W)] + kv3 + kv3
                 + [_full((C, KV_W)), _full((C, KV_W)), _full((3, GQA_GROUP * BLK, 3 * BLK))],
        out_specs=[_row(nq * BLK, Q_W), _row(nq * BLK, N_Q_HEADS)],
        out_shape=[_sds((L, Q_W), BF16), _sds((L, N_Q_HEADS), F32)],
        args=(sink, q, k, k, k, v, v, v, kc, vc, bias), comm=comm)


GMLP_CHUNKS = 4


def _split_pair(t):
    low = lax.broadcasted_iota(jnp.int32, t.shape, 1) < GROUP_DIM
    zero = jnp.zeros_like(t)
    return jnp.where(low, t, zero), jnp.where(low, zero, t)


def _gmlp_spatial(w_ref, t_b, nch):
    rows = []
    for c in range(nch):
        tiles = []
        for pr in range(N_GROUPS // 2):
            lo, hi = _split_pair(t_b[c * BLK:(c + 1) * BLK, pr * 128:(pr + 1) * 128])
            tiles.append(jnp.dot(w_ref[2 * pr], lo, preferred_element_type=F32)
                         + jnp.dot(w_ref[2 * pr + 1], hi, preferred_element_type=F32))
        rows.append(jnp.concatenate(tiles, axis=1))
    return jnp.concatenate(rows, axis=0)


def _gmlp_fwd_vals(u, vb, lnv_ref, ws_ref, bsp_ref, nch):
    uf = u.astype(F32)
    vf = vb.astype(F32)
    gu, tu = _gelu(uf)
    gv, tv = _gelu(vf)
    vhat, rstd = _ln(gv)
    vn = (vhat * lnv_ref[0:1, :] + lnv_ref[1:2, :]).astype(BF16)
    s = _gmlp_spatial(ws_ref, vn, nch) + jnp.concatenate([bsp_ref[...]] * nch, axis=0)
    return uf, vf, gu, tu, tv, vhat, rstd, vn, s


def _k_gmlp(u, vb, lnv, ws, bsp):
    L = u.shape[0]
    nch = min(GMLP_CHUNKS, L // BLK)
    tm = nch * BLK

    def body(u_ref, vb_ref, lnv_ref, ws_ref, bsp_ref, yb_ref):
        _, _, gu, _, _, _, _, _, s = _gmlp_fwd_vals(u_ref[...], vb_ref[...], lnv_ref, ws_ref, bsp_ref, nch)
        yb_ref[...] = (gu * s).astype(BF16)

    return pl.pallas_call(
        body, name="fwd_gmlp", grid=(L // tm,),
        in_specs=[_row(tm, GM_W), _row(tm, GM_W), _full((8, GM_W)), _full((N_GROUPS, BLK, BLK)), _full((BLK, GM_W))],
        out_specs=_row(tm, GM_W),
        out_shape=_sds((L, GM_W), BF16),
        compiler_params=_params(("arbitrary",)),
    )(u, vb, lnv, ws, bsp)


def _k_merge(x, ya, yb, ga, gb, w_a, w_b, w_o, modv, lnv, tm):
    L = x.shape[0]

    def body(x_ref, ya_ref, yb_ref, ga_ref, gb_ref, wa_ref, wb_ref, wo_ref, mod_ref, ln_ref,
             mg_ref, mix_ref, xm_ref, h2_ref):
        a = jnp.dot(ya_ref[...], wa_ref[...], preferred_element_type=F32)
        b = jnp.dot(yb_ref[...], wb_ref[...], preferred_element_type=F32)
        merged = (_sigmoid(ga_ref[...].astype(F32)) * a + _sigmoid(gb_ref[...].astype(F32)) * b).astype(BF16)
        mg_ref[...] = merged
        mix = jnp.dot(merged, wo_ref[...], preferred_element_type=F32)
        mix_ref[...] = mix.astype(BF16)
        r1 = ALPHA * x_ref[...] + mod_ref[2:3, :] * mix
        r1hat, _ = _ln(r1)
        xm = r1hat * ln_ref[0:1, :] + ln_ref[1:2, :]
        xm_ref[...] = xm
        n2, _ = _ln(xm)
        h2_ref[...] = (n2 * (1.0 + mod_ref[4:5, :]) + mod_ref[3:4, :]).astype(BF16)

    return pl.pallas_call(
        body, name="fwd_merge", grid=(L // tm,),
        in_specs=[_row(tm, D), _row(tm, Q_W), _row(tm, GM_W), _row(tm, D), _row(tm, D),
                  _resident((Q_W, D)), _resident((GM_W, D)), _resident((D, D)), _full((8, D)), _full((8, D))],
        out_specs=[_row(tm, D)] * 4,
        out_shape=[_sds((L, D), BF16), _sds((L, D), BF16), _sds((L, D), F32), _sds((L, D), BF16)],
        compiler_params=_params(("arbitrary",)),
    )(x, ya, yb, ga, gb, w_a, w_b, w_o, modv, lnv)


FFN_CH = FFN_H // 2
FFN_CH_FWD = 256


def _k_ffn(h2, xm, tgt, w_fi, w_fo, modv, lnv, tm):
    L = h2.shape[0]

    def body(h2_ref, xm_ref, t_ref, wi_ref, wo_ref, mod_ref, ln_ref, gate_ref, up_ref, a_ref, dr2_ref, df_ref, acc_ref):
        @pl.when(pl.program_id(0) == 0)
        def _():
            acc_ref[...] = jnp.zeros_like(acc_ref)

        h2v = h2_ref[...]
        ch = FFN_CH_FWD
        chunks = [j * ch for j in range(FFN_H // ch)]

        def project(lo):
            return (lax.dot_general(h2v, wi_ref[lo:lo + ch, :], NT, preferred_element_type=F32),
                    lax.dot_general(h2v, wi_ref[FFN_H + lo:FFN_H + lo + ch, :], NT, preferred_element_type=F32))

        f = jnp.zeros((tm, D), F32)
        ahead = [project(chunks[0])]
        for j, lo in enumerate(chunks):
            if j + 1 < len(chunks):
                ahead.append(project(chunks[j + 1]))
            gate, up = ahead[j]
            act = (gate * _sigmoid(gate) * up).astype(BF16)
            gate_ref[:, lo:lo + ch] = gate.astype(BF16)
            up_ref[:, lo:lo + ch] = up.astype(BF16)
            a_ref[:, lo:lo + ch] = act
            f = f + jnp.dot(act, wo_ref[lo:lo + ch, :], preferred_element_type=F32)
        gate2 = mod_ref[5:6, :]
        r2 = ALPHA * xm_ref[...] + gate2 * f
        r2hat, rstd = _ln(r2)
        y = r2hat * ln_ref[2:3, :] + ln_ref[3:4, :]
        err = y - t_ref[...]
        dy = err * (1.0 / D)
        dr2 = _ln_bwd(dy * ln_ref[2:3, :], r2hat, rstd)
        dr2_ref[...] = dr2
        df_ref[...] = (gate2 * dr2).astype(BF16)
        acc_ref[0:1, :] += _colsum(dy * r2hat)
        acc_ref[1:2, :] += _colsum(dy)
        acc_ref[2:3, :] += _colsum(dr2 * f)
        acc_ref[3:4, :] += _colsum(err * err) * (0.5 / D)

    return pl.pallas_call(
        body, name="fwd_ffn", grid=(L // tm,),
        in_specs=[_row(tm, D), _row(tm, D), _row(tm, D), _resident((2 * FFN_H, D)), _resident((FFN_H, D)),
                  _full((8, D)), _full((8, D))],
        out_specs=[_row(tm, FFN_H)] * 3 + [_row(tm, D), _row(tm, D), _full((8, D))],
        out_shape=[_sds((L, FFN_H), BF16)] * 3 + [_sds((L, D), F32), _sds((L, D), BF16), _sds((8, D), F32)],
        compiler_params=_params(("arbitrary",)),
    )(h2, xm, tgt, w_fi, w_fo, modv, lnv)


FFN_CH_BWD = 256


def _k_ffn_bwd(df, gate, up, xm, dr2, x, mix, w_fi, w_fo, modv, lnv, tm):
    L = df.shape[0]

    def body(df_ref, gate_ref, up_ref, xm_ref, dr2_ref, x_ref, mix_ref, wi_ref, wo_ref, mod_ref, ln_ref,
             dF_ref, dmix_ref, dxp_ref, acc_ref):
        @pl.when(pl.program_id(0) == 0)
        def _():
            acc_ref[...] = jnp.zeros_like(acc_ref)

        dfv = df_ref[...]
        ch = FFN_CH_BWD
        chunks = [j * ch for j in range(FFN_H // ch)]

        def d_act(lo):
            return lax.dot_general(dfv, wo_ref[lo:lo + ch, :], NT, preferred_element_type=F32)

        n2, rstd2 = _ln(xm_ref[...])
        mixf = mix_ref[...].astype(F32)
        gate1 = mod_ref[2:3, :]
        r1hat, rstd1 = _ln(ALPHA * x_ref[...] + gate1 * mixf)
        dh2 = jnp.zeros((tm, D), F32)
        das = [d_act(chunks[0])]
        for j, lo in enumerate(chunks):
            if j + 1 < len(chunks):
                das.append(d_act(chunks[j + 1]))
            da = das[j]
            gate = gate_ref[:, lo:lo + ch].astype(F32)
            upv = up_ref[:, lo:lo + ch].astype(F32)
            sg = _sigmoid(gate)
            d_gate = (da * upv * (sg * (1.0 + gate * (1.0 - sg)))).astype(BF16)
            d_up = (da * (gate * sg)).astype(BF16)
            dF_ref[:, lo:lo + ch] = d_gate
            dF_ref[:, FFN_H + lo:FFN_H + lo + ch] = d_up
            dh2 = dh2 + jnp.dot(d_gate, wi_ref[lo:lo + ch, :], preferred_element_type=F32)
            dh2 = dh2 + jnp.dot(d_up, wi_ref[FFN_H + lo:FFN_H + lo + ch, :], preferred_element_type=F32)
        acc_ref[0:1, :] += _colsum(dh2)
        acc_ref[1:2, :] += _colsum(dh2 * n2)
        dxm = ALPHA * dr2_ref[...] + _ln_bwd(dh2 * (1.0 + mod_ref[4:5, :]), n2, rstd2)
        acc_ref[2:3, :] += _colsum(dxm * r1hat)
        acc_ref[3:4, :] += _colsum(dxm)
        dr1 = _ln_bwd(dxm * ln_ref[0:1, :], r1hat, rstd1)
        dmix_ref[...] = (gate1 * dr1).astype(BF16)
        dxp_ref[...] = ALPHA * dr1
        acc_ref[4:5, :] += _colsum(dr1 * mixf)

    return pl.pallas_call(
        body, name="bwd_ffn", grid=(L // tm,),
        in_specs=[_row(tm, D), _row(tm, FFN_H), _row(tm, FFN_H), _row(tm, D), _row(tm, D), _row(tm, D), _row(tm, D),
                  _resident((2 * FFN_H, D)), _resident((FFN_H, D)), _full((8, D)), _full((8, D))],
        out_specs=[_row(tm, 2 * FFN_H), _row(tm, D), _row(tm, D), _full((8, D))],
        out_shape=[_sds((L, 2 * FFN_H), BF16), _sds((L, D), BF16), _sds((L, D), F32), _sds((8, D), F32)],
        compiler_params=_params(("arbitrary",)),
    )(df, gate, up, xm, dr2, x, mix, w_fi, w_fo, modv, lnv)


def _k_merge_bwd(dmix, merged, ya, yb, ga, gb, w_a, w_b, w_o, tm):
    L = dmix.shape[0]
    n = L // tm

    def body(dmix_ref, mg_ref, ya_ref, yb_ref, ga_ref, gb_ref, wa_ref, wb_ref, wo_ref,
             dga_ref, dgb_ref, dya_ref, dyb_ref, gwa_ref, gwb_ref, gwo_ref, acc_a, acc_b, acc_o):
        i = pl.program_id(0)

        @pl.when(i == 0)
        def _():
            for r in (acc_a, acc_b, acc_o):
                r[...] = jnp.zeros_like(r)

        dmixv = dmix_ref[...]
        dmg = lax.dot_general(dmixv, wo_ref[...], NT, preferred_element_type=F32)
        acc_o[...] += lax.dot_general(mg_ref[...], dmixv, TN, preferred_element_type=F32)
        ya = ya_ref[...]
        a = jnp.dot(ya, wa_ref[...], preferred_element_type=F32)
        sa = _sigmoid(ga_ref[...].astype(F32))
        dA = (dmg * sa).astype(BF16)
        dga_ref[...] = (dmg * a * (sa * (1.0 - sa))).astype(BF16)
        dya_ref[...] = lax.dot_general(dA, wa_ref[...], NT, preferred_element_type=F32).astype(BF16)
        acc_a[...] += lax.dot_general(ya, dA, TN, preferred_element_type=F32)
        yb = yb_ref[...]
        b = jnp.dot(yb, wb_ref[...], preferred_element_type=F32)
        sb = _sigmoid(gb_ref[...].astype(F32))
        dB = (dmg * sb).astype(BF16)
        dgb_ref[...] = (dmg * b * (sb * (1.0 - sb))).astype(BF16)
        dyb_ref[...] = lax.dot_general(dB, wb_ref[...], NT, preferred_element_type=F32).astype(BF16)
        acc_b[...] += lax.dot_general(yb, dB, TN, preferred_element_type=F32)

        @pl.when(i == n - 1)
        def _():
            gwa_ref[...] = acc_a[...].astype(BF16)
            gwb_ref[...] = acc_b[...].astype(BF16)
            gwo_ref[...] = acc_o[...].astype(BF16)

    return pl.pallas_call(
        body, name="bwd_merge", grid=(n,),
        in_specs=[_row(tm, D), _row(tm, D), _row(tm, Q_W), _row(tm, GM_W), _row(tm, D), _row(tm, D),
                  _resident((Q_W, D)), _resident((GM_W, D)), _resident((D, D))],
        out_specs=[_row(tm, D), _row(tm, D), _row(tm, Q_W), _row(tm, GM_W), _full((Q_W, D)), _full((GM_W, D)), _full((D, D))],
        out_shape=[_sds((L, D), BF16), _sds((L, D), BF16), _sds((L, Q_W), BF16), _sds((L, GM_W), BF16),
                   _sds((Q_W, D), BF16), _sds((GM_W, D), BF16), _sds((D, D), BF16)],
        scratch_shapes=[pltpu.VMEM((Q_W, D), F32), pltpu.VMEM((GM_W, D), F32), pltpu.VMEM((D, D), F32)],
        compiler_params=_params(("arbitrary",)),
    )(dmix, merged, ya, yb, ga, gb, w_a, w_b, w_o)


def _k_gmlp_bwd(u, vb, dyb, lnv, ws, wst, bsp):
    L = u.shape[0]
    nch = min(GMLP_CHUNKS, L // BLK)
    tm = nch * BLK

    def body(u_ref, vb_ref, dyb_ref, lnv_ref, ws_ref, wst_ref, bsp_ref, du_ref, dvb_ref, gws_ref, gbst_ref, gln_ref):
        @pl.when(pl.program_id(0) == 0)
        def _():
            gws_ref[...] = jnp.zeros_like(gws_ref)
            gbst_ref[...] = jnp.zeros_like(gbst_ref)
            gln_ref[...] = jnp.zeros_like(gln_ref)

        uf, vf, gu, tu, tv, vhat, rstd, vn, s = _gmlp_fwd_vals(u_ref[...], vb_ref[...], lnv_ref, ws_ref, bsp_ref, nch)
        dyb_f = dyb_ref[...].astype(F32)
        du_ref[...] = (dyb_f * s * _gelu_grad(uf, tu)).astype(BF16)
        ds = dyb_f * gu
        ds_b = ds.astype(BF16)
        for pr in range(N_GROUPS // 2):
            lanes = slice(pr * 128, (pr + 1) * 128)
            gw_lo = gw_hi = ds_sum = None
            for c in range(nch):
                rows = slice(c * BLK, (c + 1) * BLK)
                lo, hi = _split_pair(ds_b[rows, lanes])
                t_lo = lax.dot_general(lo, vn[rows, lanes], NT, preferred_element_type=F32)
                t_hi = lax.dot_general(hi, vn[rows, lanes], NT, preferred_element_type=F32)
                gw_lo = t_lo if c == 0 else gw_lo + t_lo
                gw_hi = t_hi if c == 0 else gw_hi + t_hi
                ds_sum = ds[rows, lanes] if c == 0 else ds_sum + ds[rows, lanes]
            gws_ref[2 * pr] += gw_lo
            gws_ref[2 * pr + 1] += gw_hi
            b_lo, b_hi = _split_pair(ds_sum)
            gbst_ref[:, 2 * pr:2 * pr + 1] += jnp.sum(b_lo, axis=1, keepdims=True)
            gbst_ref[:, 2 * pr + 1:2 * pr + 2] += jnp.sum(b_hi, axis=1, keepdims=True)
        dvn = _gmlp_spatial(wst_ref, ds_b, nch)
        gln_ref[0:1, :] += _colsum(dvn * vhat)
        gln_ref[1:2, :] += _colsum(dvn)
        dgv = _ln_bwd(dvn * lnv_ref[0:1, :], vhat, rstd)
        dvb_ref[...] = (dgv * _gelu_grad(vf, tv)).astype(BF16)

    return pl.pallas_call(
        body, name="bwd_gmlp", grid=(L // tm,),
        in_specs=[_row(tm, GM_W)] * 3 + [_full((8, GM_W)), _full((N_GROUPS, BLK, BLK)), _full((N_GROUPS, BLK, BLK)),
                                         _full((BLK, GM_W))],
        out_specs=[_row(tm, GM_W), _row(tm, GM_W), _full((N_GROUPS, BLK, BLK)), _full((BLK, N_GROUPS)), _full((8, GM_W))],
        out_shape=[_sds((L, GM_W), BF16), _sds((L, GM_W), BF16), _sds((N_GROUPS, BLK, BLK), F32),
                   _sds((BLK, N_GROUPS), F32), _sds((8, GM_W), F32)],
        compiler_params=_params(("arbitrary",)),
    )(u, vb, dyb, lnv, ws, wst, bsp)


ATTN_BWD_BLOCKS = 2


def _k_attn_bwd(sink, q, k, v, kc, vc, dya, lse, cos, sin, bias, comm=None):
    L = q.shape[0]
    C = kc.shape[0]
    nb = L // BLK
    nq = min(ATTN_BWD_BLOCKS, nb)
    steps = nb // nq
    NK = C + 3 * BLK
    chains = [(qb, hk) for qb in range(nq) for hk in range(N_KV_HEADS)]

    def body(sink_ref, q_ref, kp_ref, km_ref, kx_ref, vp_ref, vm_ref, vx_ref, kc_ref, vc_ref, do_ref, lse_ref,
             cq_ref, sq_ref, cl_ref, sl_ref, bias_ref,
             dq_ref, dk_ref, dv_ref, dkc_ref, dvc_ref, dsink_ref,
             dq_scr, ck_scr, cv_scr, k1_acc, k2_acc, v1_acc, v2_acc):
        i = pl.program_id(0)

        @pl.when(i == 0)
        def _():
            for r in (k1_acc, k2_acc, v1_acc, v2_acc, dkc_ref, dvc_ref, dsink_ref):
                r[...] = jnp.zeros_like(r)

        @pl.when(i < steps)
        def _():
            def band(qb):
                first = jnp.where(i == 0, 0, 1) if qb == 0 else 1
                return bias_ref[jnp.where(i == steps - 1, 2, first) if qb == nq - 1 else first]

            def lanes(hk):
                return slice(hk * HEAD_DIM, (hk + 1) * HEAD_DIM)

            def keys(ctx_ref, p_ref, m_ref, x_ref, qb, hk):
                sl = lanes(hk)
                blocks = [p_ref[:, sl]] + [m_ref[j * BLK:(j + 1) * BLK, sl] for j in range(nq)] + [x_ref[:, sl]]
                return jnp.concatenate([ctx_ref[:, sl]] + blocks[qb:qb + 3], axis=0)

            def stacked(ref, qb, hk, width):
                return jnp.concatenate(
                    [ref[qb * BLK:(qb + 1) * BLK, (hk * GQA_GROUP + g) * width:(hk * GQA_GROUP + g + 1) * width]
                     for g in range(GQA_GROUP)], axis=0)

            def scores(qb, hk):
                kcat = keys(kc_ref, kp_ref, km_ref, kx_ref, qb, hk)
                qg = stacked(q_ref, qb, hk, HEAD_DIM)
                s = _masked(lax.dot_general(qg, kcat, NT, preferred_element_type=F32), band(qb), C)
                dog = stacked(do_ref, qb, hk, HEAD_DIM)
                dp = lax.dot_general(dog, keys(vc_ref, vp_ref, vm_ref, vx_ref, qb, hk), NT, preferred_element_type=F32)
                return kcat, qg, dog, s, dp

            def softmax_bwd(qb, hk, s, dp):
                lse_c = stacked(lse_ref, qb, hk, 1)
                p = jnp.exp(s - lse_c)
                delta = jnp.sum(p * dp, axis=1, keepdims=True)
                ds = (p * (dp - delta)).astype(BF16)
                p_sink = jnp.exp(_sink_col(sink_ref, hk) - lse_c) * delta
                return p.astype(BF16), ds, p_sink

            def put_dq(qb, hk, dqs, p_sink):
                for g in range(GQA_GROUP):
                    h = hk * GQA_GROUP + g
                    dq_scr[qb * BLK:(qb + 1) * BLK, h * HEAD_DIM:(h + 1) * HEAD_DIM] = dqs[g * BLK:(g + 1) * BLK, :]
                    tot = jnp.sum(p_sink[g * BLK:(g + 1) * BLK, :], axis=0, keepdims=True)
                    dsink_ref[h:h + 1, :] -= jnp.broadcast_to(tot, (1, 128))

            ahead = 4
            sc = [scores(*c) for c in chains[:ahead]]
            pending = None
            for n, (qb, hk) in enumerate(chains):
                if n + ahead < len(chains):
                    sc.append(scores(*chains[n + ahead]))
                kcat, qg, dog, s, dp = sc[n]
                pb, ds, p_sink = softmax_bwd(qb, hk, s, dp)
                if pending is not None:
                    pqb, phk, pds, ppb, pqg, pdog = pending
                    ck_scr[pqb, :, lanes(phk)] = lax.dot_general(pds, pqg, TN, preferred_element_type=F32)
                    cv_scr[pqb, :, lanes(phk)] = lax.dot_general(ppb, pdog, TN, preferred_element_type=F32)
                put_dq(qb, hk, jnp.dot(ds, kcat, preferred_element_type=F32), p_sink)
                pending = (qb, hk, ds, pb, qg, dog)
            pqb, phk, pds, ppb, pqg, pdog = pending
            ck_scr[pqb, :, lanes(phk)] = lax.dot_general(pds, pqg, TN, preferred_element_type=F32)
            cq, sq = cq_ref[...], sq_ref[...]
            for j in range(4):
                dq_ref[:, j * 128:(j + 1) * 128] = _unrope(dq_scr[:, j * 128:(j + 1) * 128] * Q_SCALE, cq, sq).astype(BF16)
            cv_scr[pqb, :, lanes(phk)] = lax.dot_general(ppb, pdog, TN, preferred_element_type=F32)
            dkc_ref[...] += functools.reduce(lambda a, b: a + b, [ck_scr[qb, 0:C, :] for qb in range(nq)])
            dvc_ref[...] += functools.reduce(lambda a, b: a + b, [cv_scr[qb, 0:C, :] for qb in range(nq)])

        @pl.when(i >= steps)
        def _():
            ck_scr[...] = jnp.zeros_like(ck_scr)
            cv_scr[...] = jnp.zeros_like(cv_scr)

        def slot(scr, r, carried):
            parts = [scr[qb, C + (r - qb) * BLK:C + (r - qb + 1) * BLK, :] for qb in range(nq) if 0 <= r - qb <= 2]
            total = functools.reduce(lambda a, b: a + b, parts)
            return total if carried is None else carried[...] + total

        for r in range(nq):
            rows = slice(r * BLK, (r + 1) * BLK)
            carried_k, carried_v = ((k1_acc, v1_acc), (k2_acc, v2_acc), (None, None))[min(r, 2)]
            tables = (cl_ref[...], sl_ref[...]) if r == 0 else (cq_ref[(r - 1) * BLK:r * BLK, :], sq_ref[(r - 1) * BLK:r * BLK, :])
            dk_ref[rows, :] = _unrope(slot(ck_scr, r, carried_k), *tables).astype(BF16)
            dv_ref[rows, :] = slot(cv_scr, r, carried_v).astype(BF16)
        k1_acc[...] = slot(ck_scr, nq, None)
        v1_acc[...] = slot(cv_scr, nq, None)
        k2_acc[...] = slot(ck_scr, nq + 1, None)
        v2_acc[...] = slot(cv_scr, nq + 1, None)

    last = steps - 1
    kv3 = [pl.BlockSpec((BLK, KV_W), lambda i: (jnp.clip(nq * i - 1, 0, nb - 1), 0)),
           pl.BlockSpec((nq * BLK, KV_W), lambda i: (jnp.minimum(i, last), 0)),
           pl.BlockSpec((BLK, KV_W), lambda i: (jnp.minimum(nq * i + nq, nb - 1), 0))]
    cur = lambda w: pl.BlockSpec((nq * BLK, w), lambda i: (jnp.minimum(i, last), 0))
    late = lambda w: pl.BlockSpec((BLK, w), lambda i: (jnp.clip(nq * i - 1, 0, nb - 1), 0))
    out2 = lambda w: pl.BlockSpec((nq * BLK, w), lambda i: (i, 0))
    return _call(
        body, name="bwd_attn", grid=(steps + 1,),
        in_specs=[pl.BlockSpec(memory_space=pltpu.SMEM), cur(Q_W)] + kv3 + kv3
                 + [_full((C, KV_W)), _full((C, KV_W)), cur(Q_W), cur(N_Q_HEADS), cur(128), cur(128), late(128), late(128),
                    _full((3, GQA_GROUP * BLK, 3 * BLK))],
        out_specs=[cur(Q_W), out2(KV_W), out2(KV_W), _full((C, KV_W)), _full((C, KV_W)), _full((8, 128))],
        out_shape=[_sds((L, Q_W), BF16), _sds((L + nq * BLK, KV_W), BF16), _sds((L + nq * BLK, KV_W), BF16),
                   _sds((C, KV_W), F32), _sds((C, KV_W), F32), _sds((8, 128), F32)],
        scratch=[pltpu.VMEM((nq * BLK, Q_W), F32), pltpu.VMEM((nq, NK, KV_W), F32), pltpu.VMEM((nq, NK, KV_W), F32)]
                + [pltpu.VMEM((BLK, KV_W), F32)] * 4,
        args=(sink, q, k, k, k, v, v, v, kc, vc, dya, lse, cos, sin, cos, sin, bias), comm=comm)


def _k_ctx_bwd(ctx, modc, hc, dkc, dvc, w_kv):
    C = ctx.shape[0]

    def body(c_ref, mod_ref, hc_ref, dkc_ref, dvc_ref, w_ref, gw_ref, dmod_ref):
        dkv = jnp.concatenate([dkc_ref[...], dvc_ref[...]], axis=1).astype(BF16)
        gw_ref[...] = lax.dot_general(dkv, hc_ref[...], TN, preferred_element_type=F32)
        dhc = jnp.dot(dkv, w_ref[...], preferred_element_type=F32)
        n, _ = _ln(c_ref[...])
        dmod_ref[...] = jnp.zeros_like(dmod_ref)
        dmod_ref[0:1, :] = _colsum(dhc)
        dmod_ref[1:2, :] = _colsum(dhc * n)

    return pl.pallas_call(
        body, name="bwd_ctx", grid=(1,),
        in_specs=[_full((C, D)), _full((8, D)), _full((C, D)), _full((C, KV_W)), _full((C, KV_W)), _full((2 * KV_W, D))],
        out_specs=[_full((2 * KV_W, D)), _full((8, D))],
        out_shape=[_sds((2 * KV_W, D), F32), _sds((8, D), F32)],
        compiler_params=_params(("arbitrary",)),
    )(ctx, modc, hc, dkc, dvc, w_kv)


def _k_in_bwd(dq, dk, dv, du, dvb, dga, dgb, x, dxp, w_in, modv, tm, comm=None):
    L = x.shape[0]
    parts = [(O_Q, Q_W), (O_K, KV_W), (O_V, KV_W), (O_U, GM_W), (O_VB, GM_W), (O_GA, D), (O_GB, D)]

    def body(dq_ref, dk_ref, dv_ref, du_ref, dvb_ref, dga_ref, dgb_ref, x_ref, dxp_ref, w_ref, mod_ref,
             dP_ref, gx_ref, acc_ref):
        @pl.when(pl.program_id(0) == 0)
        def _():
            acc_ref[...] = jnp.zeros_like(acc_ref)

        for (lo, width), r in zip(parts, (dq_ref, dk_ref, dv_ref, du_ref, dvb_ref, dga_ref, dgb_ref)):
            dP_ref[:, lo:lo + width] = r[...]
        n1, rstd1 = _ln(x_ref[...])
        dh = jnp.dot(dP_ref[...], w_ref[...], preferred_element_type=F32)
        acc_ref[0:1, :] += _colsum(dh)
        acc_ref[1:2, :] += _colsum(dh * n1)
        gx_ref[...] = dxp_ref[...] + _ln_bwd(dh * (1.0 + mod_ref[1:2, :]), n1, rstd1)

    return _call(
        body, name="bwd_in", grid=(L // tm,),
        in_specs=[_row(tm, w) for _, w in parts] + [_row(tm, D), _row(tm, D), _resident((IN_W, D)), _full((8, D))],
        out_specs=[_row(tm, IN_W), _row(tm, D), _full((8, D))],
        out_shape=[_sds((L, IN_W), BF16), _sds((L, D), F32), _sds((8, D), F32)],
        args=(dq, dk, dv, du, dvb, dga, dgb, x, dxp, w_in, modv), comm=comm)


def _wgrad(a, b, name, tk, tt, comm=None, extra=None):
    T, K = a.shape
    N = b.shape[1]
    nt = T // tt

    def body(*refs):
        a_ref, b_ref = refs[:2]
        o_ref, acc_ref = refs[-2:]
        j, t = pl.program_id(0), pl.program_id(1)

        @pl.when(t == 0)
        def _():
            acc_ref[...] = jnp.zeros_like(acc_ref)

        acc_ref[...] += lax.dot_general(a_ref[...], b_ref[...], TN, preferred_element_type=F32)

        if extra is not None:
            lo, rows = extra[0] % tk, extra[1].shape[0]

            @pl.when((t == nt - 1) & (j == extra[0] // tk))
            def _():
                acc_ref[lo:lo + rows, :] += refs[2][...]

        @pl.when(t == nt - 1)
        def _():
            o_ref[...] = acc_ref[...].astype(BF16)

    extra_specs = [] if extra is None else [pl.BlockSpec(extra[1].shape, lambda j, t: (0, 0))]
    (out,), got = _call(
        body, name=name, grid=(K // tk, nt),
        in_specs=[pl.BlockSpec((tt, tk), lambda j, t: (t, j)), pl.BlockSpec((tt, N), lambda j, t: (t, 0))] + extra_specs,
        out_specs=[pl.BlockSpec((tk, N), lambda j, t: (j, 0))],
        out_shape=[_sds((K, N), BF16)],
        scratch=[pltpu.VMEM((tk, N), F32)],
        args=(a, b) + (() if extra is None else (extra[1],)), comm=comm)
    return (out, got) if comm is not None else out


def _adamw_reduce(parts, w, m, v, name, tr):
    R, C = w.shape
    n_parts = parts.shape[0]

    def body(p_ref, w_ref, m_ref, v_ref, g_ref, d_ref, m2_ref, v2_ref):
        g = p_ref[0].astype(F32)
        for i in range(1, n_parts):
            g = g + p_ref[i].astype(F32)
        delta, m2, v2 = _adamw(w_ref[...], g, m_ref[...], v_ref[...])
        g_ref[...] = g
        d_ref[...] = delta
        m2_ref[...] = m2
        v2_ref[...] = v2

    spec = _row(tr, C)
    return pl.pallas_call(
        body, name=name, grid=(R // tr,),
        in_specs=[pl.BlockSpec((n_parts, tr, C), lambda i: (0, i, 0)), spec, spec, spec],
        out_specs=[spec] * 4,
        out_shape=[_sds((R, C), F32)] * 4,
        compiler_params=_params(("arbitrary",)),
    )(parts, w, m, v)


SMALL_ORDER = ("b_ada", "ln1_g", "ln1_b", "ln2_g", "ln2_b", "gmlp_ln_g", "gmlp_ln_b", "b_spatial", "attn_sink")


def _small_step(gath, params):
    flat = [a for name in SMALL_ORDER for a in params[name]]

    def grad_of(tot, name):
        if name == "b_ada":
            return jnp.concatenate([tot[r:r + 1, :] for r in range(6)], axis=1)
        if name in ("ln1_g", "ln1_b", "ln2_g", "ln2_b"):
            r = 8 + ("ln1_g", "ln1_b", "ln2_g", "ln2_b").index(name)
            return tot[r:r + 1, :]
        if name == "gmlp_ln_g":
            return tot[12:13, :GM_W]
        if name == "gmlp_ln_b":
            return tot[12:13, GM_W:]
        if name == "b_spatial":
            return jnp.concatenate([tot[13:14, g * BLK:(g + 1) * BLK] for g in range(N_GROUPS)], axis=0)[None]
        return tot[14:15, :N_Q_HEADS]

    def body(*refs):
        g_ref, in_refs = refs[0], refs[1:1 + len(flat)]
        tot_ref, out_refs = refs[1 + len(flat)], refs[2 + len(flat):]
        tot = g_ref[0]
        for i in range(1, N_DEV):
            tot = tot + g_ref[i]
        tot_ref[...] = tot
        tot_ref[0:2, :] = tot[0:2, :] + tot[6:8, :]
        tot_ref[15:16, :] = jnp.broadcast_to(jnp.sum(tot[15:16, :], axis=1, keepdims=True), (1, D))
        tot = tot_ref[...]
        for k, name in enumerate(SMALL_ORDER):
            w_ref, m_ref, v_ref = in_refs[3 * k:3 * k + 3]
            g = grad_of(tot, name)
            delta, m2, v2 = _adamw(w_ref[...], g, m_ref[...], v_ref[...])
            for r, val in zip(out_refs[4 * k:4 * k + 4], (g, delta, m2, v2)):
                r[...] = val

    res = pl.pallas_call(
        body, name="small_step", grid=(1,),
        in_specs=[_full((N_DEV, 16, D))] + [_full(a.shape) for a in flat],
        out_specs=[_full((16, D))] + [_full(params[name][0].shape) for name in SMALL_ORDER for _ in range(4)],
        out_shape=[_sds((16, D), F32)] + [_sds(params[name][0].shape, F32) for name in SMALL_ORDER for _ in range(4)],
        compiler_params=_params(("arbitrary",)),
    )(gath, *flat)
    return res[0], {name: res[1 + 4 * k:5 + 4 * k] for k, name in enumerate(SMALL_ORDER)}


def _cctx_finish(gath, c_ctx, m, v):
    def body(g_ref, c_ref, m_ref, v_ref, gr_ref, d_ref, m2_ref, v2_ref):
        ds = g_ref[0]
        for i in range(1, N_DEV):
            ds = ds + g_ref[i]
        c = c_ref[...]
        sg = _sigmoid(c)
        g = ds * (sg * (1.0 + c * (1.0 - sg)))
        delta, m2, v2 = _adamw(c, g, m_ref[...], v_ref[...])
        gr_ref[...] = g
        d_ref[...] = delta
        m2_ref[...] = m2
        v2_ref[...] = v2

    return pl.pallas_call(
        body, name="cctx_finish", grid=(1,),
        in_specs=[_full((N_DEV, 8, D))] + [_full((8, D))] * 3, out_specs=[_full((8, D))] * 4,
        out_shape=[_sds((8, D), F32)] * 4,
        compiler_params=_params(("arbitrary",)),
    )(gath, c_ctx, m, v)


def _pad_rows(a, rows):
    return jnp.concatenate([a, jnp.zeros((rows - a.shape[0], a.shape[1]), a.dtype)], axis=0)


def kernel(x, c, ctx, c_ctx, w_ada, b_ada, w_in, attn_sink, gmlp_ln_g, gmlp_ln_b, w_spatial, b_spatial, w_branch_a, w_branch_b, w_out, ln1_g, ln1_b, w_ffn_in, w_ffn_out, ln2_g, ln2_b, loss_target, m_c_ctx, m_w_ada, m_b_ada, m_w_in, m_attn_sink, m_gmlp_ln_g, m_gmlp_ln_b, m_w_spatial, m_b_spatial, m_w_branch_a, m_w_branch_b, m_w_out, m_ln1_g, m_ln1_b, m_w_ffn_in, m_w_ffn_out, m_ln2_g, m_ln2_b, v_c_ctx, v_w_ada, v_b_ada, v_w_in, v_attn_sink, v_gmlp_ln_g, v_gmlp_ln_b, v_w_spatial, v_b_spatial, v_w_branch_a, v_w_branch_b, v_w_out, v_ln1_g, v_ln1_b, v_w_ffn_in, v_w_ffn_out, v_ln2_g, v_ln2_b):
    L = x.shape[1]
    me = 4 * lax.axis_index("x") + 2 * lax.axis_index("y") + lax.axis_index("c")
    x2, tgt, ctx2 = x[0], loss_target[0], ctx[0]
    tiles = _Tiles(L)
    tm_in, tm, tt = tiles.wide, tiles.narrow, tiles.tokens

    transposed = ("w_in", "w_ffn_in")
    tr = lambda kname, a: a.T if kname in transposed else a
    big = dict(w_in=w_in[0].T, w_branch_a=w_branch_a[0], w_branch_b=w_branch_b[0], w_out=w_out[0],
               w_ffn_in=w_ffn_in[0].T, w_ffn_out=w_ffn_out[0])
    col_sharded = ("w_branch_a", "w_branch_b")
    shard_bf = {k: a.astype(BF16) for k, a in big.items()}

    def assemble(kname, g):
        if kname in col_sharded:
            return g.transpose(1, 0, 2).reshape(g.shape[1], N_DEV * g.shape[2])
        return g.reshape(N_DEV * g.shape[1], g.shape[2])

    def to_blocks(kname, g):
        if kname in col_sharded:
            return g.reshape(g.shape[0], N_DEV, g.shape[1] // N_DEV).transpose(1, 0, 2)
        return g.reshape(N_DEV, g.shape[0] // N_DEV, g.shape[1])

    full = {}
    n_ada = w_ada.shape[2]
    b_my = lax.dynamic_slice(b_ada, (0, me * n_ada), (1, n_ada))
    act, mod_all, got = _prologue(_pad_rows(c, 8), _pad_rows(c_ctx[None, :], 8), w_ada[0], b_my,
                                  _Comm(gather=[shard_bf["w_in"]]))
    full["w_in"] = assemble("w_in", got[0])
    mod_all = mod_all.transpose(1, 0, 2).reshape(16, 6 * D)
    modv = _pad_rows(lax.dynamic_slice(mod_all, (me, 0), (1, 6 * D)).reshape(6, D), 8)
    modc = _pad_rows(mod_all[8].reshape(6, D), 8)

    lnv = _pad_rows(jnp.concatenate([ln1_g, ln1_b, ln2_g, ln2_b], axis=0), 8)
    gm_lnv = _pad_rows(jnp.concatenate([gmlp_ln_g, gmlp_ln_b], axis=0), 8)
    ws_b = w_spatial[0].astype(BF16)
    wst_b = ws_b.transpose(0, 2, 1)
    bsp = jnp.repeat(b_spatial[0].T, GROUP_DIM, axis=1)
    sink = attn_sink[0]
    cos, sin = _rope_tables(L)
    bias = _attn_bias()
    w_kv = full["w_in"][O_K:O_K + 2 * KV_W, :]

    (h, q, k, v, u, vb, ga, gb), got = _k_in(
        x2, modv, full["w_in"], cos, sin, tm_in,
        comm=_Comm(gather=[shard_bf[kname] for kname in ("w_branch_a", "w_branch_b", "w_out", "w_ffn_out")]))
    for kname, g in zip(("w_branch_a", "w_branch_b", "w_out", "w_ffn_out"), got):
        full[kname] = assemble(kname, g)
    hc, kc, vc = _k_ctx(ctx2, modc, w_kv)
    (ya, lse), got = _k_attn(sink, q, k, v, kc, vc, bias, comm=_Comm(gather=[shard_bf["w_ffn_in"]]))
    full["w_ffn_in"] = assemble("w_ffn_in", got[0])
    yb = _k_gmlp(u, vb, gm_lnv, ws_b, bsp)
    merged, mix, xm, h2 = _k_merge(x2, ya, yb, ga, gb, full["w_branch_a"], full["w_branch_b"], full["w_out"], modv, lnv, tm_in)
    gate, up, act_f, dr2, df, acc_f = _k_ffn(h2, xm, tgt, full["w_ffn_in"], full["w_ffn_out"], modv, lnv, tm_in)

    dF, dmix, dxp, acc_b = _k_ffn_bwd(df, gate, up, xm, dr2, x2, mix, full["w_ffn_in"], full["w_ffn_out"], modv, lnv, tm)
    blk_fo = to_blocks("w_ffn_out", _wgrad(act_f, df, "wgrad_ffn_out", tiles.tk_ffn, tt))
    gw_fi, (rcv_fo,) = _wgrad(dF, h2, "wgrad_ffn_in", tiles.tk_ffn, tt, comm=_Comm(scatter=[blk_fo]))
    blk_fi = to_blocks("w_ffn_in", gw_fi)
    dga, dgb, dya, dyb, gw_a, gw_b, gw_o = _k_merge_bwd(
        dmix, merged, ya, yb, ga, gb, full["w_branch_a"], full["w_branch_b"], full["w_out"], tm_in)
    du, dvb, g_ws, g_bst, g_gln = _k_gmlp_bwd(u, vb, dyb, gm_lnv, ws_b, wst_b, bsp)
    (dq, dk_late, dv_late, dkc, dvc, g_sink), (gath_ws, rcv_fi) = _k_attn_bwd(
        sink, q, k, v, kc, vc, dya, lse, cos, sin, bias,
        comm=_Comm(gather=[g_ws.reshape(N_GROUPS * BLK, BLK)], scatter=[blk_fi]))
    dk, dv = dk_late[BLK:BLK + L], dv_late[BLK:BLK + L]
    blk_a, blk_b, blk_o = to_blocks("w_branch_a", gw_a), to_blocks("w_branch_b", gw_b), to_blocks("w_out", gw_o)
    (dP, grad_x, acc_i), _ = _k_in_bwd(dq, dk, dv, du, dvb, dga, dgb, x2, dxp, full["w_in"], modv, tm_in)
    g_ctx, dmodc = _k_ctx_bwd(ctx2, modc, hc, dkc, dvc, w_kv)
    gw_in, (rcv_a, rcv_b, rcv_o) = _wgrad(dP, h, "wgrad_in", tiles.tk_in, tt, comm=_Comm(scatter=[blk_a, blk_b, blk_o]),
                                          extra=(O_K, g_ctx))

    dmod_x = jnp.concatenate([acc_i[0:2], acc_b[4:5], acc_b[0:2], acc_f[2:3]], axis=0)
    small = jnp.concatenate([
        dmod_x, dmodc[0:2], acc_b[2:4], acc_f[0:2],
        jnp.concatenate([g_gln[0:1], g_gln[1:2]], axis=1), g_bst.T.reshape(1, D),
        _pad_rows(g_sink[:, 0:1], D).T, acc_f[3:4]], axis=0)
    rcv_in, gath = _exchange_two_level(to_blocks("w_in", gw_in), small, "exchange_last")
    received = dict(w_in=rcv_in, w_branch_a=rcv_a, w_branch_b=rcv_b, w_out=rcv_o, w_ffn_in=rcv_fi, w_ffn_out=rcv_fo)
    moments = dict(w_in=(m_w_in, v_w_in), w_branch_a=(m_w_branch_a, v_w_branch_a), w_branch_b=(m_w_branch_b, v_w_branch_b),
                   w_out=(m_w_out, v_w_out), w_ffn_in=(m_w_ffn_in, v_w_ffn_in), w_ffn_out=(m_w_ffn_out, v_w_ffn_out))
    names = list(big)
    res = {}
    for kname in names:
        mm, vv = moments[kname]
        R = big[kname].shape[0]
        res[kname] = [tr(kname, r) for r in _adamw_reduce(
            received[kname], big[kname], tr(kname, mm[0]), tr(kname, vv[0]), "adamw_" + kname, 256 if R % 256 == 0 else R // 2)]

    ws2d = lambda a: a.reshape(N_GROUPS * BLK, BLK)
    res_ws = [r.reshape(w_spatial.shape) for r in _adamw_reduce(
        gath_ws, ws2d(w_spatial), ws2d(m_w_spatial), ws2d(v_w_spatial), "adamw_w_spatial", 256)]
    tot, res_small = _small_step(gath, dict(
        b_ada=(b_ada, m_b_ada, v_b_ada), ln1_g=(ln1_g, m_ln1_g, v_ln1_g), ln1_b=(ln1_b, m_ln1_b, v_ln1_b),
        ln2_g=(ln2_g, m_ln2_g, v_ln2_g), ln2_b=(ln2_b, m_ln2_b, v_ln2_b),
        gmlp_ln_g=(gmlp_ln_g, m_gmlp_ln_g, v_gmlp_ln_g), gmlp_ln_b=(gmlp_ln_b, m_gmlp_ln_b, v_gmlp_ln_b),
        b_spatial=(b_spatial, m_b_spatial, v_b_spatial), attn_sink=(attn_sink, m_attn_sink, v_attn_sink)))
    loss = tot[15, 0]

    dmod_rows = jnp.concatenate([gath[:, 0:6, :].reshape(N_DEV, 6 * D),
                                 jnp.concatenate([tot[6:8].reshape(1, 2 * D), jnp.zeros((1, 4 * D), F32)], axis=1),
                                 jnp.zeros((7, 6 * D), F32)], axis=0)
    dmod_my = lax.dynamic_slice(dmod_rows, (0, me * n_ada), (16, n_ada))
    g_wada, d_wada, m2_wada, v2_wada, pc = _ada_bwd(act, dmod_my, w_ada[0], m_w_ada[0], v_w_ada[0])
    pc_all = _gather_rows(pc, "gather_cctx")
    cc8 = lambda a: _pad_rows(a.reshape(1, D), 8)
    g_cc, d_cc, m2_cc, v2_cc = _cctx_finish(pc_all, cc8(c_ctx), cc8(m_c_ctx), cc8(v_c_ctx))

    order = ["c_ctx", "w_ada", "b_ada", "w_in", "attn_sink", "gmlp_ln_g", "gmlp_ln_b", "w_spatial", "b_spatial",
             "w_branch_a", "w_branch_b", "w_out", "ln1_g", "ln1_b", "w_ffn_in", "w_ffn_out", "ln2_g", "ln2_b"]
    grads, deltas, new_m, new_v = {}, {}, {}, {}
    grads["c_ctx"], deltas["c_ctx"], new_m["c_ctx"], new_v["c_ctx"] = g_cc[0], d_cc[0], m2_cc[0], v2_cc[0]
    grads["w_ada"], deltas["w_ada"], new_m["w_ada"], new_v["w_ada"] = g_wada[None], d_wada[None], m2_wada[None], v2_wada[None]
    for kname in names:
        g, d, m2, v2 = res[kname]
        grads[kname], deltas[kname], new_m[kname], new_v[kname] = g[None], d[None], m2[None], v2[None]
    grads["w_spatial"], deltas["w_spatial"], new_m["w_spatial"], new_v["w_spatial"] = res_ws
    for kname in SMALL_ORDER:
        grads[kname], deltas[kname], new_m[kname], new_v[kname] = res_small[kname]
    return (loss, grad_x[None], *[grads[n] for n in order], *[deltas[n] for n in order],
            *[new_m[n] for n in order], *[new_v[n] for n in order])
```

```python
import functools
import math

import jax
import jax.numpy as jnp
import numpy as np
from jax import lax
from jax.experimental import pallas as pl
from jax.experimental.pallas import tpu as pltpu

F32 = jnp.float32
BF16 = jnp.bfloat16
MESH = pl.DeviceIdType.MESH

N_DEV = 8
D = 1024
HEAD_DIM = 64
N_Q_HEADS = 8
N_KV_HEADS = 2
GQA_GROUP = 4
BLK = 128
Q_W = 512
KV_W = 128
GM_W = 512
N_GROUPS = 8
GROUP_DIM = 64
FFN_H = 2816
IN_W = 3840
O_Q, O_K, O_V, O_U, O_VB, O_GA, O_GB = 0, 512, 640, 768, 1280, 1792, 2816
LN_EPS = 1e-5
NEG_INF = -1e30
ALPHA = 2.0 ** 0.25
ROPE_BASE = 10000.0
ROPE_PAIRS = 16
Q_SCALE = HEAD_DIM ** -0.5
GELU_K0 = math.sqrt(2.0 / math.pi)
GELU_K1 = 0.044715

ADAM_LR = 0.001
ADAM_B1 = 0.9
ADAM_B2 = 0.999
ADAM_EPS = 1e-08
ADAM_WD = 0.01
ADAM_STEP = 10

V7X_VMEM_BYTES = 64 * 1024 * 1024
VMEM_LIMIT = V7X_VMEM_BYTES * 7 // 8
NT = (((1,), (1,)), ((), ()))
TN = (((0,), (0,)), ((), ()))


class _Tiles:
    def __init__(self, L):
        self.wide = min(512, L)
        self.narrow = min(256, L)
        self.tokens = min(2048, L)
        self.tk_ffn = FFN_H // 2


def _params(sem=None):
    return pltpu.CompilerParams(dimension_semantics=sem, vmem_limit_bytes=VMEM_LIMIT)


def _row(tm, w):
    return pl.BlockSpec((tm, w), lambda i: (i, 0))


def _full(shape):
    nd = len(shape)
    return pl.BlockSpec(shape, lambda i: (0,) * nd)


def _resident(shape):
    nd = len(shape)
    return pl.BlockSpec(shape, lambda i: (0,) * nd, pipeline_mode=pl.Buffered(1))


def _sds(shape, dt):
    return jax.ShapeDtypeStruct(shape, dt)


def _ln(xf):
    mu = jnp.mean(xf, axis=-1, keepdims=True)
    xc = xf - mu
    var = jnp.mean(xc * xc, axis=-1, keepdims=True)
    rstd = lax.rsqrt(var + LN_EPS)
    return xc * rstd, rstd


def _ln_bwd(dn, n, rstd):
    m1 = jnp.mean(dn, axis=-1, keepdims=True)
    m2 = jnp.mean(dn * n, axis=-1, keepdims=True)
    return rstd * (dn - m1 - n * m2)


def _colsum(t):
    return jnp.sum(t, axis=0, keepdims=True)


def _sigmoid(x):
    return 0.5 * jnp.tanh(0.5 * x) + 0.5


def _gelu(x):
    t = jnp.tanh(GELU_K0 * (x + GELU_K1 * (x * x * x)))
    return x * (0.5 * (1.0 + t)), t


def _gelu_grad(x, t):
    return 0.5 * (1.0 + t) + 0.5 * x * (1.0 - t * t) * (GELU_K0 * (1.0 + 3.0 * GELU_K1 * x * x))


def _swap16(t):
    lane = lax.broadcasted_iota(jnp.int32, t.shape, 1)
    return jnp.where((lane & 16) == 0, pltpu.roll(t, 112, 1), pltpu.roll(t, 16, 1))


def _rope(t, cos, sin):
    return t * cos + _swap16(t) * sin


def _unrope(t, cos, sin):
    return t * cos - _swap16(t) * sin


def _adamw(w, g, m, v):
    m2 = ADAM_B1 * m + (1.0 - ADAM_B1) * g
    v2 = ADAM_B2 * v + (1.0 - ADAM_B2) * (g * g)
    m_hat = m2 / (1.0 - ADAM_B1 ** ADAM_STEP)
    v_hat = v2 / (1.0 - ADAM_B2 ** ADAM_STEP)
    delta = -ADAM_LR * (m_hat / (jnp.sqrt(v_hat) + ADAM_EPS) + ADAM_WD * w)
    return delta, m2, v2


def _rope_tables(L):
    inv = (np.float32(ROPE_BASE) ** (-np.arange(ROPE_PAIRS, dtype=np.float32) / np.float32(ROPE_PAIRS))).astype(np.float32)
    t = np.arange(L, dtype=np.int32)
    rows = (t // 64).astype(np.float32)[:, None] * inv
    cols = (t % 64).astype(np.float32)[:, None] * inv
    cr, sr, cc, sc = np.cos(rows), np.sin(rows), np.cos(cols), np.sin(cols)
    cos = np.concatenate([cr, cr, cc, cc], axis=1)
    sin = np.concatenate([-sr, sr, -sc, sc], axis=1)
    return jnp.asarray(np.tile(cos, (1, 2)), F32), jnp.asarray(np.tile(sin, (1, 2)), F32)


def _me():
    return lax.axis_index("x"), lax.axis_index("y"), lax.axis_index("c")


def _peer(mx, my, mc, k):
    return (mx ^ ((k >> 2) & 1), my ^ ((k >> 1) & 1), mc ^ (k & 1))


class _Comm:
    def __init__(self, gather=(), scatter=(), spread=()):
        self.kinds = ["gather"] * len(gather) + ["scatter"] * len(scatter) + ["spread"] * len(spread)
        self.args = list(gather) + list(scatter) + list(spread)
        self.n = len(self.args)

    def out_shape(self):
        return [_sds(a.shape if k == "scatter" else (N_DEV,) + a.shape, a.dtype) for k, a in zip(self.kinds, self.args)]

    def specs(self):
        return [pl.BlockSpec(memory_space=pl.ANY)] * self.n

    def scratch(self):
        return [pltpu.SemaphoreType.DMA((7 * self.n,)), pltpu.SemaphoreType.DMA((7 * self.n,)),
                pltpu.SemaphoreType.DMA((self.n,))]

    def _plan(self, x_refs, out_refs, send_sems, recv_sems, local_sems):
        mx, my, mc = _me()
        me = 4 * mx + 2 * my + mc
        here, sibling = (mx, my, mc), (mx, my, 1 - mc)
        chips = [(1 - mx, my), (mx, 1 - my), (1 - mx, 1 - my)]
        local, first, last = [], [], []
        relay = [[], [], []]
        for a, kind in enumerate(self.kinds):
            x, out = x_refs[a], out_refs[a]

            def rc(k, src, dst, to):
                return pltpu.make_async_remote_copy(
                    src_ref=src, dst_ref=dst, send_sem=send_sems.at[7 * a + k], recv_sem=recv_sems.at[7 * a + k],
                    device_id=to, device_id_type=MESH)

            if kind == "gather":
                local.append(pltpu.make_async_copy(x, out.at[me], local_sems.at[a]))
                first.append(rc(0, x, out.at[me], sibling))
                last.append(rc(0, x, out.at[me ^ 1], here))
                for j, (cx, cy) in enumerate(chips):
                    first.append(rc(1 + j, x, out.at[me], (cx, cy, mc)))
                    landed = out.at[4 * cx + 2 * cy + mc]
                    relay[j].append((rc(1 + j, x, landed, here), rc(4 + j, landed, landed, sibling)))
                    last.append(rc(4 + j, x, out.at[4 * cx + 2 * cy + 1 - mc], here))
            else:
                own = x.at[me] if kind == "scatter" else x
                local.append(pltpu.make_async_copy(own, out.at[me], local_sems.at[a]))
                for k in range(1, N_DEV):
                    src = x.at[me ^ k] if kind == "scatter" else x
                    first.append(rc(k - 1, src, out.at[me], _peer(mx, my, mc, k)))
                    last.append(rc(k - 1, own, out.at[me ^ k], here))
        return local, first, relay[0] + relay[1] + relay[2], last

    def start(self, *refs):
        local, first, _, _ = self._plan(*refs)
        for cp in local + first:
            cp.start()

    def relay(self, *refs):
        _, _, relay, _ = self._plan(*refs)
        for arrival, onward in relay:
            arrival.wait_recv()
            onward.start()

    def finish(self, *refs):
        local, first, relay, last = self._plan(*refs)
        for cp in last:
            cp.wait_recv()
        for cp in first:
            cp.wait_send()
        for _, onward in relay:
            onward.wait_send()
        for cp in local:
            cp.wait()


def _call(body, *, name, grid, in_specs, out_specs, out_shape, args, scratch=(), comm=None, aliases=None):
    params = _params(("arbitrary",) * len(grid))
    total = math.prod(grid)

    def at(step):
        flat = functools.reduce(lambda acc, dn: acc * dn[1] + pl.program_id(dn[0]), enumerate(grid), 0)
        return flat == step

    if comm is None:
        res = pl.pallas_call(
            body, name=name, grid=grid, in_specs=list(in_specs), out_specs=list(out_specs), out_shape=list(out_shape),
            scratch_shapes=list(scratch), input_output_aliases=aliases or {}, compiler_params=params)(*args)
        return list(res), []
    n_in, n_out, n_scr, cn = len(in_specs), len(out_specs), len(scratch), comm.n

    def hosted(*refs):
        ins, refs = refs[:n_in], refs[n_in:]
        cins, refs = refs[:cn], refs[cn:]
        outs, refs = refs[:n_out], refs[n_out:]
        couts, refs = refs[:cn], refs[cn:]
        scr, sems = refs[:n_scr], refs[n_scr:]

        @pl.when(at(0))
        def _():
            comm.start(cins, couts, *sems)

        body(*ins, *outs, *scr)

        @pl.when(at((3 * total) // 4 if total >= 4 else total - 1))
        def _():
            comm.relay(cins, couts, *sems)

        @pl.when(at(total - 1))
        def _():
            comm.finish(cins, couts, *sems)

    res = pl.pallas_call(
        hosted, name=name, grid=grid, in_specs=list(in_specs) + comm.specs(), out_specs=list(out_specs) + comm.specs(),
        out_shape=list(out_shape) + comm.out_shape(), scratch_shapes=list(scratch) + comm.scratch(),
        input_output_aliases=aliases or {}, compiler_params=params)(*args, *comm.args)
    return list(res[:n_out]), list(res[n_out:])


def _exchange_two_level(blk, small, name):
    _, R, C = blk.shape
    rows = small.shape[0]

    def body(blk_ref, small_ref, stage_ref, out_ref, gath_ref, a_scr, b_scr, t_scr, s1, r1, s3, r3, ss, rs, lsem):
        mx, my, mc = _me()
        me = 4 * mx + 2 * my + mc
        mine = 2 * mx + my
        here, sibling = (mx, my, mc), (mx, my, 1 - mc)

        def rc(src, dst, send, recv, to):
            return pltpu.make_async_remote_copy(src_ref=src, dst_ref=dst, send_sem=send, recv_sem=recv,
                                                device_id=to, device_id_type=MESH)

        own_small = pltpu.make_async_copy(small_ref, gath_ref.at[me], lsem.at[0])
        own_small.start()
        spread = [rc(small_ref, gath_ref.at[me], ss.at[k - 1], rs.at[k - 1], _peer(mx, my, mc, k)) for k in range(1, N_DEV)]
        order = (1, 2, 3, 0)
        to_sib = [rc(blk_ref.at[2 * (mine ^ k) + 1 - mc], stage_ref.at[k], s1.at[k], r1.at[k], sibling) for k in order]
        for cp in spread + to_sib:
            cp.start()
        own = {k: pltpu.make_async_copy(blk_ref.at[2 * (mine ^ k) + mc], a_scr.at[k], lsem.at[1 + k]) for k in order}
        for k in order:
            own[k].start()
        onward = []
        for k in order:
            rc(blk_ref.at[0], stage_ref.at[k], s1.at[k], r1.at[k], here).wait_recv()
            landed = pltpu.make_async_copy(stage_ref.at[k], b_scr.at[k], lsem.at[5 + k])
            landed.start()
            landed.wait()
            own[k].wait()
            t_scr[k] = (a_scr[k].astype(F32) + b_scr[k].astype(F32)).astype(BF16)
            if k > 0:
                cp = rc(t_scr.at[k], out_ref.at[mine], s3.at[k - 1], r3.at[k - 1], (mx ^ (k >> 1), my ^ (k & 1), mc))
                cp.start()
                onward.append(cp)
        keep = pltpu.make_async_copy(t_scr.at[0], out_ref.at[mine], lsem.at[9])
        keep.start()
        for k in range(1, 4):
            rc(t_scr.at[0], out_ref.at[mine ^ k], s3.at[k - 1], r3.at[k - 1], here).wait_recv()
        for k in range(1, N_DEV):
            rc(small_ref, gath_ref.at[me ^ k], ss.at[k - 1], rs.at[k - 1], here).wait_recv()
        for cp in spread + to_sib + onward:
            cp.wait_send()
        keep.wait()
        own_small.wait()

    any_spec = pl.BlockSpec(memory_space=pl.ANY)
    dma = pltpu.SemaphoreType.DMA
    _, out, gath = pl.pallas_call(
        body, name=name,
        in_specs=[any_spec, any_spec], out_specs=[any_spec] * 3,
        out_shape=[_sds((4, R, C), BF16), _sds((4, R, C), BF16), _sds((N_DEV, rows, D), F32)],
        scratch_shapes=[pltpu.VMEM((4, R, C), BF16)] * 3
                       + [dma((4,)), dma((4,)), dma((3,)), dma((3,)), dma((N_DEV - 1,)), dma((N_DEV - 1,)), dma((10,))],
        compiler_params=pltpu.CompilerParams(vmem_limit_bytes=VMEM_LIMIT),
    )(blk, small)
    return out, gath


def _exchange_rows(x_ref, out_ref, send_sems, recv_sems, between=None):
    mx, my, mc = _me()
    me = 4 * mx + 2 * my + mc
    out_ref[pl.ds(me, 1)] = x_ref[...][None]
    sends = []
    for k in range(1, N_DEV):
        cp = pltpu.make_async_remote_copy(
            src_ref=x_ref, dst_ref=out_ref.at[me], send_sem=send_sems.at[k - 1], recv_sem=recv_sems.at[k - 1],
            device_id=_peer(mx, my, mc, k), device_id_type=MESH)
        cp.start()
        sends.append(cp)
    if between is not None:
        between()
    for k in range(1, N_DEV):
        pltpu.make_async_remote_copy(
            src_ref=x_ref, dst_ref=out_ref.at[me ^ k], send_sem=send_sems.at[k - 1], recv_sem=recv_sems.at[k - 1],
            device_id=(mx, my, mc), device_id_type=MESH).wait_recv()
    for cp in sends:
        cp.wait_send()


def _prologue(c8, cctx8, w_ada, b_my, comm):
    nw = w_ada.shape[1]

    cn = comm.n

    def body(*refs):
        c_ref, cctx_ref, w_ref, b_ref = refs[:4]
        cins, refs = refs[4:4 + cn], refs[4 + cn:]
        act_ref, mod_ref = refs[:2]
        couts, refs = refs[2:2 + cn], refs[2 + cn:]
        cmine_scr, call_scr, mine_scr, mall_scr, s1, r1, s2, r2 = refs[:8]
        csems = refs[8:]
        cmine_scr[...] = c_ref[...]
        _exchange_rows(cmine_scr, call_scr, s1, r1)
        rows = [call_scr[d][0:1, :] for d in range(N_DEV)] + [cctx_ref[0:1, :], jnp.zeros((7, D), F32)]
        s = jnp.concatenate(rows, axis=0)
        act = s * _sigmoid(s)
        act_ref[...] = act
        mine_scr[...] = jnp.dot(act.astype(BF16), w_ref[...].astype(BF16), preferred_element_type=F32) + b_ref[...]
        _exchange_rows(mine_scr, mall_scr, s2, r2, between=lambda: comm.start(cins, couts, *csems))
        mod_ref[...] = mall_scr[...]
        comm.relay(cins, couts, *csems)
        comm.finish(cins, couts, *csems)

    sems = [pltpu.SemaphoreType.DMA((N_DEV - 1,))] * 4
    res = pl.pallas_call(
        body, name="prologue", grid=(1,),
        in_specs=[_full((8, D)), _full((8, D)), _full((D, nw)), _full((1, nw))] + comm.specs(),
        out_specs=[_full((16, D)), _full((N_DEV, 16, nw))] + comm.specs(),
        out_shape=[_sds((16, D), F32), _sds((N_DEV, 16, nw), F32)] + comm.out_shape(),
        scratch_shapes=[pltpu.VMEM((8, D), F32), pltpu.VMEM((N_DEV, 8, D), F32), pltpu.VMEM((16, nw), F32),
                        pltpu.VMEM((N_DEV, 16, nw), F32)] + sems + comm.scratch(),
        compiler_params=_params(("arbitrary",)),
    )(c8, cctx8, w_ada, b_my, *comm.args)
    return res[0], res[1], list(res[2:])


def _gather_rows(x, name):
    def body(x_ref, out_ref, send_sems, recv_sems):
        _exchange_rows(x_ref, out_ref, send_sems, recv_sems)

    return pl.pallas_call(
        body, name=name,
        out_shape=_sds((N_DEV,) + x.shape, x.dtype),
        in_specs=[pl.BlockSpec(memory_space=pltpu.VMEM)],
        out_specs=pl.BlockSpec(memory_space=pltpu.VMEM),
        scratch_shapes=[pltpu.SemaphoreType.DMA((N_DEV - 1,)), pltpu.SemaphoreType.DMA((N_DEV - 1,))],
        compiler_params=pltpu.CompilerParams(vmem_limit_bytes=VMEM_LIMIT),
    )(x)


def _ada_bwd(act, dmod_my, w_ada, m, v, tr=256):
    nw = w_ada.shape[1]

    def body(act_ref, dm_ref, w_ref, m_ref, v_ref, g_ref, d_ref, m2_ref, v2_ref, pc_ref):
        dm = dm_ref[...].astype(BF16)
        g = lax.dot_general(act_ref[...].astype(BF16), dm, TN, preferred_element_type=F32)
        w = w_ref[...]
        delta, m2, v2 = _adamw(w, g, m_ref[...], v_ref[...])
        g_ref[...] = g
        d_ref[...] = delta
        m2_ref[...] = m2
        v2_ref[...] = v2
        pc_ref[...] = lax.dot_general(dm[8:16, :], w.astype(BF16), NT, preferred_element_type=F32)

    wspec = _row(tr, nw)
    return pl.pallas_call(
        body, name="ada_bwd", grid=(D // tr,),
        in_specs=[pl.BlockSpec((16, tr), lambda i: (0, i)), _full((16, nw)), wspec, wspec, wspec],
        out_specs=[wspec, wspec, wspec, wspec, pl.BlockSpec((8, tr), lambda i: (0, i))],
        out_shape=[_sds((D, nw), F32)] * 4 + [_sds((8, D), F32)],
        compiler_params=_params(("arbitrary",)),
    )(act, dmod_my, w_ada, m, v)


def _k_in(x, modv, w_in, cos, sin, tm, comm=None):
    L = x.shape[0]

    def body(x_ref, mod_ref, w_ref, cos_ref, sin_ref, h_ref, q_ref, k_ref, v_ref, u_ref, vb_ref, ga_ref, gb_ref):
        n, _ = _ln(x_ref[...])
        h = (n * (1.0 + mod_ref[1:2, :]) + mod_ref[0:1, :]).astype(BF16)
        h_ref[...] = h
        c, s = cos_ref[...], sin_ref[...]

        def proj(lo, width):
            return lax.dot_general(h, w_ref[lo:lo + width, :], NT, preferred_element_type=F32)

        for i in range(2):
            qh = proj(O_Q + i * 256, 256)
            for j in range(2):
                q_ref[:, i * 256 + j * 128:i * 256 + (j + 1) * 128] = (
                    _rope(qh[:, j * 128:(j + 1) * 128], c, s) * Q_SCALE).astype(BF16)
        kv = proj(O_K, 2 * KV_W)
        k_ref[...] = _rope(kv[:, :KV_W], c, s).astype(BF16)
        v_ref[...] = kv[:, KV_W:].astype(BF16)
        u_ref[...] = proj(O_U, GM_W).astype(BF16)
        vb_ref[...] = proj(O_VB, GM_W).astype(BF16)
        ga_ref[...] = proj(O_GA, D).astype(BF16)
        gb_ref[...] = proj(O_GB, D).astype(BF16)

    widths = [D, Q_W, KV_W, KV_W, GM_W, GM_W, D, D]
    return _call(
        body, name="fwd_in", grid=(L // tm,),
        in_specs=[_row(tm, D), _full((8, D)), _resident((IN_W, D)), _row(tm, 128), _row(tm, 128)],
        out_specs=[_row(tm, w) for w in widths],
        out_shape=[_sds((L, w), BF16) for w in widths],
        args=(x, modv, w_in, cos, sin), comm=comm)


def _k_ctx(ctx, modc, w_kv):
    C = ctx.shape[0]

    def body(c_ref, mod_ref, w_ref, hc_ref, kc_ref, vc_ref):
        n, _ = _ln(c_ref[...])
        hc = (n * (1.0 + mod_ref[1:2, :]) + mod_ref[0:1, :]).astype(BF16)
        hc_ref[...] = hc
        kv = lax.dot_general(hc, w_ref[...], NT, preferred_element_type=F32)
        kc_ref[...] = kv[:, :KV_W].astype(BF16)
        vc_ref[...] = kv[:, KV_W:].astype(BF16)

    return pl.pallas_call(
        body, name="fwd_ctx", grid=(1,),
        in_specs=[_full((C, D)), _full((8, D)), _full((2 * KV_W, D))],
        out_specs=[_full((C, D)), _full((C, KV_W)), _full((C, KV_W))],
        out_shape=[_sds((C, D), BF16), _sds((C, KV_W), BF16), _sds((C, KV_W), BF16)],
        compiler_params=_params(("arbitrary",)),
    )(ctx, modc, w_kv)


def _attn_bias():
    r = (np.arange(GQA_GROUP * BLK) & (BLK - 1))[:, None]
    j = np.arange(3 * BLK)[None, :]
    band = np.abs(j - BLK - r) <= BLK
    variants = [band & (j >= BLK), band, band & (j < 2 * BLK)]
    return jnp.asarray(np.stack([np.where(v, 0.0, NEG_INF) for v in variants]), F32)


def _masked(s, bias, C):
    return jnp.concatenate([s[:, :C], s[:, C:] + bias], axis=1)


def _sink_col(sink_ref, hk):
    grp = lax.broadcasted_iota(jnp.int32, (GQA_GROUP * BLK, 1), 0) >> 7
    col = jnp.full((GQA_GROUP * BLK, 1), sink_ref[hk * GQA_GROUP], F32)
    for g in range(1, GQA_GROUP):
        col = jnp.where(grp == g, sink_ref[hk * GQA_GROUP + g], col)
    return col


ATTN_FWD_BLOCKS = 4


def _k_attn(sink, q, k, v, kc, vc, bias, comm=None):
    L = q.shape[0]
    C = kc.shape[0]
    nb = L // BLK
    nq = min(ATTN_FWD_BLOCKS, nb)
    steps = nb // nq

    def body(sink_ref, q_ref, kp_ref, km_ref, kx_ref, vp_ref, vm_ref, vx_ref, kc_ref, vc_ref, bias_ref, ya_ref, lse_ref):
        i = pl.program_id(0)
        chains = [(qb, hk) for qb in range(nq) for hk in range(N_KV_HEADS)]

        def band(qb):
            first = jnp.where(i == 0, 0, 1) if qb == 0 else 1
            return bias_ref[jnp.where(i == steps - 1, 2, first) if qb == nq - 1 else first]

        def keys(ctx_ref, p_ref, m_ref, x_ref, qb, hk):
            sl = slice(hk * HEAD_DIM, (hk + 1) * HEAD_DIM)
            blocks = [p_ref[:, sl]] + [m_ref[j * BLK:(j + 1) * BLK, sl] for j in range(nq)] + [x_ref[:, sl]]
            return jnp.concatenate([ctx_ref[:, sl]] + blocks[qb:qb + 3], axis=0)

        def queries(qb, hk):
            return jnp.concatenate(
                [q_ref[qb * BLK:(qb + 1) * BLK, (hk * GQA_GROUP + g) * HEAD_DIM:(hk * GQA_GROUP + g + 1) * HEAD_DIM]
                 for g in range(GQA_GROUP)], axis=0)

        def scores(qb, hk):
            return _masked(lax.dot_general(queries(qb, hk), keys(kc_ref, kp_ref, km_ref, kx_ref, qb, hk), NT,
                                           preferred_element_type=F32), band(qb), C)

        ahead = 2
        s = [scores(*c) for c in chains[:ahead]]
        for n, (qb, hk) in enumerate(chains):
            if n + ahead < len(chains):
                s.append(scores(*chains[n + ahead]))
            s_ = s[n]
            sink_c = _sink_col(sink_ref, hk)
            m = jnp.maximum(jnp.max(s_, axis=1, keepdims=True), sink_c)
            p = jnp.exp(s_ - m)
            den = jnp.sum(p, axis=1, keepdims=True) + jnp.exp(sink_c - m)
            o = jnp.dot(p.astype(BF16), keys(vc_ref, vp_ref, vm_ref, vx_ref, qb, hk), preferred_element_type=F32) * (1.0 / den)
            lse = m + jnp.log(den)
            rows = slice(qb * BLK, (qb + 1) * BLK)
            for g in range(GQA_GROUP):
                h = hk * GQA_GROUP + g
                ya_ref[rows, h * HEAD_DIM:(h + 1) * HEAD_DIM] = o[g * BLK:(g + 1) * BLK, :].astype(BF16)
                lse_ref[rows, h:h + 1] = lse[g * BLK:(g + 1) * BLK, :]

    kv3 = [pl.BlockSpec((BLK, KV_W), lambda i: (jnp.maximum(nq * i - 1, 0), 0)),
           pl.BlockSpec((nq * BLK, KV_W), lambda i: (i, 0)),
           pl.BlockSpec((BLK, KV_W), lambda i: (jnp.minimum(nq * i + nq, nb - 1), 0))]
    return _call(
        body, name="fwd_attn", grid=(steps,),
        in_specs=[pl.BlockSpec(memory_space=pltpu.SMEM), _row(nq * BLK, Q_W)] + kv3 + kv3
                 + [_full((C, KV_W)), _full((C, KV_W)), _full((3, GQA_GROUP * BLK, 3 * BLK))],
        out_specs=[_row(nq * BLK, Q_W), _row(nq * BLK, N_Q_HEADS)],
        out_shape=[_sds((L, Q_W), BF16), _sds((L, N_Q_HEADS), F32)],
        args=(sink, q, k, k, k, v, v, v, kc, vc, bias), comm=comm)


GMLP_CHUNKS = 4


def _split_pair(t):
    low = lax.broadcasted_iota(jnp.int32, t.shape, 1) < GROUP_DIM
    zero = jnp.zeros_like(t)
    return jnp.where(low, t, zero), jnp.where(low, zero, t)


def _gmlp_spatial(w_ref, t_b, nch):
    rows = []
    for c in range(nch):
        tiles = []
        for pr in range(N_GROUPS // 2):
            lo, hi = _split_pair(t_b[c * BLK:(c + 1) * BLK, pr * 128:(pr + 1) * 128])
            tiles.append(jnp.dot(w_ref[2 * pr], lo, preferred_element_type=F32)
                         + jnp.dot(w_ref[2 * pr + 1], hi, preferred_element_type=F32))
        rows.append(jnp.concatenate(tiles, axis=1))
    return jnp.concatenate(rows, axis=0)


def _gmlp_fwd_vals(u, vb, lnv_ref, ws_ref, bsp_ref, nch):
    uf = u.astype(F32)
    vf = vb.astype(F32)
    gu, tu = _gelu(uf)
    gv, tv = _gelu(vf)
    vhat, rstd = _ln(gv)
    vn = (vhat * lnv_ref[0:1, :] + lnv_ref[1:2, :]).astype(BF16)
    s = _gmlp_spatial(ws_ref, vn, nch) + jnp.concatenate([bsp_ref[...]] * nch, axis=0)
    return uf, vf, gu, tu, tv, vhat, rstd, vn, s


def _k_gmlp(u, vb, lnv, ws, bsp):
    L = u.shape[0]
    nch = min(GMLP_CHUNKS, L // BLK)
    tm = nch * BLK

    def body(u_ref, vb_ref, lnv_ref, ws_ref, bsp_ref, yb_ref):
        _, _, gu, _, _, _, _, _, s = _gmlp_fwd_vals(u_ref[...], vb_ref[...], lnv_ref, ws_ref, bsp_ref, nch)
        yb_ref[...] = (gu * s).astype(BF16)

    return pl.pallas_call(
        body, name="fwd_gmlp", grid=(L // tm,),
        in_specs=[_row(tm, GM_W), _row(tm, GM_W), _full((8, GM_W)), _full((N_GROUPS, BLK, BLK)), _full((BLK, GM_W))],
        out_specs=_row(tm, GM_W),
        out_shape=_sds((L, GM_W), BF16),
        compiler_params=_params(("arbitrary",)),
    )(u, vb, lnv, ws, bsp)


def _k_merge(x, ya, yb, ga, gb, w_a, w_b, w_o, modv, lnv, tm):
    L = x.shape[0]

    def body(x_ref, ya_ref, yb_ref, ga_ref, gb_ref, wa_ref, wb_ref, wo_ref, mod_ref, ln_ref,
             mg_ref, mix_ref, xm_ref, h2_ref):
        a = jnp.dot(ya_ref[...], wa_ref[...], preferred_element_type=F32)
        b = jnp.dot(yb_ref[...], wb_ref[...], preferred_element_type=F32)
        merged = (_sigmoid(ga_ref[...].astype(F32)) * a + _sigmoid(gb_ref[...].astype(F32)) * b).astype(BF16)
        mg_ref[...] = merged
        mix = jnp.dot(merged, wo_ref[...], preferred_element_type=F32)
        mix_ref[...] = mix.astype(BF16)
        r1 = ALPHA * x_ref[...] + mod_ref[2:3, :] * mix
        r1hat, _ = _ln(r1)
        xm = r1hat * ln_ref[0:1, :] + ln_ref[1:2, :]
        xm_ref[...] = xm
        n2, _ = _ln(xm)
        h2_ref[...] = (n2 * (1.0 + mod_ref[4:5, :]) + mod_ref[3:4, :]).astype(BF16)

    return pl.pallas_call(
        body, name="fwd_merge", grid=(L // tm,),
        in_specs=[_row(tm, D), _row(tm, Q_W), _row(tm, GM_W), _row(tm, D), _row(tm, D),
                  _resident((Q_W, D)), _resident((GM_W, D)), _resident((D, D)), _full((8, D)), _full((8, D))],
        out_specs=[_row(tm, D)] * 4,
        out_shape=[_sds((L, D), BF16), _sds((L, D), BF16), _sds((L, D), F32), _sds((L, D), BF16)],
        compiler_params=_params(("arbitrary",)),
    )(x, ya, yb, ga, gb, w_a, w_b, w_o, modv, lnv)


FFN_CH = FFN_H // 2
FFN_CH_FWD = 256


def _k_ffn(h2, xm, tgt, w_fi, w_fo, modv, lnv, tm):
    L = h2.shape[0]

    def body(h2_ref, xm_ref, t_ref, wi_ref, wo_ref, mod_ref, ln_ref, gate_ref, up_ref, a_ref, dr2_ref, df_ref, acc_ref):
        @pl.when(pl.program_id(0) == 0)
        def _():
            acc_ref[...] = jnp.zeros_like(acc_ref)

        h2v = h2_ref[...]
        ch = FFN_CH_FWD
        chunks = [j * ch for j in range(FFN_H // ch)]

        def project(lo):
            return (lax.dot_general(h2v, wi_ref[lo:lo + ch, :], NT, preferred_element_type=F32),
                    lax.dot_general(h2v, wi_ref[FFN_H + lo:FFN_H + lo + ch, :], NT, preferred_element_type=F32))

        f = jnp.zeros((tm, D), F32)
        ahead = [project(chunks[0])]
        for j, lo in enumerate(chunks):
            if j + 1 < len(chunks):
                ahead.append(project(chunks[j + 1]))
            gate, up = ahead[j]
            act = (gate * _sigmoid(gate) * up).astype(BF16)
            gate_ref[:, lo:lo + ch] = gate.astype(BF16)
            up_ref[:, lo:lo + ch] = up.astype(BF16)
            a_ref[:, lo:lo + ch] = act
            f = f + jnp.dot(act, wo_ref[lo:lo + ch, :], preferred_element_type=F32)
        gate2 = mod_ref[5:6, :]
        r2 = ALPHA * xm_ref[...] + gate2 * f
        r2hat, rstd = _ln(r2)
        y = r2hat * ln_ref[2:3, :] + ln_ref[3:4, :]
        err = y - t_ref[...]
        dy = err * (1.0 / D)
        dr2 = _ln_bwd(dy * ln_ref[2:3, :], r2hat, rstd)
        dr2_ref[...] = dr2
        df_ref[...] = (gate2 * dr2).astype(BF16)
        acc_ref[0:1, :] += _colsum(dy * r2hat)
        acc_ref[1:2, :] += _colsum(dy)
        acc_ref[2:3, :] += _colsum(dr2 * f)
        acc_ref[3:4, :] += _colsum(err * err) * (0.5 / D)

    return pl.pallas_call(
        body, name="fwd_ffn", grid=(L // tm,),
        in_specs=[_row(tm, D), _row(tm, D), _row(tm, D), _resident((2 * FFN_H, D)), _resident((FFN_H, D)),
                  _full((8, D)), _full((8, D))],
        out_specs=[_row(tm, FFN_H)] * 3 + [_row(tm, D), _row(tm, D), _full((8, D))],
        out_shape=[_sds((L, FFN_H), BF16)] * 3 + [_sds((L, D), F32), _sds((L, D), BF16), _sds((8, D), F32)],
        compiler_params=_params(("arbitrary",)),
    )(h2, xm, tgt, w_fi, w_fo, modv, lnv)


FFN_CH_BWD = 256


def _k_ffn_bwd(df, gate, up, xm, dr2, x, mix, w_fi, w_fo, modv, lnv, tm):
    L = df.shape[0]

    def body(df_ref, gate_ref, up_ref, xm_ref, dr2_ref, x_ref, mix_ref, wi_ref, wo_ref, mod_ref, ln_ref,
             dF_ref, dmix_ref, dxp_ref, acc_ref):
        @pl.when(pl.program_id(0) == 0)
        def _():
            acc_ref[...] = jnp.zeros_like(acc_ref)

        dfv = df_ref[...]
        ch = FFN_CH_BWD
        chunks = [j * ch for j in range(FFN_H // ch)]

        def d_act(lo):
            return lax.dot_general(dfv, wo_ref[lo:lo + ch, :], NT, preferred_element_type=F32)

        n2, rstd2 = _ln(xm_ref[...])
        mixf = mix_ref[...].astype(F32)
        gate1 = mod_ref[2:3, :]
        r1hat, rstd1 = _ln(ALPHA * x_ref[...] + gate1 * mixf)
        dh2 = jnp.zeros((tm, D), F32)
        das = [d_act(chunks[0])]
        for j, lo in enumerate(chunks):
            if j + 1 < len(chunks):
                das.append(d_act(chunks[j + 1]))
            da = das[j]
            gate = gate_ref[:, lo:lo + ch].astype(F32)
            upv = up_ref[:, lo:lo + ch].astype(F32)
            sg = _sigmoid(gate)
            d_gate = (da * upv * (sg * (1.0 + gate * (1.0 - sg)))).astype(BF16)
            d_up = (da * (gate * sg)).astype(BF16)
            dF_ref[:, lo:lo + ch] = d_gate
            dF_ref[:, FFN_H + lo:FFN_H + lo + ch] = d_up
            dh2 = dh2 + jnp.dot(d_gate, wi_ref[lo:lo + ch, :], preferred_element_type=F32)
            dh2 = dh2 + jnp.dot(d_up, wi_ref[FFN_H + lo:FFN_H + lo + ch, :], preferred_element_type=F32)
        acc_ref[0:1, :] += _colsum(dh2)
        acc_ref[1:2, :] += _colsum(dh2 * n2)
        dxm = ALPHA * dr2_ref[...] + _ln_bwd(dh2 * (1.0 + mod_ref[4:5, :]), n2, rstd2)
        acc_ref[2:3, :] += _colsum(dxm * r1hat)
        acc_ref[3:4, :] += _colsum(dxm)
        dr1 = _ln_bwd(dxm * ln_ref[0:1, :], r1hat, rstd1)
        dmix_ref[...] = (gate1 * dr1).astype(BF16)
        dxp_ref[...] = ALPHA * dr1
        acc_ref[4:5, :] += _colsum(dr1 * mixf)

    return pl.pallas_call(
        body, name="bwd_ffn", grid=(L // tm,),
        in_specs=[_row(tm, D), _row(tm, FFN_H), _row(tm, FFN_H), _row(tm, D), _row(tm, D), _row(tm, D), _row(tm, D),
                  _resident((2 * FFN_H, D)), _resident((FFN_H, D)), _full((8, D)), _full((8, D))],
        out_specs=[_row(tm, 2 * FFN_H), _row(tm, D), _row(tm, D), _full((8, D))],
        out_shape=[_sds((L, 2 * FFN_H), BF16), _sds((L, D), BF16), _sds((L, D), F32), _sds((8, D), F32)],
        compiler_params=_params(("arbitrary",)),
    )(df, gate, up, xm, dr2, x, mix, w_fi, w_fo, modv, lnv)


def _k_merge_bwd(dmix, merged, ya, yb, ga, gb, w_a, w_b, w_o, tm):
    L = dmix.shape[0]
    n = L // tm

    def body(dmix_ref, mg_ref, ya_ref, yb_ref, ga_ref, gb_ref, wa_ref, wb_ref, wo_ref,
             dg_ref, dya_ref, dyb_ref, gwa_ref, gwb_ref, gwo_ref, acc_a, acc_b, acc_o):
        i = pl.program_id(0)

        @pl.when(i == 0)
        def _():
            for r in (acc_a, acc_b, acc_o):
                r[...] = jnp.zeros_like(r)

        dmixv = dmix_ref[...]
        dmg = lax.dot_general(dmixv, wo_ref[...], NT, preferred_element_type=F32)
        acc_o[...] += lax.dot_general(mg_ref[...], dmixv, TN, preferred_element_type=F32)
        ya = ya_ref[...]
        a = jnp.dot(ya, wa_ref[...], preferred_element_type=F32)
        sa = _sigmoid(ga_ref[...].astype(F32))
        dA = (dmg * sa).astype(BF16)
        dg_ref[:, :D] = (dmg * a * (sa * (1.0 - sa))).astype(BF16)
        dya_ref[...] = lax.dot_general(dA, wa_ref[...], NT, preferred_element_type=F32).astype(BF16)
        acc_a[...] += lax.dot_general(ya, dA, TN, preferred_element_type=F32)
        yb = yb_ref[...]
        b = jnp.dot(yb, wb_ref[...], preferred_element_type=F32)
        sb = _sigmoid(gb_ref[...].astype(F32))
        dB = (dmg * sb).astype(BF16)
        dg_ref[:, D:] = (dmg * b * (sb * (1.0 - sb))).astype(BF16)
        dyb_ref[...] = lax.dot_general(dB, wb_ref[...], NT, preferred_element_type=F32).astype(BF16)
        acc_b[...] += lax.dot_general(yb, dB, TN, preferred_element_type=F32)

        @pl.when(i == n - 1)
        def _():
            gwa_ref[...] = acc_a[...].astype(BF16)
            gwb_ref[...] = acc_b[...].astype(BF16)
            gwo_ref[...] = acc_o[...].astype(BF16)

    return pl.pallas_call(
        body, name="bwd_merge", grid=(n,),
        in_specs=[_row(tm, D), _row(tm, D), _row(tm, Q_W), _row(tm, GM_W), _row(tm, D), _row(tm, D),
                  _resident((Q_W, D)), _resident((GM_W, D)), _resident((D, D))],
        out_specs=[_row(tm, 2 * D), _row(tm, Q_W), _row(tm, GM_W), _full((Q_W, D)), _full((GM_W, D)), _full((D, D))],
        out_shape=[_sds((L, 2 * D), BF16), _sds((L, Q_W), BF16), _sds((L, GM_W), BF16),
                   _sds((Q_W, D), BF16), _sds((GM_W, D), BF16), _sds((D, D), BF16)],
        scratch_shapes=[pltpu.VMEM((Q_W, D), F32), pltpu.VMEM((GM_W, D), F32), pltpu.VMEM((D, D), F32)],
        compiler_params=_params(("arbitrary",)),
    )(dmix, merged, ya, yb, ga, gb, w_a, w_b, w_o)


def _k_gmlp_bwd(u, vb, dyb, lnv, ws, wst, bsp):
    L = u.shape[0]
    nch = min(GMLP_CHUNKS, L // BLK)
    tm = nch * BLK

    def body(u_ref, vb_ref, dyb_ref, lnv_ref, ws_ref, wst_ref, bsp_ref, du_ref, dvb_ref, gws_ref, gbst_ref, gln_ref):
        @pl.when(pl.program_id(0) == 0)
        def _():
            gws_ref[...] = jnp.zeros_like(gws_ref)
            gbst_ref[...] = jnp.zeros_like(gbst_ref)
            gln_ref[...] = jnp.zeros_like(gln_ref)

        uf, vf, gu, tu, tv, vhat, rstd, vn, s = _gmlp_fwd_vals(u_ref[...], vb_ref[...], lnv_ref, ws_ref, bsp_ref, nch)
        dyb_f = dyb_ref[...].astype(F32)
        du_ref[...] = (dyb_f * s * _gelu_grad(uf, tu)).astype(BF16)
        ds = dyb_f * gu
        ds_b = ds.astype(BF16)
        for pr in range(N_GROUPS // 2):
            lanes = slice(pr * 128, (pr + 1) * 128)
            gw_lo = gw_hi = ds_sum = None
            for c in range(nch):
                rows = slice(c * BLK, (c + 1) * BLK)
                lo, hi = _split_pair(ds_b[rows, lanes])
                t_lo = lax.dot_general(lo, vn[rows, lanes], NT, preferred_element_type=F32)
                t_hi = lax.dot_general(hi, vn[rows, lanes], NT, preferred_element_type=F32)
                gw_lo = t_lo if c == 0 else gw_lo + t_lo
                gw_hi = t_hi if c == 0 else gw_hi + t_hi
                ds_sum = ds[rows, lanes] if c == 0 else ds_sum + ds[rows, lanes]
            gws_ref[2 * pr] += gw_lo
            gws_ref[2 * pr + 1] += gw_hi
            b_lo, b_hi = _split_pair(ds_sum)
            gbst_ref[:, 2 * pr:2 * pr + 1] += jnp.sum(b_lo, axis=1, keepdims=True)
            gbst_ref[:, 2 * pr + 1:2 * pr + 2] += jnp.sum(b_hi, axis=1, keepdims=True)
        dvn = _gmlp_spatial(wst_ref, ds_b, nch)
        gln_ref[0:1, :] += _colsum(dvn * vhat)
        gln_ref[1:2, :] += _colsum(dvn)
        dgv = _ln_bwd(dvn * lnv_ref[0:1, :], vhat, rstd)
        dvb_ref[...] = (dgv * _gelu_grad(vf, tv)).astype(BF16)

    return pl.pallas_call(
        body, name="bwd_gmlp", grid=(L // tm,),
        in_specs=[_row(tm, GM_W)] * 3 + [_full((8, GM_W)), _full((N_GROUPS, BLK, BLK)), _full((N_GROUPS, BLK, BLK)),
                                         _full((BLK, GM_W))],
        out_specs=[_row(tm, GM_W), _row(tm, GM_W), _full((N_GROUPS, BLK, BLK)), _full((BLK, N_GROUPS)), _full((8, GM_W))],
        out_shape=[_sds((L, GM_W), BF16), _sds((L, GM_W), BF16), _sds((N_GROUPS, BLK, BLK), F32),
                   _sds((BLK, N_GROUPS), F32), _sds((8, GM_W), F32)],
        compiler_params=_params(("arbitrary",)),
    )(u, vb, dyb, lnv, ws, wst, bsp)


ATTN_BWD_BLOCKS = 2


def _k_attn_bwd(sink, q, k, v, kc, vc, dya, lse, cos, sin, bias, comm=None):
    L = q.shape[0]
    C = kc.shape[0]
    nb = L // BLK
    nq = min(ATTN_BWD_BLOCKS, nb)
    steps = nb // nq
    NK = C + 3 * BLK
    chains = [(qb, hk) for qb in range(nq) for hk in range(N_KV_HEADS)]

    def body(sink_ref, q_ref, kp_ref, km_ref, kx_ref, vp_ref, vm_ref, vx_ref, kc_ref, vc_ref, do_ref, lse_ref,
             cq_ref, sq_ref, cl_ref, sl_ref, bias_ref,
             dq_ref, dk_ref, dv_ref, dkc_ref, dvc_ref, dsink_ref,
             dq_scr, ck_scr, cv_scr, k1_acc, k2_acc, v1_acc, v2_acc):
        i = pl.program_id(0)

        @pl.when(i == 0)
        def _():
            for r in (k1_acc, k2_acc, v1_acc, v2_acc, dkc_ref, dvc_ref, dsink_ref):
                r[...] = jnp.zeros_like(r)

        @pl.when(i < steps)
        def _():
            def band(qb):
                first = jnp.where(i == 0, 0, 1) if qb == 0 else 1
                return bias_ref[jnp.where(i == steps - 1, 2, first) if qb == nq - 1 else first]

            def lanes(hk):
                return slice(hk * HEAD_DIM, (hk + 1) * HEAD_DIM)

            def keys(ctx_ref, p_ref, m_ref, x_ref, qb, hk):
                sl = lanes(hk)
                blocks = [p_ref[:, sl]] + [m_ref[j * BLK:(j + 1) * BLK, sl] for j in range(nq)] + [x_ref[:, sl]]
                return jnp.concatenate([ctx_ref[:, sl]] + blocks[qb:qb + 3], axis=0)

            def stacked(ref, qb, hk, width):
                return jnp.concatenate(
                    [ref[qb * BLK:(qb + 1) * BLK, (hk * GQA_GROUP + g) * width:(hk * GQA_GROUP + g + 1) * width]
                     for g in range(GQA_GROUP)], axis=0)

            def scores(qb, hk):
                kcat = keys(kc_ref, kp_ref, km_ref, kx_ref, qb, hk)
                qg = stacked(q_ref, qb, hk, HEAD_DIM)
                s = _masked(lax.dot_general(qg, kcat, NT, preferred_element_type=F32), band(qb), C)
                dog = stacked(do_ref, qb, hk, HEAD_DIM)
                dp = lax.dot_general(dog, keys(vc_ref, vp_ref, vm_ref, vx_ref, qb, hk), NT, preferred_element_type=F32)
                return kcat, qg, dog, s, dp

            def softmax_bwd(qb, hk, s, dp):
                lse_c = stacked(lse_ref, qb, hk, 1)
                p = jnp.exp(s - lse_c)
                delta = jnp.sum(p * dp, axis=1, keepdims=True)
                ds = (p * (dp - delta)).astype(BF16)
                p_sink = jnp.exp(_sink_col(sink_ref, hk) - lse_c) * delta
                return p.astype(BF16), ds, p_sink

            def put_dq(qb, hk, dqs, p_sink):
                for g in range(GQA_GROUP):
                    h = hk * GQA_GROUP + g
                    dq_scr[qb * BLK:(qb + 1) * BLK, h * HEAD_DIM:(h + 1) * HEAD_DIM] = dqs[g * BLK:(g + 1) * BLK, :]
                    tot = jnp.sum(p_sink[g * BLK:(g + 1) * BLK, :], axis=0, keepdims=True)
                    dsink_ref[h:h + 1, :] -= jnp.broadcast_to(tot, (1, 128))

            ahead = 4
            sc = [scores(*c) for c in chains[:ahead]]
            pending = None
            for n, (qb, hk) in enumerate(chains):
                if n + ahead < len(chains):
                    sc.append(scores(*chains[n + ahead]))
                kcat, qg, dog, s, dp = sc[n]
                pb, ds, p_sink = softmax_bwd(qb, hk, s, dp)
                if pending is not None:
                    pqb, phk, pds, ppb, pqg, pdog = pending
                    ck_scr[pqb, :, lanes(phk)] = lax.dot_general(pds, pqg, TN, preferred_element_type=F32)
                    cv_scr[pqb, :, lanes(phk)] = lax.dot_general(ppb, pdog, TN, preferred_element_type=F32)
                put_dq(qb, hk, jnp.dot(ds, kcat, preferred_element_type=F32), p_sink)
                pending = (qb, hk, ds, pb, qg, dog)
            pqb, phk, pds, ppb, pqg, pdog = pending
            ck_scr[pqb, :, lanes(phk)] = lax.dot_general(pds, pqg, TN, preferred_element_type=F32)
            cq, sq = cq_ref[...], sq_ref[...]
            for j in range(4):
                dq_ref[:, j * 128:(j + 1) * 128] = _unrope(dq_scr[:, j * 128:(j + 1) * 128] * Q_SCALE, cq, sq).astype(BF16)
            cv_scr[pqb, :, lanes(phk)] = lax.dot_general(ppb, pdog, TN, preferred_element_type=F32)
            dkc_ref[...] += functools.reduce(lambda a, b: a + b, [ck_scr[qb, 0:C, :] for qb in range(nq)])
            dvc_ref[...] += functools.reduce(lambda a, b: a + b, [cv_scr[qb, 0:C, :] for qb in range(nq)])

        @pl.when(i >= steps)
        def _():
            ck_scr[...] = jnp.zeros_like(ck_scr)
            cv_scr[...] = jnp.zeros_like(cv_scr)

        def slot(scr, r, carried):
            parts = [scr[qb, C + (r - qb) * BLK:C + (r - qb + 1) * BLK, :] for qb in range(nq) if 0 <= r - qb <= 2]
            total = functools.reduce(lambda a, b: a + b, parts)
            return total if carried is None else carried[...] + total

        for r in range(nq):
            rows = slice(r * BLK, (r + 1) * BLK)
            carried_k, carried_v = ((k1_acc, v1_acc), (k2_acc, v2_acc), (None, None))[min(r, 2)]
            tables = (cl_ref[...], sl_ref[...]) if r == 0 else (cq_ref[(r - 1) * BLK:r * BLK, :], sq_ref[(r - 1) * BLK:r * BLK, :])
            dk_ref[rows, :] = _unrope(slot(ck_scr, r, carried_k), *tables).astype(BF16)
            dv_ref[rows, :] = slot(cv_scr, r, carried_v).astype(BF16)
        k1_acc[...] = slot(ck_scr, nq, None)
        v1_acc[...] = slot(cv_scr, nq, None)
        k2_acc[...] = slot(ck_scr, nq + 1, None)
        v2_acc[...] = slot(cv_scr, nq + 1, None)

    last = steps - 1
    kv3 = [pl.BlockSpec((BLK, KV_W), lambda i: (jnp.clip(nq * i - 1, 0, nb - 1), 0)),
           pl.BlockSpec((nq * BLK, KV_W), lambda i: (jnp.minimum(i, last), 0)),
           pl.BlockSpec((BLK, KV_W), lambda i: (jnp.minimum(nq * i + nq, nb - 1), 0))]
    cur = lambda w: pl.BlockSpec((nq * BLK, w), lambda i: (jnp.minimum(i, last), 0))
    late = lambda w: pl.BlockSpec((BLK, w), lambda i: (jnp.clip(nq * i - 1, 0, nb - 1), 0))
    out2 = lambda w: pl.BlockSpec((nq * BLK, w), lambda i: (i, 0))
    return _call(
        body, name="bwd_attn", grid=(steps + 1,),
        in_specs=[pl.BlockSpec(memory_space=pltpu.SMEM), cur(Q_W)] + kv3 + kv3
                 + [_full((C, KV_W)), _full((C, KV_W)), cur(Q_W), cur(N_Q_HEADS), cur(128), cur(128), late(128), late(128),
                    _full((3, GQA_GROUP * BLK, 3 * BLK))],
        out_specs=[cur(Q_W), out2(KV_W), out2(KV_W), _full((C, KV_W)), _full((C, KV_W)), _full((8, 128))],
        out_shape=[_sds((L, Q_W), BF16), _sds((L + nq * BLK, KV_W), BF16), _sds((L + nq * BLK, KV_W), BF16),
                   _sds((C, KV_W), F32), _sds((C, KV_W), F32), _sds((8, 128), F32)],
        scratch=[pltpu.VMEM((nq * BLK, Q_W), F32), pltpu.VMEM((nq, NK, KV_W), F32), pltpu.VMEM((nq, NK, KV_W), F32)]
                + [pltpu.VMEM((BLK, KV_W), F32)] * 4,
        args=(sink, q, k, k, k, v, v, v, kc, vc, dya, lse, cos, sin, cos, sin, bias), comm=comm)


def _k_ctx_bwd(ctx, modc, hc, dkc, dvc, w_kv):
    C = ctx.shape[0]

    def body(c_ref, mod_ref, hc_ref, dkc_ref, dvc_ref, w_ref, gw_ref, dmod_ref):
        dkv = jnp.concatenate([dkc_ref[...], dvc_ref[...]], axis=1).astype(BF16)
        gw_ref[...] = lax.dot_general(dkv, hc_ref[...], TN, preferred_element_type=F32)
        dhc = jnp.dot(dkv, w_ref[...], preferred_element_type=F32)
        n, _ = _ln(c_ref[...])
        dmod_ref[...] = jnp.zeros_like(dmod_ref)
        dmod_ref[0:1, :] = _colsum(dhc)
        dmod_ref[1:2, :] = _colsum(dhc * n)

    return pl.pallas_call(
        body, name="bwd_ctx", grid=(1,),
        in_specs=[_full((C, D)), _full((8, D)), _full((C, D)), _full((C, KV_W)), _full((C, KV_W)), _full((2 * KV_W, D))],
        out_specs=[_full((2 * KV_W, D)), _full((8, D))],
        out_shape=[_sds((2 * KV_W, D), F32), _sds((8, D), F32)],
        compiler_params=_params(("arbitrary",)),
    )(ctx, modc, hc, dkc, dvc, w_kv)


def _k_in_bwd(dq, dk, dv, du, dvb, dg, x, dxp, w_in, modv, tm):
    L = x.shape[0]
    parts = [(O_Q, Q_W), (O_K, KV_W), (O_V, KV_W), (O_U, GM_W), (O_VB, GM_W)]

    def body(dq_ref, dk_ref, dv_ref, du_ref, dvb_ref, dg_ref, x_ref, dxp_ref, w_ref, mod_ref,
             dP_ref, gx_ref, acc_ref):
        @pl.when(pl.program_id(0) == 0)
        def _():
            acc_ref[...] = jnp.zeros_like(acc_ref)

        for (lo, width), r in zip(parts, (dq_ref, dk_ref, dv_ref, du_ref, dvb_ref)):
            dP_ref[:, lo:lo + width] = r[...]
        n1, rstd1 = _ln(x_ref[...])
        dh = (jnp.dot(dP_ref[...], w_ref[0:O_GA, :], preferred_element_type=F32)
              + jnp.dot(dg_ref[...], w_ref[O_GA:IN_W, :], preferred_element_type=F32))
        acc_ref[0:1, :] += _colsum(dh)
        acc_ref[1:2, :] += _colsum(dh * n1)
        gx_ref[...] = dxp_ref[...] + _ln_bwd(dh * (1.0 + mod_ref[1:2, :]), n1, rstd1)

    return pl.pallas_call(
        body, name="bwd_in", grid=(L // tm,),
        in_specs=[_row(tm, w) for _, w in parts] + [_row(tm, 2 * D), _row(tm, D), _row(tm, D), _resident((IN_W, D)),
                                                    _full((8, D))],
        out_specs=[_row(tm, O_GA), _row(tm, D), _full((8, D))],
        out_shape=[_sds((L, O_GA), BF16), _sds((L, D), F32), _sds((8, D), F32)],
        compiler_params=_params(("arbitrary",)),
    )(dq, dk, dv, du, dvb, dg, x, dxp, w_in, modv)


def _wgrad(a, b, name, tk, tt, comm=None, extra=None):
    T, K = a.shape
    N = b.shape[1]
    nt = T // tt

    def body(*refs):
        a_ref, b_ref = refs[:2]
        o_ref, acc_ref = refs[-2:]
        j, t = pl.program_id(0), pl.program_id(1)

        @pl.when(t == 0)
        def _():
            acc_ref[...] = jnp.zeros_like(acc_ref)

        acc_ref[...] += lax.dot_general(a_ref[...], b_ref[...], TN, preferred_element_type=F32)

        if extra is not None:
            lo, rows = extra[0] % tk, extra[1].shape[0]

            @pl.when((t == nt - 1) & (j == extra[0] // tk))
            def _():
                acc_ref[lo:lo + rows, :] += refs[2][...]

        @pl.when(t == nt - 1)
        def _():
            o_ref[...] = acc_ref[...].astype(BF16)

    extra_specs = [] if extra is None else [pl.BlockSpec(extra[1].shape, lambda j, t: (0, 0))]
    (out,), got = _call(
        body, name=name, grid=(K // tk, nt),
        in_specs=[pl.BlockSpec((tt, tk), lambda j, t: (t, j)), pl.BlockSpec((tt, N), lambda j, t: (t, 0))] + extra_specs,
        out_specs=[pl.BlockSpec((tk, N), lambda j, t: (j, 0))],
        out_shape=[_sds((K, N), BF16)],
        scratch=[pltpu.VMEM((tk, N), F32)],
        args=(a, b) + (() if extra is None else (extra[1],)), comm=comm)
    return (out, got) if comm is not None else out


def _adamw_reduce(parts, w, m, v, name, tr):
    R, C = w.shape
    n_parts = parts.shape[0]

    def body(p_ref, w_ref, m_ref, v_ref, g_ref, d_ref, m2_ref, v2_ref):
        g = p_ref[0].astype(F32)
        for i in range(1, n_parts):
            g = g + p_ref[i].astype(F32)
        delta, m2, v2 = _adamw(w_ref[...], g, m_ref[...], v_ref[...])
        g_ref[...] = g
        d_ref[...] = delta
        m2_ref[...] = m2
        v2_ref[...] = v2

    spec = _row(tr, C)
    return pl.pallas_call(
        body, name=name, grid=(R // tr,),
        in_specs=[pl.BlockSpec((n_parts, tr, C), lambda i: (0, i, 0)), spec, spec, spec],
        out_specs=[spec] * 4,
        out_shape=[_sds((R, C), F32)] * 4,
        compiler_params=_params(("arbitrary",)),
    )(parts, w, m, v)


SMALL_ORDER = ("b_ada", "ln1_g", "ln1_b", "ln2_g", "ln2_b", "gmlp_ln_g", "gmlp_ln_b", "b_spatial", "attn_sink")


def _small_step(gath, params):
    flat = [a for name in SMALL_ORDER for a in params[name]]

    def grad_of(tot, name):
        if name == "b_ada":
            return jnp.concatenate([tot[r:r + 1, :] for r in range(6)], axis=1)
        if name in ("ln1_g", "ln1_b", "ln2_g", "ln2_b"):
            r = 8 + ("ln1_g", "ln1_b", "ln2_g", "ln2_b").index(name)
            return tot[r:r + 1, :]
        if name == "gmlp_ln_g":
            return tot[12:13, :GM_W]
        if name == "gmlp_ln_b":
            return tot[12:13, GM_W:]
        if name == "b_spatial":
            return jnp.concatenate([tot[13:14, g * BLK:(g + 1) * BLK] for g in range(N_GROUPS)], axis=0)[None]
        return tot[14:15, :N_Q_HEADS]

    def body(*refs):
        g_ref, in_refs = refs[0], refs[1:1 + len(flat)]
        tot_ref, out_refs = refs[1 + len(flat)], refs[2 + len(flat):]
        tot = g_ref[0]
        for i in range(1, N_DEV):
            tot = tot + g_ref[i]
        tot_ref[...] = tot
        tot_ref[0:2, :] = tot[0:2, :] + tot[6:8, :]
        tot_ref[15:16, :] = jnp.broadcast_to(jnp.sum(tot[15:16, :], axis=1, keepdims=True), (1, D))
        tot = tot_ref[...]
        for k, name in enumerate(SMALL_ORDER):
            w_ref, m_ref, v_ref = in_refs[3 * k:3 * k + 3]
            g = grad_of(tot, name)
            delta, m2, v2 = _adamw(w_ref[...], g, m_ref[...], v_ref[...])
            for r, val in zip(out_refs[4 * k:4 * k + 4], (g, delta, m2, v2)):
                r[...] = val

    res = pl.pallas_call(
        body, name="small_step", grid=(1,),
        in_specs=[_full((N_DEV, 16, D))] + [_full(a.shape) for a in flat],
        out_specs=[_full((16, D))] + [_full(params[name][0].shape) for name in SMALL_ORDER for _ in range(4)],
        out_shape=[_sds((16, D), F32)] + [_sds(params[name][0].shape, F32) for name in SMALL_ORDER for _ in range(4)],
        compiler_params=_params(("arbitrary",)),
    )(gath, *flat)
    return res[0], {name: res[1 + 4 * k:5 + 4 * k] for k, name in enumerate(SMALL_ORDER)}


def _cctx_finish(gath, c_ctx, m, v):
    def body(g_ref, c_ref, m_ref, v_ref, gr_ref, d_ref, m2_ref, v2_ref):
        ds = g_ref[0]
        for i in range(1, N_DEV):
            ds = ds + g_ref[i]
        c = c_ref[...]
        sg = _sigmoid(c)
        g = ds * (sg * (1.0 + c * (1.0 - sg)))
        delta, m2, v2 = _adamw(c, g, m_ref[...], v_ref[...])
        gr_ref[...] = g
        d_ref[...] = delta
        m2_ref[...] = m2
        v2_ref[...] = v2

    return pl.pallas_call(
        body, name="cctx_finish", grid=(1,),
        in_specs=[_full((N_DEV, 8, D))] + [_full((8, D))] * 3, out_specs=[_full((8, D))] * 4,
        out_shape=[_sds((8, D), F32)] * 4,
        compiler_params=_params(("arbitrary",)),
    )(gath, c_ctx, m, v)


def _pad_rows(a, rows):
    return jnp.concatenate([a, jnp.zeros((rows - a.shape[0], a.shape[1]), a.dtype)], axis=0)


def kernel(x, c, ctx, c_ctx, w_ada, b_ada, w_in, attn_sink, gmlp_ln_g, gmlp_ln_b, w_spatial, b_spatial, w_branch_a, w_branch_b, w_out, ln1_g, ln1_b, w_ffn_in, w_ffn_out, ln2_g, ln2_b, loss_target, m_c_ctx, m_w_ada, m_b_ada, m_w_in, m_attn_sink, m_gmlp_ln_g, m_gmlp_ln_b, m_w_spatial, m_b_spatial, m_w_branch_a, m_w_branch_b, m_w_out, m_ln1_g, m_ln1_b, m_w_ffn_in, m_w_ffn_out, m_ln2_g, m_ln2_b, v_c_ctx, v_w_ada, v_b_ada, v_w_in, v_attn_sink, v_gmlp_ln_g, v_gmlp_ln_b, v_w_spatial, v_b_spatial, v_w_branch_a, v_w_branch_b, v_w_out, v_ln1_g, v_ln1_b, v_w_ffn_in, v_w_ffn_out, v_ln2_g, v_ln2_b):
    L = x.shape[1]
    me = 4 * lax.axis_index("x") + 2 * lax.axis_index("y") + lax.axis_index("c")
    x2, tgt, ctx2 = x[0], loss_target[0], ctx[0]
    tiles = _Tiles(L)
    tm_in, tm, tt = tiles.wide, tiles.narrow, tiles.tokens

    transposed = ("w_in", "w_ffn_in")
    tr = lambda kname, a: a.T if kname in transposed else a
    big = dict(w_in=w_in[0].T, w_branch_a=w_branch_a[0], w_branch_b=w_branch_b[0], w_out=w_out[0],
               w_ffn_in=w_ffn_in[0].T, w_ffn_out=w_ffn_out[0])
    col_sharded = ("w_branch_a", "w_branch_b")
    shard_bf = {k: a.astype(BF16) for k, a in big.items()}

    def assemble(kname, g):
        if kname in col_sharded:
            return g.transpose(1, 0, 2).reshape(g.shape[1], N_DEV * g.shape[2])
        return g.reshape(N_DEV * g.shape[1], g.shape[2])

    def to_blocks(kname, g):
        if kname in col_sharded:
            return g.reshape(g.shape[0], N_DEV, g.shape[1] // N_DEV).transpose(1, 0, 2)
        return g.reshape(N_DEV, g.shape[0] // N_DEV, g.shape[1])

    full = {}
    n_ada = w_ada.shape[2]
    b_my = lax.dynamic_slice(b_ada, (0, me * n_ada), (1, n_ada))
    act, mod_all, got = _prologue(_pad_rows(c, 8), _pad_rows(c_ctx[None, :], 8), w_ada[0], b_my,
                                  _Comm(gather=[shard_bf["w_in"]]))
    full["w_in"] = assemble("w_in", got[0])
    mod_all = mod_all.transpose(1, 0, 2).reshape(16, 6 * D)
    modv = _pad_rows(lax.dynamic_slice(mod_all, (me, 0), (1, 6 * D)).reshape(6, D), 8)
    modc = _pad_rows(mod_all[8].reshape(6, D), 8)

    lnv = _pad_rows(jnp.concatenate([ln1_g, ln1_b, ln2_g, ln2_b], axis=0), 8)
    gm_lnv = _pad_rows(jnp.concatenate([gmlp_ln_g, gmlp_ln_b], axis=0), 8)
    ws_b = w_spatial[0].astype(BF16)
    wst_b = ws_b.transpose(0, 2, 1)
    bsp = jnp.repeat(b_spatial[0].T, GROUP_DIM, axis=1)
    sink = attn_sink[0]
    cos, sin = _rope_tables(L)
    bias = _attn_bias()
    w_kv = full["w_in"][O_K:O_K + 2 * KV_W, :]

    (h, q, k, v, u, vb, ga, gb), got = _k_in(
        x2, modv, full["w_in"], cos, sin, tm_in,
        comm=_Comm(gather=[shard_bf[kname] for kname in ("w_branch_a", "w_branch_b", "w_out", "w_ffn_out")]))
    for kname, g in zip(("w_branch_a", "w_branch_b", "w_out", "w_ffn_out"), got):
        full[kname] = assemble(kname, g)
    hc, kc, vc = _k_ctx(ctx2, modc, w_kv)
    (ya, lse), got = _k_attn(sink, q, k, v, kc, vc, bias, comm=_Comm(gather=[shard_bf["w_ffn_in"]]))
    full["w_ffn_in"] = assemble("w_ffn_in", got[0])
    yb = _k_gmlp(u, vb, gm_lnv, ws_b, bsp)
    merged, mix, xm, h2 = _k_merge(x2, ya, yb, ga, gb, full["w_branch_a"], full["w_branch_b"], full["w_out"], modv, lnv, tm_in)
    gate, up, act_f, dr2, df, acc_f = _k_ffn(h2, xm, tgt, full["w_ffn_in"], full["w_ffn_out"], modv, lnv, tm_in)

    dF, dmix, dxp, acc_b = _k_ffn_bwd(df, gate, up, xm, dr2, x2, mix, full["w_ffn_in"], full["w_ffn_out"], modv, lnv, tm)
    blk_fo = to_blocks("w_ffn_out", _wgrad(act_f, df, "wgrad_ffn_out", tiles.tk_ffn, tt))
    gw_fi, (rcv_fo,) = _wgrad(dF, h2, "wgrad_ffn_in", tiles.tk_ffn, tt, comm=_Comm(scatter=[blk_fo]))
    blk_fi = to_blocks("w_ffn_in", gw_fi)
    dg, dya, dyb, gw_a, gw_b, gw_o = _k_merge_bwd(
        dmix, merged, ya, yb, ga, gb, full["w_branch_a"], full["w_branch_b"], full["w_out"], tm_in)
    blk_a, blk_b, blk_o = to_blocks("w_branch_a", gw_a), to_blocks("w_branch_b", gw_b), to_blocks("w_out", gw_o)
    gw_in_gates, (rcv_a, rcv_b, rcv_o) = _wgrad(dg, h, "wgrad_in_gates", D, tt, comm=_Comm(scatter=[blk_a, blk_b, blk_o]))
    du, dvb, g_ws, g_bst, g_gln = _k_gmlp_bwd(u, vb, dyb, gm_lnv, ws_b, wst_b, bsp)
    (dq, dk_late, dv_late, dkc, dvc, g_sink), (gath_ws, rcv_fi) = _k_attn_bwd(
        sink, q, k, v, kc, vc, dya, lse, cos, sin, bias,
        comm=_Comm(gather=[g_ws.reshape(N_GROUPS * BLK, BLK)], scatter=[blk_fi]))
    dk, dv = dk_late[BLK:BLK + L], dv_late[BLK:BLK + L]
    dP, grad_x, acc_i = _k_in_bwd(dq, dk, dv, du, dvb, dg, x2, dxp, full["w_in"], modv, tm_in)
    g_ctx, dmodc = _k_ctx_bwd(ctx2, modc, hc, dkc, dvc, w_kv)
    gw_in = jnp.concatenate([_wgrad(dP, h, "wgrad_in", O_GA // 2, tt, extra=(O_K, g_ctx)), gw_in_gates], axis=0)

    dmod_x = jnp.concatenate([acc_i[0:2], acc_b[4:5], acc_b[0:2], acc_f[2:3]], axis=0)
    small = jnp.concatenate([
        dmod_x, dmodc[0:2], acc_b[2:4], acc_f[0:2],
        jnp.concatenate([g_gln[0:1], g_gln[1:2]], axis=1), g_bst.T.reshape(1, D),
        _pad_rows(g_sink[:, 0:1], D).T, acc_f[3:4]], axis=0)
    rcv_in, gath = _exchange_two_level(to_blocks("w_in", gw_in), small, "exchange_last")
    received = dict(w_in=rcv_in, w_branch_a=rcv_a, w_branch_b=rcv_b, w_out=rcv_o, w_ffn_in=rcv_fi, w_ffn_out=rcv_fo)
    moments = dict(w_in=(m_w_in, v_w_in), w_branch_a=(m_w_branch_a, v_w_branch_a), w_branch_b=(m_w_branch_b, v_w_branch_b),
                   w_out=(m_w_out, v_w_out), w_ffn_in=(m_w_ffn_in, v_w_ffn_in), w_ffn_out=(m_w_ffn_out, v_w_ffn_out))
    names = list(big)
    res = {}
    for kname in names:
        mm, vv = moments[kname]
        R = big[kname].shape[0]
        res[kname] = [tr(kname, r) for r in _adamw_reduce(
            received[kname], big[kname], tr(kname, mm[0]), tr(kname, vv[0]), "adamw_" + kname, 256 if R % 256 == 0 else R // 2)]

    ws2d = lambda a: a.reshape(N_GROUPS * BLK, BLK)
    res_ws = [r.reshape(w_spatial.shape) for r in _adamw_reduce(
        gath_ws, ws2d(w_spatial), ws2d(m_w_spatial), ws2d(v_w_spatial), "adamw_w_spatial", 256)]
    tot, res_small = _small_step(gath, dict(
        b_ada=(b_ada, m_b_ada, v_b_ada), ln1_g=(ln1_g, m_ln1_g, v_ln1_g), ln1_b=(ln1_b, m_ln1_b, v_ln1_b),
        ln2_g=(ln2_g, m_ln2_g, v_ln2_g), ln2_b=(ln2_b, m_ln2_b, v_ln2_b),
        gmlp_ln_g=(gmlp_ln_g, m_gmlp_ln_g, v_gmlp_ln_g), gmlp_ln_b=(gmlp_ln_b, m_gmlp_ln_b, v_gmlp_ln_b),
        b_spatial=(b_spatial, m_b_spatial, v_b_spatial), attn_sink=(attn_sink, m_attn_sink, v_attn_sink)))
    loss = tot[15, 0]

    dmod_rows = jnp.concatenate([gath[:, 0:6, :].reshape(N_DEV, 6 * D),
                                 jnp.concatenate([tot[6:8].reshape(1, 2 * D), jnp.zeros((1, 4 * D), F32)], axis=1),
                                 jnp.zeros((7, 6 * D), F32)], axis=0)
    dmod_my = lax.dynamic_slice(dmod_rows, (0, me * n_ada), (16, n_ada))
    g_wada, d_wada, m2_wada, v2_wada, pc = _ada_bwd(act, dmod_my, w_ada[0], m_w_ada[0], v_w_ada[0])
    pc_all = _gather_rows(pc, "gather_cctx")
    cc8 = lambda a: _pad_rows(a.reshape(1, D), 8)
    g_cc, d_cc, m2_cc, v2_cc = _cctx_finish(pc_all, cc8(c_ctx), cc8(m_c_ctx), cc8(v_c_ctx))

    order = ["c_ctx", "w_ada", "b_ada", "w_in", "attn_sink", "gmlp_ln_g", "gmlp_ln_b", "w_spatial", "b_spatial",
             "w_branch_a", "w_branch_b", "w_out", "ln1_g", "ln1_b", "w_ffn_in", "w_ffn_out", "ln2_g", "ln2_b"]
    grads, deltas, new_m, new_v = {}, {}, {}, {}
    grads["c_ctx"], deltas["c_ctx"], new_m["c_ctx"], new_v["c_ctx"] = g_cc[0], d_cc[0], m2_cc[0], v2_cc[0]
    grads["w_ada"], deltas["w_ada"], new_m["w_ada"], new_v["w_ada"] = g_wada[None], d_wada[None], m2_wada[None], v2_wada[None]
    for kname in names:
        g, d, m2, v2 = res[kname]
        grads[kname], deltas[kname], new_m[kname], new_v[kname] = g[None], d[None], m2[None], v2[None]
    grads["w_spatial"], deltas["w_spatial"], new_m["w_spatial"], new_v["w_spatial"] = res_ws
    for kname in SMALL_ORDER:
        grads[kname], deltas[kname], new_m[kname], new_v[kname] = res_small[kname]
    return (loss, grad_x[None], *[grads[n] for n in order], *[deltas[n] for n in order],
            *[new_m[n] for n in order], *[new_v[n] for n in order])
```

```python
import functools
import math

import jax
import jax.numpy as jnp
import numpy as np
from jax import lax
from jax.experimental import pallas as pl
from jax.experimental.pallas import tpu as pltpu

F32 = jnp.float32
BF16 = jnp.bfloat16
MESH = pl.DeviceIdType.MESH

N_DEV = 8
D = 1024
HEAD_DIM = 64
N_Q_HEADS = 8
N_KV_HEADS = 2
GQA_GROUP = 4
BLK = 128
Q_W = 512
KV_W = 128
GM_W = 512
N_GROUPS = 8
GROUP_DIM = 64
FFN_H = 2816
IN_W = 3840
O_Q, O_K, O_V, O_U, O_VB, O_GA, O_GB = 0, 512, 640, 768, 1280, 1792, 2816
LN_EPS = 1e-5
NEG_INF = -1e30
ALPHA = 2.0 ** 0.25
ROPE_BASE = 10000.0
ROPE_PAIRS = 16
Q_SCALE = HEAD_DIM ** -0.5
GELU_K0 = math.sqrt(2.0 / math.pi)
GELU_K1 = 0.044715

ADAM_LR = 0.001
ADAM_B1 = 0.9
ADAM_B2 = 0.999
ADAM_EPS = 1e-08
ADAM_WD = 0.01
ADAM_STEP = 10

V7X_VMEM_BYTES = 64 * 1024 * 1024
VMEM_LIMIT = V7X_VMEM_BYTES * 7 // 8
NT = (((1,), (1,)), ((), ()))
TN = (((0,), (0,)), ((), ()))


class _Tiles:
    def __init__(self, L):
        self.wide = min(512, L)
        self.narrow = min(256, L)
        self.tokens = min(2048, L)
        self.tk_in = IN_W // 3
        self.tk_ffn = FFN_H // 2


def _params(sem=None):
    return pltpu.CompilerParams(dimension_semantics=sem, vmem_limit_bytes=VMEM_LIMIT)


def _row(tm, w):
    return pl.BlockSpec((tm, w), lambda i: (i, 0))


def _full(shape):
    nd = len(shape)
    return pl.BlockSpec(shape, lambda i: (0,) * nd)


def _resident(shape):
    nd = len(shape)
    return pl.BlockSpec(shape, lambda i: (0,) * nd, pipeline_mode=pl.Buffered(1))


def _sds(shape, dt):
    return jax.ShapeDtypeStruct(shape, dt)


def _ln(xf):
    mu = jnp.mean(xf, axis=-1, keepdims=True)
    xc = xf - mu
    var = jnp.mean(xc * xc, axis=-1, keepdims=True)
    rstd = lax.rsqrt(var + LN_EPS)
    return xc * rstd, rstd


def _ln_bwd(dn, n, rstd):
    m1 = jnp.mean(dn, axis=-1, keepdims=True)
    m2 = jnp.mean(dn * n, axis=-1, keepdims=True)
    return rstd * (dn - m1 - n * m2)


def _colsum(t):
    return jnp.sum(t, axis=0, keepdims=True)


def _sigmoid(x):
    return 0.5 * jnp.tanh(0.5 * x) + 0.5


def _gelu(x):
    t = jnp.tanh(x * (GELU_K0 + (GELU_K0 * GELU_K1) * (x * x)))
    h = 0.5 * x
    return h + h * t, t


def _gelu_grad(x, t):
    return 0.5 + 0.5 * t + (0.5 * x) * (1.0 - t * t) * (GELU_K0 + (3.0 * GELU_K0 * GELU_K1) * (x * x))


def _swap16(t):
    lane = lax.broadcasted_iota(jnp.int32, t.shape, 1)
    return jnp.where((lane & 16) == 0, pltpu.roll(t, 112, 1), pltpu.roll(t, 16, 1))


def _rope(t, cos, sin):
    return t * cos + _swap16(t) * sin


def _unrope(t, cos, sin):
    return t * cos - _swap16(t) * sin


def _adamw(w, g, m, v):
    m2 = ADAM_B1 * m + (1.0 - ADAM_B1) * g
    v2 = ADAM_B2 * v + (1.0 - ADAM_B2) * (g * g)
    m_hat = m2 / (1.0 - ADAM_B1 ** ADAM_STEP)
    v_hat = v2 / (1.0 - ADAM_B2 ** ADAM_STEP)
    delta = -ADAM_LR * (m_hat / (jnp.sqrt(v_hat) + ADAM_EPS) + ADAM_WD * w)
    return delta, m2, v2


def _rope_tables(L):
    inv = (np.float32(ROPE_BASE) ** (-np.arange(ROPE_PAIRS, dtype=np.float32) / np.float32(ROPE_PAIRS))).astype(np.float32)
    t = np.arange(L, dtype=np.int32)
    rows = (t // 64).astype(np.float32)[:, None] * inv
    cols = (t % 64).astype(np.float32)[:, None] * inv
    cr, sr, cc, sc = np.cos(rows), np.sin(rows), np.cos(cols), np.sin(cols)
    cos = np.concatenate([cr, cr, cc, cc], axis=1)
    sin = np.concatenate([-sr, sr, -sc, sc], axis=1)
    return jnp.asarray(np.tile(cos, (1, 2)), F32), jnp.asarray(np.tile(sin, (1, 2)), F32)


def _me():
    return lax.axis_index("x"), lax.axis_index("y"), lax.axis_index("c")


def _peer(mx, my, mc, k):
    return (mx ^ ((k >> 2) & 1), my ^ ((k >> 1) & 1), mc ^ (k & 1))


class _Comm:
    def __init__(self, gather=(), scatter=(), spread=()):
        self.kinds = ["gather"] * len(gather) + ["scatter"] * len(scatter) + ["spread"] * len(spread)
        self.args = list(gather) + list(scatter) + list(spread)
        self.n = len(self.args)

    def out_shape(self):
        return [_sds(a.shape if k == "scatter" else (N_DEV,) + a.shape, a.dtype) for k, a in zip(self.kinds, self.args)]

    def specs(self):
        return [pl.BlockSpec(memory_space=pl.ANY)] * self.n

    def scratch(self):
        return [pltpu.SemaphoreType.DMA((7 * self.n,)), pltpu.SemaphoreType.DMA((7 * self.n,)),
                pltpu.SemaphoreType.DMA((self.n,))]

    def _plan(self, x_refs, out_refs, send_sems, recv_sems, local_sems):
        mx, my, mc = _me()
        me = 4 * mx + 2 * my + mc
        here, sibling = (mx, my, mc), (mx, my, 1 - mc)
        chips = [(1 - mx, my), (mx, 1 - my), (1 - mx, 1 - my)]
        local, first, last = [], [], []
        relay = [[], [], []]
        for a, kind in enumerate(self.kinds):
            x, out = x_refs[a], out_refs[a]

            def rc(k, src, dst, to):
                return pltpu.make_async_remote_copy(
                    src_ref=src, dst_ref=dst, send_sem=send_sems.at[7 * a + k], recv_sem=recv_sems.at[7 * a + k],
                    device_id=to, device_id_type=MESH)

            if kind == "gather":
                local.append(pltpu.make_async_copy(x, out.at[me], local_sems.at[a]))
                first.append(rc(0, x, out.at[me], sibling))
                last.append(rc(0, x, out.at[me ^ 1], here))
                for j, (cx, cy) in enumerate(chips):
                    first.append(rc(1 + j, x, out.at[me], (cx, cy, mc)))
                    landed = out.at[4 * cx + 2 * cy + mc]
                    relay[j].append((rc(1 + j, x, landed, here), rc(4 + j, landed, landed, sibling)))
                    last.append(rc(4 + j, x, out.at[4 * cx + 2 * cy + 1 - mc], here))
            else:
                own = x.at[me] if kind == "scatter" else x
                local.append(pltpu.make_async_copy(own, out.at[me], local_sems.at[a]))
                for k in range(1, N_DEV):
                    src = x.at[me ^ k] if kind == "scatter" else x
                    first.append(rc(k - 1, src, out.at[me], _peer(mx, my, mc, k)))
                    last.append(rc(k - 1, own, out.at[me ^ k], here))
        return local, first, relay[0] + relay[1] + relay[2], last

    def start(self, *refs):
        local, first, _, _ = self._plan(*refs)
        for cp in local + first:
            cp.start()

    def relay(self, *refs):
        _, _, relay, _ = self._plan(*refs)
        for arrival, onward in relay:
            arrival.wait_recv()
            onward.start()

    def finish(self, *refs):
        local, first, relay, last = self._plan(*refs)
        for cp in last:
            cp.wait_recv()
        for cp in first:
            cp.wait_send()
        for _, onward in relay:
            onward.wait_send()
        for cp in local:
            cp.wait()


def _call(body, *, name, grid, in_specs, out_specs, out_shape, args, scratch=(), comm=None, aliases=None):
    params = _params(("arbitrary",) * len(grid))
    total = math.prod(grid)

    def at(step):
        flat = functools.reduce(lambda acc, dn: acc * dn[1] + pl.program_id(dn[0]), enumerate(grid), 0)
        return flat == step

    if comm is None:
        res = pl.pallas_call(
            body, name=name, grid=grid, in_specs=list(in_specs), out_specs=list(out_specs), out_shape=list(out_shape),
            scratch_shapes=list(scratch), input_output_aliases=aliases or {}, compiler_params=params)(*args)
        return list(res), []
    n_in, n_out, n_scr, cn = len(in_specs), len(out_specs), len(scratch), comm.n

    def hosted(*refs):
        ins, refs = refs[:n_in], refs[n_in:]
        cins, refs = refs[:cn], refs[cn:]
        outs, refs = refs[:n_out], refs[n_out:]
        couts, refs = refs[:cn], refs[cn:]
        scr, sems = refs[:n_scr], refs[n_scr:]

        @pl.when(at(0))
        def _():
            comm.start(cins, couts, *sems)

        body(*ins, *outs, *scr)

        @pl.when(at((3 * total) // 4 if total >= 4 else total - 1))
        def _():
            comm.relay(cins, couts, *sems)

        @pl.when(at(total - 1))
        def _():
            comm.finish(cins, couts, *sems)

    res = pl.pallas_call(
        hosted, name=name, grid=grid, in_specs=list(in_specs) + comm.specs(), out_specs=list(out_specs) + comm.specs(),
        out_shape=list(out_shape) + comm.out_shape(), scratch_shapes=list(scratch) + comm.scratch(),
        input_output_aliases=aliases or {}, compiler_params=params)(*args, *comm.args)
    return list(res[:n_out]), list(res[n_out:])


def _exchange_two_level(blk, small, name):
    _, R, C = blk.shape
    rows = small.shape[0]

    def body(blk_ref, small_ref, stage_ref, out_ref, gath_ref, a_scr, b_scr, t_scr, s1, r1, s3, r3, ss, rs, lsem):
        mx, my, mc = _me()
        me = 4 * mx + 2 * my + mc
        mine = 2 * mx + my
        here, sibling = (mx, my, mc), (mx, my, 1 - mc)

        def rc(src, dst, send, recv, to):
            return pltpu.make_async_remote_copy(src_ref=src, dst_ref=dst, send_sem=send, recv_sem=recv,
                                                device_id=to, device_id_type=MESH)

        own_small = pltpu.make_async_copy(small_ref, gath_ref.at[me], lsem.at[0])
        own_small.start()
        spread = [rc(small_ref, gath_ref.at[me], ss.at[k - 1], rs.at[k - 1], _peer(mx, my, mc, k)) for k in range(1, N_DEV)]
        order = (1, 2, 3, 0)
        to_sib = [rc(blk_ref.at[2 * (mine ^ k) + 1 - mc], stage_ref.at[k], s1.at[k], r1.at[k], sibling) for k in order]
        for cp in spread + to_sib:
            cp.start()
        own = {k: pltpu.make_async_copy(blk_ref.at[2 * (mine ^ k) + mc], a_scr.at[k], lsem.at[1 + k]) for k in order}
        for k in order:
            own[k].start()
        onward = []
        for k in order:
            rc(blk_ref.at[0], stage_ref.at[k], s1.at[k], r1.at[k], here).wait_recv()
            landed = pltpu.make_async_copy(stage_ref.at[k], b_scr.at[k], lsem.at[5 + k])
            landed.start()
            landed.wait()
            own[k].wait()
            t_scr[k] = (a_scr[k].astype(F32) + b_scr[k].astype(F32)).astype(BF16)
            if k > 0:
                cp = rc(t_scr.at[k], out_ref.at[mine], s3.at[k - 1], r3.at[k - 1], (mx ^ (k >> 1), my ^ (k & 1), mc))
                cp.start()
                onward.append(cp)
        keep = pltpu.make_async_copy(t_scr.at[0], out_ref.at[mine], lsem.at[9])
        keep.start()
        for k in range(1, 4):
            rc(t_scr.at[0], out_ref.at[mine ^ k], s3.at[k - 1], r3.at[k - 1], here).wait_recv()
        for k in range(1, N_DEV):
            rc(small_ref, gath_ref.at[me ^ k], ss.at[k - 1], rs.at[k - 1], here).wait_recv()
        for cp in spread + to_sib + onward:
            cp.wait_send()
        keep.wait()
        own_small.wait()

    any_spec = pl.BlockSpec(memory_space=pl.ANY)
    dma = pltpu.SemaphoreType.DMA
    _, out, gath = pl.pallas_call(
        body, name=name,
        in_specs=[any_spec, any_spec], out_specs=[any_spec] * 3,
        out_shape=[_sds((4, R, C), BF16), _sds((4, R, C), BF16), _sds((N_DEV, rows, D), F32)],
        scratch_shapes=[pltpu.VMEM((4, R, C), BF16)] * 3
                       + [dma((4,)), dma((4,)), dma((3,)), dma((3,)), dma((N_DEV - 1,)), dma((N_DEV - 1,)), dma((10,))],
        compiler_params=pltpu.CompilerParams(vmem_limit_bytes=VMEM_LIMIT),
    )(blk, small)
    return out, gath


def _exchange_rows(x_ref, out_ref, send_sems, recv_sems, between=None):
    mx, my, mc = _me()
    me = 4 * mx + 2 * my + mc
    out_ref[pl.ds(me, 1)] = x_ref[...][None]
    sends = []
    for k in range(1, N_DEV):
        cp = pltpu.make_async_remote_copy(
            src_ref=x_ref, dst_ref=out_ref.at[me], send_sem=send_sems.at[k - 1], recv_sem=recv_sems.at[k - 1],
            device_id=_peer(mx, my, mc, k), device_id_type=MESH)
        cp.start()
        sends.append(cp)
    if between is not None:
        between()
    for k in range(1, N_DEV):
        pltpu.make_async_remote_copy(
            src_ref=x_ref, dst_ref=out_ref.at[me ^ k], send_sem=send_sems.at[k - 1], recv_sem=recv_sems.at[k - 1],
            device_id=(mx, my, mc), device_id_type=MESH).wait_recv()
    for cp in sends:
        cp.wait_send()


def _prologue(c8, cctx8, w_ada, b_my, comm):
    nw = w_ada.shape[1]

    cn = comm.n

    def body(*refs):
        c_ref, cctx_ref, w_ref, b_ref = refs[:4]
        cins, refs = refs[4:4 + cn], refs[4 + cn:]
        act_ref, mod_ref = refs[:2]
        couts, refs = refs[2:2 + cn], refs[2 + cn:]
        cmine_scr, call_scr, mine_scr, mall_scr, s1, r1, s2, r2 = refs[:8]
        csems = refs[8:]
        cmine_scr[...] = c_ref[...]
        _exchange_rows(cmine_scr, call_scr, s1, r1)
        rows = [call_scr[d][0:1, :] for d in range(N_DEV)] + [cctx_ref[0:1, :], jnp.zeros((7, D), F32)]
        s = jnp.concatenate(rows, axis=0)
        act = s * _sigmoid(s)
        act_ref[...] = act
        mine_scr[...] = jnp.dot(act.astype(BF16), w_ref[...].astype(BF16), preferred_element_type=F32) + b_ref[...]
        _exchange_rows(mine_scr, mall_scr, s2, r2, between=lambda: comm.start(cins, couts, *csems))
        mod_ref[...] = mall_scr[...]
        comm.relay(cins, couts, *csems)
        comm.finish(cins, couts, *csems)

    sems = [pltpu.SemaphoreType.DMA((N_DEV - 1,))] * 4
    res = pl.pallas_call(
        body, name="prologue", grid=(1,),
        in_specs=[_full((8, D)), _full((8, D)), _full((D, nw)), _full((1, nw))] + comm.specs(),
        out_specs=[_full((16, D)), _full((N_DEV, 16, nw))] + comm.specs(),
        out_shape=[_sds((16, D), F32), _sds((N_DEV, 16, nw), F32)] + comm.out_shape(),
        scratch_shapes=[pltpu.VMEM((8, D), F32), pltpu.VMEM((N_DEV, 8, D), F32), pltpu.VMEM((16, nw), F32),
                        pltpu.VMEM((N_DEV, 16, nw), F32)] + sems + comm.scratch(),
        compiler_params=_params(("arbitrary",)),
    )(c8, cctx8, w_ada, b_my, *comm.args)
    return res[0], res[1], list(res[2:])


def _gather_rows(x, name):
    def body(x_ref, out_ref, send_sems, recv_sems):
        _exchange_rows(x_ref, out_ref, send_sems, recv_sems)

    return pl.pallas_call(
        body, name=name,
        out_shape=_sds((N_DEV,) + x.shape, x.dtype),
        in_specs=[pl.BlockSpec(memory_space=pltpu.VMEM)],
        out_specs=pl.BlockSpec(memory_space=pltpu.VMEM),
        scratch_shapes=[pltpu.SemaphoreType.DMA((N_DEV - 1,)), pltpu.SemaphoreType.DMA((N_DEV - 1,))],
        compiler_params=pltpu.CompilerParams(vmem_limit_bytes=VMEM_LIMIT),
    )(x)


def _ada_bwd(act, dmod_my, w_ada, m, v, tr=256):
    nw = w_ada.shape[1]

    def body(act_ref, dm_ref, w_ref, m_ref, v_ref, g_ref, d_ref, m2_ref, v2_ref, pc_ref):
        dm = dm_ref[...].astype(BF16)
        g = lax.dot_general(act_ref[...].astype(BF16), dm, TN, preferred_element_type=F32)
        w = w_ref[...]
        delta, m2, v2 = _adamw(w, g, m_ref[...], v_ref[...])
        g_ref[...] = g
        d_ref[...] = delta
        m2_ref[...] = m2
        v2_ref[...] = v2
        pc_ref[...] = lax.dot_general(dm[8:16, :], w.astype(BF16), NT, preferred_element_type=F32)

    wspec = _row(tr, nw)
    return pl.pallas_call(
        body, name="ada_bwd", grid=(D // tr,),
        in_specs=[pl.BlockSpec((16, tr), lambda i: (0, i)), _full((16, nw)), wspec, wspec, wspec],
        out_specs=[wspec, wspec, wspec, wspec, pl.BlockSpec((8, tr), lambda i: (0, i))],
        out_shape=[_sds((D, nw), F32)] * 4 + [_sds((8, D), F32)],
        compiler_params=_params(("arbitrary",)),
    )(act, dmod_my, w_ada, m, v)


def _k_in(x, modv, w_in, cos, sin, tm, comm=None):
    L = x.shape[0]

    def body(x_ref, mod_ref, w_ref, cos_ref, sin_ref, h_ref, q_ref, k_ref, v_ref, u_ref, vb_ref, ga_ref, gb_ref):
        n, _ = _ln(x_ref[...])
        h = (n * (1.0 + mod_ref[1:2, :]) + mod_ref[0:1, :]).astype(BF16)
        h_ref[...] = h
        c, s = cos_ref[...], sin_ref[...]

        def proj(lo, width):
            return lax.dot_general(h, w_ref[lo:lo + width, :], NT, preferred_element_type=F32)

        for i in range(2):
            qh = proj(O_Q + i * 256, 256)
            for j in range(2):
                q_ref[:, i * 256 + j * 128:i * 256 + (j + 1) * 128] = (
                    _rope(qh[:, j * 128:(j + 1) * 128], c, s) * Q_SCALE).astype(BF16)
        kv = proj(O_K, 2 * KV_W)
        k_ref[...] = _rope(kv[:, :KV_W], c, s).astype(BF16)
        v_ref[...] = kv[:, KV_W:].astype(BF16)
        u_ref[...] = proj(O_U, GM_W).astype(BF16)
        vb_ref[...] = proj(O_VB, GM_W).astype(BF16)
        ga_ref[...] = proj(O_GA, D).astype(BF16)
        gb_ref[...] = proj(O_GB, D).astype(BF16)

    widths = [D, Q_W, KV_W, KV_W, GM_W, GM_W, D, D]
    return _call(
        body, name="fwd_in", grid=(L // tm,),
        in_specs=[_row(tm, D), _full((8, D)), _resident((IN_W, D)), _row(tm, 128), _row(tm, 128)],
        out_specs=[_row(tm, w) for w in widths],
        out_shape=[_sds((L, w), BF16) for w in widths],
        args=(x, modv, w_in, cos, sin), comm=comm)


def _k_ctx(ctx, modc, w_kv):
    C = ctx.shape[0]

    def body(c_ref, mod_ref, w_ref, hc_ref, kc_ref, vc_ref):
        n, _ = _ln(c_ref[...])
        hc = (n * (1.0 + mod_ref[1:2, :]) + mod_ref[0:1, :]).astype(BF16)
        hc_ref[...] = hc
        kv = lax.dot_general(hc, w_ref[...], NT, preferred_element_type=F32)
        kc_ref[...] = kv[:, :KV_W].astype(BF16)
        vc_ref[...] = kv[:, KV_W:].astype(BF16)

    return pl.pallas_call(
        body, name="fwd_ctx", grid=(1,),
        in_specs=[_full((C, D)), _full((8, D)), _full((2 * KV_W, D))],
        out_specs=[_full((C, D)), _full((C, KV_W)), _full((C, KV_W))],
        out_shape=[_sds((C, D), BF16), _sds((C, KV_W), BF16), _sds((C, KV_W), BF16)],
        compiler_params=_params(("arbitrary",)),
    )(ctx, modc, w_kv)


def _attn_bias():
    r = (np.arange(GQA_GROUP * BLK) & (BLK - 1))[:, None]
    j = np.arange(3 * BLK)[None, :]
    band = np.abs(j - BLK - r) <= BLK
    variants = [band & (j >= BLK), band, band & (j < 2 * BLK)]
    return jnp.asarray(np.stack([np.where(v, 0.0, NEG_INF) for v in variants]), F32)


def _masked(s, bias, C):
    return jnp.concatenate([s[:, :C], s[:, C:] + bias], axis=1)


def _sink_col(sink_ref, hk):
    grp = lax.broadcasted_iota(jnp.int32, (GQA_GROUP * BLK, 1), 0) >> 7
    col = jnp.full((GQA_GROUP * BLK, 1), sink_ref[hk * GQA_GROUP], F32)
    for g in range(1, GQA_GROUP):
        col = jnp.where(grp == g, sink_ref[hk * GQA_GROUP + g], col)
    return col


ATTN_FWD_BLOCKS = 4


def _k_attn(sink, q, k, v, kc, vc, bias, comm=None):
    L = q.shape[0]
    C = kc.shape[0]
    nb = L // BLK
    nq = min(ATTN_FWD_BLOCKS, nb)
    steps = nb // nq

    def body(sink_ref, q_ref, kp_ref, km_ref, kx_ref, vp_ref, vm_ref, vx_ref, kc_ref, vc_ref, bias_ref, ya_ref, lse_ref):
        i = pl.program_id(0)
        chains = [(qb, hk) for qb in range(nq) for hk in range(N_KV_HEADS)]

        def band(qb):
            first = jnp.where(i == 0, 0, 1) if qb == 0 else 1
            return bias_ref[jnp.where(i == steps - 1, 2, first) if qb == nq - 1 else first]

        def keys(ctx_ref, p_ref, m_ref, x_ref, qb, hk):
            sl = slice(hk * HEAD_DIM, (hk + 1) * HEAD_DIM)
            blocks = [p_ref[:, sl]] + [m_ref[j * BLK:(j + 1) * BLK, sl] for j in range(nq)] + [x_ref[:, sl]]
            return jnp.concatenate([ctx_ref[:, sl]] + blocks[qb:qb + 3], axis=0)

        def queries(qb, hk):
            return jnp.concatenate(
                [q_ref[qb * BLK:(qb + 1) * BLK, (hk * GQA_GROUP + g) * HEAD_DIM:(hk * GQA_GROUP + g + 1) * HEAD_DIM]
                 for g in range(GQA_GROUP)], axis=0)

        def scores(qb, hk):
            return _masked(lax.dot_general(queries(qb, hk), keys(kc_ref, kp_ref, km_ref, kx_ref, qb, hk), NT,
                                           preferred_element_type=F32), band(qb), C)

        ahead = 2
        s = [scores(*c) for c in chains[:ahead]]
        for n, (qb, hk) in enumerate(chains):
            if n + ahead < len(chains):
                s.append(scores(*chains[n + ahead]))
            s_ = s[n]
            sink_c = _sink_col(sink_ref, hk)
            m = jnp.maximum(jnp.max(s_, axis=1, keepdims=True), sink_c)
            p = jnp.exp(s_ - m)
            den = jnp.sum(p, axis=1, keepdims=True) + jnp.exp(sink_c - m)
            o = jnp.dot(p.astype(BF16), keys(vc_ref, vp_ref, vm_ref, vx_ref, qb, hk), preferred_element_type=F32) * (1.0 / den)
            lse = m + jnp.log(den)
            rows = slice(qb * BLK, (qb + 1) * BLK)
            for g in range(GQA_GROUP):
                h = hk * GQA_GROUP + g
                ya_ref[rows, h * HEAD_DIM:(h + 1) * HEAD_DIM] = o[g * BLK:(g + 1) * BLK, :].astype(BF16)
                lse_ref[rows, h:h + 1] = lse[g * BLK:(g + 1) * BLK, :]

    kv3 = [pl.BlockSpec((BLK, KV_W), lambda i: (jnp.maximum(nq * i - 1, 0), 0)),
           pl.BlockSpec((nq * BLK, KV_W), lambda i: (i, 0)),
           pl.BlockSpec((BLK, KV_W), lambda i: (jnp.minimum(nq * i + nq, nb - 1), 0))]
    return _call(
        body, name="fwd_attn", grid=(steps,),
        in_specs=[pl.BlockSpec(memory_space=pltpu.SMEM), _row(nq * BLK, Q_W)] + kv3 + kv3
                 + [_full((C, KV_W)), _full((C, KV_W)), _full((3, GQA_GROUP * BLK, 3 * BLK))],
        out_specs=[_row(nq * BLK, Q_W), _row(nq * BLK, N_Q_HEADS)],
        out_shape=[_sds((L, Q_W), BF16), _sds((L, N_Q_HEADS), F32)],
        args=(sink, q, k, k, k, v, v, v, kc, vc, bias), comm=comm)


GMLP_CHUNKS = 4


def _split_pair(t):
    low = lax.broadcasted_iota(jnp.int32, t.shape, 1) < GROUP_DIM
    zero = jnp.zeros_like(t)
    return jnp.where(low, t, zero), jnp.where(low, zero, t)


def _gmlp_spatial(w_ref, t_b, nch):
    rows = []
    for c in range(nch):
        tiles = []
        for pr in range(N_GROUPS // 2):
            lo, hi = _split_pair(t_b[c * BLK:(c + 1) * BLK, pr * 128:(pr + 1) * 128])
            tiles.append(jnp.dot(w_ref[2 * pr], lo, preferred_element_type=F32)
                         + jnp.dot(w_ref[2 * pr + 1], hi, preferred_element_type=F32))
        rows.append(jnp.concatenate(tiles, axis=1))
    return jnp.concatenate(rows, axis=0)


def _gmlp_fwd_vals(u, vb, lnv_ref, ws_ref, bsp_ref, nch):
    uf = u.astype(F32)
    vf = vb.astype(F32)
    gu, tu = _gelu(uf)
    gv, tv = _gelu(vf)
    vhat, rstd = _ln(gv)
    vn = (vhat * lnv_ref[0:1, :] + lnv_ref[1:2, :]).astype(BF16)
    s = _gmlp_spatial(ws_ref, vn, nch) + jnp.concatenate([bsp_ref[...]] * nch, axis=0)
    return uf, vf, gu, tu, tv, vhat, rstd, vn, s


def _k_gmlp(u, vb, lnv, ws, bsp):
    L = u.shape[0]
    nch = min(GMLP_CHUNKS, L // BLK)
    tm = nch * BLK

    def body(u_ref, vb_ref, lnv_ref, ws_ref, bsp_ref, yb_ref):
        _, _, gu, _, _, _, _, _, s = _gmlp_fwd_vals(u_ref[...], vb_ref[...], lnv_ref, ws_ref, bsp_ref, nch)
        yb_ref[...] = (gu * s).astype(BF16)

    return pl.pallas_call(
        body, name="fwd_gmlp", grid=(L // tm,),
        in_specs=[_row(tm, GM_W), _row(tm, GM_W), _full((8, GM_W)), _full((N_GROUPS, BLK, BLK)), _full((BLK, GM_W))],
        out_specs=_row(tm, GM_W),
        out_shape=_sds((L, GM_W), BF16),
        compiler_params=_params(("arbitrary",)),
    )(u, vb, lnv, ws, bsp)


def _k_merge(x, ya, yb, ga, gb, w_a, w_b, w_o, modv, lnv, tm):
    L = x.shape[0]

    def body(x_ref, ya_ref, yb_ref, ga_ref, gb_ref, wa_ref, wb_ref, wo_ref, mod_ref, ln_ref,
             mg_ref, mix_ref, xm_ref, h2_ref):
        a = jnp.dot(ya_ref[...], wa_ref[...], preferred_element_type=F32)
        b = jnp.dot(yb_ref[...], wb_ref[...], preferred_element_type=F32)
        merged = (_sigmoid(ga_ref[...].astype(F32)) * a + _sigmoid(gb_ref[...].astype(F32)) * b).astype(BF16)
        mg_ref[...] = merged
        mix = jnp.dot(merged, wo_ref[...], preferred_element_type=F32)
        mix_ref[...] = mix.astype(BF16)
        r1 = ALPHA * x_ref[...] + mod_ref[2:3, :] * mix
        r1hat, _ = _ln(r1)
        xm = r1hat * ln_ref[0:1, :] + ln_ref[1:2, :]
        xm_ref[...] = xm
        n2, _ = _ln(xm)
        h2_ref[...] = (n2 * (1.0 + mod_ref[4:5, :]) + mod_ref[3:4, :]).astype(BF16)

    return pl.pallas_call(
        body, name="fwd_merge", grid=(L // tm,),
        in_specs=[_row(tm, D), _row(tm, Q_W), _row(tm, GM_W), _row(tm, D), _row(tm, D),
                  _resident((Q_W, D)), _resident((GM_W, D)), _resident((D, D)), _full((8, D)), _full((8, D))],
        out_specs=[_row(tm, D)] * 4,
        out_shape=[_sds((L, D), BF16), _sds((L, D), BF16), _sds((L, D), F32), _sds((L, D), BF16)],
        compiler_params=_params(("arbitrary",)),
    )(x, ya, yb, ga, gb, w_a, w_b, w_o, modv, lnv)


V7X_MXU_COLUMNS = 256
FFN_CHUNK = V7X_MXU_COLUMNS


def _k_ffn(h2, xm, tgt, w_fi, w_fo, modv, lnv, tm):
    L = h2.shape[0]

    def body(h2_ref, xm_ref, t_ref, wi_ref, wo_ref, mod_ref, ln_ref, gate_ref, up_ref, a_ref, dr2_ref, df_ref, acc_ref):
        @pl.when(pl.program_id(0) == 0)
        def _():
            acc_ref[...] = jnp.zeros_like(acc_ref)

        h2v = h2_ref[...]
        ch = FFN_CHUNK
        chunks = [j * ch for j in range(FFN_H // ch)]

        def project(lo):
            return (lax.dot_general(h2v, wi_ref[lo:lo + ch, :], NT, preferred_element_type=F32),
                    lax.dot_general(h2v, wi_ref[FFN_H + lo:FFN_H + lo + ch, :], NT, preferred_element_type=F32))

        f = jnp.zeros((tm, D), F32)
        ahead = [project(chunks[0])]
        for j, lo in enumerate(chunks):
            if j + 1 < len(chunks):
                ahead.append(project(chunks[j + 1]))
            gate, up = ahead[j]
            act = (gate * _sigmoid(gate) * up).astype(BF16)
            gate_ref[:, lo:lo + ch] = gate.astype(BF16)
            up_ref[:, lo:lo + ch] = up.astype(BF16)
            a_ref[:, lo:lo + ch] = act
            f = f + jnp.dot(act, wo_ref[lo:lo + ch, :], preferred_element_type=F32)
        gate2 = mod_ref[5:6, :]
        r2 = ALPHA * xm_ref[...] + gate2 * f
        r2hat, rstd = _ln(r2)
        y = r2hat * ln_ref[2:3, :] + ln_ref[3:4, :]
        err = y - t_ref[...]
        dy = err * (1.0 / D)
        dr2 = _ln_bwd(dy * ln_ref[2:3, :], r2hat, rstd)
        dr2_ref[...] = dr2
        df_ref[...] = (gate2 * dr2).astype(BF16)
        acc_ref[0:1, :] += _colsum(dy * r2hat)
        acc_ref[1:2, :] += _colsum(dy)
        acc_ref[2:3, :] += _colsum(dr2 * f)
        acc_ref[3:4, :] += _colsum(err * err) * (0.5 / D)

    return pl.pallas_call(
        body, name="fwd_ffn", grid=(L // tm,),
        in_specs=[_row(tm, D), _row(tm, D), _row(tm, D), _resident((2 * FFN_H, D)), _resident((FFN_H, D)),
                  _full((8, D)), _full((8, D))],
        out_specs=[_row(tm, FFN_H)] * 3 + [_row(tm, D), _row(tm, D), _full((8, D))],
        out_shape=[_sds((L, FFN_H), BF16)] * 3 + [_sds((L, D), F32), _sds((L, D), BF16), _sds((8, D), F32)],
        compiler_params=_params(("arbitrary",)),
    )(h2, xm, tgt, w_fi, w_fo, modv, lnv)


def _k_ffn_bwd(df, gate, up, xm, dr2, x, mix, w_fi, w_fo, modv, lnv, tm):
    L = df.shape[0]

    def body(df_ref, gate_ref, up_ref, xm_ref, dr2_ref, x_ref, mix_ref, wi_ref, wo_ref, mod_ref, ln_ref,
             dF_ref, dmix_ref, dxp_ref, acc_ref):
        @pl.when(pl.program_id(0) == 0)
        def _():
            acc_ref[...] = jnp.zeros_like(acc_ref)

        dfv = df_ref[...]
        ch = FFN_CHUNK
        chunks = [j * ch for j in range(FFN_H // ch)]

        def d_act(lo):
            return lax.dot_general(dfv, wo_ref[lo:lo + ch, :], NT, preferred_element_type=F32)

        n2, rstd2 = _ln(xm_ref[...])
        mixf = mix_ref[...].astype(F32)
        gate1 = mod_ref[2:3, :]
        r1hat, rstd1 = _ln(ALPHA * x_ref[...] + gate1 * mixf)
        dh2 = jnp.zeros((tm, D), F32)
        das = [d_act(chunks[0])]
        for j, lo in enumerate(chunks):
            if j + 1 < len(chunks):
                das.append(d_act(chunks[j + 1]))
            da = das[j]
            gate = gate_ref[:, lo:lo + ch].astype(F32)
            upv = up_ref[:, lo:lo + ch].astype(F32)
            sg = _sigmoid(gate)
            d_gate = (da * upv * (sg * (1.0 + gate * (1.0 - sg)))).astype(BF16)
            d_up = (da * (gate * sg)).astype(BF16)
            dF_ref[:, lo:lo + ch] = d_gate
            dF_ref[:, FFN_H + lo:FFN_H + lo + ch] = d_up
            dh2 = dh2 + jnp.dot(d_gate, wi_ref[lo:lo + ch, :], preferred_element_type=F32)
            dh2 = dh2 + jnp.dot(d_up, wi_ref[FFN_H + lo:FFN_H + lo + ch, :], preferred_element_type=F32)
        acc_ref[0:1, :] += _colsum(dh2)
        acc_ref[1:2, :] += _colsum(dh2 * n2)
        dxm = ALPHA * dr2_ref[...] + _ln_bwd(dh2 * (1.0 + mod_ref[4:5, :]), n2, rstd2)
        acc_ref[2:3, :] += _colsum(dxm * r1hat)
        acc_ref[3:4, :] += _colsum(dxm)
        dr1 = _ln_bwd(dxm * ln_ref[0:1, :], r1hat, rstd1)
        dmix_ref[...] = (gate1 * dr1).astype(BF16)
        dxp_ref[...] = ALPHA * dr1
        acc_ref[4:5, :] += _colsum(dr1 * mixf)

    return pl.pallas_call(
        body, name="bwd_ffn", grid=(L // tm,),
        in_specs=[_row(tm, D), _row(tm, FFN_H), _row(tm, FFN_H), _row(tm, D), _row(tm, D), _row(tm, D), _row(tm, D),
                  _resident((2 * FFN_H, D)), _resident((FFN_H, D)), _full((8, D)), _full((8, D))],
        out_specs=[_row(tm, 2 * FFN_H), _row(tm, D), _row(tm, D), _full((8, D))],
        out_shape=[_sds((L, 2 * FFN_H), BF16), _sds((L, D), BF16), _sds((L, D), F32), _sds((8, D), F32)],
        compiler_params=_params(("arbitrary",)),
    )(df, gate, up, xm, dr2, x, mix, w_fi, w_fo, modv, lnv)


def _k_merge_bwd(dmix, merged, ya, yb, ga, gb, w_a, w_b, w_o, tm):
    L = dmix.shape[0]
    n = L // tm

    def body(dmix_ref, mg_ref, ya_ref, yb_ref, ga_ref, gb_ref, wa_ref, wb_ref, wo_ref,
             dga_ref, dgb_ref, dya_ref, dyb_ref, gwa_ref, gwb_ref, gwo_ref, acc_a, acc_b, acc_o):
        i = pl.program_id(0)

        @pl.when(i == 0)
        def _():
            for r in (acc_a, acc_b, acc_o):
                r[...] = jnp.zeros_like(r)

        dmixv = dmix_ref[...]
        dmg = lax.dot_general(dmixv, wo_ref[...], NT, preferred_element_type=F32)
        acc_o[...] += lax.dot_general(mg_ref[...], dmixv, TN, preferred_element_type=F32)
        ya = ya_ref[...]
        a = jnp.dot(ya, wa_ref[...], preferred_element_type=F32)
        sa = _sigmoid(ga_ref[...].astype(F32))
        dA = (dmg * sa).astype(BF16)
        dga_ref[...] = (dmg * a * (sa * (1.0 - sa))).astype(BF16)
        dya_ref[...] = lax.dot_general(dA, wa_ref[...], NT, preferred_element_type=F32).astype(BF16)
        acc_a[...] += lax.dot_general(ya, dA, TN, preferred_element_type=F32)
        yb = yb_ref[...]
        b = jnp.dot(yb, wb_ref[...], preferred_element_type=F32)
        sb = _sigmoid(gb_ref[...].astype(F32))
        dB = (dmg * sb).astype(BF16)
        dgb_ref[...] = (dmg * b * (sb * (1.0 - sb))).astype(BF16)
        dyb_ref[...] = lax.dot_general(dB, wb_ref[...], NT, preferred_element_type=F32).astype(BF16)
        acc_b[...] += lax.dot_general(yb, dB, TN, preferred_element_type=F32)

        @pl.when(i == n - 1)
        def _():
            gwa_ref[...] = acc_a[...].astype(BF16)
            gwb_ref[...] = acc_b[...].astype(BF16)
            gwo_ref[...] = acc_o[...].astype(BF16)

    return pl.pallas_call(
        body, name="bwd_merge", grid=(n,),
        in_specs=[_row(tm, D), _row(tm, D), _row(tm, Q_W), _row(tm, GM_W), _row(tm, D), _row(tm, D),
                  _resident((Q_W, D)), _resident((GM_W, D)), _resident((D, D))],
        out_specs=[_row(tm, D), _row(tm, D), _row(tm, Q_W), _row(tm, GM_W), _full((Q_W, D)), _full((GM_W, D)), _full((D, D))],
        out_shape=[_sds((L, D), BF16), _sds((L, D), BF16), _sds((L, Q_W), BF16), _sds((L, GM_W), BF16),
                   _sds((Q_W, D), BF16), _sds((GM_W, D), BF16), _sds((D, D), BF16)],
        scratch_shapes=[pltpu.VMEM((Q_W, D), F32), pltpu.VMEM((GM_W, D), F32), pltpu.VMEM((D, D), F32)],
        compiler_params=_params(("arbitrary",)),
    )(dmix, merged, ya, yb, ga, gb, w_a, w_b, w_o)


def _k_gmlp_bwd(u, vb, dyb, lnv, ws, wst, bsp):
    L = u.shape[0]
    nch = min(GMLP_CHUNKS, L // BLK)
    tm = nch * BLK

    def body(u_ref, vb_ref, dyb_ref, lnv_ref, ws_ref, wst_ref, bsp_ref, du_ref, dvb_ref, gws_ref, gbst_ref, gln_ref):
        @pl.when(pl.program_id(0) == 0)
        def _():
            gws_ref[...] = jnp.zeros_like(gws_ref)
            gbst_ref[...] = jnp.zeros_like(gbst_ref)
            gln_ref[...] = jnp.zeros_like(gln_ref)

        uf, vf, gu, tu, tv, vhat, rstd, vn, s = _gmlp_fwd_vals(u_ref[...], vb_ref[...], lnv_ref, ws_ref, bsp_ref, nch)
        dyb_f = dyb_ref[...].astype(F32)
        du_ref[...] = (dyb_f * s * _gelu_grad(uf, tu)).astype(BF16)
        ds = dyb_f * gu
        ds_b = ds.astype(BF16)
        for pr in range(N_GROUPS // 2):
            lanes = slice(pr * 128, (pr + 1) * 128)
            gw_lo = gw_hi = ds_sum = None
            for c in range(nch):
                rows = slice(c * BLK, (c + 1) * BLK)
                lo, hi = _split_pair(ds_b[rows, lanes])
                t_lo = lax.dot_general(lo, vn[rows, lanes], NT, preferred_element_type=F32)
                t_hi = lax.dot_general(hi, vn[rows, lanes], NT, preferred_element_type=F32)
                gw_lo = t_lo if c == 0 else gw_lo + t_lo
                gw_hi = t_hi if c == 0 else gw_hi + t_hi
                ds_sum = ds[rows, lanes] if c == 0 else ds_sum + ds[rows, lanes]
            gws_ref[2 * pr] += gw_lo
            gws_ref[2 * pr + 1] += gw_hi
            b_lo, b_hi = _split_pair(ds_sum)
            gbst_ref[:, 2 * pr:2 * pr + 1] += jnp.sum(b_lo, axis=1, keepdims=True)
            gbst_ref[:, 2 * pr + 1:2 * pr + 2] += jnp.sum(b_hi, axis=1, keepdims=True)
        dvn = _gmlp_spatial(wst_ref, ds_b, nch)
        gln_ref[0:1, :] += _colsum(dvn * vhat)
        gln_ref[1:2, :] += _colsum(dvn)
        dgv = _ln_bwd(dvn * lnv_ref[0:1, :], vhat, rstd)
        dvb_ref[...] = (dgv * _gelu_grad(vf, tv)).astype(BF16)

    return pl.pallas_call(
        body, name="bwd_gmlp", grid=(L // tm,),
        in_specs=[_row(tm, GM_W)] * 3 + [_full((8, GM_W)), _full((N_GROUPS, BLK, BLK)), _full((N_GROUPS, BLK, BLK)),
                                         _full((BLK, GM_W))],
        out_specs=[_row(tm, GM_W), _row(tm, GM_W), _full((N_GROUPS, BLK, BLK)), _full((BLK, N_GROUPS)), _full((8, GM_W))],
        out_shape=[_sds((L, GM_W), BF16), _sds((L, GM_W), BF16), _sds((N_GROUPS, BLK, BLK), F32),
                   _sds((BLK, N_GROUPS), F32), _sds((8, GM_W), F32)],
        compiler_params=_params(("arbitrary",)),
    )(u, vb, dyb, lnv, ws, wst, bsp)


ATTN_BWD_BLOCKS = 2


def _k_attn_bwd(sink, q, k, v, kc, vc, dya, lse, cos, sin, bias, comm=None):
    L = q.shape[0]
    C = kc.shape[0]
    nb = L // BLK
    nq = min(ATTN_BWD_BLOCKS, nb)
    steps = nb // nq
    NK = C + 3 * BLK
    chains = [(qb, hk) for qb in range(nq) for hk in range(N_KV_HEADS)]

    def body(sink_ref, q_ref, kp_ref, km_ref, kx_ref, vp_ref, vm_ref, vx_ref, kc_ref, vc_ref, do_ref, lse_ref,
             cq_ref, sq_ref, cl_ref, sl_ref, bias_ref,
             dq_ref, dk_ref, dv_ref, dkc_ref, dvc_ref, dsink_ref,
             dq_scr, ck_scr, cv_scr, k1_acc, k2_acc, v1_acc, v2_acc):
        i = pl.program_id(0)

        @pl.when(i == 0)
        def _():
            for r in (k1_acc, k2_acc, v1_acc, v2_acc, dkc_ref, dvc_ref, dsink_ref):
                r[...] = jnp.zeros_like(r)

        @pl.when(i < steps)
        def _():
            def band(qb):
                first = jnp.where(i == 0, 0, 1) if qb == 0 else 1
                return bias_ref[jnp.where(i == steps - 1, 2, first) if qb == nq - 1 else first]

            def lanes(hk):
                return slice(hk * HEAD_DIM, (hk + 1) * HEAD_DIM)

            def keys(ctx_ref, p_ref, m_ref, x_ref, qb, hk):
                sl = lanes(hk)
                blocks = [p_ref[:, sl]] + [m_ref[j * BLK:(j + 1) * BLK, sl] for j in range(nq)] + [x_ref[:, sl]]
                return jnp.concatenate([ctx_ref[:, sl]] + blocks[qb:qb + 3], axis=0)

            def stacked(ref, qb, hk, width):
                return jnp.concatenate(
                    [ref[qb * BLK:(qb + 1) * BLK, (hk * GQA_GROUP + g) * width:(hk * GQA_GROUP + g + 1) * width]
                     for g in range(GQA_GROUP)], axis=0)

            def scores(qb, hk):
                kcat = keys(kc_ref, kp_ref, km_ref, kx_ref, qb, hk)
                qg = stacked(q_ref, qb, hk, HEAD_DIM)
                s = _masked(lax.dot_general(qg, kcat, NT, preferred_element_type=F32), band(qb), C)
                dog = stacked(do_ref, qb, hk, HEAD_DIM)
                dp = lax.dot_general(dog, keys(vc_ref, vp_ref, vm_ref, vx_ref, qb, hk), NT, preferred_element_type=F32)
                return kcat, qg, dog, s, dp

            def softmax_bwd(qb, hk, s, dp):
                lse_c = stacked(lse_ref, qb, hk, 1)
                p = jnp.exp(s - lse_c)
                delta = jnp.sum(p * dp, axis=1, keepdims=True)
                ds = (p * (dp - delta)).astype(BF16)
                p_sink = jnp.exp(_sink_col(sink_ref, hk) - lse_c) * delta
                return p.astype(BF16), ds, p_sink

            def put_dq(qb, hk, dqs, p_sink):
                for g in range(GQA_GROUP):
                    h = hk * GQA_GROUP + g
                    dq_scr[qb * BLK:(qb + 1) * BLK, h * HEAD_DIM:(h + 1) * HEAD_DIM] = dqs[g * BLK:(g + 1) * BLK, :]
                    tot = jnp.sum(p_sink[g * BLK:(g + 1) * BLK, :], axis=0, keepdims=True)
                    dsink_ref[h:h + 1, :] -= jnp.broadcast_to(tot, (1, 128))

            ahead = 4
            sc = [scores(*c) for c in chains[:ahead]]
            pending = None
            for n, (qb, hk) in enumerate(chains):
                if n + ahead < len(chains):
                    sc.append(scores(*chains[n + ahead]))
                kcat, qg, dog, s, dp = sc[n]
                pb, ds, p_sink = softmax_bwd(qb, hk, s, dp)
                if pending is not None:
                    pqb, phk, pds, ppb, pqg, pdog = pending
                    ck_scr[pqb, :, lanes(phk)] = lax.dot_general(pds, pqg, TN, preferred_element_type=F32)
                    cv_scr[pqb, :, lanes(phk)] = lax.dot_general(ppb, pdog, TN, preferred_element_type=F32)
                put_dq(qb, hk, jnp.dot(ds, kcat, preferred_element_type=F32), p_sink)
                pending = (qb, hk, ds, pb, qg, dog)
            pqb, phk, pds, ppb, pqg, pdog = pending
            ck_scr[pqb, :, lanes(phk)] = lax.dot_general(pds, pqg, TN, preferred_element_type=F32)
            cq, sq = cq_ref[...], sq_ref[...]
            for j in range(4):
                dq_ref[:, j * 128:(j + 1) * 128] = _unrope(dq_scr[:, j * 128:(j + 1) * 128] * Q_SCALE, cq, sq).astype(BF16)
            cv_scr[pqb, :, lanes(phk)] = lax.dot_general(ppb, pdog, TN, preferred_element_type=F32)
            dkc_ref[...] += functools.reduce(lambda a, b: a + b, [ck_scr[qb, 0:C, :] for qb in range(nq)])
            dvc_ref[...] += functools.reduce(lambda a, b: a + b, [cv_scr[qb, 0:C, :] for qb in range(nq)])

        @pl.when(i >= steps)
        def _():
            ck_scr[...] = jnp.zeros_like(ck_scr)
            cv_scr[...] = jnp.zeros_like(cv_scr)

        def slot(scr, r, carried):
            parts = [scr[qb, C + (r - qb) * BLK:C + (r - qb + 1) * BLK, :] for qb in range(nq) if 0 <= r - qb <= 2]
            total = functools.reduce(lambda a, b: a + b, parts)
            return total if carried is None else carried[...] + total

        for r in range(nq):
            rows = slice(r * BLK, (r + 1) * BLK)
            carried_k, carried_v = ((k1_acc, v1_acc), (k2_acc, v2_acc), (None, None))[min(r, 2)]
            tables = (cl_ref[...], sl_ref[...]) if r == 0 else (cq_ref[(r - 1) * BLK:r * BLK, :], sq_ref[(r - 1) * BLK:r * BLK, :])
            dk_ref[rows, :] = _unrope(slot(ck_scr, r, carried_k), *tables).astype(BF16)
            dv_ref[rows, :] = slot(cv_scr, r, carried_v).astype(BF16)
        k1_acc[...] = slot(ck_scr, nq, None)
        v1_acc[...] = slot(cv_scr, nq, None)
        k2_acc[...] = slot(ck_scr, nq + 1, None)
        v2_acc[...] = slot(cv_scr, nq + 1, None)

    last = steps - 1
    kv3 = [pl.BlockSpec((BLK, KV_W), lambda i: (jnp.clip(nq * i - 1, 0, nb - 1), 0)),
           pl.BlockSpec((nq * BLK, KV_W), lambda i: (jnp.minimum(i, last), 0)),
           pl.BlockSpec((BLK, KV_W), lambda i: (jnp.minimum(nq * i + nq, nb - 1), 0))]
    cur = lambda w: pl.BlockSpec((nq * BLK, w), lambda i: (jnp.minimum(i, last), 0))
    late = lambda w: pl.BlockSpec((BLK, w), lambda i: (jnp.clip(nq * i - 1, 0, nb - 1), 0))
    out2 = lambda w: pl.BlockSpec((nq * BLK, w), lambda i: (i, 0))
    return _call(
        body, name="bwd_attn", grid=(steps + 1,),
        in_specs=[pl.BlockSpec(memory_space=pltpu.SMEM), cur(Q_W)] + kv3 + kv3
                 + [_full((C, KV_W)), _full((C, KV_W)), cur(Q_W), cur(N_Q_HEADS), cur(128), cur(128), late(128), late(128),
                    _full((3, GQA_GROUP * BLK, 3 * BLK))],
        out_specs=[cur(Q_W), out2(KV_W), out2(KV_W), _full((C, KV_W)), _full((C, KV_W)), _full((8, 128))],
        out_shape=[_sds((L, Q_W), BF16), _sds((L + nq * BLK, KV_W), BF16), _sds((L + nq * BLK, KV_W), BF16),
                   _sds((C, KV_W), F32), _sds((C, KV_W), F32), _sds((8, 128), F32)],
        scratch=[pltpu.VMEM((nq * BLK, Q_W), F32), pltpu.VMEM((nq, NK, KV_W), F32), pltpu.VMEM((nq, NK, KV_W), F32)]
                + [pltpu.VMEM((BLK, KV_W), F32)] * 4,
        args=(sink, q, k, k, k, v, v, v, kc, vc, dya, lse, cos, sin, cos, sin, bias), comm=comm)


def _k_ctx_bwd(ctx, modc, hc, dkc, dvc, w_kv):
    C = ctx.shape[0]

    def body(c_ref, mod_ref, hc_ref, dkc_ref, dvc_ref, w_ref, gw_ref, dmod_ref):
        dkv = jnp.concatenate([dkc_ref[...], dvc_ref[...]], axis=1).astype(BF16)
        gw_ref[...] = lax.dot_general(dkv, hc_ref[...], TN, preferred_element_type=F32)
        dhc = jnp.dot(dkv, w_ref[...], preferred_element_type=F32)
        n, _ = _ln(c_ref[...])
        dmod_ref[...] = jnp.zeros_like(dmod_ref)
        dmod_ref[0:1, :] = _colsum(dhc)
        dmod_ref[1:2, :] = _colsum(dhc * n)

    return pl.pallas_call(
        body, name="bwd_ctx", grid=(1,),
        in_specs=[_full((C, D)), _full((8, D)), _full((C, D)), _full((C, KV_W)), _full((C, KV_W)), _full((2 * KV_W, D))],
        out_specs=[_full((2 * KV_W, D)), _full((8, D))],
        out_shape=[_sds((2 * KV_W, D), F32), _sds((8, D), F32)],
        compiler_params=_params(("arbitrary",)),
    )(ctx, modc, hc, dkc, dvc, w_kv)


def _k_in_bwd(dq, dk, dv, du, dvb, dga, dgb, x, dxp, w_in, modv, tm, comm=None):
    L = x.shape[0]
    parts = [(O_Q, Q_W), (O_K, KV_W), (O_V, KV_W), (O_U, GM_W), (O_VB, GM_W), (O_GA, D), (O_GB, D)]

    def body(dq_ref, dk_ref, dv_ref, du_ref, dvb_ref, dga_ref, dgb_ref, x_ref, dxp_ref, w_ref, mod_ref,
             dP_ref, gx_ref, acc_ref):
        @pl.when(pl.program_id(0) == 0)
        def _():
            acc_ref[...] = jnp.zeros_like(acc_ref)

        for (lo, width), r in zip(parts, (dq_ref, dk_ref, dv_ref, du_ref, dvb_ref, dga_ref, dgb_ref)):
            dP_ref[:, lo:lo + width] = r[...]
        n1, rstd1 = _ln(x_ref[...])
        dh = jnp.dot(dP_ref[...], w_ref[...], preferred_element_type=F32)
        acc_ref[0:1, :] += _colsum(dh)
        acc_ref[1:2, :] += _colsum(dh * n1)
        gx_ref[...] = dxp_ref[...] + _ln_bwd(dh * (1.0 + mod_ref[1:2, :]), n1, rstd1)

    return _call(
        body, name="bwd_in", grid=(L // tm,),
        in_specs=[_row(tm, w) for _, w in parts] + [_row(tm, D), _row(tm, D), _resident((IN_W, D)), _full((8, D))],
        out_specs=[_row(tm, IN_W), _row(tm, D), _full((8, D))],
        out_shape=[_sds((L, IN_W), BF16), _sds((L, D), F32), _sds((8, D), F32)],
        args=(dq, dk, dv, du, dvb, dga, dgb, x, dxp, w_in, modv), comm=comm)


def _wgrad(a, b, name, tk, tt, comm=None, extra=None):
    T, K = a.shape
    N = b.shape[1]
    nt = T // tt

    def body(*refs):
        a_ref, b_ref = refs[:2]
        o_ref, acc_ref = refs[-2:]
        j, t = pl.program_id(0), pl.program_id(1)

        @pl.when(t == 0)
        def _():
            acc_ref[...] = jnp.zeros_like(acc_ref)

        acc_ref[...] += lax.dot_general(a_ref[...], b_ref[...], TN, preferred_element_type=F32)

        if extra is not None:
            lo, rows = extra[0] % tk, extra[1].shape[0]

            @pl.when((t == nt - 1) & (j == extra[0] // tk))
            def _():
                acc_ref[lo:lo + rows, :] += refs[2][...]

        @pl.when(t == nt - 1)
        def _():
            o_ref[...] = acc_ref[...].astype(BF16)

    extra_specs = [] if extra is None else [pl.BlockSpec(extra[1].shape, lambda j, t: (0, 0))]
    (out,), got = _call(
        body, name=name, grid=(K // tk, nt),
        in_specs=[pl.BlockSpec((tt, tk), lambda j, t: (t, j)), pl.BlockSpec((tt, N), lambda j, t: (t, 0))] + extra_specs,
        out_specs=[pl.BlockSpec((tk, N), lambda j, t: (j, 0))],
        out_shape=[_sds((K, N), BF16)],
        scratch=[pltpu.VMEM((tk, N), F32)],
        args=(a, b) + (() if extra is None else (extra[1],)), comm=comm)
    return (out, got) if comm is not None else out


def _adamw_reduce(parts, w, m, v, name, tr):
    R, C = w.shape
    n_parts = parts.shape[0]

    def body(p_ref, w_ref, m_ref, v_ref, g_ref, d_ref, m2_ref, v2_ref):
        g = p_ref[0].astype(F32)
        for i in range(1, n_parts):
            g = g + p_ref[i].astype(F32)
        delta, m2, v2 = _adamw(w_ref[...], g, m_ref[...], v_ref[...])
        g_ref[...] = g
        d_ref[...] = delta
        m2_ref[...] = m2
        v2_ref[...] = v2

    spec = _row(tr, C)
    return pl.pallas_call(
        body, name=name, grid=(R // tr,),
        in_specs=[pl.BlockSpec((n_parts, tr, C), lambda i: (0, i, 0)), spec, spec, spec],
        out_specs=[spec] * 4,
        out_shape=[_sds((R, C), F32)] * 4,
        compiler_params=_params(("arbitrary",)),
    )(parts, w, m, v)


SMALL_ORDER = ("b_ada", "ln1_g", "ln1_b", "ln2_g", "ln2_b", "gmlp_ln_g", "gmlp_ln_b", "b_spatial", "attn_sink")


def _small_step(gath, params):
    flat = [a for name in SMALL_ORDER for a in params[name]]

    def grad_of(tot, name):
        if name == "b_ada":
            return jnp.concatenate([tot[r:r + 1, :] for r in range(6)], axis=1)
        if name in ("ln1_g", "ln1_b", "ln2_g", "ln2_b"):
            r = 8 + ("ln1_g", "ln1_b", "ln2_g", "ln2_b").index(name)
            return tot[r:r + 1, :]
        if name == "gmlp_ln_g":
            return tot[12:13, :GM_W]
        if name == "gmlp_ln_b":
            return tot[12:13, GM_W:]
        if name == "b_spatial":
            return jnp.concatenate([tot[13:14, g * BLK:(g + 1) * BLK] for g in range(N_GROUPS)], axis=0)[None]
        return tot[14:15, :N_Q_HEADS]

    def body(*refs):
        g_ref, in_refs = refs[0], refs[1:1 + len(flat)]
        tot_ref, out_refs = refs[1 + len(flat)], refs[2 + len(flat):]
        tot = g_ref[0]
        for i in range(1, N_DEV):
            tot = tot + g_ref[i]
        tot_ref[...] = tot
        tot_ref[0:2, :] = tot[0:2, :] + tot[6:8, :]
        tot_ref[15:16, :] = jnp.broadcast_to(jnp.sum(tot[15:16, :], axis=1, keepdims=True), (1, D))
        tot = tot_ref[...]
        for k, name in enumerate(SMALL_ORDER):
            w_ref, m_ref, v_ref = in_refs[3 * k:3 * k + 3]
            g = grad_of(tot, name)
            delta, m2, v2 = _adamw(w_ref[...], g, m_ref[...], v_ref[...])
            for r, val in zip(out_refs[4 * k:4 * k + 4], (g, delta, m2, v2)):
                r[...] = val

    res = pl.pallas_call(
        body, name="small_step", grid=(1,),
        in_specs=[_full((N_DEV, 16, D))] + [_full(a.shape) for a in flat],
        out_specs=[_full((16, D))] + [_full(params[name][0].shape) for name in SMALL_ORDER for _ in range(4)],
        out_shape=[_sds((16, D), F32)] + [_sds(params[name][0].shape, F32) for name in SMALL_ORDER for _ in range(4)],
        compiler_params=_params(("arbitrary",)),
    )(gath, *flat)
    return res[0], {name: res[1 + 4 * k:5 + 4 * k] for k, name in enumerate(SMALL_ORDER)}


def _cctx_finish(gath, c_ctx, m, v):
    def body(g_ref, c_ref, m_ref, v_ref, gr_ref, d_ref, m2_ref, v2_ref):
        ds = g_ref[0]
        for i in range(1, N_DEV):
            ds = ds + g_ref[i]
        c = c_ref[...]
        sg = _sigmoid(c)
        g = ds * (sg * (1.0 + c * (1.0 - sg)))
        delta, m2, v2 = _adamw(c, g, m_ref[...], v_ref[...])
        gr_ref[...] = g
        d_ref[...] = delta
        m2_ref[...] = m2
        v2_ref[...] = v2

    return pl.pallas_call(
        body, name="cctx_finish", grid=(1,),
        in_specs=[_full((N_DEV, 8, D))] + [_full((8, D))] * 3, out_specs=[_full((8, D))] * 4,
        out_shape=[_sds((8, D), F32)] * 4,
        compiler_params=_params(("arbitrary",)),
    )(gath, c_ctx, m, v)


def _pad_rows(a, rows):
    return jnp.concatenate([a, jnp.zeros((rows - a.shape[0], a.shape[1]), a.dtype)], axis=0)


def kernel(x, c, ctx, c_ctx, w_ada, b_ada, w_in, attn_sink, gmlp_ln_g, gmlp_ln_b, w_spatial, b_spatial, w_branch_a, w_branch_b, w_out, ln1_g, ln1_b, w_ffn_in, w_ffn_out, ln2_g, ln2_b, loss_target, m_c_ctx, m_w_ada, m_b_ada, m_w_in, m_attn_sink, m_gmlp_ln_g, m_gmlp_ln_b, m_w_spatial, m_b_spatial, m_w_branch_a, m_w_branch_b, m_w_out, m_ln1_g, m_ln1_b, m_w_ffn_in, m_w_ffn_out, m_ln2_g, m_ln2_b, v_c_ctx, v_w_ada, v_b_ada, v_w_in, v_attn_sink, v_gmlp_ln_g, v_gmlp_ln_b, v_w_spatial, v_b_spatial, v_w_branch_a, v_w_branch_b, v_w_out, v_ln1_g, v_ln1_b, v_w_ffn_in, v_w_ffn_out, v_ln2_g, v_ln2_b):
    L = x.shape[1]
    me = 4 * lax.axis_index("x") + 2 * lax.axis_index("y") + lax.axis_index("c")
    x2, tgt, ctx2 = x[0], loss_target[0], ctx[0]
    tiles = _Tiles(L)
    tm_in, tm, tt = tiles.wide, tiles.narrow, tiles.tokens

    transposed = ("w_in", "w_ffn_in")
    tr = lambda kname, a: a.T if kname in transposed else a
    big = dict(w_in=w_in[0].T, w_branch_a=w_branch_a[0], w_branch_b=w_branch_b[0], w_out=w_out[0],
               w_ffn_in=w_ffn_in[0].T, w_ffn_out=w_ffn_out[0])
    col_sharded = ("w_branch_a", "w_branch_b")
    shard_bf = {k: a.astype(BF16) for k, a in big.items()}

    def assemble(kname, g):
        if kname in col_sharded:
            return g.transpose(1, 0, 2).reshape(g.shape[1], N_DEV * g.shape[2])
        return g.reshape(N_DEV * g.shape[1], g.shape[2])

    def to_blocks(kname, g):
        if kname in col_sharded:
            return g.reshape(g.shape[0], N_DEV, g.shape[1] // N_DEV).transpose(1, 0, 2)
        return g.reshape(N_DEV, g.shape[0] // N_DEV, g.shape[1])

    full = {}
    n_ada = w_ada.shape[2]
    b_my = lax.dynamic_slice(b_ada, (0, me * n_ada), (1, n_ada))
    act, mod_all, got = _prologue(_pad_rows(c, 8), _pad_rows(c_ctx[None, :], 8), w_ada[0], b_my,
                                  _Comm(gather=[shard_bf["w_in"]]))
    full["w_in"] = assemble("w_in", got[0])
    mod_all = mod_all.transpose(1, 0, 2).reshape(16, 6 * D)
    modv = _pad_rows(lax.dynamic_slice(mod_all, (me, 0), (1, 6 * D)).reshape(6, D), 8)
    modc = _pad_rows(mod_all[8].reshape(6, D), 8)

    lnv = _pad_rows(jnp.concatenate([ln1_g, ln1_b, ln2_g, ln2_b], axis=0), 8)
    gm_lnv = _pad_rows(jnp.concatenate([gmlp_ln_g, gmlp_ln_b], axis=0), 8)
    ws_b = w_spatial[0].astype(BF16)
    wst_b = ws_b.transpose(0, 2, 1)
    bsp = jnp.repeat(b_spatial[0].T, GROUP_DIM, axis=1)
    sink = attn_sink[0]
    cos, sin = _rope_tables(L)
    bias = _attn_bias()
    w_kv = full["w_in"][O_K:O_K + 2 * KV_W, :]

    (h, q, k, v, u, vb, ga, gb), got = _k_in(
        x2, modv, full["w_in"], cos, sin, tm_in,
        comm=_Comm(gather=[shard_bf[kname] for kname in ("w_branch_a", "w_branch_b", "w_out", "w_ffn_out")]))
    for kname, g in zip(("w_branch_a", "w_branch_b", "w_out", "w_ffn_out"), got):
        full[kname] = assemble(kname, g)
    hc, kc, vc = _k_ctx(ctx2, modc, w_kv)
    (ya, lse), got = _k_attn(sink, q, k, v, kc, vc, bias, comm=_Comm(gather=[shard_bf["w_ffn_in"]]))
    full["w_ffn_in"] = assemble("w_ffn_in", got[0])
    yb = _k_gmlp(u, vb, gm_lnv, ws_b, bsp)
    merged, mix, xm, h2 = _k_merge(x2, ya, yb, ga, gb, full["w_branch_a"], full["w_branch_b"], full["w_out"], modv, lnv, tm_in)
    gate, up, act_f, dr2, df, acc_f = _k_ffn(h2, xm, tgt, full["w_ffn_in"], full["w_ffn_out"], modv, lnv, tm_in)

    dF, dmix, dxp, acc_b = _k_ffn_bwd(df, gate, up, xm, dr2, x2, mix, full["w_ffn_in"], full["w_ffn_out"], modv, lnv, tm)
    blk_fo = to_blocks("w_ffn_out", _wgrad(act_f, df, "wgrad_ffn_out", tiles.tk_ffn, tt))
    gw_fi, (rcv_fo,) = _wgrad(dF, h2, "wgrad_ffn_in", tiles.tk_ffn, tt, comm=_Comm(scatter=[blk_fo]))
    blk_fi = to_blocks("w_ffn_in", gw_fi)
    dga, dgb, dya, dyb, gw_a, gw_b, gw_o = _k_merge_bwd(
        dmix, merged, ya, yb, ga, gb, full["w_branch_a"], full["w_branch_b"], full["w_out"], tm_in)
    du, dvb, g_ws, g_bst, g_gln = _k_gmlp_bwd(u, vb, dyb, gm_lnv, ws_b, wst_b, bsp)
    (dq, dk_late, dv_late, dkc, dvc, g_sink), (gath_ws, rcv_fi) = _k_attn_bwd(
        sink, q, k, v, kc, vc, dya, lse, cos, sin, bias,
        comm=_Comm(gather=[g_ws.reshape(N_GROUPS * BLK, BLK)], scatter=[blk_fi]))
    dk, dv = dk_late[BLK:BLK + L], dv_late[BLK:BLK + L]
    blk_a, blk_b, blk_o = to_blocks("w_branch_a", gw_a), to_blocks("w_branch_b", gw_b), to_blocks("w_out", gw_o)
    (dP, grad_x, acc_i), _ = _k_in_bwd(dq, dk, dv, du, dvb, dga, dgb, x2, dxp, full["w_in"], modv, tm_in)
    g_ctx, dmodc = _k_ctx_bwd(ctx2, modc, hc, dkc, dvc, w_kv)
    gw_in, (rcv_a, rcv_b, rcv_o) = _wgrad(dP, h, "wgrad_in", tiles.tk_in, tt, comm=_Comm(scatter=[blk_a, blk_b, blk_o]),
                                          extra=(O_K, g_ctx))

    dmod_x = jnp.concatenate([acc_i[0:2], acc_b[4:5], acc_b[0:2], acc_f[2:3]], axis=0)
    small = jnp.concatenate([
        dmod_x, dmodc[0:2], acc_b[2:4], acc_f[0:2],
        jnp.concatenate([g_gln[0:1], g_gln[1:2]], axis=1), g_bst.T.reshape(1, D),
        _pad_rows(g_sink[:, 0:1], D).T, acc_f[3:4]], axis=0)
    rcv_in, gath = _exchange_two_level(to_blocks("w_in", gw_in), small, "exchange_last")
    received = dict(w_in=rcv_in, w_branch_a=rcv_a, w_branch_b=rcv_b, w_out=rcv_o, w_ffn_in=rcv_fi, w_ffn_out=rcv_fo)
    moments = dict(w_in=(m_w_in, v_w_in), w_branch_a=(m_w_branch_a, v_w_branch_a), w_branch_b=(m_w_branch_b, v_w_branch_b),
                   w_out=(m_w_out, v_w_out), w_ffn_in=(m_w_ffn_in, v_w_ffn_in), w_ffn_out=(m_w_ffn_out, v_w_ffn_out))
    names = list(big)
    res = {}
    for kname in names:
        mm, vv = moments[kname]
        R = big[kname].shape[0]
        res[kname] = [tr(kname, r) for r in _adamw_reduce(
            received[kname], big[kname], tr(kname, mm[0]), tr(kname, vv[0]), "adamw_" + kname, 256 if R % 256 == 0 else R // 2)]

    ws2d = lambda a: a.reshape(N_GROUPS * BLK, BLK)
    res_ws = [r.reshape(w_spatial.shape) for r in _adamw_reduce(
        gath_ws, ws2d(w_spatial), ws2d(m_w_spatial), ws2d(v_w_spatial), "adamw_w_spatial", 256)]
    tot, res_small = _small_step(gath, dict(
        b_ada=(b_ada, m_b_ada, v_b_ada), ln1_g=(ln1_g, m_ln1_g, v_ln1_g), ln1_b=(ln1_b, m_ln1_b, v_ln1_b),
        ln2_g=(ln2_g, m_ln2_g, v_ln2_g), ln2_b=(ln2_b, m_ln2_b, v_ln2_b),
        gmlp_ln_g=(gmlp_ln_g, m_gmlp_ln_g, v_gmlp_ln_g), gmlp_ln_b=(gmlp_ln_b, m_gmlp_ln_b, v_gmlp_ln_b),
        b_spatial=(b_spatial, m_b_spatial, v_b_spatial), attn_sink=(attn_sink, m_attn_sink, v_attn_sink)))
    loss = tot[15, 0]

    dmod_rows = jnp.concatenate([gath[:, 0:6, :].reshape(N_DEV, 6 * D),
                                 jnp.concatenate([tot[6:8].reshape(1, 2 * D), jnp.zeros((1, 4 * D), F32)], axis=1),
                                 jnp.zeros((7, 6 * D), F32)], axis=0)
    dmod_my = lax.dynamic_slice(dmod_rows, (0, me * n_ada), (16, n_ada))
    g_wada, d_wada, m2_wada, v2_wada, pc = _ada_bwd(act, dmod_my, w_ada[0], m_w_ada[0], v_w_ada[0])
    pc_all = _gather_rows(pc, "gather_cctx")
    cc8 = lambda a: _pad_rows(a.reshape(1, D), 8)
    g_cc, d_cc, m2_cc, v2_cc = _cctx_finish(pc_all, cc8(c_ctx), cc8(m_c_ctx), cc8(v_c_ctx))

    order = ["c_ctx", "w_ada", "b_ada", "w_in", "attn_sink", "gmlp_ln_g", "gmlp_ln_b", "w_spatial", "b_spatial",
             "w_branch_a", "w_branch_b", "w_out", "ln1_g", "ln1_b", "w_ffn_in", "w_ffn_out", "ln2_g", "ln2_b"]
    grads, deltas, new_m, new_v = {}, {}, {}, {}
    grads["c_ctx"], deltas["c_ctx"], new_m["c_ctx"], new_v["c_ctx"] = g_cc[0], d_cc[0], m2_cc[0], v2_cc[0]
    grads["w_ada"], deltas["w_ada"], new_m["w_ada"], new_v["w_ada"] = g_wada[None], d_wada[None], m2_wada[None], v2_wada[None]
    for kname in names:
        g, d, m2, v2 = res[kname]
        grads[kname], deltas[kname], new_m[kname], new_v[kname] = g[None], d[None], m2[None], v2[None]
    grads["w_spatial"], deltas["w_spatial"], new_m["w_spatial"], new_v["w_spatial"] = res_ws
    for kname in SMALL_ORDER:
        grads[kname], deltas[kname], new_m[kname], new_v[kname] = res_small[kname]
    return (loss, grad_x[None], *[grads[n] for n in order], *[deltas[n] for n in order],
            *[new_m[n] for n in order], *[new_v[n] for n in order])
```

```python
import functools
import math

import jax
import jax.numpy as jnp
import numpy as np
from jax import lax
from jax.experimental import pallas as pl
from jax.experimental.pallas import tpu as pltpu
from jax.experimental.pallas import tpu_sc as plsc

F32 = jnp.float32
BF16 = jnp.bfloat16
MESH = pl.DeviceIdType.MESH

N_DEV = 8
D = 1024
HEAD_DIM = 64
N_Q_HEADS = 8
N_KV_HEADS = 2
GQA_GROUP = 4
BLK = 128
Q_W = 512
KV_W = 128
GM_W = 512
N_GROUPS = 8
GROUP_DIM = 64
FFN_H = 2816
IN_W = 3840
O_Q, O_K, O_V, O_U, O_VB, O_GA, O_GB = 0, 512, 640, 768, 1280, 1792, 2816
LN_EPS = 1e-5
NEG_INF = -1e30
ALPHA = 2.0 ** 0.25
ROPE_BASE = 10000.0
ROPE_PAIRS = 16
Q_SCALE = HEAD_DIM ** -0.5
GELU_K0 = math.sqrt(2.0 / math.pi)
GELU_K1 = 0.044715

ADAM_LR = 0.001
ADAM_B1 = 0.9
ADAM_B2 = 0.999
ADAM_EPS = 1e-08
ADAM_WD = 0.01
ADAM_STEP = 10

V7X_VMEM_BYTES = 64 * 1024 * 1024
VMEM_LIMIT = V7X_VMEM_BYTES * 7 // 8
NT = (((1,), (1,)), ((), ()))
TN = (((0,), (0,)), ((), ()))


class _Tiles:
    def __init__(self, L):
        self.wide = min(512, L)
        self.narrow = min(256, L)
        self.tokens = min(2048, L)
        self.tk_in = IN_W // 3
        self.tk_ffn = FFN_H // 2


def _params(sem=None):
    return pltpu.CompilerParams(dimension_semantics=sem, vmem_limit_bytes=VMEM_LIMIT)


def _row(tm, w):
    return pl.BlockSpec((tm, w), lambda i: (i, 0))


def _full(shape):
    nd = len(shape)
    return pl.BlockSpec(shape, lambda i: (0,) * nd)


def _resident(shape):
    nd = len(shape)
    return pl.BlockSpec(shape, lambda i: (0,) * nd, pipeline_mode=pl.Buffered(1))


def _sds(shape, dt):
    return jax.ShapeDtypeStruct(shape, dt)


def _ln(xf):
    mu = jnp.mean(xf, axis=-1, keepdims=True)
    xc = xf - mu
    var = jnp.mean(xc * xc, axis=-1, keepdims=True)
    rstd = lax.rsqrt(var + LN_EPS)
    return xc * rstd, rstd


def _ln_bwd(dn, n, rstd):
    m1 = jnp.mean(dn, axis=-1, keepdims=True)
    m2 = jnp.mean(dn * n, axis=-1, keepdims=True)
    return rstd * (dn - m1 - n * m2)


def _colsum(t):
    return jnp.sum(t, axis=0, keepdims=True)


def _sigmoid(x):
    return 0.5 * jnp.tanh(0.5 * x) + 0.5


def _gelu(x):
    t = jnp.tanh(x * (GELU_K0 + (GELU_K0 * GELU_K1) * (x * x)))
    h = 0.5 * x
    return h + h * t, t


def _gelu_grad(x, t):
    return 0.5 + 0.5 * t + (0.5 * x) * (1.0 - t * t) * (GELU_K0 + (3.0 * GELU_K0 * GELU_K1) * (x * x))


def _swap16(t):
    lane = lax.broadcasted_iota(jnp.int32, t.shape, 1)
    return jnp.where((lane & 16) == 0, pltpu.roll(t, 112, 1), pltpu.roll(t, 16, 1))


def _rope(t, cos, sin):
    return t * cos + _swap16(t) * sin


def _unrope(t, cos, sin):
    return t * cos - _swap16(t) * sin


def _adamw(w, g, m, v):
    m2 = ADAM_B1 * m + (1.0 - ADAM_B1) * g
    v2 = ADAM_B2 * v + (1.0 - ADAM_B2) * (g * g)
    m_hat = m2 / (1.0 - ADAM_B1 ** ADAM_STEP)
    v_hat = v2 / (1.0 - ADAM_B2 ** ADAM_STEP)
    delta = -ADAM_LR * (m_hat / (jnp.sqrt(v_hat) + ADAM_EPS) + ADAM_WD * w)
    return delta, m2, v2


def _rope_tables(L):
    inv = (np.float32(ROPE_BASE) ** (-np.arange(ROPE_PAIRS, dtype=np.float32) / np.float32(ROPE_PAIRS))).astype(np.float32)
    t = np.arange(L, dtype=np.int32)
    rows = (t // 64).astype(np.float32)[:, None] * inv
    cols = (t % 64).astype(np.float32)[:, None] * inv
    cr, sr, cc, sc = np.cos(rows), np.sin(rows), np.cos(cols), np.sin(cols)
    cos = np.concatenate([cr, cr, cc, cc], axis=1)
    sin = np.concatenate([-sr, sr, -sc, sc], axis=1)
    return jnp.asarray(np.tile(cos, (1, 2)), F32), jnp.asarray(np.tile(sin, (1, 2)), F32)


def _me():
    return lax.axis_index("x"), lax.axis_index("y"), lax.axis_index("c")


def _peer(mx, my, mc, k):
    return (mx ^ ((k >> 2) & 1), my ^ ((k >> 1) & 1), mc ^ (k & 1))


class _Comm:
    def __init__(self, gather=(), scatter=(), spread=()):
        self.kinds = ["gather"] * len(gather) + ["scatter"] * len(scatter) + ["spread"] * len(spread)
        self.args = list(gather) + list(scatter) + list(spread)
        self.n = len(self.args)

    def out_shape(self):
        return [_sds(a.shape if k == "scatter" else (N_DEV,) + a.shape, a.dtype) for k, a in zip(self.kinds, self.args)]

    def specs(self):
        return [pl.BlockSpec(memory_space=pl.ANY)] * self.n

    def scratch(self):
        return [pltpu.SemaphoreType.DMA((7 * self.n,)), pltpu.SemaphoreType.DMA((7 * self.n,)),
                pltpu.SemaphoreType.DMA((self.n,))]

    def _plan(self, x_refs, out_refs, send_sems, recv_sems, local_sems):
        mx, my, mc = _me()
        me = 4 * mx + 2 * my + mc
        here, sibling = (mx, my, mc), (mx, my, 1 - mc)
        chips = [(1 - mx, my), (mx, 1 - my), (1 - mx, 1 - my)]
        local, first, last = [], [], []
        relay = [[], [], []]
        for a, kind in enumerate(self.kinds):
            x, out = x_refs[a], out_refs[a]

            def rc(k, src, dst, to):
                return pltpu.make_async_remote_copy(
                    src_ref=src, dst_ref=dst, send_sem=send_sems.at[7 * a + k], recv_sem=recv_sems.at[7 * a + k],
                    device_id=to, device_id_type=MESH)

            if kind == "gather":
                local.append(pltpu.make_async_copy(x, out.at[me], local_sems.at[a]))
                first.append(rc(0, x, out.at[me], sibling))
                last.append(rc(0, x, out.at[me ^ 1], here))
                for j, (cx, cy) in enumerate(chips):
                    first.append(rc(1 + j, x, out.at[me], (cx, cy, mc)))
                    landed = out.at[4 * cx + 2 * cy + mc]
                    relay[j].append((rc(1 + j, x, landed, here), rc(4 + j, landed, landed, sibling)))
                    last.append(rc(4 + j, x, out.at[4 * cx + 2 * cy + 1 - mc], here))
            else:
                own = x.at[me] if kind == "scatter" else x
                local.append(pltpu.make_async_copy(own, out.at[me], local_sems.at[a]))
                for k in range(1, N_DEV):
                    src = x.at[me ^ k] if kind == "scatter" else x
                    first.append(rc(k - 1, src, out.at[me], _peer(mx, my, mc, k)))
                    last.append(rc(k - 1, own, out.at[me ^ k], here))
        return local, first, relay[0] + relay[1] + relay[2], last

    def start(self, *refs):
        local, first, _, _ = self._plan(*refs)
        for cp in local + first:
            cp.start()

    def relay(self, *refs):
        _, _, relay, _ = self._plan(*refs)
        for arrival, onward in relay:
            arrival.wait_recv()
            onward.start()

    def finish(self, *refs):
        local, first, relay, last = self._plan(*refs)
        for cp in last:
            cp.wait_recv()
        for cp in first:
            cp.wait_send()
        for _, onward in relay:
            onward.wait_send()
        for cp in local:
            cp.wait()


def _call(body, *, name, grid, in_specs, out_specs, out_shape, args, scratch=(), comm=None, aliases=None):
    params = _params(("arbitrary",) * len(grid))
    total = math.prod(grid)

    def at(step):
        flat = functools.reduce(lambda acc, dn: acc * dn[1] + pl.program_id(dn[0]), enumerate(grid), 0)
        return flat == step

    if comm is None:
        res = pl.pallas_call(
            body, name=name, grid=grid, in_specs=list(in_specs), out_specs=list(out_specs), out_shape=list(out_shape),
            scratch_shapes=list(scratch), input_output_aliases=aliases or {}, compiler_params=params)(*args)
        return list(res), []
    n_in, n_out, n_scr, cn = len(in_specs), len(out_specs), len(scratch), comm.n

    def hosted(*refs):
        ins, refs = refs[:n_in], refs[n_in:]
        cins, refs = refs[:cn], refs[cn:]
        outs, refs = refs[:n_out], refs[n_out:]
        couts, refs = refs[:cn], refs[cn:]
        scr, sems = refs[:n_scr], refs[n_scr:]

        @pl.when(at(0))
        def _():
            comm.start(cins, couts, *sems)

        body(*ins, *outs, *scr)

        @pl.when(at((3 * total) // 4 if total >= 4 else total - 1))
        def _():
            comm.relay(cins, couts, *sems)

        @pl.when(at(total - 1))
        def _():
            comm.finish(cins, couts, *sems)

    res = pl.pallas_call(
        hosted, name=name, grid=grid, in_specs=list(in_specs) + comm.specs(), out_specs=list(out_specs) + comm.specs(),
        out_shape=list(out_shape) + comm.out_shape(), scratch_shapes=list(scratch) + comm.scratch(),
        input_output_aliases=aliases or {}, compiler_params=params)(*args, *comm.args)
    return list(res[:n_out]), list(res[n_out:])


def _sc_chip_exchange(t):
    hbm = pltpu.MemorySpace.HBM
    t_ref = jax.new_ref(t, memory_space=hbm)
    got_ref = jax.empty_ref(jax.ShapeDtypeStruct((3,) + t.shape[1:], t.dtype), memory_space=hbm)
    dma = pltpu.SemaphoreType.DMA

    @pl.kernel(mesh=plsc.ScalarSubcoreMesh(axis_name="seq", num_cores=1), name="sc_chip_exchange",
               scratch_types=(dma, dma, dma, dma, dma, dma), compiler_params=pltpu.CompilerParams(collective_id=3))
    def launch(s1, s2, s3, r1, r2, r3):
        mx, my, mc = _me()
        peers = [(mx ^ (k >> 1), my ^ (k & 1), mc) for k in range(1, 4)]
        barrier = pltpu.get_barrier_semaphore()
        for peer in peers:
            pl.semaphore_signal(barrier, inc=1, device_id=peer, device_id_type=MESH)
        pl.semaphore_wait(barrier, 3)
        copies = [pltpu.make_async_remote_copy(src_ref=t_ref.at[k + 1], dst_ref=got_ref.at[k], send_sem=s, recv_sem=r,
                                               device_id=peer, device_id_type=MESH)
                  for k, (peer, s, r) in enumerate(zip(peers, (s1, s2, s3), (r1, r2, r3)))]
        for cp in copies:
            cp.start()
        for cp in copies:
            cp.wait()

    launch()
    return got_ref[...]


def _exchange_two_level(blk, small, name, ici=True):
    _, R, C = blk.shape
    rows = small.shape[0]

    def body(blk_ref, small_ref, stage_ref, out_ref, gath_ref, a_scr, b_scr, t_scr, s1, r1, s3, r3, ss, rs, lsem):
        mx, my, mc = _me()
        me = 4 * mx + 2 * my + mc
        mine = 2 * mx + my
        here, sibling = (mx, my, mc), (mx, my, 1 - mc)

        def rc(src, dst, send, recv, to):
            return pltpu.make_async_remote_copy(src_ref=src, dst_ref=dst, send_sem=send, recv_sem=recv,
                                                device_id=to, device_id_type=MESH)

        own_small = pltpu.make_async_copy(small_ref, gath_ref.at[me], lsem.at[0])
        own_small.start()
        spread = [rc(small_ref, gath_ref.at[me], ss.at[k - 1], rs.at[k - 1], _peer(mx, my, mc, k)) for k in range(1, N_DEV)]
        order = (1, 2, 3, 0)
        to_sib = [rc(blk_ref.at[2 * (mine ^ k) + 1 - mc], stage_ref.at[k], s1.at[k], r1.at[k], sibling) for k in order]
        for cp in spread + to_sib:
            cp.start()
        own = {k: pltpu.make_async_copy(blk_ref.at[2 * (mine ^ k) + mc], a_scr.at[k], lsem.at[1 + k]) for k in order}
        for k in order:
            own[k].start()
        onward = []
        for k in order:
            rc(blk_ref.at[0], stage_ref.at[k], s1.at[k], r1.at[k], here).wait_recv()
            landed = pltpu.make_async_copy(stage_ref.at[k], b_scr.at[k], lsem.at[5 + k])
            landed.start()
            landed.wait()
            own[k].wait()
            t_scr[k] = (a_scr[k].astype(F32) + b_scr[k].astype(F32)).astype(BF16)
            if k > 0 and ici:
                cp = rc(t_scr.at[k], out_ref.at[mine], s3.at[k - 1], r3.at[k - 1], (mx ^ (k >> 1), my ^ (k & 1), mc))
                cp.start()
                onward.append(cp)
        if ici:
            keep = pltpu.make_async_copy(t_scr.at[0], out_ref.at[mine], lsem.at[9])
        else:
            keep = pltpu.make_async_copy(t_scr, out_ref, lsem.at[9])
        keep.start()
        for k in range(1, 4 if ici else 1):
            rc(t_scr.at[0], out_ref.at[mine ^ k], s3.at[k - 1], r3.at[k - 1], here).wait_recv()
        for k in range(1, N_DEV):
            rc(small_ref, gath_ref.at[me ^ k], ss.at[k - 1], rs.at[k - 1], here).wait_recv()
        for cp in spread + to_sib + onward:
            cp.wait_send()
        keep.wait()
        own_small.wait()

    any_spec = pl.BlockSpec(memory_space=pl.ANY)
    dma = pltpu.SemaphoreType.DMA
    _, out, gath = pl.pallas_call(
        body, name=name,
        in_specs=[any_spec, any_spec], out_specs=[any_spec] * 3,
        out_shape=[_sds((4, R, C), BF16), _sds((4, R, C), BF16), _sds((N_DEV, rows, D), F32)],
        scratch_shapes=[pltpu.VMEM((4, R, C), BF16)] * 3
                       + [dma((4,)), dma((4,)), dma((3,)), dma((3,)), dma((N_DEV - 1,)), dma((N_DEV - 1,)), dma((10,))],
        compiler_params=pltpu.CompilerParams(vmem_limit_bytes=VMEM_LIMIT),
    )(blk, small)
    return out, gath


def _exchange_rows(x_ref, out_ref, send_sems, recv_sems, between=None):
    mx, my, mc = _me()
    me = 4 * mx + 2 * my + mc
    out_ref[pl.ds(me, 1)] = x_ref[...][None]
    sends = []
    for k in range(1, N_DEV):
        cp = pltpu.make_async_remote_copy(
            src_ref=x_ref, dst_ref=out_ref.at[me], send_sem=send_sems.at[k - 1], recv_sem=recv_sems.at[k - 1],
            device_id=_peer(mx, my, mc, k), device_id_type=MESH)
        cp.start()
        sends.append(cp)
    if between is not None:
        between()
    for k in range(1, N_DEV):
        pltpu.make_async_remote_copy(
            src_ref=x_ref, dst_ref=out_ref.at[me ^ k], send_sem=send_sems.at[k - 1], recv_sem=recv_sems.at[k - 1],
            device_id=(mx, my, mc), device_id_type=MESH).wait_recv()
    for cp in sends:
        cp.wait_send()


def _prologue(c8, cctx8, w_ada, b_my, comm):
    nw = w_ada.shape[1]

    cn = comm.n

    def body(*refs):
        c_ref, cctx_ref, w_ref, b_ref = refs[:4]
        cins, refs = refs[4:4 + cn], refs[4 + cn:]
        act_ref, mod_ref = refs[:2]
        couts, refs = refs[2:2 + cn], refs[2 + cn:]
        cmine_scr, call_scr, mine_scr, mall_scr, s1, r1, s2, r2 = refs[:8]
        csems = refs[8:]
        cmine_scr[...] = c_ref[...]
        _exchange_rows(cmine_scr, call_scr, s1, r1)
        rows = [call_scr[d][0:1, :] for d in range(N_DEV)] + [cctx_ref[0:1, :], jnp.zeros((7, D), F32)]
        s = jnp.concatenate(rows, axis=0)
        act = s * _sigmoid(s)
        act_ref[...] = act
        mine_scr[...] = jnp.dot(act.astype(BF16), w_ref[...].astype(BF16), preferred_element_type=F32) + b_ref[...]
        _exchange_rows(mine_scr, mall_scr, s2, r2, between=lambda: comm.start(cins, couts, *csems))
        mod_ref[...] = mall_scr[...]
        comm.relay(cins, couts, *csems)
        comm.finish(cins, couts, *csems)

    sems = [pltpu.SemaphoreType.DMA((N_DEV - 1,))] * 4
    res = pl.pallas_call(
        body, name="prologue", grid=(1,),
        in_specs=[_full((8, D)), _full((8, D)), _full((D, nw)), _full((1, nw))] + comm.specs(),
        out_specs=[_full((16, D)), _full((N_DEV, 16, nw))] + comm.specs(),
        out_shape=[_sds((16, D), F32), _sds((N_DEV, 16, nw), F32)] + comm.out_shape(),
        scratch_shapes=[pltpu.VMEM((8, D), F32), pltpu.VMEM((N_DEV, 8, D), F32), pltpu.VMEM((16, nw), F32),
                        pltpu.VMEM((N_DEV, 16, nw), F32)] + sems + comm.scratch(),
        compiler_params=_params(("arbitrary",)),
    )(c8, cctx8, w_ada, b_my, *comm.args)
    return res[0], res[1], list(res[2:])


def _gather_rows(x, name):
    def body(x_ref, out_ref, send_sems, recv_sems):
        _exchange_rows(x_ref, out_ref, send_sems, recv_sems)

    return pl.pallas_call(
        body, name=name,
        out_shape=_sds((N_DEV,) + x.shape, x.dtype),
        in_specs=[pl.BlockSpec(memory_space=pltpu.VMEM)],
        out_specs=pl.BlockSpec(memory_space=pltpu.VMEM),
        scratch_shapes=[pltpu.SemaphoreType.DMA((N_DEV - 1,)), pltpu.SemaphoreType.DMA((N_DEV - 1,))],
        compiler_params=pltpu.CompilerParams(vmem_limit_bytes=VMEM_LIMIT),
    )(x)


def _ada_bwd(act, dmod_my, w_ada, m, v, tr=256):
    nw = w_ada.shape[1]

    def body(act_ref, dm_ref, w_ref, m_ref, v_ref, g_ref, d_ref, m2_ref, v2_ref, pc_ref):
        dm = dm_ref[...].astype(BF16)
        g = lax.dot_general(act_ref[...].astype(BF16), dm, TN, preferred_element_type=F32)
        w = w_ref[...]
        delta, m2, v2 = _adamw(w, g, m_ref[...], v_ref[...])
        g_ref[...] = g
        d_ref[...] = delta
        m2_ref[...] = m2
        v2_ref[...] = v2
        pc_ref[...] = lax.dot_general(dm[8:16, :], w.astype(BF16), NT, preferred_element_type=F32)

    wspec = _row(tr, nw)
    return pl.pallas_call(
        body, name="ada_bwd", grid=(D // tr,),
        in_specs=[pl.BlockSpec((16, tr), lambda i: (0, i)), _full((16, nw)), wspec, wspec, wspec],
        out_specs=[wspec, wspec, wspec, wspec, pl.BlockSpec((8, tr), lambda i: (0, i))],
        out_shape=[_sds((D, nw), F32)] * 4 + [_sds((8, D), F32)],
        compiler_params=_params(("arbitrary",)),
    )(act, dmod_my, w_ada, m, v)


def _k_in(x, modv, w_in, cos, sin, tm, comm=None):
    L = x.shape[0]

    def body(x_ref, mod_ref, w_ref, cos_ref, sin_ref, h_ref, q_ref, k_ref, v_ref, u_ref, vb_ref, ga_ref, gb_ref):
        n, _ = _ln(x_ref[...])
        h = (n * (1.0 + mod_ref[1:2, :]) + mod_ref[0:1, :]).astype(BF16)
        h_ref[...] = h
        c, s = cos_ref[...], sin_ref[...]

        def proj(lo, width):
            return lax.dot_general(h, w_ref[lo:lo + width, :], NT, preferred_element_type=F32)

        for i in range(2):
            qh = proj(O_Q + i * 256, 256)
            for j in range(2):
                q_ref[:, i * 256 + j * 128:i * 256 + (j + 1) * 128] = (
                    _rope(qh[:, j * 128:(j + 1) * 128], c, s) * Q_SCALE).astype(BF16)
        kv = proj(O_K, 2 * KV_W)
        k_ref[...] = _rope(kv[:, :KV_W], c, s).astype(BF16)
        v_ref[...] = kv[:, KV_W:].astype(BF16)
        u_ref[...] = proj(O_U, GM_W).astype(BF16)
        vb_ref[...] = proj(O_VB, GM_W).astype(BF16)
        ga_ref[...] = proj(O_GA, D).astype(BF16)
        gb_ref[...] = proj(O_GB, D).astype(BF16)

    widths = [D, Q_W, KV_W, KV_W, GM_W, GM_W, D, D]
    return _call(
        body, name="fwd_in", grid=(L // tm,),
        in_specs=[_row(tm, D), _full((8, D)), _resident((IN_W, D)), _row(tm, 128), _row(tm, 128)],
        out_specs=[_row(tm, w) for w in widths],
        out_shape=[_sds((L, w), BF16) for w in widths],
        args=(x, modv, w_in, cos, sin), comm=comm)


def _k_ctx(ctx, modc, w_kv):
    C = ctx.shape[0]

    def body(c_ref, mod_ref, w_ref, hc_ref, kc_ref, vc_ref):
        n, _ = _ln(c_ref[...])
        hc = (n * (1.0 + mod_ref[1:2, :]) + mod_ref[0:1, :]).astype(BF16)
        hc_ref[...] = hc
        kv = lax.dot_general(hc, w_ref[...], NT, preferred_element_type=F32)
        kc_ref[...] = kv[:, :KV_W].astype(BF16)
        vc_ref[...] = kv[:, KV_W:].astype(BF16)

    return pl.pallas_call(
        body, name="fwd_ctx", grid=(1,),
        in_specs=[_full((C, D)), _full((8, D)), _full((2 * KV_W, D))],
        out_specs=[_full((C, D)), _full((C, KV_W)), _full((C, KV_W))],
        out_shape=[_sds((C, D), BF16), _sds((C, KV_W), BF16), _sds((C, KV_W), BF16)],
        compiler_params=_params(("arbitrary",)),
    )(ctx, modc, w_kv)


def _attn_bias():
    r = (np.arange(GQA_GROUP * BLK) & (BLK - 1))[:, None]
    j = np.arange(3 * BLK)[None, :]
    band = np.abs(j - BLK - r) <= BLK
    variants = [band & (j >= BLK), band, band & (j < 2 * BLK)]
    return jnp.asarray(np.stack([np.where(v, 0.0, NEG_INF) for v in variants]), F32)


def _masked(s, bias, C):
    return jnp.concatenate([s[:, :C], s[:, C:] + bias], axis=1)


def _sink_col(sink_ref, hk):
    grp = lax.broadcasted_iota(jnp.int32, (GQA_GROUP * BLK, 1), 0) >> 7
    col = jnp.full((GQA_GROUP * BLK, 1), sink_ref[hk * GQA_GROUP], F32)
    for g in range(1, GQA_GROUP):
        col = jnp.where(grp == g, sink_ref[hk * GQA_GROUP + g], col)
    return col


ATTN_FWD_BLOCKS = 4


def _k_attn(sink, q, k, v, kc, vc, bias, comm=None):
    L = q.shape[0]
    C = kc.shape[0]
    nb = L // BLK
    nq = min(ATTN_FWD_BLOCKS, nb)
    steps = nb // nq

    def body(sink_ref, q_ref, kp_ref, km_ref, kx_ref, vp_ref, vm_ref, vx_ref, kc_ref, vc_ref, bias_ref, ya_ref, lse_ref):
        i = pl.program_id(0)
        chains = [(qb, hk) for qb in range(nq) for hk in range(N_KV_HEADS)]

        def band(qb):
            first = jnp.where(i == 0, 0, 1) if qb == 0 else 1
            return bias_ref[jnp.where(i == steps - 1, 2, first) if qb == nq - 1 else first]

        def keys(ctx_ref, p_ref, m_ref, x_ref, qb, hk):
            sl = slice(hk * HEAD_DIM, (hk + 1) * HEAD_DIM)
            blocks = [p_ref[:, sl]] + [m_ref[j * BLK:(j + 1) * BLK, sl] for j in range(nq)] + [x_ref[:, sl]]
            return jnp.concatenate([ctx_ref[:, sl]] + blocks[qb:qb + 3], axis=0)

        def queries(qb, hk):
            return jnp.concatenate(
                [q_ref[qb * BLK:(qb + 1) * BLK, (hk * GQA_GROUP + g) * HEAD_DIM:(hk * GQA_GROUP + g + 1) * HEAD_DIM]
                 for g in range(GQA_GROUP)], axis=0)

        def scores(qb, hk):
            return _masked(lax.dot_general(queries(qb, hk), keys(kc_ref, kp_ref, km_ref, kx_ref, qb, hk), NT,
                                           preferred_element_type=F32), band(qb), C)

        ahead = 2
        s = [scores(*c) for c in chains[:ahead]]
        for n, (qb, hk) in enumerate(chains):
            if n + ahead < len(chains):
                s.append(scores(*chains[n + ahead]))
            s_ = s[n]
            sink_c = _sink_col(sink_ref, hk)
            m = jnp.maximum(jnp.max(s_, axis=1, keepdims=True), sink_c)
            p = jnp.exp(s_ - m)
            den = jnp.sum(p, axis=1, keepdims=True) + jnp.exp(sink_c - m)
            o = jnp.dot(p.astype(BF16), keys(vc_ref, vp_ref, vm_ref, vx_ref, qb, hk), preferred_element_type=F32) * (1.0 / den)
            lse = m + jnp.log(den)
            rows = slice(qb * BLK, (qb + 1) * BLK)
            for g in range(GQA_GROUP):
                h = hk * GQA_GROUP + g
                ya_ref[rows, h * HEAD_DIM:(h + 1) * HEAD_DIM] = o[g * BLK:(g + 1) * BLK, :].astype(BF16)
                lse_ref[rows, h:h + 1] = lse[g * BLK:(g + 1) * BLK, :]

    kv3 = [pl.BlockSpec((BLK, KV_W), lambda i: (jnp.maximum(nq * i - 1, 0), 0)),
           pl.BlockSpec((nq * BLK, KV_W), lambda i: (i, 0)),
           pl.BlockSpec((BLK, KV_W), lambda i: (jnp.minimum(nq * i + nq, nb - 1), 0))]
    return _call(
        body, name="fwd_attn", grid=(steps,),
        in_specs=[pl.BlockSpec(memory_space=pltpu.SMEM), _row(nq * BLK, Q_W)] + kv3 + kv3
                 + [_full((C, KV_W)), _full((C, KV_W)), _full((3, GQA_GROUP * BLK, 3 * BLK))],
        out_specs=[_row(nq * BLK, Q_W), _row(nq * BLK, N_Q_HEADS)],
        out_shape=[_sds((L, Q_W), BF16), _sds((L, N_Q_HEADS), F32)],
        args=(sink, q, k, k, k, v, v, v, kc, vc, bias), comm=comm)


GMLP_CHUNKS = 4


def _split_pair(t):
    low = lax.broadcasted_iota(jnp.int32, t.shape, 1) < GROUP_DIM
    zero = jnp.zeros_like(t)
    return jnp.where(low, t, zero), jnp.where(low, zero, t)


def _gmlp_spatial(w_ref, t_b, nch):
    rows = []
    for c in range(nch):
        tiles = []
        for pr in range(N_GROUPS // 2):
            lo, hi = _split_pair(t_b[c * BLK:(c + 1) * BLK, pr * 128:(pr + 1) * 128])
            tiles.append(jnp.dot(w_ref[2 * pr], lo, preferred_element_type=F32)
                         + jnp.dot(w_ref[2 * pr + 1], hi, preferred_element_type=F32))
        rows.append(jnp.concatenate(tiles, axis=1))
    return jnp.concatenate(rows, axis=0)


def _gmlp_fwd_vals(u, vb, lnv_ref, ws_ref, bsp_ref, nch):
    uf = u.astype(F32)
    vf = vb.astype(F32)
    gu, tu = _gelu(uf)
    gv, tv = _gelu(vf)
    vhat, rstd = _ln(gv)
    vn = (vhat * lnv_ref[0:1, :] + lnv_ref[1:2, :]).astype(BF16)
    s = _gmlp_spatial(ws_ref, vn, nch) + jnp.concatenate([bsp_ref[...]] * nch, axis=0)
    return uf, vf, gu, tu, tv, vhat, rstd, vn, s


def _k_gmlp(u, vb, lnv, ws, bsp):
    L = u.shape[0]
    nch = min(GMLP_CHUNKS, L // BLK)
    tm = nch * BLK

    def body(u_ref, vb_ref, lnv_ref, ws_ref, bsp_ref, yb_ref):
        _, _, gu, _, _, _, _, _, s = _gmlp_fwd_vals(u_ref[...], vb_ref[...], lnv_ref, ws_ref, bsp_ref, nch)
        yb_ref[...] = (gu * s).astype(BF16)

    return pl.pallas_call(
        body, name="fwd_gmlp", grid=(L // tm,),
        in_specs=[_row(tm, GM_W), _row(tm, GM_W), _full((8, GM_W)), _full((N_GROUPS, BLK, BLK)), _full((BLK, GM_W))],
        out_specs=_row(tm, GM_W),
        out_shape=_sds((L, GM_W), BF16),
        compiler_params=_params(("arbitrary",)),
    )(u, vb, lnv, ws, bsp)


def _k_merge(x, ya, yb, ga, gb, w_a, w_b, w_o, modv, lnv, tm):
    L = x.shape[0]

    def body(x_ref, ya_ref, yb_ref, ga_ref, gb_ref, wa_ref, wb_ref, wo_ref, mod_ref, ln_ref,
             mg_ref, mix_ref, xm_ref, h2_ref):
        a = jnp.dot(ya_ref[...], wa_ref[...], preferred_element_type=F32)
        b = jnp.dot(yb_ref[...], wb_ref[...], preferred_element_type=F32)
        merged = (_sigmoid(ga_ref[...].astype(F32)) * a + _sigmoid(gb_ref[...].astype(F32)) * b).astype(BF16)
        mg_ref[...] = merged
        mix = jnp.dot(merged, wo_ref[...], preferred_element_type=F32)
        mix_ref[...] = mix.astype(BF16)
        r1 = ALPHA * x_ref[...] + mod_ref[2:3, :] * mix
        r1hat, _ = _ln(r1)
        xm = r1hat * ln_ref[0:1, :] + ln_ref[1:2, :]
        xm_ref[...] = xm
        n2, _ = _ln(xm)
        h2_ref[...] = (n2 * (1.0 + mod_ref[4:5, :]) + mod_ref[3:4, :]).astype(BF16)

    return pl.pallas_call(
        body, name="fwd_merge", grid=(L // tm,),
        in_specs=[_row(tm, D), _row(tm, Q_W), _row(tm, GM_W), _row(tm, D), _row(tm, D),
                  _resident((Q_W, D)), _resident((GM_W, D)), _resident((D, D)), _full((8, D)), _full((8, D))],
        out_specs=[_row(tm, D)] * 4,
        out_shape=[_sds((L, D), BF16), _sds((L, D), BF16), _sds((L, D), F32), _sds((L, D), BF16)],
        compiler_params=_params(("arbitrary",)),
    )(x, ya, yb, ga, gb, w_a, w_b, w_o, modv, lnv)


V7X_MXU_COLUMNS = 256
FFN_CHUNK = V7X_MXU_COLUMNS


def _k_ffn(h2, xm, tgt, w_fi, w_fo, modv, lnv, tm):
    L = h2.shape[0]

    def body(h2_ref, xm_ref, t_ref, wi_ref, wo_ref, mod_ref, ln_ref, gate_ref, up_ref, a_ref, dr2_ref, df_ref, acc_ref):
        @pl.when(pl.program_id(0) == 0)
        def _():
            acc_ref[...] = jnp.zeros_like(acc_ref)

        h2v = h2_ref[...]
        ch = FFN_CHUNK
        chunks = [j * ch for j in range(FFN_H // ch)]

        def project(lo):
            return (lax.dot_general(h2v, wi_ref[lo:lo + ch, :], NT, preferred_element_type=F32),
                    lax.dot_general(h2v, wi_ref[FFN_H + lo:FFN_H + lo + ch, :], NT, preferred_element_type=F32))

        f = jnp.zeros((tm, D), F32)
        ahead = [project(chunks[0])]
        for j, lo in enumerate(chunks):
            if j + 1 < len(chunks):
                ahead.append(project(chunks[j + 1]))
            gate, up = ahead[j]
            act = (gate * _sigmoid(gate) * up).astype(BF16)
            gate_ref[:, lo:lo + ch] = gate.astype(BF16)
            up_ref[:, lo:lo + ch] = up.astype(BF16)
            a_ref[:, lo:lo + ch] = act
            f = f + jnp.dot(act, wo_ref[lo:lo + ch, :], preferred_element_type=F32)
        gate2 = mod_ref[5:6, :]
        r2 = ALPHA * xm_ref[...] + gate2 * f
        r2hat, rstd = _ln(r2)
        y = r2hat * ln_ref[2:3, :] + ln_ref[3:4, :]
        err = y - t_ref[...]
        dy = err * (1.0 / D)
        dr2 = _ln_bwd(dy * ln_ref[2:3, :], r2hat, rstd)
        dr2_ref[...] = dr2
        df_ref[...] = (gate2 * dr2).astype(BF16)
        acc_ref[0:1, :] += _colsum(dy * r2hat)
        acc_ref[1:2, :] += _colsum(dy)
        acc_ref[2:3, :] += _colsum(dr2 * f)
        acc_ref[3:4, :] += _colsum(err * err) * (0.5 / D)

    return pl.pallas_call(
        body, name="fwd_ffn", grid=(L // tm,),
        in_specs=[_row(tm, D), _row(tm, D), _row(tm, D), _resident((2 * FFN_H, D)), _resident((FFN_H, D)),
                  _full((8, D)), _full((8, D))],
        out_specs=[_row(tm, FFN_H)] * 3 + [_row(tm, D), _row(tm, D), _full((8, D))],
        out_shape=[_sds((L, FFN_H), BF16)] * 3 + [_sds((L, D), F32), _sds((L, D), BF16), _sds((8, D), F32)],
        compiler_params=_params(("arbitrary",)),
    )(h2, xm, tgt, w_fi, w_fo, modv, lnv)


def _k_ffn_bwd(df, gate, up, xm, dr2, x, mix, w_fi, w_fo, modv, lnv, tm):
    L = df.shape[0]

    def body(df_ref, gate_ref, up_ref, xm_ref, dr2_ref, x_ref, mix_ref, wi_ref, wo_ref, mod_ref, ln_ref,
             dF_ref, dmix_ref, dxp_ref, acc_ref):
        @pl.when(pl.program_id(0) == 0)
        def _():
            acc_ref[...] = jnp.zeros_like(acc_ref)

        dfv = df_ref[...]
        ch = FFN_CHUNK
        chunks = [j * ch for j in range(FFN_H // ch)]

        def d_act(lo):
            return lax.dot_general(dfv, wo_ref[lo:lo + ch, :], NT, preferred_element_type=F32)

        n2, rstd2 = _ln(xm_ref[...])
        mixf = mix_ref[...].astype(F32)
        gate1 = mod_ref[2:3, :]
        r1hat, rstd1 = _ln(ALPHA * x_ref[...] + gate1 * mixf)
        dh2 = jnp.zeros((tm, D), F32)
        das = [d_act(chunks[0])]
        for j, lo in enumerate(chunks):
            if j + 1 < len(chunks):
                das.append(d_act(chunks[j + 1]))
            da = das[j]
            gate = gate_ref[:, lo:lo + ch].astype(F32)
            upv = up_ref[:, lo:lo + ch].astype(F32)
            sg = _sigmoid(gate)
            d_gate = (da * upv * (sg * (1.0 + gate * (1.0 - sg)))).astype(BF16)
            d_up = (da * (gate * sg)).astype(BF16)
            dF_ref[:, lo:lo + ch] = d_gate
            dF_ref[:, FFN_H + lo:FFN_H + lo + ch] = d_up
            dh2 = dh2 + jnp.dot(d_gate, wi_ref[lo:lo + ch, :], preferred_element_type=F32)
            dh2 = dh2 + jnp.dot(d_up, wi_ref[FFN_H + lo:FFN_H + lo + ch, :], preferred_element_type=F32)
        acc_ref[0:1, :] += _colsum(dh2)
        acc_ref[1:2, :] += _colsum(dh2 * n2)
        dxm = ALPHA * dr2_ref[...] + _ln_bwd(dh2 * (1.0 + mod_ref[4:5, :]), n2, rstd2)
        acc_ref[2:3, :] += _colsum(dxm * r1hat)
        acc_ref[3:4, :] += _colsum(dxm)
        dr1 = _ln_bwd(dxm * ln_ref[0:1, :], r1hat, rstd1)
        dmix_ref[...] = (gate1 * dr1).astype(BF16)
        dxp_ref[...] = ALPHA * dr1
        acc_ref[4:5, :] += _colsum(dr1 * mixf)

    return pl.pallas_call(
        body, name="bwd_ffn", grid=(L // tm,),
        in_specs=[_row(tm, D), _row(tm, FFN_H), _row(tm, FFN_H), _row(tm, D), _row(tm, D), _row(tm, D), _row(tm, D),
                  _resident((2 * FFN_H, D)), _resident((FFN_H, D)), _full((8, D)), _full((8, D))],
        out_specs=[_row(tm, 2 * FFN_H), _row(tm, D), _row(tm, D), _full((8, D))],
        out_shape=[_sds((L, 2 * FFN_H), BF16), _sds((L, D), BF16), _sds((L, D), F32), _sds((8, D), F32)],
        compiler_params=_params(("arbitrary",)),
    )(df, gate, up, xm, dr2, x, mix, w_fi, w_fo, modv, lnv)


def _k_merge_bwd(dmix, merged, ya, yb, ga, gb, w_a, w_b, w_o, tm):
    L = dmix.shape[0]
    n = L // tm

    def body(dmix_ref, mg_ref, ya_ref, yb_ref, ga_ref, gb_ref, wa_ref, wb_ref, wo_ref,
             dga_ref, dgb_ref, dya_ref, dyb_ref, gwa_ref, gwb_ref, gwo_ref, acc_a, acc_b, acc_o):
        i = pl.program_id(0)

        @pl.when(i == 0)
        def _():
            for r in (acc_a, acc_b, acc_o):
                r[...] = jnp.zeros_like(r)

        dmixv = dmix_ref[...]
        dmg = lax.dot_general(dmixv, wo_ref[...], NT, preferred_element_type=F32)
        acc_o[...] += lax.dot_general(mg_ref[...], dmixv, TN, preferred_element_type=F32)
        ya = ya_ref[...]
        a = jnp.dot(ya, wa_ref[...], preferred_element_type=F32)
        sa = _sigmoid(ga_ref[...].astype(F32))
        dA = (dmg * sa).astype(BF16)
        dga_ref[...] = (dmg * a * (sa * (1.0 - sa))).astype(BF16)
        dya_ref[...] = lax.dot_general(dA, wa_ref[...], NT, preferred_element_type=F32).astype(BF16)
        acc_a[...] += lax.dot_general(ya, dA, TN, preferred_element_type=F32)
        yb = yb_ref[...]
        b = jnp.dot(yb, wb_ref[...], preferred_element_type=F32)
        sb = _sigmoid(gb_ref[...].astype(F32))
        dB = (dmg * sb).astype(BF16)
        dgb_ref[...] = (dmg * b * (sb * (1.0 - sb))).astype(BF16)
        dyb_ref[...] = lax.dot_general(dB, wb_ref[...], NT, preferred_element_type=F32).astype(BF16)
        acc_b[...] += lax.dot_general(yb, dB, TN, preferred_element_type=F32)

        @pl.when(i == n - 1)
        def _():
            gwa_ref[...] = acc_a[...].astype(BF16)
            gwb_ref[...] = acc_b[...].astype(BF16)
            gwo_ref[...] = acc_o[...].astype(BF16)

    return pl.pallas_call(
        body, name="bwd_merge", grid=(n,),
        in_specs=[_row(tm, D), _row(tm, D), _row(tm, Q_W), _row(tm, GM_W), _row(tm, D), _row(tm, D),
                  _resident((Q_W, D)), _resident((GM_W, D)), _resident((D, D))],
        out_specs=[_row(tm, D), _row(tm, D), _row(tm, Q_W), _row(tm, GM_W), _full((Q_W, D)), _full((GM_W, D)), _full((D, D))],
        out_shape=[_sds((L, D), BF16), _sds((L, D), BF16), _sds((L, Q_W), BF16), _sds((L, GM_W), BF16),
                   _sds((Q_W, D), BF16), _sds((GM_W, D), BF16), _sds((D, D), BF16)],
        scratch_shapes=[pltpu.VMEM((Q_W, D), F32), pltpu.VMEM((GM_W, D), F32), pltpu.VMEM((D, D), F32)],
        compiler_params=_params(("arbitrary",)),
    )(dmix, merged, ya, yb, ga, gb, w_a, w_b, w_o)


def _k_gmlp_bwd(u, vb, dyb, lnv, ws, wst, bsp):
    L = u.shape[0]
    nch = min(GMLP_CHUNKS, L // BLK)
    tm = nch * BLK

    def body(u_ref, vb_ref, dyb_ref, lnv_ref, ws_ref, wst_ref, bsp_ref, du_ref, dvb_ref, gws_ref, gbst_ref, gln_ref):
        @pl.when(pl.program_id(0) == 0)
        def _():
            gws_ref[...] = jnp.zeros_like(gws_ref)
            gbst_ref[...] = jnp.zeros_like(gbst_ref)
            gln_ref[...] = jnp.zeros_like(gln_ref)

        uf, vf, gu, tu, tv, vhat, rstd, vn, s = _gmlp_fwd_vals(u_ref[...], vb_ref[...], lnv_ref, ws_ref, bsp_ref, nch)
        dyb_f = dyb_ref[...].astype(F32)
        du_ref[...] = (dyb_f * s * _gelu_grad(uf, tu)).astype(BF16)
        ds = dyb_f * gu
        ds_b = ds.astype(BF16)
        for pr in range(N_GROUPS // 2):
            lanes = slice(pr * 128, (pr + 1) * 128)
            gw_lo = gw_hi = ds_sum = None
            for c in range(nch):
                rows = slice(c * BLK, (c + 1) * BLK)
                lo, hi = _split_pair(ds_b[rows, lanes])
                t_lo = lax.dot_general(lo, vn[rows, lanes], NT, preferred_element_type=F32)
                t_hi = lax.dot_general(hi, vn[rows, lanes], NT, preferred_element_type=F32)
                gw_lo = t_lo if c == 0 else gw_lo + t_lo
                gw_hi = t_hi if c == 0 else gw_hi + t_hi
                ds_sum = ds[rows, lanes] if c == 0 else ds_sum + ds[rows, lanes]
            gws_ref[2 * pr] += gw_lo
            gws_ref[2 * pr + 1] += gw_hi
            b_lo, b_hi = _split_pair(ds_sum)
            gbst_ref[:, 2 * pr:2 * pr + 1] += jnp.sum(b_lo, axis=1, keepdims=True)
            gbst_ref[:, 2 * pr + 1:2 * pr + 2] += jnp.sum(b_hi, axis=1, keepdims=True)
        dvn = _gmlp_spatial(wst_ref, ds_b, nch)
        gln_ref[0:1, :] += _colsum(dvn * vhat)
        gln_ref[1:2, :] += _colsum(dvn)
        dgv = _ln_bwd(dvn * lnv_ref[0:1, :], vhat, rstd)
        dvb_ref[...] = (dgv * _gelu_grad(vf, tv)).astype(BF16)

    return pl.pallas_call(
        body, name="bwd_gmlp", grid=(L // tm,),
        in_specs=[_row(tm, GM_W)] * 3 + [_full((8, GM_W)), _full((N_GROUPS, BLK, BLK)), _full((N_GROUPS, BLK, BLK)),
                                         _full((BLK, GM_W))],
        out_specs=[_row(tm, GM_W), _row(tm, GM_W), _full((N_GROUPS, BLK, BLK)), _full((BLK, N_GROUPS)), _full((8, GM_W))],
        out_shape=[_sds((L, GM_W), BF16), _sds((L, GM_W), BF16), _sds((N_GROUPS, BLK, BLK), F32),
                   _sds((BLK, N_GROUPS), F32), _sds((8, GM_W), F32)],
        compiler_params=_params(("arbitrary",)),
    )(u, vb, dyb, lnv, ws, wst, bsp)


ATTN_BWD_BLOCKS = 2


def _k_attn_bwd(sink, q, k, v, kc, vc, dya, lse, cos, sin, bias, comm=None):
    L = q.shape[0]
    C = kc.shape[0]
    nb = L // BLK
    nq = min(ATTN_BWD_BLOCKS, nb)
    steps = nb // nq
    NK = C + 3 * BLK
    chains = [(qb, hk) for qb in range(nq) for hk in range(N_KV_HEADS)]

    def body(sink_ref, q_ref, kp_ref, km_ref, kx_ref, vp_ref, vm_ref, vx_ref, kc_ref, vc_ref, do_ref, lse_ref,
             cq_ref, sq_ref, cl_ref, sl_ref, bias_ref,
             dq_ref, dk_ref, dv_ref, dkc_ref, dvc_ref, dsink_ref,
             dq_scr, ck_scr, cv_scr, k1_acc, k2_acc, v1_acc, v2_acc):
        i = pl.program_id(0)

        @pl.when(i == 0)
        def _():
            for r in (k1_acc, k2_acc, v1_acc, v2_acc, dkc_ref, dvc_ref, dsink_ref):
                r[...] = jnp.zeros_like(r)

        @pl.when(i < steps)
        def _():
            def band(qb):
                first = jnp.where(i == 0, 0, 1) if qb == 0 else 1
                return bias_ref[jnp.where(i == steps - 1, 2, first) if qb == nq - 1 else first]

            def lanes(hk):
                return slice(hk * HEAD_DIM, (hk + 1) * HEAD_DIM)

            def keys(ctx_ref, p_ref, m_ref, x_ref, qb, hk):
                sl = lanes(hk)
                blocks = [p_ref[:, sl]] + [m_ref[j * BLK:(j + 1) * BLK, sl] for j in range(nq)] + [x_ref[:, sl]]
                return jnp.concatenate([ctx_ref[:, sl]] + blocks[qb:qb + 3], axis=0)

            def stacked(ref, qb, hk, width):
                return jnp.concatenate(
                    [ref[qb * BLK:(qb + 1) * BLK, (hk * GQA_GROUP + g) * width:(hk * GQA_GROUP + g + 1) * width]
                     for g in range(GQA_GROUP)], axis=0)

            def scores(qb, hk):
                kcat = keys(kc_ref, kp_ref, km_ref, kx_ref, qb, hk)
                qg = stacked(q_ref, qb, hk, HEAD_DIM)
                s = _masked(lax.dot_general(qg, kcat, NT, preferred_element_type=F32), band(qb), C)
                dog = stacked(do_ref, qb, hk, HEAD_DIM)
                dp = lax.dot_general(dog, keys(vc_ref, vp_ref, vm_ref, vx_ref, qb, hk), NT, preferred_element_type=F32)
                return kcat, qg, dog, s, dp

            def softmax_bwd(qb, hk, s, dp):
                lse_c = stacked(lse_ref, qb, hk, 1)
                p = jnp.exp(s - lse_c)
                delta = jnp.sum(p * dp, axis=1, keepdims=True)
                ds = (p * (dp - delta)).astype(BF16)
                p_sink = jnp.exp(_sink_col(sink_ref, hk) - lse_c) * delta
                return p.astype(BF16), ds, p_sink

            def put_dq(qb, hk, dqs, p_sink):
                for g in range(GQA_GROUP):
                    h = hk * GQA_GROUP + g
                    dq_scr[qb * BLK:(qb + 1) * BLK, h * HEAD_DIM:(h + 1) * HEAD_DIM] = dqs[g * BLK:(g + 1) * BLK, :]
                    tot = jnp.sum(p_sink[g * BLK:(g + 1) * BLK, :], axis=0, keepdims=True)
                    dsink_ref[h:h + 1, :] -= jnp.broadcast_to(tot, (1, 128))

            ahead = 4
            sc = [scores(*c) for c in chains[:ahead]]
            pending = None
            for n, (qb, hk) in enumerate(chains):
                if n + ahead < len(chains):
                    sc.append(scores(*chains[n + ahead]))
                kcat, qg, dog, s, dp = sc[n]
                pb, ds, p_sink = softmax_bwd(qb, hk, s, dp)
                if pending is not None:
                    pqb, phk, pds, ppb, pqg, pdog = pending
                    ck_scr[pqb, :, lanes(phk)] = lax.dot_general(pds, pqg, TN, preferred_element_type=F32)
                    cv_scr[pqb, :, lanes(phk)] = lax.dot_general(ppb, pdog, TN, preferred_element_type=F32)
                put_dq(qb, hk, jnp.dot(ds, kcat, preferred_element_type=F32), p_sink)
                pending = (qb, hk, ds, pb, qg, dog)
            pqb, phk, pds, ppb, pqg, pdog = pending
            ck_scr[pqb, :, lanes(phk)] = lax.dot_general(pds, pqg, TN, preferred_element_type=F32)
            cq, sq = cq_ref[...], sq_ref[...]
            for j in range(4):
                dq_ref[:, j * 128:(j + 1) * 128] = _unrope(dq_scr[:, j * 128:(j + 1) * 128] * Q_SCALE, cq, sq).astype(BF16)
            cv_scr[pqb, :, lanes(phk)] = lax.dot_general(ppb, pdog, TN, preferred_element_type=F32)
            dkc_ref[...] += functools.reduce(lambda a, b: a + b, [ck_scr[qb, 0:C, :] for qb in range(nq)])
            dvc_ref[...] += functools.reduce(lambda a, b: a + b, [cv_scr[qb, 0:C, :] for qb in range(nq)])

        @pl.when(i >= steps)
        def _():
            ck_scr[...] = jnp.zeros_like(ck_scr)
            cv_scr[...] = jnp.zeros_like(cv_scr)

        def slot(scr, r, carried):
            parts = [scr[qb, C + (r - qb) * BLK:C + (r - qb + 1) * BLK, :] for qb in range(nq) if 0 <= r - qb <= 2]
            total = functools.reduce(lambda a, b: a + b, parts)
            return total if carried is None else carried[...] + total

        for r in range(nq):
            rows = slice(r * BLK, (r + 1) * BLK)
            carried_k, carried_v = ((k1_acc, v1_acc), (k2_acc, v2_acc), (None, None))[min(r, 2)]
            tables = (cl_ref[...], sl_ref[...]) if r == 0 else (cq_ref[(r - 1) * BLK:r * BLK, :], sq_ref[(r - 1) * BLK:r * BLK, :])
            dk_ref[rows, :] = _unrope(slot(ck_scr, r, carried_k), *tables).astype(BF16)
            dv_ref[rows, :] = slot(cv_scr, r, carried_v).astype(BF16)
        k1_acc[...] = slot(ck_scr, nq, None)
        v1_acc[...] = slot(cv_scr, nq, None)
        k2_acc[...] = slot(ck_scr, nq + 1, None)
        v2_acc[...] = slot(cv_scr, nq + 1, None)

    last = steps - 1
    kv3 = [pl.BlockSpec((BLK, KV_W), lambda i: (jnp.clip(nq * i - 1, 0, nb - 1), 0)),
           pl.BlockSpec((nq * BLK, KV_W), lambda i: (jnp.minimum(i, last), 0)),
           pl.BlockSpec((BLK, KV_W), lambda i: (jnp.minimum(nq * i + nq, nb - 1), 0))]
    cur = lambda w: pl.BlockSpec((nq * BLK, w), lambda i: (jnp.minimum(i, last), 0))
    late = lambda w: pl.BlockSpec((BLK, w), lambda i: (jnp.clip(nq * i - 1, 0, nb - 1), 0))
    out2 = lambda w: pl.BlockSpec((nq * BLK, w), lambda i: (i, 0))
    return _call(
        body, name="bwd_attn", grid=(steps + 1,),
        in_specs=[pl.BlockSpec(memory_space=pltpu.SMEM), cur(Q_W)] + kv3 + kv3
                 + [_full((C, KV_W)), _full((C, KV_W)), cur(Q_W), cur(N_Q_HEADS), cur(128), cur(128), late(128), late(128),
                    _full((3, GQA_GROUP * BLK, 3 * BLK))],
        out_specs=[cur(Q_W), out2(KV_W), out2(KV_W), _full((C, KV_W)), _full((C, KV_W)), _full((8, 128))],
        out_shape=[_sds((L, Q_W), BF16), _sds((L + nq * BLK, KV_W), BF16), _sds((L + nq * BLK, KV_W), BF16),
                   _sds((C, KV_W), F32), _sds((C, KV_W), F32), _sds((8, 128), F32)],
        scratch=[pltpu.VMEM((nq * BLK, Q_W), F32), pltpu.VMEM((nq, NK, KV_W), F32), pltpu.VMEM((nq, NK, KV_W), F32)]
                + [pltpu.VMEM((BLK, KV_W), F32)] * 4,
        args=(sink, q, k, k, k, v, v, v, kc, vc, dya, lse, cos, sin, cos, sin, bias), comm=comm)


def _k_ctx_bwd(ctx, modc, hc, dkc, dvc, w_kv):
    C = ctx.shape[0]

    def body(c_ref, mod_ref, hc_ref, dkc_ref, dvc_ref, w_ref, gw_ref, dmod_ref):
        dkv = jnp.concatenate([dkc_ref[...], dvc_ref[...]], axis=1).astype(BF16)
        gw_ref[...] = lax.dot_general(dkv, hc_ref[...], TN, preferred_element_type=F32)
        dhc = jnp.dot(dkv, w_ref[...], preferred_element_type=F32)
        n, _ = _ln(c_ref[...])
        dmod_ref[...] = jnp.zeros_like(dmod_ref)
        dmod_ref[0:1, :] = _colsum(dhc)
        dmod_ref[1:2, :] = _colsum(dhc * n)

    return pl.pallas_call(
        body, name="bwd_ctx", grid=(1,),
        in_specs=[_full((C, D)), _full((8, D)), _full((C, D)), _full((C, KV_W)), _full((C, KV_W)), _full((2 * KV_W, D))],
        out_specs=[_full((2 * KV_W, D)), _full((8, D))],
        out_shape=[_sds((2 * KV_W, D), F32), _sds((8, D), F32)],
        compiler_params=_params(("arbitrary",)),
    )(ctx, modc, hc, dkc, dvc, w_kv)


def _k_in_bwd(dq, dk, dv, du, dvb, dga, dgb, x, dxp, w_in, modv, tm, comm=None):
    L = x.shape[0]
    parts = [(O_Q, Q_W), (O_K, KV_W), (O_V, KV_W), (O_U, GM_W), (O_VB, GM_W), (O_GA, D), (O_GB, D)]

    def body(dq_ref, dk_ref, dv_ref, du_ref, dvb_ref, dga_ref, dgb_ref, x_ref, dxp_ref, w_ref, mod_ref,
             dP_ref, gx_ref, acc_ref):
        @pl.when(pl.program_id(0) == 0)
        def _():
            acc_ref[...] = jnp.zeros_like(acc_ref)

        for (lo, width), r in zip(parts, (dq_ref, dk_ref, dv_ref, du_ref, dvb_ref, dga_ref, dgb_ref)):
            dP_ref[:, lo:lo + width] = r[...]
        n1, rstd1 = _ln(x_ref[...])
        dh = jnp.dot(dP_ref[...], w_ref[...], preferred_element_type=F32)
        acc_ref[0:1, :] += _colsum(dh)
        acc_ref[1:2, :] += _colsum(dh * n1)
        gx_ref[...] = dxp_ref[...] + _ln_bwd(dh * (1.0 + mod_ref[1:2, :]), n1, rstd1)

    return _call(
        body, name="bwd_in", grid=(L // tm,),
        in_specs=[_row(tm, w) for _, w in parts] + [_row(tm, D), _row(tm, D), _resident((IN_W, D)), _full((8, D))],
        out_specs=[_row(tm, IN_W), _row(tm, D), _full((8, D))],
        out_shape=[_sds((L, IN_W), BF16), _sds((L, D), F32), _sds((8, D), F32)],
        args=(dq, dk, dv, du, dvb, dga, dgb, x, dxp, w_in, modv), comm=comm)


def _wgrad(a, b, name, tk, tt, comm=None, extra=None):
    T, K = a.shape
    N = b.shape[1]
    nt = T // tt

    def body(*refs):
        a_ref, b_ref = refs[:2]
        o_ref, acc_ref = refs[-2:]
        j, t = pl.program_id(0), pl.program_id(1)

        @pl.when(t == 0)
        def _():
            acc_ref[...] = jnp.zeros_like(acc_ref)

        acc_ref[...] += lax.dot_general(a_ref[...], b_ref[...], TN, preferred_element_type=F32)

        if extra is not None:
            lo, rows = extra[0] % tk, extra[1].shape[0]

            @pl.when((t == nt - 1) & (j == extra[0] // tk))
            def _():
                acc_ref[lo:lo + rows, :] += refs[2][...]

        @pl.when(t == nt - 1)
        def _():
            o_ref[...] = acc_ref[...].astype(BF16)

    extra_specs = [] if extra is None else [pl.BlockSpec(extra[1].shape, lambda j, t: (0, 0))]
    (out,), got = _call(
        body, name=name, grid=(K // tk, nt),
        in_specs=[pl.BlockSpec((tt, tk), lambda j, t: (t, j)), pl.BlockSpec((tt, N), lambda j, t: (t, 0))] + extra_specs,
        out_specs=[pl.BlockSpec((tk, N), lambda j, t: (j, 0))],
        out_shape=[_sds((K, N), BF16)],
        scratch=[pltpu.VMEM((tk, N), F32)],
        args=(a, b) + (() if extra is None else (extra[1],)), comm=comm)
    return (out, got) if comm is not None else out


def _adamw_reduce(parts, w, m, v, name, tr):
    R, C = w.shape
    n_parts = parts.shape[0]

    def body(p_ref, w_ref, m_ref, v_ref, g_ref, d_ref, m2_ref, v2_ref):
        g = p_ref[0].astype(F32)
        for i in range(1, n_parts):
            g = g + p_ref[i].astype(F32)
        delta, m2, v2 = _adamw(w_ref[...], g, m_ref[...], v_ref[...])
        g_ref[...] = g
        d_ref[...] = delta
        m2_ref[...] = m2
        v2_ref[...] = v2

    spec = _row(tr, C)
    return pl.pallas_call(
        body, name=name, grid=(R // tr,),
        in_specs=[pl.BlockSpec((n_parts, tr, C), lambda i: (0, i, 0)), spec, spec, spec],
        out_specs=[spec] * 4,
        out_shape=[_sds((R, C), F32)] * 4,
        compiler_params=_params(("arbitrary",)),
    )(parts, w, m, v)


SMALL_ORDER = ("b_ada", "ln1_g", "ln1_b", "ln2_g", "ln2_b", "gmlp_ln_g", "gmlp_ln_b", "b_spatial", "attn_sink")


def _small_step(gath, params):
    flat = [a for name in SMALL_ORDER for a in params[name]]

    def grad_of(tot, name):
        if name == "b_ada":
            return jnp.concatenate([tot[r:r + 1, :] for r in range(6)], axis=1)
        if name in ("ln1_g", "ln1_b", "ln2_g", "ln2_b"):
            r = 8 + ("ln1_g", "ln1_b", "ln2_g", "ln2_b").index(name)
            return tot[r:r + 1, :]
        if name == "gmlp_ln_g":
            return tot[12:13, :GM_W]
        if name == "gmlp_ln_b":
            return tot[12:13, GM_W:]
        if name == "b_spatial":
            return jnp.concatenate([tot[13:14, g * BLK:(g + 1) * BLK] for g in range(N_GROUPS)], axis=0)[None]
        return tot[14:15, :N_Q_HEADS]

    def body(*refs):
        g_ref, in_refs = refs[0], refs[1:1 + len(flat)]
        tot_ref, out_refs = refs[1 + len(flat)], refs[2 + len(flat):]
        tot = g_ref[0]
        for i in range(1, N_DEV):
            tot = tot + g_ref[i]
        tot_ref[...] = tot
        tot_ref[0:2, :] = tot[0:2, :] + tot[6:8, :]
        tot_ref[15:16, :] = jnp.broadcast_to(jnp.sum(tot[15:16, :], axis=1, keepdims=True), (1, D))
        tot = tot_ref[...]
        for k, name in enumerate(SMALL_ORDER):
            w_ref, m_ref, v_ref = in_refs[3 * k:3 * k + 3]
            g = grad_of(tot, name)
            delta, m2, v2 = _adamw(w_ref[...], g, m_ref[...], v_ref[...])
            for r, val in zip(out_refs[4 * k:4 * k + 4], (g, delta, m2, v2)):
                r[...] = val

    res = pl.pallas_call(
        body, name="small_step", grid=(1,),
        in_specs=[_full((N_DEV, 16, D))] + [_full(a.shape) for a in flat],
        out_specs=[_full((16, D))] + [_full(params[name][0].shape) for name in SMALL_ORDER for _ in range(4)],
        out_shape=[_sds((16, D), F32)] + [_sds(params[name][0].shape, F32) for name in SMALL_ORDER for _ in range(4)],
        compiler_params=_params(("arbitrary",)),
    )(gath, *flat)
    return res[0], {name: res[1 + 4 * k:5 + 4 * k] for k, name in enumerate(SMALL_ORDER)}


def _cctx_finish(gath, c_ctx, m, v):
    def body(g_ref, c_ref, m_ref, v_ref, gr_ref, d_ref, m2_ref, v2_ref):
        ds = g_ref[0]
        for i in range(1, N_DEV):
            ds = ds + g_ref[i]
        c = c_ref[...]
        sg = _sigmoid(c)
        g = ds * (sg * (1.0 + c * (1.0 - sg)))
        delta, m2, v2 = _adamw(c, g, m_ref[...], v_ref[...])
        gr_ref[...] = g
        d_ref[...] = delta
        m2_ref[...] = m2
        v2_ref[...] = v2

    return pl.pallas_call(
        body, name="cctx_finish", grid=(1,),
        in_specs=[_full((N_DEV, 8, D))] + [_full((8, D))] * 3, out_specs=[_full((8, D))] * 4,
        out_shape=[_sds((8, D), F32)] * 4,
        compiler_params=_params(("arbitrary",)),
    )(gath, c_ctx, m, v)


def _pad_rows(a, rows):
    return jnp.concatenate([a, jnp.zeros((rows - a.shape[0], a.shape[1]), a.dtype)], axis=0)


def kernel(x, c, ctx, c_ctx, w_ada, b_ada, w_in, attn_sink, gmlp_ln_g, gmlp_ln_b, w_spatial, b_spatial, w_branch_a, w_branch_b, w_out, ln1_g, ln1_b, w_ffn_in, w_ffn_out, ln2_g, ln2_b, loss_target, m_c_ctx, m_w_ada, m_b_ada, m_w_in, m_attn_sink, m_gmlp_ln_g, m_gmlp_ln_b, m_w_spatial, m_b_spatial, m_w_branch_a, m_w_branch_b, m_w_out, m_ln1_g, m_ln1_b, m_w_ffn_in, m_w_ffn_out, m_ln2_g, m_ln2_b, v_c_ctx, v_w_ada, v_b_ada, v_w_in, v_attn_sink, v_gmlp_ln_g, v_gmlp_ln_b, v_w_spatial, v_b_spatial, v_w_branch_a, v_w_branch_b, v_w_out, v_ln1_g, v_ln1_b, v_w_ffn_in, v_w_ffn_out, v_ln2_g, v_ln2_b):
    L = x.shape[1]
    me = 4 * lax.axis_index("x") + 2 * lax.axis_index("y") + lax.axis_index("c")
    x2, tgt, ctx2 = x[0], loss_target[0], ctx[0]
    tiles = _Tiles(L)
    tm_in, tm, tt = tiles.wide, tiles.narrow, tiles.tokens

    transposed = ("w_in", "w_ffn_in")
    tr = lambda kname, a: a.T if kname in transposed else a
    big = dict(w_in=w_in[0].T, w_branch_a=w_branch_a[0], w_branch_b=w_branch_b[0], w_out=w_out[0],
               w_ffn_in=w_ffn_in[0].T, w_ffn_out=w_ffn_out[0])
    col_sharded = ("w_branch_a", "w_branch_b")
    shard_bf = {k: a.astype(BF16) for k, a in big.items()}

    def assemble(kname, g):
        if kname in col_sharded:
            return g.transpose(1, 0, 2).reshape(g.shape[1], N_DEV * g.shape[2])
        return g.reshape(N_DEV * g.shape[1], g.shape[2])

    def to_blocks(kname, g):
        if kname in col_sharded:
            return g.reshape(g.shape[0], N_DEV, g.shape[1] // N_DEV).transpose(1, 0, 2)
        return g.reshape(N_DEV, g.shape[0] // N_DEV, g.shape[1])

    full = {}
    n_ada = w_ada.shape[2]
    b_my = lax.dynamic_slice(b_ada, (0, me * n_ada), (1, n_ada))
    act, mod_all, got = _prologue(_pad_rows(c, 8), _pad_rows(c_ctx[None, :], 8), w_ada[0], b_my,
                                  _Comm(gather=[shard_bf["w_in"]]))
    full["w_in"] = assemble("w_in", got[0])
    mod_all = mod_all.transpose(1, 0, 2).reshape(16, 6 * D)
    modv = _pad_rows(lax.dynamic_slice(mod_all, (me, 0), (1, 6 * D)).reshape(6, D), 8)
    modc = _pad_rows(mod_all[8].reshape(6, D), 8)

    lnv = _pad_rows(jnp.concatenate([ln1_g, ln1_b, ln2_g, ln2_b], axis=0), 8)
    gm_lnv = _pad_rows(jnp.concatenate([gmlp_ln_g, gmlp_ln_b], axis=0), 8)
    ws_b = w_spatial[0].astype(BF16)
    wst_b = ws_b.transpose(0, 2, 1)
    bsp = jnp.repeat(b_spatial[0].T, GROUP_DIM, axis=1)
    sink = attn_sink[0]
    cos, sin = _rope_tables(L)
    bias = _attn_bias()
    w_kv = full["w_in"][O_K:O_K + 2 * KV_W, :]

    (h, q, k, v, u, vb, ga, gb), got = _k_in(
        x2, modv, full["w_in"], cos, sin, tm_in,
        comm=_Comm(gather=[shard_bf[kname] for kname in ("w_branch_a", "w_branch_b", "w_out", "w_ffn_out")]))
    for kname, g in zip(("w_branch_a", "w_branch_b", "w_out", "w_ffn_out"), got):
        full[kname] = assemble(kname, g)
    hc, kc, vc = _k_ctx(ctx2, modc, w_kv)
    (ya, lse), got = _k_attn(sink, q, k, v, kc, vc, bias, comm=_Comm(gather=[shard_bf["w_ffn_in"]]))
    full["w_ffn_in"] = assemble("w_ffn_in", got[0])
    yb = _k_gmlp(u, vb, gm_lnv, ws_b, bsp)
    merged, mix, xm, h2 = _k_merge(x2, ya, yb, ga, gb, full["w_branch_a"], full["w_branch_b"], full["w_out"], modv, lnv, tm_in)
    gate, up, act_f, dr2, df, acc_f = _k_ffn(h2, xm, tgt, full["w_ffn_in"], full["w_ffn_out"], modv, lnv, tm_in)

    dF, dmix, dxp, acc_b = _k_ffn_bwd(df, gate, up, xm, dr2, x2, mix, full["w_ffn_in"], full["w_ffn_out"], modv, lnv, tm)
    blk_fo = to_blocks("w_ffn_out", _wgrad(act_f, df, "wgrad_ffn_out", tiles.tk_ffn, tt))
    gw_fi, (rcv_fo,) = _wgrad(dF, h2, "wgrad_ffn_in", tiles.tk_ffn, tt, comm=_Comm(scatter=[blk_fo]))
    blk_fi = to_blocks("w_ffn_in", gw_fi)
    dga, dgb, dya, dyb, gw_a, gw_b, gw_o = _k_merge_bwd(
        dmix, merged, ya, yb, ga, gb, full["w_branch_a"], full["w_branch_b"], full["w_out"], tm_in)
    du, dvb, g_ws, g_bst, g_gln = _k_gmlp_bwd(u, vb, dyb, gm_lnv, ws_b, wst_b, bsp)
    (dq, dk_late, dv_late, dkc, dvc, g_sink), (gath_ws, rcv_fi) = _k_attn_bwd(
        sink, q, k, v, kc, vc, dya, lse, cos, sin, bias,
        comm=_Comm(gather=[g_ws.reshape(N_GROUPS * BLK, BLK)], scatter=[blk_fi]))
    dk, dv = dk_late[BLK:BLK + L], dv_late[BLK:BLK + L]
    blk_a, blk_b, blk_o = to_blocks("w_branch_a", gw_a), to_blocks("w_branch_b", gw_b), to_blocks("w_out", gw_o)
    (dP, grad_x, acc_i), _ = _k_in_bwd(dq, dk, dv, du, dvb, dga, dgb, x2, dxp, full["w_in"], modv, tm_in)
    g_ctx, dmodc = _k_ctx_bwd(ctx2, modc, hc, dkc, dvc, w_kv)
    gw_in, (rcv_a, rcv_b, rcv_o) = _wgrad(dP, h, "wgrad_in", tiles.tk_in, tt, comm=_Comm(scatter=[blk_a, blk_b, blk_o]),
                                          extra=(O_K, g_ctx))

    dmod_x = jnp.concatenate([acc_i[0:2], acc_b[4:5], acc_b[0:2], acc_f[2:3]], axis=0)
    small = jnp.concatenate([
        dmod_x, dmodc[0:2], acc_b[2:4], acc_f[0:2],
        jnp.concatenate([g_gln[0:1], g_gln[1:2]], axis=1), g_bst.T.reshape(1, D),
        _pad_rows(g_sink[:, 0:1], D).T, acc_f[3:4]], axis=0)
    chip_sums, gath = _exchange_two_level(to_blocks("w_in", gw_in), small, "exchange_last", ici=False)
    rcv_in = jnp.concatenate([chip_sums[0:1], _sc_chip_exchange(chip_sums)], axis=0)
    received = dict(w_in=rcv_in, w_branch_a=rcv_a, w_branch_b=rcv_b, w_out=rcv_o, w_ffn_in=rcv_fi, w_ffn_out=rcv_fo)
    moments = dict(w_in=(m_w_in, v_w_in), w_branch_a=(m_w_branch_a, v_w_branch_a), w_branch_b=(m_w_branch_b, v_w_branch_b),
                   w_out=(m_w_out, v_w_out), w_ffn_in=(m_w_ffn_in, v_w_ffn_in), w_ffn_out=(m_w_ffn_out, v_w_ffn_out))
    names = list(big)
    res = {}
    for kname in names:
        mm, vv = moments[kname]
        R = big[kname].shape[0]
        res[kname] = [tr(kname, r) for r in _adamw_reduce(
            received[kname], big[kname], tr(kname, mm[0]), tr(kname, vv[0]), "adamw_" + kname, 256 if R % 256 == 0 else R // 2)]

    ws2d = lambda a: a.reshape(N_GROUPS * BLK, BLK)
    res_ws = [r.reshape(w_spatial.shape) for r in _adamw_reduce(
        gath_ws, ws2d(w_spatial), ws2d(m_w_spatial), ws2d(v_w_spatial), "adamw_w_spatial", 256)]
    tot, res_small = _small_step(gath, dict(
        b_ada=(b_ada, m_b_ada, v_b_ada), ln1_g=(ln1_g, m_ln1_g, v_ln1_g), ln1_b=(ln1_b, m_ln1_b, v_ln1_b),
        ln2_g=(ln2_g, m_ln2_g, v_ln2_g), ln2_b=(ln2_b, m_ln2_b, v_ln2_b),
        gmlp_ln_g=(gmlp_ln_g, m_gmlp_ln_g, v_gmlp_ln_g), gmlp_ln_b=(gmlp_ln_b, m_gmlp_ln_b, v_gmlp_ln_b),
        b_spatial=(b_spatial, m_b_spatial, v_b_spatial), attn_sink=(attn_sink, m_attn_sink, v_attn_sink)))
    loss = tot[15, 0]

    dmod_rows = jnp.concatenate([gath[:, 0:6, :].reshape(N_DEV, 6 * D),
                                 jnp.concatenate([tot[6:8].reshape(1, 2 * D), jnp.zeros((1, 4 * D), F32)], axis=1),
                                 jnp.zeros((7, 6 * D), F32)], axis=0)
    dmod_my = lax.dynamic_slice(dmod_rows, (0, me * n_ada), (16, n_ada))
    g_wada, d_wada, m2_wada, v2_wada, pc = _ada_bwd(act, dmod_my, w_ada[0], m_w_ada[0], v_w_ada[0])
    pc_all = _gather_rows(pc, "gather_cctx")
    cc8 = lambda a: _pad_rows(a.reshape(1, D), 8)
    g_cc, d_cc, m2_cc, v2_cc = _cctx_finish(pc_all, cc8(c_ctx), cc8(m_c_ctx), cc8(v_c_ctx))

    order = ["c_ctx", "w_ada", "b_ada", "w_in", "attn_sink", "gmlp_ln_g", "gmlp_ln_b", "w_spatial", "b_spatial",
             "w_branch_a", "w_branch_b", "w_out", "ln1_g", "ln1_b", "w_ffn_in", "w_ffn_out", "ln2_g", "ln2_b"]
    grads, deltas, new_m, new_v = {}, {}, {}, {}
    grads["c_ctx"], deltas["c_ctx"], new_m["c_ctx"], new_v["c_ctx"] = g_cc[0], d_cc[0], m2_cc[0], v2_cc[0]
    grads["w_ada"], deltas["w_ada"], new_m["w_ada"], new_v["w_ada"] = g_wada[None], d_wada[None], m2_wada[None], v2_wada[None]
    for kname in names:
        g, d, m2, v2 = res[kname]
        grads[kname], deltas[kname], new_m[kname], new_v[kname] = g[None], d[None], m2[None], v2[None]
    grads["w_spatial"], deltas["w_spatial"], new_m["w_spatial"], new_v["w_spatial"] = res_ws
    for kname in SMALL_ORDER:
        grads[kname], deltas[kname], new_m[kname], new_v[kname] = res_small[kname]
    return (loss, grad_x[None], *[grads[n] for n in order], *[deltas[n] for n in order],
            *[new_m[n] for n in order], *[new_v[n] for n in order])
```

```python
import functools
import math

import jax
import jax.numpy as jnp
import numpy as np
from jax import lax
from jax.experimental import pallas as pl
from jax.experimental.pallas import tpu as pltpu
from jax.experimental.pallas import tpu_sc as plsc

F32 = jnp.float32
BF16 = jnp.bfloat16
MESH = pl.DeviceIdType.MESH

N_DEV = 8
D = 1024
HEAD_DIM = 64
N_Q_HEADS = 8
N_KV_HEADS = 2
GQA_GROUP = 4
BLK = 128
Q_W = 512
KV_W = 128
GM_W = 512
N_GROUPS = 8
GROUP_DIM = 64
FFN_H = 2816
IN_W = 3840
O_Q, O_K, O_V, O_U, O_VB, O_GA, O_GB = 0, 512, 640, 768, 1280, 1792, 2816
LN_EPS = 1e-5
NEG_INF = -1e30
ALPHA = 2.0 ** 0.25
ROPE_BASE = 10000.0
ROPE_PAIRS = 16
Q_SCALE = HEAD_DIM ** -0.5
GELU_K0 = math.sqrt(2.0 / math.pi)
GELU_K1 = 0.044715

ADAM_LR = 0.001
ADAM_B1 = 0.9
ADAM_B2 = 0.999
ADAM_EPS = 1e-08
ADAM_WD = 0.01
ADAM_STEP = 10

V7X_VMEM_BYTES = 64 * 1024 * 1024
VMEM_LIMIT = V7X_VMEM_BYTES * 7 // 8
NT = (((1,), (1,)), ((), ()))
TN = (((0,), (0,)), ((), ()))


class _Tiles:
    def __init__(self, L):
        self.wide = min(512, L)
        self.narrow = min(256, L)
        self.tokens = min(2048, L)
        self.tk_in = IN_W // 3
        self.tk_ffn = FFN_H // 2


def _params(sem=None):
    return pltpu.CompilerParams(dimension_semantics=sem, vmem_limit_bytes=VMEM_LIMIT)


def _row(tm, w):
    return pl.BlockSpec((tm, w), lambda i: (i, 0))


def _full(shape):
    nd = len(shape)
    return pl.BlockSpec(shape, lambda i: (0,) * nd)


def _resident(shape):
    nd = len(shape)
    return pl.BlockSpec(shape, lambda i: (0,) * nd, pipeline_mode=pl.Buffered(1))


def _sds(shape, dt):
    return jax.ShapeDtypeStruct(shape, dt)


def _ln(xf):
    mu = jnp.mean(xf, axis=-1, keepdims=True)
    xc = xf - mu
    var = jnp.mean(xc * xc, axis=-1, keepdims=True)
    rstd = lax.rsqrt(var + LN_EPS)
    return xc * rstd, rstd


def _ln_bwd(dn, n, rstd):
    m1 = jnp.mean(dn, axis=-1, keepdims=True)
    m2 = jnp.mean(dn * n, axis=-1, keepdims=True)
    return rstd * (dn - m1 - n * m2)


def _colsum(t):
    return jnp.sum(t, axis=0, keepdims=True)


def _sigmoid(x):
    return 0.5 * jnp.tanh(0.5 * x) + 0.5


def _gelu(x):
    t = jnp.tanh(x * (GELU_K0 + (GELU_K0 * GELU_K1) * (x * x)))
    h = 0.5 * x
    return h + h * t, t


def _gelu_grad(x, t):
    return 0.5 + 0.5 * t + (0.5 * x) * (1.0 - t * t) * (GELU_K0 + (3.0 * GELU_K0 * GELU_K1) * (x * x))


def _swap16(t):
    lane = lax.broadcasted_iota(jnp.int32, t.shape, 1)
    return jnp.where((lane & 16) == 0, pltpu.roll(t, 112, 1), pltpu.roll(t, 16, 1))


def _rope(t, cos, sin):
    return t * cos + _swap16(t) * sin


def _unrope(t, cos, sin):
    return t * cos - _swap16(t) * sin


def _adamw(w, g, m, v):
    m2 = ADAM_B1 * m + (1.0 - ADAM_B1) * g
    v2 = ADAM_B2 * v + (1.0 - ADAM_B2) * (g * g)
    m_hat = m2 / (1.0 - ADAM_B1 ** ADAM_STEP)
    v_hat = v2 / (1.0 - ADAM_B2 ** ADAM_STEP)
    delta = -ADAM_LR * (m_hat / (jnp.sqrt(v_hat) + ADAM_EPS) + ADAM_WD * w)
    return delta, m2, v2


def _rope_tables(L):
    inv = (np.float32(ROPE_BASE) ** (-np.arange(ROPE_PAIRS, dtype=np.float32) / np.float32(ROPE_PAIRS))).astype(np.float32)
    t = np.arange(L, dtype=np.int32)
    rows = (t // 64).astype(np.float32)[:, None] * inv
    cols = (t % 64).astype(np.float32)[:, None] * inv
    cr, sr, cc, sc = np.cos(rows), np.sin(rows), np.cos(cols), np.sin(cols)
    cos = np.concatenate([cr, cr, cc, cc], axis=1)
    sin = np.concatenate([-sr, sr, -sc, sc], axis=1)
    return jnp.asarray(np.tile(cos, (1, 2)), F32), jnp.asarray(np.tile(sin, (1, 2)), F32)


def _me():
    return lax.axis_index("x"), lax.axis_index("y"), lax.axis_index("c")


def _peer(mx, my, mc, k):
    return (mx ^ ((k >> 2) & 1), my ^ ((k >> 1) & 1), mc ^ (k & 1))


class _Comm:
    def __init__(self, gather=(), scatter=(), spread=()):
        self.kinds = ["gather"] * len(gather) + ["scatter"] * len(scatter) + ["spread"] * len(spread)
        self.args = list(gather) + list(scatter) + list(spread)
        self.n = len(self.args)

    def out_shape(self):
        return [_sds(a.shape if k == "scatter" else (N_DEV,) + a.shape, a.dtype) for k, a in zip(self.kinds, self.args)]

    def specs(self):
        return [pl.BlockSpec(memory_space=pl.ANY)] * self.n

    def scratch(self):
        return [pltpu.SemaphoreType.DMA((7 * self.n,)), pltpu.SemaphoreType.DMA((7 * self.n,)),
                pltpu.SemaphoreType.DMA((self.n,))]

    def _plan(self, x_refs, out_refs, send_sems, recv_sems, local_sems):
        mx, my, mc = _me()
        me = 4 * mx + 2 * my + mc
        here, sibling = (mx, my, mc), (mx, my, 1 - mc)
        chips = [(1 - mx, my), (mx, 1 - my), (1 - mx, 1 - my)]
        local, first, last = [], [], []
        relay = [[], [], []]
        for a, kind in enumerate(self.kinds):
            x, out = x_refs[a], out_refs[a]

            def rc(k, src, dst, to):
                return pltpu.make_async_remote_copy(
                    src_ref=src, dst_ref=dst, send_sem=send_sems.at[7 * a + k], recv_sem=recv_sems.at[7 * a + k],
                    device_id=to, device_id_type=MESH)

            if kind == "gather":
                local.append(pltpu.make_async_copy(x, out.at[me], local_sems.at[a]))
                first.append(rc(0, x, out.at[me], sibling))
                last.append(rc(0, x, out.at[me ^ 1], here))
                for j, (cx, cy) in enumerate(chips):
                    first.append(rc(1 + j, x, out.at[me], (cx, cy, mc)))
                    landed = out.at[4 * cx + 2 * cy + mc]
                    relay[j].append((rc(1 + j, x, landed, here), rc(4 + j, landed, landed, sibling)))
                    last.append(rc(4 + j, x, out.at[4 * cx + 2 * cy + 1 - mc], here))
            else:
                own = x.at[me] if kind == "scatter" else x
                local.append(pltpu.make_async_copy(own, out.at[me], local_sems.at[a]))
                for k in range(1, N_DEV):
                    src = x.at[me ^ k] if kind == "scatter" else x
                    first.append(rc(k - 1, src, out.at[me], _peer(mx, my, mc, k)))
                    last.append(rc(k - 1, own, out.at[me ^ k], here))
        return local, first, relay[0] + relay[1] + relay[2], last

    def start(self, *refs):
        local, first, _, _ = self._plan(*refs)
        for cp in local + first:
            cp.start()

    def relay(self, *refs):
        _, _, relay, _ = self._plan(*refs)
        for arrival, onward in relay:
            arrival.wait_recv()
            onward.start()

    def finish(self, *refs):
        local, first, relay, last = self._plan(*refs)
        for cp in last:
            cp.wait_recv()
        for cp in first:
            cp.wait_send()
        for _, onward in relay:
            onward.wait_send()
        for cp in local:
            cp.wait()


def _call(body, *, name, grid, in_specs, out_specs, out_shape, args, scratch=(), comm=None, aliases=None):
    params = _params(("arbitrary",) * len(grid))
    total = math.prod(grid)

    def at(step):
        flat = functools.reduce(lambda acc, dn: acc * dn[1] + pl.program_id(dn[0]), enumerate(grid), 0)
        return flat == step

    if comm is None:
        res = pl.pallas_call(
            body, name=name, grid=grid, in_specs=list(in_specs), out_specs=list(out_specs), out_shape=list(out_shape),
            scratch_shapes=list(scratch), input_output_aliases=aliases or {}, compiler_params=params)(*args)
        return list(res), []
    n_in, n_out, n_scr, cn = len(in_specs), len(out_specs), len(scratch), comm.n

    def hosted(*refs):
        ins, refs = refs[:n_in], refs[n_in:]
        cins, refs = refs[:cn], refs[cn:]
        outs, refs = refs[:n_out], refs[n_out:]
        couts, refs = refs[:cn], refs[cn:]
        scr, sems = refs[:n_scr], refs[n_scr:]

        @pl.when(at(0))
        def _():
            comm.start(cins, couts, *sems)

        body(*ins, *outs, *scr)

        @pl.when(at((3 * total) // 4 if total >= 4 else total - 1))
        def _():
            comm.relay(cins, couts, *sems)

        @pl.when(at(total - 1))
        def _():
            comm.finish(cins, couts, *sems)

    res = pl.pallas_call(
        hosted, name=name, grid=grid, in_specs=list(in_specs) + comm.specs(), out_specs=list(out_specs) + comm.specs(),
        out_shape=list(out_shape) + comm.out_shape(), scratch_shapes=list(scratch) + comm.scratch(),
        input_output_aliases=aliases or {}, compiler_params=params)(*args, *comm.args)
    return list(res[:n_out]), list(res[n_out:])


def _sc_chip_exchange(t):
    hbm = pltpu.MemorySpace.HBM
    t_ref = jax.new_ref(t, memory_space=hbm)
    got_ref = jax.empty_ref(jax.ShapeDtypeStruct((3,) + t.shape[1:], t.dtype), memory_space=hbm)
    dma = pltpu.SemaphoreType.DMA

    @pl.kernel(mesh=plsc.ScalarSubcoreMesh(axis_name="seq", num_cores=1), name="sc_chip_exchange",
               scratch_types=(dma, dma, dma, dma, dma, dma), compiler_params=pltpu.CompilerParams(collective_id=3))
    def launch(s1, s2, s3, r1, r2, r3):
        mx, my, mc = _me()
        peers = [(mx ^ (k >> 1), my ^ (k & 1), mc) for k in range(1, 4)]
        barrier = pltpu.get_barrier_semaphore()
        for peer in peers:
            pl.semaphore_signal(barrier, inc=1, device_id=peer, device_id_type=MESH)
        pl.semaphore_wait(barrier, 3)
        copies = [pltpu.make_async_remote_copy(src_ref=t_ref.at[k + 1], dst_ref=got_ref.at[k], send_sem=s, recv_sem=r,
                                               device_id=peer, device_id_type=MESH)
                  for k, (peer, s, r) in enumerate(zip(peers, (s1, s2, s3), (r1, r2, r3)))]
        for cp in copies:
            cp.start()
        for cp in copies:
            cp.wait()

    launch()
    return got_ref[...]


def _exchange_two_level(blk, small, name, ici=True):
    _, R, C = blk.shape
    rows = small.shape[0]

    def body(blk_ref, small_ref, stage_ref, out_ref, gath_ref, a_scr, b_scr, t_scr, s1, r1, s3, r3, ss, rs, lsem):
        mx, my, mc = _me()
        me = 4 * mx + 2 * my + mc
        mine = 2 * mx + my
        here, sibling = (mx, my, mc), (mx, my, 1 - mc)

        def rc(src, dst, send, recv, to):
            return pltpu.make_async_remote_copy(src_ref=src, dst_ref=dst, send_sem=send, recv_sem=recv,
                                                device_id=to, device_id_type=MESH)

        own_small = pltpu.make_async_copy(small_ref, gath_ref.at[me], lsem.at[0])
        own_small.start()
        spread = [rc(small_ref, gath_ref.at[me], ss.at[k - 1], rs.at[k - 1], _peer(mx, my, mc, k)) for k in range(1, N_DEV)]
        order = (1, 2, 3, 0)
        to_sib = [rc(blk_ref.at[2 * (mine ^ k) + 1 - mc], stage_ref.at[k], s1.at[k], r1.at[k], sibling) for k in order]
        for cp in spread + to_sib:
            cp.start()
        own = {k: pltpu.make_async_copy(blk_ref.at[2 * (mine ^ k) + mc], a_scr.at[k], lsem.at[1 + k]) for k in order}
        for k in order:
            own[k].start()
        onward = []
        for k in order:
            rc(blk_ref.at[0], stage_ref.at[k], s1.at[k], r1.at[k], here).wait_recv()
            landed = pltpu.make_async_copy(stage_ref.at[k], b_scr.at[k], lsem.at[5 + k])
            landed.start()
            landed.wait()
            own[k].wait()
            t_scr[k] = (a_scr[k].astype(F32) + b_scr[k].astype(F32)).astype(BF16)
            if k > 0 and ici:
                cp = rc(t_scr.at[k], out_ref.at[mine], s3.at[k - 1], r3.at[k - 1], (mx ^ (k >> 1), my ^ (k & 1), mc))
                cp.start()
                onward.append(cp)
        if ici:
            keep = pltpu.make_async_copy(t_scr.at[0], out_ref.at[mine], lsem.at[9])
        else:
            keep = pltpu.make_async_copy(t_scr, out_ref, lsem.at[9])
        keep.start()
        for k in range(1, 4 if ici else 1):
            rc(t_scr.at[0], out_ref.at[mine ^ k], s3.at[k - 1], r3.at[k - 1], here).wait_recv()
        for k in range(1, N_DEV):
            rc(small_ref, gath_ref.at[me ^ k], ss.at[k - 1], rs.at[k - 1], here).wait_recv()
        for cp in spread + to_sib + onward:
            cp.wait_send()
        keep.wait()
        own_small.wait()

    any_spec = pl.BlockSpec(memory_space=pl.ANY)
    dma = pltpu.SemaphoreType.DMA
    _, out, gath = pl.pallas_call(
        body, name=name,
        in_specs=[any_spec, any_spec], out_specs=[any_spec] * 3,
        out_shape=[_sds((4, R, C), BF16), _sds((4, R, C), BF16), _sds((N_DEV, rows, D), F32)],
        scratch_shapes=[pltpu.VMEM((4, R, C), BF16)] * 3
                       + [dma((4,)), dma((4,)), dma((3,)), dma((3,)), dma((N_DEV - 1,)), dma((N_DEV - 1,)), dma((10,))],
        compiler_params=pltpu.CompilerParams(vmem_limit_bytes=VMEM_LIMIT),
    )(blk, small)
    return out, gath


def _exchange_rows(x_ref, out_ref, send_sems, recv_sems, between=None):
    mx, my, mc = _me()
    me = 4 * mx + 2 * my + mc
    out_ref[pl.ds(me, 1)] = x_ref[...][None]
    sends = []
    for k in range(1, N_DEV):
        cp = pltpu.make_async_remote_copy(
            src_ref=x_ref, dst_ref=out_ref.at[me], send_sem=send_sems.at[k - 1], recv_sem=recv_sems.at[k - 1],
            device_id=_peer(mx, my, mc, k), device_id_type=MESH)
        cp.start()
        sends.append(cp)
    if between is not None:
        between()
    for k in range(1, N_DEV):
        pltpu.make_async_remote_copy(
            src_ref=x_ref, dst_ref=out_ref.at[me ^ k], send_sem=send_sems.at[k - 1], recv_sem=recv_sems.at[k - 1],
            device_id=(mx, my, mc), device_id_type=MESH).wait_recv()
    for cp in sends:
        cp.wait_send()


def _sc_gather(x):
    hbm = pltpu.MemorySpace.HBM
    x_ref = jax.new_ref(x, memory_space=hbm)
    got_ref = jax.empty_ref(jax.ShapeDtypeStruct((N_DEV,) + x.shape, x.dtype), memory_space=hbm)
    dma = pltpu.SemaphoreType.DMA

    @pl.kernel(mesh=plsc.ScalarSubcoreMesh(axis_name="seq", num_cores=1), name="sc_gather_w_in",
               scratch_types=(dma,) * 15, compiler_params=pltpu.CompilerParams(collective_id=4))
    def launch(*sems):
        send, recv, own_sem = sems[:7], sems[7:14], sems[14]
        mx, my, mc = _me()
        me = 4 * mx + 2 * my + mc
        here, sibling = (mx, my, mc), (mx, my, 1 - mc)
        chips = [(1 - mx, my), (mx, 1 - my), (1 - mx, 1 - my)]
        barrier = pltpu.get_barrier_semaphore()
        for peer in [sibling] + [(cx, cy, mc) for cx, cy in chips]:
            pl.semaphore_signal(barrier, inc=1, device_id=peer, device_id_type=MESH)
        pl.semaphore_wait(barrier, 4)

        def rc(k, src, dst, to):
            return pltpu.make_async_remote_copy(src_ref=src, dst_ref=dst, send_sem=send[k], recv_sem=recv[k],
                                                device_id=to, device_id_type=MESH)

        own = pltpu.make_async_copy(x_ref, got_ref.at[me], own_sem)
        own.start()
        first = [rc(0, x_ref, got_ref.at[me], sibling)] + [rc(1 + j, x_ref, got_ref.at[me], (cx, cy, mc))
                                                             for j, (cx, cy) in enumerate(chips)]
        for cp in first:
            cp.start()
        onward = []
        for j, (cx, cy) in enumerate(chips):
            landed = got_ref.at[4 * cx + 2 * cy + mc]
            rc(1 + j, x_ref, landed, here).wait_recv()
            cp = rc(4 + j, landed, landed, sibling)
            cp.start()
            onward.append(cp)
        rc(0, x_ref, got_ref.at[me ^ 1], here).wait_recv()
        for j, (cx, cy) in enumerate(chips):
            rc(4 + j, x_ref, got_ref.at[4 * cx + 2 * cy + 1 - mc], here).wait_recv()
        for cp in first + onward:
            cp.wait_send()
        own.wait()

    launch()
    return got_ref[...]


def _prologue(c8, cctx8, w_ada, b_my):
    nw = w_ada.shape[1]

    def body(c_ref, cctx_ref, w_ref, b_ref, act_ref, mod_ref, cmine_scr, call_scr, mine_scr, mall_scr, s1, r1, s2, r2):
        cmine_scr[...] = c_ref[...]
        _exchange_rows(cmine_scr, call_scr, s1, r1)
        rows = [call_scr[d][0:1, :] for d in range(N_DEV)] + [cctx_ref[0:1, :], jnp.zeros((7, D), F32)]
        s = jnp.concatenate(rows, axis=0)
        act = s * _sigmoid(s)
        act_ref[...] = act
        mine_scr[...] = jnp.dot(act.astype(BF16), w_ref[...].astype(BF16), preferred_element_type=F32) + b_ref[...]
        _exchange_rows(mine_scr, mall_scr, s2, r2)
        mod_ref[...] = mall_scr[...]

    sems = [pltpu.SemaphoreType.DMA((N_DEV - 1,))] * 4
    return pl.pallas_call(
        body, name="prologue", grid=(1,),
        in_specs=[_full((8, D)), _full((8, D)), _full((D, nw)), _full((1, nw))],
        out_specs=[_full((16, D)), _full((N_DEV, 16, nw))],
        out_shape=[_sds((16, D), F32), _sds((N_DEV, 16, nw), F32)],
        scratch_shapes=[pltpu.VMEM((8, D), F32), pltpu.VMEM((N_DEV, 8, D), F32), pltpu.VMEM((16, nw), F32),
                        pltpu.VMEM((N_DEV, 16, nw), F32)] + sems,
        compiler_params=_params(("arbitrary",)),
    )(c8, cctx8, w_ada, b_my)


def _gather_rows(x, name):
    def body(x_ref, out_ref, send_sems, recv_sems):
        _exchange_rows(x_ref, out_ref, send_sems, recv_sems)

    return pl.pallas_call(
        body, name=name,
        out_shape=_sds((N_DEV,) + x.shape, x.dtype),
        in_specs=[pl.BlockSpec(memory_space=pltpu.VMEM)],
        out_specs=pl.BlockSpec(memory_space=pltpu.VMEM),
        scratch_shapes=[pltpu.SemaphoreType.DMA((N_DEV - 1,)), pltpu.SemaphoreType.DMA((N_DEV - 1,))],
        compiler_params=pltpu.CompilerParams(vmem_limit_bytes=VMEM_LIMIT),
    )(x)


def _ada_bwd(act, dmod_my, w_ada, m, v, tr=256):
    nw = w_ada.shape[1]

    def body(act_ref, dm_ref, w_ref, m_ref, v_ref, g_ref, d_ref, m2_ref, v2_ref, pc_ref):
        dm = dm_ref[...].astype(BF16)
        g = lax.dot_general(act_ref[...].astype(BF16), dm, TN, preferred_element_type=F32)
        w = w_ref[...]
        delta, m2, v2 = _adamw(w, g, m_ref[...], v_ref[...])
        g_ref[...] = g
        d_ref[...] = delta
        m2_ref[...] = m2
        v2_ref[...] = v2
        pc_ref[...] = lax.dot_general(dm[8:16, :], w.astype(BF16), NT, preferred_element_type=F32)

    wspec = _row(tr, nw)
    return pl.pallas_call(
        body, name="ada_bwd", grid=(D // tr,),
        in_specs=[pl.BlockSpec((16, tr), lambda i: (0, i)), _full((16, nw)), wspec, wspec, wspec],
        out_specs=[wspec, wspec, wspec, wspec, pl.BlockSpec((8, tr), lambda i: (0, i))],
        out_shape=[_sds((D, nw), F32)] * 4 + [_sds((8, D), F32)],
        compiler_params=_params(("arbitrary",)),
    )(act, dmod_my, w_ada, m, v)


def _k_in(x, modv, w_in, cos, sin, tm, comm=None):
    L = x.shape[0]

    def body(x_ref, mod_ref, w_ref, cos_ref, sin_ref, h_ref, q_ref, k_ref, v_ref, u_ref, vb_ref, ga_ref, gb_ref):
        n, _ = _ln(x_ref[...])
        h = (n * (1.0 + mod_ref[1:2, :]) + mod_ref[0:1, :]).astype(BF16)
        h_ref[...] = h
        c, s = cos_ref[...], sin_ref[...]

        def proj(lo, width):
            return lax.dot_general(h, w_ref[lo:lo + width, :], NT, preferred_element_type=F32)

        for i in range(2):
            qh = proj(O_Q + i * 256, 256)
            for j in range(2):
                q_ref[:, i * 256 + j * 128:i * 256 + (j + 1) * 128] = (
                    _rope(qh[:, j * 128:(j + 1) * 128], c, s) * Q_SCALE).astype(BF16)
        kv = proj(O_K, 2 * KV_W)
        k_ref[...] = _rope(kv[:, :KV_W], c, s).astype(BF16)
        v_ref[...] = kv[:, KV_W:].astype(BF16)
        u_ref[...] = proj(O_U, GM_W).astype(BF16)
        vb_ref[...] = proj(O_VB, GM_W).astype(BF16)
        ga_ref[...] = proj(O_GA, D).astype(BF16)
        gb_ref[...] = proj(O_GB, D).astype(BF16)

    widths = [D, Q_W, KV_W, KV_W, GM_W, GM_W, D, D]
    return _call(
        body, name="fwd_in", grid=(L // tm,),
        in_specs=[_row(tm, D), _full((8, D)), _resident((IN_W, D)), _row(tm, 128), _row(tm, 128)],
        out_specs=[_row(tm, w) for w in widths],
        out_shape=[_sds((L, w), BF16) for w in widths],
        args=(x, modv, w_in, cos, sin), comm=comm)


def _k_ctx(ctx, modc, w_kv):
    C = ctx.shape[0]

    def body(c_ref, mod_ref, w_ref, hc_ref, kc_ref, vc_ref):
        n, _ = _ln(c_ref[...])
        hc = (n * (1.0 + mod_ref[1:2, :]) + mod_ref[0:1, :]).astype(BF16)
        hc_ref[...] = hc
        kv = lax.dot_general(hc, w_ref[...], NT, preferred_element_type=F32)
        kc_ref[...] = kv[:, :KV_W].astype(BF16)
        vc_ref[...] = kv[:, KV_W:].astype(BF16)

    return pl.pallas_call(
        body, name="fwd_ctx", grid=(1,),
        in_specs=[_full((C, D)), _full((8, D)), _full((2 * KV_W, D))],
        out_specs=[_full((C, D)), _full((C, KV_W)), _full((C, KV_W))],
        out_shape=[_sds((C, D), BF16), _sds((C, KV_W), BF16), _sds((C, KV_W), BF16)],
        compiler_params=_params(("arbitrary",)),
    )(ctx, modc, w_kv)


def _attn_bias():
    r = (np.arange(GQA_GROUP * BLK) & (BLK - 1))[:, None]
    j = np.arange(3 * BLK)[None, :]
    band = np.abs(j - BLK - r) <= BLK
    variants = [band & (j >= BLK), band, band & (j < 2 * BLK)]
    return jnp.asarray(np.stack([np.where(v, 0.0, NEG_INF) for v in variants]), F32)


def _masked(s, bias, C):
    return jnp.concatenate([s[:, :C], s[:, C:] + bias], axis=1)


def _sink_col(sink_ref, hk):
    grp = lax.broadcasted_iota(jnp.int32, (GQA_GROUP * BLK, 1), 0) >> 7
    col = jnp.full((GQA_GROUP * BLK, 1), sink_ref[hk * GQA_GROUP], F32)
    for g in range(1, GQA_GROUP):
        col = jnp.where(grp == g, sink_ref[hk * GQA_GROUP + g], col)
    return col


ATTN_FWD_BLOCKS = 4


def _k_attn(sink, q, k, v, kc, vc, bias, comm=None):
    L = q.shape[0]
    C = kc.shape[0]
    nb = L // BLK
    nq = min(ATTN_FWD_BLOCKS, nb)
    steps = nb // nq

    def body(sink_ref, q_ref, kp_ref, km_ref, kx_ref, vp_ref, vm_ref, vx_ref, kc_ref, vc_ref, bias_ref, ya_ref, lse_ref):
        i = pl.program_id(0)
        chains = [(qb, hk) for qb in range(nq) for hk in range(N_KV_HEADS)]

        def band(qb):
            first = jnp.where(i == 0, 0, 1) if qb == 0 else 1
            return bias_ref[jnp.where(i == steps - 1, 2, first) if qb == nq - 1 else first]

        def keys(ctx_ref, p_ref, m_ref, x_ref, qb, hk):
            sl = slice(hk * HEAD_DIM, (hk + 1) * HEAD_DIM)
            blocks = [p_ref[:, sl]] + [m_ref[j * BLK:(j + 1) * BLK, sl] for j in range(nq)] + [x_ref[:, sl]]
            return jnp.concatenate([ctx_ref[:, sl]] + blocks[qb:qb + 3], axis=0)

        def queries(qb, hk):
            return jnp.concatenate(
                [q_ref[qb * BLK:(qb + 1) * BLK, (hk * GQA_GROUP + g) * HEAD_DIM:(hk * GQA_GROUP + g + 1) * HEAD_DIM]
                 for g in range(GQA_GROUP)], axis=0)

        def scores(qb, hk):
            return _masked(lax.dot_general(queries(qb, hk), keys(kc_ref, kp_ref, km_ref, kx_ref, qb, hk), NT,
                                           preferred_element_type=F32), band(qb), C)

        ahead = 2
        s = [scores(*c) for c in chains[:ahead]]
        for n, (qb, hk) in enumerate(chains):
            if n + ahead < len(chains):
                s.append(scores(*chains[n + ahead]))
            s_ = s[n]
            sink_c = _sink_col(sink_ref, hk)
            m = jnp.maximum(jnp.max(s_, axis=1, keepdims=True), sink_c)
            p = jnp.exp(s_ - m)
            den = jnp.sum(p, axis=1, keepdims=True) + jnp.exp(sink_c - m)
            o = jnp.dot(p.astype(BF16), keys(vc_ref, vp_ref, vm_ref, vx_ref, qb, hk), preferred_element_type=F32) * (1.0 / den)
            lse = m + jnp.log(den)
            rows = slice(qb * BLK, (qb + 1) * BLK)
            for g in range(GQA_GROUP):
                h = hk * GQA_GROUP + g
                ya_ref[rows, h * HEAD_DIM:(h + 1) * HEAD_DIM] = o[g * BLK:(g + 1) * BLK, :].astype(BF16)
                lse_ref[rows, h:h + 1] = lse[g * BLK:(g + 1) * BLK, :]

    kv3 = [pl.BlockSpec((BLK, KV_W), lambda i: (jnp.maximum(nq * i - 1, 0), 0)),
           pl.BlockSpec((nq * BLK, KV_W), lambda i: (i, 0)),
           pl.BlockSpec((BLK, KV_W), lambda i: (jnp.minimum(nq * i + nq, nb - 1), 0))]
    return _call(
        body, name="fwd_attn", grid=(steps,),
        in_specs=[pl.BlockSpec(memory_space=pltpu.SMEM), _row(nq * BLK, Q_W)] + kv3 + kv3
                 + [_full((C, KV_W)), _full((C, KV_W)), _full((3, GQA_GROUP * BLK, 3 * BLK))],
        out_specs=[_row(nq * BLK, Q_W), _row(nq * BLK, N_Q_HEADS)],
        out_shape=[_sds((L, Q_W), BF16), _sds((L, N_Q_HEADS), F32)],
        args=(sink, q, k, k, k, v, v, v, kc, vc, bias), comm=comm)


GMLP_CHUNKS = 4


def _split_pair(t):
    low = lax.broadcasted_iota(jnp.int32, t.shape, 1) < GROUP_DIM
    zero = jnp.zeros_like(t)
    return jnp.where(low, t, zero), jnp.where(low, zero, t)


def _gmlp_spatial(w_ref, t_b, nch):
    rows = []
    for c in range(nch):
        tiles = []
        for pr in range(N_GROUPS // 2):
            lo, hi = _split_pair(t_b[c * BLK:(c + 1) * BLK, pr * 128:(pr + 1) * 128])
            tiles.append(jnp.dot(w_ref[2 * pr], lo, preferred_element_type=F32)
                         + jnp.dot(w_ref[2 * pr + 1], hi, preferred_element_type=F32))
        rows.append(jnp.concatenate(tiles, axis=1))
    return jnp.concatenate(rows, axis=0)


def _gmlp_fwd_vals(u, vb, lnv_ref, ws_ref, bsp_ref, nch):
    uf = u.astype(F32)
    vf = vb.astype(F32)
    gu, tu = _gelu(uf)
    gv, tv = _gelu(vf)
    vhat, rstd = _ln(gv)
    vn = (vhat * lnv_ref[0:1, :] + lnv_ref[1:2, :]).astype(BF16)
    s = _gmlp_spatial(ws_ref, vn, nch) + jnp.concatenate([bsp_ref[...]] * nch, axis=0)
    return uf, vf, gu, tu, tv, vhat, rstd, vn, s


def _k_gmlp(u, vb, lnv, ws, bsp):
    L = u.shape[0]
    nch = min(GMLP_CHUNKS, L // BLK)
    tm = nch * BLK

    def body(u_ref, vb_ref, lnv_ref, ws_ref, bsp_ref, yb_ref):
        _, _, gu, _, _, _, _, _, s = _gmlp_fwd_vals(u_ref[...], vb_ref[...], lnv_ref, ws_ref, bsp_ref, nch)
        yb_ref[...] = (gu * s).astype(BF16)

    return pl.pallas_call(
        body, name="fwd_gmlp", grid=(L // tm,),
        in_specs=[_row(tm, GM_W), _row(tm, GM_W), _full((8, GM_W)), _full((N_GROUPS, BLK, BLK)), _full((BLK, GM_W))],
        out_specs=_row(tm, GM_W),
        out_shape=_sds((L, GM_W), BF16),
        compiler_params=_params(("arbitrary",)),
    )(u, vb, lnv, ws, bsp)


def _k_merge(x, ya, yb, ga, gb, w_a, w_b, w_o, modv, lnv, tm):
    L = x.shape[0]

    def body(x_ref, ya_ref, yb_ref, ga_ref, gb_ref, wa_ref, wb_ref, wo_ref, mod_ref, ln_ref,
             mg_ref, mix_ref, xm_ref, h2_ref):
        a = jnp.dot(ya_ref[...], wa_ref[...], preferred_element_type=F32)
        b = jnp.dot(yb_ref[...], wb_ref[...], preferred_element_type=F32)
        merged = (_sigmoid(ga_ref[...].astype(F32)) * a + _sigmoid(gb_ref[...].astype(F32)) * b).astype(BF16)
        mg_ref[...] = merged
        mix = jnp.dot(merged, wo_ref[...], preferred_element_type=F32)
        mix_ref[...] = mix.astype(BF16)
        r1 = ALPHA * x_ref[...] + mod_ref[2:3, :] * mix
        r1hat, _ = _ln(r1)
        xm = r1hat * ln_ref[0:1, :] + ln_ref[1:2, :]
        xm_ref[...] = xm
        n2, _ = _ln(xm)
        h2_ref[...] = (n2 * (1.0 + mod_ref[4:5, :]) + mod_ref[3:4, :]).astype(BF16)

    return pl.pallas_call(
        body, name="fwd_merge", grid=(L // tm,),
        in_specs=[_row(tm, D), _row(tm, Q_W), _row(tm, GM_W), _row(tm, D), _row(tm, D),
                  _resident((Q_W, D)), _resident((GM_W, D)), _resident((D, D)), _full((8, D)), _full((8, D))],
        out_specs=[_row(tm, D)] * 4,
        out_shape=[_sds((L, D), BF16), _sds((L, D), BF16), _sds((L, D), F32), _sds((L, D), BF16)],
        compiler_params=_params(("arbitrary",)),
    )(x, ya, yb, ga, gb, w_a, w_b, w_o, modv, lnv)


V7X_MXU_COLUMNS = 256
FFN_CHUNK = V7X_MXU_COLUMNS


def _k_ffn(h2, xm, tgt, w_fi, w_fo, modv, lnv, tm):
    L = h2.shape[0]

    def body(h2_ref, xm_ref, t_ref, wi_ref, wo_ref, mod_ref, ln_ref, gate_ref, up_ref, a_ref, dr2_ref, df_ref, acc_ref):
        @pl.when(pl.program_id(0) == 0)
        def _():
            acc_ref[...] = jnp.zeros_like(acc_ref)

        h2v = h2_ref[...]
        ch = FFN_CHUNK
        chunks = [j * ch for j in range(FFN_H // ch)]

        def project(lo):
            return (lax.dot_general(h2v, wi_ref[lo:lo + ch, :], NT, preferred_element_type=F32),
                    lax.dot_general(h2v, wi_ref[FFN_H + lo:FFN_H + lo + ch, :], NT, preferred_element_type=F32))

        f = jnp.zeros((tm, D), F32)
        ahead = [project(chunks[0])]
        for j, lo in enumerate(chunks):
            if j + 1 < len(chunks):
                ahead.append(project(chunks[j + 1]))
            gate, up = ahead[j]
            act = (gate * _sigmoid(gate) * up).astype(BF16)
            gate_ref[:, lo:lo + ch] = gate.astype(BF16)
            up_ref[:, lo:lo + ch] = up.astype(BF16)
            a_ref[:, lo:lo + ch] = act
            f = f + jnp.dot(act, wo_ref[lo:lo + ch, :], preferred_element_type=F32)
        gate2 = mod_ref[5:6, :]
        r2 = ALPHA * xm_ref[...] + gate2 * f
        r2hat, rstd = _ln(r2)
        y = r2hat * ln_ref[2:3, :] + ln_ref[3:4, :]
        err = y - t_ref[...]
        dy = err * (1.0 / D)
        dr2 = _ln_bwd(dy * ln_ref[2:3, :], r2hat, rstd)
        dr2_ref[...] = dr2
        df_ref[...] = (gate2 * dr2).astype(BF16)
        acc_ref[0:1, :] += _colsum(dy * r2hat)
        acc_ref[1:2, :] += _colsum(dy)
        acc_ref[2:3, :] += _colsum(dr2 * f)
        acc_ref[3:4, :] += _colsum(err * err) * (0.5 / D)

    return pl.pallas_call(
        body, name="fwd_ffn", grid=(L // tm,),
        in_specs=[_row(tm, D), _row(tm, D), _row(tm, D), _resident((2 * FFN_H, D)), _resident((FFN_H, D)),
                  _full((8, D)), _full((8, D))],
        out_specs=[_row(tm, FFN_H)] * 3 + [_row(tm, D), _row(tm, D), _full((8, D))],
        out_shape=[_sds((L, FFN_H), BF16)] * 3 + [_sds((L, D), F32), _sds((L, D), BF16), _sds((8, D), F32)],
        compiler_params=_params(("arbitrary",)),
    )(h2, xm, tgt, w_fi, w_fo, modv, lnv)


def _k_ffn_bwd(df, gate, up, xm, dr2, x, mix, w_fi, w_fo, modv, lnv, tm):
    L = df.shape[0]

    def body(df_ref, gate_ref, up_ref, xm_ref, dr2_ref, x_ref, mix_ref, wi_ref, wo_ref, mod_ref, ln_ref,
             dF_ref, dmix_ref, dxp_ref, acc_ref):
        @pl.when(pl.program_id(0) == 0)
        def _():
            acc_ref[...] = jnp.zeros_like(acc_ref)

        dfv = df_ref[...]
        ch = FFN_CHUNK
        chunks = [j * ch for j in range(FFN_H // ch)]

        def d_act(lo):
            return lax.dot_general(dfv, wo_ref[lo:lo + ch, :], NT, preferred_element_type=F32)

        n2, rstd2 = _ln(xm_ref[...])
        mixf = mix_ref[...].astype(F32)
        gate1 = mod_ref[2:3, :]
        r1hat, rstd1 = _ln(ALPHA * x_ref[...] + gate1 * mixf)
        dh2 = jnp.zeros((tm, D), F32)
        das = [d_act(chunks[0])]
        for j, lo in enumerate(chunks):
            if j + 1 < len(chunks):
                das.append(d_act(chunks[j + 1]))
            da = das[j]
            gate = gate_ref[:, lo:lo + ch].astype(F32)
            upv = up_ref[:, lo:lo + ch].astype(F32)
            sg = _sigmoid(gate)
            d_gate = (da * upv * (sg * (1.0 + gate * (1.0 - sg)))).astype(BF16)
            d_up = (da * (gate * sg)).astype(BF16)
            dF_ref[:, lo:lo + ch] = d_gate
            dF_ref[:, FFN_H + lo:FFN_H + lo + ch] = d_up
            dh2 = dh2 + jnp.dot(d_gate, wi_ref[lo:lo + ch, :], preferred_element_type=F32)
            dh2 = dh2 + jnp.dot(d_up, wi_ref[FFN_H + lo:FFN_H + lo + ch, :], preferred_element_type=F32)
        acc_ref[0:1, :] += _colsum(dh2)
        acc_ref[1:2, :] += _colsum(dh2 * n2)
        dxm = ALPHA * dr2_ref[...] + _ln_bwd(dh2 * (1.0 + mod_ref[4:5, :]), n2, rstd2)
        acc_ref[2:3, :] += _colsum(dxm * r1hat)
        acc_ref[3:4, :] += _colsum(dxm)
        dr1 = _ln_bwd(dxm * ln_ref[0:1, :], r1hat, rstd1)
        dmix_ref[...] = (gate1 * dr1).astype(BF16)
        dxp_ref[...] = ALPHA * dr1
        acc_ref[4:5, :] += _colsum(dr1 * mixf)

    return pl.pallas_call(
        body, name="bwd_ffn", grid=(L // tm,),
        in_specs=[_row(tm, D), _row(tm, FFN_H), _row(tm, FFN_H), _row(tm, D), _row(tm, D), _row(tm, D), _row(tm, D),
                  _resident((2 * FFN_H, D)), _resident((FFN_H, D)), _full((8, D)), _full((8, D))],
        out_specs=[_row(tm, 2 * FFN_H), _row(tm, D), _row(tm, D), _full((8, D))],
        out_shape=[_sds((L, 2 * FFN_H), BF16), _sds((L, D), BF16), _sds((L, D), F32), _sds((8, D), F32)],
        compiler_params=_params(("arbitrary",)),
    )(df, gate, up, xm, dr2, x, mix, w_fi, w_fo, modv, lnv)


def _k_merge_bwd(dmix, merged, ya, yb, ga, gb, w_a, w_b, w_o, tm):
    L = dmix.shape[0]
    n = L // tm

    def body(dmix_ref, mg_ref, ya_ref, yb_ref, ga_ref, gb_ref, wa_ref, wb_ref, wo_ref,
             dga_ref, dgb_ref, dya_ref, dyb_ref, gwa_ref, gwb_ref, gwo_ref, acc_a, acc_b, acc_o):
        i = pl.program_id(0)

        @pl.when(i == 0)
        def _():
            for r in (acc_a, acc_b, acc_o):
                r[...] = jnp.zeros_like(r)

        dmixv = dmix_ref[...]
        dmg = lax.dot_general(dmixv, wo_ref[...], NT, preferred_element_type=F32)
        acc_o[...] += lax.dot_general(mg_ref[...], dmixv, TN, preferred_element_type=F32)
        ya = ya_ref[...]
        a = jnp.dot(ya, wa_ref[...], preferred_element_type=F32)
        sa = _sigmoid(ga_ref[...].astype(F32))
        dA = (dmg * sa).astype(BF16)
        dga_ref[...] = (dmg * a * (sa * (1.0 - sa))).astype(BF16)
        dya_ref[...] = lax.dot_general(dA, wa_ref[...], NT, preferred_element_type=F32).astype(BF16)
        acc_a[...] += lax.dot_general(ya, dA, TN, preferred_element_type=F32)
        yb = yb_ref[...]
        b = jnp.dot(yb, wb_ref[...], preferred_element_type=F32)
        sb = _sigmoid(gb_ref[...].astype(F32))
        dB = (dmg * sb).astype(BF16)
        dgb_ref[...] = (dmg * b * (sb * (1.0 - sb))).astype(BF16)
        dyb_ref[...] = lax.dot_general(dB, wb_ref[...], NT, preferred_element_type=F32).astype(BF16)
        acc_b[...] += lax.dot_general(yb, dB, TN, preferred_element_type=F32)

        @pl.when(i == n - 1)
        def _():
            gwa_ref[...] = acc_a[...].astype(BF16)
            gwb_ref[...] = acc_b[...].astype(BF16)
            gwo_ref[...] = acc_o[...].astype(BF16)

    return pl.pallas_call(
        body, name="bwd_merge", grid=(n,),
        in_specs=[_row(tm, D), _row(tm, D), _row(tm, Q_W), _row(tm, GM_W), _row(tm, D), _row(tm, D),
                  _resident((Q_W, D)), _resident((GM_W, D)), _resident((D, D))],
        out_specs=[_row(tm, D), _row(tm, D), _row(tm, Q_W), _row(tm, GM_W), _full((Q_W, D)), _full((GM_W, D)), _full((D, D))],
        out_shape=[_sds((L, D), BF16), _sds((L, D), BF16), _sds((L, Q_W), BF16), _sds((L, GM_W), BF16),
                   _sds((Q_W, D), BF16), _sds((GM_W, D), BF16), _sds((D, D), BF16)],
        scratch_shapes=[pltpu.VMEM((Q_W, D), F32), pltpu.VMEM((GM_W, D), F32), pltpu.VMEM((D, D), F32)],
        compiler_params=_params(("arbitrary",)),
    )(dmix, merged, ya, yb, ga, gb, w_a, w_b, w_o)


def _k_gmlp_bwd(u, vb, dyb, lnv, ws, wst, bsp):
    L = u.shape[0]
    nch = min(GMLP_CHUNKS, L // BLK)
    tm = nch * BLK

    def body(u_ref, vb_ref, dyb_ref, lnv_ref, ws_ref, wst_ref, bsp_ref, du_ref, dvb_ref, gws_ref, gbst_ref, gln_ref):
        @pl.when(pl.program_id(0) == 0)
        def _():
            gws_ref[...] = jnp.zeros_like(gws_ref)
            gbst_ref[...] = jnp.zeros_like(gbst_ref)
            gln_ref[...] = jnp.zeros_like(gln_ref)

        uf, vf, gu, tu, tv, vhat, rstd, vn, s = _gmlp_fwd_vals(u_ref[...], vb_ref[...], lnv_ref, ws_ref, bsp_ref, nch)
        dyb_f = dyb_ref[...].astype(F32)
        du_ref[...] = (dyb_f * s * _gelu_grad(uf, tu)).astype(BF16)
        ds = dyb_f * gu
        ds_b = ds.astype(BF16)
        for pr in range(N_GROUPS // 2):
            lanes = slice(pr * 128, (pr + 1) * 128)
            gw_lo = gw_hi = ds_sum = None
            for c in range(nch):
                rows = slice(c * BLK, (c + 1) * BLK)
                lo, hi = _split_pair(ds_b[rows, lanes])
                t_lo = lax.dot_general(lo, vn[rows, lanes], NT, preferred_element_type=F32)
                t_hi = lax.dot_general(hi, vn[rows, lanes], NT, preferred_element_type=F32)
                gw_lo = t_lo if c == 0 else gw_lo + t_lo
                gw_hi = t_hi if c == 0 else gw_hi + t_hi
                ds_sum = ds[rows, lanes] if c == 0 else ds_sum + ds[rows, lanes]
            gws_ref[2 * pr] += gw_lo
            gws_ref[2 * pr + 1] += gw_hi
            b_lo, b_hi = _split_pair(ds_sum)
            gbst_ref[:, 2 * pr:2 * pr + 1] += jnp.sum(b_lo, axis=1, keepdims=True)
            gbst_ref[:, 2 * pr + 1:2 * pr + 2] += jnp.sum(b_hi, axis=1, keepdims=True)
        dvn = _gmlp_spatial(wst_ref, ds_b, nch)
        gln_ref[0:1, :] += _colsum(dvn * vhat)
        gln_ref[1:2, :] += _colsum(dvn)
        dgv = _ln_bwd(dvn * lnv_ref[0:1, :], vhat, rstd)
        dvb_ref[...] = (dgv * _gelu_grad(vf, tv)).astype(BF16)

    return pl.pallas_call(
        body, name="bwd_gmlp", grid=(L // tm,),
        in_specs=[_row(tm, GM_W)] * 3 + [_full((8, GM_W)), _full((N_GROUPS, BLK, BLK)), _full((N_GROUPS, BLK, BLK)),
                                         _full((BLK, GM_W))],
        out_specs=[_row(tm, GM_W), _row(tm, GM_W), _full((N_GROUPS, BLK, BLK)), _full((BLK, N_GROUPS)), _full((8, GM_W))],
        out_shape=[_sds((L, GM_W), BF16), _sds((L, GM_W), BF16), _sds((N_GROUPS, BLK, BLK), F32),
                   _sds((BLK, N_GROUPS), F32), _sds((8, GM_W), F32)],
        compiler_params=_params(("arbitrary",)),
    )(u, vb, dyb, lnv, ws, wst, bsp)


ATTN_BWD_BLOCKS = 2


def _k_attn_bwd(sink, q, k, v, kc, vc, dya, lse, cos, sin, bias, comm=None):
    L = q.shape[0]
    C = kc.shape[0]
    nb = L // BLK
    nq = min(ATTN_BWD_BLOCKS, nb)
    steps = nb // nq
    NK = C + 3 * BLK
    chains = [(qb, hk) for qb in range(nq) for hk in range(N_KV_HEADS)]

    def body(sink_ref, q_ref, kp_ref, km_ref, kx_ref, vp_ref, vm_ref, vx_ref, kc_ref, vc_ref, do_ref, lse_ref,
             cq_ref, sq_ref, cl_ref, sl_ref, bias_ref,
             dq_ref, dk_ref, dv_ref, dkc_ref, dvc_ref, dsink_ref,
             dq_scr, ck_scr, cv_scr, k1_acc, k2_acc, v1_acc, v2_acc):
        i = pl.program_id(0)

        @pl.when(i == 0)
        def _():
            for r in (k1_acc, k2_acc, v1_acc, v2_acc, dkc_ref, dvc_ref, dsink_ref):
                r[...] = jnp.zeros_like(r)

        @pl.when(i < steps)
        def _():
            def band(qb):
                first = jnp.where(i == 0, 0, 1) if qb == 0 else 1
                return bias_ref[jnp.where(i == steps - 1, 2, first) if qb == nq - 1 else first]

            def lanes(hk):
                return slice(hk * HEAD_DIM, (hk + 1) * HEAD_DIM)

            def keys(ctx_ref, p_ref, m_ref, x_ref, qb, hk):
                sl = lanes(hk)
                blocks = [p_ref[:, sl]] + [m_ref[j * BLK:(j + 1) * BLK, sl] for j in range(nq)] + [x_ref[:, sl]]
                return jnp.concatenate([ctx_ref[:, sl]] + blocks[qb:qb + 3], axis=0)

            def stacked(ref, qb, hk, width):
                return jnp.concatenate(
                    [ref[qb * BLK:(qb + 1) * BLK, (hk * GQA_GROUP + g) * width:(hk * GQA_GROUP + g + 1) * width]
                     for g in range(GQA_GROUP)], axis=0)

            def scores(qb, hk):
                kcat = keys(kc_ref, kp_ref, km_ref, kx_ref, qb, hk)
                qg = stacked(q_ref, qb, hk, HEAD_DIM)
                s = _masked(lax.dot_general(qg, kcat, NT, preferred_element_type=F32), band(qb), C)
                dog = stacked(do_ref, qb, hk, HEAD_DIM)
                dp = lax.dot_general(dog, keys(vc_ref, vp_ref, vm_ref, vx_ref, qb, hk), NT, preferred_element_type=F32)
                return kcat, qg, dog, s, dp

            def softmax_bwd(qb, hk, s, dp):
                lse_c = stacked(lse_ref, qb, hk, 1)
                p = jnp.exp(s - lse_c)
                delta = jnp.sum(p * dp, axis=1, keepdims=True)
                ds = (p * (dp - delta)).astype(BF16)
                p_sink = jnp.exp(_sink_col(sink_ref, hk) - lse_c) * delta
                return p.astype(BF16), ds, p_sink

            def put_dq(qb, hk, dqs, p_sink):
                for g in range(GQA_GROUP):
                    h = hk * GQA_GROUP + g
                    dq_scr[qb * BLK:(qb + 1) * BLK, h * HEAD_DIM:(h + 1) * HEAD_DIM] = dqs[g * BLK:(g + 1) * BLK, :]
                    tot = jnp.sum(p_sink[g * BLK:(g + 1) * BLK, :], axis=0, keepdims=True)
                    dsink_ref[h:h + 1, :] -= jnp.broadcast_to(tot, (1, 128))

            ahead = 4
            sc = [scores(*c) for c in chains[:ahead]]
            pending = None
            for n, (qb, hk) in enumerate(chains):
                if n + ahead < len(chains):
                    sc.append(scores(*chains[n + ahead]))
                kcat, qg, dog, s, dp = sc[n]
                pb, ds, p_sink = softmax_bwd(qb, hk, s, dp)
                if pending is not None:
                    pqb, phk, pds, ppb, pqg, pdog = pending
                    ck_scr[pqb, :, lanes(phk)] = lax.dot_general(pds, pqg, TN, preferred_element_type=F32)
                    cv_scr[pqb, :, lanes(phk)] = lax.dot_general(ppb, pdog, TN, preferred_element_type=F32)
                put_dq(qb, hk, jnp.dot(ds, kcat, preferred_element_type=F32), p_sink)
                pending = (qb, hk, ds, pb, qg, dog)
            pqb, phk, pds, ppb, pqg, pdog = pending
            ck_scr[pqb, :, lanes(phk)] = lax.dot_general(pds, pqg, TN, preferred_element_type=F32)
            cq, sq = cq_ref[...], sq_ref[...]
            for j in range(4):
                dq_ref[:, j * 128:(j + 1) * 128] = _unrope(dq_scr[:, j * 128:(j + 1) * 128] * Q_SCALE, cq, sq).astype(BF16)
            cv_scr[pqb, :, lanes(phk)] = lax.dot_general(ppb, pdog, TN, preferred_element_type=F32)
            dkc_ref[...] += functools.reduce(lambda a, b: a + b, [ck_scr[qb, 0:C, :] for qb in range(nq)])
            dvc_ref[...] += functools.reduce(lambda a, b: a + b, [cv_scr[qb, 0:C, :] for qb in range(nq)])

        @pl.when(i >= steps)
        def _():
            ck_scr[...] = jnp.zeros_like(ck_scr)
            cv_scr[...] = jnp.zeros_like(cv_scr)

        def slot(scr, r, carried):
            parts = [scr[qb, C + (r - qb) * BLK:C + (r - qb + 1) * BLK, :] for qb in range(nq) if 0 <= r - qb <= 2]
            total = functools.reduce(lambda a, b: a + b, parts)
            return total if carried is None else carried[...] + total

        for r in range(nq):
            rows = slice(r * BLK, (r + 1) * BLK)
            carried_k, carried_v = ((k1_acc, v1_acc), (k2_acc, v2_acc), (None, None))[min(r, 2)]
            tables = (cl_ref[...], sl_ref[...]) if r == 0 else (cq_ref[(r - 1) * BLK:r * BLK, :], sq_ref[(r - 1) * BLK:r * BLK, :])
            dk_ref[rows, :] = _unrope(slot(ck_scr, r, carried_k), *tables).astype(BF16)
            dv_ref[rows, :] = slot(cv_scr, r, carried_v).astype(BF16)
        k1_acc[...] = slot(ck_scr, nq, None)
        v1_acc[...] = slot(cv_scr, nq, None)
        k2_acc[...] = slot(ck_scr, nq + 1, None)
        v2_acc[...] = slot(cv_scr, nq + 1, None)

    last = steps - 1
    kv3 = [pl.BlockSpec((BLK, KV_W), lambda i: (jnp.clip(nq * i - 1, 0, nb - 1), 0)),
           pl.BlockSpec((nq * BLK, KV_W), lambda i: (jnp.minimum(i, last), 0)),
           pl.BlockSpec((BLK, KV_W), lambda i: (jnp.minimum(nq * i + nq, nb - 1), 0))]
    cur = lambda w: pl.BlockSpec((nq * BLK, w), lambda i: (jnp.minimum(i, last), 0))
    late = lambda w: pl.BlockSpec((BLK, w), lambda i: (jnp.clip(nq * i - 1, 0, nb - 1), 0))
    out2 = lambda w: pl.BlockSpec((nq * BLK, w), lambda i: (i, 0))
    return _call(
        body, name="bwd_attn", grid=(steps + 1,),
        in_specs=[pl.BlockSpec(memory_space=pltpu.SMEM), cur(Q_W)] + kv3 + kv3
                 + [_full((C, KV_W)), _full((C, KV_W)), cur(Q_W), cur(N_Q_HEADS), cur(128), cur(128), late(128), late(128),
                    _full((3, GQA_GROUP * BLK, 3 * BLK))],
        out_specs=[cur(Q_W), out2(KV_W), out2(KV_W), _full((C, KV_W)), _full((C, KV_W)), _full((8, 128))],
        out_shape=[_sds((L, Q_W), BF16), _sds((L + nq * BLK, KV_W), BF16), _sds((L + nq * BLK, KV_W), BF16),
                   _sds((C, KV_W), F32), _sds((C, KV_W), F32), _sds((8, 128), F32)],
        scratch=[pltpu.VMEM((nq * BLK, Q_W), F32), pltpu.VMEM((nq, NK, KV_W), F32), pltpu.VMEM((nq, NK, KV_W), F32)]
                + [pltpu.VMEM((BLK, KV_W), F32)] * 4,
        args=(sink, q, k, k, k, v, v, v, kc, vc, dya, lse, cos, sin, cos, sin, bias), comm=comm)


def _k_ctx_bwd(ctx, modc, hc, dkc, dvc, w_kv):
    C = ctx.shape[0]

    def body(c_ref, mod_ref, hc_ref, dkc_ref, dvc_ref, w_ref, gw_ref, dmod_ref):
        dkv = jnp.concatenate([dkc_ref[...], dvc_ref[...]], axis=1).astype(BF16)
        gw_ref[...] = lax.dot_general(dkv, hc_ref[...], TN, preferred_element_type=F32)
        dhc = jnp.dot(dkv, w_ref[...], preferred_element_type=F32)
        n, _ = _ln(c_ref[...])
        dmod_ref[...] = jnp.zeros_like(dmod_ref)
        dmod_ref[0:1, :] = _colsum(dhc)
        dmod_ref[1:2, :] = _colsum(dhc * n)

    return pl.pallas_call(
        body, name="bwd_ctx", grid=(1,),
        in_specs=[_full((C, D)), _full((8, D)), _full((C, D)), _full((C, KV_W)), _full((C, KV_W)), _full((2 * KV_W, D))],
        out_specs=[_full((2 * KV_W, D)), _full((8, D))],
        out_shape=[_sds((2 * KV_W, D), F32), _sds((8, D), F32)],
        compiler_params=_params(("arbitrary",)),
    )(ctx, modc, hc, dkc, dvc, w_kv)


def _k_in_bwd(dq, dk, dv, du, dvb, dga, dgb, x, dxp, w_in, modv, tm, comm=None):
    L = x.shape[0]
    parts = [(O_Q, Q_W), (O_K, KV_W), (O_V, KV_W), (O_U, GM_W), (O_VB, GM_W), (O_GA, D), (O_GB, D)]

    def body(dq_ref, dk_ref, dv_ref, du_ref, dvb_ref, dga_ref, dgb_ref, x_ref, dxp_ref, w_ref, mod_ref,
             dP_ref, gx_ref, acc_ref):
        @pl.when(pl.program_id(0) == 0)
        def _():
            acc_ref[...] = jnp.zeros_like(acc_ref)

        for (lo, width), r in zip(parts, (dq_ref, dk_ref, dv_ref, du_ref, dvb_ref, dga_ref, dgb_ref)):
            dP_ref[:, lo:lo + width] = r[...]
        n1, rstd1 = _ln(x_ref[...])
        dh = jnp.dot(dP_ref[...], w_ref[...], preferred_element_type=F32)
        acc_ref[0:1, :] += _colsum(dh)
        acc_ref[1:2, :] += _colsum(dh * n1)
        gx_ref[...] = dxp_ref[...] + _ln_bwd(dh * (1.0 + mod_ref[1:2, :]), n1, rstd1)

    return _call(
        body, name="bwd_in", grid=(L // tm,),
        in_specs=[_row(tm, w) for _, w in parts] + [_row(tm, D), _row(tm, D), _resident((IN_W, D)), _full((8, D))],
        out_specs=[_row(tm, IN_W), _row(tm, D), _full((8, D))],
        out_shape=[_sds((L, IN_W), BF16), _sds((L, D), F32), _sds((8, D), F32)],
        args=(dq, dk, dv, du, dvb, dga, dgb, x, dxp, w_in, modv), comm=comm)


def _wgrad(a, b, name, tk, tt, comm=None, extra=None):
    T, K = a.shape
    N = b.shape[1]
    nt = T // tt

    def body(*refs):
        a_ref, b_ref = refs[:2]
        o_ref, acc_ref = refs[-2:]
        j, t = pl.program_id(0), pl.program_id(1)

        @pl.when(t == 0)
        def _():
            acc_ref[...] = jnp.zeros_like(acc_ref)

        acc_ref[...] += lax.dot_general(a_ref[...], b_ref[...], TN, preferred_element_type=F32)

        if extra is not None:
            lo, rows = extra[0] % tk, extra[1].shape[0]

            @pl.when((t == nt - 1) & (j == extra[0] // tk))
            def _():
                acc_ref[lo:lo + rows, :] += refs[2][...]

        @pl.when(t == nt - 1)
        def _():
            o_ref[...] = acc_ref[...].astype(BF16)

    extra_specs = [] if extra is None else [pl.BlockSpec(extra[1].shape, lambda j, t: (0, 0))]
    (out,), got = _call(
        body, name=name, grid=(K // tk, nt),
        in_specs=[pl.BlockSpec((tt, tk), lambda j, t: (t, j)), pl.BlockSpec((tt, N), lambda j, t: (t, 0))] + extra_specs,
        out_specs=[pl.BlockSpec((tk, N), lambda j, t: (j, 0))],
        out_shape=[_sds((K, N), BF16)],
        scratch=[pltpu.VMEM((tk, N), F32)],
        args=(a, b) + (() if extra is None else (extra[1],)), comm=comm)
    return (out, got) if comm is not None else out


def _adamw_reduce(parts, w, m, v, name, tr):
    R, C = w.shape
    n_parts = parts.shape[0]

    def body(p_ref, w_ref, m_ref, v_ref, g_ref, d_ref, m2_ref, v2_ref):
        g = p_ref[0].astype(F32)
        for i in range(1, n_parts):
            g = g + p_ref[i].astype(F32)
        delta, m2, v2 = _adamw(w_ref[...], g, m_ref[...], v_ref[...])
        g_ref[...] = g
        d_ref[...] = delta
        m2_ref[...] = m2
        v2_ref[...] = v2

    spec = _row(tr, C)
    return pl.pallas_call(
        body, name=name, grid=(R // tr,),
        in_specs=[pl.BlockSpec((n_parts, tr, C), lambda i: (0, i, 0)), spec, spec, spec],
        out_specs=[spec] * 4,
        out_shape=[_sds((R, C), F32)] * 4,
        compiler_params=_params(("arbitrary",)),
    )(parts, w, m, v)


SMALL_ORDER = ("b_ada", "ln1_g", "ln1_b", "ln2_g", "ln2_b", "gmlp_ln_g", "gmlp_ln_b", "b_spatial", "attn_sink")


def _small_step(gath, params):
    flat = [a for name in SMALL_ORDER for a in params[name]]

    def grad_of(tot, name):
        if name == "b_ada":
            return jnp.concatenate([tot[r:r + 1, :] for r in range(6)], axis=1)
        if name in ("ln1_g", "ln1_b", "ln2_g", "ln2_b"):
            r = 8 + ("ln1_g", "ln1_b", "ln2_g", "ln2_b").index(name)
            return tot[r:r + 1, :]
        if name == "gmlp_ln_g":
            return tot[12:13, :GM_W]
        if name == "gmlp_ln_b":
            return tot[12:13, GM_W:]
        if name == "b_spatial":
            return jnp.concatenate([tot[13:14, g * BLK:(g + 1) * BLK] for g in range(N_GROUPS)], axis=0)[None]
        return tot[14:15, :N_Q_HEADS]

    def body(*refs):
        g_ref, in_refs = refs[0], refs[1:1 + len(flat)]
        tot_ref, out_refs = refs[1 + len(flat)], refs[2 + len(flat):]
        tot = g_ref[0]
        for i in range(1, N_DEV):
            tot = tot + g_ref[i]
        tot_ref[...] = tot
        tot_ref[0:2, :] = tot[0:2, :] + tot[6:8, :]
        tot_ref[15:16, :] = jnp.broadcast_to(jnp.sum(tot[15:16, :], axis=1, keepdims=True), (1, D))
        tot = tot_ref[...]
        for k, name in enumerate(SMALL_ORDER):
            w_ref, m_ref, v_ref = in_refs[3 * k:3 * k + 3]
            g = grad_of(tot, name)
            delta, m2, v2 = _adamw(w_ref[...], g, m_ref[...], v_ref[...])
            for r, val in zip(out_refs[4 * k:4 * k + 4], (g, delta, m2, v2)):
                r[...] = val

    res = pl.pallas_call(
        body, name="small_step", grid=(1,),
        in_specs=[_full((N_DEV, 16, D))] + [_full(a.shape) for a in flat],
        out_specs=[_full((16, D))] + [_full(params[name][0].shape) for name in SMALL_ORDER for _ in range(4)],
        out_shape=[_sds((16, D), F32)] + [_sds(params[name][0].shape, F32) for name in SMALL_ORDER for _ in range(4)],
        compiler_params=_params(("arbitrary",)),
    )(gath, *flat)
    return res[0], {name: res[1 + 4 * k:5 + 4 * k] for k, name in enumerate(SMALL_ORDER)}


def _cctx_finish(gath, c_ctx, m, v):
    def body(g_ref, c_ref, m_ref, v_ref, gr_ref, d_ref, m2_ref, v2_ref):
        ds = g_ref[0]
        for i in range(1, N_DEV):
            ds = ds + g_ref[i]
        c = c_ref[...]
        sg = _sigmoid(c)
        g = ds * (sg * (1.0 + c * (1.0 - sg)))
        delta, m2, v2 = _adamw(c, g, m_ref[...], v_ref[...])
        gr_ref[...] = g
        d_ref[...] = delta
        m2_ref[...] = m2
        v2_ref[...] = v2

    return pl.pallas_call(
        body, name="cctx_finish", grid=(1,),
        in_specs=[_full((N_DEV, 8, D))] + [_full((8, D))] * 3, out_specs=[_full((8, D))] * 4,
        out_shape=[_sds((8, D), F32)] * 4,
        compiler_params=_params(("arbitrary",)),
    )(gath, c_ctx, m, v)


def _pad_rows(a, rows):
    return jnp.concatenate([a, jnp.zeros((rows - a.shape[0], a.shape[1]), a.dtype)], axis=0)


def kernel(x, c, ctx, c_ctx, w_ada, b_ada, w_in, attn_sink, gmlp_ln_g, gmlp_ln_b, w_spatial, b_spatial, w_branch_a, w_branch_b, w_out, ln1_g, ln1_b, w_ffn_in, w_ffn_out, ln2_g, ln2_b, loss_target, m_c_ctx, m_w_ada, m_b_ada, m_w_in, m_attn_sink, m_gmlp_ln_g, m_gmlp_ln_b, m_w_spatial, m_b_spatial, m_w_branch_a, m_w_branch_b, m_w_out, m_ln1_g, m_ln1_b, m_w_ffn_in, m_w_ffn_out, m_ln2_g, m_ln2_b, v_c_ctx, v_w_ada, v_b_ada, v_w_in, v_attn_sink, v_gmlp_ln_g, v_gmlp_ln_b, v_w_spatial, v_b_spatial, v_w_branch_a, v_w_branch_b, v_w_out, v_ln1_g, v_ln1_b, v_w_ffn_in, v_w_ffn_out, v_ln2_g, v_ln2_b):
    L = x.shape[1]
    me = 4 * lax.axis_index("x") + 2 * lax.axis_index("y") + lax.axis_index("c")
    x2, tgt, ctx2 = x[0], loss_target[0], ctx[0]
    tiles = _Tiles(L)
    tm_in, tm, tt = tiles.wide, tiles.narrow, tiles.tokens

    transposed = ("w_in", "w_ffn_in")
    tr = lambda kname, a: a.T if kname in transposed else a
    big = dict(w_in=w_in[0].T, w_branch_a=w_branch_a[0], w_branch_b=w_branch_b[0], w_out=w_out[0],
               w_ffn_in=w_ffn_in[0].T, w_ffn_out=w_ffn_out[0])
    col_sharded = ("w_branch_a", "w_branch_b")
    shard_bf = {k: a.astype(BF16) for k, a in big.items()}

    def assemble(kname, g):
        if kname in col_sharded:
            return g.transpose(1, 0, 2).reshape(g.shape[1], N_DEV * g.shape[2])
        return g.reshape(N_DEV * g.shape[1], g.shape[2])

    def to_blocks(kname, g):
        if kname in col_sharded:
            return g.reshape(g.shape[0], N_DEV, g.shape[1] // N_DEV).transpose(1, 0, 2)
        return g.reshape(N_DEV, g.shape[0] // N_DEV, g.shape[1])

    full = {}
    n_ada = w_ada.shape[2]
    b_my = lax.dynamic_slice(b_ada, (0, me * n_ada), (1, n_ada))
    got_in = _sc_gather(shard_bf["w_in"])
    act, mod_all = _prologue(_pad_rows(c, 8), _pad_rows(c_ctx[None, :], 8), w_ada[0], b_my)
    full["w_in"] = assemble("w_in", got_in)
    mod_all = mod_all.transpose(1, 0, 2).reshape(16, 6 * D)
    modv = _pad_rows(lax.dynamic_slice(mod_all, (me, 0), (1, 6 * D)).reshape(6, D), 8)
    modc = _pad_rows(mod_all[8].reshape(6, D), 8)

    lnv = _pad_rows(jnp.concatenate([ln1_g, ln1_b, ln2_g, ln2_b], axis=0), 8)
    gm_lnv = _pad_rows(jnp.concatenate([gmlp_ln_g, gmlp_ln_b], axis=0), 8)
    ws_b = w_spatial[0].astype(BF16)
    wst_b = ws_b.transpose(0, 2, 1)
    bsp = jnp.repeat(b_spatial[0].T, GROUP_DIM, axis=1)
    sink = attn_sink[0]
    cos, sin = _rope_tables(L)
    bias = _attn_bias()
    w_kv = full["w_in"][O_K:O_K + 2 * KV_W, :]

    (h, q, k, v, u, vb, ga, gb), got = _k_in(
        x2, modv, full["w_in"], cos, sin, tm_in,
        comm=_Comm(gather=[shard_bf[kname] for kname in ("w_branch_a", "w_branch_b", "w_out", "w_ffn_out")]))
    for kname, g in zip(("w_branch_a", "w_branch_b", "w_out", "w_ffn_out"), got):
        full[kname] = assemble(kname, g)
    hc, kc, vc = _k_ctx(ctx2, modc, w_kv)
    (ya, lse), got = _k_attn(sink, q, k, v, kc, vc, bias, comm=_Comm(gather=[shard_bf["w_ffn_in"]]))
    full["w_ffn_in"] = assemble("w_ffn_in", got[0])
    yb = _k_gmlp(u, vb, gm_lnv, ws_b, bsp)
    merged, mix, xm, h2 = _k_merge(x2, ya, yb, ga, gb, full["w_branch_a"], full["w_branch_b"], full["w_out"], modv, lnv, tm_in)
    gate, up, act_f, dr2, df, acc_f = _k_ffn(h2, xm, tgt, full["w_ffn_in"], full["w_ffn_out"], modv, lnv, tm_in)

    dF, dmix, dxp, acc_b = _k_ffn_bwd(df, gate, up, xm, dr2, x2, mix, full["w_ffn_in"], full["w_ffn_out"], modv, lnv, tm)
    blk_fo = to_blocks("w_ffn_out", _wgrad(act_f, df, "wgrad_ffn_out", tiles.tk_ffn, tt))
    gw_fi, (rcv_fo,) = _wgrad(dF, h2, "wgrad_ffn_in", tiles.tk_ffn, tt, comm=_Comm(scatter=[blk_fo]))
    blk_fi = to_blocks("w_ffn_in", gw_fi)
    dga, dgb, dya, dyb, gw_a, gw_b, gw_o = _k_merge_bwd(
        dmix, merged, ya, yb, ga, gb, full["w_branch_a"], full["w_branch_b"], full["w_out"], tm_in)
    du, dvb, g_ws, g_bst, g_gln = _k_gmlp_bwd(u, vb, dyb, gm_lnv, ws_b, wst_b, bsp)
    (dq, dk_late, dv_late, dkc, dvc, g_sink), (gath_ws, rcv_fi) = _k_attn_bwd(
        sink, q, k, v, kc, vc, dya, lse, cos, sin, bias,
        comm=_Comm(gather=[g_ws.reshape(N_GROUPS * BLK, BLK)], scatter=[blk_fi]))
    dk, dv = dk_late[BLK:BLK + L], dv_late[BLK:BLK + L]
    blk_a, blk_b, blk_o = to_blocks("w_branch_a", gw_a), to_blocks("w_branch_b", gw_b), to_blocks("w_out", gw_o)
    (dP, grad_x, acc_i), _ = _k_in_bwd(dq, dk, dv, du, dvb, dga, dgb, x2, dxp, full["w_in"], modv, tm_in)
    g_ctx, dmodc = _k_ctx_bwd(ctx2, modc, hc, dkc, dvc, w_kv)
    gw_in, (rcv_a, rcv_b, rcv_o) = _wgrad(dP, h, "wgrad_in", tiles.tk_in, tt, comm=_Comm(scatter=[blk_a, blk_b, blk_o]),
                                          extra=(O_K, g_ctx))

    dmod_x = jnp.concatenate([acc_i[0:2], acc_b[4:5], acc_b[0:2], acc_f[2:3]], axis=0)
    small = jnp.concatenate([
        dmod_x, dmodc[0:2], acc_b[2:4], acc_f[0:2],
        jnp.concatenate([g_gln[0:1], g_gln[1:2]], axis=1), g_bst.T.reshape(1, D),
        _pad_rows(g_sink[:, 0:1], D).T, acc_f[3:4]], axis=0)
    chip_sums, gath = _exchange_two_level(to_blocks("w_in", gw_in), small, "exchange_last", ici=False)
    rcv_in = jnp.concatenate([chip_sums[0:1], _sc_chip_exchange(chip_sums)], axis=0)
    received = dict(w_in=rcv_in, w_branch_a=rcv_a, w_branch_b=rcv_b, w_out=rcv_o, w_ffn_in=rcv_fi, w_ffn_out=rcv_fo)
    moments = dict(w_in=(m_w_in, v_w_in), w_branch_a=(m_w_branch_a, v_w_branch_a), w_branch_b=(m_w_branch_b, v_w_branch_b),
                   w_out=(m_w_out, v_w_out), w_ffn_in=(m_w_ffn_in, v_w_ffn_in), w_ffn_out=(m_w_ffn_out, v_w_ffn_out))
    names = list(big)
    res = {}
    for kname in names:
        mm, vv = moments[kname]
        R = big[kname].shape[0]
        res[kname] = [tr(kname, r) for r in _adamw_reduce(
            received[kname], big[kname], tr(kname, mm[0]), tr(kname, vv[0]), "adamw_" + kname, 256 if R % 256 == 0 else R // 2)]

    ws2d = lambda a: a.reshape(N_GROUPS * BLK, BLK)
    res_ws = [r.reshape(w_spatial.shape) for r in _adamw_reduce(
        gath_ws, ws2d(w_spatial), ws2d(m_w_spatial), ws2d(v_w_spatial), "adamw_w_spatial", 256)]
    tot, res_small = _small_step(gath, dict(
        b_ada=(b_ada, m_b_ada, v_b_ada), ln1_g=(ln1_g, m_ln1_g, v_ln1_g), ln1_b=(ln1_b, m_ln1_b, v_ln1_b),
        ln2_g=(ln2_g, m_ln2_g, v_ln2_g), ln2_b=(ln2_b, m_ln2_b, v_ln2_b),
        gmlp_ln_g=(gmlp_ln_g, m_gmlp_ln_g, v_gmlp_ln_g), gmlp_ln_b=(gmlp_ln_b, m_gmlp_ln_b, v_gmlp_ln_b),
        b_spatial=(b_spatial, m_b_spatial, v_b_spatial), attn_sink=(attn_sink, m_attn_sink, v_attn_sink)))
    loss = tot[15, 0]

    dmod_rows = jnp.concatenate([gath[:, 0:6, :].reshape(N_DEV, 6 * D),
                                 jnp.concatenate([tot[6:8].reshape(1, 2 * D), jnp.zeros((1, 4 * D), F32)], axis=1),
                                 jnp.zeros((7, 6 * D), F32)], axis=0)
    dmod_my = lax.dynamic_slice(dmod_rows, (0, me * n_ada), (16, n_ada))
    g_wada, d_wada, m2_wada, v2_wada, pc = _ada_bwd(act, dmod_my, w_ada[0], m_w_ada[0], v_w_ada[0])
    pc_all = _gather_rows(pc, "gather_cctx")
    cc8 = lambda a: _pad_rows(a.reshape(1, D), 8)
    g_cc, d_cc, m2_cc, v2_cc = _cctx_finish(pc_all, cc8(c_ctx), cc8(m_c_ctx), cc8(v_c_ctx))

    order = ["c_ctx", "w_ada", "b_ada", "w_in", "attn_sink", "gmlp_ln_g", "gmlp_ln_b", "w_spatial", "b_spatial",
             "w_branch_a", "w_branch_b", "w_out", "ln1_g", "ln1_b", "w_ffn_in", "w_ffn_out", "ln2_g", "ln2_b"]
    grads, deltas, new_m, new_v = {}, {}, {}, {}
    grads["c_ctx"], deltas["c_ctx"], new_m["c_ctx"], new_v["c_ctx"] = g_cc[0], d_cc[0], m2_cc[0], v2_cc[0]
    grads["w_ada"], deltas["w_ada"], new_m["w_ada"], new_v["w_ada"] = g_wada[None], d_wada[None], m2_wada[None], v2_wada[None]
    for kname in names:
        g, d, m2, v2 = res[kname]
        grads[kname], deltas[kname], new_m[kname], new_v[kname] = g[None], d[None], m2[None], v2[None]
    grads["w_spatial"], deltas["w_spatial"], new_m["w_spatial"], new_v["w_spatial"] = res_ws
    for kname in SMALL_ORDER:
        grads[kname], deltas[kname], new_m[kname], new_v[kname] = res_small[kname]
    return (loss, grad_x[None], *[grads[n] for n in order], *[deltas[n] for n in order],
            *[new_m[n] for n in order], *[new_v[n] for n in order])
```

```python
import functools
import math

import jax
import jax.numpy as jnp
import numpy as np
from jax import lax
from jax.experimental import pallas as pl
from jax.experimental.pallas import tpu as pltpu
from jax.experimental.pallas import tpu_sc as plsc

F32 = jnp.float32
BF16 = jnp.bfloat16
MESH = pl.DeviceIdType.MESH

N_DEV = 8
D = 1024
HEAD_DIM = 64
N_Q_HEADS = 8
N_KV_HEADS = 2
GQA_GROUP = 4
BLK = 128
Q_W = 512
KV_W = 128
GM_W = 512
N_GROUPS = 8
GROUP_DIM = 64
FFN_H = 2816
IN_W = 3840
O_Q, O_K, O_V, O_U, O_VB, O_GA, O_GB = 0, 512, 640, 768, 1280, 1792, 2816
LN_EPS = 1e-5
NEG_INF = -1e30
ALPHA = 2.0 ** 0.25
ROPE_BASE = 10000.0
ROPE_PAIRS = 16
Q_SCALE = HEAD_DIM ** -0.5
GELU_K0 = math.sqrt(2.0 / math.pi)
GELU_K1 = 0.044715

ADAM_LR = 0.001
ADAM_B1 = 0.9
ADAM_B2 = 0.999
ADAM_EPS = 1e-08
ADAM_WD = 0.01
ADAM_STEP = 10

V7X_VMEM_BYTES = 64 * 1024 * 1024
VMEM_LIMIT = V7X_VMEM_BYTES * 7 // 8
NT = (((1,), (1,)), ((), ()))
TN = (((0,), (0,)), ((), ()))


class _Tiles:
    def __init__(self, L):
        self.wide = min(512, L)
        self.narrow = min(256, L)
        self.tokens = min(2048, L)
        self.tk_in = IN_W // 3
        self.tk_ffn = FFN_H // 2


def _params(sem=None):
    return pltpu.CompilerParams(dimension_semantics=sem, vmem_limit_bytes=VMEM_LIMIT)


def _row(tm, w):
    return pl.BlockSpec((tm, w), lambda i: (i, 0))


def _full(shape):
    nd = len(shape)
    return pl.BlockSpec(shape, lambda i: (0,) * nd)


def _resident(shape):
    nd = len(shape)
    return pl.BlockSpec(shape, lambda i: (0,) * nd, pipeline_mode=pl.Buffered(1))


def _sds(shape, dt):
    return jax.ShapeDtypeStruct(shape, dt)


def _ln(xf):
    mu = jnp.mean(xf, axis=-1, keepdims=True)
    xc = xf - mu
    var = jnp.mean(xc * xc, axis=-1, keepdims=True)
    rstd = lax.rsqrt(var + LN_EPS)
    return xc * rstd, rstd


def _ln_bwd(dn, n, rstd):
    m1 = jnp.mean(dn, axis=-1, keepdims=True)
    m2 = jnp.mean(dn * n, axis=-1, keepdims=True)
    return rstd * (dn - m1 - n * m2)


def _colsum(t):
    return jnp.sum(t, axis=0, keepdims=True)


def _sigmoid(x):
    return 0.5 * jnp.tanh(0.5 * x) + 0.5


def _gelu(x):
    t = jnp.tanh(x * (GELU_K0 + (GELU_K0 * GELU_K1) * (x * x)))
    h = 0.5 * x
    return h + h * t, t


def _gelu_grad(x, t):
    return 0.5 + 0.5 * t + (0.5 * x) * (1.0 - t * t) * (GELU_K0 + (3.0 * GELU_K0 * GELU_K1) * (x * x))


def _swap16(t):
    lane = lax.broadcasted_iota(jnp.int32, t.shape, 1)
    return jnp.where((lane & 16) == 0, pltpu.roll(t, 112, 1), pltpu.roll(t, 16, 1))


def _rope(t, cos, sin):
    return t * cos + _swap16(t) * sin


def _unrope(t, cos, sin):
    return t * cos - _swap16(t) * sin


def _adamw(w, g, m, v):
    m2 = ADAM_B1 * m + (1.0 - ADAM_B1) * g
    v2 = ADAM_B2 * v + (1.0 - ADAM_B2) * (g * g)
    m_hat = m2 / (1.0 - ADAM_B1 ** ADAM_STEP)
    v_hat = v2 / (1.0 - ADAM_B2 ** ADAM_STEP)
    delta = -ADAM_LR * (m_hat / (jnp.sqrt(v_hat) + ADAM_EPS) + ADAM_WD * w)
    return delta, m2, v2


def _rope_tables(L):
    inv = (np.float32(ROPE_BASE) ** (-np.arange(ROPE_PAIRS, dtype=np.float32) / np.float32(ROPE_PAIRS))).astype(np.float32)
    t = np.arange(L, dtype=np.int32)
    rows = (t // 64).astype(np.float32)[:, None] * inv
    cols = (t % 64).astype(np.float32)[:, None] * inv
    cr, sr, cc, sc = np.cos(rows), np.sin(rows), np.cos(cols), np.sin(cols)
    cos = np.concatenate([cr, cr, cc, cc], axis=1)
    sin = np.concatenate([-sr, sr, -sc, sc], axis=1)
    return jnp.asarray(np.tile(cos, (1, 2)), F32), jnp.asarray(np.tile(sin, (1, 2)), F32)


def _me():
    return lax.axis_index("x"), lax.axis_index("y"), lax.axis_index("c")


def _peer(mx, my, mc, k):
    return (mx ^ ((k >> 2) & 1), my ^ ((k >> 1) & 1), mc ^ (k & 1))


class _Comm:
    def __init__(self, gather=(), scatter=(), spread=()):
        self.kinds = ["gather"] * len(gather) + ["scatter"] * len(scatter) + ["spread"] * len(spread)
        self.args = list(gather) + list(scatter) + list(spread)
        self.n = len(self.args)

    def out_shape(self):
        return [_sds(a.shape if k == "scatter" else (N_DEV,) + a.shape, a.dtype) for k, a in zip(self.kinds, self.args)]

    def specs(self):
        return [pl.BlockSpec(memory_space=pl.ANY)] * self.n

    def scratch(self):
        return [pltpu.SemaphoreType.DMA((7 * self.n,)), pltpu.SemaphoreType.DMA((7 * self.n,)),
                pltpu.SemaphoreType.DMA((self.n,))]

    def _plan(self, x_refs, out_refs, send_sems, recv_sems, local_sems):
        mx, my, mc = _me()
        me = 4 * mx + 2 * my + mc
        here, sibling = (mx, my, mc), (mx, my, 1 - mc)
        chips = [(1 - mx, my), (mx, 1 - my), (1 - mx, 1 - my)]
        local, first, last = [], [], []
        relay = [[], [], []]
        for a, kind in enumerate(self.kinds):
            x, out = x_refs[a], out_refs[a]

            def rc(k, src, dst, to):
                return pltpu.make_async_remote_copy(
                    src_ref=src, dst_ref=dst, send_sem=send_sems.at[7 * a + k], recv_sem=recv_sems.at[7 * a + k],
                    device_id=to, device_id_type=MESH)

            if kind == "gather":
                local.append(pltpu.make_async_copy(x, out.at[me], local_sems.at[a]))
                first.append(rc(0, x, out.at[me], sibling))
                last.append(rc(0, x, out.at[me ^ 1], here))
                for j, (cx, cy) in enumerate(chips):
                    first.append(rc(1 + j, x, out.at[me], (cx, cy, mc)))
                    landed = out.at[4 * cx + 2 * cy + mc]
                    relay[j].append((rc(1 + j, x, landed, here), rc(4 + j, landed, landed, sibling)))
                    last.append(rc(4 + j, x, out.at[4 * cx + 2 * cy + 1 - mc], here))
            else:
                own = x.at[me] if kind == "scatter" else x
                local.append(pltpu.make_async_copy(own, out.at[me], local_sems.at[a]))
                for k in range(1, N_DEV):
                    src = x.at[me ^ k] if kind == "scatter" else x
                    first.append(rc(k - 1, src, out.at[me], _peer(mx, my, mc, k)))
                    last.append(rc(k - 1, own, out.at[me ^ k], here))
        return local, first, relay[0] + relay[1] + relay[2], last

    def start(self, *refs):
        local, first, _, _ = self._plan(*refs)
        for cp in local + first:
            cp.start()

    def relay(self, *refs):
        _, _, relay, _ = self._plan(*refs)
        for arrival, onward in relay:
            arrival.wait_recv()
            onward.start()

    def finish(self, *refs):
        local, first, relay, last = self._plan(*refs)
        for cp in last:
            cp.wait_recv()
        for cp in first:
            cp.wait_send()
        for _, onward in relay:
            onward.wait_send()
        for cp in local:
            cp.wait()


def _call(body, *, name, grid, in_specs, out_specs, out_shape, args, scratch=(), comm=None, aliases=None):
    params = _params(("arbitrary",) * len(grid))
    total = math.prod(grid)

    def at(step):
        flat = functools.reduce(lambda acc, dn: acc * dn[1] + pl.program_id(dn[0]), enumerate(grid), 0)
        return flat == step

    if comm is None:
        res = pl.pallas_call(
            body, name=name, grid=grid, in_specs=list(in_specs), out_specs=list(out_specs), out_shape=list(out_shape),
            scratch_shapes=list(scratch), input_output_aliases=aliases or {}, compiler_params=params)(*args)
        return list(res), []
    n_in, n_out, n_scr, cn = len(in_specs), len(out_specs), len(scratch), comm.n

    def hosted(*refs):
        ins, refs = refs[:n_in], refs[n_in:]
        cins, refs = refs[:cn], refs[cn:]
        outs, refs = refs[:n_out], refs[n_out:]
        couts, refs = refs[:cn], refs[cn:]
        scr, sems = refs[:n_scr], refs[n_scr:]

        @pl.when(at(0))
        def _():
            comm.start(cins, couts, *sems)

        body(*ins, *outs, *scr)

        @pl.when(at((3 * total) // 4 if total >= 4 else total - 1))
        def _():
            comm.relay(cins, couts, *sems)

        @pl.when(at(total - 1))
        def _():
            comm.finish(cins, couts, *sems)

    res = pl.pallas_call(
        hosted, name=name, grid=grid, in_specs=list(in_specs) + comm.specs(), out_specs=list(out_specs) + comm.specs(),
        out_shape=list(out_shape) + comm.out_shape(), scratch_shapes=list(scratch) + comm.scratch(),
        input_output_aliases=aliases or {}, compiler_params=params)(*args, *comm.args)
    return list(res[:n_out]), list(res[n_out:])


def _sc_chip_exchange(t):
    hbm = pltpu.MemorySpace.HBM
    t_ref = jax.new_ref(t, memory_space=hbm)
    got_ref = jax.empty_ref(jax.ShapeDtypeStruct((3,) + t.shape[1:], t.dtype), memory_space=hbm)
    dma = pltpu.SemaphoreType.DMA

    @pl.kernel(mesh=plsc.ScalarSubcoreMesh(axis_name="seq", num_cores=1), name="sc_chip_exchange",
               scratch_types=(dma, dma, dma, dma, dma, dma), compiler_params=pltpu.CompilerParams(collective_id=3))
    def launch(s1, s2, s3, r1, r2, r3):
        mx, my, mc = _me()
        peers = [(mx ^ (k >> 1), my ^ (k & 1), mc) for k in range(1, 4)]
        barrier = pltpu.get_barrier_semaphore()
        for peer in peers:
            pl.semaphore_signal(barrier, inc=1, device_id=peer, device_id_type=MESH)
        pl.semaphore_wait(barrier, 3)
        copies = [pltpu.make_async_remote_copy(src_ref=t_ref.at[k + 1], dst_ref=got_ref.at[k], send_sem=s, recv_sem=r,
                                               device_id=peer, device_id_type=MESH)
                  for k, (peer, s, r) in enumerate(zip(peers, (s1, s2, s3), (r1, r2, r3)))]
        for cp in copies:
            cp.start()
        for cp in copies:
            cp.wait()

    launch()
    return got_ref[...]


def _exchange_two_level(blk, small, name, ici=True):
    _, R, C = blk.shape
    rows = small.shape[0]

    def body(blk_ref, small_ref, stage_ref, out_ref, gath_ref, a_scr, b_scr, t_scr, s1, r1, s3, r3, ss, rs, lsem):
        mx, my, mc = _me()
        me = 4 * mx + 2 * my + mc
        mine = 2 * mx + my
        here, sibling = (mx, my, mc), (mx, my, 1 - mc)

        def rc(src, dst, send, recv, to):
            return pltpu.make_async_remote_copy(src_ref=src, dst_ref=dst, send_sem=send, recv_sem=recv,
                                                device_id=to, device_id_type=MESH)

        own_small = pltpu.make_async_copy(small_ref, gath_ref.at[me], lsem.at[0])
        own_small.start()
        spread = [rc(small_ref, gath_ref.at[me], ss.at[k - 1], rs.at[k - 1], _peer(mx, my, mc, k)) for k in range(1, N_DEV)]
        order = (1, 2, 3, 0)
        to_sib = [rc(blk_ref.at[2 * (mine ^ k) + 1 - mc], stage_ref.at[k], s1.at[k], r1.at[k], sibling) for k in order]
        for cp in spread + to_sib:
            cp.start()
        own = {k: pltpu.make_async_copy(blk_ref.at[2 * (mine ^ k) + mc], a_scr.at[k], lsem.at[1 + k]) for k in order}
        for k in order:
            own[k].start()
        onward = []
        for k in order:
            rc(blk_ref.at[0], stage_ref.at[k], s1.at[k], r1.at[k], here).wait_recv()
            landed = pltpu.make_async_copy(stage_ref.at[k], b_scr.at[k], lsem.at[5 + k])
            landed.start()
            landed.wait()
            own[k].wait()
            t_scr[k] = (a_scr[k].astype(F32) + b_scr[k].astype(F32)).astype(BF16)
            if k > 0 and ici:
                cp = rc(t_scr.at[k], out_ref.at[mine], s3.at[k - 1], r3.at[k - 1], (mx ^ (k >> 1), my ^ (k & 1), mc))
                cp.start()
                onward.append(cp)
        if ici:
            keep = pltpu.make_async_copy(t_scr.at[0], out_ref.at[mine], lsem.at[9])
        else:
            keep = pltpu.make_async_copy(t_scr, out_ref, lsem.at[9])
        keep.start()
        for k in range(1, 4 if ici else 1):
            rc(t_scr.at[0], out_ref.at[mine ^ k], s3.at[k - 1], r3.at[k - 1], here).wait_recv()
        for k in range(1, N_DEV):
            rc(small_ref, gath_ref.at[me ^ k], ss.at[k - 1], rs.at[k - 1], here).wait_recv()
        for cp in spread + to_sib + onward:
            cp.wait_send()
        keep.wait()
        own_small.wait()

    any_spec = pl.BlockSpec(memory_space=pl.ANY)
    dma = pltpu.SemaphoreType.DMA
    _, out, gath = pl.pallas_call(
        body, name=name,
        in_specs=[any_spec, any_spec], out_specs=[any_spec] * 3,
        out_shape=[_sds((4, R, C), BF16), _sds((4, R, C), BF16), _sds((N_DEV, rows, D), F32)],
        scratch_shapes=[pltpu.VMEM((4, R, C), BF16)] * 3
                       + [dma((4,)), dma((4,)), dma((3,)), dma((3,)), dma((N_DEV - 1,)), dma((N_DEV - 1,)), dma((10,))],
        compiler_params=pltpu.CompilerParams(vmem_limit_bytes=VMEM_LIMIT),
    )(blk, small)
    return out, gath


def _exchange_rows(x_ref, out_ref, send_sems, recv_sems, between=None):
    mx, my, mc = _me()
    me = 4 * mx + 2 * my + mc
    out_ref[pl.ds(me, 1)] = x_ref[...][None]
    sends = []
    for k in range(1, N_DEV):
        cp = pltpu.make_async_remote_copy(
            src_ref=x_ref, dst_ref=out_ref.at[me], send_sem=send_sems.at[k - 1], recv_sem=recv_sems.at[k - 1],
            device_id=_peer(mx, my, mc, k), device_id_type=MESH)
        cp.start()
        sends.append(cp)
    if between is not None:
        between()
    for k in range(1, N_DEV):
        pltpu.make_async_remote_copy(
            src_ref=x_ref, dst_ref=out_ref.at[me ^ k], send_sem=send_sems.at[k - 1], recv_sem=recv_sems.at[k - 1],
            device_id=(mx, my, mc), device_id_type=MESH).wait_recv()
    for cp in sends:
        cp.wait_send()


def _sc_gather(x, name):
    hbm = pltpu.MemorySpace.HBM
    x_ref = jax.new_ref(x, memory_space=hbm)
    got_ref = jax.empty_ref(jax.ShapeDtypeStruct((N_DEV,) + x.shape, x.dtype), memory_space=hbm)
    dma = pltpu.SemaphoreType.DMA

    @pl.kernel(mesh=plsc.ScalarSubcoreMesh(axis_name="seq", num_cores=1), name=name,
               scratch_types=(dma,) * 15, compiler_params=pltpu.CompilerParams(collective_id=4))
    def launch(*sems):
        send, recv, own_sem = sems[:7], sems[7:14], sems[14]
        mx, my, mc = _me()
        me = 4 * mx + 2 * my + mc
        here, sibling = (mx, my, mc), (mx, my, 1 - mc)
        chips = [(1 - mx, my), (mx, 1 - my), (1 - mx, 1 - my)]
        barrier = pltpu.get_barrier_semaphore()
        for peer in [sibling] + [(cx, cy, mc) for cx, cy in chips]:
            pl.semaphore_signal(barrier, inc=1, device_id=peer, device_id_type=MESH)
        pl.semaphore_wait(barrier, 4)

        def rc(k, src, dst, to):
            return pltpu.make_async_remote_copy(src_ref=src, dst_ref=dst, send_sem=send[k], recv_sem=recv[k],
                                                device_id=to, device_id_type=MESH)

        own = pltpu.make_async_copy(x_ref, got_ref.at[me], own_sem)
        own.start()
        first = [rc(0, x_ref, got_ref.at[me], sibling)] + [rc(1 + j, x_ref, got_ref.at[me], (cx, cy, mc))
                                                             for j, (cx, cy) in enumerate(chips)]
        for cp in first:
            cp.start()
        onward = []
        for j, (cx, cy) in enumerate(chips):
            landed = got_ref.at[4 * cx + 2 * cy + mc]
            rc(1 + j, x_ref, landed, here).wait_recv()
            cp = rc(4 + j, landed, landed, sibling)
            cp.start()
            onward.append(cp)
        rc(0, x_ref, got_ref.at[me ^ 1], here).wait_recv()
        for j, (cx, cy) in enumerate(chips):
            rc(4 + j, x_ref, got_ref.at[4 * cx + 2 * cy + 1 - mc], here).wait_recv()
        for cp in first + onward:
            cp.wait_send()
        own.wait()

    launch()
    return got_ref[...]


def _prologue(c8, cctx8, w_ada, b_my):
    nw = w_ada.shape[1]

    def body(c_ref, cctx_ref, w_ref, b_ref, act_ref, mod_ref, cmine_scr, call_scr, mine_scr, mall_scr, s1, r1, s2, r2):
        cmine_scr[...] = c_ref[...]
        _exchange_rows(cmine_scr, call_scr, s1, r1)
        rows = [call_scr[d][0:1, :] for d in range(N_DEV)] + [cctx_ref[0:1, :], jnp.zeros((7, D), F32)]
        s = jnp.concatenate(rows, axis=0)
        act = s * _sigmoid(s)
        act_ref[...] = act
        mine_scr[...] = jnp.dot(act.astype(BF16), w_ref[...].astype(BF16), preferred_element_type=F32) + b_ref[...]
        _exchange_rows(mine_scr, mall_scr, s2, r2)
        mod_ref[...] = mall_scr[...]

    sems = [pltpu.SemaphoreType.DMA((N_DEV - 1,))] * 4
    return pl.pallas_call(
        body, name="prologue", grid=(1,),
        in_specs=[_full((8, D)), _full((8, D)), _full((D, nw)), _full((1, nw))],
        out_specs=[_full((16, D)), _full((N_DEV, 16, nw))],
        out_shape=[_sds((16, D), F32), _sds((N_DEV, 16, nw), F32)],
        scratch_shapes=[pltpu.VMEM((8, D), F32), pltpu.VMEM((N_DEV, 8, D), F32), pltpu.VMEM((16, nw), F32),
                        pltpu.VMEM((N_DEV, 16, nw), F32)] + sems,
        compiler_params=_params(("arbitrary",)),
    )(c8, cctx8, w_ada, b_my)


def _gather_rows(x, name):
    def body(x_ref, out_ref, send_sems, recv_sems):
        _exchange_rows(x_ref, out_ref, send_sems, recv_sems)

    return pl.pallas_call(
        body, name=name,
        out_shape=_sds((N_DEV,) + x.shape, x.dtype),
        in_specs=[pl.BlockSpec(memory_space=pltpu.VMEM)],
        out_specs=pl.BlockSpec(memory_space=pltpu.VMEM),
        scratch_shapes=[pltpu.SemaphoreType.DMA((N_DEV - 1,)), pltpu.SemaphoreType.DMA((N_DEV - 1,))],
        compiler_params=pltpu.CompilerParams(vmem_limit_bytes=VMEM_LIMIT),
    )(x)


def _ada_bwd(act, dmod_my, w_ada, m, v, tr=256):
    nw = w_ada.shape[1]

    def body(act_ref, dm_ref, w_ref, m_ref, v_ref, g_ref, d_ref, m2_ref, v2_ref, pc_ref):
        dm = dm_ref[...].astype(BF16)
        g = lax.dot_general(act_ref[...].astype(BF16), dm, TN, preferred_element_type=F32)
        w = w_ref[...]
        delta, m2, v2 = _adamw(w, g, m_ref[...], v_ref[...])
        g_ref[...] = g
        d_ref[...] = delta
        m2_ref[...] = m2
        v2_ref[...] = v2
        pc_ref[...] = lax.dot_general(dm[8:16, :], w.astype(BF16), NT, preferred_element_type=F32)

    wspec = _row(tr, nw)
    return pl.pallas_call(
        body, name="ada_bwd", grid=(D // tr,),
        in_specs=[pl.BlockSpec((16, tr), lambda i: (0, i)), _full((16, nw)), wspec, wspec, wspec],
        out_specs=[wspec, wspec, wspec, wspec, pl.BlockSpec((8, tr), lambda i: (0, i))],
        out_shape=[_sds((D, nw), F32)] * 4 + [_sds((8, D), F32)],
        compiler_params=_params(("arbitrary",)),
    )(act, dmod_my, w_ada, m, v)


def _k_in(x, modv, w_in, cos, sin, tm, comm=None):
    L = x.shape[0]

    def body(x_ref, mod_ref, w_ref, cos_ref, sin_ref, h_ref, q_ref, k_ref, v_ref, u_ref, vb_ref, ga_ref, gb_ref):
        n, _ = _ln(x_ref[...])
        h = (n * (1.0 + mod_ref[1:2, :]) + mod_ref[0:1, :]).astype(BF16)
        h_ref[...] = h
        c, s = cos_ref[...], sin_ref[...]

        def proj(lo, width):
            return lax.dot_general(h, w_ref[lo:lo + width, :], NT, preferred_element_type=F32)

        for i in range(2):
            qh = proj(O_Q + i * 256, 256)
            for j in range(2):
                q_ref[:, i * 256 + j * 128:i * 256 + (j + 1) * 128] = (
                    _rope(qh[:, j * 128:(j + 1) * 128], c, s) * Q_SCALE).astype(BF16)
        kv = proj(O_K, 2 * KV_W)
        k_ref[...] = _rope(kv[:, :KV_W], c, s).astype(BF16)
        v_ref[...] = kv[:, KV_W:].astype(BF16)
        u_ref[...] = proj(O_U, GM_W).astype(BF16)
        vb_ref[...] = proj(O_VB, GM_W).astype(BF16)
        ga_ref[...] = proj(O_GA, D).astype(BF16)
        gb_ref[...] = proj(O_GB, D).astype(BF16)

    widths = [D, Q_W, KV_W, KV_W, GM_W, GM_W, D, D]
    return _call(
        body, name="fwd_in", grid=(L // tm,),
        in_specs=[_row(tm, D), _full((8, D)), _resident((IN_W, D)), _row(tm, 128), _row(tm, 128)],
        out_specs=[_row(tm, w) for w in widths],
        out_shape=[_sds((L, w), BF16) for w in widths],
        args=(x, modv, w_in, cos, sin), comm=comm)


def _k_ctx(ctx, modc, w_kv):
    C = ctx.shape[0]

    def body(c_ref, mod_ref, w_ref, hc_ref, kc_ref, vc_ref):
        n, _ = _ln(c_ref[...])
        hc = (n * (1.0 + mod_ref[1:2, :]) + mod_ref[0:1, :]).astype(BF16)
        hc_ref[...] = hc
        kv = lax.dot_general(hc, w_ref[...], NT, preferred_element_type=F32)
        kc_ref[...] = kv[:, :KV_W].astype(BF16)
        vc_ref[...] = kv[:, KV_W:].astype(BF16)

    return pl.pallas_call(
        body, name="fwd_ctx", grid=(1,),
        in_specs=[_full((C, D)), _full((8, D)), _full((2 * KV_W, D))],
        out_specs=[_full((C, D)), _full((C, KV_W)), _full((C, KV_W))],
        out_shape=[_sds((C, D), BF16), _sds((C, KV_W), BF16), _sds((C, KV_W), BF16)],
        compiler_params=_params(("arbitrary",)),
    )(ctx, modc, w_kv)


def _attn_bias():
    r = (np.arange(GQA_GROUP * BLK) & (BLK - 1))[:, None]
    j = np.arange(3 * BLK)[None, :]
    band = np.abs(j - BLK - r) <= BLK
    variants = [band & (j >= BLK), band, band & (j < 2 * BLK)]
    return jnp.asarray(np.stack([np.where(v, 0.0, NEG_INF) for v in variants]), F32)


def _masked(s, bias, C):
    return jnp.concatenate([s[:, :C], s[:, C:] + bias], axis=1)


def _sink_col(sink_ref, hk):
    grp = lax.broadcasted_iota(jnp.int32, (GQA_GROUP * BLK, 1), 0) >> 7
    col = jnp.full((GQA_GROUP * BLK, 1), sink_ref[hk * GQA_GROUP], F32)
    for g in range(1, GQA_GROUP):
        col = jnp.where(grp == g, sink_ref[hk * GQA_GROUP + g], col)
    return col


ATTN_FWD_BLOCKS = 4


def _k_attn(sink, q, k, v, kc, vc, bias, comm=None):
    L = q.shape[0]
    C = kc.shape[0]
    nb = L // BLK
    nq = min(ATTN_FWD_BLOCKS, nb)
    steps = nb // nq

    def body(sink_ref, q_ref, kp_ref, km_ref, kx_ref, vp_ref, vm_ref, vx_ref, kc_ref, vc_ref, bias_ref, ya_ref, lse_ref):
        i = pl.program_id(0)
        chains = [(qb, hk) for qb in range(nq) for hk in range(N_KV_HEADS)]

        def band(qb):
            first = jnp.where(i == 0, 0, 1) if qb == 0 else 1
            return bias_ref[jnp.where(i == steps - 1, 2, first) if qb == nq - 1 else first]

        def keys(ctx_ref, p_ref, m_ref, x_ref, qb, hk):
            sl = slice(hk * HEAD_DIM, (hk + 1) * HEAD_DIM)
            blocks = [p_ref[:, sl]] + [m_ref[j * BLK:(j + 1) * BLK, sl] for j in range(nq)] + [x_ref[:, sl]]
            return jnp.concatenate([ctx_ref[:, sl]] + blocks[qb:qb + 3], axis=0)

        def queries(qb, hk):
            return jnp.concatenate(
                [q_ref[qb * BLK:(qb + 1) * BLK, (hk * GQA_GROUP + g) * HEAD_DIM:(hk * GQA_GROUP + g + 1) * HEAD_DIM]
                 for g in range(GQA_GROUP)], axis=0)

        def scores(qb, hk):
            return _masked(lax.dot_general(queries(qb, hk), keys(kc_ref, kp_ref, km_ref, kx_ref, qb, hk), NT,
                                           preferred_element_type=F32), band(qb), C)

        ahead = 2
        s = [scores(*c) for c in chains[:ahead]]
        for n, (qb, hk) in enumerate(chains):
            if n + ahead < len(chains):
                s.append(scores(*chains[n + ahead]))
            s_ = s[n]
            sink_c = _sink_col(sink_ref, hk)
            m = jnp.maximum(jnp.max(s_, axis=1, keepdims=True), sink_c)
            p = jnp.exp(s_ - m)
            den = jnp.sum(p, axis=1, keepdims=True) + jnp.exp(sink_c - m)
            o = jnp.dot(p.astype(BF16), keys(vc_ref, vp_ref, vm_ref, vx_ref, qb, hk), preferred_element_type=F32) * (1.0 / den)
            lse = m + jnp.log(den)
            rows = slice(qb * BLK, (qb + 1) * BLK)
            for g in range(GQA_GROUP):
                h = hk * GQA_GROUP + g
                ya_ref[rows, h * HEAD_DIM:(h + 1) * HEAD_DIM] = o[g * BLK:(g + 1) * BLK, :].astype(BF16)
                lse_ref[rows, h:h + 1] = lse[g * BLK:(g + 1) * BLK, :]

    kv3 = [pl.BlockSpec((BLK, KV_W), lambda i: (jnp.maximum(nq * i - 1, 0), 0)),
           pl.BlockSpec((nq * BLK, KV_W), lambda i: (i, 0)),
           pl.BlockSpec((BLK, KV_W), lambda i: (jnp.minimum(nq * i + nq, nb - 1), 0))]
    return _call(
        body, name="fwd_attn", grid=(steps,),
        in_specs=[pl.BlockSpec(memory_space=pltpu.SMEM), _row(nq * BLK, Q_W)] + kv3 + kv3
                 + [_full((C, KV_W)), _full((C, KV_W)), _full((3, GQA_GROUP * BLK, 3 * BLK))],
        out_specs=[_row(nq * BLK, Q_W), _row(nq * BLK, N_Q_HEADS)],
        out_shape=[_sds((L, Q_W), BF16), _sds((L, N_Q_HEADS), F32)],
        args=(sink, q, k, k, k, v, v, v, kc, vc, bias), comm=comm)


GMLP_CHUNKS = 4


def _split_pair(t):
    low = lax.broadcasted_iota(jnp.int32, t.shape, 1) < GROUP_DIM
    zero = jnp.zeros_like(t)
    return jnp.where(low, t, zero), jnp.where(low, zero, t)


def _gmlp_spatial(w_ref, t_b, nch):
    rows = []
    for c in range(nch):
        tiles = []
        for pr in range(N_GROUPS // 2):
            lo, hi = _split_pair(t_b[c * BLK:(c + 1) * BLK, pr * 128:(pr + 1) * 128])
            tiles.append(jnp.dot(w_ref[2 * pr], lo, preferred_element_type=F32)
                         + jnp.dot(w_ref[2 * pr + 1], hi, preferred_element_type=F32))
        rows.append(jnp.concatenate(tiles, axis=1))
    return jnp.concatenate(rows, axis=0)


def _gmlp_fwd_vals(u, vb, lnv_ref, ws_ref, bsp_ref, nch):
    uf = u.astype(F32)
    vf = vb.astype(F32)
    gu, tu = _gelu(uf)
    gv, tv = _gelu(vf)
    vhat, rstd = _ln(gv)
    vn = (vhat * lnv_ref[0:1, :] + lnv_ref[1:2, :]).astype(BF16)
    s = _gmlp_spatial(ws_ref, vn, nch) + jnp.concatenate([bsp_ref[...]] * nch, axis=0)
    return uf, vf, gu, tu, tv, vhat, rstd, vn, s


def _k_gmlp(u, vb, lnv, ws, bsp):
    L = u.shape[0]
    nch = min(GMLP_CHUNKS, L // BLK)
    tm = nch * BLK

    def body(u_ref, vb_ref, lnv_ref, ws_ref, bsp_ref, yb_ref):
        _, _, gu, _, _, _, _, _, s = _gmlp_fwd_vals(u_ref[...], vb_ref[...], lnv_ref, ws_ref, bsp_ref, nch)
        yb_ref[...] = (gu * s).astype(BF16)

    return pl.pallas_call(
        body, name="fwd_gmlp", grid=(L // tm,),
        in_specs=[_row(tm, GM_W), _row(tm, GM_W), _full((8, GM_W)), _full((N_GROUPS, BLK, BLK)), _full((BLK, GM_W))],
        out_specs=_row(tm, GM_W),
        out_shape=_sds((L, GM_W), BF16),
        compiler_params=_params(("arbitrary",)),
    )(u, vb, lnv, ws, bsp)


def _k_merge(x, ya, yb, ga, gb, w_a, w_b, w_o, modv, lnv, tm):
    L = x.shape[0]

    def body(x_ref, ya_ref, yb_ref, ga_ref, gb_ref, wa_ref, wb_ref, wo_ref, mod_ref, ln_ref,
             mg_ref, mix_ref, xm_ref, h2_ref):
        a = jnp.dot(ya_ref[...], wa_ref[...], preferred_element_type=F32)
        b = jnp.dot(yb_ref[...], wb_ref[...], preferred_element_type=F32)
        merged = (_sigmoid(ga_ref[...].astype(F32)) * a + _sigmoid(gb_ref[...].astype(F32)) * b).astype(BF16)
        mg_ref[...] = merged
        mix = jnp.dot(merged, wo_ref[...], preferred_element_type=F32)
        mix_ref[...] = mix.astype(BF16)
        r1 = ALPHA * x_ref[...] + mod_ref[2:3, :] * mix
        r1hat, _ = _ln(r1)
        xm = r1hat * ln_ref[0:1, :] + ln_ref[1:2, :]
        xm_ref[...] = xm
        n2, _ = _ln(xm)
        h2_ref[...] = (n2 * (1.0 + mod_ref[4:5, :]) + mod_ref[3:4, :]).astype(BF16)

    return pl.pallas_call(
        body, name="fwd_merge", grid=(L // tm,),
        in_specs=[_row(tm, D), _row(tm, Q_W), _row(tm, GM_W), _row(tm, D), _row(tm, D),
                  _resident((Q_W, D)), _resident((GM_W, D)), _resident((D, D)), _full((8, D)), _full((8, D))],
        out_specs=[_row(tm, D)] * 4,
        out_shape=[_sds((L, D), BF16), _sds((L, D), BF16), _sds((L, D), F32), _sds((L, D), BF16)],
        compiler_params=_params(("arbitrary",)),
    )(x, ya, yb, ga, gb, w_a, w_b, w_o, modv, lnv)


V7X_MXU_COLUMNS = 256
FFN_CHUNK = V7X_MXU_COLUMNS


def _k_ffn(h2, xm, tgt, w_fi, w_fo, modv, lnv, tm):
    L = h2.shape[0]

    def body(h2_ref, xm_ref, t_ref, wi_ref, wo_ref, mod_ref, ln_ref, gate_ref, up_ref, a_ref, dr2_ref, df_ref, acc_ref):
        @pl.when(pl.program_id(0) == 0)
        def _():
            acc_ref[...] = jnp.zeros_like(acc_ref)

        h2v = h2_ref[...]
        ch = FFN_CHUNK
        chunks = [j * ch for j in range(FFN_H // ch)]

        def project(lo):
            return (lax.dot_general(h2v, wi_ref[lo:lo + ch, :], NT, preferred_element_type=F32),
                    lax.dot_general(h2v, wi_ref[FFN_H + lo:FFN_H + lo + ch, :], NT, preferred_element_type=F32))

        f = jnp.zeros((tm, D), F32)
        ahead = [project(chunks[0])]
        for j, lo in enumerate(chunks):
            if j + 1 < len(chunks):
                ahead.append(project(chunks[j + 1]))
            gate, up = ahead[j]
            act = (gate * _sigmoid(gate) * up).astype(BF16)
            gate_ref[:, lo:lo + ch] = gate.astype(BF16)
            up_ref[:, lo:lo + ch] = up.astype(BF16)
            a_ref[:, lo:lo + ch] = act
            f = f + jnp.dot(act, wo_ref[lo:lo + ch, :], preferred_element_type=F32)
        gate2 = mod_ref[5:6, :]
        r2 = ALPHA * xm_ref[...] + gate2 * f
        r2hat, rstd = _ln(r2)
        y = r2hat * ln_ref[2:3, :] + ln_ref[3:4, :]
        err = y - t_ref[...]
        dy = err * (1.0 / D)
        dr2 = _ln_bwd(dy * ln_ref[2:3, :], r2hat, rstd)
        dr2_ref[...] = dr2
        df_ref[...] = (gate2 * dr2).astype(BF16)
        acc_ref[0:1, :] += _colsum(dy * r2hat)
        acc_ref[1:2, :] += _colsum(dy)
        acc_ref[2:3, :] += _colsum(dr2 * f)
        acc_ref[3:4, :] += _colsum(err * err) * (0.5 / D)

    return pl.pallas_call(
        body, name="fwd_ffn", grid=(L // tm,),
        in_specs=[_row(tm, D), _row(tm, D), _row(tm, D), _resident((2 * FFN_H, D)), _resident((FFN_H, D)),
                  _full((8, D)), _full((8, D))],
        out_specs=[_row(tm, FFN_H)] * 3 + [_row(tm, D), _row(tm, D), _full((8, D))],
        out_shape=[_sds((L, FFN_H), BF16)] * 3 + [_sds((L, D), F32), _sds((L, D), BF16), _sds((8, D), F32)],
        compiler_params=_params(("arbitrary",)),
    )(h2, xm, tgt, w_fi, w_fo, modv, lnv)


def _k_ffn_bwd(df, gate, up, xm, dr2, x, mix, w_fi, w_fo, modv, lnv, tm):
    L = df.shape[0]

    def body(df_ref, gate_ref, up_ref, xm_ref, dr2_ref, x_ref, mix_ref, wi_ref, wo_ref, mod_ref, ln_ref,
             dF_ref, dmix_ref, dxp_ref, acc_ref):
        @pl.when(pl.program_id(0) == 0)
        def _():
            acc_ref[...] = jnp.zeros_like(acc_ref)

        dfv = df_ref[...]
        ch = FFN_CHUNK
        chunks = [j * ch for j in range(FFN_H // ch)]

        def d_act(lo):
            return lax.dot_general(dfv, wo_ref[lo:lo + ch, :], NT, preferred_element_type=F32)

        n2, rstd2 = _ln(xm_ref[...])
        mixf = mix_ref[...].astype(F32)
        gate1 = mod_ref[2:3, :]
        r1hat, rstd1 = _ln(ALPHA * x_ref[...] + gate1 * mixf)
        dh2 = jnp.zeros((tm, D), F32)
        das = [d_act(chunks[0])]
        for j, lo in enumerate(chunks):
            if j + 1 < len(chunks):
                das.append(d_act(chunks[j + 1]))
            da = das[j]
            gate = gate_ref[:, lo:lo + ch].astype(F32)
            upv = up_ref[:, lo:lo + ch].astype(F32)
            sg = _sigmoid(gate)
            d_gate = (da * upv * (sg * (1.0 + gate * (1.0 - sg)))).astype(BF16)
            d_up = (da * (gate * sg)).astype(BF16)
            dF_ref[:, lo:lo + ch] = d_gate
            dF_ref[:, FFN_H + lo:FFN_H + lo + ch] = d_up
            dh2 = dh2 + jnp.dot(d_gate, wi_ref[lo:lo + ch, :], preferred_element_type=F32)
            dh2 = dh2 + jnp.dot(d_up, wi_ref[FFN_H + lo:FFN_H + lo + ch, :], preferred_element_type=F32)
        acc_ref[0:1, :] += _colsum(dh2)
        acc_ref[1:2, :] += _colsum(dh2 * n2)
        dxm = ALPHA * dr2_ref[...] + _ln_bwd(dh2 * (1.0 + mod_ref[4:5, :]), n2, rstd2)
        acc_ref[2:3, :] += _colsum(dxm * r1hat)
        acc_ref[3:4, :] += _colsum(dxm)
        dr1 = _ln_bwd(dxm * ln_ref[0:1, :], r1hat, rstd1)
        dmix_ref[...] = (gate1 * dr1).astype(BF16)
        dxp_ref[...] = ALPHA * dr1
        acc_ref[4:5, :] += _colsum(dr1 * mixf)

    return pl.pallas_call(
        body, name="bwd_ffn", grid=(L // tm,),
        in_specs=[_row(tm, D), _row(tm, FFN_H), _row(tm, FFN_H), _row(tm, D), _row(tm, D), _row(tm, D), _row(tm, D),
                  _resident((2 * FFN_H, D)), _resident((FFN_H, D)), _full((8, D)), _full((8, D))],
        out_specs=[_row(tm, 2 * FFN_H), _row(tm, D), _row(tm, D), _full((8, D))],
        out_shape=[_sds((L, 2 * FFN_H), BF16), _sds((L, D), BF16), _sds((L, D), F32), _sds((8, D), F32)],
        compiler_params=_params(("arbitrary",)),
    )(df, gate, up, xm, dr2, x, mix, w_fi, w_fo, modv, lnv)


def _k_merge_bwd(dmix, merged, ya, yb, ga, gb, w_a, w_b, w_o, tm):
    L = dmix.shape[0]
    n = L // tm

    def body(dmix_ref, mg_ref, ya_ref, yb_ref, ga_ref, gb_ref, wa_ref, wb_ref, wo_ref,
             dga_ref, dgb_ref, dya_ref, dyb_ref, gwa_ref, gwb_ref, gwo_ref, acc_a, acc_b, acc_o):
        i = pl.program_id(0)

        @pl.when(i == 0)
        def _():
            for r in (acc_a, acc_b, acc_o):
                r[...] = jnp.zeros_like(r)

        dmixv = dmix_ref[...]
        dmg = lax.dot_general(dmixv, wo_ref[...], NT, preferred_element_type=F32)
        acc_o[...] += lax.dot_general(mg_ref[...], dmixv, TN, preferred_element_type=F32)
        ya = ya_ref[...]
        a = jnp.dot(ya, wa_ref[...], preferred_element_type=F32)
        sa = _sigmoid(ga_ref[...].astype(F32))
        dA = (dmg * sa).astype(BF16)
        dga_ref[...] = (dmg * a * (sa * (1.0 - sa))).astype(BF16)
        dya_ref[...] = lax.dot_general(dA, wa_ref[...], NT, preferred_element_type=F32).astype(BF16)
        acc_a[...] += lax.dot_general(ya, dA, TN, preferred_element_type=F32)
        yb = yb_ref[...]
        b = jnp.dot(yb, wb_ref[...], preferred_element_type=F32)
        sb = _sigmoid(gb_ref[...].astype(F32))
        dB = (dmg * sb).astype(BF16)
        dgb_ref[...] = (dmg * b * (sb * (1.0 - sb))).astype(BF16)
        dyb_ref[...] = lax.dot_general(dB, wb_ref[...], NT, preferred_element_type=F32).astype(BF16)
        acc_b[...] += lax.dot_general(yb, dB, TN, preferred_element_type=F32)

        @pl.when(i == n - 1)
        def _():
            gwa_ref[...] = acc_a[...].astype(BF16)
            gwb_ref[...] = acc_b[...].astype(BF16)
            gwo_ref[...] = acc_o[...].astype(BF16)

    return pl.pallas_call(
        body, name="bwd_merge", grid=(n,),
        in_specs=[_row(tm, D), _row(tm, D), _row(tm, Q_W), _row(tm, GM_W), _row(tm, D), _row(tm, D),
                  _resident((Q_W, D)), _resident((GM_W, D)), _resident((D, D))],
        out_specs=[_row(tm, D), _row(tm, D), _row(tm, Q_W), _row(tm, GM_W), _full((Q_W, D)), _full((GM_W, D)), _full((D, D))],
        out_shape=[_sds((L, D), BF16), _sds((L, D), BF16), _sds((L, Q_W), BF16), _sds((L, GM_W), BF16),
                   _sds((Q_W, D), BF16), _sds((GM_W, D), BF16), _sds((D, D), BF16)],
        scratch_shapes=[pltpu.VMEM((Q_W, D), F32), pltpu.VMEM((GM_W, D), F32), pltpu.VMEM((D, D), F32)],
        compiler_params=_params(("arbitrary",)),
    )(dmix, merged, ya, yb, ga, gb, w_a, w_b, w_o)


def _k_gmlp_bwd(u, vb, dyb, lnv, ws, wst, bsp):
    L = u.shape[0]
    nch = min(GMLP_CHUNKS, L // BLK)
    tm = nch * BLK

    def body(u_ref, vb_ref, dyb_ref, lnv_ref, ws_ref, wst_ref, bsp_ref, du_ref, dvb_ref, gws_ref, gbst_ref, gln_ref):
        @pl.when(pl.program_id(0) == 0)
        def _():
            gws_ref[...] = jnp.zeros_like(gws_ref)
            gbst_ref[...] = jnp.zeros_like(gbst_ref)
            gln_ref[...] = jnp.zeros_like(gln_ref)

        uf, vf, gu, tu, tv, vhat, rstd, vn, s = _gmlp_fwd_vals(u_ref[...], vb_ref[...], lnv_ref, ws_ref, bsp_ref, nch)
        dyb_f = dyb_ref[...].astype(F32)
        du_ref[...] = (dyb_f * s * _gelu_grad(uf, tu)).astype(BF16)
        ds = dyb_f * gu
        ds_b = ds.astype(BF16)
        for pr in range(N_GROUPS // 2):
            lanes = slice(pr * 128, (pr + 1) * 128)
            gw_lo = gw_hi = ds_sum = None
            for c in range(nch):
                rows = slice(c * BLK, (c + 1) * BLK)
                lo, hi = _split_pair(ds_b[rows, lanes])
                t_lo = lax.dot_general(lo, vn[rows, lanes], NT, preferred_element_type=F32)
                t_hi = lax.dot_general(hi, vn[rows, lanes], NT, preferred_element_type=F32)
                gw_lo = t_lo if c == 0 else gw_lo + t_lo
                gw_hi = t_hi if c == 0 else gw_hi + t_hi
                ds_sum = ds[rows, lanes] if c == 0 else ds_sum + ds[rows, lanes]
            gws_ref[2 * pr] += gw_lo
            gws_ref[2 * pr + 1] += gw_hi
            b_lo, b_hi = _split_pair(ds_sum)
            gbst_ref[:, 2 * pr:2 * pr + 1] += jnp.sum(b_lo, axis=1, keepdims=True)
            gbst_ref[:, 2 * pr + 1:2 * pr + 2] += jnp.sum(b_hi, axis=1, keepdims=True)
        dvn = _gmlp_spatial(wst_ref, ds_b, nch)
        gln_ref[0:1, :] += _colsum(dvn * vhat)
        gln_ref[1:2, :] += _colsum(dvn)
        dgv = _ln_bwd(dvn * lnv_ref[0:1, :], vhat, rstd)
        dvb_ref[...] = (dgv * _gelu_grad(vf, tv)).astype(BF16)

    return pl.pallas_call(
        body, name="bwd_gmlp", grid=(L // tm,),
        in_specs=[_row(tm, GM_W)] * 3 + [_full((8, GM_W)), _full((N_GROUPS, BLK, BLK)), _full((N_GROUPS, BLK, BLK)),
                                         _full((BLK, GM_W))],
        out_specs=[_row(tm, GM_W), _row(tm, GM_W), _full((N_GROUPS, BLK, BLK)), _full((BLK, N_GROUPS)), _full((8, GM_W))],
        out_shape=[_sds((L, GM_W), BF16), _sds((L, GM_W), BF16), _sds((N_GROUPS, BLK, BLK), F32),
                   _sds((BLK, N_GROUPS), F32), _sds((8, GM_W), F32)],
        compiler_params=_params(("arbitrary",)),
    )(u, vb, dyb, lnv, ws, wst, bsp)


ATTN_BWD_BLOCKS = 2


def _k_attn_bwd(sink, q, k, v, kc, vc, dya, lse, cos, sin, bias, comm=None):
    L = q.shape[0]
    C = kc.shape[0]
    nb = L // BLK
    nq = min(ATTN_BWD_BLOCKS, nb)
    steps = nb // nq
    NK = C + 3 * BLK
    chains = [(qb, hk) for qb in range(nq) for hk in range(N_KV_HEADS)]

    def body(sink_ref, q_ref, kp_ref, km_ref, kx_ref, vp_ref, vm_ref, vx_ref, kc_ref, vc_ref, do_ref, lse_ref,
             cq_ref, sq_ref, cl_ref, sl_ref, bias_ref,
             dq_ref, dk_ref, dv_ref, dkc_ref, dvc_ref, dsink_ref,
             dq_scr, ck_scr, cv_scr, k1_acc, k2_acc, v1_acc, v2_acc):
        i = pl.program_id(0)

        @pl.when(i == 0)
        def _():
            for r in (k1_acc, k2_acc, v1_acc, v2_acc, dkc_ref, dvc_ref, dsink_ref):
                r[...] = jnp.zeros_like(r)

        @pl.when(i < steps)
        def _():
            def band(qb):
                first = jnp.where(i == 0, 0, 1) if qb == 0 else 1
                return bias_ref[jnp.where(i == steps - 1, 2, first) if qb == nq - 1 else first]

            def lanes(hk):
                return slice(hk * HEAD_DIM, (hk + 1) * HEAD_DIM)

            def keys(ctx_ref, p_ref, m_ref, x_ref, qb, hk):
                sl = lanes(hk)
                blocks = [p_ref[:, sl]] + [m_ref[j * BLK:(j + 1) * BLK, sl] for j in range(nq)] + [x_ref[:, sl]]
                return jnp.concatenate([ctx_ref[:, sl]] + blocks[qb:qb + 3], axis=0)

            def stacked(ref, qb, hk, width):
                return jnp.concatenate(
                    [ref[qb * BLK:(qb + 1) * BLK, (hk * GQA_GROUP + g) * width:(hk * GQA_GROUP + g + 1) * width]
                     for g in range(GQA_GROUP)], axis=0)

            def scores(qb, hk):
                kcat = keys(kc_ref, kp_ref, km_ref, kx_ref, qb, hk)
                qg = stacked(q_ref, qb, hk, HEAD_DIM)
                s = _masked(lax.dot_general(qg, kcat, NT, preferred_element_type=F32), band(qb), C)
                dog = stacked(do_ref, qb, hk, HEAD_DIM)
                dp = lax.dot_general(dog, keys(vc_ref, vp_ref, vm_ref, vx_ref, qb, hk), NT, preferred_element_type=F32)
                return kcat, qg, dog, s, dp

            def softmax_bwd(qb, hk, s, dp):
                lse_c = stacked(lse_ref, qb, hk, 1)
                p = jnp.exp(s - lse_c)
                delta = jnp.sum(p * dp, axis=1, keepdims=True)
                ds = (p * (dp - delta)).astype(BF16)
                p_sink = jnp.exp(_sink_col(sink_ref, hk) - lse_c) * delta
                return p.astype(BF16), ds, p_sink

            def put_dq(qb, hk, dqs, p_sink):
                for g in range(GQA_GROUP):
                    h = hk * GQA_GROUP + g
                    dq_scr[qb * BLK:(qb + 1) * BLK, h * HEAD_DIM:(h + 1) * HEAD_DIM] = dqs[g * BLK:(g + 1) * BLK, :]
                    tot = jnp.sum(p_sink[g * BLK:(g + 1) * BLK, :], axis=0, keepdims=True)
                    dsink_ref[h:h + 1, :] -= jnp.broadcast_to(tot, (1, 128))

            ahead = 4
            sc = [scores(*c) for c in chains[:ahead]]
            pending = None
            for n, (qb, hk) in enumerate(chains):
                if n + ahead < len(chains):
                    sc.append(scores(*chains[n + ahead]))
                kcat, qg, dog, s, dp = sc[n]
                pb, ds, p_sink = softmax_bwd(qb, hk, s, dp)
                if pending is not None:
                    pqb, phk, pds, ppb, pqg, pdog = pending
                    ck_scr[pqb, :, lanes(phk)] = lax.dot_general(pds, pqg, TN, preferred_element_type=F32)
                    cv_scr[pqb, :, lanes(phk)] = lax.dot_general(ppb, pdog, TN, preferred_element_type=F32)
                put_dq(qb, hk, jnp.dot(ds, kcat, preferred_element_type=F32), p_sink)
                pending = (qb, hk, ds, pb, qg, dog)
            pqb, phk, pds, ppb, pqg, pdog = pending
            ck_scr[pqb, :, lanes(phk)] = lax.dot_general(pds, pqg, TN, preferred_element_type=F32)
            cq, sq = cq_ref[...], sq_ref[...]
            for j in range(4):
                dq_ref[:, j * 128:(j + 1) * 128] = _unrope(dq_scr[:, j * 128:(j + 1) * 128] * Q_SCALE, cq, sq).astype(BF16)
            cv_scr[pqb, :, lanes(phk)] = lax.dot_general(ppb, pdog, TN, preferred_element_type=F32)
            dkc_ref[...] += functools.reduce(lambda a, b: a + b, [ck_scr[qb, 0:C, :] for qb in range(nq)])
            dvc_ref[...] += functools.reduce(lambda a, b: a + b, [cv_scr[qb, 0:C, :] for qb in range(nq)])

        @pl.when(i >= steps)
        def _():
            ck_scr[...] = jnp.zeros_like(ck_scr)
            cv_scr[...] = jnp.zeros_like(cv_scr)

        def slot(scr, r, carried):
            parts = [scr[qb, C + (r - qb) * BLK:C + (r - qb + 1) * BLK, :] for qb in range(nq) if 0 <= r - qb <= 2]
            total = functools.reduce(lambda a, b: a + b, parts)
            return total if carried is None else carried[...] + total

        for r in range(nq):
            rows = slice(r * BLK, (r + 1) * BLK)
            carried_k, carried_v = ((k1_acc, v1_acc), (k2_acc, v2_acc), (None, None))[min(r, 2)]
            tables = (cl_ref[...], sl_ref[...]) if r == 0 else (cq_ref[(r - 1) * BLK:r * BLK, :], sq_ref[(r - 1) * BLK:r * BLK, :])
            dk_ref[rows, :] = _unrope(slot(ck_scr, r, carried_k), *tables).astype(BF16)
            dv_ref[rows, :] = slot(cv_scr, r, carried_v).astype(BF16)
        k1_acc[...] = slot(ck_scr, nq, None)
        v1_acc[...] = slot(cv_scr, nq, None)
        k2_acc[...] = slot(ck_scr, nq + 1, None)
        v2_acc[...] = slot(cv_scr, nq + 1, None)

    last = steps - 1
    kv3 = [pl.BlockSpec((BLK, KV_W), lambda i: (jnp.clip(nq * i - 1, 0, nb - 1), 0)),
           pl.BlockSpec((nq * BLK, KV_W), lambda i: (jnp.minimum(i, last), 0)),
           pl.BlockSpec((BLK, KV_W), lambda i: (jnp.minimum(nq * i + nq, nb - 1), 0))]
    cur = lambda w: pl.BlockSpec((nq * BLK, w), lambda i: (jnp.minimum(i, last), 0))
    late = lambda w: pl.BlockSpec((BLK, w), lambda i: (jnp.clip(nq * i - 1, 0, nb - 1), 0))
    out2 = lambda w: pl.BlockSpec((nq * BLK, w), lambda i: (i, 0))
    return _call(
        body, name="bwd_attn", grid=(steps + 1,),
        in_specs=[pl.BlockSpec(memory_space=pltpu.SMEM), cur(Q_W)] + kv3 + kv3
                 + [_full((C, KV_W)), _full((C, KV_W)), cur(Q_W), cur(N_Q_HEADS), cur(128), cur(128), late(128), late(128),
                    _full((3, GQA_GROUP * BLK, 3 * BLK))],
        out_specs=[cur(Q_W), out2(KV_W), out2(KV_W), _full((C, KV_W)), _full((C, KV_W)), _full((8, 128))],
        out_shape=[_sds((L, Q_W), BF16), _sds((L + nq * BLK, KV_W), BF16), _sds((L + nq * BLK, KV_W), BF16),
                   _sds((C, KV_W), F32), _sds((C, KV_W), F32), _sds((8, 128), F32)],
        scratch=[pltpu.VMEM((nq * BLK, Q_W), F32), pltpu.VMEM((nq, NK, KV_W), F32), pltpu.VMEM((nq, NK, KV_W), F32)]
                + [pltpu.VMEM((BLK, KV_W), F32)] * 4,
        args=(sink, q, k, k, k, v, v, v, kc, vc, dya, lse, cos, sin, cos, sin, bias), comm=comm)


def _k_ctx_bwd(ctx, modc, hc, dkc, dvc, w_kv):
    C = ctx.shape[0]

    def body(c_ref, mod_ref, hc_ref, dkc_ref, dvc_ref, w_ref, gw_ref, dmod_ref):
        dkv = jnp.concatenate([dkc_ref[...], dvc_ref[...]], axis=1).astype(BF16)
        gw_ref[...] = lax.dot_general(dkv, hc_ref[...], TN, preferred_element_type=F32)
        dhc = jnp.dot(dkv, w_ref[...], preferred_element_type=F32)
        n, _ = _ln(c_ref[...])
        dmod_ref[...] = jnp.zeros_like(dmod_ref)
        dmod_ref[0:1, :] = _colsum(dhc)
        dmod_ref[1:2, :] = _colsum(dhc * n)

    return pl.pallas_call(
        body, name="bwd_ctx", grid=(1,),
        in_specs=[_full((C, D)), _full((8, D)), _full((C, D)), _full((C, KV_W)), _full((C, KV_W)), _full((2 * KV_W, D))],
        out_specs=[_full((2 * KV_W, D)), _full((8, D))],
        out_shape=[_sds((2 * KV_W, D), F32), _sds((8, D), F32)],
        compiler_params=_params(("arbitrary",)),
    )(ctx, modc, hc, dkc, dvc, w_kv)


def _k_in_bwd(dq, dk, dv, du, dvb, dga, dgb, x, dxp, w_in, modv, tm, comm=None):
    L = x.shape[0]
    parts = [(O_Q, Q_W), (O_K, KV_W), (O_V, KV_W), (O_U, GM_W), (O_VB, GM_W), (O_GA, D), (O_GB, D)]

    def body(dq_ref, dk_ref, dv_ref, du_ref, dvb_ref, dga_ref, dgb_ref, x_ref, dxp_ref, w_ref, mod_ref,
             dP_ref, gx_ref, acc_ref):
        @pl.when(pl.program_id(0) == 0)
        def _():
            acc_ref[...] = jnp.zeros_like(acc_ref)

        for (lo, width), r in zip(parts, (dq_ref, dk_ref, dv_ref, du_ref, dvb_ref, dga_ref, dgb_ref)):
            dP_ref[:, lo:lo + width] = r[...]
        n1, rstd1 = _ln(x_ref[...])
        dh = jnp.dot(dP_ref[...], w_ref[...], preferred_element_type=F32)
        acc_ref[0:1, :] += _colsum(dh)
        acc_ref[1:2, :] += _colsum(dh * n1)
        gx_ref[...] = dxp_ref[...] + _ln_bwd(dh * (1.0 + mod_ref[1:2, :]), n1, rstd1)

    return _call(
        body, name="bwd_in", grid=(L // tm,),
        in_specs=[_row(tm, w) for _, w in parts] + [_row(tm, D), _row(tm, D), _resident((IN_W, D)), _full((8, D))],
        out_specs=[_row(tm, IN_W), _row(tm, D), _full((8, D))],
        out_shape=[_sds((L, IN_W), BF16), _sds((L, D), F32), _sds((8, D), F32)],
        args=(dq, dk, dv, du, dvb, dga, dgb, x, dxp, w_in, modv), comm=comm)


def _wgrad(a, b, name, tk, tt, comm=None, extra=None):
    T, K = a.shape
    N = b.shape[1]
    nt = T // tt

    def body(*refs):
        a_ref, b_ref = refs[:2]
        o_ref, acc_ref = refs[-2:]
        j, t = pl.program_id(0), pl.program_id(1)

        @pl.when(t == 0)
        def _():
            acc_ref[...] = jnp.zeros_like(acc_ref)

        acc_ref[...] += lax.dot_general(a_ref[...], b_ref[...], TN, preferred_element_type=F32)

        if extra is not None:
            lo, rows = extra[0] % tk, extra[1].shape[0]

            @pl.when((t == nt - 1) & (j == extra[0] // tk))
            def _():
                acc_ref[lo:lo + rows, :] += refs[2][...]

        @pl.when(t == nt - 1)
        def _():
            o_ref[...] = acc_ref[...].astype(BF16)

    extra_specs = [] if extra is None else [pl.BlockSpec(extra[1].shape, lambda j, t: (0, 0))]
    (out,), got = _call(
        body, name=name, grid=(K // tk, nt),
        in_specs=[pl.BlockSpec((tt, tk), lambda j, t: (t, j)), pl.BlockSpec((tt, N), lambda j, t: (t, 0))] + extra_specs,
        out_specs=[pl.BlockSpec((tk, N), lambda j, t: (j, 0))],
        out_shape=[_sds((K, N), BF16)],
        scratch=[pltpu.VMEM((tk, N), F32)],
        args=(a, b) + (() if extra is None else (extra[1],)), comm=comm)
    return (out, got) if comm is not None else out


def _adamw_reduce(parts, w, m, v, name, tr):
    R, C = w.shape
    n_parts = parts.shape[0]

    def body(p_ref, w_ref, m_ref, v_ref, g_ref, d_ref, m2_ref, v2_ref):
        g = p_ref[0].astype(F32)
        for i in range(1, n_parts):
            g = g + p_ref[i].astype(F32)
        delta, m2, v2 = _adamw(w_ref[...], g, m_ref[...], v_ref[...])
        g_ref[...] = g
        d_ref[...] = delta
        m2_ref[...] = m2
        v2_ref[...] = v2

    spec = _row(tr, C)
    return pl.pallas_call(
        body, name=name, grid=(R // tr,),
        in_specs=[pl.BlockSpec((n_parts, tr, C), lambda i: (0, i, 0)), spec, spec, spec],
        out_specs=[spec] * 4,
        out_shape=[_sds((R, C), F32)] * 4,
        compiler_params=_params(("arbitrary",)),
    )(parts, w, m, v)


SMALL_ORDER = ("b_ada", "ln1_g", "ln1_b", "ln2_g", "ln2_b", "gmlp_ln_g", "gmlp_ln_b", "b_spatial", "attn_sink")


def _small_step(gath, params):
    flat = [a for name in SMALL_ORDER for a in params[name]]

    def grad_of(tot, name):
        if name == "b_ada":
            return jnp.concatenate([tot[r:r + 1, :] for r in range(6)], axis=1)
        if name in ("ln1_g", "ln1_b", "ln2_g", "ln2_b"):
            r = 8 + ("ln1_g", "ln1_b", "ln2_g", "ln2_b").index(name)
            return tot[r:r + 1, :]
        if name == "gmlp_ln_g":
            return tot[12:13, :GM_W]
        if name == "gmlp_ln_b":
            return tot[12:13, GM_W:]
        if name == "b_spatial":
            return jnp.concatenate([tot[13:14, g * BLK:(g + 1) * BLK] for g in range(N_GROUPS)], axis=0)[None]
        return tot[14:15, :N_Q_HEADS]

    def body(*refs):
        g_ref, in_refs = refs[0], refs[1:1 + len(flat)]
        tot_ref, out_refs = refs[1 + len(flat)], refs[2 + len(flat):]
        tot = g_ref[0]
        for i in range(1, N_DEV):
            tot = tot + g_ref[i]
        tot_ref[...] = tot
        tot_ref[0:2, :] = tot[0:2, :] + tot[6:8, :]
        tot_ref[15:16, :] = jnp.broadcast_to(jnp.sum(tot[15:16, :], axis=1, keepdims=True), (1, D))
        tot = tot_ref[...]
        for k, name in enumerate(SMALL_ORDER):
            w_ref, m_ref, v_ref = in_refs[3 * k:3 * k + 3]
            g = grad_of(tot, name)
            delta, m2, v2 = _adamw(w_ref[...], g, m_ref[...], v_ref[...])
            for r, val in zip(out_refs[4 * k:4 * k + 4], (g, delta, m2, v2)):
                r[...] = val

    res = pl.pallas_call(
        body, name="small_step", grid=(1,),
        in_specs=[_full((N_DEV, 16, D))] + [_full(a.shape) for a in flat],
        out_specs=[_full((16, D))] + [_full(params[name][0].shape) for name in SMALL_ORDER for _ in range(4)],
        out_shape=[_sds((16, D), F32)] + [_sds(params[name][0].shape, F32) for name in SMALL_ORDER for _ in range(4)],
        compiler_params=_params(("arbitrary",)),
    )(gath, *flat)
    return res[0], {name: res[1 + 4 * k:5 + 4 * k] for k, name in enumerate(SMALL_ORDER)}


def _cctx_finish(gath, c_ctx, m, v):
    def body(g_ref, c_ref, m_ref, v_ref, gr_ref, d_ref, m2_ref, v2_ref):
        ds = g_ref[0]
        for i in range(1, N_DEV):
            ds = ds + g_ref[i]
        c = c_ref[...]
        sg = _sigmoid(c)
        g = ds * (sg * (1.0 + c * (1.0 - sg)))
        delta, m2, v2 = _adamw(c, g, m_ref[...], v_ref[...])
        gr_ref[...] = g
        d_ref[...] = delta
        m2_ref[...] = m2
        v2_ref[...] = v2

    return pl.pallas_call(
        body, name="cctx_finish", grid=(1,),
        in_specs=[_full((N_DEV, 8, D))] + [_full((8, D))] * 3, out_specs=[_full((8, D))] * 4,
        out_shape=[_sds((8, D), F32)] * 4,
        compiler_params=_params(("arbitrary",)),
    )(gath, c_ctx, m, v)


def _pad_rows(a, rows):
    return jnp.concatenate([a, jnp.zeros((rows - a.shape[0], a.shape[1]), a.dtype)], axis=0)


def kernel(x, c, ctx, c_ctx, w_ada, b_ada, w_in, attn_sink, gmlp_ln_g, gmlp_ln_b, w_spatial, b_spatial, w_branch_a, w_branch_b, w_out, ln1_g, ln1_b, w_ffn_in, w_ffn_out, ln2_g, ln2_b, loss_target, m_c_ctx, m_w_ada, m_b_ada, m_w_in, m_attn_sink, m_gmlp_ln_g, m_gmlp_ln_b, m_w_spatial, m_b_spatial, m_w_branch_a, m_w_branch_b, m_w_out, m_ln1_g, m_ln1_b, m_w_ffn_in, m_w_ffn_out, m_ln2_g, m_ln2_b, v_c_ctx, v_w_ada, v_b_ada, v_w_in, v_attn_sink, v_gmlp_ln_g, v_gmlp_ln_b, v_w_spatial, v_b_spatial, v_w_branch_a, v_w_branch_b, v_w_out, v_ln1_g, v_ln1_b, v_w_ffn_in, v_w_ffn_out, v_ln2_g, v_ln2_b):
    L = x.shape[1]
    me = 4 * lax.axis_index("x") + 2 * lax.axis_index("y") + lax.axis_index("c")
    x2, tgt, ctx2 = x[0], loss_target[0], ctx[0]
    tiles = _Tiles(L)
    tm_in, tm, tt = tiles.wide, tiles.narrow, tiles.tokens

    transposed = ("w_in", "w_ffn_in")
    tr = lambda kname, a: a.T if kname in transposed else a
    big = dict(w_in=w_in[0].T, w_branch_a=w_branch_a[0], w_branch_b=w_branch_b[0], w_out=w_out[0],
               w_ffn_in=w_ffn_in[0].T, w_ffn_out=w_ffn_out[0])
    col_sharded = ("w_branch_a", "w_branch_b")
    shard_bf = {k: a.astype(BF16) for k, a in big.items()}

    def assemble(kname, g):
        if kname in col_sharded:
            return g.transpose(1, 0, 2).reshape(g.shape[1], N_DEV * g.shape[2])
        return g.reshape(N_DEV * g.shape[1], g.shape[2])

    def to_blocks(kname, g):
        if kname in col_sharded:
            return g.reshape(g.shape[0], N_DEV, g.shape[1] // N_DEV).transpose(1, 0, 2)
        return g.reshape(N_DEV, g.shape[0] // N_DEV, g.shape[1])

    full = {}
    n_ada = w_ada.shape[2]
    b_my = lax.dynamic_slice(b_ada, (0, me * n_ada), (1, n_ada))
    got_in = _sc_gather(shard_bf["w_in"], "sc_gather_w_in")
    got_ffn_in = _sc_gather(shard_bf["w_ffn_in"], "sc_gather_w_ffn_in")
    act, mod_all = _prologue(_pad_rows(c, 8), _pad_rows(c_ctx[None, :], 8), w_ada[0], b_my)
    full["w_in"] = assemble("w_in", got_in)
    mod_all = mod_all.transpose(1, 0, 2).reshape(16, 6 * D)
    modv = _pad_rows(lax.dynamic_slice(mod_all, (me, 0), (1, 6 * D)).reshape(6, D), 8)
    modc = _pad_rows(mod_all[8].reshape(6, D), 8)

    lnv = _pad_rows(jnp.concatenate([ln1_g, ln1_b, ln2_g, ln2_b], axis=0), 8)
    gm_lnv = _pad_rows(jnp.concatenate([gmlp_ln_g, gmlp_ln_b], axis=0), 8)
    ws_b = w_spatial[0].astype(BF16)
    wst_b = ws_b.transpose(0, 2, 1)
    bsp = jnp.repeat(b_spatial[0].T, GROUP_DIM, axis=1)
    sink = attn_sink[0]
    cos, sin = _rope_tables(L)
    bias = _attn_bias()
    w_kv = full["w_in"][O_K:O_K + 2 * KV_W, :]

    (h, q, k, v, u, vb, ga, gb), got = _k_in(
        x2, modv, full["w_in"], cos, sin, tm_in,
        comm=_Comm(gather=[shard_bf[kname] for kname in ("w_branch_a", "w_branch_b", "w_out", "w_ffn_out")]))
    for kname, g in zip(("w_branch_a", "w_branch_b", "w_out", "w_ffn_out"), got):
        full[kname] = assemble(kname, g)
    hc, kc, vc = _k_ctx(ctx2, modc, w_kv)
    (ya, lse), _ = _k_attn(sink, q, k, v, kc, vc, bias)
    full["w_ffn_in"] = assemble("w_ffn_in", got_ffn_in)
    yb = _k_gmlp(u, vb, gm_lnv, ws_b, bsp)
    merged, mix, xm, h2 = _k_merge(x2, ya, yb, ga, gb, full["w_branch_a"], full["w_branch_b"], full["w_out"], modv, lnv, tm_in)
    gate, up, act_f, dr2, df, acc_f = _k_ffn(h2, xm, tgt, full["w_ffn_in"], full["w_ffn_out"], modv, lnv, tm_in)

    dF, dmix, dxp, acc_b = _k_ffn_bwd(df, gate, up, xm, dr2, x2, mix, full["w_ffn_in"], full["w_ffn_out"], modv, lnv, tm)
    blk_fo = to_blocks("w_ffn_out", _wgrad(act_f, df, "wgrad_ffn_out", tiles.tk_ffn, tt))
    gw_fi, (rcv_fo,) = _wgrad(dF, h2, "wgrad_ffn_in", tiles.tk_ffn, tt, comm=_Comm(scatter=[blk_fo]))
    blk_fi = to_blocks("w_ffn_in", gw_fi)
    dga, dgb, dya, dyb, gw_a, gw_b, gw_o = _k_merge_bwd(
        dmix, merged, ya, yb, ga, gb, full["w_branch_a"], full["w_branch_b"], full["w_out"], tm_in)
    du, dvb, g_ws, g_bst, g_gln = _k_gmlp_bwd(u, vb, dyb, gm_lnv, ws_b, wst_b, bsp)
    (dq, dk_late, dv_late, dkc, dvc, g_sink), (gath_ws, rcv_fi) = _k_attn_bwd(
        sink, q, k, v, kc, vc, dya, lse, cos, sin, bias,
        comm=_Comm(gather=[g_ws.reshape(N_GROUPS * BLK, BLK)], scatter=[blk_fi]))
    dk, dv = dk_late[BLK:BLK + L], dv_late[BLK:BLK + L]
    blk_a, blk_b, blk_o = to_blocks("w_branch_a", gw_a), to_blocks("w_branch_b", gw_b), to_blocks("w_out", gw_o)
    (dP, grad_x, acc_i), _ = _k_in_bwd(dq, dk, dv, du, dvb, dga, dgb, x2, dxp, full["w_in"], modv, tm_in)
    g_ctx, dmodc = _k_ctx_bwd(ctx2, modc, hc, dkc, dvc, w_kv)
    gw_in, (rcv_a, rcv_b, rcv_o) = _wgrad(dP, h, "wgrad_in", tiles.tk_in, tt, comm=_Comm(scatter=[blk_a, blk_b, blk_o]),
                                          extra=(O_K, g_ctx))

    dmod_x = jnp.concatenate([acc_i[0:2], acc_b[4:5], acc_b[0:2], acc_f[2:3]], axis=0)
    small = jnp.concatenate([
        dmod_x, dmodc[0:2], acc_b[2:4], acc_f[0:2],
        jnp.concatenate([g_gln[0:1], g_gln[1:2]], axis=1), g_bst.T.reshape(1, D),
        _pad_rows(g_sink[:, 0:1], D).T, acc_f[3:4]], axis=0)
    chip_sums, gath = _exchange_two_level(to_blocks("w_in", gw_in), small, "exchange_last", ici=False)
    rcv_in = jnp.concatenate([chip_sums[0:1], _sc_chip_exchange(chip_sums)], axis=0)
    received = dict(w_in=rcv_in, w_branch_a=rcv_a, w_branch_b=rcv_b, w_out=rcv_o, w_ffn_in=rcv_fi, w_ffn_out=rcv_fo)
    moments = dict(w_in=(m_w_in, v_w_in), w_branch_a=(m_w_branch_a, v_w_branch_a), w_branch_b=(m_w_branch_b, v_w_branch_b),
                   w_out=(m_w_out, v_w_out), w_ffn_in=(m_w_ffn_in, v_w_ffn_in), w_ffn_out=(m_w_ffn_out, v_w_ffn_out))
    names = list(big)
    res = {}
    for kname in names:
        mm, vv = moments[kname]
        R = big[kname].shape[0]
        res[kname] = [tr(kname, r) for r in _adamw_reduce(
            received[kname], big[kname], tr(kname, mm[0]), tr(kname, vv[0]), "adamw_" + kname, 256 if R % 256 == 0 else R // 2)]

    ws2d = lambda a: a.reshape(N_GROUPS * BLK, BLK)
    res_ws = [r.reshape(w_spatial.shape) for r in _adamw_reduce(
        gath_ws, ws2d(w_spatial), ws2d(m_w_spatial), ws2d(v_w_spatial), "adamw_w_spatial", 256)]
    tot, res_small = _small_step(gath, dict(
        b_ada=(b_ada, m_b_ada, v_b_ada), ln1_g=(ln1_g, m_ln1_g, v_ln1_g), ln1_b=(ln1_b, m_ln1_b, v_ln1_b),
        ln2_g=(ln2_g, m_ln2_g, v_ln2_g), ln2_b=(ln2_b, m_ln2_b, v_ln2_b),
        gmlp_ln_g=(gmlp_ln_g, m_gmlp_ln_g, v_gmlp_ln_g), gmlp_ln_b=(gmlp_ln_b, m_gmlp_ln_b, v_gmlp_ln_b),
        b_spatial=(b_spatial, m_b_spatial, v_b_spatial), attn_sink=(attn_sink, m_attn_sink, v_attn_sink)))
    loss = tot[15, 0]

    dmod_rows = jnp.concatenate([gath[:, 0:6, :].reshape(N_DEV, 6 * D),
                                 jnp.concatenate([tot[6:8].reshape(1, 2 * D), jnp.zeros((1, 4 * D), F32)], axis=1),
                                 jnp.zeros((7, 6 * D), F32)], axis=0)
    dmod_my = lax.dynamic_slice(dmod_rows, (0, me * n_ada), (16, n_ada))
    g_wada, d_wada, m2_wada, v2_wada, pc = _ada_bwd(act, dmod_my, w_ada[0], m_w_ada[0], v_w_ada[0])
    pc_all = _gather_rows(pc, "gather_cctx")
    cc8 = lambda a: _pad_rows(a.reshape(1, D), 8)
    g_cc, d_cc, m2_cc, v2_cc = _cctx_finish(pc_all, cc8(c_ctx), cc8(m_c_ctx), cc8(v_c_ctx))

    order = ["c_ctx", "w_ada", "b_ada", "w_in", "attn_sink", "gmlp_ln_g", "gmlp_ln_b", "w_spatial", "b_spatial",
             "w_branch_a", "w_branch_b", "w_out", "ln1_g", "ln1_b", "w_ffn_in", "w_ffn_out", "ln2_g", "ln2_b"]
    grads, deltas, new_m, new_v = {}, {}, {}, {}
    grads["c_ctx"], deltas["c_ctx"], new_m["c_ctx"], new_v["c_ctx"] = g_cc[0], d_cc[0], m2_cc[0], v2_cc[0]
    grads["w_ada"], deltas["w_ada"], new_m["w_ada"], new_v["w_ada"] = g_wada[None], d_wada[None], m2_wada[None], v2_wada[None]
    for kname in names:
        g, d, m2, v2 = res[kname]
        grads[kname], deltas[kname], new_m[kname], new_v[kname] = g[None], d[None], m2[None], v2[None]
    grads["w_spatial"], deltas["w_spatial"], new_m["w_spatial"], new_v["w_spatial"] = res_ws
    for kname in SMALL_ORDER:
        grads[kname], deltas[kname], new_m[kname], new_v[kname] = res_small[kname]
    return (loss, grad_x[None], *[grads[n] for n in order], *[deltas[n] for n in order],
            *[new_m[n] for n in order], *[new_v[n] for n in order])
```

```python
import functools
import math

import jax
import jax.numpy as jnp
import numpy as np
from jax import lax
from jax.experimental import pallas as pl
from jax.experimental.pallas import tpu as pltpu
from jax.experimental.pallas import tpu_sc as plsc

F32 = jnp.float32
BF16 = jnp.bfloat16
MESH = pl.DeviceIdType.MESH

N_DEV = 8
D = 1024
HEAD_DIM = 64
N_Q_HEADS = 8
N_KV_HEADS = 2
GQA_GROUP = 4
BLK = 128
Q_W = 512
KV_W = 128
GM_W = 512
N_GROUPS = 8
GROUP_DIM = 64
FFN_H = 2816
IN_W = 3840
O_Q, O_K, O_V, O_U, O_VB, O_GA, O_GB = 0, 512, 640, 768, 1280, 1792, 2816
LN_EPS = 1e-5
NEG_INF = -1e30
ALPHA = 2.0 ** 0.25
ROPE_BASE = 10000.0
ROPE_PAIRS = 16
Q_SCALE = HEAD_DIM ** -0.5
GELU_K0 = math.sqrt(2.0 / math.pi)
GELU_K1 = 0.044715

ADAM_LR = 0.001
ADAM_B1 = 0.9
ADAM_B2 = 0.999
ADAM_EPS = 1e-08
ADAM_WD = 0.01
ADAM_STEP = 10

V7X_VMEM_BYTES = 64 * 1024 * 1024
VMEM_LIMIT = V7X_VMEM_BYTES * 7 // 8
NT = (((1,), (1,)), ((), ()))
TN = (((0,), (0,)), ((), ()))


class _Tiles:
    def __init__(self, L):
        self.wide = min(512, L)
        self.narrow = min(256, L)
        self.tokens = min(2048, L)
        self.tk_in = IN_W // 3
        self.tk_ffn = FFN_H // 2


def _params(sem=None):
    return pltpu.CompilerParams(dimension_semantics=sem, vmem_limit_bytes=VMEM_LIMIT)


def _row(tm, w):
    return pl.BlockSpec((tm, w), lambda i: (i, 0))


def _full(shape):
    nd = len(shape)
    return pl.BlockSpec(shape, lambda i: (0,) * nd)


def _resident(shape):
    nd = len(shape)
    return pl.BlockSpec(shape, lambda i: (0,) * nd, pipeline_mode=pl.Buffered(1))


def _sds(shape, dt):
    return jax.ShapeDtypeStruct(shape, dt)


def _ln(xf):
    mu = jnp.mean(xf, axis=-1, keepdims=True)
    xc = xf - mu
    var = jnp.mean(xc * xc, axis=-1, keepdims=True)
    rstd = lax.rsqrt(var + LN_EPS)
    return xc * rstd, rstd


def _ln_bwd(dn, n, rstd):
    m1 = jnp.mean(dn, axis=-1, keepdims=True)
    m2 = jnp.mean(dn * n, axis=-1, keepdims=True)
    return rstd * (dn - m1 - n * m2)


def _colsum(t):
    return jnp.sum(t, axis=0, keepdims=True)


def _sigmoid(x):
    return 0.5 * jnp.tanh(0.5 * x) + 0.5


def _gelu(x):
    t = jnp.tanh(x * (GELU_K0 + (GELU_K0 * GELU_K1) * (x * x)))
    h = 0.5 * x
    return h + h * t, t


def _gelu_grad(x, t):
    return 0.5 + 0.5 * t + (0.5 * x) * (1.0 - t * t) * (GELU_K0 + (3.0 * GELU_K0 * GELU_K1) * (x * x))


def _swap16(t):
    lane = lax.broadcasted_iota(jnp.int32, t.shape, 1)
    return jnp.where((lane & 16) == 0, pltpu.roll(t, 112, 1), pltpu.roll(t, 16, 1))


def _rope(t, cos, sin):
    return t * cos + _swap16(t) * sin


def _unrope(t, cos, sin):
    return t * cos - _swap16(t) * sin


def _adamw(w, g, m, v):
    m2 = ADAM_B1 * m + (1.0 - ADAM_B1) * g
    v2 = ADAM_B2 * v + (1.0 - ADAM_B2) * (g * g)
    m_hat = m2 / (1.0 - ADAM_B1 ** ADAM_STEP)
    v_hat = v2 / (1.0 - ADAM_B2 ** ADAM_STEP)
    delta = -ADAM_LR * (m_hat / (jnp.sqrt(v_hat) + ADAM_EPS) + ADAM_WD * w)
    return delta, m2, v2


def _rope_tables(L):
    inv = (np.float32(ROPE_BASE) ** (-np.arange(ROPE_PAIRS, dtype=np.float32) / np.float32(ROPE_PAIRS))).astype(np.float32)
    t = np.arange(L, dtype=np.int32)
    rows = (t // 64).astype(np.float32)[:, None] * inv
    cols = (t % 64).astype(np.float32)[:, None] * inv
    cr, sr, cc, sc = np.cos(rows), np.sin(rows), np.cos(cols), np.sin(cols)
    cos = np.concatenate([cr, cr, cc, cc], axis=1)
    sin = np.concatenate([-sr, sr, -sc, sc], axis=1)
    return jnp.asarray(np.tile(cos, (1, 2)), F32), jnp.asarray(np.tile(sin, (1, 2)), F32)


def _me():
    return lax.axis_index("x"), lax.axis_index("y"), lax.axis_index("c")


def _peer(mx, my, mc, k):
    return (mx ^ ((k >> 2) & 1), my ^ ((k >> 1) & 1), mc ^ (k & 1))


class _Comm:
    def __init__(self, gather=(), scatter=(), spread=()):
        self.kinds = ["gather"] * len(gather) + ["scatter"] * len(scatter) + ["spread"] * len(spread)
        self.args = list(gather) + list(scatter) + list(spread)
        self.n = len(self.args)

    def out_shape(self):
        return [_sds(a.shape if k == "scatter" else (N_DEV,) + a.shape, a.dtype) for k, a in zip(self.kinds, self.args)]

    def specs(self):
        return [pl.BlockSpec(memory_space=pl.ANY)] * self.n

    def scratch(self):
        return [pltpu.SemaphoreType.DMA((7 * self.n,)), pltpu.SemaphoreType.DMA((7 * self.n,)),
                pltpu.SemaphoreType.DMA((self.n,))]

    def _plan(self, x_refs, out_refs, send_sems, recv_sems, local_sems):
        mx, my, mc = _me()
        me = 4 * mx + 2 * my + mc
        here, sibling = (mx, my, mc), (mx, my, 1 - mc)
        chips = [(1 - mx, my), (mx, 1 - my), (1 - mx, 1 - my)]
        local, first, last = [], [], []
        relay = [[], [], []]
        for a, kind in enumerate(self.kinds):
            x, out = x_refs[a], out_refs[a]

            def rc(k, src, dst, to):
                return pltpu.make_async_remote_copy(
                    src_ref=src, dst_ref=dst, send_sem=send_sems.at[7 * a + k], recv_sem=recv_sems.at[7 * a + k],
                    device_id=to, device_id_type=MESH)

            if kind == "gather":
                local.append(pltpu.make_async_copy(x, out.at[me], local_sems.at[a]))
                first.append(rc(0, x, out.at[me], sibling))
                last.append(rc(0, x, out.at[me ^ 1], here))
                for j, (cx, cy) in enumerate(chips):
                    first.append(rc(1 + j, x, out.at[me], (cx, cy, mc)))
                    landed = out.at[4 * cx + 2 * cy + mc]
                    relay[j].append((rc(1 + j, x, landed, here), rc(4 + j, landed, landed, sibling)))
                    last.append(rc(4 + j, x, out.at[4 * cx + 2 * cy + 1 - mc], here))
            else:
                own = x.at[me] if kind == "scatter" else x
                local.append(pltpu.make_async_copy(own, out.at[me], local_sems.at[a]))
                for k in range(1, N_DEV):
                    src = x.at[me ^ k] if kind == "scatter" else x
                    first.append(rc(k - 1, src, out.at[me], _peer(mx, my, mc, k)))
                    last.append(rc(k - 1, own, out.at[me ^ k], here))
        return local, first, relay[0] + relay[1] + relay[2], last

    def start(self, *refs):
        local, first, _, _ = self._plan(*refs)
        for cp in local + first:
            cp.start()

    def relay(self, *refs):
        _, _, relay, _ = self._plan(*refs)
        for arrival, onward in relay:
            arrival.wait_recv()
            onward.start()

    def finish(self, *refs):
        local, first, relay, last = self._plan(*refs)
        for cp in last:
            cp.wait_recv()
        for cp in first:
            cp.wait_send()
        for _, onward in relay:
            onward.wait_send()
        for cp in local:
            cp.wait()


def _call(body, *, name, grid, in_specs, out_specs, out_shape, args, scratch=(), comm=None, aliases=None):
    params = _params(("arbitrary",) * len(grid))
    total = math.prod(grid)

    def at(step):
        flat = functools.reduce(lambda acc, dn: acc * dn[1] + pl.program_id(dn[0]), enumerate(grid), 0)
        return flat == step

    if comm is None:
        res = pl.pallas_call(
            body, name=name, grid=grid, in_specs=list(in_specs), out_specs=list(out_specs), out_shape=list(out_shape),
            scratch_shapes=list(scratch), input_output_aliases=aliases or {}, compiler_params=params)(*args)
        return list(res), []
    n_in, n_out, n_scr, cn = len(in_specs), len(out_specs), len(scratch), comm.n

    def hosted(*refs):
        ins, refs = refs[:n_in], refs[n_in:]
        cins, refs = refs[:cn], refs[cn:]
        outs, refs = refs[:n_out], refs[n_out:]
        couts, refs = refs[:cn], refs[cn:]
        scr, sems = refs[:n_scr], refs[n_scr:]

        @pl.when(at(0))
        def _():
            comm.start(cins, couts, *sems)

        body(*ins, *outs, *scr)

        @pl.when(at((3 * total) // 4 if total >= 4 else total - 1))
        def _():
            comm.relay(cins, couts, *sems)

        @pl.when(at(total - 1))
        def _():
            comm.finish(cins, couts, *sems)

    res = pl.pallas_call(
        hosted, name=name, grid=grid, in_specs=list(in_specs) + comm.specs(), out_specs=list(out_specs) + comm.specs(),
        out_shape=list(out_shape) + comm.out_shape(), scratch_shapes=list(scratch) + comm.scratch(),
        input_output_aliases=aliases or {}, compiler_params=params)(*args, *comm.args)
    return list(res[:n_out]), list(res[n_out:])


def _sc_chip_exchange(t):
    hbm = pltpu.MemorySpace.HBM
    t_ref = jax.new_ref(t, memory_space=hbm)
    got_ref = jax.empty_ref(jax.ShapeDtypeStruct((3,) + t.shape[1:], t.dtype), memory_space=hbm)
    dma = pltpu.SemaphoreType.DMA

    @pl.kernel(mesh=plsc.ScalarSubcoreMesh(axis_name="seq", num_cores=1), name="sc_chip_exchange",
               scratch_types=(dma, dma, dma, dma, dma, dma), compiler_params=pltpu.CompilerParams(collective_id=3))
    def launch(s1, s2, s3, r1, r2, r3):
        mx, my, mc = _me()
        peers = [(mx ^ (k >> 1), my ^ (k & 1), mc) for k in range(1, 4)]
        barrier = pltpu.get_barrier_semaphore()
        for peer in peers:
            pl.semaphore_signal(barrier, inc=1, device_id=peer, device_id_type=MESH)
        pl.semaphore_wait(barrier, 3)
        copies = [pltpu.make_async_remote_copy(src_ref=t_ref.at[k + 1], dst_ref=got_ref.at[k], send_sem=s, recv_sem=r,
                                               device_id=peer, device_id_type=MESH)
                  for k, (peer, s, r) in enumerate(zip(peers, (s1, s2, s3), (r1, r2, r3)))]
        for cp in copies:
            cp.start()
        for cp in copies:
            cp.wait()

    launch()
    return got_ref[...]


def _exchange_two_level(blk, small, name, ici=True):
    _, R, C = blk.shape
    rows = small.shape[0]

    def body(blk_ref, small_ref, stage_ref, out_ref, gath_ref, a_scr, b_scr, t_scr, s1, r1, s3, r3, ss, rs, lsem):
        mx, my, mc = _me()
        me = 4 * mx + 2 * my + mc
        mine = 2 * mx + my
        here, sibling = (mx, my, mc), (mx, my, 1 - mc)

        def rc(src, dst, send, recv, to):
            return pltpu.make_async_remote_copy(src_ref=src, dst_ref=dst, send_sem=send, recv_sem=recv,
                                                device_id=to, device_id_type=MESH)

        own_small = pltpu.make_async_copy(small_ref, gath_ref.at[me], lsem.at[0])
        own_small.start()
        spread = [rc(small_ref, gath_ref.at[me], ss.at[k - 1], rs.at[k - 1], _peer(mx, my, mc, k)) for k in range(1, N_DEV)]
        order = (1, 2, 3, 0)
        to_sib = [rc(blk_ref.at[2 * (mine ^ k) + 1 - mc], stage_ref.at[k], s1.at[k], r1.at[k], sibling) for k in order]
        for cp in spread + to_sib:
            cp.start()
        own = {k: pltpu.make_async_copy(blk_ref.at[2 * (mine ^ k) + mc], a_scr.at[k], lsem.at[1 + k]) for k in order}
        for k in order:
            own[k].start()
        onward = []
        for k in order:
            rc(blk_ref.at[0], stage_ref.at[k], s1.at[k], r1.at[k], here).wait_recv()
            landed = pltpu.make_async_copy(stage_ref.at[k], b_scr.at[k], lsem.at[5 + k])
            landed.start()
            landed.wait()
            own[k].wait()
            t_scr[k] = (a_scr[k].astype(F32) + b_scr[k].astype(F32)).astype(BF16)
            if k > 0 and ici:
                cp = rc(t_scr.at[k], out_ref.at[mine], s3.at[k - 1], r3.at[k - 1], (mx ^ (k >> 1), my ^ (k & 1), mc))
                cp.start()
                onward.append(cp)
        if ici:
            keep = pltpu.make_async_copy(t_scr.at[0], out_ref.at[mine], lsem.at[9])
        else:
            keep = pltpu.make_async_copy(t_scr, out_ref, lsem.at[9])
        keep.start()
        for k in range(1, 4 if ici else 1):
            rc(t_scr.at[0], out_ref.at[mine ^ k], s3.at[k - 1], r3.at[k - 1], here).wait_recv()
        for k in range(1, N_DEV):
            rc(small_ref, gath_ref.at[me ^ k], ss.at[k - 1], rs.at[k - 1], here).wait_recv()
        for cp in spread + to_sib + onward:
            cp.wait_send()
        keep.wait()
        own_small.wait()

    any_spec = pl.BlockSpec(memory_space=pl.ANY)
    dma = pltpu.SemaphoreType.DMA
    _, out, gath = pl.pallas_call(
        body, name=name,
        in_specs=[any_spec, any_spec], out_specs=[any_spec] * 3,
        out_shape=[_sds((4, R, C), BF16), _sds((4, R, C), BF16), _sds((N_DEV, rows, D), F32)],
        scratch_shapes=[pltpu.VMEM((4, R, C), BF16)] * 3
                       + [dma((4,)), dma((4,)), dma((3,)), dma((3,)), dma((N_DEV - 1,)), dma((N_DEV - 1,)), dma((10,))],
        compiler_params=pltpu.CompilerParams(vmem_limit_bytes=VMEM_LIMIT),
    )(blk, small)
    return out, gath


def _exchange_rows(x_ref, out_ref, send_sems, recv_sems, between=None):
    mx, my, mc = _me()
    me = 4 * mx + 2 * my + mc
    out_ref[pl.ds(me, 1)] = x_ref[...][None]
    sends = []
    for k in range(1, N_DEV):
        cp = pltpu.make_async_remote_copy(
            src_ref=x_ref, dst_ref=out_ref.at[me], send_sem=send_sems.at[k - 1], recv_sem=recv_sems.at[k - 1],
            device_id=_peer(mx, my, mc, k), device_id_type=MESH)
        cp.start()
        sends.append(cp)
    if between is not None:
        between()
    for k in range(1, N_DEV):
        pltpu.make_async_remote_copy(
            src_ref=x_ref, dst_ref=out_ref.at[me ^ k], send_sem=send_sems.at[k - 1], recv_sem=recv_sems.at[k - 1],
            device_id=(mx, my, mc), device_id_type=MESH).wait_recv()
    for cp in sends:
        cp.wait_send()


def _sc_gather(x, name):
    hbm = pltpu.MemorySpace.HBM
    x_ref = jax.new_ref(x, memory_space=hbm)
    got_ref = jax.empty_ref(jax.ShapeDtypeStruct((N_DEV,) + x.shape, x.dtype), memory_space=hbm)
    dma = pltpu.SemaphoreType.DMA

    @pl.kernel(mesh=plsc.ScalarSubcoreMesh(axis_name="seq", num_cores=1), name=name,
               scratch_types=(dma,) * 15, compiler_params=pltpu.CompilerParams(collective_id=4))
    def launch(*sems):
        send, recv, own_sem = sems[:7], sems[7:14], sems[14]
        mx, my, mc = _me()
        me = 4 * mx + 2 * my + mc
        here, sibling = (mx, my, mc), (mx, my, 1 - mc)
        chips = [(1 - mx, my), (mx, 1 - my), (1 - mx, 1 - my)]
        barrier = pltpu.get_barrier_semaphore()
        for peer in [sibling] + [(cx, cy, mc) for cx, cy in chips]:
            pl.semaphore_signal(barrier, inc=1, device_id=peer, device_id_type=MESH)
        pl.semaphore_wait(barrier, 4)

        def rc(k, src, dst, to):
            return pltpu.make_async_remote_copy(src_ref=src, dst_ref=dst, send_sem=send[k], recv_sem=recv[k],
                                                device_id=to, device_id_type=MESH)

        own = pltpu.make_async_copy(x_ref, got_ref.at[me], own_sem)
        own.start()
        first = [rc(0, x_ref, got_ref.at[me], sibling)] + [rc(1 + j, x_ref, got_ref.at[me], (cx, cy, mc))
                                                             for j, (cx, cy) in enumerate(chips)]
        for cp in first:
            cp.start()
        onward = []
        for j, (cx, cy) in enumerate(chips):
            landed = got_ref.at[4 * cx + 2 * cy + mc]
            rc(1 + j, x_ref, landed, here).wait_recv()
            cp = rc(4 + j, landed, landed, sibling)
            cp.start()
            onward.append(cp)
        rc(0, x_ref, got_ref.at[me ^ 1], here).wait_recv()
        for j, (cx, cy) in enumerate(chips):
            rc(4 + j, x_ref, got_ref.at[4 * cx + 2 * cy + 1 - mc], here).wait_recv()
        for cp in first + onward:
            cp.wait_send()
        own.wait()

    launch()
    return got_ref[...]


def _prologue(c8, cctx8, w_ada, b_my):
    nw = w_ada.shape[1]

    def body(c_ref, cctx_ref, w_ref, b_ref, act_ref, mod_ref, cmine_scr, call_scr, mine_scr, mall_scr, s1, r1, s2, r2):
        cmine_scr[...] = c_ref[...]
        _exchange_rows(cmine_scr, call_scr, s1, r1)
        rows = [call_scr[d][0:1, :] for d in range(N_DEV)] + [cctx_ref[0:1, :], jnp.zeros((7, D), F32)]
        s = jnp.concatenate(rows, axis=0)
        act = s * _sigmoid(s)
        act_ref[...] = act
        mine_scr[...] = jnp.dot(act.astype(BF16), w_ref[...].astype(BF16), preferred_element_type=F32) + b_ref[...]
        _exchange_rows(mine_scr, mall_scr, s2, r2)
        mod_ref[...] = mall_scr[...]

    sems = [pltpu.SemaphoreType.DMA((N_DEV - 1,))] * 4
    return pl.pallas_call(
        body, name="prologue", grid=(1,),
        in_specs=[_full((8, D)), _full((8, D)), _full((D, nw)), _full((1, nw))],
        out_specs=[_full((16, D)), _full((N_DEV, 16, nw))],
        out_shape=[_sds((16, D), F32), _sds((N_DEV, 16, nw), F32)],
        scratch_shapes=[pltpu.VMEM((8, D), F32), pltpu.VMEM((N_DEV, 8, D), F32), pltpu.VMEM((16, nw), F32),
                        pltpu.VMEM((N_DEV, 16, nw), F32)] + sems,
        compiler_params=_params(("arbitrary",)),
    )(c8, cctx8, w_ada, b_my)


def _gather_rows(x, name):
    def body(x_ref, out_ref, send_sems, recv_sems):
        _exchange_rows(x_ref, out_ref, send_sems, recv_sems)

    return pl.pallas_call(
        body, name=name,
        out_shape=_sds((N_DEV,) + x.shape, x.dtype),
        in_specs=[pl.BlockSpec(memory_space=pltpu.VMEM)],
        out_specs=pl.BlockSpec(memory_space=pltpu.VMEM),
        scratch_shapes=[pltpu.SemaphoreType.DMA((N_DEV - 1,)), pltpu.SemaphoreType.DMA((N_DEV - 1,))],
        compiler_params=pltpu.CompilerParams(vmem_limit_bytes=VMEM_LIMIT),
    )(x)


def _ada_bwd(act, dmod_my, w_ada, m, v, tr=256):
    nw = w_ada.shape[1]

    def body(act_ref, dm_ref, w_ref, m_ref, v_ref, g_ref, d_ref, m2_ref, v2_ref, pc_ref):
        dm = dm_ref[...].astype(BF16)
        g = lax.dot_general(act_ref[...].astype(BF16), dm, TN, preferred_element_type=F32)
        w = w_ref[...]
        delta, m2, v2 = _adamw(w, g, m_ref[...], v_ref[...])
        g_ref[...] = g
        d_ref[...] = delta
        m2_ref[...] = m2
        v2_ref[...] = v2
        pc_ref[...] = lax.dot_general(dm[8:16, :], w.astype(BF16), NT, preferred_element_type=F32)

    wspec = _row(tr, nw)
    return pl.pallas_call(
        body, name="ada_bwd", grid=(D // tr,),
        in_specs=[pl.BlockSpec((16, tr), lambda i: (0, i)), _full((16, nw)), wspec, wspec, wspec],
        out_specs=[wspec, wspec, wspec, wspec, pl.BlockSpec((8, tr), lambda i: (0, i))],
        out_shape=[_sds((D, nw), F32)] * 4 + [_sds((8, D), F32)],
        compiler_params=_params(("arbitrary",)),
    )(act, dmod_my, w_ada, m, v)


def _k_in(x, modv, w_in, cos, sin, tm, comm=None):
    L = x.shape[0]

    def body(x_ref, mod_ref, w_ref, cos_ref, sin_ref, h_ref, q_ref, k_ref, v_ref, u_ref, vb_ref, ga_ref, gb_ref):
        n, _ = _ln(x_ref[...])
        h = (n * (1.0 + mod_ref[1:2, :]) + mod_ref[0:1, :]).astype(BF16)
        h_ref[...] = h
        c, s = cos_ref[...], sin_ref[...]

        def proj(lo, width):
            return lax.dot_general(h, w_ref[lo:lo + width, :], NT, preferred_element_type=F32)

        for i in range(2):
            qh = proj(O_Q + i * 256, 256)
            for j in range(2):
                q_ref[:, i * 256 + j * 128:i * 256 + (j + 1) * 128] = (
                    _rope(qh[:, j * 128:(j + 1) * 128], c, s) * Q_SCALE).astype(BF16)
        kv = proj(O_K, 2 * KV_W)
        k_ref[...] = _rope(kv[:, :KV_W], c, s).astype(BF16)
        v_ref[...] = kv[:, KV_W:].astype(BF16)
        u_ref[...] = proj(O_U, GM_W).astype(BF16)
        vb_ref[...] = proj(O_VB, GM_W).astype(BF16)
        ga_ref[...] = proj(O_GA, D).astype(BF16)
        gb_ref[...] = proj(O_GB, D).astype(BF16)

    widths = [D, Q_W, KV_W, KV_W, GM_W, GM_W, D, D]
    return _call(
        body, name="fwd_in", grid=(L // tm,),
        in_specs=[_row(tm, D), _full((8, D)), _resident((IN_W, D)), _row(tm, 128), _row(tm, 128)],
        out_specs=[_row(tm, w) for w in widths],
        out_shape=[_sds((L, w), BF16) for w in widths],
        args=(x, modv, w_in, cos, sin), comm=comm)


def _k_ctx(ctx, modc, w_kv):
    C = ctx.shape[0]

    def body(c_ref, mod_ref, w_ref, hc_ref, kc_ref, vc_ref):
        n, _ = _ln(c_ref[...])
        hc = (n * (1.0 + mod_ref[1:2, :]) + mod_ref[0:1, :]).astype(BF16)
        hc_ref[...] = hc
        kv = lax.dot_general(hc, w_ref[...], NT, preferred_element_type=F32)
        kc_ref[...] = kv[:, :KV_W].astype(BF16)
        vc_ref[...] = kv[:, KV_W:].astype(BF16)

    return pl.pallas_call(
        body, name="fwd_ctx", grid=(1,),
        in_specs=[_full((C, D)), _full((8, D)), _full((2 * KV_W, D))],
        out_specs=[_full((C, D)), _full((C, KV_W)), _full((C, KV_W))],
        out_shape=[_sds((C, D), BF16), _sds((C, KV_W), BF16), _sds((C, KV_W), BF16)],
        compiler_params=_params(("arbitrary",)),
    )(ctx, modc, w_kv)


def _attn_bias():
    r = (np.arange(GQA_GROUP * BLK) & (BLK - 1))[:, None]
    j = np.arange(3 * BLK)[None, :]
    band = np.abs(j - BLK - r) <= BLK
    variants = [band & (j >= BLK), band, band & (j < 2 * BLK)]
    return jnp.asarray(np.stack([np.where(v, 0.0, NEG_INF) for v in variants]), F32)


def _masked(s, bias, C):
    return jnp.concatenate([s[:, :C], s[:, C:] + bias], axis=1)


def _sink_col(sink_ref, hk):
    grp = lax.broadcasted_iota(jnp.int32, (GQA_GROUP * BLK, 1), 0) >> 7
    col = jnp.full((GQA_GROUP * BLK, 1), sink_ref[hk * GQA_GROUP], F32)
    for g in range(1, GQA_GROUP):
        col = jnp.where(grp == g, sink_ref[hk * GQA_GROUP + g], col)
    return col


ATTN_FWD_BLOCKS = 4


def _k_attn(sink, q, k, v, kc, vc, bias, comm=None):
    L = q.shape[0]
    C = kc.shape[0]
    nb = L // BLK
    nq = min(ATTN_FWD_BLOCKS, nb)
    steps = nb // nq

    def body(sink_ref, q_ref, kp_ref, km_ref, kx_ref, vp_ref, vm_ref, vx_ref, kc_ref, vc_ref, bias_ref, ya_ref, lse_ref):
        i = pl.program_id(0)
        chains = [(qb, hk) for qb in range(nq) for hk in range(N_KV_HEADS)]

        def band(qb):
            first = jnp.where(i == 0, 0, 1) if qb == 0 else 1
            return bias_ref[jnp.where(i == steps - 1, 2, first) if qb == nq - 1 else first]

        def keys(ctx_ref, p_ref, m_ref, x_ref, qb, hk):
            sl = slice(hk * HEAD_DIM, (hk + 1) * HEAD_DIM)
            blocks = [p_ref[:, sl]] + [m_ref[j * BLK:(j + 1) * BLK, sl] for j in range(nq)] + [x_ref[:, sl]]
            return jnp.concatenate([ctx_ref[:, sl]] + blocks[qb:qb + 3], axis=0)

        def queries(qb, hk):
            return jnp.concatenate(
                [q_ref[qb * BLK:(qb + 1) * BLK, (hk * GQA_GROUP + g) * HEAD_DIM:(hk * GQA_GROUP + g + 1) * HEAD_DIM]
                 for g in range(GQA_GROUP)], axis=0)

        def scores(qb, hk):
            return _masked(lax.dot_general(queries(qb, hk), keys(kc_ref, kp_ref, km_ref, kx_ref, qb, hk), NT,
                                           preferred_element_type=F32), band(qb), C)

        ahead = 2
        s = [scores(*c) for c in chains[:ahead]]
        for n, (qb, hk) in enumerate(chains):
            if n + ahead < len(chains):
                s.append(scores(*chains[n + ahead]))
            s_ = s[n]
            sink_c = _sink_col(sink_ref, hk)
            m = jnp.maximum(jnp.max(s_, axis=1, keepdims=True), sink_c)
            p = jnp.exp(s_ - m)
            den = jnp.sum(p, axis=1, keepdims=True) + jnp.exp(sink_c - m)
            o = jnp.dot(p.astype(BF16), keys(vc_ref, vp_ref, vm_ref, vx_ref, qb, hk), preferred_element_type=F32) * (1.0 / den)
            lse = m + jnp.log(den)
            rows = slice(qb * BLK, (qb + 1) * BLK)
            for g in range(GQA_GROUP):
                h = hk * GQA_GROUP + g
                ya_ref[rows, h * HEAD_DIM:(h + 1) * HEAD_DIM] = o[g * BLK:(g + 1) * BLK, :].astype(BF16)
                lse_ref[rows, h:h + 1] = lse[g * BLK:(g + 1) * BLK, :]

    kv3 = [pl.BlockSpec((BLK, KV_W), lambda i: (jnp.maximum(nq * i - 1, 0), 0)),
           pl.BlockSpec((nq * BLK, KV_W), lambda i: (i, 0)),
           pl.BlockSpec((BLK, KV_W), lambda i: (jnp.minimum(nq * i + nq, nb - 1), 0))]
    return _call(
        body, name="fwd_attn", grid=(steps,),
        in_specs=[pl.BlockSpec(memory_space=pltpu.SMEM), _row(nq * BLK, Q_W)] + kv3 + kv3
                 + [_full((C, KV_W)), _full((C, KV_W)), _full((3, GQA_GROUP * BLK, 3 * BLK))],
        out_specs=[_row(nq * BLK, Q_W), _row(nq * BLK, N_Q_HEADS)],
        out_shape=[_sds((L, Q_W), BF16), _sds((L, N_Q_HEADS), F32)],
        args=(sink, q, k, k, k, v, v, v, kc, vc, bias), comm=comm)


GMLP_CHUNKS = 4


def _split_pair(t):
    low = lax.broadcasted_iota(jnp.int32, t.shape, 1) < GROUP_DIM
    zero = jnp.zeros_like(t)
    return jnp.where(low, t, zero), jnp.where(low, zero, t)


def _gmlp_spatial(w_ref, t_b, nch):
    rows = []
    for c in range(nch):
        tiles = []
        for pr in range(N_GROUPS // 2):
            lo, hi = _split_pair(t_b[c * BLK:(c + 1) * BLK, pr * 128:(pr + 1) * 128])
            tiles.append(jnp.dot(w_ref[2 * pr], lo, preferred_element_type=F32)
                         + jnp.dot(w_ref[2 * pr + 1], hi, preferred_element_type=F32))
        rows.append(jnp.concatenate(tiles, axis=1))
    return jnp.concatenate(rows, axis=0)


def _gmlp_fwd_vals(u, vb, lnv_ref, ws_ref, bsp_ref, nch):
    uf = u.astype(F32)
    vf = vb.astype(F32)
    gu, tu = _gelu(uf)
    gv, tv = _gelu(vf)
    vhat, rstd = _ln(gv)
    vn = (vhat * lnv_ref[0:1, :] + lnv_ref[1:2, :]).astype(BF16)
    s = _gmlp_spatial(ws_ref, vn, nch) + jnp.concatenate([bsp_ref[...]] * nch, axis=0)
    return uf, vf, gu, tu, tv, vhat, rstd, vn, s


def _k_gmlp(u, vb, lnv, ws, bsp):
    L = u.shape[0]
    nch = min(GMLP_CHUNKS, L // BLK)
    tm = nch * BLK

    def body(u_ref, vb_ref, lnv_ref, ws_ref, bsp_ref, yb_ref):
        _, _, gu, _, _, _, _, _, s = _gmlp_fwd_vals(u_ref[...], vb_ref[...], lnv_ref, ws_ref, bsp_ref, nch)
        yb_ref[...] = (gu * s).astype(BF16)

    return pl.pallas_call(
        body, name="fwd_gmlp", grid=(L // tm,),
        in_specs=[_row(tm, GM_W), _row(tm, GM_W), _full((8, GM_W)), _full((N_GROUPS, BLK, BLK)), _full((BLK, GM_W))],
        out_specs=_row(tm, GM_W),
        out_shape=_sds((L, GM_W), BF16),
        compiler_params=_params(("arbitrary",)),
    )(u, vb, lnv, ws, bsp)


def _k_merge(x, ya, yb, ga, gb, w_a, w_b, w_o, modv, lnv, tm):
    L = x.shape[0]

    def body(x_ref, ya_ref, yb_ref, ga_ref, gb_ref, wa_ref, wb_ref, wo_ref, mod_ref, ln_ref,
             mg_ref, mix_ref, xm_ref, h2_ref):
        a = jnp.dot(ya_ref[...], wa_ref[...], preferred_element_type=F32)
        b = jnp.dot(yb_ref[...], wb_ref[...], preferred_element_type=F32)
        merged = (_sigmoid(ga_ref[...].astype(F32)) * a + _sigmoid(gb_ref[...].astype(F32)) * b).astype(BF16)
        mg_ref[...] = merged
        mix = jnp.dot(merged, wo_ref[...], preferred_element_type=F32)
        mix_ref[...] = mix.astype(BF16)
        r1 = ALPHA * x_ref[...] + mod_ref[2:3, :] * mix
        r1hat, _ = _ln(r1)
        xm = r1hat * ln_ref[0:1, :] + ln_ref[1:2, :]
        xm_ref[...] = xm
        n2, _ = _ln(xm)
        h2_ref[...] = (n2 * (1.0 + mod_ref[4:5, :]) + mod_ref[3:4, :]).astype(BF16)

    return pl.pallas_call(
        body, name="fwd_merge", grid=(L // tm,),
        in_specs=[_row(tm, D), _row(tm, Q_W), _row(tm, GM_W), _row(tm, D), _row(tm, D),
                  _resident((Q_W, D)), _resident((GM_W, D)), _resident((D, D)), _full((8, D)), _full((8, D))],
        out_specs=[_row(tm, D)] * 4,
        out_shape=[_sds((L, D), BF16), _sds((L, D), BF16), _sds((L, D), F32), _sds((L, D), BF16)],
        compiler_params=_params(("arbitrary",)),
    )(x, ya, yb, ga, gb, w_a, w_b, w_o, modv, lnv)


V7X_MXU_COLUMNS = 256
FFN_CHUNK = V7X_MXU_COLUMNS


def _k_ffn(h2, xm, tgt, w_fi, w_fo, modv, lnv, tm):
    L = h2.shape[0]

    def body(h2_ref, xm_ref, t_ref, wi_ref, wo_ref, mod_ref, ln_ref, gate_ref, up_ref, a_ref, dr2_ref, df_ref, acc_ref):
        @pl.when(pl.program_id(0) == 0)
        def _():
            acc_ref[...] = jnp.zeros_like(acc_ref)

        h2v = h2_ref[...]
        ch = FFN_CHUNK
        chunks = [j * ch for j in range(FFN_H // ch)]

        def project(lo):
            return (lax.dot_general(h2v, wi_ref[lo:lo + ch, :], NT, preferred_element_type=F32),
                    lax.dot_general(h2v, wi_ref[FFN_H + lo:FFN_H + lo + ch, :], NT, preferred_element_type=F32))

        f = jnp.zeros((tm, D), F32)
        ahead = [project(chunks[0])]
        for j, lo in enumerate(chunks):
            if j + 1 < len(chunks):
                ahead.append(project(chunks[j + 1]))
            gate, up = ahead[j]
            act = (gate * _sigmoid(gate) * up).astype(BF16)
            gate_ref[:, lo:lo + ch] = gate.astype(BF16)
            up_ref[:, lo:lo + ch] = up.astype(BF16)
            a_ref[:, lo:lo + ch] = act
            f = f + jnp.dot(act, wo_ref[lo:lo + ch, :], preferred_element_type=F32)
        gate2 = mod_ref[5:6, :]
        r2 = ALPHA * xm_ref[...] + gate2 * f
        r2hat, rstd = _ln(r2)
        y = r2hat * ln_ref[2:3, :] + ln_ref[3:4, :]
        err = y - t_ref[...]
        dy = err * (1.0 / D)
        dr2 = _ln_bwd(dy * ln_ref[2:3, :], r2hat, rstd)
        dr2_ref[...] = dr2
        df_ref[...] = (gate2 * dr2).astype(BF16)
        acc_ref[0:1, :] += _colsum(dy * r2hat)
        acc_ref[1:2, :] += _colsum(dy)
        acc_ref[2:3, :] += _colsum(dr2 * f)
        acc_ref[3:4, :] += _colsum(err * err) * (0.5 / D)

    return pl.pallas_call(
        body, name="fwd_ffn", grid=(L // tm,),
        in_specs=[_row(tm, D), _row(tm, D), _row(tm, D), _resident((2 * FFN_H, D)), _resident((FFN_H, D)),
                  _full((8, D)), _full((8, D))],
        out_specs=[_row(tm, FFN_H)] * 3 + [_row(tm, D), _row(tm, D), _full((8, D))],
        out_shape=[_sds((L, FFN_H), BF16)] * 3 + [_sds((L, D), F32), _sds((L, D), BF16), _sds((8, D), F32)],
        compiler_params=_params(("arbitrary",)),
    )(h2, xm, tgt, w_fi, w_fo, modv, lnv)


def _k_ffn_bwd(df, gate, up, xm, dr2, x, mix, w_fi, w_fo, modv, lnv, tm):
    L = df.shape[0]

    def body(df_ref, gate_ref, up_ref, xm_ref, dr2_ref, x_ref, mix_ref, wi_ref, wo_ref, mod_ref, ln_ref,
             dF_ref, dmix_ref, dxp_ref, acc_ref):
        @pl.when(pl.program_id(0) == 0)
        def _():
            acc_ref[...] = jnp.zeros_like(acc_ref)

        dfv = df_ref[...]
        ch = FFN_CHUNK
        chunks = [j * ch for j in range(FFN_H // ch)]

        def d_act(lo):
            return lax.dot_general(dfv, wo_ref[lo:lo + ch, :], NT, preferred_element_type=F32)

        n2, rstd2 = _ln(xm_ref[...])
        mixf = mix_ref[...].astype(F32)
        gate1 = mod_ref[2:3, :]
        r1hat, rstd1 = _ln(ALPHA * x_ref[...] + gate1 * mixf)
        dh2 = jnp.zeros((tm, D), F32)
        das = [d_act(chunks[0])]
        for j, lo in enumerate(chunks):
            if j + 1 < len(chunks):
                das.append(d_act(chunks[j + 1]))
            da = das[j]
            gate = gate_ref[:, lo:lo + ch].astype(F32)
            upv = up_ref[:, lo:lo + ch].astype(F32)
            sg = _sigmoid(gate)
            d_gate = (da * upv * (sg * (1.0 + gate * (1.0 - sg)))).astype(BF16)
            d_up = (da * (gate * sg)).astype(BF16)
            dF_ref[:, lo:lo + ch] = d_gate
            dF_ref[:, FFN_H + lo:FFN_H + lo + ch] = d_up
            dh2 = dh2 + jnp.dot(d_gate, wi_ref[lo:lo + ch, :], preferred_element_type=F32)
            dh2 = dh2 + jnp.dot(d_up, wi_ref[FFN_H + lo:FFN_H + lo + ch, :], preferred_element_type=F32)
        acc_ref[0:1, :] += _colsum(dh2)
        acc_ref[1:2, :] += _colsum(dh2 * n2)
        dxm = ALPHA * dr2_ref[...] + _ln_bwd(dh2 * (1.0 + mod_ref[4:5, :]), n2, rstd2)
        acc_ref[2:3, :] += _colsum(dxm * r1hat)
        acc_ref[3:4, :] += _colsum(dxm)
        dr1 = _ln_bwd(dxm * ln_ref[0:1, :], r1hat, rstd1)
        dmix_ref[...] = (gate1 * dr1).astype(BF16)
        dxp_ref[...] = ALPHA * dr1
        acc_ref[4:5, :] += _colsum(dr1 * mixf)

    return pl.pallas_call(
        body, name="bwd_ffn", grid=(L // tm,),
        in_specs=[_row(tm, D), _row(tm, FFN_H), _row(tm, FFN_H), _row(tm, D), _row(tm, D), _row(tm, D), _row(tm, D),
                  _resident((2 * FFN_H, D)), _resident((FFN_H, D)), _full((8, D)), _full((8, D))],
        out_specs=[_row(tm, 2 * FFN_H), _row(tm, D), _row(tm, D), _full((8, D))],
        out_shape=[_sds((L, 2 * FFN_H), BF16), _sds((L, D), BF16), _sds((L, D), F32), _sds((8, D), F32)],
        compiler_params=_params(("arbitrary",)),
    )(df, gate, up, xm, dr2, x, mix, w_fi, w_fo, modv, lnv)


def _k_merge_bwd(dmix, merged, ya, yb, ga, gb, w_a, w_b, w_o, tm):
    L = dmix.shape[0]
    n = L // tm

    def body(dmix_ref, mg_ref, ya_ref, yb_ref, ga_ref, gb_ref, wa_ref, wb_ref, wo_ref,
             dga_ref, dgb_ref, dya_ref, dyb_ref, gwa_ref, gwb_ref, gwo_ref, acc_a, acc_b, acc_o):
        i = pl.program_id(0)

        @pl.when(i == 0)
        def _():
            for r in (acc_a, acc_b, acc_o):
                r[...] = jnp.zeros_like(r)

        dmixv = dmix_ref[...]
        dmg = lax.dot_general(dmixv, wo_ref[...], NT, preferred_element_type=F32)
        acc_o[...] += lax.dot_general(mg_ref[...], dmixv, TN, preferred_element_type=F32)
        ya = ya_ref[...]
        a = jnp.dot(ya, wa_ref[...], preferred_element_type=F32)
        sa = _sigmoid(ga_ref[...].astype(F32))
        dA = (dmg * sa).astype(BF16)
        dga_ref[...] = (dmg * a * (sa * (1.0 - sa))).astype(BF16)
        dya_ref[...] = lax.dot_general(dA, wa_ref[...], NT, preferred_element_type=F32).astype(BF16)
        acc_a[...] += lax.dot_general(ya, dA, TN, preferred_element_type=F32)
        yb = yb_ref[...]
        b = jnp.dot(yb, wb_ref[...], preferred_element_type=F32)
        sb = _sigmoid(gb_ref[...].astype(F32))
        dB = (dmg * sb).astype(BF16)
        dgb_ref[...] = (dmg * b * (sb * (1.0 - sb))).astype(BF16)
        dyb_ref[...] = lax.dot_general(dB, wb_ref[...], NT, preferred_element_type=F32).astype(BF16)
        acc_b[...] += lax.dot_general(yb, dB, TN, preferred_element_type=F32)

        @pl.when(i == n - 1)
        def _():
            gwa_ref[...] = acc_a[...].astype(BF16)
            gwb_ref[...] = acc_b[...].astype(BF16)
            gwo_ref[...] = acc_o[...].astype(BF16)

    return pl.pallas_call(
        body, name="bwd_merge", grid=(n,),
        in_specs=[_row(tm, D), _row(tm, D), _row(tm, Q_W), _row(tm, GM_W), _row(tm, D), _row(tm, D),
                  _resident((Q_W, D)), _resident((GM_W, D)), _resident((D, D))],
        out_specs=[_row(tm, D), _row(tm, D), _row(tm, Q_W), _row(tm, GM_W), _full((Q_W, D)), _full((GM_W, D)), _full((D, D))],
        out_shape=[_sds((L, D), BF16), _sds((L, D), BF16), _sds((L, Q_W), BF16), _sds((L, GM_W), BF16),
                   _sds((Q_W, D), BF16), _sds((GM_W, D), BF16), _sds((D, D), BF16)],
        scratch_shapes=[pltpu.VMEM((Q_W, D), F32), pltpu.VMEM((GM_W, D), F32), pltpu.VMEM((D, D), F32)],
        compiler_params=_params(("arbitrary",)),
    )(dmix, merged, ya, yb, ga, gb, w_a, w_b, w_o)


def _k_gmlp_bwd(u, vb, dyb, lnv, ws, wst, bsp):
    L = u.shape[0]
    nch = min(GMLP_CHUNKS, L // BLK)
    tm = nch * BLK

    def body(u_ref, vb_ref, dyb_ref, lnv_ref, ws_ref, wst_ref, bsp_ref, du_ref, dvb_ref, gws_ref, gbst_ref, gln_ref):
        @pl.when(pl.program_id(0) == 0)
        def _():
            gws_ref[...] = jnp.zeros_like(gws_ref)
            gbst_ref[...] = jnp.zeros_like(gbst_ref)
            gln_ref[...] = jnp.zeros_like(gln_ref)

        uf, vf, gu, tu, tv, vhat, rstd, vn, s = _gmlp_fwd_vals(u_ref[...], vb_ref[...], lnv_ref, ws_ref, bsp_ref, nch)
        dyb_f = dyb_ref[...].astype(F32)
        du_ref[...] = (dyb_f * s * _gelu_grad(uf, tu)).astype(BF16)
        ds = dyb_f * gu
        ds_b = ds.astype(BF16)
        for pr in range(N_GROUPS // 2):
            lanes = slice(pr * 128, (pr + 1) * 128)
            gw_lo = gw_hi = ds_sum = None
            for c in range(nch):
                rows = slice(c * BLK, (c + 1) * BLK)
                lo, hi = _split_pair(ds_b[rows, lanes])
                t_lo = lax.dot_general(lo, vn[rows, lanes], NT, preferred_element_type=F32)
                t_hi = lax.dot_general(hi, vn[rows, lanes], NT, preferred_element_type=F32)
                gw_lo = t_lo if c == 0 else gw_lo + t_lo
                gw_hi = t_hi if c == 0 else gw_hi + t_hi
                ds_sum = ds[rows, lanes] if c == 0 else ds_sum + ds[rows, lanes]
            gws_ref[2 * pr] += gw_lo
            gws_ref[2 * pr + 1] += gw_hi
            b_lo, b_hi = _split_pair(ds_sum)
            gbst_ref[:, 2 * pr:2 * pr + 1] += jnp.sum(b_lo, axis=1, keepdims=True)
            gbst_ref[:, 2 * pr + 1:2 * pr + 2] += jnp.sum(b_hi, axis=1, keepdims=True)
        dvn = _gmlp_spatial(wst_ref, ds_b, nch)
        gln_ref[0:1, :] += _colsum(dvn * vhat)
        gln_ref[1:2, :] += _colsum(dvn)
        dgv = _ln_bwd(dvn * lnv_ref[0:1, :], vhat, rstd)
        dvb_ref[...] = (dgv * _gelu_grad(vf, tv)).astype(BF16)

    return pl.pallas_call(
        body, name="bwd_gmlp", grid=(L // tm,),
        in_specs=[_row(tm, GM_W)] * 3 + [_full((8, GM_W)), _full((N_GROUPS, BLK, BLK)), _full((N_GROUPS, BLK, BLK)),
                                         _full((BLK, GM_W))],
        out_specs=[_row(tm, GM_W), _row(tm, GM_W), _full((N_GROUPS, BLK, BLK)), _full((BLK, N_GROUPS)), _full((8, GM_W))],
        out_shape=[_sds((L, GM_W), BF16), _sds((L, GM_W), BF16), _sds((N_GROUPS, BLK, BLK), F32),
                   _sds((BLK, N_GROUPS), F32), _sds((8, GM_W), F32)],
        compiler_params=_params(("arbitrary",)),
    )(u, vb, dyb, lnv, ws, wst, bsp)


ATTN_BWD_BLOCKS = 2


def _k_attn_bwd(sink, q, k, v, kc, vc, dya, lse, cos, sin, bias, comm=None):
    L = q.shape[0]
    C = kc.shape[0]
    nb = L // BLK
    nq = min(ATTN_BWD_BLOCKS, nb)
    steps = nb // nq
    NK = C + 3 * BLK
    chains = [(qb, hk) for qb in range(nq) for hk in range(N_KV_HEADS)]

    def body(sink_ref, q_ref, kp_ref, km_ref, kx_ref, vp_ref, vm_ref, vx_ref, kc_ref, vc_ref, do_ref, lse_ref,
             cq_ref, sq_ref, cl_ref, sl_ref, bias_ref,
             dq_ref, dk_ref, dv_ref, dkc_ref, dvc_ref, dsink_ref,
             dq_scr, ck_scr, cv_scr, k1_acc, k2_acc, v1_acc, v2_acc):
        i = pl.program_id(0)

        @pl.when(i == 0)
        def _():
            for r in (k1_acc, k2_acc, v1_acc, v2_acc, dkc_ref, dvc_ref, dsink_ref):
                r[...] = jnp.zeros_like(r)

        @pl.when(i < steps)
        def _():
            def band(qb):
                first = jnp.where(i == 0, 0, 1) if qb == 0 else 1
                return bias_ref[jnp.where(i == steps - 1, 2, first) if qb == nq - 1 else first]

            def lanes(hk):
                return slice(hk * HEAD_DIM, (hk + 1) * HEAD_DIM)

            def keys(ctx_ref, p_ref, m_ref, x_ref, qb, hk):
                sl = lanes(hk)
                blocks = [p_ref[:, sl]] + [m_ref[j * BLK:(j + 1) * BLK, sl] for j in range(nq)] + [x_ref[:, sl]]
                return jnp.concatenate([ctx_ref[:, sl]] + blocks[qb:qb + 3], axis=0)

            def stacked(ref, qb, hk, width):
                return jnp.concatenate(
                    [ref[qb * BLK:(qb + 1) * BLK, (hk * GQA_GROUP + g) * width:(hk * GQA_GROUP + g + 1) * width]
                     for g in range(GQA_GROUP)], axis=0)

            def scores(qb, hk):
                kcat = keys(kc_ref, kp_ref, km_ref, kx_ref, qb, hk)
                qg = stacked(q_ref, qb, hk, HEAD_DIM)
                s = _masked(lax.dot_general(qg, kcat, NT, preferred_element_type=F32), band(qb), C)
                dog = stacked(do_ref, qb, hk, HEAD_DIM)
                dp = lax.dot_general(dog, keys(vc_ref, vp_ref, vm_ref, vx_ref, qb, hk), NT, preferred_element_type=F32)
                return kcat, qg, dog, s, dp

            def softmax_bwd(qb, hk, s, dp):
                lse_c = stacked(lse_ref, qb, hk, 1)
                p = jnp.exp(s - lse_c)
                delta = jnp.sum(p * dp, axis=1, keepdims=True)
                ds = (p * (dp - delta)).astype(BF16)
                p_sink = jnp.exp(_sink_col(sink_ref, hk) - lse_c) * delta
                return p.astype(BF16), ds, p_sink

            def put_dq(qb, hk, dqs, p_sink):
                for g in range(GQA_GROUP):
                    h = hk * GQA_GROUP + g
                    dq_scr[qb * BLK:(qb + 1) * BLK, h * HEAD_DIM:(h + 1) * HEAD_DIM] = dqs[g * BLK:(g + 1) * BLK, :]
                    tot = jnp.sum(p_sink[g * BLK:(g + 1) * BLK, :], axis=0, keepdims=True)
                    dsink_ref[h:h + 1, :] -= jnp.broadcast_to(tot, (1, 128))

            ahead = 4
            sc = [scores(*c) for c in chains[:ahead]]
            pending = None
            for n, (qb, hk) in enumerate(chains):
                if n + ahead < len(chains):
                    sc.append(scores(*chains[n + ahead]))
                kcat, qg, dog, s, dp = sc[n]
                pb, ds, p_sink = softmax_bwd(qb, hk, s, dp)
                if pending is not None:
                    pqb, phk, pds, ppb, pqg, pdog = pending
                    ck_scr[pqb, :, lanes(phk)] = lax.dot_general(pds, pqg, TN, preferred_element_type=F32)
                    cv_scr[pqb, :, lanes(phk)] = lax.dot_general(ppb, pdog, TN, preferred_element_type=F32)
                put_dq(qb, hk, jnp.dot(ds, kcat, preferred_element_type=F32), p_sink)
                pending = (qb, hk, ds, pb, qg, dog)
            pqb, phk, pds, ppb, pqg, pdog = pending
            ck_scr[pqb, :, lanes(phk)] = lax.dot_general(pds, pqg, TN, preferred_element_type=F32)
            cq, sq = cq_ref[...], sq_ref[...]
            for j in range(4):
                dq_ref[:, j * 128:(j + 1) * 128] = _unrope(dq_scr[:, j * 128:(j + 1) * 128] * Q_SCALE, cq, sq).astype(BF16)
            cv_scr[pqb, :, lanes(phk)] = lax.dot_general(ppb, pdog, TN, preferred_element_type=F32)
            dkc_ref[...] += functools.reduce(lambda a, b: a + b, [ck_scr[qb, 0:C, :] for qb in range(nq)])
            dvc_ref[...] += functools.reduce(lambda a, b: a + b, [cv_scr[qb, 0:C, :] for qb in range(nq)])

        @pl.when(i >= steps)
        def _():
            ck_scr[...] = jnp.zeros_like(ck_scr)
            cv_scr[...] = jnp.zeros_like(cv_scr)

        def slot(scr, r, carried):
            parts = [scr[qb, C + (r - qb) * BLK:C + (r - qb + 1) * BLK, :] for qb in range(nq) if 0 <= r - qb <= 2]
            total = functools.reduce(lambda a, b: a + b, parts)
            return total if carried is None else carried[...] + total

        for r in range(nq):
            rows = slice(r * BLK, (r + 1) * BLK)
            carried_k, carried_v = ((k1_acc, v1_acc), (k2_acc, v2_acc), (None, None))[min(r, 2)]
            tables = (cl_ref[...], sl_ref[...]) if r == 0 else (cq_ref[(r - 1) * BLK:r * BLK, :], sq_ref[(r - 1) * BLK:r * BLK, :])
            dk_ref[rows, :] = _unrope(slot(ck_scr, r, carried_k), *tables).astype(BF16)
            dv_ref[rows, :] = slot(cv_scr, r, carried_v).astype(BF16)
        k1_acc[...] = slot(ck_scr, nq, None)
        v1_acc[...] = slot(cv_scr, nq, None)
        k2_acc[...] = slot(ck_scr, nq + 1, None)
        v2_acc[...] = slot(cv_scr, nq + 1, None)

    last = steps - 1
    kv3 = [pl.BlockSpec((BLK, KV_W), lambda i: (jnp.clip(nq * i - 1, 0, nb - 1), 0)),
           pl.BlockSpec((nq * BLK, KV_W), lambda i: (jnp.minimum(i, last), 0)),
           pl.BlockSpec((BLK, KV_W), lambda i: (jnp.minimum(nq * i + nq, nb - 1), 0))]
    cur = lambda w: pl.BlockSpec((nq * BLK, w), lambda i: (jnp.minimum(i, last), 0))
    late = lambda w: pl.BlockSpec((BLK, w), lambda i: (jnp.clip(nq * i - 1, 0, nb - 1), 0))
    out2 = lambda w: pl.BlockSpec((nq * BLK, w), lambda i: (i, 0))
    return _call(
        body, name="bwd_attn", grid=(steps + 1,),
        in_specs=[pl.BlockSpec(memory_space=pltpu.SMEM), cur(Q_W)] + kv3 + kv3
                 + [_full((C, KV_W)), _full((C, KV_W)), cur(Q_W), cur(N_Q_HEADS), cur(128), cur(128), late(128), late(128),
                    _full((3, GQA_GROUP * BLK, 3 * BLK))],
        out_specs=[cur(Q_W), out2(KV_W), out2(KV_W), _full((C, KV_W)), _full((C, KV_W)), _full((8, 128))],
        out_shape=[_sds((L, Q_W), BF16), _sds((L + nq * BLK, KV_W), BF16), _sds((L + nq * BLK, KV_W), BF16),
                   _sds((C, KV_W), F32), _sds((C, KV_W), F32), _sds((8, 128), F32)],
        scratch=[pltpu.VMEM((nq * BLK, Q_W), F32), pltpu.VMEM((nq, NK, KV_W), F32), pltpu.VMEM((nq, NK, KV_W), F32)]
                + [pltpu.VMEM((BLK, KV_W), F32)] * 4,
        args=(sink, q, k, k, k, v, v, v, kc, vc, dya, lse, cos, sin, cos, sin, bias), comm=comm)


def _k_ctx_bwd(ctx, modc, hc, dkc, dvc, w_kv):
    C = ctx.shape[0]

    def body(c_ref, mod_ref, hc_ref, dkc_ref, dvc_ref, w_ref, gw_ref, dmod_ref):
        dkv = jnp.concatenate([dkc_ref[...], dvc_ref[...]], axis=1).astype(BF16)
        gw_ref[...] = lax.dot_general(dkv, hc_ref[...], TN, preferred_element_type=F32)
        dhc = jnp.dot(dkv, w_ref[...], preferred_element_type=F32)
        n, _ = _ln(c_ref[...])
        dmod_ref[...] = jnp.zeros_like(dmod_ref)
        dmod_ref[0:1, :] = _colsum(dhc)
        dmod_ref[1:2, :] = _colsum(dhc * n)

    return pl.pallas_call(
        body, name="bwd_ctx", grid=(1,),
        in_specs=[_full((C, D)), _full((8, D)), _full((C, D)), _full((C, KV_W)), _full((C, KV_W)), _full((2 * KV_W, D))],
        out_specs=[_full((2 * KV_W, D)), _full((8, D))],
        out_shape=[_sds((2 * KV_W, D), F32), _sds((8, D), F32)],
        compiler_params=_params(("arbitrary",)),
    )(ctx, modc, hc, dkc, dvc, w_kv)


def _k_in_bwd(dq, dk, dv, du, dvb, dga, dgb, x, dxp, w_in, modv, tm, comm=None):
    L = x.shape[0]
    parts = [(O_Q, Q_W), (O_K, KV_W), (O_V, KV_W), (O_U, GM_W), (O_VB, GM_W), (O_GA, D), (O_GB, D)]

    def body(dq_ref, dk_ref, dv_ref, du_ref, dvb_ref, dga_ref, dgb_ref, x_ref, dxp_ref, w_ref, mod_ref,
             dP_ref, gx_ref, acc_ref):
        @pl.when(pl.program_id(0) == 0)
        def _():
            acc_ref[...] = jnp.zeros_like(acc_ref)

        for (lo, width), r in zip(parts, (dq_ref, dk_ref, dv_ref, du_ref, dvb_ref, dga_ref, dgb_ref)):
            dP_ref[:, lo:lo + width] = r[...]
        n1, rstd1 = _ln(x_ref[...])
        dh = jnp.dot(dP_ref[...], w_ref[...], preferred_element_type=F32)
        acc_ref[0:1, :] += _colsum(dh)
        acc_ref[1:2, :] += _colsum(dh * n1)
        gx_ref[...] = dxp_ref[...] + _ln_bwd(dh * (1.0 + mod_ref[1:2, :]), n1, rstd1)

    return _call(
        body, name="bwd_in", grid=(L // tm,),
        in_specs=[_row(tm, w) for _, w in parts] + [_row(tm, D), _row(tm, D), _resident((IN_W, D)), _full((8, D))],
        out_specs=[_row(tm, IN_W), _row(tm, D), _full((8, D))],
        out_shape=[_sds((L, IN_W), BF16), _sds((L, D), F32), _sds((8, D), F32)],
        args=(dq, dk, dv, du, dvb, dga, dgb, x, dxp, w_in, modv), comm=comm)


def _wgrad(a, b, name, tk, tt, comm=None, extra=None):
    T, K = a.shape
    N = b.shape[1]
    nt = T // tt

    def body(*refs):
        a_ref, b_ref = refs[:2]
        o_ref, acc_ref = refs[-2:]
        j, t = pl.program_id(0), pl.program_id(1)

        @pl.when(t == 0)
        def _():
            acc_ref[...] = jnp.zeros_like(acc_ref)

        acc_ref[...] += lax.dot_general(a_ref[...], b_ref[...], TN, preferred_element_type=F32)

        if extra is not None:
            lo, rows = extra[0] % tk, extra[1].shape[0]

            @pl.when((t == nt - 1) & (j == extra[0] // tk))
            def _():
                acc_ref[lo:lo + rows, :] += refs[2][...]

        @pl.when(t == nt - 1)
        def _():
            o_ref[...] = acc_ref[...].astype(BF16)

    extra_specs = [] if extra is None else [pl.BlockSpec(extra[1].shape, lambda j, t: (0, 0))]
    (out,), got = _call(
        body, name=name, grid=(K // tk, nt),
        in_specs=[pl.BlockSpec((tt, tk), lambda j, t: (t, j)), pl.BlockSpec((tt, N), lambda j, t: (t, 0))] + extra_specs,
        out_specs=[pl.BlockSpec((tk, N), lambda j, t: (j, 0))],
        out_shape=[_sds((K, N), BF16)],
        scratch=[pltpu.VMEM((tk, N), F32)],
        args=(a, b) + (() if extra is None else (extra[1],)), comm=comm)
    return (out, got) if comm is not None else out


def _adamw_reduce(parts, w, m, v, name, tr):
    R, C = w.shape
    n_parts = parts.shape[0]

    def body(p_ref, w_ref, m_ref, v_ref, g_ref, d_ref, m2_ref, v2_ref):
        g = p_ref[0].astype(F32)
        for i in range(1, n_parts):
            g = g + p_ref[i].astype(F32)
        delta, m2, v2 = _adamw(w_ref[...], g, m_ref[...], v_ref[...])
        g_ref[...] = g
        d_ref[...] = delta
        m2_ref[...] = m2
        v2_ref[...] = v2

    spec = _row(tr, C)
    return pl.pallas_call(
        body, name=name, grid=(R // tr,),
        in_specs=[pl.BlockSpec((n_parts, tr, C), lambda i: (0, i, 0)), spec, spec, spec],
        out_specs=[spec] * 4,
        out_shape=[_sds((R, C), F32)] * 4,
        compiler_params=_params(("arbitrary",)),
    )(parts, w, m, v)


SMALL_ORDER = ("b_ada", "ln1_g", "ln1_b", "ln2_g", "ln2_b", "gmlp_ln_g", "gmlp_ln_b", "b_spatial", "attn_sink")


def _small_step(gath, params):
    flat = [a for name in SMALL_ORDER for a in params[name]]

    def grad_of(tot, name):
        if name == "b_ada":
            return jnp.concatenate([tot[r:r + 1, :] for r in range(6)], axis=1)
        if name in ("ln1_g", "ln1_b", "ln2_g", "ln2_b"):
            r = 8 + ("ln1_g", "ln1_b", "ln2_g", "ln2_b").index(name)
            return tot[r:r + 1, :]
        if name == "gmlp_ln_g":
            return tot[12:13, :GM_W]
        if name == "gmlp_ln_b":
            return tot[12:13, GM_W:]
        if name == "b_spatial":
            return jnp.concatenate([tot[13:14, g * BLK:(g + 1) * BLK] for g in range(N_GROUPS)], axis=0)[None]
        return tot[14:15, :N_Q_HEADS]

    def body(*refs):
        g_ref, in_refs = refs[0], refs[1:1 + len(flat)]
        tot_ref, out_refs = refs[1 + len(flat)], refs[2 + len(flat):]
        tot = g_ref[0]
        for i in range(1, N_DEV):
            tot = tot + g_ref[i]
        tot_ref[...] = tot
        tot_ref[0:2, :] = tot[0:2, :] + tot[6:8, :]
        tot_ref[15:16, :] = jnp.broadcast_to(jnp.sum(tot[15:16, :], axis=1, keepdims=True), (1, D))
        tot = tot_ref[...]
        for k, name in enumerate(SMALL_ORDER):
            w_ref, m_ref, v_ref = in_refs[3 * k:3 * k + 3]
            g = grad_of(tot, name)
            delta, m2, v2 = _adamw(w_ref[...], g, m_ref[...], v_ref[...])
            for r, val in zip(out_refs[4 * k:4 * k + 4], (g, delta, m2, v2)):
                r[...] = val

    res = pl.pallas_call(
        body, name="small_step", grid=(1,),
        in_specs=[_full((N_DEV, 16, D))] + [_full(a.shape) for a in flat],
        out_specs=[_full((16, D))] + [_full(params[name][0].shape) for name in SMALL_ORDER for _ in range(4)],
        out_shape=[_sds((16, D), F32)] + [_sds(params[name][0].shape, F32) for name in SMALL_ORDER for _ in range(4)],
        compiler_params=_params(("arbitrary",)),
    )(gath, *flat)
    return res[0], {name: res[1 + 4 * k:5 + 4 * k] for k, name in enumerate(SMALL_ORDER)}


def _cctx_finish(gath, c_ctx, m, v):
    def body(g_ref, c_ref, m_ref, v_ref, gr_ref, d_ref, m2_ref, v2_ref):
        ds = g_ref[0]
        for i in range(1, N_DEV):
            ds = ds + g_ref[i]
        c = c_ref[...]
        sg = _sigmoid(c)
        g = ds * (sg * (1.0 + c * (1.0 - sg)))
        delta, m2, v2 = _adamw(c, g, m_ref[...], v_ref[...])
        gr_ref[...] = g
        d_ref[...] = delta
        m2_ref[...] = m2
        v2_ref[...] = v2

    return pl.pallas_call(
        body, name="cctx_finish", grid=(1,),
        in_specs=[_full((N_DEV, 8, D))] + [_full((8, D))] * 3, out_specs=[_full((8, D))] * 4,
        out_shape=[_sds((8, D), F32)] * 4,
        compiler_params=_params(("arbitrary",)),
    )(gath, c_ctx, m, v)


def _pad_rows(a, rows):
    return jnp.concatenate([a, jnp.zeros((rows - a.shape[0], a.shape[1]), a.dtype)], axis=0)


def kernel(x, c, ctx, c_ctx, w_ada, b_ada, w_in, attn_sink, gmlp_ln_g, gmlp_ln_b, w_spatial, b_spatial, w_branch_a, w_branch_b, w_out, ln1_g, ln1_b, w_ffn_in, w_ffn_out, ln2_g, ln2_b, loss_target, m_c_ctx, m_w_ada, m_b_ada, m_w_in, m_attn_sink, m_gmlp_ln_g, m_gmlp_ln_b, m_w_spatial, m_b_spatial, m_w_branch_a, m_w_branch_b, m_w_out, m_ln1_g, m_ln1_b, m_w_ffn_in, m_w_ffn_out, m_ln2_g, m_ln2_b, v_c_ctx, v_w_ada, v_b_ada, v_w_in, v_attn_sink, v_gmlp_ln_g, v_gmlp_ln_b, v_w_spatial, v_b_spatial, v_w_branch_a, v_w_branch_b, v_w_out, v_ln1_g, v_ln1_b, v_w_ffn_in, v_w_ffn_out, v_ln2_g, v_ln2_b):
    L = x.shape[1]
    me = 4 * lax.axis_index("x") + 2 * lax.axis_index("y") + lax.axis_index("c")
    x2, tgt, ctx2 = x[0], loss_target[0], ctx[0]
    tiles = _Tiles(L)
    tm_in, tm, tt = tiles.wide, tiles.narrow, tiles.tokens

    transposed = ("w_in", "w_ffn_in")
    tr = lambda kname, a: a.T if kname in transposed else a
    big = dict(w_in=w_in[0].T, w_branch_a=w_branch_a[0], w_branch_b=w_branch_b[0], w_out=w_out[0],
               w_ffn_in=w_ffn_in[0].T, w_ffn_out=w_ffn_out[0])
    col_sharded = ("w_branch_a", "w_branch_b")
    shard_bf = {k: a.astype(BF16) for k, a in big.items()}

    def assemble(kname, g):
        if kname in col_sharded:
            return g.transpose(1, 0, 2).reshape(g.shape[1], N_DEV * g.shape[2])
        return g.reshape(N_DEV * g.shape[1], g.shape[2])

    def to_blocks(kname, g):
        if kname in col_sharded:
            return g.reshape(g.shape[0], N_DEV, g.shape[1] // N_DEV).transpose(1, 0, 2)
        return g.reshape(N_DEV, g.shape[0] // N_DEV, g.shape[1])

    full = {}
    n_ada = w_ada.shape[2]
    b_my = lax.dynamic_slice(b_ada, (0, me * n_ada), (1, n_ada))
    got_in = _sc_gather(shard_bf["w_in"], "sc_gather_w_in")
    later = ("w_branch_a", "w_branch_b", "w_out", "w_ffn_out")
    got_later = [_sc_gather(shard_bf[kname], "sc_gather_" + kname) for kname in later]
    got_ffn_in = _sc_gather(shard_bf["w_ffn_in"], "sc_gather_w_ffn_in")
    act, mod_all = _prologue(_pad_rows(c, 8), _pad_rows(c_ctx[None, :], 8), w_ada[0], b_my)
    full["w_in"] = assemble("w_in", got_in)
    mod_all = mod_all.transpose(1, 0, 2).reshape(16, 6 * D)
    modv = _pad_rows(lax.dynamic_slice(mod_all, (me, 0), (1, 6 * D)).reshape(6, D), 8)
    modc = _pad_rows(mod_all[8].reshape(6, D), 8)

    lnv = _pad_rows(jnp.concatenate([ln1_g, ln1_b, ln2_g, ln2_b], axis=0), 8)
    gm_lnv = _pad_rows(jnp.concatenate([gmlp_ln_g, gmlp_ln_b], axis=0), 8)
    ws_b = w_spatial[0].astype(BF16)
    wst_b = ws_b.transpose(0, 2, 1)
    bsp = jnp.repeat(b_spatial[0].T, GROUP_DIM, axis=1)
    sink = attn_sink[0]
    cos, sin = _rope_tables(L)
    bias = _attn_bias()
    w_kv = full["w_in"][O_K:O_K + 2 * KV_W, :]

    (h, q, k, v, u, vb, ga, gb), _ = _k_in(x2, modv, full["w_in"], cos, sin, tm_in)
    for kname, g in zip(later, got_later):
        full[kname] = assemble(kname, g)
    hc, kc, vc = _k_ctx(ctx2, modc, w_kv)
    (ya, lse), _ = _k_attn(sink, q, k, v, kc, vc, bias)
    full["w_ffn_in"] = assemble("w_ffn_in", got_ffn_in)
    yb = _k_gmlp(u, vb, gm_lnv, ws_b, bsp)
    merged, mix, xm, h2 = _k_merge(x2, ya, yb, ga, gb, full["w_branch_a"], full["w_branch_b"], full["w_out"], modv, lnv, tm_in)
    gate, up, act_f, dr2, df, acc_f = _k_ffn(h2, xm, tgt, full["w_ffn_in"], full["w_ffn_out"], modv, lnv, tm_in)

    dF, dmix, dxp, acc_b = _k_ffn_bwd(df, gate, up, xm, dr2, x2, mix, full["w_ffn_in"], full["w_ffn_out"], modv, lnv, tm)
    blk_fo = to_blocks("w_ffn_out", _wgrad(act_f, df, "wgrad_ffn_out", tiles.tk_ffn, tt))
    gw_fi, (rcv_fo,) = _wgrad(dF, h2, "wgrad_ffn_in", tiles.tk_ffn, tt, comm=_Comm(scatter=[blk_fo]))
    blk_fi = to_blocks("w_ffn_in", gw_fi)
    dga, dgb, dya, dyb, gw_a, gw_b, gw_o = _k_merge_bwd(
        dmix, merged, ya, yb, ga, gb, full["w_branch_a"], full["w_branch_b"], full["w_out"], tm_in)
    du, dvb, g_ws, g_bst, g_gln = _k_gmlp_bwd(u, vb, dyb, gm_lnv, ws_b, wst_b, bsp)
    (dq, dk_late, dv_late, dkc, dvc, g_sink), (gath_ws, rcv_fi) = _k_attn_bwd(
        sink, q, k, v, kc, vc, dya, lse, cos, sin, bias,
        comm=_Comm(gather=[g_ws.reshape(N_GROUPS * BLK, BLK)], scatter=[blk_fi]))
    dk, dv = dk_late[BLK:BLK + L], dv_late[BLK:BLK + L]
    blk_a, blk_b, blk_o = to_blocks("w_branch_a", gw_a), to_blocks("w_branch_b", gw_b), to_blocks("w_out", gw_o)
    (dP, grad_x, acc_i), _ = _k_in_bwd(dq, dk, dv, du, dvb, dga, dgb, x2, dxp, full["w_in"], modv, tm_in)
    g_ctx, dmodc = _k_ctx_bwd(ctx2, modc, hc, dkc, dvc, w_kv)
    gw_in, (rcv_a, rcv_b, rcv_o) = _wgrad(dP, h, "wgrad_in", tiles.tk_in, tt, comm=_Comm(scatter=[blk_a, blk_b, blk_o]),
                                          extra=(O_K, g_ctx))

    dmod_x = jnp.concatenate([acc_i[0:2], acc_b[4:5], acc_b[0:2], acc_f[2:3]], axis=0)
    small = jnp.concatenate([
        dmod_x, dmodc[0:2], acc_b[2:4], acc_f[0:2],
        jnp.concatenate([g_gln[0:1], g_gln[1:2]], axis=1), g_bst.T.reshape(1, D),
        _pad_rows(g_sink[:, 0:1], D).T, acc_f[3:4]], axis=0)
    chip_sums, gath = _exchange_two_level(to_blocks("w_in", gw_in), small, "exchange_last", ici=False)
    rcv_in = jnp.concatenate([chip_sums[0:1], _sc_chip_exchange(chip_sums)], axis=0)
    received = dict(w_in=rcv_in, w_branch_a=rcv_a, w_branch_b=rcv_b, w_out=rcv_o, w_ffn_in=rcv_fi, w_ffn_out=rcv_fo)
    moments = dict(w_in=(m_w_in, v_w_in), w_branch_a=(m_w_branch_a, v_w_branch_a), w_branch_b=(m_w_branch_b, v_w_branch_b),
                   w_out=(m_w_out, v_w_out), w_ffn_in=(m_w_ffn_in, v_w_ffn_in), w_ffn_out=(m_w_ffn_out, v_w_ffn_out))
    names = list(big)
    res = {}
    for kname in names:
        mm, vv = moments[kname]
        R = big[kname].shape[0]
        res[kname] = [tr(kname, r) for r in _adamw_reduce(
            received[kname], big[kname], tr(kname, mm[0]), tr(kname, vv[0]), "adamw_" + kname, 256 if R % 256 == 0 else R // 2)]

    ws2d = lambda a: a.reshape(N_GROUPS * BLK, BLK)
    res_ws = [r.reshape(w_spatial.shape) for r in _adamw_reduce(
        gath_ws, ws2d(w_spatial), ws2d(m_w_spatial), ws2d(v_w_spatial), "adamw_w_spatial", 256)]
    tot, res_small = _small_step(gath, dict(
        b_ada=(b_ada, m_b_ada, v_b_ada), ln1_g=(ln1_g, m_ln1_g, v_ln1_g), ln1_b=(ln1_b, m_ln1_b, v_ln1_b),
        ln2_g=(ln2_g, m_ln2_g, v_ln2_g), ln2_b=(ln2_b, m_ln2_b, v_ln2_b),
        gmlp_ln_g=(gmlp_ln_g, m_gmlp_ln_g, v_gmlp_ln_g), gmlp_ln_b=(gmlp_ln_b, m_gmlp_ln_b, v_gmlp_ln_b),
        b_spatial=(b_spatial, m_b_spatial, v_b_spatial), attn_sink=(attn_sink, m_attn_sink, v_attn_sink)))
    loss = tot[15, 0]

    dmod_rows = jnp.concatenate([gath[:, 0:6, :].reshape(N_DEV, 6 * D),
                                 jnp.concatenate([tot[6:8].reshape(1, 2 * D), jnp.zeros((1, 4 * D), F32)], axis=1),
                                 jnp.zeros((7, 6 * D), F32)], axis=0)
    dmod_my = lax.dynamic_slice(dmod_rows, (0, me * n_ada), (16, n_ada))
    g_wada, d_wada, m2_wada, v2_wada, pc = _ada_bwd(act, dmod_my, w_ada[0], m_w_ada[0], v_w_ada[0])
    pc_all = _gather_rows(pc, "gather_cctx")
    cc8 = lambda a: _pad_rows(a.reshape(1, D), 8)
    g_cc, d_cc, m2_cc, v2_cc = _cctx_finish(pc_all, cc8(c_ctx), cc8(m_c_ctx), cc8(v_c_ctx))

    order = ["c_ctx", "w_ada", "b_ada", "w_in", "attn_sink", "gmlp_ln_g", "gmlp_ln_b", "w_spatial", "b_spatial",
             "w_branch_a", "w_branch_b", "w_out", "ln1_g", "ln1_b", "w_ffn_in", "w_ffn_out", "ln2_g", "ln2_b"]
    grads, deltas, new_m, new_v = {}, {}, {}, {}
    grads["c_ctx"], deltas["c_ctx"], new_m["c_ctx"], new_v["c_ctx"] = g_cc[0], d_cc[0], m2_cc[0], v2_cc[0]
    grads["w_ada"], deltas["w_ada"], new_m["w_ada"], new_v["w_ada"] = g_wada[None], d_wada[None], m2_wada[None], v2_wada[None]
    for kname in names:
        g, d, m2, v2 = res[kname]
        grads[kname], deltas[kname], new_m[kname], new_v[kname] = g[None], d[None], m2[None], v2[None]
    grads["w_spatial"], deltas["w_spatial"], new_m["w_spatial"], new_v["w_spatial"] = res_ws
    for kname in SMALL_ORDER:
        grads[kname], deltas[kname], new_m[kname], new_v[kname] = res_small[kname]
    return (loss, grad_x[None], *[grads[n] for n in order], *[deltas[n] for n in order],
            *[new_m[n] for n in order], *[new_v[n] for n in order])
```
